```python
import jax, jax.numpy as jnp
from jax import lax
import numpy as np

D_MODEL = 1024
BATCH = 8
SEQ = 8192
DEPTH = 2

MIX_WIDTH = D_MODEL
N_MIXERS = 4
GROUP_WIDTH = MIX_WIDTH // N_MIXERS
HEADS_PER_GROUP = 4
HEAD_DIM = GROUP_WIDTH // HEADS_PER_GROUP
SGU_CHUNK = 128
SC_WIDTH = 3
DN_CONV_WIDTH = 4
DN_CHUNK = 64
GLA_CHUNK = 64
GLA_GATE_RANK = 16
GLA_GATE_TEMP = 16.0
D_FF = ((8 * D_MODEL // 3 + 255) // 256) * 256
EPS = 1e-6

_G = GROUP_WIDTH
_H = HEADS_PER_GROUP
IN_SPLITS = (_G, _G,
             _G, _G, _G,
             _G, _G, _G, _H, _H, _G,
             _G, _G, _G, GLA_GATE_RANK, _G)
IN_COLS = sum(IN_SPLITS)

kernel_name = "hybrid_sgu_shortconv_gdn_gla"


def rmsnorm(x, w):
    xf = x.astype(jnp.float32)
    y = xf * lax.rsqrt(jnp.mean(xf * xf, axis=-1, keepdims=True) + EPS)
    return (y * w.astype(jnp.float32)).astype(x.dtype)


def layernorm(x, w, b):
    xf = x.astype(jnp.float32)
    mu = jnp.mean(xf, axis=-1, keepdims=True)
    var = jnp.mean(jnp.square(xf - mu), axis=-1, keepdims=True)
    y = (xf - mu) * lax.rsqrt(var + EPS)
    return (y * w.astype(jnp.float32) + b.astype(jnp.float32)).astype(x.dtype)


def l2norm(t):
    return t * lax.rsqrt(jnp.sum(t * t, axis=-1, keepdims=True) + EPS)


def causal_dwconv(x, w):
    k_width, ch = w.shape
    return lax.conv_general_dilated(
        x, w[:, None, :].astype(x.dtype), window_strides=(1,),
        padding=[(k_width - 1, 0)], dimension_numbers=('NWC', 'WIO', 'NWC'),
        feature_group_count=ch)


def to_heads(t):
    b, s, _ = t.shape
    return t.reshape(b, s, HEADS_PER_GROUP, HEAD_DIM).astype(jnp.float32)


def to_chunks(t, c):
    b, s = t.shape[:2]
    t = t.reshape(b, s // c, c, *t.shape[2:])
    return jnp.moveaxis(t, 3, 1)


def from_chunks(t):
    b, h, n, c, d = t.shape
    return jnp.moveaxis(t, 1, 3).reshape(b, n * c, h, d)


def sgu_mixer(u, v, ln_w, ln_b, w_s, b_s):
    bsz, s, g = u.shape
    n = s // SGU_CHUNK
    u = jax.nn.gelu(u)
    v = layernorm(jax.nn.gelu(v), ln_w, ln_b)
    vc = v.reshape(bsz, n, SGU_CHUNK, HEADS_PER_GROUP, HEAD_DIM)
    mask = jnp.tril(jnp.ones((SGU_CHUNK, SGU_CHUNK), dtype=bool))
    ws = jnp.where(mask, w_s, 0.0).astype(v.dtype)
    mixed = jnp.einsum('hts,bnshd->bnthd', ws, vc) + b_s.T.astype(v.dtype)[None, None, :, :, None]
    return u * mixed.reshape(bsz, s, g)


def short_conv_mixer(gate_b, gate_c, h, w_conv):
    return gate_b * causal_dwconv(gate_c * h, w_conv)


def chunk_gated_delta_rule(q, k, v, g, beta):
    bsz, s, h, dk = q.shape
    dv = v.shape[-1]
    c = DN_CHUNK
    q, k, v = to_chunks(q, c), to_chunks(k, c), to_chunks(v, c)
    g, beta = to_chunks(g, c), to_chunks(beta, c)
    q = q * dk ** -0.5
    gc = jnp.cumsum(g, axis=-1)
    causal = jnp.tril(jnp.ones((c, c), dtype=bool))
    strict = jnp.tril(jnp.ones((c, c), dtype=bool), k=-1)
    decay = jnp.exp(jnp.where(causal, gc[..., :, None] - gc[..., None, :], -jnp.inf))
    kb = k * beta[..., None]
    low = jnp.where(strict, jnp.einsum('bhncd,bhnsd->bhncs', kb, k) * decay, 0.0)
    rhs = jnp.concatenate([v * beta[..., None], kb * jnp.exp(gc)[..., None]], axis=-1)
    sol = lax.linalg.triangular_solve(low, rhs, left_side=True, lower=True, unit_diagonal=True)
    u, w = sol[..., :dv], sol[..., dv:]
    attn = jnp.einsum('bhncd,bhnsd->bhncs', q, k) * decay
    qg = q * jnp.exp(gc)[..., None]
    k_dec = k * jnp.exp(gc[..., -1:] - gc)[..., None]
    chunk_dec = jnp.exp(gc[..., -1])

    def step(state, inp):
        qg_n, w_n, u_n, attn_n, kd_n, dec_n = inp
        v_new = u_n - jnp.einsum('bhcd,bhde->bhce', w_n, state)
        o = jnp.einsum('bhcd,bhde->bhce', qg_n, state) + jnp.einsum('bhcs,bhse->bhce', attn_n, v_new)
        state = state * dec_n[..., None, None] + jnp.einsum('bhcd,bhce->bhde', kd_n, v_new)
        return state, o

    s0 = jnp.zeros((bsz, h, dk, dv), jnp.float32)
    xs = tuple(jnp.moveaxis(t, 2, 0) for t in (qg, w, u, attn, k_dec, chunk_dec))
    _, o = lax.scan(step, s0, xs)
    return from_chunks(jnp.moveaxis(o, 0, 2))


def deltanet_mixer(q, k, v, a, b, z, conv_w, a_log, dt_bias, norm_w):
    bsz, s, _ = q.shape
    qkv = jax.nn.silu(causal_dwconv(jnp.concatenate([q, k, v], axis=-1), conv_w))
    q, k, v = jnp.split(qkv, 3, axis=-1)
    q, k, v = l2norm(to_heads(q)), l2norm(to_heads(k)), to_heads(v)
    g = -jnp.exp(a_log.astype(jnp.float32)) * jax.nn.softplus(a.astype(jnp.float32) + dt_bias.astype(jnp.float32))
    beta = jax.nn.sigmoid(b.astype(jnp.float32))
    o = chunk_gated_delta_rule(q, k, v, g, beta)
    o = rmsnorm(o, norm_w) * jax.nn.silu(to_heads(z))
    return o.reshape(bsz, s, GROUP_WIDTH).astype(z.dtype)


def chunk_gla(q, k, v, log_a):
    bsz, s, h, dk = q.shape
    dv = v.shape[-1]
    c = GLA_CHUNK
    q, k, v, log_a = (to_chunks(t, c) for t in (q, k, v, log_a))
    q = q * dk ** -0.5
    gcum = jnp.cumsum(log_a, axis=3)
    g_mid = gcum[:, :, :, c // 2:c // 2 + 1, :]
    qa = q * jnp.exp(gcum - g_mid)
    ka = k * jnp.exp(g_mid - gcum)
    causal = jnp.tril(jnp.ones((c, c), dtype=bool))
    attn = jnp.where(causal, jnp.einsum('bhncd,bhnsd->bhncs', qa, ka), 0.0)
    o_intra = jnp.einsum('bhncs,bhnse->bhnce', attn, v)
    qg = q * jnp.exp(gcum)
    k_last = k * jnp.exp(gcum[:, :, :, -1:, :] - gcum)
    chunk_dec = jnp.exp(gcum[:, :, :, -1, :])

    def step(state, inp):
        qg_n, kl_n, v_n, dec_n = inp
        o = jnp.einsum('bhcd,bhde->bhce', qg_n, state)
        state = state * dec_n[..., :, None] + jnp.einsum('bhcd,bhce->bhde', kl_n, v_n)
        return state, o

    s0 = jnp.zeros((bsz, h, dk, dv), jnp.float32)
    xs = tuple(jnp.moveaxis(t, 2, 0) for t in (qg, k_last, v, chunk_dec))
    _, o_inter = lax.scan(step, s0, xs)
    return from_chunks(o_intra + jnp.moveaxis(o_inter, 0, 2))


def gla_mixer(q, k, v, g_lr, z, w_gate2, gate_bias, norm_w):
    bsz, s, _ = q.shape
    pre = jnp.einsum('btr,rg->btg', g_lr, w_gate2.astype(g_lr.dtype)).astype(jnp.float32) + gate_bias.astype(jnp.float32)
    log_a = jax.nn.log_sigmoid(pre) / GLA_GATE_TEMP
    o = chunk_gla(to_heads(q), to_heads(k), to_heads(v), to_heads(log_a))
    o = rmsnorm(o, norm_w) * jax.nn.silu(to_heads(z))
    return o.reshape(bsz, s, GROUP_WIDTH).astype(z.dtype)


def swiglu(h, w_gate_up, w_down):
    gu = h @ w_gate_up
    gate, up = jnp.split(gu, 2, axis=-1)
    return (jax.nn.silu(gate) * up) @ w_down


def _fwd_setup_inputs(seed: int = 0) -> dict:
    key = jax.random.key(seed)
    ks = jax.random.split(key, 24)
    f32 = jnp.float32

    def nrm(k, shape, scale):
        return jax.random.normal(k, shape, f32) * scale

    def gain(k, shape):
        return 1.0 + 0.02 * jax.random.normal(k, shape, f32)

    dt = jnp.exp(jax.random.uniform(ks[11], (DEPTH, HEADS_PER_GROUP), f32, np.log(1e-3), np.log(1e-1)))
    return {
        "x": jax.random.normal(ks[0], (BATCH, SEQ, D_MODEL), f32),
        "norm1_w": gain(ks[1], (DEPTH, D_MODEL)),
        "w_in": nrm(ks[2], (DEPTH, D_MODEL, IN_COLS), D_MODEL ** -0.5),
        "sgu_ln_w": gain(ks[3], (DEPTH, GROUP_WIDTH)),
        "sgu_ln_b": nrm(ks[4], (DEPTH, GROUP_WIDTH), 0.02),
        "sgu_w_spatial": nrm(ks[5], (DEPTH, HEADS_PER_GROUP, SGU_CHUNK, SGU_CHUNK), SGU_CHUNK ** -0.5),
        "sgu_b_spatial": gain(ks[6], (DEPTH, HEADS_PER_GROUP, SGU_CHUNK)),
        "sc_conv_w": nrm(ks[7], (DEPTH, SC_WIDTH, GROUP_WIDTH), SC_WIDTH ** -0.5),
        "dn_conv_w": nrm(ks[8], (DEPTH, DN_CONV_WIDTH, 3 * GROUP_WIDTH), DN_CONV_WIDTH ** -0.5),
        "dn_a_log": jnp.log(jax.random.uniform(ks[9], (DEPTH, HEADS_PER_GROUP), f32, 1.0, 16.0)),
        "dn_dt_bias": dt + jnp.log(-jnp.expm1(-dt)),
        "dn_norm_w": gain(ks[10], (DEPTH, HEAD_DIM)),
        "gla_w_gate2": nrm(ks[12], (DEPTH, GLA_GATE_RANK, GROUP_WIDTH), GLA_GATE_RANK ** -0.5),
        "gla_gate_bias": nrm(ks[13], (DEPTH, GROUP_WIDTH), 0.1),
        "gla_norm_w": gain(ks[14], (DEPTH, HEAD_DIM)),
        "w_out": nrm(ks[15], (DEPTH, MIX_WIDTH, D_MODEL), MIX_WIDTH ** -0.5),
        "norm2_w": gain(ks[16], (DEPTH, D_MODEL)),
        "w_gate_up": nrm(ks[17], (DEPTH, D_MODEL, 2 * D_FF), D_MODEL ** -0.5),
        "w_down": nrm(ks[18], (DEPTH, D_FF, D_MODEL), D_FF ** -0.5),
        "final_norm_w": gain(ks[19], (D_MODEL,)),
    }


def _fwd_reference(x, norm1_w, w_in, sgu_ln_w, sgu_ln_b, sgu_w_spatial, sgu_b_spatial, sc_conv_w,
              dn_conv_w, dn_a_log, dn_dt_bias, dn_norm_w, gla_w_gate2, gla_gate_bias, gla_norm_w,
              w_out, norm2_w, w_gate_up, w_down, final_norm_w):
    split_idx = [int(i) for i in np.cumsum(IN_SPLITS)[:-1]]
    for l in range(DEPTH):
        h = rmsnorm(x, norm1_w[l])
        p = h @ w_in[l]
        (a_u, a_v, b_b, b_c, b_h, c_q, c_k, c_v, c_a, c_b, c_z,
         d_q, d_k, d_v, d_g, d_z) = jnp.split(p, split_idx, axis=-1)
        y_a = sgu_mixer(a_u, a_v, sgu_ln_w[l], sgu_ln_b[l], sgu_w_spatial[l], sgu_b_spatial[l])
        y_b = short_conv_mixer(b_b, b_c, b_h, sc_conv_w[l])
        y_c = deltanet_mixer(c_q, c_k, c_v, c_a, c_b, c_z, dn_conv_w[l], dn_a_log[l], dn_dt_bias[l], dn_norm_w[l])
        y_d = gla_mixer(d_q, d_k, d_v, d_g, d_z, gla_w_gate2[l], gla_gate_bias[l], gla_norm_w[l])
        mix = jnp.concatenate([y_a, y_b.astype(x.dtype), y_c, y_d], axis=-1)
        x = x + (mix @ w_out[l]).astype(x.dtype)
        x = x + swiglu(rmsnorm(x, norm2_w[l]), w_gate_up[l], w_down[l]).astype(x.dtype)
    return rmsnorm(x, final_norm_w)


import jax as _jax
import jax.numpy as _jnp

TWIN_FORMAT = 'train_step'
FWD_PARAMS = ['x', 'norm1_w', 'w_in', 'sgu_ln_w', 'sgu_ln_b', 'sgu_w_spatial', 'sgu_b_spatial', 'sc_conv_w', 'dn_conv_w', 'dn_a_log', 'dn_dt_bias', 'dn_norm_w', 'gla_w_gate2', 'gla_gate_bias', 'gla_norm_w', 'w_out', 'norm2_w', 'w_gate_up', 'w_down', 'final_norm_w']
TWIN_WEIGHTS = ['norm1_w', 'w_in', 'sgu_ln_w', 'sgu_ln_b', 'sgu_w_spatial', 'sgu_b_spatial', 'sc_conv_w', 'dn_conv_w', 'dn_a_log', 'dn_dt_bias', 'dn_norm_w', 'gla_w_gate2', 'gla_gate_bias', 'gla_norm_w', 'w_out', 'norm2_w', 'w_gate_up', 'w_down', 'final_norm_w']
TWIN_DIFF_INPUT = 'x'
TWIN_INPUTS = ['x', 'norm1_w', 'w_in', 'sgu_ln_w', 'sgu_ln_b', 'sgu_w_spatial', 'sgu_b_spatial', 'sc_conv_w', 'dn_conv_w', 'dn_a_log', 'dn_dt_bias', 'dn_norm_w', 'gla_w_gate2', 'gla_gate_bias', 'gla_norm_w', 'w_out', 'norm2_w', 'w_gate_up', 'w_down', 'final_norm_w', 'loss_target', 'm_norm1_w', 'm_w_in', 'm_sgu_ln_w', 'm_sgu_ln_b', 'm_sgu_w_spatial', 'm_sgu_b_spatial', 'm_sc_conv_w', 'm_dn_conv_w', 'm_dn_a_log', 'm_dn_dt_bias', 'm_dn_norm_w', 'm_gla_w_gate2', 'm_gla_gate_bias', 'm_gla_norm_w', 'm_w_out', 'm_norm2_w', 'm_w_gate_up', 'm_w_down', 'm_final_norm_w', 'v_norm1_w', 'v_w_in', 'v_sgu_ln_w', 'v_sgu_ln_b', 'v_sgu_w_spatial', 'v_sgu_b_spatial', 'v_sc_conv_w', 'v_dn_conv_w', 'v_dn_a_log', 'v_dn_dt_bias', 'v_dn_norm_w', 'v_gla_w_gate2', 'v_gla_gate_bias', 'v_gla_norm_w', 'v_w_out', 'v_norm2_w', 'v_w_gate_up', 'v_w_down', 'v_final_norm_w']
TWIN_OUTPUTS = ['loss', 'grad_x', 'grad_norm1_w', 'grad_w_in', 'grad_sgu_ln_w', 'grad_sgu_ln_b', 'grad_sgu_w_spatial', 'grad_sgu_b_spatial', 'grad_sc_conv_w', 'grad_dn_conv_w', 'grad_dn_a_log', 'grad_dn_dt_bias', 'grad_dn_norm_w', 'grad_gla_w_gate2', 'grad_gla_gate_bias', 'grad_gla_norm_w', 'grad_w_out', 'grad_norm2_w', 'grad_w_gate_up', 'grad_w_down', 'grad_final_norm_w', 'delta_norm1_w', 'delta_w_in', 'delta_sgu_ln_w', 'delta_sgu_ln_b', 'delta_sgu_w_spatial', 'delta_sgu_b_spatial', 'delta_sc_conv_w', 'delta_dn_conv_w', 'delta_dn_a_log', 'delta_dn_dt_bias', 'delta_dn_norm_w', 'delta_gla_w_gate2', 'delta_gla_gate_bias', 'delta_gla_norm_w', 'delta_w_out', 'delta_norm2_w', 'delta_w_gate_up', 'delta_w_down', 'delta_final_norm_w', 'new_m_norm1_w', 'new_m_w_in', 'new_m_sgu_ln_w', 'new_m_sgu_ln_b', 'new_m_sgu_w_spatial', 'new_m_sgu_b_spatial', 'new_m_sc_conv_w', 'new_m_dn_conv_w', 'new_m_dn_a_log', 'new_m_dn_dt_bias', 'new_m_dn_norm_w', 'new_m_gla_w_gate2', 'new_m_gla_gate_bias', 'new_m_gla_norm_w', 'new_m_w_out', 'new_m_norm2_w', 'new_m_w_gate_up', 'new_m_w_down', 'new_m_final_norm_w', 'new_v_norm1_w', 'new_v_w_in', 'new_v_sgu_ln_w', 'new_v_sgu_ln_b', 'new_v_sgu_w_spatial', 'new_v_sgu_b_spatial', 'new_v_sc_conv_w', 'new_v_dn_conv_w', 'new_v_dn_a_log', 'new_v_dn_dt_bias', 'new_v_dn_norm_w', 'new_v_gla_w_gate2', 'new_v_gla_gate_bias', 'new_v_gla_norm_w', 'new_v_w_out', 'new_v_norm2_w', 'new_v_w_gate_up', 'new_v_w_down', 'new_v_final_norm_w']
TWIN_LEAF_KINDS = {'loss': 'loss', 'grad_x': 'grad_x', 'grad_norm1_w': 'grad_w', 'grad_w_in': 'grad_w', 'grad_sgu_ln_w': 'grad_w', 'grad_sgu_ln_b': 'grad_w', 'grad_sgu_w_spatial': 'grad_w', 'grad_sgu_b_spatial': 'grad_w', 'grad_sc_conv_w': 'grad_w', 'grad_dn_conv_w': 'grad_w', 'grad_dn_a_log': 'grad_w', 'grad_dn_dt_bias': 'grad_w', 'grad_dn_norm_w': 'grad_w', 'grad_gla_w_gate2': 'grad_w', 'grad_gla_gate_bias': 'grad_w', 'grad_gla_norm_w': 'grad_w', 'grad_w_out': 'grad_w', 'grad_norm2_w': 'grad_w', 'grad_w_gate_up': 'grad_w', 'grad_w_down': 'grad_w', 'grad_final_norm_w': 'grad_w', 'delta_norm1_w': 'delta_w', 'delta_w_in': 'delta_w', 'delta_sgu_ln_w': 'delta_w', 'delta_sgu_ln_b': 'delta_w', 'delta_sgu_w_spatial': 'delta_w', 'delta_sgu_b_spatial': 'delta_w', 'delta_sc_conv_w': 'delta_w', 'delta_dn_conv_w': 'delta_w', 'delta_dn_a_log': 'delta_w', 'delta_dn_dt_bias': 'delta_w', 'delta_dn_norm_w': 'delta_w', 'delta_gla_w_gate2': 'delta_w', 'delta_gla_gate_bias': 'delta_w', 'delta_gla_norm_w': 'delta_w', 'delta_w_out': 'delta_w', 'delta_norm2_w': 'delta_w', 'delta_w_gate_up': 'delta_w', 'delta_w_down': 'delta_w', 'delta_final_norm_w': 'delta_w', 'new_m_norm1_w': 'new_m', 'new_m_w_in': 'new_m', 'new_m_sgu_ln_w': 'new_m', 'new_m_sgu_ln_b': 'new_m', 'new_m_sgu_w_spatial': 'new_m', 'new_m_sgu_b_spatial': 'new_m', 'new_m_sc_conv_w': 'new_m', 'new_m_dn_conv_w': 'new_m', 'new_m_dn_a_log': 'new_m', 'new_m_dn_dt_bias': 'new_m', 'new_m_dn_norm_w': 'new_m', 'new_m_gla_w_gate2': 'new_m', 'new_m_gla_gate_bias': 'new_m', 'new_m_gla_norm_w': 'new_m', 'new_m_w_out': 'new_m', 'new_m_norm2_w': 'new_m', 'new_m_w_gate_up': 'new_m', 'new_m_w_down': 'new_m', 'new_m_final_norm_w': 'new_m', 'new_v_norm1_w': 'new_v', 'new_v_w_in': 'new_v', 'new_v_sgu_ln_w': 'new_v', 'new_v_sgu_ln_b': 'new_v', 'new_v_sgu_w_spatial': 'new_v', 'new_v_sgu_b_spatial': 'new_v', 'new_v_sc_conv_w': 'new_v', 'new_v_dn_conv_w': 'new_v', 'new_v_dn_a_log': 'new_v', 'new_v_dn_dt_bias': 'new_v', 'new_v_dn_norm_w': 'new_v', 'new_v_gla_w_gate2': 'new_v', 'new_v_gla_gate_bias': 'new_v', 'new_v_gla_norm_w': 'new_v', 'new_v_w_out': 'new_v', 'new_v_norm2_w': 'new_v', 'new_v_w_gate_up': 'new_v', 'new_v_w_down': 'new_v', 'new_v_final_norm_w': 'new_v'}


def _forward(args):
    return _fwd_reference(*[args[k] for k in FWD_PARAMS])


def _output_shape():
    def fwd():
        inp = _fwd_setup_inputs(0)
        return _fwd_reference(*[inp[k] for k in FWD_PARAMS])
    out = _jax.eval_shape(fwd)
    return out.shape, out.dtype

N_MICROBATCH = 1
ADAM_LR = 0.001
ADAM_B1 = 0.9
ADAM_B2 = 0.999
ADAM_EPS = 1e-08
ADAM_WD = 0.01
ADAM_STEP = 10
PER_EXAMPLE_BATCH_AXIS = {'x': 0, 'loss_target': 0}
SHARED_INPUTS = []
_WEIGHT_DTYPES = {'norm1_w': _jnp.float32, 'w_in': _jnp.float32, 'sgu_ln_w': _jnp.float32, 'sgu_ln_b': _jnp.float32, 'sgu_w_spatial': _jnp.float32, 'sgu_b_spatial': _jnp.float32, 'sc_conv_w': _jnp.float32, 'dn_conv_w': _jnp.float32, 'dn_a_log': _jnp.float32, 'dn_dt_bias': _jnp.float32, 'dn_norm_w': _jnp.float32, 'gla_w_gate2': _jnp.float32, 'gla_gate_bias': _jnp.float32, 'gla_norm_w': _jnp.float32, 'w_out': _jnp.float32, 'norm2_w': _jnp.float32, 'w_gate_up': _jnp.float32, 'w_down': _jnp.float32, 'final_norm_w': _jnp.float32}
MOMENT_SCALE = {'norm1_w': 3.059213e-01, 'w_in': 1.656265e-01, 'sgu_ln_w': 1.186312e-01, 'sgu_ln_b': 1.024831e-01, 'sgu_w_spatial': 7.449679e-02, 'sgu_b_spatial': 1.121974e-01, 'sc_conv_w': 2.318314e-01, 'dn_conv_w': 1.304801e-01, 'dn_a_log': 6.684918e-01, 'dn_dt_bias': 6.574553e-01, 'dn_norm_w': 2.808633e-01, 'gla_w_gate2': 1.643681e-02, 'gla_gate_bias': 7.288314e-02, 'gla_norm_w': 2.904786e-01, 'w_out': 1.748842e-01, 'norm2_w': 1.677771e-01, 'w_gate_up': 7.116207e-02, 'w_down': 1.159333e-01, 'final_norm_w': 6.403054e+01}


def _to_microbatches(a, axis):
    t = _jnp.moveaxis(a, axis, 0)
    t = t.reshape((N_MICROBATCH, t.shape[0] // N_MICROBATCH) + t.shape[1:])
    return _jnp.moveaxis(t, 1, axis + 1)


def setup_inputs(seed: int = 0) -> dict:
    inp = _fwd_setup_inputs(seed)
    key = _jax.random.fold_in(_jax.random.key(seed), 7919)
    shape, _ = _output_shape()
    out = dict(inp)
    out["loss_target"] = _jax.random.normal(_jax.random.fold_in(key, 0), shape, _jnp.float32)
    for i, name in enumerate(TWIN_WEIGHTS):
        w = inp[name].astype(_jnp.float32)
        if MOMENT_SCALE is None:
            s = _jnp.sqrt(_jnp.mean(_jnp.square(w)) + 1e-30)
        else:
            s = MOMENT_SCALE[name]
        km, kv = _jax.random.split(_jax.random.fold_in(key, i + 1))
        out[name] = w
        out["m_" + name] = s * _jax.random.normal(km, w.shape, _jnp.float32)
        out["v_" + name] = (s * s) * _jax.random.uniform(kv, w.shape, _jnp.float32, 0.5, 1.5)
    if N_MICROBATCH > 1:
        for name, axis in PER_EXAMPLE_BATCH_AXIS.items():
            out[name] = _to_microbatches(out[name], axis)
    return {'x': out['x'], 'norm1_w': out['norm1_w'], 'w_in': out['w_in'], 'sgu_ln_w': out['sgu_ln_w'], 'sgu_ln_b': out['sgu_ln_b'], 'sgu_w_spatial': out['sgu_w_spatial'], 'sgu_b_spatial': out['sgu_b_spatial'], 'sc_conv_w': out['sc_conv_w'], 'dn_conv_w': out['dn_conv_w'], 'dn_a_log': out['dn_a_log'], 'dn_dt_bias': out['dn_dt_bias'], 'dn_norm_w': out['dn_norm_w'], 'gla_w_gate2': out['gla_w_gate2'], 'gla_gate_bias': out['gla_gate_bias'], 'gla_norm_w': out['gla_norm_w'], 'w_out': out['w_out'], 'norm2_w': out['norm2_w'], 'w_gate_up': out['w_gate_up'], 'w_down': out['w_down'], 'final_norm_w': out['final_norm_w'], 'loss_target': out['loss_target'], 'm_norm1_w': out['m_norm1_w'], 'm_w_in': out['m_w_in'], 'm_sgu_ln_w': out['m_sgu_ln_w'], 'm_sgu_ln_b': out['m_sgu_ln_b'], 'm_sgu_w_spatial': out['m_sgu_w_spatial'], 'm_sgu_b_spatial': out['m_sgu_b_spatial'], 'm_sc_conv_w': out['m_sc_conv_w'], 'm_dn_conv_w': out['m_dn_conv_w'], 'm_dn_a_log': out['m_dn_a_log'], 'm_dn_dt_bias': out['m_dn_dt_bias'], 'm_dn_norm_w': out['m_dn_norm_w'], 'm_gla_w_gate2': out['m_gla_w_gate2'], 'm_gla_gate_bias': out['m_gla_gate_bias'], 'm_gla_norm_w': out['m_gla_norm_w'], 'm_w_out': out['m_w_out'], 'm_norm2_w': out['m_norm2_w'], 'm_w_gate_up': out['m_w_gate_up'], 'm_w_down': out['m_w_down'], 'm_final_norm_w': out['m_final_norm_w'], 'v_norm1_w': out['v_norm1_w'], 'v_w_in': out['v_w_in'], 'v_sgu_ln_w': out['v_sgu_ln_w'], 'v_sgu_ln_b': out['v_sgu_ln_b'], 'v_sgu_w_spatial': out['v_sgu_w_spatial'], 'v_sgu_b_spatial': out['v_sgu_b_spatial'], 'v_sc_conv_w': out['v_sc_conv_w'], 'v_dn_conv_w': out['v_dn_conv_w'], 'v_dn_a_log': out['v_dn_a_log'], 'v_dn_dt_bias': out['v_dn_dt_bias'], 'v_dn_norm_w': out['v_dn_norm_w'], 'v_gla_w_gate2': out['v_gla_w_gate2'], 'v_gla_gate_bias': out['v_gla_gate_bias'], 'v_gla_norm_w': out['v_gla_norm_w'], 'v_w_out': out['v_w_out'], 'v_norm2_w': out['v_norm2_w'], 'v_w_gate_up': out['v_w_gate_up'], 'v_w_down': out['v_w_down'], 'v_final_norm_w': out['v_final_norm_w']}


def _loss(weights, diff, rest, loss_target):
    with _jax.named_scope("forward"):
        args = {**rest, TWIN_DIFF_INPUT: diff, **{k: w.astype(_WEIGHT_DTYPES[k]) for k, w in weights.items()}}
        y = _forward(args)
    with _jax.named_scope("loss_head"):
        err = _jnp.square(y.astype(_jnp.float32) - loss_target)
        return 0.5 * _jnp.sum(_jnp.mean(err, axis=-1)) if err.ndim else 0.5 * err


def _adamw(w, g, m, v):
    m = ADAM_B1 * m + (1.0 - ADAM_B1) * g
    v = ADAM_B2 * v + (1.0 - ADAM_B2) * _jnp.square(g)
    m_hat = m / (1.0 - ADAM_B1 ** ADAM_STEP)
    v_hat = v / (1.0 - ADAM_B2 ** ADAM_STEP)
    delta = -ADAM_LR * (m_hat / (_jnp.sqrt(v_hat) + ADAM_EPS) + ADAM_WD * w)
    return delta, m, v


def reference(x, norm1_w, w_in, sgu_ln_w, sgu_ln_b, sgu_w_spatial, sgu_b_spatial, sc_conv_w, dn_conv_w, dn_a_log, dn_dt_bias, dn_norm_w, gla_w_gate2, gla_gate_bias, gla_norm_w, w_out, norm2_w, w_gate_up, w_down, final_norm_w, loss_target, m_norm1_w, m_w_in, m_sgu_ln_w, m_sgu_ln_b, m_sgu_w_spatial, m_sgu_b_spatial, m_sc_conv_w, m_dn_conv_w, m_dn_a_log, m_dn_dt_bias, m_dn_norm_w, m_gla_w_gate2, m_gla_gate_bias, m_gla_norm_w, m_w_out, m_norm2_w, m_w_gate_up, m_w_down, m_final_norm_w, v_norm1_w, v_w_in, v_sgu_ln_w, v_sgu_ln_b, v_sgu_w_spatial, v_sgu_b_spatial, v_sc_conv_w, v_dn_conv_w, v_dn_a_log, v_dn_dt_bias, v_dn_norm_w, v_gla_w_gate2, v_gla_gate_bias, v_gla_norm_w, v_w_out, v_norm2_w, v_w_gate_up, v_w_down, v_final_norm_w):
    given = dict(x=x, norm1_w=norm1_w, w_in=w_in, sgu_ln_w=sgu_ln_w, sgu_ln_b=sgu_ln_b, sgu_w_spatial=sgu_w_spatial, sgu_b_spatial=sgu_b_spatial, sc_conv_w=sc_conv_w, dn_conv_w=dn_conv_w, dn_a_log=dn_a_log, dn_dt_bias=dn_dt_bias, dn_norm_w=dn_norm_w, gla_w_gate2=gla_w_gate2, gla_gate_bias=gla_gate_bias, gla_norm_w=gla_norm_w, w_out=w_out, norm2_w=norm2_w, w_gate_up=w_gate_up, w_down=w_down, final_norm_w=final_norm_w, loss_target=loss_target, m_norm1_w=m_norm1_w, m_w_in=m_w_in, m_sgu_ln_w=m_sgu_ln_w, m_sgu_ln_b=m_sgu_ln_b, m_sgu_w_spatial=m_sgu_w_spatial, m_sgu_b_spatial=m_sgu_b_spatial, m_sc_conv_w=m_sc_conv_w, m_dn_conv_w=m_dn_conv_w, m_dn_a_log=m_dn_a_log, m_dn_dt_bias=m_dn_dt_bias, m_dn_norm_w=m_dn_norm_w, m_gla_w_gate2=m_gla_w_gate2, m_gla_gate_bias=m_gla_gate_bias, m_gla_norm_w=m_gla_norm_w, m_w_out=m_w_out, m_norm2_w=m_norm2_w, m_w_gate_up=m_w_gate_up, m_w_down=m_w_down, m_final_norm_w=m_final_norm_w, v_norm1_w=v_norm1_w, v_w_in=v_w_in, v_sgu_ln_w=v_sgu_ln_w, v_sgu_ln_b=v_sgu_ln_b, v_sgu_w_spatial=v_sgu_w_spatial, v_sgu_b_spatial=v_sgu_b_spatial, v_sc_conv_w=v_sc_conv_w, v_dn_conv_w=v_dn_conv_w, v_dn_a_log=v_dn_a_log, v_dn_dt_bias=v_dn_dt_bias, v_dn_norm_w=v_dn_norm_w, v_gla_w_gate2=v_gla_w_gate2, v_gla_gate_bias=v_gla_gate_bias, v_gla_norm_w=v_gla_norm_w, v_w_out=v_w_out, v_norm2_w=v_norm2_w, v_w_gate_up=v_w_gate_up, v_w_down=v_w_down, v_final_norm_w=v_final_norm_w)
    weights = {n: given[n] for n in TWIN_WEIGHTS}
    shared = {n: given[n] for n in SHARED_INPUTS}
    per_example = {n: given[n] for n in ['x']}
    grad_fn = _jax.value_and_grad(_loss, argnums=(0, 1))

    def one_microbatch(ex, loss_target):
        ex = dict(ex)
        diff = ex.pop(TWIN_DIFF_INPUT)
        return grad_fn(weights, diff, {**shared, **ex}, loss_target)

    if N_MICROBATCH == 1:
        loss, (grad_w, grad_x) = one_microbatch(per_example, given["loss_target"])
    else:
        def body(carry, xs):
            loss_sum, grad_sum = carry
            l_k, (gw_k, gx_k) = one_microbatch(xs[0], xs[1])
            with _jax.named_scope("update"):
                return (loss_sum + l_k, _jax.tree.map(_jnp.add, grad_sum, gw_k)), gx_k

        init = (_jnp.zeros((), _jnp.float32), _jax.tree.map(_jnp.zeros_like, weights))
        (loss, grad_w), grad_x = _jax.lax.scan(body, init, (per_example, given["loss_target"]))
    with _jax.named_scope("update"):
        delta_w, new_m, new_v = {}, {}, {}
        for n in TWIN_WEIGHTS:
            delta_w[n], new_m[n], new_v[n] = _adamw(weights[n], grad_w[n], given["m_" + n], given["v_" + n])
    return (loss, grad_x, *[grad_w[n] for n in TWIN_WEIGHTS], *[delta_w[n] for n in TWIN_WEIGHTS],
            *[new_m[n] for n in TWIN_WEIGHTS], *[new_v[n] for n in TWIN_WEIGHTS])
```

```python
import functools

import jax
import jax.numpy as jnp
from jax import lax
from jax.experimental import pallas as pl
from jax.experimental.pallas import tpu as pltpu

F32, BF16 = jnp.float32, jnp.bfloat16
DEPTH = 2
D = 1024
G = 256
NH, HD = 4, 64
FF = 2816
IN_COLS = 3352
P = 3584
EPS = 1e-6
SGU_C, DN_C, GLA_C = 128, 64, 64
N_DEV = 8
LANES = 128
VMEM_LIMIT = 56 * 2 ** 20

_PAD_SEGS = ((1280, 2048, 0),
             (2312, 3080, 0),
             (0, 512, 0),
             (512, 1280, 0),
             (2056, 2312, 0),
             (3096, 3352, 0),
             (2048, 2056, 120),
             (3080, 3096, 112))

ADAM_LR, ADAM_B1, ADAM_B2, ADAM_EPS, ADAM_WD, ADAM_STEP = 0.001, 0.9, 0.999, 1e-08, 0.01, 10


def _cparams(*sem):
    return pltpu.CompilerParams(dimension_semantics=sem, vmem_limit_bytes=VMEM_LIMIT)


def _resident(shape):
    return pl.BlockSpec(shape, lambda *_: (0,) * len(shape), pipeline_mode=pl.Buffered(1))


def _acc(shape):
    return pl.BlockSpec(shape, lambda *_: (0,) * len(shape))


def _dg(a, b, ca, cb, precision=None):
    return lax.dot_general(a, b, (((ca,), (cb,)), ((), ())), preferred_element_type=F32, precision=precision)


def _make_dot(cast, precision):
    def raw(a, b, form):
        ca = 0 if form == "tn" else 1
        cb = 1 if form == "nt" else 0
        return _dg(cast(a), cast(b), ca, cb, precision)

    @functools.partial(jax.custom_vjp, nondiff_argnums=(2,))
    def dot(a, b, form):
        return raw(a, b, form)

    def fwd(a, b, form):
        return raw(a, b, form), (a, b)

    def bwd(form, res, g):
        a, b = res
        if form == "nn":
            return raw(g, b, "nt"), raw(a, g, "tn")
        if form == "nt":
            return raw(g, b, "nn"), raw(g, a, "tn")
        return raw(b, g, "nt"), raw(a, g, "nn")

    dot.defvjp(fwd, bwd)
    return dot


_bdot = _make_dot(lambda t: t.astype(BF16), None)
_hdot = _make_dot(lambda t: t, lax.Precision.HIGHEST)


def _inv_unit_lower(low):
    n = low.shape[0]
    eye = (lax.broadcasted_iota(jnp.int32, (n, n), 0) == lax.broadcasted_iota(jnp.int32, (n, n), 1)).astype(F32)
    p = -low
    inv = eye + p
    k = 1
    while 2 * k < n:
        p = _dg(p, p, 1, 0, lax.Precision.HIGHEST)
        inv = inv + _dg(inv, p, 1, 0, lax.Precision.HIGHEST)
        k *= 2
    return inv


@jax.custom_vjp
def _unit_lower_solve(low, rhs):
    return _dg(_inv_unit_lower(low), rhs, 1, 0, lax.Precision.HIGHEST)


def _uls_fwd(low, rhs):
    inv = _inv_unit_lower(low)
    sol = _dg(inv, rhs, 1, 0, lax.Precision.HIGHEST)
    return sol, (inv, sol)


def _uls_bwd(res, g):
    inv, sol = res
    d_rhs = _dg(inv, g, 0, 0, lax.Precision.HIGHEST)
    return -_dg(d_rhs, sol, 1, 1, lax.Precision.HIGHEST), d_rhs


_unit_lower_solve.defvjp(_uls_fwd, _uls_bwd)


def _sigmoid(x):
    return 1.0 / (1.0 + jnp.exp(-x))


def _softplus(x):
    return jnp.maximum(x, 0.0) + jnp.log(1.0 + jnp.exp(-jnp.maximum(x, -x)))


def _gelu(x):
    return 0.5 * x * (1.0 + jnp.tanh(0.7978845608028654 * (x + 0.044715 * (x * x * x))))


def _tri(n):
    ri = lax.broadcasted_iota(jnp.int32, (n, n), 0)
    ci = lax.broadcasted_iota(jnp.int32, (n, n), 1)
    return ri, ci


def _sgu_chunk(uv, ln_w, ln_b, ws, bs_t):
    u = _gelu(uv[:, :G])
    v = _gelu(uv[:, G:])
    mu = jnp.mean(v, axis=-1, keepdims=True)
    vc = v - mu
    var = jnp.mean(vc * vc, axis=-1, keepdims=True)
    vn = vc * lax.rsqrt(var + EPS) * ln_w + ln_b
    ri, ci = _tri(SGU_C)
    mixed = []
    for h in range(NH):
        wm = jnp.where(ri >= ci, ws[h], 0.0)
        mixed.append(_bdot(wm, vn[:, HD * h:HD * (h + 1)], "nn") + bs_t[:, h:h + 1])
    return u * jnp.concatenate(mixed, axis=1)


def _head_out(o, zh, nw):
    on = o * lax.rsqrt(jnp.mean(o * o, axis=-1, keepdims=True) + EPS) * nw
    return on * (zh * _sigmoid(zh))


def _dn_chunk(q, k, v, ab, z, states, a_log, dt_bias, nw):
    c = DN_C
    ri, ci = _tri(c)
    causal, strict = ri >= ci, ri > ci
    g4 = -jnp.exp(a_log) * _softplus(ab[:, 0:NH] + dt_bias)
    beta4 = _sigmoid(ab[:, NH:2 * NH])
    gc4 = _hdot(causal.astype(F32), g4, "nn")
    gr4 = _hdot(g4, (ri <= ci).astype(F32), "tn")
    gl4 = jnp.sum(g4, axis=0, keepdims=True)
    ys, new_states = [], []
    for h in range(NH):
        sl = slice(HD * h, HD * (h + 1))
        qh, kh, vh = q[:, sl], k[:, sl], v[:, sl]
        qh = qh * lax.rsqrt(jnp.sum(qh * qh, axis=-1, keepdims=True) + EPS) * (HD ** -0.5)
        kh = kh * lax.rsqrt(jnp.sum(kh * kh, axis=-1, keepdims=True) + EPS)
        gc, gr, gl, beta = gc4[:, h:h + 1], gr4[h:h + 1, :], gl4[:, h:h + 1], beta4[:, h:h + 1]
        decay = jnp.exp(jnp.where(causal, gc - gr, -jnp.inf))
        kb = kh * beta
        low = jnp.where(strict, _bdot(kb, kh, "nt") * decay, 0.0)
        eg = jnp.exp(gc)
        sol = _unit_lower_solve(low, jnp.concatenate([vh * beta, kb * eg], axis=1))
        u, w = sol[:, :HD], sol[:, HD:]
        attn = _bdot(qh, kh, "nt") * decay
        s = states[h]
        v_new = u - _bdot(w, s, "nn")
        o = _bdot(qh * eg, s, "nn") + _bdot(attn, v_new, "nn")
        new_states.append(s * jnp.exp(gl) + _bdot(kh * jnp.exp(gl - gc), v_new, "tn"))
        ys.append(_head_out(o, z[:, sl], nw))
    return jnp.concatenate(ys, axis=1), new_states


def _gla_chunk(q, k, v, glr, z, states_t, w2, gate_bias, nw):
    c = GLA_C
    ri, ci = _tri(c)
    causal = ri >= ci
    pre = _bdot(glr, w2, "nn") + gate_bias
    log_a = (jnp.minimum(pre, 0.0) - jnp.log(1.0 + jnp.exp(-jnp.maximum(pre, -pre)))) * (1.0 / 16.0)
    gcum = _hdot(causal.astype(F32), log_a, "nn")
    row = lax.broadcasted_iota(jnp.int32, (c, G), 0)
    g_mid = jnp.sum(jnp.where(row == c // 2, gcum, 0.0), axis=0, keepdims=True)
    g_last = jnp.sum(log_a, axis=0, keepdims=True)
    qs = q * (HD ** -0.5)
    qa, ka = qs * jnp.exp(gcum - g_mid), k * jnp.exp(g_mid - gcum)
    qg, kl, dec = qs * jnp.exp(gcum), k * jnp.exp(g_last - gcum), jnp.exp(g_last)
    ys, new_states = [], []
    for h in range(NH):
        sl = slice(HD * h, HD * (h + 1))
        attn = jnp.where(causal, _bdot(qa[:, sl], ka[:, sl], "nt"), 0.0)
        st = states_t[h]
        o = _bdot(attn, v[:, sl], "nn") + _bdot(qg[:, sl], st, "nt")
        new_states.append(st * dec[:, sl] + _bdot(v[:, sl], kl[:, sl], "tn"))
        ys.append(_head_out(o, z[:, sl], nw))
    return jnp.concatenate(ys, axis=1), new_states


def _rms(x, nw):
    r = lax.rsqrt(jnp.mean(x * x, axis=-1, keepdims=True) + EPS)
    xh = x * r
    return xh * nw, xh, r


def _rms_bwd(g, xh, r, nw):
    gw = g * nw
    return r * (gw - xh * jnp.mean(gw * xh, axis=-1, keepdims=True)), jnp.sum(g * xh, axis=0, keepdims=True)


def _in_proj(x, nw, wp, tm=256):
    t = x.shape[0]

    def body(x_ref, nw_ref, w_ref, h_ref, p_ref):
        h = _rms(x_ref[...], nw_ref[...])[0].astype(BF16)
        h_ref[...] = h
        p_ref[...] = jnp.dot(h, w_ref[...], preferred_element_type=F32)

    return pl.pallas_call(
        body, name="in_proj", grid=(t // tm,),
        in_specs=[pl.BlockSpec((tm, D), lambda i: (i, 0)), _resident((1, D)), _resident((D, P))],
        out_specs=[pl.BlockSpec((tm, D), lambda i: (i, 0)), pl.BlockSpec((tm, P), lambda i: (i, 0))],
        out_shape=[jax.ShapeDtypeStruct((t, D), BF16), jax.ShapeDtypeStruct((t, P), F32)],
        compiler_params=_cparams("parallel"))(x, nw, wp)


def _out_mlp(x, ys, w_out, nw2, w_gu, w_down, tm=256):
    t = x.shape[0]

    def body(x_ref, ya, yb, yc, yd, wo_ref, nw_ref, wgu_ref, wd_ref, mix_ref, x1_ref, h2_ref, gu_ref, act_ref, x2_ref):
        mix = jnp.concatenate([ya[...], yb[...], yc[...], yd[...]], axis=1)
        mix_ref[...] = mix
        x1 = x_ref[...] + jnp.dot(mix, wo_ref[...], preferred_element_type=F32)
        x1_ref[...] = x1
        h2 = _rms(x1, nw_ref[...])[0].astype(BF16)
        h2_ref[...] = h2
        gu = jnp.dot(h2, wgu_ref[...], preferred_element_type=F32)
        gu_ref[...] = gu.astype(BF16)
        gate, up = gu[:, :FF], gu[:, FF:]
        act = (gate * _sigmoid(gate) * up).astype(BF16)
        act_ref[...] = act
        x2_ref[...] = x1 + jnp.dot(act, wd_ref[...], preferred_element_type=F32)

    row = lambda w: pl.BlockSpec((tm, w), lambda i: (i, 0))
    return pl.pallas_call(
        body, name="out_mlp", grid=(t // tm,),
        in_specs=[row(D), row(G), row(G), row(G), row(G), _resident((D, D)), _resident((1, D)),
                  _resident((D, 2 * FF)), _resident((FF, D))],
        out_specs=[row(D), row(D), row(D), row(2 * FF), row(FF), row(D)],
        out_shape=[jax.ShapeDtypeStruct((t, D), BF16), jax.ShapeDtypeStruct((t, D), F32),
                   jax.ShapeDtypeStruct((t, D), BF16), jax.ShapeDtypeStruct((t, 2 * FF), BF16),
                   jax.ShapeDtypeStruct((t, FF), BF16), jax.ShapeDtypeStruct((t, D), F32)],
        compiler_params=_cparams("parallel"))(x, *ys, w_out, nw2, w_gu, w_down)


def _loss_head(x, nw, target, tm=512):
    t = x.shape[0]

    def body(x_ref, nw_ref, tg_ref, loss_ref, dnw_ref, dx_ref):
        @pl.when(pl.program_id(0) == 0)
        def _():
            loss_ref[...] = jnp.zeros_like(loss_ref)
            dnw_ref[...] = jnp.zeros_like(dnw_ref)
        nw_v = nw_ref[...]
        y, xh, r = _rms(x_ref[...], nw_v)
        err = y - tg_ref[...]
        loss_ref[...] += 0.5 * jnp.sum(err * err) * (1.0 / D)
        dx, dnw = _rms_bwd(err * (1.0 / D), xh, r, nw_v)
        dx_ref[...] = dx
        dnw_ref[...] += dnw

    return pl.pallas_call(
        body, name="loss_head", grid=(t // tm,),
        in_specs=[pl.BlockSpec((tm, D), lambda i: (i, 0)), _resident((1, D)), pl.BlockSpec((tm, D), lambda i: (i, 0))],
        out_specs=[_acc((8, LANES)), _acc((1, D)), pl.BlockSpec((tm, D), lambda i: (i, 0))],
        out_shape=[jax.ShapeDtypeStruct((8, LANES), F32), jax.ShapeDtypeStruct((1, D), F32),
                   jax.ShapeDtypeStruct((t, D), F32)],
        compiler_params=_cparams("arbitrary"))(x, nw, target)


def _mlp_out_bwd(dx2, x1, gu, nw2, w_down, w_gu, w_out, tm=256):
    t = dx2.shape[0]

    def body(dx2_ref, x1_ref, gu_ref, nw_ref, wd_ref, wgu_ref, wo_ref, dgu_ref, dx1_ref, dmix_ref, dnw_ref):
        @pl.when(pl.program_id(0) == 0)
        def _():
            dnw_ref[...] = jnp.zeros_like(dnw_ref)
        dx2 = dx2_ref[...]
        dact = _dg(dx2.astype(BF16), wd_ref[...], 1, 1)
        gu = gu_ref[...].astype(F32)
        gate, up = gu[:, :FF], gu[:, FF:]
        s = _sigmoid(gate)
        dgu = jnp.concatenate([dact * up * (s * (1.0 + gate * (1.0 - s))), dact * (gate * s)], axis=1).astype(BF16)
        dgu_ref[...] = dgu
        dh2 = _dg(dgu, wgu_ref[...], 1, 1)
        nw_v = nw_ref[...]
        _, xh, r = _rms(x1_ref[...], nw_v)
        dxn, dnw = _rms_bwd(dh2, xh, r, nw_v)
        dx1 = dx2 + dxn
        dx1_ref[...] = dx1
        dnw_ref[...] += dnw
        dmix_ref[...] = _dg(dx1.astype(BF16), wo_ref[...], 1, 1)

    row = lambda w: pl.BlockSpec((tm, w), lambda i: (i, 0))
    return pl.pallas_call(
        body, name="mlp_out_bwd", grid=(t // tm,),
        in_specs=[row(D), row(D), row(2 * FF), _resident((1, D)), _resident((FF, D)), _resident((D, 2 * FF)),
                  _resident((D, D))],
        out_specs=[row(2 * FF), row(D), row(D), _acc((1, D))],
        out_shape=[jax.ShapeDtypeStruct((t, 2 * FF), BF16), jax.ShapeDtypeStruct((t, D), F32),
                   jax.ShapeDtypeStruct((t, D), F32), jax.ShapeDtypeStruct((1, D), F32)],
        compiler_params=_cparams("arbitrary"))(dx2, x1, gu, nw2, w_down, w_gu, w_out)


_DP_WIDTHS = (768, 768, 512, 768, 256, 256, 128, 128)


def _in_proj_bwd(dps, x, dx1, nw, wp, tm=256):
    t = x.shape[0]

    def body(*refs):
        dp_refs, (x_ref, dx1_ref, nw_ref, w_ref, dp_ref, dx_ref, dnw_ref) = refs[:8], refs[8:]

        @pl.when(pl.program_id(0) == 0)
        def _():
            dnw_ref[...] = jnp.zeros_like(dnw_ref)
        dp = jnp.concatenate([r[...] for r in dp_refs], axis=1)
        dp_ref[...] = dp
        dh = _dg(dp, w_ref[...], 1, 1)
        nw_v = nw_ref[...]
        _, xh, r = _rms(x_ref[...], nw_v)
        dxn, dnw = _rms_bwd(dh, xh, r, nw_v)
        dx_ref[...] = dx1_ref[...] + dxn
        dnw_ref[...] += dnw

    row = lambda w: pl.BlockSpec((tm, w), lambda i: (i, 0))
    return pl.pallas_call(
        body, name="in_proj_bwd", grid=(t // tm,),
        in_specs=[row(w) for w in _DP_WIDTHS] + [row(D), row(D), _resident((1, D)), _resident((D, P))],
        out_specs=[row(P), row(D), _acc((1, D))],
        out_shape=[jax.ShapeDtypeStruct((t, P), BF16), jax.ShapeDtypeStruct((t, D), F32),
                   jax.ShapeDtypeStruct((1, D), F32)],
        compiler_params=_cparams("arbitrary"))(*dps, x, dx1, nw, wp)


def _matmul_tn(a, b, name, tn=512, tt=512):
    t, m = a.shape
    n = b.shape[1]
    tn = min(tn, n)

    def body(a_ref, b_ref, o_ref):
        @pl.when(pl.program_id(1) == 0)
        def _():
            o_ref[...] = jnp.zeros_like(o_ref)
        o_ref[...] += _dg(a_ref[...].astype(BF16), b_ref[...].astype(BF16), 0, 0)

    return pl.pallas_call(
        body, name=name, grid=(n // tn, t // tt),
        in_specs=[pl.BlockSpec((tt, m), lambda j, i: (i, 0)), pl.BlockSpec((tt, tn), lambda j, i: (i, j))],
        out_specs=pl.BlockSpec((m, tn), lambda j, i: (0, j)),
        out_shape=jax.ShapeDtypeStruct((m, n), F32),
        compiler_params=_cparams("parallel", "arbitrary"))(a, b)


_A_BLK = 3


def _sgu_fwd(p, ln_w, ln_b, ws, bs_t, tm=256):
    t = p.shape[0]

    def body(uv_ref, lw_ref, lb_ref, ws_ref, bs_ref, y_ref):
        ws_v = [ws_ref[h] for h in range(NH)]
        for c in range(tm // SGU_C):
            rows = pl.ds(c * SGU_C, SGU_C)
            y_ref[rows, :] = _sgu_chunk(uv_ref[rows, :], lw_ref[...], lb_ref[...], ws_v, bs_ref[...]).astype(BF16)

    return pl.pallas_call(
        body, name="sgu_fwd", grid=(t // tm,),
        in_specs=[pl.BlockSpec((tm, 2 * G), lambda i: (i, _A_BLK)), _resident((1, G)), _resident((1, G)),
                  _resident((NH, SGU_C, SGU_C)), _resident((SGU_C, NH))],
        out_specs=pl.BlockSpec((tm, G), lambda i: (i, 0)),
        out_shape=jax.ShapeDtypeStruct((t, G), BF16),
        compiler_params=_cparams("parallel"))(p, ln_w, ln_b, ws, bs_t)


def _sgu_bwd(p, dmix, ln_w, ln_b, ws, bs_t, tm=256):
    t = p.shape[0]

    def body(uv_ref, dy_ref, lw_ref, lb_ref, ws_ref, bs_ref, duv_ref, dlw_ref, dlb_ref, dws_ref, dbs_ref):
        @pl.when(pl.program_id(0) == 0)
        def _():
            for r in (dlw_ref, dlb_ref, dws_ref, dbs_ref):
                r[...] = jnp.zeros_like(r)
        ws_v = [ws_ref[h] for h in range(NH)]
        for c in range(tm // SGU_C):
            rows = pl.ds(c * SGU_C, SGU_C)
            _, vjp = jax.vjp(_sgu_chunk, uv_ref[rows, :], lw_ref[...], lb_ref[...], ws_v, bs_ref[...])
            duv, dlw, dlb, dws, dbs = vjp(dy_ref[rows, :])
            duv_ref[rows, :] = duv.astype(BF16)
            dlw_ref[...] += dlw
            dlb_ref[...] += dlb
            dbs_ref[...] += dbs
            for h in range(NH):
                dws_ref[h] += dws[h]

    return pl.pallas_call(
        body, name="sgu_bwd", grid=(t // tm,),
        in_specs=[pl.BlockSpec((tm, 2 * G), lambda i: (i, _A_BLK)), pl.BlockSpec((tm, G), lambda i: (i, 0)),
                  _resident((1, G)), _resident((1, G)), _resident((NH, SGU_C, SGU_C)), _resident((SGU_C, NH))],
        out_specs=[pl.BlockSpec((tm, 2 * G), lambda i: (i, 0)), _acc((1, G)), _acc((1, G)),
                   _acc((NH, SGU_C, SGU_C)), _acc((SGU_C, NH))],
        out_shape=[jax.ShapeDtypeStruct((t, 2 * G), BF16), jax.ShapeDtypeStruct((1, G), F32),
                   jax.ShapeDtypeStruct((1, G), F32), jax.ShapeDtypeStruct((NH, SGU_C, SGU_C), F32),
                   jax.ShapeDtypeStruct((SGU_C, NH), F32)],
        compiler_params=_cparams("arbitrary"))(p, dmix, ln_w, ln_b, ws, bs_t)


HALO = 8


def _prev_halo(tm, width, col):
    return pl.BlockSpec((HALO, width), lambda i: (jnp.maximum(i * (tm // HALO) - 1, 0), col))


def _next_halo(tm, width, col, t):
    return pl.BlockSpec((HALO, width), lambda i: (jnp.minimum((i + 1) * (tm // HALO), t // HALO - 1), col))


def _with_prev(halo_ref, x_ref):
    halo = jnp.where(pl.program_id(0) == 0, 0.0, halo_ref[...])
    return jnp.concatenate([halo, x_ref[...]], axis=0)


def _with_next(x_ref, halo_ref):
    halo = jnp.where(pl.program_id(0) == pl.num_programs(0) - 1, 0.0, halo_ref[...])
    return jnp.concatenate([x_ref[...], halo], axis=0)


def _delay(ext, j):
    return ext if j == 0 else pltpu.roll(ext, j, axis=0)


def _advance(ext, j):
    return ext if j == 0 else pltpu.roll(ext, ext.shape[0] - j, axis=0)


def _causal_conv(ext, w, tm):
    kw = w.shape[0]
    out = None
    for k in range(kw):
        term = _delay(ext, kw - 1 - k)[HALO:HALO + tm] * w[k:k + 1, :]
        out = term if out is None else out + term
    return out


def _causal_conv_t(ext, w, tm):
    kw = w.shape[0]
    out = None
    for k in range(kw):
        term = _advance(ext, kw - 1 - k)[0:tm] * w[k:k + 1, :]
        out = term if out is None else out + term
    return out


def _conv_wgrad(ext, dy, kw, tm):
    return jnp.concatenate(
        [jnp.sum(_delay(ext, kw - 1 - k)[HALO:HALO + tm] * dy, axis=0, keepdims=True) for k in range(kw)], axis=0)


_B_GB, _B_GC, _B_H = 8, 9, 10
_C_QKV, _D_QKV = 0, 1
_C_Z, _D_Z = 11, 12
_C_AB, _D_GLR = 26, 27


def _sconv_fwd(p, w, tm=512):
    t = p.shape[0]

    def body(gb_ref, gc_ref, h_ref, gc_halo, h_halo, w_ref, y_ref):
        s_ext = _with_prev(gc_halo, gc_ref) * _with_prev(h_halo, h_ref)
        y_ref[...] = (gb_ref[...] * _causal_conv(s_ext, w_ref[...], tm)).astype(BF16)

    col = lambda c: pl.BlockSpec((tm, G), lambda i: (i, c))
    return pl.pallas_call(
        body, name="sconv_fwd", grid=(t // tm,),
        in_specs=[col(_B_GB), col(_B_GC), col(_B_H), _prev_halo(tm, G, _B_GC), _prev_halo(tm, G, _B_H), _resident((3, G))],
        out_specs=pl.BlockSpec((tm, G), lambda i: (i, 0)),
        out_shape=jax.ShapeDtypeStruct((t, G), BF16),
        compiler_params=_cparams("parallel"))(p, p, p, p, p, w)


def _sconv_bwd(p, dmix, w, tm=512):
    t = p.shape[0]

    def body(gb_ref, gc_ref, h_ref, gc_halo, h_halo, gb_next, dy_ref, dy_next, w_ref, dp_ref, dw_ref):
        @pl.when(pl.program_id(0) == 0)
        def _():
            dw_ref[...] = jnp.zeros_like(dw_ref)
        w_v = w_ref[...]
        s_ext = _with_prev(gc_halo, gc_ref) * _with_prev(h_halo, h_ref)
        dout = dy_ref[...]
        d_gb = dout * _causal_conv(s_ext, w_v, tm)
        dconv_ext = _with_next(dy_ref, dy_next) * _with_next(gb_ref, gb_next)
        ds = _causal_conv_t(dconv_ext, w_v, tm)
        dw_ref[...] += _conv_wgrad(s_ext, dconv_ext[0:tm], 3, tm)
        dp_ref[...] = jnp.concatenate([d_gb, ds * h_ref[...], ds * gc_ref[...]], axis=1).astype(BF16)

    col = lambda c: pl.BlockSpec((tm, G), lambda i: (i, c))
    return pl.pallas_call(
        body, name="sconv_bwd", grid=(t // tm,),
        in_specs=[col(_B_GB), col(_B_GC), col(_B_H), _prev_halo(tm, G, _B_GC), _prev_halo(tm, G, _B_H),
                  _next_halo(tm, G, _B_GB, t), col(1), _next_halo(tm, G, 1, t), _resident((3, G))],
        out_specs=[pl.BlockSpec((tm, 3 * G), lambda i: (i, 0)), _acc((3, G))],
        out_shape=[jax.ShapeDtypeStruct((t, 3 * G), BF16), jax.ShapeDtypeStruct((3, G), F32)],
        compiler_params=_cparams("arbitrary"))(p, p, p, p, p, p, dmix, dmix, w)


def _qkv_conv_fwd(p, w, tm=512):
    t = p.shape[0]

    def body(x_ref, x_halo, w_ref, y_ref):
        c = _causal_conv(_with_prev(x_halo, x_ref), w_ref[...], tm)
        y_ref[...] = c * _sigmoid(c)

    return pl.pallas_call(
        body, name="qkv_conv_fwd", grid=(t // tm,),
        in_specs=[pl.BlockSpec((tm, 3 * G), lambda i: (i, _C_QKV)), _prev_halo(tm, 3 * G, _C_QKV), _resident((4, 3 * G))],
        out_specs=pl.BlockSpec((tm, 3 * G), lambda i: (i, 0)),
        out_shape=jax.ShapeDtypeStruct((t, 3 * G), F32),
        compiler_params=_cparams("parallel"))(p, p, w)


def _qkv_conv_bwd(p, dy, w, tm=512):
    t = p.shape[0]

    def body(x_ref, x_halo, x_next, dy_ref, dy_next, w_ref, dx_ref, dw_ref):
        @pl.when(pl.program_id(0) == 0)
        def _():
            dw_ref[...] = jnp.zeros_like(dw_ref)
        w_v = w_ref[...]
        x_all = jnp.concatenate([_with_prev(x_halo, x_ref), jnp.where(
            pl.program_id(0) == pl.num_programs(0) - 1, 0.0, x_next[...])], axis=0)
        c = _causal_conv(x_all, w_v, tm + HALO)
        s = _sigmoid(c)
        dc_ext = _with_next(dy_ref, dy_next) * (s * (1.0 + c * (1.0 - s)))
        dx_ref[...] = _causal_conv_t(dc_ext, w_v, tm).astype(BF16)
        dw_ref[...] += _conv_wgrad(x_all[0:HALO + tm], dc_ext[0:tm], 4, tm)

    return pl.pallas_call(
        body, name="qkv_conv_bwd", grid=(t // tm,),
        in_specs=[pl.BlockSpec((tm, 3 * G), lambda i: (i, _C_QKV)), _prev_halo(tm, 3 * G, _C_QKV),
                  _next_halo(tm, 3 * G, _C_QKV, t), pl.BlockSpec((tm, 3 * G), lambda i: (i, 0)),
                  _next_halo(tm, 3 * G, 0, t), _resident((4, 3 * G))],
        out_specs=[pl.BlockSpec((tm, 3 * G), lambda i: (i, 0)), _acc((4, 3 * G))],
        out_shape=[jax.ShapeDtypeStruct((t, 3 * G), BF16), jax.ShapeDtypeStruct((4, 3 * G), F32)],
        compiler_params=_cparams("arbitrary"))(p, p, p, dy, dy, w)


def _recurrent_fwd(name, chunk_fn, qkv, qkv_blk, p, z_blk, small_blk, params, cps=2):
    t = qkv.shape[0]
    c = 64
    tm = c * cps
    nparam = len(params)

    def body(*refs):
        qkv_ref, z_ref, sm_ref = refs[:3]
        prm = [r[...] for r in refs[3:3 + nparam]]
        y_ref, st_ref, state = refs[3 + nparam:]

        @pl.when(pl.program_id(0) == 0)
        def _():
            state[...] = jnp.zeros_like(state)
        for j in range(cps):
            rows = pl.ds(j * c, c)
            st = [state[h] for h in range(NH)]
            for h in range(NH):
                st_ref[j, h] = st[h]
            y, new = chunk_fn(qkv_ref[rows, 0:G], qkv_ref[rows, G:2 * G], qkv_ref[rows, 2 * G:3 * G], sm_ref[rows, :],
                              z_ref[rows, :], st, *prm)
            y_ref[rows, :] = y.astype(BF16)
            for h in range(NH):
                state[h] = new[h]

    return pl.pallas_call(
        body, name=name, grid=(t // tm,),
        in_specs=[pl.BlockSpec((tm, 3 * G), lambda i: (i, qkv_blk)), pl.BlockSpec((tm, G), lambda i: (i, z_blk)),
                  pl.BlockSpec((tm, LANES), lambda i: (i, small_blk))] + [_resident(a.shape) for a in params],
        out_specs=[pl.BlockSpec((tm, G), lambda i: (i, 0)), pl.BlockSpec((cps, NH, HD, HD), lambda i: (i, 0, 0, 0))],
        out_shape=[jax.ShapeDtypeStruct((t, G), BF16), jax.ShapeDtypeStruct((t // c, NH, HD, HD), F32)],
        scratch_shapes=[pltpu.VMEM((NH, HD, HD), F32)],
        compiler_params=_cparams("arbitrary"))(qkv, p, p, *params)


def _recurrent_bwd(name, chunk_fn, qkv, qkv_blk, p, z_blk, small_blk, states, dmix, dmix_blk, params, dqkv_dtype,
                   cps=2):
    t = qkv.shape[0]
    c = 64
    tm = c * cps
    n = t // tm
    nparam = len(params)

    def body(*refs):
        qkv_ref, z_ref, sm_ref, st_ref, dy_ref = refs[:5]
        prm = [r[...] for r in refs[5:5 + nparam]]
        dqkv_ref, dz_ref, dsm_ref = refs[5 + nparam:8 + nparam]
        dprm_refs = refs[8 + nparam:8 + 2 * nparam]
        dstate = refs[8 + 2 * nparam]

        @pl.when(pl.program_id(0) == 0)
        def _():
            dstate[...] = jnp.zeros_like(dstate)
            for r in dprm_refs:
                r[...] = jnp.zeros_like(r)
        for j in reversed(range(cps)):
            rows = pl.ds(j * c, c)
            st = [st_ref[j, h] for h in range(NH)]
            _, vjp = jax.vjp(chunk_fn, qkv_ref[rows, 0:G], qkv_ref[rows, G:2 * G], qkv_ref[rows, 2 * G:3 * G],
                             sm_ref[rows, :], z_ref[rows, :], st, *prm)
            grads = vjp((dy_ref[rows, :], [dstate[h] for h in range(NH)]))
            dq, dk, dv, dsm, dz, dst = grads[:6]
            dqkv_ref[rows, :] = jnp.concatenate([dq, dk, dv], axis=1).astype(dqkv_dtype)
            dz_ref[rows, :] = dz.astype(BF16)
            dsm_ref[rows, :] = dsm.astype(BF16)
            for h in range(NH):
                dstate[h] = dst[h]
            for r, g in zip(dprm_refs, grads[6:]):
                r[...] += g

    rev = lambda w, blk: pl.BlockSpec((tm, w), lambda i: (n - 1 - i, blk))
    return pl.pallas_call(
        body, name=name, grid=(n,),
        in_specs=[rev(3 * G, qkv_blk), rev(G, z_blk), rev(LANES, small_blk),
                  pl.BlockSpec((cps, NH, HD, HD), lambda i: (n - 1 - i, 0, 0, 0)), rev(G, dmix_blk)]
        + [_resident(a.shape) for a in params],
        out_specs=[rev(3 * G, 0), rev(G, 0), rev(LANES, 0)] + [_acc(a.shape) for a in params],
        out_shape=[jax.ShapeDtypeStruct((t, 3 * G), dqkv_dtype), jax.ShapeDtypeStruct((t, G), BF16),
                   jax.ShapeDtypeStruct((t, LANES), BF16)] + [jax.ShapeDtypeStruct(a.shape, F32) for a in params],
        scratch_shapes=[pltpu.VMEM((NH, HD, HD), F32)],
        compiler_params=_cparams("arbitrary"))(qkv, p, p, states, dmix, *params)


def _adamw_math(w, g, m, v):
    m = ADAM_B1 * m + (1.0 - ADAM_B1) * g
    v = ADAM_B2 * v + (1.0 - ADAM_B2) * (g * g)
    m_hat = m / (1.0 - ADAM_B1 ** ADAM_STEP)
    v_hat = v / (1.0 - ADAM_B2 ** ADAM_STEP)
    return -ADAM_LR * (m_hat / (jnp.sqrt(v_hat) + ADAM_EPS) + ADAM_WD * w), m, v


def _adamw(parts, w, m, v, name, rows):
    n, r = parts.shape[0], w.shape[0]

    def body(p_ref, w_ref, m_ref, v_ref, g_ref, d_ref, nm_ref, nv_ref):
        g = p_ref[0].astype(F32)
        for k in range(1, n):
            g = g + p_ref[k].astype(F32)
        g_ref[...] = g
        d_ref[...], nm_ref[...], nv_ref[...] = _adamw_math(w_ref[...], g, m_ref[...], v_ref[...])

    blk = pl.BlockSpec((rows, LANES), lambda i: (i, 0))
    return pl.pallas_call(
        body, name=name, grid=(r // rows,),
        in_specs=[pl.BlockSpec((n, rows, LANES), lambda i: (0, i, 0)), blk, blk, blk],
        out_specs=[blk] * 4, out_shape=[jax.ShapeDtypeStruct((r, LANES), F32)] * 4,
        compiler_params=_cparams("parallel"))(parts, w, m, v)


def _sum_parts(parts, name):
    def body(p_ref, o_ref):
        g = p_ref[0]
        for k in range(1, parts.shape[0]):
            g = g + p_ref[k]
        o_ref[...] = g

    return pl.pallas_call(body, name=name, out_shape=jax.ShapeDtypeStruct(parts.shape[1:], F32),
                          compiler_params=_cparams())(parts)


_ANY = pl.BlockSpec(memory_space=pl.ANY)


def _place():
    return lax.axis_index("x"), lax.axis_index("y"), lax.axis_index("c")


def _all_gather(slab, name):
    def body(x_ref, out_ref, send_sems, recv_sems, local_sem):
        x, y, c = _place()
        me, sibling = (x, y, c), (x, y, 1 - c)
        chips = [(1 - x, y), (x, 1 - y), (1 - x, 1 - y)]

        def slot(px, py, pc):
            return out_ref.at[4 * px + 2 * py + pc]

        def copy(k, block, to, src=None):
            return pltpu.make_async_remote_copy(
                src_ref=slot(*block) if src is None else src, dst_ref=slot(*block),
                send_sem=send_sems.at[k], recv_sem=recv_sems.at[k], device_id=to, device_id_type=pl.DeviceIdType.MESH)

        mine = pltpu.make_async_copy(x_ref, slot(*me), local_sem)
        mine.start()
        first = [copy(0, me, sibling, src=x_ref)]
        first += [copy(1 + j, me, (*chip, c), src=x_ref) for j, chip in enumerate(chips)]
        for cp in first:
            cp.start()
        passed = [copy(4 + j, (*chip, c), sibling) for j, chip in enumerate(chips)]
        for j, chip in enumerate(chips):
            copy(1 + j, (*chip, c), me).wait_recv()
            passed[j].start()
        copy(0, sibling, me).wait_recv()
        for j, chip in enumerate(chips):
            copy(4 + j, (*chip, 1 - c), me).wait_recv()
        for cp in first + passed:
            cp.wait_send()
        mine.wait()

    return pl.pallas_call(
        body, name=name, out_shape=jax.ShapeDtypeStruct((N_DEV,) + slab.shape, slab.dtype),
        in_specs=[_ANY], out_specs=_ANY,
        scratch_shapes=[pltpu.SemaphoreType.DMA((7,)), pltpu.SemaphoreType.DMA((7,)), pltpu.SemaphoreType.DMA])(slab)


def _all_to_all(slabs, name):
    def body(x_ref, out_ref, send_sems, recv_sems, local_sem):
        x, y, c = _place()
        me = 4 * x + 2 * y + c

        def peer(k):
            return (x ^ ((k >> 2) & 1), y ^ ((k >> 1) & 1), c ^ (k & 1))

        def copy(k):
            px, py, pc = peer(k)
            return pltpu.make_async_remote_copy(
                src_ref=x_ref.at[4 * px + 2 * py + pc], dst_ref=out_ref.at[me],
                send_sem=send_sems.at[k - 1], recv_sem=recv_sems.at[k - 1], device_id=(px, py, pc),
                device_id_type=pl.DeviceIdType.MESH)

        mine = pltpu.make_async_copy(x_ref.at[me], out_ref.at[me], local_sem)
        mine.start()
        copies = [copy(k) for k in range(1, N_DEV)]
        for cp in copies:
            cp.start()
        for k in range(1, N_DEV):
            px, py, pc = peer(k)
            pltpu.make_async_remote_copy(
                src_ref=x_ref.at[me], dst_ref=out_ref.at[4 * px + 2 * py + pc], send_sem=send_sems.at[k - 1],
                recv_sem=recv_sems.at[k - 1], device_id=(px, py, pc), device_id_type=pl.DeviceIdType.MESH).wait_recv()
        for cp in copies:
            cp.wait_send()
        mine.wait()

    return pl.pallas_call(
        body, name=name, out_shape=jax.ShapeDtypeStruct(slabs.shape, slabs.dtype), in_specs=[_ANY], out_specs=_ANY,
        scratch_shapes=[pltpu.SemaphoreType.DMA((7,)), pltpu.SemaphoreType.DMA((7,)), pltpu.SemaphoreType.DMA])(slabs)


def _pad_cols(w):
    parts = []
    for a, b, z in _PAD_SEGS:
        parts.append(w[..., a:b])
        if z:
            parts.append(jnp.zeros(w.shape[:-1] + (z,), w.dtype))
    return jnp.concatenate(parts, axis=-1)


def _unpad_cols(wp):
    pieces, off = [], 0
    for a, b, z in _PAD_SEGS:
        pieces.append((a, wp[..., off:off + (b - a)]))
        off += (b - a) + z
    return jnp.concatenate([piece for _, piece in sorted(pieces, key=lambda ap: ap[0])], axis=-1)


def _to_slab(parts, rows):
    flat = jnp.concatenate([a.reshape(-1) for a in parts])
    return jnp.pad(flat, (0, rows * LANES - flat.shape[0])).reshape(rows, LANES)


def _from_slab(slab, shapes):
    flat, out, off = slab.reshape(-1), [], 0
    for s in shapes:
        n = 1
        for d in s:
            n *= d
        out.append(flat[off:off + n].reshape(s))
        off += n
    return out


def _local_grads(x, target, wts):
    saved = []
    for l in range(DEPTH):
        w = {k: v[l] for k, v in wts.items() if k != "final_norm_w"}
        h, p = _in_proj(x, w["norm1_w"][None], w["w_in"])
        ya = _sgu_fwd(p, w["sgu_ln_w"][None], w["sgu_ln_b"][None], w["sgu_w_spatial"], w["sgu_b_spatial"].T)
        yb = _sconv_fwd(p, w["sc_conv_w"])
        qkv_c = _qkv_conv_fwd(p, w["dn_conv_w"])
        dn_params = (w["dn_a_log"][None], w["dn_dt_bias"][None], w["dn_norm_w"][None])
        yc, st_c = _recurrent_fwd("dn_fwd", _dn_chunk, qkv_c, 0, p, _C_Z, _C_AB, dn_params)
        w2 = jnp.pad(w["gla_w_gate2"], ((0, LANES - 16), (0, 0)))
        gla_params = (w2, w["gla_gate_bias"][None], w["gla_norm_w"][None])
        yd, st_d = _recurrent_fwd("gla_fwd", _gla_chunk, p, _D_QKV, p, _D_Z, _D_GLR, gla_params)
        mix, x1, h2, gu, act, x2 = _out_mlp(x, (ya, yb, yc, yd), w["w_out"], w["norm2_w"][None], w["w_gate_up"],
                                            w["w_down"])
        saved.append(dict(w=w, x=x, h=h, p=p, qkv_c=qkv_c, st_c=st_c, st_d=st_d, mix=mix, x1=x1, h2=h2,
                          gu=gu, act=act, dn_params=dn_params, gla_params=gla_params))
        x = x2
    loss, d_fnw, dx = _loss_head(x, wts["final_norm_w"][None], target)
    grads = {k: [None] * DEPTH for k in wts if k != "final_norm_w"}
    grads["final_norm_w"] = d_fnw[0]
    for l in reversed(range(DEPTH)):
        s = saved[l]
        w, p = s["w"], s["p"]
        dgu, dx1, dmix, d_nw2 = _mlp_out_bwd(dx, s["x1"], s["gu"], w["norm2_w"][None], w["w_down"], w["w_gate_up"],
                                             w["w_out"])
        g = {}
        g["norm2_w"] = d_nw2[0]
        g["w_gate_up"] = _matmul_tn(s["h2"], dgu, "wgrad_gate_up")
        g["w_down"] = _matmul_tn(s["act"], dx, "wgrad_down")
        g["w_out"] = _matmul_tn(s["mix"], dx1, "wgrad_out")
        d_a, g_lw, g_lb, g_ws, g_bs = _sgu_bwd(p, dmix, w["sgu_ln_w"][None], w["sgu_ln_b"][None], w["sgu_w_spatial"],
                                              w["sgu_b_spatial"].T)
        g["sgu_ln_w"], g["sgu_ln_b"], g["sgu_w_spatial"], g["sgu_b_spatial"] = g_lw[0], g_lb[0], g_ws, g_bs.T
        d_b, g["sc_conv_w"] = _sconv_bwd(p, dmix, w["sc_conv_w"])
        dqkv_c, dz_c, dab, g_alog, g_dt, g_cn = _recurrent_bwd(
            "dn_bwd", _dn_chunk, s["qkv_c"], 0, p, _C_Z, _C_AB, s["st_c"], dmix, 2, s["dn_params"], F32)
        g["dn_a_log"], g["dn_dt_bias"], g["dn_norm_w"] = g_alog[0], g_dt[0], g_cn[0]
        d_c, g["dn_conv_w"] = _qkv_conv_bwd(p, dqkv_c, w["dn_conv_w"])
        d_d, dz_d, dglr, g_w2, g_gb, g_dn = _recurrent_bwd(
            "gla_bwd", _gla_chunk, p, _D_QKV, p, _D_Z, _D_GLR, s["st_d"], dmix, 3, s["gla_params"], BF16)
        g["gla_w_gate2"], g["gla_gate_bias"], g["gla_norm_w"] = g_w2[:16], g_gb[0], g_dn[0]
        dp, dx, d_nw1 = _in_proj_bwd((d_c, d_d, d_a, d_b, dz_c, dz_d, dab, dglr), s["x"], dx1, w["norm1_w"][None],
                                     w["w_in"])
        g["norm1_w"] = d_nw1[0]
        g["w_in"] = _matmul_tn(s["h"], dp, "wgrad_in")
        for k, v in g.items():
            grads[k][l] = v
    grads = {k: (v if k == "final_norm_w" else jnp.stack(v)) for k, v in grads.items()}
    return loss[0, 0], dx, grads


_BIG = ("w_in", "w_out", "w_gate_up", "w_down")
_SMALL_SHARDED = ("sc_conv_w", "dn_conv_w", "gla_w_gate2")
_REPLICATED = ("norm1_w", "sgu_ln_w", "sgu_ln_b", "sgu_w_spatial", "sgu_b_spatial", "dn_a_log", "dn_dt_bias",
               "dn_norm_w", "gla_gate_bias", "gla_norm_w", "norm2_w", "final_norm_w")
_ORDER = ("norm1_w", "w_in", "sgu_ln_w", "sgu_ln_b", "sgu_w_spatial", "sgu_b_spatial", "sc_conv_w", "dn_conv_w",
          "dn_a_log", "dn_dt_bias", "dn_norm_w", "gla_w_gate2", "gla_gate_bias", "gla_norm_w", "w_out", "norm2_w",
          "w_gate_up", "w_down", "final_norm_w")
_BIG_ROWS = 26112
_BIG_TILE = 1536
_SMALL_ROWS = 1216
_OWN_ROWS = 1104


def _gather_cols(g):
    return jnp.transpose(g, (1, 2, 0, 3)).reshape(g.shape[1], g.shape[2], N_DEV * g.shape[3])


def _gather_rows(g):
    return jnp.transpose(g, (1, 0, 2, 3)).reshape(g.shape[1], N_DEV * g.shape[2], g.shape[3])


def _scatter_cols(w):
    return jnp.transpose(w.reshape(w.shape[0], w.shape[1], N_DEV, w.shape[2] // N_DEV), (2, 0, 1, 3))


def _scatter_rows(w):
    return jnp.transpose(w.reshape(w.shape[0], N_DEV, w.shape[1] // N_DEV, w.shape[2]), (1, 0, 2, 3))


def kernel(x, norm1_w, w_in, sgu_ln_w, sgu_ln_b, sgu_w_spatial, sgu_b_spatial, sc_conv_w, dn_conv_w, dn_a_log, dn_dt_bias, dn_norm_w, gla_w_gate2, gla_gate_bias, gla_norm_w, w_out, norm2_w, w_gate_up, w_down, final_norm_w, loss_target, m_norm1_w, m_w_in, m_sgu_ln_w, m_sgu_ln_b, m_sgu_w_spatial, m_sgu_b_spatial, m_sc_conv_w, m_dn_conv_w, m_dn_a_log, m_dn_dt_bias, m_dn_norm_w, m_gla_w_gate2, m_gla_gate_bias, m_gla_norm_w, m_w_out, m_norm2_w, m_w_gate_up, m_w_down, m_final_norm_w, v_norm1_w, v_w_in, v_sgu_ln_w, v_sgu_ln_b, v_sgu_w_spatial, v_sgu_b_spatial, v_sc_conv_w, v_dn_conv_w, v_dn_a_log, v_dn_dt_bias, v_dn_norm_w, v_gla_w_gate2, v_gla_gate_bias, v_gla_norm_w, v_w_out, v_norm2_w, v_w_gate_up, v_w_down, v_final_norm_w):
    args = dict(locals())
    wts = {k: args[k] for k in _ORDER}
    mom = {k: args["m_" + k] for k in _ORDER}
    var = {k: args["v_" + k] for k in _ORDER}
    me = 4 * lax.axis_index("x") + 2 * lax.axis_index("y") + lax.axis_index("c")

    big_shapes = [wts[k].shape for k in _BIG]
    got = _all_gather(_to_slab([wts[k].astype(BF16) for k in _BIG], _BIG_ROWS), "gather_weights")
    got = [jnp.stack(parts) for parts in zip(*[_from_slab(got[d], big_shapes) for d in range(N_DEV)])]
    full = dict(wts)
    full["w_in"] = _pad_cols(_gather_cols(got[0]))
    full["w_out"] = _gather_rows(got[1])
    full["w_gate_up"] = _gather_cols(got[2])
    full["w_down"] = _gather_rows(got[3])
    small_shapes = [wts[k].shape for k in _SMALL_SHARDED]
    got = _all_gather(_to_slab([wts[k] for k in _SMALL_SHARDED], 16), "gather_small_weights")
    got = [jnp.stack(parts) for parts in zip(*[_from_slab(got[d], small_shapes) for d in range(N_DEV)])]
    for k, g in zip(_SMALL_SHARDED, got):
        full[k] = _gather_cols(g)

    loss, dx, grads = _local_grads(x[0], loss_target[0], full)
    loss = lax.psum(loss, ("x", "y", "c"))

    per_dev = [_scatter_cols(_unpad_cols(grads["w_in"])), _scatter_rows(grads["w_out"]),
               _scatter_cols(grads["w_gate_up"]), _scatter_rows(grads["w_down"])]
    send = jnp.stack([_to_slab([a[d].astype(BF16) for a in per_dev], _BIG_ROWS) for d in range(N_DEV)])
    parts = _all_to_all(send, "exchange_grads")
    outs = _adamw(parts, _to_slab([wts[k] for k in _BIG], _BIG_ROWS), _to_slab([mom[k] for k in _BIG], _BIG_ROWS),
                  _to_slab([var[k] for k in _BIG], _BIG_ROWS), "adamw_large", _BIG_TILE)
    result = {}
    for kind, slab in zip(("grad", "delta", "new_m", "new_v"), outs):
        for k, a in zip(_BIG, _from_slab(slab, big_shapes)):
            result[kind, k] = a

    small = _REPLICATED + _SMALL_SHARDED
    parts = _all_gather(_to_slab([grads[k] for k in small], _SMALL_ROWS), "gather_small_grads")
    gs = dict(zip(small, _from_slab(_sum_parts(parts, "sum_small_grads"), [grads[k].shape for k in small])))
    own = lambda d, k: lax.dynamic_slice_in_dim(d, me * wts[k].shape[-1], wts[k].shape[-1], axis=d.ndim - 1)
    own_grads = _to_slab([gs[k] if k in _REPLICATED else own(gs[k], k) for k in small], _OWN_ROWS)
    outs = _adamw(own_grads[None], _to_slab([wts[k] for k in small], _OWN_ROWS),
                  _to_slab([mom[k] for k in small], _OWN_ROWS), _to_slab([var[k] for k in small], _OWN_ROWS),
                  "adamw_small", _OWN_ROWS)
    own_shapes = [wts[k].shape for k in small]
    for kind, slab in zip(("grad", "delta", "new_m", "new_v"), outs):
        for k, a in zip(small, _from_slab(slab, own_shapes)):
            result[kind, k] = a

    return (loss, dx[None], *[result[kind, k] for kind in ("grad", "delta", "new_m", "new_v") for k in _ORDER])
```

```python
import functools

import jax
import jax.numpy as jnp
from jax import lax
from jax.experimental import pallas as pl
from jax.experimental.pallas import tpu as pltpu

F32, BF16 = jnp.float32, jnp.bfloat16
DEPTH = 2
D = 1024
G = 256
NH, HD = 4, 64
FF = 2816
IN_COLS = 3352
P = 3584
EPS = 1e-6
SGU_C, DN_C, GLA_C = 128, 64, 64
N_DEV = 8
LANES = 128
VMEM_LIMIT = 56 * 2 ** 20

_PAD_SEGS = ((1280, 2048, 0),
             (2312, 3080, 0),
             (0, 512, 0),
             (512, 1280, 0),
             (2056, 2312, 0),
             (3096, 3352, 0),
             (2048, 2056, 120),
             (3080, 3096, 112))

ADAM_LR, ADAM_B1, ADAM_B2, ADAM_EPS, ADAM_WD, ADAM_STEP = 0.001, 0.9, 0.999, 1e-08, 0.01, 10


def _cparams(*sem):
    return pltpu.CompilerParams(dimension_semantics=sem, vmem_limit_bytes=VMEM_LIMIT)


def _resident(shape):
    return pl.BlockSpec(shape, lambda *_: (0,) * len(shape), pipeline_mode=pl.Buffered(1))


def _acc(shape):
    return pl.BlockSpec(shape, lambda *_: (0,) * len(shape))


def _dg(a, b, ca, cb, precision=None):
    lead = a.ndim - 2
    batch = ((0,), (0,)) if lead else ((), ())
    return lax.dot_general(a, b, (((ca + lead,), (cb + lead,)), batch), preferred_element_type=F32, precision=precision)


def _make_dot(cast, precision):
    def raw(a, b, form):
        ca = 0 if form == "tn" else 1
        cb = 1 if form == "nt" else 0
        return _dg(cast(a), cast(b), ca, cb, precision)

    @functools.partial(jax.custom_vjp, nondiff_argnums=(2,))
    def dot(a, b, form):
        return raw(a, b, form)

    def fwd(a, b, form):
        return raw(a, b, form), (a, b)

    def bwd(form, res, g):
        a, b = res
        if form == "nn":
            return raw(g, b, "nt"), raw(a, g, "tn")
        if form == "nt":
            return raw(g, b, "nn"), raw(g, a, "tn")
        return raw(b, g, "nt"), raw(a, g, "nn")

    dot.defvjp(fwd, bwd)
    return dot


_bdot = _make_dot(lambda t: t.astype(BF16), None)
_hdot = _make_dot(lambda t: t, lax.Precision.HIGHEST)


def _inv_unit_lower(low):
    n = low.shape[-1]
    eye = (lax.broadcasted_iota(jnp.int32, (n, n), 0) == lax.broadcasted_iota(jnp.int32, (n, n), 1)).astype(F32)
    p = -low
    inv = eye + p
    k = 1
    while 2 * k < n:
        p = _dg(p, p, 1, 0, lax.Precision.HIGHEST)
        inv = inv + _dg(inv, p, 1, 0, lax.Precision.HIGHEST)
        k *= 2
    return inv


@jax.custom_vjp
def _unit_lower_solve(low, rhs, inv):
    return _dg(inv, rhs, 1, 0, lax.Precision.HIGHEST)


def _uls_fwd(low, rhs, inv):
    sol = _dg(inv, rhs, 1, 0, lax.Precision.HIGHEST)
    return sol, (inv, sol)


def _uls_bwd(res, g):
    inv, sol = res
    d_rhs = _dg(inv, g, 0, 0, lax.Precision.HIGHEST)
    return -_dg(d_rhs, sol, 1, 1, lax.Precision.HIGHEST), d_rhs, jnp.zeros_like(inv)


_unit_lower_solve.defvjp(_uls_fwd, _uls_bwd)


def _sigmoid(x):
    return 1.0 / (1.0 + jnp.exp(-x))


def _softplus(x):
    return jnp.maximum(x, 0.0) + jnp.log(1.0 + jnp.exp(-jnp.maximum(x, -x)))


def _gelu(x):
    return 0.5 * x * (1.0 + jnp.tanh(0.7978845608028654 * (x + 0.044715 * (x * x * x))))


def _tri(n):
    ri = lax.broadcasted_iota(jnp.int32, (n, n), 0)
    ci = lax.broadcasted_iota(jnp.int32, (n, n), 1)
    return ri, ci


def _sgu_chunk(uv, ln_w, ln_b, ws, bs_t):
    u = _gelu(uv[:, :G])
    v = _gelu(uv[:, G:])
    mu = jnp.mean(v, axis=-1, keepdims=True)
    vc = v - mu
    var = jnp.mean(vc * vc, axis=-1, keepdims=True)
    vn = vc * lax.rsqrt(var + EPS) * ln_w + ln_b
    ri, ci = _tri(SGU_C)
    mixed = []
    for h in range(NH):
        wm = jnp.where(ri >= ci, ws[h], 0.0)
        mixed.append(_bdot(wm, vn[:, HD * h:HD * (h + 1)], "nn") + bs_t[:, h:h + 1])
    return u * jnp.concatenate(mixed, axis=1)


def _head_out(o, zh, nw):
    on = o * lax.rsqrt(jnp.mean(o * o, axis=-1, keepdims=True) + EPS) * nw
    return on * (zh * _sigmoid(zh))


def _dn_tile(q, k, v, ab, z, state, inv, a_log, dt_bias, nw):
    c = DN_C
    tm = q.shape[0]
    cps = tm // c
    ri, ci = _tri(tm)
    shift = c.bit_length() - 1
    same = jnp.right_shift(ri, shift) == jnp.right_shift(ci, shift)
    g4 = -jnp.exp(a_log) * _softplus(ab[:, 0:NH] + dt_bias)
    beta4 = _sigmoid(ab[:, NH:2 * NH])
    gc4 = _hdot((same & (ri >= ci)).astype(F32), g4, "nn")
    gr4 = _hdot(g4, (same & (ri <= ci)).astype(F32), "tn")
    pairs = [(slice(c * j, c * (j + 1)), h) for j in range(cps) for h in range(NH)]
    heads = lambda t: jnp.stack([t[rows, HD * h:HD * (h + 1)] for rows, h in pairs])
    col = lambda t: jnp.stack([t[rows, h:h + 1] for rows, h in pairs])
    qn, kn, vh = heads(q), heads(k), heads(v)
    qn = qn * lax.rsqrt(jnp.sum(qn * qn, axis=-1, keepdims=True) + EPS) * (HD ** -0.5)
    kn = kn * lax.rsqrt(jnp.sum(kn * kn, axis=-1, keepdims=True) + EPS)
    gc, beta = col(gc4), col(beta4)
    gr = jnp.stack([gr4[h:h + 1, rows] for rows, h in pairs])
    gl = jnp.sum(col(g4), axis=1, keepdims=True)
    r2, c2 = _tri(c)
    decay = jnp.exp(jnp.where(r2 >= c2, gc - gr, -jnp.inf))
    kb = kn * beta
    low = jnp.where(r2 > c2, _bdot(kb, kn, "nt") * decay, 0.0)
    eg = jnp.exp(gc)
    if inv is None:
        inv = _inv_unit_lower(low)
    sol = _unit_lower_solve(low, jnp.concatenate([vh * beta, kb * eg], axis=2), inv)
    u, w = sol[:, :, :HD], sol[:, :, HD:]
    attn = _bdot(qn, kn, "nt") * decay
    qg, kd, cd = qn * eg, kn * jnp.exp(gl - gc), jnp.exp(gl)
    ys = []
    for j in range(cps):
        b = slice(NH * j, NH * (j + 1))
        v_new = u[b] - _bdot(w[b], state, "nn")
        o = _bdot(qg[b], state, "nn") + _bdot(attn[b], v_new, "nn")
        state = state * cd[b] + _bdot(kd[b], v_new, "tn")
        on = o * lax.rsqrt(jnp.mean(o * o, axis=-1, keepdims=True) + EPS) * nw
        ys.append(jnp.concatenate([on[h] for h in range(NH)], axis=1))
    return jnp.concatenate(ys, axis=0) * (z * _sigmoid(z)), state, inv


def _gla_chunk(q, k, v, glr, z, states_t, w2, gate_bias, nw):
    c = GLA_C
    ri, ci = _tri(c)
    causal = ri >= ci
    pre = _bdot(glr, w2, "nn") + gate_bias
    log_a = (jnp.minimum(pre, 0.0) - jnp.log(1.0 + jnp.exp(-jnp.maximum(pre, -pre)))) * (1.0 / 16.0)
    gcum = _hdot(causal.astype(F32), log_a, "nn")
    row = lax.broadcasted_iota(jnp.int32, (c, G), 0)
    g_mid = jnp.sum(jnp.where(row == c // 2, gcum, 0.0), axis=0, keepdims=True)
    g_last = jnp.sum(log_a, axis=0, keepdims=True)
    qs = q * (HD ** -0.5)
    qa, ka = qs * jnp.exp(gcum - g_mid), k * jnp.exp(g_mid - gcum)
    qg, kl, dec = qs * jnp.exp(gcum), k * jnp.exp(g_last - gcum), jnp.exp(g_last)
    ys, new_states = [], []
    for h in range(NH):
        sl = slice(HD * h, HD * (h + 1))
        attn = jnp.where(causal, _bdot(qa[:, sl], ka[:, sl], "nt"), 0.0)
        st = states_t[h]
        o = _bdot(attn, v[:, sl], "nn") + _bdot(qg[:, sl], st, "nt")
        new_states.append(st * dec[:, sl] + _bdot(v[:, sl], kl[:, sl], "tn"))
        ys.append(_head_out(o, z[:, sl], nw))
    return jnp.concatenate(ys, axis=1), new_states


def _rms(x, nw):
    r = lax.rsqrt(jnp.mean(x * x, axis=-1, keepdims=True) + EPS)
    xh = x * r
    return xh * nw, xh, r


def _rms_bwd(g, xh, r, nw):
    gw = g * nw
    return r * (gw - xh * jnp.mean(gw * xh, axis=-1, keepdims=True)), jnp.sum(g * xh, axis=0, keepdims=True)


def _in_proj(x, nw, wp, tm=256):
    t = x.shape[0]

    def body(x_ref, nw_ref, w_ref, h_ref, p_ref):
        h = _rms(x_ref[...], nw_ref[...])[0].astype(BF16)
        h_ref[...] = h
        p_ref[...] = jnp.dot(h, w_ref[...], preferred_element_type=F32)

    return pl.pallas_call(
        body, name="in_proj", grid=(t // tm,),
        in_specs=[pl.BlockSpec((tm, D), lambda i: (i, 0)), _resident((1, D)), _resident((D, P))],
        out_specs=[pl.BlockSpec((tm, D), lambda i: (i, 0)), pl.BlockSpec((tm, P), lambda i: (i, 0))],
        out_shape=[jax.ShapeDtypeStruct((t, D), BF16), jax.ShapeDtypeStruct((t, P), F32)],
        compiler_params=_cparams("parallel"))(x, nw, wp)


def _out_mlp(x, ys, w_out, nw2, w_gu, w_down, tm=256):
    t = x.shape[0]

    def body(x_ref, ya, yb, yc, yd, wo_ref, nw_ref, wgu_ref, wd_ref, mix_ref, x1_ref, h2_ref, gu_ref, act_ref, x2_ref):
        mix = jnp.concatenate([ya[...], yb[...], yc[...], yd[...]], axis=1)
        mix_ref[...] = mix
        x1 = x_ref[...] + jnp.dot(mix, wo_ref[...], preferred_element_type=F32)
        x1_ref[...] = x1
        h2 = _rms(x1, nw_ref[...])[0].astype(BF16)
        h2_ref[...] = h2
        gu = jnp.dot(h2, wgu_ref[...], preferred_element_type=F32)
        gu_ref[...] = gu.astype(BF16)
        gate, up = gu[:, :FF], gu[:, FF:]
        act = (gate * _sigmoid(gate) * up).astype(BF16)
        act_ref[...] = act
        x2_ref[...] = x1 + jnp.dot(act, wd_ref[...], preferred_element_type=F32)

    row = lambda w: pl.BlockSpec((tm, w), lambda i: (i, 0))
    return pl.pallas_call(
        body, name="out_mlp", grid=(t // tm,),
        in_specs=[row(D), row(G), row(G), row(G), row(G), _resident((D, D)), _resident((1, D)),
                  _resident((D, 2 * FF)), _resident((FF, D))],
        out_specs=[row(D), row(D), row(D), row(2 * FF), row(FF), row(D)],
        out_shape=[jax.ShapeDtypeStruct((t, D), BF16), jax.ShapeDtypeStruct((t, D), F32),
                   jax.ShapeDtypeStruct((t, D), BF16), jax.ShapeDtypeStruct((t, 2 * FF), BF16),
                   jax.ShapeDtypeStruct((t, FF), BF16), jax.ShapeDtypeStruct((t, D), F32)],
        compiler_params=_cparams("parallel"))(x, *ys, w_out, nw2, w_gu, w_down)


def _loss_head(x, nw, target, tm=512):
    t = x.shape[0]

    def body(x_ref, nw_ref, tg_ref, loss_ref, dnw_ref, dx_ref):
        @pl.when(pl.program_id(0) == 0)
        def _():
            loss_ref[...] = jnp.zeros_like(loss_ref)
            dnw_ref[...] = jnp.zeros_like(dnw_ref)
        nw_v = nw_ref[...]
        y, xh, r = _rms(x_ref[...], nw_v)
        err = y - tg_ref[...]
        loss_ref[...] += 0.5 * jnp.sum(err * err) * (1.0 / D)
        dx, dnw = _rms_bwd(err * (1.0 / D), xh, r, nw_v)
        dx_ref[...] = dx
        dnw_ref[...] += dnw

    return pl.pallas_call(
        body, name="loss_head", grid=(t // tm,),
        in_specs=[pl.BlockSpec((tm, D), lambda i: (i, 0)), _resident((1, D)), pl.BlockSpec((tm, D), lambda i: (i, 0))],
        out_specs=[_acc((8, LANES)), _acc((1, D)), pl.BlockSpec((tm, D), lambda i: (i, 0))],
        out_shape=[jax.ShapeDtypeStruct((8, LANES), F32), jax.ShapeDtypeStruct((1, D), F32),
                   jax.ShapeDtypeStruct((t, D), F32)],
        compiler_params=_cparams("arbitrary"))(x, nw, target)


def _mlp_out_bwd(dx2, x1, gu, nw2, w_down, w_gu, w_out, tm=256):
    t = dx2.shape[0]

    def body(dx2_ref, x1_ref, gu_ref, nw_ref, wd_ref, wgu_ref, wo_ref, dgu_ref, dx1_ref, dmix_ref, dnw_ref):
        @pl.when(pl.program_id(0) == 0)
        def _():
            dnw_ref[...] = jnp.zeros_like(dnw_ref)
        dx2 = dx2_ref[...]
        dact = _dg(dx2.astype(BF16), wd_ref[...], 1, 1)
        gu = gu_ref[...].astype(F32)
        gate, up = gu[:, :FF], gu[:, FF:]
        s = _sigmoid(gate)
        dgu = jnp.concatenate([dact * up * (s * (1.0 + gate * (1.0 - s))), dact * (gate * s)], axis=1).astype(BF16)
        dgu_ref[...] = dgu
        dh2 = _dg(dgu, wgu_ref[...], 1, 1)
        nw_v = nw_ref[...]
        _, xh, r = _rms(x1_ref[...], nw_v)
        dxn, dnw = _rms_bwd(dh2, xh, r, nw_v)
        dx1 = dx2 + dxn
        dx1_ref[...] = dx1
        dnw_ref[...] += dnw
        dmix_ref[...] = _dg(dx1.astype(BF16), wo_ref[...], 1, 1)

    row = lambda w: pl.BlockSpec((tm, w), lambda i: (i, 0))
    return pl.pallas_call(
        body, name="mlp_out_bwd", grid=(t // tm,),
        in_specs=[row(D), row(D), row(2 * FF), _resident((1, D)), _resident((FF, D)), _resident((D, 2 * FF)),
                  _resident((D, D))],
        out_specs=[row(2 * FF), row(D), row(D), _acc((1, D))],
        out_shape=[jax.ShapeDtypeStruct((t, 2 * FF), BF16), jax.ShapeDtypeStruct((t, D), F32),
                   jax.ShapeDtypeStruct((t, D), F32), jax.ShapeDtypeStruct((1, D), F32)],
        compiler_params=_cparams("arbitrary"))(dx2, x1, gu, nw2, w_down, w_gu, w_out)


_DP_WIDTHS = (768, 768, 512, 768, 256, 256, 128, 128)


def _in_proj_bwd(dps, x, dx1, nw, wp, tm=256):
    t = x.shape[0]

    def body(*refs):
        dp_refs, (x_ref, dx1_ref, nw_ref, w_ref, dp_ref, dx_ref, dnw_ref) = refs[:8], refs[8:]

        @pl.when(pl.program_id(0) == 0)
        def _():
            dnw_ref[...] = jnp.zeros_like(dnw_ref)
        dp = jnp.concatenate([r[...] for r in dp_refs], axis=1)
        dp_ref[...] = dp
        dh = _dg(dp, w_ref[...], 1, 1)
        nw_v = nw_ref[...]
        _, xh, r = _rms(x_ref[...], nw_v)
        dxn, dnw = _rms_bwd(dh, xh, r, nw_v)
        dx_ref[...] = dx1_ref[...] + dxn
        dnw_ref[...] += dnw

    row = lambda w: pl.BlockSpec((tm, w), lambda i: (i, 0))
    return pl.pallas_call(
        body, name="in_proj_bwd", grid=(t // tm,),
        in_specs=[row(w) for w in _DP_WIDTHS] + [row(D), row(D), _resident((1, D)), _resident((D, P))],
        out_specs=[row(P), row(D), _acc((1, D))],
        out_shape=[jax.ShapeDtypeStruct((t, P), BF16), jax.ShapeDtypeStruct((t, D), F32),
                   jax.ShapeDtypeStruct((1, D), F32)],
        compiler_params=_cparams("arbitrary"))(*dps, x, dx1, nw, wp)


def _matmul_tn(a, b, name, tn=512, tt=512):
    t, m = a.shape
    n = b.shape[1]
    tn = min(tn, n)

    def body(a_ref, b_ref, o_ref):
        @pl.when(pl.program_id(1) == 0)
        def _():
            o_ref[...] = jnp.zeros_like(o_ref)
        o_ref[...] += _dg(a_ref[...].astype(BF16), b_ref[...].astype(BF16), 0, 0)

    return pl.pallas_call(
        body, name=name, grid=(n // tn, t // tt),
        in_specs=[pl.BlockSpec((tt, m), lambda j, i: (i, 0)), pl.BlockSpec((tt, tn), lambda j, i: (i, j))],
        out_specs=pl.BlockSpec((m, tn), lambda j, i: (0, j)),
        out_shape=jax.ShapeDtypeStruct((m, n), F32),
        compiler_params=_cparams("parallel", "arbitrary"))(a, b)


_A_BLK = 3


def _sgu_fwd(p, ln_w, ln_b, ws, bs_t, tm=256):
    t = p.shape[0]

    def body(uv_ref, lw_ref, lb_ref, ws_ref, bs_ref, y_ref):
        ws_v = [ws_ref[h] for h in range(NH)]
        for c in range(tm // SGU_C):
            rows = pl.ds(c * SGU_C, SGU_C)
            y_ref[rows, :] = _sgu_chunk(uv_ref[rows, :], lw_ref[...], lb_ref[...], ws_v, bs_ref[...]).astype(BF16)

    return pl.pallas_call(
        body, name="sgu_fwd", grid=(t // tm,),
        in_specs=[pl.BlockSpec((tm, 2 * G), lambda i: (i, _A_BLK)), _resident((1, G)), _resident((1, G)),
                  _resident((NH, SGU_C, SGU_C)), _resident((SGU_C, NH))],
        out_specs=pl.BlockSpec((tm, G), lambda i: (i, 0)),
        out_shape=jax.ShapeDtypeStruct((t, G), BF16),
        compiler_params=_cparams("parallel"))(p, ln_w, ln_b, ws, bs_t)


def _sgu_bwd(p, dmix, ln_w, ln_b, ws, bs_t, tm=256):
    t = p.shape[0]

    def body(uv_ref, dy_ref, lw_ref, lb_ref, ws_ref, bs_ref, duv_ref, dlw_ref, dlb_ref, dws_ref, dbs_ref):
        @pl.when(pl.program_id(0) == 0)
        def _():
            for r in (dlw_ref, dlb_ref, dws_ref, dbs_ref):
                r[...] = jnp.zeros_like(r)
        ws_v = [ws_ref[h] for h in range(NH)]
        for c in range(tm // SGU_C):
            rows = pl.ds(c * SGU_C, SGU_C)
            _, vjp = jax.vjp(_sgu_chunk, uv_ref[rows, :], lw_ref[...], lb_ref[...], ws_v, bs_ref[...])
            duv, dlw, dlb, dws, dbs = vjp(dy_ref[rows, :])
            duv_ref[rows, :] = duv.astype(BF16)
            dlw_ref[...] += dlw
            dlb_ref[...] += dlb
            dbs_ref[...] += dbs
            for h in range(NH):
                dws_ref[h] += dws[h]

    return pl.pallas_call(
        body, name="sgu_bwd", grid=(t // tm,),
        in_specs=[pl.BlockSpec((tm, 2 * G), lambda i: (i, _A_BLK)), pl.BlockSpec((tm, G), lambda i: (i, 0)),
                  _resident((1, G)), _resident((1, G)), _resident((NH, SGU_C, SGU_C)), _resident((SGU_C, NH))],
        out_specs=[pl.BlockSpec((tm, 2 * G), lambda i: (i, 0)), _acc((1, G)), _acc((1, G)),
                   _acc((NH, SGU_C, SGU_C)), _acc((SGU_C, NH))],
        out_shape=[jax.ShapeDtypeStruct((t, 2 * G), BF16), jax.ShapeDtypeStruct((1, G), F32),
                   jax.ShapeDtypeStruct((1, G), F32), jax.ShapeDtypeStruct((NH, SGU_C, SGU_C), F32),
                   jax.ShapeDtypeStruct((SGU_C, NH), F32)],
        compiler_params=_cparams("arbitrary"))(p, dmix, ln_w, ln_b, ws, bs_t)


HALO = 8


def _prev_halo(tm, width, col):
    return pl.BlockSpec((HALO, width), lambda i: (jnp.maximum(i * (tm // HALO) - 1, 0), col))


def _next_halo(tm, width, col, t):
    return pl.BlockSpec((HALO, width), lambda i: (jnp.minimum((i + 1) * (tm // HALO), t // HALO - 1), col))


def _with_prev(halo_ref, x_ref):
    halo = jnp.where(pl.program_id(0) == 0, 0.0, halo_ref[...])
    return jnp.concatenate([halo, x_ref[...]], axis=0)


def _with_next(x_ref, halo_ref):
    halo = jnp.where(pl.program_id(0) == pl.num_programs(0) - 1, 0.0, halo_ref[...])
    return jnp.concatenate([x_ref[...], halo], axis=0)


def _delay(ext, j):
    return ext if j == 0 else pltpu.roll(ext, j, axis=0)


def _advance(ext, j):
    return ext if j == 0 else pltpu.roll(ext, ext.shape[0] - j, axis=0)


def _causal_conv(ext, w, tm):
    kw = w.shape[0]
    out = None
    for k in range(kw):
        term = _delay(ext, kw - 1 - k)[HALO:HALO + tm] * w[k:k + 1, :]
        out = term if out is None else out + term
    return out


def _causal_conv_t(ext, w, tm):
    kw = w.shape[0]
    out = None
    for k in range(kw):
        term = _advance(ext, kw - 1 - k)[0:tm] * w[k:k + 1, :]
        out = term if out is None else out + term
    return out


def _conv_wgrad(ext, dy, kw, tm):
    return jnp.concatenate(
        [jnp.sum(_delay(ext, kw - 1 - k)[HALO:HALO + tm] * dy, axis=0, keepdims=True) for k in range(kw)], axis=0)


_B_GB, _B_GC, _B_H = 8, 9, 10
_C_QKV, _D_QKV = 0, 1
_C_Z, _D_Z = 11, 12
_C_AB, _D_GLR = 26, 27


def _sconv_fwd(p, w, tm=512):
    t = p.shape[0]

    def body(gb_ref, gc_ref, h_ref, gc_halo, h_halo, w_ref, y_ref):
        s_ext = _with_prev(gc_halo, gc_ref) * _with_prev(h_halo, h_ref)
        y_ref[...] = (gb_ref[...] * _causal_conv(s_ext, w_ref[...], tm)).astype(BF16)

    col = lambda c: pl.BlockSpec((tm, G), lambda i: (i, c))
    return pl.pallas_call(
        body, name="sconv_fwd", grid=(t // tm,),
        in_specs=[col(_B_GB), col(_B_GC), col(_B_H), _prev_halo(tm, G, _B_GC), _prev_halo(tm, G, _B_H), _resident((3, G))],
        out_specs=pl.BlockSpec((tm, G), lambda i: (i, 0)),
        out_shape=jax.ShapeDtypeStruct((t, G), BF16),
        compiler_params=_cparams("parallel"))(p, p, p, p, p, w)


def _sconv_bwd(p, dmix, w, tm=512):
    t = p.shape[0]

    def body(gb_ref, gc_ref, h_ref, gc_halo, h_halo, gb_next, dy_ref, dy_next, w_ref, dp_ref, dw_ref):
        @pl.when(pl.program_id(0) == 0)
        def _():
            dw_ref[...] = jnp.zeros_like(dw_ref)
        w_v = w_ref[...]
        s_ext = _with_prev(gc_halo, gc_ref) * _with_prev(h_halo, h_ref)
        dout = dy_ref[...]
        d_gb = dout * _causal_conv(s_ext, w_v, tm)
        dconv_ext = _with_next(dy_ref, dy_next) * _with_next(gb_ref, gb_next)
        ds = _causal_conv_t(dconv_ext, w_v, tm)
        dw_ref[...] += _conv_wgrad(s_ext, dconv_ext[0:tm], 3, tm)
        dp_ref[...] = jnp.concatenate([d_gb, ds * h_ref[...], ds * gc_ref[...]], axis=1).astype(BF16)

    col = lambda c: pl.BlockSpec((tm, G), lambda i: (i, c))
    return pl.pallas_call(
        body, name="sconv_bwd", grid=(t // tm,),
        in_specs=[col(_B_GB), col(_B_GC), col(_B_H), _prev_halo(tm, G, _B_GC), _prev_halo(tm, G, _B_H),
                  _next_halo(tm, G, _B_GB, t), col(1), _next_halo(tm, G, 1, t), _resident((3, G))],
        out_specs=[pl.BlockSpec((tm, 3 * G), lambda i: (i, 0)), _acc((3, G))],
        out_shape=[jax.ShapeDtypeStruct((t, 3 * G), BF16), jax.ShapeDtypeStruct((3, G), F32)],
        compiler_params=_cparams("arbitrary"))(p, p, p, p, p, p, dmix, dmix, w)


def _qkv_conv_fwd(p, w, tm=512):
    t = p.shape[0]

    def body(x_ref, x_halo, w_ref, y_ref):
        c = _causal_conv(_with_prev(x_halo, x_ref), w_ref[...], tm)
        y_ref[...] = c * _sigmoid(c)

    return pl.pallas_call(
        body, name="qkv_conv_fwd", grid=(t // tm,),
        in_specs=[pl.BlockSpec((tm, 3 * G), lambda i: (i, _C_QKV)), _prev_halo(tm, 3 * G, _C_QKV), _resident((4, 3 * G))],
        out_specs=pl.BlockSpec((tm, 3 * G), lambda i: (i, 0)),
        out_shape=jax.ShapeDtypeStruct((t, 3 * G), F32),
        compiler_params=_cparams("parallel"))(p, p, w)


def _qkv_conv_bwd(p, dy, w, tm=512):
    t = p.shape[0]

    def body(x_ref, x_halo, x_next, dy_ref, dy_next, w_ref, dx_ref, dw_ref):
        @pl.when(pl.program_id(0) == 0)
        def _():
            dw_ref[...] = jnp.zeros_like(dw_ref)
        w_v = w_ref[...]
        x_all = jnp.concatenate([_with_prev(x_halo, x_ref), jnp.where(
            pl.program_id(0) == pl.num_programs(0) - 1, 0.0, x_next[...])], axis=0)
        c = _causal_conv(x_all, w_v, tm + HALO)
        s = _sigmoid(c)
        dc_ext = _with_next(dy_ref, dy_next) * (s * (1.0 + c * (1.0 - s)))
        dx_ref[...] = _causal_conv_t(dc_ext, w_v, tm).astype(BF16)
        dw_ref[...] += _conv_wgrad(x_all[0:HALO + tm], dc_ext[0:tm], 4, tm)

    return pl.pallas_call(
        body, name="qkv_conv_bwd", grid=(t // tm,),
        in_specs=[pl.BlockSpec((tm, 3 * G), lambda i: (i, _C_QKV)), _prev_halo(tm, 3 * G, _C_QKV),
                  _next_halo(tm, 3 * G, _C_QKV, t), pl.BlockSpec((tm, 3 * G), lambda i: (i, 0)),
                  _next_halo(tm, 3 * G, 0, t), _resident((4, 3 * G))],
        out_specs=[pl.BlockSpec((tm, 3 * G), lambda i: (i, 0)), _acc((4, 3 * G))],
        out_shape=[jax.ShapeDtypeStruct((t, 3 * G), BF16), jax.ShapeDtypeStruct((4, 3 * G), F32)],
        compiler_params=_cparams("arbitrary"))(p, p, p, dy, dy, w)


def _dn_fwd(qkv, p, params, cps=4):
    t = qkv.shape[0]
    tm = DN_C * cps
    nb = cps * NH

    def body(qkv_ref, z_ref, ab_ref, alog_ref, dt_ref, nw_ref, y_ref, st_ref, inv_ref, state):
        @pl.when(pl.program_id(0) == 0)
        def _():
            state[...] = jnp.zeros_like(state)
        st_ref[0] = state[...]
        y, new, inv = _dn_tile(qkv_ref[:, 0:G], qkv_ref[:, G:2 * G], qkv_ref[:, 2 * G:3 * G], ab_ref[...], z_ref[...],
                               state[...], None, alog_ref[...], dt_ref[...], nw_ref[...])
        y_ref[...] = y.astype(BF16)
        inv_ref[...] = inv
        state[...] = new

    return pl.pallas_call(
        body, name="dn_fwd", grid=(t // tm,),
        in_specs=[pl.BlockSpec((tm, 3 * G), lambda i: (i, 0)), pl.BlockSpec((tm, G), lambda i: (i, _C_Z)),
                  pl.BlockSpec((tm, LANES), lambda i: (i, _C_AB))] + [_resident(a.shape) for a in params],
        out_specs=[pl.BlockSpec((tm, G), lambda i: (i, 0)), pl.BlockSpec((1, NH, HD, HD), lambda i: (i, 0, 0, 0)),
                   pl.BlockSpec((nb, DN_C, DN_C), lambda i: (i, 0, 0))],
        out_shape=[jax.ShapeDtypeStruct((t, G), BF16), jax.ShapeDtypeStruct((t // tm, NH, HD, HD), F32),
                   jax.ShapeDtypeStruct((t // DN_C * NH, DN_C, DN_C), F32)],
        scratch_shapes=[pltpu.VMEM((NH, HD, HD), F32)],
        compiler_params=_cparams("arbitrary"))(qkv, p, p, *params)


def _dn_bwd(qkv, p, states, inv, dmix, params, cps=4):
    t = qkv.shape[0]
    tm = DN_C * cps
    n = t // tm
    nb = cps * NH

    def body(qkv_ref, z_ref, ab_ref, st_ref, inv_ref, dy_ref, alog_ref, dt_ref, nw_ref,
             dqkv_ref, dz_ref, dab_ref, dalog_ref, ddt_ref, dnw_ref, dstate):
        dprm_refs = (dalog_ref, ddt_ref, dnw_ref)

        @pl.when(pl.program_id(0) == 0)
        def _():
            dstate[...] = jnp.zeros_like(dstate)
            for r in dprm_refs:
                r[...] = jnp.zeros_like(r)
        inv_v = inv_ref[...]
        tile = lambda q, k, v, ab, z, st, alog, dt, nw: _dn_tile(q, k, v, ab, z, st, inv_v, alog, dt, nw)[:2]
        _, vjp = jax.vjp(tile, qkv_ref[:, 0:G], qkv_ref[:, G:2 * G], qkv_ref[:, 2 * G:3 * G], ab_ref[...], z_ref[...],
                         st_ref[0], alog_ref[...], dt_ref[...], nw_ref[...])
        dq, dk, dv, dab, dz, dst, *dprm = vjp((dy_ref[...], dstate[...]))
        dqkv_ref[...] = jnp.concatenate([dq, dk, dv], axis=1)
        dz_ref[...] = dz.astype(BF16)
        dab_ref[...] = dab.astype(BF16)
        dstate[...] = dst
        for r, g in zip(dprm_refs, dprm):
            r[...] += g

    rev = lambda w, blk: pl.BlockSpec((tm, w), lambda i: (n - 1 - i, blk))
    return pl.pallas_call(
        body, name="dn_bwd", grid=(n,),
        in_specs=[rev(3 * G, 0), rev(G, _C_Z), rev(LANES, _C_AB),
                  pl.BlockSpec((1, NH, HD, HD), lambda i: (n - 1 - i, 0, 0, 0)),
                  pl.BlockSpec((nb, DN_C, DN_C), lambda i: (n - 1 - i, 0, 0)), rev(G, 2)]
        + [_resident(a.shape) for a in params],
        out_specs=[rev(3 * G, 0), rev(G, 0), rev(LANES, 0)] + [_acc(a.shape) for a in params],
        out_shape=[jax.ShapeDtypeStruct((t, 3 * G), F32), jax.ShapeDtypeStruct((t, G), BF16),
                   jax.ShapeDtypeStruct((t, LANES), BF16)] + [jax.ShapeDtypeStruct(a.shape, F32) for a in params],
        scratch_shapes=[pltpu.VMEM((NH, HD, HD), F32)],
        compiler_params=_cparams("arbitrary"))(qkv, p, p, states, inv, dmix, *params)


def _recurrent_fwd(name, chunk_fn, qkv, qkv_blk, p, z_blk, small_blk, params, cps=2):
    t = qkv.shape[0]
    c = 64
    tm = c * cps
    nparam = len(params)

    def body(*refs):
        qkv_ref, z_ref, sm_ref = refs[:3]
        prm = [r[...] for r in refs[3:3 + nparam]]
        y_ref, st_ref, state = refs[3 + nparam:]

        @pl.when(pl.program_id(0) == 0)
        def _():
            state[...] = jnp.zeros_like(state)
        for j in range(cps):
            rows = pl.ds(j * c, c)
            st = [state[h] for h in range(NH)]
            for h in range(NH):
                st_ref[j, h] = st[h]
            y, new = chunk_fn(qkv_ref[rows, 0:G], qkv_ref[rows, G:2 * G], qkv_ref[rows, 2 * G:3 * G], sm_ref[rows, :],
                              z_ref[rows, :], st, *prm)
            y_ref[rows, :] = y.astype(BF16)
            for h in range(NH):
                state[h] = new[h]

    return pl.pallas_call(
        body, name=name, grid=(t // tm,),
        in_specs=[pl.BlockSpec((tm, 3 * G), lambda i: (i, qkv_blk)), pl.BlockSpec((tm, G), lambda i: (i, z_blk)),
                  pl.BlockSpec((tm, LANES), lambda i: (i, small_blk))] + [_resident(a.shape) for a in params],
        out_specs=[pl.BlockSpec((tm, G), lambda i: (i, 0)), pl.BlockSpec((cps, NH, HD, HD), lambda i: (i, 0, 0, 0))],
        out_shape=[jax.ShapeDtypeStruct((t, G), BF16), jax.ShapeDtypeStruct((t // c, NH, HD, HD), F32)],
        scratch_shapes=[pltpu.VMEM((NH, HD, HD), F32)],
        compiler_params=_cparams("arbitrary"))(qkv, p, p, *params)


def _recurrent_bwd(name, chunk_fn, qkv, qkv_blk, p, z_blk, small_blk, states, dmix, dmix_blk, params, dqkv_dtype,
                   cps=2):
    t = qkv.shape[0]
    c = 64
    tm = c * cps
    n = t // tm
    nparam = len(params)

    def body(*refs):
        qkv_ref, z_ref, sm_ref, st_ref, dy_ref = refs[:5]
        prm = [r[...] for r in refs[5:5 + nparam]]
        dqkv_ref, dz_ref, dsm_ref = refs[5 + nparam:8 + nparam]
        dprm_refs = refs[8 + nparam:8 + 2 * nparam]
        dstate = refs[8 + 2 * nparam]

        @pl.when(pl.program_id(0) == 0)
        def _():
            dstate[...] = jnp.zeros_like(dstate)
            for r in dprm_refs:
                r[...] = jnp.zeros_like(r)
        for j in reversed(range(cps)):
            rows = pl.ds(j * c, c)
            st = [st_ref[j, h] for h in range(NH)]
            _, vjp = jax.vjp(chunk_fn, qkv_ref[rows, 0:G], qkv_ref[rows, G:2 * G], qkv_ref[rows, 2 * G:3 * G],
                             sm_ref[rows, :], z_ref[rows, :], st, *prm)
            grads = vjp((dy_ref[rows, :], [dstate[h] for h in range(NH)]))
            dq, dk, dv, dsm, dz, dst = grads[:6]
            dqkv_ref[rows, :] = jnp.concatenate([dq, dk, dv], axis=1).astype(dqkv_dtype)
            dz_ref[rows, :] = dz.astype(BF16)
            dsm_ref[rows, :] = dsm.astype(BF16)
            for h in range(NH):
                dstate[h] = dst[h]
            for r, g in zip(dprm_refs, grads[6:]):
                r[...] += g

    rev = lambda w, blk: pl.BlockSpec((tm, w), lambda i: (n - 1 - i, blk))
    return pl.pallas_call(
        body, name=name, grid=(n,),
        in_specs=[rev(3 * G, qkv_blk), rev(G, z_blk), rev(LANES, small_blk),
                  pl.BlockSpec((cps, NH, HD, HD), lambda i: (n - 1 - i, 0, 0, 0)), rev(G, dmix_blk)]
        + [_resident(a.shape) for a in params],
        out_specs=[rev(3 * G, 0), rev(G, 0), rev(LANES, 0)] + [_acc(a.shape) for a in params],
        out_shape=[jax.ShapeDtypeStruct((t, 3 * G), dqkv_dtype), jax.ShapeDtypeStruct((t, G), BF16),
                   jax.ShapeDtypeStruct((t, LANES), BF16)] + [jax.ShapeDtypeStruct(a.shape, F32) for a in params],
        scratch_shapes=[pltpu.VMEM((NH, HD, HD), F32)],
        compiler_params=_cparams("arbitrary"))(qkv, p, p, states, dmix, *params)


def _adamw_math(w, g, m, v):
    m = ADAM_B1 * m + (1.0 - ADAM_B1) * g
    v = ADAM_B2 * v + (1.0 - ADAM_B2) * (g * g)
    m_hat = m / (1.0 - ADAM_B1 ** ADAM_STEP)
    v_hat = v / (1.0 - ADAM_B2 ** ADAM_STEP)
    return -ADAM_LR * (m_hat / (jnp.sqrt(v_hat) + ADAM_EPS) + ADAM_WD * w), m, v


def _adamw(parts, w, m, v, name, rows):
    n, r = parts.shape[0], w.shape[0]

    def body(p_ref, w_ref, m_ref, v_ref, g_ref, d_ref, nm_ref, nv_ref):
        g = p_ref[0].astype(F32)
        for k in range(1, n):
            g = g + p_ref[k].astype(F32)
        g_ref[...] = g
        d_ref[...], nm_ref[...], nv_ref[...] = _adamw_math(w_ref[...], g, m_ref[...], v_ref[...])

    blk = pl.BlockSpec((rows, LANES), lambda i: (i, 0))
    return pl.pallas_call(
        body, name=name, grid=(r // rows,),
        in_specs=[pl.BlockSpec((n, rows, LANES), lambda i: (0, i, 0)), blk, blk, blk],
        out_specs=[blk] * 4, out_shape=[jax.ShapeDtypeStruct((r, LANES), F32)] * 4,
        compiler_params=_cparams("parallel"))(parts, w, m, v)


def _sum_parts(parts, name):
    def body(p_ref, o_ref):
        g = p_ref[0]
        for k in range(1, parts.shape[0]):
            g = g + p_ref[k]
        o_ref[...] = g

    return pl.pallas_call(body, name=name, out_shape=jax.ShapeDtypeStruct(parts.shape[1:], F32),
                          compiler_params=_cparams())(parts)


_ANY = pl.BlockSpec(memory_space=pl.ANY)


def _place():
    return lax.axis_index("x"), lax.axis_index("y"), lax.axis_index("c")


def _all_gather(slab, name):
    def body(x_ref, out_ref, send_sems, recv_sems, local_sem):
        x, y, c = _place()
        me, sibling = (x, y, c), (x, y, 1 - c)
        chips = [(1 - x, y), (x, 1 - y), (1 - x, 1 - y)]

        def slot(px, py, pc):
            return out_ref.at[4 * px + 2 * py + pc]

        def copy(k, block, to, src=None):
            return pltpu.make_async_remote_copy(
                src_ref=slot(*block) if src is None else src, dst_ref=slot(*block),
                send_sem=send_sems.at[k], recv_sem=recv_sems.at[k], device_id=to, device_id_type=pl.DeviceIdType.MESH)

        mine = pltpu.make_async_copy(x_ref, slot(*me), local_sem)
        mine.start()
        first = [copy(0, me, sibling, src=x_ref)]
        first += [copy(1 + j, me, (*chip, c), src=x_ref) for j, chip in enumerate(chips)]
        for cp in first:
            cp.start()
        passed = [copy(4 + j, (*chip, c), sibling) for j, chip in enumerate(chips)]
        for j, chip in enumerate(chips):
            copy(1 + j, (*chip, c), me).wait_recv()
            passed[j].start()
        copy(0, sibling, me).wait_recv()
        for j, chip in enumerate(chips):
            copy(4 + j, (*chip, 1 - c), me).wait_recv()
        for cp in first + passed:
            cp.wait_send()
        mine.wait()

    return pl.pallas_call(
        body, name=name, out_shape=jax.ShapeDtypeStruct((N_DEV,) + slab.shape, slab.dtype),
        in_specs=[_ANY], out_specs=_ANY,
        scratch_shapes=[pltpu.SemaphoreType.DMA((7,)), pltpu.SemaphoreType.DMA((7,)), pltpu.SemaphoreType.DMA])(slab)


def _all_to_all(slabs, name):
    def body(x_ref, out_ref, send_sems, recv_sems, local_sem):
        x, y, c = _place()
        me = 4 * x + 2 * y + c

        def peer(k):
            return (x ^ ((k >> 2) & 1), y ^ ((k >> 1) & 1), c ^ (k & 1))

        def copy(k):
            px, py, pc = peer(k)
            return pltpu.make_async_remote_copy(
                src_ref=x_ref.at[4 * px + 2 * py + pc], dst_ref=out_ref.at[me],
                send_sem=send_sems.at[k - 1], recv_sem=recv_sems.at[k - 1], device_id=(px, py, pc),
                device_id_type=pl.DeviceIdType.MESH)

        mine = pltpu.make_async_copy(x_ref.at[me], out_ref.at[me], local_sem)
        mine.start()
        copies = [copy(k) for k in range(1, N_DEV)]
        for cp in copies:
            cp.start()
        for k in range(1, N_DEV):
            px, py, pc = peer(k)
            pltpu.make_async_remote_copy(
                src_ref=x_ref.at[me], dst_ref=out_ref.at[4 * px + 2 * py + pc], send_sem=send_sems.at[k - 1],
                recv_sem=recv_sems.at[k - 1], device_id=(px, py, pc), device_id_type=pl.DeviceIdType.MESH).wait_recv()
        for cp in copies:
            cp.wait_send()
        mine.wait()

    return pl.pallas_call(
        body, name=name, out_shape=jax.ShapeDtypeStruct(slabs.shape, slabs.dtype), in_specs=[_ANY], out_specs=_ANY,
        scratch_shapes=[pltpu.SemaphoreType.DMA((7,)), pltpu.SemaphoreType.DMA((7,)), pltpu.SemaphoreType.DMA])(slabs)


def _pad_cols(w):
    parts = []
    for a, b, z in _PAD_SEGS:
        parts.append(w[..., a:b])
        if z:
            parts.append(jnp.zeros(w.shape[:-1] + (z,), w.dtype))
    return jnp.concatenate(parts, axis=-1)


def _unpad_cols(wp):
    pieces, off = [], 0
    for a, b, z in _PAD_SEGS:
        pieces.append((a, wp[..., off:off + (b - a)]))
        off += (b - a) + z
    return jnp.concatenate([piece for _, piece in sorted(pieces, key=lambda ap: ap[0])], axis=-1)


def _to_slab(parts, rows):
    flat = jnp.concatenate([a.reshape(-1) for a in parts])
    return jnp.pad(flat, (0, rows * LANES - flat.shape[0])).reshape(rows, LANES)


def _from_slab(slab, shapes):
    flat, out, off = slab.reshape(-1), [], 0
    for s in shapes:
        n = 1
        for d in s:
            n *= d
        out.append(flat[off:off + n].reshape(s))
        off += n
    return out


def _local_grads(x, target, wts):
    saved = []
    for l in range(DEPTH):
        w = {k: v[l] for k, v in wts.items() if k != "final_norm_w"}
        h, p = _in_proj(x, w["norm1_w"][None], w["w_in"])
        ya = _sgu_fwd(p, w["sgu_ln_w"][None], w["sgu_ln_b"][None], w["sgu_w_spatial"], w["sgu_b_spatial"].T)
        yb = _sconv_fwd(p, w["sc_conv_w"])
        qkv_c = _qkv_conv_fwd(p, w["dn_conv_w"])
        dn_params = (w["dn_a_log"][None], w["dn_dt_bias"][None], w["dn_norm_w"][None])
        yc, st_c, inv_c = _dn_fwd(qkv_c, p, dn_params)
        w2 = jnp.pad(w["gla_w_gate2"], ((0, LANES - 16), (0, 0)))
        gla_params = (w2, w["gla_gate_bias"][None], w["gla_norm_w"][None])
        yd, st_d = _recurrent_fwd("gla_fwd", _gla_chunk, p, _D_QKV, p, _D_Z, _D_GLR, gla_params)
        mix, x1, h2, gu, act, x2 = _out_mlp(x, (ya, yb, yc, yd), w["w_out"], w["norm2_w"][None], w["w_gate_up"],
                                            w["w_down"])
        saved.append(dict(w=w, x=x, h=h, p=p, qkv_c=qkv_c, st_c=st_c, inv_c=inv_c, st_d=st_d, mix=mix, x1=x1, h2=h2,
                          gu=gu, act=act, dn_params=dn_params, gla_params=gla_params))
        x = x2
    loss, d_fnw, dx = _loss_head(x, wts["final_norm_w"][None], target)
    grads = {k: [None] * DEPTH for k in wts if k != "final_norm_w"}
    grads["final_norm_w"] = d_fnw[0]
    for l in reversed(range(DEPTH)):
        s = saved[l]
        w, p = s["w"], s["p"]
        dgu, dx1, dmix, d_nw2 = _mlp_out_bwd(dx, s["x1"], s["gu"], w["norm2_w"][None], w["w_down"], w["w_gate_up"],
                                             w["w_out"])
        g = {}
        g["norm2_w"] = d_nw2[0]
        g["w_gate_up"] = _matmul_tn(s["h2"], dgu, "wgrad_gate_up")
        g["w_down"] = _matmul_tn(s["act"], dx, "wgrad_down")
        g["w_out"] = _matmul_tn(s["mix"], dx1, "wgrad_out")
        d_a, g_lw, g_lb, g_ws, g_bs = _sgu_bwd(p, dmix, w["sgu_ln_w"][None], w["sgu_ln_b"][None], w["sgu_w_spatial"],
                                              w["sgu_b_spatial"].T)
        g["sgu_ln_w"], g["sgu_ln_b"], g["sgu_w_spatial"], g["sgu_b_spatial"] = g_lw[0], g_lb[0], g_ws, g_bs.T
        d_b, g["sc_conv_w"] = _sconv_bwd(p, dmix, w["sc_conv_w"])
        dqkv_c, dz_c, dab, g_alog, g_dt, g_cn = _dn_bwd(s["qkv_c"], p, s["st_c"], s["inv_c"], dmix, s["dn_params"])
        g["dn_a_log"], g["dn_dt_bias"], g["dn_norm_w"] = g_alog[0], g_dt[0], g_cn[0]
        d_c, g["dn_conv_w"] = _qkv_conv_bwd(p, dqkv_c, w["dn_conv_w"])
        d_d, dz_d, dglr, g_w2, g_gb, g_dn = _recurrent_bwd(
            "gla_bwd", _gla_chunk, p, _D_QKV, p, _D_Z, _D_GLR, s["st_d"], dmix, 3, s["gla_params"], BF16)
        g["gla_w_gate2"], g["gla_gate_bias"], g["gla_norm_w"] = g_w2[:16], g_gb[0], g_dn[0]
        dp, dx, d_nw1 = _in_proj_bwd((d_c, d_d, d_a, d_b, dz_c, dz_d, dab, dglr), s["x"], dx1, w["norm1_w"][None],
                                     w["w_in"])
        g["norm1_w"] = d_nw1[0]
        g["w_in"] = _matmul_tn(s["h"], dp, "wgrad_in")
        for k, v in g.items():
            grads[k][l] = v
    grads = {k: (v if k == "final_norm_w" else jnp.stack(v)) for k, v in grads.items()}
    return loss[0, 0], dx, grads


_BIG = ("w_in", "w_out", "w_gate_up", "w_down")
_SMALL_SHARDED = ("sc_conv_w", "dn_conv_w", "gla_w_gate2")
_REPLICATED = ("norm1_w", "sgu_ln_w", "sgu_ln_b", "sgu_w_spatial", "sgu_b_spatial", "dn_a_log", "dn_dt_bias",
               "dn_norm_w", "gla_gate_bias", "gla_norm_w", "norm2_w", "final_norm_w")
_ORDER = ("norm1_w", "w_in", "sgu_ln_w", "sgu_ln_b", "sgu_w_spatial", "sgu_b_spatial", "sc_conv_w", "dn_conv_w",
          "dn_a_log", "dn_dt_bias", "dn_norm_w", "gla_w_gate2", "gla_gate_bias", "gla_norm_w", "w_out", "norm2_w",
          "w_gate_up", "w_down", "final_norm_w")
_BIG_ROWS = 26112
_BIG_TILE = 1536
_SMALL_ROWS = 1216
_OWN_ROWS = 1104


def _gather_cols(g):
    return jnp.transpose(g, (1, 2, 0, 3)).reshape(g.shape[1], g.shape[2], N_DEV * g.shape[3])


def _gather_rows(g):
    return jnp.transpose(g, (1, 0, 2, 3)).reshape(g.shape[1], N_DEV * g.shape[2], g.shape[3])


def _scatter_cols(w):
    return jnp.transpose(w.reshape(w.shape[0], w.shape[1], N_DEV, w.shape[2] // N_DEV), (2, 0, 1, 3))


def _scatter_rows(w):
    return jnp.transpose(w.reshape(w.shape[0], N_DEV, w.shape[1] // N_DEV, w.shape[2]), (1, 0, 2, 3))


def kernel(x, norm1_w, w_in, sgu_ln_w, sgu_ln_b, sgu_w_spatial, sgu_b_spatial, sc_conv_w, dn_conv_w, dn_a_log, dn_dt_bias, dn_norm_w, gla_w_gate2, gla_gate_bias, gla_norm_w, w_out, norm2_w, w_gate_up, w_down, final_norm_w, loss_target, m_norm1_w, m_w_in, m_sgu_ln_w, m_sgu_ln_b, m_sgu_w_spatial, m_sgu_b_spatial, m_sc_conv_w, m_dn_conv_w, m_dn_a_log, m_dn_dt_bias, m_dn_norm_w, m_gla_w_gate2, m_gla_gate_bias, m_gla_norm_w, m_w_out, m_norm2_w, m_w_gate_up, m_w_down, m_final_norm_w, v_norm1_w, v_w_in, v_sgu_ln_w, v_sgu_ln_b, v_sgu_w_spatial, v_sgu_b_spatial, v_sc_conv_w, v_dn_conv_w, v_dn_a_log, v_dn_dt_bias, v_dn_norm_w, v_gla_w_gate2, v_gla_gate_bias, v_gla_norm_w, v_w_out, v_norm2_w, v_w_gate_up, v_w_down, v_final_norm_w):
    args = dict(locals())
    wts = {k: args[k] for k in _ORDER}
    mom = {k: args["m_" + k] for k in _ORDER}
    var = {k: args["v_" + k] for k in _ORDER}
    me = 4 * lax.axis_index("x") + 2 * lax.axis_index("y") + lax.axis_index("c")

    big_shapes = [wts[k].shape for k in _BIG]
    got = _all_gather(_to_slab([wts[k].astype(BF16) for k in _BIG], _BIG_ROWS), "gather_weights")
    got = [jnp.stack(parts) for parts in zip(*[_from_slab(got[d], big_shapes) for d in range(N_DEV)])]
    full = dict(wts)
    full["w_in"] = _pad_cols(_gather_cols(got[0]))
    full["w_out"] = _gather_rows(got[1])
    full["w_gate_up"] = _gather_cols(got[2])
    full["w_down"] = _gather_rows(got[3])
    small_shapes = [wts[k].shape for k in _SMALL_SHARDED]
    got = _all_gather(_to_slab([wts[k] for k in _SMALL_SHARDED], 16), "gather_small_weights")
    got = [jnp.stack(parts) for parts in zip(*[_from_slab(got[d], small_shapes) for d in range(N_DEV)])]
    for k, g in zip(_SMALL_SHARDED, got):
        full[k] = _gather_cols(g)

    loss, dx, grads = _local_grads(x[0], loss_target[0], full)
    loss = lax.psum(loss, ("x", "y", "c"))

    per_dev = [_scatter_cols(_unpad_cols(grads["w_in"])), _scatter_rows(grads["w_out"]),
               _scatter_cols(grads["w_gate_up"]), _scatter_rows(grads["w_down"])]
    send = jnp.stack([_to_slab([a[d].astype(BF16) for a in per_dev], _BIG_ROWS) for d in range(N_DEV)])
    parts = _all_to_all(send, "exchange_grads")
    outs = _adamw(parts, _to_slab([wts[k] for k in _BIG], _BIG_ROWS), _to_slab([mom[k] for k in _BIG], _BIG_ROWS),
                  _to_slab([var[k] for k in _BIG], _BIG_ROWS), "adamw_large", _BIG_TILE)
    result = {}
    for kind, slab in zip(("grad", "delta", "new_m", "new_v"), outs):
        for k, a in zip(_BIG, _from_slab(slab, big_shapes)):
            result[kind, k] = a

    small = _REPLICATED + _SMALL_SHARDED
    parts = _all_gather(_to_slab([grads[k] for k in small], _SMALL_ROWS), "gather_small_grads")
    gs = dict(zip(small, _from_slab(_sum_parts(parts, "sum_small_grads"), [grads[k].shape for k in small])))
    own = lambda d, k: lax.dynamic_slice_in_dim(d, me * wts[k].shape[-1], wts[k].shape[-1], axis=d.ndim - 1)
    own_grads = _to_slab([gs[k] if k in _REPLICATED else own(gs[k], k) for k in small], _OWN_ROWS)
    outs = _adamw(own_grads[None], _to_slab([wts[k] for k in small], _OWN_ROWS),
                  _to_slab([mom[k] for k in small], _OWN_ROWS), _to_slab([var[k] for k in small], _OWN_ROWS),
                  "adamw_small", _OWN_ROWS)
    own_shapes = [wts[k].shape for k in small]
    for kind, slab in zip(("grad", "delta", "new_m", "new_v"), outs):
        for k, a in zip(small, _from_slab(slab, own_shapes)):
            result[kind, k] = a

    return (loss, dx[None], *[result[kind, k] for kind in ("grad", "delta", "new_m", "new_v") for k in _ORDER])
```

```python
import functools

import jax
import jax.numpy as jnp
from jax import lax
from jax.experimental import pallas as pl
from jax.experimental.pallas import tpu as pltpu

F32, BF16 = jnp.float32, jnp.bfloat16
DEPTH = 2
D = 1024
G = 256
NH, HD = 4, 64
FF = 2816
IN_COLS = 3352
P = 3584
EPS = 1e-6
SGU_C, DN_C, GLA_C = 128, 64, 64
N_DEV = 8
LANES = 128
VMEM_LIMIT = 56 * 2 ** 20

_PAD_SEGS = ((1280, 2048, 0),
             (2312, 3080, 0),
             (0, 512, 0),
             (512, 1280, 0),
             (2056, 2312, 0),
             (3096, 3352, 0),
             (2048, 2056, 120),
             (3080, 3096, 112))

ADAM_LR, ADAM_B1, ADAM_B2, ADAM_EPS, ADAM_WD, ADAM_STEP = 0.001, 0.9, 0.999, 1e-08, 0.01, 10


def _cparams(*sem):
    return pltpu.CompilerParams(dimension_semantics=sem, vmem_limit_bytes=VMEM_LIMIT)


def _resident(shape):
    return pl.BlockSpec(shape, lambda *_: (0,) * len(shape), pipeline_mode=pl.Buffered(1))


def _acc(shape):
    return pl.BlockSpec(shape, lambda *_: (0,) * len(shape))


def _dg(a, b, ca, cb, precision=None):
    lead = a.ndim - 2
    batch = ((0,), (0,)) if lead else ((), ())
    return lax.dot_general(a, b, (((ca + lead,), (cb + lead,)), batch), preferred_element_type=F32, precision=precision)


def _make_dot(cast, precision):
    def raw(a, b, form):
        ca = 0 if form == "tn" else 1
        cb = 1 if form == "nt" else 0
        return _dg(cast(a), cast(b), ca, cb, precision)

    @functools.partial(jax.custom_vjp, nondiff_argnums=(2,))
    def dot(a, b, form):
        return raw(a, b, form)

    def fwd(a, b, form):
        return raw(a, b, form), (a, b)

    def bwd(form, res, g):
        a, b = res
        if form == "nn":
            return raw(g, b, "nt"), raw(a, g, "tn")
        if form == "nt":
            return raw(g, b, "nn"), raw(g, a, "tn")
        return raw(b, g, "nt"), raw(a, g, "nn")

    dot.defvjp(fwd, bwd)
    return dot


_bdot = _make_dot(lambda t: t.astype(BF16), None)
_hdot = _make_dot(lambda t: t, lax.Precision.HIGHEST)


def _inv_unit_lower(low):
    n = low.shape[-1]
    eye = (lax.broadcasted_iota(jnp.int32, (n, n), 0) == lax.broadcasted_iota(jnp.int32, (n, n), 1)).astype(F32)
    p = -low
    inv = eye + p
    k = 1
    while 2 * k < n:
        p = _dg(p, p, 1, 0, lax.Precision.HIGHEST)
        inv = inv + _dg(inv, p, 1, 0, lax.Precision.HIGHEST)
        k *= 2
    return inv


@jax.custom_vjp
def _unit_lower_solve(low, rhs, inv):
    return _dg(inv, rhs, 1, 0, lax.Precision.HIGHEST)


def _uls_fwd(low, rhs, inv):
    sol = _dg(inv, rhs, 1, 0, lax.Precision.HIGHEST)
    return sol, (inv, sol)


def _uls_bwd(res, g):
    inv, sol = res
    d_rhs = _dg(inv, g, 0, 0, lax.Precision.HIGHEST)
    return -_dg(d_rhs, sol, 1, 1, lax.Precision.HIGHEST), d_rhs, jnp.zeros_like(inv)


_unit_lower_solve.defvjp(_uls_fwd, _uls_bwd)


def _sigmoid(x):
    return 1.0 / (1.0 + jnp.exp(-x))


def _softplus(x):
    return jnp.maximum(x, 0.0) + jnp.log(1.0 + jnp.exp(-jnp.maximum(x, -x)))


def _gelu(x):
    return 0.5 * x * (1.0 + jnp.tanh(0.7978845608028654 * (x + 0.044715 * (x * x * x))))


def _tri(n):
    ri = lax.broadcasted_iota(jnp.int32, (n, n), 0)
    ci = lax.broadcasted_iota(jnp.int32, (n, n), 1)
    return ri, ci


def _sgu_chunk(uv, ln_w, ln_b, ws, bs_t):
    u = _gelu(uv[:, :G])
    v = _gelu(uv[:, G:])
    mu = jnp.mean(v, axis=-1, keepdims=True)
    vc = v - mu
    var = jnp.mean(vc * vc, axis=-1, keepdims=True)
    vn = vc * lax.rsqrt(var + EPS) * ln_w + ln_b
    ri, ci = _tri(SGU_C)
    mixed = []
    for h in range(NH):
        wm = jnp.where(ri >= ci, ws[h], 0.0)
        mixed.append(_bdot(wm, vn[:, HD * h:HD * (h + 1)], "nn") + bs_t[:, h:h + 1])
    return u * jnp.concatenate(mixed, axis=1)


def _stack(ts):
    return jnp.concatenate([t[None] for t in ts], axis=0)


def _dn_tile(q, k, v, ab, z, state, inv, a_log, dt_bias, nw):
    c = DN_C
    tm = q.shape[0]
    cps = tm // c
    ri, ci = _tri(tm)
    shift = c.bit_length() - 1
    same = jnp.right_shift(ri, shift) == jnp.right_shift(ci, shift)
    g4 = -jnp.exp(a_log) * _softplus(ab[:, 0:NH] + dt_bias)
    beta4 = _sigmoid(ab[:, NH:2 * NH])
    gc4 = _hdot((same & (ri >= ci)).astype(F32), g4, "nn")
    gr4 = _hdot(g4, (same & (ri <= ci)).astype(F32), "tn")
    pairs = [(slice(c * j, c * (j + 1)), h) for j in range(cps) for h in range(NH)]
    heads = lambda t: _stack([t[rows, HD * h:HD * (h + 1)] for rows, h in pairs])
    col = lambda t: _stack([t[rows, h:h + 1] for rows, h in pairs])
    qn, kn, vh = heads(q), heads(k), heads(v)
    qn = qn * lax.rsqrt(jnp.sum(qn * qn, axis=-1, keepdims=True) + EPS) * (HD ** -0.5)
    kn = kn * lax.rsqrt(jnp.sum(kn * kn, axis=-1, keepdims=True) + EPS)
    gc, beta = col(gc4), col(beta4)
    gr = _stack([gr4[h:h + 1, rows] for rows, h in pairs])
    gl = jnp.sum(col(g4), axis=1, keepdims=True)
    r2, c2 = _tri(c)
    decay = jnp.exp(jnp.where(r2 >= c2, gc - gr, -jnp.inf))
    kb = kn * beta
    low = jnp.where(r2 > c2, _bdot(kb, kn, "nt") * decay, 0.0)
    eg = jnp.exp(gc)
    if inv is None:
        inv = _inv_unit_lower(low)
    sol = _unit_lower_solve(low, jnp.concatenate([vh * beta, kb * eg], axis=2), inv)
    u, w = sol[:, :, :HD], sol[:, :, HD:]
    attn = _bdot(qn, kn, "nt") * decay
    qg, kd, cd = qn * eg, kn * jnp.exp(gl - gc), jnp.exp(gl)
    ys = []
    for j in range(cps):
        b = slice(NH * j, NH * (j + 1))
        v_new = u[b] - _bdot(w[b], state, "nn")
        o = _bdot(qg[b], state, "nn") + _bdot(attn[b], v_new, "nn")
        state = state * cd[b] + _bdot(kd[b], v_new, "tn")
        on = o * lax.rsqrt(jnp.mean(o * o, axis=-1, keepdims=True) + EPS) * nw
        ys.append(jnp.concatenate([on[h] for h in range(NH)], axis=1))
    return jnp.concatenate(ys, axis=0) * (z * _sigmoid(z)), state, inv


def _gla_tile(q, k, v, glr, z, state_t, unused, w2, gate_bias, nw):
    c = GLA_C
    tm = q.shape[0]
    cps = tm // c
    ri, ci = _tri(tm)
    shift = c.bit_length() - 1
    same = jnp.right_shift(ri, shift) == jnp.right_shift(ci, shift)
    pre = _bdot(glr, w2, "nn") + gate_bias
    log_a = (jnp.minimum(pre, 0.0) - jnp.log(1.0 + jnp.exp(-jnp.maximum(pre, -pre)))) * (1.0 / 16.0)
    gcum = _hdot((same & (ri >= ci)).astype(F32), log_a, "nn")
    row = lax.broadcasted_iota(jnp.int32, (c, G), 0)
    qs = q * (HD ** -0.5)
    qa, ka, qg, kl, dec = [], [], [], [], []
    for j in range(cps):
        rows = slice(c * j, c * (j + 1))
        gj = gcum[rows]
        g_mid = jnp.sum(jnp.where(row == c // 2, gj, 0.0), axis=0, keepdims=True)
        g_last = jnp.sum(log_a[rows], axis=0, keepdims=True)
        qa.append(qs[rows] * jnp.exp(gj - g_mid))
        ka.append(k[rows] * jnp.exp(g_mid - gj))
        qg.append(qs[rows] * jnp.exp(gj))
        kl.append(k[rows] * jnp.exp(g_last - gj))
        dec.append(jnp.exp(g_last))
    heads = lambda ts: _stack([t[:, HD * h:HD * (h + 1)] for t in ts for h in range(NH)])
    qa, ka, qg, kl, dec = heads(qa), heads(ka), heads(qg), heads(kl), heads(dec)
    vh = heads([v[c * j:c * (j + 1)] for j in range(cps)])
    r2, c2 = _tri(c)
    o_intra = _bdot(jnp.where(r2 >= c2, _bdot(qa, ka, "nt"), 0.0), vh, "nn")
    ys = []
    for j in range(cps):
        b = slice(NH * j, NH * (j + 1))
        o = o_intra[b] + _bdot(qg[b], state_t, "nt")
        state_t = state_t * dec[b] + _bdot(vh[b], kl[b], "tn")
        on = o * lax.rsqrt(jnp.mean(o * o, axis=-1, keepdims=True) + EPS) * nw
        ys.append(jnp.concatenate([on[h] for h in range(NH)], axis=1))
    return jnp.concatenate(ys, axis=0) * (z * _sigmoid(z)), state_t, None


def _rms(x, nw):
    r = lax.rsqrt(jnp.mean(x * x, axis=-1, keepdims=True) + EPS)
    xh = x * r
    return xh * nw, xh, r


def _rms_bwd(g, xh, r, nw):
    gw = g * nw
    return r * (gw - xh * jnp.mean(gw * xh, axis=-1, keepdims=True)), jnp.sum(g * xh, axis=0, keepdims=True)


def _in_proj(x, nw, wp, tm=256):
    t = x.shape[0]

    def body(x_ref, nw_ref, w_ref, h_ref, p_ref):
        h = _rms(x_ref[...], nw_ref[...])[0].astype(BF16)
        h_ref[...] = h
        p_ref[...] = jnp.dot(h, w_ref[...], preferred_element_type=F32)

    return pl.pallas_call(
        body, name="in_proj", grid=(t // tm,),
        in_specs=[pl.BlockSpec((tm, D), lambda i: (i, 0)), _resident((1, D)), _resident((D, P))],
        out_specs=[pl.BlockSpec((tm, D), lambda i: (i, 0)), pl.BlockSpec((tm, P), lambda i: (i, 0))],
        out_shape=[jax.ShapeDtypeStruct((t, D), BF16), jax.ShapeDtypeStruct((t, P), F32)],
        compiler_params=_cparams("parallel"))(x, nw, wp)


def _out_mlp(x, ys, w_out, nw2, w_gu, w_down, tm=256):
    t = x.shape[0]

    def body(x_ref, ya, yb, yc, yd, wo_ref, nw_ref, wgu_ref, wd_ref, mix_ref, x1_ref, h2_ref, gu_ref, act_ref, x2_ref):
        mix = jnp.concatenate([ya[...], yb[...], yc[...], yd[...]], axis=1)
        mix_ref[...] = mix
        x1 = x_ref[...] + jnp.dot(mix, wo_ref[...], preferred_element_type=F32)
        x1_ref[...] = x1
        h2 = _rms(x1, nw_ref[...])[0].astype(BF16)
        h2_ref[...] = h2
        gu = jnp.dot(h2, wgu_ref[...], preferred_element_type=F32)
        gu_ref[...] = gu.astype(BF16)
        gate, up = gu[:, :FF], gu[:, FF:]
        act = (gate * _sigmoid(gate) * up).astype(BF16)
        act_ref[...] = act
        x2_ref[...] = x1 + jnp.dot(act, wd_ref[...], preferred_element_type=F32)

    row = lambda w: pl.BlockSpec((tm, w), lambda i: (i, 0))
    return pl.pallas_call(
        body, name="out_mlp", grid=(t // tm,),
        in_specs=[row(D), row(G), row(G), row(G), row(G), _resident((D, D)), _resident((1, D)),
                  _resident((D, 2 * FF)), _resident((FF, D))],
        out_specs=[row(D), row(D), row(D), row(2 * FF), row(FF), row(D)],
        out_shape=[jax.ShapeDtypeStruct((t, D), BF16), jax.ShapeDtypeStruct((t, D), F32),
                   jax.ShapeDtypeStruct((t, D), BF16), jax.ShapeDtypeStruct((t, 2 * FF), BF16),
                   jax.ShapeDtypeStruct((t, FF), BF16), jax.ShapeDtypeStruct((t, D), F32)],
        compiler_params=_cparams("parallel"))(x, *ys, w_out, nw2, w_gu, w_down)


def _loss_head(x, nw, target, tm=512):
    t = x.shape[0]

    def body(x_ref, nw_ref, tg_ref, loss_ref, dnw_ref, dx_ref):
        @pl.when(pl.program_id(0) == 0)
        def _():
            loss_ref[...] = jnp.zeros_like(loss_ref)
            dnw_ref[...] = jnp.zeros_like(dnw_ref)
        nw_v = nw_ref[...]
        y, xh, r = _rms(x_ref[...], nw_v)
        err = y - tg_ref[...]
        loss_ref[...] += 0.5 * jnp.sum(err * err) * (1.0 / D)
        dx, dnw = _rms_bwd(err * (1.0 / D), xh, r, nw_v)
        dx_ref[...] = dx
        dnw_ref[...] += dnw

    return pl.pallas_call(
        body, name="loss_head", grid=(t // tm,),
        in_specs=[pl.BlockSpec((tm, D), lambda i: (i, 0)), _resident((1, D)), pl.BlockSpec((tm, D), lambda i: (i, 0))],
        out_specs=[_acc((8, LANES)), _acc((1, D)), pl.BlockSpec((tm, D), lambda i: (i, 0))],
        out_shape=[jax.ShapeDtypeStruct((8, LANES), F32), jax.ShapeDtypeStruct((1, D), F32),
                   jax.ShapeDtypeStruct((t, D), F32)],
        compiler_params=_cparams("arbitrary"))(x, nw, target)


def _mlp_out_bwd(dx2, x1, gu, nw2, w_down, w_gu, w_out, tm=256):
    t = dx2.shape[0]

    def body(dx2_ref, x1_ref, gu_ref, nw_ref, wd_ref, wgu_ref, wo_ref, dgu_ref, dx1_ref, dmix_ref, dnw_ref):
        @pl.when(pl.program_id(0) == 0)
        def _():
            dnw_ref[...] = jnp.zeros_like(dnw_ref)
        dx2 = dx2_ref[...]
        dact = _dg(dx2.astype(BF16), wd_ref[...], 1, 1)
        gu = gu_ref[...].astype(F32)
        gate, up = gu[:, :FF], gu[:, FF:]
        s = _sigmoid(gate)
        dgu = jnp.concatenate([dact * up * (s * (1.0 + gate * (1.0 - s))), dact * (gate * s)], axis=1).astype(BF16)
        dgu_ref[...] = dgu
        dh2 = _dg(dgu, wgu_ref[...], 1, 1)
        nw_v = nw_ref[...]
        _, xh, r = _rms(x1_ref[...], nw_v)
        dxn, dnw = _rms_bwd(dh2, xh, r, nw_v)
        dx1 = dx2 + dxn
        dx1_ref[...] = dx1
        dnw_ref[...] += dnw
        dmix_ref[...] = _dg(dx1.astype(BF16), wo_ref[...], 1, 1)

    row = lambda w: pl.BlockSpec((tm, w), lambda i: (i, 0))
    return pl.pallas_call(
        body, name="mlp_out_bwd", grid=(t // tm,),
        in_specs=[row(D), row(D), row(2 * FF), _resident((1, D)), _resident((FF, D)), _resident((D, 2 * FF)),
                  _resident((D, D))],
        out_specs=[row(2 * FF), row(D), row(D), _acc((1, D))],
        out_shape=[jax.ShapeDtypeStruct((t, 2 * FF), BF16), jax.ShapeDtypeStruct((t, D), F32),
                   jax.ShapeDtypeStruct((t, D), F32), jax.ShapeDtypeStruct((1, D), F32)],
        compiler_params=_cparams("arbitrary"))(dx2, x1, gu, nw2, w_down, w_gu, w_out)


_DP_WIDTHS = (768, 768, 512, 768, 256, 256, 128, 128)


def _in_proj_bwd(dps, x, dx1, nw, wp, tm=256):
    t = x.shape[0]

    def body(*refs):
        dp_refs, (x_ref, dx1_ref, nw_ref, w_ref, dp_ref, dx_ref, dnw_ref) = refs[:8], refs[8:]

        @pl.when(pl.program_id(0) == 0)
        def _():
            dnw_ref[...] = jnp.zeros_like(dnw_ref)
        dp = jnp.concatenate([r[...] for r in dp_refs], axis=1)
        dp_ref[...] = dp
        dh = _dg(dp, w_ref[...], 1, 1)
        nw_v = nw_ref[...]
        _, xh, r = _rms(x_ref[...], nw_v)
        dxn, dnw = _rms_bwd(dh, xh, r, nw_v)
        dx_ref[...] = dx1_ref[...] + dxn
        dnw_ref[...] += dnw

    row = lambda w: pl.BlockSpec((tm, w), lambda i: (i, 0))
    return pl.pallas_call(
        body, name="in_proj_bwd", grid=(t // tm,),
        in_specs=[row(w) for w in _DP_WIDTHS] + [row(D), row(D), _resident((1, D)), _resident((D, P))],
        out_specs=[row(P), row(D), _acc((1, D))],
        out_shape=[jax.ShapeDtypeStruct((t, P), BF16), jax.ShapeDtypeStruct((t, D), F32),
                   jax.ShapeDtypeStruct((1, D), F32)],
        compiler_params=_cparams("arbitrary"))(*dps, x, dx1, nw, wp)


def _matmul_tn(a, b, name, tn=512, tt=512):
    t, m = a.shape
    n = b.shape[1]
    tn = min(tn, n)

    def body(a_ref, b_ref, o_ref):
        @pl.when(pl.program_id(1) == 0)
        def _():
            o_ref[...] = jnp.zeros_like(o_ref)
        o_ref[...] += _dg(a_ref[...].astype(BF16), b_ref[...].astype(BF16), 0, 0)

    return pl.pallas_call(
        body, name=name, grid=(n // tn, t // tt),
        in_specs=[pl.BlockSpec((tt, m), lambda j, i: (i, 0)), pl.BlockSpec((tt, tn), lambda j, i: (i, j))],
        out_specs=pl.BlockSpec((m, tn), lambda j, i: (0, j)),
        out_shape=jax.ShapeDtypeStruct((m, n), F32),
        compiler_params=_cparams("parallel", "arbitrary"))(a, b)


_A_BLK = 3


def _sgu_fwd(p, ln_w, ln_b, ws, bs_t, tm=256):
    t = p.shape[0]

    def body(uv_ref, lw_ref, lb_ref, ws_ref, bs_ref, y_ref):
        ws_v = [ws_ref[h] for h in range(NH)]
        for c in range(tm // SGU_C):
            rows = pl.ds(c * SGU_C, SGU_C)
            y_ref[rows, :] = _sgu_chunk(uv_ref[rows, :], lw_ref[...], lb_ref[...], ws_v, bs_ref[...]).astype(BF16)

    return pl.pallas_call(
        body, name="sgu_fwd", grid=(t // tm,),
        in_specs=[pl.BlockSpec((tm, 2 * G), lambda i: (i, _A_BLK)), _resident((1, G)), _resident((1, G)),
                  _resident((NH, SGU_C, SGU_C)), _resident((SGU_C, NH))],
        out_specs=pl.BlockSpec((tm, G), lambda i: (i, 0)),
        out_shape=jax.ShapeDtypeStruct((t, G), BF16),
        compiler_params=_cparams("parallel"))(p, ln_w, ln_b, ws, bs_t)


def _sgu_bwd(p, dmix, ln_w, ln_b, ws, bs_t, tm=256):
    t = p.shape[0]

    def body(uv_ref, dy_ref, lw_ref, lb_ref, ws_ref, bs_ref, duv_ref, dlw_ref, dlb_ref, dws_ref, dbs_ref):
        @pl.when(pl.program_id(0) == 0)
        def _():
            for r in (dlw_ref, dlb_ref, dws_ref, dbs_ref):
                r[...] = jnp.zeros_like(r)
        ws_v = [ws_ref[h] for h in range(NH)]
        for c in range(tm // SGU_C):
            rows = pl.ds(c * SGU_C, SGU_C)
            _, vjp = jax.vjp(_sgu_chunk, uv_ref[rows, :], lw_ref[...], lb_ref[...], ws_v, bs_ref[...])
            duv, dlw, dlb, dws, dbs = vjp(dy_ref[rows, :])
            duv_ref[rows, :] = duv.astype(BF16)
            dlw_ref[...] += dlw
            dlb_ref[...] += dlb
            dbs_ref[...] += dbs
            for h in range(NH):
                dws_ref[h] += dws[h]

    return pl.pallas_call(
        body, name="sgu_bwd", grid=(t // tm,),
        in_specs=[pl.BlockSpec((tm, 2 * G), lambda i: (i, _A_BLK)), pl.BlockSpec((tm, G), lambda i: (i, 0)),
                  _resident((1, G)), _resident((1, G)), _resident((NH, SGU_C, SGU_C)), _resident((SGU_C, NH))],
        out_specs=[pl.BlockSpec((tm, 2 * G), lambda i: (i, 0)), _acc((1, G)), _acc((1, G)),
                   _acc((NH, SGU_C, SGU_C)), _acc((SGU_C, NH))],
        out_shape=[jax.ShapeDtypeStruct((t, 2 * G), BF16), jax.ShapeDtypeStruct((1, G), F32),
                   jax.ShapeDtypeStruct((1, G), F32), jax.ShapeDtypeStruct((NH, SGU_C, SGU_C), F32),
                   jax.ShapeDtypeStruct((SGU_C, NH), F32)],
        compiler_params=_cparams("arbitrary"))(p, dmix, ln_w, ln_b, ws, bs_t)


HALO = 8


def _prev_halo(tm, width, col):
    return pl.BlockSpec((HALO, width), lambda i: (jnp.maximum(i * (tm // HALO) - 1, 0), col))


def _next_halo(tm, width, col, t):
    return pl.BlockSpec((HALO, width), lambda i: (jnp.minimum((i + 1) * (tm // HALO), t // HALO - 1), col))


def _with_prev(halo_ref, x_ref):
    halo = jnp.where(pl.program_id(0) == 0, 0.0, halo_ref[...])
    return jnp.concatenate([halo, x_ref[...]], axis=0)


def _with_next(x_ref, halo_ref):
    halo = jnp.where(pl.program_id(0) == pl.num_programs(0) - 1, 0.0, halo_ref[...])
    return jnp.concatenate([x_ref[...], halo], axis=0)


def _delay(ext, j):
    return ext if j == 0 else pltpu.roll(ext, j, axis=0)


def _advance(ext, j):
    return ext if j == 0 else pltpu.roll(ext, ext.shape[0] - j, axis=0)


def _causal_conv(ext, w, tm):
    kw = w.shape[0]
    out = None
    for k in range(kw):
        term = _delay(ext, kw - 1 - k)[HALO:HALO + tm] * w[k:k + 1, :]
        out = term if out is None else out + term
    return out


def _causal_conv_t(ext, w, tm):
    kw = w.shape[0]
    out = None
    for k in range(kw):
        term = _advance(ext, kw - 1 - k)[0:tm] * w[k:k + 1, :]
        out = term if out is None else out + term
    return out


def _conv_wgrad(ext, dy, kw, tm):
    return jnp.concatenate(
        [jnp.sum(_delay(ext, kw - 1 - k)[HALO:HALO + tm] * dy, axis=0, keepdims=True) for k in range(kw)], axis=0)


_B_GB, _B_GC, _B_H = 8, 9, 10
_C_QKV, _D_QKV = 0, 1
_C_Z, _D_Z = 11, 12
_C_AB, _D_GLR = 26, 27


def _sconv_fwd(p, w, tm=512):
    t = p.shape[0]

    def body(gb_ref, gc_ref, h_ref, gc_halo, h_halo, w_ref, y_ref):
        s_ext = _with_prev(gc_halo, gc_ref) * _with_prev(h_halo, h_ref)
        y_ref[...] = (gb_ref[...] * _causal_conv(s_ext, w_ref[...], tm)).astype(BF16)

    col = lambda c: pl.BlockSpec((tm, G), lambda i: (i, c))
    return pl.pallas_call(
        body, name="sconv_fwd", grid=(t // tm,),
        in_specs=[col(_B_GB), col(_B_GC), col(_B_H), _prev_halo(tm, G, _B_GC), _prev_halo(tm, G, _B_H), _resident((3, G))],
        out_specs=pl.BlockSpec((tm, G), lambda i: (i, 0)),
        out_shape=jax.ShapeDtypeStruct((t, G), BF16),
        compiler_params=_cparams("parallel"))(p, p, p, p, p, w)


def _sconv_bwd(p, dmix, w, tm=512):
    t = p.shape[0]

    def body(gb_ref, gc_ref, h_ref, gc_halo, h_halo, gb_next, dy_ref, dy_next, w_ref, dp_ref, dw_ref):
        @pl.when(pl.program_id(0) == 0)
        def _():
            dw_ref[...] = jnp.zeros_like(dw_ref)
        w_v = w_ref[...]
        s_ext = _with_prev(gc_halo, gc_ref) * _with_prev(h_halo, h_ref)
        dout = dy_ref[...]
        d_gb = dout * _causal_conv(s_ext, w_v, tm)
        dconv_ext = _with_next(dy_ref, dy_next) * _with_next(gb_ref, gb_next)
        ds = _causal_conv_t(dconv_ext, w_v, tm)
        dw_ref[...] += _conv_wgrad(s_ext, dconv_ext[0:tm], 3, tm)
        dp_ref[...] = jnp.concatenate([d_gb, ds * h_ref[...], ds * gc_ref[...]], axis=1).astype(BF16)

    col = lambda c: pl.BlockSpec((tm, G), lambda i: (i, c))
    return pl.pallas_call(
        body, name="sconv_bwd", grid=(t // tm,),
        in_specs=[col(_B_GB), col(_B_GC), col(_B_H), _prev_halo(tm, G, _B_GC), _prev_halo(tm, G, _B_H),
                  _next_halo(tm, G, _B_GB, t), col(1), _next_halo(tm, G, 1, t), _resident((3, G))],
        out_specs=[pl.BlockSpec((tm, 3 * G), lambda i: (i, 0)), _acc((3, G))],
        out_shape=[jax.ShapeDtypeStruct((t, 3 * G), BF16), jax.ShapeDtypeStruct((3, G), F32)],
        compiler_params=_cparams("arbitrary"))(p, p, p, p, p, p, dmix, dmix, w)


def _qkv_conv_fwd(p, w, tm=512):
    t = p.shape[0]

    def body(x_ref, x_halo, w_ref, y_ref):
        c = _causal_conv(_with_prev(x_halo, x_ref), w_ref[...], tm)
        y_ref[...] = c * _sigmoid(c)

    return pl.pallas_call(
        body, name="qkv_conv_fwd", grid=(t // tm,),
        in_specs=[pl.BlockSpec((tm, 3 * G), lambda i: (i, _C_QKV)), _prev_halo(tm, 3 * G, _C_QKV), _resident((4, 3 * G))],
        out_specs=pl.BlockSpec((tm, 3 * G), lambda i: (i, 0)),
        out_shape=jax.ShapeDtypeStruct((t, 3 * G), F32),
        compiler_params=_cparams("parallel"))(p, p, w)


def _qkv_conv_bwd(p, dy, w, tm=512):
    t = p.shape[0]

    def body(x_ref, x_halo, x_next, dy_ref, dy_next, w_ref, dx_ref, dw_ref):
        @pl.when(pl.program_id(0) == 0)
        def _():
            dw_ref[...] = jnp.zeros_like(dw_ref)
        w_v = w_ref[...]
        x_all = jnp.concatenate([_with_prev(x_halo, x_ref), jnp.where(
            pl.program_id(0) == pl.num_programs(0) - 1, 0.0, x_next[...])], axis=0)
        c = _causal_conv(x_all, w_v, tm + HALO)
        s = _sigmoid(c)
        dc_ext = _with_next(dy_ref, dy_next) * (s * (1.0 + c * (1.0 - s)))
        dx_ref[...] = _causal_conv_t(dc_ext, w_v, tm).astype(BF16)
        dw_ref[...] += _conv_wgrad(x_all[0:HALO + tm], dc_ext[0:tm], 4, tm)

    return pl.pallas_call(
        body, name="qkv_conv_bwd", grid=(t // tm,),
        in_specs=[pl.BlockSpec((tm, 3 * G), lambda i: (i, _C_QKV)), _prev_halo(tm, 3 * G, _C_QKV),
                  _next_halo(tm, 3 * G, _C_QKV, t), pl.BlockSpec((tm, 3 * G), lambda i: (i, 0)),
                  _next_halo(tm, 3 * G, 0, t), _resident((4, 3 * G))],
        out_specs=[pl.BlockSpec((tm, 3 * G), lambda i: (i, 0)), _acc((4, 3 * G))],
        out_shape=[jax.ShapeDtypeStruct((t, 3 * G), BF16), jax.ShapeDtypeStruct((4, 3 * G), F32)],
        compiler_params=_cparams("arbitrary"))(p, p, p, dy, dy, w)


def _dn_fwd(qkv, p, params, cps=4):
    t = qkv.shape[0]
    tm = DN_C * cps
    nb = cps * NH

    def body(qkv_ref, z_ref, ab_ref, alog_ref, dt_ref, nw_ref, y_ref, st_ref, inv_ref, state):
        @pl.when(pl.program_id(0) == 0)
        def _():
            state[...] = jnp.zeros_like(state)
        st_ref[0] = state[...]
        y, new, inv = _dn_tile(qkv_ref[:, 0:G], qkv_ref[:, G:2 * G], qkv_ref[:, 2 * G:3 * G], ab_ref[...], z_ref[...],
                               state[...], None, alog_ref[...], dt_ref[...], nw_ref[...])
        y_ref[...] = y.astype(BF16)
        inv_ref[...] = inv
        state[...] = new

    return pl.pallas_call(
        body, name="dn_fwd", grid=(t // tm,),
        in_specs=[pl.BlockSpec((tm, 3 * G), lambda i: (i, 0)), pl.BlockSpec((tm, G), lambda i: (i, _C_Z)),
                  pl.BlockSpec((tm, LANES), lambda i: (i, _C_AB))] + [_resident(a.shape) for a in params],
        out_specs=[pl.BlockSpec((tm, G), lambda i: (i, 0)), pl.BlockSpec((1, NH, HD, HD), lambda i: (i, 0, 0, 0)),
                   pl.BlockSpec((nb, DN_C, DN_C), lambda i: (i, 0, 0))],
        out_shape=[jax.ShapeDtypeStruct((t, G), BF16), jax.ShapeDtypeStruct((t // tm, NH, HD, HD), F32),
                   jax.ShapeDtypeStruct((t // DN_C * NH, DN_C, DN_C), F32)],
        scratch_shapes=[pltpu.VMEM((NH, HD, HD), F32)],
        compiler_params=_cparams("arbitrary"))(qkv, p, p, *params)


def _dn_bwd(qkv, p, states, inv, dmix, params, cps=4):
    t = qkv.shape[0]
    tm = DN_C * cps
    n = t // tm
    nb = cps * NH

    def body(qkv_ref, z_ref, ab_ref, st_ref, inv_ref, dy_ref, alog_ref, dt_ref, nw_ref,
             dqkv_ref, dz_ref, dab_ref, dalog_ref, ddt_ref, dnw_ref, dstate):
        dprm_refs = (dalog_ref, ddt_ref, dnw_ref)

        @pl.when(pl.program_id(0) == 0)
        def _():
            dstate[...] = jnp.zeros_like(dstate)
            for r in dprm_refs:
                r[...] = jnp.zeros_like(r)
        inv_v = inv_ref[...]
        tile = lambda q, k, v, ab, z, st, alog, dt, nw: _dn_tile(q, k, v, ab, z, st, inv_v, alog, dt, nw)[:2]
        _, vjp = jax.vjp(tile, qkv_ref[:, 0:G], qkv_ref[:, G:2 * G], qkv_ref[:, 2 * G:3 * G], ab_ref[...], z_ref[...],
                         st_ref[0], alog_ref[...], dt_ref[...], nw_ref[...])
        dq, dk, dv, dab, dz, dst, *dprm = vjp((dy_ref[...], dstate[...]))
        dqkv_ref[...] = jnp.concatenate([dq, dk, dv], axis=1)
        dz_ref[...] = dz.astype(BF16)
        dab_ref[...] = dab.astype(BF16)
        dstate[...] = dst
        for r, g in zip(dprm_refs, dprm):
            r[...] += g

    rev = lambda w, blk: pl.BlockSpec((tm, w), lambda i: (n - 1 - i, blk))
    return pl.pallas_call(
        body, name="dn_bwd", grid=(n,),
        in_specs=[rev(3 * G, 0), rev(G, _C_Z), rev(LANES, _C_AB),
                  pl.BlockSpec((1, NH, HD, HD), lambda i: (n - 1 - i, 0, 0, 0)),
                  pl.BlockSpec((nb, DN_C, DN_C), lambda i: (n - 1 - i, 0, 0)), rev(G, 2)]
        + [_resident(a.shape) for a in params],
        out_specs=[rev(3 * G, 0), rev(G, 0), rev(LANES, 0)] + [_acc(a.shape) for a in params],
        out_shape=[jax.ShapeDtypeStruct((t, 3 * G), F32), jax.ShapeDtypeStruct((t, G), BF16),
                   jax.ShapeDtypeStruct((t, LANES), BF16)] + [jax.ShapeDtypeStruct(a.shape, F32) for a in params],
        scratch_shapes=[pltpu.VMEM((NH, HD, HD), F32)],
        compiler_params=_cparams("arbitrary"))(qkv, p, p, states, inv, dmix, *params)


def _gla_fwd(p, params, cps=4):
    t = p.shape[0]
    tm = GLA_C * cps

    def body(qkv_ref, z_ref, glr_ref, w2_ref, gb_ref, nw_ref, y_ref, st_ref, state):
        @pl.when(pl.program_id(0) == 0)
        def _():
            state[...] = jnp.zeros_like(state)
        st_ref[0] = state[...]
        y, new, _ = _gla_tile(qkv_ref[:, 0:G], qkv_ref[:, G:2 * G], qkv_ref[:, 2 * G:3 * G], glr_ref[...], z_ref[...],
                              state[...], None, w2_ref[...], gb_ref[...], nw_ref[...])
        y_ref[...] = y.astype(BF16)
        state[...] = new

    return pl.pallas_call(
        body, name="gla_fwd", grid=(t // tm,),
        in_specs=[pl.BlockSpec((tm, 3 * G), lambda i: (i, _D_QKV)), pl.BlockSpec((tm, G), lambda i: (i, _D_Z)),
                  pl.BlockSpec((tm, LANES), lambda i: (i, _D_GLR))] + [_resident(a.shape) for a in params],
        out_specs=[pl.BlockSpec((tm, G), lambda i: (i, 0)), pl.BlockSpec((1, NH, HD, HD), lambda i: (i, 0, 0, 0))],
        out_shape=[jax.ShapeDtypeStruct((t, G), BF16), jax.ShapeDtypeStruct((t // tm, NH, HD, HD), F32)],
        scratch_shapes=[pltpu.VMEM((NH, HD, HD), F32)],
        compiler_params=_cparams("arbitrary"))(p, p, p, *params)


def _gla_bwd(p, states, dmix, params, cps=4):
    t = p.shape[0]
    tm = GLA_C * cps
    n = t // tm

    def body(qkv_ref, z_ref, glr_ref, st_ref, dy_ref, w2_ref, gb_ref, nw_ref,
             dqkv_ref, dz_ref, dglr_ref, dw2_ref, dgb_ref, dnw_ref, dstate):
        dprm_refs = (dw2_ref, dgb_ref, dnw_ref)

        @pl.when(pl.program_id(0) == 0)
        def _():
            dstate[...] = jnp.zeros_like(dstate)
            for r in dprm_refs:
                r[...] = jnp.zeros_like(r)
        tile = lambda q, k, v, glr, z, st, w2, gb, nw: _gla_tile(q, k, v, glr, z, st, None, w2, gb, nw)[:2]
        _, vjp = jax.vjp(tile, qkv_ref[:, 0:G], qkv_ref[:, G:2 * G], qkv_ref[:, 2 * G:3 * G], glr_ref[...], z_ref[...],
                         st_ref[0], w2_ref[...], gb_ref[...], nw_ref[...])
        dq, dk, dv, dglr, dz, dst, *dprm = vjp((dy_ref[...], dstate[...]))
        dqkv_ref[...] = jnp.concatenate([dq, dk, dv], axis=1).astype(BF16)
        dz_ref[...] = dz.astype(BF16)
        dglr_ref[...] = dglr.astype(BF16)
        dstate[...] = dst
        for r, g in zip(dprm_refs, dprm):
            r[...] += g

    rev = lambda w, blk: pl.BlockSpec((tm, w), lambda i: (n - 1 - i, blk))
    return pl.pallas_call(
        body, name="gla_bwd", grid=(n,),
        in_specs=[rev(3 * G, _D_QKV), rev(G, _D_Z), rev(LANES, _D_GLR),
                  pl.BlockSpec((1, NH, HD, HD), lambda i: (n - 1 - i, 0, 0, 0)), rev(G, 3)]
        + [_resident(a.shape) for a in params],
        out_specs=[rev(3 * G, 0), rev(G, 0), rev(LANES, 0)] + [_acc(a.shape) for a in params],
        out_shape=[jax.ShapeDtypeStruct((t, 3 * G), BF16), jax.ShapeDtypeStruct((t, G), BF16),
                   jax.ShapeDtypeStruct((t, LANES), BF16)] + [jax.ShapeDtypeStruct(a.shape, F32) for a in params],
        scratch_shapes=[pltpu.VMEM((NH, HD, HD), F32)],
        compiler_params=_cparams("arbitrary"))(p, p, p, states, dmix, *params)


def _adamw_math(w, g, m, v):
    m = ADAM_B1 * m + (1.0 - ADAM_B1) * g
    v = ADAM_B2 * v + (1.0 - ADAM_B2) * (g * g)
    m_hat = m / (1.0 - ADAM_B1 ** ADAM_STEP)
    v_hat = v / (1.0 - ADAM_B2 ** ADAM_STEP)
    return -ADAM_LR * (m_hat / (jnp.sqrt(v_hat) + ADAM_EPS) + ADAM_WD * w), m, v


def _adamw(parts, w, m, v, name, rows):
    n, r = parts.shape[0], w.shape[0]

    def body(p_ref, w_ref, m_ref, v_ref, g_ref, d_ref, nm_ref, nv_ref):
        g = p_ref[0].astype(F32)
        for k in range(1, n):
            g = g + p_ref[k].astype(F32)
        g_ref[...] = g
        d_ref[...], nm_ref[...], nv_ref[...] = _adamw_math(w_ref[...], g, m_ref[...], v_ref[...])

    blk = pl.BlockSpec((rows, LANES), lambda i: (i, 0))
    return pl.pallas_call(
        body, name=name, grid=(r // rows,),
        in_specs=[pl.BlockSpec((n, rows, LANES), lambda i: (0, i, 0)), blk, blk, blk],
        out_specs=[blk] * 4, out_shape=[jax.ShapeDtypeStruct((r, LANES), F32)] * 4,
        compiler_params=_cparams("parallel"))(parts, w, m, v)


def _sum_parts(parts, name):
    def body(p_ref, o_ref):
        g = p_ref[0]
        for k in range(1, parts.shape[0]):
            g = g + p_ref[k]
        o_ref[...] = g

    return pl.pallas_call(body, name=name, out_shape=jax.ShapeDtypeStruct(parts.shape[1:], F32),
                          compiler_params=_cparams())(parts)


_ANY = pl.BlockSpec(memory_space=pl.ANY)


def _place():
    return lax.axis_index("x"), lax.axis_index("y"), lax.axis_index("c")


def _all_gather(slab, name):
    def body(x_ref, out_ref, send_sems, recv_sems, local_sem):
        x, y, c = _place()
        me, sibling = (x, y, c), (x, y, 1 - c)
        chips = [(1 - x, y), (x, 1 - y), (1 - x, 1 - y)]

        def slot(px, py, pc):
            return out_ref.at[4 * px + 2 * py + pc]

        def copy(k, block, to, src=None):
            return pltpu.make_async_remote_copy(
                src_ref=slot(*block) if src is None else src, dst_ref=slot(*block),
                send_sem=send_sems.at[k], recv_sem=recv_sems.at[k], device_id=to, device_id_type=pl.DeviceIdType.MESH)

        mine = pltpu.make_async_copy(x_ref, slot(*me), local_sem)
        mine.start()
        first = [copy(0, me, sibling, src=x_ref)]
        first += [copy(1 + j, me, (*chip, c), src=x_ref) for j, chip in enumerate(chips)]
        for cp in first:
            cp.start()
        passed = [copy(4 + j, (*chip, c), sibling) for j, chip in enumerate(chips)]
        for j, chip in enumerate(chips):
            copy(1 + j, (*chip, c), me).wait_recv()
            passed[j].start()
        copy(0, sibling, me).wait_recv()
        for j, chip in enumerate(chips):
            copy(4 + j, (*chip, 1 - c), me).wait_recv()
        for cp in first + passed:
            cp.wait_send()
        mine.wait()

    return pl.pallas_call(
        body, name=name, out_shape=jax.ShapeDtypeStruct((N_DEV,) + slab.shape, slab.dtype),
        in_specs=[_ANY], out_specs=_ANY,
        scratch_shapes=[pltpu.SemaphoreType.DMA((7,)), pltpu.SemaphoreType.DMA((7,)), pltpu.SemaphoreType.DMA])(slab)


def _all_to_all(slabs, name):
    def body(x_ref, out_ref, send_sems, recv_sems, local_sem):
        x, y, c = _place()
        me = 4 * x + 2 * y + c

        def peer(k):
            return (x ^ ((k >> 2) & 1), y ^ ((k >> 1) & 1), c ^ (k & 1))

        def copy(k):
            px, py, pc = peer(k)
            return pltpu.make_async_remote_copy(
                src_ref=x_ref.at[4 * px + 2 * py + pc], dst_ref=out_ref.at[me],
                send_sem=send_sems.at[k - 1], recv_sem=recv_sems.at[k - 1], device_id=(px, py, pc),
                device_id_type=pl.DeviceIdType.MESH)

        mine = pltpu.make_async_copy(x_ref.at[me], out_ref.at[me], local_sem)
        mine.start()
        copies = [copy(k) for k in range(1, N_DEV)]
        for cp in copies:
            cp.start()
        for k in range(1, N_DEV):
            px, py, pc = peer(k)
            pltpu.make_async_remote_copy(
                src_ref=x_ref.at[me], dst_ref=out_ref.at[4 * px + 2 * py + pc], send_sem=send_sems.at[k - 1],
                recv_sem=recv_sems.at[k - 1], device_id=(px, py, pc), device_id_type=pl.DeviceIdType.MESH).wait_recv()
        for cp in copies:
            cp.wait_send()
        mine.wait()

    return pl.pallas_call(
        body, name=name, out_shape=jax.ShapeDtypeStruct(slabs.shape, slabs.dtype), in_specs=[_ANY], out_specs=_ANY,
        scratch_shapes=[pltpu.SemaphoreType.DMA((7,)), pltpu.SemaphoreType.DMA((7,)), pltpu.SemaphoreType.DMA])(slabs)


def _pad_cols(w):
    parts = []
    for a, b, z in _PAD_SEGS:
        parts.append(w[..., a:b])
        if z:
            parts.append(jnp.zeros(w.shape[:-1] + (z,), w.dtype))
    return jnp.concatenate(parts, axis=-1)


def _unpad_cols(wp):
    pieces, off = [], 0
    for a, b, z in _PAD_SEGS:
        pieces.append((a, wp[..., off:off + (b - a)]))
        off += (b - a) + z
    return jnp.concatenate([piece for _, piece in sorted(pieces, key=lambda ap: ap[0])], axis=-1)


def _to_slab(parts, rows):
    flat = jnp.concatenate([a.reshape(-1) for a in parts])
    return jnp.pad(flat, (0, rows * LANES - flat.shape[0])).reshape(rows, LANES)


def _from_slab(slab, shapes):
    flat, out, off = slab.reshape(-1), [], 0
    for s in shapes:
        n = 1
        for d in s:
            n *= d
        out.append(flat[off:off + n].reshape(s))
        off += n
    return out


def _local_grads(x, target, wts):
    saved = []
    for l in range(DEPTH):
        w = {k: v[l] for k, v in wts.items() if k != "final_norm_w"}
        h, p = _in_proj(x, w["norm1_w"][None], w["w_in"])
        ya = _sgu_fwd(p, w["sgu_ln_w"][None], w["sgu_ln_b"][None], w["sgu_w_spatial"], w["sgu_b_spatial"].T)
        yb = _sconv_fwd(p, w["sc_conv_w"])
        qkv_c = _qkv_conv_fwd(p, w["dn_conv_w"])
        dn_params = (w["dn_a_log"][None], w["dn_dt_bias"][None], w["dn_norm_w"][None])
        yc, st_c, inv_c = _dn_fwd(qkv_c, p, dn_params)
        w2 = jnp.pad(w["gla_w_gate2"], ((0, LANES - 16), (0, 0)))
        gla_params = (w2, w["gla_gate_bias"][None], w["gla_norm_w"][None])
        yd, st_d = _gla_fwd(p, gla_params)
        mix, x1, h2, gu, act, x2 = _out_mlp(x, (ya, yb, yc, yd), w["w_out"], w["norm2_w"][None], w["w_gate_up"],
                                            w["w_down"])
        saved.append(dict(w=w, x=x, h=h, p=p, qkv_c=qkv_c, st_c=st_c, inv_c=inv_c, st_d=st_d, mix=mix, x1=x1, h2=h2,
                          gu=gu, act=act, dn_params=dn_params, gla_params=gla_params))
        x = x2
    loss, d_fnw, dx = _loss_head(x, wts["final_norm_w"][None], target)
    grads = {k: [None] * DEPTH for k in wts if k != "final_norm_w"}
    grads["final_norm_w"] = d_fnw[0]
    for l in reversed(range(DEPTH)):
        s = saved[l]
        w, p = s["w"], s["p"]
        dgu, dx1, dmix, d_nw2 = _mlp_out_bwd(dx, s["x1"], s["gu"], w["norm2_w"][None], w["w_down"], w["w_gate_up"],
                                             w["w_out"])
        g = {}
        g["norm2_w"] = d_nw2[0]
        g["w_gate_up"] = _matmul_tn(s["h2"], dgu, "wgrad_gate_up")
        g["w_down"] = _matmul_tn(s["act"], dx, "wgrad_down")
        g["w_out"] = _matmul_tn(s["mix"], dx1, "wgrad_out")
        d_a, g_lw, g_lb, g_ws, g_bs = _sgu_bwd(p, dmix, w["sgu_ln_w"][None], w["sgu_ln_b"][None], w["sgu_w_spatial"],
                                              w["sgu_b_spatial"].T)
        g["sgu_ln_w"], g["sgu_ln_b"], g["sgu_w_spatial"], g["sgu_b_spatial"] = g_lw[0], g_lb[0], g_ws, g_bs.T
        d_b, g["sc_conv_w"] = _sconv_bwd(p, dmix, w["sc_conv_w"])
        dqkv_c, dz_c, dab, g_alog, g_dt, g_cn = _dn_bwd(s["qkv_c"], p, s["st_c"], s["inv_c"], dmix, s["dn_params"])
        g["dn_a_log"], g["dn_dt_bias"], g["dn_norm_w"] = g_alog[0], g_dt[0], g_cn[0]
        d_c, g["dn_conv_w"] = _qkv_conv_bwd(p, dqkv_c, w["dn_conv_w"])
        d_d, dz_d, dglr, g_w2, g_gb, g_dn = _gla_bwd(p, s["st_d"], dmix, s["gla_params"])
        g["gla_w_gate2"], g["gla_gate_bias"], g["gla_norm_w"] = g_w2[:16], g_gb[0], g_dn[0]
        dp, dx, d_nw1 = _in_proj_bwd((d_c, d_d, d_a, d_b, dz_c, dz_d, dab, dglr), s["x"], dx1, w["norm1_w"][None],
                                     w["w_in"])
        g["norm1_w"] = d_nw1[0]
        g["w_in"] = _matmul_tn(s["h"], dp, "wgrad_in")
        for k, v in g.items():
            grads[k][l] = v
    grads = {k: (v if k == "final_norm_w" else jnp.stack(v)) for k, v in grads.items()}
    return loss[0, 0], dx, grads


_BIG = ("w_in", "w_out", "w_gate_up", "w_down")
_SMALL_SHARDED = ("sc_conv_w", "dn_conv_w", "gla_w_gate2")
_REPLICATED = ("norm1_w", "sgu_ln_w", "sgu_ln_b", "sgu_w_spatial", "sgu_b_spatial", "dn_a_log", "dn_dt_bias",
               "dn_norm_w", "gla_gate_bias", "gla_norm_w", "norm2_w", "final_norm_w")
_ORDER = ("norm1_w", "w_in", "sgu_ln_w", "sgu_ln_b", "sgu_w_spatial", "sgu_b_spatial", "sc_conv_w", "dn_conv_w",
          "dn_a_log", "dn_dt_bias", "dn_norm_w", "gla_w_gate2", "gla_gate_bias", "gla_norm_w", "w_out", "norm2_w",
          "w_gate_up", "w_down", "final_norm_w")
_BIG_ROWS = 26112
_BIG_TILE = 1536
_SMALL_ROWS = 1216
_OWN_ROWS = 1104


def _gather_cols(g):
    return jnp.transpose(g, (1, 2, 0, 3)).reshape(g.shape[1], g.shape[2], N_DEV * g.shape[3])


def _gather_rows(g):
    return jnp.transpose(g, (1, 0, 2, 3)).reshape(g.shape[1], N_DEV * g.shape[2], g.shape[3])


def _scatter_cols(w):
    return jnp.transpose(w.reshape(w.shape[0], w.shape[1], N_DEV, w.shape[2] // N_DEV), (2, 0, 1, 3))


def _scatter_rows(w):
    return jnp.transpose(w.reshape(w.shape[0], N_DEV, w.shape[1] // N_DEV, w.shape[2]), (1, 0, 2, 3))


def kernel(x, norm1_w, w_in, sgu_ln_w, sgu_ln_b, sgu_w_spatial, sgu_b_spatial, sc_conv_w, dn_conv_w, dn_a_log, dn_dt_bias, dn_norm_w, gla_w_gate2, gla_gate_bias, gla_norm_w, w_out, norm2_w, w_gate_up, w_down, final_norm_w, loss_target, m_norm1_w, m_w_in, m_sgu_ln_w, m_sgu_ln_b, m_sgu_w_spatial, m_sgu_b_spatial, m_sc_conv_w, m_dn_conv_w, m_dn_a_log, m_dn_dt_bias, m_dn_norm_w, m_gla_w_gate2, m_gla_gate_bias, m_gla_norm_w, m_w_out, m_norm2_w, m_w_gate_up, m_w_down, m_final_norm_w, v_norm1_w, v_w_in, v_sgu_ln_w, v_sgu_ln_b, v_sgu_w_spatial, v_sgu_b_spatial, v_sc_conv_w, v_dn_conv_w, v_dn_a_log, v_dn_dt_bias, v_dn_norm_w, v_gla_w_gate2, v_gla_gate_bias, v_gla_norm_w, v_w_out, v_norm2_w, v_w_gate_up, v_w_down, v_final_norm_w):
    args = dict(locals())
    wts = {k: args[k] for k in _ORDER}
    mom = {k: args["m_" + k] for k in _ORDER}
    var = {k: args["v_" + k] for k in _ORDER}
    me = 4 * lax.axis_index("x") + 2 * lax.axis_index("y") + lax.axis_index("c")

    big_shapes = [wts[k].shape for k in _BIG]
    got = _all_gather(_to_slab([wts[k].astype(BF16) for k in _BIG], _BIG_ROWS), "gather_weights")
    got = [jnp.stack(parts) for parts in zip(*[_from_slab(got[d], big_shapes) for d in range(N_DEV)])]
    full = dict(wts)
    full["w_in"] = _pad_cols(_gather_cols(got[0]))
    full["w_out"] = _gather_rows(got[1])
    full["w_gate_up"] = _gather_cols(got[2])
    full["w_down"] = _gather_rows(got[3])
    small_shapes = [wts[k].shape for k in _SMALL_SHARDED]
    got = _all_gather(_to_slab([wts[k] for k in _SMALL_SHARDED], 16), "gather_small_weights")
    got = [jnp.stack(parts) for parts in zip(*[_from_slab(got[d], small_shapes) for d in range(N_DEV)])]
    for k, g in zip(_SMALL_SHARDED, got):
        full[k] = _gather_cols(g)

    loss, dx, grads = _local_grads(x[0], loss_target[0], full)
    loss = lax.psum(loss, ("x", "y", "c"))

    per_dev = [_scatter_cols(_unpad_cols(grads["w_in"])), _scatter_rows(grads["w_out"]),
               _scatter_cols(grads["w_gate_up"]), _scatter_rows(grads["w_down"])]
    send = jnp.stack([_to_slab([a[d].astype(BF16) for a in per_dev], _BIG_ROWS) for d in range(N_DEV)])
    parts = _all_to_all(send, "exchange_grads")
    outs = _adamw(parts, _to_slab([wts[k] for k in _BIG], _BIG_ROWS), _to_slab([mom[k] for k in _BIG], _BIG_ROWS),
                  _to_slab([var[k] for k in _BIG], _BIG_ROWS), "adamw_large", _BIG_TILE)
    result = {}
    for kind, slab in zip(("grad", "delta", "new_m", "new_v"), outs):
        for k, a in zip(_BIG, _from_slab(slab, big_shapes)):
            result[kind, k] = a

    small = _REPLICATED + _SMALL_SHARDED
    parts = _all_gather(_to_slab([grads[k] for k in small], _SMALL_ROWS), "gather_small_grads")
    gs = dict(zip(small, _from_slab(_sum_parts(parts, "sum_small_grads"), [grads[k].shape for k in small])))
    own = lambda d, k: lax.dynamic_slice_in_dim(d, me * wts[k].shape[-1], wts[k].shape[-1], axis=d.ndim - 1)
    own_grads = _to_slab([gs[k] if k in _REPLICATED else own(gs[k], k) for k in small], _OWN_ROWS)
    outs = _adamw(own_grads[None], _to_slab([wts[k] for k in small], _OWN_ROWS),
                  _to_slab([mom[k] for k in small], _OWN_ROWS), _to_slab([var[k] for k in small], _OWN_ROWS),
                  "adamw_small", _OWN_ROWS)
    own_shapes = [wts[k].shape for k in small]
    for kind, slab in zip(("grad", "delta", "new_m", "new_v"), outs):
        for k, a in zip(small, _from_slab(slab, own_shapes)):
            result[kind, k] = a

    return (loss, dx[None], *[result[kind, k] for kind in ("grad", "delta", "new_m", "new_v") for k in _ORDER])
```

```python
import functools

import jax
import jax.numpy as jnp
from jax import lax
from jax.experimental import pallas as pl
from jax.experimental.pallas import tpu as pltpu

F32, BF16 = jnp.float32, jnp.bfloat16
DEPTH = 2
D = 1024
G = 256
NH, HD = 4, 64
FF = 2816
IN_COLS = 3352
P = 3584
EPS = 1e-6
SGU_C, DN_C, GLA_C = 128, 64, 64
N_DEV = 8
IN_SHARD = IN_COLS // N_DEV
GU_SHARD = 2 * FF // N_DEV
N_GATE = N_DEV // 2
LANES = 128
VMEM_LIMIT = 56 * 2 ** 20

_PAD_SEGS = ((1280, 2048, 0),
             (2312, 3080, 0),
             (0, 512, 0),
             (512, 1280, 0),
             (2056, 2312, 0),
             (3096, 3352, 0),
             (2048, 2056, 120),
             (3080, 3096, 112))

ADAM_LR, ADAM_B1, ADAM_B2, ADAM_EPS, ADAM_WD, ADAM_STEP = 0.001, 0.9, 0.999, 1e-08, 0.01, 10


def _cparams(*sem):
    return pltpu.CompilerParams(dimension_semantics=sem, vmem_limit_bytes=VMEM_LIMIT)


def _resident(shape):
    return pl.BlockSpec(shape, lambda *_: (0,) * len(shape), pipeline_mode=pl.Buffered(1))


def _layer(shape, l):
    return pl.BlockSpec((None,) + tuple(shape), lambda *_: (l,) + (0,) * len(shape), pipeline_mode=pl.Buffered(1))


def _acc(shape):
    return pl.BlockSpec(shape, lambda *_: (0,) * len(shape))


def _dg(a, b, ca, cb, precision=None):
    lead = a.ndim - 2
    batch = ((0,), (0,)) if lead else ((), ())
    return lax.dot_general(a, b, (((ca + lead,), (cb + lead,)), batch), preferred_element_type=F32, precision=precision)


def _make_dot(cast, precision):
    def raw(a, b, form):
        ca = 0 if form == "tn" else 1
        cb = 1 if form == "nt" else 0
        return _dg(cast(a), cast(b), ca, cb, precision)

    @functools.partial(jax.custom_vjp, nondiff_argnums=(2,))
    def dot(a, b, form):
        return raw(a, b, form)

    def fwd(a, b, form):
        return raw(a, b, form), (a, b)

    def bwd(form, res, g):
        a, b = res
        if form == "nn":
            return raw(g, b, "nt"), raw(a, g, "tn")
        if form == "nt":
            return raw(g, b, "nn"), raw(g, a, "tn")
        return raw(b, g, "nt"), raw(a, g, "nn")

    dot.defvjp(fwd, bwd)
    return dot


_bdot = _make_dot(lambda t: t.astype(BF16), None)
_hdot = _make_dot(lambda t: t, lax.Precision.HIGHEST)


def _inv_unit_lower(low):
    n = low.shape[-1]
    eye = (lax.broadcasted_iota(jnp.int32, (n, n), 0) == lax.broadcasted_iota(jnp.int32, (n, n), 1)).astype(F32)
    p = -low
    inv = eye + p
    k = 1
    while 2 * k < n:
        p = _dg(p, p, 1, 0, lax.Precision.HIGHEST)
        inv = inv + _dg(inv, p, 1, 0, lax.Precision.HIGHEST)
        k *= 2
    return inv


@jax.custom_vjp
def _unit_lower_solve(low, rhs, inv):
    return _dg(inv, rhs, 1, 0, lax.Precision.HIGHEST)


def _uls_fwd(low, rhs, inv):
    sol = _dg(inv, rhs, 1, 0, lax.Precision.HIGHEST)
    return sol, (inv, sol)


def _uls_bwd(res, g):
    inv, sol = res
    d_rhs = _dg(inv, g, 0, 0, lax.Precision.HIGHEST)
    return -_dg(d_rhs, sol, 1, 1, lax.Precision.HIGHEST), d_rhs, jnp.zeros_like(inv)


_unit_lower_solve.defvjp(_uls_fwd, _uls_bwd)


def _sigmoid(x):
    return 1.0 / (1.0 + jnp.exp(-x))


def _softplus(x):
    return jnp.maximum(x, 0.0) + jnp.log(1.0 + jnp.exp(-jnp.maximum(x, -x)))


def _gelu(x):
    return 0.5 * x * (1.0 + jnp.tanh(0.7978845608028654 * (x + 0.044715 * (x * x * x))))


def _tri(n):
    ri = lax.broadcasted_iota(jnp.int32, (n, n), 0)
    ci = lax.broadcasted_iota(jnp.int32, (n, n), 1)
    return ri, ci


def _stack(ts):
    return jnp.concatenate([t[None] for t in ts], axis=0)


def _sgu_chunk(uv, ln_w, ln_b, ws, bs):
    u = _gelu(uv[:, :G])
    v = _gelu(uv[:, G:])
    mu = jnp.mean(v, axis=-1, keepdims=True)
    vc = v - mu
    var = jnp.mean(vc * vc, axis=-1, keepdims=True)
    vn = vc * lax.rsqrt(var + EPS) * ln_w + ln_b
    ri, ci = _tri(SGU_C)
    bs_t = _hdot((ri == ci).astype(F32), bs, "nt")
    mixed = []
    for h in range(NH):
        wm = jnp.where(ri >= ci, ws[h], 0.0)
        mixed.append(_bdot(wm, vn[:, HD * h:HD * (h + 1)], "nn") + bs_t[:, h:h + 1])
    return u * jnp.concatenate(mixed, axis=1)


def _dn_tile(q, k, v, ab, z, state, inv, a_log, dt_bias, nw):
    c = DN_C
    tm = q.shape[0]
    cps = tm // c
    ri, ci = _tri(tm)
    shift = c.bit_length() - 1
    same = jnp.right_shift(ri, shift) == jnp.right_shift(ci, shift)
    g4 = -jnp.exp(a_log) * _softplus(ab[:, 0:NH] + dt_bias)
    beta4 = _sigmoid(ab[:, NH:2 * NH])
    gc4 = _hdot((same & (ri >= ci)).astype(F32), g4, "nn")
    gr4 = _hdot(g4, (same & (ri <= ci)).astype(F32), "tn")
    pairs = [(slice(c * j, c * (j + 1)), h) for j in range(cps) for h in range(NH)]
    heads = lambda t: _stack([t[rows, HD * h:HD * (h + 1)] for rows, h in pairs])
    col = lambda t: _stack([t[rows, h:h + 1] for rows, h in pairs])
    qn, kn, vh = heads(q), heads(k), heads(v)
    qn = qn * lax.rsqrt(jnp.sum(qn * qn, axis=-1, keepdims=True) + EPS) * (HD ** -0.5)
    kn = kn * lax.rsqrt(jnp.sum(kn * kn, axis=-1, keepdims=True) + EPS)
    gc, beta = col(gc4), col(beta4)
    gr = _stack([gr4[h:h + 1, rows] for rows, h in pairs])
    gl = jnp.sum(col(g4), axis=1, keepdims=True)
    r2, c2 = _tri(c)
    decay = jnp.exp(jnp.where(r2 >= c2, gc - gr, -jnp.inf))
    kb = kn * beta
    low = jnp.where(r2 > c2, _bdot(kb, kn, "nt") * decay, 0.0)
    eg = jnp.exp(gc)
    if inv is None:
        inv = _inv_unit_lower(low)
    sol = _unit_lower_solve(low, jnp.concatenate([vh * beta, kb * eg], axis=2), inv)
    u, w = sol[:, :, :HD], sol[:, :, HD:]
    attn = _bdot(qn, kn, "nt") * decay
    qg, kd, cd = qn * eg, kn * jnp.exp(gl - gc), jnp.exp(gl)
    ys = []
    for j in range(cps):
        b = slice(NH * j, NH * (j + 1))
        v_new = u[b] - _bdot(w[b], state, "nn")
        o = _bdot(qg[b], state, "nn") + _bdot(attn[b], v_new, "nn")
        state = state * cd[b] + _bdot(kd[b], v_new, "tn")
        on = o * lax.rsqrt(jnp.mean(o * o, axis=-1, keepdims=True) + EPS) * nw
        ys.append(jnp.concatenate([on[h] for h in range(NH)], axis=1))
    return jnp.concatenate(ys, axis=0) * (z * _sigmoid(z)), state, inv


def _gla_tile(q, k, v, glr, z, state_t, w2, gate_bias, nw):
    c = GLA_C
    tm = q.shape[0]
    cps = tm // c
    ri, ci = _tri(tm)
    shift = c.bit_length() - 1
    same = jnp.right_shift(ri, shift) == jnp.right_shift(ci, shift)
    w2_rows = jnp.concatenate([w2, jnp.zeros((LANES - w2.shape[0], G), F32)], axis=0)
    pre = _bdot(glr, w2_rows, "nn") + gate_bias
    log_a = (jnp.minimum(pre, 0.0) - jnp.log(1.0 + jnp.exp(-jnp.maximum(pre, -pre)))) * (1.0 / 16.0)
    gcum = _hdot((same & (ri >= ci)).astype(F32), log_a, "nn")
    row = lax.broadcasted_iota(jnp.int32, (c, G), 0)
    qs = q * (HD ** -0.5)
    qa, ka, qg, kl, dec = [], [], [], [], []
    for j in range(cps):
        rows = slice(c * j, c * (j + 1))
        gj = gcum[rows]
        g_mid = jnp.sum(jnp.where(row == c // 2, gj, 0.0), axis=0, keepdims=True)
        g_last = jnp.sum(log_a[rows], axis=0, keepdims=True)
        qa.append(qs[rows] * jnp.exp(gj - g_mid))
        ka.append(k[rows] * jnp.exp(g_mid - gj))
        qg.append(qs[rows] * jnp.exp(gj))
        kl.append(k[rows] * jnp.exp(g_last - gj))
        dec.append(jnp.exp(g_last))
    heads = lambda ts: _stack([t[:, HD * h:HD * (h + 1)] for t in ts for h in range(NH)])
    qa, ka, qg, kl, dec = heads(qa), heads(ka), heads(qg), heads(kl), heads(dec)
    vh = heads([v[c * j:c * (j + 1)] for j in range(cps)])
    r2, c2 = _tri(c)
    o_intra = _bdot(jnp.where(r2 >= c2, _bdot(qa, ka, "nt"), 0.0), vh, "nn")
    ys = []
    for j in range(cps):
        b = slice(NH * j, NH * (j + 1))
        o = o_intra[b] + _bdot(qg[b], state_t, "nt")
        state_t = state_t * dec[b] + _bdot(vh[b], kl[b], "tn")
        on = o * lax.rsqrt(jnp.mean(o * o, axis=-1, keepdims=True) + EPS) * nw
        ys.append(jnp.concatenate([on[h] for h in range(NH)], axis=1))
    return jnp.concatenate(ys, axis=0) * (z * _sigmoid(z)), state_t


def _rms(x, nw):
    r = lax.rsqrt(jnp.mean(x * x, axis=-1, keepdims=True) + EPS)
    xh = x * r
    return xh * nw, xh, r


def _rms_bwd(g, xh, r, nw):
    gw = g * nw
    return r * (gw - xh * jnp.mean(gw * xh, axis=-1, keepdims=True)), jnp.sum(g * xh, axis=0, keepdims=True)


def _row(tm, w, blk=0):
    return pl.BlockSpec((tm, w), lambda i: (i, blk))


def _in_proj(x, nw, wp, l, tm=256):
    t = x.shape[0]

    def body(x_ref, nw_ref, w_ref, h_ref, p_ref):
        h = _rms(x_ref[...], nw_ref[l:l + 1, :])[0].astype(BF16)
        h_ref[...] = h
        p_ref[...] = jnp.dot(h, w_ref[...], preferred_element_type=F32)

    return pl.pallas_call(
        body, name="in_proj", grid=(t // tm,),
        in_specs=[_row(tm, D), _resident((DEPTH, D)), _layer((D, P), l)],
        out_specs=[_row(tm, D), _row(tm, P)],
        out_shape=[jax.ShapeDtypeStruct((t, D), BF16), jax.ShapeDtypeStruct((t, P), F32)],
        compiler_params=_cparams("parallel"))(x, nw, wp)


def _gu_spec(l):
    return pl.BlockSpec((N_DEV, None, D, GU_SHARD), lambda *_: (0, l, 0, 0), pipeline_mode=pl.Buffered(1))


def _out_mlp(x, ys, w_out, nw2, w_gu, w_down, l, tm=256):
    t = x.shape[0]

    def body(x_ref, ya, yb, yc, yd, wo_ref, nw_ref, wgu_ref, wd_ref, mix_ref, x1_ref, h2_ref, gu_ref, act_ref, x2_ref):
        mix = jnp.concatenate([ya[...], yb[...], yc[...], yd[...]], axis=1)
        mix_ref[...] = mix
        x1 = x_ref[...] + jnp.dot(mix, wo_ref[...], preferred_element_type=F32)
        x1_ref[...] = x1
        h2 = _rms(x1, nw_ref[l:l + 1, :])[0].astype(BF16)
        h2_ref[...] = h2
        x2 = x1
        for d in range(N_GATE):
            gate = jnp.dot(h2, wgu_ref[d], preferred_element_type=F32)
            up = jnp.dot(h2, wgu_ref[d + N_GATE], preferred_element_type=F32)
            gu_ref[d] = gate.astype(BF16)
            gu_ref[d + N_GATE] = up.astype(BF16)
            act = (gate * _sigmoid(gate) * up).astype(BF16)
            act_ref[d] = act
            x2 = x2 + jnp.dot(act, wd_ref[GU_SHARD * d:GU_SHARD * (d + 1), :], preferred_element_type=F32)
        x2_ref[...] = x2

    dev = lambda n: pl.BlockSpec((n, tm, GU_SHARD), lambda i: (0, i, 0))
    return pl.pallas_call(
        body, name="out_mlp", grid=(t // tm,),
        in_specs=[_row(tm, D), _row(tm, G), _row(tm, G), _row(tm, G), _row(tm, G), _layer((D, D), l),
                  _resident((DEPTH, D)), _gu_spec(l), _layer((FF, D), l)],
        out_specs=[_row(tm, D), _row(tm, D), _row(tm, D), dev(N_DEV), dev(N_GATE), _row(tm, D)],
        out_shape=[jax.ShapeDtypeStruct((t, D), BF16), jax.ShapeDtypeStruct((t, D), F32),
                   jax.ShapeDtypeStruct((t, D), BF16), jax.ShapeDtypeStruct((N_DEV, t, GU_SHARD), BF16),
                   jax.ShapeDtypeStruct((N_GATE, t, GU_SHARD), BF16), jax.ShapeDtypeStruct((t, D), F32)],
        compiler_params=_cparams("parallel"))(x, *ys, w_out, nw2, w_gu, w_down)


def _loss_head(x, nw, target, tm=512):
    t = x.shape[0]

    def body(x_ref, nw_ref, tg_ref, loss_ref, dnw_ref, dx_ref):
        @pl.when(pl.program_id(0) == 0)
        def _():
            loss_ref[...] = jnp.zeros_like(loss_ref)
            dnw_ref[...] = jnp.zeros_like(dnw_ref)
        nw_v = nw_ref[...]
        y, xh, r = _rms(x_ref[...], nw_v)
        err = y - tg_ref[...]
        loss_ref[...] += 0.5 * jnp.sum(err * err) * (1.0 / D)
        dx, dnw = _rms_bwd(err * (1.0 / D), xh, r, nw_v)
        dx_ref[...] = dx
        dnw_ref[...] += dnw

    return pl.pallas_call(
        body, name="loss_head", grid=(t // tm,),
        in_specs=[_row(tm, D), _resident((1, D)), _row(tm, D)],
        out_specs=[_acc((8, LANES)), _acc((1, D)), _row(tm, D)],
        out_shape=[jax.ShapeDtypeStruct((8, LANES), F32), jax.ShapeDtypeStruct((1, D), F32),
                   jax.ShapeDtypeStruct((t, D), F32)],
        compiler_params=_cparams("arbitrary"))(x, nw, target)


def _mlp_out_bwd(dx2, x1, gu, nw2, w_down, w_gu, w_out, l, tm=256):
    t = dx2.shape[0]

    def body(dx2_ref, x1_ref, gu_ref, nw_ref, wd_ref, wgu_ref, wo_ref, dgu_ref, dx1_ref, dmix_ref, dnw_ref):
        @pl.when(pl.program_id(0) == 0)
        def _():
            dnw_ref[...] = jnp.zeros_like(dnw_ref)
        dx2 = dx2_ref[...]
        dx2_b = dx2.astype(BF16)
        dh2 = jnp.zeros((tm, D), F32)
        for d in range(N_GATE):
            dact = _dg(dx2_b, wd_ref[GU_SHARD * d:GU_SHARD * (d + 1), :], 1, 1)
            gate, up = gu_ref[d].astype(F32), gu_ref[d + N_GATE].astype(F32)
            s = _sigmoid(gate)
            dgate = (dact * up * (s * (1.0 + gate * (1.0 - s)))).astype(BF16)
            dup = (dact * (gate * s)).astype(BF16)
            dgu_ref[d] = dgate
            dgu_ref[d + N_GATE] = dup
            dh2 = dh2 + _dg(dgate, wgu_ref[d], 1, 1) + _dg(dup, wgu_ref[d + N_GATE], 1, 1)
        nw_v = nw_ref[l:l + 1, :]
        _, xh, r = _rms(x1_ref[...], nw_v)
        dxn, dnw = _rms_bwd(dh2, xh, r, nw_v)
        dx1 = dx2 + dxn
        dx1_ref[...] = dx1
        dnw_ref[...] += dnw
        dmix_ref[...] = _dg(dx1.astype(BF16), wo_ref[...], 1, 1)

    dev = pl.BlockSpec((N_DEV, tm, GU_SHARD), lambda i: (0, i, 0))
    return pl.pallas_call(
        body, name="mlp_out_bwd", grid=(t // tm,),
        in_specs=[_row(tm, D), _row(tm, D), dev, _resident((DEPTH, D)), _layer((FF, D), l), _gu_spec(l),
                  _layer((D, D), l)],
        out_specs=[dev, _row(tm, D), _row(tm, D), _acc((1, D))],
        out_shape=[jax.ShapeDtypeStruct((N_DEV, t, GU_SHARD), BF16), jax.ShapeDtypeStruct((t, D), F32),
                   jax.ShapeDtypeStruct((t, D), F32), jax.ShapeDtypeStruct((1, D), F32)],
        compiler_params=_cparams("arbitrary"))(dx2, x1, gu, nw2, w_down, w_gu, w_out)


_DP_WIDTHS = (768, 768, 512, 768, 256, 256, 128, 128)


def _in_proj_bwd(dps, x, dx1, nw, wp, l, tm=256):
    t = x.shape[0]

    def body(*refs):
        dp_refs, (x_ref, dx1_ref, nw_ref, w_ref, dp_ref, dx_ref, dnw_ref) = refs[:8], refs[8:]

        @pl.when(pl.program_id(0) == 0)
        def _():
            dnw_ref[...] = jnp.zeros_like(dnw_ref)
        dp = jnp.concatenate([r[...] for r in dp_refs], axis=1)
        dp_ref[...] = dp
        dh = _dg(dp, w_ref[...], 1, 1)
        nw_v = nw_ref[l:l + 1, :]
        _, xh, r = _rms(x_ref[...], nw_v)
        dxn, dnw = _rms_bwd(dh, xh, r, nw_v)
        dx_ref[...] = dx1_ref[...] + dxn
        dnw_ref[...] += dnw

    return pl.pallas_call(
        body, name="in_proj_bwd", grid=(t // tm,),
        in_specs=[_row(tm, w) for w in _DP_WIDTHS] + [_row(tm, D), _row(tm, D), _resident((DEPTH, D)), _layer((D, P), l)],
        out_specs=[_row(tm, P), _row(tm, D), _acc((1, D))],
        out_shape=[jax.ShapeDtypeStruct((t, P), BF16), jax.ShapeDtypeStruct((t, D), F32),
                   jax.ShapeDtypeStruct((1, D), F32)],
        compiler_params=_cparams("arbitrary"))(*dps, x, dx1, nw, wp)


def _accumulate(acc, o_ref, t_axis, term):
    @pl.when(pl.program_id(t_axis) == 0)
    def _():
        acc[...] = jnp.zeros_like(acc)
    acc[...] += term

    @pl.when(pl.program_id(t_axis) == pl.num_programs(t_axis) - 1)
    def _():
        o_ref[...] = acc[...].astype(o_ref.dtype)


def _matmul_tn(a, b, name, out_dtype, tn=512, tt=512):
    t, m = a.shape
    n = b.shape[1]

    def body(a_ref, b_ref, o_ref, acc):
        _accumulate(acc, o_ref, 1, _dg(a_ref[...].astype(BF16), b_ref[...].astype(BF16), 0, 0))

    return pl.pallas_call(
        body, name=name, grid=(n // tn, t // tt),
        in_specs=[pl.BlockSpec((tt, m), lambda j, i: (i, 0)), pl.BlockSpec((tt, tn), lambda j, i: (i, j))],
        out_specs=pl.BlockSpec((m, tn), lambda j, i: (0, j)),
        out_shape=jax.ShapeDtypeStruct((m, n), out_dtype),
        scratch_shapes=[pltpu.VMEM((m, tn), F32)],
        compiler_params=_cparams("parallel", "arbitrary"))(a, b)


def _wgrad_gate_up(h2, dgu, tt=512):
    t = h2.shape[0]

    def body(a_ref, b_ref, o_ref, acc):
        _accumulate(acc, o_ref, 1, _dg(a_ref[...], b_ref[...], 0, 0))

    return pl.pallas_call(
        body, name="wgrad_gate_up", grid=(N_DEV, t // tt),
        in_specs=[pl.BlockSpec((tt, D), lambda d, i: (i, 0)), pl.BlockSpec((None, tt, GU_SHARD), lambda d, i: (d, i, 0))],
        out_specs=pl.BlockSpec((None, D, GU_SHARD), lambda d, i: (d, 0, 0)),
        out_shape=jax.ShapeDtypeStruct((N_DEV, D, GU_SHARD), BF16),
        scratch_shapes=[pltpu.VMEM((D, GU_SHARD), F32)],
        compiler_params=_cparams("parallel", "arbitrary"))(h2, dgu)


def _wgrad_down(act, dx2, tn=512, tt=512):
    t = dx2.shape[0]

    def body(a_ref, b_ref, o_ref, acc):
        _accumulate(acc, o_ref, 2, _dg(a_ref[...], b_ref[...].astype(BF16), 0, 0))

    return pl.pallas_call(
        body, name="wgrad_down", grid=(N_GATE, D // tn, t // tt),
        in_specs=[pl.BlockSpec((None, tt, GU_SHARD), lambda d, j, i: (d, i, 0)),
                  pl.BlockSpec((tt, tn), lambda d, j, i: (i, j))],
        out_specs=pl.BlockSpec((GU_SHARD, tn), lambda d, j, i: (d, j)),
        out_shape=jax.ShapeDtypeStruct((FF, D), BF16),
        scratch_shapes=[pltpu.VMEM((GU_SHARD, tn), F32)],
        compiler_params=_cparams("parallel", "parallel", "arbitrary"))(act, dx2)


def _relayout_w_in(g, tr=256):
    def body(w_ref, o_ref):
        full = jnp.concatenate([w_ref[d] for d in range(N_DEV)], axis=1)
        parts = []
        for a, b, z in _PAD_SEGS:
            parts.append(full[:, a:b])
            if z:
                parts.append(jnp.zeros((tr, z), full.dtype))
        o_ref[...] = jnp.concatenate(parts, axis=1)

    return pl.pallas_call(
        body, name="relayout_w_in", grid=(DEPTH, D // tr),
        in_specs=[pl.BlockSpec((N_DEV, None, tr, IN_SHARD), lambda l, i: (0, l, i, 0))],
        out_specs=pl.BlockSpec((None, tr, P), lambda l, i: (l, i, 0)),
        out_shape=jax.ShapeDtypeStruct((DEPTH, D, P), g.dtype),
        compiler_params=_cparams("parallel", "parallel"))(g)


def _scatter_w_in_grad(gp, tr=256):
    def body(g_ref, o_ref):
        g = g_ref[...]
        pieces, off = [], 0
        for a, b, z in _PAD_SEGS:
            pieces.append((a, g[:, off:off + (b - a)]))
            off += (b - a) + z
        nat = jnp.concatenate([piece for _, piece in sorted(pieces, key=lambda ap: ap[0])], axis=1)
        for d in range(N_DEV):
            o_ref[d] = nat[:, IN_SHARD * d:IN_SHARD * (d + 1)].astype(BF16)

    return pl.pallas_call(
        body, name="scatter_w_in_grad", grid=(D // tr,),
        in_specs=[pl.BlockSpec((tr, P), lambda i: (i, 0))],
        out_specs=pl.BlockSpec((N_DEV, tr, IN_SHARD), lambda i: (0, i, 0)),
        out_shape=jax.ShapeDtypeStruct((N_DEV, D, IN_SHARD), BF16),
        compiler_params=_cparams("parallel"))(gp)


_A_BLK = 3


def _sgu_specs(l):
    return [_resident((DEPTH, G)), _resident((DEPTH, G)), _layer((NH, SGU_C, SGU_C), l), _layer((NH, SGU_C), l)]


def _sgu_fwd(p, ln_w, ln_b, ws, bs, l, tm=256):
    t = p.shape[0]

    def body(uv_ref, lw_ref, lb_ref, ws_ref, bs_ref, y_ref):
        ws_v = [ws_ref[h] for h in range(NH)]
        for c in range(tm // SGU_C):
            rows = pl.ds(c * SGU_C, SGU_C)
            y_ref[rows, :] = _sgu_chunk(uv_ref[rows, :], lw_ref[l:l + 1, :], lb_ref[l:l + 1, :], ws_v,
                                        bs_ref[...]).astype(BF16)

    return pl.pallas_call(
        body, name="sgu_fwd", grid=(t // tm,),
        in_specs=[_row(tm, 2 * G, _A_BLK)] + _sgu_specs(l), out_specs=_row(tm, G),
        out_shape=jax.ShapeDtypeStruct((t, G), BF16),
        compiler_params=_cparams("parallel"))(p, ln_w, ln_b, ws, bs)


def _sgu_bwd(p, dmix, ln_w, ln_b, ws, bs, l, tm=256):
    t = p.shape[0]

    def body(uv_ref, dy_ref, lw_ref, lb_ref, ws_ref, bs_ref, duv_ref, dlw_ref, dlb_ref, dws_ref, dbs_ref):
        @pl.when(pl.program_id(0) == 0)
        def _():
            for r in (dlw_ref, dlb_ref, dws_ref, dbs_ref):
                r[...] = jnp.zeros_like(r)
        ws_v = [ws_ref[h] for h in range(NH)]
        for c in range(tm // SGU_C):
            rows = pl.ds(c * SGU_C, SGU_C)
            _, vjp = jax.vjp(_sgu_chunk, uv_ref[rows, :], lw_ref[l:l + 1, :], lb_ref[l:l + 1, :], ws_v, bs_ref[...])
            duv, dlw, dlb, dws, dbs = vjp(dy_ref[rows, :])
            duv_ref[rows, :] = duv.astype(BF16)
            dlw_ref[...] += dlw
            dlb_ref[...] += dlb
            dbs_ref[...] += dbs
            for h in range(NH):
                dws_ref[h] += dws[h]

    return pl.pallas_call(
        body, name="sgu_bwd", grid=(t // tm,),
        in_specs=[_row(tm, 2 * G, _A_BLK), _row(tm, G, 0)] + _sgu_specs(l),
        out_specs=[_row(tm, 2 * G), _acc((1, G)), _acc((1, G)), _acc((NH, SGU_C, SGU_C)), _acc((NH, SGU_C))],
        out_shape=[jax.ShapeDtypeStruct((t, 2 * G), BF16), jax.ShapeDtypeStruct((1, G), F32),
                   jax.ShapeDtypeStruct((1, G), F32), jax.ShapeDtypeStruct((NH, SGU_C, SGU_C), F32),
                   jax.ShapeDtypeStruct((NH, SGU_C), F32)],
        compiler_params=_cparams("arbitrary"))(p, dmix, ln_w, ln_b, ws, bs)


HALO = 8


def _prev_halo(tm, width, col):
    return pl.BlockSpec((HALO, width), lambda i: (jnp.maximum(i * (tm // HALO) - 1, 0), col))


def _next_halo(tm, width, col, t):
    return pl.BlockSpec((HALO, width), lambda i: (jnp.minimum((i + 1) * (tm // HALO), t // HALO - 1), col))


def _with_prev(halo_ref, x_ref):
    halo = jnp.where(pl.program_id(0) == 0, 0.0, halo_ref[...])
    return jnp.concatenate([halo, x_ref[...]], axis=0)


def _with_next(x_ref, halo_ref):
    halo = jnp.where(pl.program_id(0) == pl.num_programs(0) - 1, 0.0, halo_ref[...])
    return jnp.concatenate([x_ref[...], halo], axis=0)


def _delay(ext, j):
    return ext if j == 0 else pltpu.roll(ext, j, axis=0)


def _advance(ext, j):
    return ext if j == 0 else pltpu.roll(ext, ext.shape[0] - j, axis=0)


def _causal_conv(ext, w, tm):
    kw = w.shape[0]
    out = None
    for k in range(kw):
        term = _delay(ext, kw - 1 - k)[HALO:HALO + tm] * w[k:k + 1, :]
        out = term if out is None else out + term
    return out


def _causal_conv_t(ext, w, tm):
    kw = w.shape[0]
    out = None
    for k in range(kw):
        term = _advance(ext, kw - 1 - k)[0:tm] * w[k:k + 1, :]
        out = term if out is None else out + term
    return out


def _conv_wgrad(ext, dy, kw, tm):
    return jnp.concatenate(
        [jnp.sum(_delay(ext, kw - 1 - k)[HALO:HALO + tm] * dy, axis=0, keepdims=True) for k in range(kw)], axis=0)


_B_GB, _B_GC, _B_H = 8, 9, 10
_C_QKV, _D_QKV = 0, 1
_C_Z, _D_Z = 11, 12
_C_AB, _D_GLR = 26, 27


def _sconv_fwd(p, w, l, tm=512):
    t = p.shape[0]

    def body(gb_ref, gc_ref, h_ref, gc_halo, h_halo, w_ref, y_ref):
        s_ext = _with_prev(gc_halo, gc_ref) * _with_prev(h_halo, h_ref)
        y_ref[...] = (gb_ref[...] * _causal_conv(s_ext, w_ref[...], tm)).astype(BF16)

    return pl.pallas_call(
        body, name="sconv_fwd", grid=(t // tm,),
        in_specs=[_row(tm, G, _B_GB), _row(tm, G, _B_GC), _row(tm, G, _B_H), _prev_halo(tm, G, _B_GC),
                  _prev_halo(tm, G, _B_H), _layer((3, G), l)],
        out_specs=_row(tm, G), out_shape=jax.ShapeDtypeStruct((t, G), BF16),
        compiler_params=_cparams("parallel"))(p, p, p, p, p, w)


def _sconv_bwd(p, dmix, w, l, tm=512):
    t = p.shape[0]

    def body(gb_ref, gc_ref, h_ref, gc_halo, h_halo, gb_next, dy_ref, dy_next, w_ref, dp_ref, dw_ref):
        @pl.when(pl.program_id(0) == 0)
        def _():
            dw_ref[...] = jnp.zeros_like(dw_ref)
        w_v = w_ref[...]
        s_ext = _with_prev(gc_halo, gc_ref) * _with_prev(h_halo, h_ref)
        dout = dy_ref[...]
        d_gb = dout * _causal_conv(s_ext, w_v, tm)
        dconv_ext = _with_next(dy_ref, dy_next) * _with_next(gb_ref, gb_next)
        ds = _causal_conv_t(dconv_ext, w_v, tm)
        dw_ref[...] += _conv_wgrad(s_ext, dconv_ext[0:tm], 3, tm)
        dp_ref[...] = jnp.concatenate([d_gb, ds * h_ref[...], ds * gc_ref[...]], axis=1).astype(BF16)

    return pl.pallas_call(
        body, name="sconv_bwd", grid=(t // tm,),
        in_specs=[_row(tm, G, _B_GB), _row(tm, G, _B_GC), _row(tm, G, _B_H), _prev_halo(tm, G, _B_GC),
                  _prev_halo(tm, G, _B_H), _next_halo(tm, G, _B_GB, t), _row(tm, G, 1), _next_halo(tm, G, 1, t),
                  _layer((3, G), l)],
        out_specs=[_row(tm, 3 * G), _acc((3, G))],
        out_shape=[jax.ShapeDtypeStruct((t, 3 * G), BF16), jax.ShapeDtypeStruct((3, G), F32)],
        compiler_params=_cparams("arbitrary"))(p, p, p, p, p, p, dmix, dmix, w)


def _qkv_conv_fwd(p, w, l, tm=512):
    t = p.shape[0]

    def body(x_ref, x_halo, w_ref, y_ref):
        c = _causal_conv(_with_prev(x_halo, x_ref), w_ref[...], tm)
        y_ref[...] = c * _sigmoid(c)

    return pl.pallas_call(
        body, name="qkv_conv_fwd", grid=(t // tm,),
        in_specs=[_row(tm, 3 * G, _C_QKV), _prev_halo(tm, 3 * G, _C_QKV), _layer((4, 3 * G), l)],
        out_specs=_row(tm, 3 * G), out_shape=jax.ShapeDtypeStruct((t, 3 * G), F32),
        compiler_params=_cparams("parallel"))(p, p, w)


def _qkv_conv_bwd(p, dy, w, l, tm=512):
    t = p.shape[0]

    def body(x_ref, x_halo, x_next, dy_ref, dy_next, w_ref, dx_ref, dw_ref):
        @pl.when(pl.program_id(0) == 0)
        def _():
            dw_ref[...] = jnp.zeros_like(dw_ref)
        w_v = w_ref[...]
        x_all = jnp.concatenate([_with_prev(x_halo, x_ref), jnp.where(
            pl.program_id(0) == pl.num_programs(0) - 1, 0.0, x_next[...])], axis=0)
        c = _causal_conv(x_all, w_v, tm + HALO)
        s = _sigmoid(c)
        dc_ext = _with_next(dy_ref, dy_next) * (s * (1.0 + c * (1.0 - s)))
        dx_ref[...] = _causal_conv_t(dc_ext, w_v, tm).astype(BF16)
        dw_ref[...] += _conv_wgrad(x_all[0:HALO + tm], dc_ext[0:tm], 4, tm)

    return pl.pallas_call(
        body, name="qkv_conv_bwd", grid=(t // tm,),
        in_specs=[_row(tm, 3 * G, _C_QKV), _prev_halo(tm, 3 * G, _C_QKV), _next_halo(tm, 3 * G, _C_QKV, t),
                  _row(tm, 3 * G), _next_halo(tm, 3 * G, 0, t), _layer((4, 3 * G), l)],
        out_specs=[_row(tm, 3 * G), _acc((4, 3 * G))],
        out_shape=[jax.ShapeDtypeStruct((t, 3 * G), BF16), jax.ShapeDtypeStruct((4, 3 * G), F32)],
        compiler_params=_cparams("arbitrary"))(p, p, p, dy, dy, w)


_STATE = pl.BlockSpec((1, NH, HD, HD), lambda i: (i, 0, 0, 0))


def _dn_specs():
    return [_resident((DEPTH, NH)), _resident((DEPTH, NH)), _resident((DEPTH, HD))]


def _dn_fwd(qkv, p, params, l, cps=4):
    t = qkv.shape[0]
    tm = DN_C * cps
    nb = cps * NH

    def body(qkv_ref, z_ref, ab_ref, alog_ref, dt_ref, nw_ref, y_ref, st_ref, inv_ref, state):
        @pl.when(pl.program_id(0) == 0)
        def _():
            state[...] = jnp.zeros_like(state)
        st_ref[0] = state[...]
        y, new, inv = _dn_tile(qkv_ref[:, 0:G], qkv_ref[:, G:2 * G], qkv_ref[:, 2 * G:3 * G], ab_ref[...], z_ref[...],
                               state[...], None, alog_ref[l:l + 1, :], dt_ref[l:l + 1, :], nw_ref[l:l + 1, :])
        y_ref[...] = y.astype(BF16)
        inv_ref[...] = inv
        state[...] = new

    return pl.pallas_call(
        body, name="dn_fwd", grid=(t // tm,),
        in_specs=[_row(tm, 3 * G), _row(tm, G, _C_Z), _row(tm, LANES, _C_AB)] + _dn_specs(),
        out_specs=[_row(tm, G), _STATE, pl.BlockSpec((nb, DN_C, DN_C), lambda i: (i, 0, 0))],
        out_shape=[jax.ShapeDtypeStruct((t, G), BF16), jax.ShapeDtypeStruct((t // tm, NH, HD, HD), F32),
                   jax.ShapeDtypeStruct((t // DN_C * NH, DN_C, DN_C), F32)],
        scratch_shapes=[pltpu.VMEM((NH, HD, HD), F32)],
        compiler_params=_cparams("arbitrary"))(qkv, p, p, *params)


def _dn_bwd(qkv, p, states, inv, dmix, params, l, cps=4):
    t = qkv.shape[0]
    tm = DN_C * cps
    n = t // tm
    nb = cps * NH

    def body(qkv_ref, z_ref, ab_ref, st_ref, inv_ref, dy_ref, alog_ref, dt_ref, nw_ref,
             dqkv_ref, dz_ref, dab_ref, dalog_ref, ddt_ref, dnw_ref, dstate):
        dprm_refs = (dalog_ref, ddt_ref, dnw_ref)

        @pl.when(pl.program_id(0) == 0)
        def _():
            dstate[...] = jnp.zeros_like(dstate)
            for r in dprm_refs:
                r[...] = jnp.zeros_like(r)
        inv_v = inv_ref[...]
        tile = lambda q, k, v, ab, z, st, alog, dt, nw: _dn_tile(q, k, v, ab, z, st, inv_v, alog, dt, nw)[:2]
        _, vjp = jax.vjp(tile, qkv_ref[:, 0:G], qkv_ref[:, G:2 * G], qkv_ref[:, 2 * G:3 * G], ab_ref[...], z_ref[...],
                         st_ref[0], alog_ref[l:l + 1, :], dt_ref[l:l + 1, :], nw_ref[l:l + 1, :])
        dq, dk, dv, dab, dz, dst, *dprm = vjp((dy_ref[...], dstate[...]))
        dqkv_ref[...] = jnp.concatenate([dq, dk, dv], axis=1)
        dz_ref[...] = dz.astype(BF16)
        dab_ref[...] = dab.astype(BF16)
        dstate[...] = dst
        for r, g in zip(dprm_refs, dprm):
            r[...] += g

    rev = lambda w, blk: pl.BlockSpec((tm, w), lambda i: (n - 1 - i, blk))
    return pl.pallas_call(
        body, name="dn_bwd", grid=(n,),
        in_specs=[rev(3 * G, 0), rev(G, _C_Z), rev(LANES, _C_AB),
                  pl.BlockSpec((1, NH, HD, HD), lambda i: (n - 1 - i, 0, 0, 0)),
                  pl.BlockSpec((nb, DN_C, DN_C), lambda i: (n - 1 - i, 0, 0)), rev(G, 2)] + _dn_specs(),
        out_specs=[rev(3 * G, 0), rev(G, 0), rev(LANES, 0), _acc((1, NH)), _acc((1, NH)), _acc((1, HD))],
        out_shape=[jax.ShapeDtypeStruct((t, 3 * G), F32), jax.ShapeDtypeStruct((t, G), BF16),
                   jax.ShapeDtypeStruct((t, LANES), BF16), jax.ShapeDtypeStruct((1, NH), F32),
                   jax.ShapeDtypeStruct((1, NH), F32), jax.ShapeDtypeStruct((1, HD), F32)],
        scratch_shapes=[pltpu.VMEM((NH, HD, HD), F32)],
        compiler_params=_cparams("arbitrary"))(qkv, p, p, states, inv, dmix, *params)


def _gla_specs(l):
    return [_layer((16, G), l), _resident((DEPTH, G)), _resident((DEPTH, HD))]


def _gla_fwd(p, params, l, cps=4):
    t = p.shape[0]
    tm = GLA_C * cps

    def body(qkv_ref, z_ref, glr_ref, w2_ref, gb_ref, nw_ref, y_ref, st_ref, state):
        @pl.when(pl.program_id(0) == 0)
        def _():
            state[...] = jnp.zeros_like(state)
        st_ref[0] = state[...]
        y, new = _gla_tile(qkv_ref[:, 0:G], qkv_ref[:, G:2 * G], qkv_ref[:, 2 * G:3 * G], glr_ref[...], z_ref[...],
                           state[...], w2_ref[...], gb_ref[l:l + 1, :], nw_ref[l:l + 1, :])
        y_ref[...] = y.astype(BF16)
        state[...] = new

    return pl.pallas_call(
        body, name="gla_fwd", grid=(t // tm,),
        in_specs=[_row(tm, 3 * G, _D_QKV), _row(tm, G, _D_Z), _row(tm, LANES, _D_GLR)] + _gla_specs(l),
        out_specs=[_row(tm, G), _STATE],
        out_shape=[jax.ShapeDtypeStruct((t, G), BF16), jax.ShapeDtypeStruct((t // tm, NH, HD, HD), F32)],
        scratch_shapes=[pltpu.VMEM((NH, HD, HD), F32)],
        compiler_params=_cparams("arbitrary"))(p, p, p, *params)


def _gla_bwd(p, states, dmix, params, l, cps=4):
    t = p.shape[0]
    tm = GLA_C * cps
    n = t // tm

    def body(qkv_ref, z_ref, glr_ref, st_ref, dy_ref, w2_ref, gb_ref, nw_ref,
             dqkv_ref, dz_ref, dglr_ref, dw2_ref, dgb_ref, dnw_ref, dstate):
        dprm_refs = (dw2_ref, dgb_ref, dnw_ref)

        @pl.when(pl.program_id(0) == 0)
        def _():
            dstate[...] = jnp.zeros_like(dstate)
            for r in dprm_refs:
                r[...] = jnp.zeros_like(r)
        _, vjp = jax.vjp(_gla_tile, qkv_ref[:, 0:G], qkv_ref[:, G:2 * G], qkv_ref[:, 2 * G:3 * G], glr_ref[...],
                         z_ref[...], st_ref[0], w2_ref[...], gb_ref[l:l + 1, :], nw_ref[l:l + 1, :])
        dq, dk, dv, dglr, dz, dst, *dprm = vjp((dy_ref[...], dstate[...]))
        dqkv_ref[...] = jnp.concatenate([dq, dk, dv], axis=1).astype(BF16)
        dz_ref[...] = dz.astype(BF16)
        dglr_ref[...] = dglr.astype(BF16)
        dstate[...] = dst
        for r, g in zip(dprm_refs, dprm):
            r[...] += g

    rev = lambda w, blk: pl.BlockSpec((tm, w), lambda i: (n - 1 - i, blk))
    return pl.pallas_call(
        body, name="gla_bwd", grid=(n,),
        in_specs=[rev(3 * G, _D_QKV), rev(G, _D_Z), rev(LANES, _D_GLR),
                  pl.BlockSpec((1, NH, HD, HD), lambda i: (n - 1 - i, 0, 0, 0)), rev(G, 3)] + _gla_specs(l),
        out_specs=[rev(3 * G, 0), rev(G, 0), rev(LANES, 0), _acc((16, G)), _acc((1, G)), _acc((1, HD))],
        out_shape=[jax.ShapeDtypeStruct((t, 3 * G), BF16), jax.ShapeDtypeStruct((t, G), BF16),
                   jax.ShapeDtypeStruct((t, LANES), BF16), jax.ShapeDtypeStruct((16, G), F32),
                   jax.ShapeDtypeStruct((1, G), F32), jax.ShapeDtypeStruct((1, HD), F32)],
        scratch_shapes=[pltpu.VMEM((NH, HD, HD), F32)],
        compiler_params=_cparams("arbitrary"))(p, p, p, states, dmix, *params)


def _adamw_math(w, g, m, v):
    m = ADAM_B1 * m + (1.0 - ADAM_B1) * g
    v = ADAM_B2 * v + (1.0 - ADAM_B2) * (g * g)
    m_hat = m / (1.0 - ADAM_B1 ** ADAM_STEP)
    v_hat = v / (1.0 - ADAM_B2 ** ADAM_STEP)
    return -ADAM_LR * (m_hat / (jnp.sqrt(v_hat) + ADAM_EPS) + ADAM_WD * w), m, v


def _adamw_large(parts, w, m, v, name, tr):
    _, r, c = w.shape

    def body(p_ref, w_ref, m_ref, v_ref, g_ref, d_ref, nm_ref, nv_ref):
        g = p_ref[0].astype(F32)
        for k in range(1, N_DEV):
            g = g + p_ref[k].astype(F32)
        g_ref[...] = g
        d_ref[...], nm_ref[...], nv_ref[...] = _adamw_math(w_ref[...], g, m_ref[...], v_ref[...])

    blk = pl.BlockSpec((None, tr, c), lambda l, i: (l, i, 0))
    return pl.pallas_call(
        body, name=name, grid=(DEPTH, r // tr),
        in_specs=[pl.BlockSpec((None, N_DEV, tr, c), lambda l, i: (l, 0, i, 0)), blk, blk, blk],
        out_specs=[blk] * 4, out_shape=[jax.ShapeDtypeStruct(w.shape, F32)] * 4,
        compiler_params=_cparams("parallel", "parallel"))(parts, w, m, v)


_SLAB_AT = {
    "sgu_w_spatial": (0, 0, SGU_C, LANES),
    "sgu_b_spatial": (128, 0, NH, SGU_C),
    "norm1_w": (132, 0, 1, D),
    "norm2_w": (133, 0, 1, D),
    "sgu_ln_w": (134, 0, 1, G),
    "sgu_ln_b": (134, 256, 1, G),
    "gla_gate_bias": (134, 512, 1, G),
    "dn_norm_w": (134, 768, 1, HD),
    "gla_norm_w": (134, 896, 1, HD),
    "dn_a_log": (135, 0, 1, NH),
    "dn_dt_bias": (135, 128, 1, NH),
    "gla_w_gate2": (136, 0, 16, G),
    "dn_conv_w": (136, 256, 4, 3 * G),
    "sc_conv_w": (140, 256, 3, G),
}
_LAYER_ROWS = 152
_FINAL_ROW = DEPTH * _LAYER_ROWS
_SLAB_ROWS, _SLAB_COLS = _FINAL_ROW + 8, 1024
_SLAB_ORDER = tuple(_SLAB_AT)
_SHARDED_SMALL = ("sc_conv_w", "dn_conv_w", "gla_w_gate2")
_REPLICATED = tuple(k for k in _SLAB_ORDER if k not in _SHARDED_SMALL)


def _region(name, l, h=0):
    r0, c0, nr, nc = _SLAB_AT[name]
    return slice(_LAYER_ROWS * l + r0, _LAYER_ROWS * l + r0 + nr), slice(c0 + nc * h, c0 + nc * (h + 1))


def _pack_small(layer_grads, d_final):
    def body(*refs):
        slab = refs[-1]
        slab[...] = jnp.zeros_like(slab)
        it = iter(refs[:-1])
        for l in range(DEPTH):
            for name in _SLAB_ORDER:
                ref = next(it)
                if name == "sgu_w_spatial":
                    for h in range(NH):
                        slab[_region(name, l, h)] = ref[h]
                else:
                    slab[_region(name, l)] = ref[...]
        slab[_FINAL_ROW:_FINAL_ROW + 1, :] = next(it)[...]

    flat = [layer_grads[l][name] for l in range(DEPTH) for name in _SLAB_ORDER] + [d_final]
    return pl.pallas_call(body, name="pack_small_grads", out_shape=jax.ShapeDtypeStruct((_SLAB_ROWS, _SLAB_COLS), F32),
                          compiler_params=_cparams())(*flat)


def _adamw_small(parts, w, m, v):
    names = _REPLICATED + ("final_norm_w",)
    n_in = len(names)

    def body(*refs):
        def total(rows, cols):
            g = refs[0][0, rows, cols]
            for k in range(1, N_DEV):
                g = g + refs[0][k, rows, cols]
            return g

        w_refs = dict(zip(names, refs[1:1 + n_in]))
        m_refs = dict(zip(names, refs[1 + n_in:1 + 2 * n_in]))
        v_refs = dict(zip(names, refs[1 + 2 * n_in:1 + 3 * n_in]))
        outs = refs[1 + 3 * n_in:]
        out_refs = [dict(zip(names, outs[j * n_in:(j + 1) * n_in])) for j in range(4)]
        full_refs = dict(zip(_SHARDED_SMALL, outs[4 * n_in:]))

        def update(name, idx, g):
            res = (g,) + _adamw_math(w_refs[name][idx], g, m_refs[name][idx], v_refs[name][idx])
            for o, val in zip(out_refs, res):
                o[name][idx] = val

        for l in range(DEPTH):
            for name in _REPLICATED:
                if name == "sgu_w_spatial":
                    for h in range(NH):
                        update(name, (l, h), total(*_region(name, l, h)))
                elif name == "sgu_b_spatial":
                    update(name, (l,), total(*_region(name, l)))
                else:
                    update(name, (slice(l, l + 1), slice(None)), total(*_region(name, l)))
            for name in _SHARDED_SMALL:
                full_refs[name][l] = total(*_region(name, l))
        update("final_norm_w", (slice(None), slice(None)), total(slice(_FINAL_ROW, _FINAL_ROW + 1), slice(None)))

    shapes = [jax.ShapeDtypeStruct(w[k].shape, F32) for k in names]
    full_shapes = [jax.ShapeDtypeStruct((DEPTH,) + _SLAB_AT[k][2:], F32) for k in _SHARDED_SMALL]
    outs = pl.pallas_call(body, name="adamw_small", out_shape=shapes * 4 + full_shapes, compiler_params=_cparams())(
        parts, *[w[k] for k in names], *[m[k] for k in names], *[v[k] for k in names])
    result = [dict(zip(names, outs[j * n_in:(j + 1) * n_in])) for j in range(4)]
    return result, dict(zip(_SHARDED_SMALL, outs[4 * n_in:]))


def _adamw_shards(g, w, m, v):
    names = _SHARDED_SMALL
    n = len(names)

    def body(*refs):
        for j in range(n):
            g_v = refs[j][...]
            res = _adamw_math(refs[n + j][...], g_v, refs[2 * n + j][...], refs[3 * n + j][...])
            for o, val in zip(refs[4 * n + j::n], res):
                o[...] = val

    shapes = [jax.ShapeDtypeStruct(w[k].shape, F32) for k in names]
    outs = pl.pallas_call(body, name="adamw_small_shards", out_shape=shapes * 3, compiler_params=_cparams())(
        *[g[k] for k in names], *[w[k] for k in names], *[m[k] for k in names], *[v[k] for k in names])
    return [dict(zip(names, outs[j * n:(j + 1) * n])) for j in range(3)]


_ANY = pl.BlockSpec(memory_space=pl.ANY)


def _place():
    return lax.axis_index("x"), lax.axis_index("y"), lax.axis_index("c")


def _all_gather(blocks, second, name):
    n = len(blocks)

    def body(*refs):
        x_refs, out_refs, (send_sems, recv_sems, local_sems) = refs[:n], refs[n:2 * n], refs[2 * n:]
        x, y, c = _place()
        me, sibling = (x, y, c), (x, y, 1 - c)
        chips = [(1 - x, y), (x, 1 - y), (1 - x, 1 - y)]

        def slot(j, px, py, pc):
            idx = 4 * px + 2 * py + pc
            return out_refs[j].at[:, idx] if second[j] else out_refs[j].at[idx]

        def copies(k, block, to, own=False):
            return [pltpu.make_async_remote_copy(
                src_ref=x_refs[j] if own else slot(j, *block), dst_ref=slot(j, *block),
                send_sem=send_sems.at[j, k], recv_sem=recv_sems.at[j, k], device_id=to,
                device_id_type=pl.DeviceIdType.MESH) for j in range(n)]

        mine = [pltpu.make_async_copy(x_refs[j], slot(j, *me), local_sems.at[j]) for j in range(n)]
        for cp in mine:
            cp.start()
        first = copies(0, me, sibling, own=True)
        for j, chip in enumerate(chips):
            first += copies(1 + j, me, (*chip, c), own=True)
        for cp in first:
            cp.start()
        passed = []
        for j, chip in enumerate(chips):
            for cp in copies(1 + j, (*chip, c), me):
                cp.wait_recv()
            onward = copies(4 + j, (*chip, c), sibling)
            for cp in onward:
                cp.start()
            passed += onward
        for cp in copies(0, sibling, me):
            cp.wait_recv()
        for j, chip in enumerate(chips):
            for cp in copies(4 + j, (*chip, 1 - c), me):
                cp.wait_recv()
        for cp in first + passed:
            cp.wait_send()
        for cp in mine:
            cp.wait()

    def out_shape(a, sec):
        shape = (a.shape[0], N_DEV) + a.shape[1:] if sec else (N_DEV,) + a.shape
        return jax.ShapeDtypeStruct(shape, a.dtype)

    return pl.pallas_call(
        body, name=name, out_shape=[out_shape(a, s) for a, s in zip(blocks, second)],
        in_specs=[_ANY] * n, out_specs=[_ANY] * n,
        scratch_shapes=[pltpu.SemaphoreType.DMA((n, 7)), pltpu.SemaphoreType.DMA((n, 7)),
                        pltpu.SemaphoreType.DMA((n,))])(*blocks)


def _exchange(sends, name):
    flat = [(wi, l, a) for wi, per_layer in enumerate(sends) for l, a in enumerate(per_layer)]
    n, nw = len(flat), len(sends)

    def body(*refs):
        x_refs, out_refs, (send_sems, recv_sems, local_sems) = refs[:n], refs[n:n + nw], refs[n + nw:]
        x, y, c = _place()
        me = 4 * x + 2 * y + c

        def peer(k):
            return (x ^ ((k >> 2) & 1), y ^ ((k >> 1) & 1), c ^ (k & 1))

        def copies(k, landing):
            px, py, pc = peer(k)
            them = 4 * px + 2 * py + pc
            return [pltpu.make_async_remote_copy(
                src_ref=x_refs[j].at[me if landing else them], dst_ref=out_refs[wi].at[l, them if landing else me],
                send_sem=send_sems.at[j, k - 1], recv_sem=recv_sems.at[j, k - 1], device_id=(px, py, pc),
                device_id_type=pl.DeviceIdType.MESH) for j, (wi, l, _) in enumerate(flat)]

        mine = [pltpu.make_async_copy(x_refs[j].at[me], out_refs[wi].at[l, me], local_sems.at[j])
                for j, (wi, l, _) in enumerate(flat)]
        for cp in mine:
            cp.start()
        started = []
        for k in range(1, N_DEV):
            started += copies(k, False)
        for cp in started:
            cp.start()
        for k in range(1, N_DEV):
            for cp in copies(k, True):
                cp.wait_recv()
        for cp in started:
            cp.wait_send()
        for cp in mine:
            cp.wait()

    out_shape = [jax.ShapeDtypeStruct((DEPTH,) + per_layer[0].shape, per_layer[0].dtype) for per_layer in sends]
    return pl.pallas_call(
        body, name=name, out_shape=out_shape, in_specs=[_ANY] * n, out_specs=[_ANY] * nw,
        scratch_shapes=[pltpu.SemaphoreType.DMA((n, 7)), pltpu.SemaphoreType.DMA((n, 7)),
                        pltpu.SemaphoreType.DMA((n,))])(*[a for _, _, a in flat])


def _local_grads(x, target, w):
    dn_params = (w["dn_a_log"], w["dn_dt_bias"], w["dn_norm_w"])
    gla_params = (w["gla_w_gate2"], w["gla_gate_bias"], w["gla_norm_w"])
    sgu_params = (w["sgu_ln_w"], w["sgu_ln_b"], w["sgu_w_spatial"], w["sgu_b_spatial"])
    saved = []
    for l in range(DEPTH):
        h, p = _in_proj(x, w["norm1_w"], w["w_in"], l)
        ya = _sgu_fwd(p, *sgu_params, l)
        yb = _sconv_fwd(p, w["sc_conv_w"], l)
        qkv_c = _qkv_conv_fwd(p, w["dn_conv_w"], l)
        yc, st_c, inv_c = _dn_fwd(qkv_c, p, dn_params, l)
        yd, st_d = _gla_fwd(p, gla_params, l)
        mix, x1, h2, gu, act, x2 = _out_mlp(x, (ya, yb, yc, yd), w["w_out"], w["norm2_w"], w["w_gate_up"], w["w_down"], l)
        saved.append(dict(x=x, h=h, p=p, qkv_c=qkv_c, st_c=st_c, inv_c=inv_c, st_d=st_d, mix=mix, x1=x1, h2=h2, gu=gu,
                          act=act))
        x = x2
    loss, d_final, dx = _loss_head(x, w["final_norm_w"], target)
    large = {k: [None] * DEPTH for k in ("w_in", "w_out", "w_gate_up", "w_down")}
    small = [None] * DEPTH
    for l in reversed(range(DEPTH)):
        s = saved[l]
        p = s["p"]
        g = {}
        dgu, dx1, dmix, g["norm2_w"] = _mlp_out_bwd(dx, s["x1"], s["gu"], w["norm2_w"], w["w_down"], w["w_gate_up"],
                                                    w["w_out"], l)
        large["w_gate_up"][l] = _wgrad_gate_up(s["h2"], dgu)
        large["w_down"][l] = _wgrad_down(s["act"], dx).reshape(N_DEV, FF // N_DEV, D)
        large["w_out"][l] = _matmul_tn(s["mix"], dx1, "wgrad_out", BF16).reshape(N_DEV, D // N_DEV, D)
        d_a, g["sgu_ln_w"], g["sgu_ln_b"], g["sgu_w_spatial"], g["sgu_b_spatial"] = _sgu_bwd(p, dmix, *sgu_params, l)
        d_b, g["sc_conv_w"] = _sconv_bwd(p, dmix, w["sc_conv_w"], l)
        dqkv_c, dz_c, dab, g["dn_a_log"], g["dn_dt_bias"], g["dn_norm_w"] = _dn_bwd(
            s["qkv_c"], p, s["st_c"], s["inv_c"], dmix, dn_params, l)
        d_c, g["dn_conv_w"] = _qkv_conv_bwd(p, dqkv_c, w["dn_conv_w"], l)
        d_d, dz_d, dglr, g["gla_w_gate2"], g["gla_gate_bias"], g["gla_norm_w"] = _gla_bwd(p, s["st_d"], dmix, gla_params, l)
        dp, dx, g["norm1_w"] = _in_proj_bwd((d_c, d_d, d_a, d_b, dz_c, dz_d, dab, dglr), s["x"], dx1, w["norm1_w"],
                                            w["w_in"], l)
        large["w_in"][l] = _scatter_w_in_grad(_matmul_tn(s["h"], dp, "wgrad_in", F32))
        small[l] = g
    return loss[0, 0], dx, large, small, d_final


_ORDER = ("norm1_w", "w_in", "sgu_ln_w", "sgu_ln_b", "sgu_w_spatial", "sgu_b_spatial", "sc_conv_w", "dn_conv_w",
          "dn_a_log", "dn_dt_bias", "dn_norm_w", "gla_w_gate2", "gla_gate_bias", "gla_norm_w", "w_out", "norm2_w",
          "w_gate_up", "w_down", "final_norm_w")
_LARGE = ("w_in", "w_gate_up", "w_out", "w_down")
_LARGE_TILE = {"w_in": 256, "w_gate_up": 256, "w_out": 128, "w_down": 352}


def _gather_cols(g):
    return jnp.transpose(g, (1, 2, 0, 3)).reshape(g.shape[1], g.shape[2], N_DEV * g.shape[3])


def kernel(x, norm1_w, w_in, sgu_ln_w, sgu_ln_b, sgu_w_spatial, sgu_b_spatial, sc_conv_w, dn_conv_w, dn_a_log, dn_dt_bias, dn_norm_w, gla_w_gate2, gla_gate_bias, gla_norm_w, w_out, norm2_w, w_gate_up, w_down, final_norm_w, loss_target, m_norm1_w, m_w_in, m_sgu_ln_w, m_sgu_ln_b, m_sgu_w_spatial, m_sgu_b_spatial, m_sc_conv_w, m_dn_conv_w, m_dn_a_log, m_dn_dt_bias, m_dn_norm_w, m_gla_w_gate2, m_gla_gate_bias, m_gla_norm_w, m_w_out, m_norm2_w, m_w_gate_up, m_w_down, m_final_norm_w, v_norm1_w, v_w_in, v_sgu_ln_w, v_sgu_ln_b, v_sgu_w_spatial, v_sgu_b_spatial, v_sc_conv_w, v_dn_conv_w, v_dn_a_log, v_dn_dt_bias, v_dn_norm_w, v_gla_w_gate2, v_gla_gate_bias, v_gla_norm_w, v_w_out, v_norm2_w, v_w_gate_up, v_w_down, v_final_norm_w):
    args = dict(locals())
    wts = {k: args[k] for k in _ORDER}
    mom = {k: args["m_" + k] for k in _ORDER}
    var = {k: args["v_" + k] for k in _ORDER}
    for d in (wts, mom, var):
        d["final_norm_w"] = d["final_norm_w"][None]
    me = 4 * lax.axis_index("x") + 2 * lax.axis_index("y") + lax.axis_index("c")

    blocks = [wts[k].astype(BF16) for k in _LARGE] + [wts[k] for k in _SHARDED_SMALL]
    got = _all_gather(blocks, [False, False, True, True, False, False, False], "gather_weights")
    full = dict(wts)
    full["w_in"] = _relayout_w_in(got[0])
    full["w_gate_up"] = got[1]
    full["w_out"] = got[2].reshape(DEPTH, D, D)
    full["w_down"] = got[3].reshape(DEPTH, FF, D)
    for k, g in zip(_SHARDED_SMALL, got[4:]):
        full[k] = _gather_cols(g)

    loss, dx, large, small, d_final = _local_grads(x[0], loss_target[0], full)
    loss = lax.psum(loss, ("x", "y", "c"))

    parts = _exchange([large[k] for k in _LARGE], "exchange_grads")
    result = {}
    for k, part in zip(_LARGE, parts):
        outs = _adamw_large(part, wts[k], mom[k], var[k], "adamw_" + k, _LARGE_TILE[k])
        for kind, a in zip(("grad", "delta", "new_m", "new_v"), outs):
            result[kind, k] = a

    slabs = _all_gather([_pack_small(small, d_final)], [False], "gather_small_grads")[0]
    rep = _REPLICATED + ("final_norm_w",)
    outs, summed = _adamw_small(slabs, {k: wts[k] for k in rep}, {k: mom[k] for k in rep}, {k: var[k] for k in rep})
    for kind, d in zip(("grad", "delta", "new_m", "new_v"), outs):
        for k, a in d.items():
            result[kind, k] = a[0] if k == "final_norm_w" else a
    own = {k: lax.dynamic_slice_in_dim(summed[k], me * wts[k].shape[-1], wts[k].shape[-1], axis=2) for k in _SHARDED_SMALL}
    outs = _adamw_shards(own, {k: wts[k] for k in _SHARDED_SMALL}, {k: mom[k] for k in _SHARDED_SMALL},
                         {k: var[k] for k in _SHARDED_SMALL})
    for kind, d in zip(("grad", "delta", "new_m", "new_v"), [own] + outs):
        for k, a in d.items():
            result[kind, k] = a

    return (loss, dx[None], *[result[kind, k] for kind in ("grad", "delta", "new_m", "new_v") for k in _ORDER])
```

```python
import functools

import jax
import jax.numpy as jnp
from jax import lax
from jax.experimental import pallas as pl
from jax.experimental.pallas import tpu as pltpu

F32, BF16 = jnp.float32, jnp.bfloat16
DEPTH = 2
D = 1024
G = 256
NH, HD = 4, 64
FF = 2816
IN_COLS = 3352
P = 3584
EPS = 1e-6
SGU_C, DN_C, GLA_C = 128, 64, 64
N_DEV = 8
IN_SHARD = IN_COLS // N_DEV
GU_SHARD = 2 * FF // N_DEV
N_GATE = N_DEV // 2
LANES = 128
VMEM_LIMIT = 56 * 2 ** 20

_PAD_SEGS = ((1280, 2048, 0),
             (2312, 3080, 0),
             (0, 512, 0),
             (512, 1280, 0),
             (2056, 2312, 0),
             (3096, 3352, 0),
             (2048, 2056, 120),
             (3080, 3096, 112))

ADAM_LR, ADAM_B1, ADAM_B2, ADAM_EPS, ADAM_WD, ADAM_STEP = 0.001, 0.9, 0.999, 1e-08, 0.01, 10


def _cparams(*sem):
    return pltpu.CompilerParams(dimension_semantics=sem, vmem_limit_bytes=VMEM_LIMIT)


def _resident(shape):
    return pl.BlockSpec(shape, lambda *_: (0,) * len(shape), pipeline_mode=pl.Buffered(1))


def _layer(shape, l):
    return pl.BlockSpec((None,) + tuple(shape), lambda *_: (l,) + (0,) * len(shape), pipeline_mode=pl.Buffered(1))


def _acc(shape):
    return pl.BlockSpec(shape, lambda *_: (0,) * len(shape))


def _dg(a, b, ca, cb, precision=None):
    lead = a.ndim - 2
    batch = ((0,), (0,)) if lead else ((), ())
    return lax.dot_general(a, b, (((ca + lead,), (cb + lead,)), batch), preferred_element_type=F32, precision=precision)


def _make_dot(cast, precision):
    def raw(a, b, form):
        ca = 0 if form == "tn" else 1
        cb = 1 if form == "nt" else 0
        return _dg_any(cast(a), cast(b), ca, cb, precision)

    @functools.partial(jax.custom_vjp, nondiff_argnums=(2,))
    def dot(a, b, form):
        return raw(a, b, form)

    def fwd(a, b, form):
        return raw(a, b, form), (a, b)

    def bwd(form, res, g):
        a, b = res
        if form == "nn":
            return raw(g, b, "nt"), raw(a, g, "tn")
        if form == "nt":
            return raw(g, b, "nn"), raw(g, a, "tn")
        return raw(b, g, "nt"), raw(a, g, "nn")

    dot.defvjp(fwd, bwd)
    return dot


def _split(t):
    hi = t.astype(BF16)
    return hi, (t - hi.astype(F32)).astype(BF16)


def _dg3(a, b, ca, cb):
    (ah, al), (bh, bl) = a, b
    return _dg(ah, bh, ca, cb) + (_dg(ah, bl, ca, cb) + _dg(al, bh, ca, cb))


class _Split3:
    pass


def _dg_any(a, b, ca, cb, precision):
    if precision is _Split3:
        return _dg3(_split(a), _split(b), ca, cb)
    return _dg(a, b, ca, cb, precision)


_bdot = _make_dot(lambda t: t.astype(BF16), None)
_hdot = _make_dot(lambda t: t, _Split3)


def _inv_unit_lower(low):
    n = low.shape[-1]
    eye = (lax.broadcasted_iota(jnp.int32, (n, n), 0) == lax.broadcasted_iota(jnp.int32, (n, n), 1)).astype(F32)
    p = -low
    inv = eye + p
    ps = _split(p)
    k = 1
    while 2 * k < n:
        ps = _split(_dg3(ps, ps, 1, 0))
        inv = inv + _dg3(_split(inv), ps, 1, 0)
        k *= 2
    return inv


@jax.custom_vjp
def _unit_lower_solve(low, rhs, inv):
    return _dg3(_split(inv), _split(rhs), 1, 0)


def _uls_fwd(low, rhs, inv):
    sol = _dg3(_split(inv), _split(rhs), 1, 0)
    return sol, (inv, sol)


def _uls_bwd(res, g):
    inv, sol = res
    d_rhs = _dg3(_split(inv), _split(g), 0, 0)
    return -_dg3(_split(d_rhs), _split(sol), 1, 1), d_rhs, jnp.zeros_like(inv)


_unit_lower_solve.defvjp(_uls_fwd, _uls_bwd)


def _sigmoid(x):
    return 1.0 / (1.0 + jnp.exp(-x))


def _softplus(x):
    return jnp.maximum(x, 0.0) + jnp.log(1.0 + jnp.exp(-jnp.maximum(x, -x)))


def _gelu(x):
    return 0.5 * x * (1.0 + jnp.tanh(0.7978845608028654 * (x + 0.044715 * (x * x * x))))


def _tri(n):
    ri = lax.broadcasted_iota(jnp.int32, (n, n), 0)
    ci = lax.broadcasted_iota(jnp.int32, (n, n), 1)
    return ri, ci


def _stack(ts):
    return jnp.concatenate([t[None] for t in ts], axis=0)


def _sgu_chunk(uv, ln_w, ln_b, ws, bs):
    u = _gelu(uv[:, :G])
    v = _gelu(uv[:, G:])
    mu = jnp.mean(v, axis=-1, keepdims=True)
    vc = v - mu
    var = jnp.mean(vc * vc, axis=-1, keepdims=True)
    vn = vc * lax.rsqrt(var + EPS) * ln_w + ln_b
    ri, ci = _tri(SGU_C)
    bs_t = _hdot((ri == ci).astype(F32), bs, "nt")
    mixed = []
    for h in range(NH):
        wm = jnp.where(ri >= ci, ws[h], 0.0)
        mixed.append(_bdot(wm, vn[:, HD * h:HD * (h + 1)], "nn") + bs_t[:, h:h + 1])
    return u * jnp.concatenate(mixed, axis=1)


def _dn_tile(q, k, v, ab, z, state, inv, a_log, dt_bias, nw):
    c = DN_C
    tm = q.shape[0]
    cps = tm // c
    ri, ci = _tri(tm)
    shift = c.bit_length() - 1
    same = jnp.right_shift(ri, shift) == jnp.right_shift(ci, shift)
    g4 = -jnp.exp(a_log) * _softplus(ab[:, 0:NH] + dt_bias)
    beta4 = _sigmoid(ab[:, NH:2 * NH])
    gc4 = _hdot((same & (ri >= ci)).astype(F32), g4, "nn")
    gr4 = _hdot(g4, (same & (ri <= ci)).astype(F32), "tn")
    pairs = [(slice(c * j, c * (j + 1)), h) for j in range(cps) for h in range(NH)]
    heads = lambda t: _stack([t[rows, HD * h:HD * (h + 1)] for rows, h in pairs])
    col = lambda t: _stack([t[rows, h:h + 1] for rows, h in pairs])
    qn, kn, vh = heads(q), heads(k), heads(v)
    qn = qn * lax.rsqrt(jnp.sum(qn * qn, axis=-1, keepdims=True) + EPS) * (HD ** -0.5)
    kn = kn * lax.rsqrt(jnp.sum(kn * kn, axis=-1, keepdims=True) + EPS)
    gc, beta = col(gc4), col(beta4)
    gr = _stack([gr4[h:h + 1, rows] for rows, h in pairs])
    gl = jnp.sum(col(g4), axis=1, keepdims=True)
    r2, c2 = _tri(c)
    decay = jnp.exp(jnp.where(r2 >= c2, gc - gr, -jnp.inf))
    kb = kn * beta
    low = jnp.where(r2 > c2, _bdot(kb, kn, "nt") * decay, 0.0)
    eg = jnp.exp(gc)
    if inv is None:
        inv = _inv_unit_lower(low)
    sol = _unit_lower_solve(low, jnp.concatenate([vh * beta, kb * eg], axis=2), inv)
    u, w = sol[:, :, :HD], sol[:, :, HD:]
    attn = _bdot(qn, kn, "nt") * decay
    qg, kd, cd = qn * eg, kn * jnp.exp(gl - gc), jnp.exp(gl)
    ys = []
    for j in range(cps):
        b = slice(NH * j, NH * (j + 1))
        v_new = u[b] - _bdot(w[b], state, "nn")
        o = _bdot(qg[b], state, "nn") + _bdot(attn[b], v_new, "nn")
        state = state * cd[b] + _bdot(kd[b], v_new, "tn")
        on = o * lax.rsqrt(jnp.mean(o * o, axis=-1, keepdims=True) + EPS) * nw
        ys.append(jnp.concatenate([on[h] for h in range(NH)], axis=1))
    return jnp.concatenate(ys, axis=0) * (z * _sigmoid(z)), state, inv


def _gla_tile(q, k, v, glr, z, state_t, w2, gate_bias, nw):
    c = GLA_C
    tm = q.shape[0]
    cps = tm // c
    ri, ci = _tri(tm)
    shift = c.bit_length() - 1
    same = jnp.right_shift(ri, shift) == jnp.right_shift(ci, shift)
    w2_rows = jnp.concatenate([w2, jnp.zeros((LANES - w2.shape[0], G), F32)], axis=0)
    pre = _bdot(glr, w2_rows, "nn") + gate_bias
    log_a = (jnp.minimum(pre, 0.0) - jnp.log(1.0 + jnp.exp(-jnp.maximum(pre, -pre)))) * (1.0 / 16.0)
    gcum = _hdot((same & (ri >= ci)).astype(F32), log_a, "nn")
    row = lax.broadcasted_iota(jnp.int32, (c, G), 0)
    qs = q * (HD ** -0.5)
    qa, ka, qg, kl, dec = [], [], [], [], []
    for j in range(cps):
        rows = slice(c * j, c * (j + 1))
        gj = gcum[rows]
        g_mid = jnp.sum(jnp.where(row == c // 2, gj, 0.0), axis=0, keepdims=True)
        g_last = jnp.sum(log_a[rows], axis=0, keepdims=True)
        qa.append(qs[rows] * jnp.exp(gj - g_mid))
        ka.append(k[rows] * jnp.exp(g_mid - gj))
        qg.append(qs[rows] * jnp.exp(gj))
        kl.append(k[rows] * jnp.exp(g_last - gj))
        dec.append(jnp.exp(g_last))
    heads = lambda ts: _stack([t[:, HD * h:HD * (h + 1)] for t in ts for h in range(NH)])
    qa, ka, qg, kl, dec = heads(qa), heads(ka), heads(qg), heads(kl), heads(dec)
    vh = heads([v[c * j:c * (j + 1)] for j in range(cps)])
    r2, c2 = _tri(c)
    o_intra = _bdot(jnp.where(r2 >= c2, _bdot(qa, ka, "nt"), 0.0), vh, "nn")
    ys = []
    for j in range(cps):
        b = slice(NH * j, NH * (j + 1))
        o = o_intra[b] + _bdot(qg[b], state_t, "nt")
        state_t = state_t * dec[b] + _bdot(vh[b], kl[b], "tn")
        on = o * lax.rsqrt(jnp.mean(o * o, axis=-1, keepdims=True) + EPS) * nw
        ys.append(jnp.concatenate([on[h] for h in range(NH)], axis=1))
    return jnp.concatenate(ys, axis=0) * (z * _sigmoid(z)), state_t


def _rms(x, nw):
    r = lax.rsqrt(jnp.mean(x * x, axis=-1, keepdims=True) + EPS)
    xh = x * r
    return xh * nw, xh, r


def _rms_bwd(g, xh, r, nw):
    gw = g * nw
    return r * (gw - xh * jnp.mean(gw * xh, axis=-1, keepdims=True)), jnp.sum(g * xh, axis=0, keepdims=True)


def _row(tm, w, blk=0):
    return pl.BlockSpec((tm, w), lambda i: (i, blk))


def _in_proj(x, nw, wp, l, tm=256):
    t = x.shape[0]

    def body(x_ref, nw_ref, w_ref, h_ref, p_ref):
        h = _rms(x_ref[...], nw_ref[l:l + 1, :])[0].astype(BF16)
        h_ref[...] = h
        p_ref[...] = jnp.dot(h, w_ref[...], preferred_element_type=F32)

    return pl.pallas_call(
        body, name="in_proj", grid=(t // tm,),
        in_specs=[_row(tm, D), _resident((DEPTH, D)), _layer((D, P), l)],
        out_specs=[_row(tm, D), _row(tm, P)],
        out_shape=[jax.ShapeDtypeStruct((t, D), BF16), jax.ShapeDtypeStruct((t, P), F32)],
        compiler_params=_cparams("parallel"))(x, nw, wp)


def _gu_spec(l):
    return pl.BlockSpec((N_DEV, None, D, GU_SHARD), lambda *_: (0, l, 0, 0), pipeline_mode=pl.Buffered(1))


def _out_mlp(x, ys, w_out, nw2, w_gu, w_down, l, tm=256):
    t = x.shape[0]

    def body(x_ref, ya, yb, yc, yd, wo_ref, nw_ref, wgu_ref, wd_ref, mix_ref, x1_ref, h2_ref, gu_ref, act_ref, x2_ref):
        mix = jnp.concatenate([ya[...], yb[...], yc[...], yd[...]], axis=1)
        mix_ref[...] = mix
        x1 = x_ref[...] + jnp.dot(mix, wo_ref[...], preferred_element_type=F32)
        x1_ref[...] = x1
        h2 = _rms(x1, nw_ref[l:l + 1, :])[0].astype(BF16)
        h2_ref[...] = h2
        x2 = x1
        for d in range(N_GATE):
            gate = jnp.dot(h2, wgu_ref[d], preferred_element_type=F32)
            up = jnp.dot(h2, wgu_ref[d + N_GATE], preferred_element_type=F32)
            gu_ref[d] = gate.astype(BF16)
            gu_ref[d + N_GATE] = up.astype(BF16)
            act = (gate * _sigmoid(gate) * up).astype(BF16)
            act_ref[d] = act
            x2 = x2 + jnp.dot(act, wd_ref[GU_SHARD * d:GU_SHARD * (d + 1), :], preferred_element_type=F32)
        x2_ref[...] = x2

    dev = lambda n: pl.BlockSpec((n, tm, GU_SHARD), lambda i: (0, i, 0))
    return pl.pallas_call(
        body, name="out_mlp", grid=(t // tm,),
        in_specs=[_row(tm, D), _row(tm, G), _row(tm, G), _row(tm, G), _row(tm, G), _layer((D, D), l),
                  _resident((DEPTH, D)), _gu_spec(l), _layer((FF, D), l)],
        out_specs=[_row(tm, D), _row(tm, D), _row(tm, D), dev(N_DEV), dev(N_GATE), _row(tm, D)],
        out_shape=[jax.ShapeDtypeStruct((t, D), BF16), jax.ShapeDtypeStruct((t, D), F32),
                   jax.ShapeDtypeStruct((t, D), BF16), jax.ShapeDtypeStruct((N_DEV, t, GU_SHARD), BF16),
                   jax.ShapeDtypeStruct((N_GATE, t, GU_SHARD), BF16), jax.ShapeDtypeStruct((t, D), F32)],
        compiler_params=_cparams("parallel"))(x, *ys, w_out, nw2, w_gu, w_down)


def _loss_head(x, nw, target, tm=512):
    t = x.shape[0]

    def body(x_ref, nw_ref, tg_ref, loss_ref, dnw_ref, dx_ref):
        @pl.when(pl.program_id(0) == 0)
        def _():
            loss_ref[...] = jnp.zeros_like(loss_ref)
            dnw_ref[...] = jnp.zeros_like(dnw_ref)
        nw_v = nw_ref[...]
        y, xh, r = _rms(x_ref[...], nw_v)
        err = y - tg_ref[...]
        loss_ref[...] += 0.5 * jnp.sum(err * err) * (1.0 / D)
        dx, dnw = _rms_bwd(err * (1.0 / D), xh, r, nw_v)
        dx_ref[...] = dx
        dnw_ref[...] += dnw

    return pl.pallas_call(
        body, name="loss_head", grid=(t // tm,),
        in_specs=[_row(tm, D), _resident((1, D)), _row(tm, D)],
        out_specs=[_acc((8, LANES)), _acc((1, D)), _row(tm, D)],
        out_shape=[jax.ShapeDtypeStruct((8, LANES), F32), jax.ShapeDtypeStruct((1, D), F32),
                   jax.ShapeDtypeStruct((t, D), F32)],
        compiler_params=_cparams("arbitrary"))(x, nw, target)


def _mlp_out_bwd(dx2, x1, gu, nw2, w_down, w_gu, w_out, l, tm=256):
    t = dx2.shape[0]

    def body(dx2_ref, x1_ref, gu_ref, nw_ref, wd_ref, wgu_ref, wo_ref, dgu_ref, dx1_ref, dmix_ref, dnw_ref):
        @pl.when(pl.program_id(0) == 0)
        def _():
            dnw_ref[...] = jnp.zeros_like(dnw_ref)
        dx2 = dx2_ref[...]
        dx2_b = dx2.astype(BF16)
        dh2 = jnp.zeros((tm, D), F32)
        for d in range(N_GATE):
            dact = _dg(dx2_b, wd_ref[GU_SHARD * d:GU_SHARD * (d + 1), :], 1, 1)
            gate, up = gu_ref[d].astype(F32), gu_ref[d + N_GATE].astype(F32)
            s = _sigmoid(gate)
            dgate = (dact * up * (s * (1.0 + gate * (1.0 - s)))).astype(BF16)
            dup = (dact * (gate * s)).astype(BF16)
            dgu_ref[d] = dgate
            dgu_ref[d + N_GATE] = dup
            dh2 = dh2 + _dg(dgate, wgu_ref[d], 1, 1) + _dg(dup, wgu_ref[d + N_GATE], 1, 1)
        nw_v = nw_ref[l:l + 1, :]
        _, xh, r = _rms(x1_ref[...], nw_v)
        dxn, dnw = _rms_bwd(dh2, xh, r, nw_v)
        dx1 = dx2 + dxn
        dx1_ref[...] = dx1
        dnw_ref[...] += dnw
        dmix_ref[...] = _dg(dx1.astype(BF16), wo_ref[...], 1, 1)

    dev = pl.BlockSpec((N_DEV, tm, GU_SHARD), lambda i: (0, i, 0))
    return pl.pallas_call(
        body, name="mlp_out_bwd", grid=(t // tm,),
        in_specs=[_row(tm, D), _row(tm, D), dev, _resident((DEPTH, D)), _layer((FF, D), l), _gu_spec(l),
                  _layer((D, D), l)],
        out_specs=[dev, _row(tm, D), _row(tm, D), _acc((1, D))],
        out_shape=[jax.ShapeDtypeStruct((N_DEV, t, GU_SHARD), BF16), jax.ShapeDtypeStruct((t, D), F32),
                   jax.ShapeDtypeStruct((t, D), F32), jax.ShapeDtypeStruct((1, D), F32)],
        compiler_params=_cparams("arbitrary"))(dx2, x1, gu, nw2, w_down, w_gu, w_out)


_DP_WIDTHS = (768, 768, 512, 768, 256, 256, 128, 128)


def _in_proj_bwd(dps, x, dx1, nw, wp, l, tm=256):
    t = x.shape[0]

    def body(*refs):
        dp_refs, (x_ref, dx1_ref, nw_ref, w_ref, dp_ref, dx_ref, dnw_ref) = refs[:8], refs[8:]

        @pl.when(pl.program_id(0) == 0)
        def _():
            dnw_ref[...] = jnp.zeros_like(dnw_ref)
        dp = jnp.concatenate([r[...] for r in dp_refs], axis=1)
        dp_ref[...] = dp
        dh = _dg(dp, w_ref[...], 1, 1)
        nw_v = nw_ref[l:l + 1, :]
        _, xh, r = _rms(x_ref[...], nw_v)
        dxn, dnw = _rms_bwd(dh, xh, r, nw_v)
        dx_ref[...] = dx1_ref[...] + dxn
        dnw_ref[...] += dnw

    return pl.pallas_call(
        body, name="in_proj_bwd", grid=(t // tm,),
        in_specs=[_row(tm, w) for w in _DP_WIDTHS] + [_row(tm, D), _row(tm, D), _resident((DEPTH, D)), _layer((D, P), l)],
        out_specs=[_row(tm, P), _row(tm, D), _acc((1, D))],
        out_shape=[jax.ShapeDtypeStruct((t, P), BF16), jax.ShapeDtypeStruct((t, D), F32),
                   jax.ShapeDtypeStruct((1, D), F32)],
        compiler_params=_cparams("arbitrary"))(*dps, x, dx1, nw, wp)


def _accumulate(acc, o_ref, t_axis, term):
    @pl.when(pl.program_id(t_axis) == 0)
    def _():
        acc[...] = jnp.zeros_like(acc)
    acc[...] += term

    @pl.when(pl.program_id(t_axis) == pl.num_programs(t_axis) - 1)
    def _():
        o_ref[...] = acc[...].astype(o_ref.dtype)


WGRAD_TOKENS = 2048


def _matmul_tn(a, b, name, out_dtype, tn=512, tt=WGRAD_TOKENS):
    t, m = a.shape
    n = b.shape[1]
    tt = min(tt, t)

    def body(a_ref, b_ref, o_ref, acc):
        _accumulate(acc, o_ref, 1, _dg(a_ref[...].astype(BF16), b_ref[...].astype(BF16), 0, 0))

    return pl.pallas_call(
        body, name=name, grid=(n // tn, t // tt),
        in_specs=[pl.BlockSpec((tt, m), lambda j, i: (i, 0)), pl.BlockSpec((tt, tn), lambda j, i: (i, j))],
        out_specs=pl.BlockSpec((m, tn), lambda j, i: (0, j)),
        out_shape=jax.ShapeDtypeStruct((m, n), out_dtype),
        scratch_shapes=[pltpu.VMEM((m, tn), F32)],
        compiler_params=_cparams("parallel", "arbitrary"))(a, b)


def _wgrad_gate_up(h2, dgu, tt=WGRAD_TOKENS):
    t = h2.shape[0]
    tt = min(tt, t)

    def body(a_ref, b_ref, o_ref, acc):
        _accumulate(acc, o_ref, 1, _dg(a_ref[...], b_ref[...], 0, 0))

    return pl.pallas_call(
        body, name="wgrad_gate_up", grid=(N_DEV, t // tt),
        in_specs=[pl.BlockSpec((tt, D), lambda d, i: (i, 0)), pl.BlockSpec((None, tt, GU_SHARD), lambda d, i: (d, i, 0))],
        out_specs=pl.BlockSpec((None, D, GU_SHARD), lambda d, i: (d, 0, 0)),
        out_shape=jax.ShapeDtypeStruct((N_DEV, D, GU_SHARD), BF16),
        scratch_shapes=[pltpu.VMEM((D, GU_SHARD), F32)],
        compiler_params=_cparams("parallel", "arbitrary"))(h2, dgu)


def _wgrad_down(act, dx2, tt=WGRAD_TOKENS):
    t = dx2.shape[0]
    tt = min(tt, t)

    def body(a_ref, b_ref, o_ref, acc):
        _accumulate(acc, o_ref, 1, _dg(a_ref[...], b_ref[...].astype(BF16), 0, 0))

    return pl.pallas_call(
        body, name="wgrad_down", grid=(N_GATE, t // tt),
        in_specs=[pl.BlockSpec((None, tt, GU_SHARD), lambda d, i: (d, i, 0)), pl.BlockSpec((tt, D), lambda d, i: (i, 0))],
        out_specs=pl.BlockSpec((GU_SHARD, D), lambda d, i: (d, 0)),
        out_shape=jax.ShapeDtypeStruct((FF, D), BF16),
        scratch_shapes=[pltpu.VMEM((GU_SHARD, D), F32)],
        compiler_params=_cparams("parallel", "arbitrary"))(act, dx2)


def _relayout_w_in(g, tr=256):
    def body(w_ref, o_ref):
        full = jnp.concatenate([w_ref[d] for d in range(N_DEV)], axis=1)
        parts = []
        for a, b, z in _PAD_SEGS:
            parts.append(full[:, a:b])
            if z:
                parts.append(jnp.zeros((tr, z), full.dtype))
        o_ref[...] = jnp.concatenate(parts, axis=1)

    return pl.pallas_call(
        body, name="relayout_w_in", grid=(DEPTH, D // tr),
        in_specs=[pl.BlockSpec((N_DEV, None, tr, IN_SHARD), lambda l, i: (0, l, i, 0))],
        out_specs=pl.BlockSpec((None, tr, P), lambda l, i: (l, i, 0)),
        out_shape=jax.ShapeDtypeStruct((DEPTH, D, P), g.dtype),
        compiler_params=_cparams("parallel", "parallel"))(g)


def _scatter_w_in_grad(gp, tr=256):
    def body(g_ref, o_ref):
        g = g_ref[...]
        pieces, off = [], 0
        for a, b, z in _PAD_SEGS:
            pieces.append((a, g[:, off:off + (b - a)]))
            off += (b - a) + z
        nat = jnp.concatenate([piece for _, piece in sorted(pieces, key=lambda ap: ap[0])], axis=1)
        for d in range(N_DEV):
            o_ref[d] = nat[:, IN_SHARD * d:IN_SHARD * (d + 1)].astype(BF16)

    return pl.pallas_call(
        body, name="scatter_w_in_grad", grid=(D // tr,),
        in_specs=[pl.BlockSpec((tr, P), lambda i: (i, 0))],
        out_specs=pl.BlockSpec((N_DEV, tr, IN_SHARD), lambda i: (0, i, 0)),
        out_shape=jax.ShapeDtypeStruct((N_DEV, D, IN_SHARD), BF16),
        compiler_params=_cparams("parallel"))(gp)


_A_BLK = 3


def _sgu_specs(l):
    return [_resident((DEPTH, G)), _resident((DEPTH, G)), _layer((NH, SGU_C, SGU_C), l), _layer((NH, SGU_C), l)]


def _sgu_fwd(p, ln_w, ln_b, ws, bs, l, tm=256):
    t = p.shape[0]

    def body(uv_ref, lw_ref, lb_ref, ws_ref, bs_ref, y_ref):
        ws_v = [ws_ref[h] for h in range(NH)]
        for c in range(tm // SGU_C):
            rows = pl.ds(c * SGU_C, SGU_C)
            y_ref[rows, :] = _sgu_chunk(uv_ref[rows, :], lw_ref[l:l + 1, :], lb_ref[l:l + 1, :], ws_v,
                                        bs_ref[...]).astype(BF16)

    return pl.pallas_call(
        body, name="sgu_fwd", grid=(t // tm,),
        in_specs=[_row(tm, 2 * G, _A_BLK)] + _sgu_specs(l), out_specs=_row(tm, G),
        out_shape=jax.ShapeDtypeStruct((t, G), BF16),
        compiler_params=_cparams("parallel"))(p, ln_w, ln_b, ws, bs)


def _sgu_bwd(p, dmix, ln_w, ln_b, ws, bs, l, tm=256):
    t = p.shape[0]

    def body(uv_ref, dy_ref, lw_ref, lb_ref, ws_ref, bs_ref, duv_ref, dlw_ref, dlb_ref, dws_ref, dbs_ref):
        @pl.when(pl.program_id(0) == 0)
        def _():
            for r in (dlw_ref, dlb_ref, dws_ref, dbs_ref):
                r[...] = jnp.zeros_like(r)
        ws_v = [ws_ref[h] for h in range(NH)]
        for c in range(tm // SGU_C):
            rows = pl.ds(c * SGU_C, SGU_C)
            _, vjp = jax.vjp(_sgu_chunk, uv_ref[rows, :], lw_ref[l:l + 1, :], lb_ref[l:l + 1, :], ws_v, bs_ref[...])
            duv, dlw, dlb, dws, dbs = vjp(dy_ref[rows, :])
            duv_ref[rows, :] = duv.astype(BF16)
            dlw_ref[...] += dlw
            dlb_ref[...] += dlb
            dbs_ref[...] += dbs
            for h in range(NH):
                dws_ref[h] += dws[h]

    return pl.pallas_call(
        body, name="sgu_bwd", grid=(t // tm,),
        in_specs=[_row(tm, 2 * G, _A_BLK), _row(tm, G, 0)] + _sgu_specs(l),
        out_specs=[_row(tm, 2 * G), _acc((1, G)), _acc((1, G)), _acc((NH, SGU_C, SGU_C)), _acc((NH, SGU_C))],
        out_shape=[jax.ShapeDtypeStruct((t, 2 * G), BF16), jax.ShapeDtypeStruct((1, G), F32),
                   jax.ShapeDtypeStruct((1, G), F32), jax.ShapeDtypeStruct((NH, SGU_C, SGU_C), F32),
                   jax.ShapeDtypeStruct((NH, SGU_C), F32)],
        compiler_params=_cparams("arbitrary"))(p, dmix, ln_w, ln_b, ws, bs)


HALO = 8


def _prev_halo(tm, width, col):
    return pl.BlockSpec((HALO, width), lambda i: (jnp.maximum(i * (tm // HALO) - 1, 0), col))


def _next_halo(tm, width, col, t):
    return pl.BlockSpec((HALO, width), lambda i: (jnp.minimum((i + 1) * (tm // HALO), t // HALO - 1), col))


def _with_prev(halo_ref, x_ref):
    halo = jnp.where(pl.program_id(0) == 0, 0.0, halo_ref[...])
    return jnp.concatenate([halo, x_ref[...]], axis=0)


def _with_next(x_ref, halo_ref):
    halo = jnp.where(pl.program_id(0) == pl.num_programs(0) - 1, 0.0, halo_ref[...])
    return jnp.concatenate([x_ref[...], halo], axis=0)


def _delay(ext, j):
    return ext if j == 0 else pltpu.roll(ext, j, axis=0)


def _advance(ext, j):
    return ext if j == 0 else pltpu.roll(ext, ext.shape[0] - j, axis=0)


def _causal_conv(ext, w, tm):
    kw = w.shape[0]
    out = None
    for k in range(kw):
        term = _delay(ext, kw - 1 - k)[HALO:HALO + tm] * w[k:k + 1, :]
        out = term if out is None else out + term
    return out


def _causal_conv_t(ext, w, tm):
    kw = w.shape[0]
    out = None
    for k in range(kw):
        term = _advance(ext, kw - 1 - k)[0:tm] * w[k:k + 1, :]
        out = term if out is None else out + term
    return out


def _conv_wgrad(ext, dy, kw, tm):
    return jnp.concatenate(
        [jnp.sum(_delay(ext, kw - 1 - k)[HALO:HALO + tm] * dy, axis=0, keepdims=True) for k in range(kw)], axis=0)


_B_GB, _B_GC, _B_H = 8, 9, 10
_C_QKV, _D_QKV = 0, 1
_C_Z, _D_Z = 11, 12
_C_AB, _D_GLR = 26, 27


def _sconv_fwd(p, w, l, tm=512):
    t = p.shape[0]

    def body(gb_ref, gc_ref, h_ref, gc_halo, h_halo, w_ref, y_ref):
        s_ext = _with_prev(gc_halo, gc_ref) * _with_prev(h_halo, h_ref)
        y_ref[...] = (gb_ref[...] * _causal_conv(s_ext, w_ref[...], tm)).astype(BF16)

    return pl.pallas_call(
        body, name="sconv_fwd", grid=(t // tm,),
        in_specs=[_row(tm, G, _B_GB), _row(tm, G, _B_GC), _row(tm, G, _B_H), _prev_halo(tm, G, _B_GC),
                  _prev_halo(tm, G, _B_H), _layer((3, G), l)],
        out_specs=_row(tm, G), out_shape=jax.ShapeDtypeStruct((t, G), BF16),
        compiler_params=_cparams("parallel"))(p, p, p, p, p, w)


def _sconv_bwd(p, dmix, w, l, tm=512):
    t = p.shape[0]

    def body(gb_ref, gc_ref, h_ref, gc_halo, h_halo, gb_next, dy_ref, dy_next, w_ref, dp_ref, dw_ref):
        @pl.when(pl.program_id(0) == 0)
        def _():
            dw_ref[...] = jnp.zeros_like(dw_ref)
        w_v = w_ref[...]
        s_ext = _with_prev(gc_halo, gc_ref) * _with_prev(h_halo, h_ref)
        dout = dy_ref[...]
        d_gb = dout * _causal_conv(s_ext, w_v, tm)
        dconv_ext = _with_next(dy_ref, dy_next) * _with_next(gb_ref, gb_next)
        ds = _causal_conv_t(dconv_ext, w_v, tm)
        dw_ref[...] += _conv_wgrad(s_ext, dconv_ext[0:tm], 3, tm)
        dp_ref[...] = jnp.concatenate([d_gb, ds * h_ref[...], ds * gc_ref[...]], axis=1).astype(BF16)

    return pl.pallas_call(
        body, name="sconv_bwd", grid=(t // tm,),
        in_specs=[_row(tm, G, _B_GB), _row(tm, G, _B_GC), _row(tm, G, _B_H), _prev_halo(tm, G, _B_GC),
                  _prev_halo(tm, G, _B_H), _next_halo(tm, G, _B_GB, t), _row(tm, G, 1), _next_halo(tm, G, 1, t),
                  _layer((3, G), l)],
        out_specs=[_row(tm, 3 * G), _acc((3, G))],
        out_shape=[jax.ShapeDtypeStruct((t, 3 * G), BF16), jax.ShapeDtypeStruct((3, G), F32)],
        compiler_params=_cparams("arbitrary"))(p, p, p, p, p, p, dmix, dmix, w)


def _qkv_conv_fwd(p, w, l, tm=512):
    t = p.shape[0]

    def body(x_ref, x_halo, w_ref, y_ref):
        c = _causal_conv(_with_prev(x_halo, x_ref), w_ref[...], tm)
        y_ref[...] = c * _sigmoid(c)

    return pl.pallas_call(
        body, name="qkv_conv_fwd", grid=(t // tm,),
        in_specs=[_row(tm, 3 * G, _C_QKV), _prev_halo(tm, 3 * G, _C_QKV), _layer((4, 3 * G), l)],
        out_specs=_row(tm, 3 * G), out_shape=jax.ShapeDtypeStruct((t, 3 * G), F32),
        compiler_params=_cparams("parallel"))(p, p, w)


def _qkv_conv_bwd(p, dy, w, l, tm=512):
    t = p.shape[0]

    def body(x_ref, x_halo, x_next, dy_ref, dy_next, w_ref, dx_ref, dw_ref):
        @pl.when(pl.program_id(0) == 0)
        def _():
            dw_ref[...] = jnp.zeros_like(dw_ref)
        w_v = w_ref[...]
        x_all = jnp.concatenate([_with_prev(x_halo, x_ref), jnp.where(
            pl.program_id(0) == pl.num_programs(0) - 1, 0.0, x_next[...])], axis=0)
        c = _causal_conv(x_all, w_v, tm + HALO)
        s = _sigmoid(c)
        dc_ext = _with_next(dy_ref, dy_next) * (s * (1.0 + c * (1.0 - s)))
        dx_ref[...] = _causal_conv_t(dc_ext, w_v, tm).astype(BF16)
        dw_ref[...] += _conv_wgrad(x_all[0:HALO + tm], dc_ext[0:tm], 4, tm)

    return pl.pallas_call(
        body, name="qkv_conv_bwd", grid=(t // tm,),
        in_specs=[_row(tm, 3 * G, _C_QKV), _prev_halo(tm, 3 * G, _C_QKV), _next_halo(tm, 3 * G, _C_QKV, t),
                  _row(tm, 3 * G), _next_halo(tm, 3 * G, 0, t), _layer((4, 3 * G), l)],
        out_specs=[_row(tm, 3 * G), _acc((4, 3 * G))],
        out_shape=[jax.ShapeDtypeStruct((t, 3 * G), BF16), jax.ShapeDtypeStruct((4, 3 * G), F32)],
        compiler_params=_cparams("arbitrary"))(p, p, p, dy, dy, w)


_STATE = pl.BlockSpec((1, NH, HD, HD), lambda i: (i, 0, 0, 0))


def _dn_specs():
    return [_resident((DEPTH, NH)), _resident((DEPTH, NH)), _resident((DEPTH, HD))]


def _dn_fwd(qkv, p, params, l, cps=4):
    t = qkv.shape[0]
    tm = DN_C * cps
    nb = cps * NH

    def body(qkv_ref, z_ref, ab_ref, alog_ref, dt_ref, nw_ref, y_ref, st_ref, inv_ref, state):
        @pl.when(pl.program_id(0) == 0)
        def _():
            state[...] = jnp.zeros_like(state)
        st_ref[0] = state[...]
        y, new, inv = _dn_tile(qkv_ref[:, 0:G], qkv_ref[:, G:2 * G], qkv_ref[:, 2 * G:3 * G], ab_ref[...], z_ref[...],
                               state[...], None, alog_ref[l:l + 1, :], dt_ref[l:l + 1, :], nw_ref[l:l + 1, :])
        y_ref[...] = y.astype(BF16)
        inv_ref[...] = inv
        state[...] = new

    return pl.pallas_call(
        body, name="dn_fwd", grid=(t // tm,),
        in_specs=[_row(tm, 3 * G), _row(tm, G, _C_Z), _row(tm, LANES, _C_AB)] + _dn_specs(),
        out_specs=[_row(tm, G), _STATE, pl.BlockSpec((nb, DN_C, DN_C), lambda i: (i, 0, 0))],
        out_shape=[jax.ShapeDtypeStruct((t, G), BF16), jax.ShapeDtypeStruct((t // tm, NH, HD, HD), F32),
                   jax.ShapeDtypeStruct((t // DN_C * NH, DN_C, DN_C), F32)],
        scratch_shapes=[pltpu.VMEM((NH, HD, HD), F32)],
        compiler_params=_cparams("arbitrary"))(qkv, p, p, *params)


def _dn_bwd(qkv, p, states, inv, dmix, params, l, cps=4):
    t = qkv.shape[0]
    tm = DN_C * cps
    n = t // tm
    nb = cps * NH

    def body(qkv_ref, z_ref, ab_ref, st_ref, inv_ref, dy_ref, alog_ref, dt_ref, nw_ref,
             dqkv_ref, dz_ref, dab_ref, dalog_ref, ddt_ref, dnw_ref, dstate):
        dprm_refs = (dalog_ref, ddt_ref, dnw_ref)

        @pl.when(pl.program_id(0) == 0)
        def _():
            dstate[...] = jnp.zeros_like(dstate)
            for r in dprm_refs:
                r[...] = jnp.zeros_like(r)
        inv_v = inv_ref[...]
        tile = lambda q, k, v, ab, z, st, alog, dt, nw: _dn_tile(q, k, v, ab, z, st, inv_v, alog, dt, nw)[:2]
        _, vjp = jax.vjp(tile, qkv_ref[:, 0:G], qkv_ref[:, G:2 * G], qkv_ref[:, 2 * G:3 * G], ab_ref[...], z_ref[...],
                         st_ref[0], alog_ref[l:l + 1, :], dt_ref[l:l + 1, :], nw_ref[l:l + 1, :])
        dq, dk, dv, dab, dz, dst, *dprm = vjp((dy_ref[...], dstate[...]))
        dqkv_ref[...] = jnp.concatenate([dq, dk, dv], axis=1)
        dz_ref[...] = dz.astype(BF16)
        dab_ref[...] = dab.astype(BF16)
        dstate[...] = dst
        for r, g in zip(dprm_refs, dprm):
            r[...] += g

    rev = lambda w, blk: pl.BlockSpec((tm, w), lambda i: (n - 1 - i, blk))
    return pl.pallas_call(
        body, name="dn_bwd", grid=(n,),
        in_specs=[rev(3 * G, 0), rev(G, _C_Z), rev(LANES, _C_AB),
                  pl.BlockSpec((1, NH, HD, HD), lambda i: (n - 1 - i, 0, 0, 0)),
                  pl.BlockSpec((nb, DN_C, DN_C), lambda i: (n - 1 - i, 0, 0)), rev(G, 2)] + _dn_specs(),
        out_specs=[rev(3 * G, 0), rev(G, 0), rev(LANES, 0), _acc((1, NH)), _acc((1, NH)), _acc((1, HD))],
        out_shape=[jax.ShapeDtypeStruct((t, 3 * G), F32), jax.ShapeDtypeStruct((t, G), BF16),
                   jax.ShapeDtypeStruct((t, LANES), BF16), jax.ShapeDtypeStruct((1, NH), F32),
                   jax.ShapeDtypeStruct((1, NH), F32), jax.ShapeDtypeStruct((1, HD), F32)],
        scratch_shapes=[pltpu.VMEM((NH, HD, HD), F32)],
        compiler_params=_cparams("arbitrary"))(qkv, p, p, states, inv, dmix, *params)


def _gla_specs(l):
    return [_layer((16, G), l), _resident((DEPTH, G)), _resident((DEPTH, HD))]


def _gla_fwd(p, params, l, cps=4):
    t = p.shape[0]
    tm = GLA_C * cps

    def body(qkv_ref, z_ref, glr_ref, w2_ref, gb_ref, nw_ref, y_ref, st_ref, state):
        @pl.when(pl.program_id(0) == 0)
        def _():
            state[...] = jnp.zeros_like(state)
        st_ref[0] = state[...]
        y, new = _gla_tile(qkv_ref[:, 0:G], qkv_ref[:, G:2 * G], qkv_ref[:, 2 * G:3 * G], glr_ref[...], z_ref[...],
                           state[...], w2_ref[...], gb_ref[l:l + 1, :], nw_ref[l:l + 1, :])
        y_ref[...] = y.astype(BF16)
        state[...] = new

    return pl.pallas_call(
        body, name="gla_fwd", grid=(t // tm,),
        in_specs=[_row(tm, 3 * G, _D_QKV), _row(tm, G, _D_Z), _row(tm, LANES, _D_GLR)] + _gla_specs(l),
        out_specs=[_row(tm, G), _STATE],
        out_shape=[jax.ShapeDtypeStruct((t, G), BF16), jax.ShapeDtypeStruct((t // tm, NH, HD, HD), F32)],
        scratch_shapes=[pltpu.VMEM((NH, HD, HD), F32)],
        compiler_params=_cparams("arbitrary"))(p, p, p, *params)


def _gla_bwd(p, states, dmix, params, l, cps=4):
    t = p.shape[0]
    tm = GLA_C * cps
    n = t // tm

    def body(qkv_ref, z_ref, glr_ref, st_ref, dy_ref, w2_ref, gb_ref, nw_ref,
             dqkv_ref, dz_ref, dglr_ref, dw2_ref, dgb_ref, dnw_ref, dstate):
        dprm_refs = (dw2_ref, dgb_ref, dnw_ref)

        @pl.when(pl.program_id(0) == 0)
        def _():
            dstate[...] = jnp.zeros_like(dstate)
            for r in dprm_refs:
                r[...] = jnp.zeros_like(r)
        _, vjp = jax.vjp(_gla_tile, qkv_ref[:, 0:G], qkv_ref[:, G:2 * G], qkv_ref[:, 2 * G:3 * G], glr_ref[...],
                         z_ref[...], st_ref[0], w2_ref[...], gb_ref[l:l + 1, :], nw_ref[l:l + 1, :])
        dq, dk, dv, dglr, dz, dst, *dprm = vjp((dy_ref[...], dstate[...]))
        dqkv_ref[...] = jnp.concatenate([dq, dk, dv], axis=1).astype(BF16)
        dz_ref[...] = dz.astype(BF16)
        dglr_ref[...] = dglr.astype(BF16)
        dstate[...] = dst
        for r, g in zip(dprm_refs, dprm):
            r[...] += g

    rev = lambda w, blk: pl.BlockSpec((tm, w), lambda i: (n - 1 - i, blk))
    return pl.pallas_call(
        body, name="gla_bwd", grid=(n,),
        in_specs=[rev(3 * G, _D_QKV), rev(G, _D_Z), rev(LANES, _D_GLR),
                  pl.BlockSpec((1, NH, HD, HD), lambda i: (n - 1 - i, 0, 0, 0)), rev(G, 3)] + _gla_specs(l),
        out_specs=[rev(3 * G, 0), rev(G, 0), rev(LANES, 0), _acc((16, G)), _acc((1, G)), _acc((1, HD))],
        out_shape=[jax.ShapeDtypeStruct((t, 3 * G), BF16), jax.ShapeDtypeStruct((t, G), BF16),
                   jax.ShapeDtypeStruct((t, LANES), BF16), jax.ShapeDtypeStruct((16, G), F32),
                   jax.ShapeDtypeStruct((1, G), F32), jax.ShapeDtypeStruct((1, HD), F32)],
        scratch_shapes=[pltpu.VMEM((NH, HD, HD), F32)],
        compiler_params=_cparams("arbitrary"))(p, p, p, states, dmix, *params)


def _adamw_math(w, g, m, v):
    m = ADAM_B1 * m + (1.0 - ADAM_B1) * g
    v = ADAM_B2 * v + (1.0 - ADAM_B2) * (g * g)
    m_hat = m / (1.0 - ADAM_B1 ** ADAM_STEP)
    v_hat = v / (1.0 - ADAM_B2 ** ADAM_STEP)
    return -ADAM_LR * (m_hat / (jnp.sqrt(v_hat) + ADAM_EPS) + ADAM_WD * w), m, v


def _adamw_large(parts, w, m, v, name, tr):
    _, r, c = w.shape

    def body(p_ref, w_ref, m_ref, v_ref, g_ref, d_ref, nm_ref, nv_ref):
        g = p_ref[0].astype(F32)
        for k in range(1, N_DEV):
            g = g + p_ref[k].astype(F32)
        g_ref[...] = g
        d_ref[...], nm_ref[...], nv_ref[...] = _adamw_math(w_ref[...], g, m_ref[...], v_ref[...])

    blk = pl.BlockSpec((None, tr, c), lambda l, i: (l, i, 0))
    return pl.pallas_call(
        body, name=name, grid=(DEPTH, r // tr),
        in_specs=[pl.BlockSpec((None, N_DEV, tr, c), lambda l, i: (l, 0, i, 0)), blk, blk, blk],
        out_specs=[blk] * 4, out_shape=[jax.ShapeDtypeStruct(w.shape, F32)] * 4,
        compiler_params=_cparams("parallel", "parallel"))(parts, w, m, v)


_SLAB_AT = {
    "sgu_w_spatial": (0, 0, SGU_C, LANES),
    "sgu_b_spatial": (128, 0, NH, SGU_C),
    "norm1_w": (132, 0, 1, D),
    "norm2_w": (133, 0, 1, D),
    "sgu_ln_w": (134, 0, 1, G),
    "sgu_ln_b": (134, 256, 1, G),
    "gla_gate_bias": (134, 512, 1, G),
    "dn_norm_w": (134, 768, 1, HD),
    "gla_norm_w": (134, 896, 1, HD),
    "dn_a_log": (135, 0, 1, NH),
    "dn_dt_bias": (135, 128, 1, NH),
    "gla_w_gate2": (136, 0, 16, G),
    "dn_conv_w": (136, 256, 4, 3 * G),
    "sc_conv_w": (140, 256, 3, G),
}
_LAYER_ROWS = 152
_FINAL_ROW = DEPTH * _LAYER_ROWS
_SLAB_ROWS, _SLAB_COLS = _FINAL_ROW + 8, 1024
_SLAB_ORDER = tuple(_SLAB_AT)
_SHARDED_SMALL = ("sc_conv_w", "dn_conv_w", "gla_w_gate2")
_REPLICATED = tuple(k for k in _SLAB_ORDER if k not in _SHARDED_SMALL)


def _region(name, l, h=0):
    r0, c0, nr, nc = _SLAB_AT[name]
    return slice(_LAYER_ROWS * l + r0, _LAYER_ROWS * l + r0 + nr), slice(c0 + nc * h, c0 + nc * (h + 1))


def _pack_small(layer_grads, d_final):
    def body(*refs):
        slab = refs[-1]
        slab[...] = jnp.zeros_like(slab)
        it = iter(refs[:-1])
        for l in range(DEPTH):
            for name in _SLAB_ORDER:
                ref = next(it)
                if name == "sgu_w_spatial":
                    for h in range(NH):
                        slab[_region(name, l, h)] = ref[h]
                else:
                    slab[_region(name, l)] = ref[...]
        slab[_FINAL_ROW:_FINAL_ROW + 1, :] = next(it)[...]

    flat = [layer_grads[l][name] for l in range(DEPTH) for name in _SLAB_ORDER] + [d_final]
    return pl.pallas_call(body, name="pack_small_grads", out_shape=jax.ShapeDtypeStruct((_SLAB_ROWS, _SLAB_COLS), F32),
                          compiler_params=_cparams())(*flat)


def _adamw_small(parts, w, m, v):
    names = _REPLICATED + ("final_norm_w",)
    n_in = len(names)

    def body(*refs):
        def total(rows, cols):
            g = refs[0][0, rows, cols]
            for k in range(1, N_DEV):
                g = g + refs[0][k, rows, cols]
            return g

        w_refs = dict(zip(names, refs[1:1 + n_in]))
        m_refs = dict(zip(names, refs[1 + n_in:1 + 2 * n_in]))
        v_refs = dict(zip(names, refs[1 + 2 * n_in:1 + 3 * n_in]))
        outs = refs[1 + 3 * n_in:]
        out_refs = [dict(zip(names, outs[j * n_in:(j + 1) * n_in])) for j in range(4)]
        full_refs = dict(zip(_SHARDED_SMALL, outs[4 * n_in:]))

        def update(name, idx, g):
            res = (g,) + _adamw_math(w_refs[name][idx], g, m_refs[name][idx], v_refs[name][idx])
            for o, val in zip(out_refs, res):
                o[name][idx] = val

        for l in range(DEPTH):
            for name in _REPLICATED:
                if name == "sgu_w_spatial":
                    for h in range(NH):
                        update(name, (l, h), total(*_region(name, l, h)))
                elif name == "sgu_b_spatial":
                    update(name, (l,), total(*_region(name, l)))
                else:
                    update(name, (slice(l, l + 1), slice(None)), total(*_region(name, l)))
            for name in _SHARDED_SMALL:
                full_refs[name][l] = total(*_region(name, l))
        update("final_norm_w", (slice(None), slice(None)), total(slice(_FINAL_ROW, _FINAL_ROW + 1), slice(None)))

    shapes = [jax.ShapeDtypeStruct(w[k].shape, F32) for k in names]
    full_shapes = [jax.ShapeDtypeStruct((DEPTH,) + _SLAB_AT[k][2:], F32) for k in _SHARDED_SMALL]
    outs = pl.pallas_call(body, name="adamw_small", out_shape=shapes * 4 + full_shapes, compiler_params=_cparams())(
        parts, *[w[k] for k in names], *[m[k] for k in names], *[v[k] for k in names])
    result = [dict(zip(names, outs[j * n_in:(j + 1) * n_in])) for j in range(4)]
    return result, dict(zip(_SHARDED_SMALL, outs[4 * n_in:]))


def _adamw_shards(g, w, m, v):
    names = _SHARDED_SMALL
    n = len(names)

    def body(*refs):
        for j in range(n):
            g_v = refs[j][...]
            res = _adamw_math(refs[n + j][...], g_v, refs[2 * n + j][...], refs[3 * n + j][...])
            for o, val in zip(refs[4 * n + j::n], res):
                o[...] = val

    shapes = [jax.ShapeDtypeStruct(w[k].shape, F32) for k in names]
    outs = pl.pallas_call(body, name="adamw_small_shards", out_shape=shapes * 3, compiler_params=_cparams())(
        *[g[k] for k in names], *[w[k] for k in names], *[m[k] for k in names], *[v[k] for k in names])
    return [dict(zip(names, outs[j * n:(j + 1) * n])) for j in range(3)]


_ANY = pl.BlockSpec(memory_space=pl.ANY)


def _place():
    return lax.axis_index("x"), lax.axis_index("y"), lax.axis_index("c")


def _all_gather(blocks, second, name):
    n = len(blocks)

    def body(*refs):
        x_refs, out_refs, (send_sems, recv_sems, local_sems) = refs[:n], refs[n:2 * n], refs[2 * n:]
        x, y, c = _place()
        me, sibling = (x, y, c), (x, y, 1 - c)
        chips = [(1 - x, y), (x, 1 - y), (1 - x, 1 - y)]

        def slot(j, px, py, pc):
            idx = 4 * px + 2 * py + pc
            return out_refs[j].at[:, idx] if second[j] else out_refs[j].at[idx]

        def copies(k, block, to, own=False):
            return [pltpu.make_async_remote_copy(
                src_ref=x_refs[j] if own else slot(j, *block), dst_ref=slot(j, *block),
                send_sem=send_sems.at[j, k], recv_sem=recv_sems.at[j, k], device_id=to,
                device_id_type=pl.DeviceIdType.MESH) for j in range(n)]

        mine = [pltpu.make_async_copy(x_refs[j], slot(j, *me), local_sems.at[j]) for j in range(n)]
        for cp in mine:
            cp.start()
        first = copies(0, me, sibling, own=True)
        for j, chip in enumerate(chips):
            first += copies(1 + j, me, (*chip, c), own=True)
        for cp in first:
            cp.start()
        passed = []
        for j, chip in enumerate(chips):
            for cp in copies(1 + j, (*chip, c), me):
                cp.wait_recv()
            onward = copies(4 + j, (*chip, c), sibling)
            for cp in onward:
                cp.start()
            passed += onward
        for cp in copies(0, sibling, me):
            cp.wait_recv()
        for j, chip in enumerate(chips):
            for cp in copies(4 + j, (*chip, 1 - c), me):
                cp.wait_recv()
        for cp in first + passed:
            cp.wait_send()
        for cp in mine:
            cp.wait()

    def out_shape(a, sec):
        shape = (a.shape[0], N_DEV) + a.shape[1:] if sec else (N_DEV,) + a.shape
        return jax.ShapeDtypeStruct(shape, a.dtype)

    return pl.pallas_call(
        body, name=name, out_shape=[out_shape(a, s) for a, s in zip(blocks, second)],
        in_specs=[_ANY] * n, out_specs=[_ANY] * n,
        scratch_shapes=[pltpu.SemaphoreType.DMA((n, 7)), pltpu.SemaphoreType.DMA((n, 7)),
                        pltpu.SemaphoreType.DMA((n,))])(*blocks)


def _exchange(sends, name):
    flat = [(wi, l, a) for wi, per_layer in enumerate(sends) for l, a in enumerate(per_layer)]
    n, nw = len(flat), len(sends)

    def body(*refs):
        x_refs, out_refs, (send_sems, recv_sems, local_sems) = refs[:n], refs[n:n + nw], refs[n + nw:]
        x, y, c = _place()
        me = 4 * x + 2 * y + c

        def peer(k):
            return (x ^ ((k >> 2) & 1), y ^ ((k >> 1) & 1), c ^ (k & 1))

        def copies(k, landing):
            px, py, pc = peer(k)
            them = 4 * px + 2 * py + pc
            return [pltpu.make_async_remote_copy(
                src_ref=x_refs[j].at[me if landing else them], dst_ref=out_refs[wi].at[l, them if landing else me],
                send_sem=send_sems.at[j, k - 1], recv_sem=recv_sems.at[j, k - 1], device_id=(px, py, pc),
                device_id_type=pl.DeviceIdType.MESH) for j, (wi, l, _) in enumerate(flat)]

        mine = [pltpu.make_async_copy(x_refs[j].at[me], out_refs[wi].at[l, me], local_sems.at[j])
                for j, (wi, l, _) in enumerate(flat)]
        for cp in mine:
            cp.start()
        started = []
        for k in range(1, N_DEV):
            started += copies(k, False)
        for cp in started:
            cp.start()
        for k in range(1, N_DEV):
            for cp in copies(k, True):
                cp.wait_recv()
        for cp in started:
            cp.wait_send()
        for cp in mine:
            cp.wait()

    out_shape = [jax.ShapeDtypeStruct((DEPTH,) + per_layer[0].shape, per_layer[0].dtype) for per_layer in sends]
    return pl.pallas_call(
        body, name=name, out_shape=out_shape, in_specs=[_ANY] * n, out_specs=[_ANY] * nw,
        scratch_shapes=[pltpu.SemaphoreType.DMA((n, 7)), pltpu.SemaphoreType.DMA((n, 7)),
                        pltpu.SemaphoreType.DMA((n,))])(*[a for _, _, a in flat])


def _local_grads(x, target, w):
    dn_params = (w["dn_a_log"], w["dn_dt_bias"], w["dn_norm_w"])
    gla_params = (w["gla_w_gate2"], w["gla_gate_bias"], w["gla_norm_w"])
    sgu_params = (w["sgu_ln_w"], w["sgu_ln_b"], w["sgu_w_spatial"], w["sgu_b_spatial"])
    saved = []
    for l in range(DEPTH):
        h, p = _in_proj(x, w["norm1_w"], w["w_in"], l)
        ya = _sgu_fwd(p, *sgu_params, l)
        yb = _sconv_fwd(p, w["sc_conv_w"], l)
        qkv_c = _qkv_conv_fwd(p, w["dn_conv_w"], l)
        yc, st_c, inv_c = _dn_fwd(qkv_c, p, dn_params, l)
        yd, st_d = _gla_fwd(p, gla_params, l)
        mix, x1, h2, gu, act, x2 = _out_mlp(x, (ya, yb, yc, yd), w["w_out"], w["norm2_w"], w["w_gate_up"], w["w_down"], l)
        saved.append(dict(x=x, h=h, p=p, qkv_c=qkv_c, st_c=st_c, inv_c=inv_c, st_d=st_d, mix=mix, x1=x1, h2=h2, gu=gu,
                          act=act))
        x = x2
    loss, d_final, dx = _loss_head(x, w["final_norm_w"], target)
    large = {k: [None] * DEPTH for k in ("w_in", "w_out", "w_gate_up", "w_down")}
    small = [None] * DEPTH
    for l in reversed(range(DEPTH)):
        s = saved[l]
        p = s["p"]
        g = {}
        dgu, dx1, dmix, g["norm2_w"] = _mlp_out_bwd(dx, s["x1"], s["gu"], w["norm2_w"], w["w_down"], w["w_gate_up"],
                                                    w["w_out"], l)
        large["w_gate_up"][l] = _wgrad_gate_up(s["h2"], dgu)
        large["w_down"][l] = _wgrad_down(s["act"], dx).reshape(N_DEV, FF // N_DEV, D)
        large["w_out"][l] = _matmul_tn(s["mix"], dx1, "wgrad_out", BF16).reshape(N_DEV, D // N_DEV, D)
        d_a, g["sgu_ln_w"], g["sgu_ln_b"], g["sgu_w_spatial"], g["sgu_b_spatial"] = _sgu_bwd(p, dmix, *sgu_params, l)
        d_b, g["sc_conv_w"] = _sconv_bwd(p, dmix, w["sc_conv_w"], l)
        dqkv_c, dz_c, dab, g["dn_a_log"], g["dn_dt_bias"], g["dn_norm_w"] = _dn_bwd(
            s["qkv_c"], p, s["st_c"], s["inv_c"], dmix, dn_params, l)
        d_c, g["dn_conv_w"] = _qkv_conv_bwd(p, dqkv_c, w["dn_conv_w"], l)
        d_d, dz_d, dglr, g["gla_w_gate2"], g["gla_gate_bias"], g["gla_norm_w"] = _gla_bwd(p, s["st_d"], dmix, gla_params, l)
        dp, dx, g["norm1_w"] = _in_proj_bwd((d_c, d_d, d_a, d_b, dz_c, dz_d, dab, dglr), s["x"], dx1, w["norm1_w"],
                                            w["w_in"], l)
        large["w_in"][l] = _scatter_w_in_grad(_matmul_tn(s["h"], dp, "wgrad_in", F32))
        small[l] = g
    return loss[0, 0], dx, large, small, d_final


_ORDER = ("norm1_w", "w_in", "sgu_ln_w", "sgu_ln_b", "sgu_w_spatial", "sgu_b_spatial", "sc_conv_w", "dn_conv_w",
          "dn_a_log", "dn_dt_bias", "dn_norm_w", "gla_w_gate2", "gla_gate_bias", "gla_norm_w", "w_out", "norm2_w",
          "w_gate_up", "w_down", "final_norm_w")
_LARGE = ("w_in", "w_gate_up", "w_out", "w_down")
_LARGE_TILE = {"w_in": 256, "w_gate_up": 256, "w_out": 128, "w_down": 352}


def _gather_cols(g):
    return jnp.transpose(g, (1, 2, 0, 3)).reshape(g.shape[1], g.shape[2], N_DEV * g.shape[3])


def kernel(x, norm1_w, w_in, sgu_ln_w, sgu_ln_b, sgu_w_spatial, sgu_b_spatial, sc_conv_w, dn_conv_w, dn_a_log, dn_dt_bias, dn_norm_w, gla_w_gate2, gla_gate_bias, gla_norm_w, w_out, norm2_w, w_gate_up, w_down, final_norm_w, loss_target, m_norm1_w, m_w_in, m_sgu_ln_w, m_sgu_ln_b, m_sgu_w_spatial, m_sgu_b_spatial, m_sc_conv_w, m_dn_conv_w, m_dn_a_log, m_dn_dt_bias, m_dn_norm_w, m_gla_w_gate2, m_gla_gate_bias, m_gla_norm_w, m_w_out, m_norm2_w, m_w_gate_up, m_w_down, m_final_norm_w, v_norm1_w, v_w_in, v_sgu_ln_w, v_sgu_ln_b, v_sgu_w_spatial, v_sgu_b_spatial, v_sc_conv_w, v_dn_conv_w, v_dn_a_log, v_dn_dt_bias, v_dn_norm_w, v_gla_w_gate2, v_gla_gate_bias, v_gla_norm_w, v_w_out, v_norm2_w, v_w_gate_up, v_w_down, v_final_norm_w):
    args = dict(locals())
    wts = {k: args[k] for k in _ORDER}
    mom = {k: args["m_" + k] for k in _ORDER}
    var = {k: args["v_" + k] for k in _ORDER}
    for d in (wts, mom, var):
        d["final_norm_w"] = d["final_norm_w"][None]
    me = 4 * lax.axis_index("x") + 2 * lax.axis_index("y") + lax.axis_index("c")

    blocks = [wts[k].astype(BF16) for k in _LARGE] + [wts[k] for k in _SHARDED_SMALL]
    got = _all_gather(blocks, [False, False, True, True, False, False, False], "gather_weights")
    full = dict(wts)
    full["w_in"] = _relayout_w_in(got[0])
    full["w_gate_up"] = got[1]
    full["w_out"] = got[2].reshape(DEPTH, D, D)
    full["w_down"] = got[3].reshape(DEPTH, FF, D)
    for k, g in zip(_SHARDED_SMALL, got[4:]):
        full[k] = _gather_cols(g)

    loss, dx, large, small, d_final = _local_grads(x[0], loss_target[0], full)
    loss = lax.psum(loss, ("x", "y", "c"))

    parts = _exchange([large[k] for k in _LARGE], "exchange_grads")
    result = {}
    for k, part in zip(_LARGE, parts):
        outs = _adamw_large(part, wts[k], mom[k], var[k], "adamw_" + k, _LARGE_TILE[k])
        for kind, a in zip(("grad", "delta", "new_m", "new_v"), outs):
            result[kind, k] = a

    slabs = _all_gather([_pack_small(small, d_final)], [False], "gather_small_grads")[0]
    rep = _REPLICATED + ("final_norm_w",)
    outs, summed = _adamw_small(slabs, {k: wts[k] for k in rep}, {k: mom[k] for k in rep}, {k: var[k] for k in rep})
    for kind, d in zip(("grad", "delta", "new_m", "new_v"), outs):
        for k, a in d.items():
            result[kind, k] = a[0] if k == "final_norm_w" else a
    own = {k: lax.dynamic_slice_in_dim(summed[k], me * wts[k].shape[-1], wts[k].shape[-1], axis=2) for k in _SHARDED_SMALL}
    outs = _adamw_shards(own, {k: wts[k] for k in _SHARDED_SMALL}, {k: mom[k] for k in _SHARDED_SMALL},
                         {k: var[k] for k in _SHARDED_SMALL})
    for kind, d in zip(("grad", "delta", "new_m", "new_v"), [own] + outs):
        for k, a in d.items():
            result[kind, k] = a

    return (loss, dx[None], *[result[kind, k] for kind in ("grad", "delta", "new_m", "new_v") for k in _ORDER])
```

```python
import functools

import jax
import jax.numpy as jnp
from jax import lax
from jax.experimental import pallas as pl
from jax.experimental.pallas import tpu as pltpu

F32, BF16 = jnp.float32, jnp.bfloat16
DEPTH = 2
D = 1024
G = 256
NH, HD = 4, 64
FF = 2816
IN_COLS = 3352
P = 3584
EPS = 1e-6
SGU_C, DN_C, GLA_C = 128, 64, 64
N_DEV = 8
IN_SHARD = IN_COLS // N_DEV
GU_SHARD = 2 * FF // N_DEV
N_GATE = N_DEV // 2
LANES = 128
VMEM_LIMIT = 56 * 2 ** 20

_PAD_SEGS = ((1280, 2048, 0),
             (2312, 3080, 0),
             (0, 512, 0),
             (512, 1280, 0),
             (2056, 2312, 0),
             (3096, 3352, 0),
             (2048, 2056, 120),
             (3080, 3096, 112))

ADAM_LR, ADAM_B1, ADAM_B2, ADAM_EPS, ADAM_WD, ADAM_STEP = 0.001, 0.9, 0.999, 1e-08, 0.01, 10


def _cparams(*sem):
    return pltpu.CompilerParams(dimension_semantics=sem, vmem_limit_bytes=VMEM_LIMIT)


def _resident(shape):
    return pl.BlockSpec(shape, lambda *_: (0,) * len(shape), pipeline_mode=pl.Buffered(1))


def _layer(shape, l):
    return pl.BlockSpec((None,) + tuple(shape), lambda *_: (l,) + (0,) * len(shape), pipeline_mode=pl.Buffered(1))


def _acc(shape):
    return pl.BlockSpec(shape, lambda *_: (0,) * len(shape))


def _dg(a, b, ca, cb, precision=None):
    lead = a.ndim - 2
    batch = ((0,), (0,)) if lead else ((), ())
    return lax.dot_general(a, b, (((ca + lead,), (cb + lead,)), batch), preferred_element_type=F32, precision=precision)


def _make_dot(cast, precision):
    def raw(a, b, form):
        ca = 0 if form == "tn" else 1
        cb = 1 if form == "nt" else 0
        return _dg_any(cast(a), cast(b), ca, cb, precision)

    @functools.partial(jax.custom_vjp, nondiff_argnums=(2,))
    def dot(a, b, form):
        return raw(a, b, form)

    def fwd(a, b, form):
        return raw(a, b, form), (a, b)

    def bwd(form, res, g):
        a, b = res
        if form == "nn":
            return raw(g, b, "nt"), raw(a, g, "tn")
        if form == "nt":
            return raw(g, b, "nn"), raw(g, a, "tn")
        return raw(b, g, "nt"), raw(a, g, "nn")

    dot.defvjp(fwd, bwd)
    return dot


def _split(t):
    hi = t.astype(BF16)
    return hi, (t - hi.astype(F32)).astype(BF16)


def _dg3(a, b, ca, cb):
    (ah, al), (bh, bl) = a, b
    return _dg(ah, bh, ca, cb) + (_dg(ah, bl, ca, cb) + _dg(al, bh, ca, cb))


class _Split3:
    pass


def _dg_any(a, b, ca, cb, precision):
    if precision is _Split3:
        return _dg3(_split(a), _split(b), ca, cb)
    return _dg(a, b, ca, cb, precision)


_bdot = _make_dot(lambda t: t.astype(BF16), None)
_hdot = _make_dot(lambda t: t, _Split3)


def _inv_unit_lower(low):
    n = low.shape[-1]
    eye = (lax.broadcasted_iota(jnp.int32, (n, n), 0) == lax.broadcasted_iota(jnp.int32, (n, n), 1)).astype(F32)
    p = -low
    inv = eye + p
    ps = _split(p)
    k = 1
    while 2 * k < n:
        ps = _split(_dg3(ps, ps, 1, 0))
        inv = inv + _dg3(_split(inv), ps, 1, 0)
        k *= 2
    return inv


@jax.custom_vjp
def _unit_lower_solve(low, rhs, inv):
    return _dg3(_split(inv), _split(rhs), 1, 0)


def _uls_fwd(low, rhs, inv):
    sol = _dg3(_split(inv), _split(rhs), 1, 0)
    return sol, (inv, sol)


def _uls_bwd(res, g):
    inv, sol = res
    d_rhs = _dg3(_split(inv), _split(g), 0, 0)
    return -_dg3(_split(d_rhs), _split(sol), 1, 1), d_rhs, jnp.zeros_like(inv)


_unit_lower_solve.defvjp(_uls_fwd, _uls_bwd)


def _sigmoid(x):
    return 1.0 / (1.0 + jnp.exp(-x))


def _softplus(x):
    return jnp.maximum(x, 0.0) + jnp.log(1.0 + jnp.exp(-jnp.maximum(x, -x)))


def _gelu(x):
    return 0.5 * x * (1.0 + jnp.tanh(0.7978845608028654 * (x + 0.044715 * (x * x * x))))


def _tri(n):
    ri = lax.broadcasted_iota(jnp.int32, (n, n), 0)
    ci = lax.broadcasted_iota(jnp.int32, (n, n), 1)
    return ri, ci


def _stack(ts):
    return jnp.concatenate([t[None] for t in ts], axis=0)


def _sgu_chunk(uv, ln_w, ln_b, ws, bs):
    u = _gelu(uv[:, :G])
    v = _gelu(uv[:, G:])
    mu = jnp.mean(v, axis=-1, keepdims=True)
    vc = v - mu
    var = jnp.mean(vc * vc, axis=-1, keepdims=True)
    vn = vc * lax.rsqrt(var + EPS) * ln_w + ln_b
    ri, ci = _tri(SGU_C)
    bs_t = _hdot((ri == ci).astype(F32), bs, "nt")
    mixed = []
    for h in range(NH):
        wm = jnp.where(ri >= ci, ws[h], 0.0)
        mixed.append(_bdot(wm, vn[:, HD * h:HD * (h + 1)], "nn") + bs_t[:, h:h + 1])
    return u * jnp.concatenate(mixed, axis=1)


def _dn_tile(q, k, v, ab, z, state, inv, a_log, dt_bias, nw):
    c = DN_C
    tm = q.shape[0]
    cps = tm // c
    ri, ci = _tri(tm)
    shift = c.bit_length() - 1
    same = jnp.right_shift(ri, shift) == jnp.right_shift(ci, shift)
    g4 = -jnp.exp(a_log) * _softplus(ab[:, 0:NH] + dt_bias)
    beta4 = _sigmoid(ab[:, NH:2 * NH])
    gc4 = _hdot((same & (ri >= ci)).astype(F32), g4, "nn")
    gr4 = _hdot(g4, (same & (ri <= ci)).astype(F32), "tn")
    pairs = [(slice(c * j, c * (j + 1)), h) for j in range(cps) for h in range(NH)]
    heads = lambda t: _stack([t[rows, HD * h:HD * (h + 1)] for rows, h in pairs])
    col = lambda t: _stack([t[rows, h:h + 1] for rows, h in pairs])
    qn, kn, vh = heads(q), heads(k), heads(v)
    qn = qn * lax.rsqrt(jnp.sum(qn * qn, axis=-1, keepdims=True) + EPS) * (HD ** -0.5)
    kn = kn * lax.rsqrt(jnp.sum(kn * kn, axis=-1, keepdims=True) + EPS)
    gc, beta = col(gc4), col(beta4)
    gr = _stack([gr4[h:h + 1, rows] for rows, h in pairs])
    gl = jnp.sum(col(g4), axis=1, keepdims=True)
    r2, c2 = _tri(c)
    decay = jnp.exp(jnp.where(r2 >= c2, gc - gr, -jnp.inf))
    kb = kn * beta
    low = jnp.where(r2 > c2, _bdot(kb, kn, "nt") * decay, 0.0)
    eg = jnp.exp(gc)
    if inv is None:
        inv = _inv_unit_lower(low)
    sol = _unit_lower_solve(low, jnp.concatenate([vh * beta, kb * eg], axis=2), inv)
    u, w = sol[:, :, :HD], sol[:, :, HD:]
    attn = _bdot(qn, kn, "nt") * decay
    qg, kd, cd = qn * eg, kn * jnp.exp(gl - gc), jnp.exp(gl)
    ys = []
    for j in range(cps):
        b = slice(NH * j, NH * (j + 1))
        v_new = u[b] - _bdot(w[b], state, "nn")
        o = _bdot(qg[b], state, "nn") + _bdot(attn[b], v_new, "nn")
        state = state * cd[b] + _bdot(kd[b], v_new, "tn")
        on = o * lax.rsqrt(jnp.mean(o * o, axis=-1, keepdims=True) + EPS) * nw
        ys.append(jnp.concatenate([on[h] for h in range(NH)], axis=1))
    return jnp.concatenate(ys, axis=0) * (z * _sigmoid(z)), state, inv


def _gla_tile(q, k, v, glr, z, state_t, w2, gate_bias, nw):
    c = GLA_C
    tm = q.shape[0]
    cps = tm // c
    ri, ci = _tri(tm)
    shift = c.bit_length() - 1
    same = jnp.right_shift(ri, shift) == jnp.right_shift(ci, shift)
    w2_rows = jnp.concatenate([w2, jnp.zeros((LANES - w2.shape[0], G), F32)], axis=0)
    pre = _bdot(glr, w2_rows, "nn") + gate_bias
    log_a = (jnp.minimum(pre, 0.0) - jnp.log(1.0 + jnp.exp(-jnp.maximum(pre, -pre)))) * (1.0 / 16.0)
    gcum = _hdot((same & (ri >= ci)).astype(F32), log_a, "nn")
    row = lax.broadcasted_iota(jnp.int32, (c, G), 0)
    qs = q * (HD ** -0.5)
    qa, ka, qg, kl, dec = [], [], [], [], []
    for j in range(cps):
        rows = slice(c * j, c * (j + 1))
        gj = gcum[rows]
        g_mid = jnp.sum(jnp.where(row == c // 2, gj, 0.0), axis=0, keepdims=True)
        g_last = jnp.sum(log_a[rows], axis=0, keepdims=True)
        qa.append(qs[rows] * jnp.exp(gj - g_mid))
        ka.append(k[rows] * jnp.exp(g_mid - gj))
        qg.append(qs[rows] * jnp.exp(gj))
        kl.append(k[rows] * jnp.exp(g_last - gj))
        dec.append(jnp.exp(g_last))
    heads = lambda ts: _stack([t[:, HD * h:HD * (h + 1)] for t in ts for h in range(NH)])
    qa, ka, qg, kl, dec = heads(qa), heads(ka), heads(qg), heads(kl), heads(dec)
    vh = heads([v[c * j:c * (j + 1)] for j in range(cps)])
    r2, c2 = _tri(c)
    o_intra = _bdot(jnp.where(r2 >= c2, _bdot(qa, ka, "nt"), 0.0), vh, "nn")
    ys = []
    for j in range(cps):
        b = slice(NH * j, NH * (j + 1))
        o = o_intra[b] + _bdot(qg[b], state_t, "nt")
        state_t = state_t * dec[b] + _bdot(vh[b], kl[b], "tn")
        on = o * lax.rsqrt(jnp.mean(o * o, axis=-1, keepdims=True) + EPS) * nw
        ys.append(jnp.concatenate([on[h] for h in range(NH)], axis=1))
    return jnp.concatenate(ys, axis=0) * (z * _sigmoid(z)), state_t


def _rms(x, nw):
    r = lax.rsqrt(jnp.mean(x * x, axis=-1, keepdims=True) + EPS)
    xh = x * r
    return xh * nw, xh, r


def _rms_bwd(g, xh, r, nw):
    gw = g * nw
    return r * (gw - xh * jnp.mean(gw * xh, axis=-1, keepdims=True)), jnp.sum(g * xh, axis=0, keepdims=True)


def _row(tm, w, blk=0):
    return pl.BlockSpec((tm, w), lambda i: (i, blk))


def _in_proj(x, nw, wp, l, tm=256):
    t = x.shape[0]

    def body(x_ref, nw_ref, w_ref, h_ref, p_ref):
        h = _rms(x_ref[...], nw_ref[l:l + 1, :])[0].astype(BF16)
        h_ref[...] = h
        p_ref[...] = jnp.dot(h, w_ref[...], preferred_element_type=F32)

    return pl.pallas_call(
        body, name="in_proj", grid=(t // tm,),
        in_specs=[_row(tm, D), _resident((DEPTH, D)), _layer((D, P), l)],
        out_specs=[_row(tm, D), _row(tm, P)],
        out_shape=[jax.ShapeDtypeStruct((t, D), BF16), jax.ShapeDtypeStruct((t, P), F32)],
        compiler_params=_cparams("parallel"))(x, nw, wp)


def _gu_spec(l):
    return pl.BlockSpec((N_DEV, None, D, GU_SHARD), lambda *_: (0, l, 0, 0), pipeline_mode=pl.Buffered(1))


def _out_mlp(x, ys, w_out, nw2, w_gu, w_down, l, tm=256):
    t = x.shape[0]

    def body(x_ref, ya, yb, yc, yd, wo_ref, nw_ref, wgu_ref, wd_ref, mix_ref, x1_ref, h2_ref, gu_ref, act_ref, x2_ref):
        mix = jnp.concatenate([ya[...], yb[...], yc[...], yd[...]], axis=1)
        mix_ref[...] = mix
        x1 = x_ref[...] + jnp.dot(mix, wo_ref[...], preferred_element_type=F32)
        x1_ref[...] = x1
        h2 = _rms(x1, nw_ref[l:l + 1, :])[0].astype(BF16)
        h2_ref[...] = h2
        x2 = x1
        for d in range(N_GATE):
            gate = jnp.dot(h2, wgu_ref[d], preferred_element_type=F32)
            up = jnp.dot(h2, wgu_ref[d + N_GATE], preferred_element_type=F32)
            gu_ref[d] = gate.astype(BF16)
            gu_ref[d + N_GATE] = up.astype(BF16)
            act = (gate * _sigmoid(gate) * up).astype(BF16)
            act_ref[d] = act
            x2 = x2 + jnp.dot(act, wd_ref[GU_SHARD * d:GU_SHARD * (d + 1), :], preferred_element_type=F32)
        x2_ref[...] = x2

    dev = lambda n: pl.BlockSpec((n, tm, GU_SHARD), lambda i: (0, i, 0))
    return pl.pallas_call(
        body, name="out_mlp", grid=(t // tm,),
        in_specs=[_row(tm, D), _row(tm, G), _row(tm, G), _row(tm, G), _row(tm, G), _layer((D, D), l),
                  _resident((DEPTH, D)), _gu_spec(l), _layer((FF, D), l)],
        out_specs=[_row(tm, D), _row(tm, D), _row(tm, D), dev(N_DEV), dev(N_GATE), _row(tm, D)],
        out_shape=[jax.ShapeDtypeStruct((t, D), BF16), jax.ShapeDtypeStruct((t, D), F32),
                   jax.ShapeDtypeStruct((t, D), BF16), jax.ShapeDtypeStruct((N_DEV, t, GU_SHARD), BF16),
                   jax.ShapeDtypeStruct((N_GATE, t, GU_SHARD), BF16), jax.ShapeDtypeStruct((t, D), F32)],
        compiler_params=_cparams("parallel"))(x, *ys, w_out, nw2, w_gu, w_down)


def _loss_head(x, nw, target, tm=512):
    t = x.shape[0]

    def body(x_ref, nw_ref, tg_ref, loss_ref, dnw_ref, dx_ref):
        @pl.when(pl.program_id(0) == 0)
        def _():
            loss_ref[...] = jnp.zeros_like(loss_ref)
            dnw_ref[...] = jnp.zeros_like(dnw_ref)
        nw_v = nw_ref[...]
        y, xh, r = _rms(x_ref[...], nw_v)
        err = y - tg_ref[...]
        loss_ref[...] += 0.5 * jnp.sum(err * err) * (1.0 / D)
        dx, dnw = _rms_bwd(err * (1.0 / D), xh, r, nw_v)
        dx_ref[...] = dx
        dnw_ref[...] += dnw

    return pl.pallas_call(
        body, name="loss_head", grid=(t // tm,),
        in_specs=[_row(tm, D), _resident((1, D)), _row(tm, D)],
        out_specs=[_acc((8, LANES)), _acc((1, D)), _row(tm, D)],
        out_shape=[jax.ShapeDtypeStruct((8, LANES), F32), jax.ShapeDtypeStruct((1, D), F32),
                   jax.ShapeDtypeStruct((t, D), F32)],
        compiler_params=_cparams("arbitrary"))(x, nw, target)


def _mlp_out_bwd(dx2, x1, gu, nw2, w_down, w_gu, w_out, l, tm=256):
    t = dx2.shape[0]

    def body(dx2_ref, x1_ref, gu_ref, nw_ref, wd_ref, wgu_ref, wo_ref, dgu_ref, dx1_ref, dmix_ref, dnw_ref):
        @pl.when(pl.program_id(0) == 0)
        def _():
            dnw_ref[...] = jnp.zeros_like(dnw_ref)
        dx2 = dx2_ref[...]
        dx2_b = dx2.astype(BF16)
        dh2 = jnp.zeros((tm, D), F32)
        for d in range(N_GATE):
            dact = _dg(dx2_b, wd_ref[GU_SHARD * d:GU_SHARD * (d + 1), :], 1, 1)
            gate, up = gu_ref[d].astype(F32), gu_ref[d + N_GATE].astype(F32)
            s = _sigmoid(gate)
            dgate = (dact * up * (s * (1.0 + gate * (1.0 - s)))).astype(BF16)
            dup = (dact * (gate * s)).astype(BF16)
            dgu_ref[d] = dgate
            dgu_ref[d + N_GATE] = dup
            dh2 = dh2 + _dg(dgate, wgu_ref[d], 1, 1) + _dg(dup, wgu_ref[d + N_GATE], 1, 1)
        nw_v = nw_ref[l:l + 1, :]
        _, xh, r = _rms(x1_ref[...], nw_v)
        dxn, dnw = _rms_bwd(dh2, xh, r, nw_v)
        dx1 = dx2 + dxn
        dx1_ref[...] = dx1
        dnw_ref[...] += dnw
        dmix_ref[...] = _dg(dx1.astype(BF16), wo_ref[...], 1, 1)

    dev = pl.BlockSpec((N_DEV, tm, GU_SHARD), lambda i: (0, i, 0))
    return pl.pallas_call(
        body, name="mlp_out_bwd", grid=(t // tm,),
        in_specs=[_row(tm, D), _row(tm, D), dev, _resident((DEPTH, D)), _layer((FF, D), l), _gu_spec(l),
                  _layer((D, D), l)],
        out_specs=[dev, _row(tm, D), _row(tm, D), _acc((1, D))],
        out_shape=[jax.ShapeDtypeStruct((N_DEV, t, GU_SHARD), BF16), jax.ShapeDtypeStruct((t, D), F32),
                   jax.ShapeDtypeStruct((t, D), F32), jax.ShapeDtypeStruct((1, D), F32)],
        compiler_params=_cparams("arbitrary"))(dx2, x1, gu, nw2, w_down, w_gu, w_out)


_DP_WIDTHS = (768, 768, 512, 768, 256, 256, 128, 128)


def _in_proj_bwd(dps, x, dx1, nw, wp, l, tm=256):
    t = x.shape[0]

    def body(*refs):
        dp_refs, (x_ref, dx1_ref, nw_ref, w_ref, dp_ref, dx_ref, dnw_ref) = refs[:8], refs[8:]

        @pl.when(pl.program_id(0) == 0)
        def _():
            dnw_ref[...] = jnp.zeros_like(dnw_ref)
        dp = jnp.concatenate([r[...] for r in dp_refs], axis=1)
        dp_ref[...] = dp
        dh = _dg(dp, w_ref[...], 1, 1)
        nw_v = nw_ref[l:l + 1, :]
        _, xh, r = _rms(x_ref[...], nw_v)
        dxn, dnw = _rms_bwd(dh, xh, r, nw_v)
        dx_ref[...] = dx1_ref[...] + dxn
        dnw_ref[...] += dnw

    return pl.pallas_call(
        body, name="in_proj_bwd", grid=(t // tm,),
        in_specs=[_row(tm, w) for w in _DP_WIDTHS] + [_row(tm, D), _row(tm, D), _resident((DEPTH, D)), _layer((D, P), l)],
        out_specs=[_row(tm, P), _row(tm, D), _acc((1, D))],
        out_shape=[jax.ShapeDtypeStruct((t, P), BF16), jax.ShapeDtypeStruct((t, D), F32),
                   jax.ShapeDtypeStruct((1, D), F32)],
        compiler_params=_cparams("arbitrary"))(*dps, x, dx1, nw, wp)


def _accumulate(acc, o_ref, t_axis, term):
    @pl.when(pl.program_id(t_axis) == 0)
    def _():
        acc[...] = jnp.zeros_like(acc)
    acc[...] += term

    @pl.when(pl.program_id(t_axis) == pl.num_programs(t_axis) - 1)
    def _():
        o_ref[...] = acc[...].astype(o_ref.dtype)


WGRAD_TOKENS = 2048


def _matmul_tn(a, b, name, out_dtype, tn=512, tt=WGRAD_TOKENS):
    t, m = a.shape
    n = b.shape[1]
    tt = min(tt, t)

    def body(a_ref, b_ref, o_ref, acc):
        _accumulate(acc, o_ref, 1, _dg(a_ref[...].astype(BF16), b_ref[...].astype(BF16), 0, 0))

    return pl.pallas_call(
        body, name=name, grid=(n // tn, t // tt),
        in_specs=[pl.BlockSpec((tt, m), lambda j, i: (i, 0)), pl.BlockSpec((tt, tn), lambda j, i: (i, j))],
        out_specs=pl.BlockSpec((m, tn), lambda j, i: (0, j)),
        out_shape=jax.ShapeDtypeStruct((m, n), out_dtype),
        scratch_shapes=[pltpu.VMEM((m, tn), F32)],
        compiler_params=_cparams("parallel", "arbitrary"))(a, b)


def _wgrad_gate_up(h2, dgu, tt=WGRAD_TOKENS):
    t = h2.shape[0]
    tt = min(tt, t)

    def body(a_ref, b_ref, o_ref, acc):
        _accumulate(acc, o_ref, 1, _dg(a_ref[...], b_ref[...], 0, 0))

    return pl.pallas_call(
        body, name="wgrad_gate_up", grid=(N_DEV, t // tt),
        in_specs=[pl.BlockSpec((tt, D), lambda d, i: (i, 0)), pl.BlockSpec((None, tt, GU_SHARD), lambda d, i: (d, i, 0))],
        out_specs=pl.BlockSpec((None, D, GU_SHARD), lambda d, i: (d, 0, 0)),
        out_shape=jax.ShapeDtypeStruct((N_DEV, D, GU_SHARD), BF16),
        scratch_shapes=[pltpu.VMEM((D, GU_SHARD), F32)],
        compiler_params=_cparams("parallel", "arbitrary"))(h2, dgu)


def _wgrad_down(act, dx2, tt=WGRAD_TOKENS):
    t = dx2.shape[0]
    tt = min(tt, t)

    def body(a_ref, b_ref, o_ref, acc):
        _accumulate(acc, o_ref, 1, _dg(a_ref[...], b_ref[...].astype(BF16), 0, 0))

    return pl.pallas_call(
        body, name="wgrad_down", grid=(N_GATE, t // tt),
        in_specs=[pl.BlockSpec((None, tt, GU_SHARD), lambda d, i: (d, i, 0)), pl.BlockSpec((tt, D), lambda d, i: (i, 0))],
        out_specs=pl.BlockSpec((GU_SHARD, D), lambda d, i: (d, 0)),
        out_shape=jax.ShapeDtypeStruct((FF, D), BF16),
        scratch_shapes=[pltpu.VMEM((GU_SHARD, D), F32)],
        compiler_params=_cparams("parallel", "arbitrary"))(act, dx2)


def _relayout_w_in(g, tr=256):
    def body(w_ref, o_ref):
        full = jnp.concatenate([w_ref[d] for d in range(N_DEV)], axis=1)
        parts = []
        for a, b, z in _PAD_SEGS:
            parts.append(full[:, a:b])
            if z:
                parts.append(jnp.zeros((tr, z), full.dtype))
        o_ref[...] = jnp.concatenate(parts, axis=1)

    return pl.pallas_call(
        body, name="relayout_w_in", grid=(DEPTH, D // tr),
        in_specs=[pl.BlockSpec((N_DEV, None, tr, IN_SHARD), lambda l, i: (0, l, i, 0))],
        out_specs=pl.BlockSpec((None, tr, P), lambda l, i: (l, i, 0)),
        out_shape=jax.ShapeDtypeStruct((DEPTH, D, P), g.dtype),
        compiler_params=_cparams("parallel", "parallel"))(g)


def _scatter_w_in_grad(gp, tr=256):
    def body(g_ref, o_ref):
        g = g_ref[...]
        pieces, off = [], 0
        for a, b, z in _PAD_SEGS:
            pieces.append((a, g[:, off:off + (b - a)]))
            off += (b - a) + z
        nat = jnp.concatenate([piece for _, piece in sorted(pieces, key=lambda ap: ap[0])], axis=1)
        for d in range(N_DEV):
            o_ref[d] = nat[:, IN_SHARD * d:IN_SHARD * (d + 1)].astype(BF16)

    return pl.pallas_call(
        body, name="scatter_w_in_grad", grid=(D // tr,),
        in_specs=[pl.BlockSpec((tr, P), lambda i: (i, 0))],
        out_specs=pl.BlockSpec((N_DEV, tr, IN_SHARD), lambda i: (0, i, 0)),
        out_shape=jax.ShapeDtypeStruct((N_DEV, D, IN_SHARD), BF16),
        compiler_params=_cparams("parallel"))(gp)


_A_BLK = 3


def _sgu_specs(l):
    return [_resident((DEPTH, G)), _resident((DEPTH, G)), _layer((NH, SGU_C, SGU_C), l), _layer((NH, SGU_C), l)]


def _sgu_fwd(p, ln_w, ln_b, ws, bs, l, tm=256):
    t = p.shape[0]

    def body(uv_ref, lw_ref, lb_ref, ws_ref, bs_ref, y_ref):
        ws_v = [ws_ref[h] for h in range(NH)]
        for c in range(tm // SGU_C):
            rows = pl.ds(c * SGU_C, SGU_C)
            y_ref[rows, :] = _sgu_chunk(uv_ref[rows, :], lw_ref[l:l + 1, :], lb_ref[l:l + 1, :], ws_v,
                                        bs_ref[...]).astype(BF16)

    return pl.pallas_call(
        body, name="sgu_fwd", grid=(t // tm,),
        in_specs=[_row(tm, 2 * G, _A_BLK)] + _sgu_specs(l), out_specs=_row(tm, G),
        out_shape=jax.ShapeDtypeStruct((t, G), BF16),
        compiler_params=_cparams("parallel"))(p, ln_w, ln_b, ws, bs)


def _sgu_bwd(p, dmix, ln_w, ln_b, ws, bs, l, tm=256):
    t = p.shape[0]

    def body(uv_ref, dy_ref, lw_ref, lb_ref, ws_ref, bs_ref, duv_ref, dlw_ref, dlb_ref, dws_ref, dbs_ref):
        @pl.when(pl.program_id(0) == 0)
        def _():
            for r in (dlw_ref, dlb_ref, dws_ref, dbs_ref):
                r[...] = jnp.zeros_like(r)
        ws_v = [ws_ref[h] for h in range(NH)]
        for c in range(tm // SGU_C):
            rows = pl.ds(c * SGU_C, SGU_C)
            _, vjp = jax.vjp(_sgu_chunk, uv_ref[rows, :], lw_ref[l:l + 1, :], lb_ref[l:l + 1, :], ws_v, bs_ref[...])
            duv, dlw, dlb, dws, dbs = vjp(dy_ref[rows, :])
            duv_ref[rows, :] = duv.astype(BF16)
            dlw_ref[...] += dlw
            dlb_ref[...] += dlb
            dbs_ref[...] += dbs
            for h in range(NH):
                dws_ref[h] += dws[h]

    return pl.pallas_call(
        body, name="sgu_bwd", grid=(t // tm,),
        in_specs=[_row(tm, 2 * G, _A_BLK), _row(tm, G, 0)] + _sgu_specs(l),
        out_specs=[_row(tm, 2 * G), _acc((1, G)), _acc((1, G)), _acc((NH, SGU_C, SGU_C)), _acc((NH, SGU_C))],
        out_shape=[jax.ShapeDtypeStruct((t, 2 * G), BF16), jax.ShapeDtypeStruct((1, G), F32),
                   jax.ShapeDtypeStruct((1, G), F32), jax.ShapeDtypeStruct((NH, SGU_C, SGU_C), F32),
                   jax.ShapeDtypeStruct((NH, SGU_C), F32)],
        compiler_params=_cparams("arbitrary"))(p, dmix, ln_w, ln_b, ws, bs)


HALO = 8


def _prev_halo(tm, width, col):
    return pl.BlockSpec((HALO, width), lambda i: (jnp.maximum(i * (tm // HALO) - 1, 0), col))


def _next_halo(tm, width, col, t):
    return pl.BlockSpec((HALO, width), lambda i: (jnp.minimum((i + 1) * (tm // HALO), t // HALO - 1), col))


def _with_prev(halo_ref, x_ref):
    halo = jnp.where(pl.program_id(0) == 0, 0.0, halo_ref[...])
    return jnp.concatenate([halo, x_ref[...]], axis=0)


def _with_next(x_ref, halo_ref):
    halo = jnp.where(pl.program_id(0) == pl.num_programs(0) - 1, 0.0, halo_ref[...])
    return jnp.concatenate([x_ref[...], halo], axis=0)


def _delay(ext, j):
    return ext if j == 0 else pltpu.roll(ext, j, axis=0)


def _advance(ext, j):
    return ext if j == 0 else pltpu.roll(ext, ext.shape[0] - j, axis=0)


def _causal_conv(ext, w, tm):
    kw = w.shape[0]
    out = None
    for k in range(kw):
        term = _delay(ext, kw - 1 - k)[HALO:HALO + tm] * w[k:k + 1, :]
        out = term if out is None else out + term
    return out


def _causal_conv_t(ext, w, tm):
    kw = w.shape[0]
    out = None
    for k in range(kw):
        term = _advance(ext, kw - 1 - k)[0:tm] * w[k:k + 1, :]
        out = term if out is None else out + term
    return out


def _conv_wgrad(ext, dy, kw, tm):
    return jnp.concatenate(
        [jnp.sum(_delay(ext, kw - 1 - k)[HALO:HALO + tm] * dy, axis=0, keepdims=True) for k in range(kw)], axis=0)


_B_GB, _B_GC, _B_H = 8, 9, 10
_C_QKV, _D_QKV = 0, 1
_C_Z, _D_Z = 11, 12
_C_AB, _D_GLR = 26, 27


def _sconv_fwd(p, w, l, tm=512):
    t = p.shape[0]

    def body(gb_ref, gc_ref, h_ref, gc_halo, h_halo, w_ref, y_ref):
        s_ext = _with_prev(gc_halo, gc_ref) * _with_prev(h_halo, h_ref)
        y_ref[...] = (gb_ref[...] * _causal_conv(s_ext, w_ref[...], tm)).astype(BF16)

    return pl.pallas_call(
        body, name="sconv_fwd", grid=(t // tm,),
        in_specs=[_row(tm, G, _B_GB), _row(tm, G, _B_GC), _row(tm, G, _B_H), _prev_halo(tm, G, _B_GC),
                  _prev_halo(tm, G, _B_H), _layer((3, G), l)],
        out_specs=_row(tm, G), out_shape=jax.ShapeDtypeStruct((t, G), BF16),
        compiler_params=_cparams("parallel"))(p, p, p, p, p, w)


def _sconv_bwd(p, dmix, w, l, tm=512):
    t = p.shape[0]

    def body(gb_ref, gc_ref, h_ref, gc_halo, h_halo, gb_next, dy_ref, dy_next, w_ref, dp_ref, dw_ref):
        @pl.when(pl.program_id(0) == 0)
        def _():
            dw_ref[...] = jnp.zeros_like(dw_ref)
        w_v = w_ref[...]
        s_ext = _with_prev(gc_halo, gc_ref) * _with_prev(h_halo, h_ref)
        dout = dy_ref[...]
        d_gb = dout * _causal_conv(s_ext, w_v, tm)
        dconv_ext = _with_next(dy_ref, dy_next) * _with_next(gb_ref, gb_next)
        ds = _causal_conv_t(dconv_ext, w_v, tm)
        dw_ref[...] += _conv_wgrad(s_ext, dconv_ext[0:tm], 3, tm)
        dp_ref[...] = jnp.concatenate([d_gb, ds * h_ref[...], ds * gc_ref[...]], axis=1).astype(BF16)

    return pl.pallas_call(
        body, name="sconv_bwd", grid=(t // tm,),
        in_specs=[_row(tm, G, _B_GB), _row(tm, G, _B_GC), _row(tm, G, _B_H), _prev_halo(tm, G, _B_GC),
                  _prev_halo(tm, G, _B_H), _next_halo(tm, G, _B_GB, t), _row(tm, G, 1), _next_halo(tm, G, 1, t),
                  _layer((3, G), l)],
        out_specs=[_row(tm, 3 * G), _acc((3, G))],
        out_shape=[jax.ShapeDtypeStruct((t, 3 * G), BF16), jax.ShapeDtypeStruct((3, G), F32)],
        compiler_params=_cparams("arbitrary"))(p, p, p, p, p, p, dmix, dmix, w)


def _qkv_conv_fwd(p, w, l, tm=512):
    t = p.shape[0]

    def body(x_ref, x_halo, w_ref, y_ref):
        c = _causal_conv(_with_prev(x_halo, x_ref), w_ref[...], tm)
        y_ref[...] = c * _sigmoid(c)

    return pl.pallas_call(
        body, name="qkv_conv_fwd", grid=(t // tm,),
        in_specs=[_row(tm, 3 * G, _C_QKV), _prev_halo(tm, 3 * G, _C_QKV), _layer((4, 3 * G), l)],
        out_specs=_row(tm, 3 * G), out_shape=jax.ShapeDtypeStruct((t, 3 * G), F32),
        compiler_params=_cparams("parallel"))(p, p, w)


def _qkv_conv_bwd(p, dy, w, l, tm=512):
    t = p.shape[0]

    def body(x_ref, x_halo, x_next, dy_ref, dy_next, w_ref, dx_ref, dw_ref):
        @pl.when(pl.program_id(0) == 0)
        def _():
            dw_ref[...] = jnp.zeros_like(dw_ref)
        w_v = w_ref[...]
        x_all = jnp.concatenate([_with_prev(x_halo, x_ref), jnp.where(
            pl.program_id(0) == pl.num_programs(0) - 1, 0.0, x_next[...])], axis=0)
        c = _causal_conv(x_all, w_v, tm + HALO)
        s = _sigmoid(c)
        dc_ext = _with_next(dy_ref, dy_next) * (s * (1.0 + c * (1.0 - s)))
        dx_ref[...] = _causal_conv_t(dc_ext, w_v, tm).astype(BF16)
        dw_ref[...] += _conv_wgrad(x_all[0:HALO + tm], dc_ext[0:tm], 4, tm)

    return pl.pallas_call(
        body, name="qkv_conv_bwd", grid=(t // tm,),
        in_specs=[_row(tm, 3 * G, _C_QKV), _prev_halo(tm, 3 * G, _C_QKV), _next_halo(tm, 3 * G, _C_QKV, t),
                  _row(tm, 3 * G), _next_halo(tm, 3 * G, 0, t), _layer((4, 3 * G), l)],
        out_specs=[_row(tm, 3 * G), _acc((4, 3 * G))],
        out_shape=[jax.ShapeDtypeStruct((t, 3 * G), BF16), jax.ShapeDtypeStruct((4, 3 * G), F32)],
        compiler_params=_cparams("arbitrary"))(p, p, p, dy, dy, w)


_STATE = pl.BlockSpec((1, NH, HD, HD), lambda i: (i, 0, 0, 0))


def _dn_specs():
    return [_resident((DEPTH, NH)), _resident((DEPTH, NH)), _resident((DEPTH, HD))]


def _dn_fwd(qkv, p, params, l, cps=4, rider=None):
    t = qkv.shape[0]
    tm = DN_C * cps
    nb = cps * NH

    def body(*refs):
        (qkv_ref, z_ref, ab_ref, alog_ref, dt_ref, nw_ref, y_ref, st_ref, inv_ref, state), ride = _split_refs(
            refs, 6, 3, 1, rider)
        _ride_begin(rider, ride)

        @pl.when(pl.program_id(0) == 0)
        def _():
            state[...] = jnp.zeros_like(state)
        st_ref[0] = state[...]
        y, new, inv = _dn_tile(qkv_ref[:, 0:G], qkv_ref[:, G:2 * G], qkv_ref[:, 2 * G:3 * G], ab_ref[...], z_ref[...],
                               state[...], None, alog_ref[l:l + 1, :], dt_ref[l:l + 1, :], nw_ref[l:l + 1, :])
        y_ref[...] = y.astype(BF16)
        inv_ref[...] = inv
        state[...] = new
        _ride_end(rider, ride)

    specs, operands = _host(
        rider, [_row(tm, 3 * G), _row(tm, G, _C_Z), _row(tm, LANES, _C_AB)] + _dn_specs(),
        [_row(tm, G), _STATE, pl.BlockSpec((nb, DN_C, DN_C), lambda i: (i, 0, 0))],
        [jax.ShapeDtypeStruct((t, G), BF16), jax.ShapeDtypeStruct((t // tm, NH, HD, HD), F32),
         jax.ShapeDtypeStruct((t // DN_C * NH, DN_C, DN_C), F32)],
        [pltpu.VMEM((NH, HD, HD), F32)], [qkv, p, p, *params])
    return pl.pallas_call(body, name="dn_fwd", grid=(t // tm,), compiler_params=_cparams("arbitrary"), **specs)(*operands)


def _dn_bwd(qkv, p, states, inv, dmix, params, l, cps=4, rider=None):
    t = qkv.shape[0]
    tm = DN_C * cps
    n = t // tm
    nb = cps * NH

    def body(*refs):
        (qkv_ref, z_ref, ab_ref, st_ref, inv_ref, dy_ref, alog_ref, dt_ref, nw_ref,
         dqkv_ref, dz_ref, dab_ref, dalog_ref, ddt_ref, dnw_ref, dstate), ride = _split_refs(refs, 9, 6, 1, rider)
        _ride_begin(rider, ride)
        dprm_refs = (dalog_ref, ddt_ref, dnw_ref)

        @pl.when(pl.program_id(0) == 0)
        def _():
            dstate[...] = jnp.zeros_like(dstate)
            for r in dprm_refs:
                r[...] = jnp.zeros_like(r)
        inv_v = inv_ref[...]
        tile = lambda q, k, v, ab, z, st, alog, dt, nw: _dn_tile(q, k, v, ab, z, st, inv_v, alog, dt, nw)[:2]
        _, vjp = jax.vjp(tile, qkv_ref[:, 0:G], qkv_ref[:, G:2 * G], qkv_ref[:, 2 * G:3 * G], ab_ref[...], z_ref[...],
                         st_ref[0], alog_ref[l:l + 1, :], dt_ref[l:l + 1, :], nw_ref[l:l + 1, :])
        dq, dk, dv, dab, dz, dst, *dprm = vjp((dy_ref[...], dstate[...]))
        dqkv_ref[...] = jnp.concatenate([dq, dk, dv], axis=1)
        dz_ref[...] = dz.astype(BF16)
        dab_ref[...] = dab.astype(BF16)
        dstate[...] = dst
        for r, g in zip(dprm_refs, dprm):
            r[...] += g
        _ride_end(rider, ride)

    rev = lambda w, blk: pl.BlockSpec((tm, w), lambda i: (n - 1 - i, blk))
    specs, operands = _host(
        rider, [rev(3 * G, 0), rev(G, _C_Z), rev(LANES, _C_AB),
                pl.BlockSpec((1, NH, HD, HD), lambda i: (n - 1 - i, 0, 0, 0)),
                pl.BlockSpec((nb, DN_C, DN_C), lambda i: (n - 1 - i, 0, 0)), rev(G, 2)] + _dn_specs(),
        [rev(3 * G, 0), rev(G, 0), rev(LANES, 0), _acc((1, NH)), _acc((1, NH)), _acc((1, HD))],
        [jax.ShapeDtypeStruct((t, 3 * G), F32), jax.ShapeDtypeStruct((t, G), BF16),
         jax.ShapeDtypeStruct((t, LANES), BF16), jax.ShapeDtypeStruct((1, NH), F32),
         jax.ShapeDtypeStruct((1, NH), F32), jax.ShapeDtypeStruct((1, HD), F32)],
        [pltpu.VMEM((NH, HD, HD), F32)], [qkv, p, p, states, inv, dmix, *params])
    return pl.pallas_call(body, name="dn_bwd", grid=(n,), compiler_params=_cparams("arbitrary"), **specs)(*operands)


def _gla_specs(l):
    return [_layer((16, G), l), _resident((DEPTH, G)), _resident((DEPTH, HD))]


def _gla_fwd(p, params, l, cps=4):
    t = p.shape[0]
    tm = GLA_C * cps

    def body(qkv_ref, z_ref, glr_ref, w2_ref, gb_ref, nw_ref, y_ref, st_ref, state):
        @pl.when(pl.program_id(0) == 0)
        def _():
            state[...] = jnp.zeros_like(state)
        st_ref[0] = state[...]
        y, new = _gla_tile(qkv_ref[:, 0:G], qkv_ref[:, G:2 * G], qkv_ref[:, 2 * G:3 * G], glr_ref[...], z_ref[...],
                           state[...], w2_ref[...], gb_ref[l:l + 1, :], nw_ref[l:l + 1, :])
        y_ref[...] = y.astype(BF16)
        state[...] = new

    return pl.pallas_call(
        body, name="gla_fwd", grid=(t // tm,),
        in_specs=[_row(tm, 3 * G, _D_QKV), _row(tm, G, _D_Z), _row(tm, LANES, _D_GLR)] + _gla_specs(l),
        out_specs=[_row(tm, G), _STATE],
        out_shape=[jax.ShapeDtypeStruct((t, G), BF16), jax.ShapeDtypeStruct((t // tm, NH, HD, HD), F32)],
        scratch_shapes=[pltpu.VMEM((NH, HD, HD), F32)],
        compiler_params=_cparams("arbitrary"))(p, p, p, *params)


def _gla_bwd(p, states, dmix, params, l, cps=4):
    t = p.shape[0]
    tm = GLA_C * cps
    n = t // tm

    def body(qkv_ref, z_ref, glr_ref, st_ref, dy_ref, w2_ref, gb_ref, nw_ref,
             dqkv_ref, dz_ref, dglr_ref, dw2_ref, dgb_ref, dnw_ref, dstate):
        dprm_refs = (dw2_ref, dgb_ref, dnw_ref)

        @pl.when(pl.program_id(0) == 0)
        def _():
            dstate[...] = jnp.zeros_like(dstate)
            for r in dprm_refs:
                r[...] = jnp.zeros_like(r)
        _, vjp = jax.vjp(_gla_tile, qkv_ref[:, 0:G], qkv_ref[:, G:2 * G], qkv_ref[:, 2 * G:3 * G], glr_ref[...],
                         z_ref[...], st_ref[0], w2_ref[...], gb_ref[l:l + 1, :], nw_ref[l:l + 1, :])
        dq, dk, dv, dglr, dz, dst, *dprm = vjp((dy_ref[...], dstate[...]))
        dqkv_ref[...] = jnp.concatenate([dq, dk, dv], axis=1).astype(BF16)
        dz_ref[...] = dz.astype(BF16)
        dglr_ref[...] = dglr.astype(BF16)
        dstate[...] = dst
        for r, g in zip(dprm_refs, dprm):
            r[...] += g

    rev = lambda w, blk: pl.BlockSpec((tm, w), lambda i: (n - 1 - i, blk))
    return pl.pallas_call(
        body, name="gla_bwd", grid=(n,),
        in_specs=[rev(3 * G, _D_QKV), rev(G, _D_Z), rev(LANES, _D_GLR),
                  pl.BlockSpec((1, NH, HD, HD), lambda i: (n - 1 - i, 0, 0, 0)), rev(G, 3)] + _gla_specs(l),
        out_specs=[rev(3 * G, 0), rev(G, 0), rev(LANES, 0), _acc((16, G)), _acc((1, G)), _acc((1, HD))],
        out_shape=[jax.ShapeDtypeStruct((t, 3 * G), BF16), jax.ShapeDtypeStruct((t, G), BF16),
                   jax.ShapeDtypeStruct((t, LANES), BF16), jax.ShapeDtypeStruct((16, G), F32),
                   jax.ShapeDtypeStruct((1, G), F32), jax.ShapeDtypeStruct((1, HD), F32)],
        scratch_shapes=[pltpu.VMEM((NH, HD, HD), F32)],
        compiler_params=_cparams("arbitrary"))(p, p, p, states, dmix, *params)


def _adamw_math(w, g, m, v):
    m = ADAM_B1 * m + (1.0 - ADAM_B1) * g
    v = ADAM_B2 * v + (1.0 - ADAM_B2) * (g * g)
    m_hat = m / (1.0 - ADAM_B1 ** ADAM_STEP)
    v_hat = v / (1.0 - ADAM_B2 ** ADAM_STEP)
    return -ADAM_LR * (m_hat / (jnp.sqrt(v_hat) + ADAM_EPS) + ADAM_WD * w), m, v


def _adamw_large(parts, w, m, v, name, tr):
    _, r, c = w.shape

    def body(p0_ref, p1_ref, w_ref, m_ref, v_ref, g_ref, d_ref, nm_ref, nv_ref):
        def total(p_ref):
            g = p_ref[0].astype(F32)
            for k in range(1, N_DEV):
                g = g + p_ref[k].astype(F32)
            return g

        g = jnp.where(pl.program_id(0) == 0, total(p0_ref), total(p1_ref))
        g_ref[...] = g
        d_ref[...], nm_ref[...], nv_ref[...] = _adamw_math(w_ref[...], g, m_ref[...], v_ref[...])

    blk = pl.BlockSpec((None, tr, c), lambda l, i: (l, i, 0))
    part = pl.BlockSpec((N_DEV, tr, c), lambda l, i: (0, i, 0))
    return pl.pallas_call(
        body, name=name, grid=(DEPTH, r // tr), in_specs=[part, part, blk, blk, blk],
        out_specs=[blk] * 4, out_shape=[jax.ShapeDtypeStruct(w.shape, F32)] * 4,
        compiler_params=_cparams("parallel", "parallel"))(*parts, w, m, v)


_SLAB_AT = {
    "sgu_w_spatial": (0, 0, SGU_C, LANES),
    "sgu_b_spatial": (128, 0, NH, SGU_C),
    "norm1_w": (132, 0, 1, D),
    "norm2_w": (133, 0, 1, D),
    "sgu_ln_w": (134, 0, 1, G),
    "sgu_ln_b": (134, 256, 1, G),
    "gla_gate_bias": (134, 512, 1, G),
    "dn_norm_w": (134, 768, 1, HD),
    "gla_norm_w": (134, 896, 1, HD),
    "dn_a_log": (135, 0, 1, NH),
    "dn_dt_bias": (135, 128, 1, NH),
    "gla_w_gate2": (136, 0, 16, G),
    "dn_conv_w": (136, 256, 4, 3 * G),
    "sc_conv_w": (140, 256, 3, G),
}
_LAYER_ROWS = 152
_FINAL_ROW = DEPTH * _LAYER_ROWS
_SLAB_ROWS, _SLAB_COLS = _FINAL_ROW + 8, 1024
_SLAB_ORDER = tuple(_SLAB_AT)
_SHARDED_SMALL = ("sc_conv_w", "dn_conv_w", "gla_w_gate2")
_REPLICATED = tuple(k for k in _SLAB_ORDER if k not in _SHARDED_SMALL)


def _region(name, l, h=0):
    r0, c0, nr, nc = _SLAB_AT[name]
    return slice(_LAYER_ROWS * l + r0, _LAYER_ROWS * l + r0 + nr), slice(c0 + nc * h, c0 + nc * (h + 1))


def _pack_small(layer_grads, d_final):
    def body(*refs):
        slab = refs[-1]
        slab[...] = jnp.zeros_like(slab)
        it = iter(refs[:-1])
        for l in range(DEPTH):
            for name in _SLAB_ORDER:
                ref = next(it)
                if name == "sgu_w_spatial":
                    for h in range(NH):
                        slab[_region(name, l, h)] = ref[h]
                else:
                    slab[_region(name, l)] = ref[...]
        slab[_FINAL_ROW:_FINAL_ROW + 1, :] = next(it)[...]

    flat = [layer_grads[l][name] for l in range(DEPTH) for name in _SLAB_ORDER] + [d_final]
    return pl.pallas_call(body, name="pack_small_grads", out_shape=jax.ShapeDtypeStruct((_SLAB_ROWS, _SLAB_COLS), F32),
                          compiler_params=_cparams())(*flat)


def _adamw_small(parts, w, m, v):
    names = _REPLICATED + ("final_norm_w",)
    n_in = len(names)

    def body(*refs):
        def total(rows, cols):
            g = refs[0][0, rows, cols]
            for k in range(1, N_DEV):
                g = g + refs[0][k, rows, cols]
            return g

        w_refs = dict(zip(names, refs[1:1 + n_in]))
        m_refs = dict(zip(names, refs[1 + n_in:1 + 2 * n_in]))
        v_refs = dict(zip(names, refs[1 + 2 * n_in:1 + 3 * n_in]))
        outs = refs[1 + 3 * n_in:]
        out_refs = [dict(zip(names, outs[j * n_in:(j + 1) * n_in])) for j in range(4)]
        full_refs = dict(zip(_SHARDED_SMALL, outs[4 * n_in:]))

        def update(name, idx, g):
            res = (g,) + _adamw_math(w_refs[name][idx], g, m_refs[name][idx], v_refs[name][idx])
            for o, val in zip(out_refs, res):
                o[name][idx] = val

        for l in range(DEPTH):
            for name in _REPLICATED:
                if name == "sgu_w_spatial":
                    for h in range(NH):
                        update(name, (l, h), total(*_region(name, l, h)))
                elif name == "sgu_b_spatial":
                    update(name, (l,), total(*_region(name, l)))
                else:
                    update(name, (slice(l, l + 1), slice(None)), total(*_region(name, l)))
            for name in _SHARDED_SMALL:
                full_refs[name][l] = total(*_region(name, l))
        update("final_norm_w", (slice(None), slice(None)), total(slice(_FINAL_ROW, _FINAL_ROW + 1), slice(None)))

    shapes = [jax.ShapeDtypeStruct(w[k].shape, F32) for k in names]
    full_shapes = [jax.ShapeDtypeStruct((DEPTH,) + _SLAB_AT[k][2:], F32) for k in _SHARDED_SMALL]
    outs = pl.pallas_call(body, name="adamw_small", out_shape=shapes * 4 + full_shapes, compiler_params=_cparams())(
        parts, *[w[k] for k in names], *[m[k] for k in names], *[v[k] for k in names])
    result = [dict(zip(names, outs[j * n_in:(j + 1) * n_in])) for j in range(4)]
    return result, dict(zip(_SHARDED_SMALL, outs[4 * n_in:]))


def _adamw_shards(g, w, m, v):
    names = _SHARDED_SMALL
    n = len(names)

    def body(*refs):
        for j in range(n):
            g_v = refs[j][...]
            res = _adamw_math(refs[n + j][...], g_v, refs[2 * n + j][...], refs[3 * n + j][...])
            for o, val in zip(refs[4 * n + j::n], res):
                o[...] = val

    shapes = [jax.ShapeDtypeStruct(w[k].shape, F32) for k in names]
    outs = pl.pallas_call(body, name="adamw_small_shards", out_shape=shapes * 3, compiler_params=_cparams())(
        *[g[k] for k in names], *[w[k] for k in names], *[m[k] for k in names], *[v[k] for k in names])
    return [dict(zip(names, outs[j * n:(j + 1) * n])) for j in range(3)]


_ANY = pl.BlockSpec(memory_space=pl.ANY)


def _place():
    return lax.axis_index("x"), lax.axis_index("y"), lax.axis_index("c")


def _sems(n):
    return [pltpu.SemaphoreType.DMA((n, 7)), pltpu.SemaphoreType.DMA((n, 7)), pltpu.SemaphoreType.DMA((n,))]


class _Gather:
    def __init__(self, blocks, second):
        self.inputs, self.second = list(blocks), list(second)
        self.out_shape = [jax.ShapeDtypeStruct((a.shape[0], N_DEV) + a.shape[1:] if s else (N_DEV,) + a.shape, a.dtype)
                          for a, s in zip(blocks, second)]
        self.scratch = _sems(len(blocks))

    def _copies(self, refs):
        x_refs, out_refs, (send_sems, recv_sems, local_sems) = refs
        n = len(x_refs)
        x, y, c = _place()
        me, sibling = (x, y, c), (x, y, 1 - c)
        chips = [(1 - x, y), (x, 1 - y), (1 - x, 1 - y)]

        def slot(j, px, py, pc):
            idx = 4 * px + 2 * py + pc
            return out_refs[j].at[:, idx] if self.second[j] else out_refs[j].at[idx]

        def copies(k, block, to, own=False):
            return [pltpu.make_async_remote_copy(
                src_ref=x_refs[j] if own else slot(j, *block), dst_ref=slot(j, *block),
                send_sem=send_sems.at[j, k], recv_sem=recv_sems.at[j, k], device_id=to,
                device_id_type=pl.DeviceIdType.MESH) for j in range(n)]

        mine = [pltpu.make_async_copy(x_refs[j], slot(j, *me), local_sems.at[j]) for j in range(n)]
        first = copies(0, me, sibling, own=True)
        for j, chip in enumerate(chips):
            first += copies(1 + j, me, (*chip, c), own=True)
        landed = [copies(1 + j, (*chip, c), me) for j, chip in enumerate(chips)]
        onward = [copies(4 + j, (*chip, c), sibling) for j, chip in enumerate(chips)]
        from_sibling = copies(0, sibling, me)
        for j, chip in enumerate(chips):
            from_sibling += copies(4 + j, (*chip, 1 - c), me)
        return mine, first, landed, onward, from_sibling

    def start(self, refs):
        mine, first, _, _, _ = self._copies(refs)
        for cp in mine + first:
            cp.start()

    def forward(self, refs):
        _, _, landed, onward, _ = self._copies(refs)
        for arrived, passed in zip(landed, onward):
            for cp in arrived:
                cp.wait_recv()
            for cp in passed:
                cp.start()

    def finish(self, refs):
        mine, first, _, onward, from_sibling = self._copies(refs)
        for cp in from_sibling:
            cp.wait_recv()
        for cp in first + [cp for passed in onward for cp in passed]:
            cp.wait_send()
        for cp in mine:
            cp.wait()


class _Exchange:
    def __init__(self, sends):
        self.inputs = list(sends)
        self.out_shape = [jax.ShapeDtypeStruct(a.shape, a.dtype) for a in sends]
        self.scratch = _sems(len(sends))

    def _copies(self, refs):
        x_refs, out_refs, (send_sems, recv_sems, local_sems) = refs
        n = len(x_refs)
        x, y, c = _place()
        me = 4 * x + 2 * y + c
        sent, landing = [], []
        for k in range(1, N_DEV):
            px, py, pc = x ^ ((k >> 2) & 1), y ^ ((k >> 1) & 1), c ^ (k & 1)
            them = 4 * px + 2 * py + pc
            for j in range(n):
                for here, there, into in ((them, me, sent), (me, them, landing)):
                    into.append(pltpu.make_async_remote_copy(
                        src_ref=x_refs[j].at[here], dst_ref=out_refs[j].at[there], send_sem=send_sems.at[j, k - 1],
                        recv_sem=recv_sems.at[j, k - 1], device_id=(px, py, pc), device_id_type=pl.DeviceIdType.MESH))
        mine = [pltpu.make_async_copy(x_refs[j].at[me], out_refs[j].at[me], local_sems.at[j]) for j in range(n)]
        return mine, sent, landing

    def start(self, refs):
        mine, sent, _ = self._copies(refs)
        for cp in mine + sent:
            cp.start()

    def forward(self, refs):
        pass

    def finish(self, refs):
        mine, sent, landing = self._copies(refs)
        for cp in landing:
            cp.wait_recv()
        for cp in sent:
            cp.wait_send()
        for cp in mine:
            cp.wait()


def _run(rider, name):
    n_in, n_out = len(rider.inputs), len(rider.out_shape)

    def body(*refs):
        parts = (refs[:n_in], refs[n_in:n_in + n_out], refs[n_in + n_out:])
        rider.start(parts)
        rider.forward(parts)
        rider.finish(parts)

    return pl.pallas_call(body, name=name, out_shape=rider.out_shape, in_specs=[_ANY] * n_in, out_specs=[_ANY] * n_out,
                          scratch_shapes=rider.scratch)(*rider.inputs)


def _split_refs(refs, n_in, n_out, n_scratch, rider):
    r_in = len(rider.inputs) if rider else 0
    r_out = len(rider.out_shape) if rider else 0
    a = n_in + r_in
    b = a + n_out + r_out
    own = refs[:n_in] + refs[a:a + n_out] + refs[b:b + n_scratch]
    return own, (refs[n_in:a], refs[a + n_out:b], refs[b + n_scratch:])


def _ride_begin(rider, ride_refs):
    if rider is not None:
        pl.when(pl.program_id(0) == 0)(lambda: rider.start(ride_refs))


def _ride_end(rider, ride_refs):
    if rider is not None:
        @pl.when(pl.program_id(0) == pl.num_programs(0) - 1)
        def _():
            rider.forward(ride_refs)
            rider.finish(ride_refs)


def _host(rider, in_specs, out_specs, out_shape, scratch, operands):
    if rider is None:
        return dict(in_specs=in_specs, out_specs=out_specs, out_shape=out_shape, scratch_shapes=scratch), operands
    return dict(in_specs=in_specs + [_ANY] * len(rider.inputs), out_specs=out_specs + [_ANY] * len(rider.out_shape),
                out_shape=out_shape + rider.out_shape, scratch_shapes=scratch + rider.scratch), operands + rider.inputs


_LATE = ("w_gate_up", "w_out", "w_down")


def _local_grads(x, target, w, late=None, exchange=False):
    w = dict(w)
    dn_params = (w["dn_a_log"], w["dn_dt_bias"], w["dn_norm_w"])
    gla_params = (w["gla_w_gate2"], w["gla_gate_bias"], w["gla_norm_w"])
    sgu_params = (w["sgu_ln_w"], w["sgu_ln_b"], w["sgu_w_spatial"], w["sgu_b_spatial"])
    saved = []
    for l in range(DEPTH):
        h, p = _in_proj(x, w["norm1_w"], w["w_in"], l)
        ya = _sgu_fwd(p, *sgu_params, l)
        yb = _sconv_fwd(p, w["sc_conv_w"], l)
        qkv_c = _qkv_conv_fwd(p, w["dn_conv_w"], l)
        if l == 0 and late is not None:
            yc, st_c, inv_c, w_gu, w_o, w_d = _dn_fwd(qkv_c, p, dn_params, l,
                                                      rider=_Gather([late[k] for k in _LATE], [False, True, True]))
            w.update(w_gate_up=w_gu, w_out=w_o.reshape(DEPTH, D, D), w_down=w_d.reshape(DEPTH, FF, D))
        else:
            yc, st_c, inv_c = _dn_fwd(qkv_c, p, dn_params, l)
        yd, st_d = _gla_fwd(p, gla_params, l)
        mix, x1, h2, gu, act, x2 = _out_mlp(x, (ya, yb, yc, yd), w["w_out"], w["norm2_w"], w["w_gate_up"], w["w_down"], l)
        saved.append(dict(x=x, h=h, p=p, qkv_c=qkv_c, st_c=st_c, inv_c=inv_c, st_d=st_d, mix=mix, x1=x1, h2=h2, gu=gu,
                          act=act))
        x = x2
    loss, d_final, dx = _loss_head(x, w["final_norm_w"], target)
    large = {k: [None] * DEPTH for k in ("w_in", "w_out", "w_gate_up", "w_down")}
    small = [None] * DEPTH
    for l in reversed(range(DEPTH)):
        s = saved[l]
        p = s["p"]
        g = {}
        dgu, dx1, dmix, g["norm2_w"] = _mlp_out_bwd(dx, s["x1"], s["gu"], w["norm2_w"], w["w_down"], w["w_gate_up"],
                                                    w["w_out"], l)
        large["w_gate_up"][l] = _wgrad_gate_up(s["h2"], dgu)
        large["w_down"][l] = _wgrad_down(s["act"], dx).reshape(N_DEV, FF // N_DEV, D)
        large["w_out"][l] = _matmul_tn(s["mix"], dx1, "wgrad_out", BF16).reshape(N_DEV, D // N_DEV, D)
        d_a, g["sgu_ln_w"], g["sgu_ln_b"], g["sgu_w_spatial"], g["sgu_b_spatial"] = _sgu_bwd(p, dmix, *sgu_params, l)
        d_b, g["sc_conv_w"] = _sconv_bwd(p, dmix, w["sc_conv_w"], l)
        rider = _Exchange([large[k][1] for k in _LARGE]) if exchange and l == 0 else None
        dqkv_c, dz_c, dab, g["dn_a_log"], g["dn_dt_bias"], g["dn_norm_w"], *arrived = _dn_bwd(
            s["qkv_c"], p, s["st_c"], s["inv_c"], dmix, dn_params, l, rider=rider)
        for k, a in zip(_LARGE, arrived):
            large[k][1] = a
        d_c, g["dn_conv_w"] = _qkv_conv_bwd(p, dqkv_c, w["dn_conv_w"], l)
        d_d, dz_d, dglr, g["gla_w_gate2"], g["gla_gate_bias"], g["gla_norm_w"] = _gla_bwd(p, s["st_d"], dmix, gla_params, l)
        dp, dx, g["norm1_w"] = _in_proj_bwd((d_c, d_d, d_a, d_b, dz_c, dz_d, dab, dglr), s["x"], dx1, w["norm1_w"],
                                            w["w_in"], l)
        large["w_in"][l] = _scatter_w_in_grad(_matmul_tn(s["h"], dp, "wgrad_in", F32))
        small[l] = g
    return loss[0, 0], dx, large, small, d_final


_ORDER = ("norm1_w", "w_in", "sgu_ln_w", "sgu_ln_b", "sgu_w_spatial", "sgu_b_spatial", "sc_conv_w", "dn_conv_w",
          "dn_a_log", "dn_dt_bias", "dn_norm_w", "gla_w_gate2", "gla_gate_bias", "gla_norm_w", "w_out", "norm2_w",
          "w_gate_up", "w_down", "final_norm_w")
_LARGE = ("w_in", "w_gate_up", "w_out", "w_down")
_LARGE_TILE = {"w_in": 256, "w_gate_up": 256, "w_out": 128, "w_down": 352}


def _gather_cols(g):
    return jnp.transpose(g, (1, 2, 0, 3)).reshape(g.shape[1], g.shape[2], N_DEV * g.shape[3])


def kernel(x, norm1_w, w_in, sgu_ln_w, sgu_ln_b, sgu_w_spatial, sgu_b_spatial, sc_conv_w, dn_conv_w, dn_a_log, dn_dt_bias, dn_norm_w, gla_w_gate2, gla_gate_bias, gla_norm_w, w_out, norm2_w, w_gate_up, w_down, final_norm_w, loss_target, m_norm1_w, m_w_in, m_sgu_ln_w, m_sgu_ln_b, m_sgu_w_spatial, m_sgu_b_spatial, m_sc_conv_w, m_dn_conv_w, m_dn_a_log, m_dn_dt_bias, m_dn_norm_w, m_gla_w_gate2, m_gla_gate_bias, m_gla_norm_w, m_w_out, m_norm2_w, m_w_gate_up, m_w_down, m_final_norm_w, v_norm1_w, v_w_in, v_sgu_ln_w, v_sgu_ln_b, v_sgu_w_spatial, v_sgu_b_spatial, v_sc_conv_w, v_dn_conv_w, v_dn_a_log, v_dn_dt_bias, v_dn_norm_w, v_gla_w_gate2, v_gla_gate_bias, v_gla_norm_w, v_w_out, v_norm2_w, v_w_gate_up, v_w_down, v_final_norm_w):
    args = dict(locals())
    wts = {k: args[k] for k in _ORDER}
    mom = {k: args["m_" + k] for k in _ORDER}
    var = {k: args["v_" + k] for k in _ORDER}
    for d in (wts, mom, var):
        d["final_norm_w"] = d["final_norm_w"][None]
    me = 4 * lax.axis_index("x") + 2 * lax.axis_index("y") + lax.axis_index("c")

    shards = {k: wts[k].astype(BF16) for k in _LARGE}
    got = _run(_Gather([shards["w_in"]] + [wts[k] for k in _SHARDED_SMALL], [False] * 4), "gather_w_in")
    full = dict(wts)
    full["w_in"] = _relayout_w_in(got[0])
    for k, g in zip(_SHARDED_SMALL, got[1:]):
        full[k] = _gather_cols(g)

    loss, dx, large, small, d_final = _local_grads(x[0], loss_target[0], full, late=shards, exchange=True)
    loss = lax.psum(loss, ("x", "y", "c"))

    arrived = _run(_Exchange([large[k][0] for k in _LARGE]), "exchange_grads")
    result = {}
    for k, part in zip(_LARGE, arrived):
        outs = _adamw_large((part, large[k][1]), wts[k], mom[k], var[k], "adamw_" + k, _LARGE_TILE[k])
        for kind, a in zip(("grad", "delta", "new_m", "new_v"), outs):
            result[kind, k] = a

    slabs = _run(_Gather([_pack_small(small, d_final)], [False]), "gather_small_grads")[0]
    rep = _REPLICATED + ("final_norm_w",)
    outs, summed = _adamw_small(slabs, {k: wts[k] for k in rep}, {k: mom[k] for k in rep}, {k: var[k] for k in rep})
    for kind, d in zip(("grad", "delta", "new_m", "new_v"), outs):
        for k, a in d.items():
            result[kind, k] = a[0] if k == "final_norm_w" else a
    own = {k: lax.dynamic_slice_in_dim(summed[k], me * wts[k].shape[-1], wts[k].shape[-1], axis=2) for k in _SHARDED_SMALL}
    outs = _adamw_shards(own, {k: wts[k] for k in _SHARDED_SMALL}, {k: mom[k] for k in _SHARDED_SMALL},
                         {k: var[k] for k in _SHARDED_SMALL})
    for kind, d in zip(("grad", "delta", "new_m", "new_v"), [own] + outs):
        for k, a in d.items():
            result[kind, k] = a

    return (loss, dx[None], *[result[kind, k] for kind in ("grad", "delta", "new_m", "new_v") for k in _ORDER])
```

```python
import functools

import jax
import jax.numpy as jnp
from jax import lax
from jax.experimental import pallas as pl
from jax.experimental.pallas import tpu as pltpu

F32, BF16 = jnp.float32, jnp.bfloat16
DEPTH = 2
D = 1024
G = 256
NH, HD = 4, 64
FF = 2816
IN_COLS = 3352
P = 3584
EPS = 1e-6
SGU_C, DN_C, GLA_C = 128, 64, 64
N_DEV = 8
IN_SHARD = IN_COLS // N_DEV
GU_SHARD = 2 * FF // N_DEV
N_GATE = N_DEV // 2
LANES = 128
VMEM_LIMIT = 56 * 2 ** 20

_PAD_SEGS = ((1280, 2048, 0),
             (2312, 3080, 0),
             (0, 512, 0),
             (512, 1280, 0),
             (2056, 2312, 0),
             (3096, 3352, 0),
             (2048, 2056, 120),
             (3080, 3096, 112))

ADAM_LR, ADAM_B1, ADAM_B2, ADAM_EPS, ADAM_WD, ADAM_STEP = 0.001, 0.9, 0.999, 1e-08, 0.01, 10


def _cparams(*sem):
    return pltpu.CompilerParams(dimension_semantics=sem, vmem_limit_bytes=VMEM_LIMIT)


def _resident(shape):
    return pl.BlockSpec(shape, lambda *_: (0,) * len(shape), pipeline_mode=pl.Buffered(1))


def _layer(shape, l):
    return pl.BlockSpec((None,) + tuple(shape), lambda *_: (l,) + (0,) * len(shape), pipeline_mode=pl.Buffered(1))


def _acc(shape):
    return pl.BlockSpec(shape, lambda *_: (0,) * len(shape))


def _dg(a, b, ca, cb, precision=None):
    lead = a.ndim - 2
    batch = ((0,), (0,)) if lead else ((), ())
    return lax.dot_general(a, b, (((ca + lead,), (cb + lead,)), batch), preferred_element_type=F32, precision=precision)


def _make_dot(cast, precision):
    def raw(a, b, form):
        ca = 0 if form == "tn" else 1
        cb = 1 if form == "nt" else 0
        return _dg_any(cast(a), cast(b), ca, cb, precision)

    @functools.partial(jax.custom_vjp, nondiff_argnums=(2,))
    def dot(a, b, form):
        return raw(a, b, form)

    def fwd(a, b, form):
        return raw(a, b, form), (a, b)

    def bwd(form, res, g):
        a, b = res
        if form == "nn":
            return raw(g, b, "nt"), raw(a, g, "tn")
        if form == "nt":
            return raw(g, b, "nn"), raw(g, a, "tn")
        return raw(b, g, "nt"), raw(a, g, "nn")

    dot.defvjp(fwd, bwd)
    return dot


def _split(t):
    hi = t.astype(BF16)
    return hi, (t - hi.astype(F32)).astype(BF16)


def _dg3(a, b, ca, cb):
    (ah, al), (bh, bl) = a, b
    return _dg(ah, bh, ca, cb) + (_dg(ah, bl, ca, cb) + _dg(al, bh, ca, cb))


class _Split3:
    pass


def _dg_any(a, b, ca, cb, precision):
    if precision is _Split3:
        return _dg3(_split(a), _split(b), ca, cb)
    return _dg(a, b, ca, cb, precision)


_bdot = _make_dot(lambda t: t.astype(BF16), None)
_hdot = _make_dot(lambda t: t, _Split3)


def _inv_unit_lower(low):
    n = low.shape[-1]
    eye = (lax.broadcasted_iota(jnp.int32, (n, n), 0) == lax.broadcasted_iota(jnp.int32, (n, n), 1)).astype(F32)
    p = -low
    inv = eye + p
    ps = _split(p)
    k = 1
    while 2 * k < n:
        ps = _split(_dg3(ps, ps, 1, 0))
        inv = inv + _dg3(_split(inv), ps, 1, 0)
        k *= 2
    return inv


@jax.custom_vjp
def _unit_lower_solve(low, rhs, inv):
    return _dg3(_split(inv), _split(rhs), 1, 0)


def _uls_fwd(low, rhs, inv):
    sol = _dg3(_split(inv), _split(rhs), 1, 0)
    return sol, (inv, sol)


def _uls_bwd(res, g):
    inv, sol = res
    d_rhs = _dg3(_split(inv), _split(g), 0, 0)
    return -_dg3(_split(d_rhs), _split(sol), 1, 1), d_rhs, jnp.zeros_like(inv)


_unit_lower_solve.defvjp(_uls_fwd, _uls_bwd)


def _sigmoid(x):
    return 1.0 / (1.0 + jnp.exp(-x))


def _softplus(x):
    return jnp.maximum(x, 0.0) + jnp.log(1.0 + jnp.exp(-jnp.maximum(x, -x)))


def _gelu(x):
    return 0.5 * x * (1.0 + jnp.tanh(0.7978845608028654 * (x + 0.044715 * (x * x * x))))


def _tri(n):
    ri = lax.broadcasted_iota(jnp.int32, (n, n), 0)
    ci = lax.broadcasted_iota(jnp.int32, (n, n), 1)
    return ri, ci


def _stack(ts):
    return jnp.concatenate([t[None] for t in ts], axis=0)


def _sgu_chunk(uv, ln_w, ln_b, ws, bs):
    u = _gelu(uv[:, :G])
    v = _gelu(uv[:, G:])
    mu = jnp.mean(v, axis=-1, keepdims=True)
    vc = v - mu
    var = jnp.mean(vc * vc, axis=-1, keepdims=True)
    vn = vc * lax.rsqrt(var + EPS) * ln_w + ln_b
    ri, ci = _tri(SGU_C)
    bs_t = _hdot((ri == ci).astype(F32), bs, "nt")
    mixed = []
    for h in range(NH):
        wm = jnp.where(ri >= ci, ws[h], 0.0)
        mixed.append(_bdot(wm, vn[:, HD * h:HD * (h + 1)], "nn") + bs_t[:, h:h + 1])
    return u * jnp.concatenate(mixed, axis=1)


def _dn_tile(q, k, v, ab, z, state, inv, a_log, dt_bias, nw):
    c = DN_C
    tm = q.shape[0]
    cps = tm // c
    ri, ci = _tri(tm)
    shift = c.bit_length() - 1
    same = jnp.right_shift(ri, shift) == jnp.right_shift(ci, shift)
    g4 = -jnp.exp(a_log) * _softplus(ab[:, 0:NH] + dt_bias)
    beta4 = _sigmoid(ab[:, NH:2 * NH])
    gc4 = _hdot((same & (ri >= ci)).astype(F32), g4, "nn")
    gr4 = _hdot(g4, (same & (ri <= ci)).astype(F32), "tn")
    pairs = [(slice(c * j, c * (j + 1)), h) for j in range(cps) for h in range(NH)]
    heads = lambda t: _stack([t[rows, HD * h:HD * (h + 1)] for rows, h in pairs])
    col = lambda t: _stack([t[rows, h:h + 1] for rows, h in pairs])
    qn, kn, vh = heads(q), heads(k), heads(v)
    qn = qn * lax.rsqrt(jnp.sum(qn * qn, axis=-1, keepdims=True) + EPS) * (HD ** -0.5)
    kn = kn * lax.rsqrt(jnp.sum(kn * kn, axis=-1, keepdims=True) + EPS)
    gc, beta = col(gc4), col(beta4)
    gr = _stack([gr4[h:h + 1, rows] for rows, h in pairs])
    gl = jnp.sum(col(g4), axis=1, keepdims=True)
    r2, c2 = _tri(c)
    decay = jnp.exp(jnp.where(r2 >= c2, gc - gr, -jnp.inf))
    kb = kn * beta
    low = jnp.where(r2 > c2, _bdot(kb, kn, "nt") * decay, 0.0)
    eg = jnp.exp(gc)
    if inv is None:
        inv = _inv_unit_lower(low)
    sol = _unit_lower_solve(low, jnp.concatenate([vh * beta, kb * eg], axis=2), inv)
    u, w = sol[:, :, :HD], sol[:, :, HD:]
    attn = _bdot(qn, kn, "nt") * decay
    qg, kd, cd = qn * eg, kn * jnp.exp(gl - gc), jnp.exp(gl)
    ys = []
    for j in range(cps):
        b = slice(NH * j, NH * (j + 1))
        v_new = u[b] - _bdot(w[b], state, "nn")
        o = _bdot(qg[b], state, "nn") + _bdot(attn[b], v_new, "nn")
        state = state * cd[b] + _bdot(kd[b], v_new, "tn")
        on = o * lax.rsqrt(jnp.mean(o * o, axis=-1, keepdims=True) + EPS) * nw
        ys.append(jnp.concatenate([on[h] for h in range(NH)], axis=1))
    return jnp.concatenate(ys, axis=0) * (z * _sigmoid(z)), state, inv


def _gla_tile(q, k, v, glr, z, state_t, w2, gate_bias, nw):
    c = GLA_C
    tm = q.shape[0]
    cps = tm // c
    ri, ci = _tri(tm)
    shift = c.bit_length() - 1
    same = jnp.right_shift(ri, shift) == jnp.right_shift(ci, shift)
    w2_rows = jnp.concatenate([w2, jnp.zeros((LANES - w2.shape[0], G), F32)], axis=0)
    pre = _bdot(glr, w2_rows, "nn") + gate_bias
    log_a = (jnp.minimum(pre, 0.0) - jnp.log(1.0 + jnp.exp(-jnp.maximum(pre, -pre)))) * (1.0 / 16.0)
    gcum = _hdot((same & (ri >= ci)).astype(F32), log_a, "nn")
    row = lax.broadcasted_iota(jnp.int32, (c, G), 0)
    qs = q * (HD ** -0.5)
    qa, ka, qg, kl, dec = [], [], [], [], []
    for j in range(cps):
        rows = slice(c * j, c * (j + 1))
        gj = gcum[rows]
        g_mid = jnp.sum(jnp.where(row == c // 2, gj, 0.0), axis=0, keepdims=True)
        g_last = jnp.sum(log_a[rows], axis=0, keepdims=True)
        qa.append(qs[rows] * jnp.exp(gj - g_mid))
        ka.append(k[rows] * jnp.exp(g_mid - gj))
        qg.append(qs[rows] * jnp.exp(gj))
        kl.append(k[rows] * jnp.exp(g_last - gj))
        dec.append(jnp.exp(g_last))
    heads = lambda ts: _stack([t[:, HD * h:HD * (h + 1)] for t in ts for h in range(NH)])
    qa, ka, qg, kl, dec = heads(qa), heads(ka), heads(qg), heads(kl), heads(dec)
    vh = heads([v[c * j:c * (j + 1)] for j in range(cps)])
    r2, c2 = _tri(c)
    o_intra = _bdot(jnp.where(r2 >= c2, _bdot(qa, ka, "nt"), 0.0), vh, "nn")
    ys = []
    for j in range(cps):
        b = slice(NH * j, NH * (j + 1))
        o = o_intra[b] + _bdot(qg[b], state_t, "nt")
        state_t = state_t * dec[b] + _bdot(vh[b], kl[b], "tn")
        on = o * lax.rsqrt(jnp.mean(o * o, axis=-1, keepdims=True) + EPS) * nw
        ys.append(jnp.concatenate([on[h] for h in range(NH)], axis=1))
    return jnp.concatenate(ys, axis=0) * (z * _sigmoid(z)), state_t


def _rms(x, nw):
    r = lax.rsqrt(jnp.mean(x * x, axis=-1, keepdims=True) + EPS)
    xh = x * r
    return xh * nw, xh, r


def _rms_bwd(g, xh, r, nw):
    gw = g * nw
    return r * (gw - xh * jnp.mean(gw * xh, axis=-1, keepdims=True)), jnp.sum(g * xh, axis=0, keepdims=True)


def _row(tm, w, blk=0):
    return pl.BlockSpec((tm, w), lambda i: (i, blk))


def _in_proj(x, nw, wp, l, tm=256):
    t = x.shape[0]

    def body(x_ref, nw_ref, w_ref, h_ref, p_ref):
        h = _rms(x_ref[...], nw_ref[l:l + 1, :])[0].astype(BF16)
        h_ref[...] = h
        p_ref[...] = jnp.dot(h, w_ref[...], preferred_element_type=F32)

    return pl.pallas_call(
        body, name="in_proj", grid=(t // tm,),
        in_specs=[_row(tm, D), _resident((DEPTH, D)), _layer((D, P), l)],
        out_specs=[_row(tm, D), _row(tm, P)],
        out_shape=[jax.ShapeDtypeStruct((t, D), BF16), jax.ShapeDtypeStruct((t, P), F32)],
        compiler_params=_cparams("parallel"))(x, nw, wp)


def _gu_spec(l):
    return pl.BlockSpec((N_DEV, None, D, GU_SHARD), lambda *_: (0, l, 0, 0), pipeline_mode=pl.Buffered(1))


def _out_mlp(x, ys, w_out, nw2, w_gu, w_down, l, tm=256):
    t = x.shape[0]

    def body(x_ref, ya, yb, yc, yd, wo_ref, nw_ref, wgu_ref, wd_ref, mix_ref, x1_ref, h2_ref, gu_ref, act_ref, x2_ref):
        mix = jnp.concatenate([ya[...], yb[...], yc[...], yd[...]], axis=1)
        mix_ref[...] = mix
        x1 = x_ref[...] + jnp.dot(mix, wo_ref[...], preferred_element_type=F32)
        x1_ref[...] = x1
        h2 = _rms(x1, nw_ref[l:l + 1, :])[0].astype(BF16)
        h2_ref[...] = h2
        x2 = x1
        for d in range(N_GATE):
            gate = jnp.dot(h2, wgu_ref[d], preferred_element_type=F32)
            up = jnp.dot(h2, wgu_ref[d + N_GATE], preferred_element_type=F32)
            gu_ref[d] = gate.astype(BF16)
            gu_ref[d + N_GATE] = up.astype(BF16)
            act = (gate * _sigmoid(gate) * up).astype(BF16)
            act_ref[d] = act
            x2 = x2 + jnp.dot(act, wd_ref[GU_SHARD * d:GU_SHARD * (d + 1), :], preferred_element_type=F32)
        x2_ref[...] = x2

    dev = lambda n: pl.BlockSpec((n, tm, GU_SHARD), lambda i: (0, i, 0))
    return pl.pallas_call(
        body, name="out_mlp", grid=(t // tm,),
        in_specs=[_row(tm, D), _row(tm, G), _row(tm, G), _row(tm, G), _row(tm, G), _layer((D, D), l),
                  _resident((DEPTH, D)), _gu_spec(l), _layer((FF, D), l)],
        out_specs=[_row(tm, D), _row(tm, D), _row(tm, D), dev(N_DEV), dev(N_GATE), _row(tm, D)],
        out_shape=[jax.ShapeDtypeStruct((t, D), BF16), jax.ShapeDtypeStruct((t, D), F32),
                   jax.ShapeDtypeStruct((t, D), BF16), jax.ShapeDtypeStruct((N_DEV, t, GU_SHARD), BF16),
                   jax.ShapeDtypeStruct((N_GATE, t, GU_SHARD), BF16), jax.ShapeDtypeStruct((t, D), F32)],
        compiler_params=_cparams("parallel"))(x, *ys, w_out, nw2, w_gu, w_down)


def _loss_head(x, nw, target, tm=512):
    t = x.shape[0]

    def body(x_ref, nw_ref, tg_ref, loss_ref, dnw_ref, dx_ref):
        @pl.when(pl.program_id(0) == 0)
        def _():
            loss_ref[...] = jnp.zeros_like(loss_ref)
            dnw_ref[...] = jnp.zeros_like(dnw_ref)
        nw_v = nw_ref[...]
        y, xh, r = _rms(x_ref[...], nw_v)
        err = y - tg_ref[...]
        loss_ref[...] += 0.5 * jnp.sum(err * err) * (1.0 / D)
        dx, dnw = _rms_bwd(err * (1.0 / D), xh, r, nw_v)
        dx_ref[...] = dx
        dnw_ref[...] += dnw

    return pl.pallas_call(
        body, name="loss_head", grid=(t // tm,),
        in_specs=[_row(tm, D), _resident((1, D)), _row(tm, D)],
        out_specs=[_acc((8, LANES)), _acc((1, D)), _row(tm, D)],
        out_shape=[jax.ShapeDtypeStruct((8, LANES), F32), jax.ShapeDtypeStruct((1, D), F32),
                   jax.ShapeDtypeStruct((t, D), F32)],
        compiler_params=_cparams("arbitrary"))(x, nw, target)


def _mlp_out_bwd(dx2, x1, gu, nw2, w_down, w_gu, w_out, l, tm=256, rider=None):
    t = dx2.shape[0]

    def body(*refs):
        (dx2_ref, x1_ref, gu_ref, nw_ref, wd_ref, wgu_ref, wo_ref, dgu_ref, dx1_ref, dmix_ref, dnw_ref), ride = \
            _split_refs(refs, 7, 4, 0, rider)
        _ride_begin(rider, ride)

        @pl.when(pl.program_id(0) == 0)
        def _():
            dnw_ref[...] = jnp.zeros_like(dnw_ref)
        dx2 = dx2_ref[...]
        dx2_b = dx2.astype(BF16)
        dh2 = jnp.zeros((tm, D), F32)
        for d in range(N_GATE):
            dact = _dg(dx2_b, wd_ref[GU_SHARD * d:GU_SHARD * (d + 1), :], 1, 1)
            gate, up = gu_ref[d].astype(F32), gu_ref[d + N_GATE].astype(F32)
            s = _sigmoid(gate)
            dgate = (dact * up * (s * (1.0 + gate * (1.0 - s)))).astype(BF16)
            dup = (dact * (gate * s)).astype(BF16)
            dgu_ref[d] = dgate
            dgu_ref[d + N_GATE] = dup
            dh2 = dh2 + _dg(dgate, wgu_ref[d], 1, 1) + _dg(dup, wgu_ref[d + N_GATE], 1, 1)
        nw_v = nw_ref[l:l + 1, :]
        _, xh, r = _rms(x1_ref[...], nw_v)
        dxn, dnw = _rms_bwd(dh2, xh, r, nw_v)
        dx1 = dx2 + dxn
        dx1_ref[...] = dx1
        dnw_ref[...] += dnw
        dmix_ref[...] = _dg(dx1.astype(BF16), wo_ref[...], 1, 1)
        _ride_end(rider, ride)

    dev = pl.BlockSpec((N_DEV, tm, GU_SHARD), lambda i: (0, i, 0))
    specs, operands = _host(
        rider, [_row(tm, D), _row(tm, D), dev, _resident((DEPTH, D)), _layer((FF, D), l), _gu_spec(l), _layer((D, D), l)],
        [dev, _row(tm, D), _row(tm, D), _acc((1, D))],
        [jax.ShapeDtypeStruct((N_DEV, t, GU_SHARD), BF16), jax.ShapeDtypeStruct((t, D), F32),
         jax.ShapeDtypeStruct((t, D), F32), jax.ShapeDtypeStruct((1, D), F32)],
        [], [dx2, x1, gu, nw2, w_down, w_gu, w_out])
    return pl.pallas_call(body, name="mlp_out_bwd", grid=(t // tm,), compiler_params=_cparams("arbitrary"),
                          **specs)(*operands)


_DP_WIDTHS = (768, 768, 512, 768, 256, 256, 128, 128)


def _in_proj_bwd(dps, x, dx1, nw, wp, l, tm=256):
    t = x.shape[0]

    def body(*refs):
        dp_refs, (x_ref, dx1_ref, nw_ref, w_ref, dp_ref, dx_ref, dnw_ref) = refs[:8], refs[8:]

        @pl.when(pl.program_id(0) == 0)
        def _():
            dnw_ref[...] = jnp.zeros_like(dnw_ref)
        dp = jnp.concatenate([r[...] for r in dp_refs], axis=1)
        dp_ref[...] = dp
        dh = _dg(dp, w_ref[...], 1, 1)
        nw_v = nw_ref[l:l + 1, :]
        _, xh, r = _rms(x_ref[...], nw_v)
        dxn, dnw = _rms_bwd(dh, xh, r, nw_v)
        dx_ref[...] = dx1_ref[...] + dxn
        dnw_ref[...] += dnw

    return pl.pallas_call(
        body, name="in_proj_bwd", grid=(t // tm,),
        in_specs=[_row(tm, w) for w in _DP_WIDTHS] + [_row(tm, D), _row(tm, D), _resident((DEPTH, D)), _layer((D, P), l)],
        out_specs=[_row(tm, P), _row(tm, D), _acc((1, D))],
        out_shape=[jax.ShapeDtypeStruct((t, P), BF16), jax.ShapeDtypeStruct((t, D), F32),
                   jax.ShapeDtypeStruct((1, D), F32)],
        compiler_params=_cparams("arbitrary"))(*dps, x, dx1, nw, wp)


def _accumulate(acc, o_ref, t_axis, term):
    @pl.when(pl.program_id(t_axis) == 0)
    def _():
        acc[...] = jnp.zeros_like(acc)
    acc[...] += term

    @pl.when(pl.program_id(t_axis) == pl.num_programs(t_axis) - 1)
    def _():
        o_ref[...] = acc[...].astype(o_ref.dtype)


WGRAD_TOKENS = 2048


def _matmul_tn(a, b, name, out_dtype, tn=512, tt=WGRAD_TOKENS):
    t, m = a.shape
    n = b.shape[1]
    tt = min(tt, t)

    def body(a_ref, b_ref, o_ref, acc):
        _accumulate(acc, o_ref, 1, _dg(a_ref[...].astype(BF16), b_ref[...].astype(BF16), 0, 0))

    return pl.pallas_call(
        body, name=name, grid=(n // tn, t // tt),
        in_specs=[pl.BlockSpec((tt, m), lambda j, i: (i, 0)), pl.BlockSpec((tt, tn), lambda j, i: (i, j))],
        out_specs=pl.BlockSpec((m, tn), lambda j, i: (0, j)),
        out_shape=jax.ShapeDtypeStruct((m, n), out_dtype),
        scratch_shapes=[pltpu.VMEM((m, tn), F32)],
        compiler_params=_cparams("parallel", "arbitrary"))(a, b)


def _wgrad_gate_up(h2, dgu, tt=WGRAD_TOKENS):
    t = h2.shape[0]
    tt = min(tt, t)

    def body(a_ref, b_ref, o_ref, acc):
        _accumulate(acc, o_ref, 1, _dg(a_ref[...], b_ref[...], 0, 0))

    return pl.pallas_call(
        body, name="wgrad_gate_up", grid=(N_DEV, t // tt),
        in_specs=[pl.BlockSpec((tt, D), lambda d, i: (i, 0)), pl.BlockSpec((None, tt, GU_SHARD), lambda d, i: (d, i, 0))],
        out_specs=pl.BlockSpec((None, D, GU_SHARD), lambda d, i: (d, 0, 0)),
        out_shape=jax.ShapeDtypeStruct((N_DEV, D, GU_SHARD), BF16),
        scratch_shapes=[pltpu.VMEM((D, GU_SHARD), F32)],
        compiler_params=_cparams("parallel", "arbitrary"))(h2, dgu)


def _wgrad_down(act, dx2, tt=WGRAD_TOKENS):
    t = dx2.shape[0]
    tt = min(tt, t)

    def body(a_ref, b_ref, o_ref, acc):
        _accumulate(acc, o_ref, 1, _dg(a_ref[...], b_ref[...].astype(BF16), 0, 0))

    return pl.pallas_call(
        body, name="wgrad_down", grid=(N_GATE, t // tt),
        in_specs=[pl.BlockSpec((None, tt, GU_SHARD), lambda d, i: (d, i, 0)), pl.BlockSpec((tt, D), lambda d, i: (i, 0))],
        out_specs=pl.BlockSpec((GU_SHARD, D), lambda d, i: (d, 0)),
        out_shape=jax.ShapeDtypeStruct((FF, D), BF16),
        scratch_shapes=[pltpu.VMEM((GU_SHARD, D), F32)],
        compiler_params=_cparams("parallel", "arbitrary"))(act, dx2)


def _relayout_w_in(g, tr=256):
    def body(w_ref, o_ref):
        full = jnp.concatenate([w_ref[d] for d in range(N_DEV)], axis=1)
        parts = []
        for a, b, z in _PAD_SEGS:
            parts.append(full[:, a:b])
            if z:
                parts.append(jnp.zeros((tr, z), full.dtype))
        o_ref[...] = jnp.concatenate(parts, axis=1)

    return pl.pallas_call(
        body, name="relayout_w_in", grid=(DEPTH, D // tr),
        in_specs=[pl.BlockSpec((N_DEV, None, tr, IN_SHARD), lambda l, i: (0, l, i, 0))],
        out_specs=pl.BlockSpec((None, tr, P), lambda l, i: (l, i, 0)),
        out_shape=jax.ShapeDtypeStruct((DEPTH, D, P), g.dtype),
        compiler_params=_cparams("parallel", "parallel"))(g)


def _scatter_w_in_grad(gp, tr=256):
    def body(g_ref, o_ref):
        g = g_ref[...]
        pieces, off = [], 0
        for a, b, z in _PAD_SEGS:
            pieces.append((a, g[:, off:off + (b - a)]))
            off += (b - a) + z
        nat = jnp.concatenate([piece for _, piece in sorted(pieces, key=lambda ap: ap[0])], axis=1)
        for d in range(N_DEV):
            o_ref[d] = nat[:, IN_SHARD * d:IN_SHARD * (d + 1)].astype(BF16)

    return pl.pallas_call(
        body, name="scatter_w_in_grad", grid=(D // tr,),
        in_specs=[pl.BlockSpec((tr, P), lambda i: (i, 0))],
        out_specs=pl.BlockSpec((N_DEV, tr, IN_SHARD), lambda i: (0, i, 0)),
        out_shape=jax.ShapeDtypeStruct((N_DEV, D, IN_SHARD), BF16),
        compiler_params=_cparams("parallel"))(gp)


_A_BLK = 3


def _sgu_specs(l):
    return [_resident((DEPTH, G)), _resident((DEPTH, G)), _layer((NH, SGU_C, SGU_C), l), _layer((NH, SGU_C), l)]


def _sgu_fwd(p, ln_w, ln_b, ws, bs, l, tm=256):
    t = p.shape[0]

    def body(uv_ref, lw_ref, lb_ref, ws_ref, bs_ref, y_ref):
        ws_v = [ws_ref[h] for h in range(NH)]
        for c in range(tm // SGU_C):
            rows = pl.ds(c * SGU_C, SGU_C)
            y_ref[rows, :] = _sgu_chunk(uv_ref[rows, :], lw_ref[l:l + 1, :], lb_ref[l:l + 1, :], ws_v,
                                        bs_ref[...]).astype(BF16)

    return pl.pallas_call(
        body, name="sgu_fwd", grid=(t // tm,),
        in_specs=[_row(tm, 2 * G, _A_BLK)] + _sgu_specs(l), out_specs=_row(tm, G),
        out_shape=jax.ShapeDtypeStruct((t, G), BF16),
        compiler_params=_cparams("parallel"))(p, ln_w, ln_b, ws, bs)


def _sgu_bwd(p, dmix, ln_w, ln_b, ws, bs, l, tm=256):
    t = p.shape[0]

    def body(uv_ref, dy_ref, lw_ref, lb_ref, ws_ref, bs_ref, duv_ref, dlw_ref, dlb_ref, dws_ref, dbs_ref):
        @pl.when(pl.program_id(0) == 0)
        def _():
            for r in (dlw_ref, dlb_ref, dws_ref, dbs_ref):
                r[...] = jnp.zeros_like(r)
        ws_v = [ws_ref[h] for h in range(NH)]
        for c in range(tm // SGU_C):
            rows = pl.ds(c * SGU_C, SGU_C)
            _, vjp = jax.vjp(_sgu_chunk, uv_ref[rows, :], lw_ref[l:l + 1, :], lb_ref[l:l + 1, :], ws_v, bs_ref[...])
            duv, dlw, dlb, dws, dbs = vjp(dy_ref[rows, :])
            duv_ref[rows, :] = duv.astype(BF16)
            dlw_ref[...] += dlw
            dlb_ref[...] += dlb
            dbs_ref[...] += dbs
            for h in range(NH):
                dws_ref[h] += dws[h]

    return pl.pallas_call(
        body, name="sgu_bwd", grid=(t // tm,),
        in_specs=[_row(tm, 2 * G, _A_BLK), _row(tm, G, 0)] + _sgu_specs(l),
        out_specs=[_row(tm, 2 * G), _acc((1, G)), _acc((1, G)), _acc((NH, SGU_C, SGU_C)), _acc((NH, SGU_C))],
        out_shape=[jax.ShapeDtypeStruct((t, 2 * G), BF16), jax.ShapeDtypeStruct((1, G), F32),
                   jax.ShapeDtypeStruct((1, G), F32), jax.ShapeDtypeStruct((NH, SGU_C, SGU_C), F32),
                   jax.ShapeDtypeStruct((NH, SGU_C), F32)],
        compiler_params=_cparams("arbitrary"))(p, dmix, ln_w, ln_b, ws, bs)


HALO = 8


def _prev_halo(tm, width, col):
    return pl.BlockSpec((HALO, width), lambda i: (jnp.maximum(i * (tm // HALO) - 1, 0), col))


def _next_halo(tm, width, col, t):
    return pl.BlockSpec((HALO, width), lambda i: (jnp.minimum((i + 1) * (tm // HALO), t // HALO - 1), col))


def _with_prev(halo_ref, x_ref):
    halo = jnp.where(pl.program_id(0) == 0, 0.0, halo_ref[...])
    return jnp.concatenate([halo, x_ref[...]], axis=0)


def _with_next(x_ref, halo_ref):
    halo = jnp.where(pl.program_id(0) == pl.num_programs(0) - 1, 0.0, halo_ref[...])
    return jnp.concatenate([x_ref[...], halo], axis=0)


def _delay(ext, j):
    return ext if j == 0 else pltpu.roll(ext, j, axis=0)


def _advance(ext, j):
    return ext if j == 0 else pltpu.roll(ext, ext.shape[0] - j, axis=0)


def _causal_conv(ext, w, tm):
    kw = w.shape[0]
    out = None
    for k in range(kw):
        term = _delay(ext, kw - 1 - k)[HALO:HALO + tm] * w[k:k + 1, :]
        out = term if out is None else out + term
    return out


def _causal_conv_t(ext, w, tm):
    kw = w.shape[0]
    out = None
    for k in range(kw):
        term = _advance(ext, kw - 1 - k)[0:tm] * w[k:k + 1, :]
        out = term if out is None else out + term
    return out


def _conv_wgrad(ext, dy, kw, tm):
    return jnp.concatenate(
        [jnp.sum(_delay(ext, kw - 1 - k)[HALO:HALO + tm] * dy, axis=0, keepdims=True) for k in range(kw)], axis=0)


_B_GB, _B_GC, _B_H = 8, 9, 10
_C_QKV, _D_QKV = 0, 1
_C_Z, _D_Z = 11, 12
_C_AB, _D_GLR = 26, 27


def _sconv_fwd(p, w, l, tm=512):
    t = p.shape[0]

    def body(gb_ref, gc_ref, h_ref, gc_halo, h_halo, w_ref, y_ref):
        s_ext = _with_prev(gc_halo, gc_ref) * _with_prev(h_halo, h_ref)
        y_ref[...] = (gb_ref[...] * _causal_conv(s_ext, w_ref[...], tm)).astype(BF16)

    return pl.pallas_call(
        body, name="sconv_fwd", grid=(t // tm,),
        in_specs=[_row(tm, G, _B_GB), _row(tm, G, _B_GC), _row(tm, G, _B_H), _prev_halo(tm, G, _B_GC),
                  _prev_halo(tm, G, _B_H), _layer((3, G), l)],
        out_specs=_row(tm, G), out_shape=jax.ShapeDtypeStruct((t, G), BF16),
        compiler_params=_cparams("parallel"))(p, p, p, p, p, w)


def _sconv_bwd(p, dmix, w, l, tm=512):
    t = p.shape[0]

    def body(gb_ref, gc_ref, h_ref, gc_halo, h_halo, gb_next, dy_ref, dy_next, w_ref, dp_ref, dw_ref):
        @pl.when(pl.program_id(0) == 0)
        def _():
            dw_ref[...] = jnp.zeros_like(dw_ref)
        w_v = w_ref[...]
        s_ext = _with_prev(gc_halo, gc_ref) * _with_prev(h_halo, h_ref)
        dout = dy_ref[...]
        d_gb = dout * _causal_conv(s_ext, w_v, tm)
        dconv_ext = _with_next(dy_ref, dy_next) * _with_next(gb_ref, gb_next)
        ds = _causal_conv_t(dconv_ext, w_v, tm)
        dw_ref[...] += _conv_wgrad(s_ext, dconv_ext[0:tm], 3, tm)
        dp_ref[...] = jnp.concatenate([d_gb, ds * h_ref[...], ds * gc_ref[...]], axis=1).astype(BF16)

    return pl.pallas_call(
        body, name="sconv_bwd", grid=(t // tm,),
        in_specs=[_row(tm, G, _B_GB), _row(tm, G, _B_GC), _row(tm, G, _B_H), _prev_halo(tm, G, _B_GC),
                  _prev_halo(tm, G, _B_H), _next_halo(tm, G, _B_GB, t), _row(tm, G, 1), _next_halo(tm, G, 1, t),
                  _layer((3, G), l)],
        out_specs=[_row(tm, 3 * G), _acc((3, G))],
        out_shape=[jax.ShapeDtypeStruct((t, 3 * G), BF16), jax.ShapeDtypeStruct((3, G), F32)],
        compiler_params=_cparams("arbitrary"))(p, p, p, p, p, p, dmix, dmix, w)


def _qkv_conv_fwd(p, w, l, tm=512):
    t = p.shape[0]

    def body(x_ref, x_halo, w_ref, y_ref):
        c = _causal_conv(_with_prev(x_halo, x_ref), w_ref[...], tm)
        y_ref[...] = c * _sigmoid(c)

    return pl.pallas_call(
        body, name="qkv_conv_fwd", grid=(t // tm,),
        in_specs=[_row(tm, 3 * G, _C_QKV), _prev_halo(tm, 3 * G, _C_QKV), _layer((4, 3 * G), l)],
        out_specs=_row(tm, 3 * G), out_shape=jax.ShapeDtypeStruct((t, 3 * G), F32),
        compiler_params=_cparams("parallel"))(p, p, w)


def _qkv_conv_bwd(p, dy, w, l, tm=512):
    t = p.shape[0]

    def body(x_ref, x_halo, x_next, dy_ref, dy_next, w_ref, dx_ref, dw_ref):
        @pl.when(pl.program_id(0) == 0)
        def _():
            dw_ref[...] = jnp.zeros_like(dw_ref)
        w_v = w_ref[...]
        x_all = jnp.concatenate([_with_prev(x_halo, x_ref), jnp.where(
            pl.program_id(0) == pl.num_programs(0) - 1, 0.0, x_next[...])], axis=0)
        c = _causal_conv(x_all, w_v, tm + HALO)
        s = _sigmoid(c)
        dc_ext = _with_next(dy_ref, dy_next) * (s * (1.0 + c * (1.0 - s)))
        dx_ref[...] = _causal_conv_t(dc_ext, w_v, tm).astype(BF16)
        dw_ref[...] += _conv_wgrad(x_all[0:HALO + tm], dc_ext[0:tm], 4, tm)

    return pl.pallas_call(
        body, name="qkv_conv_bwd", grid=(t // tm,),
        in_specs=[_row(tm, 3 * G, _C_QKV), _prev_halo(tm, 3 * G, _C_QKV), _next_halo(tm, 3 * G, _C_QKV, t),
                  _row(tm, 3 * G), _next_halo(tm, 3 * G, 0, t), _layer((4, 3 * G), l)],
        out_specs=[_row(tm, 3 * G), _acc((4, 3 * G))],
        out_shape=[jax.ShapeDtypeStruct((t, 3 * G), BF16), jax.ShapeDtypeStruct((4, 3 * G), F32)],
        compiler_params=_cparams("arbitrary"))(p, p, p, dy, dy, w)


_STATE = pl.BlockSpec((1, NH, HD, HD), lambda i: (i, 0, 0, 0))


def _dn_specs():
    return [_resident((DEPTH, NH)), _resident((DEPTH, NH)), _resident((DEPTH, HD))]


def _dn_fwd(qkv, p, params, l, cps=4, rider=None):
    t = qkv.shape[0]
    tm = DN_C * cps
    nb = cps * NH

    def body(*refs):
        (qkv_ref, z_ref, ab_ref, alog_ref, dt_ref, nw_ref, y_ref, st_ref, inv_ref, state), ride = _split_refs(
            refs, 6, 3, 1, rider)
        _ride_begin(rider, ride)

        @pl.when(pl.program_id(0) == 0)
        def _():
            state[...] = jnp.zeros_like(state)
        st_ref[0] = state[...]
        y, new, inv = _dn_tile(qkv_ref[:, 0:G], qkv_ref[:, G:2 * G], qkv_ref[:, 2 * G:3 * G], ab_ref[...], z_ref[...],
                               state[...], None, alog_ref[l:l + 1, :], dt_ref[l:l + 1, :], nw_ref[l:l + 1, :])
        y_ref[...] = y.astype(BF16)
        inv_ref[...] = inv
        state[...] = new
        _ride_end(rider, ride)

    specs, operands = _host(
        rider, [_row(tm, 3 * G), _row(tm, G, _C_Z), _row(tm, LANES, _C_AB)] + _dn_specs(),
        [_row(tm, G), _STATE, pl.BlockSpec((nb, DN_C, DN_C), lambda i: (i, 0, 0))],
        [jax.ShapeDtypeStruct((t, G), BF16), jax.ShapeDtypeStruct((t // tm, NH, HD, HD), F32),
         jax.ShapeDtypeStruct((t // DN_C * NH, DN_C, DN_C), F32)],
        [pltpu.VMEM((NH, HD, HD), F32)], [qkv, p, p, *params])
    return pl.pallas_call(body, name="dn_fwd", grid=(t // tm,), compiler_params=_cparams("arbitrary"), **specs)(*operands)


def _dn_bwd(qkv, p, states, inv, dmix, params, l, cps=4, rider=None):
    t = qkv.shape[0]
    tm = DN_C * cps
    n = t // tm
    nb = cps * NH

    def body(*refs):
        (qkv_ref, z_ref, ab_ref, st_ref, inv_ref, dy_ref, alog_ref, dt_ref, nw_ref,
         dqkv_ref, dz_ref, dab_ref, dalog_ref, ddt_ref, dnw_ref, dstate), ride = _split_refs(refs, 9, 6, 1, rider)
        _ride_begin(rider, ride)
        dprm_refs = (dalog_ref, ddt_ref, dnw_ref)

        @pl.when(pl.program_id(0) == 0)
        def _():
            dstate[...] = jnp.zeros_like(dstate)
            for r in dprm_refs:
                r[...] = jnp.zeros_like(r)
        inv_v = inv_ref[...]
        tile = lambda q, k, v, ab, z, st, alog, dt, nw: _dn_tile(q, k, v, ab, z, st, inv_v, alog, dt, nw)[:2]
        _, vjp = jax.vjp(tile, qkv_ref[:, 0:G], qkv_ref[:, G:2 * G], qkv_ref[:, 2 * G:3 * G], ab_ref[...], z_ref[...],
                         st_ref[0], alog_ref[l:l + 1, :], dt_ref[l:l + 1, :], nw_ref[l:l + 1, :])
        dq, dk, dv, dab, dz, dst, *dprm = vjp((dy_ref[...], dstate[...]))
        dqkv_ref[...] = jnp.concatenate([dq, dk, dv], axis=1)
        dz_ref[...] = dz.astype(BF16)
        dab_ref[...] = dab.astype(BF16)
        dstate[...] = dst
        for r, g in zip(dprm_refs, dprm):
            r[...] += g
        _ride_end(rider, ride)

    rev = lambda w, blk: pl.BlockSpec((tm, w), lambda i: (n - 1 - i, blk))
    specs, operands = _host(
        rider, [rev(3 * G, 0), rev(G, _C_Z), rev(LANES, _C_AB),
                pl.BlockSpec((1, NH, HD, HD), lambda i: (n - 1 - i, 0, 0, 0)),
                pl.BlockSpec((nb, DN_C, DN_C), lambda i: (n - 1 - i, 0, 0)), rev(G, 2)] + _dn_specs(),
        [rev(3 * G, 0), rev(G, 0), rev(LANES, 0), _acc((1, NH)), _acc((1, NH)), _acc((1, HD))],
        [jax.ShapeDtypeStruct((t, 3 * G), F32), jax.ShapeDtypeStruct((t, G), BF16),
         jax.ShapeDtypeStruct((t, LANES), BF16), jax.ShapeDtypeStruct((1, NH), F32),
         jax.ShapeDtypeStruct((1, NH), F32), jax.ShapeDtypeStruct((1, HD), F32)],
        [pltpu.VMEM((NH, HD, HD), F32)], [qkv, p, p, states, inv, dmix, *params])
    return pl.pallas_call(body, name="dn_bwd", grid=(n,), compiler_params=_cparams("arbitrary"), **specs)(*operands)


def _gla_specs(l):
    return [_layer((16, G), l), _resident((DEPTH, G)), _resident((DEPTH, HD))]


def _gla_fwd(p, params, l, cps=4):
    t = p.shape[0]
    tm = GLA_C * cps

    def body(qkv_ref, z_ref, glr_ref, w2_ref, gb_ref, nw_ref, y_ref, st_ref, state):
        @pl.when(pl.program_id(0) == 0)
        def _():
            state[...] = jnp.zeros_like(state)
        st_ref[0] = state[...]
        y, new = _gla_tile(qkv_ref[:, 0:G], qkv_ref[:, G:2 * G], qkv_ref[:, 2 * G:3 * G], glr_ref[...], z_ref[...],
                           state[...], w2_ref[...], gb_ref[l:l + 1, :], nw_ref[l:l + 1, :])
        y_ref[...] = y.astype(BF16)
        state[...] = new

    return pl.pallas_call(
        body, name="gla_fwd", grid=(t // tm,),
        in_specs=[_row(tm, 3 * G, _D_QKV), _row(tm, G, _D_Z), _row(tm, LANES, _D_GLR)] + _gla_specs(l),
        out_specs=[_row(tm, G), _STATE],
        out_shape=[jax.ShapeDtypeStruct((t, G), BF16), jax.ShapeDtypeStruct((t // tm, NH, HD, HD), F32)],
        scratch_shapes=[pltpu.VMEM((NH, HD, HD), F32)],
        compiler_params=_cparams("arbitrary"))(p, p, p, *params)


def _gla_bwd(p, states, dmix, params, l, cps=4):
    t = p.shape[0]
    tm = GLA_C * cps
    n = t // tm

    def body(qkv_ref, z_ref, glr_ref, st_ref, dy_ref, w2_ref, gb_ref, nw_ref,
             dqkv_ref, dz_ref, dglr_ref, dw2_ref, dgb_ref, dnw_ref, dstate):
        dprm_refs = (dw2_ref, dgb_ref, dnw_ref)

        @pl.when(pl.program_id(0) == 0)
        def _():
            dstate[...] = jnp.zeros_like(dstate)
            for r in dprm_refs:
                r[...] = jnp.zeros_like(r)
        _, vjp = jax.vjp(_gla_tile, qkv_ref[:, 0:G], qkv_ref[:, G:2 * G], qkv_ref[:, 2 * G:3 * G], glr_ref[...],
                         z_ref[...], st_ref[0], w2_ref[...], gb_ref[l:l + 1, :], nw_ref[l:l + 1, :])
        dq, dk, dv, dglr, dz, dst, *dprm = vjp((dy_ref[...], dstate[...]))
        dqkv_ref[...] = jnp.concatenate([dq, dk, dv], axis=1).astype(BF16)
        dz_ref[...] = dz.astype(BF16)
        dglr_ref[...] = dglr.astype(BF16)
        dstate[...] = dst
        for r, g in zip(dprm_refs, dprm):
            r[...] += g

    rev = lambda w, blk: pl.BlockSpec((tm, w), lambda i: (n - 1 - i, blk))
    return pl.pallas_call(
        body, name="gla_bwd", grid=(n,),
        in_specs=[rev(3 * G, _D_QKV), rev(G, _D_Z), rev(LANES, _D_GLR),
                  pl.BlockSpec((1, NH, HD, HD), lambda i: (n - 1 - i, 0, 0, 0)), rev(G, 3)] + _gla_specs(l),
        out_specs=[rev(3 * G, 0), rev(G, 0), rev(LANES, 0), _acc((16, G)), _acc((1, G)), _acc((1, HD))],
        out_shape=[jax.ShapeDtypeStruct((t, 3 * G), BF16), jax.ShapeDtypeStruct((t, G), BF16),
                   jax.ShapeDtypeStruct((t, LANES), BF16), jax.ShapeDtypeStruct((16, G), F32),
                   jax.ShapeDtypeStruct((1, G), F32), jax.ShapeDtypeStruct((1, HD), F32)],
        scratch_shapes=[pltpu.VMEM((NH, HD, HD), F32)],
        compiler_params=_cparams("arbitrary"))(p, p, p, states, dmix, *params)


def _adamw_math(w, g, m, v):
    m = ADAM_B1 * m + (1.0 - ADAM_B1) * g
    v = ADAM_B2 * v + (1.0 - ADAM_B2) * (g * g)
    m_hat = m / (1.0 - ADAM_B1 ** ADAM_STEP)
    v_hat = v / (1.0 - ADAM_B2 ** ADAM_STEP)
    return -ADAM_LR * (m_hat / (jnp.sqrt(v_hat) + ADAM_EPS) + ADAM_WD * w), m, v


def _adamw_large(parts, w, m, v, name, tr):
    _, r, c = w.shape

    def body(p0_ref, p1_ref, w_ref, m_ref, v_ref, g_ref, d_ref, nm_ref, nv_ref):
        def total(p_ref):
            g = p_ref[0].astype(F32)
            for k in range(1, N_DEV):
                g = g + p_ref[k].astype(F32)
            return g

        g = jnp.where(pl.program_id(0) == 0, total(p0_ref), total(p1_ref))
        g_ref[...] = g
        d_ref[...], nm_ref[...], nv_ref[...] = _adamw_math(w_ref[...], g, m_ref[...], v_ref[...])

    blk = pl.BlockSpec((None, tr, c), lambda l, i: (l, i, 0))
    part = pl.BlockSpec((N_DEV, tr, c), lambda l, i: (0, i, 0))
    return pl.pallas_call(
        body, name=name, grid=(DEPTH, r // tr), in_specs=[part, part, blk, blk, blk],
        out_specs=[blk] * 4, out_shape=[jax.ShapeDtypeStruct(w.shape, F32)] * 4,
        compiler_params=_cparams("parallel", "parallel"))(*parts, w, m, v)


_SLAB_AT = {
    "sgu_w_spatial": (0, 0, SGU_C, LANES),
    "sgu_b_spatial": (128, 0, NH, SGU_C),
    "norm1_w": (132, 0, 1, D),
    "norm2_w": (133, 0, 1, D),
    "sgu_ln_w": (134, 0, 1, G),
    "sgu_ln_b": (134, 256, 1, G),
    "gla_gate_bias": (134, 512, 1, G),
    "dn_norm_w": (134, 768, 1, HD),
    "gla_norm_w": (134, 896, 1, HD),
    "dn_a_log": (135, 0, 1, NH),
    "dn_dt_bias": (135, 128, 1, NH),
    "gla_w_gate2": (136, 0, 16, G),
    "dn_conv_w": (136, 256, 4, 3 * G),
    "sc_conv_w": (140, 256, 3, G),
}
_LAYER_ROWS = 152
_FINAL_ROW = DEPTH * _LAYER_ROWS
_SLAB_ROWS, _SLAB_COLS = _FINAL_ROW + 8, 1024
_SLAB_ORDER = tuple(_SLAB_AT)
_SHARDED_SMALL = ("sc_conv_w", "dn_conv_w", "gla_w_gate2")
_REPLICATED = tuple(k for k in _SLAB_ORDER if k not in _SHARDED_SMALL)


def _region(name, l, h=0):
    r0, c0, nr, nc = _SLAB_AT[name]
    return slice(_LAYER_ROWS * l + r0, _LAYER_ROWS * l + r0 + nr), slice(c0 + nc * h, c0 + nc * (h + 1))


def _pack_small(layer_grads, d_final):
    def body(*refs):
        slab = refs[-1]
        slab[...] = jnp.zeros_like(slab)
        it = iter(refs[:-1])
        for l in range(DEPTH):
            for name in _SLAB_ORDER:
                ref = next(it)
                if name == "sgu_w_spatial":
                    for h in range(NH):
                        slab[_region(name, l, h)] = ref[h]
                else:
                    slab[_region(name, l)] = ref[...]
        slab[_FINAL_ROW:_FINAL_ROW + 1, :] = next(it)[...]

    flat = [layer_grads[l][name] for l in range(DEPTH) for name in _SLAB_ORDER] + [d_final]
    return pl.pallas_call(body, name="pack_small_grads", out_shape=jax.ShapeDtypeStruct((_SLAB_ROWS, _SLAB_COLS), F32),
                          compiler_params=_cparams())(*flat)


def _adamw_small(parts, w, m, v):
    names = _REPLICATED + ("final_norm_w",)
    n_in = len(names)

    def body(*refs):
        def total(rows, cols):
            g = refs[0][0, rows, cols]
            for k in range(1, N_DEV):
                g = g + refs[0][k, rows, cols]
            return g

        w_refs = dict(zip(names, refs[1:1 + n_in]))
        m_refs = dict(zip(names, refs[1 + n_in:1 + 2 * n_in]))
        v_refs = dict(zip(names, refs[1 + 2 * n_in:1 + 3 * n_in]))
        outs = refs[1 + 3 * n_in:]
        out_refs = [dict(zip(names, outs[j * n_in:(j + 1) * n_in])) for j in range(4)]
        full_refs = dict(zip(_SHARDED_SMALL, outs[4 * n_in:]))

        def update(name, idx, g):
            res = (g,) + _adamw_math(w_refs[name][idx], g, m_refs[name][idx], v_refs[name][idx])
            for o, val in zip(out_refs, res):
                o[name][idx] = val

        for l in range(DEPTH):
            for name in _REPLICATED:
                if name == "sgu_w_spatial":
                    for h in range(NH):
                        update(name, (l, h), total(*_region(name, l, h)))
                elif name == "sgu_b_spatial":
                    update(name, (l,), total(*_region(name, l)))
                else:
                    update(name, (slice(l, l + 1), slice(None)), total(*_region(name, l)))
            for name in _SHARDED_SMALL:
                full_refs[name][l] = total(*_region(name, l))
        update("final_norm_w", (slice(None), slice(None)), total(slice(_FINAL_ROW, _FINAL_ROW + 1), slice(None)))

    shapes = [jax.ShapeDtypeStruct(w[k].shape, F32) for k in names]
    full_shapes = [jax.ShapeDtypeStruct((DEPTH,) + _SLAB_AT[k][2:], F32) for k in _SHARDED_SMALL]
    outs = pl.pallas_call(body, name="adamw_small", out_shape=shapes * 4 + full_shapes, compiler_params=_cparams())(
        parts, *[w[k] for k in names], *[m[k] for k in names], *[v[k] for k in names])
    result = [dict(zip(names, outs[j * n_in:(j + 1) * n_in])) for j in range(4)]
    return result, dict(zip(_SHARDED_SMALL, outs[4 * n_in:]))


def _adamw_shards(g, w, m, v):
    names = _SHARDED_SMALL
    n = len(names)

    def body(*refs):
        for j in range(n):
            g_v = refs[j][...]
            res = _adamw_math(refs[n + j][...], g_v, refs[2 * n + j][...], refs[3 * n + j][...])
            for o, val in zip(refs[4 * n + j::n], res):
                o[...] = val

    shapes = [jax.ShapeDtypeStruct(w[k].shape, F32) for k in names]
    outs = pl.pallas_call(body, name="adamw_small_shards", out_shape=shapes * 3, compiler_params=_cparams())(
        *[g[k] for k in names], *[w[k] for k in names], *[m[k] for k in names], *[v[k] for k in names])
    return [dict(zip(names, outs[j * n:(j + 1) * n])) for j in range(3)]


_ANY = pl.BlockSpec(memory_space=pl.ANY)


def _place():
    return lax.axis_index("x"), lax.axis_index("y"), lax.axis_index("c")


def _sems(n):
    return [pltpu.SemaphoreType.DMA((n, 7)), pltpu.SemaphoreType.DMA((n, 7)), pltpu.SemaphoreType.DMA((n,))]


class _Gather:
    def __init__(self, blocks, second):
        self.inputs, self.second = list(blocks), list(second)
        self.out_shape = [jax.ShapeDtypeStruct((a.shape[0], N_DEV) + a.shape[1:] if s else (N_DEV,) + a.shape, a.dtype)
                          for a, s in zip(blocks, second)]
        self.scratch = _sems(len(blocks))

    def _copies(self, refs):
        x_refs, out_refs, (send_sems, recv_sems, local_sems) = refs
        n = len(x_refs)
        x, y, c = _place()
        me, sibling = (x, y, c), (x, y, 1 - c)
        chips = [(1 - x, y), (x, 1 - y), (1 - x, 1 - y)]

        def slot(j, px, py, pc):
            idx = 4 * px + 2 * py + pc
            return out_refs[j].at[:, idx] if self.second[j] else out_refs[j].at[idx]

        def copies(k, block, to, own=False):
            return [pltpu.make_async_remote_copy(
                src_ref=x_refs[j] if own else slot(j, *block), dst_ref=slot(j, *block),
                send_sem=send_sems.at[j, k], recv_sem=recv_sems.at[j, k], device_id=to,
                device_id_type=pl.DeviceIdType.MESH) for j in range(n)]

        mine = [pltpu.make_async_copy(x_refs[j], slot(j, *me), local_sems.at[j]) for j in range(n)]
        first = copies(0, me, sibling, own=True)
        for j, chip in enumerate(chips):
            first += copies(1 + j, me, (*chip, c), own=True)
        landed = [copies(1 + j, (*chip, c), me) for j, chip in enumerate(chips)]
        onward = [copies(4 + j, (*chip, c), sibling) for j, chip in enumerate(chips)]
        from_sibling = copies(0, sibling, me)
        for j, chip in enumerate(chips):
            from_sibling += copies(4 + j, (*chip, 1 - c), me)
        return mine, first, landed, onward, from_sibling

    def start(self, refs):
        mine, first, _, _, _ = self._copies(refs)
        for cp in mine + first:
            cp.start()

    def forward(self, refs):
        _, _, landed, onward, _ = self._copies(refs)
        for arrived, passed in zip(landed, onward):
            for cp in arrived:
                cp.wait_recv()
            for cp in passed:
                cp.start()

    def finish(self, refs):
        mine, first, _, onward, from_sibling = self._copies(refs)
        for cp in from_sibling:
            cp.wait_recv()
        for cp in first + [cp for passed in onward for cp in passed]:
            cp.wait_send()
        for cp in mine:
            cp.wait()


class _Exchange:
    def __init__(self, sends):
        self.inputs = list(sends)
        self.out_shape = [jax.ShapeDtypeStruct(a.shape, a.dtype) for a in sends]
        self.scratch = _sems(len(sends))

    def _copies(self, refs):
        x_refs, out_refs, (send_sems, recv_sems, local_sems) = refs
        n = len(x_refs)
        x, y, c = _place()
        me = 4 * x + 2 * y + c
        sent, landing = [], []
        for k in range(1, N_DEV):
            px, py, pc = x ^ ((k >> 2) & 1), y ^ ((k >> 1) & 1), c ^ (k & 1)
            them = 4 * px + 2 * py + pc
            for j in range(n):
                for here, there, into in ((them, me, sent), (me, them, landing)):
                    into.append(pltpu.make_async_remote_copy(
                        src_ref=x_refs[j].at[here], dst_ref=out_refs[j].at[there], send_sem=send_sems.at[j, k - 1],
                        recv_sem=recv_sems.at[j, k - 1], device_id=(px, py, pc), device_id_type=pl.DeviceIdType.MESH))
        mine = [pltpu.make_async_copy(x_refs[j].at[me], out_refs[j].at[me], local_sems.at[j]) for j in range(n)]
        return mine, sent, landing

    def start(self, refs):
        mine, sent, _ = self._copies(refs)
        for cp in mine + sent:
            cp.start()

    def forward(self, refs):
        pass

    def finish(self, refs):
        mine, sent, landing = self._copies(refs)
        for cp in landing:
            cp.wait_recv()
        for cp in sent:
            cp.wait_send()
        for cp in mine:
            cp.wait()


def _run(rider, name):
    n_in, n_out = len(rider.inputs), len(rider.out_shape)

    def body(*refs):
        parts = (refs[:n_in], refs[n_in:n_in + n_out], refs[n_in + n_out:])
        rider.start(parts)
        rider.forward(parts)
        rider.finish(parts)

    return pl.pallas_call(body, name=name, out_shape=rider.out_shape, in_specs=[_ANY] * n_in, out_specs=[_ANY] * n_out,
                          scratch_shapes=rider.scratch)(*rider.inputs)


def _split_refs(refs, n_in, n_out, n_scratch, rider):
    r_in = len(rider.inputs) if rider else 0
    r_out = len(rider.out_shape) if rider else 0
    a = n_in + r_in
    b = a + n_out + r_out
    own = refs[:n_in] + refs[a:a + n_out] + refs[b:b + n_scratch]
    return own, (refs[n_in:a], refs[a + n_out:b], refs[b + n_scratch:])


def _ride_begin(rider, ride_refs):
    if rider is not None:
        pl.when(pl.program_id(0) == 0)(lambda: rider.start(ride_refs))


def _ride_end(rider, ride_refs):
    if rider is not None:
        @pl.when(pl.program_id(0) == pl.num_programs(0) - 1)
        def _():
            rider.forward(ride_refs)
            rider.finish(ride_refs)


def _host(rider, in_specs, out_specs, out_shape, scratch, operands):
    if rider is None:
        return dict(in_specs=in_specs, out_specs=out_specs, out_shape=out_shape, scratch_shapes=scratch), operands
    return dict(in_specs=in_specs + [_ANY] * len(rider.inputs), out_specs=out_specs + [_ANY] * len(rider.out_shape),
                out_shape=out_shape + rider.out_shape, scratch_shapes=scratch + rider.scratch), operands + rider.inputs


_LATE = ("w_gate_up", "w_out", "w_down")


def _local_grads(x, target, w, late=None, exchange=False):
    w = dict(w)
    dn_params = (w["dn_a_log"], w["dn_dt_bias"], w["dn_norm_w"])
    gla_params = (w["gla_w_gate2"], w["gla_gate_bias"], w["gla_norm_w"])
    sgu_params = (w["sgu_ln_w"], w["sgu_ln_b"], w["sgu_w_spatial"], w["sgu_b_spatial"])
    saved = []
    for l in range(DEPTH):
        h, p = _in_proj(x, w["norm1_w"], w["w_in"], l)
        ya = _sgu_fwd(p, *sgu_params, l)
        yb = _sconv_fwd(p, w["sc_conv_w"], l)
        qkv_c = _qkv_conv_fwd(p, w["dn_conv_w"], l)
        if l == 0 and late is not None:
            yc, st_c, inv_c, w_gu, w_o, w_d = _dn_fwd(qkv_c, p, dn_params, l,
                                                      rider=_Gather([late[k] for k in _LATE], [False, True, True]))
            w.update(w_gate_up=w_gu, w_out=w_o.reshape(DEPTH, D, D), w_down=w_d.reshape(DEPTH, FF, D))
        else:
            yc, st_c, inv_c = _dn_fwd(qkv_c, p, dn_params, l)
        yd, st_d = _gla_fwd(p, gla_params, l)
        mix, x1, h2, gu, act, x2 = _out_mlp(x, (ya, yb, yc, yd), w["w_out"], w["norm2_w"], w["w_gate_up"], w["w_down"], l)
        saved.append(dict(x=x, h=h, p=p, qkv_c=qkv_c, st_c=st_c, inv_c=inv_c, st_d=st_d, mix=mix, x1=x1, h2=h2, gu=gu,
                          act=act))
        x = x2
    loss, d_final, dx = _loss_head(x, w["final_norm_w"], target)
    large = {k: [None] * DEPTH for k in ("w_in", "w_out", "w_gate_up", "w_down")}
    small = [None] * DEPTH
    for l in reversed(range(DEPTH)):
        s = saved[l]
        p = s["p"]
        g = {}
        riding = [(k, 1) for k in _LATE] if exchange and l == 0 else []
        dgu, dx1, dmix, g["norm2_w"], *arrived = _mlp_out_bwd(
            dx, s["x1"], s["gu"], w["norm2_w"], w["w_down"], w["w_gate_up"], w["w_out"], l,
            rider=_Exchange([large[k][j] for k, j in riding]) if riding else None)
        for (k, j), a in zip(riding, arrived):
            large[k][j] = a
        large["w_gate_up"][l] = _wgrad_gate_up(s["h2"], dgu)
        large["w_down"][l] = _wgrad_down(s["act"], dx).reshape(N_DEV, FF // N_DEV, D)
        large["w_out"][l] = _matmul_tn(s["mix"], dx1, "wgrad_out", BF16).reshape(N_DEV, D // N_DEV, D)
        d_a, g["sgu_ln_w"], g["sgu_ln_b"], g["sgu_w_spatial"], g["sgu_b_spatial"] = _sgu_bwd(p, dmix, *sgu_params, l)
        d_b, g["sc_conv_w"] = _sconv_bwd(p, dmix, w["sc_conv_w"], l)
        riding = [("w_in", 1)] + [(k, 0) for k in _LATE] if exchange and l == 0 else []
        dqkv_c, dz_c, dab, g["dn_a_log"], g["dn_dt_bias"], g["dn_norm_w"], *arrived = _dn_bwd(
            s["qkv_c"], p, s["st_c"], s["inv_c"], dmix, dn_params, l,
            rider=_Exchange([large[k][j] for k, j in riding]) if riding else None)
        for (k, j), a in zip(riding, arrived):
            large[k][j] = a
        d_c, g["dn_conv_w"] = _qkv_conv_bwd(p, dqkv_c, w["dn_conv_w"], l)
        d_d, dz_d, dglr, g["gla_w_gate2"], g["gla_gate_bias"], g["gla_norm_w"] = _gla_bwd(p, s["st_d"], dmix, gla_params, l)
        dp, dx, g["norm1_w"] = _in_proj_bwd((d_c, d_d, d_a, d_b, dz_c, dz_d, dab, dglr), s["x"], dx1, w["norm1_w"],
                                            w["w_in"], l)
        large["w_in"][l] = _scatter_w_in_grad(_matmul_tn(s["h"], dp, "wgrad_in", F32))
        small[l] = g
    return loss[0, 0], dx, large, small, d_final


_ORDER = ("norm1_w", "w_in", "sgu_ln_w", "sgu_ln_b", "sgu_w_spatial", "sgu_b_spatial", "sc_conv_w", "dn_conv_w",
          "dn_a_log", "dn_dt_bias", "dn_norm_w", "gla_w_gate2", "gla_gate_bias", "gla_norm_w", "w_out", "norm2_w",
          "w_gate_up", "w_down", "final_norm_w")
_LARGE = ("w_in", "w_gate_up", "w_out", "w_down")
_LARGE_TILE = {"w_in": 256, "w_gate_up": 256, "w_out": 128, "w_down": 352}


def _gather_cols(g):
    return jnp.transpose(g, (1, 2, 0, 3)).reshape(g.shape[1], g.shape[2], N_DEV * g.shape[3])


def kernel(x, norm1_w, w_in, sgu_ln_w, sgu_ln_b, sgu_w_spatial, sgu_b_spatial, sc_conv_w, dn_conv_w, dn_a_log, dn_dt_bias, dn_norm_w, gla_w_gate2, gla_gate_bias, gla_norm_w, w_out, norm2_w, w_gate_up, w_down, final_norm_w, loss_target, m_norm1_w, m_w_in, m_sgu_ln_w, m_sgu_ln_b, m_sgu_w_spatial, m_sgu_b_spatial, m_sc_conv_w, m_dn_conv_w, m_dn_a_log, m_dn_dt_bias, m_dn_norm_w, m_gla_w_gate2, m_gla_gate_bias, m_gla_norm_w, m_w_out, m_norm2_w, m_w_gate_up, m_w_down, m_final_norm_w, v_norm1_w, v_w_in, v_sgu_ln_w, v_sgu_ln_b, v_sgu_w_spatial, v_sgu_b_spatial, v_sc_conv_w, v_dn_conv_w, v_dn_a_log, v_dn_dt_bias, v_dn_norm_w, v_gla_w_gate2, v_gla_gate_bias, v_gla_norm_w, v_w_out, v_norm2_w, v_w_gate_up, v_w_down, v_final_norm_w):
    args = dict(locals())
    wts = {k: args[k] for k in _ORDER}
    mom = {k: args["m_" + k] for k in _ORDER}
    var = {k: args["v_" + k] for k in _ORDER}
    for d in (wts, mom, var):
        d["final_norm_w"] = d["final_norm_w"][None]
    me = 4 * lax.axis_index("x") + 2 * lax.axis_index("y") + lax.axis_index("c")

    shards = {k: wts[k].astype(BF16) for k in _LARGE}
    got = _run(_Gather([shards["w_in"]] + [wts[k] for k in _SHARDED_SMALL], [False] * 4), "gather_w_in")
    full = dict(wts)
    full["w_in"] = _relayout_w_in(got[0])
    for k, g in zip(_SHARDED_SMALL, got[1:]):
        full[k] = _gather_cols(g)

    loss, dx, large, small, d_final = _local_grads(x[0], loss_target[0], full, late=shards, exchange=True)
    loss = lax.psum(loss, ("x", "y", "c"))

    large["w_in"][0], = _run(_Exchange([large["w_in"][0]]), "exchange_grads")
    result = {}
    for k in _LARGE:
        outs = _adamw_large(large[k], wts[k], mom[k], var[k], "adamw_" + k, _LARGE_TILE[k])
        for kind, a in zip(("grad", "delta", "new_m", "new_v"), outs):
            result[kind, k] = a

    slabs = _run(_Gather([_pack_small(small, d_final)], [False]), "gather_small_grads")[0]
    rep = _REPLICATED + ("final_norm_w",)
    outs, summed = _adamw_small(slabs, {k: wts[k] for k in rep}, {k: mom[k] for k in rep}, {k: var[k] for k in rep})
    for kind, d in zip(("grad", "delta", "new_m", "new_v"), outs):
        for k, a in d.items():
            result[kind, k] = a[0] if k == "final_norm_w" else a
    own = {k: lax.dynamic_slice_in_dim(summed[k], me * wts[k].shape[-1], wts[k].shape[-1], axis=2) for k in _SHARDED_SMALL}
    outs = _adamw_shards(own, {k: wts[k] for k in _SHARDED_SMALL}, {k: mom[k] for k in _SHARDED_SMALL},
                         {k: var[k] for k in _SHARDED_SMALL})
    for kind, d in zip(("grad", "delta", "new_m", "new_v"), [own] + outs):
        for k, a in d.items():
            result[kind, k] = a

    return (loss, dx[None], *[result[kind, k] for kind in ("grad", "delta", "new_m", "new_v") for k in _ORDER])
```

```python
import functools

import jax
import jax.numpy as jnp
from jax import lax
from jax.experimental import pallas as pl
from jax.experimental.pallas import tpu as pltpu

F32, BF16 = jnp.float32, jnp.bfloat16
DEPTH = 2
D = 1024
G = 256
NH, HD = 4, 64
FF = 2816
IN_COLS = 3352
P = 3584
EPS = 1e-6
SGU_C, DN_C, GLA_C = 128, 64, 64
N_DEV = 8
IN_SHARD = IN_COLS // N_DEV
GU_SHARD = 2 * FF // N_DEV
N_GATE = N_DEV // 2
LANES = 128
VMEM_LIMIT = 56 * 2 ** 20

_PAD_SEGS = ((1280, 2048, 0),
             (2312, 3080, 0),
             (0, 512, 0),
             (512, 1280, 0),
             (2056, 2312, 0),
             (3096, 3352, 0),
             (2048, 2056, 120),
             (3080, 3096, 112))

ADAM_LR, ADAM_B1, ADAM_B2, ADAM_EPS, ADAM_WD, ADAM_STEP = 0.001, 0.9, 0.999, 1e-08, 0.01, 10


def _cparams(*sem):
    return pltpu.CompilerParams(dimension_semantics=sem, vmem_limit_bytes=VMEM_LIMIT)


def _resident(shape):
    return pl.BlockSpec(shape, lambda *_: (0,) * len(shape), pipeline_mode=pl.Buffered(1))


def _layer(shape, l):
    return pl.BlockSpec((None,) + tuple(shape), lambda *_: (l,) + (0,) * len(shape), pipeline_mode=pl.Buffered(1))


def _acc(shape):
    return pl.BlockSpec(shape, lambda *_: (0,) * len(shape))


def _dg(a, b, ca, cb, precision=None):
    lead = a.ndim - 2
    batch = ((0,), (0,)) if lead else ((), ())
    return lax.dot_general(a, b, (((ca + lead,), (cb + lead,)), batch), preferred_element_type=F32, precision=precision)


def _make_dot(cast, precision):
    def raw(a, b, form):
        ca = 0 if form == "tn" else 1
        cb = 1 if form == "nt" else 0
        return _dg_any(cast(a), cast(b), ca, cb, precision)

    @functools.partial(jax.custom_vjp, nondiff_argnums=(2,))
    def dot(a, b, form):
        return raw(a, b, form)

    def fwd(a, b, form):
        return raw(a, b, form), (a, b)

    def bwd(form, res, g):
        a, b = res
        if form == "nn":
            return raw(g, b, "nt"), raw(a, g, "tn")
        if form == "nt":
            return raw(g, b, "nn"), raw(g, a, "tn")
        return raw(b, g, "nt"), raw(a, g, "nn")

    dot.defvjp(fwd, bwd)
    return dot


def _split(t):
    hi = t.astype(BF16)
    return hi, (t - hi.astype(F32)).astype(BF16)


def _dg3(a, b, ca, cb):
    (ah, al), (bh, bl) = a, b
    return _dg(ah, bh, ca, cb) + (_dg(ah, bl, ca, cb) + _dg(al, bh, ca, cb))


class _Split3:
    pass


def _dg_any(a, b, ca, cb, precision):
    if precision is _Split3:
        return _dg3(_split(a), _split(b), ca, cb)
    return _dg(a, b, ca, cb, precision)


_bdot = _make_dot(lambda t: t.astype(BF16), None)
_hdot = _make_dot(lambda t: t, _Split3)


def _inv_unit_lower(low):
    n = low.shape[-1]
    eye = (lax.broadcasted_iota(jnp.int32, (n, n), 0) == lax.broadcasted_iota(jnp.int32, (n, n), 1)).astype(F32)
    p = -low
    inv = eye + p
    ps = _split(p)
    k = 1
    while 2 * k < n:
        ps = _split(_dg3(ps, ps, 1, 0))
        inv = inv + _dg3(_split(inv), ps, 1, 0)
        k *= 2
    return inv


@jax.custom_vjp
def _unit_lower_solve(low, rhs, inv):
    return _dg3(_split(inv), _split(rhs), 1, 0)


def _uls_fwd(low, rhs, inv):
    sol = _dg3(_split(inv), _split(rhs), 1, 0)
    return sol, (inv, sol)


def _uls_bwd(res, g):
    inv, sol = res
    d_rhs = _dg3(_split(inv), _split(g), 0, 0)
    return -_dg3(_split(d_rhs), _split(sol), 1, 1), d_rhs, jnp.zeros_like(inv)


_unit_lower_solve.defvjp(_uls_fwd, _uls_bwd)


def _sigmoid(x):
    return 1.0 / (1.0 + jnp.exp(-x))


def _softplus(x):
    return jnp.maximum(x, 0.0) + jnp.log(1.0 + jnp.exp(-jnp.maximum(x, -x)))


def _gelu(x):
    return 0.5 * x * (1.0 + jnp.tanh(0.7978845608028654 * (x + 0.044715 * (x * x * x))))


def _tri(n):
    ri = lax.broadcasted_iota(jnp.int32, (n, n), 0)
    ci = lax.broadcasted_iota(jnp.int32, (n, n), 1)
    return ri, ci


def _stack(ts):
    return jnp.concatenate([t[None] for t in ts], axis=0)


def _sgu_chunk(uv, ln_w, ln_b, ws, bs):
    u = _gelu(uv[:, :G])
    v = _gelu(uv[:, G:])
    mu = jnp.mean(v, axis=-1, keepdims=True)
    vc = v - mu
    var = jnp.mean(vc * vc, axis=-1, keepdims=True)
    vn = vc * lax.rsqrt(var + EPS) * ln_w + ln_b
    ri, ci = _tri(SGU_C)
    bs_t = _hdot((ri == ci).astype(F32), bs, "nt")
    mixed = []
    for h in range(NH):
        wm = jnp.where(ri >= ci, ws[h], 0.0)
        mixed.append(_bdot(wm, vn[:, HD * h:HD * (h + 1)], "nn") + bs_t[:, h:h + 1])
    return u * jnp.concatenate(mixed, axis=1)


def _dn_tile(q, k, v, ab, z, state, inv, a_log, dt_bias, nw):
    c = DN_C
    tm = q.shape[0]
    cps = tm // c
    ri, ci = _tri(tm)
    shift = c.bit_length() - 1
    same = jnp.right_shift(ri, shift) == jnp.right_shift(ci, shift)
    g4 = -jnp.exp(a_log) * _softplus(ab[:, 0:NH] + dt_bias)
    beta4 = _sigmoid(ab[:, NH:2 * NH])
    gc4 = _hdot((same & (ri >= ci)).astype(F32), g4, "nn")
    gr4 = _hdot(g4, (same & (ri <= ci)).astype(F32), "tn")
    pairs = [(slice(c * j, c * (j + 1)), h) for j in range(cps) for h in range(NH)]
    heads = lambda t: _stack([t[rows, HD * h:HD * (h + 1)] for rows, h in pairs])
    col = lambda t: _stack([t[rows, h:h + 1] for rows, h in pairs])
    qn, kn, vh = heads(q), heads(k), heads(v)
    qn = qn * lax.rsqrt(jnp.sum(qn * qn, axis=-1, keepdims=True) + EPS) * (HD ** -0.5)
    kn = kn * lax.rsqrt(jnp.sum(kn * kn, axis=-1, keepdims=True) + EPS)
    gc, beta = col(gc4), col(beta4)
    gr = _stack([gr4[h:h + 1, rows] for rows, h in pairs])
    gl = jnp.sum(col(g4), axis=1, keepdims=True)
    r2, c2 = _tri(c)
    decay = jnp.exp(jnp.where(r2 >= c2, gc - gr, -jnp.inf))
    kb = kn * beta
    low = jnp.where(r2 > c2, _bdot(kb, kn, "nt") * decay, 0.0)
    eg = jnp.exp(gc)
    if inv is None:
        inv = _inv_unit_lower(low)
    sol = _unit_lower_solve(low, jnp.concatenate([vh * beta, kb * eg], axis=2), inv)
    u, w = sol[:, :, :HD], sol[:, :, HD:]
    attn = _bdot(qn, kn, "nt") * decay
    qg, kd, cd = qn * eg, kn * jnp.exp(gl - gc), jnp.exp(gl)
    ys = []
    for j in range(cps):
        b = slice(NH * j, NH * (j + 1))
        v_new = u[b] - _bdot(w[b], state, "nn")
        o = _bdot(qg[b], state, "nn") + _bdot(attn[b], v_new, "nn")
        state = state * cd[b] + _bdot(kd[b], v_new, "tn")
        on = o * lax.rsqrt(jnp.mean(o * o, axis=-1, keepdims=True) + EPS) * nw
        ys.append(jnp.concatenate([on[h] for h in range(NH)], axis=1))
    return jnp.concatenate(ys, axis=0) * (z * _sigmoid(z)), state, inv


def _gla_tile(q, k, v, glr, z, state_t, w2, gate_bias, nw):
    c = GLA_C
    tm = q.shape[0]
    cps = tm // c
    ri, ci = _tri(tm)
    shift = c.bit_length() - 1
    same = jnp.right_shift(ri, shift) == jnp.right_shift(ci, shift)
    w2_rows = jnp.concatenate([w2, jnp.zeros((LANES - w2.shape[0], G), F32)], axis=0)
    pre = _bdot(glr, w2_rows, "nn") + gate_bias
    log_a = (jnp.minimum(pre, 0.0) - jnp.log(1.0 + jnp.exp(-jnp.maximum(pre, -pre)))) * (1.0 / 16.0)
    gcum = _hdot((same & (ri >= ci)).astype(F32), log_a, "nn")
    row = lax.broadcasted_iota(jnp.int32, (c, G), 0)
    qs = q * (HD ** -0.5)
    qa, ka, qg, kl, dec = [], [], [], [], []
    for j in range(cps):
        rows = slice(c * j, c * (j + 1))
        gj = gcum[rows]
        g_mid = jnp.sum(jnp.where(row == c // 2, gj, 0.0), axis=0, keepdims=True)
        g_last = jnp.sum(log_a[rows], axis=0, keepdims=True)
        qa.append(qs[rows] * jnp.exp(gj - g_mid))
        ka.append(k[rows] * jnp.exp(g_mid - gj))
        qg.append(qs[rows] * jnp.exp(gj))
        kl.append(k[rows] * jnp.exp(g_last - gj))
        dec.append(jnp.exp(g_last))
    heads = lambda ts: _stack([t[:, HD * h:HD * (h + 1)] for t in ts for h in range(NH)])
    qa, ka, qg, kl, dec = heads(qa), heads(ka), heads(qg), heads(kl), heads(dec)
    vh = heads([v[c * j:c * (j + 1)] for j in range(cps)])
    r2, c2 = _tri(c)
    o_intra = _bdot(jnp.where(r2 >= c2, _bdot(qa, ka, "nt"), 0.0), vh, "nn")
    ys = []
    for j in range(cps):
        b = slice(NH * j, NH * (j + 1))
        o = o_intra[b] + _bdot(qg[b], state_t, "nt")
        state_t = state_t * dec[b] + _bdot(vh[b], kl[b], "tn")
        on = o * lax.rsqrt(jnp.mean(o * o, axis=-1, keepdims=True) + EPS) * nw
        ys.append(jnp.concatenate([on[h] for h in range(NH)], axis=1))
    return jnp.concatenate(ys, axis=0) * (z * _sigmoid(z)), state_t


def _rms(x, nw):
    r = lax.rsqrt(jnp.mean(x * x, axis=-1, keepdims=True) + EPS)
    xh = x * r
    return xh * nw, xh, r


def _rms_bwd(g, xh, r, nw):
    gw = g * nw
    return r * (gw - xh * jnp.mean(gw * xh, axis=-1, keepdims=True)), jnp.sum(g * xh, axis=0, keepdims=True)


def _row(tm, w, blk=0):
    return pl.BlockSpec((tm, w), lambda i: (i, blk))


def _in_proj(x, nw, wp, l, tm=512, rider=None):
    t = x.shape[0]

    def body(*refs):
        (x_ref, nw_ref, w_ref, h_ref, p_ref), ride = _split_refs(refs, 3, 2, 0, rider)
        _ride_begin(rider, ride)
        h = _rms(x_ref[...], nw_ref[l:l + 1, :])[0].astype(BF16)
        h_ref[...] = h
        p_ref[...] = jnp.dot(h, w_ref[...], preferred_element_type=F32)
        _ride_end(rider, ride)

    specs, operands = _host(
        rider, [_row(tm, D), _resident((DEPTH, D)), _resident((D, P))], [_row(tm, D), _row(tm, P)],
        [jax.ShapeDtypeStruct((t, D), BF16), jax.ShapeDtypeStruct((t, P), F32)], [], [x, nw, wp])
    return pl.pallas_call(body, name="in_proj", grid=(t // tm,), compiler_params=_cparams("arbitrary"),
                          **specs)(*operands)


def _gu_spec(l):
    return pl.BlockSpec((N_DEV, None, D, GU_SHARD), lambda *_: (0, l, 0, 0), pipeline_mode=pl.Buffered(1))


def _out_mlp(x, ys, w_out, nw2, w_gu, w_down, l, tm=256):
    t = x.shape[0]

    def body(x_ref, ya, yb, yc, yd, wo_ref, nw_ref, wgu_ref, wd_ref, mix_ref, x1_ref, h2_ref, gu_ref, act_ref, x2_ref):
        mix = jnp.concatenate([ya[...], yb[...], yc[...], yd[...]], axis=1)
        mix_ref[...] = mix
        x1 = x_ref[...] + jnp.dot(mix, wo_ref[...], preferred_element_type=F32)
        x1_ref[...] = x1
        h2 = _rms(x1, nw_ref[l:l + 1, :])[0].astype(BF16)
        h2_ref[...] = h2
        x2 = x1
        for d in range(N_GATE):
            gate = jnp.dot(h2, wgu_ref[d], preferred_element_type=F32)
            up = jnp.dot(h2, wgu_ref[d + N_GATE], preferred_element_type=F32)
            gu_ref[d] = gate.astype(BF16)
            gu_ref[d + N_GATE] = up.astype(BF16)
            act = (gate * _sigmoid(gate) * up).astype(BF16)
            act_ref[d] = act
            x2 = x2 + jnp.dot(act, wd_ref[GU_SHARD * d:GU_SHARD * (d + 1), :], preferred_element_type=F32)
        x2_ref[...] = x2

    dev = lambda n: pl.BlockSpec((n, tm, GU_SHARD), lambda i: (0, i, 0))
    return pl.pallas_call(
        body, name="out_mlp", grid=(t // tm,),
        in_specs=[_row(tm, D), _row(tm, G), _row(tm, G), _row(tm, G), _row(tm, G), _layer((D, D), l),
                  _resident((DEPTH, D)), _gu_spec(l), _layer((FF, D), l)],
        out_specs=[_row(tm, D), _row(tm, D), _row(tm, D), dev(N_DEV), dev(N_GATE), _row(tm, D)],
        out_shape=[jax.ShapeDtypeStruct((t, D), BF16), jax.ShapeDtypeStruct((t, D), F32),
                   jax.ShapeDtypeStruct((t, D), BF16), jax.ShapeDtypeStruct((N_DEV, t, GU_SHARD), BF16),
                   jax.ShapeDtypeStruct((N_GATE, t, GU_SHARD), BF16), jax.ShapeDtypeStruct((t, D), F32)],
        compiler_params=_cparams("parallel"))(x, *ys, w_out, nw2, w_gu, w_down)


def _loss_head(x, nw, target, tm=512):
    t = x.shape[0]

    def body(x_ref, nw_ref, tg_ref, loss_ref, dnw_ref, dx_ref):
        @pl.when(pl.program_id(0) == 0)
        def _():
            loss_ref[...] = jnp.zeros_like(loss_ref)
            dnw_ref[...] = jnp.zeros_like(dnw_ref)
        nw_v = nw_ref[...]
        y, xh, r = _rms(x_ref[...], nw_v)
        err = y - tg_ref[...]
        loss_ref[...] += 0.5 * jnp.sum(err * err) * (1.0 / D)
        dx, dnw = _rms_bwd(err * (1.0 / D), xh, r, nw_v)
        dx_ref[...] = dx
        dnw_ref[...] += dnw

    return pl.pallas_call(
        body, name="loss_head", grid=(t // tm,),
        in_specs=[_row(tm, D), _resident((1, D)), _row(tm, D)],
        out_specs=[_acc((8, LANES)), _acc((1, D)), _row(tm, D)],
        out_shape=[jax.ShapeDtypeStruct((8, LANES), F32), jax.ShapeDtypeStruct((1, D), F32),
                   jax.ShapeDtypeStruct((t, D), F32)],
        compiler_params=_cparams("arbitrary"))(x, nw, target)


def _mlp_out_bwd(dx2, x1, gu, nw2, w_down, w_gu, w_out, l, tm=256, rider=None):
    t = dx2.shape[0]

    def body(*refs):
        (dx2_ref, x1_ref, gu_ref, nw_ref, wd_ref, wgu_ref, wo_ref, dgu_ref, dx1_ref, dmix_ref, dnw_ref), ride = \
            _split_refs(refs, 7, 4, 0, rider)
        _ride_begin(rider, ride)

        @pl.when(pl.program_id(0) == 0)
        def _():
            dnw_ref[...] = jnp.zeros_like(dnw_ref)
        dx2 = dx2_ref[...]
        dx2_b = dx2.astype(BF16)
        dh2 = jnp.zeros((tm, D), F32)
        for d in range(N_GATE):
            dact = _dg(dx2_b, wd_ref[GU_SHARD * d:GU_SHARD * (d + 1), :], 1, 1)
            gate, up = gu_ref[d].astype(F32), gu_ref[d + N_GATE].astype(F32)
            s = _sigmoid(gate)
            dgate = (dact * up * (s * (1.0 + gate * (1.0 - s)))).astype(BF16)
            dup = (dact * (gate * s)).astype(BF16)
            dgu_ref[d] = dgate
            dgu_ref[d + N_GATE] = dup
            dh2 = dh2 + _dg(dgate, wgu_ref[d], 1, 1) + _dg(dup, wgu_ref[d + N_GATE], 1, 1)
        nw_v = nw_ref[l:l + 1, :]
        _, xh, r = _rms(x1_ref[...], nw_v)
        dxn, dnw = _rms_bwd(dh2, xh, r, nw_v)
        dx1 = dx2 + dxn
        dx1_ref[...] = dx1
        dnw_ref[...] += dnw
        dmix_ref[...] = _dg(dx1.astype(BF16), wo_ref[...], 1, 1)
        _ride_end(rider, ride)

    dev = pl.BlockSpec((N_DEV, tm, GU_SHARD), lambda i: (0, i, 0))
    specs, operands = _host(
        rider, [_row(tm, D), _row(tm, D), dev, _resident((DEPTH, D)), _layer((FF, D), l), _gu_spec(l), _layer((D, D), l)],
        [dev, _row(tm, D), _row(tm, D), _acc((1, D))],
        [jax.ShapeDtypeStruct((N_DEV, t, GU_SHARD), BF16), jax.ShapeDtypeStruct((t, D), F32),
         jax.ShapeDtypeStruct((t, D), F32), jax.ShapeDtypeStruct((1, D), F32)],
        [], [dx2, x1, gu, nw2, w_down, w_gu, w_out])
    return pl.pallas_call(body, name="mlp_out_bwd", grid=(t // tm,), compiler_params=_cparams("arbitrary"),
                          **specs)(*operands)


_DP_WIDTHS = (768, 768, 512, 768, 256, 256, 128, 128)


def _in_proj_bwd(dps, x, dx1, nw, wp, l, tm=512):
    t = x.shape[0]

    def body(*refs):
        dp_refs, (x_ref, dx1_ref, nw_ref, w_ref, dp_ref, dx_ref, dnw_ref) = refs[:8], refs[8:]

        @pl.when(pl.program_id(0) == 0)
        def _():
            dnw_ref[...] = jnp.zeros_like(dnw_ref)
        dp = jnp.concatenate([r[...] for r in dp_refs], axis=1)
        dp_ref[...] = dp
        dh = _dg(dp, w_ref[...], 1, 1)
        nw_v = nw_ref[l:l + 1, :]
        _, xh, r = _rms(x_ref[...], nw_v)
        dxn, dnw = _rms_bwd(dh, xh, r, nw_v)
        dx_ref[...] = dx1_ref[...] + dxn
        dnw_ref[...] += dnw

    return pl.pallas_call(
        body, name="in_proj_bwd", grid=(t // tm,),
        in_specs=[_row(tm, w) for w in _DP_WIDTHS] + [_row(tm, D), _row(tm, D), _resident((DEPTH, D)), _resident((D, P))],
        out_specs=[_row(tm, P), _row(tm, D), _acc((1, D))],
        out_shape=[jax.ShapeDtypeStruct((t, P), BF16), jax.ShapeDtypeStruct((t, D), F32),
                   jax.ShapeDtypeStruct((1, D), F32)],
        compiler_params=_cparams("arbitrary"))(*dps, x, dx1, nw, wp)


def _accumulate(acc, o_ref, t_axis, term):
    @pl.when(pl.program_id(t_axis) == 0)
    def _():
        acc[...] = jnp.zeros_like(acc)
    acc[...] += term

    @pl.when(pl.program_id(t_axis) == pl.num_programs(t_axis) - 1)
    def _():
        o_ref[...] = acc[...].astype(o_ref.dtype)


WGRAD_TOKENS = 2048


def _matmul_tn(a, b, name, out_dtype, tn=512, tt=WGRAD_TOKENS):
    t, m = a.shape
    n = b.shape[1]
    tt = min(tt, t)

    def body(a_ref, b_ref, o_ref, acc):
        _accumulate(acc, o_ref, 1, _dg(a_ref[...].astype(BF16), b_ref[...].astype(BF16), 0, 0))

    return pl.pallas_call(
        body, name=name, grid=(n // tn, t // tt),
        in_specs=[pl.BlockSpec((tt, m), lambda j, i: (i, 0)), pl.BlockSpec((tt, tn), lambda j, i: (i, j))],
        out_specs=pl.BlockSpec((m, tn), lambda j, i: (0, j)),
        out_shape=jax.ShapeDtypeStruct((m, n), out_dtype),
        scratch_shapes=[pltpu.VMEM((m, tn), F32)],
        compiler_params=_cparams("parallel", "arbitrary"))(a, b)


def _wgrad_gate_up(h2, dgu, tt=WGRAD_TOKENS):
    t = h2.shape[0]
    tt = min(tt, t)

    def body(a_ref, b_ref, o_ref, acc):
        _accumulate(acc, o_ref, 1, _dg(a_ref[...], b_ref[...], 0, 0))

    return pl.pallas_call(
        body, name="wgrad_gate_up", grid=(N_DEV, t // tt),
        in_specs=[pl.BlockSpec((tt, D), lambda d, i: (i, 0)), pl.BlockSpec((None, tt, GU_SHARD), lambda d, i: (d, i, 0))],
        out_specs=pl.BlockSpec((None, D, GU_SHARD), lambda d, i: (d, 0, 0)),
        out_shape=jax.ShapeDtypeStruct((N_DEV, D, GU_SHARD), BF16),
        scratch_shapes=[pltpu.VMEM((D, GU_SHARD), F32)],
        compiler_params=_cparams("parallel", "arbitrary"))(h2, dgu)


def _wgrad_down(act, dx2, tt=WGRAD_TOKENS):
    t = dx2.shape[0]
    tt = min(tt, t)

    def body(a_ref, b_ref, o_ref, acc):
        _accumulate(acc, o_ref, 1, _dg(a_ref[...], b_ref[...].astype(BF16), 0, 0))

    return pl.pallas_call(
        body, name="wgrad_down", grid=(N_GATE, t // tt),
        in_specs=[pl.BlockSpec((None, tt, GU_SHARD), lambda d, i: (d, i, 0)), pl.BlockSpec((tt, D), lambda d, i: (i, 0))],
        out_specs=pl.BlockSpec((GU_SHARD, D), lambda d, i: (d, 0)),
        out_shape=jax.ShapeDtypeStruct((FF, D), BF16),
        scratch_shapes=[pltpu.VMEM((GU_SHARD, D), F32)],
        compiler_params=_cparams("parallel", "arbitrary"))(act, dx2)


def _relayout_w_in(g, tr=256):
    def body(w_ref, o_ref):
        full = jnp.concatenate([w_ref[d] for d in range(N_DEV)], axis=1)
        parts = []
        for a, b, z in _PAD_SEGS:
            parts.append(full[:, a:b])
            if z:
                parts.append(jnp.zeros((tr, z), full.dtype))
        o_ref[...] = jnp.concatenate(parts, axis=1)

    return pl.pallas_call(
        body, name="relayout_w_in", grid=(D // tr,),
        in_specs=[pl.BlockSpec((N_DEV, tr, IN_SHARD), lambda i: (0, i, 0))], out_specs=_row(tr, P),
        out_shape=jax.ShapeDtypeStruct((D, P), g.dtype), compiler_params=_cparams("parallel"))(g)


def _scatter_w_in_grad(gp, tr=256):
    def body(g_ref, o_ref):
        g = g_ref[...]
        pieces, off = [], 0
        for a, b, z in _PAD_SEGS:
            pieces.append((a, g[:, off:off + (b - a)]))
            off += (b - a) + z
        nat = jnp.concatenate([piece for _, piece in sorted(pieces, key=lambda ap: ap[0])], axis=1)
        for d in range(N_DEV):
            o_ref[d] = nat[:, IN_SHARD * d:IN_SHARD * (d + 1)].astype(BF16)

    return pl.pallas_call(
        body, name="scatter_w_in_grad", grid=(D // tr,),
        in_specs=[pl.BlockSpec((tr, P), lambda i: (i, 0))],
        out_specs=pl.BlockSpec((N_DEV, tr, IN_SHARD), lambda i: (0, i, 0)),
        out_shape=jax.ShapeDtypeStruct((N_DEV, D, IN_SHARD), BF16),
        compiler_params=_cparams("parallel"))(gp)


_A_BLK = 3


def _sgu_specs(l):
    return [_resident((DEPTH, G)), _resident((DEPTH, G)), _layer((NH, SGU_C, SGU_C), l), _layer((NH, SGU_C), l)]


def _sgu_fwd(p, ln_w, ln_b, ws, bs, l, tm=256):
    t = p.shape[0]

    def body(uv_ref, lw_ref, lb_ref, ws_ref, bs_ref, y_ref):
        ws_v = [ws_ref[h] for h in range(NH)]
        for c in range(tm // SGU_C):
            rows = pl.ds(c * SGU_C, SGU_C)
            y_ref[rows, :] = _sgu_chunk(uv_ref[rows, :], lw_ref[l:l + 1, :], lb_ref[l:l + 1, :], ws_v,
                                        bs_ref[...]).astype(BF16)

    return pl.pallas_call(
        body, name="sgu_fwd", grid=(t // tm,),
        in_specs=[_row(tm, 2 * G, _A_BLK)] + _sgu_specs(l), out_specs=_row(tm, G),
        out_shape=jax.ShapeDtypeStruct((t, G), BF16),
        compiler_params=_cparams("parallel"))(p, ln_w, ln_b, ws, bs)


def _sgu_bwd(p, dmix, ln_w, ln_b, ws, bs, l, tm=256):
    t = p.shape[0]

    def body(uv_ref, dy_ref, lw_ref, lb_ref, ws_ref, bs_ref, duv_ref, dlw_ref, dlb_ref, dws_ref, dbs_ref):
        @pl.when(pl.program_id(0) == 0)
        def _():
            for r in (dlw_ref, dlb_ref, dws_ref, dbs_ref):
                r[...] = jnp.zeros_like(r)
        ws_v = [ws_ref[h] for h in range(NH)]
        for c in range(tm // SGU_C):
            rows = pl.ds(c * SGU_C, SGU_C)
            _, vjp = jax.vjp(_sgu_chunk, uv_ref[rows, :], lw_ref[l:l + 1, :], lb_ref[l:l + 1, :], ws_v, bs_ref[...])
            duv, dlw, dlb, dws, dbs = vjp(dy_ref[rows, :])
            duv_ref[rows, :] = duv.astype(BF16)
            dlw_ref[...] += dlw
            dlb_ref[...] += dlb
            dbs_ref[...] += dbs
            for h in range(NH):
                dws_ref[h] += dws[h]

    return pl.pallas_call(
        body, name="sgu_bwd", grid=(t // tm,),
        in_specs=[_row(tm, 2 * G, _A_BLK), _row(tm, G, 0)] + _sgu_specs(l),
        out_specs=[_row(tm, 2 * G), _acc((1, G)), _acc((1, G)), _acc((NH, SGU_C, SGU_C)), _acc((NH, SGU_C))],
        out_shape=[jax.ShapeDtypeStruct((t, 2 * G), BF16), jax.ShapeDtypeStruct((1, G), F32),
                   jax.ShapeDtypeStruct((1, G), F32), jax.ShapeDtypeStruct((NH, SGU_C, SGU_C), F32),
                   jax.ShapeDtypeStruct((NH, SGU_C), F32)],
        compiler_params=_cparams("arbitrary"))(p, dmix, ln_w, ln_b, ws, bs)


HALO = 8


def _prev_halo(tm, width, col):
    return pl.BlockSpec((HALO, width), lambda i: (jnp.maximum(i * (tm // HALO) - 1, 0), col))


def _next_halo(tm, width, col, t):
    return pl.BlockSpec((HALO, width), lambda i: (jnp.minimum((i + 1) * (tm // HALO), t // HALO - 1), col))


def _with_prev(halo_ref, x_ref):
    halo = jnp.where(pl.program_id(0) == 0, 0.0, halo_ref[...])
    return jnp.concatenate([halo, x_ref[...]], axis=0)


def _with_next(x_ref, halo_ref):
    halo = jnp.where(pl.program_id(0) == pl.num_programs(0) - 1, 0.0, halo_ref[...])
    return jnp.concatenate([x_ref[...], halo], axis=0)


def _delay(ext, j):
    return ext if j == 0 else pltpu.roll(ext, j, axis=0)


def _advance(ext, j):
    return ext if j == 0 else pltpu.roll(ext, ext.shape[0] - j, axis=0)


def _causal_conv(ext, w, tm):
    kw = w.shape[0]
    out = None
    for k in range(kw):
        term = _delay(ext, kw - 1 - k)[HALO:HALO + tm] * w[k:k + 1, :]
        out = term if out is None else out + term
    return out


def _causal_conv_t(ext, w, tm):
    kw = w.shape[0]
    out = None
    for k in range(kw):
        term = _advance(ext, kw - 1 - k)[0:tm] * w[k:k + 1, :]
        out = term if out is None else out + term
    return out


def _conv_wgrad(ext, dy, kw, tm):
    return jnp.concatenate(
        [jnp.sum(_delay(ext, kw - 1 - k)[HALO:HALO + tm] * dy, axis=0, keepdims=True) for k in range(kw)], axis=0)


_B_GB, _B_GC, _B_H = 8, 9, 10
_C_QKV, _D_QKV = 0, 1
_C_Z, _D_Z = 11, 12
_C_AB, _D_GLR = 26, 27


def _sconv_fwd(p, w, l, tm=512):
    t = p.shape[0]

    def body(gb_ref, gc_ref, h_ref, gc_halo, h_halo, w_ref, y_ref):
        s_ext = _with_prev(gc_halo, gc_ref) * _with_prev(h_halo, h_ref)
        y_ref[...] = (gb_ref[...] * _causal_conv(s_ext, w_ref[...], tm)).astype(BF16)

    return pl.pallas_call(
        body, name="sconv_fwd", grid=(t // tm,),
        in_specs=[_row(tm, G, _B_GB), _row(tm, G, _B_GC), _row(tm, G, _B_H), _prev_halo(tm, G, _B_GC),
                  _prev_halo(tm, G, _B_H), _layer((3, G), l)],
        out_specs=_row(tm, G), out_shape=jax.ShapeDtypeStruct((t, G), BF16),
        compiler_params=_cparams("parallel"))(p, p, p, p, p, w)


def _sconv_bwd(p, dmix, w, l, tm=512):
    t = p.shape[0]

    def body(gb_ref, gc_ref, h_ref, gc_halo, h_halo, gb_next, dy_ref, dy_next, w_ref, dp_ref, dw_ref):
        @pl.when(pl.program_id(0) == 0)
        def _():
            dw_ref[...] = jnp.zeros_like(dw_ref)
        w_v = w_ref[...]
        s_ext = _with_prev(gc_halo, gc_ref) * _with_prev(h_halo, h_ref)
        dout = dy_ref[...]
        d_gb = dout * _causal_conv(s_ext, w_v, tm)
        dconv_ext = _with_next(dy_ref, dy_next) * _with_next(gb_ref, gb_next)
        ds = _causal_conv_t(dconv_ext, w_v, tm)
        dw_ref[...] += _conv_wgrad(s_ext, dconv_ext[0:tm], 3, tm)
        dp_ref[...] = jnp.concatenate([d_gb, ds * h_ref[...], ds * gc_ref[...]], axis=1).astype(BF16)

    return pl.pallas_call(
        body, name="sconv_bwd", grid=(t // tm,),
        in_specs=[_row(tm, G, _B_GB), _row(tm, G, _B_GC), _row(tm, G, _B_H), _prev_halo(tm, G, _B_GC),
                  _prev_halo(tm, G, _B_H), _next_halo(tm, G, _B_GB, t), _row(tm, G, 1), _next_halo(tm, G, 1, t),
                  _layer((3, G), l)],
        out_specs=[_row(tm, 3 * G), _acc((3, G))],
        out_shape=[jax.ShapeDtypeStruct((t, 3 * G), BF16), jax.ShapeDtypeStruct((3, G), F32)],
        compiler_params=_cparams("arbitrary"))(p, p, p, p, p, p, dmix, dmix, w)


def _qkv_conv_fwd(p, w, l, tm=512):
    t = p.shape[0]

    def body(x_ref, x_halo, w_ref, y_ref):
        c = _causal_conv(_with_prev(x_halo, x_ref), w_ref[...], tm)
        y_ref[...] = c * _sigmoid(c)

    return pl.pallas_call(
        body, name="qkv_conv_fwd", grid=(t // tm,),
        in_specs=[_row(tm, 3 * G, _C_QKV), _prev_halo(tm, 3 * G, _C_QKV), _layer((4, 3 * G), l)],
        out_specs=_row(tm, 3 * G), out_shape=jax.ShapeDtypeStruct((t, 3 * G), F32),
        compiler_params=_cparams("parallel"))(p, p, w)


def _qkv_conv_bwd(p, dy, w, l, tm=512):
    t = p.shape[0]

    def body(x_ref, x_halo, x_next, dy_ref, dy_next, w_ref, dx_ref, dw_ref):
        @pl.when(pl.program_id(0) == 0)
        def _():
            dw_ref[...] = jnp.zeros_like(dw_ref)
        w_v = w_ref[...]
        x_all = jnp.concatenate([_with_prev(x_halo, x_ref), jnp.where(
            pl.program_id(0) == pl.num_programs(0) - 1, 0.0, x_next[...])], axis=0)
        c = _causal_conv(x_all, w_v, tm + HALO)
        s = _sigmoid(c)
        dc_ext = _with_next(dy_ref, dy_next) * (s * (1.0 + c * (1.0 - s)))
        dx_ref[...] = _causal_conv_t(dc_ext, w_v, tm).astype(BF16)
        dw_ref[...] += _conv_wgrad(x_all[0:HALO + tm], dc_ext[0:tm], 4, tm)

    return pl.pallas_call(
        body, name="qkv_conv_bwd", grid=(t // tm,),
        in_specs=[_row(tm, 3 * G, _C_QKV), _prev_halo(tm, 3 * G, _C_QKV), _next_halo(tm, 3 * G, _C_QKV, t),
                  _row(tm, 3 * G), _next_halo(tm, 3 * G, 0, t), _layer((4, 3 * G), l)],
        out_specs=[_row(tm, 3 * G), _acc((4, 3 * G))],
        out_shape=[jax.ShapeDtypeStruct((t, 3 * G), BF16), jax.ShapeDtypeStruct((4, 3 * G), F32)],
        compiler_params=_cparams("arbitrary"))(p, p, p, dy, dy, w)


_STATE = pl.BlockSpec((1, NH, HD, HD), lambda i: (i, 0, 0, 0))


def _dn_specs():
    return [_resident((DEPTH, NH)), _resident((DEPTH, NH)), _resident((DEPTH, HD))]


def _dn_fwd(qkv, p, params, l, cps=4, rider=None):
    t = qkv.shape[0]
    tm = DN_C * cps
    nb = cps * NH

    def body(*refs):
        (qkv_ref, z_ref, ab_ref, alog_ref, dt_ref, nw_ref, y_ref, st_ref, inv_ref, state), ride = _split_refs(
            refs, 6, 3, 1, rider)
        _ride_begin(rider, ride)

        @pl.when(pl.program_id(0) == 0)
        def _():
            state[...] = jnp.zeros_like(state)
        st_ref[0] = state[...]
        y, new, inv = _dn_tile(qkv_ref[:, 0:G], qkv_ref[:, G:2 * G], qkv_ref[:, 2 * G:3 * G], ab_ref[...], z_ref[...],
                               state[...], None, alog_ref[l:l + 1, :], dt_ref[l:l + 1, :], nw_ref[l:l + 1, :])
        y_ref[...] = y.astype(BF16)
        inv_ref[...] = inv
        state[...] = new
        _ride_end(rider, ride)

    specs, operands = _host(
        rider, [_row(tm, 3 * G), _row(tm, G, _C_Z), _row(tm, LANES, _C_AB)] + _dn_specs(),
        [_row(tm, G), _STATE, pl.BlockSpec((nb, DN_C, DN_C), lambda i: (i, 0, 0))],
        [jax.ShapeDtypeStruct((t, G), BF16), jax.ShapeDtypeStruct((t // tm, NH, HD, HD), F32),
         jax.ShapeDtypeStruct((t // DN_C * NH, DN_C, DN_C), F32)],
        [pltpu.VMEM((NH, HD, HD), F32)], [qkv, p, p, *params])
    return pl.pallas_call(body, name="dn_fwd", grid=(t // tm,), compiler_params=_cparams("arbitrary"), **specs)(*operands)


def _dn_bwd(qkv, p, states, inv, dmix, params, l, cps=4, rider=None):
    t = qkv.shape[0]
    tm = DN_C * cps
    n = t // tm
    nb = cps * NH

    def body(*refs):
        (qkv_ref, z_ref, ab_ref, st_ref, inv_ref, dy_ref, alog_ref, dt_ref, nw_ref,
         dqkv_ref, dz_ref, dab_ref, dalog_ref, ddt_ref, dnw_ref, dstate), ride = _split_refs(refs, 9, 6, 1, rider)
        _ride_begin(rider, ride)
        dprm_refs = (dalog_ref, ddt_ref, dnw_ref)

        @pl.when(pl.program_id(0) == 0)
        def _():
            dstate[...] = jnp.zeros_like(dstate)
            for r in dprm_refs:
                r[...] = jnp.zeros_like(r)
        inv_v = inv_ref[...]
        tile = lambda q, k, v, ab, z, st, alog, dt, nw: _dn_tile(q, k, v, ab, z, st, inv_v, alog, dt, nw)[:2]
        _, vjp = jax.vjp(tile, qkv_ref[:, 0:G], qkv_ref[:, G:2 * G], qkv_ref[:, 2 * G:3 * G], ab_ref[...], z_ref[...],
                         st_ref[0], alog_ref[l:l + 1, :], dt_ref[l:l + 1, :], nw_ref[l:l + 1, :])
        dq, dk, dv, dab, dz, dst, *dprm = vjp((dy_ref[...], dstate[...]))
        dqkv_ref[...] = jnp.concatenate([dq, dk, dv], axis=1)
        dz_ref[...] = dz.astype(BF16)
        dab_ref[...] = dab.astype(BF16)
        dstate[...] = dst
        for r, g in zip(dprm_refs, dprm):
            r[...] += g
        _ride_end(rider, ride)

    rev = lambda w, blk: pl.BlockSpec((tm, w), lambda i: (n - 1 - i, blk))
    specs, operands = _host(
        rider, [rev(3 * G, 0), rev(G, _C_Z), rev(LANES, _C_AB),
                pl.BlockSpec((1, NH, HD, HD), lambda i: (n - 1 - i, 0, 0, 0)),
                pl.BlockSpec((nb, DN_C, DN_C), lambda i: (n - 1 - i, 0, 0)), rev(G, 2)] + _dn_specs(),
        [rev(3 * G, 0), rev(G, 0), rev(LANES, 0), _acc((1, NH)), _acc((1, NH)), _acc((1, HD))],
        [jax.ShapeDtypeStruct((t, 3 * G), F32), jax.ShapeDtypeStruct((t, G), BF16),
         jax.ShapeDtypeStruct((t, LANES), BF16), jax.ShapeDtypeStruct((1, NH), F32),
         jax.ShapeDtypeStruct((1, NH), F32), jax.ShapeDtypeStruct((1, HD), F32)],
        [pltpu.VMEM((NH, HD, HD), F32)], [qkv, p, p, states, inv, dmix, *params])
    return pl.pallas_call(body, name="dn_bwd", grid=(n,), compiler_params=_cparams("arbitrary"), **specs)(*operands)


def _gla_specs(l):
    return [_layer((16, G), l), _resident((DEPTH, G)), _resident((DEPTH, HD))]


def _gla_fwd(p, params, l, cps=4):
    t = p.shape[0]
    tm = GLA_C * cps

    def body(qkv_ref, z_ref, glr_ref, w2_ref, gb_ref, nw_ref, y_ref, st_ref, state):
        @pl.when(pl.program_id(0) == 0)
        def _():
            state[...] = jnp.zeros_like(state)
        st_ref[0] = state[...]
        y, new = _gla_tile(qkv_ref[:, 0:G], qkv_ref[:, G:2 * G], qkv_ref[:, 2 * G:3 * G], glr_ref[...], z_ref[...],
                           state[...], w2_ref[...], gb_ref[l:l + 1, :], nw_ref[l:l + 1, :])
        y_ref[...] = y.astype(BF16)
        state[...] = new

    return pl.pallas_call(
        body, name="gla_fwd", grid=(t // tm,),
        in_specs=[_row(tm, 3 * G, _D_QKV), _row(tm, G, _D_Z), _row(tm, LANES, _D_GLR)] + _gla_specs(l),
        out_specs=[_row(tm, G), _STATE],
        out_shape=[jax.ShapeDtypeStruct((t, G), BF16), jax.ShapeDtypeStruct((t // tm, NH, HD, HD), F32)],
        scratch_shapes=[pltpu.VMEM((NH, HD, HD), F32)],
        compiler_params=_cparams("arbitrary"))(p, p, p, *params)


def _gla_bwd(p, states, dmix, params, l, cps=4):
    t = p.shape[0]
    tm = GLA_C * cps
    n = t // tm

    def body(qkv_ref, z_ref, glr_ref, st_ref, dy_ref, w2_ref, gb_ref, nw_ref,
             dqkv_ref, dz_ref, dglr_ref, dw2_ref, dgb_ref, dnw_ref, dstate):
        dprm_refs = (dw2_ref, dgb_ref, dnw_ref)

        @pl.when(pl.program_id(0) == 0)
        def _():
            dstate[...] = jnp.zeros_like(dstate)
            for r in dprm_refs:
                r[...] = jnp.zeros_like(r)
        _, vjp = jax.vjp(_gla_tile, qkv_ref[:, 0:G], qkv_ref[:, G:2 * G], qkv_ref[:, 2 * G:3 * G], glr_ref[...],
                         z_ref[...], st_ref[0], w2_ref[...], gb_ref[l:l + 1, :], nw_ref[l:l + 1, :])
        dq, dk, dv, dglr, dz, dst, *dprm = vjp((dy_ref[...], dstate[...]))
        dqkv_ref[...] = jnp.concatenate([dq, dk, dv], axis=1).astype(BF16)
        dz_ref[...] = dz.astype(BF16)
        dglr_ref[...] = dglr.astype(BF16)
        dstate[...] = dst
        for r, g in zip(dprm_refs, dprm):
            r[...] += g

    rev = lambda w, blk: pl.BlockSpec((tm, w), lambda i: (n - 1 - i, blk))
    return pl.pallas_call(
        body, name="gla_bwd", grid=(n,),
        in_specs=[rev(3 * G, _D_QKV), rev(G, _D_Z), rev(LANES, _D_GLR),
                  pl.BlockSpec((1, NH, HD, HD), lambda i: (n - 1 - i, 0, 0, 0)), rev(G, 3)] + _gla_specs(l),
        out_specs=[rev(3 * G, 0), rev(G, 0), rev(LANES, 0), _acc((16, G)), _acc((1, G)), _acc((1, HD))],
        out_shape=[jax.ShapeDtypeStruct((t, 3 * G), BF16), jax.ShapeDtypeStruct((t, G), BF16),
                   jax.ShapeDtypeStruct((t, LANES), BF16), jax.ShapeDtypeStruct((16, G), F32),
                   jax.ShapeDtypeStruct((1, G), F32), jax.ShapeDtypeStruct((1, HD), F32)],
        scratch_shapes=[pltpu.VMEM((NH, HD, HD), F32)],
        compiler_params=_cparams("arbitrary"))(p, p, p, states, dmix, *params)


def _adamw_math(w, g, m, v):
    m = ADAM_B1 * m + (1.0 - ADAM_B1) * g
    v = ADAM_B2 * v + (1.0 - ADAM_B2) * (g * g)
    m_hat = m / (1.0 - ADAM_B1 ** ADAM_STEP)
    v_hat = v / (1.0 - ADAM_B2 ** ADAM_STEP)
    return -ADAM_LR * (m_hat / (jnp.sqrt(v_hat) + ADAM_EPS) + ADAM_WD * w), m, v


def _adamw_large(parts, w, m, v, name, tr):
    _, r, c = w.shape

    def body(p0_ref, p1_ref, w_ref, m_ref, v_ref, g_ref, d_ref, nm_ref, nv_ref):
        def total(p_ref):
            g = p_ref[0].astype(F32)
            for k in range(1, N_DEV):
                g = g + p_ref[k].astype(F32)
            return g

        g = jnp.where(pl.program_id(0) == 0, total(p0_ref), total(p1_ref))
        g_ref[...] = g
        d_ref[...], nm_ref[...], nv_ref[...] = _adamw_math(w_ref[...], g, m_ref[...], v_ref[...])

    blk = pl.BlockSpec((None, tr, c), lambda l, i: (l, i, 0))
    part = pl.BlockSpec((N_DEV, tr, c), lambda l, i: (0, i, 0))
    return pl.pallas_call(
        body, name=name, grid=(DEPTH, r // tr), in_specs=[part, part, blk, blk, blk],
        out_specs=[blk] * 4, out_shape=[jax.ShapeDtypeStruct(w.shape, F32)] * 4,
        compiler_params=_cparams("parallel", "parallel"))(*parts, w, m, v)


_SLAB_AT = {
    "sgu_w_spatial": (0, 0, SGU_C, LANES),
    "sgu_b_spatial": (128, 0, NH, SGU_C),
    "norm1_w": (132, 0, 1, D),
    "norm2_w": (133, 0, 1, D),
    "sgu_ln_w": (134, 0, 1, G),
    "sgu_ln_b": (134, 256, 1, G),
    "gla_gate_bias": (134, 512, 1, G),
    "dn_norm_w": (134, 768, 1, HD),
    "gla_norm_w": (134, 896, 1, HD),
    "dn_a_log": (135, 0, 1, NH),
    "dn_dt_bias": (135, 128, 1, NH),
    "gla_w_gate2": (136, 0, 16, G),
    "dn_conv_w": (136, 256, 4, 3 * G),
    "sc_conv_w": (140, 256, 3, G),
}
_LAYER_ROWS = 152
_FINAL_ROW = DEPTH * _LAYER_ROWS
_SLAB_ROWS, _SLAB_COLS = _FINAL_ROW + 8, 1024
_SLAB_ORDER = tuple(_SLAB_AT)
_SHARDED_SMALL = ("sc_conv_w", "dn_conv_w", "gla_w_gate2")
_REPLICATED = tuple(k for k in _SLAB_ORDER if k not in _SHARDED_SMALL)


def _region(name, l, h=0):
    r0, c0, nr, nc = _SLAB_AT[name]
    return slice(_LAYER_ROWS * l + r0, _LAYER_ROWS * l + r0 + nr), slice(c0 + nc * h, c0 + nc * (h + 1))


def _pack_small(layer_grads, d_final):
    def body(*refs):
        slab = refs[-1]
        slab[...] = jnp.zeros_like(slab)
        it = iter(refs[:-1])
        for l in range(DEPTH):
            for name in _SLAB_ORDER:
                ref = next(it)
                if name == "sgu_w_spatial":
                    for h in range(NH):
                        slab[_region(name, l, h)] = ref[h]
                else:
                    slab[_region(name, l)] = ref[...]
        slab[_FINAL_ROW:_FINAL_ROW + 1, :] = next(it)[...]

    flat = [layer_grads[l][name] for l in range(DEPTH) for name in _SLAB_ORDER] + [d_final]
    return pl.pallas_call(body, name="pack_small_grads", out_shape=jax.ShapeDtypeStruct((_SLAB_ROWS, _SLAB_COLS), F32),
                          compiler_params=_cparams())(*flat)


def _adamw_small(parts, w, m, v):
    names = _REPLICATED + ("final_norm_w",)
    n_in = len(names)

    def body(*refs):
        def total(rows, cols):
            g = refs[0][0, rows, cols]
            for k in range(1, N_DEV):
                g = g + refs[0][k, rows, cols]
            return g

        w_refs = dict(zip(names, refs[1:1 + n_in]))
        m_refs = dict(zip(names, refs[1 + n_in:1 + 2 * n_in]))
        v_refs = dict(zip(names, refs[1 + 2 * n_in:1 + 3 * n_in]))
        outs = refs[1 + 3 * n_in:]
        out_refs = [dict(zip(names, outs[j * n_in:(j + 1) * n_in])) for j in range(4)]
        full_refs = dict(zip(_SHARDED_SMALL, outs[4 * n_in:]))

        def update(name, idx, g):
            res = (g,) + _adamw_math(w_refs[name][idx], g, m_refs[name][idx], v_refs[name][idx])
            for o, val in zip(out_refs, res):
                o[name][idx] = val

        for l in range(DEPTH):
            for name in _REPLICATED:
                if name == "sgu_w_spatial":
                    for h in range(NH):
                        update(name, (l, h), total(*_region(name, l, h)))
                elif name == "sgu_b_spatial":
                    update(name, (l,), total(*_region(name, l)))
                else:
                    update(name, (slice(l, l + 1), slice(None)), total(*_region(name, l)))
            for name in _SHARDED_SMALL:
                full_refs[name][l] = total(*_region(name, l))
        update("final_norm_w", (slice(None), slice(None)), total(slice(_FINAL_ROW, _FINAL_ROW + 1), slice(None)))

    shapes = [jax.ShapeDtypeStruct(w[k].shape, F32) for k in names]
    full_shapes = [jax.ShapeDtypeStruct((DEPTH,) + _SLAB_AT[k][2:], F32) for k in _SHARDED_SMALL]
    outs = pl.pallas_call(body, name="adamw_small", out_shape=shapes * 4 + full_shapes, compiler_params=_cparams())(
        parts, *[w[k] for k in names], *[m[k] for k in names], *[v[k] for k in names])
    result = [dict(zip(names, outs[j * n_in:(j + 1) * n_in])) for j in range(4)]
    return result, dict(zip(_SHARDED_SMALL, outs[4 * n_in:]))


def _adamw_shards(g, w, m, v):
    names = _SHARDED_SMALL
    n = len(names)

    def body(*refs):
        for j in range(n):
            g_v = refs[j][...]
            res = _adamw_math(refs[n + j][...], g_v, refs[2 * n + j][...], refs[3 * n + j][...])
            for o, val in zip(refs[4 * n + j::n], res):
                o[...] = val

    shapes = [jax.ShapeDtypeStruct(w[k].shape, F32) for k in names]
    outs = pl.pallas_call(body, name="adamw_small_shards", out_shape=shapes * 3, compiler_params=_cparams())(
        *[g[k] for k in names], *[w[k] for k in names], *[m[k] for k in names], *[v[k] for k in names])
    return [dict(zip(names, outs[j * n:(j + 1) * n])) for j in range(3)]


_ANY = pl.BlockSpec(memory_space=pl.ANY)


def _place():
    return lax.axis_index("x"), lax.axis_index("y"), lax.axis_index("c")


def _sems(n):
    return [pltpu.SemaphoreType.DMA((n, 7)), pltpu.SemaphoreType.DMA((n, 7)), pltpu.SemaphoreType.DMA((n,))]


class _Gather:
    def __init__(self, blocks, second):
        self.inputs, self.second = list(blocks), list(second)
        self.out_shape = [jax.ShapeDtypeStruct((a.shape[0], N_DEV) + a.shape[1:] if s else (N_DEV,) + a.shape, a.dtype)
                          for a, s in zip(blocks, second)]
        self.scratch = _sems(len(blocks))

    def _copies(self, refs):
        x_refs, out_refs, (send_sems, recv_sems, local_sems) = refs
        n = len(x_refs)
        x, y, c = _place()
        me, sibling = (x, y, c), (x, y, 1 - c)
        chips = [(1 - x, y), (x, 1 - y), (1 - x, 1 - y)]

        def slot(j, px, py, pc):
            idx = 4 * px + 2 * py + pc
            return out_refs[j].at[:, idx] if self.second[j] else out_refs[j].at[idx]

        def copies(k, block, to, own=False):
            return [pltpu.make_async_remote_copy(
                src_ref=x_refs[j] if own else slot(j, *block), dst_ref=slot(j, *block),
                send_sem=send_sems.at[j, k], recv_sem=recv_sems.at[j, k], device_id=to,
                device_id_type=pl.DeviceIdType.MESH) for j in range(n)]

        mine = [pltpu.make_async_copy(x_refs[j], slot(j, *me), local_sems.at[j]) for j in range(n)]
        first = copies(0, me, sibling, own=True)
        for j, chip in enumerate(chips):
            first += copies(1 + j, me, (*chip, c), own=True)
        landed = [copies(1 + j, (*chip, c), me) for j, chip in enumerate(chips)]
        onward = [copies(4 + j, (*chip, c), sibling) for j, chip in enumerate(chips)]
        from_sibling = copies(0, sibling, me)
        for j, chip in enumerate(chips):
            from_sibling += copies(4 + j, (*chip, 1 - c), me)
        return mine, first, landed, onward, from_sibling

    def start(self, refs):
        mine, first, _, _, _ = self._copies(refs)
        for cp in mine + first:
            cp.start()

    def forward(self, refs):
        _, _, landed, onward, _ = self._copies(refs)
        for arrived, passed in zip(landed, onward):
            for cp in arrived:
                cp.wait_recv()
            for cp in passed:
                cp.start()

    def finish(self, refs):
        mine, first, _, onward, from_sibling = self._copies(refs)
        for cp in from_sibling:
            cp.wait_recv()
        for cp in first + [cp for passed in onward for cp in passed]:
            cp.wait_send()
        for cp in mine:
            cp.wait()


class _Exchange:
    def __init__(self, sends):
        self.inputs = list(sends)
        self.out_shape = [jax.ShapeDtypeStruct(a.shape, a.dtype) for a in sends]
        self.scratch = _sems(len(sends))

    def _copies(self, refs):
        x_refs, out_refs, (send_sems, recv_sems, local_sems) = refs
        n = len(x_refs)
        x, y, c = _place()
        me = 4 * x + 2 * y + c
        sent, landing = [], []
        for k in range(1, N_DEV):
            px, py, pc = x ^ ((k >> 2) & 1), y ^ ((k >> 1) & 1), c ^ (k & 1)
            them = 4 * px + 2 * py + pc
            for j in range(n):
                for here, there, into in ((them, me, sent), (me, them, landing)):
                    into.append(pltpu.make_async_remote_copy(
                        src_ref=x_refs[j].at[here], dst_ref=out_refs[j].at[there], send_sem=send_sems.at[j, k - 1],
                        recv_sem=recv_sems.at[j, k - 1], device_id=(px, py, pc), device_id_type=pl.DeviceIdType.MESH))
        mine = [pltpu.make_async_copy(x_refs[j].at[me], out_refs[j].at[me], local_sems.at[j]) for j in range(n)]
        return mine, sent, landing

    def start(self, refs):
        mine, sent, _ = self._copies(refs)
        for cp in mine + sent:
            cp.start()

    def forward(self, refs):
        pass

    def finish(self, refs):
        mine, sent, landing = self._copies(refs)
        for cp in landing:
            cp.wait_recv()
        for cp in sent:
            cp.wait_send()
        for cp in mine:
            cp.wait()


def _run(rider, name):
    n_in, n_out = len(rider.inputs), len(rider.out_shape)

    def body(*refs):
        parts = (refs[:n_in], refs[n_in:n_in + n_out], refs[n_in + n_out:])
        rider.start(parts)
        rider.forward(parts)
        rider.finish(parts)

    return pl.pallas_call(body, name=name, out_shape=rider.out_shape, in_specs=[_ANY] * n_in, out_specs=[_ANY] * n_out,
                          scratch_shapes=rider.scratch)(*rider.inputs)


def _split_refs(refs, n_in, n_out, n_scratch, rider):
    r_in = len(rider.inputs) if rider else 0
    r_out = len(rider.out_shape) if rider else 0
    a = n_in + r_in
    b = a + n_out + r_out
    own = refs[:n_in] + refs[a:a + n_out] + refs[b:b + n_scratch]
    return own, (refs[n_in:a], refs[a + n_out:b], refs[b + n_scratch:])


def _ride_begin(rider, ride_refs):
    if rider is not None:
        pl.when(pl.program_id(0) == 0)(lambda: rider.start(ride_refs))


def _ride_end(rider, ride_refs):
    if rider is not None:
        @pl.when(pl.program_id(0) == pl.num_programs(0) - 1)
        def _():
            rider.forward(ride_refs)
            rider.finish(ride_refs)


def _host(rider, in_specs, out_specs, out_shape, scratch, operands):
    if rider is None:
        return dict(in_specs=in_specs, out_specs=out_specs, out_shape=out_shape, scratch_shapes=scratch), operands
    return dict(in_specs=in_specs + [_ANY] * len(rider.inputs), out_specs=out_specs + [_ANY] * len(rider.out_shape),
                out_shape=out_shape + rider.out_shape, scratch_shapes=scratch + rider.scratch), operands + rider.inputs


_LATE = ("w_gate_up", "w_out", "w_down")


def _local_grads(x, target, w, late=None, exchange=False):
    w = dict(w)
    w_in = list(w["w_in"])
    dn_params = (w["dn_a_log"], w["dn_dt_bias"], w["dn_norm_w"])
    gla_params = (w["gla_w_gate2"], w["gla_gate_bias"], w["gla_norm_w"])
    sgu_params = (w["sgu_ln_w"], w["sgu_ln_b"], w["sgu_w_spatial"], w["sgu_b_spatial"])
    saved = []
    for l in range(DEPTH):
        riding = l == 0 and late is not None
        h, p, *got = _in_proj(x, w["norm1_w"], w_in[l], l,
                              rider=_Gather([late["w_out"], late["w_down"]], [True, True]) if riding else None)
        if riding:
            w.update(w_out=got[0].reshape(DEPTH, D, D), w_down=got[1].reshape(DEPTH, FF, D))
        ya = _sgu_fwd(p, *sgu_params, l)
        yb = _sconv_fwd(p, w["sc_conv_w"], l)
        qkv_c = _qkv_conv_fwd(p, w["dn_conv_w"], l)
        yc, st_c, inv_c, *got = _dn_fwd(qkv_c, p, dn_params, l,
                                        rider=_Gather([late["w_gate_up"], late["w_in"]], [False, False]) if riding else None)
        if riding:
            w["w_gate_up"] = got[0]
            w_in[1] = _relayout_w_in(got[1])
        yd, st_d = _gla_fwd(p, gla_params, l)
        mix, x1, h2, gu, act, x2 = _out_mlp(x, (ya, yb, yc, yd), w["w_out"], w["norm2_w"], w["w_gate_up"], w["w_down"], l)
        saved.append(dict(x=x, h=h, p=p, qkv_c=qkv_c, st_c=st_c, inv_c=inv_c, st_d=st_d, mix=mix, x1=x1, h2=h2, gu=gu,
                          act=act))
        x = x2
    loss, d_final, dx = _loss_head(x, w["final_norm_w"], target)
    large = {k: [None] * DEPTH for k in ("w_in", "w_out", "w_gate_up", "w_down")}
    small = [None] * DEPTH
    for l in reversed(range(DEPTH)):
        s = saved[l]
        p = s["p"]
        g = {}
        riding = [(k, 1) for k in _LATE] if exchange and l == 0 else []
        dgu, dx1, dmix, g["norm2_w"], *arrived = _mlp_out_bwd(
            dx, s["x1"], s["gu"], w["norm2_w"], w["w_down"], w["w_gate_up"], w["w_out"], l,
            rider=_Exchange([large[k][j] for k, j in riding]) if riding else None)
        for (k, j), a in zip(riding, arrived):
            large[k][j] = a
        large["w_gate_up"][l] = _wgrad_gate_up(s["h2"], dgu)
        large["w_down"][l] = _wgrad_down(s["act"], dx).reshape(N_DEV, FF // N_DEV, D)
        large["w_out"][l] = _matmul_tn(s["mix"], dx1, "wgrad_out", BF16).reshape(N_DEV, D // N_DEV, D)
        d_a, g["sgu_ln_w"], g["sgu_ln_b"], g["sgu_w_spatial"], g["sgu_b_spatial"] = _sgu_bwd(p, dmix, *sgu_params, l)
        d_b, g["sc_conv_w"] = _sconv_bwd(p, dmix, w["sc_conv_w"], l)
        riding = [("w_in", 1)] + [(k, 0) for k in _LATE] if exchange and l == 0 else []
        dqkv_c, dz_c, dab, g["dn_a_log"], g["dn_dt_bias"], g["dn_norm_w"], *arrived = _dn_bwd(
            s["qkv_c"], p, s["st_c"], s["inv_c"], dmix, dn_params, l,
            rider=_Exchange([large[k][j] for k, j in riding]) if riding else None)
        for (k, j), a in zip(riding, arrived):
            large[k][j] = a
        d_c, g["dn_conv_w"] = _qkv_conv_bwd(p, dqkv_c, w["dn_conv_w"], l)
        d_d, dz_d, dglr, g["gla_w_gate2"], g["gla_gate_bias"], g["gla_norm_w"] = _gla_bwd(p, s["st_d"], dmix, gla_params, l)
        dp, dx, g["norm1_w"] = _in_proj_bwd((d_c, d_d, d_a, d_b, dz_c, dz_d, dab, dglr), s["x"], dx1, w["norm1_w"],
                                            w_in[l], l)
        large["w_in"][l] = _scatter_w_in_grad(_matmul_tn(s["h"], dp, "wgrad_in", F32))
        small[l] = g
    return loss[0, 0], dx, large, small, d_final


_ORDER = ("norm1_w", "w_in", "sgu_ln_w", "sgu_ln_b", "sgu_w_spatial", "sgu_b_spatial", "sc_conv_w", "dn_conv_w",
          "dn_a_log", "dn_dt_bias", "dn_norm_w", "gla_w_gate2", "gla_gate_bias", "gla_norm_w", "w_out", "norm2_w",
          "w_gate_up", "w_down", "final_norm_w")
_LARGE = ("w_in", "w_gate_up", "w_out", "w_down")
_LARGE_TILE = {"w_in": 256, "w_gate_up": 256, "w_out": 128, "w_down": 352}


def _gather_cols(g):
    return jnp.transpose(g, (1, 2, 0, 3)).reshape(g.shape[1], g.shape[2], N_DEV * g.shape[3])


def kernel(x, norm1_w, w_in, sgu_ln_w, sgu_ln_b, sgu_w_spatial, sgu_b_spatial, sc_conv_w, dn_conv_w, dn_a_log, dn_dt_bias, dn_norm_w, gla_w_gate2, gla_gate_bias, gla_norm_w, w_out, norm2_w, w_gate_up, w_down, final_norm_w, loss_target, m_norm1_w, m_w_in, m_sgu_ln_w, m_sgu_ln_b, m_sgu_w_spatial, m_sgu_b_spatial, m_sc_conv_w, m_dn_conv_w, m_dn_a_log, m_dn_dt_bias, m_dn_norm_w, m_gla_w_gate2, m_gla_gate_bias, m_gla_norm_w, m_w_out, m_norm2_w, m_w_gate_up, m_w_down, m_final_norm_w, v_norm1_w, v_w_in, v_sgu_ln_w, v_sgu_ln_b, v_sgu_w_spatial, v_sgu_b_spatial, v_sc_conv_w, v_dn_conv_w, v_dn_a_log, v_dn_dt_bias, v_dn_norm_w, v_gla_w_gate2, v_gla_gate_bias, v_gla_norm_w, v_w_out, v_norm2_w, v_w_gate_up, v_w_down, v_final_norm_w):
    args = dict(locals())
    wts = {k: args[k] for k in _ORDER}
    mom = {k: args["m_" + k] for k in _ORDER}
    var = {k: args["v_" + k] for k in _ORDER}
    for d in (wts, mom, var):
        d["final_norm_w"] = d["final_norm_w"][None]
    me = 4 * lax.axis_index("x") + 2 * lax.axis_index("y") + lax.axis_index("c")

    late = {k: wts[k].astype(BF16) for k in _LATE}
    w_in = wts["w_in"].astype(BF16)
    late["w_in"] = w_in[1]
    got = _run(_Gather([w_in[0]] + [wts[k] for k in _SHARDED_SMALL], [False] * 4), "gather_w_in")
    full = dict(wts)
    full["w_in"] = [_relayout_w_in(got[0]), None]
    for k, g in zip(_SHARDED_SMALL, got[1:]):
        full[k] = _gather_cols(g)

    loss, dx, large, small, d_final = _local_grads(x[0], loss_target[0], full, late=late, exchange=True)
    loss = lax.psum(loss, ("x", "y", "c"))

    large["w_in"][0], = _run(_Exchange([large["w_in"][0]]), "exchange_grads")
    result = {}
    for k in _LARGE:
        outs = _adamw_large(large[k], wts[k], mom[k], var[k], "adamw_" + k, _LARGE_TILE[k])
        for kind, a in zip(("grad", "delta", "new_m", "new_v"), outs):
            result[kind, k] = a

    slabs = _run(_Gather([_pack_small(small, d_final)], [False]), "gather_small_grads")[0]
    rep = _REPLICATED + ("final_norm_w",)
    outs, summed = _adamw_small(slabs, {k: wts[k] for k in rep}, {k: mom[k] for k in rep}, {k: var[k] for k in rep})
    for kind, d in zip(("grad", "delta", "new_m", "new_v"), outs):
        for k, a in d.items():
            result[kind, k] = a[0] if k == "final_norm_w" else a
    own = {k: lax.dynamic_slice_in_dim(summed[k], me * wts[k].shape[-1], wts[k].shape[-1], axis=2) for k in _SHARDED_SMALL}
    outs = _adamw_shards(own, {k: wts[k] for k in _SHARDED_SMALL}, {k: mom[k] for k in _SHARDED_SMALL},
                         {k: var[k] for k in _SHARDED_SMALL})
    for kind, d in zip(("grad", "delta", "new_m", "new_v"), [own] + outs):
        for k, a in d.items():
            result[kind, k] = a

    return (loss, dx[None], *[result[kind, k] for kind in ("grad", "delta", "new_m", "new_v") for k in _ORDER])
```

```python
import functools

import jax
import jax.numpy as jnp
from jax import lax
from jax.experimental import pallas as pl
from jax.experimental.pallas import tpu as pltpu

F32, BF16 = jnp.float32, jnp.bfloat16
DEPTH = 2
D = 1024
G = 256
NH, HD = 4, 64
FF = 2816
IN_COLS = 3352
P = 3584
EPS = 1e-6
SGU_C, DN_C, GLA_C = 128, 64, 64
N_DEV = 8
IN_SHARD = IN_COLS // N_DEV
GU_SHARD = 2 * FF // N_DEV
N_GATE = N_DEV // 2
LANES = 128
VMEM_LIMIT = 56 * 2 ** 20

_PAD_SEGS = ((1280, 2048, 0),
             (2312, 3080, 0),
             (0, 512, 0),
             (512, 1280, 0),
             (2056, 2312, 0),
             (3096, 3352, 0),
             (2048, 2056, 120),
             (3080, 3096, 112))

ADAM_LR, ADAM_B1, ADAM_B2, ADAM_EPS, ADAM_WD, ADAM_STEP = 0.001, 0.9, 0.999, 1e-08, 0.01, 10


def _cparams(*sem):
    return pltpu.CompilerParams(dimension_semantics=sem, vmem_limit_bytes=VMEM_LIMIT)


def _resident(shape):
    return pl.BlockSpec(shape, lambda *_: (0,) * len(shape), pipeline_mode=pl.Buffered(1))


def _layer(shape, l):
    return pl.BlockSpec((None,) + tuple(shape), lambda *_: (l,) + (0,) * len(shape), pipeline_mode=pl.Buffered(1))


def _acc(shape):
    return pl.BlockSpec(shape, lambda *_: (0,) * len(shape))


def _dg(a, b, ca, cb, precision=None):
    lead = a.ndim - 2
    batch = ((0,), (0,)) if lead else ((), ())
    return lax.dot_general(a, b, (((ca + lead,), (cb + lead,)), batch), preferred_element_type=F32, precision=precision)


def _make_dot(cast, precision):
    def raw(a, b, form):
        ca = 0 if form == "tn" else 1
        cb = 1 if form == "nt" else 0
        return _dg_any(cast(a), cast(b), ca, cb, precision)

    @functools.partial(jax.custom_vjp, nondiff_argnums=(2,))
    def dot(a, b, form):
        return raw(a, b, form)

    def fwd(a, b, form):
        return raw(a, b, form), (a, b)

    def bwd(form, res, g):
        a, b = res
        if form == "nn":
            return raw(g, b, "nt"), raw(a, g, "tn")
        if form == "nt":
            return raw(g, b, "nn"), raw(g, a, "tn")
        return raw(b, g, "nt"), raw(a, g, "nn")

    dot.defvjp(fwd, bwd)
    return dot


def _split(t):
    hi = t.astype(BF16)
    return hi, (t - hi.astype(F32)).astype(BF16)


def _dg3(a, b, ca, cb):
    (ah, al), (bh, bl) = a, b
    return _dg(ah, bh, ca, cb) + (_dg(ah, bl, ca, cb) + _dg(al, bh, ca, cb))


class _Split3:
    pass


def _dg_any(a, b, ca, cb, precision):
    if precision is _Split3:
        return _dg3(_split(a), _split(b), ca, cb)
    return _dg(a, b, ca, cb, precision)


_bdot = _make_dot(lambda t: t.astype(BF16), None)
_hdot = _make_dot(lambda t: t, _Split3)


def _inv_unit_lower(low):
    n = low.shape[-1]
    eye = (lax.broadcasted_iota(jnp.int32, (n, n), 0) == lax.broadcasted_iota(jnp.int32, (n, n), 1)).astype(F32)
    p = -low
    inv = eye + p
    ps = _split(p)
    k = 1
    while 2 * k < n:
        ps = _split(_dg3(ps, ps, 1, 0))
        inv = inv + _dg3(_split(inv), ps, 1, 0)
        k *= 2
    return inv


@jax.custom_vjp
def _unit_lower_solve(low, rhs, inv):
    return _dg3(_split(inv), _split(rhs), 1, 0)


def _uls_fwd(low, rhs, inv):
    sol = _dg3(_split(inv), _split(rhs), 1, 0)
    return sol, (inv, sol)


def _uls_bwd(res, g):
    inv, sol = res
    d_rhs = _dg3(_split(inv), _split(g), 0, 0)
    return -_dg3(_split(d_rhs), _split(sol), 1, 1), d_rhs, jnp.zeros_like(inv)


_unit_lower_solve.defvjp(_uls_fwd, _uls_bwd)


def _sigmoid(x):
    return 1.0 / (1.0 + jnp.exp(-x))


def _softplus(x):
    return jnp.maximum(x, 0.0) + jnp.log(1.0 + jnp.exp(-jnp.maximum(x, -x)))


def _gelu(x):
    return 0.5 * x * (1.0 + jnp.tanh(0.7978845608028654 * (x + 0.044715 * (x * x * x))))


def _tri(n):
    ri = lax.broadcasted_iota(jnp.int32, (n, n), 0)
    ci = lax.broadcasted_iota(jnp.int32, (n, n), 1)
    return ri, ci


def _stack(ts):
    return jnp.concatenate([t[None] for t in ts], axis=0)


def _sgu_chunk(uv, ln_w, ln_b, ws, bs):
    u = _gelu(uv[:, :G])
    v = _gelu(uv[:, G:])
    mu = jnp.mean(v, axis=-1, keepdims=True)
    vc = v - mu
    var = jnp.mean(vc * vc, axis=-1, keepdims=True)
    vn = vc * lax.rsqrt(var + EPS) * ln_w + ln_b
    ri, ci = _tri(SGU_C)
    bs_t = _hdot((ri == ci).astype(F32), bs, "nt")
    mixed = []
    for h in range(NH):
        wm = jnp.where(ri >= ci, ws[h], 0.0)
        mixed.append(_bdot(wm, vn[:, HD * h:HD * (h + 1)], "nn") + bs_t[:, h:h + 1])
    return u * jnp.concatenate(mixed, axis=1)


def _dn_tile(q, k, v, ab, z, state, inv, a_log, dt_bias, nw):
    c = DN_C
    tm = q.shape[0]
    cps = tm // c
    ri, ci = _tri(tm)
    shift = c.bit_length() - 1
    same = jnp.right_shift(ri, shift) == jnp.right_shift(ci, shift)
    g4 = -jnp.exp(a_log) * _softplus(ab[:, 0:NH] + dt_bias)
    beta4 = _sigmoid(ab[:, NH:2 * NH])
    gc4 = _hdot((same & (ri >= ci)).astype(F32), g4, "nn")
    gr4 = _hdot(g4, (same & (ri <= ci)).astype(F32), "tn")
    pairs = [(slice(c * j, c * (j + 1)), h) for j in range(cps) for h in range(NH)]
    heads = lambda t: _stack([t[rows, HD * h:HD * (h + 1)] for rows, h in pairs])
    col = lambda t: _stack([t[rows, h:h + 1] for rows, h in pairs])
    qn, kn, vh = heads(q), heads(k), heads(v)
    qn = qn * lax.rsqrt(jnp.sum(qn * qn, axis=-1, keepdims=True) + EPS) * (HD ** -0.5)
    kn = kn * lax.rsqrt(jnp.sum(kn * kn, axis=-1, keepdims=True) + EPS)
    gc, beta = col(gc4), col(beta4)
    gr = _stack([gr4[h:h + 1, rows] for rows, h in pairs])
    gl = jnp.sum(col(g4), axis=1, keepdims=True)
    r2, c2 = _tri(c)
    decay = jnp.exp(jnp.where(r2 >= c2, gc - gr, -jnp.inf))
    kb = kn * beta
    low = jnp.where(r2 > c2, _bdot(kb, kn, "nt") * decay, 0.0)
    eg = jnp.exp(gc)
    if inv is None:
        inv = _inv_unit_lower(low)
    sol = _unit_lower_solve(low, jnp.concatenate([vh * beta, kb * eg], axis=2), inv)
    u, w = sol[:, :, :HD], sol[:, :, HD:]
    attn = _bdot(qn, kn, "nt") * decay
    qg, kd, cd = qn * eg, kn * jnp.exp(gl - gc), jnp.exp(gl)
    ys = []
    for j in range(cps):
        b = slice(NH * j, NH * (j + 1))
        v_new = u[b] - _bdot(w[b], state, "nn")
        o = _bdot(qg[b], state, "nn") + _bdot(attn[b], v_new, "nn")
        state = state * cd[b] + _bdot(kd[b], v_new, "tn")
        on = o * lax.rsqrt(jnp.mean(o * o, axis=-1, keepdims=True) + EPS) * nw
        ys.append(jnp.concatenate([on[h] for h in range(NH)], axis=1))
    return jnp.concatenate(ys, axis=0) * (z * _sigmoid(z)), state, inv


def _gla_tile(q, k, v, glr, z, state_t, w2, gate_bias, nw):
    c = GLA_C
    tm = q.shape[0]
    cps = tm // c
    ri, ci = _tri(tm)
    shift = c.bit_length() - 1
    same = jnp.right_shift(ri, shift) == jnp.right_shift(ci, shift)
    w2_rows = jnp.concatenate([w2, jnp.zeros((LANES - w2.shape[0], G), F32)], axis=0)
    pre = _bdot(glr, w2_rows, "nn") + gate_bias
    log_a = (jnp.minimum(pre, 0.0) - jnp.log(1.0 + jnp.exp(-jnp.maximum(pre, -pre)))) * (1.0 / 16.0)
    gcum = _hdot((same & (ri >= ci)).astype(F32), log_a, "nn")
    row = lax.broadcasted_iota(jnp.int32, (c, G), 0)
    qs = q * (HD ** -0.5)
    qa, ka, qg, kl, dec = [], [], [], [], []
    for j in range(cps):
        rows = slice(c * j, c * (j + 1))
        gj = gcum[rows]
        g_mid = jnp.sum(jnp.where(row == c // 2, gj, 0.0), axis=0, keepdims=True)
        g_last = jnp.sum(log_a[rows], axis=0, keepdims=True)
        qa.append(qs[rows] * jnp.exp(gj - g_mid))
        ka.append(k[rows] * jnp.exp(g_mid - gj))
        qg.append(qs[rows] * jnp.exp(gj))
        kl.append(k[rows] * jnp.exp(g_last - gj))
        dec.append(jnp.exp(g_last))
    heads = lambda ts: _stack([t[:, HD * h:HD * (h + 1)] for t in ts for h in range(NH)])
    qa, ka, qg, kl, dec = heads(qa), heads(ka), heads(qg), heads(kl), heads(dec)
    vh = heads([v[c * j:c * (j + 1)] for j in range(cps)])
    r2, c2 = _tri(c)
    o_intra = _bdot(jnp.where(r2 >= c2, _bdot(qa, ka, "nt"), 0.0), vh, "nn")
    ys = []
    for j in range(cps):
        b = slice(NH * j, NH * (j + 1))
        o = o_intra[b] + _bdot(qg[b], state_t, "nt")
        state_t = state_t * dec[b] + _bdot(vh[b], kl[b], "tn")
        on = o * lax.rsqrt(jnp.mean(o * o, axis=-1, keepdims=True) + EPS) * nw
        ys.append(jnp.concatenate([on[h] for h in range(NH)], axis=1))
    return jnp.concatenate(ys, axis=0) * (z * _sigmoid(z)), state_t


def _rms(x, nw):
    r = lax.rsqrt(jnp.mean(x * x, axis=-1, keepdims=True) + EPS)
    xh = x * r
    return xh * nw, xh, r


def _rms_bwd(g, xh, r, nw):
    gw = g * nw
    return r * (gw - xh * jnp.mean(gw * xh, axis=-1, keepdims=True)), jnp.sum(g * xh, axis=0, keepdims=True)


def _row(tm, w, blk=0):
    return pl.BlockSpec((tm, w), lambda i: (i, blk))


def _in_proj(x, nw, wp, l, tm=512, rider=None):
    t = x.shape[0]

    def body(*refs):
        (x_ref, nw_ref, w_ref, h_ref, p_ref), ride = _split_refs(refs, 3, 2, 0, rider)
        _ride_begin(rider, ride)
        h = _rms(x_ref[...], nw_ref[l:l + 1, :])[0].astype(BF16)
        h_ref[...] = h
        p_ref[...] = jnp.dot(h, w_ref[...], preferred_element_type=F32)
        _ride_end(rider, ride)

    specs, operands = _host(
        rider, [_row(tm, D), _resident((DEPTH, D)), _resident((D, P))], [_row(tm, D), _row(tm, P)],
        [jax.ShapeDtypeStruct((t, D), BF16), jax.ShapeDtypeStruct((t, P), F32)], [], [x, nw, wp])
    return pl.pallas_call(body, name="in_proj", grid=(t // tm,), compiler_params=_cparams("arbitrary"),
                          **specs)(*operands)


def _gu_spec(l):
    return pl.BlockSpec((N_DEV, None, GU_SHARD, D), lambda *_: (0, l, 0, 0), pipeline_mode=pl.Buffered(1))


def _out_mlp(x, ys, w_out, nw2, w_gu_t, w_down, l, tm=256):
    t = x.shape[0]

    def body(x_ref, ya, yb, yc, yd, wo_ref, nw_ref, wgu_ref, wd_ref, mix_ref, x1_ref, h2_ref, gu_ref, act_ref, x2_ref):
        mix = jnp.concatenate([ya[...], yb[...], yc[...], yd[...]], axis=1)
        mix_ref[...] = mix
        x1 = x_ref[...] + jnp.dot(mix, wo_ref[...], preferred_element_type=F32)
        x1_ref[...] = x1
        h2 = _rms(x1, nw_ref[l:l + 1, :])[0].astype(BF16)
        h2_ref[...] = h2
        gu = _dg(h2, wgu_ref[...].reshape(2 * FF, D), 1, 1)
        gu_ref[...] = gu.astype(BF16)
        gate, up = gu[:, :FF], gu[:, FF:]
        act = (gate * _sigmoid(gate) * up).astype(BF16)
        act_ref[...] = act
        x2_ref[...] = x1 + jnp.dot(act, wd_ref[...], preferred_element_type=F32)

    return pl.pallas_call(
        body, name="out_mlp", grid=(t // tm,),
        in_specs=[_row(tm, D), _row(tm, G), _row(tm, G), _row(tm, G), _row(tm, G), _layer((D, D), l),
                  _resident((DEPTH, D)), _gu_spec(l), _layer((FF, D), l)],
        out_specs=[_row(tm, D), _row(tm, D), _row(tm, D), _row(tm, 2 * FF), _row(tm, FF), _row(tm, D)],
        out_shape=[jax.ShapeDtypeStruct((t, D), BF16), jax.ShapeDtypeStruct((t, D), F32),
                   jax.ShapeDtypeStruct((t, D), BF16), jax.ShapeDtypeStruct((t, 2 * FF), BF16),
                   jax.ShapeDtypeStruct((t, FF), BF16), jax.ShapeDtypeStruct((t, D), F32)],
        compiler_params=_cparams("parallel"))(x, *ys, w_out, nw2, w_gu_t, w_down)


def _loss_head(x, nw, target, tm=512):
    t = x.shape[0]

    def body(x_ref, nw_ref, tg_ref, loss_ref, dnw_ref, dx_ref):
        @pl.when(pl.program_id(0) == 0)
        def _():
            loss_ref[...] = jnp.zeros_like(loss_ref)
            dnw_ref[...] = jnp.zeros_like(dnw_ref)
        nw_v = nw_ref[...]
        y, xh, r = _rms(x_ref[...], nw_v)
        err = y - tg_ref[...]
        loss_ref[...] += 0.5 * jnp.sum(err * err) * (1.0 / D)
        dx, dnw = _rms_bwd(err * (1.0 / D), xh, r, nw_v)
        dx_ref[...] = dx
        dnw_ref[...] += dnw

    return pl.pallas_call(
        body, name="loss_head", grid=(t // tm,),
        in_specs=[_row(tm, D), _resident((1, D)), _row(tm, D)],
        out_specs=[_acc((8, LANES)), _acc((1, D)), _row(tm, D)],
        out_shape=[jax.ShapeDtypeStruct((8, LANES), F32), jax.ShapeDtypeStruct((1, D), F32),
                   jax.ShapeDtypeStruct((t, D), F32)],
        compiler_params=_cparams("arbitrary"))(x, nw, target)


def _mlp_out_bwd(dx2, x1, gu, nw2, w_down, w_gu, w_out, l, tm=256, rider=None):
    t = dx2.shape[0]

    def body(*refs):
        (dx2_ref, x1_ref, gu_ref, nw_ref, wd_ref, wgu_ref, wo_ref, dgu_ref, dx1_ref, dmix_ref, dnw_ref), ride = \
            _split_refs(refs, 7, 4, 0, rider)
        _ride_begin(rider, ride)

        @pl.when(pl.program_id(0) == 0)
        def _():
            dnw_ref[...] = jnp.zeros_like(dnw_ref)
        dx2 = dx2_ref[...]
        dact = _dg(dx2.astype(BF16), wd_ref[...], 1, 1)
        gu = gu_ref[...].astype(F32)
        gate, up = gu[:, :FF], gu[:, FF:]
        s = _sigmoid(gate)
        dgu = jnp.concatenate([dact * up * (s * (1.0 + gate * (1.0 - s))), dact * (gate * s)], axis=1).astype(BF16)
        dgu_ref[...] = dgu
        dh2 = jnp.dot(dgu, wgu_ref[...].reshape(2 * FF, D), preferred_element_type=F32)
        nw_v = nw_ref[l:l + 1, :]
        _, xh, r = _rms(x1_ref[...], nw_v)
        dxn, dnw = _rms_bwd(dh2, xh, r, nw_v)
        dx1 = dx2 + dxn
        dx1_ref[...] = dx1
        dnw_ref[...] += dnw
        dmix_ref[...] = _dg(dx1.astype(BF16), wo_ref[...], 1, 1)
        _ride_end(rider, ride)

    specs, operands = _host(
        rider, [_row(tm, D), _row(tm, D), _row(tm, 2 * FF), _resident((DEPTH, D)), _layer((FF, D), l), _gu_spec(l),
                _layer((D, D), l)],
        [_row(tm, 2 * FF), _row(tm, D), _row(tm, D), _acc((1, D))],
        [jax.ShapeDtypeStruct((t, 2 * FF), BF16), jax.ShapeDtypeStruct((t, D), F32),
         jax.ShapeDtypeStruct((t, D), F32), jax.ShapeDtypeStruct((1, D), F32)],
        [], [dx2, x1, gu, nw2, w_down, w_gu, w_out])
    return pl.pallas_call(body, name="mlp_out_bwd", grid=(t // tm,), compiler_params=_cparams("arbitrary"),
                          **specs)(*operands)


_DP_WIDTHS = (768, 768, 512, 768, 256, 256, 128, 128)


def _in_proj_bwd(dps, x, dx1, nw, wp, l, tm=512):
    t = x.shape[0]

    def body(*refs):
        dp_refs, (x_ref, dx1_ref, nw_ref, w_ref, dp_ref, dx_ref, dnw_ref) = refs[:8], refs[8:]

        @pl.when(pl.program_id(0) == 0)
        def _():
            dnw_ref[...] = jnp.zeros_like(dnw_ref)
        dp = jnp.concatenate([r[...] for r in dp_refs], axis=1)
        dp_ref[...] = dp
        dh = _dg(dp, w_ref[...], 1, 1)
        nw_v = nw_ref[l:l + 1, :]
        _, xh, r = _rms(x_ref[...], nw_v)
        dxn, dnw = _rms_bwd(dh, xh, r, nw_v)
        dx_ref[...] = dx1_ref[...] + dxn
        dnw_ref[...] += dnw

    return pl.pallas_call(
        body, name="in_proj_bwd", grid=(t // tm,),
        in_specs=[_row(tm, w) for w in _DP_WIDTHS] + [_row(tm, D), _row(tm, D), _resident((DEPTH, D)), _resident((D, P))],
        out_specs=[_row(tm, P), _row(tm, D), _acc((1, D))],
        out_shape=[jax.ShapeDtypeStruct((t, P), BF16), jax.ShapeDtypeStruct((t, D), F32),
                   jax.ShapeDtypeStruct((1, D), F32)],
        compiler_params=_cparams("arbitrary"))(*dps, x, dx1, nw, wp)


def _accumulate(acc, o_ref, t_axis, term):
    @pl.when(pl.program_id(t_axis) == 0)
    def _():
        acc[...] = jnp.zeros_like(acc)
    acc[...] += term

    @pl.when(pl.program_id(t_axis) == pl.num_programs(t_axis) - 1)
    def _():
        o_ref[...] = acc[...].astype(o_ref.dtype)


WGRAD_TOKENS = 2048


def _matmul_tn(a, b, name, out_dtype, tn=512, tt=WGRAD_TOKENS):
    t, m = a.shape
    n = b.shape[1]
    tt = min(tt, t)

    def body(a_ref, b_ref, o_ref, acc):
        _accumulate(acc, o_ref, 1, _dg(a_ref[...].astype(BF16), b_ref[...].astype(BF16), 0, 0))

    return pl.pallas_call(
        body, name=name, grid=(n // tn, t // tt),
        in_specs=[pl.BlockSpec((tt, m), lambda j, i: (i, 0)), pl.BlockSpec((tt, tn), lambda j, i: (i, j))],
        out_specs=pl.BlockSpec((m, tn), lambda j, i: (0, j)),
        out_shape=jax.ShapeDtypeStruct((m, n), out_dtype),
        scratch_shapes=[pltpu.VMEM((m, tn), F32)],
        compiler_params=_cparams("parallel", "arbitrary"))(a, b)


WGRAD_ROWS = 2 * GU_SHARD


def _wgrad_rows(a, b, name, tt=WGRAD_TOKENS):
    t, m = a.shape
    tt = min(tt, t)

    def body(a_ref, b_ref, o_ref, acc):
        _accumulate(acc, o_ref, 1, _dg(a_ref[...], b_ref[...].astype(BF16), 0, 0))

    return pl.pallas_call(
        body, name=name, grid=(m // WGRAD_ROWS, t // tt),
        in_specs=[pl.BlockSpec((tt, WGRAD_ROWS), lambda j, i: (i, j)), pl.BlockSpec((tt, D), lambda j, i: (i, 0))],
        out_specs=pl.BlockSpec((WGRAD_ROWS, D), lambda j, i: (j, 0)),
        out_shape=jax.ShapeDtypeStruct((m, D), BF16),
        scratch_shapes=[pltpu.VMEM((WGRAD_ROWS, D), F32)],
        compiler_params=_cparams("parallel", "arbitrary"))(a, b)


def _relayout_w_in(g, tr=256):
    def body(w_ref, o_ref):
        full = jnp.concatenate([w_ref[d] for d in range(N_DEV)], axis=1)
        parts = []
        for a, b, z in _PAD_SEGS:
            parts.append(full[:, a:b])
            if z:
                parts.append(jnp.zeros((tr, z), full.dtype))
        o_ref[...] = jnp.concatenate(parts, axis=1)

    return pl.pallas_call(
        body, name="relayout_w_in", grid=(D // tr,),
        in_specs=[pl.BlockSpec((N_DEV, tr, IN_SHARD), lambda i: (0, i, 0))], out_specs=_row(tr, P),
        out_shape=jax.ShapeDtypeStruct((D, P), g.dtype), compiler_params=_cparams("parallel"))(g)


def _scatter_w_in_grad(gp, tr=256):
    def body(g_ref, o_ref):
        g = g_ref[...]
        pieces, off = [], 0
        for a, b, z in _PAD_SEGS:
            pieces.append((a, g[:, off:off + (b - a)]))
            off += (b - a) + z
        nat = jnp.concatenate([piece for _, piece in sorted(pieces, key=lambda ap: ap[0])], axis=1)
        for d in range(N_DEV):
            o_ref[d] = nat[:, IN_SHARD * d:IN_SHARD * (d + 1)].astype(BF16)

    return pl.pallas_call(
        body, name="scatter_w_in_grad", grid=(D // tr,),
        in_specs=[pl.BlockSpec((tr, P), lambda i: (i, 0))],
        out_specs=pl.BlockSpec((N_DEV, tr, IN_SHARD), lambda i: (0, i, 0)),
        out_shape=jax.ShapeDtypeStruct((N_DEV, D, IN_SHARD), BF16),
        compiler_params=_cparams("parallel"))(gp)


_A_BLK = 3


def _sgu_specs(l):
    return [_resident((DEPTH, G)), _resident((DEPTH, G)), _layer((NH, SGU_C, SGU_C), l), _layer((NH, SGU_C), l)]


def _sgu_fwd(p, ln_w, ln_b, ws, bs, l, tm=256):
    t = p.shape[0]

    def body(uv_ref, lw_ref, lb_ref, ws_ref, bs_ref, y_ref):
        ws_v = [ws_ref[h] for h in range(NH)]
        for c in range(tm // SGU_C):
            rows = pl.ds(c * SGU_C, SGU_C)
            y_ref[rows, :] = _sgu_chunk(uv_ref[rows, :], lw_ref[l:l + 1, :], lb_ref[l:l + 1, :], ws_v,
                                        bs_ref[...]).astype(BF16)

    return pl.pallas_call(
        body, name="sgu_fwd", grid=(t // tm,),
        in_specs=[_row(tm, 2 * G, _A_BLK)] + _sgu_specs(l), out_specs=_row(tm, G),
        out_shape=jax.ShapeDtypeStruct((t, G), BF16),
        compiler_params=_cparams("parallel"))(p, ln_w, ln_b, ws, bs)


def _sgu_bwd(p, dmix, ln_w, ln_b, ws, bs, l, tm=256):
    t = p.shape[0]

    def body(uv_ref, dy_ref, lw_ref, lb_ref, ws_ref, bs_ref, duv_ref, dlw_ref, dlb_ref, dws_ref, dbs_ref):
        @pl.when(pl.program_id(0) == 0)
        def _():
            for r in (dlw_ref, dlb_ref, dws_ref, dbs_ref):
                r[...] = jnp.zeros_like(r)
        ws_v = [ws_ref[h] for h in range(NH)]
        for c in range(tm // SGU_C):
            rows = pl.ds(c * SGU_C, SGU_C)
            _, vjp = jax.vjp(_sgu_chunk, uv_ref[rows, :], lw_ref[l:l + 1, :], lb_ref[l:l + 1, :], ws_v, bs_ref[...])
            duv, dlw, dlb, dws, dbs = vjp(dy_ref[rows, :])
            duv_ref[rows, :] = duv.astype(BF16)
            dlw_ref[...] += dlw
            dlb_ref[...] += dlb
            dbs_ref[...] += dbs
            for h in range(NH):
                dws_ref[h] += dws[h]

    return pl.pallas_call(
        body, name="sgu_bwd", grid=(t // tm,),
        in_specs=[_row(tm, 2 * G, _A_BLK), _row(tm, G, 0)] + _sgu_specs(l),
        out_specs=[_row(tm, 2 * G), _acc((1, G)), _acc((1, G)), _acc((NH, SGU_C, SGU_C)), _acc((NH, SGU_C))],
        out_shape=[jax.ShapeDtypeStruct((t, 2 * G), BF16), jax.ShapeDtypeStruct((1, G), F32),
                   jax.ShapeDtypeStruct((1, G), F32), jax.ShapeDtypeStruct((NH, SGU_C, SGU_C), F32),
                   jax.ShapeDtypeStruct((NH, SGU_C), F32)],
        compiler_params=_cparams("arbitrary"))(p, dmix, ln_w, ln_b, ws, bs)


HALO = 8


def _prev_halo(tm, width, col):
    return pl.BlockSpec((HALO, width), lambda i: (jnp.maximum(i * (tm // HALO) - 1, 0), col))


def _next_halo(tm, width, col, t):
    return pl.BlockSpec((HALO, width), lambda i: (jnp.minimum((i + 1) * (tm // HALO), t // HALO - 1), col))


def _with_prev(halo_ref, x_ref):
    halo = jnp.where(pl.program_id(0) == 0, 0.0, halo_ref[...])
    return jnp.concatenate([halo, x_ref[...]], axis=0)


def _with_next(x_ref, halo_ref):
    halo = jnp.where(pl.program_id(0) == pl.num_programs(0) - 1, 0.0, halo_ref[...])
    return jnp.concatenate([x_ref[...], halo], axis=0)


def _delay(ext, j):
    return ext if j == 0 else pltpu.roll(ext, j, axis=0)


def _advance(ext, j):
    return ext if j == 0 else pltpu.roll(ext, ext.shape[0] - j, axis=0)


def _causal_conv(ext, w, tm):
    kw = w.shape[0]
    out = None
    for k in range(kw):
        term = _delay(ext, kw - 1 - k)[HALO:HALO + tm] * w[k:k + 1, :]
        out = term if out is None else out + term
    return out


def _causal_conv_t(ext, w, tm):
    kw = w.shape[0]
    out = None
    for k in range(kw):
        term = _advance(ext, kw - 1 - k)[0:tm] * w[k:k + 1, :]
        out = term if out is None else out + term
    return out


def _conv_wgrad(ext, dy, kw, tm):
    return jnp.concatenate(
        [jnp.sum(_delay(ext, kw - 1 - k)[HALO:HALO + tm] * dy, axis=0, keepdims=True) for k in range(kw)], axis=0)


_B_GB, _B_GC, _B_H = 8, 9, 10
_C_QKV, _D_QKV = 0, 1
_C_Z, _D_Z = 11, 12
_C_AB, _D_GLR = 26, 27


def _sconv_fwd(p, w, l, tm=512):
    t = p.shape[0]

    def body(gb_ref, gc_ref, h_ref, gc_halo, h_halo, w_ref, y_ref):
        s_ext = _with_prev(gc_halo, gc_ref) * _with_prev(h_halo, h_ref)
        y_ref[...] = (gb_ref[...] * _causal_conv(s_ext, w_ref[...], tm)).astype(BF16)

    return pl.pallas_call(
        body, name="sconv_fwd", grid=(t // tm,),
        in_specs=[_row(tm, G, _B_GB), _row(tm, G, _B_GC), _row(tm, G, _B_H), _prev_halo(tm, G, _B_GC),
                  _prev_halo(tm, G, _B_H), _layer((3, G), l)],
        out_specs=_row(tm, G), out_shape=jax.ShapeDtypeStruct((t, G), BF16),
        compiler_params=_cparams("parallel"))(p, p, p, p, p, w)


def _sconv_bwd(p, dmix, w, l, tm=512):
    t = p.shape[0]

    def body(gb_ref, gc_ref, h_ref, gc_halo, h_halo, gb_next, dy_ref, dy_next, w_ref, dp_ref, dw_ref):
        @pl.when(pl.program_id(0) == 0)
        def _():
            dw_ref[...] = jnp.zeros_like(dw_ref)
        w_v = w_ref[...]
        s_ext = _with_prev(gc_halo, gc_ref) * _with_prev(h_halo, h_ref)
        dout = dy_ref[...]
        d_gb = dout * _causal_conv(s_ext, w_v, tm)
        dconv_ext = _with_next(dy_ref, dy_next) * _with_next(gb_ref, gb_next)
        ds = _causal_conv_t(dconv_ext, w_v, tm)
        dw_ref[...] += _conv_wgrad(s_ext, dconv_ext[0:tm], 3, tm)
        dp_ref[...] = jnp.concatenate([d_gb, ds * h_ref[...], ds * gc_ref[...]], axis=1).astype(BF16)

    return pl.pallas_call(
        body, name="sconv_bwd", grid=(t // tm,),
        in_specs=[_row(tm, G, _B_GB), _row(tm, G, _B_GC), _row(tm, G, _B_H), _prev_halo(tm, G, _B_GC),
                  _prev_halo(tm, G, _B_H), _next_halo(tm, G, _B_GB, t), _row(tm, G, 1), _next_halo(tm, G, 1, t),
                  _layer((3, G), l)],
        out_specs=[_row(tm, 3 * G), _acc((3, G))],
        out_shape=[jax.ShapeDtypeStruct((t, 3 * G), BF16), jax.ShapeDtypeStruct((3, G), F32)],
        compiler_params=_cparams("arbitrary"))(p, p, p, p, p, p, dmix, dmix, w)


def _qkv_conv_fwd(p, w, l, tm=512):
    t = p.shape[0]

    def body(x_ref, x_halo, w_ref, y_ref):
        c = _causal_conv(_with_prev(x_halo, x_ref), w_ref[...], tm)
        y_ref[...] = c * _sigmoid(c)

    return pl.pallas_call(
        body, name="qkv_conv_fwd", grid=(t // tm,),
        in_specs=[_row(tm, 3 * G, _C_QKV), _prev_halo(tm, 3 * G, _C_QKV), _layer((4, 3 * G), l)],
        out_specs=_row(tm, 3 * G), out_shape=jax.ShapeDtypeStruct((t, 3 * G), F32),
        compiler_params=_cparams("parallel"))(p, p, w)


def _qkv_conv_bwd(p, dy, w, l, tm=512):
    t = p.shape[0]

    def body(x_ref, x_halo, x_next, dy_ref, dy_next, w_ref, dx_ref, dw_ref):
        @pl.when(pl.program_id(0) == 0)
        def _():
            dw_ref[...] = jnp.zeros_like(dw_ref)
        w_v = w_ref[...]
        x_all = jnp.concatenate([_with_prev(x_halo, x_ref), jnp.where(
            pl.program_id(0) == pl.num_programs(0) - 1, 0.0, x_next[...])], axis=0)
        c = _causal_conv(x_all, w_v, tm + HALO)
        s = _sigmoid(c)
        dc_ext = _with_next(dy_ref, dy_next) * (s * (1.0 + c * (1.0 - s)))
        dx_ref[...] = _causal_conv_t(dc_ext, w_v, tm).astype(BF16)
        dw_ref[...] += _conv_wgrad(x_all[0:HALO + tm], dc_ext[0:tm], 4, tm)

    return pl.pallas_call(
        body, name="qkv_conv_bwd", grid=(t // tm,),
        in_specs=[_row(tm, 3 * G, _C_QKV), _prev_halo(tm, 3 * G, _C_QKV), _next_halo(tm, 3 * G, _C_QKV, t),
                  _row(tm, 3 * G), _next_halo(tm, 3 * G, 0, t), _layer((4, 3 * G), l)],
        out_specs=[_row(tm, 3 * G), _acc((4, 3 * G))],
        out_shape=[jax.ShapeDtypeStruct((t, 3 * G), BF16), jax.ShapeDtypeStruct((4, 3 * G), F32)],
        compiler_params=_cparams("arbitrary"))(p, p, p, dy, dy, w)


_STATE = pl.BlockSpec((1, NH, HD, HD), lambda i: (i, 0, 0, 0))


def _dn_specs():
    return [_resident((DEPTH, NH)), _resident((DEPTH, NH)), _resident((DEPTH, HD))]


def _dn_fwd(qkv, p, params, l, cps=4, rider=None):
    t = qkv.shape[0]
    tm = DN_C * cps
    nb = cps * NH

    def body(*refs):
        (qkv_ref, z_ref, ab_ref, alog_ref, dt_ref, nw_ref, y_ref, st_ref, inv_ref, state), ride = _split_refs(
            refs, 6, 3, 1, rider)
        _ride_begin(rider, ride)

        @pl.when(pl.program_id(0) == 0)
        def _():
            state[...] = jnp.zeros_like(state)
        st_ref[0] = state[...]
        y, new, inv = _dn_tile(qkv_ref[:, 0:G], qkv_ref[:, G:2 * G], qkv_ref[:, 2 * G:3 * G], ab_ref[...], z_ref[...],
                               state[...], None, alog_ref[l:l + 1, :], dt_ref[l:l + 1, :], nw_ref[l:l + 1, :])
        y_ref[...] = y.astype(BF16)
        inv_ref[...] = inv
        state[...] = new
        _ride_end(rider, ride)

    specs, operands = _host(
        rider, [_row(tm, 3 * G), _row(tm, G, _C_Z), _row(tm, LANES, _C_AB)] + _dn_specs(),
        [_row(tm, G), _STATE, pl.BlockSpec((nb, DN_C, DN_C), lambda i: (i, 0, 0))],
        [jax.ShapeDtypeStruct((t, G), BF16), jax.ShapeDtypeStruct((t // tm, NH, HD, HD), F32),
         jax.ShapeDtypeStruct((t // DN_C * NH, DN_C, DN_C), F32)],
        [pltpu.VMEM((NH, HD, HD), F32)], [qkv, p, p, *params])
    return pl.pallas_call(body, name="dn_fwd", grid=(t // tm,), compiler_params=_cparams("arbitrary"), **specs)(*operands)


def _dn_bwd(qkv, p, states, inv, dmix, params, l, cps=4, rider=None):
    t = qkv.shape[0]
    tm = DN_C * cps
    n = t // tm
    nb = cps * NH

    def body(*refs):
        (qkv_ref, z_ref, ab_ref, st_ref, inv_ref, dy_ref, alog_ref, dt_ref, nw_ref,
         dqkv_ref, dz_ref, dab_ref, dalog_ref, ddt_ref, dnw_ref, dstate), ride = _split_refs(refs, 9, 6, 1, rider)
        _ride_begin(rider, ride)
        dprm_refs = (dalog_ref, ddt_ref, dnw_ref)

        @pl.when(pl.program_id(0) == 0)
        def _():
            dstate[...] = jnp.zeros_like(dstate)
            for r in dprm_refs:
                r[...] = jnp.zeros_like(r)
        inv_v = inv_ref[...]
        tile = lambda q, k, v, ab, z, st, alog, dt, nw: _dn_tile(q, k, v, ab, z, st, inv_v, alog, dt, nw)[:2]
        _, vjp = jax.vjp(tile, qkv_ref[:, 0:G], qkv_ref[:, G:2 * G], qkv_ref[:, 2 * G:3 * G], ab_ref[...], z_ref[...],
                         st_ref[0], alog_ref[l:l + 1, :], dt_ref[l:l + 1, :], nw_ref[l:l + 1, :])
        dq, dk, dv, dab, dz, dst, *dprm = vjp((dy_ref[...], dstate[...]))
        dqkv_ref[...] = jnp.concatenate([dq, dk, dv], axis=1)
        dz_ref[...] = dz.astype(BF16)
        dab_ref[...] = dab.astype(BF16)
        dstate[...] = dst
        for r, g in zip(dprm_refs, dprm):
            r[...] += g
        _ride_end(rider, ride)

    rev = lambda w, blk: pl.BlockSpec((tm, w), lambda i: (n - 1 - i, blk))
    specs, operands = _host(
        rider, [rev(3 * G, 0), rev(G, _C_Z), rev(LANES, _C_AB),
                pl.BlockSpec((1, NH, HD, HD), lambda i: (n - 1 - i, 0, 0, 0)),
                pl.BlockSpec((nb, DN_C, DN_C), lambda i: (n - 1 - i, 0, 0)), rev(G, 2)] + _dn_specs(),
        [rev(3 * G, 0), rev(G, 0), rev(LANES, 0), _acc((1, NH)), _acc((1, NH)), _acc((1, HD))],
        [jax.ShapeDtypeStruct((t, 3 * G), F32), jax.ShapeDtypeStruct((t, G), BF16),
         jax.ShapeDtypeStruct((t, LANES), BF16), jax.ShapeDtypeStruct((1, NH), F32),
         jax.ShapeDtypeStruct((1, NH), F32), jax.ShapeDtypeStruct((1, HD), F32)],
        [pltpu.VMEM((NH, HD, HD), F32)], [qkv, p, p, states, inv, dmix, *params])
    return pl.pallas_call(body, name="dn_bwd", grid=(n,), compiler_params=_cparams("arbitrary"), **specs)(*operands)


def _gla_specs(l):
    return [_layer((16, G), l), _resident((DEPTH, G)), _resident((DEPTH, HD))]


def _gla_fwd(p, params, l, cps=4):
    t = p.shape[0]
    tm = GLA_C * cps

    def body(qkv_ref, z_ref, glr_ref, w2_ref, gb_ref, nw_ref, y_ref, st_ref, state):
        @pl.when(pl.program_id(0) == 0)
        def _():
            state[...] = jnp.zeros_like(state)
        st_ref[0] = state[...]
        y, new = _gla_tile(qkv_ref[:, 0:G], qkv_ref[:, G:2 * G], qkv_ref[:, 2 * G:3 * G], glr_ref[...], z_ref[...],
                           state[...], w2_ref[...], gb_ref[l:l + 1, :], nw_ref[l:l + 1, :])
        y_ref[...] = y.astype(BF16)
        state[...] = new

    return pl.pallas_call(
        body, name="gla_fwd", grid=(t // tm,),
        in_specs=[_row(tm, 3 * G, _D_QKV), _row(tm, G, _D_Z), _row(tm, LANES, _D_GLR)] + _gla_specs(l),
        out_specs=[_row(tm, G), _STATE],
        out_shape=[jax.ShapeDtypeStruct((t, G), BF16), jax.ShapeDtypeStruct((t // tm, NH, HD, HD), F32)],
        scratch_shapes=[pltpu.VMEM((NH, HD, HD), F32)],
        compiler_params=_cparams("arbitrary"))(p, p, p, *params)


def _gla_bwd(p, states, dmix, params, l, cps=4):
    t = p.shape[0]
    tm = GLA_C * cps
    n = t // tm

    def body(qkv_ref, z_ref, glr_ref, st_ref, dy_ref, w2_ref, gb_ref, nw_ref,
             dqkv_ref, dz_ref, dglr_ref, dw2_ref, dgb_ref, dnw_ref, dstate):
        dprm_refs = (dw2_ref, dgb_ref, dnw_ref)

        @pl.when(pl.program_id(0) == 0)
        def _():
            dstate[...] = jnp.zeros_like(dstate)
            for r in dprm_refs:
                r[...] = jnp.zeros_like(r)
        _, vjp = jax.vjp(_gla_tile, qkv_ref[:, 0:G], qkv_ref[:, G:2 * G], qkv_ref[:, 2 * G:3 * G], glr_ref[...],
                         z_ref[...], st_ref[0], w2_ref[...], gb_ref[l:l + 1, :], nw_ref[l:l + 1, :])
        dq, dk, dv, dglr, dz, dst, *dprm = vjp((dy_ref[...], dstate[...]))
        dqkv_ref[...] = jnp.concatenate([dq, dk, dv], axis=1).astype(BF16)
        dz_ref[...] = dz.astype(BF16)
        dglr_ref[...] = dglr.astype(BF16)
        dstate[...] = dst
        for r, g in zip(dprm_refs, dprm):
            r[...] += g

    rev = lambda w, blk: pl.BlockSpec((tm, w), lambda i: (n - 1 - i, blk))
    return pl.pallas_call(
        body, name="gla_bwd", grid=(n,),
        in_specs=[rev(3 * G, _D_QKV), rev(G, _D_Z), rev(LANES, _D_GLR),
                  pl.BlockSpec((1, NH, HD, HD), lambda i: (n - 1 - i, 0, 0, 0)), rev(G, 3)] + _gla_specs(l),
        out_specs=[rev(3 * G, 0), rev(G, 0), rev(LANES, 0), _acc((16, G)), _acc((1, G)), _acc((1, HD))],
        out_shape=[jax.ShapeDtypeStruct((t, 3 * G), BF16), jax.ShapeDtypeStruct((t, G), BF16),
                   jax.ShapeDtypeStruct((t, LANES), BF16), jax.ShapeDtypeStruct((16, G), F32),
                   jax.ShapeDtypeStruct((1, G), F32), jax.ShapeDtypeStruct((1, HD), F32)],
        scratch_shapes=[pltpu.VMEM((NH, HD, HD), F32)],
        compiler_params=_cparams("arbitrary"))(p, p, p, states, dmix, *params)


def _adamw_math(w, g, m, v):
    m = ADAM_B1 * m + (1.0 - ADAM_B1) * g
    v = ADAM_B2 * v + (1.0 - ADAM_B2) * (g * g)
    m_hat = m / (1.0 - ADAM_B1 ** ADAM_STEP)
    v_hat = v / (1.0 - ADAM_B2 ** ADAM_STEP)
    return -ADAM_LR * (m_hat / (jnp.sqrt(v_hat) + ADAM_EPS) + ADAM_WD * w), m, v


def _adamw_large(parts, w, m, v, name, tr):
    _, r, c = w.shape

    def body(p0_ref, p1_ref, w_ref, m_ref, v_ref, g_ref, d_ref, nm_ref, nv_ref):
        def total(p_ref):
            g = p_ref[0].astype(F32)
            for k in range(1, N_DEV):
                g = g + p_ref[k].astype(F32)
            return g

        g = jnp.where(pl.program_id(0) == 0, total(p0_ref), total(p1_ref))
        g_ref[...] = g
        d_ref[...], nm_ref[...], nv_ref[...] = _adamw_math(w_ref[...], g, m_ref[...], v_ref[...])

    blk = pl.BlockSpec((None, tr, c), lambda l, i: (l, i, 0))
    part = pl.BlockSpec((N_DEV, tr, c), lambda l, i: (0, i, 0))
    return pl.pallas_call(
        body, name=name, grid=(DEPTH, r // tr), in_specs=[part, part, blk, blk, blk],
        out_specs=[blk] * 4, out_shape=[jax.ShapeDtypeStruct(w.shape, F32)] * 4,
        compiler_params=_cparams("parallel", "parallel"))(*parts, w, m, v)


_SLAB_AT = {
    "sgu_w_spatial": (0, 0, SGU_C, LANES),
    "sgu_b_spatial": (128, 0, NH, SGU_C),
    "norm1_w": (132, 0, 1, D),
    "norm2_w": (133, 0, 1, D),
    "sgu_ln_w": (134, 0, 1, G),
    "sgu_ln_b": (134, 256, 1, G),
    "gla_gate_bias": (134, 512, 1, G),
    "dn_norm_w": (134, 768, 1, HD),
    "gla_norm_w": (134, 896, 1, HD),
    "dn_a_log": (135, 0, 1, NH),
    "dn_dt_bias": (135, 128, 1, NH),
    "gla_w_gate2": (136, 0, 16, G),
    "dn_conv_w": (136, 256, 4, 3 * G),
    "sc_conv_w": (140, 256, 3, G),
}
_LAYER_ROWS = 152
_FINAL_ROW = DEPTH * _LAYER_ROWS
_SLAB_ROWS, _SLAB_COLS = _FINAL_ROW + 8, 1024
_SLAB_ORDER = tuple(_SLAB_AT)
_SHARDED_SMALL = ("sc_conv_w", "dn_conv_w", "gla_w_gate2")
_REPLICATED = tuple(k for k in _SLAB_ORDER if k not in _SHARDED_SMALL)


def _region(name, l, h=0):
    r0, c0, nr, nc = _SLAB_AT[name]
    return slice(_LAYER_ROWS * l + r0, _LAYER_ROWS * l + r0 + nr), slice(c0 + nc * h, c0 + nc * (h + 1))


def _pack_small(layer_grads, d_final):
    def body(*refs):
        slab = refs[-1]
        slab[...] = jnp.zeros_like(slab)
        it = iter(refs[:-1])
        for l in range(DEPTH):
            for name in _SLAB_ORDER:
                ref = next(it)
                if name == "sgu_w_spatial":
                    for h in range(NH):
                        slab[_region(name, l, h)] = ref[h]
                else:
                    slab[_region(name, l)] = ref[...]
        slab[_FINAL_ROW:_FINAL_ROW + 1, :] = next(it)[...]

    flat = [layer_grads[l][name] for l in range(DEPTH) for name in _SLAB_ORDER] + [d_final]
    return pl.pallas_call(body, name="pack_small_grads", out_shape=jax.ShapeDtypeStruct((_SLAB_ROWS, _SLAB_COLS), F32),
                          compiler_params=_cparams())(*flat)


def _adamw_small(parts, w, m, v):
    names = _REPLICATED + ("final_norm_w",)
    n_in = len(names)

    def body(*refs):
        def total(rows, cols):
            g = refs[0][0, rows, cols]
            for k in range(1, N_DEV):
                g = g + refs[0][k, rows, cols]
            return g

        w_refs = dict(zip(names, refs[1:1 + n_in]))
        m_refs = dict(zip(names, refs[1 + n_in:1 + 2 * n_in]))
        v_refs = dict(zip(names, refs[1 + 2 * n_in:1 + 3 * n_in]))
        outs = refs[1 + 3 * n_in:]
        out_refs = [dict(zip(names, outs[j * n_in:(j + 1) * n_in])) for j in range(4)]
        full_refs = dict(zip(_SHARDED_SMALL, outs[4 * n_in:]))

        def update(name, idx, g):
            res = (g,) + _adamw_math(w_refs[name][idx], g, m_refs[name][idx], v_refs[name][idx])
            for o, val in zip(out_refs, res):
                o[name][idx] = val

        for l in range(DEPTH):
            for name in _REPLICATED:
                if name == "sgu_w_spatial":
                    for h in range(NH):
                        update(name, (l, h), total(*_region(name, l, h)))
                elif name == "sgu_b_spatial":
                    update(name, (l,), total(*_region(name, l)))
                else:
                    update(name, (slice(l, l + 1), slice(None)), total(*_region(name, l)))
            for name in _SHARDED_SMALL:
                full_refs[name][l] = total(*_region(name, l))
        update("final_norm_w", (slice(None), slice(None)), total(slice(_FINAL_ROW, _FINAL_ROW + 1), slice(None)))

    shapes = [jax.ShapeDtypeStruct(w[k].shape, F32) for k in names]
    full_shapes = [jax.ShapeDtypeStruct((DEPTH,) + _SLAB_AT[k][2:], F32) for k in _SHARDED_SMALL]
    outs = pl.pallas_call(body, name="adamw_small", out_shape=shapes * 4 + full_shapes, compiler_params=_cparams())(
        parts, *[w[k] for k in names], *[m[k] for k in names], *[v[k] for k in names])
    result = [dict(zip(names, outs[j * n_in:(j + 1) * n_in])) for j in range(4)]
    return result, dict(zip(_SHARDED_SMALL, outs[4 * n_in:]))


def _adamw_shards(g, w, m, v):
    names = _SHARDED_SMALL
    n = len(names)

    def body(*refs):
        for j in range(n):
            g_v = refs[j][...]
            res = _adamw_math(refs[n + j][...], g_v, refs[2 * n + j][...], refs[3 * n + j][...])
            for o, val in zip(refs[4 * n + j::n], res):
                o[...] = val

    shapes = [jax.ShapeDtypeStruct(w[k].shape, F32) for k in names]
    outs = pl.pallas_call(body, name="adamw_small_shards", out_shape=shapes * 3, compiler_params=_cparams())(
        *[g[k] for k in names], *[w[k] for k in names], *[m[k] for k in names], *[v[k] for k in names])
    return [dict(zip(names, outs[j * n:(j + 1) * n])) for j in range(3)]


_ANY = pl.BlockSpec(memory_space=pl.ANY)


def _place():
    return lax.axis_index("x"), lax.axis_index("y"), lax.axis_index("c")


def _sems(n):
    return [pltpu.SemaphoreType.DMA((n, 7)), pltpu.SemaphoreType.DMA((n, 7)), pltpu.SemaphoreType.DMA((n,))]


class _Gather:
    def __init__(self, blocks, second):
        self.inputs, self.second = list(blocks), list(second)
        self.out_shape = [jax.ShapeDtypeStruct((a.shape[0], N_DEV) + a.shape[1:] if s else (N_DEV,) + a.shape, a.dtype)
                          for a, s in zip(blocks, second)]
        self.scratch = _sems(len(blocks))

    def _copies(self, refs):
        x_refs, out_refs, (send_sems, recv_sems, local_sems) = refs
        n = len(x_refs)
        x, y, c = _place()
        me, sibling = (x, y, c), (x, y, 1 - c)
        chips = [(1 - x, y), (x, 1 - y), (1 - x, 1 - y)]

        def slot(j, px, py, pc):
            idx = 4 * px + 2 * py + pc
            return out_refs[j].at[:, idx] if self.second[j] else out_refs[j].at[idx]

        def copies(k, block, to, own=False):
            return [pltpu.make_async_remote_copy(
                src_ref=x_refs[j] if own else slot(j, *block), dst_ref=slot(j, *block),
                send_sem=send_sems.at[j, k], recv_sem=recv_sems.at[j, k], device_id=to,
                device_id_type=pl.DeviceIdType.MESH) for j in range(n)]

        mine = [pltpu.make_async_copy(x_refs[j], slot(j, *me), local_sems.at[j]) for j in range(n)]
        first = copies(0, me, sibling, own=True)
        for j, chip in enumerate(chips):
            first += copies(1 + j, me, (*chip, c), own=True)
        landed = [copies(1 + j, (*chip, c), me) for j, chip in enumerate(chips)]
        onward = [copies(4 + j, (*chip, c), sibling) for j, chip in enumerate(chips)]
        from_sibling = copies(0, sibling, me)
        for j, chip in enumerate(chips):
            from_sibling += copies(4 + j, (*chip, 1 - c), me)
        return mine, first, landed, onward, from_sibling

    def start(self, refs):
        mine, first, _, _, _ = self._copies(refs)
        for cp in mine + first:
            cp.start()

    def forward(self, refs):
        _, _, landed, onward, _ = self._copies(refs)
        for arrived, passed in zip(landed, onward):
            for cp in arrived:
                cp.wait_recv()
            for cp in passed:
                cp.start()

    def finish(self, refs):
        mine, first, _, onward, from_sibling = self._copies(refs)
        for cp in from_sibling:
            cp.wait_recv()
        for cp in first + [cp for passed in onward for cp in passed]:
            cp.wait_send()
        for cp in mine:
            cp.wait()


class _Exchange:
    def __init__(self, sends):
        self.inputs = list(sends)
        self.out_shape = [jax.ShapeDtypeStruct(a.shape, a.dtype) for a in sends]
        self.scratch = _sems(len(sends))

    def _copies(self, refs):
        x_refs, out_refs, (send_sems, recv_sems, local_sems) = refs
        n = len(x_refs)
        x, y, c = _place()
        me = 4 * x + 2 * y + c
        sent, landing = [], []
        for k in range(1, N_DEV):
            px, py, pc = x ^ ((k >> 2) & 1), y ^ ((k >> 1) & 1), c ^ (k & 1)
            them = 4 * px + 2 * py + pc
            for j in range(n):
                for here, there, into in ((them, me, sent), (me, them, landing)):
                    into.append(pltpu.make_async_remote_copy(
                        src_ref=x_refs[j].at[here], dst_ref=out_refs[j].at[there], send_sem=send_sems.at[j, k - 1],
                        recv_sem=recv_sems.at[j, k - 1], device_id=(px, py, pc), device_id_type=pl.DeviceIdType.MESH))
        mine = [pltpu.make_async_copy(x_refs[j].at[me], out_refs[j].at[me], local_sems.at[j]) for j in range(n)]
        return mine, sent, landing

    def start(self, refs):
        mine, sent, _ = self._copies(refs)
        for cp in mine + sent:
            cp.start()

    def forward(self, refs):
        pass

    def finish(self, refs):
        mine, sent, landing = self._copies(refs)
        for cp in landing:
            cp.wait_recv()
        for cp in sent:
            cp.wait_send()
        for cp in mine:
            cp.wait()


def _run(rider, name):
    n_in, n_out = len(rider.inputs), len(rider.out_shape)

    def body(*refs):
        parts = (refs[:n_in], refs[n_in:n_in + n_out], refs[n_in + n_out:])
        rider.start(parts)
        rider.forward(parts)
        rider.finish(parts)

    return pl.pallas_call(body, name=name, out_shape=rider.out_shape, in_specs=[_ANY] * n_in, out_specs=[_ANY] * n_out,
                          scratch_shapes=rider.scratch)(*rider.inputs)


def _split_refs(refs, n_in, n_out, n_scratch, rider):
    r_in = len(rider.inputs) if rider else 0
    r_out = len(rider.out_shape) if rider else 0
    a = n_in + r_in
    b = a + n_out + r_out
    own = refs[:n_in] + refs[a:a + n_out] + refs[b:b + n_scratch]
    return own, (refs[n_in:a], refs[a + n_out:b], refs[b + n_scratch:])


def _ride_begin(rider, ride_refs):
    if rider is not None:
        pl.when(pl.program_id(0) == 0)(lambda: rider.start(ride_refs))


def _ride_end(rider, ride_refs):
    if rider is not None:
        @pl.when(pl.program_id(0) == pl.num_programs(0) - 1)
        def _():
            rider.forward(ride_refs)
            rider.finish(ride_refs)


def _host(rider, in_specs, out_specs, out_shape, scratch, operands):
    if rider is None:
        return dict(in_specs=in_specs, out_specs=out_specs, out_shape=out_shape, scratch_shapes=scratch), operands
    return dict(in_specs=in_specs + [_ANY] * len(rider.inputs), out_specs=out_specs + [_ANY] * len(rider.out_shape),
                out_shape=out_shape + rider.out_shape, scratch_shapes=scratch + rider.scratch), operands + rider.inputs


_LATE = ("w_gate_up", "w_out", "w_down")


def _local_grads(x, target, w, late=None, exchange=False):
    w = dict(w)
    w_in = list(w["w_in"])
    dn_params = (w["dn_a_log"], w["dn_dt_bias"], w["dn_norm_w"])
    gla_params = (w["gla_w_gate2"], w["gla_gate_bias"], w["gla_norm_w"])
    sgu_params = (w["sgu_ln_w"], w["sgu_ln_b"], w["sgu_w_spatial"], w["sgu_b_spatial"])
    saved = []
    for l in range(DEPTH):
        riding = l == 0 and late is not None
        h, p, *got = _in_proj(x, w["norm1_w"], w_in[l], l,
                              rider=_Gather([late["w_out"], late["w_down"]], [True, True]) if riding else None)
        if riding:
            w.update(w_out=got[0].reshape(DEPTH, D, D), w_down=got[1].reshape(DEPTH, FF, D))
        ya = _sgu_fwd(p, *sgu_params, l)
        yb = _sconv_fwd(p, w["sc_conv_w"], l)
        qkv_c = _qkv_conv_fwd(p, w["dn_conv_w"], l)
        yc, st_c, inv_c, *got = _dn_fwd(qkv_c, p, dn_params, l,
                                        rider=_Gather([late["w_gate_up"], late["w_in"]], [False, False]) if riding else None)
        if riding:
            w["w_gate_up"] = got[0]
            w_in[1] = _relayout_w_in(got[1])
        yd, st_d = _gla_fwd(p, gla_params, l)
        mix, x1, h2, gu, act, x2 = _out_mlp(x, (ya, yb, yc, yd), w["w_out"], w["norm2_w"], w["w_gate_up"], w["w_down"], l)
        saved.append(dict(x=x, h=h, p=p, qkv_c=qkv_c, st_c=st_c, inv_c=inv_c, st_d=st_d, mix=mix, x1=x1, h2=h2, gu=gu,
                          act=act))
        x = x2
    loss, d_final, dx = _loss_head(x, w["final_norm_w"], target)
    large = {k: [None] * DEPTH for k in ("w_in", "w_out", "w_gate_up", "w_down")}
    small = [None] * DEPTH
    for l in reversed(range(DEPTH)):
        s = saved[l]
        p = s["p"]
        g = {}
        riding = [(k, 1) for k in _LATE] if exchange and l == 0 else []
        dgu, dx1, dmix, g["norm2_w"], *arrived = _mlp_out_bwd(
            dx, s["x1"], s["gu"], w["norm2_w"], w["w_down"], w["w_gate_up"], w["w_out"], l,
            rider=_Exchange([large[k][j] for k, j in riding]) if riding else None)
        for (k, j), a in zip(riding, arrived):
            large[k][j] = a
        large["w_gate_up"][l] = _wgrad_rows(dgu, s["h2"], "wgrad_gate_up").reshape(N_DEV, GU_SHARD, D)
        large["w_down"][l] = _wgrad_rows(s["act"], dx, "wgrad_down").reshape(N_DEV, FF // N_DEV, D)
        large["w_out"][l] = _matmul_tn(s["mix"], dx1, "wgrad_out", BF16).reshape(N_DEV, D // N_DEV, D)
        d_a, g["sgu_ln_w"], g["sgu_ln_b"], g["sgu_w_spatial"], g["sgu_b_spatial"] = _sgu_bwd(p, dmix, *sgu_params, l)
        d_b, g["sc_conv_w"] = _sconv_bwd(p, dmix, w["sc_conv_w"], l)
        riding = [("w_in", 1)] + [(k, 0) for k in _LATE] if exchange and l == 0 else []
        dqkv_c, dz_c, dab, g["dn_a_log"], g["dn_dt_bias"], g["dn_norm_w"], *arrived = _dn_bwd(
            s["qkv_c"], p, s["st_c"], s["inv_c"], dmix, dn_params, l,
            rider=_Exchange([large[k][j] for k, j in riding]) if riding else None)
        for (k, j), a in zip(riding, arrived):
            large[k][j] = a
        d_c, g["dn_conv_w"] = _qkv_conv_bwd(p, dqkv_c, w["dn_conv_w"], l)
        d_d, dz_d, dglr, g["gla_w_gate2"], g["gla_gate_bias"], g["gla_norm_w"] = _gla_bwd(p, s["st_d"], dmix, gla_params, l)
        dp, dx, g["norm1_w"] = _in_proj_bwd((d_c, d_d, d_a, d_b, dz_c, dz_d, dab, dglr), s["x"], dx1, w["norm1_w"],
                                            w_in[l], l)
        large["w_in"][l] = _scatter_w_in_grad(_matmul_tn(s["h"], dp, "wgrad_in", F32))
        small[l] = g
    return loss[0, 0], dx, large, small, d_final


_ORDER = ("norm1_w", "w_in", "sgu_ln_w", "sgu_ln_b", "sgu_w_spatial", "sgu_b_spatial", "sc_conv_w", "dn_conv_w",
          "dn_a_log", "dn_dt_bias", "dn_norm_w", "gla_w_gate2", "gla_gate_bias", "gla_norm_w", "w_out", "norm2_w",
          "w_gate_up", "w_down", "final_norm_w")
_LARGE = ("w_in", "w_gate_up", "w_out", "w_down")
_LARGE_TILE = {"w_in": 256, "w_gate_up": 352, "w_out": 128, "w_down": 352}


def _gather_cols(g):
    return jnp.transpose(g, (1, 2, 0, 3)).reshape(g.shape[1], g.shape[2], N_DEV * g.shape[3])


def kernel(x, norm1_w, w_in, sgu_ln_w, sgu_ln_b, sgu_w_spatial, sgu_b_spatial, sc_conv_w, dn_conv_w, dn_a_log, dn_dt_bias, dn_norm_w, gla_w_gate2, gla_gate_bias, gla_norm_w, w_out, norm2_w, w_gate_up, w_down, final_norm_w, loss_target, m_norm1_w, m_w_in, m_sgu_ln_w, m_sgu_ln_b, m_sgu_w_spatial, m_sgu_b_spatial, m_sc_conv_w, m_dn_conv_w, m_dn_a_log, m_dn_dt_bias, m_dn_norm_w, m_gla_w_gate2, m_gla_gate_bias, m_gla_norm_w, m_w_out, m_norm2_w, m_w_gate_up, m_w_down, m_final_norm_w, v_norm1_w, v_w_in, v_sgu_ln_w, v_sgu_ln_b, v_sgu_w_spatial, v_sgu_b_spatial, v_sc_conv_w, v_dn_conv_w, v_dn_a_log, v_dn_dt_bias, v_dn_norm_w, v_gla_w_gate2, v_gla_gate_bias, v_gla_norm_w, v_w_out, v_norm2_w, v_w_gate_up, v_w_down, v_final_norm_w):
    args = dict(locals())
    wts = {k: args[k] for k in _ORDER}
    mom = {k: args["m_" + k] for k in _ORDER}
    var = {k: args["v_" + k] for k in _ORDER}
    for d in (wts, mom, var):
        d["final_norm_w"] = d["final_norm_w"][None]
    me = 4 * lax.axis_index("x") + 2 * lax.axis_index("y") + lax.axis_index("c")
    for d in (wts, mom, var):
        d["w_gate_up"] = jnp.swapaxes(d["w_gate_up"], 1, 2)

    late = {k: wts[k].astype(BF16) for k in _LATE}
    w_in = wts["w_in"].astype(BF16)
    late["w_in"] = w_in[1]
    got = _run(_Gather([w_in[0]] + [wts[k] for k in _SHARDED_SMALL], [False] * 4), "gather_w_in")
    full = dict(wts)
    full["w_in"] = [_relayout_w_in(got[0]), None]
    for k, g in zip(_SHARDED_SMALL, got[1:]):
        full[k] = _gather_cols(g)

    loss, dx, large, small, d_final = _local_grads(x[0], loss_target[0], full, late=late, exchange=True)
    loss = lax.psum(loss, ("x", "y", "c"))

    large["w_in"][0], = _run(_Exchange([large["w_in"][0]]), "exchange_grads")
    result = {}
    for k in _LARGE:
        outs = _adamw_large(large[k], wts[k], mom[k], var[k], "adamw_" + k, _LARGE_TILE[k])
        for kind, a in zip(("grad", "delta", "new_m", "new_v"), outs):
            result[kind, k] = jnp.swapaxes(a, 1, 2) if k == "w_gate_up" else a

    slabs = _run(_Gather([_pack_small(small, d_final)], [False]), "gather_small_grads")[0]
    rep = _REPLICATED + ("final_norm_w",)
    outs, summed = _adamw_small(slabs, {k: wts[k] for k in rep}, {k: mom[k] for k in rep}, {k: var[k] for k in rep})
    for kind, d in zip(("grad", "delta", "new_m", "new_v"), outs):
        for k, a in d.items():
            result[kind, k] = a[0] if k == "final_norm_w" else a
    own = {k: lax.dynamic_slice_in_dim(summed[k], me * wts[k].shape[-1], wts[k].shape[-1], axis=2) for k in _SHARDED_SMALL}
    outs = _adamw_shards(own, {k: wts[k] for k in _SHARDED_SMALL}, {k: mom[k] for k in _SHARDED_SMALL},
                         {k: var[k] for k in _SHARDED_SMALL})
    for kind, d in zip(("grad", "delta", "new_m", "new_v"), [own] + outs):
        for k, a in d.items():
            result[kind, k] = a

    return (loss, dx[None], *[result[kind, k] for kind in ("grad", "delta", "new_m", "new_v") for k in _ORDER])
```

```python
import functools

import jax
import jax.numpy as jnp
from jax import lax
from jax.experimental import pallas as pl
from jax.experimental.pallas import tpu as pltpu

F32, BF16 = jnp.float32, jnp.bfloat16
DEPTH = 2
D = 1024
G = 256
NH, HD = 4, 64
FF = 2816
IN_COLS = 3352
P = 3584
EPS = 1e-6
SGU_C, DN_C, GLA_C = 128, 64, 64
N_DEV = 8
IN_SHARD = IN_COLS // N_DEV
GU_SHARD = 2 * FF // N_DEV
N_GATE = N_DEV // 2
LANES = 128
VMEM_LIMIT = 56 * 2 ** 20

_PAD_SEGS = ((1280, 2048, 0),
             (2312, 3080, 0),
             (0, 512, 0),
             (512, 1280, 0),
             (2056, 2312, 0),
             (3096, 3352, 0),
             (2048, 2056, 120),
             (3080, 3096, 112))

ADAM_LR, ADAM_B1, ADAM_B2, ADAM_EPS, ADAM_WD, ADAM_STEP = 0.001, 0.9, 0.999, 1e-08, 0.01, 10


def _cparams(*sem):
    return pltpu.CompilerParams(dimension_semantics=sem, vmem_limit_bytes=VMEM_LIMIT)


def _resident(shape):
    return pl.BlockSpec(shape, lambda *_: (0,) * len(shape), pipeline_mode=pl.Buffered(1))


def _layer(shape, l):
    return pl.BlockSpec((None,) + tuple(shape), lambda *_: (l,) + (0,) * len(shape), pipeline_mode=pl.Buffered(1))


def _acc(shape):
    return pl.BlockSpec(shape, lambda *_: (0,) * len(shape))


def _dg(a, b, ca, cb, precision=None):
    lead = a.ndim - 2
    batch = ((0,), (0,)) if lead else ((), ())
    return lax.dot_general(a, b, (((ca + lead,), (cb + lead,)), batch), preferred_element_type=F32, precision=precision)


def _make_dot(cast, precision):
    def raw(a, b, form):
        ca = 0 if form == "tn" else 1
        cb = 1 if form == "nt" else 0
        return _dg_any(cast(a), cast(b), ca, cb, precision)

    @functools.partial(jax.custom_vjp, nondiff_argnums=(2,))
    def dot(a, b, form):
        return raw(a, b, form)

    def fwd(a, b, form):
        return raw(a, b, form), (a, b)

    def bwd(form, res, g):
        a, b = res
        if form == "nn":
            return raw(g, b, "nt"), raw(a, g, "tn")
        if form == "nt":
            return raw(g, b, "nn"), raw(g, a, "tn")
        return raw(b, g, "nt"), raw(a, g, "nn")

    dot.defvjp(fwd, bwd)
    return dot


def _split(t):
    hi = t.astype(BF16)
    return hi, (t - hi.astype(F32)).astype(BF16)


def _dg3(a, b, ca, cb):
    (ah, al), (bh, bl) = a, b
    return _dg(ah, bh, ca, cb) + (_dg(ah, bl, ca, cb) + _dg(al, bh, ca, cb))


class _Split3:
    pass


def _dg_any(a, b, ca, cb, precision):
    if precision is _Split3:
        return _dg3(_split(a), _split(b), ca, cb)
    return _dg(a, b, ca, cb, precision)


_bdot = _make_dot(lambda t: t.astype(BF16), None)
_hdot = _make_dot(lambda t: t, _Split3)


def _inv_unit_lower(low):
    n = low.shape[-1]
    eye = (lax.broadcasted_iota(jnp.int32, (n, n), 0) == lax.broadcasted_iota(jnp.int32, (n, n), 1)).astype(F32)
    p = -low
    inv = eye + p
    ps = _split(p)
    k = 1
    while 2 * k < n:
        ps = _split(_dg3(ps, ps, 1, 0))
        inv = inv + _dg3(_split(inv), ps, 1, 0)
        k *= 2
    return inv


@jax.custom_vjp
def _unit_lower_solve(low, rhs, inv):
    return _dg3(_split(inv), _split(rhs), 1, 0)


def _uls_fwd(low, rhs, inv):
    sol = _dg3(_split(inv), _split(rhs), 1, 0)
    return sol, (inv, sol)


def _uls_bwd(res, g):
    inv, sol = res
    d_rhs = _dg3(_split(inv), _split(g), 0, 0)
    return -_dg3(_split(d_rhs), _split(sol), 1, 1), d_rhs, jnp.zeros_like(inv)


_unit_lower_solve.defvjp(_uls_fwd, _uls_bwd)


def _sigmoid(x):
    return 1.0 / (1.0 + jnp.exp(-x))


def _softplus(x):
    return jnp.maximum(x, 0.0) + jnp.log(1.0 + jnp.exp(-jnp.maximum(x, -x)))


def _gelu(x):
    return 0.5 * x * (1.0 + jnp.tanh(0.7978845608028654 * (x + 0.044715 * (x * x * x))))


def _tri(n):
    ri = lax.broadcasted_iota(jnp.int32, (n, n), 0)
    ci = lax.broadcasted_iota(jnp.int32, (n, n), 1)
    return ri, ci


def _stack(ts):
    return jnp.concatenate([t[None] for t in ts], axis=0)


def _sgu_chunk(uv, ln_w, ln_b, ws, bs):
    u = _gelu(uv[:, :G])
    v = _gelu(uv[:, G:])
    mu = jnp.mean(v, axis=-1, keepdims=True)
    vc = v - mu
    var = jnp.mean(vc * vc, axis=-1, keepdims=True)
    vn = vc * lax.rsqrt(var + EPS) * ln_w + ln_b
    ri, ci = _tri(SGU_C)
    bs_t = _hdot((ri == ci).astype(F32), bs, "nt")
    mixed = []
    for h in range(NH):
        wm = jnp.where(ri >= ci, ws[h], 0.0)
        mixed.append(_bdot(wm, vn[:, HD * h:HD * (h + 1)], "nn") + bs_t[:, h:h + 1])
    return u * jnp.concatenate(mixed, axis=1)


def _dn_tile(q, k, v, ab, z, state, inv, a_log, dt_bias, nw):
    c = DN_C
    tm = q.shape[0]
    cps = tm // c
    ri, ci = _tri(tm)
    shift = c.bit_length() - 1
    same = jnp.right_shift(ri, shift) == jnp.right_shift(ci, shift)
    g4 = -jnp.exp(a_log) * _softplus(ab[:, 0:NH] + dt_bias)
    beta4 = _sigmoid(ab[:, NH:2 * NH])
    gc4 = _hdot((same & (ri >= ci)).astype(F32), g4, "nn")
    gr4 = _hdot(g4, (same & (ri <= ci)).astype(F32), "tn")
    pairs = [(slice(c * j, c * (j + 1)), h) for j in range(cps) for h in range(NH)]
    heads = lambda t: _stack([t[rows, HD * h:HD * (h + 1)] for rows, h in pairs])
    col = lambda t: _stack([t[rows, h:h + 1] for rows, h in pairs])
    qn, kn, vh = heads(q), heads(k), heads(v)
    qn = qn * lax.rsqrt(jnp.sum(qn * qn, axis=-1, keepdims=True) + EPS) * (HD ** -0.5)
    kn = kn * lax.rsqrt(jnp.sum(kn * kn, axis=-1, keepdims=True) + EPS)
    gc, beta = col(gc4), col(beta4)
    gr = _stack([gr4[h:h + 1, rows] for rows, h in pairs])
    gl = jnp.sum(col(g4), axis=1, keepdims=True)
    r2, c2 = _tri(c)
    decay = jnp.exp(jnp.where(r2 >= c2, gc - gr, -jnp.inf))
    kb = kn * beta
    low = jnp.where(r2 > c2, _bdot(kb, kn, "nt") * decay, 0.0)
    eg = jnp.exp(gc)
    if inv is None:
        inv = _inv_unit_lower(low)
    sol = _unit_lower_solve(low, jnp.concatenate([vh * beta, kb * eg], axis=2), inv)
    u, w = sol[:, :, :HD], sol[:, :, HD:]
    attn = _bdot(qn, kn, "nt") * decay
    qg, kd, cd = qn * eg, kn * jnp.exp(gl - gc), jnp.exp(gl)
    ys = []
    for j in range(cps):
        b = slice(NH * j, NH * (j + 1))
        v_new = u[b] - _bdot(w[b], state, "nn")
        o = _bdot(qg[b], state, "nn") + _bdot(attn[b], v_new, "nn")
        state = state * cd[b] + _bdot(kd[b], v_new, "tn")
        on = o * lax.rsqrt(jnp.mean(o * o, axis=-1, keepdims=True) + EPS) * nw
        ys.append(jnp.concatenate([on[h] for h in range(NH)], axis=1))
    return jnp.concatenate(ys, axis=0) * (z * _sigmoid(z)), state, inv


def _gla_tile(q, k, v, glr, z, state_t, w2, gate_bias, nw):
    c = GLA_C
    tm = q.shape[0]
    cps = tm // c
    ri, ci = _tri(tm)
    shift = c.bit_length() - 1
    same = jnp.right_shift(ri, shift) == jnp.right_shift(ci, shift)
    w2_rows = jnp.concatenate([w2, jnp.zeros((LANES - w2.shape[0], G), F32)], axis=0)
    pre = _bdot(glr, w2_rows, "nn") + gate_bias
    log_a = (jnp.minimum(pre, 0.0) - jnp.log(1.0 + jnp.exp(-jnp.maximum(pre, -pre)))) * (1.0 / 16.0)
    gcum = _hdot((same & (ri >= ci)).astype(F32), log_a, "nn")
    row = lax.broadcasted_iota(jnp.int32, (c, G), 0)
    qs = q * (HD ** -0.5)
    qa, ka, qg, kl, dec = [], [], [], [], []
    for j in range(cps):
        rows = slice(c * j, c * (j + 1))
        gj = gcum[rows]
        g_mid = jnp.sum(jnp.where(row == c // 2, gj, 0.0), axis=0, keepdims=True)
        g_last = jnp.sum(log_a[rows], axis=0, keepdims=True)
        qa.append(qs[rows] * jnp.exp(gj - g_mid))
        ka.append(k[rows] * jnp.exp(g_mid - gj))
        qg.append(qs[rows] * jnp.exp(gj))
        kl.append(k[rows] * jnp.exp(g_last - gj))
        dec.append(jnp.exp(g_last))
    heads = lambda ts: _stack([t[:, HD * h:HD * (h + 1)] for t in ts for h in range(NH)])
    qa, ka, qg, kl, dec = heads(qa), heads(ka), heads(qg), heads(kl), heads(dec)
    vh = heads([v[c * j:c * (j + 1)] for j in range(cps)])
    r2, c2 = _tri(c)
    o_intra = _bdot(jnp.where(r2 >= c2, _bdot(qa, ka, "nt"), 0.0), vh, "nn")
    ys = []
    for j in range(cps):
        b = slice(NH * j, NH * (j + 1))
        o = o_intra[b] + _bdot(qg[b], state_t, "nt")
        state_t = state_t * dec[b] + _bdot(vh[b], kl[b], "tn")
        on = o * lax.rsqrt(jnp.mean(o * o, axis=-1, keepdims=True) + EPS) * nw
        ys.append(jnp.concatenate([on[h] for h in range(NH)], axis=1))
    return jnp.concatenate(ys, axis=0) * (z * _sigmoid(z)), state_t


def _rms(x, nw):
    r = lax.rsqrt(jnp.mean(x * x, axis=-1, keepdims=True) + EPS)
    xh = x * r
    return xh * nw, xh, r


def _rms_bwd(g, xh, r, nw):
    gw = g * nw
    return r * (gw - xh * jnp.mean(gw * xh, axis=-1, keepdims=True)), jnp.sum(g * xh, axis=0, keepdims=True)


def _row(tm, w, blk=0):
    return pl.BlockSpec((tm, w), lambda i: (i, blk))


def _in_proj(x, nw, wp, l, tm=512, rider=None):
    t = x.shape[0]

    def body(*refs):
        (x_ref, nw_ref, w_ref, h_ref, p_ref), ride = _split_refs(refs, 3, 2, 0, rider)
        _ride_begin(rider, ride, (t // tm,))
        h = _rms(x_ref[...], nw_ref[l:l + 1, :])[0].astype(BF16)
        h_ref[...] = h
        p_ref[...] = jnp.dot(h, w_ref[...], preferred_element_type=F32)
        _ride_end(rider, ride, (t // tm,))

    specs, operands = _host(
        rider, [_row(tm, D), _resident((DEPTH, D)), _resident((D, P))], [_row(tm, D), _row(tm, P)],
        [jax.ShapeDtypeStruct((t, D), BF16), jax.ShapeDtypeStruct((t, P), F32)], [], [x, nw, wp])
    return pl.pallas_call(body, name="in_proj", grid=(t // tm,), compiler_params=_cparams("arbitrary"),
                          **specs)(*operands)


def _gu_spec(l):
    return pl.BlockSpec((N_DEV, None, GU_SHARD, D), lambda *_: (0, l, 0, 0), pipeline_mode=pl.Buffered(1))


def _out_mlp(x, ys, w_out, nw2, w_gu_t, w_down, l, tm=256):
    t = x.shape[0]

    def body(x_ref, ya, yb, yc, yd, wo_ref, nw_ref, wgu_ref, wd_ref, mix_ref, x1_ref, h2_ref, gu_ref, act_ref, x2_ref):
        mix = jnp.concatenate([ya[...], yb[...], yc[...], yd[...]], axis=1)
        mix_ref[...] = mix
        x1 = x_ref[...] + jnp.dot(mix, wo_ref[...], preferred_element_type=F32)
        x1_ref[...] = x1
        h2 = _rms(x1, nw_ref[l:l + 1, :])[0].astype(BF16)
        h2_ref[...] = h2
        gu = _dg(h2, wgu_ref[...].reshape(2 * FF, D), 1, 1)
        gu_ref[...] = gu.astype(BF16)
        gate, up = gu[:, :FF], gu[:, FF:]
        act = (gate * _sigmoid(gate) * up).astype(BF16)
        act_ref[...] = act
        x2_ref[...] = x1 + jnp.dot(act, wd_ref[...], preferred_element_type=F32)

    return pl.pallas_call(
        body, name="out_mlp", grid=(t // tm,),
        in_specs=[_row(tm, D), _row(tm, G), _row(tm, G), _row(tm, G), _row(tm, G), _layer((D, D), l),
                  _resident((DEPTH, D)), _gu_spec(l), _layer((FF, D), l)],
        out_specs=[_row(tm, D), _row(tm, D), _row(tm, D), _row(tm, 2 * FF), _row(tm, FF), _row(tm, D)],
        out_shape=[jax.ShapeDtypeStruct((t, D), BF16), jax.ShapeDtypeStruct((t, D), F32),
                   jax.ShapeDtypeStruct((t, D), BF16), jax.ShapeDtypeStruct((t, 2 * FF), BF16),
                   jax.ShapeDtypeStruct((t, FF), BF16), jax.ShapeDtypeStruct((t, D), F32)],
        compiler_params=_cparams("parallel"))(x, *ys, w_out, nw2, w_gu_t, w_down)


def _loss_head(x, nw, target, tm=512):
    t = x.shape[0]

    def body(x_ref, nw_ref, tg_ref, loss_ref, dnw_ref, dx_ref):
        @pl.when(pl.program_id(0) == 0)
        def _():
            loss_ref[...] = jnp.zeros_like(loss_ref)
            dnw_ref[...] = jnp.zeros_like(dnw_ref)
        nw_v = nw_ref[...]
        y, xh, r = _rms(x_ref[...], nw_v)
        err = y - tg_ref[...]
        loss_ref[...] += 0.5 * jnp.sum(err * err) * (1.0 / D)
        dx, dnw = _rms_bwd(err * (1.0 / D), xh, r, nw_v)
        dx_ref[...] = dx
        dnw_ref[...] += dnw

    return pl.pallas_call(
        body, name="loss_head", grid=(t // tm,),
        in_specs=[_row(tm, D), _resident((1, D)), _row(tm, D)],
        out_specs=[_acc((8, LANES)), _acc((1, D)), _row(tm, D)],
        out_shape=[jax.ShapeDtypeStruct((8, LANES), F32), jax.ShapeDtypeStruct((1, D), F32),
                   jax.ShapeDtypeStruct((t, D), F32)],
        compiler_params=_cparams("arbitrary"))(x, nw, target)


def _mlp_out_bwd(dx2, x1, gu, nw2, w_down, w_gu, w_out, l, tm=256, rider=None):
    t = dx2.shape[0]

    def body(*refs):
        (dx2_ref, x1_ref, gu_ref, nw_ref, wd_ref, wgu_ref, wo_ref, dgu_ref, dx1_ref, dmix_ref, dnw_ref), ride = \
            _split_refs(refs, 7, 4, 0, rider)
        _ride_begin(rider, ride, (t // tm,))

        @pl.when(pl.program_id(0) == 0)
        def _():
            dnw_ref[...] = jnp.zeros_like(dnw_ref)
        dx2 = dx2_ref[...]
        dact = _dg(dx2.astype(BF16), wd_ref[...], 1, 1)
        gu = gu_ref[...].astype(F32)
        gate, up = gu[:, :FF], gu[:, FF:]
        s = _sigmoid(gate)
        dgu = jnp.concatenate([dact * up * (s * (1.0 + gate * (1.0 - s))), dact * (gate * s)], axis=1).astype(BF16)
        dgu_ref[...] = dgu
        dh2 = jnp.dot(dgu, wgu_ref[...].reshape(2 * FF, D), preferred_element_type=F32)
        nw_v = nw_ref[l:l + 1, :]
        _, xh, r = _rms(x1_ref[...], nw_v)
        dxn, dnw = _rms_bwd(dh2, xh, r, nw_v)
        dx1 = dx2 + dxn
        dx1_ref[...] = dx1
        dnw_ref[...] += dnw
        dmix_ref[...] = _dg(dx1.astype(BF16), wo_ref[...], 1, 1)
        _ride_end(rider, ride, (t // tm,))

    specs, operands = _host(
        rider, [_row(tm, D), _row(tm, D), _row(tm, 2 * FF), _resident((DEPTH, D)), _layer((FF, D), l), _gu_spec(l),
                _layer((D, D), l)],
        [_row(tm, 2 * FF), _row(tm, D), _row(tm, D), _acc((1, D))],
        [jax.ShapeDtypeStruct((t, 2 * FF), BF16), jax.ShapeDtypeStruct((t, D), F32),
         jax.ShapeDtypeStruct((t, D), F32), jax.ShapeDtypeStruct((1, D), F32)],
        [], [dx2, x1, gu, nw2, w_down, w_gu, w_out])
    return pl.pallas_call(body, name="mlp_out_bwd", grid=(t // tm,), compiler_params=_cparams("arbitrary"),
                          **specs)(*operands)


_DP_WIDTHS = (768, 768, 512, 768, 256, 256, 128, 128)


def _in_proj_bwd(dps, x, dx1, nw, wp, l, tm=512):
    t = x.shape[0]

    def body(*refs):
        dp_refs, (x_ref, dx1_ref, nw_ref, w_ref, dp_ref, dx_ref, dnw_ref) = refs[:8], refs[8:]

        @pl.when(pl.program_id(0) == 0)
        def _():
            dnw_ref[...] = jnp.zeros_like(dnw_ref)
        dp = jnp.concatenate([r[...] for r in dp_refs], axis=1)
        dp_ref[...] = dp
        dh = _dg(dp, w_ref[...], 1, 1)
        nw_v = nw_ref[l:l + 1, :]
        _, xh, r = _rms(x_ref[...], nw_v)
        dxn, dnw = _rms_bwd(dh, xh, r, nw_v)
        dx_ref[...] = dx1_ref[...] + dxn
        dnw_ref[...] += dnw

    return pl.pallas_call(
        body, name="in_proj_bwd", grid=(t // tm,),
        in_specs=[_row(tm, w) for w in _DP_WIDTHS] + [_row(tm, D), _row(tm, D), _resident((DEPTH, D)), _resident((D, P))],
        out_specs=[_row(tm, P), _row(tm, D), _acc((1, D))],
        out_shape=[jax.ShapeDtypeStruct((t, P), BF16), jax.ShapeDtypeStruct((t, D), F32),
                   jax.ShapeDtypeStruct((1, D), F32)],
        compiler_params=_cparams("arbitrary"))(*dps, x, dx1, nw, wp)


def _accumulate(acc, o_ref, t_axis, term):
    @pl.when(pl.program_id(t_axis) == 0)
    def _():
        acc[...] = jnp.zeros_like(acc)
    acc[...] += term

    @pl.when(pl.program_id(t_axis) == pl.num_programs(t_axis) - 1)
    def _():
        o_ref[...] = acc[...].astype(o_ref.dtype)


WGRAD_TOKENS = 2048


def _matmul_tn(a, b, name, out_dtype, tn=512, tt=WGRAD_TOKENS, rider=None):
    t, m = a.shape
    n = b.shape[1]
    tt = min(tt, t)
    grid = (n // tn, t // tt)

    def body(*refs):
        (a_ref, b_ref, o_ref, acc), ride = _split_refs(refs, 2, 1, 1, rider)
        _ride_begin(rider, ride, grid)
        _accumulate(acc, o_ref, 1, _dg(a_ref[...].astype(BF16), b_ref[...].astype(BF16), 0, 0))
        _ride_end(rider, ride, grid)

    specs, operands = _host(
        rider, [pl.BlockSpec((tt, m), lambda j, i: (i, 0)), pl.BlockSpec((tt, tn), lambda j, i: (i, j))],
        [pl.BlockSpec((m, tn), lambda j, i: (0, j))], [jax.ShapeDtypeStruct((m, n), out_dtype)],
        [pltpu.VMEM((m, tn), F32)], [a, b])
    out = pl.pallas_call(body, name=name, grid=grid, compiler_params=_cparams("arbitrary", "arbitrary"), **specs)(*operands)
    return out[0] if rider is None else out


WGRAD_ROWS = 2 * GU_SHARD


def _wgrad_rows(a, b, name, tt=WGRAD_TOKENS):
    t, m = a.shape
    tt = min(tt, t)

    def body(a_ref, b_ref, o_ref, acc):
        _accumulate(acc, o_ref, 1, _dg(a_ref[...], b_ref[...].astype(BF16), 0, 0))

    return pl.pallas_call(
        body, name=name, grid=(m // WGRAD_ROWS, t // tt),
        in_specs=[pl.BlockSpec((tt, WGRAD_ROWS), lambda j, i: (i, j)), pl.BlockSpec((tt, D), lambda j, i: (i, 0))],
        out_specs=pl.BlockSpec((WGRAD_ROWS, D), lambda j, i: (j, 0)),
        out_shape=jax.ShapeDtypeStruct((m, D), BF16),
        scratch_shapes=[pltpu.VMEM((WGRAD_ROWS, D), F32)],
        compiler_params=_cparams("parallel", "arbitrary"))(a, b)


def _relayout_w_in(g, tr=256):
    def body(w_ref, o_ref):
        full = jnp.concatenate([w_ref[d] for d in range(N_DEV)], axis=1)
        parts = []
        for a, b, z in _PAD_SEGS:
            parts.append(full[:, a:b])
            if z:
                parts.append(jnp.zeros((tr, z), full.dtype))
        o_ref[...] = jnp.concatenate(parts, axis=1)

    return pl.pallas_call(
        body, name="relayout_w_in", grid=(D // tr,),
        in_specs=[pl.BlockSpec((N_DEV, tr, IN_SHARD), lambda i: (0, i, 0))], out_specs=_row(tr, P),
        out_shape=jax.ShapeDtypeStruct((D, P), g.dtype), compiler_params=_cparams("parallel"))(g)


def _scatter_w_in_grad(gp, tr=256):
    def body(g_ref, o_ref):
        g = g_ref[...]
        pieces, off = [], 0
        for a, b, z in _PAD_SEGS:
            pieces.append((a, g[:, off:off + (b - a)]))
            off += (b - a) + z
        nat = jnp.concatenate([piece for _, piece in sorted(pieces, key=lambda ap: ap[0])], axis=1)
        for d in range(N_DEV):
            o_ref[d] = nat[:, IN_SHARD * d:IN_SHARD * (d + 1)].astype(BF16)

    return pl.pallas_call(
        body, name="scatter_w_in_grad", grid=(D // tr,),
        in_specs=[pl.BlockSpec((tr, P), lambda i: (i, 0))],
        out_specs=pl.BlockSpec((N_DEV, tr, IN_SHARD), lambda i: (0, i, 0)),
        out_shape=jax.ShapeDtypeStruct((N_DEV, D, IN_SHARD), BF16),
        compiler_params=_cparams("parallel"))(gp)


_A_BLK = 3


def _sgu_specs(l):
    return [_resident((DEPTH, G)), _resident((DEPTH, G)), _layer((NH, SGU_C, SGU_C), l), _layer((NH, SGU_C), l)]


def _sgu_fwd(p, ln_w, ln_b, ws, bs, l, tm=256):
    t = p.shape[0]

    def body(uv_ref, lw_ref, lb_ref, ws_ref, bs_ref, y_ref):
        ws_v = [ws_ref[h] for h in range(NH)]
        for c in range(tm // SGU_C):
            rows = pl.ds(c * SGU_C, SGU_C)
            y_ref[rows, :] = _sgu_chunk(uv_ref[rows, :], lw_ref[l:l + 1, :], lb_ref[l:l + 1, :], ws_v,
                                        bs_ref[...]).astype(BF16)

    return pl.pallas_call(
        body, name="sgu_fwd", grid=(t // tm,),
        in_specs=[_row(tm, 2 * G, _A_BLK)] + _sgu_specs(l), out_specs=_row(tm, G),
        out_shape=jax.ShapeDtypeStruct((t, G), BF16),
        compiler_params=_cparams("parallel"))(p, ln_w, ln_b, ws, bs)


def _sgu_bwd(p, dmix, ln_w, ln_b, ws, bs, l, tm=256):
    t = p.shape[0]

    def body(uv_ref, dy_ref, lw_ref, lb_ref, ws_ref, bs_ref, duv_ref, dlw_ref, dlb_ref, dws_ref, dbs_ref):
        @pl.when(pl.program_id(0) == 0)
        def _():
            for r in (dlw_ref, dlb_ref, dws_ref, dbs_ref):
                r[...] = jnp.zeros_like(r)
        ws_v = [ws_ref[h] for h in range(NH)]
        for c in range(tm // SGU_C):
            rows = pl.ds(c * SGU_C, SGU_C)
            _, vjp = jax.vjp(_sgu_chunk, uv_ref[rows, :], lw_ref[l:l + 1, :], lb_ref[l:l + 1, :], ws_v, bs_ref[...])
            duv, dlw, dlb, dws, dbs = vjp(dy_ref[rows, :])
            duv_ref[rows, :] = duv.astype(BF16)
            dlw_ref[...] += dlw
            dlb_ref[...] += dlb
            dbs_ref[...] += dbs
            for h in range(NH):
                dws_ref[h] += dws[h]

    return pl.pallas_call(
        body, name="sgu_bwd", grid=(t // tm,),
        in_specs=[_row(tm, 2 * G, _A_BLK), _row(tm, G, 0)] + _sgu_specs(l),
        out_specs=[_row(tm, 2 * G), _acc((1, G)), _acc((1, G)), _acc((NH, SGU_C, SGU_C)), _acc((NH, SGU_C))],
        out_shape=[jax.ShapeDtypeStruct((t, 2 * G), BF16), jax.ShapeDtypeStruct((1, G), F32),
                   jax.ShapeDtypeStruct((1, G), F32), jax.ShapeDtypeStruct((NH, SGU_C, SGU_C), F32),
                   jax.ShapeDtypeStruct((NH, SGU_C), F32)],
        compiler_params=_cparams("arbitrary"))(p, dmix, ln_w, ln_b, ws, bs)


HALO = 8


def _prev_halo(tm, width, col):
    return pl.BlockSpec((HALO, width), lambda i: (jnp.maximum(i * (tm // HALO) - 1, 0), col))


def _next_halo(tm, width, col, t):
    return pl.BlockSpec((HALO, width), lambda i: (jnp.minimum((i + 1) * (tm // HALO), t // HALO - 1), col))


def _with_prev(halo_ref, x_ref):
    halo = jnp.where(pl.program_id(0) == 0, 0.0, halo_ref[...])
    return jnp.concatenate([halo, x_ref[...]], axis=0)


def _with_next(x_ref, halo_ref):
    halo = jnp.where(pl.program_id(0) == pl.num_programs(0) - 1, 0.0, halo_ref[...])
    return jnp.concatenate([x_ref[...], halo], axis=0)


def _delay(ext, j):
    return ext if j == 0 else pltpu.roll(ext, j, axis=0)


def _advance(ext, j):
    return ext if j == 0 else pltpu.roll(ext, ext.shape[0] - j, axis=0)


def _causal_conv(ext, w, tm):
    kw = w.shape[0]
    out = None
    for k in range(kw):
        term = _delay(ext, kw - 1 - k)[HALO:HALO + tm] * w[k:k + 1, :]
        out = term if out is None else out + term
    return out


def _causal_conv_t(ext, w, tm):
    kw = w.shape[0]
    out = None
    for k in range(kw):
        term = _advance(ext, kw - 1 - k)[0:tm] * w[k:k + 1, :]
        out = term if out is None else out + term
    return out


def _conv_wgrad(ext, dy, kw, tm):
    return jnp.concatenate(
        [jnp.sum(_delay(ext, kw - 1 - k)[HALO:HALO + tm] * dy, axis=0, keepdims=True) for k in range(kw)], axis=0)


_B_GB, _B_GC, _B_H = 8, 9, 10
_C_QKV, _D_QKV = 0, 1
_C_Z, _D_Z = 11, 12
_C_AB, _D_GLR = 26, 27


def _sconv_fwd(p, w, l, tm=512):
    t = p.shape[0]

    def body(gb_ref, gc_ref, h_ref, gc_halo, h_halo, w_ref, y_ref):
        s_ext = _with_prev(gc_halo, gc_ref) * _with_prev(h_halo, h_ref)
        y_ref[...] = (gb_ref[...] * _causal_conv(s_ext, w_ref[...], tm)).astype(BF16)

    return pl.pallas_call(
        body, name="sconv_fwd", grid=(t // tm,),
        in_specs=[_row(tm, G, _B_GB), _row(tm, G, _B_GC), _row(tm, G, _B_H), _prev_halo(tm, G, _B_GC),
                  _prev_halo(tm, G, _B_H), _layer((3, G), l)],
        out_specs=_row(tm, G), out_shape=jax.ShapeDtypeStruct((t, G), BF16),
        compiler_params=_cparams("parallel"))(p, p, p, p, p, w)


def _sconv_bwd(p, dmix, w, l, tm=512):
    t = p.shape[0]

    def body(gb_ref, gc_ref, h_ref, gc_halo, h_halo, gb_next, dy_ref, dy_next, w_ref, dp_ref, dw_ref):
        @pl.when(pl.program_id(0) == 0)
        def _():
            dw_ref[...] = jnp.zeros_like(dw_ref)
        w_v = w_ref[...]
        s_ext = _with_prev(gc_halo, gc_ref) * _with_prev(h_halo, h_ref)
        dout = dy_ref[...]
        d_gb = dout * _causal_conv(s_ext, w_v, tm)
        dconv_ext = _with_next(dy_ref, dy_next) * _with_next(gb_ref, gb_next)
        ds = _causal_conv_t(dconv_ext, w_v, tm)
        dw_ref[...] += _conv_wgrad(s_ext, dconv_ext[0:tm], 3, tm)
        dp_ref[...] = jnp.concatenate([d_gb, ds * h_ref[...], ds * gc_ref[...]], axis=1).astype(BF16)

    return pl.pallas_call(
        body, name="sconv_bwd", grid=(t // tm,),
        in_specs=[_row(tm, G, _B_GB), _row(tm, G, _B_GC), _row(tm, G, _B_H), _prev_halo(tm, G, _B_GC),
                  _prev_halo(tm, G, _B_H), _next_halo(tm, G, _B_GB, t), _row(tm, G, 1), _next_halo(tm, G, 1, t),
                  _layer((3, G), l)],
        out_specs=[_row(tm, 3 * G), _acc((3, G))],
        out_shape=[jax.ShapeDtypeStruct((t, 3 * G), BF16), jax.ShapeDtypeStruct((3, G), F32)],
        compiler_params=_cparams("arbitrary"))(p, p, p, p, p, p, dmix, dmix, w)


def _qkv_conv_fwd(p, w, l, tm=512):
    t = p.shape[0]

    def body(x_ref, x_halo, w_ref, y_ref):
        c = _causal_conv(_with_prev(x_halo, x_ref), w_ref[...], tm)
        y_ref[...] = c * _sigmoid(c)

    return pl.pallas_call(
        body, name="qkv_conv_fwd", grid=(t // tm,),
        in_specs=[_row(tm, 3 * G, _C_QKV), _prev_halo(tm, 3 * G, _C_QKV), _layer((4, 3 * G), l)],
        out_specs=_row(tm, 3 * G), out_shape=jax.ShapeDtypeStruct((t, 3 * G), F32),
        compiler_params=_cparams("parallel"))(p, p, w)


def _qkv_conv_bwd(p, dy, w, l, tm=512):
    t = p.shape[0]

    def body(x_ref, x_halo, x_next, dy_ref, dy_next, w_ref, dx_ref, dw_ref):
        @pl.when(pl.program_id(0) == 0)
        def _():
            dw_ref[...] = jnp.zeros_like(dw_ref)
        w_v = w_ref[...]
        x_all = jnp.concatenate([_with_prev(x_halo, x_ref), jnp.where(
            pl.program_id(0) == pl.num_programs(0) - 1, 0.0, x_next[...])], axis=0)
        c = _causal_conv(x_all, w_v, tm + HALO)
        s = _sigmoid(c)
        dc_ext = _with_next(dy_ref, dy_next) * (s * (1.0 + c * (1.0 - s)))
        dx_ref[...] = _causal_conv_t(dc_ext, w_v, tm).astype(BF16)
        dw_ref[...] += _conv_wgrad(x_all[0:HALO + tm], dc_ext[0:tm], 4, tm)

    return pl.pallas_call(
        body, name="qkv_conv_bwd", grid=(t // tm,),
        in_specs=[_row(tm, 3 * G, _C_QKV), _prev_halo(tm, 3 * G, _C_QKV), _next_halo(tm, 3 * G, _C_QKV, t),
                  _row(tm, 3 * G), _next_halo(tm, 3 * G, 0, t), _layer((4, 3 * G), l)],
        out_specs=[_row(tm, 3 * G), _acc((4, 3 * G))],
        out_shape=[jax.ShapeDtypeStruct((t, 3 * G), BF16), jax.ShapeDtypeStruct((4, 3 * G), F32)],
        compiler_params=_cparams("arbitrary"))(p, p, p, dy, dy, w)


_STATE = pl.BlockSpec((1, NH, HD, HD), lambda i: (i, 0, 0, 0))


def _dn_specs():
    return [_resident((DEPTH, NH)), _resident((DEPTH, NH)), _resident((DEPTH, HD))]


def _dn_fwd(qkv, p, params, l, cps=4, rider=None, forward_at=1.0):
    t = qkv.shape[0]
    tm = DN_C * cps
    nb = cps * NH

    def body(*refs):
        (qkv_ref, z_ref, ab_ref, alog_ref, dt_ref, nw_ref, y_ref, st_ref, inv_ref, state), ride = _split_refs(
            refs, 6, 3, 1, rider)
        _ride_begin(rider, ride, (t // tm,))

        @pl.when(pl.program_id(0) == 0)
        def _():
            state[...] = jnp.zeros_like(state)
        st_ref[0] = state[...]
        y, new, inv = _dn_tile(qkv_ref[:, 0:G], qkv_ref[:, G:2 * G], qkv_ref[:, 2 * G:3 * G], ab_ref[...], z_ref[...],
                               state[...], None, alog_ref[l:l + 1, :], dt_ref[l:l + 1, :], nw_ref[l:l + 1, :])
        y_ref[...] = y.astype(BF16)
        inv_ref[...] = inv
        state[...] = new
        _ride_end(rider, ride, (t // tm,), forward_at)

    specs, operands = _host(
        rider, [_row(tm, 3 * G), _row(tm, G, _C_Z), _row(tm, LANES, _C_AB)] + _dn_specs(),
        [_row(tm, G), _STATE, pl.BlockSpec((nb, DN_C, DN_C), lambda i: (i, 0, 0))],
        [jax.ShapeDtypeStruct((t, G), BF16), jax.ShapeDtypeStruct((t // tm, NH, HD, HD), F32),
         jax.ShapeDtypeStruct((t // DN_C * NH, DN_C, DN_C), F32)],
        [pltpu.VMEM((NH, HD, HD), F32)], [qkv, p, p, *params])
    return pl.pallas_call(body, name="dn_fwd", grid=(t // tm,), compiler_params=_cparams("arbitrary"), **specs)(*operands)


def _dn_bwd(qkv, p, states, inv, dmix, params, l, cps=4, rider=None):
    t = qkv.shape[0]
    tm = DN_C * cps
    n = t // tm
    nb = cps * NH

    def body(*refs):
        (qkv_ref, z_ref, ab_ref, st_ref, inv_ref, dy_ref, alog_ref, dt_ref, nw_ref,
         dqkv_ref, dz_ref, dab_ref, dalog_ref, ddt_ref, dnw_ref, dstate), ride = _split_refs(refs, 9, 6, 1, rider)
        _ride_begin(rider, ride, (n,))
        dprm_refs = (dalog_ref, ddt_ref, dnw_ref)

        @pl.when(pl.program_id(0) == 0)
        def _():
            dstate[...] = jnp.zeros_like(dstate)
            for r in dprm_refs:
                r[...] = jnp.zeros_like(r)
        inv_v = inv_ref[...]
        tile = lambda q, k, v, ab, z, st, alog, dt, nw: _dn_tile(q, k, v, ab, z, st, inv_v, alog, dt, nw)[:2]
        _, vjp = jax.vjp(tile, qkv_ref[:, 0:G], qkv_ref[:, G:2 * G], qkv_ref[:, 2 * G:3 * G], ab_ref[...], z_ref[...],
                         st_ref[0], alog_ref[l:l + 1, :], dt_ref[l:l + 1, :], nw_ref[l:l + 1, :])
        dq, dk, dv, dab, dz, dst, *dprm = vjp((dy_ref[...], dstate[...]))
        dqkv_ref[...] = jnp.concatenate([dq, dk, dv], axis=1)
        dz_ref[...] = dz.astype(BF16)
        dab_ref[...] = dab.astype(BF16)
        dstate[...] = dst
        for r, g in zip(dprm_refs, dprm):
            r[...] += g
        _ride_end(rider, ride, (n,))

    rev = lambda w, blk: pl.BlockSpec((tm, w), lambda i: (n - 1 - i, blk))
    specs, operands = _host(
        rider, [rev(3 * G, 0), rev(G, _C_Z), rev(LANES, _C_AB),
                pl.BlockSpec((1, NH, HD, HD), lambda i: (n - 1 - i, 0, 0, 0)),
                pl.BlockSpec((nb, DN_C, DN_C), lambda i: (n - 1 - i, 0, 0)), rev(G, 2)] + _dn_specs(),
        [rev(3 * G, 0), rev(G, 0), rev(LANES, 0), _acc((1, NH)), _acc((1, NH)), _acc((1, HD))],
        [jax.ShapeDtypeStruct((t, 3 * G), F32), jax.ShapeDtypeStruct((t, G), BF16),
         jax.ShapeDtypeStruct((t, LANES), BF16), jax.ShapeDtypeStruct((1, NH), F32),
         jax.ShapeDtypeStruct((1, NH), F32), jax.ShapeDtypeStruct((1, HD), F32)],
        [pltpu.VMEM((NH, HD, HD), F32)], [qkv, p, p, states, inv, dmix, *params])
    return pl.pallas_call(body, name="dn_bwd", grid=(n,), compiler_params=_cparams("arbitrary"), **specs)(*operands)


def _gla_specs(l):
    return [_layer((16, G), l), _resident((DEPTH, G)), _resident((DEPTH, HD))]


def _gla_fwd(p, params, l, cps=4, rider=None):
    t = p.shape[0]
    tm = GLA_C * cps

    def body(*refs):
        (qkv_ref, z_ref, glr_ref, w2_ref, gb_ref, nw_ref, y_ref, st_ref, state), ride = _split_refs(refs, 6, 2, 1, rider)
        _ride_begin(rider, ride, (t // tm,))

        @pl.when(pl.program_id(0) == 0)
        def _():
            state[...] = jnp.zeros_like(state)
        st_ref[0] = state[...]
        y, new = _gla_tile(qkv_ref[:, 0:G], qkv_ref[:, G:2 * G], qkv_ref[:, 2 * G:3 * G], glr_ref[...], z_ref[...],
                           state[...], w2_ref[...], gb_ref[l:l + 1, :], nw_ref[l:l + 1, :])
        y_ref[...] = y.astype(BF16)
        state[...] = new
        _ride_end(rider, ride, (t // tm,))

    specs, operands = _host(
        rider, [_row(tm, 3 * G, _D_QKV), _row(tm, G, _D_Z), _row(tm, LANES, _D_GLR)] + _gla_specs(l),
        [_row(tm, G), _STATE],
        [jax.ShapeDtypeStruct((t, G), BF16), jax.ShapeDtypeStruct((t // tm, NH, HD, HD), F32)],
        [pltpu.VMEM((NH, HD, HD), F32)], [p, p, p, *params])
    return pl.pallas_call(body, name="gla_fwd", grid=(t // tm,), compiler_params=_cparams("arbitrary"), **specs)(*operands)


def _gla_bwd(p, states, dmix, params, l, cps=4):
    t = p.shape[0]
    tm = GLA_C * cps
    n = t // tm

    def body(qkv_ref, z_ref, glr_ref, st_ref, dy_ref, w2_ref, gb_ref, nw_ref,
             dqkv_ref, dz_ref, dglr_ref, dw2_ref, dgb_ref, dnw_ref, dstate):
        dprm_refs = (dw2_ref, dgb_ref, dnw_ref)

        @pl.when(pl.program_id(0) == 0)
        def _():
            dstate[...] = jnp.zeros_like(dstate)
            for r in dprm_refs:
                r[...] = jnp.zeros_like(r)
        _, vjp = jax.vjp(_gla_tile, qkv_ref[:, 0:G], qkv_ref[:, G:2 * G], qkv_ref[:, 2 * G:3 * G], glr_ref[...],
                         z_ref[...], st_ref[0], w2_ref[...], gb_ref[l:l + 1, :], nw_ref[l:l + 1, :])
        dq, dk, dv, dglr, dz, dst, *dprm = vjp((dy_ref[...], dstate[...]))
        dqkv_ref[...] = jnp.concatenate([dq, dk, dv], axis=1).astype(BF16)
        dz_ref[...] = dz.astype(BF16)
        dglr_ref[...] = dglr.astype(BF16)
        dstate[...] = dst
        for r, g in zip(dprm_refs, dprm):
            r[...] += g

    rev = lambda w, blk: pl.BlockSpec((tm, w), lambda i: (n - 1 - i, blk))
    return pl.pallas_call(
        body, name="gla_bwd", grid=(n,),
        in_specs=[rev(3 * G, _D_QKV), rev(G, _D_Z), rev(LANES, _D_GLR),
                  pl.BlockSpec((1, NH, HD, HD), lambda i: (n - 1 - i, 0, 0, 0)), rev(G, 3)] + _gla_specs(l),
        out_specs=[rev(3 * G, 0), rev(G, 0), rev(LANES, 0), _acc((16, G)), _acc((1, G)), _acc((1, HD))],
        out_shape=[jax.ShapeDtypeStruct((t, 3 * G), BF16), jax.ShapeDtypeStruct((t, G), BF16),
                   jax.ShapeDtypeStruct((t, LANES), BF16), jax.ShapeDtypeStruct((16, G), F32),
                   jax.ShapeDtypeStruct((1, G), F32), jax.ShapeDtypeStruct((1, HD), F32)],
        scratch_shapes=[pltpu.VMEM((NH, HD, HD), F32)],
        compiler_params=_cparams("arbitrary"))(p, p, p, states, dmix, *params)


def _adamw_math(w, g, m, v):
    m = ADAM_B1 * m + (1.0 - ADAM_B1) * g
    v = ADAM_B2 * v + (1.0 - ADAM_B2) * (g * g)
    m_hat = m / (1.0 - ADAM_B1 ** ADAM_STEP)
    v_hat = v / (1.0 - ADAM_B2 ** ADAM_STEP)
    return -ADAM_LR * (m_hat / (jnp.sqrt(v_hat) + ADAM_EPS) + ADAM_WD * w), m, v


def _adamw_large(parts, w, m, v, name, tr):
    _, r, c = w.shape

    def body(p0_ref, p1_ref, w_ref, m_ref, v_ref, g_ref, d_ref, nm_ref, nv_ref):
        def total(p_ref):
            g = p_ref[0].astype(F32)
            for k in range(1, N_DEV):
                g = g + p_ref[k].astype(F32)
            return g

        g = jnp.where(pl.program_id(0) == 0, total(p0_ref), total(p1_ref))
        g_ref[...] = g
        d_ref[...], nm_ref[...], nv_ref[...] = _adamw_math(w_ref[...], g, m_ref[...], v_ref[...])

    blk = pl.BlockSpec((None, tr, c), lambda l, i: (l, i, 0))
    part = pl.BlockSpec((N_DEV, tr, c), lambda l, i: (0, i, 0))
    return pl.pallas_call(
        body, name=name, grid=(DEPTH, r // tr), in_specs=[part, part, blk, blk, blk],
        out_specs=[blk] * 4, out_shape=[jax.ShapeDtypeStruct(w.shape, F32)] * 4,
        compiler_params=_cparams("parallel", "parallel"))(*parts, w, m, v)


_SLAB_AT = {
    "sgu_w_spatial": (0, 0, SGU_C, LANES),
    "sgu_b_spatial": (128, 0, NH, SGU_C),
    "norm1_w": (132, 0, 1, D),
    "norm2_w": (133, 0, 1, D),
    "sgu_ln_w": (134, 0, 1, G),
    "sgu_ln_b": (134, 256, 1, G),
    "gla_gate_bias": (134, 512, 1, G),
    "dn_norm_w": (134, 768, 1, HD),
    "gla_norm_w": (134, 896, 1, HD),
    "dn_a_log": (135, 0, 1, NH),
    "dn_dt_bias": (135, 128, 1, NH),
    "gla_w_gate2": (136, 0, 16, G),
    "dn_conv_w": (136, 256, 4, 3 * G),
    "sc_conv_w": (140, 256, 3, G),
}
_LAYER_ROWS = 152
_FINAL_ROW = DEPTH * _LAYER_ROWS
_SLAB_ROWS, _SLAB_COLS = _FINAL_ROW + 8, 1024
_SLAB_ORDER = tuple(_SLAB_AT)
_SHARDED_SMALL = ("sc_conv_w", "dn_conv_w", "gla_w_gate2")
_REPLICATED = tuple(k for k in _SLAB_ORDER if k not in _SHARDED_SMALL)


def _region(name, l, h=0):
    r0, c0, nr, nc = _SLAB_AT[name]
    return slice(_LAYER_ROWS * l + r0, _LAYER_ROWS * l + r0 + nr), slice(c0 + nc * h, c0 + nc * (h + 1))


def _pack_small(layer_grads, d_final):
    def body(*refs):
        slab = refs[-1]
        slab[...] = jnp.zeros_like(slab)
        it = iter(refs[:-1])
        for l in range(DEPTH):
            for name in _SLAB_ORDER:
                ref = next(it)
                if name == "sgu_w_spatial":
                    for h in range(NH):
                        slab[_region(name, l, h)] = ref[h]
                else:
                    slab[_region(name, l)] = ref[...]
        slab[_FINAL_ROW:_FINAL_ROW + 1, :] = next(it)[...]

    flat = [layer_grads[l][name] for l in range(DEPTH) for name in _SLAB_ORDER] + [d_final]
    return pl.pallas_call(body, name="pack_small_grads", out_shape=jax.ShapeDtypeStruct((_SLAB_ROWS, _SLAB_COLS), F32),
                          compiler_params=_cparams())(*flat)


def _adamw_small(parts, w, m, v):
    names = _REPLICATED + ("final_norm_w",)
    n_in = len(names)

    def body(*refs):
        def total(rows, cols):
            g = refs[0][0, rows, cols]
            for k in range(1, N_DEV):
                g = g + refs[0][k, rows, cols]
            return g

        w_refs = dict(zip(names, refs[1:1 + n_in]))
        m_refs = dict(zip(names, refs[1 + n_in:1 + 2 * n_in]))
        v_refs = dict(zip(names, refs[1 + 2 * n_in:1 + 3 * n_in]))
        outs = refs[1 + 3 * n_in:]
        out_refs = [dict(zip(names, outs[j * n_in:(j + 1) * n_in])) for j in range(4)]
        full_refs = dict(zip(_SHARDED_SMALL, outs[4 * n_in:]))

        def update(name, idx, g):
            res = (g,) + _adamw_math(w_refs[name][idx], g, m_refs[name][idx], v_refs[name][idx])
            for o, val in zip(out_refs, res):
                o[name][idx] = val

        for l in range(DEPTH):
            for name in _REPLICATED:
                if name == "sgu_w_spatial":
                    for h in range(NH):
                        update(name, (l, h), total(*_region(name, l, h)))
                elif name == "sgu_b_spatial":
                    update(name, (l,), total(*_region(name, l)))
                else:
                    update(name, (slice(l, l + 1), slice(None)), total(*_region(name, l)))
            for name in _SHARDED_SMALL:
                full_refs[name][l] = total(*_region(name, l))
        update("final_norm_w", (slice(None), slice(None)), total(slice(_FINAL_ROW, _FINAL_ROW + 1), slice(None)))

    shapes = [jax.ShapeDtypeStruct(w[k].shape, F32) for k in names]
    full_shapes = [jax.ShapeDtypeStruct((DEPTH,) + _SLAB_AT[k][2:], F32) for k in _SHARDED_SMALL]
    outs = pl.pallas_call(body, name="adamw_small", out_shape=shapes * 4 + full_shapes, compiler_params=_cparams())(
        parts, *[w[k] for k in names], *[m[k] for k in names], *[v[k] for k in names])
    result = [dict(zip(names, outs[j * n_in:(j + 1) * n_in])) for j in range(4)]
    return result, dict(zip(_SHARDED_SMALL, outs[4 * n_in:]))


def _adamw_shards(g, w, m, v):
    names = _SHARDED_SMALL
    n = len(names)

    def body(*refs):
        for j in range(n):
            g_v = refs[j][...]
            res = _adamw_math(refs[n + j][...], g_v, refs[2 * n + j][...], refs[3 * n + j][...])
            for o, val in zip(refs[4 * n + j::n], res):
                o[...] = val

    shapes = [jax.ShapeDtypeStruct(w[k].shape, F32) for k in names]
    outs = pl.pallas_call(body, name="adamw_small_shards", out_shape=shapes * 3, compiler_params=_cparams())(
        *[g[k] for k in names], *[w[k] for k in names], *[m[k] for k in names], *[v[k] for k in names])
    return [dict(zip(names, outs[j * n:(j + 1) * n])) for j in range(3)]


_ANY = pl.BlockSpec(memory_space=pl.ANY)


def _place():
    return lax.axis_index("x"), lax.axis_index("y"), lax.axis_index("c")


def _sems(n):
    return [pltpu.SemaphoreType.DMA((n, 7)), pltpu.SemaphoreType.DMA((n, 7)), pltpu.SemaphoreType.DMA((n,))]


class _Gather:
    def __init__(self, blocks, second):
        self.inputs, self.second = list(blocks), list(second)
        self.out_shape = [jax.ShapeDtypeStruct((a.shape[0], N_DEV) + a.shape[1:] if s else (N_DEV,) + a.shape, a.dtype)
                          for a, s in zip(blocks, second)]
        self.scratch = _sems(len(blocks))

    def _copies(self, refs):
        x_refs, out_refs, (send_sems, recv_sems, local_sems) = refs
        n = len(x_refs)
        x, y, c = _place()
        me, sibling = (x, y, c), (x, y, 1 - c)
        chips = [(1 - x, y), (x, 1 - y), (1 - x, 1 - y)]

        def slot(j, px, py, pc):
            idx = 4 * px + 2 * py + pc
            return out_refs[j].at[:, idx] if self.second[j] else out_refs[j].at[idx]

        def copies(k, block, to, own=False):
            return [pltpu.make_async_remote_copy(
                src_ref=x_refs[j] if own else slot(j, *block), dst_ref=slot(j, *block),
                send_sem=send_sems.at[j, k], recv_sem=recv_sems.at[j, k], device_id=to,
                device_id_type=pl.DeviceIdType.MESH) for j in range(n)]

        mine = [pltpu.make_async_copy(x_refs[j], slot(j, *me), local_sems.at[j]) for j in range(n)]
        first = copies(0, me, sibling, own=True)
        for j, chip in enumerate(chips):
            first += copies(1 + j, me, (*chip, c), own=True)
        landed = [copies(1 + j, (*chip, c), me) for j, chip in enumerate(chips)]
        onward = [copies(4 + j, (*chip, c), sibling) for j, chip in enumerate(chips)]
        from_sibling = copies(0, sibling, me)
        for j, chip in enumerate(chips):
            from_sibling += copies(4 + j, (*chip, 1 - c), me)
        return mine, first, landed, onward, from_sibling

    def start(self, refs):
        mine, first, _, _, _ = self._copies(refs)
        for cp in mine + first:
            cp.start()

    def forward(self, refs):
        _, _, landed, onward, _ = self._copies(refs)
        for arrived, passed in zip(landed, onward):
            for cp in arrived:
                cp.wait_recv()
            for cp in passed:
                cp.start()

    def finish(self, refs):
        mine, first, _, onward, from_sibling = self._copies(refs)
        for cp in from_sibling:
            cp.wait_recv()
        for cp in first + [cp for passed in onward for cp in passed]:
            cp.wait_send()
        for cp in mine:
            cp.wait()


class _Exchange:
    def __init__(self, sends):
        self.inputs = list(sends)
        self.out_shape = [jax.ShapeDtypeStruct(a.shape, a.dtype) for a in sends]
        self.scratch = _sems(len(sends))

    def _copies(self, refs):
        x_refs, out_refs, (send_sems, recv_sems, local_sems) = refs
        n = len(x_refs)
        x, y, c = _place()
        me = 4 * x + 2 * y + c
        sent, landing = [], []
        for k in range(1, N_DEV):
            px, py, pc = x ^ ((k >> 2) & 1), y ^ ((k >> 1) & 1), c ^ (k & 1)
            them = 4 * px + 2 * py + pc
            for j in range(n):
                for here, there, into in ((them, me, sent), (me, them, landing)):
                    into.append(pltpu.make_async_remote_copy(
                        src_ref=x_refs[j].at[here], dst_ref=out_refs[j].at[there], send_sem=send_sems.at[j, k - 1],
                        recv_sem=recv_sems.at[j, k - 1], device_id=(px, py, pc), device_id_type=pl.DeviceIdType.MESH))
        mine = [pltpu.make_async_copy(x_refs[j].at[me], out_refs[j].at[me], local_sems.at[j]) for j in range(n)]
        return mine, sent, landing

    def start(self, refs):
        mine, sent, _ = self._copies(refs)
        for cp in mine + sent:
            cp.start()

    def forward(self, refs):
        pass

    def finish(self, refs):
        mine, sent, landing = self._copies(refs)
        for cp in landing:
            cp.wait_recv()
        for cp in sent:
            cp.wait_send()
        for cp in mine:
            cp.wait()


def _run(rider, name):
    n_in, n_out = len(rider.inputs), len(rider.out_shape)

    def body(*refs):
        parts = (refs[:n_in], refs[n_in:n_in + n_out], refs[n_in + n_out:])
        rider.start(parts)
        rider.forward(parts)
        rider.finish(parts)

    return pl.pallas_call(body, name=name, out_shape=rider.out_shape, in_specs=[_ANY] * n_in, out_specs=[_ANY] * n_out,
                          scratch_shapes=rider.scratch)(*rider.inputs)


def _split_refs(refs, n_in, n_out, n_scratch, rider):
    r_in = len(rider.inputs) if rider else 0
    r_out = len(rider.out_shape) if rider else 0
    a = n_in + r_in
    b = a + n_out + r_out
    own = refs[:n_in] + refs[a:a + n_out] + refs[b:b + n_scratch]
    return own, (refs[n_in:a], refs[a + n_out:b], refs[b + n_scratch:])


def _grid_step(grid):
    step, stride = 0, 1
    for axis in reversed(range(len(grid))):
        step = step + pl.program_id(axis) * stride
        stride *= grid[axis]
    return step


def _ride_begin(rider, ride_refs, grid):
    if rider is not None:
        pl.when(_grid_step(grid) == 0)(lambda: rider.start(ride_refs))


def _ride_end(rider, ride_refs, grid, forward_at=1.0):
    if rider is not None:
        steps = 1
        for n in grid:
            steps *= n
        step = _grid_step(grid)
        pl.when(step == min(steps - 1, int(steps * forward_at)))(lambda: rider.forward(ride_refs))
        pl.when(step == steps - 1)(lambda: rider.finish(ride_refs))


def _host(rider, in_specs, out_specs, out_shape, scratch, operands):
    if rider is None:
        return dict(in_specs=in_specs, out_specs=out_specs, out_shape=out_shape, scratch_shapes=scratch), operands
    return dict(in_specs=in_specs + [_ANY] * len(rider.inputs), out_specs=out_specs + [_ANY] * len(rider.out_shape),
                out_shape=out_shape + rider.out_shape, scratch_shapes=scratch + rider.scratch), operands + rider.inputs


_LATE = ("w_gate_up", "w_out", "w_down")


def _local_grads(x, target, w, late=None, exchange=False):
    w = dict(w)
    w_in = list(w["w_in"])
    dn_params = (w["dn_a_log"], w["dn_dt_bias"], w["dn_norm_w"])
    gla_params = (w["gla_w_gate2"], w["gla_gate_bias"], w["gla_norm_w"])
    sgu_params = (w["sgu_ln_w"], w["sgu_ln_b"], w["sgu_w_spatial"], w["sgu_b_spatial"])
    saved = []
    for l in range(DEPTH):
        riding = l == 0 and late is not None
        h, p, *got = _in_proj(x, w["norm1_w"], w_in[l], l, rider=_Gather([late["w_down"]], [True]) if riding else None)
        if riding:
            w["w_down"] = got[0].reshape(DEPTH, FF, D)
        ya = _sgu_fwd(p, *sgu_params, l)
        yb = _sconv_fwd(p, w["sc_conv_w"], l)
        qkv_c = _qkv_conv_fwd(p, w["dn_conv_w"], l)
        yc, st_c, inv_c, *got = _dn_fwd(
            qkv_c, p, dn_params, l, forward_at=0.75,
            rider=_Gather([late["w_gate_up"], late["w_out"]], [False, True]) if riding else None)
        if riding:
            w.update(w_gate_up=got[0], w_out=got[1].reshape(DEPTH, D, D))
        yd, st_d, *got = _gla_fwd(p, gla_params, l, rider=_Gather([late["w_in"]], [False]) if riding else None)
        if riding:
            w_in[1] = _relayout_w_in(got[0])
        mix, x1, h2, gu, act, x2 = _out_mlp(x, (ya, yb, yc, yd), w["w_out"], w["norm2_w"], w["w_gate_up"], w["w_down"], l)
        saved.append(dict(x=x, h=h, p=p, qkv_c=qkv_c, st_c=st_c, inv_c=inv_c, st_d=st_d, mix=mix, x1=x1, h2=h2, gu=gu,
                          act=act))
        x = x2
    loss, d_final, dx = _loss_head(x, w["final_norm_w"], target)
    large = {k: [None] * DEPTH for k in ("w_in", "w_out", "w_gate_up", "w_down")}
    small = [None] * DEPTH
    for l in reversed(range(DEPTH)):
        s = saved[l]
        p = s["p"]
        g = {}
        riding = [(k, 1) for k in _LATE] if exchange and l == 0 else []
        dgu, dx1, dmix, g["norm2_w"], *arrived = _mlp_out_bwd(
            dx, s["x1"], s["gu"], w["norm2_w"], w["w_down"], w["w_gate_up"], w["w_out"], l,
            rider=_Exchange([large[k][j] for k, j in riding]) if riding else None)
        for (k, j), a in zip(riding, arrived):
            large[k][j] = a
        large["w_gate_up"][l] = _wgrad_rows(dgu, s["h2"], "wgrad_gate_up").reshape(N_DEV, GU_SHARD, D)
        large["w_down"][l] = _wgrad_rows(s["act"], dx, "wgrad_down").reshape(N_DEV, FF // N_DEV, D)
        large["w_out"][l] = _matmul_tn(s["mix"], dx1, "wgrad_out", BF16).reshape(N_DEV, D // N_DEV, D)
        d_a, g["sgu_ln_w"], g["sgu_ln_b"], g["sgu_w_spatial"], g["sgu_b_spatial"] = _sgu_bwd(p, dmix, *sgu_params, l)
        d_b, g["sc_conv_w"] = _sconv_bwd(p, dmix, w["sc_conv_w"], l)
        riding = [("w_in", 1)] + [(k, 0) for k in _LATE] if exchange and l == 0 else []
        dqkv_c, dz_c, dab, g["dn_a_log"], g["dn_dt_bias"], g["dn_norm_w"], *arrived = _dn_bwd(
            s["qkv_c"], p, s["st_c"], s["inv_c"], dmix, dn_params, l,
            rider=_Exchange([large[k][j] for k, j in riding]) if riding else None)
        for (k, j), a in zip(riding, arrived):
            large[k][j] = a
        d_c, g["dn_conv_w"] = _qkv_conv_bwd(p, dqkv_c, w["dn_conv_w"], l)
        d_d, dz_d, dglr, g["gla_w_gate2"], g["gla_gate_bias"], g["gla_norm_w"] = _gla_bwd(p, s["st_d"], dmix, gla_params, l)
        dp, dx, g["norm1_w"] = _in_proj_bwd((d_c, d_d, d_a, d_b, dz_c, dz_d, dab, dglr), s["x"], dx1, w["norm1_w"],
                                            w_in[l], l)
        small[l] = g
        if exchange and l == 0:
            d_w_in, small = _matmul_tn(s["h"], dp, "wgrad_in", F32, rider=_Gather([_pack_small(small, d_final)], [False]))
        else:
            d_w_in = _matmul_tn(s["h"], dp, "wgrad_in", F32)
        large["w_in"][l] = _scatter_w_in_grad(d_w_in)
    return loss[0, 0], dx, large, small, d_final


_ORDER = ("norm1_w", "w_in", "sgu_ln_w", "sgu_ln_b", "sgu_w_spatial", "sgu_b_spatial", "sc_conv_w", "dn_conv_w",
          "dn_a_log", "dn_dt_bias", "dn_norm_w", "gla_w_gate2", "gla_gate_bias", "gla_norm_w", "w_out", "norm2_w",
          "w_gate_up", "w_down", "final_norm_w")
_LARGE = ("w_in", "w_gate_up", "w_out", "w_down")
_LARGE_TILE = {"w_in": 256, "w_gate_up": 352, "w_out": 128, "w_down": 352}


def _gather_cols(g):
    return jnp.transpose(g, (1, 2, 0, 3)).reshape(g.shape[1], g.shape[2], N_DEV * g.shape[3])


def kernel(x, norm1_w, w_in, sgu_ln_w, sgu_ln_b, sgu_w_spatial, sgu_b_spatial, sc_conv_w, dn_conv_w, dn_a_log, dn_dt_bias, dn_norm_w, gla_w_gate2, gla_gate_bias, gla_norm_w, w_out, norm2_w, w_gate_up, w_down, final_norm_w, loss_target, m_norm1_w, m_w_in, m_sgu_ln_w, m_sgu_ln_b, m_sgu_w_spatial, m_sgu_b_spatial, m_sc_conv_w, m_dn_conv_w, m_dn_a_log, m_dn_dt_bias, m_dn_norm_w, m_gla_w_gate2, m_gla_gate_bias, m_gla_norm_w, m_w_out, m_norm2_w, m_w_gate_up, m_w_down, m_final_norm_w, v_norm1_w, v_w_in, v_sgu_ln_w, v_sgu_ln_b, v_sgu_w_spatial, v_sgu_b_spatial, v_sc_conv_w, v_dn_conv_w, v_dn_a_log, v_dn_dt_bias, v_dn_norm_w, v_gla_w_gate2, v_gla_gate_bias, v_gla_norm_w, v_w_out, v_norm2_w, v_w_gate_up, v_w_down, v_final_norm_w):
    args = dict(locals())
    wts = {k: args[k] for k in _ORDER}
    mom = {k: args["m_" + k] for k in _ORDER}
    var = {k: args["v_" + k] for k in _ORDER}
    for d in (wts, mom, var):
        d["final_norm_w"] = d["final_norm_w"][None]
    me = 4 * lax.axis_index("x") + 2 * lax.axis_index("y") + lax.axis_index("c")
    for d in (wts, mom, var):
        d["w_gate_up"] = jnp.swapaxes(d["w_gate_up"], 1, 2)

    late = {k: wts[k].astype(BF16) for k in _LATE}
    w_in = wts["w_in"].astype(BF16)
    late["w_in"] = w_in[1]
    got = _run(_Gather([w_in[0]] + [wts[k] for k in _SHARDED_SMALL], [False] * 4), "gather_w_in")
    full = dict(wts)
    full["w_in"] = [_relayout_w_in(got[0]), None]
    for k, g in zip(_SHARDED_SMALL, got[1:]):
        full[k] = _gather_cols(g)

    loss, dx, large, small, d_final = _local_grads(x[0], loss_target[0], full, late=late, exchange=True)
    loss = lax.psum(loss, ("x", "y", "c"))

    large["w_in"][0], = _run(_Exchange([large["w_in"][0]]), "exchange_grads")
    result = {}
    for k in _LARGE:
        outs = _adamw_large(large[k], wts[k], mom[k], var[k], "adamw_" + k, _LARGE_TILE[k])
        for kind, a in zip(("grad", "delta", "new_m", "new_v"), outs):
            result[kind, k] = jnp.swapaxes(a, 1, 2) if k == "w_gate_up" else a

    slabs = small
    rep = _REPLICATED + ("final_norm_w",)
    outs, summed = _adamw_small(slabs, {k: wts[k] for k in rep}, {k: mom[k] for k in rep}, {k: var[k] for k in rep})
    for kind, d in zip(("grad", "delta", "new_m", "new_v"), outs):
        for k, a in d.items():
            result[kind, k] = a[0] if k == "final_norm_w" else a
    own = {k: lax.dynamic_slice_in_dim(summed[k], me * wts[k].shape[-1], wts[k].shape[-1], axis=2) for k in _SHARDED_SMALL}
    outs = _adamw_shards(own, {k: wts[k] for k in _SHARDED_SMALL}, {k: mom[k] for k in _SHARDED_SMALL},
                         {k: var[k] for k in _SHARDED_SMALL})
    for kind, d in zip(("grad", "delta", "new_m", "new_v"), [own] + outs):
        for k, a in d.items():
            result[kind, k] = a

    return (loss, dx[None], *[result[kind, k] for kind in ("grad", "delta", "new_m", "new_v") for k in _ORDER])
```

```python
import functools

import jax
import jax.numpy as jnp
from jax import lax
from jax.experimental import pallas as pl
from jax.experimental.pallas import tpu as pltpu

F32, BF16 = jnp.float32, jnp.bfloat16
DEPTH = 2
D = 1024
G = 256
NH, HD = 4, 64
FF = 2816
IN_COLS = 3352
P = 3584
EPS = 1e-6
SGU_C, DN_C, GLA_C = 128, 64, 64
N_DEV = 8
IN_SHARD = IN_COLS // N_DEV
GU_SHARD = 2 * FF // N_DEV
N_GATE = N_DEV // 2
LANES = 128
VMEM_LIMIT = 56 * 2 ** 20

_PAD_SEGS = ((1280, 2048, 0),
             (2312, 3080, 0),
             (0, 512, 0),
             (512, 1280, 0),
             (2056, 2312, 0),
             (3096, 3352, 0),
             (2048, 2056, 120),
             (3080, 3096, 112))

ADAM_LR, ADAM_B1, ADAM_B2, ADAM_EPS, ADAM_WD, ADAM_STEP = 0.001, 0.9, 0.999, 1e-08, 0.01, 10


def _cparams(*sem):
    return pltpu.CompilerParams(dimension_semantics=sem, vmem_limit_bytes=VMEM_LIMIT)


def _resident(shape):
    return pl.BlockSpec(shape, lambda *_: (0,) * len(shape), pipeline_mode=pl.Buffered(1))


def _layer(shape, l):
    return pl.BlockSpec((None,) + tuple(shape), lambda *_: (l,) + (0,) * len(shape), pipeline_mode=pl.Buffered(1))


def _acc(shape):
    return pl.BlockSpec(shape, lambda *_: (0,) * len(shape))


def _dg(a, b, ca, cb, precision=None):
    lead = a.ndim - 2
    batch = ((0,), (0,)) if lead else ((), ())
    return lax.dot_general(a, b, (((ca + lead,), (cb + lead,)), batch), preferred_element_type=F32, precision=precision)


def _make_dot(cast, precision):
    def raw(a, b, form):
        ca = 0 if form == "tn" else 1
        cb = 1 if form == "nt" else 0
        return _dg_any(cast(a), cast(b), ca, cb, precision)

    @functools.partial(jax.custom_vjp, nondiff_argnums=(2,))
    def dot(a, b, form):
        return raw(a, b, form)

    def fwd(a, b, form):
        return raw(a, b, form), (a, b)

    def bwd(form, res, g):
        a, b = res
        if form == "nn":
            return raw(g, b, "nt"), raw(a, g, "tn")
        if form == "nt":
            return raw(g, b, "nn"), raw(g, a, "tn")
        return raw(b, g, "nt"), raw(a, g, "nn")

    dot.defvjp(fwd, bwd)
    return dot


def _split(t):
    hi = t.astype(BF16)
    return hi, (t - hi.astype(F32)).astype(BF16)


def _dg3(a, b, ca, cb):
    (ah, al), (bh, bl) = a, b
    return _dg(ah, bh, ca, cb) + (_dg(ah, bl, ca, cb) + _dg(al, bh, ca, cb))


class _Split3:
    pass


def _dg_any(a, b, ca, cb, precision):
    if precision is _Split3:
        return _dg3(_split(a), _split(b), ca, cb)
    return _dg(a, b, ca, cb, precision)


_bdot = _make_dot(lambda t: t.astype(BF16), None)
_hdot = _make_dot(lambda t: t, _Split3)


def _inv_unit_lower(low):
    n = low.shape[-1]
    eye = (lax.broadcasted_iota(jnp.int32, (n, n), 0) == lax.broadcasted_iota(jnp.int32, (n, n), 1)).astype(F32)
    p = -low
    inv = eye + p
    ps = _split(p)
    k = 1
    while 2 * k < n:
        ps = _split(_dg3(ps, ps, 1, 0))
        inv = inv + _dg3(_split(inv), ps, 1, 0)
        k *= 2
    return inv


@jax.custom_vjp
def _unit_lower_solve(low, rhs, inv):
    return _dg3(_split(inv), _split(rhs), 1, 0)


def _uls_fwd(low, rhs, inv):
    sol = _dg3(_split(inv), _split(rhs), 1, 0)
    return sol, (inv, sol)


def _uls_bwd(res, g):
    inv, sol = res
    d_rhs = _dg3(_split(inv), _split(g), 0, 0)
    return -_dg3(_split(d_rhs), _split(sol), 1, 1), d_rhs, jnp.zeros_like(inv)


_unit_lower_solve.defvjp(_uls_fwd, _uls_bwd)


def _sigmoid(x):
    return 1.0 / (1.0 + jnp.exp(-x))


def _softplus(x):
    return jnp.maximum(x, 0.0) + jnp.log(1.0 + jnp.exp(-jnp.maximum(x, -x)))


def _gelu(x):
    return 0.5 * x * (1.0 + jnp.tanh(0.7978845608028654 * (x + 0.044715 * (x * x * x))))


def _tri(n):
    ri = lax.broadcasted_iota(jnp.int32, (n, n), 0)
    ci = lax.broadcasted_iota(jnp.int32, (n, n), 1)
    return ri, ci


def _stack(ts):
    return jnp.concatenate([t[None] for t in ts], axis=0)


@jax.custom_vjp
def _head_sums(x):
    ri, ci = _tri(x.shape[-1])
    shift = HD.bit_length() - 1
    ones = (jnp.right_shift(ri, shift) == jnp.right_shift(ci, shift)).astype(BF16)
    hi, lo = _split(x)
    return _dg(hi, ones, 1, 0) + _dg(lo, ones, 1, 0)


_head_sums.defvjp(lambda x: (_head_sums(x), None), lambda _, g: (_head_sums(g),))


def _sgu_chunk(uv, ln_w, ln_b, ws, bs):
    u = _gelu(uv[:, :G])
    v = _gelu(uv[:, G:])
    mu = jnp.mean(v, axis=-1, keepdims=True)
    vc = v - mu
    var = jnp.mean(vc * vc, axis=-1, keepdims=True)
    vn = vc * lax.rsqrt(var + EPS) * ln_w + ln_b
    ri, ci = _tri(SGU_C)
    bs_t = _hdot((ri == ci).astype(F32), bs, "nt")
    mixed = []
    for h in range(NH):
        wm = jnp.where(ri >= ci, ws[h], 0.0)
        mixed.append(_bdot(wm, vn[:, HD * h:HD * (h + 1)], "nn") + bs_t[:, h:h + 1])
    return u * jnp.concatenate(mixed, axis=1)


def _dn_tile(q, k, v, ab, z, state, inv, a_log, dt_bias, nw):
    c = DN_C
    tm = q.shape[0]
    cps = tm // c
    ri, ci = _tri(tm)
    shift = c.bit_length() - 1
    same = jnp.right_shift(ri, shift) == jnp.right_shift(ci, shift)
    g4 = -jnp.exp(a_log) * _softplus(ab[:, 0:NH] + dt_bias)
    beta4 = _sigmoid(ab[:, NH:2 * NH])
    gc4 = _hdot((same & (ri >= ci)).astype(F32), g4, "nn")
    gr4 = _hdot(g4, (same & (ri <= ci)).astype(F32), "tn")
    pairs = [(slice(c * j, c * (j + 1)), h) for j in range(cps) for h in range(NH)]
    heads = lambda t: _stack([t[rows, HD * h:HD * (h + 1)] for rows, h in pairs])
    col = lambda t: _stack([t[rows, h:h + 1] for rows, h in pairs])
    qn = heads(q * lax.rsqrt(_head_sums(q * q) + EPS) * (HD ** -0.5))
    kn = heads(k * lax.rsqrt(_head_sums(k * k) + EPS))
    vh = heads(v)
    gc, beta = col(gc4), col(beta4)
    gr = _stack([gr4[h:h + 1, rows] for rows, h in pairs])
    gl = jnp.sum(col(g4), axis=1, keepdims=True)
    r2, c2 = _tri(c)
    decay = jnp.exp(jnp.where(r2 >= c2, gc - gr, -jnp.inf))
    kb = kn * beta
    low = jnp.where(r2 > c2, _bdot(kb, kn, "nt") * decay, 0.0)
    eg = jnp.exp(gc)
    if inv is None:
        inv = _inv_unit_lower(low)
    sol = _unit_lower_solve(low, jnp.concatenate([vh * beta, kb * eg], axis=2), inv)
    u, w = sol[:, :, :HD], sol[:, :, HD:]
    attn = _bdot(qn, kn, "nt") * decay
    qg, kd, cd = qn * eg, kn * jnp.exp(gl - gc), jnp.exp(gl)
    ys = []
    for j in range(cps):
        b = slice(NH * j, NH * (j + 1))
        v_new = u[b] - _bdot(w[b], state, "nn")
        o = _bdot(qg[b], state, "nn") + _bdot(attn[b], v_new, "nn")
        state = state * cd[b] + _bdot(kd[b], v_new, "tn")
        on = o * lax.rsqrt(jnp.mean(o * o, axis=-1, keepdims=True) + EPS) * nw
        ys.append(jnp.concatenate([on[h] for h in range(NH)], axis=1))
    return jnp.concatenate(ys, axis=0) * (z * _sigmoid(z)), state, inv


def _gla_tile(q, k, v, glr, z, state_t, w2, gate_bias, nw):
    c = GLA_C
    tm = q.shape[0]
    cps = tm // c
    ri, ci = _tri(tm)
    shift = c.bit_length() - 1
    same = jnp.right_shift(ri, shift) == jnp.right_shift(ci, shift)
    w2_rows = jnp.concatenate([w2, jnp.zeros((LANES - w2.shape[0], G), F32)], axis=0)
    pre = _bdot(glr, w2_rows, "nn") + gate_bias
    log_a = (jnp.minimum(pre, 0.0) - jnp.log(1.0 + jnp.exp(-jnp.maximum(pre, -pre)))) * (1.0 / 16.0)
    gcum = _hdot((same & (ri >= ci)).astype(F32), log_a, "nn")
    row = lax.broadcasted_iota(jnp.int32, (c, G), 0)
    qs = q * (HD ** -0.5)
    qa, ka, qg, kl, dec = [], [], [], [], []
    for j in range(cps):
        rows = slice(c * j, c * (j + 1))
        gj = gcum[rows]
        g_mid = jnp.sum(jnp.where(row == c // 2, gj, 0.0), axis=0, keepdims=True)
        g_last = jnp.sum(log_a[rows], axis=0, keepdims=True)
        qa.append(qs[rows] * jnp.exp(gj - g_mid))
        ka.append(k[rows] * jnp.exp(g_mid - gj))
        qg.append(qs[rows] * jnp.exp(gj))
        kl.append(k[rows] * jnp.exp(g_last - gj))
        dec.append(jnp.exp(g_last))
    heads = lambda ts: _stack([t[:, HD * h:HD * (h + 1)] for t in ts for h in range(NH)])
    qa, ka, qg, kl, dec = heads(qa), heads(ka), heads(qg), heads(kl), heads(dec)
    vh = heads([v[c * j:c * (j + 1)] for j in range(cps)])
    r2, c2 = _tri(c)
    o_intra = _bdot(jnp.where(r2 >= c2, _bdot(qa, ka, "nt"), 0.0), vh, "nn")
    ys = []
    for j in range(cps):
        b = slice(NH * j, NH * (j + 1))
        o = o_intra[b] + _bdot(qg[b], state_t, "nt")
        state_t = state_t * dec[b] + _bdot(vh[b], kl[b], "tn")
        on = o * lax.rsqrt(jnp.mean(o * o, axis=-1, keepdims=True) + EPS) * nw
        ys.append(jnp.concatenate([on[h] for h in range(NH)], axis=1))
    return jnp.concatenate(ys, axis=0) * (z * _sigmoid(z)), state_t


def _rms(x, nw):
    r = lax.rsqrt(jnp.mean(x * x, axis=-1, keepdims=True) + EPS)
    xh = x * r
    return xh * nw, xh, r


def _rms_bwd(g, xh, r, nw):
    gw = g * nw
    return r * (gw - xh * jnp.mean(gw * xh, axis=-1, keepdims=True)), jnp.sum(g * xh, axis=0, keepdims=True)


def _row(tm, w, blk=0):
    return pl.BlockSpec((tm, w), lambda i: (i, blk))


def _in_proj(x, nw, wp, l, tm=512, rider=None):
    t = x.shape[0]

    def body(*refs):
        (x_ref, nw_ref, w_ref, h_ref, p_ref), ride = _split_refs(refs, 3, 2, 0, rider)
        _ride_begin(rider, ride, (t // tm,))
        h = _rms(x_ref[...], nw_ref[l:l + 1, :])[0].astype(BF16)
        h_ref[...] = h
        p_ref[...] = jnp.dot(h, w_ref[...], preferred_element_type=F32)
        _ride_end(rider, ride, (t // tm,))

    specs, operands = _host(
        rider, [_row(tm, D), _resident((DEPTH, D)), _resident((D, P))], [_row(tm, D), _row(tm, P)],
        [jax.ShapeDtypeStruct((t, D), BF16), jax.ShapeDtypeStruct((t, P), F32)], [], [x, nw, wp])
    return pl.pallas_call(body, name="in_proj", grid=(t // tm,), compiler_params=_cparams("arbitrary"),
                          **specs)(*operands)


def _gu_spec(l):
    return pl.BlockSpec((N_DEV, None, GU_SHARD, D), lambda *_: (0, l, 0, 0), pipeline_mode=pl.Buffered(1))


def _out_mlp(x, ys, w_out, nw2, w_gu_t, w_down, l, tm=256):
    t = x.shape[0]

    def body(x_ref, ya, yb, yc, yd, wo_ref, nw_ref, wgu_ref, wd_ref, mix_ref, x1_ref, h2_ref, gu_ref, act_ref, x2_ref):
        mix = jnp.concatenate([ya[...], yb[...], yc[...], yd[...]], axis=1)
        mix_ref[...] = mix
        x1 = x_ref[...] + jnp.dot(mix, wo_ref[...], preferred_element_type=F32)
        x1_ref[...] = x1
        h2 = _rms(x1, nw_ref[l:l + 1, :])[0].astype(BF16)
        h2_ref[...] = h2
        gu = _dg(h2, wgu_ref[...].reshape(2 * FF, D), 1, 1)
        gu_ref[...] = gu.astype(BF16)
        gate, up = gu[:, :FF], gu[:, FF:]
        act = (gate * _sigmoid(gate) * up).astype(BF16)
        act_ref[...] = act
        x2_ref[...] = x1 + jnp.dot(act, wd_ref[...], preferred_element_type=F32)

    return pl.pallas_call(
        body, name="out_mlp", grid=(t // tm,),
        in_specs=[_row(tm, D), _row(tm, G), _row(tm, G), _row(tm, G), _row(tm, G), _layer((D, D), l),
                  _resident((DEPTH, D)), _gu_spec(l), _layer((FF, D), l)],
        out_specs=[_row(tm, D), _row(tm, D), _row(tm, D), _row(tm, 2 * FF), _row(tm, FF), _row(tm, D)],
        out_shape=[jax.ShapeDtypeStruct((t, D), BF16), jax.ShapeDtypeStruct((t, D), F32),
                   jax.ShapeDtypeStruct((t, D), BF16), jax.ShapeDtypeStruct((t, 2 * FF), BF16),
                   jax.ShapeDtypeStruct((t, FF), BF16), jax.ShapeDtypeStruct((t, D), F32)],
        compiler_params=_cparams("parallel"))(x, *ys, w_out, nw2, w_gu_t, w_down)


def _loss_head(x, nw, target, tm=512):
    t = x.shape[0]

    def body(x_ref, nw_ref, tg_ref, loss_ref, dnw_ref, dx_ref):
        @pl.when(pl.program_id(0) == 0)
        def _():
            loss_ref[...] = jnp.zeros_like(loss_ref)
            dnw_ref[...] = jnp.zeros_like(dnw_ref)
        nw_v = nw_ref[...]
        y, xh, r = _rms(x_ref[...], nw_v)
        err = y - tg_ref[...]
        loss_ref[...] += 0.5 * jnp.sum(err * err) * (1.0 / D)
        dx, dnw = _rms_bwd(err * (1.0 / D), xh, r, nw_v)
        dx_ref[...] = dx
        dnw_ref[...] += dnw

    return pl.pallas_call(
        body, name="loss_head", grid=(t // tm,),
        in_specs=[_row(tm, D), _resident((1, D)), _row(tm, D)],
        out_specs=[_acc((8, LANES)), _acc((1, D)), _row(tm, D)],
        out_shape=[jax.ShapeDtypeStruct((8, LANES), F32), jax.ShapeDtypeStruct((1, D), F32),
                   jax.ShapeDtypeStruct((t, D), F32)],
        compiler_params=_cparams("arbitrary"))(x, nw, target)


def _mlp_out_bwd(dx2, x1, gu, nw2, w_down, w_gu, w_out, l, tm=256, rider=None):
    t = dx2.shape[0]

    def body(*refs):
        (dx2_ref, x1_ref, gu_ref, nw_ref, wd_ref, wgu_ref, wo_ref, dgu_ref, dx1_ref, dmix_ref, dnw_ref), ride = \
            _split_refs(refs, 7, 4, 0, rider)
        _ride_begin(rider, ride, (t // tm,))

        @pl.when(pl.program_id(0) == 0)
        def _():
            dnw_ref[...] = jnp.zeros_like(dnw_ref)
        dx2 = dx2_ref[...]
        dact = _dg(dx2.astype(BF16), wd_ref[...], 1, 1)
        gu = gu_ref[...].astype(F32)
        gate, up = gu[:, :FF], gu[:, FF:]
        s = _sigmoid(gate)
        dgu = jnp.concatenate([dact * up * (s * (1.0 + gate * (1.0 - s))), dact * (gate * s)], axis=1).astype(BF16)
        dgu_ref[...] = dgu
        dh2 = jnp.dot(dgu, wgu_ref[...].reshape(2 * FF, D), preferred_element_type=F32)
        nw_v = nw_ref[l:l + 1, :]
        _, xh, r = _rms(x1_ref[...], nw_v)
        dxn, dnw = _rms_bwd(dh2, xh, r, nw_v)
        dx1 = dx2 + dxn
        dx1_ref[...] = dx1
        dnw_ref[...] += dnw
        dmix_ref[...] = _dg(dx1.astype(BF16), wo_ref[...], 1, 1)
        _ride_end(rider, ride, (t // tm,))

    specs, operands = _host(
        rider, [_row(tm, D), _row(tm, D), _row(tm, 2 * FF), _resident((DEPTH, D)), _layer((FF, D), l), _gu_spec(l),
                _layer((D, D), l)],
        [_row(tm, 2 * FF), _row(tm, D), _row(tm, D), _acc((1, D))],
        [jax.ShapeDtypeStruct((t, 2 * FF), BF16), jax.ShapeDtypeStruct((t, D), F32),
         jax.ShapeDtypeStruct((t, D), F32), jax.ShapeDtypeStruct((1, D), F32)],
        [], [dx2, x1, gu, nw2, w_down, w_gu, w_out])
    return pl.pallas_call(body, name="mlp_out_bwd", grid=(t // tm,), compiler_params=_cparams("arbitrary"),
                          **specs)(*operands)


_DP_WIDTHS = (768, 768, 512, 768, 256, 256, 128, 128)


def _in_proj_bwd(dps, x, dx1, nw, wp, l, tm=512):
    t = x.shape[0]

    def body(*refs):
        dp_refs, (x_ref, dx1_ref, nw_ref, w_ref, dp_ref, dx_ref, dnw_ref) = refs[:8], refs[8:]

        @pl.when(pl.program_id(0) == 0)
        def _():
            dnw_ref[...] = jnp.zeros_like(dnw_ref)
        dp = jnp.concatenate([r[...] for r in dp_refs], axis=1)
        dp_ref[...] = dp
        dh = _dg(dp, w_ref[...], 1, 1)
        nw_v = nw_ref[l:l + 1, :]
        _, xh, r = _rms(x_ref[...], nw_v)
        dxn, dnw = _rms_bwd(dh, xh, r, nw_v)
        dx_ref[...] = dx1_ref[...] + dxn
        dnw_ref[...] += dnw

    return pl.pallas_call(
        body, name="in_proj_bwd", grid=(t // tm,),
        in_specs=[_row(tm, w) for w in _DP_WIDTHS] + [_row(tm, D), _row(tm, D), _resident((DEPTH, D)), _resident((D, P))],
        out_specs=[_row(tm, P), _row(tm, D), _acc((1, D))],
        out_shape=[jax.ShapeDtypeStruct((t, P), BF16), jax.ShapeDtypeStruct((t, D), F32),
                   jax.ShapeDtypeStruct((1, D), F32)],
        compiler_params=_cparams("arbitrary"))(*dps, x, dx1, nw, wp)


def _accumulate(acc, o_ref, t_axis, term):
    @pl.when(pl.program_id(t_axis) == 0)
    def _():
        acc[...] = jnp.zeros_like(acc)
    acc[...] += term

    @pl.when(pl.program_id(t_axis) == pl.num_programs(t_axis) - 1)
    def _():
        o_ref[...] = acc[...].astype(o_ref.dtype)


WGRAD_TOKENS = 2048


def _matmul_tn(a, b, name, out_dtype, tn=512, tt=WGRAD_TOKENS, rider=None):
    t, m = a.shape
    n = b.shape[1]
    tt = min(tt, t)
    grid = (n // tn, t // tt)

    def body(*refs):
        (a_ref, b_ref, o_ref, acc), ride = _split_refs(refs, 2, 1, 1, rider)
        _ride_begin(rider, ride, grid)
        _accumulate(acc, o_ref, 1, _dg(a_ref[...].astype(BF16), b_ref[...].astype(BF16), 0, 0))
        _ride_end(rider, ride, grid)

    specs, operands = _host(
        rider, [pl.BlockSpec((tt, m), lambda j, i: (i, 0)), pl.BlockSpec((tt, tn), lambda j, i: (i, j))],
        [pl.BlockSpec((m, tn), lambda j, i: (0, j))], [jax.ShapeDtypeStruct((m, n), out_dtype)],
        [pltpu.VMEM((m, tn), F32)], [a, b])
    out = pl.pallas_call(body, name=name, grid=grid, compiler_params=_cparams("arbitrary", "arbitrary"), **specs)(*operands)
    return out[0] if rider is None else out


WGRAD_ROWS = 2 * GU_SHARD


def _wgrad_rows(a, b, name, tt=WGRAD_TOKENS):
    t, m = a.shape
    tt = min(tt, t)

    def body(a_ref, b_ref, o_ref, acc):
        _accumulate(acc, o_ref, 1, _dg(a_ref[...], b_ref[...].astype(BF16), 0, 0))

    return pl.pallas_call(
        body, name=name, grid=(m // WGRAD_ROWS, t // tt),
        in_specs=[pl.BlockSpec((tt, WGRAD_ROWS), lambda j, i: (i, j)), pl.BlockSpec((tt, D), lambda j, i: (i, 0))],
        out_specs=pl.BlockSpec((WGRAD_ROWS, D), lambda j, i: (j, 0)),
        out_shape=jax.ShapeDtypeStruct((m, D), BF16),
        scratch_shapes=[pltpu.VMEM((WGRAD_ROWS, D), F32)],
        compiler_params=_cparams("parallel", "arbitrary"))(a, b)


def _relayout_w_in(g, tr=256):
    def body(w_ref, o_ref):
        full = jnp.concatenate([w_ref[d] for d in range(N_DEV)], axis=1)
        parts = []
        for a, b, z in _PAD_SEGS:
            parts.append(full[:, a:b])
            if z:
                parts.append(jnp.zeros((tr, z), full.dtype))
        o_ref[...] = jnp.concatenate(parts, axis=1)

    return pl.pallas_call(
        body, name="relayout_w_in", grid=(D // tr,),
        in_specs=[pl.BlockSpec((N_DEV, tr, IN_SHARD), lambda i: (0, i, 0))], out_specs=_row(tr, P),
        out_shape=jax.ShapeDtypeStruct((D, P), g.dtype), compiler_params=_cparams("parallel"))(g)


def _scatter_w_in_grad(gp, tr=256):
    def body(g_ref, o_ref):
        g = g_ref[...]
        pieces, off = [], 0
        for a, b, z in _PAD_SEGS:
            pieces.append((a, g[:, off:off + (b - a)]))
            off += (b - a) + z
        nat = jnp.concatenate([piece for _, piece in sorted(pieces, key=lambda ap: ap[0])], axis=1)
        for d in range(N_DEV):
            o_ref[d] = nat[:, IN_SHARD * d:IN_SHARD * (d + 1)].astype(BF16)

    return pl.pallas_call(
        body, name="scatter_w_in_grad", grid=(D // tr,),
        in_specs=[pl.BlockSpec((tr, P), lambda i: (i, 0))],
        out_specs=pl.BlockSpec((N_DEV, tr, IN_SHARD), lambda i: (0, i, 0)),
        out_shape=jax.ShapeDtypeStruct((N_DEV, D, IN_SHARD), BF16),
        compiler_params=_cparams("parallel"))(gp)


_A_BLK = 3


def _sgu_specs(l):
    return [_resident((DEPTH, G)), _resident((DEPTH, G)), _layer((NH, SGU_C, SGU_C), l), _layer((NH, SGU_C), l)]


def _sgu_fwd(p, ln_w, ln_b, ws, bs, l, tm=256):
    t = p.shape[0]

    def body(uv_ref, lw_ref, lb_ref, ws_ref, bs_ref, y_ref):
        ws_v = [ws_ref[h] for h in range(NH)]
        for c in range(tm // SGU_C):
            rows = pl.ds(c * SGU_C, SGU_C)
            y_ref[rows, :] = _sgu_chunk(uv_ref[rows, :], lw_ref[l:l + 1, :], lb_ref[l:l + 1, :], ws_v,
                                        bs_ref[...]).astype(BF16)

    return pl.pallas_call(
        body, name="sgu_fwd", grid=(t // tm,),
        in_specs=[_row(tm, 2 * G, _A_BLK)] + _sgu_specs(l), out_specs=_row(tm, G),
        out_shape=jax.ShapeDtypeStruct((t, G), BF16),
        compiler_params=_cparams("parallel"))(p, ln_w, ln_b, ws, bs)


def _sgu_bwd(p, dmix, ln_w, ln_b, ws, bs, l, tm=256):
    t = p.shape[0]

    def body(uv_ref, dy_ref, lw_ref, lb_ref, ws_ref, bs_ref, duv_ref, dlw_ref, dlb_ref, dws_ref, dbs_ref):
        @pl.when(pl.program_id(0) == 0)
        def _():
            for r in (dlw_ref, dlb_ref, dws_ref, dbs_ref):
                r[...] = jnp.zeros_like(r)
        ws_v = [ws_ref[h] for h in range(NH)]
        for c in range(tm // SGU_C):
            rows = pl.ds(c * SGU_C, SGU_C)
            _, vjp = jax.vjp(_sgu_chunk, uv_ref[rows, :], lw_ref[l:l + 1, :], lb_ref[l:l + 1, :], ws_v, bs_ref[...])
            duv, dlw, dlb, dws, dbs = vjp(dy_ref[rows, :])
            duv_ref[rows, :] = duv.astype(BF16)
            dlw_ref[...] += dlw
            dlb_ref[...] += dlb
            dbs_ref[...] += dbs
            for h in range(NH):
                dws_ref[h] += dws[h]

    return pl.pallas_call(
        body, name="sgu_bwd", grid=(t // tm,),
        in_specs=[_row(tm, 2 * G, _A_BLK), _row(tm, G, 0)] + _sgu_specs(l),
        out_specs=[_row(tm, 2 * G), _acc((1, G)), _acc((1, G)), _acc((NH, SGU_C, SGU_C)), _acc((NH, SGU_C))],
        out_shape=[jax.ShapeDtypeStruct((t, 2 * G), BF16), jax.ShapeDtypeStruct((1, G), F32),
                   jax.ShapeDtypeStruct((1, G), F32), jax.ShapeDtypeStruct((NH, SGU_C, SGU_C), F32),
                   jax.ShapeDtypeStruct((NH, SGU_C), F32)],
        compiler_params=_cparams("arbitrary"))(p, dmix, ln_w, ln_b, ws, bs)


HALO = 8


def _prev_halo(tm, width, col):
    return pl.BlockSpec((HALO, width), lambda i: (jnp.maximum(i * (tm // HALO) - 1, 0), col))


def _next_halo(tm, width, col, t):
    return pl.BlockSpec((HALO, width), lambda i: (jnp.minimum((i + 1) * (tm // HALO), t // HALO - 1), col))


def _with_prev(halo_ref, x_ref):
    halo = jnp.where(pl.program_id(0) == 0, 0.0, halo_ref[...])
    return jnp.concatenate([halo, x_ref[...]], axis=0)


def _with_next(x_ref, halo_ref):
    halo = jnp.where(pl.program_id(0) == pl.num_programs(0) - 1, 0.0, halo_ref[...])
    return jnp.concatenate([x_ref[...], halo], axis=0)


def _delay(ext, j):
    return ext if j == 0 else pltpu.roll(ext, j, axis=0)


def _advance(ext, j):
    return ext if j == 0 else pltpu.roll(ext, ext.shape[0] - j, axis=0)


def _causal_conv(ext, w, tm):
    kw = w.shape[0]
    out = None
    for k in range(kw):
        term = _delay(ext, kw - 1 - k)[HALO:HALO + tm] * w[k:k + 1, :]
        out = term if out is None else out + term
    return out


def _causal_conv_t(ext, w, tm):
    kw = w.shape[0]
    out = None
    for k in range(kw):
        term = _advance(ext, kw - 1 - k)[0:tm] * w[k:k + 1, :]
        out = term if out is None else out + term
    return out


def _conv_wgrad(ext, dy, kw, tm):
    return jnp.concatenate(
        [jnp.sum(_delay(ext, kw - 1 - k)[HALO:HALO + tm] * dy, axis=0, keepdims=True) for k in range(kw)], axis=0)


_B_GB, _B_GC, _B_H = 8, 9, 10
_C_QKV, _D_QKV = 0, 1
_C_Z, _D_Z = 11, 12
_C_AB, _D_GLR = 26, 27


def _sconv_fwd(p, w, l, tm=512):
    t = p.shape[0]

    def body(gb_ref, gc_ref, h_ref, gc_halo, h_halo, w_ref, y_ref):
        s_ext = _with_prev(gc_halo, gc_ref) * _with_prev(h_halo, h_ref)
        y_ref[...] = (gb_ref[...] * _causal_conv(s_ext, w_ref[...], tm)).astype(BF16)

    return pl.pallas_call(
        body, name="sconv_fwd", grid=(t // tm,),
        in_specs=[_row(tm, G, _B_GB), _row(tm, G, _B_GC), _row(tm, G, _B_H), _prev_halo(tm, G, _B_GC),
                  _prev_halo(tm, G, _B_H), _layer((3, G), l)],
        out_specs=_row(tm, G), out_shape=jax.ShapeDtypeStruct((t, G), BF16),
        compiler_params=_cparams("parallel"))(p, p, p, p, p, w)


def _sconv_bwd(p, dmix, w, l, tm=512):
    t = p.shape[0]

    def body(gb_ref, gc_ref, h_ref, gc_halo, h_halo, gb_next, dy_ref, dy_next, w_ref, dp_ref, dw_ref):
        @pl.when(pl.program_id(0) == 0)
        def _():
            dw_ref[...] = jnp.zeros_like(dw_ref)
        w_v = w_ref[...]
        s_ext = _with_prev(gc_halo, gc_ref) * _with_prev(h_halo, h_ref)
        dout = dy_ref[...]
        d_gb = dout * _causal_conv(s_ext, w_v, tm)
        dconv_ext = _with_next(dy_ref, dy_next) * _with_next(gb_ref, gb_next)
        ds = _causal_conv_t(dconv_ext, w_v, tm)
        dw_ref[...] += _conv_wgrad(s_ext, dconv_ext[0:tm], 3, tm)
        dp_ref[...] = jnp.concatenate([d_gb, ds * h_ref[...], ds * gc_ref[...]], axis=1).astype(BF16)

    return pl.pallas_call(
        body, name="sconv_bwd", grid=(t // tm,),
        in_specs=[_row(tm, G, _B_GB), _row(tm, G, _B_GC), _row(tm, G, _B_H), _prev_halo(tm, G, _B_GC),
                  _prev_halo(tm, G, _B_H), _next_halo(tm, G, _B_GB, t), _row(tm, G, 1), _next_halo(tm, G, 1, t),
                  _layer((3, G), l)],
        out_specs=[_row(tm, 3 * G), _acc((3, G))],
        out_shape=[jax.ShapeDtypeStruct((t, 3 * G), BF16), jax.ShapeDtypeStruct((3, G), F32)],
        compiler_params=_cparams("arbitrary"))(p, p, p, p, p, p, dmix, dmix, w)


def _qkv_conv_fwd(p, w, l, tm=512):
    t = p.shape[0]

    def body(x_ref, x_halo, w_ref, y_ref):
        c = _causal_conv(_with_prev(x_halo, x_ref), w_ref[...], tm)
        y_ref[...] = c * _sigmoid(c)

    return pl.pallas_call(
        body, name="qkv_conv_fwd", grid=(t // tm,),
        in_specs=[_row(tm, 3 * G, _C_QKV), _prev_halo(tm, 3 * G, _C_QKV), _layer((4, 3 * G), l)],
        out_specs=_row(tm, 3 * G), out_shape=jax.ShapeDtypeStruct((t, 3 * G), F32),
        compiler_params=_cparams("parallel"))(p, p, w)


def _qkv_conv_bwd(p, dy, w, l, tm=512):
    t = p.shape[0]

    def body(x_ref, x_halo, x_next, dy_ref, dy_next, w_ref, dx_ref, dw_ref):
        @pl.when(pl.program_id(0) == 0)
        def _():
            dw_ref[...] = jnp.zeros_like(dw_ref)
        w_v = w_ref[...]
        x_all = jnp.concatenate([_with_prev(x_halo, x_ref), jnp.where(
            pl.program_id(0) == pl.num_programs(0) - 1, 0.0, x_next[...])], axis=0)
        c = _causal_conv(x_all, w_v, tm + HALO)
        s = _sigmoid(c)
        dc_ext = _with_next(dy_ref, dy_next) * (s * (1.0 + c * (1.0 - s)))
        dx_ref[...] = _causal_conv_t(dc_ext, w_v, tm).astype(BF16)
        dw_ref[...] += _conv_wgrad(x_all[0:HALO + tm], dc_ext[0:tm], 4, tm)

    return pl.pallas_call(
        body, name="qkv_conv_bwd", grid=(t // tm,),
        in_specs=[_row(tm, 3 * G, _C_QKV), _prev_halo(tm, 3 * G, _C_QKV), _next_halo(tm, 3 * G, _C_QKV, t),
                  _row(tm, 3 * G), _next_halo(tm, 3 * G, 0, t), _layer((4, 3 * G), l)],
        out_specs=[_row(tm, 3 * G), _acc((4, 3 * G))],
        out_shape=[jax.ShapeDtypeStruct((t, 3 * G), BF16), jax.ShapeDtypeStruct((4, 3 * G), F32)],
        compiler_params=_cparams("arbitrary"))(p, p, p, dy, dy, w)


_STATE = pl.BlockSpec((1, NH, HD, HD), lambda i: (i, 0, 0, 0))


def _dn_specs():
    return [_resident((DEPTH, NH)), _resident((DEPTH, NH)), _resident((DEPTH, HD))]


def _dn_fwd(qkv, p, params, l, cps=4, rider=None, forward_at=1.0):
    t = qkv.shape[0]
    tm = DN_C * cps
    nb = cps * NH

    def body(*refs):
        (qkv_ref, z_ref, ab_ref, alog_ref, dt_ref, nw_ref, y_ref, st_ref, inv_ref, state), ride = _split_refs(
            refs, 6, 3, 1, rider)
        _ride_begin(rider, ride, (t // tm,))

        @pl.when(pl.program_id(0) == 0)
        def _():
            state[...] = jnp.zeros_like(state)
        st_ref[0] = state[...]
        y, new, inv = _dn_tile(qkv_ref[:, 0:G], qkv_ref[:, G:2 * G], qkv_ref[:, 2 * G:3 * G], ab_ref[...], z_ref[...],
                               state[...], None, alog_ref[l:l + 1, :], dt_ref[l:l + 1, :], nw_ref[l:l + 1, :])
        y_ref[...] = y.astype(BF16)
        inv_ref[...] = inv
        state[...] = new
        _ride_end(rider, ride, (t // tm,), forward_at)

    specs, operands = _host(
        rider, [_row(tm, 3 * G), _row(tm, G, _C_Z), _row(tm, LANES, _C_AB)] + _dn_specs(),
        [_row(tm, G), _STATE, pl.BlockSpec((nb, DN_C, DN_C), lambda i: (i, 0, 0))],
        [jax.ShapeDtypeStruct((t, G), BF16), jax.ShapeDtypeStruct((t // tm, NH, HD, HD), F32),
         jax.ShapeDtypeStruct((t // DN_C * NH, DN_C, DN_C), F32)],
        [pltpu.VMEM((NH, HD, HD), F32)], [qkv, p, p, *params])
    return pl.pallas_call(body, name="dn_fwd", grid=(t // tm,), compiler_params=_cparams("arbitrary"), **specs)(*operands)


def _dn_bwd(qkv, p, states, inv, dmix, params, l, cps=4, rider=None):
    t = qkv.shape[0]
    tm = DN_C * cps
    n = t // tm
    nb = cps * NH

    def body(*refs):
        (qkv_ref, z_ref, ab_ref, st_ref, inv_ref, dy_ref, alog_ref, dt_ref, nw_ref,
         dqkv_ref, dz_ref, dab_ref, dalog_ref, ddt_ref, dnw_ref, dstate), ride = _split_refs(refs, 9, 6, 1, rider)
        _ride_begin(rider, ride, (n,))
        dprm_refs = (dalog_ref, ddt_ref, dnw_ref)

        @pl.when(pl.program_id(0) == 0)
        def _():
            dstate[...] = jnp.zeros_like(dstate)
            for r in dprm_refs:
                r[...] = jnp.zeros_like(r)
        inv_v = inv_ref[...]
        tile = lambda q, k, v, ab, z, st, alog, dt, nw: _dn_tile(q, k, v, ab, z, st, inv_v, alog, dt, nw)[:2]
        _, vjp = jax.vjp(tile, qkv_ref[:, 0:G], qkv_ref[:, G:2 * G], qkv_ref[:, 2 * G:3 * G], ab_ref[...], z_ref[...],
                         st_ref[0], alog_ref[l:l + 1, :], dt_ref[l:l + 1, :], nw_ref[l:l + 1, :])
        dq, dk, dv, dab, dz, dst, *dprm = vjp((dy_ref[...], dstate[...]))
        dqkv_ref[...] = jnp.concatenate([dq, dk, dv], axis=1)
        dz_ref[...] = dz.astype(BF16)
        dab_ref[...] = dab.astype(BF16)
        dstate[...] = dst
        for r, g in zip(dprm_refs, dprm):
            r[...] += g
        _ride_end(rider, ride, (n,))

    rev = lambda w, blk: pl.BlockSpec((tm, w), lambda i: (n - 1 - i, blk))
    specs, operands = _host(
        rider, [rev(3 * G, 0), rev(G, _C_Z), rev(LANES, _C_AB),
                pl.BlockSpec((1, NH, HD, HD), lambda i: (n - 1 - i, 0, 0, 0)),
                pl.BlockSpec((nb, DN_C, DN_C), lambda i: (n - 1 - i, 0, 0)), rev(G, 2)] + _dn_specs(),
        [rev(3 * G, 0), rev(G, 0), rev(LANES, 0), _acc((1, NH)), _acc((1, NH)), _acc((1, HD))],
        [jax.ShapeDtypeStruct((t, 3 * G), F32), jax.ShapeDtypeStruct((t, G), BF16),
         jax.ShapeDtypeStruct((t, LANES), BF16), jax.ShapeDtypeStruct((1, NH), F32),
         jax.ShapeDtypeStruct((1, NH), F32), jax.ShapeDtypeStruct((1, HD), F32)],
        [pltpu.VMEM((NH, HD, HD), F32)], [qkv, p, p, states, inv, dmix, *params])
    return pl.pallas_call(body, name="dn_bwd", grid=(n,), compiler_params=_cparams("arbitrary"), **specs)(*operands)


def _gla_specs(l):
    return [_layer((16, G), l), _resident((DEPTH, G)), _resident((DEPTH, HD))]


def _gla_fwd(p, params, l, cps=4, rider=None):
    t = p.shape[0]
    tm = GLA_C * cps

    def body(*refs):
        (qkv_ref, z_ref, glr_ref, w2_ref, gb_ref, nw_ref, y_ref, st_ref, state), ride = _split_refs(refs, 6, 2, 1, rider)
        _ride_begin(rider, ride, (t // tm,))

        @pl.when(pl.program_id(0) == 0)
        def _():
            state[...] = jnp.zeros_like(state)
        st_ref[0] = state[...]
        y, new = _gla_tile(qkv_ref[:, 0:G], qkv_ref[:, G:2 * G], qkv_ref[:, 2 * G:3 * G], glr_ref[...], z_ref[...],
                           state[...], w2_ref[...], gb_ref[l:l + 1, :], nw_ref[l:l + 1, :])
        y_ref[...] = y.astype(BF16)
        state[...] = new
        _ride_end(rider, ride, (t // tm,))

    specs, operands = _host(
        rider, [_row(tm, 3 * G, _D_QKV), _row(tm, G, _D_Z), _row(tm, LANES, _D_GLR)] + _gla_specs(l),
        [_row(tm, G), _STATE],
        [jax.ShapeDtypeStruct((t, G), BF16), jax.ShapeDtypeStruct((t // tm, NH, HD, HD), F32)],
        [pltpu.VMEM((NH, HD, HD), F32)], [p, p, p, *params])
    return pl.pallas_call(body, name="gla_fwd", grid=(t // tm,), compiler_params=_cparams("arbitrary"), **specs)(*operands)


def _gla_bwd(p, states, dmix, params, l, cps=4):
    t = p.shape[0]
    tm = GLA_C * cps
    n = t // tm

    def body(qkv_ref, z_ref, glr_ref, st_ref, dy_ref, w2_ref, gb_ref, nw_ref,
             dqkv_ref, dz_ref, dglr_ref, dw2_ref, dgb_ref, dnw_ref, dstate):
        dprm_refs = (dw2_ref, dgb_ref, dnw_ref)

        @pl.when(pl.program_id(0) == 0)
        def _():
            dstate[...] = jnp.zeros_like(dstate)
            for r in dprm_refs:
                r[...] = jnp.zeros_like(r)
        _, vjp = jax.vjp(_gla_tile, qkv_ref[:, 0:G], qkv_ref[:, G:2 * G], qkv_ref[:, 2 * G:3 * G], glr_ref[...],
                         z_ref[...], st_ref[0], w2_ref[...], gb_ref[l:l + 1, :], nw_ref[l:l + 1, :])
        dq, dk, dv, dglr, dz, dst, *dprm = vjp((dy_ref[...], dstate[...]))
        dqkv_ref[...] = jnp.concatenate([dq, dk, dv], axis=1).astype(BF16)
        dz_ref[...] = dz.astype(BF16)
        dglr_ref[...] = dglr.astype(BF16)
        dstate[...] = dst
        for r, g in zip(dprm_refs, dprm):
            r[...] += g

    rev = lambda w, blk: pl.BlockSpec((tm, w), lambda i: (n - 1 - i, blk))
    return pl.pallas_call(
        body, name="gla_bwd", grid=(n,),
        in_specs=[rev(3 * G, _D_QKV), rev(G, _D_Z), rev(LANES, _D_GLR),
                  pl.BlockSpec((1, NH, HD, HD), lambda i: (n - 1 - i, 0, 0, 0)), rev(G, 3)] + _gla_specs(l),
        out_specs=[rev(3 * G, 0), rev(G, 0), rev(LANES, 0), _acc((16, G)), _acc((1, G)), _acc((1, HD))],
        out_shape=[jax.ShapeDtypeStruct((t, 3 * G), BF16), jax.ShapeDtypeStruct((t, G), BF16),
                   jax.ShapeDtypeStruct((t, LANES), BF16), jax.ShapeDtypeStruct((16, G), F32),
                   jax.ShapeDtypeStruct((1, G), F32), jax.ShapeDtypeStruct((1, HD), F32)],
        scratch_shapes=[pltpu.VMEM((NH, HD, HD), F32)],
        compiler_params=_cparams("arbitrary"))(p, p, p, states, dmix, *params)


def _adamw_math(w, g, m, v):
    m = ADAM_B1 * m + (1.0 - ADAM_B1) * g
    v = ADAM_B2 * v + (1.0 - ADAM_B2) * (g * g)
    m_hat = m / (1.0 - ADAM_B1 ** ADAM_STEP)
    v_hat = v / (1.0 - ADAM_B2 ** ADAM_STEP)
    return -ADAM_LR * (m_hat / (jnp.sqrt(v_hat) + ADAM_EPS) + ADAM_WD * w), m, v


def _adamw_large(parts, w, m, v, name, tr):
    _, r, c = w.shape

    def body(p0_ref, p1_ref, w_ref, m_ref, v_ref, g_ref, d_ref, nm_ref, nv_ref):
        def total(p_ref):
            g = p_ref[0].astype(F32)
            for k in range(1, N_DEV):
                g = g + p_ref[k].astype(F32)
            return g

        g = jnp.where(pl.program_id(0) == 0, total(p0_ref), total(p1_ref))
        g_ref[...] = g
        d_ref[...], nm_ref[...], nv_ref[...] = _adamw_math(w_ref[...], g, m_ref[...], v_ref[...])

    blk = pl.BlockSpec((None, tr, c), lambda l, i: (l, i, 0))
    part = pl.BlockSpec((N_DEV, tr, c), lambda l, i: (0, i, 0))
    return pl.pallas_call(
        body, name=name, grid=(DEPTH, r // tr), in_specs=[part, part, blk, blk, blk],
        out_specs=[blk] * 4, out_shape=[jax.ShapeDtypeStruct(w.shape, F32)] * 4,
        compiler_params=_cparams("parallel", "parallel"))(*parts, w, m, v)


_SLAB_AT = {
    "sgu_w_spatial": (0, 0, SGU_C, LANES),
    "sgu_b_spatial": (128, 0, NH, SGU_C),
    "norm1_w": (132, 0, 1, D),
    "norm2_w": (133, 0, 1, D),
    "sgu_ln_w": (134, 0, 1, G),
    "sgu_ln_b": (134, 256, 1, G),
    "gla_gate_bias": (134, 512, 1, G),
    "dn_norm_w": (134, 768, 1, HD),
    "gla_norm_w": (134, 896, 1, HD),
    "dn_a_log": (135, 0, 1, NH),
    "dn_dt_bias": (135, 128, 1, NH),
    "gla_w_gate2": (136, 0, 16, G),
    "dn_conv_w": (136, 256, 4, 3 * G),
    "sc_conv_w": (140, 256, 3, G),
}
_LAYER_ROWS = 152
_FINAL_ROW = DEPTH * _LAYER_ROWS
_SLAB_ROWS, _SLAB_COLS = _FINAL_ROW + 8, 1024
_SLAB_ORDER = tuple(_SLAB_AT)
_SHARDED_SMALL = ("sc_conv_w", "dn_conv_w", "gla_w_gate2")
_REPLICATED = tuple(k for k in _SLAB_ORDER if k not in _SHARDED_SMALL)


def _region(name, l, h=0):
    r0, c0, nr, nc = _SLAB_AT[name]
    return slice(_LAYER_ROWS * l + r0, _LAYER_ROWS * l + r0 + nr), slice(c0 + nc * h, c0 + nc * (h + 1))


def _pack_small(layer_grads, d_final):
    def body(*refs):
        slab = refs[-1]
        slab[...] = jnp.zeros_like(slab)
        it = iter(refs[:-1])
        for l in range(DEPTH):
            for name in _SLAB_ORDER:
                ref = next(it)
                if name == "sgu_w_spatial":
                    for h in range(NH):
                        slab[_region(name, l, h)] = ref[h]
                else:
                    slab[_region(name, l)] = ref[...]
        slab[_FINAL_ROW:_FINAL_ROW + 1, :] = next(it)[...]

    flat = [layer_grads[l][name] for l in range(DEPTH) for name in _SLAB_ORDER] + [d_final]
    return pl.pallas_call(body, name="pack_small_grads", out_shape=jax.ShapeDtypeStruct((_SLAB_ROWS, _SLAB_COLS), F32),
                          compiler_params=_cparams())(*flat)


def _adamw_small(parts, w, m, v):
    names = _REPLICATED + ("final_norm_w",)
    n_in = len(names)

    def body(*refs):
        def total(rows, cols):
            g = refs[0][0, rows, cols]
            for k in range(1, N_DEV):
                g = g + refs[0][k, rows, cols]
            return g

        w_refs = dict(zip(names, refs[1:1 + n_in]))
        m_refs = dict(zip(names, refs[1 + n_in:1 + 2 * n_in]))
        v_refs = dict(zip(names, refs[1 + 2 * n_in:1 + 3 * n_in]))
        outs = refs[1 + 3 * n_in:]
        out_refs = [dict(zip(names, outs[j * n_in:(j + 1) * n_in])) for j in range(4)]
        full_refs = dict(zip(_SHARDED_SMALL, outs[4 * n_in:]))

        def update(name, idx, g):
            res = (g,) + _adamw_math(w_refs[name][idx], g, m_refs[name][idx], v_refs[name][idx])
            for o, val in zip(out_refs, res):
                o[name][idx] = val

        for l in range(DEPTH):
            for name in _REPLICATED:
                if name == "sgu_w_spatial":
                    for h in range(NH):
                        update(name, (l, h), total(*_region(name, l, h)))
                elif name == "sgu_b_spatial":
                    update(name, (l,), total(*_region(name, l)))
                else:
                    update(name, (slice(l, l + 1), slice(None)), total(*_region(name, l)))
            for name in _SHARDED_SMALL:
                full_refs[name][l] = total(*_region(name, l))
        update("final_norm_w", (slice(None), slice(None)), total(slice(_FINAL_ROW, _FINAL_ROW + 1), slice(None)))

    shapes = [jax.ShapeDtypeStruct(w[k].shape, F32) for k in names]
    full_shapes = [jax.ShapeDtypeStruct((DEPTH,) + _SLAB_AT[k][2:], F32) for k in _SHARDED_SMALL]
    outs = pl.pallas_call(body, name="adamw_small", out_shape=shapes * 4 + full_shapes, compiler_params=_cparams())(
        parts, *[w[k] for k in names], *[m[k] for k in names], *[v[k] for k in names])
    result = [dict(zip(names, outs[j * n_in:(j + 1) * n_in])) for j in range(4)]
    return result, dict(zip(_SHARDED_SMALL, outs[4 * n_in:]))


def _adamw_shards(g, w, m, v):
    names = _SHARDED_SMALL
    n = len(names)

    def body(*refs):
        for j in range(n):
            g_v = refs[j][...]
            res = _adamw_math(refs[n + j][...], g_v, refs[2 * n + j][...], refs[3 * n + j][...])
            for o, val in zip(refs[4 * n + j::n], res):
                o[...] = val

    shapes = [jax.ShapeDtypeStruct(w[k].shape, F32) for k in names]
    outs = pl.pallas_call(body, name="adamw_small_shards", out_shape=shapes * 3, compiler_params=_cparams())(
        *[g[k] for k in names], *[w[k] for k in names], *[m[k] for k in names], *[v[k] for k in names])
    return [dict(zip(names, outs[j * n:(j + 1) * n])) for j in range(3)]


_ANY = pl.BlockSpec(memory_space=pl.ANY)


def _place():
    return lax.axis_index("x"), lax.axis_index("y"), lax.axis_index("c")


def _sems(n):
    return [pltpu.SemaphoreType.DMA((n, 7)), pltpu.SemaphoreType.DMA((n, 7)), pltpu.SemaphoreType.DMA((n,))]


class _Gather:
    def __init__(self, blocks, second):
        self.inputs, self.second = list(blocks), list(second)
        self.out_shape = [jax.ShapeDtypeStruct((a.shape[0], N_DEV) + a.shape[1:] if s else (N_DEV,) + a.shape, a.dtype)
                          for a, s in zip(blocks, second)]
        self.scratch = _sems(len(blocks))

    def _copies(self, refs):
        x_refs, out_refs, (send_sems, recv_sems, local_sems) = refs
        n = len(x_refs)
        x, y, c = _place()
        me, sibling = (x, y, c), (x, y, 1 - c)
        chips = [(1 - x, y), (x, 1 - y), (1 - x, 1 - y)]

        def slot(j, px, py, pc):
            idx = 4 * px + 2 * py + pc
            return out_refs[j].at[:, idx] if self.second[j] else out_refs[j].at[idx]

        def copies(k, block, to, own=False):
            return [pltpu.make_async_remote_copy(
                src_ref=x_refs[j] if own else slot(j, *block), dst_ref=slot(j, *block),
                send_sem=send_sems.at[j, k], recv_sem=recv_sems.at[j, k], device_id=to,
                device_id_type=pl.DeviceIdType.MESH) for j in range(n)]

        mine = [pltpu.make_async_copy(x_refs[j], slot(j, *me), local_sems.at[j]) for j in range(n)]
        first = copies(0, me, sibling, own=True)
        for j, chip in enumerate(chips):
            first += copies(1 + j, me, (*chip, c), own=True)
        landed = [copies(1 + j, (*chip, c), me) for j, chip in enumerate(chips)]
        onward = [copies(4 + j, (*chip, c), sibling) for j, chip in enumerate(chips)]
        from_sibling = copies(0, sibling, me)
        for j, chip in enumerate(chips):
            from_sibling += copies(4 + j, (*chip, 1 - c), me)
        return mine, first, landed, onward, from_sibling

    def start(self, refs):
        mine, first, _, _, _ = self._copies(refs)
        for cp in mine + first:
            cp.start()

    def forward(self, refs):
        _, _, landed, onward, _ = self._copies(refs)
        for arrived, passed in zip(landed, onward):
            for cp in arrived:
                cp.wait_recv()
            for cp in passed:
                cp.start()

    def finish(self, refs):
        mine, first, _, onward, from_sibling = self._copies(refs)
        for cp in from_sibling:
            cp.wait_recv()
        for cp in first + [cp for passed in onward for cp in passed]:
            cp.wait_send()
        for cp in mine:
            cp.wait()


class _Exchange:
    def __init__(self, sends):
        self.inputs = list(sends)
        self.out_shape = [jax.ShapeDtypeStruct(a.shape, a.dtype) for a in sends]
        self.scratch = _sems(len(sends))

    def _copies(self, refs):
        x_refs, out_refs, (send_sems, recv_sems, local_sems) = refs
        n = len(x_refs)
        x, y, c = _place()
        me = 4 * x + 2 * y + c
        sent, landing = [], []
        for k in range(1, N_DEV):
            px, py, pc = x ^ ((k >> 2) & 1), y ^ ((k >> 1) & 1), c ^ (k & 1)
            them = 4 * px + 2 * py + pc
            for j in range(n):
                for here, there, into in ((them, me, sent), (me, them, landing)):
                    into.append(pltpu.make_async_remote_copy(
                        src_ref=x_refs[j].at[here], dst_ref=out_refs[j].at[there], send_sem=send_sems.at[j, k - 1],
                        recv_sem=recv_sems.at[j, k - 1], device_id=(px, py, pc), device_id_type=pl.DeviceIdType.MESH))
        mine = [pltpu.make_async_copy(x_refs[j].at[me], out_refs[j].at[me], local_sems.at[j]) for j in range(n)]
        return mine, sent, landing

    def start(self, refs):
        mine, sent, _ = self._copies(refs)
        for cp in mine + sent:
            cp.start()

    def forward(self, refs):
        pass

    def finish(self, refs):
        mine, sent, landing = self._copies(refs)
        for cp in landing:
            cp.wait_recv()
        for cp in sent:
            cp.wait_send()
        for cp in mine:
            cp.wait()


def _run(rider, name):
    n_in, n_out = len(rider.inputs), len(rider.out_shape)

    def body(*refs):
        parts = (refs[:n_in], refs[n_in:n_in + n_out], refs[n_in + n_out:])
        rider.start(parts)
        rider.forward(parts)
        rider.finish(parts)

    return pl.pallas_call(body, name=name, out_shape=rider.out_shape, in_specs=[_ANY] * n_in, out_specs=[_ANY] * n_out,
                          scratch_shapes=rider.scratch)(*rider.inputs)


def _split_refs(refs, n_in, n_out, n_scratch, rider):
    r_in = len(rider.inputs) if rider else 0
    r_out = len(rider.out_shape) if rider else 0
    a = n_in + r_in
    b = a + n_out + r_out
    own = refs[:n_in] + refs[a:a + n_out] + refs[b:b + n_scratch]
    return own, (refs[n_in:a], refs[a + n_out:b], refs[b + n_scratch:])


def _grid_step(grid):
    step, stride = 0, 1
    for axis in reversed(range(len(grid))):
        step = step + pl.program_id(axis) * stride
        stride *= grid[axis]
    return step


def _ride_begin(rider, ride_refs, grid):
    if rider is not None:
        pl.when(_grid_step(grid) == 0)(lambda: rider.start(ride_refs))


def _ride_end(rider, ride_refs, grid, forward_at=1.0):
    if rider is not None:
        steps = 1
        for n in grid:
            steps *= n
        step = _grid_step(grid)
        pl.when(step == min(steps - 1, int(steps * forward_at)))(lambda: rider.forward(ride_refs))
        pl.when(step == steps - 1)(lambda: rider.finish(ride_refs))


def _host(rider, in_specs, out_specs, out_shape, scratch, operands):
    if rider is None:
        return dict(in_specs=in_specs, out_specs=out_specs, out_shape=out_shape, scratch_shapes=scratch), operands
    return dict(in_specs=in_specs + [_ANY] * len(rider.inputs), out_specs=out_specs + [_ANY] * len(rider.out_shape),
                out_shape=out_shape + rider.out_shape, scratch_shapes=scratch + rider.scratch), operands + rider.inputs


_LATE = ("w_gate_up", "w_out", "w_down")


def _local_grads(x, target, w, late=None, exchange=False):
    w = dict(w)
    w_in = list(w["w_in"])
    dn_params = (w["dn_a_log"], w["dn_dt_bias"], w["dn_norm_w"])
    gla_params = (w["gla_w_gate2"], w["gla_gate_bias"], w["gla_norm_w"])
    sgu_params = (w["sgu_ln_w"], w["sgu_ln_b"], w["sgu_w_spatial"], w["sgu_b_spatial"])
    saved = []
    for l in range(DEPTH):
        riding = l == 0 and late is not None
        h, p, *got = _in_proj(x, w["norm1_w"], w_in[l], l, rider=_Gather([late["w_down"]], [True]) if riding else None)
        if riding:
            w["w_down"] = got[0].reshape(DEPTH, FF, D)
        ya = _sgu_fwd(p, *sgu_params, l)
        yb = _sconv_fwd(p, w["sc_conv_w"], l)
        qkv_c = _qkv_conv_fwd(p, w["dn_conv_w"], l)
        yc, st_c, inv_c, *got = _dn_fwd(
            qkv_c, p, dn_params, l, forward_at=0.75,
            rider=_Gather([late["w_gate_up"], late["w_out"]], [False, True]) if riding else None)
        if riding:
            w.update(w_gate_up=got[0], w_out=got[1].reshape(DEPTH, D, D))
        yd, st_d, *got = _gla_fwd(p, gla_params, l, rider=_Gather([late["w_in"]], [False]) if riding else None)
        if riding:
            w_in[1] = _relayout_w_in(got[0])
        mix, x1, h2, gu, act, x2 = _out_mlp(x, (ya, yb, yc, yd), w["w_out"], w["norm2_w"], w["w_gate_up"], w["w_down"], l)
        saved.append(dict(x=x, h=h, p=p, qkv_c=qkv_c, st_c=st_c, inv_c=inv_c, st_d=st_d, mix=mix, x1=x1, h2=h2, gu=gu,
                          act=act))
        x = x2
    loss, d_final, dx = _loss_head(x, w["final_norm_w"], target)
    large = {k: [None] * DEPTH for k in ("w_in", "w_out", "w_gate_up", "w_down")}
    small = [None] * DEPTH
    for l in reversed(range(DEPTH)):
        s = saved[l]
        p = s["p"]
        g = {}
        riding = [(k, 1) for k in _LATE] if exchange and l == 0 else []
        dgu, dx1, dmix, g["norm2_w"], *arrived = _mlp_out_bwd(
            dx, s["x1"], s["gu"], w["norm2_w"], w["w_down"], w["w_gate_up"], w["w_out"], l,
            rider=_Exchange([large[k][j] for k, j in riding]) if riding else None)
        for (k, j), a in zip(riding, arrived):
            large[k][j] = a
        large["w_gate_up"][l] = _wgrad_rows(dgu, s["h2"], "wgrad_gate_up").reshape(N_DEV, GU_SHARD, D)
        large["w_down"][l] = _wgrad_rows(s["act"], dx, "wgrad_down").reshape(N_DEV, FF // N_DEV, D)
        large["w_out"][l] = _matmul_tn(s["mix"], dx1, "wgrad_out", BF16).reshape(N_DEV, D // N_DEV, D)
        d_a, g["sgu_ln_w"], g["sgu_ln_b"], g["sgu_w_spatial"], g["sgu_b_spatial"] = _sgu_bwd(p, dmix, *sgu_params, l)
        d_b, g["sc_conv_w"] = _sconv_bwd(p, dmix, w["sc_conv_w"], l)
        riding = [("w_in", 1)] + [(k, 0) for k in _LATE] if exchange and l == 0 else []
        dqkv_c, dz_c, dab, g["dn_a_log"], g["dn_dt_bias"], g["dn_norm_w"], *arrived = _dn_bwd(
            s["qkv_c"], p, s["st_c"], s["inv_c"], dmix, dn_params, l,
            rider=_Exchange([large[k][j] for k, j in riding]) if riding else None)
        for (k, j), a in zip(riding, arrived):
            large[k][j] = a
        d_c, g["dn_conv_w"] = _qkv_conv_bwd(p, dqkv_c, w["dn_conv_w"], l)
        d_d, dz_d, dglr, g["gla_w_gate2"], g["gla_gate_bias"], g["gla_norm_w"] = _gla_bwd(p, s["st_d"], dmix, gla_params, l)
        dp, dx, g["norm1_w"] = _in_proj_bwd((d_c, d_d, d_a, d_b, dz_c, dz_d, dab, dglr), s["x"], dx1, w["norm1_w"],
                                            w_in[l], l)
        small[l] = g
        if exchange and l == 0:
            d_w_in, small = _matmul_tn(s["h"], dp, "wgrad_in", F32, rider=_Gather([_pack_small(small, d_final)], [False]))
        else:
            d_w_in = _matmul_tn(s["h"], dp, "wgrad_in", F32)
        large["w_in"][l] = _scatter_w_in_grad(d_w_in)
    return loss[0, 0], dx, large, small, d_final


_ORDER = ("norm1_w", "w_in", "sgu_ln_w", "sgu_ln_b", "sgu_w_spatial", "sgu_b_spatial", "sc_conv_w", "dn_conv_w",
          "dn_a_log", "dn_dt_bias", "dn_norm_w", "gla_w_gate2", "gla_gate_bias", "gla_norm_w", "w_out", "norm2_w",
          "w_gate_up", "w_down", "final_norm_w")
_LARGE = ("w_in", "w_gate_up", "w_out", "w_down")
_LARGE_TILE = {"w_in": 256, "w_gate_up": 352, "w_out": 128, "w_down": 352}


def _gather_cols(g):
    return jnp.transpose(g, (1, 2, 0, 3)).reshape(g.shape[1], g.shape[2], N_DEV * g.shape[3])


def kernel(x, norm1_w, w_in, sgu_ln_w, sgu_ln_b, sgu_w_spatial, sgu_b_spatial, sc_conv_w, dn_conv_w, dn_a_log, dn_dt_bias, dn_norm_w, gla_w_gate2, gla_gate_bias, gla_norm_w, w_out, norm2_w, w_gate_up, w_down, final_norm_w, loss_target, m_norm1_w, m_w_in, m_sgu_ln_w, m_sgu_ln_b, m_sgu_w_spatial, m_sgu_b_spatial, m_sc_conv_w, m_dn_conv_w, m_dn_a_log, m_dn_dt_bias, m_dn_norm_w, m_gla_w_gate2, m_gla_gate_bias, m_gla_norm_w, m_w_out, m_norm2_w, m_w_gate_up, m_w_down, m_final_norm_w, v_norm1_w, v_w_in, v_sgu_ln_w, v_sgu_ln_b, v_sgu_w_spatial, v_sgu_b_spatial, v_sc_conv_w, v_dn_conv_w, v_dn_a_log, v_dn_dt_bias, v_dn_norm_w, v_gla_w_gate2, v_gla_gate_bias, v_gla_norm_w, v_w_out, v_norm2_w, v_w_gate_up, v_w_down, v_final_norm_w):
    args = dict(locals())
    wts = {k: args[k] for k in _ORDER}
    mom = {k: args["m_" + k] for k in _ORDER}
    var = {k: args["v_" + k] for k in _ORDER}
    for d in (wts, mom, var):
        d["final_norm_w"] = d["final_norm_w"][None]
    me = 4 * lax.axis_index("x") + 2 * lax.axis_index("y") + lax.axis_index("c")
    for d in (wts, mom, var):
        d["w_gate_up"] = jnp.swapaxes(d["w_gate_up"], 1, 2)

    late = {k: wts[k].astype(BF16) for k in _LATE}
    w_in = wts["w_in"].astype(BF16)
    late["w_in"] = w_in[1]
    got = _run(_Gather([w_in[0]] + [wts[k] for k in _SHARDED_SMALL], [False] * 4), "gather_w_in")
    full = dict(wts)
    full["w_in"] = [_relayout_w_in(got[0]), None]
    for k, g in zip(_SHARDED_SMALL, got[1:]):
        full[k] = _gather_cols(g)

    loss, dx, large, small, d_final = _local_grads(x[0], loss_target[0], full, late=late, exchange=True)
    loss = lax.psum(loss, ("x", "y", "c"))

    large["w_in"][0], = _run(_Exchange([large["w_in"][0]]), "exchange_grads")
    result = {}
    for k in _LARGE:
        outs = _adamw_large(large[k], wts[k], mom[k], var[k], "adamw_" + k, _LARGE_TILE[k])
        for kind, a in zip(("grad", "delta", "new_m", "new_v"), outs):
            result[kind, k] = jnp.swapaxes(a, 1, 2) if k == "w_gate_up" else a

    slabs = small
    rep = _REPLICATED + ("final_norm_w",)
    outs, summed = _adamw_small(slabs, {k: wts[k] for k in rep}, {k: mom[k] for k in rep}, {k: var[k] for k in rep})
    for kind, d in zip(("grad", "delta", "new_m", "new_v"), outs):
        for k, a in d.items():
            result[kind, k] = a[0] if k == "final_norm_w" else a
    own = {k: lax.dynamic_slice_in_dim(summed[k], me * wts[k].shape[-1], wts[k].shape[-1], axis=2) for k in _SHARDED_SMALL}
    outs = _adamw_shards(own, {k: wts[k] for k in _SHARDED_SMALL}, {k: mom[k] for k in _SHARDED_SMALL},
                         {k: var[k] for k in _SHARDED_SMALL})
    for kind, d in zip(("grad", "delta", "new_m", "new_v"), [own] + outs):
        for k, a in d.items():
            result[kind, k] = a

    return (loss, dx[None], *[result[kind, k] for kind in ("grad", "delta", "new_m", "new_v") for k in _ORDER])
```

```python
import functools

import jax
import jax.numpy as jnp
from jax import lax
from jax.experimental import pallas as pl
from jax.experimental.pallas import tpu as pltpu

F32, BF16 = jnp.float32, jnp.bfloat16
DEPTH = 2
D = 1024
G = 256
NH, HD = 4, 64
FF = 2816
IN_COLS = 3352
P = 3584
EPS = 1e-6
SGU_C, DN_C, GLA_C = 128, 64, 64
N_DEV = 8
IN_SHARD = IN_COLS // N_DEV
GU_SHARD = 2 * FF // N_DEV
N_GATE = N_DEV // 2
LANES = 128
VMEM_LIMIT = 56 * 2 ** 20

_PAD_SEGS = ((1280, 2048, 0),
             (2312, 3080, 0),
             (0, 512, 0),
             (512, 1280, 0),
             (2056, 2312, 0),
             (3096, 3352, 0),
             (2048, 2056, 120),
             (3080, 3096, 112))

ADAM_LR, ADAM_B1, ADAM_B2, ADAM_EPS, ADAM_WD, ADAM_STEP = 0.001, 0.9, 0.999, 1e-08, 0.01, 10


def _cparams(*sem):
    return pltpu.CompilerParams(dimension_semantics=sem, vmem_limit_bytes=VMEM_LIMIT)


def _resident(shape):
    return pl.BlockSpec(shape, lambda *_: (0,) * len(shape), pipeline_mode=pl.Buffered(1))


def _layer(shape, l):
    return pl.BlockSpec((None,) + tuple(shape), lambda *_: (l,) + (0,) * len(shape), pipeline_mode=pl.Buffered(1))


def _acc(shape):
    return pl.BlockSpec(shape, lambda *_: (0,) * len(shape))


def _dg(a, b, ca, cb, precision=None):
    lead = a.ndim - 2
    batch = ((0,), (0,)) if lead else ((), ())
    return lax.dot_general(a, b, (((ca + lead,), (cb + lead,)), batch), preferred_element_type=F32, precision=precision)


def _make_dot(cast, precision):
    def raw(a, b, form):
        ca = 0 if form == "tn" else 1
        cb = 1 if form == "nt" else 0
        return _dg_any(cast(a), cast(b), ca, cb, precision)

    @functools.partial(jax.custom_vjp, nondiff_argnums=(2,))
    def dot(a, b, form):
        return raw(a, b, form)

    def fwd(a, b, form):
        return raw(a, b, form), (a, b)

    def bwd(form, res, g):
        a, b = res
        if form == "nn":
            return raw(g, b, "nt"), raw(a, g, "tn")
        if form == "nt":
            return raw(g, b, "nn"), raw(g, a, "tn")
        return raw(b, g, "nt"), raw(a, g, "nn")

    dot.defvjp(fwd, bwd)
    return dot


def _split(t):
    hi = t.astype(BF16)
    return hi, (t - hi.astype(F32)).astype(BF16)


def _dg3(a, b, ca, cb):
    (ah, al), (bh, bl) = a, b
    return _dg(ah, bh, ca, cb) + (_dg(ah, bl, ca, cb) + _dg(al, bh, ca, cb))


class _Split3:
    pass


def _dg_any(a, b, ca, cb, precision):
    if precision is _Split3:
        return _dg3(_split(a), _split(b), ca, cb)
    return _dg(a, b, ca, cb, precision)


_bdot = _make_dot(lambda t: t.astype(BF16), None)
_hdot = _make_dot(lambda t: t, _Split3)


def _inv_unit_lower(low):
    n = low.shape[-1]
    eye = (lax.broadcasted_iota(jnp.int32, (n, n), 0) == lax.broadcasted_iota(jnp.int32, (n, n), 1)).astype(F32)
    p = -low
    inv = eye + p
    ps = _split(p)
    k = 1
    while 2 * k < n:
        ps = _split(_dg3(ps, ps, 1, 0))
        inv = inv + _dg3(_split(inv), ps, 1, 0)
        k *= 2
    return inv


@jax.custom_vjp
def _unit_lower_solve(low, rhs, inv):
    return _dg3(_split(inv), _split(rhs), 1, 0)


def _uls_fwd(low, rhs, inv):
    sol = _dg3(_split(inv), _split(rhs), 1, 0)
    return sol, (inv, sol)


def _uls_bwd(res, g):
    inv, sol = res
    d_rhs = _dg3(_split(inv), _split(g), 0, 0)
    return -_dg3(_split(d_rhs), _split(sol), 1, 1), d_rhs, jnp.zeros_like(inv)


_unit_lower_solve.defvjp(_uls_fwd, _uls_bwd)


def _sigmoid(x):
    return 1.0 / (1.0 + jnp.exp(-x))


def _softplus(x):
    return jnp.maximum(x, 0.0) + jnp.log(1.0 + jnp.exp(-jnp.maximum(x, -x)))


def _gelu(x):
    return 0.5 * x * (1.0 + jnp.tanh(0.7978845608028654 * (x + 0.044715 * (x * x * x))))


def _tri(n):
    ri = lax.broadcasted_iota(jnp.int32, (n, n), 0)
    ci = lax.broadcasted_iota(jnp.int32, (n, n), 1)
    return ri, ci


def _stack(ts):
    return jnp.concatenate([t[None] for t in ts], axis=0)


@jax.custom_vjp
def _head_sums(x):
    ri, ci = _tri(x.shape[-1])
    shift = HD.bit_length() - 1
    ones = (jnp.right_shift(ri, shift) == jnp.right_shift(ci, shift)).astype(BF16)
    hi, lo = _split(x)
    return _dg(hi, ones, 1, 0) + _dg(lo, ones, 1, 0)


_head_sums.defvjp(lambda x: (_head_sums(x), None), lambda _, g: (_head_sums(g),))


def _chunk_mask_dot(x, c, running, transpose):
    ri, ci = _tri(x.shape[0])
    shift = c.bit_length() - 1
    mask = jnp.right_shift(ri, shift) == jnp.right_shift(ci, shift)
    if running:
        mask = mask & (ri >= ci)
    hi, lo = _split(x)
    m = mask.astype(BF16)
    ca = 0 if transpose else 1
    return _dg(m, hi, ca, 0) + _dg(m, lo, ca, 0)


@functools.partial(jax.custom_vjp, nondiff_argnums=(1, 2))
def _chunk_sums(x, c, running):
    return _chunk_mask_dot(x, c, running, False)


_chunk_sums.defvjp(lambda x, c, running: (_chunk_mask_dot(x, c, running, False), None),
                   lambda c, running, _, g: (_chunk_mask_dot(g, c, running, True),))


def _spread_onto(x, first, transpose):
    ri = lax.broadcasted_iota(jnp.int32, (LANES, G), 0)
    ci = lax.broadcasted_iota(jnp.int32, (LANES, G), 1)
    sel = (ri == first + jnp.right_shift(ci, HD.bit_length() - 1)).astype(BF16)
    hi, lo = _split(x)
    cb = 1 if transpose else 0
    return _dg(hi, sel, 1, cb) + _dg(lo, sel, 1, cb)


@functools.partial(jax.custom_vjp, nondiff_argnums=(1,))
def _spread(x, first):
    return _spread_onto(x, first, False)


_spread.defvjp(lambda x, first: (_spread_onto(x, first, False), None),
               lambda first, _, g: (_spread_onto(g, first, True),))


def _sgu_chunk(uv, ln_w, ln_b, ws, bs):
    u = _gelu(uv[:, :G])
    v = _gelu(uv[:, G:])
    mu = jnp.mean(v, axis=-1, keepdims=True)
    vc = v - mu
    var = jnp.mean(vc * vc, axis=-1, keepdims=True)
    vn = vc * lax.rsqrt(var + EPS) * ln_w + ln_b
    ri, ci = _tri(SGU_C)
    bs_t = _hdot((ri == ci).astype(F32), bs, "nt")
    mixed = []
    for h in range(NH):
        wm = jnp.where(ri >= ci, ws[h], 0.0)
        mixed.append(_bdot(wm, vn[:, HD * h:HD * (h + 1)], "nn") + bs_t[:, h:h + 1])
    return u * jnp.concatenate(mixed, axis=1)


def _dn_tile(q, k, v, ab, z, state, inv, a_log, dt_bias, nw):
    c = DN_C
    tm = q.shape[0]
    cps = tm // c
    ri, ci = _tri(tm)
    shift = c.bit_length() - 1
    same = jnp.right_shift(ri, shift) == jnp.right_shift(ci, shift)
    per_head = lambda t: jnp.concatenate([jnp.broadcast_to(t[:, h:h + 1], (1, HD)) for h in range(NH)], axis=1)
    g = -jnp.exp(per_head(a_log)) * _softplus(_spread(ab, 0) + per_head(dt_bias))
    beta = _sigmoid(_spread(ab, NH))
    gc = _chunk_sums(g, c, True)
    gl = _chunk_sums(g, c, False)
    g4 = -jnp.exp(a_log) * _softplus(ab[:, 0:NH] + dt_bias)
    gr4 = _hdot(g4, (same & (ri <= ci)).astype(F32), "tn")
    pairs = [(slice(c * j, c * (j + 1)), h) for j in range(cps) for h in range(NH)]
    heads = lambda t: _stack([t[rows, HD * h:HD * (h + 1)] for rows, h in pairs])
    qn = q * lax.rsqrt(_head_sums(q * q) + EPS) * (HD ** -0.5)
    kn = k * lax.rsqrt(_head_sums(k * k) + EPS)
    eg = jnp.exp(gc)
    kb = kn * beta
    rhs = jnp.concatenate([heads(v * beta), heads(kb * eg)], axis=2)
    qg, kd = heads(qn * eg), heads(kn * jnp.exp(gl - gc))
    cd = _stack([jnp.exp(gl[c * j:c * j + 1, HD * h:HD * (h + 1)]) for j in range(cps) for h in range(NH)])
    qn, kn, kb, gc = heads(qn), heads(kn), heads(kb), heads(gc)
    gr = _stack([gr4[h:h + 1, rows] for rows, h in pairs])
    r2, c2 = _tri(c)
    decay = jnp.exp(jnp.where(r2 >= c2, gc - gr, -jnp.inf))
    low = jnp.where(r2 > c2, _bdot(kb, kn, "nt") * decay, 0.0)
    if inv is None:
        inv = _inv_unit_lower(low)
    sol = _unit_lower_solve(low, rhs, inv)
    u, w = sol[:, :, :HD], sol[:, :, HD:]
    attn = _bdot(qn, kn, "nt") * decay
    ys = []
    for j in range(cps):
        b = slice(NH * j, NH * (j + 1))
        v_new = u[b] - _bdot(w[b], state, "nn")
        o = _bdot(qg[b], state, "nn") + _bdot(attn[b], v_new, "nn")
        state = state * cd[b] + _bdot(kd[b], v_new, "tn")
        on = o * lax.rsqrt(jnp.mean(o * o, axis=-1, keepdims=True) + EPS) * nw
        ys.append(jnp.concatenate([on[h] for h in range(NH)], axis=1))
    return jnp.concatenate(ys, axis=0) * (z * _sigmoid(z)), state, inv


def _gla_tile(q, k, v, glr, z, state_t, w2, gate_bias, nw):
    c = GLA_C
    tm = q.shape[0]
    cps = tm // c
    w2_rows = jnp.concatenate([w2, jnp.zeros((LANES - w2.shape[0], G), F32)], axis=0)
    pre = _bdot(glr, w2_rows, "nn") + gate_bias
    log_a = (jnp.minimum(pre, 0.0) - jnp.log(1.0 + jnp.exp(-jnp.maximum(pre, -pre)))) * (1.0 / 16.0)
    gcum = _chunk_sums(log_a, c, True)
    row = lax.broadcasted_iota(jnp.int32, (c, G), 0)
    qs = q * (HD ** -0.5)
    qa, ka, qg, kl, dec = [], [], [], [], []
    for j in range(cps):
        rows = slice(c * j, c * (j + 1))
        gj = gcum[rows]
        g_mid = jnp.sum(jnp.where(row == c // 2, gj, 0.0), axis=0, keepdims=True)
        g_last = jnp.sum(log_a[rows], axis=0, keepdims=True)
        qa.append(qs[rows] * jnp.exp(gj - g_mid))
        ka.append(k[rows] * jnp.exp(g_mid - gj))
        qg.append(qs[rows] * jnp.exp(gj))
        kl.append(k[rows] * jnp.exp(g_last - gj))
        dec.append(jnp.exp(g_last))
    heads = lambda ts: _stack([t[:, HD * h:HD * (h + 1)] for t in ts for h in range(NH)])
    qa, ka, qg, kl, dec = heads(qa), heads(ka), heads(qg), heads(kl), heads(dec)
    vh = heads([v[c * j:c * (j + 1)] for j in range(cps)])
    r2, c2 = _tri(c)
    o_intra = _bdot(jnp.where(r2 >= c2, _bdot(qa, ka, "nt"), 0.0), vh, "nn")
    ys = []
    for j in range(cps):
        b = slice(NH * j, NH * (j + 1))
        o = o_intra[b] + _bdot(qg[b], state_t, "nt")
        state_t = state_t * dec[b] + _bdot(vh[b], kl[b], "tn")
        on = o * lax.rsqrt(jnp.mean(o * o, axis=-1, keepdims=True) + EPS) * nw
        ys.append(jnp.concatenate([on[h] for h in range(NH)], axis=1))
    return jnp.concatenate(ys, axis=0) * (z * _sigmoid(z)), state_t


def _rms(x, nw):
    r = lax.rsqrt(jnp.mean(x * x, axis=-1, keepdims=True) + EPS)
    xh = x * r
    return xh * nw, xh, r


def _rms_bwd(g, xh, r, nw):
    gw = g * nw
    return r * (gw - xh * jnp.mean(gw * xh, axis=-1, keepdims=True)), jnp.sum(g * xh, axis=0, keepdims=True)


def _row(tm, w, blk=0):
    return pl.BlockSpec((tm, w), lambda i: (i, blk))


def _in_proj(x, nw, wp, l, tm=512, rider=None):
    t = x.shape[0]

    def body(*refs):
        (x_ref, nw_ref, w_ref, h_ref, p_ref), ride = _split_refs(refs, 3, 2, 0, rider)
        _ride_begin(rider, ride, (t // tm,))
        h = _rms(x_ref[...], nw_ref[l:l + 1, :])[0].astype(BF16)
        h_ref[...] = h
        p_ref[...] = jnp.dot(h, w_ref[...], preferred_element_type=F32)
        _ride_end(rider, ride, (t // tm,))

    specs, operands = _host(
        rider, [_row(tm, D), _resident((DEPTH, D)), _resident((D, P))], [_row(tm, D), _row(tm, P)],
        [jax.ShapeDtypeStruct((t, D), BF16), jax.ShapeDtypeStruct((t, P), F32)], [], [x, nw, wp])
    return pl.pallas_call(body, name="in_proj", grid=(t // tm,), compiler_params=_cparams("arbitrary"),
                          **specs)(*operands)


def _gu_spec(l):
    return pl.BlockSpec((N_DEV, None, GU_SHARD, D), lambda *_: (0, l, 0, 0), pipeline_mode=pl.Buffered(1))


def _out_mlp(x, ys, w_out, nw2, w_gu_t, w_down, l, tm=256):
    t = x.shape[0]

    def body(x_ref, ya, yb, yc, yd, wo_ref, nw_ref, wgu_ref, wd_ref, mix_ref, x1_ref, h2_ref, gu_ref, act_ref, x2_ref):
        mix = jnp.concatenate([ya[...], yb[...], yc[...], yd[...]], axis=1)
        mix_ref[...] = mix
        x1 = x_ref[...] + jnp.dot(mix, wo_ref[...], preferred_element_type=F32)
        x1_ref[...] = x1
        h2 = _rms(x1, nw_ref[l:l + 1, :])[0].astype(BF16)
        h2_ref[...] = h2
        gu = _dg(h2, wgu_ref[...].reshape(2 * FF, D), 1, 1)
        gu_ref[...] = gu.astype(BF16)
        gate, up = gu[:, :FF], gu[:, FF:]
        act = (gate * _sigmoid(gate) * up).astype(BF16)
        act_ref[...] = act
        x2_ref[...] = x1 + jnp.dot(act, wd_ref[...], preferred_element_type=F32)

    return pl.pallas_call(
        body, name="out_mlp", grid=(t // tm,),
        in_specs=[_row(tm, D), _row(tm, G), _row(tm, G), _row(tm, G), _row(tm, G), _layer((D, D), l),
                  _resident((DEPTH, D)), _gu_spec(l), _layer((FF, D), l)],
        out_specs=[_row(tm, D), _row(tm, D), _row(tm, D), _row(tm, 2 * FF), _row(tm, FF), _row(tm, D)],
        out_shape=[jax.ShapeDtypeStruct((t, D), BF16), jax.ShapeDtypeStruct((t, D), F32),
                   jax.ShapeDtypeStruct((t, D), BF16), jax.ShapeDtypeStruct((t, 2 * FF), BF16),
                   jax.ShapeDtypeStruct((t, FF), BF16), jax.ShapeDtypeStruct((t, D), F32)],
        compiler_params=_cparams("parallel"))(x, *ys, w_out, nw2, w_gu_t, w_down)


def _loss_head(x, nw, target, tm=512):
    t = x.shape[0]

    def body(x_ref, nw_ref, tg_ref, loss_ref, dnw_ref, dx_ref):
        @pl.when(pl.program_id(0) == 0)
        def _():
            loss_ref[...] = jnp.zeros_like(loss_ref)
            dnw_ref[...] = jnp.zeros_like(dnw_ref)
        nw_v = nw_ref[...]
        y, xh, r = _rms(x_ref[...], nw_v)
        err = y - tg_ref[...]
        loss_ref[...] += 0.5 * jnp.sum(err * err) * (1.0 / D)
        dx, dnw = _rms_bwd(err * (1.0 / D), xh, r, nw_v)
        dx_ref[...] = dx
        dnw_ref[...] += dnw

    return pl.pallas_call(
        body, name="loss_head", grid=(t // tm,),
        in_specs=[_row(tm, D), _resident((1, D)), _row(tm, D)],
        out_specs=[_acc((8, LANES)), _acc((1, D)), _row(tm, D)],
        out_shape=[jax.ShapeDtypeStruct((8, LANES), F32), jax.ShapeDtypeStruct((1, D), F32),
                   jax.ShapeDtypeStruct((t, D), F32)],
        compiler_params=_cparams("arbitrary"))(x, nw, target)


def _mlp_out_bwd(dx2, x1, gu, nw2, w_down, w_gu, w_out, l, tm=256, rider=None):
    t = dx2.shape[0]

    def body(*refs):
        (dx2_ref, x1_ref, gu_ref, nw_ref, wd_ref, wgu_ref, wo_ref, dgu_ref, dx1_ref, dmix_ref, dnw_ref), ride = \
            _split_refs(refs, 7, 4, 0, rider)
        _ride_begin(rider, ride, (t // tm,))

        @pl.when(pl.program_id(0) == 0)
        def _():
            dnw_ref[...] = jnp.zeros_like(dnw_ref)
        dx2 = dx2_ref[...]
        dact = _dg(dx2.astype(BF16), wd_ref[...], 1, 1)
        gu = gu_ref[...].astype(F32)
        gate, up = gu[:, :FF], gu[:, FF:]
        s = _sigmoid(gate)
        dgu = jnp.concatenate([dact * up * (s * (1.0 + gate * (1.0 - s))), dact * (gate * s)], axis=1).astype(BF16)
        dgu_ref[...] = dgu
        dh2 = jnp.dot(dgu, wgu_ref[...].reshape(2 * FF, D), preferred_element_type=F32)
        nw_v = nw_ref[l:l + 1, :]
        _, xh, r = _rms(x1_ref[...], nw_v)
        dxn, dnw = _rms_bwd(dh2, xh, r, nw_v)
        dx1 = dx2 + dxn
        dx1_ref[...] = dx1
        dnw_ref[...] += dnw
        dmix_ref[...] = _dg(dx1.astype(BF16), wo_ref[...], 1, 1)
        _ride_end(rider, ride, (t // tm,))

    specs, operands = _host(
        rider, [_row(tm, D), _row(tm, D), _row(tm, 2 * FF), _resident((DEPTH, D)), _layer((FF, D), l), _gu_spec(l),
                _layer((D, D), l)],
        [_row(tm, 2 * FF), _row(tm, D), _row(tm, D), _acc((1, D))],
        [jax.ShapeDtypeStruct((t, 2 * FF), BF16), jax.ShapeDtypeStruct((t, D), F32),
         jax.ShapeDtypeStruct((t, D), F32), jax.ShapeDtypeStruct((1, D), F32)],
        [], [dx2, x1, gu, nw2, w_down, w_gu, w_out])
    return pl.pallas_call(body, name="mlp_out_bwd", grid=(t // tm,), compiler_params=_cparams("arbitrary"),
                          **specs)(*operands)


_DP_WIDTHS = (768, 768, 512, 768, 256, 256, 128, 128)


def _in_proj_bwd(dps, x, dx1, nw, wp, l, tm=512):
    t = x.shape[0]

    def body(*refs):
        dp_refs, (x_ref, dx1_ref, nw_ref, w_ref, dp_ref, dx_ref, dnw_ref) = refs[:8], refs[8:]

        @pl.when(pl.program_id(0) == 0)
        def _():
            dnw_ref[...] = jnp.zeros_like(dnw_ref)
        dp = jnp.concatenate([r[...] for r in dp_refs], axis=1)
        dp_ref[...] = dp
        dh = _dg(dp, w_ref[...], 1, 1)
        nw_v = nw_ref[l:l + 1, :]
        _, xh, r = _rms(x_ref[...], nw_v)
        dxn, dnw = _rms_bwd(dh, xh, r, nw_v)
        dx_ref[...] = dx1_ref[...] + dxn
        dnw_ref[...] += dnw

    return pl.pallas_call(
        body, name="in_proj_bwd", grid=(t // tm,),
        in_specs=[_row(tm, w) for w in _DP_WIDTHS] + [_row(tm, D), _row(tm, D), _resident((DEPTH, D)), _resident((D, P))],
        out_specs=[_row(tm, P), _row(tm, D), _acc((1, D))],
        out_shape=[jax.ShapeDtypeStruct((t, P), BF16), jax.ShapeDtypeStruct((t, D), F32),
                   jax.ShapeDtypeStruct((1, D), F32)],
        compiler_params=_cparams("arbitrary"))(*dps, x, dx1, nw, wp)


def _accumulate(acc, o_ref, t_axis, term):
    @pl.when(pl.program_id(t_axis) == 0)
    def _():
        acc[...] = jnp.zeros_like(acc)
    acc[...] += term

    @pl.when(pl.program_id(t_axis) == pl.num_programs(t_axis) - 1)
    def _():
        o_ref[...] = acc[...].astype(o_ref.dtype)


WGRAD_TOKENS = 2048


def _matmul_tn(a, b, name, out_dtype, tn=512, tt=WGRAD_TOKENS, rider=None):
    t, m = a.shape
    n = b.shape[1]
    tt = min(tt, t)
    grid = (n // tn, t // tt)

    def body(*refs):
        (a_ref, b_ref, o_ref, acc), ride = _split_refs(refs, 2, 1, 1, rider)
        _ride_begin(rider, ride, grid)
        _accumulate(acc, o_ref, 1, _dg(a_ref[...].astype(BF16), b_ref[...].astype(BF16), 0, 0))
        _ride_end(rider, ride, grid)

    specs, operands = _host(
        rider, [pl.BlockSpec((tt, m), lambda j, i: (i, 0)), pl.BlockSpec((tt, tn), lambda j, i: (i, j))],
        [pl.BlockSpec((m, tn), lambda j, i: (0, j))], [jax.ShapeDtypeStruct((m, n), out_dtype)],
        [pltpu.VMEM((m, tn), F32)], [a, b])
    out = pl.pallas_call(body, name=name, grid=grid, compiler_params=_cparams("arbitrary", "arbitrary"), **specs)(*operands)
    return out[0] if rider is None else out


WGRAD_ROWS = 2 * GU_SHARD


def _wgrad_rows(a, b, name, tt=WGRAD_TOKENS, rider=None):
    t, m = a.shape
    tt = min(tt, t)
    grid = (m // WGRAD_ROWS, t // tt)

    def body(*refs):
        (a_ref, b_ref, o_ref, acc), ride = _split_refs(refs, 2, 1, 1, rider)
        _ride_begin(rider, ride, grid)
        _accumulate(acc, o_ref, 1, _dg(a_ref[...], b_ref[...].astype(BF16), 0, 0))
        _ride_end(rider, ride, grid)

    specs, operands = _host(
        rider, [pl.BlockSpec((tt, WGRAD_ROWS), lambda j, i: (i, j)), pl.BlockSpec((tt, D), lambda j, i: (i, 0))],
        [pl.BlockSpec((WGRAD_ROWS, D), lambda j, i: (j, 0))], [jax.ShapeDtypeStruct((m, D), BF16)],
        [pltpu.VMEM((WGRAD_ROWS, D), F32)], [a, b])
    out = pl.pallas_call(body, name=name, grid=grid, compiler_params=_cparams("arbitrary", "arbitrary"), **specs)(*operands)
    return out[0] if rider is None else out


def _relayout_w_in(g, tr=256):
    def body(w_ref, o_ref):
        full = jnp.concatenate([w_ref[d] for d in range(N_DEV)], axis=1)
        parts = []
        for a, b, z in _PAD_SEGS:
            parts.append(full[:, a:b])
            if z:
                parts.append(jnp.zeros((tr, z), full.dtype))
        o_ref[...] = jnp.concatenate(parts, axis=1)

    return pl.pallas_call(
        body, name="relayout_w_in", grid=(D // tr,),
        in_specs=[pl.BlockSpec((N_DEV, tr, IN_SHARD), lambda i: (0, i, 0))], out_specs=_row(tr, P),
        out_shape=jax.ShapeDtypeStruct((D, P), g.dtype), compiler_params=_cparams("parallel"))(g)


def _scatter_w_in_grad(gp, tr=256):
    def body(g_ref, o_ref):
        g = g_ref[...]
        pieces, off = [], 0
        for a, b, z in _PAD_SEGS:
            pieces.append((a, g[:, off:off + (b - a)]))
            off += (b - a) + z
        nat = jnp.concatenate([piece for _, piece in sorted(pieces, key=lambda ap: ap[0])], axis=1)
        for d in range(N_DEV):
            o_ref[d] = nat[:, IN_SHARD * d:IN_SHARD * (d + 1)].astype(BF16)

    return pl.pallas_call(
        body, name="scatter_w_in_grad", grid=(D // tr,),
        in_specs=[pl.BlockSpec((tr, P), lambda i: (i, 0))],
        out_specs=pl.BlockSpec((N_DEV, tr, IN_SHARD), lambda i: (0, i, 0)),
        out_shape=jax.ShapeDtypeStruct((N_DEV, D, IN_SHARD), BF16),
        compiler_params=_cparams("parallel"))(gp)


_A_BLK = 3


def _sgu_specs(l):
    return [_resident((DEPTH, G)), _resident((DEPTH, G)), _layer((NH, SGU_C, SGU_C), l), _layer((NH, SGU_C), l)]


def _sgu_fwd(p, ln_w, ln_b, ws, bs, l, tm=256):
    t = p.shape[0]

    def body(uv_ref, lw_ref, lb_ref, ws_ref, bs_ref, y_ref):
        ws_v = [ws_ref[h] for h in range(NH)]
        for c in range(tm // SGU_C):
            rows = pl.ds(c * SGU_C, SGU_C)
            y_ref[rows, :] = _sgu_chunk(uv_ref[rows, :], lw_ref[l:l + 1, :], lb_ref[l:l + 1, :], ws_v,
                                        bs_ref[...]).astype(BF16)

    return pl.pallas_call(
        body, name="sgu_fwd", grid=(t // tm,),
        in_specs=[_row(tm, 2 * G, _A_BLK)] + _sgu_specs(l), out_specs=_row(tm, G),
        out_shape=jax.ShapeDtypeStruct((t, G), BF16),
        compiler_params=_cparams("parallel"))(p, ln_w, ln_b, ws, bs)


def _sgu_bwd(p, dmix, ln_w, ln_b, ws, bs, l, tm=256):
    t = p.shape[0]

    def body(uv_ref, dy_ref, lw_ref, lb_ref, ws_ref, bs_ref, duv_ref, dlw_ref, dlb_ref, dws_ref, dbs_ref):
        @pl.when(pl.program_id(0) == 0)
        def _():
            for r in (dlw_ref, dlb_ref, dws_ref, dbs_ref):
                r[...] = jnp.zeros_like(r)
        ws_v = [ws_ref[h] for h in range(NH)]
        for c in range(tm // SGU_C):
            rows = pl.ds(c * SGU_C, SGU_C)
            _, vjp = jax.vjp(_sgu_chunk, uv_ref[rows, :], lw_ref[l:l + 1, :], lb_ref[l:l + 1, :], ws_v, bs_ref[...])
            duv, dlw, dlb, dws, dbs = vjp(dy_ref[rows, :])
            duv_ref[rows, :] = duv.astype(BF16)
            dlw_ref[...] += dlw
            dlb_ref[...] += dlb
            dbs_ref[...] += dbs
            for h in range(NH):
                dws_ref[h] += dws[h]

    return pl.pallas_call(
        body, name="sgu_bwd", grid=(t // tm,),
        in_specs=[_row(tm, 2 * G, _A_BLK), _row(tm, G, 0)] + _sgu_specs(l),
        out_specs=[_row(tm, 2 * G), _acc((1, G)), _acc((1, G)), _acc((NH, SGU_C, SGU_C)), _acc((NH, SGU_C))],
        out_shape=[jax.ShapeDtypeStruct((t, 2 * G), BF16), jax.ShapeDtypeStruct((1, G), F32),
                   jax.ShapeDtypeStruct((1, G), F32), jax.ShapeDtypeStruct((NH, SGU_C, SGU_C), F32),
                   jax.ShapeDtypeStruct((NH, SGU_C), F32)],
        compiler_params=_cparams("arbitrary"))(p, dmix, ln_w, ln_b, ws, bs)


HALO = 8


def _prev_halo(tm, width, col):
    return pl.BlockSpec((HALO, width), lambda i: (jnp.maximum(i * (tm // HALO) - 1, 0), col))


def _next_halo(tm, width, col, t):
    return pl.BlockSpec((HALO, width), lambda i: (jnp.minimum((i + 1) * (tm // HALO), t // HALO - 1), col))


def _with_prev(halo_ref, x_ref):
    halo = jnp.where(pl.program_id(0) == 0, 0.0, halo_ref[...])
    return jnp.concatenate([halo, x_ref[...]], axis=0)


def _with_next(x_ref, halo_ref):
    halo = jnp.where(pl.program_id(0) == pl.num_programs(0) - 1, 0.0, halo_ref[...])
    return jnp.concatenate([x_ref[...], halo], axis=0)


def _delay(ext, j):
    return ext if j == 0 else pltpu.roll(ext, j, axis=0)


def _advance(ext, j):
    return ext if j == 0 else pltpu.roll(ext, ext.shape[0] - j, axis=0)


def _causal_conv(ext, w, tm):
    kw = w.shape[0]
    out = None
    for k in range(kw):
        term = _delay(ext, kw - 1 - k)[HALO:HALO + tm] * w[k:k + 1, :]
        out = term if out is None else out + term
    return out


def _causal_conv_t(ext, w, tm):
    kw = w.shape[0]
    out = None
    for k in range(kw):
        term = _advance(ext, kw - 1 - k)[0:tm] * w[k:k + 1, :]
        out = term if out is None else out + term
    return out


def _conv_wgrad(ext, dy, kw, tm):
    return jnp.concatenate(
        [jnp.sum(_delay(ext, kw - 1 - k)[HALO:HALO + tm] * dy, axis=0, keepdims=True) for k in range(kw)], axis=0)


_B_GB, _B_GC, _B_H = 8, 9, 10
_C_QKV, _D_QKV = 0, 1
_C_Z, _D_Z = 11, 12
_C_AB, _D_GLR = 26, 27


def _sconv_fwd(p, w, l, tm=512):
    t = p.shape[0]

    def body(gb_ref, gc_ref, h_ref, gc_halo, h_halo, w_ref, y_ref):
        s_ext = _with_prev(gc_halo, gc_ref) * _with_prev(h_halo, h_ref)
        y_ref[...] = (gb_ref[...] * _causal_conv(s_ext, w_ref[...], tm)).astype(BF16)

    return pl.pallas_call(
        body, name="sconv_fwd", grid=(t // tm,),
        in_specs=[_row(tm, G, _B_GB), _row(tm, G, _B_GC), _row(tm, G, _B_H), _prev_halo(tm, G, _B_GC),
                  _prev_halo(tm, G, _B_H), _layer((3, G), l)],
        out_specs=_row(tm, G), out_shape=jax.ShapeDtypeStruct((t, G), BF16),
        compiler_params=_cparams("parallel"))(p, p, p, p, p, w)


def _sconv_bwd(p, dmix, w, l, tm=512):
    t = p.shape[0]

    def body(gb_ref, gc_ref, h_ref, gc_halo, h_halo, gb_next, dy_ref, dy_next, w_ref, dp_ref, dw_ref):
        @pl.when(pl.program_id(0) == 0)
        def _():
            dw_ref[...] = jnp.zeros_like(dw_ref)
        w_v = w_ref[...]
        s_ext = _with_prev(gc_halo, gc_ref) * _with_prev(h_halo, h_ref)
        dout = dy_ref[...]
        d_gb = dout * _causal_conv(s_ext, w_v, tm)
        dconv_ext = _with_next(dy_ref, dy_next) * _with_next(gb_ref, gb_next)
        ds = _causal_conv_t(dconv_ext, w_v, tm)
        dw_ref[...] += _conv_wgrad(s_ext, dconv_ext[0:tm], 3, tm)
        dp_ref[...] = jnp.concatenate([d_gb, ds * h_ref[...], ds * gc_ref[...]], axis=1).astype(BF16)

    return pl.pallas_call(
        body, name="sconv_bwd", grid=(t // tm,),
        in_specs=[_row(tm, G, _B_GB), _row(tm, G, _B_GC), _row(tm, G, _B_H), _prev_halo(tm, G, _B_GC),
                  _prev_halo(tm, G, _B_H), _next_halo(tm, G, _B_GB, t), _row(tm, G, 1), _next_halo(tm, G, 1, t),
                  _layer((3, G), l)],
        out_specs=[_row(tm, 3 * G), _acc((3, G))],
        out_shape=[jax.ShapeDtypeStruct((t, 3 * G), BF16), jax.ShapeDtypeStruct((3, G), F32)],
        compiler_params=_cparams("arbitrary"))(p, p, p, p, p, p, dmix, dmix, w)


def _qkv_conv_fwd(p, w, l, tm=512):
    t = p.shape[0]

    def body(x_ref, x_halo, w_ref, y_ref):
        c = _causal_conv(_with_prev(x_halo, x_ref), w_ref[...], tm)
        y_ref[...] = c * _sigmoid(c)

    return pl.pallas_call(
        body, name="qkv_conv_fwd", grid=(t // tm,),
        in_specs=[_row(tm, 3 * G, _C_QKV), _prev_halo(tm, 3 * G, _C_QKV), _layer((4, 3 * G), l)],
        out_specs=_row(tm, 3 * G), out_shape=jax.ShapeDtypeStruct((t, 3 * G), F32),
        compiler_params=_cparams("parallel"))(p, p, w)


def _qkv_conv_bwd(p, dy, w, l, tm=512):
    t = p.shape[0]

    def body(x_ref, x_halo, x_next, dy_ref, dy_next, w_ref, dx_ref, dw_ref):
        @pl.when(pl.program_id(0) == 0)
        def _():
            dw_ref[...] = jnp.zeros_like(dw_ref)
        w_v = w_ref[...]
        x_all = jnp.concatenate([_with_prev(x_halo, x_ref), jnp.where(
            pl.program_id(0) == pl.num_programs(0) - 1, 0.0, x_next[...])], axis=0)
        c = _causal_conv(x_all, w_v, tm + HALO)
        s = _sigmoid(c)
        dc_ext = _with_next(dy_ref, dy_next) * (s * (1.0 + c * (1.0 - s)))
        dx_ref[...] = _causal_conv_t(dc_ext, w_v, tm).astype(BF16)
        dw_ref[...] += _conv_wgrad(x_all[0:HALO + tm], dc_ext[0:tm], 4, tm)

    return pl.pallas_call(
        body, name="qkv_conv_bwd", grid=(t // tm,),
        in_specs=[_row(tm, 3 * G, _C_QKV), _prev_halo(tm, 3 * G, _C_QKV), _next_halo(tm, 3 * G, _C_QKV, t),
                  _row(tm, 3 * G), _next_halo(tm, 3 * G, 0, t), _layer((4, 3 * G), l)],
        out_specs=[_row(tm, 3 * G), _acc((4, 3 * G))],
        out_shape=[jax.ShapeDtypeStruct((t, 3 * G), BF16), jax.ShapeDtypeStruct((4, 3 * G), F32)],
        compiler_params=_cparams("arbitrary"))(p, p, p, dy, dy, w)


_STATE = pl.BlockSpec((1, NH, HD, HD), lambda i: (i, 0, 0, 0))


def _dn_specs():
    return [_resident((DEPTH, NH)), _resident((DEPTH, NH)), _resident((DEPTH, HD))]


def _dn_fwd(qkv, p, params, l, cps=4, rider=None, forward_at=1.0):
    t = qkv.shape[0]
    tm = DN_C * cps
    nb = cps * NH

    def body(*refs):
        (qkv_ref, z_ref, ab_ref, alog_ref, dt_ref, nw_ref, y_ref, st_ref, inv_ref, state), ride = _split_refs(
            refs, 6, 3, 1, rider)
        _ride_begin(rider, ride, (t // tm,))

        @pl.when(pl.program_id(0) == 0)
        def _():
            state[...] = jnp.zeros_like(state)
        st_ref[0] = state[...]
        y, new, inv = _dn_tile(qkv_ref[:, 0:G], qkv_ref[:, G:2 * G], qkv_ref[:, 2 * G:3 * G], ab_ref[...], z_ref[...],
                               state[...], None, alog_ref[l:l + 1, :], dt_ref[l:l + 1, :], nw_ref[l:l + 1, :])
        y_ref[...] = y.astype(BF16)
        inv_ref[...] = inv
        state[...] = new
        _ride_end(rider, ride, (t // tm,), forward_at)

    specs, operands = _host(
        rider, [_row(tm, 3 * G), _row(tm, G, _C_Z), _row(tm, LANES, _C_AB)] + _dn_specs(),
        [_row(tm, G), _STATE, pl.BlockSpec((nb, DN_C, DN_C), lambda i: (i, 0, 0))],
        [jax.ShapeDtypeStruct((t, G), BF16), jax.ShapeDtypeStruct((t // tm, NH, HD, HD), F32),
         jax.ShapeDtypeStruct((t // DN_C * NH, DN_C, DN_C), F32)],
        [pltpu.VMEM((NH, HD, HD), F32)], [qkv, p, p, *params])
    return pl.pallas_call(body, name="dn_fwd", grid=(t // tm,), compiler_params=_cparams("arbitrary"), **specs)(*operands)


def _dn_bwd(qkv, p, states, inv, dmix, params, l, cps=4, rider=None):
    t = qkv.shape[0]
    tm = DN_C * cps
    n = t // tm
    nb = cps * NH

    def body(*refs):
        (qkv_ref, z_ref, ab_ref, st_ref, inv_ref, dy_ref, alog_ref, dt_ref, nw_ref,
         dqkv_ref, dz_ref, dab_ref, dalog_ref, ddt_ref, dnw_ref, dstate), ride = _split_refs(refs, 9, 6, 1, rider)
        _ride_begin(rider, ride, (n,))
        dprm_refs = (dalog_ref, ddt_ref, dnw_ref)

        @pl.when(pl.program_id(0) == 0)
        def _():
            dstate[...] = jnp.zeros_like(dstate)
            for r in dprm_refs:
                r[...] = jnp.zeros_like(r)
        inv_v = inv_ref[...]
        tile = lambda q, k, v, ab, z, st, alog, dt, nw: _dn_tile(q, k, v, ab, z, st, inv_v, alog, dt, nw)[:2]
        _, vjp = jax.vjp(tile, qkv_ref[:, 0:G], qkv_ref[:, G:2 * G], qkv_ref[:, 2 * G:3 * G], ab_ref[...], z_ref[...],
                         st_ref[0], alog_ref[l:l + 1, :], dt_ref[l:l + 1, :], nw_ref[l:l + 1, :])
        dq, dk, dv, dab, dz, dst, *dprm = vjp((dy_ref[...], dstate[...]))
        dqkv_ref[...] = jnp.concatenate([dq, dk, dv], axis=1)
        dz_ref[...] = dz.astype(BF16)
        dab_ref[...] = dab.astype(BF16)
        dstate[...] = dst
        for r, g in zip(dprm_refs, dprm):
            r[...] += g
        _ride_end(rider, ride, (n,))

    rev = lambda w, blk: pl.BlockSpec((tm, w), lambda i: (n - 1 - i, blk))
    specs, operands = _host(
        rider, [rev(3 * G, 0), rev(G, _C_Z), rev(LANES, _C_AB),
                pl.BlockSpec((1, NH, HD, HD), lambda i: (n - 1 - i, 0, 0, 0)),
                pl.BlockSpec((nb, DN_C, DN_C), lambda i: (n - 1 - i, 0, 0)), rev(G, 2)] + _dn_specs(),
        [rev(3 * G, 0), rev(G, 0), rev(LANES, 0), _acc((1, NH)), _acc((1, NH)), _acc((1, HD))],
        [jax.ShapeDtypeStruct((t, 3 * G), F32), jax.ShapeDtypeStruct((t, G), BF16),
         jax.ShapeDtypeStruct((t, LANES), BF16), jax.ShapeDtypeStruct((1, NH), F32),
         jax.ShapeDtypeStruct((1, NH), F32), jax.ShapeDtypeStruct((1, HD), F32)],
        [pltpu.VMEM((NH, HD, HD), F32)], [qkv, p, p, states, inv, dmix, *params])
    return pl.pallas_call(body, name="dn_bwd", grid=(n,), compiler_params=_cparams("arbitrary"), **specs)(*operands)


def _gla_specs(l):
    return [_layer((16, G), l), _resident((DEPTH, G)), _resident((DEPTH, HD))]


def _gla_fwd(p, params, l, cps=4, rider=None):
    t = p.shape[0]
    tm = GLA_C * cps

    def body(*refs):
        (qkv_ref, z_ref, glr_ref, w2_ref, gb_ref, nw_ref, y_ref, st_ref, state), ride = _split_refs(refs, 6, 2, 1, rider)
        _ride_begin(rider, ride, (t // tm,))

        @pl.when(pl.program_id(0) == 0)
        def _():
            state[...] = jnp.zeros_like(state)
        st_ref[0] = state[...]
        y, new = _gla_tile(qkv_ref[:, 0:G], qkv_ref[:, G:2 * G], qkv_ref[:, 2 * G:3 * G], glr_ref[...], z_ref[...],
                           state[...], w2_ref[...], gb_ref[l:l + 1, :], nw_ref[l:l + 1, :])
        y_ref[...] = y.astype(BF16)
        state[...] = new
        _ride_end(rider, ride, (t // tm,))

    specs, operands = _host(
        rider, [_row(tm, 3 * G, _D_QKV), _row(tm, G, _D_Z), _row(tm, LANES, _D_GLR)] + _gla_specs(l),
        [_row(tm, G), _STATE],
        [jax.ShapeDtypeStruct((t, G), BF16), jax.ShapeDtypeStruct((t // tm, NH, HD, HD), F32)],
        [pltpu.VMEM((NH, HD, HD), F32)], [p, p, p, *params])
    return pl.pallas_call(body, name="gla_fwd", grid=(t // tm,), compiler_params=_cparams("arbitrary"), **specs)(*operands)


def _gla_bwd(p, states, dmix, params, l, cps=4):
    t = p.shape[0]
    tm = GLA_C * cps
    n = t // tm

    def body(qkv_ref, z_ref, glr_ref, st_ref, dy_ref, w2_ref, gb_ref, nw_ref,
             dqkv_ref, dz_ref, dglr_ref, dw2_ref, dgb_ref, dnw_ref, dstate):
        dprm_refs = (dw2_ref, dgb_ref, dnw_ref)

        @pl.when(pl.program_id(0) == 0)
        def _():
            dstate[...] = jnp.zeros_like(dstate)
            for r in dprm_refs:
                r[...] = jnp.zeros_like(r)
        _, vjp = jax.vjp(_gla_tile, qkv_ref[:, 0:G], qkv_ref[:, G:2 * G], qkv_ref[:, 2 * G:3 * G], glr_ref[...],
                         z_ref[...], st_ref[0], w2_ref[...], gb_ref[l:l + 1, :], nw_ref[l:l + 1, :])
        dq, dk, dv, dglr, dz, dst, *dprm = vjp((dy_ref[...], dstate[...]))
        dqkv_ref[...] = jnp.concatenate([dq, dk, dv], axis=1).astype(BF16)
        dz_ref[...] = dz.astype(BF16)
        dglr_ref[...] = dglr.astype(BF16)
        dstate[...] = dst
        for r, g in zip(dprm_refs, dprm):
            r[...] += g

    rev = lambda w, blk: pl.BlockSpec((tm, w), lambda i: (n - 1 - i, blk))
    return pl.pallas_call(
        body, name="gla_bwd", grid=(n,),
        in_specs=[rev(3 * G, _D_QKV), rev(G, _D_Z), rev(LANES, _D_GLR),
                  pl.BlockSpec((1, NH, HD, HD), lambda i: (n - 1 - i, 0, 0, 0)), rev(G, 3)] + _gla_specs(l),
        out_specs=[rev(3 * G, 0), rev(G, 0), rev(LANES, 0), _acc((16, G)), _acc((1, G)), _acc((1, HD))],
        out_shape=[jax.ShapeDtypeStruct((t, 3 * G), BF16), jax.ShapeDtypeStruct((t, G), BF16),
                   jax.ShapeDtypeStruct((t, LANES), BF16), jax.ShapeDtypeStruct((16, G), F32),
                   jax.ShapeDtypeStruct((1, G), F32), jax.ShapeDtypeStruct((1, HD), F32)],
        scratch_shapes=[pltpu.VMEM((NH, HD, HD), F32)],
        compiler_params=_cparams("arbitrary"))(p, p, p, states, dmix, *params)


def _adamw_math(w, g, m, v):
    m = ADAM_B1 * m + (1.0 - ADAM_B1) * g
    v = ADAM_B2 * v + (1.0 - ADAM_B2) * (g * g)
    m_hat = m / (1.0 - ADAM_B1 ** ADAM_STEP)
    v_hat = v / (1.0 - ADAM_B2 ** ADAM_STEP)
    return -ADAM_LR * (m_hat / (jnp.sqrt(v_hat) + ADAM_EPS) + ADAM_WD * w), m, v


def _adamw_large(parts, w, m, v, name, tr):
    _, r, c = w.shape

    def body(p0_ref, p1_ref, w_ref, m_ref, v_ref, g_ref, d_ref, nm_ref, nv_ref):
        def total(p_ref):
            g = p_ref[0].astype(F32)
            for k in range(1, N_DEV):
                g = g + p_ref[k].astype(F32)
            return g

        g = jnp.where(pl.program_id(0) == 0, total(p0_ref), total(p1_ref))
        g_ref[...] = g
        d_ref[...], nm_ref[...], nv_ref[...] = _adamw_math(w_ref[...], g, m_ref[...], v_ref[...])

    blk = pl.BlockSpec((None, tr, c), lambda l, i: (l, i, 0))
    part = pl.BlockSpec((N_DEV, tr, c), lambda l, i: (0, i, 0))
    return pl.pallas_call(
        body, name=name, grid=(DEPTH, r // tr), in_specs=[part, part, blk, blk, blk],
        out_specs=[blk] * 4, out_shape=[jax.ShapeDtypeStruct(w.shape, F32)] * 4,
        compiler_params=_cparams("parallel", "parallel"))(*parts, w, m, v)


_SLAB_AT = {
    "sgu_w_spatial": (0, 0, SGU_C, LANES),
    "sgu_b_spatial": (128, 0, NH, SGU_C),
    "norm1_w": (132, 0, 1, D),
    "norm2_w": (133, 0, 1, D),
    "sgu_ln_w": (134, 0, 1, G),
    "sgu_ln_b": (134, 256, 1, G),
    "gla_gate_bias": (134, 512, 1, G),
    "dn_norm_w": (134, 768, 1, HD),
    "gla_norm_w": (134, 896, 1, HD),
    "dn_a_log": (135, 0, 1, NH),
    "dn_dt_bias": (135, 128, 1, NH),
    "gla_w_gate2": (136, 0, 16, G),
    "dn_conv_w": (136, 256, 4, 3 * G),
    "sc_conv_w": (140, 256, 3, G),
}
_LAYER_ROWS = 152
_FINAL_ROW = DEPTH * _LAYER_ROWS
_SLAB_ROWS, _SLAB_COLS = _FINAL_ROW + 8, 1024
_SLAB_ORDER = tuple(_SLAB_AT)
_SHARDED_SMALL = ("sc_conv_w", "dn_conv_w", "gla_w_gate2")
_REPLICATED = tuple(k for k in _SLAB_ORDER if k not in _SHARDED_SMALL)


def _region(name, l, h=0):
    r0, c0, nr, nc = _SLAB_AT[name]
    return slice(_LAYER_ROWS * l + r0, _LAYER_ROWS * l + r0 + nr), slice(c0 + nc * h, c0 + nc * (h + 1))


def _pack_small(layer_grads, d_final):
    def body(*refs):
        slab = refs[-1]
        slab[...] = jnp.zeros_like(slab)
        it = iter(refs[:-1])
        for l in range(DEPTH):
            for name in _SLAB_ORDER:
                ref = next(it)
                if name == "sgu_w_spatial":
                    for h in range(NH):
                        slab[_region(name, l, h)] = ref[h]
                else:
                    slab[_region(name, l)] = ref[...]
        slab[_FINAL_ROW:_FINAL_ROW + 1, :] = next(it)[...]

    flat = [layer_grads[l][name] for l in range(DEPTH) for name in _SLAB_ORDER] + [d_final]
    return pl.pallas_call(body, name="pack_small_grads", out_shape=jax.ShapeDtypeStruct((_SLAB_ROWS, _SLAB_COLS), F32),
                          compiler_params=_cparams())(*flat)


def _adamw_small(parts, w, m, v):
    names = _REPLICATED + ("final_norm_w",)
    n_in = len(names)

    def body(*refs):
        def total(rows, cols):
            g = refs[0][0, rows, cols]
            for k in range(1, N_DEV):
                g = g + refs[0][k, rows, cols]
            return g

        w_refs = dict(zip(names, refs[1:1 + n_in]))
        m_refs = dict(zip(names, refs[1 + n_in:1 + 2 * n_in]))
        v_refs = dict(zip(names, refs[1 + 2 * n_in:1 + 3 * n_in]))
        outs = refs[1 + 3 * n_in:]
        out_refs = [dict(zip(names, outs[j * n_in:(j + 1) * n_in])) for j in range(4)]
        full_refs = dict(zip(_SHARDED_SMALL, outs[4 * n_in:]))

        def update(name, idx, g):
            res = (g,) + _adamw_math(w_refs[name][idx], g, m_refs[name][idx], v_refs[name][idx])
            for o, val in zip(out_refs, res):
                o[name][idx] = val

        for l in range(DEPTH):
            for name in _REPLICATED:
                if name == "sgu_w_spatial":
                    for h in range(NH):
                        update(name, (l, h), total(*_region(name, l, h)))
                elif name == "sgu_b_spatial":
                    update(name, (l,), total(*_region(name, l)))
                else:
                    update(name, (slice(l, l + 1), slice(None)), total(*_region(name, l)))
            for name in _SHARDED_SMALL:
                full_refs[name][l] = total(*_region(name, l))
        update("final_norm_w", (slice(None), slice(None)), total(slice(_FINAL_ROW, _FINAL_ROW + 1), slice(None)))

    shapes = [jax.ShapeDtypeStruct(w[k].shape, F32) for k in names]
    full_shapes = [jax.ShapeDtypeStruct((DEPTH,) + _SLAB_AT[k][2:], F32) for k in _SHARDED_SMALL]
    outs = pl.pallas_call(body, name="adamw_small", out_shape=shapes * 4 + full_shapes, compiler_params=_cparams())(
        parts, *[w[k] for k in names], *[m[k] for k in names], *[v[k] for k in names])
    result = [dict(zip(names, outs[j * n_in:(j + 1) * n_in])) for j in range(4)]
    return result, dict(zip(_SHARDED_SMALL, outs[4 * n_in:]))


def _adamw_shards(g, w, m, v):
    names = _SHARDED_SMALL
    n = len(names)

    def body(*refs):
        for j in range(n):
            g_v = refs[j][...]
            res = _adamw_math(refs[n + j][...], g_v, refs[2 * n + j][...], refs[3 * n + j][...])
            for o, val in zip(refs[4 * n + j::n], res):
                o[...] = val

    shapes = [jax.ShapeDtypeStruct(w[k].shape, F32) for k in names]
    outs = pl.pallas_call(body, name="adamw_small_shards", out_shape=shapes * 3, compiler_params=_cparams())(
        *[g[k] for k in names], *[w[k] for k in names], *[m[k] for k in names], *[v[k] for k in names])
    return [dict(zip(names, outs[j * n:(j + 1) * n])) for j in range(3)]


_ANY = pl.BlockSpec(memory_space=pl.ANY)


def _place():
    return lax.axis_index("x"), lax.axis_index("y"), lax.axis_index("c")


def _sems(n):
    return [pltpu.SemaphoreType.DMA((n, 7)), pltpu.SemaphoreType.DMA((n, 7)), pltpu.SemaphoreType.DMA((n,))]


class _Gather:
    def __init__(self, blocks, second):
        self.inputs, self.second = list(blocks), list(second)
        self.out_shape = [jax.ShapeDtypeStruct((a.shape[0], N_DEV) + a.shape[1:] if s else (N_DEV,) + a.shape, a.dtype)
                          for a, s in zip(blocks, second)]
        self.scratch = _sems(len(blocks))

    def _copies(self, refs):
        x_refs, out_refs, (send_sems, recv_sems, local_sems) = refs
        n = len(x_refs)
        x, y, c = _place()
        me, sibling = (x, y, c), (x, y, 1 - c)
        chips = [(1 - x, y), (x, 1 - y), (1 - x, 1 - y)]

        def slot(j, px, py, pc):
            idx = 4 * px + 2 * py + pc
            return out_refs[j].at[:, idx] if self.second[j] else out_refs[j].at[idx]

        def copies(k, block, to, own=False):
            return [pltpu.make_async_remote_copy(
                src_ref=x_refs[j] if own else slot(j, *block), dst_ref=slot(j, *block),
                send_sem=send_sems.at[j, k], recv_sem=recv_sems.at[j, k], device_id=to,
                device_id_type=pl.DeviceIdType.MESH) for j in range(n)]

        mine = [pltpu.make_async_copy(x_refs[j], slot(j, *me), local_sems.at[j]) for j in range(n)]
        first = copies(0, me, sibling, own=True)
        for j, chip in enumerate(chips):
            first += copies(1 + j, me, (*chip, c), own=True)
        landed = [copies(1 + j, (*chip, c), me) for j, chip in enumerate(chips)]
        onward = [copies(4 + j, (*chip, c), sibling) for j, chip in enumerate(chips)]
        from_sibling = copies(0, sibling, me)
        for j, chip in enumerate(chips):
            from_sibling += copies(4 + j, (*chip, 1 - c), me)
        return mine, first, landed, onward, from_sibling

    def start(self, refs):
        mine, first, _, _, _ = self._copies(refs)
        for cp in mine + first:
            cp.start()

    def forward(self, refs):
        _, _, landed, onward, _ = self._copies(refs)
        for arrived, passed in zip(landed, onward):
            for cp in arrived:
                cp.wait_recv()
            for cp in passed:
                cp.start()

    def finish(self, refs):
        mine, first, _, onward, from_sibling = self._copies(refs)
        for cp in from_sibling:
            cp.wait_recv()
        for cp in first + [cp for passed in onward for cp in passed]:
            cp.wait_send()
        for cp in mine:
            cp.wait()


class _Exchange:
    def __init__(self, sends):
        self.inputs = list(sends)
        self.out_shape = [jax.ShapeDtypeStruct(a.shape, a.dtype) for a in sends]
        self.scratch = _sems(len(sends))

    def _copies(self, refs):
        x_refs, out_refs, (send_sems, recv_sems, local_sems) = refs
        n = len(x_refs)
        x, y, c = _place()
        me = 4 * x + 2 * y + c
        sent, landing = [], []
        for k in range(1, N_DEV):
            px, py, pc = x ^ ((k >> 2) & 1), y ^ ((k >> 1) & 1), c ^ (k & 1)
            them = 4 * px + 2 * py + pc
            for j in range(n):
                for here, there, into in ((them, me, sent), (me, them, landing)):
                    into.append(pltpu.make_async_remote_copy(
                        src_ref=x_refs[j].at[here], dst_ref=out_refs[j].at[there], send_sem=send_sems.at[j, k - 1],
                        recv_sem=recv_sems.at[j, k - 1], device_id=(px, py, pc), device_id_type=pl.DeviceIdType.MESH))
        mine = [pltpu.make_async_copy(x_refs[j].at[me], out_refs[j].at[me], local_sems.at[j]) for j in range(n)]
        return mine, sent, landing

    def start(self, refs):
        mine, sent, _ = self._copies(refs)
        for cp in mine + sent:
            cp.start()

    def forward(self, refs):
        pass

    def finish(self, refs):
        mine, sent, landing = self._copies(refs)
        for cp in landing:
            cp.wait_recv()
        for cp in sent:
            cp.wait_send()
        for cp in mine:
            cp.wait()


def _run(rider, name):
    n_in, n_out = len(rider.inputs), len(rider.out_shape)

    def body(*refs):
        parts = (refs[:n_in], refs[n_in:n_in + n_out], refs[n_in + n_out:])
        rider.start(parts)
        rider.forward(parts)
        rider.finish(parts)

    return pl.pallas_call(body, name=name, out_shape=rider.out_shape, in_specs=[_ANY] * n_in, out_specs=[_ANY] * n_out,
                          scratch_shapes=rider.scratch)(*rider.inputs)


def _split_refs(refs, n_in, n_out, n_scratch, rider):
    r_in = len(rider.inputs) if rider else 0
    r_out = len(rider.out_shape) if rider else 0
    a = n_in + r_in
    b = a + n_out + r_out
    own = refs[:n_in] + refs[a:a + n_out] + refs[b:b + n_scratch]
    return own, (refs[n_in:a], refs[a + n_out:b], refs[b + n_scratch:])


def _grid_step(grid):
    step, stride = 0, 1
    for axis in reversed(range(len(grid))):
        step = step + pl.program_id(axis) * stride
        stride *= grid[axis]
    return step


def _ride_begin(rider, ride_refs, grid):
    if rider is not None:
        pl.when(_grid_step(grid) == 0)(lambda: rider.start(ride_refs))


def _ride_end(rider, ride_refs, grid, forward_at=1.0):
    if rider is not None:
        steps = 1
        for n in grid:
            steps *= n
        step = _grid_step(grid)
        pl.when(step == min(steps - 1, int(steps * forward_at)))(lambda: rider.forward(ride_refs))
        pl.when(step == steps - 1)(lambda: rider.finish(ride_refs))


def _host(rider, in_specs, out_specs, out_shape, scratch, operands):
    if rider is None:
        return dict(in_specs=in_specs, out_specs=out_specs, out_shape=out_shape, scratch_shapes=scratch), operands
    return dict(in_specs=in_specs + [_ANY] * len(rider.inputs), out_specs=out_specs + [_ANY] * len(rider.out_shape),
                out_shape=out_shape + rider.out_shape, scratch_shapes=scratch + rider.scratch), operands + rider.inputs


_LATE = ("w_gate_up", "w_out", "w_down")


def _local_grads(x, target, w, late=None, exchange=False):
    w = dict(w)
    w_in = list(w["w_in"])
    dn_params = (w["dn_a_log"], w["dn_dt_bias"], w["dn_norm_w"])
    gla_params = (w["gla_w_gate2"], w["gla_gate_bias"], w["gla_norm_w"])
    sgu_params = (w["sgu_ln_w"], w["sgu_ln_b"], w["sgu_w_spatial"], w["sgu_b_spatial"])
    saved = []
    for l in range(DEPTH):
        riding = l == 0 and late is not None
        h, p, *got = _in_proj(x, w["norm1_w"], w_in[l], l, rider=_Gather([late["w_down"]], [True]) if riding else None)
        if riding:
            w["w_down"] = got[0].reshape(DEPTH, FF, D)
        ya = _sgu_fwd(p, *sgu_params, l)
        yb = _sconv_fwd(p, w["sc_conv_w"], l)
        qkv_c = _qkv_conv_fwd(p, w["dn_conv_w"], l)
        yc, st_c, inv_c, *got = _dn_fwd(
            qkv_c, p, dn_params, l, forward_at=0.85,
            rider=_Gather([late["w_gate_up"], late["w_out"]], [False, True]) if riding else None)
        if riding:
            w.update(w_gate_up=got[0], w_out=got[1].reshape(DEPTH, D, D))
        yd, st_d, *got = _gla_fwd(p, gla_params, l, rider=_Gather([late["w_in"]], [False]) if riding else None)
        if riding:
            w_in[1] = _relayout_w_in(got[0])
        mix, x1, h2, gu, act, x2 = _out_mlp(x, (ya, yb, yc, yd), w["w_out"], w["norm2_w"], w["w_gate_up"], w["w_down"], l)
        saved.append(dict(x=x, h=h, p=p, qkv_c=qkv_c, st_c=st_c, inv_c=inv_c, st_d=st_d, mix=mix, x1=x1, h2=h2, gu=gu,
                          act=act))
        x = x2
    loss, d_final, dx = _loss_head(x, w["final_norm_w"], target)
    large = {k: [None] * DEPTH for k in ("w_in", "w_out", "w_gate_up", "w_down")}
    small = [None] * DEPTH
    for l in reversed(range(DEPTH)):
        s = saved[l]
        p = s["p"]
        g = {}
        riding = exchange and l == 0

        def exchanging(pairs):
            return _Exchange([large[k][j] for k, j in pairs]) if riding else None

        def arrive(pairs, arrived):
            for (k, j), a in zip(pairs, arrived):
                large[k][j] = a

        pairs = (("w_gate_up", 1), ("w_down", 1))
        dgu, dx1, dmix, g["norm2_w"], *arrived = _mlp_out_bwd(
            dx, s["x1"], s["gu"], w["norm2_w"], w["w_down"], w["w_gate_up"], w["w_out"], l, rider=exchanging(pairs))
        arrive(pairs, arrived)
        pairs = (("w_in", 1), ("w_out", 1))
        d_gu, *arrived = _wgrad_rows(dgu, s["h2"], "wgrad_gate_up", rider=exchanging(pairs)) if riding else (
            _wgrad_rows(dgu, s["h2"], "wgrad_gate_up"),)
        arrive(pairs, arrived)
        large["w_gate_up"][l] = d_gu.reshape(N_DEV, GU_SHARD, D)
        large["w_down"][l] = _wgrad_rows(s["act"], dx, "wgrad_down").reshape(N_DEV, FF // N_DEV, D)
        large["w_out"][l] = _matmul_tn(s["mix"], dx1, "wgrad_out", BF16).reshape(N_DEV, D // N_DEV, D)
        d_a, g["sgu_ln_w"], g["sgu_ln_b"], g["sgu_w_spatial"], g["sgu_b_spatial"] = _sgu_bwd(p, dmix, *sgu_params, l)
        d_b, g["sc_conv_w"] = _sconv_bwd(p, dmix, w["sc_conv_w"], l)
        pairs = tuple((k, 0) for k in _LATE)
        dqkv_c, dz_c, dab, g["dn_a_log"], g["dn_dt_bias"], g["dn_norm_w"], *arrived = _dn_bwd(
            s["qkv_c"], p, s["st_c"], s["inv_c"], dmix, dn_params, l, rider=exchanging(pairs))
        arrive(pairs, arrived)
        d_c, g["dn_conv_w"] = _qkv_conv_bwd(p, dqkv_c, w["dn_conv_w"], l)
        d_d, dz_d, dglr, g["gla_w_gate2"], g["gla_gate_bias"], g["gla_norm_w"] = _gla_bwd(p, s["st_d"], dmix, gla_params, l)
        dp, dx, g["norm1_w"] = _in_proj_bwd((d_c, d_d, d_a, d_b, dz_c, dz_d, dab, dglr), s["x"], dx1, w["norm1_w"],
                                            w_in[l], l)
        small[l] = g
        if exchange and l == 0:
            d_w_in, small = _matmul_tn(s["h"], dp, "wgrad_in", F32, rider=_Gather([_pack_small(small, d_final)], [False]))
        else:
            d_w_in = _matmul_tn(s["h"], dp, "wgrad_in", F32)
        large["w_in"][l] = _scatter_w_in_grad(d_w_in)
    return loss[0, 0], dx, large, small, d_final


_ORDER = ("norm1_w", "w_in", "sgu_ln_w", "sgu_ln_b", "sgu_w_spatial", "sgu_b_spatial", "sc_conv_w", "dn_conv_w",
          "dn_a_log", "dn_dt_bias", "dn_norm_w", "gla_w_gate2", "gla_gate_bias", "gla_norm_w", "w_out", "norm2_w",
          "w_gate_up", "w_down", "final_norm_w")
_LARGE = ("w_in", "w_gate_up", "w_out", "w_down")
_LARGE_TILE = {"w_in": 256, "w_gate_up": 352, "w_out": 128, "w_down": 352}


def _gather_cols(g):
    return jnp.transpose(g, (1, 2, 0, 3)).reshape(g.shape[1], g.shape[2], N_DEV * g.shape[3])


def kernel(x, norm1_w, w_in, sgu_ln_w, sgu_ln_b, sgu_w_spatial, sgu_b_spatial, sc_conv_w, dn_conv_w, dn_a_log, dn_dt_bias, dn_norm_w, gla_w_gate2, gla_gate_bias, gla_norm_w, w_out, norm2_w, w_gate_up, w_down, final_norm_w, loss_target, m_norm1_w, m_w_in, m_sgu_ln_w, m_sgu_ln_b, m_sgu_w_spatial, m_sgu_b_spatial, m_sc_conv_w, m_dn_conv_w, m_dn_a_log, m_dn_dt_bias, m_dn_norm_w, m_gla_w_gate2, m_gla_gate_bias, m_gla_norm_w, m_w_out, m_norm2_w, m_w_gate_up, m_w_down, m_final_norm_w, v_norm1_w, v_w_in, v_sgu_ln_w, v_sgu_ln_b, v_sgu_w_spatial, v_sgu_b_spatial, v_sc_conv_w, v_dn_conv_w, v_dn_a_log, v_dn_dt_bias, v_dn_norm_w, v_gla_w_gate2, v_gla_gate_bias, v_gla_norm_w, v_w_out, v_norm2_w, v_w_gate_up, v_w_down, v_final_norm_w):
    args = dict(locals())
    wts = {k: args[k] for k in _ORDER}
    mom = {k: args["m_" + k] for k in _ORDER}
    var = {k: args["v_" + k] for k in _ORDER}
    for d in (wts, mom, var):
        d["final_norm_w"] = d["final_norm_w"][None]
    me = 4 * lax.axis_index("x") + 2 * lax.axis_index("y") + lax.axis_index("c")
    for d in (wts, mom, var):
        d["w_gate_up"] = jnp.swapaxes(d["w_gate_up"], 1, 2)

    late = {k: wts[k].astype(BF16) for k in _LATE}
    w_in = wts["w_in"].astype(BF16)
    late["w_in"] = w_in[1]
    got = _run(_Gather([w_in[0]] + [wts[k] for k in _SHARDED_SMALL], [False] * 4), "gather_w_in")
    full = dict(wts)
    full["w_in"] = [_relayout_w_in(got[0]), None]
    for k, g in zip(_SHARDED_SMALL, got[1:]):
        full[k] = _gather_cols(g)

    loss, dx, large, small, d_final = _local_grads(x[0], loss_target[0], full, late=late, exchange=True)
    loss = lax.psum(loss, ("x", "y", "c"))

    large["w_in"][0], = _run(_Exchange([large["w_in"][0]]), "exchange_grads")
    result = {}
    for k in _LARGE:
        outs = _adamw_large(large[k], wts[k], mom[k], var[k], "adamw_" + k, _LARGE_TILE[k])
        for kind, a in zip(("grad", "delta", "new_m", "new_v"), outs):
            result[kind, k] = jnp.swapaxes(a, 1, 2) if k == "w_gate_up" else a

    slabs = small
    rep = _REPLICATED + ("final_norm_w",)
    outs, summed = _adamw_small(slabs, {k: wts[k] for k in rep}, {k: mom[k] for k in rep}, {k: var[k] for k in rep})
    for kind, d in zip(("grad", "delta", "new_m", "new_v"), outs):
        for k, a in d.items():
            result[kind, k] = a[0] if k == "final_norm_w" else a
    own = {k: lax.dynamic_slice_in_dim(summed[k], me * wts[k].shape[-1], wts[k].shape[-1], axis=2) for k in _SHARDED_SMALL}
    outs = _adamw_shards(own, {k: wts[k] for k in _SHARDED_SMALL}, {k: mom[k] for k in _SHARDED_SMALL},
                         {k: var[k] for k in _SHARDED_SMALL})
    for kind, d in zip(("grad", "delta", "new_m", "new_v"), [own] + outs):
        for k, a in d.items():
            result[kind, k] = a

    return (loss, dx[None], *[result[kind, k] for kind in ("grad", "delta", "new_m", "new_v") for k in _ORDER])
```

```python
import functools

import jax
import jax.numpy as jnp
from jax import lax
from jax.experimental import pallas as pl
from jax.experimental.pallas import tpu as pltpu

F32, BF16 = jnp.float32, jnp.bfloat16
DEPTH = 2
D = 1024
G = 256
NH, HD = 4, 64
FF = 2816
IN_COLS = 3352
P = 3584
EPS = 1e-6
SGU_C, DN_C, GLA_C = 128, 64, 64
N_DEV = 8
IN_SHARD = IN_COLS // N_DEV
GU_SHARD = 2 * FF // N_DEV
LANES = 128
VMEM_LIMIT = 56 * 2 ** 20

_PAD_SEGS = ((1280, 2048, 0),
             (2312, 3080, 0),
             (0, 512, 0),
             (512, 1280, 0),
             (2056, 2312, 0),
             (3096, 3352, 0),
             (2048, 2056, 120),
             (3080, 3096, 112))

ADAM_LR, ADAM_B1, ADAM_B2, ADAM_EPS, ADAM_WD, ADAM_STEP = 0.001, 0.9, 0.999, 1e-08, 0.01, 10


def _cparams(*sem):
    return pltpu.CompilerParams(dimension_semantics=sem, vmem_limit_bytes=VMEM_LIMIT)


def _resident(shape):
    return pl.BlockSpec(shape, lambda *_: (0,) * len(shape), pipeline_mode=pl.Buffered(1))


def _layer(shape, l):
    return pl.BlockSpec((None,) + tuple(shape), lambda *_: (l,) + (0,) * len(shape), pipeline_mode=pl.Buffered(1))


def _acc(shape):
    return pl.BlockSpec(shape, lambda *_: (0,) * len(shape))


def _dg(a, b, ca, cb):
    lead = a.ndim - 2
    batch = ((0,), (0,)) if lead else ((), ())
    return lax.dot_general(a, b, (((ca + lead,), (cb + lead,)), batch), preferred_element_type=F32)


def _split(t):
    hi = t.astype(BF16)
    return hi, (t - hi.astype(F32)).astype(BF16)


def _dg3(a, b, ca, cb):
    (ah, al), (bh, bl) = a, b
    return _dg(ah, bh, ca, cb) + (_dg(ah, bl, ca, cb) + _dg(al, bh, ca, cb))


def _make_dot(accurate):
    def raw(a, b, form):
        ca = 0 if form == "tn" else 1
        cb = 1 if form == "nt" else 0
        if accurate:
            return _dg3(_split(a), _split(b), ca, cb)
        return _dg(a.astype(BF16), b.astype(BF16), ca, cb)

    @functools.partial(jax.custom_vjp, nondiff_argnums=(2,))
    def dot(a, b, form):
        return raw(a, b, form)

    def fwd(a, b, form):
        return raw(a, b, form), (a, b)

    def bwd(form, res, g):
        a, b = res
        if form == "nn":
            return raw(g, b, "nt"), raw(a, g, "tn")
        if form == "nt":
            return raw(g, b, "nn"), raw(g, a, "tn")
        return raw(b, g, "nt"), raw(a, g, "nn")

    dot.defvjp(fwd, bwd)
    return dot


_bdot = _make_dot(False)
_hdot = _make_dot(True)


def _inv_unit_lower(low):
    n = low.shape[-1]
    eye = (lax.broadcasted_iota(jnp.int32, (n, n), 0) == lax.broadcasted_iota(jnp.int32, (n, n), 1)).astype(F32)
    p = -low
    inv = eye + p
    ps = _split(p)
    k = 1
    while 2 * k < n:
        ps = _split(_dg3(ps, ps, 1, 0))
        inv = inv + _dg3(_split(inv), ps, 1, 0)
        k *= 2
    return inv


@jax.custom_vjp
def _unit_lower_solve(low, rhs, inv):
    return _dg3(_split(inv), _split(rhs), 1, 0)


def _uls_fwd(low, rhs, inv):
    sol = _dg3(_split(inv), _split(rhs), 1, 0)
    return sol, (inv, sol)


def _uls_bwd(res, g):
    inv, sol = res
    d_rhs = _dg3(_split(inv), _split(g), 0, 0)
    return -_dg3(_split(d_rhs), _split(sol), 1, 1), d_rhs, jnp.zeros_like(inv)


_unit_lower_solve.defvjp(_uls_fwd, _uls_bwd)


def _sigmoid(x):
    return 1.0 / (1.0 + jnp.exp(-x))


def _softplus(x):
    return jnp.maximum(x, 0.0) + jnp.log(1.0 + jnp.exp(-jnp.maximum(x, -x)))


def _gelu(x):
    return 0.5 * x * (1.0 + jnp.tanh(0.7978845608028654 * (x + 0.044715 * (x * x * x))))


def _tri(n):
    ri = lax.broadcasted_iota(jnp.int32, (n, n), 0)
    ci = lax.broadcasted_iota(jnp.int32, (n, n), 1)
    return ri, ci


def _stack(ts):
    return jnp.concatenate([t[None] for t in ts], axis=0)


@jax.custom_vjp
def _head_sums(x):
    ri, ci = _tri(x.shape[-1])
    shift = HD.bit_length() - 1
    ones = (jnp.right_shift(ri, shift) == jnp.right_shift(ci, shift)).astype(BF16)
    hi, lo = _split(x)
    return _dg(hi, ones, 1, 0) + _dg(lo, ones, 1, 0)


_head_sums.defvjp(lambda x: (_head_sums(x), None), lambda _, g: (_head_sums(g),))


def _chunk_mask_dot(x, c, running, transpose):
    ri, ci = _tri(x.shape[0])
    shift = c.bit_length() - 1
    mask = jnp.right_shift(ri, shift) == jnp.right_shift(ci, shift)
    if running:
        mask = mask & (ri >= ci)
    hi, lo = _split(x)
    m = mask.astype(BF16)
    ca = 0 if transpose else 1
    return _dg(m, hi, ca, 0) + _dg(m, lo, ca, 0)


@functools.partial(jax.custom_vjp, nondiff_argnums=(1, 2))
def _chunk_sums(x, c, running):
    return _chunk_mask_dot(x, c, running, False)


_chunk_sums.defvjp(lambda x, c, running: (_chunk_mask_dot(x, c, running, False), None),
                   lambda c, running, _, g: (_chunk_mask_dot(g, c, running, True),))


def _spread_onto(x, first, transpose):
    ri = lax.broadcasted_iota(jnp.int32, (LANES, G), 0)
    ci = lax.broadcasted_iota(jnp.int32, (LANES, G), 1)
    sel = (ri == first + jnp.right_shift(ci, HD.bit_length() - 1)).astype(BF16)
    hi, lo = _split(x)
    cb = 1 if transpose else 0
    return _dg(hi, sel, 1, cb) + _dg(lo, sel, 1, cb)


@functools.partial(jax.custom_vjp, nondiff_argnums=(1,))
def _spread(x, first):
    return _spread_onto(x, first, False)


_spread.defvjp(lambda x, first: (_spread_onto(x, first, False), None),
               lambda first, _, g: (_spread_onto(g, first, True),))


def _sgu_chunk(uv, ln_w, ln_b, ws, bs):
    u = _gelu(uv[:, :G])
    v = _gelu(uv[:, G:])
    mu = jnp.mean(v, axis=-1, keepdims=True)
    vc = v - mu
    var = jnp.mean(vc * vc, axis=-1, keepdims=True)
    vn = vc * lax.rsqrt(var + EPS) * ln_w + ln_b
    ri, ci = _tri(SGU_C)
    bs_t = _hdot((ri == ci).astype(F32), bs, "nt")
    mixed = []
    for h in range(NH):
        wm = jnp.where(ri >= ci, ws[h], 0.0)
        mixed.append(_bdot(wm, vn[:, HD * h:HD * (h + 1)], "nn") + bs_t[:, h:h + 1])
    return u * jnp.concatenate(mixed, axis=1)


def _dn_tile(q, k, v, ab, z, state, inv, a_log, dt_bias, nw):
    c = DN_C
    tm = q.shape[0]
    cps = tm // c
    ri, ci = _tri(tm)
    shift = c.bit_length() - 1
    same = jnp.right_shift(ri, shift) == jnp.right_shift(ci, shift)
    per_head = lambda t: jnp.concatenate([jnp.broadcast_to(t[:, h:h + 1], (1, HD)) for h in range(NH)], axis=1)
    g = -jnp.exp(per_head(a_log)) * _softplus(_spread(ab, 0) + per_head(dt_bias))
    beta = _sigmoid(_spread(ab, NH))
    gc = _chunk_sums(g, c, True)
    gl = _chunk_sums(g, c, False)
    g4 = -jnp.exp(a_log) * _softplus(ab[:, 0:NH] + dt_bias)
    gr4 = _hdot(g4, (same & (ri <= ci)).astype(F32), "tn")
    pairs = [(slice(c * j, c * (j + 1)), h) for j in range(cps) for h in range(NH)]
    heads = lambda t: _stack([t[rows, HD * h:HD * (h + 1)] for rows, h in pairs])
    qn = q * lax.rsqrt(_head_sums(q * q) + EPS) * (HD ** -0.5)
    kn = k * lax.rsqrt(_head_sums(k * k) + EPS)
    eg = jnp.exp(gc)
    kb = kn * beta
    rhs = jnp.concatenate([heads(v * beta), heads(kb * eg)], axis=2)
    qg, kd = heads(qn * eg), heads(kn * jnp.exp(gl - gc))
    cd = _stack([jnp.exp(gl[c * j:c * j + 1, HD * h:HD * (h + 1)]) for j in range(cps) for h in range(NH)])
    qn, kn, kb, gc = heads(qn), heads(kn), heads(kb), heads(gc)
    gr = _stack([gr4[h:h + 1, rows] for rows, h in pairs])
    r2, c2 = _tri(c)
    decay = jnp.exp(jnp.where(r2 >= c2, gc - gr, -jnp.inf))
    low = jnp.where(r2 > c2, _bdot(kb, kn, "nt") * decay, 0.0)
    if inv is None:
        inv = _inv_unit_lower(low)
    sol = _unit_lower_solve(low, rhs, inv)
    u, w = sol[:, :, :HD], sol[:, :, HD:]
    attn = _bdot(qn, kn, "nt") * decay
    ys = []
    for j in range(cps):
        b = slice(NH * j, NH * (j + 1))
        v_new = u[b] - _bdot(w[b], state, "nn")
        o = _bdot(qg[b], state, "nn") + _bdot(attn[b], v_new, "nn")
        state = state * cd[b] + _bdot(kd[b], v_new, "tn")
        on = o * lax.rsqrt(jnp.mean(o * o, axis=-1, keepdims=True) + EPS) * nw
        ys.append(jnp.concatenate([on[h] for h in range(NH)], axis=1))
    return jnp.concatenate(ys, axis=0) * (z * _sigmoid(z)), state, inv


def _gla_tile(q, k, v, glr, z, state_t, w2, gate_bias, nw):
    c = GLA_C
    tm = q.shape[0]
    cps = tm // c
    w2_rows = jnp.concatenate([w2, jnp.zeros((LANES - w2.shape[0], G), F32)], axis=0)
    pre = _bdot(glr, w2_rows, "nn") + gate_bias
    log_a = (jnp.minimum(pre, 0.0) - jnp.log(1.0 + jnp.exp(-jnp.maximum(pre, -pre)))) * (1.0 / 16.0)
    gcum = _chunk_sums(log_a, c, True)
    row = lax.broadcasted_iota(jnp.int32, (c, G), 0)
    qs = q * (HD ** -0.5)
    qa, ka, qg, kl, dec = [], [], [], [], []
    for j in range(cps):
        rows = slice(c * j, c * (j + 1))
        gj = gcum[rows]
        g_mid = jnp.sum(jnp.where(row == c // 2, gj, 0.0), axis=0, keepdims=True)
        g_last = jnp.sum(log_a[rows], axis=0, keepdims=True)
        qa.append(qs[rows] * jnp.exp(gj - g_mid))
        ka.append(k[rows] * jnp.exp(g_mid - gj))
        qg.append(qs[rows] * jnp.exp(gj))
        kl.append(k[rows] * jnp.exp(g_last - gj))
        dec.append(jnp.exp(g_last))
    heads = lambda ts: _stack([t[:, HD * h:HD * (h + 1)] for t in ts for h in range(NH)])
    qa, ka, qg, kl, dec = heads(qa), heads(ka), heads(qg), heads(kl), heads(dec)
    vh = heads([v[c * j:c * (j + 1)] for j in range(cps)])
    r2, c2 = _tri(c)
    o_intra = _bdot(jnp.where(r2 >= c2, _bdot(qa, ka, "nt"), 0.0), vh, "nn")
    ys = []
    for j in range(cps):
        b = slice(NH * j, NH * (j + 1))
        o = o_intra[b] + _bdot(qg[b], state_t, "nt")
        state_t = state_t * dec[b] + _bdot(vh[b], kl[b], "tn")
        on = o * lax.rsqrt(jnp.mean(o * o, axis=-1, keepdims=True) + EPS) * nw
        ys.append(jnp.concatenate([on[h] for h in range(NH)], axis=1))
    return jnp.concatenate(ys, axis=0) * (z * _sigmoid(z)), state_t


def _rms(x, nw):
    r = lax.rsqrt(jnp.mean(x * x, axis=-1, keepdims=True) + EPS)
    xh = x * r
    return xh * nw, xh, r


def _rms_bwd(g, xh, r, nw):
    gw = g * nw
    return r * (gw - xh * jnp.mean(gw * xh, axis=-1, keepdims=True)), jnp.sum(g * xh, axis=0, keepdims=True)


def _row(tm, w, blk=0):
    return pl.BlockSpec((tm, w), lambda i: (i, blk))


def _in_proj(x, nw, wp, l, tm=512, rider=None):
    t = x.shape[0]

    def body(*refs):
        (x_ref, nw_ref, w_ref, h_ref, p_ref), ride = _split_refs(refs, 3, 2, 0, rider)
        _ride_begin(rider, ride, (t // tm,))
        h = _rms(x_ref[...], nw_ref[l:l + 1, :])[0].astype(BF16)
        h_ref[...] = h
        p_ref[...] = jnp.dot(h, w_ref[...], preferred_element_type=F32)
        _ride_end(rider, ride, (t // tm,))

    specs, operands = _host(
        rider, [_row(tm, D), _resident((DEPTH, D)), _resident((D, P))], [_row(tm, D), _row(tm, P)],
        [jax.ShapeDtypeStruct((t, D), BF16), jax.ShapeDtypeStruct((t, P), F32)], [], [x, nw, wp])
    return pl.pallas_call(body, name="in_proj", grid=(t // tm,), compiler_params=_cparams("arbitrary"),
                          **specs)(*operands)


def _gu_spec(l):
    return pl.BlockSpec((N_DEV, None, GU_SHARD, D), lambda *_: (0, l, 0, 0), pipeline_mode=pl.Buffered(1))


def _out_mlp(x, ys, w_out, nw2, w_gu_t, w_down, l, tm=256):
    t = x.shape[0]

    def body(x_ref, ya, yb, yc, yd, wo_ref, nw_ref, wgu_ref, wd_ref, mix_ref, x1_ref, h2_ref, gu_ref, act_ref, x2_ref):
        mix = jnp.concatenate([ya[...], yb[...], yc[...], yd[...]], axis=1)
        mix_ref[...] = mix
        x1 = x_ref[...] + jnp.dot(mix, wo_ref[...], preferred_element_type=F32)
        x1_ref[...] = x1
        h2 = _rms(x1, nw_ref[l:l + 1, :])[0].astype(BF16)
        h2_ref[...] = h2
        gu = _dg(h2, wgu_ref[...].reshape(2 * FF, D), 1, 1)
        gu_ref[...] = gu.astype(BF16)
        gate, up = gu[:, :FF], gu[:, FF:]
        act = (gate * _sigmoid(gate) * up).astype(BF16)
        act_ref[...] = act
        x2_ref[...] = x1 + jnp.dot(act, wd_ref[...], preferred_element_type=F32)

    return pl.pallas_call(
        body, name="out_mlp", grid=(t // tm,),
        in_specs=[_row(tm, D), _row(tm, G), _row(tm, G), _row(tm, G), _row(tm, G), _layer((D, D), l),
                  _resident((DEPTH, D)), _gu_spec(l), _layer((FF, D), l)],
        out_specs=[_row(tm, D), _row(tm, D), _row(tm, D), _row(tm, 2 * FF), _row(tm, FF), _row(tm, D)],
        out_shape=[jax.ShapeDtypeStruct((t, D), BF16), jax.ShapeDtypeStruct((t, D), F32),
                   jax.ShapeDtypeStruct((t, D), BF16), jax.ShapeDtypeStruct((t, 2 * FF), BF16),
                   jax.ShapeDtypeStruct((t, FF), BF16), jax.ShapeDtypeStruct((t, D), F32)],
        compiler_params=_cparams("parallel"))(x, *ys, w_out, nw2, w_gu_t, w_down)


def _loss_head(x, nw, target, tm=512):
    t = x.shape[0]

    def body(x_ref, nw_ref, tg_ref, loss_ref, dnw_ref, dx_ref):
        @pl.when(pl.program_id(0) == 0)
        def _():
            loss_ref[...] = jnp.zeros_like(loss_ref)
            dnw_ref[...] = jnp.zeros_like(dnw_ref)
        nw_v = nw_ref[...]
        y, xh, r = _rms(x_ref[...], nw_v)
        err = y - tg_ref[...]
        loss_ref[...] += 0.5 * jnp.sum(err * err) * (1.0 / D)
        dx, dnw = _rms_bwd(err * (1.0 / D), xh, r, nw_v)
        dx_ref[...] = dx
        dnw_ref[...] += dnw

    return pl.pallas_call(
        body, name="loss_head", grid=(t // tm,),
        in_specs=[_row(tm, D), _resident((1, D)), _row(tm, D)],
        out_specs=[_acc((8, LANES)), _acc((1, D)), _row(tm, D)],
        out_shape=[jax.ShapeDtypeStruct((8, LANES), F32), jax.ShapeDtypeStruct((1, D), F32),
                   jax.ShapeDtypeStruct((t, D), F32)],
        compiler_params=_cparams("arbitrary"))(x, nw, target)


def _mlp_out_bwd(dx2, x1, gu, nw2, w_down, w_gu, w_out, l, tm=256, rider=None):
    t = dx2.shape[0]

    def body(*refs):
        (dx2_ref, x1_ref, gu_ref, nw_ref, wd_ref, wgu_ref, wo_ref, dgu_ref, dx1_ref, dmix_ref, dnw_ref), ride = \
            _split_refs(refs, 7, 4, 0, rider)
        _ride_begin(rider, ride, (t // tm,))

        @pl.when(pl.program_id(0) == 0)
        def _():
            dnw_ref[...] = jnp.zeros_like(dnw_ref)
        dx2 = dx2_ref[...]
        dact = _dg(dx2.astype(BF16), wd_ref[...], 1, 1)
        gu = gu_ref[...].astype(F32)
        gate, up = gu[:, :FF], gu[:, FF:]
        s = _sigmoid(gate)
        dgu = jnp.concatenate([dact * up * (s * (1.0 + gate * (1.0 - s))), dact * (gate * s)], axis=1).astype(BF16)
        dgu_ref[...] = dgu
        dh2 = jnp.dot(dgu, wgu_ref[...].reshape(2 * FF, D), preferred_element_type=F32)
        nw_v = nw_ref[l:l + 1, :]
        _, xh, r = _rms(x1_ref[...], nw_v)
        dxn, dnw = _rms_bwd(dh2, xh, r, nw_v)
        dx1 = dx2 + dxn
        dx1_ref[...] = dx1
        dnw_ref[...] += dnw
        dmix_ref[...] = _dg(dx1.astype(BF16), wo_ref[...], 1, 1)
        _ride_end(rider, ride, (t // tm,))

    specs, operands = _host(
        rider, [_row(tm, D), _row(tm, D), _row(tm, 2 * FF), _resident((DEPTH, D)), _layer((FF, D), l), _gu_spec(l),
                _layer((D, D), l)],
        [_row(tm, 2 * FF), _row(tm, D), _row(tm, D), _acc((1, D))],
        [jax.ShapeDtypeStruct((t, 2 * FF), BF16), jax.ShapeDtypeStruct((t, D), F32),
         jax.ShapeDtypeStruct((t, D), F32), jax.ShapeDtypeStruct((1, D), F32)],
        [], [dx2, x1, gu, nw2, w_down, w_gu, w_out])
    return pl.pallas_call(body, name="mlp_out_bwd", grid=(t // tm,), compiler_params=_cparams("arbitrary"),
                          **specs)(*operands)


_DP_WIDTHS = (768, 768, 512, 768, 256, 256, 128, 128)


def _in_proj_bwd(dps, x, dx1, nw, wp, l, tm=512):
    t = x.shape[0]

    def body(*refs):
        dp_refs, (x_ref, dx1_ref, nw_ref, w_ref, dp_ref, dx_ref, dnw_ref) = refs[:8], refs[8:]

        @pl.when(pl.program_id(0) == 0)
        def _():
            dnw_ref[...] = jnp.zeros_like(dnw_ref)
        dp = jnp.concatenate([r[...] for r in dp_refs], axis=1)
        dp_ref[...] = dp
        dh = _dg(dp, w_ref[...], 1, 1)
        nw_v = nw_ref[l:l + 1, :]
        _, xh, r = _rms(x_ref[...], nw_v)
        dxn, dnw = _rms_bwd(dh, xh, r, nw_v)
        dx_ref[...] = dx1_ref[...] + dxn
        dnw_ref[...] += dnw

    return pl.pallas_call(
        body, name="in_proj_bwd", grid=(t // tm,),
        in_specs=[_row(tm, w) for w in _DP_WIDTHS] + [_row(tm, D), _row(tm, D), _resident((DEPTH, D)), _resident((D, P))],
        out_specs=[_row(tm, P), _row(tm, D), _acc((1, D))],
        out_shape=[jax.ShapeDtypeStruct((t, P), BF16), jax.ShapeDtypeStruct((t, D), F32),
                   jax.ShapeDtypeStruct((1, D), F32)],
        compiler_params=_cparams("arbitrary"))(*dps, x, dx1, nw, wp)


def _accumulate(acc, o_ref, t_axis, term):
    @pl.when(pl.program_id(t_axis) == 0)
    def _():
        acc[...] = jnp.zeros_like(acc)
    acc[...] += term

    @pl.when(pl.program_id(t_axis) == pl.num_programs(t_axis) - 1)
    def _():
        o_ref[...] = acc[...].astype(o_ref.dtype)


WGRAD_TOKENS = 2048


def _matmul_tn(a, b, name, out_dtype, tn=512, tt=WGRAD_TOKENS, rider=None):
    t, m = a.shape
    n = b.shape[1]
    tt = min(tt, t)
    grid = (n // tn, t // tt)

    def body(*refs):
        (a_ref, b_ref, o_ref, acc), ride = _split_refs(refs, 2, 1, 1, rider)
        _ride_begin(rider, ride, grid)
        _accumulate(acc, o_ref, 1, _dg(a_ref[...].astype(BF16), b_ref[...].astype(BF16), 0, 0))
        _ride_end(rider, ride, grid)

    specs, operands = _host(
        rider, [pl.BlockSpec((tt, m), lambda j, i: (i, 0)), pl.BlockSpec((tt, tn), lambda j, i: (i, j))],
        [pl.BlockSpec((m, tn), lambda j, i: (0, j))], [jax.ShapeDtypeStruct((m, n), out_dtype)],
        [pltpu.VMEM((m, tn), F32)], [a, b])
    out = pl.pallas_call(body, name=name, grid=grid, compiler_params=_cparams("arbitrary", "arbitrary"), **specs)(*operands)
    return out[0] if rider is None else out


WGRAD_ROWS = 2 * GU_SHARD


def _wgrad_rows(a, b, name, tt=WGRAD_TOKENS, rider=None):
    t, m = a.shape
    tt = min(tt, t)
    grid = (m // WGRAD_ROWS, t // tt)

    def body(*refs):
        (a_ref, b_ref, o_ref, acc), ride = _split_refs(refs, 2, 1, 1, rider)
        _ride_begin(rider, ride, grid)
        _accumulate(acc, o_ref, 1, _dg(a_ref[...], b_ref[...].astype(BF16), 0, 0))
        _ride_end(rider, ride, grid)

    specs, operands = _host(
        rider, [pl.BlockSpec((tt, WGRAD_ROWS), lambda j, i: (i, j)), pl.BlockSpec((tt, D), lambda j, i: (i, 0))],
        [pl.BlockSpec((WGRAD_ROWS, D), lambda j, i: (j, 0))], [jax.ShapeDtypeStruct((m, D), BF16)],
        [pltpu.VMEM((WGRAD_ROWS, D), F32)], [a, b])
    out = pl.pallas_call(body, name=name, grid=grid, compiler_params=_cparams("arbitrary", "arbitrary"), **specs)(*operands)
    return out[0] if rider is None else out


def _relayout_w_in(g, tr=256):
    def body(w_ref, o_ref):
        full = jnp.concatenate([w_ref[d] for d in range(N_DEV)], axis=1)
        parts = []
        for a, b, z in _PAD_SEGS:
            parts.append(full[:, a:b])
            if z:
                parts.append(jnp.zeros((tr, z), full.dtype))
        o_ref[...] = jnp.concatenate(parts, axis=1)

    return pl.pallas_call(
        body, name="relayout_w_in", grid=(D // tr,),
        in_specs=[pl.BlockSpec((N_DEV, tr, IN_SHARD), lambda i: (0, i, 0))], out_specs=_row(tr, P),
        out_shape=jax.ShapeDtypeStruct((D, P), g.dtype), compiler_params=_cparams("parallel"))(g)


def _scatter_w_in_grad(gp, tr=256):
    def body(g_ref, o_ref):
        g = g_ref[...]
        pieces, off = [], 0
        for a, b, z in _PAD_SEGS:
            pieces.append((a, g[:, off:off + (b - a)]))
            off += (b - a) + z
        nat = jnp.concatenate([piece for _, piece in sorted(pieces, key=lambda ap: ap[0])], axis=1)
        for d in range(N_DEV):
            o_ref[d] = nat[:, IN_SHARD * d:IN_SHARD * (d + 1)].astype(BF16)

    return pl.pallas_call(
        body, name="scatter_w_in_grad", grid=(D // tr,),
        in_specs=[pl.BlockSpec((tr, P), lambda i: (i, 0))],
        out_specs=pl.BlockSpec((N_DEV, tr, IN_SHARD), lambda i: (0, i, 0)),
        out_shape=jax.ShapeDtypeStruct((N_DEV, D, IN_SHARD), BF16),
        compiler_params=_cparams("parallel"))(gp)


_A_BLK = 3


def _sgu_specs(l):
    return [_resident((DEPTH, G)), _resident((DEPTH, G)), _layer((NH, SGU_C, SGU_C), l), _layer((NH, SGU_C), l)]


def _sgu_fwd(p, ln_w, ln_b, ws, bs, l, tm=256):
    t = p.shape[0]

    def body(uv_ref, lw_ref, lb_ref, ws_ref, bs_ref, y_ref):
        ws_v = [ws_ref[h] for h in range(NH)]
        for c in range(tm // SGU_C):
            rows = pl.ds(c * SGU_C, SGU_C)
            y_ref[rows, :] = _sgu_chunk(uv_ref[rows, :], lw_ref[l:l + 1, :], lb_ref[l:l + 1, :], ws_v,
                                        bs_ref[...]).astype(BF16)

    return pl.pallas_call(
        body, name="sgu_fwd", grid=(t // tm,),
        in_specs=[_row(tm, 2 * G, _A_BLK)] + _sgu_specs(l), out_specs=_row(tm, G),
        out_shape=jax.ShapeDtypeStruct((t, G), BF16),
        compiler_params=_cparams("parallel"))(p, ln_w, ln_b, ws, bs)


def _sgu_bwd(p, dmix, ln_w, ln_b, ws, bs, l, tm=256):
    t = p.shape[0]

    def body(uv_ref, dy_ref, lw_ref, lb_ref, ws_ref, bs_ref, duv_ref, dlw_ref, dlb_ref, dws_ref, dbs_ref):
        @pl.when(pl.program_id(0) == 0)
        def _():
            for r in (dlw_ref, dlb_ref, dws_ref, dbs_ref):
                r[...] = jnp.zeros_like(r)
        ws_v = [ws_ref[h] for h in range(NH)]
        for c in range(tm // SGU_C):
            rows = pl.ds(c * SGU_C, SGU_C)
            _, vjp = jax.vjp(_sgu_chunk, uv_ref[rows, :], lw_ref[l:l + 1, :], lb_ref[l:l + 1, :], ws_v, bs_ref[...])
            duv, dlw, dlb, dws, dbs = vjp(dy_ref[rows, :])
            duv_ref[rows, :] = duv.astype(BF16)
            dlw_ref[...] += dlw
            dlb_ref[...] += dlb
            dbs_ref[...] += dbs
            for h in range(NH):
                dws_ref[h] += dws[h]

    return pl.pallas_call(
        body, name="sgu_bwd", grid=(t // tm,),
        in_specs=[_row(tm, 2 * G, _A_BLK), _row(tm, G, 0)] + _sgu_specs(l),
        out_specs=[_row(tm, 2 * G), _acc((1, G)), _acc((1, G)), _acc((NH, SGU_C, SGU_C)), _acc((NH, SGU_C))],
        out_shape=[jax.ShapeDtypeStruct((t, 2 * G), BF16), jax.ShapeDtypeStruct((1, G), F32),
                   jax.ShapeDtypeStruct((1, G), F32), jax.ShapeDtypeStruct((NH, SGU_C, SGU_C), F32),
                   jax.ShapeDtypeStruct((NH, SGU_C), F32)],
        compiler_params=_cparams("arbitrary"))(p, dmix, ln_w, ln_b, ws, bs)


HALO = 8


def _prev_halo(tm, width, col):
    return pl.BlockSpec((HALO, width), lambda i: (jnp.maximum(i * (tm // HALO) - 1, 0), col))


def _next_halo(tm, width, col, t):
    return pl.BlockSpec((HALO, width), lambda i: (jnp.minimum((i + 1) * (tm // HALO), t // HALO - 1), col))


def _with_prev(halo_ref, x_ref):
    halo = jnp.where(pl.program_id(0) == 0, 0.0, halo_ref[...])
    return jnp.concatenate([halo, x_ref[...]], axis=0)


def _with_next(x_ref, halo_ref):
    halo = jnp.where(pl.program_id(0) == pl.num_programs(0) - 1, 0.0, halo_ref[...])
    return jnp.concatenate([x_ref[...], halo], axis=0)


def _delay(ext, j):
    return ext if j == 0 else pltpu.roll(ext, j, axis=0)


def _advance(ext, j):
    return ext if j == 0 else pltpu.roll(ext, ext.shape[0] - j, axis=0)


def _causal_conv(ext, w, tm):
    kw = w.shape[0]
    out = None
    for k in range(kw):
        term = _delay(ext, kw - 1 - k)[HALO:HALO + tm] * w[k:k + 1, :]
        out = term if out is None else out + term
    return out


def _causal_conv_t(ext, w, tm):
    kw = w.shape[0]
    out = None
    for k in range(kw):
        term = _advance(ext, kw - 1 - k)[0:tm] * w[k:k + 1, :]
        out = term if out is None else out + term
    return out


def _conv_wgrad(ext, dy, kw, tm):
    return jnp.concatenate(
        [jnp.sum(_delay(ext, kw - 1 - k)[HALO:HALO + tm] * dy, axis=0, keepdims=True) for k in range(kw)], axis=0)


_B_GB, _B_GC, _B_H = 8, 9, 10
_C_QKV, _D_QKV = 0, 1
_C_Z, _D_Z = 11, 12
_C_AB, _D_GLR = 26, 27


def _sconv_fwd(p, w, l, tm=512):
    t = p.shape[0]

    def body(gb_ref, gc_ref, h_ref, gc_halo, h_halo, w_ref, y_ref):
        s_ext = _with_prev(gc_halo, gc_ref) * _with_prev(h_halo, h_ref)
        y_ref[...] = (gb_ref[...] * _causal_conv(s_ext, w_ref[...], tm)).astype(BF16)

    return pl.pallas_call(
        body, name="sconv_fwd", grid=(t // tm,),
        in_specs=[_row(tm, G, _B_GB), _row(tm, G, _B_GC), _row(tm, G, _B_H), _prev_halo(tm, G, _B_GC),
                  _prev_halo(tm, G, _B_H), _layer((3, G), l)],
        out_specs=_row(tm, G), out_shape=jax.ShapeDtypeStruct((t, G), BF16),
        compiler_params=_cparams("parallel"))(p, p, p, p, p, w)


def _sconv_bwd(p, dmix, w, l, tm=512):
    t = p.shape[0]

    def body(gb_ref, gc_ref, h_ref, gc_halo, h_halo, gb_next, dy_ref, dy_next, w_ref, dp_ref, dw_ref):
        @pl.when(pl.program_id(0) == 0)
        def _():
            dw_ref[...] = jnp.zeros_like(dw_ref)
        w_v = w_ref[...]
        s_ext = _with_prev(gc_halo, gc_ref) * _with_prev(h_halo, h_ref)
        dout = dy_ref[...]
        d_gb = dout * _causal_conv(s_ext, w_v, tm)
        dconv_ext = _with_next(dy_ref, dy_next) * _with_next(gb_ref, gb_next)
        ds = _causal_conv_t(dconv_ext, w_v, tm)
        dw_ref[...] += _conv_wgrad(s_ext, dconv_ext[0:tm], 3, tm)
        dp_ref[...] = jnp.concatenate([d_gb, ds * h_ref[...], ds * gc_ref[...]], axis=1).astype(BF16)

    return pl.pallas_call(
        body, name="sconv_bwd", grid=(t // tm,),
        in_specs=[_row(tm, G, _B_GB), _row(tm, G, _B_GC), _row(tm, G, _B_H), _prev_halo(tm, G, _B_GC),
                  _prev_halo(tm, G, _B_H), _next_halo(tm, G, _B_GB, t), _row(tm, G, 1), _next_halo(tm, G, 1, t),
                  _layer((3, G), l)],
        out_specs=[_row(tm, 3 * G), _acc((3, G))],
        out_shape=[jax.ShapeDtypeStruct((t, 3 * G), BF16), jax.ShapeDtypeStruct((3, G), F32)],
        compiler_params=_cparams("arbitrary"))(p, p, p, p, p, p, dmix, dmix, w)


def _qkv_conv_fwd(p, w, l, tm=512):
    t = p.shape[0]

    def body(x_ref, x_halo, w_ref, y_ref):
        c = _causal_conv(_with_prev(x_halo, x_ref), w_ref[...], tm)
        y_ref[...] = c * _sigmoid(c)

    return pl.pallas_call(
        body, name="qkv_conv_fwd", grid=(t // tm,),
        in_specs=[_row(tm, 3 * G, _C_QKV), _prev_halo(tm, 3 * G, _C_QKV), _layer((4, 3 * G), l)],
        out_specs=_row(tm, 3 * G), out_shape=jax.ShapeDtypeStruct((t, 3 * G), F32),
        compiler_params=_cparams("parallel"))(p, p, w)


def _qkv_conv_bwd(p, dy, w, l, tm=512):
    t = p.shape[0]

    def body(x_ref, x_halo, x_next, dy_ref, dy_next, w_ref, dx_ref, dw_ref):
        @pl.when(pl.program_id(0) == 0)
        def _():
            dw_ref[...] = jnp.zeros_like(dw_ref)
        w_v = w_ref[...]
        x_all = jnp.concatenate([_with_prev(x_halo, x_ref), jnp.where(
            pl.program_id(0) == pl.num_programs(0) - 1, 0.0, x_next[...])], axis=0)
        c = _causal_conv(x_all, w_v, tm + HALO)
        s = _sigmoid(c)
        dc_ext = _with_next(dy_ref, dy_next) * (s * (1.0 + c * (1.0 - s)))
        dx_ref[...] = _causal_conv_t(dc_ext, w_v, tm).astype(BF16)
        dw_ref[...] += _conv_wgrad(x_all[0:HALO + tm], dc_ext[0:tm], 4, tm)

    return pl.pallas_call(
        body, name="qkv_conv_bwd", grid=(t // tm,),
        in_specs=[_row(tm, 3 * G, _C_QKV), _prev_halo(tm, 3 * G, _C_QKV), _next_halo(tm, 3 * G, _C_QKV, t),
                  _row(tm, 3 * G), _next_halo(tm, 3 * G, 0, t), _layer((4, 3 * G), l)],
        out_specs=[_row(tm, 3 * G), _acc((4, 3 * G))],
        out_shape=[jax.ShapeDtypeStruct((t, 3 * G), BF16), jax.ShapeDtypeStruct((4, 3 * G), F32)],
        compiler_params=_cparams("arbitrary"))(p, p, p, dy, dy, w)


_STATE = pl.BlockSpec((1, NH, HD, HD), lambda i: (i, 0, 0, 0))


def _dn_specs():
    return [_resident((DEPTH, NH)), _resident((DEPTH, NH)), _resident((DEPTH, HD))]


def _dn_fwd(qkv, p, params, l, cps=4, rider=None, forward_at=1.0):
    t = qkv.shape[0]
    tm = DN_C * cps
    nb = cps * NH

    def body(*refs):
        (qkv_ref, z_ref, ab_ref, alog_ref, dt_ref, nw_ref, y_ref, st_ref, inv_ref, state), ride = _split_refs(
            refs, 6, 3, 1, rider)
        _ride_begin(rider, ride, (t // tm,))

        @pl.when(pl.program_id(0) == 0)
        def _():
            state[...] = jnp.zeros_like(state)
        st_ref[0] = state[...]
        y, new, inv = _dn_tile(qkv_ref[:, 0:G], qkv_ref[:, G:2 * G], qkv_ref[:, 2 * G:3 * G], ab_ref[...], z_ref[...],
                               state[...], None, alog_ref[l:l + 1, :], dt_ref[l:l + 1, :], nw_ref[l:l + 1, :])
        y_ref[...] = y.astype(BF16)
        inv_ref[...] = inv
        state[...] = new
        _ride_end(rider, ride, (t // tm,), forward_at)

    specs, operands = _host(
        rider, [_row(tm, 3 * G), _row(tm, G, _C_Z), _row(tm, LANES, _C_AB)] + _dn_specs(),
        [_row(tm, G), _STATE, pl.BlockSpec((nb, DN_C, DN_C), lambda i: (i, 0, 0))],
        [jax.ShapeDtypeStruct((t, G), BF16), jax.ShapeDtypeStruct((t // tm, NH, HD, HD), F32),
         jax.ShapeDtypeStruct((t // DN_C * NH, DN_C, DN_C), F32)],
        [pltpu.VMEM((NH, HD, HD), F32)], [qkv, p, p, *params])
    return pl.pallas_call(body, name="dn_fwd", grid=(t // tm,), compiler_params=_cparams("arbitrary"), **specs)(*operands)


def _dn_bwd(qkv, p, states, inv, dmix, params, l, cps=4, rider=None):
    t = qkv.shape[0]
    tm = DN_C * cps
    n = t // tm
    nb = cps * NH

    def body(*refs):
        (qkv_ref, z_ref, ab_ref, st_ref, inv_ref, dy_ref, alog_ref, dt_ref, nw_ref,
         dqkv_ref, dz_ref, dab_ref, dalog_ref, ddt_ref, dnw_ref, dstate), ride = _split_refs(refs, 9, 6, 1, rider)
        _ride_begin(rider, ride, (n,))
        dprm_refs = (dalog_ref, ddt_ref, dnw_ref)

        @pl.when(pl.program_id(0) == 0)
        def _():
            dstate[...] = jnp.zeros_like(dstate)
            for r in dprm_refs:
                r[...] = jnp.zeros_like(r)
        inv_v = inv_ref[...]
        tile = lambda q, k, v, ab, z, st, alog, dt, nw: _dn_tile(q, k, v, ab, z, st, inv_v, alog, dt, nw)[:2]
        _, vjp = jax.vjp(tile, qkv_ref[:, 0:G], qkv_ref[:, G:2 * G], qkv_ref[:, 2 * G:3 * G], ab_ref[...], z_ref[...],
                         st_ref[0], alog_ref[l:l + 1, :], dt_ref[l:l + 1, :], nw_ref[l:l + 1, :])
        dq, dk, dv, dab, dz, dst, *dprm = vjp((dy_ref[...], dstate[...]))
        dqkv_ref[...] = jnp.concatenate([dq, dk, dv], axis=1)
        dz_ref[...] = dz.astype(BF16)
        dab_ref[...] = dab.astype(BF16)
        dstate[...] = dst
        for r, g in zip(dprm_refs, dprm):
            r[...] += g
        _ride_end(rider, ride, (n,))

    rev = lambda w, blk: pl.BlockSpec((tm, w), lambda i: (n - 1 - i, blk))
    specs, operands = _host(
        rider, [rev(3 * G, 0), rev(G, _C_Z), rev(LANES, _C_AB),
                pl.BlockSpec((1, NH, HD, HD), lambda i: (n - 1 - i, 0, 0, 0)),
                pl.BlockSpec((nb, DN_C, DN_C), lambda i: (n - 1 - i, 0, 0)), rev(G, 2)] + _dn_specs(),
        [rev(3 * G, 0), rev(G, 0), rev(LANES, 0), _acc((1, NH)), _acc((1, NH)), _acc((1, HD))],
        [jax.ShapeDtypeStruct((t, 3 * G), F32), jax.ShapeDtypeStruct((t, G), BF16),
         jax.ShapeDtypeStruct((t, LANES), BF16), jax.ShapeDtypeStruct((1, NH), F32),
         jax.ShapeDtypeStruct((1, NH), F32), jax.ShapeDtypeStruct((1, HD), F32)],
        [pltpu.VMEM((NH, HD, HD), F32)], [qkv, p, p, states, inv, dmix, *params])
    return pl.pallas_call(body, name="dn_bwd", grid=(n,), compiler_params=_cparams("arbitrary"), **specs)(*operands)


def _gla_specs(l):
    return [_layer((16, G), l), _resident((DEPTH, G)), _resident((DEPTH, HD))]


def _gla_fwd(p, params, l, cps=4, rider=None):
    t = p.shape[0]
    tm = GLA_C * cps

    def body(*refs):
        (qkv_ref, z_ref, glr_ref, w2_ref, gb_ref, nw_ref, y_ref, st_ref, state), ride = _split_refs(refs, 6, 2, 1, rider)
        _ride_begin(rider, ride, (t // tm,))

        @pl.when(pl.program_id(0) == 0)
        def _():
            state[...] = jnp.zeros_like(state)
        st_ref[0] = state[...]
        y, new = _gla_tile(qkv_ref[:, 0:G], qkv_ref[:, G:2 * G], qkv_ref[:, 2 * G:3 * G], glr_ref[...], z_ref[...],
                           state[...], w2_ref[...], gb_ref[l:l + 1, :], nw_ref[l:l + 1, :])
        y_ref[...] = y.astype(BF16)
        state[...] = new
        _ride_end(rider, ride, (t // tm,))

    specs, operands = _host(
        rider, [_row(tm, 3 * G, _D_QKV), _row(tm, G, _D_Z), _row(tm, LANES, _D_GLR)] + _gla_specs(l),
        [_row(tm, G), _STATE],
        [jax.ShapeDtypeStruct((t, G), BF16), jax.ShapeDtypeStruct((t // tm, NH, HD, HD), F32)],
        [pltpu.VMEM((NH, HD, HD), F32)], [p, p, p, *params])
    return pl.pallas_call(body, name="gla_fwd", grid=(t // tm,), compiler_params=_cparams("arbitrary"), **specs)(*operands)


def _gla_bwd(p, states, dmix, params, l, cps=4):
    t = p.shape[0]
    tm = GLA_C * cps
    n = t // tm

    def body(qkv_ref, z_ref, glr_ref, st_ref, dy_ref, w2_ref, gb_ref, nw_ref,
             dqkv_ref, dz_ref, dglr_ref, dw2_ref, dgb_ref, dnw_ref, dstate):
        dprm_refs = (dw2_ref, dgb_ref, dnw_ref)

        @pl.when(pl.program_id(0) == 0)
        def _():
            dstate[...] = jnp.zeros_like(dstate)
            for r in dprm_refs:
                r[...] = jnp.zeros_like(r)
        _, vjp = jax.vjp(_gla_tile, qkv_ref[:, 0:G], qkv_ref[:, G:2 * G], qkv_ref[:, 2 * G:3 * G], glr_ref[...],
                         z_ref[...], st_ref[0], w2_ref[...], gb_ref[l:l + 1, :], nw_ref[l:l + 1, :])
        dq, dk, dv, dglr, dz, dst, *dprm = vjp((dy_ref[...], dstate[...]))
        dqkv_ref[...] = jnp.concatenate([dq, dk, dv], axis=1).astype(BF16)
        dz_ref[...] = dz.astype(BF16)
        dglr_ref[...] = dglr.astype(BF16)
        dstate[...] = dst
        for r, g in zip(dprm_refs, dprm):
            r[...] += g

    rev = lambda w, blk: pl.BlockSpec((tm, w), lambda i: (n - 1 - i, blk))
    return pl.pallas_call(
        body, name="gla_bwd", grid=(n,),
        in_specs=[rev(3 * G, _D_QKV), rev(G, _D_Z), rev(LANES, _D_GLR),
                  pl.BlockSpec((1, NH, HD, HD), lambda i: (n - 1 - i, 0, 0, 0)), rev(G, 3)] + _gla_specs(l),
        out_specs=[rev(3 * G, 0), rev(G, 0), rev(LANES, 0), _acc((16, G)), _acc((1, G)), _acc((1, HD))],
        out_shape=[jax.ShapeDtypeStruct((t, 3 * G), BF16), jax.ShapeDtypeStruct((t, G), BF16),
                   jax.ShapeDtypeStruct((t, LANES), BF16), jax.ShapeDtypeStruct((16, G), F32),
                   jax.ShapeDtypeStruct((1, G), F32), jax.ShapeDtypeStruct((1, HD), F32)],
        scratch_shapes=[pltpu.VMEM((NH, HD, HD), F32)],
        compiler_params=_cparams("arbitrary"))(p, p, p, states, dmix, *params)


def _adamw_math(w, g, m, v):
    m = ADAM_B1 * m + (1.0 - ADAM_B1) * g
    v = ADAM_B2 * v + (1.0 - ADAM_B2) * (g * g)
    m_hat = m / (1.0 - ADAM_B1 ** ADAM_STEP)
    v_hat = v / (1.0 - ADAM_B2 ** ADAM_STEP)
    return -ADAM_LR * (m_hat / (jnp.sqrt(v_hat) + ADAM_EPS) + ADAM_WD * w), m, v


def _adamw_large(parts, w, m, v, name, tr, rider=None):
    _, r, c = w.shape
    grid = (DEPTH, r // tr)

    def body(*refs):
        (p0_ref, p1_ref, w_ref, m_ref, v_ref, g_ref, d_ref, nm_ref, nv_ref), ride = _split_refs(refs, 5, 4, 0, rider)
        _ride_begin(rider, ride, grid)

        def total(p_ref):
            g = p_ref[0].astype(F32)
            for k in range(1, N_DEV):
                g = g + p_ref[k].astype(F32)
            return g

        g = jnp.where(pl.program_id(0) == 0, total(p0_ref), total(p1_ref))
        g_ref[...] = g
        d_ref[...], nm_ref[...], nv_ref[...] = _adamw_math(w_ref[...], g, m_ref[...], v_ref[...])
        _ride_end(rider, ride, grid)

    blk = pl.BlockSpec((None, tr, c), lambda l, i: (l, i, 0))
    part = pl.BlockSpec((N_DEV, tr, c), lambda l, i: (0, i, 0))
    specs, operands = _host(rider, [part, part, blk, blk, blk], [blk] * 4, [jax.ShapeDtypeStruct(w.shape, F32)] * 4, [],
                            [*parts, w, m, v])
    return pl.pallas_call(body, name=name, grid=grid, compiler_params=_cparams("arbitrary", "arbitrary"),
                          **specs)(*operands)


_SLAB_AT = {
    "sgu_w_spatial": (0, 0, SGU_C, LANES),
    "sgu_b_spatial": (128, 0, NH, SGU_C),
    "norm1_w": (132, 0, 1, D),
    "norm2_w": (133, 0, 1, D),
    "sgu_ln_w": (134, 0, 1, G),
    "sgu_ln_b": (134, 256, 1, G),
    "gla_gate_bias": (134, 512, 1, G),
    "dn_norm_w": (134, 768, 1, HD),
    "gla_norm_w": (134, 896, 1, HD),
    "dn_a_log": (135, 0, 1, NH),
    "dn_dt_bias": (135, 128, 1, NH),
    "gla_w_gate2": (136, 0, 16, G),
    "dn_conv_w": (136, 256, 4, 3 * G),
    "sc_conv_w": (140, 256, 3, G),
}
_LAYER_ROWS = 152
_FINAL_ROW = DEPTH * _LAYER_ROWS
_SLAB_ROWS, _SLAB_COLS = _FINAL_ROW + 8, 1024
_SLAB_ORDER = tuple(_SLAB_AT)
_SHARDED_SMALL = ("sc_conv_w", "dn_conv_w", "gla_w_gate2")
_REPLICATED = tuple(k for k in _SLAB_ORDER if k not in _SHARDED_SMALL)


def _region(name, l, h=0):
    r0, c0, nr, nc = _SLAB_AT[name]
    return slice(_LAYER_ROWS * l + r0, _LAYER_ROWS * l + r0 + nr), slice(c0 + nc * h, c0 + nc * (h + 1))


def _pack_small(layer_grads, d_final):
    def body(*refs):
        slab = refs[-1]
        slab[...] = jnp.zeros_like(slab)
        it = iter(refs[:-1])
        for l in range(DEPTH):
            for name in _SLAB_ORDER:
                ref = next(it)
                if name == "sgu_w_spatial":
                    for h in range(NH):
                        slab[_region(name, l, h)] = ref[h]
                else:
                    slab[_region(name, l)] = ref[...]
        slab[_FINAL_ROW:_FINAL_ROW + 1, :] = next(it)[...]

    flat = [layer_grads[l][name] for l in range(DEPTH) for name in _SLAB_ORDER] + [d_final]
    return pl.pallas_call(body, name="pack_small_grads", out_shape=jax.ShapeDtypeStruct((_SLAB_ROWS, _SLAB_COLS), F32),
                          compiler_params=_cparams())(*flat)


def _adamw_small(parts, w, m, v):
    names = _REPLICATED + ("final_norm_w",)
    n_in = len(names)

    def body(*refs):
        def total(rows, cols):
            g = refs[0][0, rows, cols]
            for k in range(1, N_DEV):
                g = g + refs[0][k, rows, cols]
            return g

        w_refs = dict(zip(names, refs[1:1 + n_in]))
        m_refs = dict(zip(names, refs[1 + n_in:1 + 2 * n_in]))
        v_refs = dict(zip(names, refs[1 + 2 * n_in:1 + 3 * n_in]))
        outs = refs[1 + 3 * n_in:]
        out_refs = [dict(zip(names, outs[j * n_in:(j + 1) * n_in])) for j in range(4)]
        full_refs = dict(zip(_SHARDED_SMALL, outs[4 * n_in:]))

        def update(name, idx, g):
            res = (g,) + _adamw_math(w_refs[name][idx], g, m_refs[name][idx], v_refs[name][idx])
            for o, val in zip(out_refs, res):
                o[name][idx] = val

        for l in range(DEPTH):
            for name in _REPLICATED:
                if name == "sgu_w_spatial":
                    for h in range(NH):
                        update(name, (l, h), total(*_region(name, l, h)))
                elif name == "sgu_b_spatial":
                    update(name, (l,), total(*_region(name, l)))
                else:
                    update(name, (slice(l, l + 1), slice(None)), total(*_region(name, l)))
            for name in _SHARDED_SMALL:
                full_refs[name][l] = total(*_region(name, l))
        update("final_norm_w", (slice(None), slice(None)), total(slice(_FINAL_ROW, _FINAL_ROW + 1), slice(None)))

    shapes = [jax.ShapeDtypeStruct(w[k].shape, F32) for k in names]
    full_shapes = [jax.ShapeDtypeStruct((DEPTH,) + _SLAB_AT[k][2:], F32) for k in _SHARDED_SMALL]
    outs = pl.pallas_call(body, name="adamw_small", out_shape=shapes * 4 + full_shapes, compiler_params=_cparams())(
        parts, *[w[k] for k in names], *[m[k] for k in names], *[v[k] for k in names])
    result = [dict(zip(names, outs[j * n_in:(j + 1) * n_in])) for j in range(4)]
    return result, dict(zip(_SHARDED_SMALL, outs[4 * n_in:]))


def _adamw_shards(g, w, m, v):
    names = _SHARDED_SMALL
    n = len(names)

    def body(*refs):
        for j in range(n):
            g_v = refs[j][...]
            res = _adamw_math(refs[n + j][...], g_v, refs[2 * n + j][...], refs[3 * n + j][...])
            for o, val in zip(refs[4 * n + j::n], res):
                o[...] = val

    shapes = [jax.ShapeDtypeStruct(w[k].shape, F32) for k in names]
    outs = pl.pallas_call(body, name="adamw_small_shards", out_shape=shapes * 3, compiler_params=_cparams())(
        *[g[k] for k in names], *[w[k] for k in names], *[m[k] for k in names], *[v[k] for k in names])
    return [dict(zip(names, outs[j * n:(j + 1) * n])) for j in range(3)]


_ANY = pl.BlockSpec(memory_space=pl.ANY)


def _place():
    return lax.axis_index("x"), lax.axis_index("y"), lax.axis_index("c")


def _sems(n):
    return [pltpu.SemaphoreType.DMA((n, 7)), pltpu.SemaphoreType.DMA((n, 7)), pltpu.SemaphoreType.DMA((n,))]


class _Gather:
    def __init__(self, blocks, second):
        self.inputs, self.second = list(blocks), list(second)
        self.out_shape = [jax.ShapeDtypeStruct((a.shape[0], N_DEV) + a.shape[1:] if s else (N_DEV,) + a.shape, a.dtype)
                          for a, s in zip(blocks, second)]
        self.scratch = _sems(len(blocks))

    def _copies(self, refs):
        x_refs, out_refs, (send_sems, recv_sems, local_sems) = refs
        n = len(x_refs)
        x, y, c = _place()
        me, sibling = (x, y, c), (x, y, 1 - c)
        chips = [(1 - x, y), (x, 1 - y), (1 - x, 1 - y)]

        def slot(j, px, py, pc):
            idx = 4 * px + 2 * py + pc
            return out_refs[j].at[:, idx] if self.second[j] else out_refs[j].at[idx]

        def copies(k, block, to, own=False):
            return [pltpu.make_async_remote_copy(
                src_ref=x_refs[j] if own else slot(j, *block), dst_ref=slot(j, *block),
                send_sem=send_sems.at[j, k], recv_sem=recv_sems.at[j, k], device_id=to,
                device_id_type=pl.DeviceIdType.MESH) for j in range(n)]

        mine = [pltpu.make_async_copy(x_refs[j], slot(j, *me), local_sems.at[j]) for j in range(n)]
        first = copies(0, me, sibling, own=True)
        for j, chip in enumerate(chips):
            first += copies(1 + j, me, (*chip, c), own=True)
        landed = [copies(1 + j, (*chip, c), me) for j, chip in enumerate(chips)]
        onward = [copies(4 + j, (*chip, c), sibling) for j, chip in enumerate(chips)]
        from_sibling = copies(0, sibling, me)
        for j, chip in enumerate(chips):
            from_sibling += copies(4 + j, (*chip, 1 - c), me)
        return mine, first, landed, onward, from_sibling

    def start(self, refs):
        mine, first, _, _, _ = self._copies(refs)
        for cp in mine + first:
            cp.start()

    def forward(self, refs):
        _, _, landed, onward, _ = self._copies(refs)
        for arrived, passed in zip(landed, onward):
            for cp in arrived:
                cp.wait_recv()
            for cp in passed:
                cp.start()

    def finish(self, refs):
        mine, first, _, onward, from_sibling = self._copies(refs)
        for cp in from_sibling:
            cp.wait_recv()
        for cp in first + [cp for passed in onward for cp in passed]:
            cp.wait_send()
        for cp in mine:
            cp.wait()


class _Exchange:
    def __init__(self, sends):
        self.inputs = list(sends)
        self.out_shape = [jax.ShapeDtypeStruct(a.shape, a.dtype) for a in sends]
        self.scratch = _sems(len(sends))

    def _copies(self, refs):
        x_refs, out_refs, (send_sems, recv_sems, local_sems) = refs
        n = len(x_refs)
        x, y, c = _place()
        me = 4 * x + 2 * y + c
        sent, landing = [], []
        for k in range(1, N_DEV):
            px, py, pc = x ^ ((k >> 2) & 1), y ^ ((k >> 1) & 1), c ^ (k & 1)
            them = 4 * px + 2 * py + pc
            for j in range(n):
                for here, there, into in ((them, me, sent), (me, them, landing)):
                    into.append(pltpu.make_async_remote_copy(
                        src_ref=x_refs[j].at[here], dst_ref=out_refs[j].at[there], send_sem=send_sems.at[j, k - 1],
                        recv_sem=recv_sems.at[j, k - 1], device_id=(px, py, pc), device_id_type=pl.DeviceIdType.MESH))
        mine = [pltpu.make_async_copy(x_refs[j].at[me], out_refs[j].at[me], local_sems.at[j]) for j in range(n)]
        return mine, sent, landing

    def start(self, refs):
        mine, sent, _ = self._copies(refs)
        for cp in mine + sent:
            cp.start()

    def forward(self, refs):
        pass

    def finish(self, refs):
        mine, sent, landing = self._copies(refs)
        for cp in landing:
            cp.wait_recv()
        for cp in sent:
            cp.wait_send()
        for cp in mine:
            cp.wait()


def _run(rider, name):
    n_in, n_out = len(rider.inputs), len(rider.out_shape)

    def body(*refs):
        parts = (refs[:n_in], refs[n_in:n_in + n_out], refs[n_in + n_out:])
        rider.start(parts)
        rider.forward(parts)
        rider.finish(parts)

    return pl.pallas_call(body, name=name, out_shape=rider.out_shape, in_specs=[_ANY] * n_in, out_specs=[_ANY] * n_out,
                          scratch_shapes=rider.scratch)(*rider.inputs)


def _split_refs(refs, n_in, n_out, n_scratch, rider):
    r_in = len(rider.inputs) if rider else 0
    r_out = len(rider.out_shape) if rider else 0
    a = n_in + r_in
    b = a + n_out + r_out
    own = refs[:n_in] + refs[a:a + n_out] + refs[b:b + n_scratch]
    return own, (refs[n_in:a], refs[a + n_out:b], refs[b + n_scratch:])


def _grid_step(grid):
    step, stride = 0, 1
    for axis in reversed(range(len(grid))):
        step = step + pl.program_id(axis) * stride
        stride *= grid[axis]
    return step


def _ride_begin(rider, ride_refs, grid):
    if rider is not None:
        pl.when(_grid_step(grid) == 0)(lambda: rider.start(ride_refs))


def _ride_end(rider, ride_refs, grid, forward_at=1.0):
    if rider is not None:
        steps = 1
        for n in grid:
            steps *= n
        step = _grid_step(grid)
        pl.when(step == min(steps - 1, int(steps * forward_at)))(lambda: rider.forward(ride_refs))
        pl.when(step == steps - 1)(lambda: rider.finish(ride_refs))


def _host(rider, in_specs, out_specs, out_shape, scratch, operands):
    if rider is None:
        return dict(in_specs=in_specs, out_specs=out_specs, out_shape=out_shape, scratch_shapes=scratch), operands
    return dict(in_specs=in_specs + [_ANY] * len(rider.inputs), out_specs=out_specs + [_ANY] * len(rider.out_shape),
                out_shape=out_shape + rider.out_shape, scratch_shapes=scratch + rider.scratch), operands + rider.inputs


_LATE = ("w_gate_up", "w_out", "w_down")


def _local_grads(x, target, w, late=None, exchange=False):
    w = dict(w)
    w_in = list(w["w_in"])
    dn_params = (w["dn_a_log"], w["dn_dt_bias"], w["dn_norm_w"])
    gla_params = (w["gla_w_gate2"], w["gla_gate_bias"], w["gla_norm_w"])
    sgu_params = (w["sgu_ln_w"], w["sgu_ln_b"], w["sgu_w_spatial"], w["sgu_b_spatial"])
    saved = []
    for l in range(DEPTH):
        riding = l == 0 and late is not None
        h, p, *got = _in_proj(x, w["norm1_w"], w_in[l], l, rider=_Gather([late["w_down"]], [True]) if riding else None)
        if riding:
            w["w_down"] = got[0].reshape(DEPTH, FF, D)
        ya = _sgu_fwd(p, *sgu_params, l)
        yb = _sconv_fwd(p, w["sc_conv_w"], l)
        qkv_c = _qkv_conv_fwd(p, w["dn_conv_w"], l)
        yc, st_c, inv_c, *got = _dn_fwd(
            qkv_c, p, dn_params, l, forward_at=0.85,
            rider=_Gather([late["w_gate_up"], late["w_out"]], [False, True]) if riding else None)
        if riding:
            w.update(w_gate_up=got[0], w_out=got[1].reshape(DEPTH, D, D))
        yd, st_d, *got = _gla_fwd(p, gla_params, l, rider=_Gather([late["w_in"]], [False]) if riding else None)
        if riding:
            w_in[1] = _relayout_w_in(got[0])
        mix, x1, h2, gu, act, x2 = _out_mlp(x, (ya, yb, yc, yd), w["w_out"], w["norm2_w"], w["w_gate_up"], w["w_down"], l)
        saved.append(dict(x=x, h=h, p=p, qkv_c=qkv_c, st_c=st_c, inv_c=inv_c, st_d=st_d, mix=mix, x1=x1, h2=h2, gu=gu,
                          act=act))
        x = x2
    loss, d_final, dx = _loss_head(x, w["final_norm_w"], target)
    large = {k: [None] * DEPTH for k in ("w_in", "w_out", "w_gate_up", "w_down")}
    small = [None] * DEPTH
    for l in reversed(range(DEPTH)):
        s = saved[l]
        p = s["p"]
        g = {}
        riding = exchange and l == 0

        def exchanging(pairs):
            return _Exchange([large[k][j] for k, j in pairs]) if riding else None

        def arrive(pairs, arrived):
            for (k, j), a in zip(pairs, arrived):
                large[k][j] = a

        pairs = (("w_gate_up", 1), ("w_down", 1))
        dgu, dx1, dmix, g["norm2_w"], *arrived = _mlp_out_bwd(
            dx, s["x1"], s["gu"], w["norm2_w"], w["w_down"], w["w_gate_up"], w["w_out"], l, rider=exchanging(pairs))
        arrive(pairs, arrived)
        pairs = (("w_in", 1), ("w_out", 1))
        d_gu, *arrived = _wgrad_rows(dgu, s["h2"], "wgrad_gate_up", rider=exchanging(pairs)) if riding else (
            _wgrad_rows(dgu, s["h2"], "wgrad_gate_up"),)
        arrive(pairs, arrived)
        large["w_gate_up"][l] = d_gu.reshape(N_DEV, GU_SHARD, D)
        large["w_down"][l] = _wgrad_rows(s["act"], dx, "wgrad_down").reshape(N_DEV, FF // N_DEV, D)
        large["w_out"][l] = _matmul_tn(s["mix"], dx1, "wgrad_out", BF16).reshape(N_DEV, D // N_DEV, D)
        d_a, g["sgu_ln_w"], g["sgu_ln_b"], g["sgu_w_spatial"], g["sgu_b_spatial"] = _sgu_bwd(p, dmix, *sgu_params, l)
        d_b, g["sc_conv_w"] = _sconv_bwd(p, dmix, w["sc_conv_w"], l)
        pairs = tuple((k, 0) for k in _LATE)
        dqkv_c, dz_c, dab, g["dn_a_log"], g["dn_dt_bias"], g["dn_norm_w"], *arrived = _dn_bwd(
            s["qkv_c"], p, s["st_c"], s["inv_c"], dmix, dn_params, l, rider=exchanging(pairs))
        arrive(pairs, arrived)
        d_c, g["dn_conv_w"] = _qkv_conv_bwd(p, dqkv_c, w["dn_conv_w"], l)
        d_d, dz_d, dglr, g["gla_w_gate2"], g["gla_gate_bias"], g["gla_norm_w"] = _gla_bwd(p, s["st_d"], dmix, gla_params, l)
        dp, dx, g["norm1_w"] = _in_proj_bwd((d_c, d_d, d_a, d_b, dz_c, dz_d, dab, dglr), s["x"], dx1, w["norm1_w"],
                                            w_in[l], l)
        small[l] = g
        if exchange and l == 0:
            d_w_in, small = _matmul_tn(s["h"], dp, "wgrad_in", F32, rider=_Gather([_pack_small(small, d_final)], [False]))
        else:
            d_w_in = _matmul_tn(s["h"], dp, "wgrad_in", F32)
        large["w_in"][l] = _scatter_w_in_grad(d_w_in)
    return loss[0, 0], dx, large, small, d_final


_ORDER = ("norm1_w", "w_in", "sgu_ln_w", "sgu_ln_b", "sgu_w_spatial", "sgu_b_spatial", "sc_conv_w", "dn_conv_w",
          "dn_a_log", "dn_dt_bias", "dn_norm_w", "gla_w_gate2", "gla_gate_bias", "gla_norm_w", "w_out", "norm2_w",
          "w_gate_up", "w_down", "final_norm_w")
_LARGE = ("w_in", "w_gate_up", "w_out", "w_down")
_LARGE_TILE = {"w_in": 256, "w_gate_up": 352, "w_out": 128, "w_down": 352}


def _gather_cols(g):
    return jnp.transpose(g, (1, 2, 0, 3)).reshape(g.shape[1], g.shape[2], N_DEV * g.shape[3])


def kernel(x, norm1_w, w_in, sgu_ln_w, sgu_ln_b, sgu_w_spatial, sgu_b_spatial, sc_conv_w, dn_conv_w, dn_a_log, dn_dt_bias, dn_norm_w, gla_w_gate2, gla_gate_bias, gla_norm_w, w_out, norm2_w, w_gate_up, w_down, final_norm_w, loss_target, m_norm1_w, m_w_in, m_sgu_ln_w, m_sgu_ln_b, m_sgu_w_spatial, m_sgu_b_spatial, m_sc_conv_w, m_dn_conv_w, m_dn_a_log, m_dn_dt_bias, m_dn_norm_w, m_gla_w_gate2, m_gla_gate_bias, m_gla_norm_w, m_w_out, m_norm2_w, m_w_gate_up, m_w_down, m_final_norm_w, v_norm1_w, v_w_in, v_sgu_ln_w, v_sgu_ln_b, v_sgu_w_spatial, v_sgu_b_spatial, v_sc_conv_w, v_dn_conv_w, v_dn_a_log, v_dn_dt_bias, v_dn_norm_w, v_gla_w_gate2, v_gla_gate_bias, v_gla_norm_w, v_w_out, v_norm2_w, v_w_gate_up, v_w_down, v_final_norm_w):
    args = dict(locals())
    wts = {k: args[k] for k in _ORDER}
    mom = {k: args["m_" + k] for k in _ORDER}
    var = {k: args["v_" + k] for k in _ORDER}
    for d in (wts, mom, var):
        d["final_norm_w"] = d["final_norm_w"][None]
    me = 4 * lax.axis_index("x") + 2 * lax.axis_index("y") + lax.axis_index("c")
    for d in (wts, mom, var):
        d["w_gate_up"] = jnp.swapaxes(d["w_gate_up"], 1, 2)

    late = {k: wts[k].astype(BF16) for k in _LATE}
    w_in = wts["w_in"].astype(BF16)
    late["w_in"] = w_in[1]
    got = _run(_Gather([w_in[0]] + [wts[k] for k in _SHARDED_SMALL], [False] * 4), "gather_w_in")
    full = dict(wts)
    full["w_in"] = [_relayout_w_in(got[0]), None]
    for k, g in zip(_SHARDED_SMALL, got[1:]):
        full[k] = _gather_cols(g)

    loss, dx, large, small, d_final = _local_grads(x[0], loss_target[0], full, late=late, exchange=True)
    loss = lax.psum(loss, ("x", "y", "c"))

    result = {}
    for k in ("w_gate_up", "w_out", "w_down", "w_in"):
        rider = _Exchange([large["w_in"][0]]) if k == "w_gate_up" else None
        *outs, = _adamw_large(large[k], wts[k], mom[k], var[k], "adamw_" + k, _LARGE_TILE[k], rider=rider)
        if rider is not None:
            large["w_in"][0] = outs.pop()
        for kind, a in zip(("grad", "delta", "new_m", "new_v"), outs):
            result[kind, k] = jnp.swapaxes(a, 1, 2) if k == "w_gate_up" else a

    slabs = small
    rep = _REPLICATED + ("final_norm_w",)
    outs, summed = _adamw_small(slabs, {k: wts[k] for k in rep}, {k: mom[k] for k in rep}, {k: var[k] for k in rep})
    for kind, d in zip(("grad", "delta", "new_m", "new_v"), outs):
        for k, a in d.items():
            result[kind, k] = a[0] if k == "final_norm_w" else a
    own = {k: lax.dynamic_slice_in_dim(summed[k], me * wts[k].shape[-1], wts[k].shape[-1], axis=2) for k in _SHARDED_SMALL}
    outs = _adamw_shards(own, {k: wts[k] for k in _SHARDED_SMALL}, {k: mom[k] for k in _SHARDED_SMALL},
                         {k: var[k] for k in _SHARDED_SMALL})
    for kind, d in zip(("grad", "delta", "new_m", "new_v"), [own] + outs):
        for k, a in d.items():
            result[kind, k] = a

    return (loss, dx[None], *[result[kind, k] for kind in ("grad", "delta", "new_m", "new_v") for k in _ORDER])
```

```python
import functools

import jax
import jax.numpy as jnp
from jax import lax
from jax.experimental import pallas as pl
from jax.experimental.pallas import tpu as pltpu

F32, BF16 = jnp.float32, jnp.bfloat16
DEPTH = 2
D = 1024
G = 256
NH, HD = 4, 64
FF = 2816
IN_COLS = 3352
P = 3584
EPS = 1e-6
SGU_C, DN_C, GLA_C = 128, 64, 64
N_DEV = 8
IN_SHARD = IN_COLS // N_DEV
GU_SHARD = 2 * FF // N_DEV
LANES = 128
VMEM_LIMIT = 56 * 2 ** 20

_PAD_SEGS = ((1280, 2048, 0),
             (2312, 3080, 0),
             (0, 512, 0),
             (512, 1280, 0),
             (2056, 2312, 0),
             (3096, 3352, 0),
             (2048, 2056, 120),
             (3080, 3096, 112))

ADAM_LR, ADAM_B1, ADAM_B2, ADAM_EPS, ADAM_WD, ADAM_STEP = 0.001, 0.9, 0.999, 1e-08, 0.01, 10


def _cparams(*sem):
    return pltpu.CompilerParams(dimension_semantics=sem, vmem_limit_bytes=VMEM_LIMIT)


def _resident(shape):
    return pl.BlockSpec(shape, lambda *_: (0,) * len(shape), pipeline_mode=pl.Buffered(1))


def _layer(shape, l):
    return pl.BlockSpec((None,) + tuple(shape), lambda *_: (l,) + (0,) * len(shape), pipeline_mode=pl.Buffered(1))


def _acc(shape):
    return pl.BlockSpec(shape, lambda *_: (0,) * len(shape))


def _dg(a, b, ca, cb):
    lead = a.ndim - 2
    batch = ((0,), (0,)) if lead else ((), ())
    return lax.dot_general(a, b, (((ca + lead,), (cb + lead,)), batch), preferred_element_type=F32)


def _split(t):
    hi = t.astype(BF16)
    return hi, (t - hi.astype(F32)).astype(BF16)


def _dg3(a, b, ca, cb):
    (ah, al), (bh, bl) = a, b
    return _dg(ah, bh, ca, cb) + (_dg(ah, bl, ca, cb) + _dg(al, bh, ca, cb))


def _make_dot(accurate):
    def raw(a, b, form):
        ca = 0 if form == "tn" else 1
        cb = 1 if form == "nt" else 0
        if accurate:
            return _dg3(_split(a), _split(b), ca, cb)
        return _dg(a.astype(BF16), b.astype(BF16), ca, cb)

    @functools.partial(jax.custom_vjp, nondiff_argnums=(2,))
    def dot(a, b, form):
        return raw(a, b, form)

    def fwd(a, b, form):
        return raw(a, b, form), (a, b)

    def bwd(form, res, g):
        a, b = res
        if form == "nn":
            return raw(g, b, "nt"), raw(a, g, "tn")
        if form == "nt":
            return raw(g, b, "nn"), raw(g, a, "tn")
        return raw(b, g, "nt"), raw(a, g, "nn")

    dot.defvjp(fwd, bwd)
    return dot


_bdot = _make_dot(False)
_hdot = _make_dot(True)


def _inv_unit_lower(low):
    n = low.shape[-1]
    eye = (lax.broadcasted_iota(jnp.int32, (n, n), 0) == lax.broadcasted_iota(jnp.int32, (n, n), 1)).astype(F32)
    p = -low
    inv = eye + p
    ps = _split(p)
    k = 1
    while 2 * k < n:
        ps = _split(_dg3(ps, ps, 1, 0))
        inv = inv + _dg3(_split(inv), ps, 1, 0)
        k *= 2
    return inv


@jax.custom_vjp
def _unit_lower_solve(low, rhs, inv):
    return _dg3(_split(inv), _split(rhs), 1, 0)


def _uls_fwd(low, rhs, inv):
    sol = _dg3(_split(inv), _split(rhs), 1, 0)
    return sol, (inv, sol)


def _uls_bwd(res, g):
    inv, sol = res
    d_rhs = _dg3(_split(inv), _split(g), 0, 0)
    return -_dg3(_split(d_rhs), _split(sol), 1, 1), d_rhs, jnp.zeros_like(inv)


_unit_lower_solve.defvjp(_uls_fwd, _uls_bwd)


def _sigmoid(x):
    return 1.0 / (1.0 + jnp.exp(-x))


def _softplus(x):
    return jnp.maximum(x, 0.0) + jnp.log(1.0 + jnp.exp(-jnp.maximum(x, -x)))


def _gelu(x):
    return 0.5 * x * (1.0 + jnp.tanh(0.7978845608028654 * (x + 0.044715 * (x * x * x))))


def _tri(n):
    ri = lax.broadcasted_iota(jnp.int32, (n, n), 0)
    ci = lax.broadcasted_iota(jnp.int32, (n, n), 1)
    return ri, ci


def _stack(ts):
    return jnp.concatenate([t[None] for t in ts], axis=0)


@jax.custom_vjp
def _head_sums(x):
    ri, ci = _tri(x.shape[-1])
    shift = HD.bit_length() - 1
    ones = (jnp.right_shift(ri, shift) == jnp.right_shift(ci, shift)).astype(BF16)
    hi, lo = _split(x)
    return _dg(hi, ones, 1, 0) + _dg(lo, ones, 1, 0)


_head_sums.defvjp(lambda x: (_head_sums(x), None), lambda _, g: (_head_sums(g),))


def _chunk_mask_dot(x, c, running, transpose):
    ri, ci = _tri(x.shape[0])
    shift = c.bit_length() - 1
    mask = jnp.right_shift(ri, shift) == jnp.right_shift(ci, shift)
    if running:
        mask = mask & (ri >= ci)
    hi, lo = _split(x)
    m = mask.astype(BF16)
    ca = 0 if transpose else 1
    return _dg(m, hi, ca, 0) + _dg(m, lo, ca, 0)


@functools.partial(jax.custom_vjp, nondiff_argnums=(1, 2))
def _chunk_sums(x, c, running):
    return _chunk_mask_dot(x, c, running, False)


_chunk_sums.defvjp(lambda x, c, running: (_chunk_mask_dot(x, c, running, False), None),
                   lambda c, running, _, g: (_chunk_mask_dot(g, c, running, True),))


def _spread_onto(x, first, transpose):
    ri = lax.broadcasted_iota(jnp.int32, (LANES, G), 0)
    ci = lax.broadcasted_iota(jnp.int32, (LANES, G), 1)
    sel = (ri == first + jnp.right_shift(ci, HD.bit_length() - 1)).astype(BF16)
    hi, lo = _split(x)
    cb = 1 if transpose else 0
    return _dg(hi, sel, 1, cb) + _dg(lo, sel, 1, cb)


@functools.partial(jax.custom_vjp, nondiff_argnums=(1,))
def _spread(x, first):
    return _spread_onto(x, first, False)


_spread.defvjp(lambda x, first: (_spread_onto(x, first, False), None),
               lambda first, _, g: (_spread_onto(g, first, True),))


def _head_columns_dot(x, transpose):
    sel = (lax.broadcasted_iota(jnp.int32, (NH, G), 0)
           == jnp.right_shift(lax.broadcasted_iota(jnp.int32, (NH, G), 1), HD.bit_length() - 1)).astype(BF16)
    hi, lo = _split(x)
    if transpose:
        return _dg(sel, hi, 1, 1) + _dg(sel, lo, 1, 1)
    return _dg(hi, sel, 0, 0) + _dg(lo, sel, 0, 0)


@jax.custom_vjp
def _head_columns(x):
    return _head_columns_dot(x, False)


_head_columns.defvjp(lambda x: (_head_columns_dot(x, False), None), lambda _, g: (_head_columns_dot(g, True),))


def _sgu_chunk(uv, ln_w, ln_b, ws, bs):
    u = _gelu(uv[:, :G])
    v = _gelu(uv[:, G:])
    mu = jnp.mean(v, axis=-1, keepdims=True)
    vc = v - mu
    var = jnp.mean(vc * vc, axis=-1, keepdims=True)
    vn = vc * lax.rsqrt(var + EPS) * ln_w + ln_b
    ri, ci = _tri(SGU_C)
    mixed = [_bdot(jnp.where(ri >= ci, ws[h], 0.0), vn[:, HD * h:HD * (h + 1)], "nn") for h in range(NH)]
    return u * (jnp.concatenate(mixed, axis=1) + _head_columns(bs))


def _dn_tile(q, k, v, ab, z, state, inv, a_log, dt_bias, nw):
    c = DN_C
    tm = q.shape[0]
    cps = tm // c
    ri, ci = _tri(tm)
    shift = c.bit_length() - 1
    same = jnp.right_shift(ri, shift) == jnp.right_shift(ci, shift)
    per_head = lambda t: jnp.concatenate([jnp.broadcast_to(t[:, h:h + 1], (1, HD)) for h in range(NH)], axis=1)
    g = -jnp.exp(per_head(a_log)) * _softplus(_spread(ab, 0) + per_head(dt_bias))
    beta = _sigmoid(_spread(ab, NH))
    gc = _chunk_sums(g, c, True)
    gl = _chunk_sums(g, c, False)
    g4 = -jnp.exp(a_log) * _softplus(ab[:, 0:NH] + dt_bias)
    gr4 = _hdot(g4, (same & (ri <= ci)).astype(F32), "tn")
    pairs = [(slice(c * j, c * (j + 1)), h) for j in range(cps) for h in range(NH)]
    heads = lambda t: _stack([t[rows, HD * h:HD * (h + 1)] for rows, h in pairs])
    qn = q * lax.rsqrt(_head_sums(q * q) + EPS) * (HD ** -0.5)
    kn = k * lax.rsqrt(_head_sums(k * k) + EPS)
    eg = jnp.exp(gc)
    kb = kn * beta
    rhs = jnp.concatenate([heads(v * beta), heads(kb * eg)], axis=2)
    qg, kd = heads(qn * eg), heads(kn * jnp.exp(gl - gc))
    cd = _stack([jnp.exp(gl[c * j:c * j + 1, HD * h:HD * (h + 1)]) for j in range(cps) for h in range(NH)])
    qn, kn, kb, gc = heads(qn), heads(kn), heads(kb), heads(gc)
    gr = _stack([gr4[h:h + 1, rows] for rows, h in pairs])
    r2, c2 = _tri(c)
    decay = jnp.exp(jnp.where(r2 >= c2, gc - gr, -jnp.inf))
    low = jnp.where(r2 > c2, _bdot(kb, kn, "nt") * decay, 0.0)
    if inv is None:
        inv = _inv_unit_lower(low)
    sol = _unit_lower_solve(low, rhs, inv)
    u, w = sol[:, :, :HD], sol[:, :, HD:]
    attn = _bdot(qn, kn, "nt") * decay
    ys = []
    for j in range(cps):
        b = slice(NH * j, NH * (j + 1))
        v_new = u[b] - _bdot(w[b], state, "nn")
        o = _bdot(qg[b], state, "nn") + _bdot(attn[b], v_new, "nn")
        state = state * cd[b] + _bdot(kd[b], v_new, "tn")
        on = o * lax.rsqrt(jnp.mean(o * o, axis=-1, keepdims=True) + EPS) * nw
        ys.append(jnp.concatenate([on[h] for h in range(NH)], axis=1))
    return jnp.concatenate(ys, axis=0) * (z * _sigmoid(z)), state, inv


def _gla_tile(q, k, v, glr, z, state_t, w2, gate_bias, nw):
    c = GLA_C
    tm = q.shape[0]
    cps = tm // c
    w2_rows = jnp.concatenate([w2, jnp.zeros((LANES - w2.shape[0], G), F32)], axis=0)
    pre = _bdot(glr, w2_rows, "nn") + gate_bias
    log_a = (jnp.minimum(pre, 0.0) - jnp.log(1.0 + jnp.exp(-jnp.maximum(pre, -pre)))) * (1.0 / 16.0)
    gcum = _chunk_sums(log_a, c, True)
    row = lax.broadcasted_iota(jnp.int32, (c, G), 0)
    qs = q * (HD ** -0.5)
    qa, ka, qg, kl, dec = [], [], [], [], []
    for j in range(cps):
        rows = slice(c * j, c * (j + 1))
        gj = gcum[rows]
        g_mid = jnp.sum(jnp.where(row == c // 2, gj, 0.0), axis=0, keepdims=True)
        g_last = jnp.sum(log_a[rows], axis=0, keepdims=True)
        qa.append(qs[rows] * jnp.exp(gj - g_mid))
        ka.append(k[rows] * jnp.exp(g_mid - gj))
        qg.append(qs[rows] * jnp.exp(gj))
        kl.append(k[rows] * jnp.exp(g_last - gj))
        dec.append(jnp.exp(g_last))
    heads = lambda ts: _stack([t[:, HD * h:HD * (h + 1)] for t in ts for h in range(NH)])
    qa, ka, qg, kl, dec = heads(qa), heads(ka), heads(qg), heads(kl), heads(dec)
    vh = heads([v[c * j:c * (j + 1)] for j in range(cps)])
    r2, c2 = _tri(c)
    o_intra = _bdot(jnp.where(r2 >= c2, _bdot(qa, ka, "nt"), 0.0), vh, "nn")
    ys = []
    for j in range(cps):
        b = slice(NH * j, NH * (j + 1))
        o = o_intra[b] + _bdot(qg[b], state_t, "nt")
        state_t = state_t * dec[b] + _bdot(vh[b], kl[b], "tn")
        on = o * lax.rsqrt(jnp.mean(o * o, axis=-1, keepdims=True) + EPS) * nw
        ys.append(jnp.concatenate([on[h] for h in range(NH)], axis=1))
    return jnp.concatenate(ys, axis=0) * (z * _sigmoid(z)), state_t


def _rms(x, nw):
    r = lax.rsqrt(jnp.mean(x * x, axis=-1, keepdims=True) + EPS)
    xh = x * r
    return xh * nw, xh, r


def _rms_bwd(g, xh, r, nw):
    gw = g * nw
    return r * (gw - xh * jnp.mean(gw * xh, axis=-1, keepdims=True)), jnp.sum(g * xh, axis=0, keepdims=True)


def _row(tm, w, blk=0):
    return pl.BlockSpec((tm, w), lambda i: (i, blk))


def _in_proj(x, nw, wp, l, tm=512, rider=None):
    t = x.shape[0]

    def body(*refs):
        (x_ref, nw_ref, w_ref, h_ref, p_ref), ride = _split_refs(refs, 3, 2, 0, rider)
        _ride_begin(rider, ride, (t // tm,))
        h = _rms(x_ref[...], nw_ref[l:l + 1, :])[0].astype(BF16)
        h_ref[...] = h
        p_ref[...] = jnp.dot(h, w_ref[...], preferred_element_type=F32)
        _ride_end(rider, ride, (t // tm,))

    specs, operands = _host(
        rider, [_row(tm, D), _resident((DEPTH, D)), _resident((D, P))], [_row(tm, D), _row(tm, P)],
        [jax.ShapeDtypeStruct((t, D), BF16), jax.ShapeDtypeStruct((t, P), F32)], [], [x, nw, wp])
    return pl.pallas_call(body, name="in_proj", grid=(t // tm,), compiler_params=_cparams("arbitrary"),
                          **specs)(*operands)


def _gu_spec(l):
    return pl.BlockSpec((N_DEV, None, GU_SHARD, D), lambda *_: (0, l, 0, 0), pipeline_mode=pl.Buffered(1))


def _out_mlp(x, ys, w_out, nw2, w_gu_t, w_down, l, tm=256):
    t = x.shape[0]

    def body(x_ref, ya, yb, yc, yd, wo_ref, nw_ref, wgu_ref, wd_ref, mix_ref, x1_ref, h2_ref, gu_ref, act_ref, x2_ref):
        mix = jnp.concatenate([ya[...], yb[...], yc[...], yd[...]], axis=1)
        mix_ref[...] = mix
        x1 = x_ref[...] + jnp.dot(mix, wo_ref[...], preferred_element_type=F32)
        x1_ref[...] = x1
        h2 = _rms(x1, nw_ref[l:l + 1, :])[0].astype(BF16)
        h2_ref[...] = h2
        gu = _dg(h2, wgu_ref[...].reshape(2 * FF, D), 1, 1)
        gu_ref[...] = gu.astype(BF16)
        gate, up = gu[:, :FF], gu[:, FF:]
        act = (gate * _sigmoid(gate) * up).astype(BF16)
        act_ref[...] = act
        x2_ref[...] = x1 + jnp.dot(act, wd_ref[...], preferred_element_type=F32)

    return pl.pallas_call(
        body, name="out_mlp", grid=(t // tm,),
        in_specs=[_row(tm, D), _row(tm, G), _row(tm, G), _row(tm, G), _row(tm, G), _layer((D, D), l),
                  _resident((DEPTH, D)), _gu_spec(l), _layer((FF, D), l)],
        out_specs=[_row(tm, D), _row(tm, D), _row(tm, D), _row(tm, 2 * FF), _row(tm, FF), _row(tm, D)],
        out_shape=[jax.ShapeDtypeStruct((t, D), BF16), jax.ShapeDtypeStruct((t, D), F32),
                   jax.ShapeDtypeStruct((t, D), BF16), jax.ShapeDtypeStruct((t, 2 * FF), BF16),
                   jax.ShapeDtypeStruct((t, FF), BF16), jax.ShapeDtypeStruct((t, D), F32)],
        compiler_params=_cparams("parallel"))(x, *ys, w_out, nw2, w_gu_t, w_down)


def _loss_head(x, nw, target, tm=512):
    t = x.shape[0]

    def body(x_ref, nw_ref, tg_ref, loss_ref, dnw_ref, dx_ref):
        @pl.when(pl.program_id(0) == 0)
        def _():
            loss_ref[...] = jnp.zeros_like(loss_ref)
            dnw_ref[...] = jnp.zeros_like(dnw_ref)
        nw_v = nw_ref[...]
        y, xh, r = _rms(x_ref[...], nw_v)
        err = y - tg_ref[...]
        loss_ref[...] += 0.5 * jnp.sum(err * err) * (1.0 / D)
        dx, dnw = _rms_bwd(err * (1.0 / D), xh, r, nw_v)
        dx_ref[...] = dx
        dnw_ref[...] += dnw

    return pl.pallas_call(
        body, name="loss_head", grid=(t // tm,),
        in_specs=[_row(tm, D), _resident((1, D)), _row(tm, D)],
        out_specs=[_acc((8, LANES)), _acc((1, D)), _row(tm, D)],
        out_shape=[jax.ShapeDtypeStruct((8, LANES), F32), jax.ShapeDtypeStruct((1, D), F32),
                   jax.ShapeDtypeStruct((t, D), F32)],
        compiler_params=_cparams("arbitrary"))(x, nw, target)


def _mlp_out_bwd(dx2, x1, gu, nw2, w_down, w_gu, w_out, l, tm=256, rider=None):
    t = dx2.shape[0]

    def body(*refs):
        (dx2_ref, x1_ref, gu_ref, nw_ref, wd_ref, wgu_ref, wo_ref, dgu_ref, dx1_ref, dmix_ref, dnw_ref), ride = \
            _split_refs(refs, 7, 4, 0, rider)
        _ride_begin(rider, ride, (t // tm,))

        @pl.when(pl.program_id(0) == 0)
        def _():
            dnw_ref[...] = jnp.zeros_like(dnw_ref)
        dx2 = dx2_ref[...]
        dact = _dg(dx2.astype(BF16), wd_ref[...], 1, 1)
        gu = gu_ref[...].astype(F32)
        gate, up = gu[:, :FF], gu[:, FF:]
        s = _sigmoid(gate)
        dgu = jnp.concatenate([dact * up * (s * (1.0 + gate * (1.0 - s))), dact * (gate * s)], axis=1).astype(BF16)
        dgu_ref[...] = dgu
        dh2 = jnp.dot(dgu, wgu_ref[...].reshape(2 * FF, D), preferred_element_type=F32)
        nw_v = nw_ref[l:l + 1, :]
        _, xh, r = _rms(x1_ref[...], nw_v)
        dxn, dnw = _rms_bwd(dh2, xh, r, nw_v)
        dx1 = dx2 + dxn
        dx1_ref[...] = dx1
        dnw_ref[...] += dnw
        dmix_ref[...] = _dg(dx1.astype(BF16), wo_ref[...], 1, 1)
        _ride_end(rider, ride, (t // tm,))

    specs, operands = _host(
        rider, [_row(tm, D), _row(tm, D), _row(tm, 2 * FF), _resident((DEPTH, D)), _layer((FF, D), l), _gu_spec(l),
                _layer((D, D), l)],
        [_row(tm, 2 * FF), _row(tm, D), _row(tm, D), _acc((1, D))],
        [jax.ShapeDtypeStruct((t, 2 * FF), BF16), jax.ShapeDtypeStruct((t, D), F32),
         jax.ShapeDtypeStruct((t, D), F32), jax.ShapeDtypeStruct((1, D), F32)],
        [], [dx2, x1, gu, nw2, w_down, w_gu, w_out])
    return pl.pallas_call(body, name="mlp_out_bwd", grid=(t // tm,), compiler_params=_cparams("arbitrary"),
                          **specs)(*operands)


_DP_WIDTHS = (768, 768, 512, 768, 256, 256, 128, 128)


def _in_proj_bwd(dps, x, dx1, nw, wp, l, tm=512):
    t = x.shape[0]

    def body(*refs):
        dp_refs, (x_ref, dx1_ref, nw_ref, w_ref, dp_ref, dx_ref, dnw_ref) = refs[:8], refs[8:]

        @pl.when(pl.program_id(0) == 0)
        def _():
            dnw_ref[...] = jnp.zeros_like(dnw_ref)
        dp = jnp.concatenate([r[...] for r in dp_refs], axis=1)
        dp_ref[...] = dp
        dh = _dg(dp, w_ref[...], 1, 1)
        nw_v = nw_ref[l:l + 1, :]
        _, xh, r = _rms(x_ref[...], nw_v)
        dxn, dnw = _rms_bwd(dh, xh, r, nw_v)
        dx_ref[...] = dx1_ref[...] + dxn
        dnw_ref[...] += dnw

    return pl.pallas_call(
        body, name="in_proj_bwd", grid=(t // tm,),
        in_specs=[_row(tm, w) for w in _DP_WIDTHS] + [_row(tm, D), _row(tm, D), _resident((DEPTH, D)), _resident((D, P))],
        out_specs=[_row(tm, P), _row(tm, D), _acc((1, D))],
        out_shape=[jax.ShapeDtypeStruct((t, P), BF16), jax.ShapeDtypeStruct((t, D), F32),
                   jax.ShapeDtypeStruct((1, D), F32)],
        compiler_params=_cparams("arbitrary"))(*dps, x, dx1, nw, wp)


def _accumulate(acc, o_ref, t_axis, term):
    @pl.when(pl.program_id(t_axis) == 0)
    def _():
        acc[...] = jnp.zeros_like(acc)
    acc[...] += term

    @pl.when(pl.program_id(t_axis) == pl.num_programs(t_axis) - 1)
    def _():
        o_ref[...] = acc[...].astype(o_ref.dtype)


WGRAD_TOKENS = 2048


def _matmul_tn(a, b, name, out_dtype, tn=512, tt=WGRAD_TOKENS, rider=None):
    t, m = a.shape
    n = b.shape[1]
    tt = min(tt, t)
    grid = (n // tn, t // tt)

    def body(*refs):
        (a_ref, b_ref, o_ref, acc), ride = _split_refs(refs, 2, 1, 1, rider)
        _ride_begin(rider, ride, grid)
        _accumulate(acc, o_ref, 1, _dg(a_ref[...].astype(BF16), b_ref[...].astype(BF16), 0, 0))
        _ride_end(rider, ride, grid)

    specs, operands = _host(
        rider, [pl.BlockSpec((tt, m), lambda j, i: (i, 0)), pl.BlockSpec((tt, tn), lambda j, i: (i, j))],
        [pl.BlockSpec((m, tn), lambda j, i: (0, j))], [jax.ShapeDtypeStruct((m, n), out_dtype)],
        [pltpu.VMEM((m, tn), F32)], [a, b])
    out = pl.pallas_call(body, name=name, grid=grid, compiler_params=_cparams("arbitrary", "arbitrary"), **specs)(*operands)
    return out[0] if rider is None else out


WGRAD_ROWS = 2 * GU_SHARD


def _wgrad_rows(a, b, name, tt=WGRAD_TOKENS, rider=None):
    t, m = a.shape
    tt = min(tt, t)
    grid = (m // WGRAD_ROWS, t // tt)

    def body(*refs):
        (a_ref, b_ref, o_ref, acc), ride = _split_refs(refs, 2, 1, 1, rider)
        _ride_begin(rider, ride, grid)
        _accumulate(acc, o_ref, 1, _dg(a_ref[...], b_ref[...].astype(BF16), 0, 0))
        _ride_end(rider, ride, grid)

    specs, operands = _host(
        rider, [pl.BlockSpec((tt, WGRAD_ROWS), lambda j, i: (i, j)), pl.BlockSpec((tt, D), lambda j, i: (i, 0))],
        [pl.BlockSpec((WGRAD_ROWS, D), lambda j, i: (j, 0))], [jax.ShapeDtypeStruct((m, D), BF16)],
        [pltpu.VMEM((WGRAD_ROWS, D), F32)], [a, b])
    out = pl.pallas_call(body, name=name, grid=grid, compiler_params=_cparams("arbitrary", "arbitrary"), **specs)(*operands)
    return out[0] if rider is None else out


def _relayout_w_in(g, tr=256):
    def body(w_ref, o_ref):
        full = jnp.concatenate([w_ref[d] for d in range(N_DEV)], axis=1)
        parts = []
        for a, b, z in _PAD_SEGS:
            parts.append(full[:, a:b])
            if z:
                parts.append(jnp.zeros((tr, z), full.dtype))
        o_ref[...] = jnp.concatenate(parts, axis=1)

    return pl.pallas_call(
        body, name="relayout_w_in", grid=(D // tr,),
        in_specs=[pl.BlockSpec((N_DEV, tr, IN_SHARD), lambda i: (0, i, 0))], out_specs=_row(tr, P),
        out_shape=jax.ShapeDtypeStruct((D, P), g.dtype), compiler_params=_cparams("parallel"))(g)


def _scatter_w_in_grad(gp, tr=256):
    def body(g_ref, o_ref):
        g = g_ref[...]
        pieces, off = [], 0
        for a, b, z in _PAD_SEGS:
            pieces.append((a, g[:, off:off + (b - a)]))
            off += (b - a) + z
        nat = jnp.concatenate([piece for _, piece in sorted(pieces, key=lambda ap: ap[0])], axis=1)
        for d in range(N_DEV):
            o_ref[d] = nat[:, IN_SHARD * d:IN_SHARD * (d + 1)].astype(BF16)

    return pl.pallas_call(
        body, name="scatter_w_in_grad", grid=(D // tr,),
        in_specs=[pl.BlockSpec((tr, P), lambda i: (i, 0))],
        out_specs=pl.BlockSpec((N_DEV, tr, IN_SHARD), lambda i: (0, i, 0)),
        out_shape=jax.ShapeDtypeStruct((N_DEV, D, IN_SHARD), BF16),
        compiler_params=_cparams("parallel"))(gp)


_A_BLK = 3


def _sgu_specs(l):
    return [_resident((DEPTH, G)), _resident((DEPTH, G)), _layer((NH, SGU_C, SGU_C), l), _layer((NH, SGU_C), l)]


def _sgu_fwd(p, ln_w, ln_b, ws, bs, l, tm=256):
    t = p.shape[0]

    def body(uv_ref, lw_ref, lb_ref, ws_ref, bs_ref, y_ref):
        ws_v = [ws_ref[h] for h in range(NH)]
        for c in range(tm // SGU_C):
            rows = pl.ds(c * SGU_C, SGU_C)
            y_ref[rows, :] = _sgu_chunk(uv_ref[rows, :], lw_ref[l:l + 1, :], lb_ref[l:l + 1, :], ws_v,
                                        bs_ref[...]).astype(BF16)

    return pl.pallas_call(
        body, name="sgu_fwd", grid=(t // tm,),
        in_specs=[_row(tm, 2 * G, _A_BLK)] + _sgu_specs(l), out_specs=_row(tm, G),
        out_shape=jax.ShapeDtypeStruct((t, G), BF16),
        compiler_params=_cparams("parallel"))(p, ln_w, ln_b, ws, bs)


def _sgu_bwd(p, dmix, ln_w, ln_b, ws, bs, l, tm=256):
    t = p.shape[0]

    def body(uv_ref, dy_ref, lw_ref, lb_ref, ws_ref, bs_ref, duv_ref, dlw_ref, dlb_ref, dws_ref, dbs_ref):
        @pl.when(pl.program_id(0) == 0)
        def _():
            for r in (dlw_ref, dlb_ref, dws_ref, dbs_ref):
                r[...] = jnp.zeros_like(r)
        ws_v = [ws_ref[h] for h in range(NH)]
        for c in range(tm // SGU_C):
            rows = pl.ds(c * SGU_C, SGU_C)
            _, vjp = jax.vjp(_sgu_chunk, uv_ref[rows, :], lw_ref[l:l + 1, :], lb_ref[l:l + 1, :], ws_v, bs_ref[...])
            duv, dlw, dlb, dws, dbs = vjp(dy_ref[rows, :])
            duv_ref[rows, :] = duv.astype(BF16)
            dlw_ref[...] += dlw
            dlb_ref[...] += dlb
            dbs_ref[...] += dbs
            for h in range(NH):
                dws_ref[h] += dws[h]

    return pl.pallas_call(
        body, name="sgu_bwd", grid=(t // tm,),
        in_specs=[_row(tm, 2 * G, _A_BLK), _row(tm, G, 0)] + _sgu_specs(l),
        out_specs=[_row(tm, 2 * G), _acc((1, G)), _acc((1, G)), _acc((NH, SGU_C, SGU_C)), _acc((NH, SGU_C))],
        out_shape=[jax.ShapeDtypeStruct((t, 2 * G), BF16), jax.ShapeDtypeStruct((1, G), F32),
                   jax.ShapeDtypeStruct((1, G), F32), jax.ShapeDtypeStruct((NH, SGU_C, SGU_C), F32),
                   jax.ShapeDtypeStruct((NH, SGU_C), F32)],
        compiler_params=_cparams("arbitrary"))(p, dmix, ln_w, ln_b, ws, bs)


HALO = 8


def _prev_halo(tm, width, col):
    return pl.BlockSpec((HALO, width), lambda i: (jnp.maximum(i * (tm // HALO) - 1, 0), col))


def _next_halo(tm, width, col, t):
    return pl.BlockSpec((HALO, width), lambda i: (jnp.minimum((i + 1) * (tm // HALO), t // HALO - 1), col))


def _with_prev(halo_ref, x_ref):
    halo = jnp.where(pl.program_id(0) == 0, 0.0, halo_ref[...])
    return jnp.concatenate([halo, x_ref[...]], axis=0)


def _with_next(x_ref, halo_ref):
    halo = jnp.where(pl.program_id(0) == pl.num_programs(0) - 1, 0.0, halo_ref[...])
    return jnp.concatenate([x_ref[...], halo], axis=0)


def _delay(ext, j):
    return ext if j == 0 else pltpu.roll(ext, j, axis=0)


def _advance(ext, j):
    return ext if j == 0 else pltpu.roll(ext, ext.shape[0] - j, axis=0)


def _causal_conv(ext, w, tm):
    kw = w.shape[0]
    out = None
    for k in range(kw):
        term = _delay(ext, kw - 1 - k)[HALO:HALO + tm] * w[k:k + 1, :]
        out = term if out is None else out + term
    return out


def _causal_conv_t(ext, w, tm):
    kw = w.shape[0]
    out = None
    for k in range(kw):
        term = _advance(ext, kw - 1 - k)[0:tm] * w[k:k + 1, :]
        out = term if out is None else out + term
    return out


def _conv_wgrad(ext, dy, kw, tm):
    return jnp.concatenate(
        [jnp.sum(_delay(ext, kw - 1 - k)[HALO:HALO + tm] * dy, axis=0, keepdims=True) for k in range(kw)], axis=0)


_B_GB, _B_GC, _B_H = 8, 9, 10
_C_QKV, _D_QKV = 0, 1
_C_Z, _D_Z = 11, 12
_C_AB, _D_GLR = 26, 27


def _sconv_fwd(p, w, l, tm=512):
    t = p.shape[0]

    def body(gb_ref, gc_ref, h_ref, gc_halo, h_halo, w_ref, y_ref):
        s_ext = _with_prev(gc_halo, gc_ref) * _with_prev(h_halo, h_ref)
        y_ref[...] = (gb_ref[...] * _causal_conv(s_ext, w_ref[...], tm)).astype(BF16)

    return pl.pallas_call(
        body, name="sconv_fwd", grid=(t // tm,),
        in_specs=[_row(tm, G, _B_GB), _row(tm, G, _B_GC), _row(tm, G, _B_H), _prev_halo(tm, G, _B_GC),
                  _prev_halo(tm, G, _B_H), _layer((3, G), l)],
        out_specs=_row(tm, G), out_shape=jax.ShapeDtypeStruct((t, G), BF16),
        compiler_params=_cparams("parallel"))(p, p, p, p, p, w)


def _sconv_bwd(p, dmix, w, l, tm=512):
    t = p.shape[0]

    def body(gb_ref, gc_ref, h_ref, gc_halo, h_halo, gb_next, dy_ref, dy_next, w_ref, dp_ref, dw_ref):
        @pl.when(pl.program_id(0) == 0)
        def _():
            dw_ref[...] = jnp.zeros_like(dw_ref)
        w_v = w_ref[...]
        s_ext = _with_prev(gc_halo, gc_ref) * _with_prev(h_halo, h_ref)
        dout = dy_ref[...]
        d_gb = dout * _causal_conv(s_ext, w_v, tm)
        dconv_ext = _with_next(dy_ref, dy_next) * _with_next(gb_ref, gb_next)
        ds = _causal_conv_t(dconv_ext, w_v, tm)
        dw_ref[...] += _conv_wgrad(s_ext, dconv_ext[0:tm], 3, tm)
        dp_ref[...] = jnp.concatenate([d_gb, ds * h_ref[...], ds * gc_ref[...]], axis=1).astype(BF16)

    return pl.pallas_call(
        body, name="sconv_bwd", grid=(t // tm,),
        in_specs=[_row(tm, G, _B_GB), _row(tm, G, _B_GC), _row(tm, G, _B_H), _prev_halo(tm, G, _B_GC),
                  _prev_halo(tm, G, _B_H), _next_halo(tm, G, _B_GB, t), _row(tm, G, 1), _next_halo(tm, G, 1, t),
                  _layer((3, G), l)],
        out_specs=[_row(tm, 3 * G), _acc((3, G))],
        out_shape=[jax.ShapeDtypeStruct((t, 3 * G), BF16), jax.ShapeDtypeStruct((3, G), F32)],
        compiler_params=_cparams("arbitrary"))(p, p, p, p, p, p, dmix, dmix, w)


def _qkv_conv_fwd(p, w, l, tm=512):
    t = p.shape[0]

    def body(x_ref, x_halo, w_ref, y_ref):
        c = _causal_conv(_with_prev(x_halo, x_ref), w_ref[...], tm)
        y_ref[...] = c * _sigmoid(c)

    return pl.pallas_call(
        body, name="qkv_conv_fwd", grid=(t // tm,),
        in_specs=[_row(tm, 3 * G, _C_QKV), _prev_halo(tm, 3 * G, _C_QKV), _layer((4, 3 * G), l)],
        out_specs=_row(tm, 3 * G), out_shape=jax.ShapeDtypeStruct((t, 3 * G), F32),
        compiler_params=_cparams("parallel"))(p, p, w)


def _qkv_conv_bwd(p, dy, w, l, tm=512):
    t = p.shape[0]

    def body(x_ref, x_halo, x_next, dy_ref, dy_next, w_ref, dx_ref, dw_ref):
        @pl.when(pl.program_id(0) == 0)
        def _():
            dw_ref[...] = jnp.zeros_like(dw_ref)
        w_v = w_ref[...]
        x_all = jnp.concatenate([_with_prev(x_halo, x_ref), jnp.where(
            pl.program_id(0) == pl.num_programs(0) - 1, 0.0, x_next[...])], axis=0)
        c = _causal_conv(x_all, w_v, tm + HALO)
        s = _sigmoid(c)
        dc_ext = _with_next(dy_ref, dy_next) * (s * (1.0 + c * (1.0 - s)))
        dx_ref[...] = _causal_conv_t(dc_ext, w_v, tm).astype(BF16)
        dw_ref[...] += _conv_wgrad(x_all[0:HALO + tm], dc_ext[0:tm], 4, tm)

    return pl.pallas_call(
        body, name="qkv_conv_bwd", grid=(t // tm,),
        in_specs=[_row(tm, 3 * G, _C_QKV), _prev_halo(tm, 3 * G, _C_QKV), _next_halo(tm, 3 * G, _C_QKV, t),
                  _row(tm, 3 * G), _next_halo(tm, 3 * G, 0, t), _layer((4, 3 * G), l)],
        out_specs=[_row(tm, 3 * G), _acc((4, 3 * G))],
        out_shape=[jax.ShapeDtypeStruct((t, 3 * G), BF16), jax.ShapeDtypeStruct((4, 3 * G), F32)],
        compiler_params=_cparams("arbitrary"))(p, p, p, dy, dy, w)


_STATE = pl.BlockSpec((1, NH, HD, HD), lambda i: (i, 0, 0, 0))


def _dn_specs():
    return [_resident((DEPTH, NH)), _resident((DEPTH, NH)), _resident((DEPTH, HD))]


def _dn_fwd(qkv, p, params, l, cps=4, rider=None, forward_at=1.0):
    t = qkv.shape[0]
    tm = DN_C * cps
    nb = cps * NH

    def body(*refs):
        (qkv_ref, z_ref, ab_ref, alog_ref, dt_ref, nw_ref, y_ref, st_ref, inv_ref, state), ride = _split_refs(
            refs, 6, 3, 1, rider)
        _ride_begin(rider, ride, (t // tm,))

        @pl.when(pl.program_id(0) == 0)
        def _():
            state[...] = jnp.zeros_like(state)
        st_ref[0] = state[...]
        y, new, inv = _dn_tile(qkv_ref[:, 0:G], qkv_ref[:, G:2 * G], qkv_ref[:, 2 * G:3 * G], ab_ref[...], z_ref[...],
                               state[...], None, alog_ref[l:l + 1, :], dt_ref[l:l + 1, :], nw_ref[l:l + 1, :])
        y_ref[...] = y.astype(BF16)
        inv_ref[...] = inv
        state[...] = new
        _ride_end(rider, ride, (t // tm,), forward_at)

    specs, operands = _host(
        rider, [_row(tm, 3 * G), _row(tm, G, _C_Z), _row(tm, LANES, _C_AB)] + _dn_specs(),
        [_row(tm, G), _STATE, pl.BlockSpec((nb, DN_C, DN_C), lambda i: (i, 0, 0))],
        [jax.ShapeDtypeStruct((t, G), BF16), jax.ShapeDtypeStruct((t // tm, NH, HD, HD), F32),
         jax.ShapeDtypeStruct((t // DN_C * NH, DN_C, DN_C), F32)],
        [pltpu.VMEM((NH, HD, HD), F32)], [qkv, p, p, *params])
    return pl.pallas_call(body, name="dn_fwd", grid=(t // tm,), compiler_params=_cparams("arbitrary"), **specs)(*operands)


def _dn_bwd(qkv, p, states, inv, dmix, params, l, cps=4, rider=None):
    t = qkv.shape[0]
    tm = DN_C * cps
    n = t // tm
    nb = cps * NH

    def body(*refs):
        (qkv_ref, z_ref, ab_ref, st_ref, inv_ref, dy_ref, alog_ref, dt_ref, nw_ref,
         dqkv_ref, dz_ref, dab_ref, dalog_ref, ddt_ref, dnw_ref, dstate), ride = _split_refs(refs, 9, 6, 1, rider)
        _ride_begin(rider, ride, (n,))
        dprm_refs = (dalog_ref, ddt_ref, dnw_ref)

        @pl.when(pl.program_id(0) == 0)
        def _():
            dstate[...] = jnp.zeros_like(dstate)
            for r in dprm_refs:
                r[...] = jnp.zeros_like(r)
        inv_v = inv_ref[...]
        tile = lambda q, k, v, ab, z, st, alog, dt, nw: _dn_tile(q, k, v, ab, z, st, inv_v, alog, dt, nw)[:2]
        _, vjp = jax.vjp(tile, qkv_ref[:, 0:G], qkv_ref[:, G:2 * G], qkv_ref[:, 2 * G:3 * G], ab_ref[...], z_ref[...],
                         st_ref[0], alog_ref[l:l + 1, :], dt_ref[l:l + 1, :], nw_ref[l:l + 1, :])
        dq, dk, dv, dab, dz, dst, *dprm = vjp((dy_ref[...], dstate[...]))
        dqkv_ref[...] = jnp.concatenate([dq, dk, dv], axis=1)
        dz_ref[...] = dz.astype(BF16)
        dab_ref[...] = dab.astype(BF16)
        dstate[...] = dst
        for r, g in zip(dprm_refs, dprm):
            r[...] += g
        _ride_end(rider, ride, (n,))

    rev = lambda w, blk: pl.BlockSpec((tm, w), lambda i: (n - 1 - i, blk))
    specs, operands = _host(
        rider, [rev(3 * G, 0), rev(G, _C_Z), rev(LANES, _C_AB),
                pl.BlockSpec((1, NH, HD, HD), lambda i: (n - 1 - i, 0, 0, 0)),
                pl.BlockSpec((nb, DN_C, DN_C), lambda i: (n - 1 - i, 0, 0)), rev(G, 2)] + _dn_specs(),
        [rev(3 * G, 0), rev(G, 0), rev(LANES, 0), _acc((1, NH)), _acc((1, NH)), _acc((1, HD))],
        [jax.ShapeDtypeStruct((t, 3 * G), F32), jax.ShapeDtypeStruct((t, G), BF16),
         jax.ShapeDtypeStruct((t, LANES), BF16), jax.ShapeDtypeStruct((1, NH), F32),
         jax.ShapeDtypeStruct((1, NH), F32), jax.ShapeDtypeStruct((1, HD), F32)],
        [pltpu.VMEM((NH, HD, HD), F32)], [qkv, p, p, states, inv, dmix, *params])
    return pl.pallas_call(body, name="dn_bwd", grid=(n,), compiler_params=_cparams("arbitrary"), **specs)(*operands)


def _gla_specs(l):
    return [_layer((16, G), l), _resident((DEPTH, G)), _resident((DEPTH, HD))]


def _gla_fwd(p, params, l, cps=4, rider=None):
    t = p.shape[0]
    tm = GLA_C * cps

    def body(*refs):
        (qkv_ref, z_ref, glr_ref, w2_ref, gb_ref, nw_ref, y_ref, st_ref, state), ride = _split_refs(refs, 6, 2, 1, rider)
        _ride_begin(rider, ride, (t // tm,))

        @pl.when(pl.program_id(0) == 0)
        def _():
            state[...] = jnp.zeros_like(state)
        st_ref[0] = state[...]
        y, new = _gla_tile(qkv_ref[:, 0:G], qkv_ref[:, G:2 * G], qkv_ref[:, 2 * G:3 * G], glr_ref[...], z_ref[...],
                           state[...], w2_ref[...], gb_ref[l:l + 1, :], nw_ref[l:l + 1, :])
        y_ref[...] = y.astype(BF16)
        state[...] = new
        _ride_end(rider, ride, (t // tm,))

    specs, operands = _host(
        rider, [_row(tm, 3 * G, _D_QKV), _row(tm, G, _D_Z), _row(tm, LANES, _D_GLR)] + _gla_specs(l),
        [_row(tm, G), _STATE],
        [jax.ShapeDtypeStruct((t, G), BF16), jax.ShapeDtypeStruct((t // tm, NH, HD, HD), F32)],
        [pltpu.VMEM((NH, HD, HD), F32)], [p, p, p, *params])
    return pl.pallas_call(body, name="gla_fwd", grid=(t // tm,), compiler_params=_cparams("arbitrary"), **specs)(*operands)


def _gla_bwd(p, states, dmix, params, l, cps=4):
    t = p.shape[0]
    tm = GLA_C * cps
    n = t // tm

    def body(qkv_ref, z_ref, glr_ref, st_ref, dy_ref, w2_ref, gb_ref, nw_ref,
             dqkv_ref, dz_ref, dglr_ref, dw2_ref, dgb_ref, dnw_ref, dstate):
        dprm_refs = (dw2_ref, dgb_ref, dnw_ref)

        @pl.when(pl.program_id(0) == 0)
        def _():
            dstate[...] = jnp.zeros_like(dstate)
            for r in dprm_refs:
                r[...] = jnp.zeros_like(r)
        _, vjp = jax.vjp(_gla_tile, qkv_ref[:, 0:G], qkv_ref[:, G:2 * G], qkv_ref[:, 2 * G:3 * G], glr_ref[...],
                         z_ref[...], st_ref[0], w2_ref[...], gb_ref[l:l + 1, :], nw_ref[l:l + 1, :])
        dq, dk, dv, dglr, dz, dst, *dprm = vjp((dy_ref[...], dstate[...]))
        dqkv_ref[...] = jnp.concatenate([dq, dk, dv], axis=1).astype(BF16)
        dz_ref[...] = dz.astype(BF16)
        dglr_ref[...] = dglr.astype(BF16)
        dstate[...] = dst
        for r, g in zip(dprm_refs, dprm):
            r[...] += g

    rev = lambda w, blk: pl.BlockSpec((tm, w), lambda i: (n - 1 - i, blk))
    return pl.pallas_call(
        body, name="gla_bwd", grid=(n,),
        in_specs=[rev(3 * G, _D_QKV), rev(G, _D_Z), rev(LANES, _D_GLR),
                  pl.BlockSpec((1, NH, HD, HD), lambda i: (n - 1 - i, 0, 0, 0)), rev(G, 3)] + _gla_specs(l),
        out_specs=[rev(3 * G, 0), rev(G, 0), rev(LANES, 0), _acc((16, G)), _acc((1, G)), _acc((1, HD))],
        out_shape=[jax.ShapeDtypeStruct((t, 3 * G), BF16), jax.ShapeDtypeStruct((t, G), BF16),
                   jax.ShapeDtypeStruct((t, LANES), BF16), jax.ShapeDtypeStruct((16, G), F32),
                   jax.ShapeDtypeStruct((1, G), F32), jax.ShapeDtypeStruct((1, HD), F32)],
        scratch_shapes=[pltpu.VMEM((NH, HD, HD), F32)],
        compiler_params=_cparams("arbitrary"))(p, p, p, states, dmix, *params)


def _adamw_math(w, g, m, v):
    m = ADAM_B1 * m + (1.0 - ADAM_B1) * g
    v = ADAM_B2 * v + (1.0 - ADAM_B2) * (g * g)
    m_hat = m / (1.0 - ADAM_B1 ** ADAM_STEP)
    v_hat = v / (1.0 - ADAM_B2 ** ADAM_STEP)
    return -ADAM_LR * (m_hat / (jnp.sqrt(v_hat) + ADAM_EPS) + ADAM_WD * w), m, v


def _adamw_large(parts, w, m, v, name, tr, rider=None):
    _, r, c = w.shape
    grid = (DEPTH, r // tr)

    def body(*refs):
        (p0_ref, p1_ref, w_ref, m_ref, v_ref, g_ref, d_ref, nm_ref, nv_ref), ride = _split_refs(refs, 5, 4, 0, rider)
        _ride_begin(rider, ride, grid)

        def total(p_ref):
            g = p_ref[0].astype(F32)
            for k in range(1, N_DEV):
                g = g + p_ref[k].astype(F32)
            return g

        g = jnp.where(pl.program_id(0) == 0, total(p0_ref), total(p1_ref))
        g_ref[...] = g
        d_ref[...], nm_ref[...], nv_ref[...] = _adamw_math(w_ref[...], g, m_ref[...], v_ref[...])
        _ride_end(rider, ride, grid)

    blk = pl.BlockSpec((None, tr, c), lambda l, i: (l, i, 0))
    part = pl.BlockSpec((N_DEV, tr, c), lambda l, i: (0, i, 0))
    specs, operands = _host(rider, [part, part, blk, blk, blk], [blk] * 4, [jax.ShapeDtypeStruct(w.shape, F32)] * 4, [],
                            [*parts, w, m, v])
    return pl.pallas_call(body, name=name, grid=grid, compiler_params=_cparams("arbitrary", "arbitrary"),
                          **specs)(*operands)


_SLAB_AT = {
    "sgu_w_spatial": (0, 0, SGU_C, LANES),
    "sgu_b_spatial": (128, 0, NH, SGU_C),
    "norm1_w": (132, 0, 1, D),
    "norm2_w": (133, 0, 1, D),
    "sgu_ln_w": (134, 0, 1, G),
    "sgu_ln_b": (134, 256, 1, G),
    "gla_gate_bias": (134, 512, 1, G),
    "dn_norm_w": (134, 768, 1, HD),
    "gla_norm_w": (134, 896, 1, HD),
    "dn_a_log": (135, 0, 1, NH),
    "dn_dt_bias": (135, 128, 1, NH),
    "gla_w_gate2": (136, 0, 16, G),
    "dn_conv_w": (136, 256, 4, 3 * G),
    "sc_conv_w": (140, 256, 3, G),
}
_LAYER_ROWS = 152
_FINAL_ROW = DEPTH * _LAYER_ROWS
_SLAB_ROWS, _SLAB_COLS = _FINAL_ROW + 8, 1024
_SLAB_ORDER = tuple(_SLAB_AT)
_SHARDED_SMALL = ("sc_conv_w", "dn_conv_w", "gla_w_gate2")
_REPLICATED = tuple(k for k in _SLAB_ORDER if k not in _SHARDED_SMALL)


def _region(name, l, h=0):
    r0, c0, nr, nc = _SLAB_AT[name]
    return slice(_LAYER_ROWS * l + r0, _LAYER_ROWS * l + r0 + nr), slice(c0 + nc * h, c0 + nc * (h + 1))


def _pack_small(layer_grads, d_final):
    def body(*refs):
        slab = refs[-1]
        slab[...] = jnp.zeros_like(slab)
        it = iter(refs[:-1])
        for l in range(DEPTH):
            for name in _SLAB_ORDER:
                ref = next(it)
                if name == "sgu_w_spatial":
                    for h in range(NH):
                        slab[_region(name, l, h)] = ref[h]
                else:
                    slab[_region(name, l)] = ref[...]
        slab[_FINAL_ROW:_FINAL_ROW + 1, :] = next(it)[...]

    flat = [layer_grads[l][name] for l in range(DEPTH) for name in _SLAB_ORDER] + [d_final]
    return pl.pallas_call(body, name="pack_small_grads", out_shape=jax.ShapeDtypeStruct((_SLAB_ROWS, _SLAB_COLS), F32),
                          compiler_params=_cparams())(*flat)


def _adamw_small(parts, w, m, v):
    names = _REPLICATED + ("final_norm_w",)
    n_in = len(names)

    def body(*refs):
        def total(rows, cols):
            g = refs[0][0, rows, cols]
            for k in range(1, N_DEV):
                g = g + refs[0][k, rows, cols]
            return g

        w_refs = dict(zip(names, refs[1:1 + n_in]))
        m_refs = dict(zip(names, refs[1 + n_in:1 + 2 * n_in]))
        v_refs = dict(zip(names, refs[1 + 2 * n_in:1 + 3 * n_in]))
        outs = refs[1 + 3 * n_in:]
        out_refs = [dict(zip(names, outs[j * n_in:(j + 1) * n_in])) for j in range(4)]
        full_refs = dict(zip(_SHARDED_SMALL, outs[4 * n_in:]))

        def update(name, idx, g):
            res = (g,) + _adamw_math(w_refs[name][idx], g, m_refs[name][idx], v_refs[name][idx])
            for o, val in zip(out_refs, res):
                o[name][idx] = val

        for l in range(DEPTH):
            for name in _REPLICATED:
                if name == "sgu_w_spatial":
                    for h in range(NH):
                        update(name, (l, h), total(*_region(name, l, h)))
                elif name == "sgu_b_spatial":
                    update(name, (l,), total(*_region(name, l)))
                else:
                    update(name, (slice(l, l + 1), slice(None)), total(*_region(name, l)))
            for name in _SHARDED_SMALL:
                full_refs[name][l] = total(*_region(name, l))
        update("final_norm_w", (slice(None), slice(None)), total(slice(_FINAL_ROW, _FINAL_ROW + 1), slice(None)))

    shapes = [jax.ShapeDtypeStruct(w[k].shape, F32) for k in names]
    full_shapes = [jax.ShapeDtypeStruct((DEPTH,) + _SLAB_AT[k][2:], F32) for k in _SHARDED_SMALL]
    outs = pl.pallas_call(body, name="adamw_small", out_shape=shapes * 4 + full_shapes, compiler_params=_cparams())(
        parts, *[w[k] for k in names], *[m[k] for k in names], *[v[k] for k in names])
    result = [dict(zip(names, outs[j * n_in:(j + 1) * n_in])) for j in range(4)]
    return result, dict(zip(_SHARDED_SMALL, outs[4 * n_in:]))


def _adamw_shards(g, w, m, v):
    names = _SHARDED_SMALL
    n = len(names)

    def body(*refs):
        for j in range(n):
            g_v = refs[j][...]
            res = _adamw_math(refs[n + j][...], g_v, refs[2 * n + j][...], refs[3 * n + j][...])
            for o, val in zip(refs[4 * n + j::n], res):
                o[...] = val

    shapes = [jax.ShapeDtypeStruct(w[k].shape, F32) for k in names]
    outs = pl.pallas_call(body, name="adamw_small_shards", out_shape=shapes * 3, compiler_params=_cparams())(
        *[g[k] for k in names], *[w[k] for k in names], *[m[k] for k in names], *[v[k] for k in names])
    return [dict(zip(names, outs[j * n:(j + 1) * n])) for j in range(3)]


_ANY = pl.BlockSpec(memory_space=pl.ANY)


def _place():
    return lax.axis_index("x"), lax.axis_index("y"), lax.axis_index("c")


def _sems(n):
    return [pltpu.SemaphoreType.DMA((n, 7)), pltpu.SemaphoreType.DMA((n, 7)), pltpu.SemaphoreType.DMA((n,))]


class _Gather:
    def __init__(self, blocks, second):
        self.inputs, self.second = list(blocks), list(second)
        self.out_shape = [jax.ShapeDtypeStruct((a.shape[0], N_DEV) + a.shape[1:] if s else (N_DEV,) + a.shape, a.dtype)
                          for a, s in zip(blocks, second)]
        self.scratch = _sems(len(blocks))

    def _copies(self, refs):
        x_refs, out_refs, (send_sems, recv_sems, local_sems) = refs
        n = len(x_refs)
        x, y, c = _place()
        me, sibling = (x, y, c), (x, y, 1 - c)
        chips = [(1 - x, y), (x, 1 - y), (1 - x, 1 - y)]

        def slot(j, px, py, pc):
            idx = 4 * px + 2 * py + pc
            return out_refs[j].at[:, idx] if self.second[j] else out_refs[j].at[idx]

        def copies(k, block, to, own=False):
            return [pltpu.make_async_remote_copy(
                src_ref=x_refs[j] if own else slot(j, *block), dst_ref=slot(j, *block),
                send_sem=send_sems.at[j, k], recv_sem=recv_sems.at[j, k], device_id=to,
                device_id_type=pl.DeviceIdType.MESH) for j in range(n)]

        mine = [pltpu.make_async_copy(x_refs[j], slot(j, *me), local_sems.at[j]) for j in range(n)]
        first = copies(0, me, sibling, own=True)
        for j, chip in enumerate(chips):
            first += copies(1 + j, me, (*chip, c), own=True)
        landed = [copies(1 + j, (*chip, c), me) for j, chip in enumerate(chips)]
        onward = [copies(4 + j, (*chip, c), sibling) for j, chip in enumerate(chips)]
        from_sibling = copies(0, sibling, me)
        for j, chip in enumerate(chips):
            from_sibling += copies(4 + j, (*chip, 1 - c), me)
        return mine, first, landed, onward, from_sibling

    def start(self, refs):
        mine, first, _, _, _ = self._copies(refs)
        for cp in mine + first:
            cp.start()

    def forward(self, refs):
        _, _, landed, onward, _ = self._copies(refs)
        for arrived, passed in zip(landed, onward):
            for cp in arrived:
                cp.wait_recv()
            for cp in passed:
                cp.start()

    def finish(self, refs):
        mine, first, _, onward, from_sibling = self._copies(refs)
        for cp in from_sibling:
            cp.wait_recv()
        for cp in first + [cp for passed in onward for cp in passed]:
            cp.wait_send()
        for cp in mine:
            cp.wait()


class _Exchange:
    def __init__(self, sends):
        self.inputs = list(sends)
        self.out_shape = [jax.ShapeDtypeStruct(a.shape, a.dtype) for a in sends]
        self.scratch = _sems(len(sends))

    def _copies(self, refs):
        x_refs, out_refs, (send_sems, recv_sems, local_sems) = refs
        n = len(x_refs)
        x, y, c = _place()
        me = 4 * x + 2 * y + c
        sent, landing = [], []
        for k in range(1, N_DEV):
            px, py, pc = x ^ ((k >> 2) & 1), y ^ ((k >> 1) & 1), c ^ (k & 1)
            them = 4 * px + 2 * py + pc
            for j in range(n):
                for here, there, into in ((them, me, sent), (me, them, landing)):
                    into.append(pltpu.make_async_remote_copy(
                        src_ref=x_refs[j].at[here], dst_ref=out_refs[j].at[there], send_sem=send_sems.at[j, k - 1],
                        recv_sem=recv_sems.at[j, k - 1], device_id=(px, py, pc), device_id_type=pl.DeviceIdType.MESH))
        mine = [pltpu.make_async_copy(x_refs[j].at[me], out_refs[j].at[me], local_sems.at[j]) for j in range(n)]
        return mine, sent, landing

    def start(self, refs):
        mine, sent, _ = self._copies(refs)
        for cp in mine + sent:
            cp.start()

    def forward(self, refs):
        pass

    def finish(self, refs):
        mine, sent, landing = self._copies(refs)
        for cp in landing:
            cp.wait_recv()
        for cp in sent:
            cp.wait_send()
        for cp in mine:
            cp.wait()


def _run(rider, name):
    n_in, n_out = len(rider.inputs), len(rider.out_shape)

    def body(*refs):
        parts = (refs[:n_in], refs[n_in:n_in + n_out], refs[n_in + n_out:])
        rider.start(parts)
        rider.forward(parts)
        rider.finish(parts)

    return pl.pallas_call(body, name=name, out_shape=rider.out_shape, in_specs=[_ANY] * n_in, out_specs=[_ANY] * n_out,
                          scratch_shapes=rider.scratch)(*rider.inputs)


def _split_refs(refs, n_in, n_out, n_scratch, rider):
    r_in = len(rider.inputs) if rider else 0
    r_out = len(rider.out_shape) if rider else 0
    a = n_in + r_in
    b = a + n_out + r_out
    own = refs[:n_in] + refs[a:a + n_out] + refs[b:b + n_scratch]
    return own, (refs[n_in:a], refs[a + n_out:b], refs[b + n_scratch:])


def _grid_step(grid):
    step, stride = 0, 1
    for axis in reversed(range(len(grid))):
        step = step + pl.program_id(axis) * stride
        stride *= grid[axis]
    return step


def _ride_begin(rider, ride_refs, grid):
    if rider is not None:
        pl.when(_grid_step(grid) == 0)(lambda: rider.start(ride_refs))


def _ride_end(rider, ride_refs, grid, forward_at=1.0):
    if rider is not None:
        steps = 1
        for n in grid:
            steps *= n
        step = _grid_step(grid)
        pl.when(step == min(steps - 1, int(steps * forward_at)))(lambda: rider.forward(ride_refs))
        pl.when(step == steps - 1)(lambda: rider.finish(ride_refs))


def _host(rider, in_specs, out_specs, out_shape, scratch, operands):
    if rider is None:
        return dict(in_specs=in_specs, out_specs=out_specs, out_shape=out_shape, scratch_shapes=scratch), operands
    return dict(in_specs=in_specs + [_ANY] * len(rider.inputs), out_specs=out_specs + [_ANY] * len(rider.out_shape),
                out_shape=out_shape + rider.out_shape, scratch_shapes=scratch + rider.scratch), operands + rider.inputs


_LATE = ("w_gate_up", "w_out", "w_down")


def _local_grads(x, target, w, late=None, exchange=False):
    w = dict(w)
    w_in = list(w["w_in"])
    dn_params = (w["dn_a_log"], w["dn_dt_bias"], w["dn_norm_w"])
    gla_params = (w["gla_w_gate2"], w["gla_gate_bias"], w["gla_norm_w"])
    sgu_params = (w["sgu_ln_w"], w["sgu_ln_b"], w["sgu_w_spatial"], w["sgu_b_spatial"])
    saved = []
    for l in range(DEPTH):
        riding = l == 0 and late is not None
        h, p, *got = _in_proj(x, w["norm1_w"], w_in[l], l, rider=_Gather([late["w_down"]], [True]) if riding else None)
        if riding:
            w["w_down"] = got[0].reshape(DEPTH, FF, D)
        ya = _sgu_fwd(p, *sgu_params, l)
        yb = _sconv_fwd(p, w["sc_conv_w"], l)
        qkv_c = _qkv_conv_fwd(p, w["dn_conv_w"], l)
        yc, st_c, inv_c, *got = _dn_fwd(
            qkv_c, p, dn_params, l, forward_at=0.85,
            rider=_Gather([late["w_gate_up"], late["w_out"]], [False, True]) if riding else None)
        if riding:
            w.update(w_gate_up=got[0], w_out=got[1].reshape(DEPTH, D, D))
        yd, st_d, *got = _gla_fwd(p, gla_params, l, rider=_Gather([late["w_in"]], [False]) if riding else None)
        if riding:
            w_in[1] = _relayout_w_in(got[0])
        mix, x1, h2, gu, act, x2 = _out_mlp(x, (ya, yb, yc, yd), w["w_out"], w["norm2_w"], w["w_gate_up"], w["w_down"], l)
        saved.append(dict(x=x, h=h, p=p, qkv_c=qkv_c, st_c=st_c, inv_c=inv_c, st_d=st_d, mix=mix, x1=x1, h2=h2, gu=gu,
                          act=act))
        x = x2
    loss, d_final, dx = _loss_head(x, w["final_norm_w"], target)
    large = {k: [None] * DEPTH for k in ("w_in", "w_out", "w_gate_up", "w_down")}
    small = [None] * DEPTH
    for l in reversed(range(DEPTH)):
        s = saved[l]
        p = s["p"]
        g = {}
        riding = exchange and l == 0

        def exchanging(pairs):
            return _Exchange([large[k][j] for k, j in pairs]) if riding else None

        def arrive(pairs, arrived):
            for (k, j), a in zip(pairs, arrived):
                large[k][j] = a

        pairs = (("w_gate_up", 1), ("w_down", 1))
        dgu, dx1, dmix, g["norm2_w"], *arrived = _mlp_out_bwd(
            dx, s["x1"], s["gu"], w["norm2_w"], w["w_down"], w["w_gate_up"], w["w_out"], l, rider=exchanging(pairs))
        arrive(pairs, arrived)
        pairs = (("w_in", 1), ("w_out", 1))
        d_gu, *arrived = _wgrad_rows(dgu, s["h2"], "wgrad_gate_up", rider=exchanging(pairs)) if riding else (
            _wgrad_rows(dgu, s["h2"], "wgrad_gate_up"),)
        arrive(pairs, arrived)
        large["w_gate_up"][l] = d_gu.reshape(N_DEV, GU_SHARD, D)
        large["w_down"][l] = _wgrad_rows(s["act"], dx, "wgrad_down").reshape(N_DEV, FF // N_DEV, D)
        large["w_out"][l] = _matmul_tn(s["mix"], dx1, "wgrad_out", BF16).reshape(N_DEV, D // N_DEV, D)
        d_a, g["sgu_ln_w"], g["sgu_ln_b"], g["sgu_w_spatial"], g["sgu_b_spatial"] = _sgu_bwd(p, dmix, *sgu_params, l)
        d_b, g["sc_conv_w"] = _sconv_bwd(p, dmix, w["sc_conv_w"], l)
        pairs = tuple((k, 0) for k in _LATE)
        dqkv_c, dz_c, dab, g["dn_a_log"], g["dn_dt_bias"], g["dn_norm_w"], *arrived = _dn_bwd(
            s["qkv_c"], p, s["st_c"], s["inv_c"], dmix, dn_params, l, rider=exchanging(pairs))
        arrive(pairs, arrived)
        d_c, g["dn_conv_w"] = _qkv_conv_bwd(p, dqkv_c, w["dn_conv_w"], l)
        d_d, dz_d, dglr, g["gla_w_gate2"], g["gla_gate_bias"], g["gla_norm_w"] = _gla_bwd(p, s["st_d"], dmix, gla_params, l)
        dp, dx, g["norm1_w"] = _in_proj_bwd((d_c, d_d, d_a, d_b, dz_c, dz_d, dab, dglr), s["x"], dx1, w["norm1_w"],
                                            w_in[l], l)
        small[l] = g
        if exchange and l == 0:
            d_w_in, small = _matmul_tn(s["h"], dp, "wgrad_in", F32, rider=_Gather([_pack_small(small, d_final)], [False]))
        else:
            d_w_in = _matmul_tn(s["h"], dp, "wgrad_in", F32)
        large["w_in"][l] = _scatter_w_in_grad(d_w_in)
    return loss[0, 0], dx, large, small, d_final


_ORDER = ("norm1_w", "w_in", "sgu_ln_w", "sgu_ln_b", "sgu_w_spatial", "sgu_b_spatial", "sc_conv_w", "dn_conv_w",
          "dn_a_log", "dn_dt_bias", "dn_norm_w", "gla_w_gate2", "gla_gate_bias", "gla_norm_w", "w_out", "norm2_w",
          "w_gate_up", "w_down", "final_norm_w")
_LARGE = ("w_in", "w_gate_up", "w_out", "w_down")
_LARGE_TILE = {"w_in": 256, "w_gate_up": 352, "w_out": 128, "w_down": 352}


def _gather_cols(g):
    return jnp.transpose(g, (1, 2, 0, 3)).reshape(g.shape[1], g.shape[2], N_DEV * g.shape[3])


def kernel(x, norm1_w, w_in, sgu_ln_w, sgu_ln_b, sgu_w_spatial, sgu_b_spatial, sc_conv_w, dn_conv_w, dn_a_log, dn_dt_bias, dn_norm_w, gla_w_gate2, gla_gate_bias, gla_norm_w, w_out, norm2_w, w_gate_up, w_down, final_norm_w, loss_target, m_norm1_w, m_w_in, m_sgu_ln_w, m_sgu_ln_b, m_sgu_w_spatial, m_sgu_b_spatial, m_sc_conv_w, m_dn_conv_w, m_dn_a_log, m_dn_dt_bias, m_dn_norm_w, m_gla_w_gate2, m_gla_gate_bias, m_gla_norm_w, m_w_out, m_norm2_w, m_w_gate_up, m_w_down, m_final_norm_w, v_norm1_w, v_w_in, v_sgu_ln_w, v_sgu_ln_b, v_sgu_w_spatial, v_sgu_b_spatial, v_sc_conv_w, v_dn_conv_w, v_dn_a_log, v_dn_dt_bias, v_dn_norm_w, v_gla_w_gate2, v_gla_gate_bias, v_gla_norm_w, v_w_out, v_norm2_w, v_w_gate_up, v_w_down, v_final_norm_w):
    args = dict(locals())
    wts = {k: args[k] for k in _ORDER}
    mom = {k: args["m_" + k] for k in _ORDER}
    var = {k: args["v_" + k] for k in _ORDER}
    for d in (wts, mom, var):
        d["final_norm_w"] = d["final_norm_w"][None]
    me = 4 * lax.axis_index("x") + 2 * lax.axis_index("y") + lax.axis_index("c")
    for d in (wts, mom, var):
        d["w_gate_up"] = jnp.swapaxes(d["w_gate_up"], 1, 2)

    late = {k: wts[k].astype(BF16) for k in _LATE}
    w_in = wts["w_in"].astype(BF16)
    late["w_in"] = w_in[1]
    got = _run(_Gather([w_in[0]] + [wts[k] for k in _SHARDED_SMALL], [False] * 4), "gather_w_in")
    full = dict(wts)
    full["w_in"] = [_relayout_w_in(got[0]), None]
    for k, g in zip(_SHARDED_SMALL, got[1:]):
        full[k] = _gather_cols(g)

    loss, dx, large, small, d_final = _local_grads(x[0], loss_target[0], full, late=late, exchange=True)
    loss = lax.psum(loss, ("x", "y", "c"))

    result = {}
    for k in ("w_gate_up", "w_out", "w_down", "w_in"):
        rider = _Exchange([large["w_in"][0]]) if k == "w_gate_up" else None
        *outs, = _adamw_large(large[k], wts[k], mom[k], var[k], "adamw_" + k, _LARGE_TILE[k], rider=rider)
        if rider is not None:
            large["w_in"][0] = outs.pop()
        for kind, a in zip(("grad", "delta", "new_m", "new_v"), outs):
            result[kind, k] = jnp.swapaxes(a, 1, 2) if k == "w_gate_up" else a

    slabs = small
    rep = _REPLICATED + ("final_norm_w",)
    outs, summed = _adamw_small(slabs, {k: wts[k] for k in rep}, {k: mom[k] for k in rep}, {k: var[k] for k in rep})
    for kind, d in zip(("grad", "delta", "new_m", "new_v"), outs):
        for k, a in d.items():
            result[kind, k] = a[0] if k == "final_norm_w" else a
    own = {k: lax.dynamic_slice_in_dim(summed[k], me * wts[k].shape[-1], wts[k].shape[-1], axis=2) for k in _SHARDED_SMALL}
    outs = _adamw_shards(own, {k: wts[k] for k in _SHARDED_SMALL}, {k: mom[k] for k in _SHARDED_SMALL},
                         {k: var[k] for k in _SHARDED_SMALL})
    for kind, d in zip(("grad", "delta", "new_m", "new_v"), [own] + outs):
        for k, a in d.items():
            result[kind, k] = a

    return (loss, dx[None], *[result[kind, k] for kind in ("grad", "delta", "new_m", "new_v") for k in _ORDER])
```

```python
import functools

import jax
import jax.numpy as jnp
from jax import lax
from jax.experimental import pallas as pl
from jax.experimental.pallas import tpu as pltpu

F32, BF16 = jnp.float32, jnp.bfloat16
DEPTH = 2
D = 1024
G = 256
NH, HD = 4, 64
FF = 2816
IN_COLS = 3352
P = 3584
EPS = 1e-6
SGU_C, DN_C, GLA_C = 128, 64, 64
N_DEV = 8
IN_SHARD = IN_COLS // N_DEV
GU_SHARD = 2 * FF // N_DEV
LANES = 128
VMEM_LIMIT = 56 * 2 ** 20

_PAD_SEGS = ((1280, 2048, 0),
             (2312, 3080, 0),
             (0, 512, 0),
             (512, 1280, 0),
             (2056, 2312, 0),
             (3096, 3352, 0),
             (2048, 2056, 120),
             (3080, 3096, 112))

ADAM_LR, ADAM_B1, ADAM_B2, ADAM_EPS, ADAM_WD, ADAM_STEP = 0.001, 0.9, 0.999, 1e-08, 0.01, 10


def _cparams(*sem):
    return pltpu.CompilerParams(dimension_semantics=sem, vmem_limit_bytes=VMEM_LIMIT)


def _resident(shape):
    return pl.BlockSpec(shape, lambda *_: (0,) * len(shape), pipeline_mode=pl.Buffered(1))


def _layer(shape, l):
    return pl.BlockSpec((None,) + tuple(shape), lambda *_: (l,) + (0,) * len(shape), pipeline_mode=pl.Buffered(1))


def _acc(shape):
    return pl.BlockSpec(shape, lambda *_: (0,) * len(shape))


def _dg(a, b, ca, cb):
    lead = a.ndim - 2
    batch = ((0,), (0,)) if lead else ((), ())
    return lax.dot_general(a, b, (((ca + lead,), (cb + lead,)), batch), preferred_element_type=F32)


def _split(t):
    hi = t.astype(BF16)
    return hi, (t - hi.astype(F32)).astype(BF16)


def _dg3(a, b, ca, cb):
    (ah, al), (bh, bl) = a, b
    return _dg(ah, bh, ca, cb) + (_dg(ah, bl, ca, cb) + _dg(al, bh, ca, cb))


def _make_dot(accurate):
    def raw(a, b, form):
        ca = 0 if form == "tn" else 1
        cb = 1 if form == "nt" else 0
        if accurate:
            return _dg3(_split(a), _split(b), ca, cb)
        return _dg(a.astype(BF16), b.astype(BF16), ca, cb)

    @functools.partial(jax.custom_vjp, nondiff_argnums=(2,))
    def dot(a, b, form):
        return raw(a, b, form)

    def fwd(a, b, form):
        return raw(a, b, form), (a, b)

    def bwd(form, res, g):
        a, b = res
        if form == "nn":
            return raw(g, b, "nt"), raw(a, g, "tn")
        if form == "nt":
            return raw(g, b, "nn"), raw(g, a, "tn")
        return raw(b, g, "nt"), raw(a, g, "nn")

    dot.defvjp(fwd, bwd)
    return dot


_bdot = _make_dot(False)
_hdot = _make_dot(True)


def _inv_unit_lower(low):
    n = low.shape[-1]
    eye = (lax.broadcasted_iota(jnp.int32, (n, n), 0) == lax.broadcasted_iota(jnp.int32, (n, n), 1)).astype(F32)
    p = -low
    inv = eye + p
    ps = _split(p)
    k = 1
    while 2 * k < n:
        ps = _split(_dg3(ps, ps, 1, 0))
        inv = inv + _dg3(_split(inv), ps, 1, 0)
        k *= 2
    return inv


@jax.custom_vjp
def _unit_lower_solve(low, rhs, inv):
    return _dg3(_split(inv), _split(rhs), 1, 0)


def _uls_fwd(low, rhs, inv):
    sol = _dg3(_split(inv), _split(rhs), 1, 0)
    return sol, (inv, sol)


def _uls_bwd(res, g):
    inv, sol = res
    d_rhs = _dg3(_split(inv), _split(g), 0, 0)
    return -_dg3(_split(d_rhs), _split(sol), 1, 1), d_rhs, jnp.zeros_like(inv)


_unit_lower_solve.defvjp(_uls_fwd, _uls_bwd)


def _sigmoid(x):
    return 1.0 / (1.0 + jnp.exp(-x))


def _softplus(x):
    return jnp.maximum(x, 0.0) + jnp.log(1.0 + jnp.exp(-jnp.maximum(x, -x)))


def _gelu(x):
    return 0.5 * x * (1.0 + jnp.tanh(0.7978845608028654 * (x + 0.044715 * (x * x * x))))


def _tri(n):
    ri = lax.broadcasted_iota(jnp.int32, (n, n), 0)
    ci = lax.broadcasted_iota(jnp.int32, (n, n), 1)
    return ri, ci


def _stack(ts):
    return jnp.concatenate([t[None] for t in ts], axis=0)


@jax.custom_vjp
def _head_sums(x):
    ri, ci = _tri(x.shape[-1])
    shift = HD.bit_length() - 1
    ones = (jnp.right_shift(ri, shift) == jnp.right_shift(ci, shift)).astype(BF16)
    hi, lo = _split(x)
    return _dg(hi, ones, 1, 0) + _dg(lo, ones, 1, 0)


_head_sums.defvjp(lambda x: (_head_sums(x), None), lambda _, g: (_head_sums(g),))


def _chunk_mask_dot(x, c, running, transpose):
    ri, ci = _tri(x.shape[0])
    shift = c.bit_length() - 1
    mask = jnp.right_shift(ri, shift) == jnp.right_shift(ci, shift)
    if running:
        mask = mask & (ri >= ci)
    hi, lo = _split(x)
    m = mask.astype(BF16)
    ca = 0 if transpose else 1
    return _dg(m, hi, ca, 0) + _dg(m, lo, ca, 0)


@functools.partial(jax.custom_vjp, nondiff_argnums=(1, 2))
def _chunk_sums(x, c, running):
    return _chunk_mask_dot(x, c, running, False)


_chunk_sums.defvjp(lambda x, c, running: (_chunk_mask_dot(x, c, running, False), None),
                   lambda c, running, _, g: (_chunk_mask_dot(g, c, running, True),))


def _spread_onto(x, first, transpose):
    ri = lax.broadcasted_iota(jnp.int32, (LANES, G), 0)
    ci = lax.broadcasted_iota(jnp.int32, (LANES, G), 1)
    sel = (ri == first + jnp.right_shift(ci, HD.bit_length() - 1)).astype(BF16)
    hi, lo = _split(x)
    cb = 1 if transpose else 0
    return _dg(hi, sel, 1, cb) + _dg(lo, sel, 1, cb)


@functools.partial(jax.custom_vjp, nondiff_argnums=(1,))
def _spread(x, first):
    return _spread_onto(x, first, False)


_spread.defvjp(lambda x, first: (_spread_onto(x, first, False), None),
               lambda first, _, g: (_spread_onto(g, first, True),))


def _head_columns_dot(x, transpose):
    sel = (lax.broadcasted_iota(jnp.int32, (NH, G), 0)
           == jnp.right_shift(lax.broadcasted_iota(jnp.int32, (NH, G), 1), HD.bit_length() - 1)).astype(BF16)
    hi, lo = _split(x)
    if transpose:
        return _dg(sel, hi, 1, 1) + _dg(sel, lo, 1, 1)
    return _dg(hi, sel, 0, 0) + _dg(lo, sel, 0, 0)


@jax.custom_vjp
def _head_columns(x):
    return _head_columns_dot(x, False)


_head_columns.defvjp(lambda x: (_head_columns_dot(x, False), None), lambda _, g: (_head_columns_dot(g, True),))


def _sgu_chunk(uv, ln_w, ln_b, ws, bs):
    u = _gelu(uv[:, :G])
    v = _gelu(uv[:, G:])
    mu = jnp.mean(v, axis=-1, keepdims=True)
    vc = v - mu
    var = jnp.mean(vc * vc, axis=-1, keepdims=True)
    vn = vc * lax.rsqrt(var + EPS) * ln_w + ln_b
    ri, ci = _tri(SGU_C)
    mixed = [_bdot(jnp.where(ri >= ci, ws[h], 0.0), vn[:, HD * h:HD * (h + 1)], "nn") for h in range(NH)]
    return u * (jnp.concatenate(mixed, axis=1) + _head_columns(bs))


def _dn_tile(q, k, v, ab, z, state, inv, a_log, dt_bias, nw):
    c = DN_C
    tm = q.shape[0]
    cps = tm // c
    ri, ci = _tri(tm)
    shift = c.bit_length() - 1
    same = jnp.right_shift(ri, shift) == jnp.right_shift(ci, shift)
    g4 = -jnp.exp(a_log) * _softplus(ab[:, 0:NH] + dt_bias)
    gates = jnp.concatenate([g4, _sigmoid(ab[:, NH:2 * NH]), jnp.zeros((tm, LANES - 2 * NH), F32)], axis=1)
    g, beta = _spread(gates, 0), _spread(gates, NH)
    gc = _chunk_sums(g, c, True)
    gl = _chunk_sums(g, c, False)
    gr4 = _hdot(g4, (same & (ri <= ci)).astype(F32), "tn")
    pairs = [(slice(c * j, c * (j + 1)), h) for j in range(cps) for h in range(NH)]
    heads = lambda t: _stack([t[rows, HD * h:HD * (h + 1)] for rows, h in pairs])
    qn = q * lax.rsqrt(_head_sums(q * q) + EPS) * (HD ** -0.5)
    kn = k * lax.rsqrt(_head_sums(k * k) + EPS)
    eg = jnp.exp(gc)
    kb = kn * beta
    rhs = jnp.concatenate([heads(v * beta), heads(kb * eg)], axis=2)
    qg, kd = heads(qn * eg), heads(kn * jnp.exp(gl - gc))
    cd = _stack([jnp.exp(gl[c * j:c * j + 1, HD * h:HD * (h + 1)]) for j in range(cps) for h in range(NH)])
    qn, kn, kb, gc = heads(qn), heads(kn), heads(kb), heads(gc)
    gr = _stack([gr4[h:h + 1, rows] for rows, h in pairs])
    r2, c2 = _tri(c)
    decay = jnp.exp(jnp.where(r2 >= c2, gc - gr, -jnp.inf))
    low = jnp.where(r2 > c2, _bdot(kb, kn, "nt") * decay, 0.0)
    if inv is None:
        inv = _inv_unit_lower(low)
    sol = _unit_lower_solve(low, rhs, inv)
    u, w = sol[:, :, :HD], sol[:, :, HD:]
    attn = _bdot(qn, kn, "nt") * decay
    ys = []
    for j in range(cps):
        b = slice(NH * j, NH * (j + 1))
        v_new = u[b] - _bdot(w[b], state, "nn")
        o = _bdot(qg[b], state, "nn") + _bdot(attn[b], v_new, "nn")
        state = state * cd[b] + _bdot(kd[b], v_new, "tn")
        on = o * lax.rsqrt(jnp.mean(o * o, axis=-1, keepdims=True) + EPS) * nw
        ys.append(jnp.concatenate([on[h] for h in range(NH)], axis=1))
    return jnp.concatenate(ys, axis=0) * (z * _sigmoid(z)), state, inv


def _gla_tile(q, k, v, glr, z, state_t, w2, gate_bias, nw):
    c = GLA_C
    tm = q.shape[0]
    cps = tm // c
    w2_rows = jnp.concatenate([w2, jnp.zeros((LANES - w2.shape[0], G), F32)], axis=0)
    pre = _bdot(glr, w2_rows, "nn") + gate_bias
    log_a = (jnp.minimum(pre, 0.0) - jnp.log(1.0 + jnp.exp(-jnp.maximum(pre, -pre)))) * (1.0 / 16.0)
    gcum = _chunk_sums(log_a, c, True)
    row = lax.broadcasted_iota(jnp.int32, (c, G), 0)
    qs = q * (HD ** -0.5)
    qa, ka, qg, kl, dec = [], [], [], [], []
    for j in range(cps):
        rows = slice(c * j, c * (j + 1))
        gj = gcum[rows]
        g_mid = jnp.sum(jnp.where(row == c // 2, gj, 0.0), axis=0, keepdims=True)
        g_last = jnp.sum(log_a[rows], axis=0, keepdims=True)
        qa.append(qs[rows] * jnp.exp(gj - g_mid))
        ka.append(k[rows] * jnp.exp(g_mid - gj))
        qg.append(qs[rows] * jnp.exp(gj))
        kl.append(k[rows] * jnp.exp(g_last - gj))
        dec.append(jnp.exp(g_last))
    heads = lambda ts: _stack([t[:, HD * h:HD * (h + 1)] for t in ts for h in range(NH)])
    qa, ka, qg, kl, dec = heads(qa), heads(ka), heads(qg), heads(kl), heads(dec)
    vh = heads([v[c * j:c * (j + 1)] for j in range(cps)])
    r2, c2 = _tri(c)
    o_intra = _bdot(jnp.where(r2 >= c2, _bdot(qa, ka, "nt"), 0.0), vh, "nn")
    ys = []
    for j in range(cps):
        b = slice(NH * j, NH * (j + 1))
        o = o_intra[b] + _bdot(qg[b], state_t, "nt")
        state_t = state_t * dec[b] + _bdot(vh[b], kl[b], "tn")
        on = o * lax.rsqrt(jnp.mean(o * o, axis=-1, keepdims=True) + EPS) * nw
        ys.append(jnp.concatenate([on[h] for h in range(NH)], axis=1))
    return jnp.concatenate(ys, axis=0) * (z * _sigmoid(z)), state_t


def _rms(x, nw):
    r = lax.rsqrt(jnp.mean(x * x, axis=-1, keepdims=True) + EPS)
    xh = x * r
    return xh * nw, xh, r


def _rms_bwd(g, xh, r, nw):
    gw = g * nw
    return r * (gw - xh * jnp.mean(gw * xh, axis=-1, keepdims=True)), jnp.sum(g * xh, axis=0, keepdims=True)


def _row(tm, w, blk=0):
    return pl.BlockSpec((tm, w), lambda i: (i, blk))


def _in_proj(x, nw, wp, l, tm=512, rider=None):
    t = x.shape[0]

    def body(*refs):
        (x_ref, nw_ref, w_ref, h_ref, p_ref), ride = _split_refs(refs, 3, 2, 0, rider)
        _ride_begin(rider, ride, (t // tm,))
        h = _rms(x_ref[...], nw_ref[l:l + 1, :])[0].astype(BF16)
        h_ref[...] = h
        p_ref[...] = jnp.dot(h, w_ref[...], preferred_element_type=F32)
        _ride_end(rider, ride, (t // tm,))

    specs, operands = _host(
        rider, [_row(tm, D), _resident((DEPTH, D)), _resident((D, P))], [_row(tm, D), _row(tm, P)],
        [jax.ShapeDtypeStruct((t, D), BF16), jax.ShapeDtypeStruct((t, P), F32)], [], [x, nw, wp])
    return pl.pallas_call(body, name="in_proj", grid=(t // tm,), compiler_params=_cparams("arbitrary"),
                          **specs)(*operands)


def _gu_spec(l):
    return pl.BlockSpec((N_DEV, None, GU_SHARD, D), lambda *_: (0, l, 0, 0), pipeline_mode=pl.Buffered(1))


def _out_mlp(x, ys, w_out, nw2, w_gu_t, w_down, l, tm=256):
    t = x.shape[0]

    def body(x_ref, ya, yb, yc, yd, wo_ref, nw_ref, wgu_ref, wd_ref, mix_ref, x1_ref, h2_ref, gu_ref, act_ref, x2_ref):
        mix = jnp.concatenate([ya[...], yb[...], yc[...], yd[...]], axis=1)
        mix_ref[...] = mix
        x1 = x_ref[...] + jnp.dot(mix, wo_ref[...], preferred_element_type=F32)
        x1_ref[...] = x1
        h2 = _rms(x1, nw_ref[l:l + 1, :])[0].astype(BF16)
        h2_ref[...] = h2
        gu = _dg(h2, wgu_ref[...].reshape(2 * FF, D), 1, 1)
        gu_ref[...] = gu.astype(BF16)
        gate, up = gu[:, :FF], gu[:, FF:]
        act = (gate * _sigmoid(gate) * up).astype(BF16)
        act_ref[...] = act
        x2_ref[...] = x1 + jnp.dot(act, wd_ref[...], preferred_element_type=F32)

    return pl.pallas_call(
        body, name="out_mlp", grid=(t // tm,),
        in_specs=[_row(tm, D), _row(tm, G), _row(tm, G), _row(tm, G), _row(tm, G), _layer((D, D), l),
                  _resident((DEPTH, D)), _gu_spec(l), _layer((FF, D), l)],
        out_specs=[_row(tm, D), _row(tm, D), _row(tm, D), _row(tm, 2 * FF), _row(tm, FF), _row(tm, D)],
        out_shape=[jax.ShapeDtypeStruct((t, D), BF16), jax.ShapeDtypeStruct((t, D), F32),
                   jax.ShapeDtypeStruct((t, D), BF16), jax.ShapeDtypeStruct((t, 2 * FF), BF16),
                   jax.ShapeDtypeStruct((t, FF), BF16), jax.ShapeDtypeStruct((t, D), F32)],
        compiler_params=_cparams("parallel"))(x, *ys, w_out, nw2, w_gu_t, w_down)


def _loss_head(x, nw, target, tm=512):
    t = x.shape[0]

    def body(x_ref, nw_ref, tg_ref, loss_ref, dnw_ref, dx_ref):
        @pl.when(pl.program_id(0) == 0)
        def _():
            loss_ref[...] = jnp.zeros_like(loss_ref)
            dnw_ref[...] = jnp.zeros_like(dnw_ref)
        nw_v = nw_ref[...]
        y, xh, r = _rms(x_ref[...], nw_v)
        err = y - tg_ref[...]
        loss_ref[...] += 0.5 * jnp.sum(err * err) * (1.0 / D)
        dx, dnw = _rms_bwd(err * (1.0 / D), xh, r, nw_v)
        dx_ref[...] = dx
        dnw_ref[...] += dnw

    return pl.pallas_call(
        body, name="loss_head", grid=(t // tm,),
        in_specs=[_row(tm, D), _resident((1, D)), _row(tm, D)],
        out_specs=[_acc((8, LANES)), _acc((1, D)), _row(tm, D)],
        out_shape=[jax.ShapeDtypeStruct((8, LANES), F32), jax.ShapeDtypeStruct((1, D), F32),
                   jax.ShapeDtypeStruct((t, D), F32)],
        compiler_params=_cparams("arbitrary"))(x, nw, target)


def _mlp_out_bwd(dx2, x1, gu, nw2, w_down, w_gu, w_out, l, tm=256, rider=None):
    t = dx2.shape[0]

    def body(*refs):
        (dx2_ref, x1_ref, gu_ref, nw_ref, wd_ref, wgu_ref, wo_ref, dgu_ref, dx1_ref, dmix_ref, dnw_ref), ride = \
            _split_refs(refs, 7, 4, 0, rider)
        _ride_begin(rider, ride, (t // tm,))

        @pl.when(pl.program_id(0) == 0)
        def _():
            dnw_ref[...] = jnp.zeros_like(dnw_ref)
        dx2 = dx2_ref[...]
        dact = _dg(dx2.astype(BF16), wd_ref[...], 1, 1)
        gu = gu_ref[...].astype(F32)
        gate, up = gu[:, :FF], gu[:, FF:]
        s = _sigmoid(gate)
        dgu = jnp.concatenate([dact * up * (s * (1.0 + gate * (1.0 - s))), dact * (gate * s)], axis=1).astype(BF16)
        dgu_ref[...] = dgu
        dh2 = jnp.dot(dgu, wgu_ref[...].reshape(2 * FF, D), preferred_element_type=F32)
        nw_v = nw_ref[l:l + 1, :]
        _, xh, r = _rms(x1_ref[...], nw_v)
        dxn, dnw = _rms_bwd(dh2, xh, r, nw_v)
        dx1 = dx2 + dxn
        dx1_ref[...] = dx1
        dnw_ref[...] += dnw
        dmix_ref[...] = _dg(dx1.astype(BF16), wo_ref[...], 1, 1)
        _ride_end(rider, ride, (t // tm,))

    specs, operands = _host(
        rider, [_row(tm, D), _row(tm, D), _row(tm, 2 * FF), _resident((DEPTH, D)), _layer((FF, D), l), _gu_spec(l),
                _layer((D, D), l)],
        [_row(tm, 2 * FF), _row(tm, D), _row(tm, D), _acc((1, D))],
        [jax.ShapeDtypeStruct((t, 2 * FF), BF16), jax.ShapeDtypeStruct((t, D), F32),
         jax.ShapeDtypeStruct((t, D), F32), jax.ShapeDtypeStruct((1, D), F32)],
        [], [dx2, x1, gu, nw2, w_down, w_gu, w_out])
    return pl.pallas_call(body, name="mlp_out_bwd", grid=(t // tm,), compiler_params=_cparams("arbitrary"),
                          **specs)(*operands)


_DP_WIDTHS = (768, 768, 512, 768, 256, 256, 128, 128)


def _in_proj_bwd(dps, x, dx1, nw, wp, l, tm=512, rider=None):
    t = x.shape[0]
    n_dp = len(_DP_WIDTHS)

    def body(*refs):
        own, ride = _split_refs(refs, n_dp + 4, 2, 0, rider)
        dp_refs, (x_ref, dx1_ref, nw_ref, w_ref, dx_ref, dnw_ref) = own[:n_dp], own[n_dp:]
        _ride_begin(rider, ride, (t // tm,))

        @pl.when(pl.program_id(0) == 0)
        def _():
            dnw_ref[...] = jnp.zeros_like(dnw_ref)
        dh = _dg(jnp.concatenate([r[...] for r in dp_refs], axis=1), w_ref[...], 1, 1)
        nw_v = nw_ref[l:l + 1, :]
        _, xh, r = _rms(x_ref[...], nw_v)
        dxn, dnw = _rms_bwd(dh, xh, r, nw_v)
        dx_ref[...] = dx1_ref[...] + dxn
        dnw_ref[...] += dnw
        _ride_end(rider, ride, (t // tm,))

    specs, operands = _host(
        rider, [_row(tm, w) for w in _DP_WIDTHS] + [_row(tm, D), _row(tm, D), _resident((DEPTH, D)), _resident((D, P))],
        [_row(tm, D), _acc((1, D))], [jax.ShapeDtypeStruct((t, D), F32), jax.ShapeDtypeStruct((1, D), F32)], [],
        [*dps, x, dx1, nw, wp])
    return pl.pallas_call(body, name="in_proj_bwd", grid=(t // tm,), compiler_params=_cparams("arbitrary"),
                          **specs)(*operands)


def _accumulate(acc, o_ref, t_axis, term):
    @pl.when(pl.program_id(t_axis) == 0)
    def _():
        acc[...] = jnp.zeros_like(acc)
    acc[...] += term

    @pl.when(pl.program_id(t_axis) == pl.num_programs(t_axis) - 1)
    def _():
        o_ref[...] = acc[...].astype(o_ref.dtype)


WGRAD_TOKENS = 2048


def _matmul_tn(a, b, name, out_dtype, tn=512, tt=WGRAD_TOKENS, rider=None):
    t, m = a.shape
    n = b.shape[1]
    tt = min(tt, t)
    grid = (n // tn, t // tt)

    def body(*refs):
        (a_ref, b_ref, o_ref, acc), ride = _split_refs(refs, 2, 1, 1, rider)
        _ride_begin(rider, ride, grid)
        _accumulate(acc, o_ref, 1, _dg(a_ref[...].astype(BF16), b_ref[...].astype(BF16), 0, 0))
        _ride_end(rider, ride, grid)

    specs, operands = _host(
        rider, [pl.BlockSpec((tt, m), lambda j, i: (i, 0)), pl.BlockSpec((tt, tn), lambda j, i: (i, j))],
        [pl.BlockSpec((m, tn), lambda j, i: (0, j))], [jax.ShapeDtypeStruct((m, n), out_dtype)],
        [pltpu.VMEM((m, tn), F32)], [a, b])
    out = pl.pallas_call(body, name=name, grid=grid, compiler_params=_cparams("arbitrary", "arbitrary"), **specs)(*operands)
    return out[0] if rider is None else out


WGRAD_ROWS = 2 * GU_SHARD


def _wgrad_rows(a, b, name, tt=WGRAD_TOKENS, rider=None):
    t, m = a.shape
    tt = min(tt, t)
    grid = (m // WGRAD_ROWS, t // tt)

    def body(*refs):
        (a_ref, b_ref, o_ref, acc), ride = _split_refs(refs, 2, 1, 1, rider)
        _ride_begin(rider, ride, grid)
        _accumulate(acc, o_ref, 1, _dg(a_ref[...], b_ref[...].astype(BF16), 0, 0))
        _ride_end(rider, ride, grid)

    specs, operands = _host(
        rider, [pl.BlockSpec((tt, WGRAD_ROWS), lambda j, i: (i, j)), pl.BlockSpec((tt, D), lambda j, i: (i, 0))],
        [pl.BlockSpec((WGRAD_ROWS, D), lambda j, i: (j, 0))], [jax.ShapeDtypeStruct((m, D), BF16)],
        [pltpu.VMEM((WGRAD_ROWS, D), F32)], [a, b])
    out = pl.pallas_call(body, name=name, grid=grid, compiler_params=_cparams("arbitrary", "arbitrary"), **specs)(*operands)
    return out[0] if rider is None else out


def _relayout_w_in(g, tr=256):
    def body(w_ref, o_ref):
        full = jnp.concatenate([w_ref[d] for d in range(N_DEV)], axis=1)
        parts = []
        for a, b, z in _PAD_SEGS:
            parts.append(full[:, a:b])
            if z:
                parts.append(jnp.zeros((tr, z), full.dtype))
        o_ref[...] = jnp.concatenate(parts, axis=1)

    return pl.pallas_call(
        body, name="relayout_w_in", grid=(D // tr,),
        in_specs=[pl.BlockSpec((N_DEV, tr, IN_SHARD), lambda i: (0, i, 0))], out_specs=_row(tr, P),
        out_shape=jax.ShapeDtypeStruct((D, P), g.dtype), compiler_params=_cparams("parallel"))(g)


def _scatter_w_in_grad(gps, tr=256):
    def body(*refs):
        g_refs, o_ref = refs[:-1], refs[-1]
        g = jnp.concatenate([r[...] for r in g_refs], axis=1)
        pieces, off = [], 0
        for a, b, z in _PAD_SEGS:
            pieces.append((a, g[:, off:off + (b - a)]))
            off += (b - a) + z
        nat = jnp.concatenate([piece for _, piece in sorted(pieces, key=lambda ap: ap[0])], axis=1)
        for d in range(N_DEV):
            o_ref[d] = nat[:, IN_SHARD * d:IN_SHARD * (d + 1)].astype(BF16)

    return pl.pallas_call(
        body, name="scatter_w_in_grad", grid=(D // tr,),
        in_specs=[_row(tr, g.shape[1]) for g in gps],
        out_specs=pl.BlockSpec((N_DEV, tr, IN_SHARD), lambda i: (0, i, 0)),
        out_shape=jax.ShapeDtypeStruct((N_DEV, D, IN_SHARD), BF16),
        compiler_params=_cparams("parallel"))(*gps)


_A_BLK = 3


def _sgu_specs(l):
    return [_resident((DEPTH, G)), _resident((DEPTH, G)), _layer((NH, SGU_C, SGU_C), l), _layer((NH, SGU_C), l)]


def _sgu_fwd(p, ln_w, ln_b, ws, bs, l, tm=256):
    t = p.shape[0]

    def body(uv_ref, lw_ref, lb_ref, ws_ref, bs_ref, y_ref):
        ws_v = [ws_ref[h] for h in range(NH)]
        for c in range(tm // SGU_C):
            rows = pl.ds(c * SGU_C, SGU_C)
            y_ref[rows, :] = _sgu_chunk(uv_ref[rows, :], lw_ref[l:l + 1, :], lb_ref[l:l + 1, :], ws_v,
                                        bs_ref[...]).astype(BF16)

    return pl.pallas_call(
        body, name="sgu_fwd", grid=(t // tm,),
        in_specs=[_row(tm, 2 * G, _A_BLK)] + _sgu_specs(l), out_specs=_row(tm, G),
        out_shape=jax.ShapeDtypeStruct((t, G), BF16),
        compiler_params=_cparams("parallel"))(p, ln_w, ln_b, ws, bs)


def _sgu_bwd(p, dmix, ln_w, ln_b, ws, bs, l, tm=256):
    t = p.shape[0]

    def body(uv_ref, dy_ref, lw_ref, lb_ref, ws_ref, bs_ref, duv_ref, dlw_ref, dlb_ref, dws_ref, dbs_ref):
        @pl.when(pl.program_id(0) == 0)
        def _():
            for r in (dlw_ref, dlb_ref, dws_ref, dbs_ref):
                r[...] = jnp.zeros_like(r)
        ws_v = [ws_ref[h] for h in range(NH)]
        for c in range(tm // SGU_C):
            rows = pl.ds(c * SGU_C, SGU_C)
            _, vjp = jax.vjp(_sgu_chunk, uv_ref[rows, :], lw_ref[l:l + 1, :], lb_ref[l:l + 1, :], ws_v, bs_ref[...])
            duv, dlw, dlb, dws, dbs = vjp(dy_ref[rows, :])
            duv_ref[rows, :] = duv.astype(BF16)
            dlw_ref[...] += dlw
            dlb_ref[...] += dlb
            dbs_ref[...] += dbs
            for h in range(NH):
                dws_ref[h] += dws[h]

    return pl.pallas_call(
        body, name="sgu_bwd", grid=(t // tm,),
        in_specs=[_row(tm, 2 * G, _A_BLK), _row(tm, G, 0)] + _sgu_specs(l),
        out_specs=[_row(tm, 2 * G), _acc((1, G)), _acc((1, G)), _acc((NH, SGU_C, SGU_C)), _acc((NH, SGU_C))],
        out_shape=[jax.ShapeDtypeStruct((t, 2 * G), BF16), jax.ShapeDtypeStruct((1, G), F32),
                   jax.ShapeDtypeStruct((1, G), F32), jax.ShapeDtypeStruct((NH, SGU_C, SGU_C), F32),
                   jax.ShapeDtypeStruct((NH, SGU_C), F32)],
        compiler_params=_cparams("arbitrary"))(p, dmix, ln_w, ln_b, ws, bs)


HALO = 8


def _prev_halo(tm, width, col):
    return pl.BlockSpec((HALO, width), lambda i: (jnp.maximum(i * (tm // HALO) - 1, 0), col))


def _next_halo(tm, width, col, t):
    return pl.BlockSpec((HALO, width), lambda i: (jnp.minimum((i + 1) * (tm // HALO), t // HALO - 1), col))


def _with_prev(halo_ref, x_ref):
    halo = jnp.where(pl.program_id(0) == 0, 0.0, halo_ref[...])
    return jnp.concatenate([halo, x_ref[...]], axis=0)


def _with_next(x_ref, halo_ref):
    halo = jnp.where(pl.program_id(0) == pl.num_programs(0) - 1, 0.0, halo_ref[...])
    return jnp.concatenate([x_ref[...], halo], axis=0)


def _delay(ext, j):
    return ext if j == 0 else pltpu.roll(ext, j, axis=0)


def _advance(ext, j):
    return ext if j == 0 else pltpu.roll(ext, ext.shape[0] - j, axis=0)


def _causal_conv(ext, w, tm):
    kw = w.shape[0]
    out = None
    for k in range(kw):
        term = _delay(ext, kw - 1 - k)[HALO:HALO + tm] * w[k:k + 1, :]
        out = term if out is None else out + term
    return out


def _causal_conv_t(ext, w, tm):
    kw = w.shape[0]
    out = None
    for k in range(kw):
        term = _advance(ext, kw - 1 - k)[0:tm] * w[k:k + 1, :]
        out = term if out is None else out + term
    return out


def _conv_wgrad(ext, dy, kw, tm):
    return jnp.concatenate(
        [jnp.sum(_delay(ext, kw - 1 - k)[HALO:HALO + tm] * dy, axis=0, keepdims=True) for k in range(kw)], axis=0)


_B_GB, _B_GC, _B_H = 8, 9, 10
_C_QKV, _D_QKV = 0, 1
_C_Z, _D_Z = 11, 12
_C_AB, _D_GLR = 26, 27


def _sconv_fwd(p, w, l, tm=512):
    t = p.shape[0]

    def body(gb_ref, gc_ref, h_ref, gc_halo, h_halo, w_ref, y_ref):
        s_ext = _with_prev(gc_halo, gc_ref) * _with_prev(h_halo, h_ref)
        y_ref[...] = (gb_ref[...] * _causal_conv(s_ext, w_ref[...], tm)).astype(BF16)

    return pl.pallas_call(
        body, name="sconv_fwd", grid=(t // tm,),
        in_specs=[_row(tm, G, _B_GB), _row(tm, G, _B_GC), _row(tm, G, _B_H), _prev_halo(tm, G, _B_GC),
                  _prev_halo(tm, G, _B_H), _layer((3, G), l)],
        out_specs=_row(tm, G), out_shape=jax.ShapeDtypeStruct((t, G), BF16),
        compiler_params=_cparams("parallel"))(p, p, p, p, p, w)


def _sconv_bwd(p, dmix, w, l, tm=512):
    t = p.shape[0]

    def body(gb_ref, gc_ref, h_ref, gc_halo, h_halo, gb_next, dy_ref, dy_next, w_ref, dp_ref, dw_ref):
        @pl.when(pl.program_id(0) == 0)
        def _():
            dw_ref[...] = jnp.zeros_like(dw_ref)
        w_v = w_ref[...]
        s_ext = _with_prev(gc_halo, gc_ref) * _with_prev(h_halo, h_ref)
        dout = dy_ref[...]
        d_gb = dout * _causal_conv(s_ext, w_v, tm)
        dconv_ext = _with_next(dy_ref, dy_next) * _with_next(gb_ref, gb_next)
        ds = _causal_conv_t(dconv_ext, w_v, tm)
        dw_ref[...] += _conv_wgrad(s_ext, dconv_ext[0:tm], 3, tm)
        dp_ref[...] = jnp.concatenate([d_gb, ds * h_ref[...], ds * gc_ref[...]], axis=1).astype(BF16)

    return pl.pallas_call(
        body, name="sconv_bwd", grid=(t // tm,),
        in_specs=[_row(tm, G, _B_GB), _row(tm, G, _B_GC), _row(tm, G, _B_H), _prev_halo(tm, G, _B_GC),
                  _prev_halo(tm, G, _B_H), _next_halo(tm, G, _B_GB, t), _row(tm, G, 1), _next_halo(tm, G, 1, t),
                  _layer((3, G), l)],
        out_specs=[_row(tm, 3 * G), _acc((3, G))],
        out_shape=[jax.ShapeDtypeStruct((t, 3 * G), BF16), jax.ShapeDtypeStruct((3, G), F32)],
        compiler_params=_cparams("arbitrary"))(p, p, p, p, p, p, dmix, dmix, w)


def _qkv_conv_fwd(p, w, l, tm=512):
    t = p.shape[0]

    def body(x_ref, x_halo, w_ref, y_ref):
        c = _causal_conv(_with_prev(x_halo, x_ref), w_ref[...], tm)
        y_ref[...] = c * _sigmoid(c)

    return pl.pallas_call(
        body, name="qkv_conv_fwd", grid=(t // tm,),
        in_specs=[_row(tm, 3 * G, _C_QKV), _prev_halo(tm, 3 * G, _C_QKV), _layer((4, 3 * G), l)],
        out_specs=_row(tm, 3 * G), out_shape=jax.ShapeDtypeStruct((t, 3 * G), F32),
        compiler_params=_cparams("parallel"))(p, p, w)


def _qkv_conv_bwd(p, dy, w, l, tm=512):
    t = p.shape[0]

    def body(x_ref, x_halo, x_next, dy_ref, dy_next, w_ref, dx_ref, dw_ref):
        @pl.when(pl.program_id(0) == 0)
        def _():
            dw_ref[...] = jnp.zeros_like(dw_ref)
        w_v = w_ref[...]
        x_all = jnp.concatenate([_with_prev(x_halo, x_ref), jnp.where(
            pl.program_id(0) == pl.num_programs(0) - 1, 0.0, x_next[...])], axis=0)
        c = _causal_conv(x_all, w_v, tm + HALO)
        s = _sigmoid(c)
        dc_ext = _with_next(dy_ref, dy_next) * (s * (1.0 + c * (1.0 - s)))
        dx_ref[...] = _causal_conv_t(dc_ext, w_v, tm).astype(BF16)
        dw_ref[...] += _conv_wgrad(x_all[0:HALO + tm], dc_ext[0:tm], 4, tm)

    return pl.pallas_call(
        body, name="qkv_conv_bwd", grid=(t // tm,),
        in_specs=[_row(tm, 3 * G, _C_QKV), _prev_halo(tm, 3 * G, _C_QKV), _next_halo(tm, 3 * G, _C_QKV, t),
                  _row(tm, 3 * G), _next_halo(tm, 3 * G, 0, t), _layer((4, 3 * G), l)],
        out_specs=[_row(tm, 3 * G), _acc((4, 3 * G))],
        out_shape=[jax.ShapeDtypeStruct((t, 3 * G), BF16), jax.ShapeDtypeStruct((4, 3 * G), F32)],
        compiler_params=_cparams("arbitrary"))(p, p, p, dy, dy, w)


_STATE = pl.BlockSpec((1, NH, HD, HD), lambda i: (i, 0, 0, 0))


def _dn_specs():
    return [_resident((DEPTH, NH)), _resident((DEPTH, NH)), _resident((DEPTH, HD))]


def _dn_fwd(qkv, p, params, l, cps=4, rider=None, forward_at=1.0):
    t = qkv.shape[0]
    tm = DN_C * cps
    nb = cps * NH

    def body(*refs):
        (qkv_ref, z_ref, ab_ref, alog_ref, dt_ref, nw_ref, y_ref, st_ref, inv_ref, state), ride = _split_refs(
            refs, 6, 3, 1, rider)
        _ride_begin(rider, ride, (t // tm,))

        @pl.when(pl.program_id(0) == 0)
        def _():
            state[...] = jnp.zeros_like(state)
        st_ref[0] = state[...]
        y, new, inv = _dn_tile(qkv_ref[:, 0:G], qkv_ref[:, G:2 * G], qkv_ref[:, 2 * G:3 * G], ab_ref[...], z_ref[...],
                               state[...], None, alog_ref[l:l + 1, :], dt_ref[l:l + 1, :], nw_ref[l:l + 1, :])
        y_ref[...] = y.astype(BF16)
        inv_ref[...] = inv
        state[...] = new
        _ride_end(rider, ride, (t // tm,), forward_at)

    specs, operands = _host(
        rider, [_row(tm, 3 * G), _row(tm, G, _C_Z), _row(tm, LANES, _C_AB)] + _dn_specs(),
        [_row(tm, G), _STATE, pl.BlockSpec((nb, DN_C, DN_C), lambda i: (i, 0, 0))],
        [jax.ShapeDtypeStruct((t, G), BF16), jax.ShapeDtypeStruct((t // tm, NH, HD, HD), F32),
         jax.ShapeDtypeStruct((t // DN_C * NH, DN_C, DN_C), F32)],
        [pltpu.VMEM((NH, HD, HD), F32)], [qkv, p, p, *params])
    return pl.pallas_call(body, name="dn_fwd", grid=(t // tm,), compiler_params=_cparams("arbitrary"), **specs)(*operands)


def _dn_bwd(qkv, p, states, inv, dmix, params, l, cps=4, rider=None):
    t = qkv.shape[0]
    tm = DN_C * cps
    n = t // tm
    nb = cps * NH

    def body(*refs):
        (qkv_ref, z_ref, ab_ref, st_ref, inv_ref, dy_ref, alog_ref, dt_ref, nw_ref,
         dqkv_ref, dz_ref, dab_ref, dalog_ref, ddt_ref, dnw_ref, dstate), ride = _split_refs(refs, 9, 6, 1, rider)
        _ride_begin(rider, ride, (n,))
        dprm_refs = (dalog_ref, ddt_ref, dnw_ref)

        @pl.when(pl.program_id(0) == 0)
        def _():
            dstate[...] = jnp.zeros_like(dstate)
            for r in dprm_refs:
                r[...] = jnp.zeros_like(r)
        inv_v = inv_ref[...]
        tile = lambda q, k, v, ab, z, st, alog, dt, nw: _dn_tile(q, k, v, ab, z, st, inv_v, alog, dt, nw)[:2]
        _, vjp = jax.vjp(tile, qkv_ref[:, 0:G], qkv_ref[:, G:2 * G], qkv_ref[:, 2 * G:3 * G], ab_ref[...], z_ref[...],
                         st_ref[0], alog_ref[l:l + 1, :], dt_ref[l:l + 1, :], nw_ref[l:l + 1, :])
        dq, dk, dv, dab, dz, dst, *dprm = vjp((dy_ref[...], dstate[...]))
        dqkv_ref[...] = jnp.concatenate([dq, dk, dv], axis=1)
        dz_ref[...] = dz.astype(BF16)
        dab_ref[...] = dab.astype(BF16)
        dstate[...] = dst
        for r, g in zip(dprm_refs, dprm):
            r[...] += g
        _ride_end(rider, ride, (n,))

    rev = lambda w, blk: pl.BlockSpec((tm, w), lambda i: (n - 1 - i, blk))
    specs, operands = _host(
        rider, [rev(3 * G, 0), rev(G, _C_Z), rev(LANES, _C_AB),
                pl.BlockSpec((1, NH, HD, HD), lambda i: (n - 1 - i, 0, 0, 0)),
                pl.BlockSpec((nb, DN_C, DN_C), lambda i: (n - 1 - i, 0, 0)), rev(G, 2)] + _dn_specs(),
        [rev(3 * G, 0), rev(G, 0), rev(LANES, 0), _acc((1, NH)), _acc((1, NH)), _acc((1, HD))],
        [jax.ShapeDtypeStruct((t, 3 * G), F32), jax.ShapeDtypeStruct((t, G), BF16),
         jax.ShapeDtypeStruct((t, LANES), BF16), jax.ShapeDtypeStruct((1, NH), F32),
         jax.ShapeDtypeStruct((1, NH), F32), jax.ShapeDtypeStruct((1, HD), F32)],
        [pltpu.VMEM((NH, HD, HD), F32)], [qkv, p, p, states, inv, dmix, *params])
    return pl.pallas_call(body, name="dn_bwd", grid=(n,), compiler_params=_cparams("arbitrary"), **specs)(*operands)


def _gla_specs(l):
    return [_layer((16, G), l), _resident((DEPTH, G)), _resident((DEPTH, HD))]


def _gla_fwd(p, params, l, cps=4, rider=None):
    t = p.shape[0]
    tm = GLA_C * cps

    def body(*refs):
        (qkv_ref, z_ref, glr_ref, w2_ref, gb_ref, nw_ref, y_ref, st_ref, state), ride = _split_refs(refs, 6, 2, 1, rider)
        _ride_begin(rider, ride, (t // tm,))

        @pl.when(pl.program_id(0) == 0)
        def _():
            state[...] = jnp.zeros_like(state)
        st_ref[0] = state[...]
        y, new = _gla_tile(qkv_ref[:, 0:G], qkv_ref[:, G:2 * G], qkv_ref[:, 2 * G:3 * G], glr_ref[...], z_ref[...],
                           state[...], w2_ref[...], gb_ref[l:l + 1, :], nw_ref[l:l + 1, :])
        y_ref[...] = y.astype(BF16)
        state[...] = new
        _ride_end(rider, ride, (t // tm,))

    specs, operands = _host(
        rider, [_row(tm, 3 * G, _D_QKV), _row(tm, G, _D_Z), _row(tm, LANES, _D_GLR)] + _gla_specs(l),
        [_row(tm, G), _STATE],
        [jax.ShapeDtypeStruct((t, G), BF16), jax.ShapeDtypeStruct((t // tm, NH, HD, HD), F32)],
        [pltpu.VMEM((NH, HD, HD), F32)], [p, p, p, *params])
    return pl.pallas_call(body, name="gla_fwd", grid=(t // tm,), compiler_params=_cparams("arbitrary"), **specs)(*operands)


def _gla_bwd(p, states, dmix, params, l, cps=4):
    t = p.shape[0]
    tm = GLA_C * cps
    n = t // tm

    def body(qkv_ref, z_ref, glr_ref, st_ref, dy_ref, w2_ref, gb_ref, nw_ref,
             dqkv_ref, dz_ref, dglr_ref, dw2_ref, dgb_ref, dnw_ref, dstate):
        dprm_refs = (dw2_ref, dgb_ref, dnw_ref)

        @pl.when(pl.program_id(0) == 0)
        def _():
            dstate[...] = jnp.zeros_like(dstate)
            for r in dprm_refs:
                r[...] = jnp.zeros_like(r)
        _, vjp = jax.vjp(_gla_tile, qkv_ref[:, 0:G], qkv_ref[:, G:2 * G], qkv_ref[:, 2 * G:3 * G], glr_ref[...],
                         z_ref[...], st_ref[0], w2_ref[...], gb_ref[l:l + 1, :], nw_ref[l:l + 1, :])
        dq, dk, dv, dglr, dz, dst, *dprm = vjp((dy_ref[...], dstate[...]))
        dqkv_ref[...] = jnp.concatenate([dq, dk, dv], axis=1).astype(BF16)
        dz_ref[...] = dz.astype(BF16)
        dglr_ref[...] = dglr.astype(BF16)
        dstate[...] = dst
        for r, g in zip(dprm_refs, dprm):
            r[...] += g

    rev = lambda w, blk: pl.BlockSpec((tm, w), lambda i: (n - 1 - i, blk))
    return pl.pallas_call(
        body, name="gla_bwd", grid=(n,),
        in_specs=[rev(3 * G, _D_QKV), rev(G, _D_Z), rev(LANES, _D_GLR),
                  pl.BlockSpec((1, NH, HD, HD), lambda i: (n - 1 - i, 0, 0, 0)), rev(G, 3)] + _gla_specs(l),
        out_specs=[rev(3 * G, 0), rev(G, 0), rev(LANES, 0), _acc((16, G)), _acc((1, G)), _acc((1, HD))],
        out_shape=[jax.ShapeDtypeStruct((t, 3 * G), BF16), jax.ShapeDtypeStruct((t, G), BF16),
                   jax.ShapeDtypeStruct((t, LANES), BF16), jax.ShapeDtypeStruct((16, G), F32),
                   jax.ShapeDtypeStruct((1, G), F32), jax.ShapeDtypeStruct((1, HD), F32)],
        scratch_shapes=[pltpu.VMEM((NH, HD, HD), F32)],
        compiler_params=_cparams("arbitrary"))(p, p, p, states, dmix, *params)


def _adamw_math(w, g, m, v):
    m = ADAM_B1 * m + (1.0 - ADAM_B1) * g
    v = ADAM_B2 * v + (1.0 - ADAM_B2) * (g * g)
    m_hat = m / (1.0 - ADAM_B1 ** ADAM_STEP)
    v_hat = v / (1.0 - ADAM_B2 ** ADAM_STEP)
    return -ADAM_LR * (m_hat / (jnp.sqrt(v_hat) + ADAM_EPS) + ADAM_WD * w), m, v


def _adamw_large(parts, w, m, v, name, tr, rider=None):
    _, r, c = w.shape
    grid = (DEPTH, r // tr)

    def body(*refs):
        (p0_ref, p1_ref, w_ref, m_ref, v_ref, g_ref, d_ref, nm_ref, nv_ref), ride = _split_refs(refs, 5, 4, 0, rider)
        _ride_begin(rider, ride, grid)

        def total(p_ref):
            g = p_ref[0].astype(F32)
            for k in range(1, N_DEV):
                g = g + p_ref[k].astype(F32)
            return g

        g = jnp.where(pl.program_id(0) == 0, total(p0_ref), total(p1_ref))
        g_ref[...] = g
        d_ref[...], nm_ref[...], nv_ref[...] = _adamw_math(w_ref[...], g, m_ref[...], v_ref[...])
        _ride_end(rider, ride, grid)

    blk = pl.BlockSpec((None, tr, c), lambda l, i: (l, i, 0))
    part = pl.BlockSpec((N_DEV, tr, c), lambda l, i: (0, i, 0))
    specs, operands = _host(rider, [part, part, blk, blk, blk], [blk] * 4, [jax.ShapeDtypeStruct(w.shape, F32)] * 4, [],
                            [*parts, w, m, v])
    return pl.pallas_call(body, name=name, grid=grid, compiler_params=_cparams("arbitrary", "arbitrary"),
                          **specs)(*operands)


_SLAB_AT = {
    "sgu_w_spatial": (0, 0, SGU_C, LANES),
    "sgu_b_spatial": (128, 0, NH, SGU_C),
    "norm1_w": (132, 0, 1, D),
    "norm2_w": (133, 0, 1, D),
    "sgu_ln_w": (134, 0, 1, G),
    "sgu_ln_b": (134, 256, 1, G),
    "gla_gate_bias": (134, 512, 1, G),
    "dn_norm_w": (134, 768, 1, HD),
    "gla_norm_w": (134, 896, 1, HD),
    "dn_a_log": (135, 0, 1, NH),
    "dn_dt_bias": (135, 128, 1, NH),
    "gla_w_gate2": (136, 0, 16, G),
    "dn_conv_w": (136, 256, 4, 3 * G),
    "sc_conv_w": (140, 256, 3, G),
}
_LAYER_ROWS = 152
_FINAL_ROW = DEPTH * _LAYER_ROWS
_SLAB_ROWS, _SLAB_COLS = _FINAL_ROW + 8, 1024
_SLAB_ORDER = tuple(_SLAB_AT)
_SHARDED_SMALL = ("sc_conv_w", "dn_conv_w", "gla_w_gate2")
_REPLICATED = tuple(k for k in _SLAB_ORDER if k not in _SHARDED_SMALL)


def _region(name, l, h=0):
    r0, c0, nr, nc = _SLAB_AT[name]
    return slice(_LAYER_ROWS * l + r0, _LAYER_ROWS * l + r0 + nr), slice(c0 + nc * h, c0 + nc * (h + 1))


def _pack_small(layer_grads, d_final):
    def body(*refs):
        slab = refs[-1]
        slab[...] = jnp.zeros_like(slab)
        it = iter(refs[:-1])
        for l in range(DEPTH):
            for name in _SLAB_ORDER:
                ref = next(it)
                if name == "sgu_w_spatial":
                    for h in range(NH):
                        slab[_region(name, l, h)] = ref[h]
                else:
                    slab[_region(name, l)] = ref[...]
        slab[_FINAL_ROW:_FINAL_ROW + 1, :] = next(it)[...]

    flat = [layer_grads[l][name] for l in range(DEPTH) for name in _SLAB_ORDER] + [d_final]
    return pl.pallas_call(body, name="pack_small_grads", out_shape=jax.ShapeDtypeStruct((_SLAB_ROWS, _SLAB_COLS), F32),
                          compiler_params=_cparams())(*flat)


def _adamw_small(parts, w, m, v):
    names = _REPLICATED + ("final_norm_w",)
    n_in = len(names)

    def body(*refs):
        def total(rows, cols):
            g = refs[0][0, rows, cols]
            for k in range(1, N_DEV):
                g = g + refs[0][k, rows, cols]
            return g

        w_refs = dict(zip(names, refs[1:1 + n_in]))
        m_refs = dict(zip(names, refs[1 + n_in:1 + 2 * n_in]))
        v_refs = dict(zip(names, refs[1 + 2 * n_in:1 + 3 * n_in]))
        outs = refs[1 + 3 * n_in:]
        out_refs = [dict(zip(names, outs[j * n_in:(j + 1) * n_in])) for j in range(4)]
        full_refs = dict(zip(_SHARDED_SMALL, outs[4 * n_in:]))

        def update(name, idx, g):
            res = (g,) + _adamw_math(w_refs[name][idx], g, m_refs[name][idx], v_refs[name][idx])
            for o, val in zip(out_refs, res):
                o[name][idx] = val

        for l in range(DEPTH):
            for name in _REPLICATED:
                if name == "sgu_w_spatial":
                    for h in range(NH):
                        update(name, (l, h), total(*_region(name, l, h)))
                elif name == "sgu_b_spatial":
                    update(name, (l,), total(*_region(name, l)))
                else:
                    update(name, (slice(l, l + 1), slice(None)), total(*_region(name, l)))
            for name in _SHARDED_SMALL:
                full_refs[name][l] = total(*_region(name, l))
        update("final_norm_w", (slice(None), slice(None)), total(slice(_FINAL_ROW, _FINAL_ROW + 1), slice(None)))

    shapes = [jax.ShapeDtypeStruct(w[k].shape, F32) for k in names]
    full_shapes = [jax.ShapeDtypeStruct((DEPTH,) + _SLAB_AT[k][2:], F32) for k in _SHARDED_SMALL]
    outs = pl.pallas_call(body, name="adamw_small", out_shape=shapes * 4 + full_shapes, compiler_params=_cparams())(
        parts, *[w[k] for k in names], *[m[k] for k in names], *[v[k] for k in names])
    result = [dict(zip(names, outs[j * n_in:(j + 1) * n_in])) for j in range(4)]
    return result, dict(zip(_SHARDED_SMALL, outs[4 * n_in:]))


def _adamw_shards(g, w, m, v):
    names = _SHARDED_SMALL
    n = len(names)

    def body(*refs):
        for j in range(n):
            g_v = refs[j][...]
            res = _adamw_math(refs[n + j][...], g_v, refs[2 * n + j][...], refs[3 * n + j][...])
            for o, val in zip(refs[4 * n + j::n], res):
                o[...] = val

    shapes = [jax.ShapeDtypeStruct(w[k].shape, F32) for k in names]
    outs = pl.pallas_call(body, name="adamw_small_shards", out_shape=shapes * 3, compiler_params=_cparams())(
        *[g[k] for k in names], *[w[k] for k in names], *[m[k] for k in names], *[v[k] for k in names])
    return [dict(zip(names, outs[j * n:(j + 1) * n])) for j in range(3)]


_ANY = pl.BlockSpec(memory_space=pl.ANY)


def _place():
    return lax.axis_index("x"), lax.axis_index("y"), lax.axis_index("c")


def _sems(n):
    return [pltpu.SemaphoreType.DMA((n, 7)), pltpu.SemaphoreType.DMA((n, 7)), pltpu.SemaphoreType.DMA((n,))]


class _Gather:
    def __init__(self, blocks, second):
        self.inputs, self.second = list(blocks), list(second)
        self.out_shape = [jax.ShapeDtypeStruct((a.shape[0], N_DEV) + a.shape[1:] if s else (N_DEV,) + a.shape, a.dtype)
                          for a, s in zip(blocks, second)]
        self.scratch = _sems(len(blocks))

    def _copies(self, refs):
        x_refs, out_refs, (send_sems, recv_sems, local_sems) = refs
        n = len(x_refs)
        x, y, c = _place()
        me, sibling = (x, y, c), (x, y, 1 - c)
        chips = [(1 - x, y), (x, 1 - y), (1 - x, 1 - y)]

        def slot(j, px, py, pc):
            idx = 4 * px + 2 * py + pc
            return out_refs[j].at[:, idx] if self.second[j] else out_refs[j].at[idx]

        def copies(k, block, to, own=False):
            return [pltpu.make_async_remote_copy(
                src_ref=x_refs[j] if own else slot(j, *block), dst_ref=slot(j, *block),
                send_sem=send_sems.at[j, k], recv_sem=recv_sems.at[j, k], device_id=to,
                device_id_type=pl.DeviceIdType.MESH) for j in range(n)]

        mine = [pltpu.make_async_copy(x_refs[j], slot(j, *me), local_sems.at[j]) for j in range(n)]
        first = copies(0, me, sibling, own=True)
        for j, chip in enumerate(chips):
            first += copies(1 + j, me, (*chip, c), own=True)
        landed = [copies(1 + j, (*chip, c), me) for j, chip in enumerate(chips)]
        onward = [copies(4 + j, (*chip, c), sibling) for j, chip in enumerate(chips)]
        from_sibling = copies(0, sibling, me)
        for j, chip in enumerate(chips):
            from_sibling += copies(4 + j, (*chip, 1 - c), me)
        return mine, first, landed, onward, from_sibling

    def start(self, refs):
        mine, first, _, _, _ = self._copies(refs)
        for cp in mine + first:
            cp.start()

    def forward(self, refs):
        _, _, landed, onward, _ = self._copies(refs)
        for arrived, passed in zip(landed, onward):
            for cp in arrived:
                cp.wait_recv()
            for cp in passed:
                cp.start()

    def finish(self, refs):
        mine, first, _, onward, from_sibling = self._copies(refs)
        for cp in from_sibling:
            cp.wait_recv()
        for cp in first + [cp for passed in onward for cp in passed]:
            cp.wait_send()
        for cp in mine:
            cp.wait()


class _Exchange:
    def __init__(self, sends):
        self.inputs = list(sends)
        self.out_shape = [jax.ShapeDtypeStruct(a.shape, a.dtype) for a in sends]
        self.scratch = _sems(len(sends))

    def _copies(self, refs):
        x_refs, out_refs, (send_sems, recv_sems, local_sems) = refs
        n = len(x_refs)
        x, y, c = _place()
        me = 4 * x + 2 * y + c
        sent, landing = [], []
        for k in range(1, N_DEV):
            px, py, pc = x ^ ((k >> 2) & 1), y ^ ((k >> 1) & 1), c ^ (k & 1)
            them = 4 * px + 2 * py + pc
            for j in range(n):
                for here, there, into in ((them, me, sent), (me, them, landing)):
                    into.append(pltpu.make_async_remote_copy(
                        src_ref=x_refs[j].at[here], dst_ref=out_refs[j].at[there], send_sem=send_sems.at[j, k - 1],
                        recv_sem=recv_sems.at[j, k - 1], device_id=(px, py, pc), device_id_type=pl.DeviceIdType.MESH))
        mine = [pltpu.make_async_copy(x_refs[j].at[me], out_refs[j].at[me], local_sems.at[j]) for j in range(n)]
        return mine, sent, landing

    def start(self, refs):
        mine, sent, _ = self._copies(refs)
        for cp in mine + sent:
            cp.start()

    def forward(self, refs):
        pass

    def finish(self, refs):
        mine, sent, landing = self._copies(refs)
        for cp in landing:
            cp.wait_recv()
        for cp in sent:
            cp.wait_send()
        for cp in mine:
            cp.wait()


def _run(rider, name):
    n_in, n_out = len(rider.inputs), len(rider.out_shape)

    def body(*refs):
        parts = (refs[:n_in], refs[n_in:n_in + n_out], refs[n_in + n_out:])
        rider.start(parts)
        rider.forward(parts)
        rider.finish(parts)

    return pl.pallas_call(body, name=name, out_shape=rider.out_shape, in_specs=[_ANY] * n_in, out_specs=[_ANY] * n_out,
                          scratch_shapes=rider.scratch)(*rider.inputs)


def _split_refs(refs, n_in, n_out, n_scratch, rider):
    r_in = len(rider.inputs) if rider else 0
    r_out = len(rider.out_shape) if rider else 0
    a = n_in + r_in
    b = a + n_out + r_out
    own = refs[:n_in] + refs[a:a + n_out] + refs[b:b + n_scratch]
    return own, (refs[n_in:a], refs[a + n_out:b], refs[b + n_scratch:])


def _grid_step(grid):
    step, stride = 0, 1
    for axis in reversed(range(len(grid))):
        step = step + pl.program_id(axis) * stride
        stride *= grid[axis]
    return step


def _ride_begin(rider, ride_refs, grid):
    if rider is not None:
        pl.when(_grid_step(grid) == 0)(lambda: rider.start(ride_refs))


def _ride_end(rider, ride_refs, grid, forward_at=1.0):
    if rider is not None:
        steps = 1
        for n in grid:
            steps *= n
        step = _grid_step(grid)
        pl.when(step == min(steps - 1, int(steps * forward_at)))(lambda: rider.forward(ride_refs))
        pl.when(step == steps - 1)(lambda: rider.finish(ride_refs))


def _host(rider, in_specs, out_specs, out_shape, scratch, operands):
    if rider is None:
        return dict(in_specs=in_specs, out_specs=out_specs, out_shape=out_shape, scratch_shapes=scratch), operands
    return dict(in_specs=in_specs + [_ANY] * len(rider.inputs), out_specs=out_specs + [_ANY] * len(rider.out_shape),
                out_shape=out_shape + rider.out_shape, scratch_shapes=scratch + rider.scratch), operands + rider.inputs


_LATE = ("w_gate_up", "w_out", "w_down")


def _local_grads(x, target, w, late=None, exchange=False):
    w = dict(w)
    w_in = list(w["w_in"])
    dn_params = (w["dn_a_log"], w["dn_dt_bias"], w["dn_norm_w"])
    gla_params = (w["gla_w_gate2"], w["gla_gate_bias"], w["gla_norm_w"])
    sgu_params = (w["sgu_ln_w"], w["sgu_ln_b"], w["sgu_w_spatial"], w["sgu_b_spatial"])
    saved = []
    for l in range(DEPTH):
        riding = l == 0 and late is not None
        h, p, *got = _in_proj(x, w["norm1_w"], w_in[l], l, rider=_Gather([late["w_down"]], [True]) if riding else None)
        if riding:
            w["w_down"] = got[0].reshape(DEPTH, FF, D)
        ya = _sgu_fwd(p, *sgu_params, l)
        yb = _sconv_fwd(p, w["sc_conv_w"], l)
        qkv_c = _qkv_conv_fwd(p, w["dn_conv_w"], l)
        yc, st_c, inv_c, *got = _dn_fwd(
            qkv_c, p, dn_params, l, forward_at=0.85,
            rider=_Gather([late["w_gate_up"], late["w_out"]], [False, True]) if riding else None)
        if riding:
            w.update(w_gate_up=got[0], w_out=got[1].reshape(DEPTH, D, D))
        yd, st_d, *got = _gla_fwd(p, gla_params, l, rider=_Gather([late["w_in"]], [False]) if riding else None)
        if riding:
            w_in[1] = _relayout_w_in(got[0])
        mix, x1, h2, gu, act, x2 = _out_mlp(x, (ya, yb, yc, yd), w["w_out"], w["norm2_w"], w["w_gate_up"], w["w_down"], l)
        saved.append(dict(x=x, h=h, p=p, qkv_c=qkv_c, st_c=st_c, inv_c=inv_c, st_d=st_d, mix=mix, x1=x1, h2=h2, gu=gu,
                          act=act))
        x = x2
    loss, d_final, dx = _loss_head(x, w["final_norm_w"], target)
    large = {k: [None] * DEPTH for k in ("w_in", "w_out", "w_gate_up", "w_down")}
    small = [None] * DEPTH
    for l in reversed(range(DEPTH)):
        s = saved[l]
        p = s["p"]
        g = {}
        riding = exchange and l == 0

        def exchanging(pairs):
            return _Exchange([large[k][j] for k, j in pairs]) if riding else None

        def arrive(pairs, arrived):
            for (k, j), a in zip(pairs, arrived):
                large[k][j] = a

        pairs = (("w_gate_up", 1), ("w_down", 1))
        dgu, dx1, dmix, g["norm2_w"], *arrived = _mlp_out_bwd(
            dx, s["x1"], s["gu"], w["norm2_w"], w["w_down"], w["w_gate_up"], w["w_out"], l, rider=exchanging(pairs))
        arrive(pairs, arrived)
        pairs = (("w_in", 1), ("w_out", 1))
        d_gu, *arrived = _wgrad_rows(dgu, s["h2"], "wgrad_gate_up", rider=exchanging(pairs)) if riding else (
            _wgrad_rows(dgu, s["h2"], "wgrad_gate_up"),)
        arrive(pairs, arrived)
        large["w_gate_up"][l] = d_gu.reshape(N_DEV, GU_SHARD, D)
        large["w_down"][l] = _wgrad_rows(s["act"], dx, "wgrad_down").reshape(N_DEV, FF // N_DEV, D)
        large["w_out"][l] = _matmul_tn(s["mix"], dx1, "wgrad_out", BF16).reshape(N_DEV, D // N_DEV, D)
        d_a, g["sgu_ln_w"], g["sgu_ln_b"], g["sgu_w_spatial"], g["sgu_b_spatial"] = _sgu_bwd(p, dmix, *sgu_params, l)
        d_b, g["sc_conv_w"] = _sconv_bwd(p, dmix, w["sc_conv_w"], l)
        pairs = tuple((k, 0) for k in _LATE)
        dqkv_c, dz_c, dab, g["dn_a_log"], g["dn_dt_bias"], g["dn_norm_w"], *arrived = _dn_bwd(
            s["qkv_c"], p, s["st_c"], s["inv_c"], dmix, dn_params, l, rider=exchanging(pairs))
        arrive(pairs, arrived)
        d_c, g["dn_conv_w"] = _qkv_conv_bwd(p, dqkv_c, w["dn_conv_w"], l)
        d_d, dz_d, dglr, g["gla_w_gate2"], g["gla_gate_bias"], g["gla_norm_w"] = _gla_bwd(p, s["st_d"], dmix, gla_params, l)
        dps = (d_c, d_d, d_a, d_b, dz_c, dz_d, dab, dglr)
        pairs = (("w_in", 0),)
        large["w_in"][l] = _scatter_w_in_grad(
            [_matmul_tn(s["h"], dp, "wgrad_in_%d" % j, F32, tn=dp.shape[1]) for j, dp in enumerate(dps)])
        dx, g["norm1_w"], *arrived = _in_proj_bwd(dps, s["x"], dx1, w["norm1_w"], w_in[l], l, rider=exchanging(pairs))
        arrive(pairs, arrived)
        small[l] = g
    return loss[0, 0], dx, large, small, d_final


_ORDER = ("norm1_w", "w_in", "sgu_ln_w", "sgu_ln_b", "sgu_w_spatial", "sgu_b_spatial", "sc_conv_w", "dn_conv_w",
          "dn_a_log", "dn_dt_bias", "dn_norm_w", "gla_w_gate2", "gla_gate_bias", "gla_norm_w", "w_out", "norm2_w",
          "w_gate_up", "w_down", "final_norm_w")
_LARGE = ("w_gate_up", "w_in", "w_out", "w_down")
_LARGE_TILE = {"w_in": 256, "w_gate_up": 352, "w_out": 128, "w_down": 352}


def _gather_cols(g):
    return jnp.transpose(g, (1, 2, 0, 3)).reshape(g.shape[1], g.shape[2], N_DEV * g.shape[3])


def kernel(x, norm1_w, w_in, sgu_ln_w, sgu_ln_b, sgu_w_spatial, sgu_b_spatial, sc_conv_w, dn_conv_w, dn_a_log, dn_dt_bias, dn_norm_w, gla_w_gate2, gla_gate_bias, gla_norm_w, w_out, norm2_w, w_gate_up, w_down, final_norm_w, loss_target, m_norm1_w, m_w_in, m_sgu_ln_w, m_sgu_ln_b, m_sgu_w_spatial, m_sgu_b_spatial, m_sc_conv_w, m_dn_conv_w, m_dn_a_log, m_dn_dt_bias, m_dn_norm_w, m_gla_w_gate2, m_gla_gate_bias, m_gla_norm_w, m_w_out, m_norm2_w, m_w_gate_up, m_w_down, m_final_norm_w, v_norm1_w, v_w_in, v_sgu_ln_w, v_sgu_ln_b, v_sgu_w_spatial, v_sgu_b_spatial, v_sc_conv_w, v_dn_conv_w, v_dn_a_log, v_dn_dt_bias, v_dn_norm_w, v_gla_w_gate2, v_gla_gate_bias, v_gla_norm_w, v_w_out, v_norm2_w, v_w_gate_up, v_w_down, v_final_norm_w):
    args = dict(locals())
    wts = {k: args[k] for k in _ORDER}
    mom = {k: args["m_" + k] for k in _ORDER}
    var = {k: args["v_" + k] for k in _ORDER}
    for d in (wts, mom, var):
        d["final_norm_w"] = d["final_norm_w"][None]
    me = 4 * lax.axis_index("x") + 2 * lax.axis_index("y") + lax.axis_index("c")
    for d in (wts, mom, var):
        d["w_gate_up"] = jnp.swapaxes(d["w_gate_up"], 1, 2)

    late = {k: wts[k].astype(BF16) for k in _LATE}
    w_in = wts["w_in"].astype(BF16)
    late["w_in"] = w_in[1]
    got = _run(_Gather([w_in[0]] + [wts[k] for k in _SHARDED_SMALL], [False] * 4), "gather_w_in")
    full = dict(wts)
    full["w_in"] = [_relayout_w_in(got[0]), None]
    for k, g in zip(_SHARDED_SMALL, got[1:]):
        full[k] = _gather_cols(g)

    loss, dx, large, small, d_final = _local_grads(x[0], loss_target[0], full, late=late, exchange=True)
    loss = lax.psum(loss, ("x", "y", "c"))

    result = {}
    for k in _LARGE:
        rider = _Gather([_pack_small(small, d_final)], [False]) if k == _LARGE[0] else None
        *outs, = _adamw_large(large[k], wts[k], mom[k], var[k], "adamw_" + k, _LARGE_TILE[k], rider=rider)
        if rider is not None:
            slabs = outs.pop()
        for kind, a in zip(("grad", "delta", "new_m", "new_v"), outs):
            result[kind, k] = jnp.swapaxes(a, 1, 2) if k == "w_gate_up" else a

    rep = _REPLICATED + ("final_norm_w",)
    outs, summed = _adamw_small(slabs, {k: wts[k] for k in rep}, {k: mom[k] for k in rep}, {k: var[k] for k in rep})
    for kind, d in zip(("grad", "delta", "new_m", "new_v"), outs):
        for k, a in d.items():
            result[kind, k] = a[0] if k == "final_norm_w" else a
    own = {k: lax.dynamic_slice_in_dim(summed[k], me * wts[k].shape[-1], wts[k].shape[-1], axis=2) for k in _SHARDED_SMALL}
    outs = _adamw_shards(own, {k: wts[k] for k in _SHARDED_SMALL}, {k: mom[k] for k in _SHARDED_SMALL},
                         {k: var[k] for k in _SHARDED_SMALL})
    for kind, d in zip(("grad", "delta", "new_m", "new_v"), [own] + outs):
        for k, a in d.items():
            result[kind, k] = a

    return (loss, dx[None], *[result[kind, k] for kind in ("grad", "delta", "new_m", "new_v") for k in _ORDER])
```

```python
import functools

import jax
import jax.numpy as jnp
from jax import lax
from jax.experimental import pallas as pl
from jax.experimental.pallas import tpu as pltpu

F32, BF16 = jnp.float32, jnp.bfloat16
DEPTH = 2
D = 1024
G = 256
NH, HD = 4, 64
FF = 2816
IN_COLS = 3352
P = 3584
EPS = 1e-6
SGU_C, DN_C, GLA_C = 128, 64, 64
N_DEV = 8
IN_SHARD = IN_COLS // N_DEV
GU_SHARD = 2 * FF // N_DEV
LANES = 128
VMEM_LIMIT = 56 * 2 ** 20

_PAD_SEGS = ((1280, 2048, 0),
             (2312, 3080, 0),
             (0, 512, 0),
             (512, 1280, 0),
             (2056, 2312, 0),
             (3096, 3352, 0),
             (2048, 2056, 120),
             (3080, 3096, 112))

ADAM_LR, ADAM_B1, ADAM_B2, ADAM_EPS, ADAM_WD, ADAM_STEP = 0.001, 0.9, 0.999, 1e-08, 0.01, 10


def _cparams(*sem):
    return pltpu.CompilerParams(dimension_semantics=sem, vmem_limit_bytes=VMEM_LIMIT)


def _resident(shape):
    return pl.BlockSpec(shape, lambda *_: (0,) * len(shape), pipeline_mode=pl.Buffered(1))


def _layer(shape, l):
    return pl.BlockSpec((None,) + tuple(shape), lambda *_: (l,) + (0,) * len(shape), pipeline_mode=pl.Buffered(1))


def _acc(shape):
    return pl.BlockSpec(shape, lambda *_: (0,) * len(shape))


def _dg(a, b, ca, cb):
    lead = a.ndim - 2
    batch = ((0,), (0,)) if lead else ((), ())
    return lax.dot_general(a, b, (((ca + lead,), (cb + lead,)), batch), preferred_element_type=F32)


def _split(t):
    hi = t.astype(BF16)
    return hi, (t - hi.astype(F32)).astype(BF16)


def _dg3(a, b, ca, cb):
    (ah, al), (bh, bl) = a, b
    return _dg(ah, bh, ca, cb) + (_dg(ah, bl, ca, cb) + _dg(al, bh, ca, cb))


def _make_dot(accurate):
    def raw(a, b, form):
        ca = 0 if form == "tn" else 1
        cb = 1 if form == "nt" else 0
        if accurate:
            return _dg3(_split(a), _split(b), ca, cb)
        return _dg(a.astype(BF16), b.astype(BF16), ca, cb)

    @functools.partial(jax.custom_vjp, nondiff_argnums=(2,))
    def dot(a, b, form):
        return raw(a, b, form)

    def fwd(a, b, form):
        return raw(a, b, form), (a, b)

    def bwd(form, res, g):
        a, b = res
        if form == "nn":
            return raw(g, b, "nt"), raw(a, g, "tn")
        if form == "nt":
            return raw(g, b, "nn"), raw(g, a, "tn")
        return raw(b, g, "nt"), raw(a, g, "nn")

    dot.defvjp(fwd, bwd)
    return dot


_bdot = _make_dot(False)
_hdot = _make_dot(True)


def _inv_unit_lower(low):
    n = low.shape[-1]
    eye = (lax.broadcasted_iota(jnp.int32, (n, n), 0) == lax.broadcasted_iota(jnp.int32, (n, n), 1)).astype(F32)
    p = -low
    inv = eye + p
    ps = _split(p)
    k = 1
    while 2 * k < n:
        ps = _split(_dg3(ps, ps, 1, 0))
        inv = inv + _dg3(_split(inv), ps, 1, 0)
        k *= 2
    return inv


@jax.custom_vjp
def _unit_lower_solve(low, rhs, inv):
    return _dg3(_split(inv), _split(rhs), 1, 0)


def _uls_fwd(low, rhs, inv):
    sol = _dg3(_split(inv), _split(rhs), 1, 0)
    return sol, (inv, sol)


def _uls_bwd(res, g):
    inv, sol = res
    d_rhs = _dg3(_split(inv), _split(g), 0, 0)
    return -_dg3(_split(d_rhs), _split(sol), 1, 1), d_rhs, jnp.zeros_like(inv)


_unit_lower_solve.defvjp(_uls_fwd, _uls_bwd)


def _sigmoid(x):
    return 0.5 * jnp.tanh(0.5 * x) + 0.5


def _softplus(x):
    return jnp.maximum(x, 0.0) + jnp.log(1.0 + jnp.exp(-jnp.maximum(x, -x)))


def _gelu(x):
    return 0.5 * x * (1.0 + jnp.tanh(0.7978845608028654 * (x + 0.044715 * (x * x * x))))


def _tri(n):
    ri = lax.broadcasted_iota(jnp.int32, (n, n), 0)
    ci = lax.broadcasted_iota(jnp.int32, (n, n), 1)
    return ri, ci


def _stack(ts):
    return jnp.concatenate([t[None] for t in ts], axis=0)


@jax.custom_vjp
def _head_sums(x):
    ri, ci = _tri(x.shape[-1])
    shift = HD.bit_length() - 1
    ones = (jnp.right_shift(ri, shift) == jnp.right_shift(ci, shift)).astype(BF16)
    hi, lo = _split(x)
    return _dg(hi, ones, 1, 0) + _dg(lo, ones, 1, 0)


_head_sums.defvjp(lambda x: (_head_sums(x), None), lambda _, g: (_head_sums(g),))


def _chunk_mask_dot(x, c, running, transpose):
    ri, ci = _tri(x.shape[0])
    shift = c.bit_length() - 1
    mask = jnp.right_shift(ri, shift) == jnp.right_shift(ci, shift)
    if running:
        mask = mask & (ri >= ci)
    hi, lo = _split(x)
    m = mask.astype(BF16)
    ca = 0 if transpose else 1
    return _dg(m, hi, ca, 0) + _dg(m, lo, ca, 0)


@functools.partial(jax.custom_vjp, nondiff_argnums=(1, 2))
def _chunk_sums(x, c, running):
    return _chunk_mask_dot(x, c, running, False)


_chunk_sums.defvjp(lambda x, c, running: (_chunk_mask_dot(x, c, running, False), None),
                   lambda c, running, _, g: (_chunk_mask_dot(g, c, running, True),))


def _spread_onto(x, first, transpose):
    ri = lax.broadcasted_iota(jnp.int32, (LANES, G), 0)
    ci = lax.broadcasted_iota(jnp.int32, (LANES, G), 1)
    sel = (ri == first + jnp.right_shift(ci, HD.bit_length() - 1)).astype(BF16)
    hi, lo = _split(x)
    cb = 1 if transpose else 0
    return _dg(hi, sel, 1, cb) + _dg(lo, sel, 1, cb)


@functools.partial(jax.custom_vjp, nondiff_argnums=(1,))
def _spread(x, first):
    return _spread_onto(x, first, False)


_spread.defvjp(lambda x, first: (_spread_onto(x, first, False), None),
               lambda first, _, g: (_spread_onto(g, first, True),))


def _head_columns_dot(x, transpose):
    sel = (lax.broadcasted_iota(jnp.int32, (NH, G), 0)
           == jnp.right_shift(lax.broadcasted_iota(jnp.int32, (NH, G), 1), HD.bit_length() - 1)).astype(BF16)
    hi, lo = _split(x)
    if transpose:
        return _dg(sel, hi, 1, 1) + _dg(sel, lo, 1, 1)
    return _dg(hi, sel, 0, 0) + _dg(lo, sel, 0, 0)


@jax.custom_vjp
def _head_columns(x):
    return _head_columns_dot(x, False)


_head_columns.defvjp(lambda x: (_head_columns_dot(x, False), None), lambda _, g: (_head_columns_dot(g, True),))


def _sgu_chunk(uv, ln_w, ln_b, ws, bs):
    u = _gelu(uv[:, :G])
    v = _gelu(uv[:, G:])
    mu = jnp.mean(v, axis=-1, keepdims=True)
    vc = v - mu
    var = jnp.mean(vc * vc, axis=-1, keepdims=True)
    vn = vc * lax.rsqrt(var + EPS) * ln_w + ln_b
    ri, ci = _tri(SGU_C)
    mixed = [_bdot(jnp.where(ri >= ci, ws[h], 0.0), vn[:, HD * h:HD * (h + 1)], "nn") for h in range(NH)]
    return u * (jnp.concatenate(mixed, axis=1) + _head_columns(bs))


def _dn_tile(q, k, v, ab, z, state, inv, a_log, dt_bias, nw):
    c = DN_C
    tm = q.shape[0]
    cps = tm // c
    ri, ci = _tri(tm)
    shift = c.bit_length() - 1
    same = jnp.right_shift(ri, shift) == jnp.right_shift(ci, shift)
    g4 = -jnp.exp(a_log) * _softplus(ab[:, 0:NH] + dt_bias)
    gates = jnp.concatenate([g4, _sigmoid(ab[:, NH:2 * NH]), jnp.zeros((tm, LANES - 2 * NH), F32)], axis=1)
    g, beta = _spread(gates, 0), _spread(gates, NH)
    gc = _chunk_sums(g, c, True)
    gl = _chunk_sums(g, c, False)
    gr4 = _hdot(g4, (same & (ri <= ci)).astype(F32), "tn")
    pairs = [(slice(c * j, c * (j + 1)), h) for j in range(cps) for h in range(NH)]
    heads = lambda t: _stack([t[rows, HD * h:HD * (h + 1)] for rows, h in pairs])
    qn = q * lax.rsqrt(_head_sums(q * q) + EPS) * (HD ** -0.5)
    kn = k * lax.rsqrt(_head_sums(k * k) + EPS)
    eg = jnp.exp(gc)
    kb = kn * beta
    rhs = jnp.concatenate([heads(v * beta), heads(kb * eg)], axis=2)
    qg, kd = heads(qn * eg), heads(kn * jnp.exp(gl - gc))
    cd = _stack([jnp.exp(gl[c * j:c * j + 1, HD * h:HD * (h + 1)]) for j in range(cps) for h in range(NH)])
    qn, kn, kb, gc = heads(qn), heads(kn), heads(kb), heads(gc)
    gr = _stack([gr4[h:h + 1, rows] for rows, h in pairs])
    r2, c2 = _tri(c)
    decay = jnp.exp(jnp.where(r2 >= c2, gc - gr, -jnp.inf))
    low = jnp.where(r2 > c2, _bdot(kb, kn, "nt") * decay, 0.0)
    if inv is None:
        inv = _inv_unit_lower(low)
    sol = _unit_lower_solve(low, rhs, inv)
    u, w = sol[:, :, :HD], sol[:, :, HD:]
    attn = _bdot(qn, kn, "nt") * decay
    ys = []
    for j in range(cps):
        b = slice(NH * j, NH * (j + 1))
        v_new = u[b] - _bdot(w[b], state, "nn")
        o = _bdot(qg[b], state, "nn") + _bdot(attn[b], v_new, "nn")
        state = state * cd[b] + _bdot(kd[b], v_new, "tn")
        on = o * lax.rsqrt(jnp.mean(o * o, axis=-1, keepdims=True) + EPS) * nw
        ys.append(jnp.concatenate([on[h] for h in range(NH)], axis=1))
    return jnp.concatenate(ys, axis=0) * (z * _sigmoid(z)), state, inv


def _gla_tile(q, k, v, glr, z, state_t, w2, gate_bias, nw):
    c = GLA_C
    tm = q.shape[0]
    cps = tm // c
    w2_rows = jnp.concatenate([w2, jnp.zeros((LANES - w2.shape[0], G), F32)], axis=0)
    pre = _bdot(glr, w2_rows, "nn") + gate_bias
    log_a = (jnp.minimum(pre, 0.0) - jnp.log(1.0 + jnp.exp(-jnp.maximum(pre, -pre)))) * (1.0 / 16.0)
    gcum = _chunk_sums(log_a, c, True)
    row = lax.broadcasted_iota(jnp.int32, (c, G), 0)
    qs = q * (HD ** -0.5)
    qa, ka, qg, kl, dec = [], [], [], [], []
    for j in range(cps):
        rows = slice(c * j, c * (j + 1))
        gj = gcum[rows]
        g_mid = jnp.sum(jnp.where(row == c // 2, gj, 0.0), axis=0, keepdims=True)
        g_last = jnp.sum(log_a[rows], axis=0, keepdims=True)
        qa.append(qs[rows] * jnp.exp(gj - g_mid))
        ka.append(k[rows] * jnp.exp(g_mid - gj))
        qg.append(qs[rows] * jnp.exp(gj))
        kl.append(k[rows] * jnp.exp(g_last - gj))
        dec.append(jnp.exp(g_last))
    heads = lambda ts: _stack([t[:, HD * h:HD * (h + 1)] for t in ts for h in range(NH)])
    qa, ka, qg, kl, dec = heads(qa), heads(ka), heads(qg), heads(kl), heads(dec)
    vh = heads([v[c * j:c * (j + 1)] for j in range(cps)])
    r2, c2 = _tri(c)
    o_intra = _bdot(jnp.where(r2 >= c2, _bdot(qa, ka, "nt"), 0.0), vh, "nn")
    ys = []
    for j in range(cps):
        b = slice(NH * j, NH * (j + 1))
        o = o_intra[b] + _bdot(qg[b], state_t, "nt")
        state_t = state_t * dec[b] + _bdot(vh[b], kl[b], "tn")
        on = o * lax.rsqrt(jnp.mean(o * o, axis=-1, keepdims=True) + EPS) * nw
        ys.append(jnp.concatenate([on[h] for h in range(NH)], axis=1))
    return jnp.concatenate(ys, axis=0) * (z * _sigmoid(z)), state_t


def _rms(x, nw):
    r = lax.rsqrt(jnp.mean(x * x, axis=-1, keepdims=True) + EPS)
    xh = x * r
    return xh * nw, xh, r


def _rms_bwd(g, xh, r, nw):
    gw = g * nw
    return r * (gw - xh * jnp.mean(gw * xh, axis=-1, keepdims=True)), jnp.sum(g * xh, axis=0, keepdims=True)


def _row(tm, w, blk=0):
    return pl.BlockSpec((tm, w), lambda i: (i, blk))


def _in_proj(x, nw, wp, l, tm=512, rider=None):
    t = x.shape[0]

    def body(*refs):
        (x_ref, nw_ref, w_ref, h_ref, p_ref), ride = _split_refs(refs, 3, 2, 0, rider)
        _ride_begin(rider, ride, (t // tm,))
        h = _rms(x_ref[...], nw_ref[l:l + 1, :])[0].astype(BF16)
        h_ref[...] = h
        p_ref[...] = jnp.dot(h, w_ref[...], preferred_element_type=F32)
        _ride_end(rider, ride, (t // tm,))

    specs, operands = _host(
        rider, [_row(tm, D), _resident((DEPTH, D)), _resident((D, P))], [_row(tm, D), _row(tm, P)],
        [jax.ShapeDtypeStruct((t, D), BF16), jax.ShapeDtypeStruct((t, P), F32)], [], [x, nw, wp])
    return pl.pallas_call(body, name="in_proj", grid=(t // tm,), compiler_params=_cparams("arbitrary"),
                          **specs)(*operands)


def _gu_spec(l):
    return pl.BlockSpec((N_DEV, None, GU_SHARD, D), lambda *_: (0, l, 0, 0), pipeline_mode=pl.Buffered(1))


def _out_mlp(x, ys, w_out, nw2, w_gu_t, w_down, l, tm=256):
    t = x.shape[0]

    def body(x_ref, ya, yb, yc, yd, wo_ref, nw_ref, wgu_ref, wd_ref, mix_ref, x1_ref, h2_ref, gu_ref, act_ref, x2_ref):
        mix = jnp.concatenate([ya[...], yb[...], yc[...], yd[...]], axis=1)
        mix_ref[...] = mix
        x1 = x_ref[...] + jnp.dot(mix, wo_ref[...], preferred_element_type=F32)
        x1_ref[...] = x1
        h2 = _rms(x1, nw_ref[l:l + 1, :])[0].astype(BF16)
        h2_ref[...] = h2
        gu = _dg(h2, wgu_ref[...].reshape(2 * FF, D), 1, 1)
        gu_ref[...] = gu.astype(BF16)
        gate, up = gu[:, :FF], gu[:, FF:]
        act = (gate * _sigmoid(gate) * up).astype(BF16)
        act_ref[...] = act
        x2_ref[...] = x1 + jnp.dot(act, wd_ref[...], preferred_element_type=F32)

    return pl.pallas_call(
        body, name="out_mlp", grid=(t // tm,),
        in_specs=[_row(tm, D), _row(tm, G), _row(tm, G), _row(tm, G), _row(tm, G), _layer((D, D), l),
                  _resident((DEPTH, D)), _gu_spec(l), _layer((FF, D), l)],
        out_specs=[_row(tm, D), _row(tm, D), _row(tm, D), _row(tm, 2 * FF), _row(tm, FF), _row(tm, D)],
        out_shape=[jax.ShapeDtypeStruct((t, D), BF16), jax.ShapeDtypeStruct((t, D), F32),
                   jax.ShapeDtypeStruct((t, D), BF16), jax.ShapeDtypeStruct((t, 2 * FF), BF16),
                   jax.ShapeDtypeStruct((t, FF), BF16), jax.ShapeDtypeStruct((t, D), F32)],
        compiler_params=_cparams("parallel"))(x, *ys, w_out, nw2, w_gu_t, w_down)


def _loss_head(x, nw, target, tm=512):
    t = x.shape[0]

    def body(x_ref, nw_ref, tg_ref, loss_ref, dnw_ref, dx_ref):
        @pl.when(pl.program_id(0) == 0)
        def _():
            loss_ref[...] = jnp.zeros_like(loss_ref)
            dnw_ref[...] = jnp.zeros_like(dnw_ref)
        nw_v = nw_ref[...]
        y, xh, r = _rms(x_ref[...], nw_v)
        err = y - tg_ref[...]
        loss_ref[...] += 0.5 * jnp.sum(err * err) * (1.0 / D)
        dx, dnw = _rms_bwd(err * (1.0 / D), xh, r, nw_v)
        dx_ref[...] = dx
        dnw_ref[...] += dnw

    return pl.pallas_call(
        body, name="loss_head", grid=(t // tm,),
        in_specs=[_row(tm, D), _resident((1, D)), _row(tm, D)],
        out_specs=[_acc((8, LANES)), _acc((1, D)), _row(tm, D)],
        out_shape=[jax.ShapeDtypeStruct((8, LANES), F32), jax.ShapeDtypeStruct((1, D), F32),
                   jax.ShapeDtypeStruct((t, D), F32)],
        compiler_params=_cparams("arbitrary"))(x, nw, target)


def _mlp_out_bwd(dx2, x1, gu, nw2, w_down, w_gu, w_out, l, tm=256, rider=None):
    t = dx2.shape[0]

    def body(*refs):
        (dx2_ref, x1_ref, gu_ref, nw_ref, wd_ref, wgu_ref, wo_ref, dgu_ref, dx1_ref, dmix_ref, dnw_ref), ride = \
            _split_refs(refs, 7, 4, 0, rider)
        _ride_begin(rider, ride, (t // tm,))

        @pl.when(pl.program_id(0) == 0)
        def _():
            dnw_ref[...] = jnp.zeros_like(dnw_ref)
        dx2 = dx2_ref[...]
        dact = _dg(dx2.astype(BF16), wd_ref[...], 1, 1)
        gu = gu_ref[...].astype(F32)
        gate, up = gu[:, :FF], gu[:, FF:]
        s = _sigmoid(gate)
        dgu = jnp.concatenate([dact * up * (s * (1.0 + gate * (1.0 - s))), dact * (gate * s)], axis=1).astype(BF16)
        dgu_ref[...] = dgu
        dh2 = jnp.dot(dgu, wgu_ref[...].reshape(2 * FF, D), preferred_element_type=F32)
        nw_v = nw_ref[l:l + 1, :]
        _, xh, r = _rms(x1_ref[...], nw_v)
        dxn, dnw = _rms_bwd(dh2, xh, r, nw_v)
        dx1 = dx2 + dxn
        dx1_ref[...] = dx1
        dnw_ref[...] += dnw
        dmix_ref[...] = _dg(dx1.astype(BF16), wo_ref[...], 1, 1)
        _ride_end(rider, ride, (t // tm,))

    specs, operands = _host(
        rider, [_row(tm, D), _row(tm, D), _row(tm, 2 * FF), _resident((DEPTH, D)), _layer((FF, D), l), _gu_spec(l),
                _layer((D, D), l)],
        [_row(tm, 2 * FF), _row(tm, D), _row(tm, D), _acc((1, D))],
        [jax.ShapeDtypeStruct((t, 2 * FF), BF16), jax.ShapeDtypeStruct((t, D), F32),
         jax.ShapeDtypeStruct((t, D), F32), jax.ShapeDtypeStruct((1, D), F32)],
        [], [dx2, x1, gu, nw2, w_down, w_gu, w_out])
    return pl.pallas_call(body, name="mlp_out_bwd", grid=(t // tm,), compiler_params=_cparams("arbitrary"),
                          **specs)(*operands)


_DP_WIDTHS = (768, 768, 512, 768, 256, 256, 128, 128)


def _in_proj_bwd(dps, x, dx1, nw, wp, l, tm=512):
    t = x.shape[0]

    def body(*refs):
        dp_refs, (x_ref, dx1_ref, nw_ref, w_ref, dp_ref, dx_ref, dnw_ref) = refs[:8], refs[8:]

        @pl.when(pl.program_id(0) == 0)
        def _():
            dnw_ref[...] = jnp.zeros_like(dnw_ref)
        dp = jnp.concatenate([r[...] for r in dp_refs], axis=1)
        dp_ref[...] = dp
        dh = _dg(dp, w_ref[...], 1, 1)
        nw_v = nw_ref[l:l + 1, :]
        _, xh, r = _rms(x_ref[...], nw_v)
        dxn, dnw = _rms_bwd(dh, xh, r, nw_v)
        dx_ref[...] = dx1_ref[...] + dxn
        dnw_ref[...] += dnw

    return pl.pallas_call(
        body, name="in_proj_bwd", grid=(t // tm,),
        in_specs=[_row(tm, w) for w in _DP_WIDTHS] + [_row(tm, D), _row(tm, D), _resident((DEPTH, D)), _resident((D, P))],
        out_specs=[_row(tm, P), _row(tm, D), _acc((1, D))],
        out_shape=[jax.ShapeDtypeStruct((t, P), BF16), jax.ShapeDtypeStruct((t, D), F32),
                   jax.ShapeDtypeStruct((1, D), F32)],
        compiler_params=_cparams("arbitrary"))(*dps, x, dx1, nw, wp)


def _accumulate(acc, o_ref, t_axis, term):
    @pl.when(pl.program_id(t_axis) == 0)
    def _():
        acc[...] = jnp.zeros_like(acc)
    acc[...] += term

    @pl.when(pl.program_id(t_axis) == pl.num_programs(t_axis) - 1)
    def _():
        o_ref[...] = acc[...].astype(o_ref.dtype)


WGRAD_TOKENS = 2048


def _matmul_tn(a, b, name, out_dtype, tn=512, tt=WGRAD_TOKENS, rider=None):
    t, m = a.shape
    n = b.shape[1]
    tt = min(tt, t)
    grid = (n // tn, t // tt)

    def body(*refs):
        (a_ref, b_ref, o_ref, acc), ride = _split_refs(refs, 2, 1, 1, rider)
        _ride_begin(rider, ride, grid)
        _accumulate(acc, o_ref, 1, _dg(a_ref[...].astype(BF16), b_ref[...].astype(BF16), 0, 0))
        _ride_end(rider, ride, grid)

    specs, operands = _host(
        rider, [pl.BlockSpec((tt, m), lambda j, i: (i, 0)), pl.BlockSpec((tt, tn), lambda j, i: (i, j))],
        [pl.BlockSpec((m, tn), lambda j, i: (0, j))], [jax.ShapeDtypeStruct((m, n), out_dtype)],
        [pltpu.VMEM((m, tn), F32)], [a, b])
    out = pl.pallas_call(body, name=name, grid=grid, compiler_params=_cparams("arbitrary", "arbitrary"), **specs)(*operands)
    return out[0] if rider is None else out


WGRAD_ROWS = 2 * GU_SHARD


def _wgrad_rows(a, b, name, tt=WGRAD_TOKENS, rows=WGRAD_ROWS, out_dtype=BF16, rider=None):
    t, m = a.shape
    tt = min(tt, t)
    grid = (m // rows, t // tt)

    def body(*refs):
        (a_ref, b_ref, o_ref, acc), ride = _split_refs(refs, 2, 1, 1, rider)
        _ride_begin(rider, ride, grid)
        _accumulate(acc, o_ref, 1, _dg(a_ref[...], b_ref[...].astype(BF16), 0, 0))
        _ride_end(rider, ride, grid)

    specs, operands = _host(
        rider, [pl.BlockSpec((tt, rows), lambda j, i: (i, j)), pl.BlockSpec((tt, D), lambda j, i: (i, 0))],
        [pl.BlockSpec((rows, D), lambda j, i: (j, 0))], [jax.ShapeDtypeStruct((m, D), out_dtype)],
        [pltpu.VMEM((rows, D), F32)], [a, b])
    out = pl.pallas_call(body, name=name, grid=grid, compiler_params=_cparams("arbitrary", "arbitrary"), **specs)(*operands)
    return out[0] if rider is None else out


def _relayout_w_in(g, tr=256):
    def body(w_ref, o_ref):
        full = jnp.concatenate([w_ref[d] for d in range(N_DEV)], axis=1)
        parts = []
        for a, b, z in _PAD_SEGS:
            parts.append(full[:, a:b])
            if z:
                parts.append(jnp.zeros((tr, z), full.dtype))
        o_ref[...] = jnp.concatenate(parts, axis=1)

    return pl.pallas_call(
        body, name="relayout_w_in", grid=(D // tr,),
        in_specs=[pl.BlockSpec((N_DEV, tr, IN_SHARD), lambda i: (0, i, 0))], out_specs=_row(tr, P),
        out_shape=jax.ShapeDtypeStruct((D, P), g.dtype), compiler_params=_cparams("parallel"))(g)


IN_ROWS = 432


def _scatter_w_in_grad(gp_t, tc=256):
    def body(g_ref, o_ref):
        g = g_ref[...]
        pieces, off = [], 0
        for a, b, z in _PAD_SEGS:
            pieces.append((a, g[off:off + (b - a)]))
            off += (b - a) + z
        spill = -(-(IN_SHARD * (N_DEV - 1) + IN_ROWS - IN_COLS) // 8) * 8
        nat = jnp.concatenate([piece for _, piece in sorted(pieces, key=lambda ap: ap[0])]
                              + [jnp.zeros((spill, tc), F32)], axis=0)
        for d in range(N_DEV):
            o_ref[d] = nat[IN_SHARD * d:IN_SHARD * d + IN_ROWS].astype(BF16)

    return pl.pallas_call(
        body, name="scatter_w_in_grad", grid=(D // tc,),
        in_specs=[pl.BlockSpec((P, tc), lambda i: (0, i))],
        out_specs=pl.BlockSpec((N_DEV, IN_ROWS, tc), lambda i: (0, 0, i)),
        out_shape=jax.ShapeDtypeStruct((N_DEV, IN_ROWS, D), BF16),
        compiler_params=_cparams("parallel"))(gp_t)


_A_BLK = 3


def _sgu_specs(l):
    return [_resident((DEPTH, G)), _resident((DEPTH, G)), _layer((NH, SGU_C, SGU_C), l), _layer((NH, SGU_C), l)]


def _sgu_fwd(p, ln_w, ln_b, ws, bs, l, tm=256):
    t = p.shape[0]

    def body(uv_ref, lw_ref, lb_ref, ws_ref, bs_ref, y_ref):
        ws_v = [ws_ref[h] for h in range(NH)]
        for c in range(tm // SGU_C):
            rows = pl.ds(c * SGU_C, SGU_C)
            y_ref[rows, :] = _sgu_chunk(uv_ref[rows, :], lw_ref[l:l + 1, :], lb_ref[l:l + 1, :], ws_v,
                                        bs_ref[...]).astype(BF16)

    return pl.pallas_call(
        body, name="sgu_fwd", grid=(t // tm,),
        in_specs=[_row(tm, 2 * G, _A_BLK)] + _sgu_specs(l), out_specs=_row(tm, G),
        out_shape=jax.ShapeDtypeStruct((t, G), BF16),
        compiler_params=_cparams("parallel"))(p, ln_w, ln_b, ws, bs)


def _sgu_bwd(p, dmix, ln_w, ln_b, ws, bs, l, tm=256):
    t = p.shape[0]

    def body(uv_ref, dy_ref, lw_ref, lb_ref, ws_ref, bs_ref, duv_ref, dlw_ref, dlb_ref, dws_ref, dbs_ref):
        @pl.when(pl.program_id(0) == 0)
        def _():
            for r in (dlw_ref, dlb_ref, dws_ref, dbs_ref):
                r[...] = jnp.zeros_like(r)
        ws_v = [ws_ref[h] for h in range(NH)]
        for c in range(tm // SGU_C):
            rows = pl.ds(c * SGU_C, SGU_C)
            _, vjp = jax.vjp(_sgu_chunk, uv_ref[rows, :], lw_ref[l:l + 1, :], lb_ref[l:l + 1, :], ws_v, bs_ref[...])
            duv, dlw, dlb, dws, dbs = vjp(dy_ref[rows, :])
            duv_ref[rows, :] = duv.astype(BF16)
            dlw_ref[...] += dlw
            dlb_ref[...] += dlb
            dbs_ref[...] += dbs
            for h in range(NH):
                dws_ref[h] += dws[h]

    return pl.pallas_call(
        body, name="sgu_bwd", grid=(t // tm,),
        in_specs=[_row(tm, 2 * G, _A_BLK), _row(tm, G, 0)] + _sgu_specs(l),
        out_specs=[_row(tm, 2 * G), _acc((1, G)), _acc((1, G)), _acc((NH, SGU_C, SGU_C)), _acc((NH, SGU_C))],
        out_shape=[jax.ShapeDtypeStruct((t, 2 * G), BF16), jax.ShapeDtypeStruct((1, G), F32),
                   jax.ShapeDtypeStruct((1, G), F32), jax.ShapeDtypeStruct((NH, SGU_C, SGU_C), F32),
                   jax.ShapeDtypeStruct((NH, SGU_C), F32)],
        compiler_params=_cparams("arbitrary"))(p, dmix, ln_w, ln_b, ws, bs)


HALO = 8


def _prev_halo(tm, width, col):
    return pl.BlockSpec((HALO, width), lambda i: (jnp.maximum(i * (tm // HALO) - 1, 0), col))


def _next_halo(tm, width, col, t):
    return pl.BlockSpec((HALO, width), lambda i: (jnp.minimum((i + 1) * (tm // HALO), t // HALO - 1), col))


def _with_prev(halo_ref, x_ref):
    halo = jnp.where(pl.program_id(0) == 0, 0.0, halo_ref[...])
    return jnp.concatenate([halo, x_ref[...]], axis=0)


def _with_next(x_ref, halo_ref):
    halo = jnp.where(pl.program_id(0) == pl.num_programs(0) - 1, 0.0, halo_ref[...])
    return jnp.concatenate([x_ref[...], halo], axis=0)


def _delay(ext, j):
    return ext if j == 0 else pltpu.roll(ext, j, axis=0)


def _advance(ext, j):
    return ext if j == 0 else pltpu.roll(ext, ext.shape[0] - j, axis=0)


def _causal_conv(ext, w, tm):
    kw = w.shape[0]
    out = None
    for k in range(kw):
        term = _delay(ext, kw - 1 - k)[HALO:HALO + tm] * w[k:k + 1, :]
        out = term if out is None else out + term
    return out


def _causal_conv_t(ext, w, tm):
    kw = w.shape[0]
    out = None
    for k in range(kw):
        term = _advance(ext, kw - 1 - k)[0:tm] * w[k:k + 1, :]
        out = term if out is None else out + term
    return out


def _conv_wgrad(ext, dy, kw, tm):
    return jnp.concatenate(
        [jnp.sum(_delay(ext, kw - 1 - k)[HALO:HALO + tm] * dy, axis=0, keepdims=True) for k in range(kw)], axis=0)


_B_GB, _B_GC, _B_H = 8, 9, 10
_C_QKV, _D_QKV = 0, 1
_C_Z, _D_Z = 11, 12
_C_AB, _D_GLR = 26, 27


def _sconv_fwd(p, w, l, tm=512):
    t = p.shape[0]

    def body(gb_ref, gc_ref, h_ref, gc_halo, h_halo, w_ref, y_ref):
        s_ext = _with_prev(gc_halo, gc_ref) * _with_prev(h_halo, h_ref)
        y_ref[...] = (gb_ref[...] * _causal_conv(s_ext, w_ref[...], tm)).astype(BF16)

    return pl.pallas_call(
        body, name="sconv_fwd", grid=(t // tm,),
        in_specs=[_row(tm, G, _B_GB), _row(tm, G, _B_GC), _row(tm, G, _B_H), _prev_halo(tm, G, _B_GC),
                  _prev_halo(tm, G, _B_H), _layer((3, G), l)],
        out_specs=_row(tm, G), out_shape=jax.ShapeDtypeStruct((t, G), BF16),
        compiler_params=_cparams("parallel"))(p, p, p, p, p, w)


def _sconv_bwd(p, dmix, w, l, tm=512):
    t = p.shape[0]

    def body(gb_ref, gc_ref, h_ref, gc_halo, h_halo, gb_next, dy_ref, dy_next, w_ref, dp_ref, dw_ref):
        @pl.when(pl.program_id(0) == 0)
        def _():
            dw_ref[...] = jnp.zeros_like(dw_ref)
        w_v = w_ref[...]
        s_ext = _with_prev(gc_halo, gc_ref) * _with_prev(h_halo, h_ref)
        dout = dy_ref[...]
        d_gb = dout * _causal_conv(s_ext, w_v, tm)
        dconv_ext = _with_next(dy_ref, dy_next) * _with_next(gb_ref, gb_next)
        ds = _causal_conv_t(dconv_ext, w_v, tm)
        dw_ref[...] += _conv_wgrad(s_ext, dconv_ext[0:tm], 3, tm)
        dp_ref[...] = jnp.concatenate([d_gb, ds * h_ref[...], ds * gc_ref[...]], axis=1).astype(BF16)

    return pl.pallas_call(
        body, name="sconv_bwd", grid=(t // tm,),
        in_specs=[_row(tm, G, _B_GB), _row(tm, G, _B_GC), _row(tm, G, _B_H), _prev_halo(tm, G, _B_GC),
                  _prev_halo(tm, G, _B_H), _next_halo(tm, G, _B_GB, t), _row(tm, G, 1), _next_halo(tm, G, 1, t),
                  _layer((3, G), l)],
        out_specs=[_row(tm, 3 * G), _acc((3, G))],
        out_shape=[jax.ShapeDtypeStruct((t, 3 * G), BF16), jax.ShapeDtypeStruct((3, G), F32)],
        compiler_params=_cparams("arbitrary"))(p, p, p, p, p, p, dmix, dmix, w)


def _qkv_conv_fwd(p, w, l, tm=512):
    t = p.shape[0]

    def body(x_ref, x_halo, w_ref, y_ref):
        c = _causal_conv(_with_prev(x_halo, x_ref), w_ref[...], tm)
        y_ref[...] = c * _sigmoid(c)

    return pl.pallas_call(
        body, name="qkv_conv_fwd", grid=(t // tm,),
        in_specs=[_row(tm, 3 * G, _C_QKV), _prev_halo(tm, 3 * G, _C_QKV), _layer((4, 3 * G), l)],
        out_specs=_row(tm, 3 * G), out_shape=jax.ShapeDtypeStruct((t, 3 * G), F32),
        compiler_params=_cparams("parallel"))(p, p, w)


def _qkv_conv_bwd(p, dy, w, l, tm=512):
    t = p.shape[0]

    def body(x_ref, x_halo, x_next, dy_ref, dy_next, w_ref, dx_ref, dw_ref):
        @pl.when(pl.program_id(0) == 0)
        def _():
            dw_ref[...] = jnp.zeros_like(dw_ref)
        w_v = w_ref[...]
        x_all = jnp.concatenate([_with_prev(x_halo, x_ref), jnp.where(
            pl.program_id(0) == pl.num_programs(0) - 1, 0.0, x_next[...])], axis=0)
        c = _causal_conv(x_all, w_v, tm + HALO)
        s = _sigmoid(c)
        dc_ext = _with_next(dy_ref, dy_next) * (s * (1.0 + c * (1.0 - s)))
        dx_ref[...] = _causal_conv_t(dc_ext, w_v, tm).astype(BF16)
        dw_ref[...] += _conv_wgrad(x_all[0:HALO + tm], dc_ext[0:tm], 4, tm)

    return pl.pallas_call(
        body, name="qkv_conv_bwd", grid=(t // tm,),
        in_specs=[_row(tm, 3 * G, _C_QKV), _prev_halo(tm, 3 * G, _C_QKV), _next_halo(tm, 3 * G, _C_QKV, t),
                  _row(tm, 3 * G), _next_halo(tm, 3 * G, 0, t), _layer((4, 3 * G), l)],
        out_specs=[_row(tm, 3 * G), _acc((4, 3 * G))],
        out_shape=[jax.ShapeDtypeStruct((t, 3 * G), BF16), jax.ShapeDtypeStruct((4, 3 * G), F32)],
        compiler_params=_cparams("arbitrary"))(p, p, p, dy, dy, w)


_STATE = pl.BlockSpec((1, NH, HD, HD), lambda i: (i, 0, 0, 0))


def _dn_specs():
    return [_resident((DEPTH, NH)), _resident((DEPTH, NH)), _resident((DEPTH, HD))]


def _dn_fwd(qkv, p, params, l, cps=4, rider=None, forward_at=1.0):
    t = qkv.shape[0]
    tm = DN_C * cps
    nb = cps * NH

    def body(*refs):
        (qkv_ref, z_ref, ab_ref, alog_ref, dt_ref, nw_ref, y_ref, st_ref, inv_ref, state), ride = _split_refs(
            refs, 6, 3, 1, rider)
        _ride_begin(rider, ride, (t // tm,))

        @pl.when(pl.program_id(0) == 0)
        def _():
            state[...] = jnp.zeros_like(state)
        st_ref[0] = state[...]
        y, new, inv = _dn_tile(qkv_ref[:, 0:G], qkv_ref[:, G:2 * G], qkv_ref[:, 2 * G:3 * G], ab_ref[...], z_ref[...],
                               state[...], None, alog_ref[l:l + 1, :], dt_ref[l:l + 1, :], nw_ref[l:l + 1, :])
        y_ref[...] = y.astype(BF16)
        inv_ref[...] = inv
        state[...] = new
        _ride_end(rider, ride, (t // tm,), forward_at)

    specs, operands = _host(
        rider, [_row(tm, 3 * G), _row(tm, G, _C_Z), _row(tm, LANES, _C_AB)] + _dn_specs(),
        [_row(tm, G), _STATE, pl.BlockSpec((nb, DN_C, DN_C), lambda i: (i, 0, 0))],
        [jax.ShapeDtypeStruct((t, G), BF16), jax.ShapeDtypeStruct((t // tm, NH, HD, HD), F32),
         jax.ShapeDtypeStruct((t // DN_C * NH, DN_C, DN_C), F32)],
        [pltpu.VMEM((NH, HD, HD), F32)], [qkv, p, p, *params])
    return pl.pallas_call(body, name="dn_fwd", grid=(t // tm,), compiler_params=_cparams("arbitrary"), **specs)(*operands)


def _dn_bwd(qkv, p, states, inv, dmix, params, l, cps=4, rider=None):
    t = qkv.shape[0]
    tm = DN_C * cps
    n = t // tm
    nb = cps * NH

    def body(*refs):
        (qkv_ref, z_ref, ab_ref, st_ref, inv_ref, dy_ref, alog_ref, dt_ref, nw_ref,
         dqkv_ref, dz_ref, dab_ref, dalog_ref, ddt_ref, dnw_ref, dstate), ride = _split_refs(refs, 9, 6, 1, rider)
        _ride_begin(rider, ride, (n,))
        dprm_refs = (dalog_ref, ddt_ref, dnw_ref)

        @pl.when(pl.program_id(0) == 0)
        def _():
            dstate[...] = jnp.zeros_like(dstate)
            for r in dprm_refs:
                r[...] = jnp.zeros_like(r)
        inv_v = inv_ref[...]
        tile = lambda q, k, v, ab, z, st, alog, dt, nw: _dn_tile(q, k, v, ab, z, st, inv_v, alog, dt, nw)[:2]
        _, vjp = jax.vjp(tile, qkv_ref[:, 0:G], qkv_ref[:, G:2 * G], qkv_ref[:, 2 * G:3 * G], ab_ref[...], z_ref[...],
                         st_ref[0], alog_ref[l:l + 1, :], dt_ref[l:l + 1, :], nw_ref[l:l + 1, :])
        dq, dk, dv, dab, dz, dst, *dprm = vjp((dy_ref[...], dstate[...]))
        dqkv_ref[...] = jnp.concatenate([dq, dk, dv], axis=1)
        dz_ref[...] = dz.astype(BF16)
        dab_ref[...] = dab.astype(BF16)
        dstate[...] = dst
        for r, g in zip(dprm_refs, dprm):
            r[...] += g
        _ride_end(rider, ride, (n,))

    rev = lambda w, blk: pl.BlockSpec((tm, w), lambda i: (n - 1 - i, blk))
    specs, operands = _host(
        rider, [rev(3 * G, 0), rev(G, _C_Z), rev(LANES, _C_AB),
                pl.BlockSpec((1, NH, HD, HD), lambda i: (n - 1 - i, 0, 0, 0)),
                pl.BlockSpec((nb, DN_C, DN_C), lambda i: (n - 1 - i, 0, 0)), rev(G, 2)] + _dn_specs(),
        [rev(3 * G, 0), rev(G, 0), rev(LANES, 0), _acc((1, NH)), _acc((1, NH)), _acc((1, HD))],
        [jax.ShapeDtypeStruct((t, 3 * G), F32), jax.ShapeDtypeStruct((t, G), BF16),
         jax.ShapeDtypeStruct((t, LANES), BF16), jax.ShapeDtypeStruct((1, NH), F32),
         jax.ShapeDtypeStruct((1, NH), F32), jax.ShapeDtypeStruct((1, HD), F32)],
        [pltpu.VMEM((NH, HD, HD), F32)], [qkv, p, p, states, inv, dmix, *params])
    return pl.pallas_call(body, name="dn_bwd", grid=(n,), compiler_params=_cparams("arbitrary"), **specs)(*operands)


def _gla_specs(l):
    return [_layer((16, G), l), _resident((DEPTH, G)), _resident((DEPTH, HD))]


def _gla_fwd(p, params, l, cps=4, rider=None):
    t = p.shape[0]
    tm = GLA_C * cps

    def body(*refs):
        (qkv_ref, z_ref, glr_ref, w2_ref, gb_ref, nw_ref, y_ref, st_ref, state), ride = _split_refs(refs, 6, 2, 1, rider)
        _ride_begin(rider, ride, (t // tm,))

        @pl.when(pl.program_id(0) == 0)
        def _():
            state[...] = jnp.zeros_like(state)
        st_ref[0] = state[...]
        y, new = _gla_tile(qkv_ref[:, 0:G], qkv_ref[:, G:2 * G], qkv_ref[:, 2 * G:3 * G], glr_ref[...], z_ref[...],
                           state[...], w2_ref[...], gb_ref[l:l + 1, :], nw_ref[l:l + 1, :])
        y_ref[...] = y.astype(BF16)
        state[...] = new
        _ride_end(rider, ride, (t // tm,))

    specs, operands = _host(
        rider, [_row(tm, 3 * G, _D_QKV), _row(tm, G, _D_Z), _row(tm, LANES, _D_GLR)] + _gla_specs(l),
        [_row(tm, G), _STATE],
        [jax.ShapeDtypeStruct((t, G), BF16), jax.ShapeDtypeStruct((t // tm, NH, HD, HD), F32)],
        [pltpu.VMEM((NH, HD, HD), F32)], [p, p, p, *params])
    return pl.pallas_call(body, name="gla_fwd", grid=(t // tm,), compiler_params=_cparams("arbitrary"), **specs)(*operands)


def _gla_bwd(p, states, dmix, params, l, cps=4):
    t = p.shape[0]
    tm = GLA_C * cps
    n = t // tm

    def body(qkv_ref, z_ref, glr_ref, st_ref, dy_ref, w2_ref, gb_ref, nw_ref,
             dqkv_ref, dz_ref, dglr_ref, dw2_ref, dgb_ref, dnw_ref, dstate):
        dprm_refs = (dw2_ref, dgb_ref, dnw_ref)

        @pl.when(pl.program_id(0) == 0)
        def _():
            dstate[...] = jnp.zeros_like(dstate)
            for r in dprm_refs:
                r[...] = jnp.zeros_like(r)
        _, vjp = jax.vjp(_gla_tile, qkv_ref[:, 0:G], qkv_ref[:, G:2 * G], qkv_ref[:, 2 * G:3 * G], glr_ref[...],
                         z_ref[...], st_ref[0], w2_ref[...], gb_ref[l:l + 1, :], nw_ref[l:l + 1, :])
        dq, dk, dv, dglr, dz, dst, *dprm = vjp((dy_ref[...], dstate[...]))
        dqkv_ref[...] = jnp.concatenate([dq, dk, dv], axis=1).astype(BF16)
        dz_ref[...] = dz.astype(BF16)
        dglr_ref[...] = dglr.astype(BF16)
        dstate[...] = dst
        for r, g in zip(dprm_refs, dprm):
            r[...] += g

    rev = lambda w, blk: pl.BlockSpec((tm, w), lambda i: (n - 1 - i, blk))
    return pl.pallas_call(
        body, name="gla_bwd", grid=(n,),
        in_specs=[rev(3 * G, _D_QKV), rev(G, _D_Z), rev(LANES, _D_GLR),
                  pl.BlockSpec((1, NH, HD, HD), lambda i: (n - 1 - i, 0, 0, 0)), rev(G, 3)] + _gla_specs(l),
        out_specs=[rev(3 * G, 0), rev(G, 0), rev(LANES, 0), _acc((16, G)), _acc((1, G)), _acc((1, HD))],
        out_shape=[jax.ShapeDtypeStruct((t, 3 * G), BF16), jax.ShapeDtypeStruct((t, G), BF16),
                   jax.ShapeDtypeStruct((t, LANES), BF16), jax.ShapeDtypeStruct((16, G), F32),
                   jax.ShapeDtypeStruct((1, G), F32), jax.ShapeDtypeStruct((1, HD), F32)],
        scratch_shapes=[pltpu.VMEM((NH, HD, HD), F32)],
        compiler_params=_cparams("arbitrary"))(p, p, p, states, dmix, *params)


def _adamw_math(w, g, m, v):
    m = ADAM_B1 * m + (1.0 - ADAM_B1) * g
    v = ADAM_B2 * v + (1.0 - ADAM_B2) * (g * g)
    m_hat = m / (1.0 - ADAM_B1 ** ADAM_STEP)
    v_hat = v / (1.0 - ADAM_B2 ** ADAM_STEP)
    return -ADAM_LR * (m_hat / (jnp.sqrt(v_hat) + ADAM_EPS) + ADAM_WD * w), m, v


def _adamw_large(parts, w, m, v, name, tr):
    _, r, c = w.shape

    def body(p0_ref, p1_ref, w_ref, m_ref, v_ref, g_ref, d_ref, nm_ref, nv_ref):
        def total(p_ref):
            g = p_ref[0].astype(F32)
            for k in range(1, N_DEV):
                g = g + p_ref[k].astype(F32)
            return g

        g = jnp.where(pl.program_id(0) == 0, total(p0_ref), total(p1_ref))
        g_ref[...] = g
        d_ref[...], nm_ref[...], nv_ref[...] = _adamw_math(w_ref[...], g, m_ref[...], v_ref[...])

    blk = pl.BlockSpec((None, tr, c), lambda l, i: (l, i, 0))
    part = pl.BlockSpec((N_DEV, tr, c), lambda l, i: (0, i, 0))
    return pl.pallas_call(
        body, name=name, grid=(DEPTH, r // tr), in_specs=[part, part, blk, blk, blk],
        out_specs=[blk] * 4, out_shape=[jax.ShapeDtypeStruct(w.shape, F32)] * 4,
        compiler_params=_cparams("parallel", "parallel"))(*parts, w, m, v)


def _adamw_w_in(parts, w, m, v, tc=256):
    def body(p0_ref, p1_ref, w_ref, m_ref, v_ref, g_ref, d_ref, nm_ref, nv_ref):
        for l, p_ref in enumerate((p0_ref, p1_ref)):
            g = p_ref[0, 0:IN_SHARD, :].astype(F32)
            for k in range(1, N_DEV):
                g = g + p_ref[k, 0:IN_SHARD, :].astype(F32)
            g_ref[:, l, :] = g
            d_ref[:, l, :], nm_ref[:, l, :], nv_ref[:, l, :] = _adamw_math(w_ref[:, l, :], g, m_ref[:, l, :], v_ref[:, l, :])

    blk = pl.BlockSpec((IN_SHARD, DEPTH, tc), lambda i: (0, 0, i))
    part = pl.BlockSpec((N_DEV, IN_ROWS, tc), lambda i: (0, 0, i))
    return pl.pallas_call(
        body, name="adamw_w_in", grid=(D // tc,), in_specs=[part, part, blk, blk, blk], out_specs=[blk] * 4,
        out_shape=[jax.ShapeDtypeStruct(w.shape, F32)] * 4, compiler_params=_cparams("parallel"))(*parts, w, m, v)


_SLAB_AT = {
    "sgu_w_spatial": (0, 0, SGU_C, LANES),
    "sgu_b_spatial": (128, 0, NH, SGU_C),
    "norm1_w": (132, 0, 1, D),
    "norm2_w": (133, 0, 1, D),
    "sgu_ln_w": (134, 0, 1, G),
    "sgu_ln_b": (134, 256, 1, G),
    "gla_gate_bias": (134, 512, 1, G),
    "dn_norm_w": (134, 768, 1, HD),
    "gla_norm_w": (134, 896, 1, HD),
    "dn_a_log": (135, 0, 1, NH),
    "dn_dt_bias": (135, 128, 1, NH),
    "gla_w_gate2": (136, 0, 16, G),
    "dn_conv_w": (136, 256, 4, 3 * G),
    "sc_conv_w": (140, 256, 3, G),
}
_LAYER_ROWS = 152
_FINAL_ROW = DEPTH * _LAYER_ROWS
_SLAB_ROWS, _SLAB_COLS = _FINAL_ROW + 8, 1024
_SLAB_ORDER = tuple(_SLAB_AT)
_SHARDED_SMALL = ("sc_conv_w", "dn_conv_w", "gla_w_gate2")
_REPLICATED = tuple(k for k in _SLAB_ORDER if k not in _SHARDED_SMALL)


def _region(name, l, h=0):
    r0, c0, nr, nc = _SLAB_AT[name]
    return slice(_LAYER_ROWS * l + r0, _LAYER_ROWS * l + r0 + nr), slice(c0 + nc * h, c0 + nc * (h + 1))


def _pack_small(layer_grads, d_final):
    def body(*refs):
        slab = refs[-1]
        slab[...] = jnp.zeros_like(slab)
        it = iter(refs[:-1])
        for l in range(DEPTH):
            for name in _SLAB_ORDER:
                ref = next(it)
                if name == "sgu_w_spatial":
                    for h in range(NH):
                        slab[_region(name, l, h)] = ref[h]
                else:
                    slab[_region(name, l)] = ref[...]
        slab[_FINAL_ROW:_FINAL_ROW + 1, :] = next(it)[...]

    flat = [layer_grads[l][name] for l in range(DEPTH) for name in _SLAB_ORDER] + [d_final]
    return pl.pallas_call(body, name="pack_small_grads", out_shape=jax.ShapeDtypeStruct((_SLAB_ROWS, _SLAB_COLS), F32),
                          compiler_params=_cparams())(*flat)


def _adamw_small(parts, w, m, v):
    names = _REPLICATED + ("final_norm_w",)
    n_in = len(names)

    def body(*refs):
        def total(rows, cols):
            g = refs[0][0, rows, cols]
            for k in range(1, N_DEV):
                g = g + refs[0][k, rows, cols]
            return g

        w_refs = dict(zip(names, refs[1:1 + n_in]))
        m_refs = dict(zip(names, refs[1 + n_in:1 + 2 * n_in]))
        v_refs = dict(zip(names, refs[1 + 2 * n_in:1 + 3 * n_in]))
        outs = refs[1 + 3 * n_in:]
        out_refs = [dict(zip(names, outs[j * n_in:(j + 1) * n_in])) for j in range(4)]
        full_refs = dict(zip(_SHARDED_SMALL, outs[4 * n_in:]))

        def update(name, idx, g):
            res = (g,) + _adamw_math(w_refs[name][idx], g, m_refs[name][idx], v_refs[name][idx])
            for o, val in zip(out_refs, res):
                o[name][idx] = val

        for l in range(DEPTH):
            for name in _REPLICATED:
                if name == "sgu_w_spatial":
                    for h in range(NH):
                        update(name, (l, h), total(*_region(name, l, h)))
                elif name == "sgu_b_spatial":
                    update(name, (l,), total(*_region(name, l)))
                else:
                    update(name, (slice(l, l + 1), slice(None)), total(*_region(name, l)))
            for name in _SHARDED_SMALL:
                full_refs[name][l] = total(*_region(name, l))
        update("final_norm_w", (slice(None), slice(None)), total(slice(_FINAL_ROW, _FINAL_ROW + 1), slice(None)))

    shapes = [jax.ShapeDtypeStruct(w[k].shape, F32) for k in names]
    full_shapes = [jax.ShapeDtypeStruct((DEPTH,) + _SLAB_AT[k][2:], F32) for k in _SHARDED_SMALL]
    outs = pl.pallas_call(body, name="adamw_small", out_shape=shapes * 4 + full_shapes, compiler_params=_cparams())(
        parts, *[w[k] for k in names], *[m[k] for k in names], *[v[k] for k in names])
    result = [dict(zip(names, outs[j * n_in:(j + 1) * n_in])) for j in range(4)]
    return result, dict(zip(_SHARDED_SMALL, outs[4 * n_in:]))


def _adamw_shards(g, w, m, v):
    names = _SHARDED_SMALL
    n = len(names)

    def body(*refs):
        for j in range(n):
            g_v = refs[j][...]
            res = _adamw_math(refs[n + j][...], g_v, refs[2 * n + j][...], refs[3 * n + j][...])
            for o, val in zip(refs[4 * n + j::n], res):
                o[...] = val

    shapes = [jax.ShapeDtypeStruct(w[k].shape, F32) for k in names]
    outs = pl.pallas_call(body, name="adamw_small_shards", out_shape=shapes * 3, compiler_params=_cparams())(
        *[g[k] for k in names], *[w[k] for k in names], *[m[k] for k in names], *[v[k] for k in names])
    return [dict(zip(names, outs[j * n:(j + 1) * n])) for j in range(3)]


_ANY = pl.BlockSpec(memory_space=pl.ANY)


def _place():
    return lax.axis_index("x"), lax.axis_index("y"), lax.axis_index("c")


def _sems(n):
    return [pltpu.SemaphoreType.DMA((n, 7)), pltpu.SemaphoreType.DMA((n, 7)), pltpu.SemaphoreType.DMA((n,))]


class _Gather:
    def __init__(self, blocks, second):
        self.inputs, self.second = list(blocks), list(second)
        self.out_shape = [jax.ShapeDtypeStruct((a.shape[0], N_DEV) + a.shape[1:] if s else (N_DEV,) + a.shape, a.dtype)
                          for a, s in zip(blocks, second)]
        self.scratch = _sems(len(blocks))

    def _copies(self, refs):
        x_refs, out_refs, (send_sems, recv_sems, local_sems) = refs
        n = len(x_refs)
        x, y, c = _place()
        me, sibling = (x, y, c), (x, y, 1 - c)
        chips = [(1 - x, y), (x, 1 - y), (1 - x, 1 - y)]

        def slot(j, px, py, pc):
            idx = 4 * px + 2 * py + pc
            return out_refs[j].at[:, idx] if self.second[j] else out_refs[j].at[idx]

        def copies(k, block, to, own=False):
            return [pltpu.make_async_remote_copy(
                src_ref=x_refs[j] if own else slot(j, *block), dst_ref=slot(j, *block),
                send_sem=send_sems.at[j, k], recv_sem=recv_sems.at[j, k], device_id=to,
                device_id_type=pl.DeviceIdType.MESH) for j in range(n)]

        mine = [pltpu.make_async_copy(x_refs[j], slot(j, *me), local_sems.at[j]) for j in range(n)]
        first = copies(0, me, sibling, own=True)
        for j, chip in enumerate(chips):
            first += copies(1 + j, me, (*chip, c), own=True)
        landed = [copies(1 + j, (*chip, c), me) for j, chip in enumerate(chips)]
        onward = [copies(4 + j, (*chip, c), sibling) for j, chip in enumerate(chips)]
        from_sibling = copies(0, sibling, me)
        for j, chip in enumerate(chips):
            from_sibling += copies(4 + j, (*chip, 1 - c), me)
        return mine, first, landed, onward, from_sibling

    def start(self, refs):
        mine, first, _, _, _ = self._copies(refs)
        for cp in mine + first:
            cp.start()

    def forward(self, refs):
        _, _, landed, onward, _ = self._copies(refs)
        for arrived, passed in zip(landed, onward):
            for cp in arrived:
                cp.wait_recv()
            for cp in passed:
                cp.start()

    def finish(self, refs):
        mine, first, _, onward, from_sibling = self._copies(refs)
        for cp in from_sibling:
            cp.wait_recv()
        for cp in first + [cp for passed in onward for cp in passed]:
            cp.wait_send()
        for cp in mine:
            cp.wait()


class _Exchange:
    def __init__(self, sends):
        self.inputs = list(sends)
        self.out_shape = [jax.ShapeDtypeStruct(a.shape, a.dtype) for a in sends]
        self.scratch = _sems(len(sends))

    def _copies(self, refs):
        x_refs, out_refs, (send_sems, recv_sems, local_sems) = refs
        n = len(x_refs)
        x, y, c = _place()
        me = 4 * x + 2 * y + c
        sent, landing = [], []
        for k in range(1, N_DEV):
            px, py, pc = x ^ ((k >> 2) & 1), y ^ ((k >> 1) & 1), c ^ (k & 1)
            them = 4 * px + 2 * py + pc
            for j in range(n):
                for here, there, into in ((them, me, sent), (me, them, landing)):
                    into.append(pltpu.make_async_remote_copy(
                        src_ref=x_refs[j].at[here], dst_ref=out_refs[j].at[there], send_sem=send_sems.at[j, k - 1],
                        recv_sem=recv_sems.at[j, k - 1], device_id=(px, py, pc), device_id_type=pl.DeviceIdType.MESH))
        mine = [pltpu.make_async_copy(x_refs[j].at[me], out_refs[j].at[me], local_sems.at[j]) for j in range(n)]
        return mine, sent, landing

    def start(self, refs):
        mine, sent, _ = self._copies(refs)
        for cp in mine + sent:
            cp.start()

    def forward(self, refs):
        pass

    def finish(self, refs):
        mine, sent, landing = self._copies(refs)
        for cp in landing:
            cp.wait_recv()
        for cp in sent:
            cp.wait_send()
        for cp in mine:
            cp.wait()


def _run(rider, name):
    n_in, n_out = len(rider.inputs), len(rider.out_shape)

    def body(*refs):
        parts = (refs[:n_in], refs[n_in:n_in + n_out], refs[n_in + n_out:])
        rider.start(parts)
        rider.forward(parts)
        rider.finish(parts)

    return pl.pallas_call(body, name=name, out_shape=rider.out_shape, in_specs=[_ANY] * n_in, out_specs=[_ANY] * n_out,
                          scratch_shapes=rider.scratch)(*rider.inputs)


def _split_refs(refs, n_in, n_out, n_scratch, rider):
    r_in = len(rider.inputs) if rider else 0
    r_out = len(rider.out_shape) if rider else 0
    a = n_in + r_in
    b = a + n_out + r_out
    own = refs[:n_in] + refs[a:a + n_out] + refs[b:b + n_scratch]
    return own, (refs[n_in:a], refs[a + n_out:b], refs[b + n_scratch:])


def _grid_step(grid):
    step, stride = 0, 1
    for axis in reversed(range(len(grid))):
        step = step + pl.program_id(axis) * stride
        stride *= grid[axis]
    return step


def _ride_begin(rider, ride_refs, grid):
    if rider is not None:
        pl.when(_grid_step(grid) == 0)(lambda: rider.start(ride_refs))


def _ride_end(rider, ride_refs, grid, forward_at=1.0):
    if rider is not None:
        steps = 1
        for n in grid:
            steps *= n
        step = _grid_step(grid)
        pl.when(step == min(steps - 1, int(steps * forward_at)))(lambda: rider.forward(ride_refs))
        pl.when(step == steps - 1)(lambda: rider.finish(ride_refs))


def _host(rider, in_specs, out_specs, out_shape, scratch, operands):
    if rider is None:
        return dict(in_specs=in_specs, out_specs=out_specs, out_shape=out_shape, scratch_shapes=scratch), operands
    return dict(in_specs=in_specs + [_ANY] * len(rider.inputs), out_specs=out_specs + [_ANY] * len(rider.out_shape),
                out_shape=out_shape + rider.out_shape, scratch_shapes=scratch + rider.scratch), operands + rider.inputs


_LATE = ("w_gate_up", "w_out", "w_down")


def _local_grads(x, target, w, late=None, exchange=False):
    w = dict(w)
    w_in = list(w["w_in"])
    dn_params = (w["dn_a_log"], w["dn_dt_bias"], w["dn_norm_w"])
    gla_params = (w["gla_w_gate2"], w["gla_gate_bias"], w["gla_norm_w"])
    sgu_params = (w["sgu_ln_w"], w["sgu_ln_b"], w["sgu_w_spatial"], w["sgu_b_spatial"])
    saved = []
    for l in range(DEPTH):
        riding = l == 0 and late is not None
        h, p, *got = _in_proj(x, w["norm1_w"], w_in[l], l, rider=_Gather([late["w_down"]], [True]) if riding else None)
        if riding:
            w["w_down"] = got[0].reshape(DEPTH, FF, D)
        ya = _sgu_fwd(p, *sgu_params, l)
        yb = _sconv_fwd(p, w["sc_conv_w"], l)
        qkv_c = _qkv_conv_fwd(p, w["dn_conv_w"], l)
        yc, st_c, inv_c, *got = _dn_fwd(
            qkv_c, p, dn_params, l, forward_at=0.85,
            rider=_Gather([late["w_gate_up"], late["w_out"]], [False, True]) if riding else None)
        if riding:
            w.update(w_gate_up=got[0], w_out=got[1].reshape(DEPTH, D, D))
        yd, st_d, *got = _gla_fwd(p, gla_params, l, rider=_Gather([late["w_in"]], [False]) if riding else None)
        if riding:
            w_in[1] = _relayout_w_in(got[0])
        mix, x1, h2, gu, act, x2 = _out_mlp(x, (ya, yb, yc, yd), w["w_out"], w["norm2_w"], w["w_gate_up"], w["w_down"], l)
        saved.append(dict(x=x, h=h, p=p, qkv_c=qkv_c, st_c=st_c, inv_c=inv_c, st_d=st_d, mix=mix, x1=x1, h2=h2, gu=gu,
                          act=act))
        x = x2
    loss, d_final, dx = _loss_head(x, w["final_norm_w"], target)
    large = {k: [None] * DEPTH for k in ("w_in", "w_out", "w_gate_up", "w_down")}
    small = [None] * DEPTH
    for l in reversed(range(DEPTH)):
        s = saved[l]
        p = s["p"]
        g = {}
        riding = exchange and l == 0

        def exchanging(pairs):
            return _Exchange([large[k][j] for k, j in pairs]) if riding else None

        def arrive(pairs, arrived):
            for (k, j), a in zip(pairs, arrived):
                large[k][j] = a

        pairs = (("w_gate_up", 1), ("w_down", 1))
        dgu, dx1, dmix, g["norm2_w"], *arrived = _mlp_out_bwd(
            dx, s["x1"], s["gu"], w["norm2_w"], w["w_down"], w["w_gate_up"], w["w_out"], l, rider=exchanging(pairs))
        arrive(pairs, arrived)
        pairs = (("w_in", 1), ("w_out", 1))
        d_gu, *arrived = _wgrad_rows(dgu, s["h2"], "wgrad_gate_up", rider=exchanging(pairs)) if riding else (
            _wgrad_rows(dgu, s["h2"], "wgrad_gate_up"),)
        arrive(pairs, arrived)
        large["w_gate_up"][l] = d_gu.reshape(N_DEV, GU_SHARD, D)
        large["w_down"][l] = _wgrad_rows(s["act"], dx, "wgrad_down").reshape(N_DEV, FF // N_DEV, D)
        large["w_out"][l] = _matmul_tn(s["mix"], dx1, "wgrad_out", BF16).reshape(N_DEV, D // N_DEV, D)
        d_a, g["sgu_ln_w"], g["sgu_ln_b"], g["sgu_w_spatial"], g["sgu_b_spatial"] = _sgu_bwd(p, dmix, *sgu_params, l)
        d_b, g["sc_conv_w"] = _sconv_bwd(p, dmix, w["sc_conv_w"], l)
        pairs = tuple((k, 0) for k in _LATE)
        dqkv_c, dz_c, dab, g["dn_a_log"], g["dn_dt_bias"], g["dn_norm_w"], *arrived = _dn_bwd(
            s["qkv_c"], p, s["st_c"], s["inv_c"], dmix, dn_params, l, rider=exchanging(pairs))
        arrive(pairs, arrived)
        d_c, g["dn_conv_w"] = _qkv_conv_bwd(p, dqkv_c, w["dn_conv_w"], l)
        d_d, dz_d, dglr, g["gla_w_gate2"], g["gla_gate_bias"], g["gla_norm_w"] = _gla_bwd(p, s["st_d"], dmix, gla_params, l)
        dp, dx, g["norm1_w"] = _in_proj_bwd((d_c, d_d, d_a, d_b, dz_c, dz_d, dab, dglr), s["x"], dx1, w["norm1_w"],
                                            w_in[l], l)
        small[l] = g
        w_in_rows = dict(tt=WGRAD_TOKENS // 2, rows=P // 2, out_dtype=F32)
        if riding:
            d_w_in, small = _wgrad_rows(dp, s["h"], "wgrad_in", rider=_Gather([_pack_small(small, d_final)], [False]),
                                        **w_in_rows)
        else:
            d_w_in = _wgrad_rows(dp, s["h"], "wgrad_in", **w_in_rows)
        large["w_in"][l] = _scatter_w_in_grad(d_w_in)
    return loss[0, 0], dx, large, small, d_final


_ORDER = ("norm1_w", "w_in", "sgu_ln_w", "sgu_ln_b", "sgu_w_spatial", "sgu_b_spatial", "sc_conv_w", "dn_conv_w",
          "dn_a_log", "dn_dt_bias", "dn_norm_w", "gla_w_gate2", "gla_gate_bias", "gla_norm_w", "w_out", "norm2_w",
          "w_gate_up", "w_down", "final_norm_w")
_LARGE = ("w_in", "w_gate_up", "w_out", "w_down")
_LARGE_TILE = {"w_gate_up": 352, "w_out": 128, "w_down": 352}


def _gather_cols(g):
    return jnp.transpose(g, (1, 2, 0, 3)).reshape(g.shape[1], g.shape[2], N_DEV * g.shape[3])


def kernel(x, norm1_w, w_in, sgu_ln_w, sgu_ln_b, sgu_w_spatial, sgu_b_spatial, sc_conv_w, dn_conv_w, dn_a_log, dn_dt_bias, dn_norm_w, gla_w_gate2, gla_gate_bias, gla_norm_w, w_out, norm2_w, w_gate_up, w_down, final_norm_w, loss_target, m_norm1_w, m_w_in, m_sgu_ln_w, m_sgu_ln_b, m_sgu_w_spatial, m_sgu_b_spatial, m_sc_conv_w, m_dn_conv_w, m_dn_a_log, m_dn_dt_bias, m_dn_norm_w, m_gla_w_gate2, m_gla_gate_bias, m_gla_norm_w, m_w_out, m_norm2_w, m_w_gate_up, m_w_down, m_final_norm_w, v_norm1_w, v_w_in, v_sgu_ln_w, v_sgu_ln_b, v_sgu_w_spatial, v_sgu_b_spatial, v_sc_conv_w, v_dn_conv_w, v_dn_a_log, v_dn_dt_bias, v_dn_norm_w, v_gla_w_gate2, v_gla_gate_bias, v_gla_norm_w, v_w_out, v_norm2_w, v_w_gate_up, v_w_down, v_final_norm_w):
    args = dict(locals())
    wts = {k: args[k] for k in _ORDER}
    mom = {k: args["m_" + k] for k in _ORDER}
    var = {k: args["v_" + k] for k in _ORDER}
    for d in (wts, mom, var):
        d["final_norm_w"] = d["final_norm_w"][None]
    me = 4 * lax.axis_index("x") + 2 * lax.axis_index("y") + lax.axis_index("c")
    for d in (wts, mom, var):
        d["w_gate_up"] = jnp.swapaxes(d["w_gate_up"], 1, 2)

    late = {k: wts[k].astype(BF16) for k in _LATE}
    w_in = wts["w_in"].astype(BF16)
    late["w_in"] = w_in[1]
    got = _run(_Gather([w_in[0]] + [wts[k] for k in _SHARDED_SMALL], [False] * 4), "gather_w_in")
    full = dict(wts)
    full["w_in"] = [_relayout_w_in(got[0]), None]
    for k, g in zip(_SHARDED_SMALL, got[1:]):
        full[k] = _gather_cols(g)

    loss, dx, large, small, d_final = _local_grads(x[0], loss_target[0], full, late=late, exchange=True)
    loss = lax.psum(loss, ("x", "y", "c"))

    large["w_in"][0], = _run(_Exchange([large["w_in"][0]]), "exchange_grads")
    result = {}
    for k in ("w_gate_up", "w_out", "w_down"):
        outs = _adamw_large(large[k], wts[k], mom[k], var[k], "adamw_" + k, _LARGE_TILE[k])
        for kind, a in zip(("grad", "delta", "new_m", "new_v"), outs):
            result[kind, k] = jnp.swapaxes(a, 1, 2) if k == "w_gate_up" else a
    outs = _adamw_w_in(large["w_in"], *[jnp.transpose(d["w_in"], (2, 0, 1)) for d in (wts, mom, var)])
    for kind, a in zip(("grad", "delta", "new_m", "new_v"), outs):
        result[kind, "w_in"] = jnp.transpose(a, (1, 2, 0))

    slabs = small
    rep = _REPLICATED + ("final_norm_w",)
    outs, summed = _adamw_small(slabs, {k: wts[k] for k in rep}, {k: mom[k] for k in rep}, {k: var[k] for k in rep})
    for kind, d in zip(("grad", "delta", "new_m", "new_v"), outs):
        for k, a in d.items():
            result[kind, k] = a[0] if k == "final_norm_w" else a
    own = {k: lax.dynamic_slice_in_dim(summed[k], me * wts[k].shape[-1], wts[k].shape[-1], axis=2) for k in _SHARDED_SMALL}
    outs = _adamw_shards(own, {k: wts[k] for k in _SHARDED_SMALL}, {k: mom[k] for k in _SHARDED_SMALL},
                         {k: var[k] for k in _SHARDED_SMALL})
    for kind, d in zip(("grad", "delta", "new_m", "new_v"), [own] + outs):
        for k, a in d.items():
            result[kind, k] = a

    return (loss, dx[None], *[result[kind, k] for kind in ("grad", "delta", "new_m", "new_v") for k in _ORDER])
```

```python
import functools

import jax
import jax.numpy as jnp
from jax import lax
from jax.experimental import pallas as pl
from jax.experimental.pallas import tpu as pltpu

F32, BF16 = jnp.float32, jnp.bfloat16
DEPTH = 2
D = 1024
G = 256
NH, HD = 4, 64
FF = 2816
IN_COLS = 3352
P = 3584
EPS = 1e-6
SGU_C, DN_C, GLA_C = 128, 64, 64
N_DEV = 8
IN_SHARD = IN_COLS // N_DEV
GU_SHARD = 2 * FF // N_DEV
LANES = 128
VMEM_LIMIT = 56 * 2 ** 20

_PAD_SEGS = ((1280, 2048, 0),
             (2312, 3080, 0),
             (0, 512, 0),
             (512, 1280, 0),
             (2056, 2312, 0),
             (3096, 3352, 0),
             (2048, 2056, 120),
             (3080, 3096, 112))

ADAM_LR, ADAM_B1, ADAM_B2, ADAM_EPS, ADAM_WD, ADAM_STEP = 0.001, 0.9, 0.999, 1e-08, 0.01, 10


def _cparams(*sem):
    return pltpu.CompilerParams(dimension_semantics=sem, vmem_limit_bytes=VMEM_LIMIT)


def _resident(shape):
    return pl.BlockSpec(shape, lambda *_: (0,) * len(shape), pipeline_mode=pl.Buffered(1))


def _layer(shape, l):
    return pl.BlockSpec((None,) + tuple(shape), lambda *_: (l,) + (0,) * len(shape), pipeline_mode=pl.Buffered(1))


def _acc(shape):
    return pl.BlockSpec(shape, lambda *_: (0,) * len(shape))


def _dg(a, b, ca, cb):
    lead = a.ndim - 2
    batch = ((0,), (0,)) if lead else ((), ())
    return lax.dot_general(a, b, (((ca + lead,), (cb + lead,)), batch), preferred_element_type=F32)


def _split(t):
    hi = t.astype(BF16)
    return hi, (t - hi.astype(F32)).astype(BF16)


def _dg3(a, b, ca, cb):
    (ah, al), (bh, bl) = a, b
    return _dg(ah, bh, ca, cb) + (_dg(ah, bl, ca, cb) + _dg(al, bh, ca, cb))


def _make_dot(accurate):
    def raw(a, b, form):
        ca = 0 if form == "tn" else 1
        cb = 1 if form == "nt" else 0
        if accurate:
            return _dg3(_split(a), _split(b), ca, cb)
        return _dg(a.astype(BF16), b.astype(BF16), ca, cb)

    @functools.partial(jax.custom_vjp, nondiff_argnums=(2,))
    def dot(a, b, form):
        return raw(a, b, form)

    def fwd(a, b, form):
        return raw(a, b, form), (a, b)

    def bwd(form, res, g):
        a, b = res
        if form == "nn":
            return raw(g, b, "nt"), raw(a, g, "tn")
        if form == "nt":
            return raw(g, b, "nn"), raw(g, a, "tn")
        return raw(b, g, "nt"), raw(a, g, "nn")

    dot.defvjp(fwd, bwd)
    return dot


_bdot = _make_dot(False)
_hdot = _make_dot(True)


def _inv_unit_lower(low):
    n = low.shape[-1]
    eye = (lax.broadcasted_iota(jnp.int32, (n, n), 0) == lax.broadcasted_iota(jnp.int32, (n, n), 1)).astype(F32)
    p = -low
    inv = eye + p
    ps = _split(p)
    k = 1
    while 2 * k < n:
        ps = _split(_dg3(ps, ps, 1, 0))
        inv = inv + _dg3(_split(inv), ps, 1, 0)
        k *= 2
    return inv


@jax.custom_vjp
def _unit_lower_solve(low, rhs, inv):
    return _dg3(_split(inv), _split(rhs), 1, 0)


def _uls_fwd(low, rhs, inv):
    sol = _dg3(_split(inv), _split(rhs), 1, 0)
    return sol, (inv, sol)


def _uls_bwd(res, g):
    inv, sol = res
    d_rhs = _dg3(_split(inv), _split(g), 0, 0)
    return -_dg3(_split(d_rhs), _split(sol), 1, 1), d_rhs, jnp.zeros_like(inv)


_unit_lower_solve.defvjp(_uls_fwd, _uls_bwd)


def _sigmoid(x):
    return 0.5 * jnp.tanh(0.5 * x) + 0.5


def _softplus(x):
    return jnp.maximum(x, 0.0) + jnp.log(1.0 + jnp.exp(-jnp.maximum(x, -x)))


def _gelu(x):
    return 0.5 * x * (1.0 + jnp.tanh(0.7978845608028654 * (x + 0.044715 * (x * x * x))))


def _tri(n):
    ri = lax.broadcasted_iota(jnp.int32, (n, n), 0)
    ci = lax.broadcasted_iota(jnp.int32, (n, n), 1)
    return ri, ci


def _stack(ts):
    return jnp.concatenate([t[None] for t in ts], axis=0)


@jax.custom_vjp
def _head_sums(x):
    ri, ci = _tri(x.shape[-1])
    shift = HD.bit_length() - 1
    ones = (jnp.right_shift(ri, shift) == jnp.right_shift(ci, shift)).astype(BF16)
    hi, lo = _split(x)
    return _dg(hi, ones, 1, 0) + _dg(lo, ones, 1, 0)


_head_sums.defvjp(lambda x: (_head_sums(x), None), lambda _, g: (_head_sums(g),))


def _chunk_mask_dot(x, c, running, transpose):
    ri, ci = _tri(x.shape[0])
    shift = c.bit_length() - 1
    mask = jnp.right_shift(ri, shift) == jnp.right_shift(ci, shift)
    if running:
        mask = mask & (ri >= ci)
    hi, lo = _split(x)
    m = mask.astype(BF16)
    ca = 0 if transpose else 1
    return _dg(m, hi, ca, 0) + _dg(m, lo, ca, 0)


@functools.partial(jax.custom_vjp, nondiff_argnums=(1, 2))
def _chunk_sums(x, c, running):
    return _chunk_mask_dot(x, c, running, False)


_chunk_sums.defvjp(lambda x, c, running: (_chunk_mask_dot(x, c, running, False), None),
                   lambda c, running, _, g: (_chunk_mask_dot(g, c, running, True),))


def _spread_onto(x, first, transpose):
    ri = lax.broadcasted_iota(jnp.int32, (LANES, G), 0)
    ci = lax.broadcasted_iota(jnp.int32, (LANES, G), 1)
    sel = (ri == first + jnp.right_shift(ci, HD.bit_length() - 1)).astype(BF16)
    hi, lo = _split(x)
    cb = 1 if transpose else 0
    return _dg(hi, sel, 1, cb) + _dg(lo, sel, 1, cb)


@functools.partial(jax.custom_vjp, nondiff_argnums=(1,))
def _spread(x, first):
    return _spread_onto(x, first, False)


_spread.defvjp(lambda x, first: (_spread_onto(x, first, False), None),
               lambda first, _, g: (_spread_onto(g, first, True),))


def _head_columns_dot(x, transpose):
    sel = (lax.broadcasted_iota(jnp.int32, (NH, G), 0)
           == jnp.right_shift(lax.broadcasted_iota(jnp.int32, (NH, G), 1), HD.bit_length() - 1)).astype(BF16)
    hi, lo = _split(x)
    if transpose:
        return _dg(sel, hi, 1, 1) + _dg(sel, lo, 1, 1)
    return _dg(hi, sel, 0, 0) + _dg(lo, sel, 0, 0)


@jax.custom_vjp
def _head_columns(x):
    return _head_columns_dot(x, False)


_head_columns.defvjp(lambda x: (_head_columns_dot(x, False), None), lambda _, g: (_head_columns_dot(g, True),))


def _sgu_chunk(uv, ln_w, ln_b, ws, bs):
    u = _gelu(uv[:, :G])
    v = _gelu(uv[:, G:])
    mu = jnp.mean(v, axis=-1, keepdims=True)
    vc = v - mu
    var = jnp.mean(vc * vc, axis=-1, keepdims=True)
    vn = vc * lax.rsqrt(var + EPS) * ln_w + ln_b
    ri, ci = _tri(SGU_C)
    mixed = [_bdot(jnp.where(ri >= ci, ws[h], 0.0), vn[:, HD * h:HD * (h + 1)], "nn") for h in range(NH)]
    return u * (jnp.concatenate(mixed, axis=1) + _head_columns(bs))


def _dn_tile(q, k, v, ab, z, state, inv, a_log, dt_bias, nw):
    c = DN_C
    tm = q.shape[0]
    cps = tm // c
    ri, ci = _tri(tm)
    shift = c.bit_length() - 1
    same = jnp.right_shift(ri, shift) == jnp.right_shift(ci, shift)
    g4 = -jnp.exp(a_log) * _softplus(ab[:, 0:NH] + dt_bias)
    gates = jnp.concatenate([g4, _sigmoid(ab[:, NH:2 * NH]), jnp.zeros((tm, LANES - 2 * NH), F32)], axis=1)
    g, beta = _spread(gates, 0), _spread(gates, NH)
    gc = _chunk_sums(g, c, True)
    gl = _chunk_sums(g, c, False)
    gr4 = _hdot(g4, (same & (ri <= ci)).astype(F32), "tn")
    pairs = [(slice(c * j, c * (j + 1)), h) for j in range(cps) for h in range(NH)]
    heads = lambda t: _stack([t[rows, HD * h:HD * (h + 1)] for rows, h in pairs])
    qn = q * lax.rsqrt(_head_sums(q * q) + EPS) * (HD ** -0.5)
    kn = k * lax.rsqrt(_head_sums(k * k) + EPS)
    eg = jnp.exp(gc)
    kb = kn * beta
    rhs = jnp.concatenate([heads(v * beta), heads(kb * eg)], axis=2)
    qg, kd = heads(qn * eg), heads(kn * jnp.exp(gl - gc))
    cd = _stack([jnp.exp(gl[c * j:c * j + 1, HD * h:HD * (h + 1)]) for j in range(cps) for h in range(NH)])
    qn, kn, kb, gc = heads(qn), heads(kn), heads(kb), heads(gc)
    gr = _stack([gr4[h:h + 1, rows] for rows, h in pairs])
    r2, c2 = _tri(c)
    decay = jnp.exp(jnp.where(r2 >= c2, gc - gr, -jnp.inf))
    low = jnp.where(r2 > c2, _bdot(kb, kn, "nt") * decay, 0.0)
    if inv is None:
        inv = _inv_unit_lower(low)
    sol = _unit_lower_solve(low, rhs, inv)
    u, w = sol[:, :, :HD], sol[:, :, HD:]
    attn = _bdot(qn, kn, "nt") * decay
    ys = []
    for j in range(cps):
        b = slice(NH * j, NH * (j + 1))
        v_new = u[b] - _bdot(w[b], state, "nn")
        o = _bdot(qg[b], state, "nn") + _bdot(attn[b], v_new, "nn")
        state = state * cd[b] + _bdot(kd[b], v_new, "tn")
        on = o * lax.rsqrt(jnp.mean(o * o, axis=-1, keepdims=True) + EPS) * nw
        ys.append(jnp.concatenate([on[h] for h in range(NH)], axis=1))
    return jnp.concatenate(ys, axis=0) * (z * _sigmoid(z)), state, inv


def _gla_tile(q, k, v, glr, z, state_t, w2, gate_bias, nw):
    c = GLA_C
    tm = q.shape[0]
    cps = tm // c
    w2_rows = jnp.concatenate([w2, jnp.zeros((LANES - w2.shape[0], G), F32)], axis=0)
    pre = _bdot(glr, w2_rows, "nn") + gate_bias
    log_a = (jnp.minimum(pre, 0.0) - jnp.log(1.0 + jnp.exp(-jnp.maximum(pre, -pre)))) * (1.0 / 16.0)
    gcum = _chunk_sums(log_a, c, True)
    row = lax.broadcasted_iota(jnp.int32, (c, G), 0)
    qs = q * (HD ** -0.5)
    qa, ka, qg, kl, dec = [], [], [], [], []
    for j in range(cps):
        rows = slice(c * j, c * (j + 1))
        gj = gcum[rows]
        g_mid = jnp.sum(jnp.where(row == c // 2, gj, 0.0), axis=0, keepdims=True)
        g_last = jnp.sum(log_a[rows], axis=0, keepdims=True)
        qa.append(qs[rows] * jnp.exp(gj - g_mid))
        ka.append(k[rows] * jnp.exp(g_mid - gj))
        qg.append(qs[rows] * jnp.exp(gj))
        kl.append(k[rows] * jnp.exp(g_last - gj))
        dec.append(jnp.exp(g_last))
    heads = lambda ts: _stack([t[:, HD * h:HD * (h + 1)] for t in ts for h in range(NH)])
    qa, ka, qg, kl, dec = heads(qa), heads(ka), heads(qg), heads(kl), heads(dec)
    vh = heads([v[c * j:c * (j + 1)] for j in range(cps)])
    r2, c2 = _tri(c)
    o_intra = _bdot(jnp.where(r2 >= c2, _bdot(qa, ka, "nt"), 0.0), vh, "nn")
    ys = []
    for j in range(cps):
        b = slice(NH * j, NH * (j + 1))
        o = o_intra[b] + _bdot(qg[b], state_t, "nt")
        state_t = state_t * dec[b] + _bdot(vh[b], kl[b], "tn")
        on = o * lax.rsqrt(jnp.mean(o * o, axis=-1, keepdims=True) + EPS) * nw
        ys.append(jnp.concatenate([on[h] for h in range(NH)], axis=1))
    return jnp.concatenate(ys, axis=0) * (z * _sigmoid(z)), state_t


def _rms(x, nw):
    r = lax.rsqrt(jnp.mean(x * x, axis=-1, keepdims=True) + EPS)
    xh = x * r
    return xh * nw, xh, r


def _rms_bwd(g, xh, r, nw):
    gw = g * nw
    return r * (gw - xh * jnp.mean(gw * xh, axis=-1, keepdims=True)), jnp.sum(g * xh, axis=0, keepdims=True)


def _row(tm, w, blk=0):
    return pl.BlockSpec((tm, w), lambda i: (i, blk))


def _in_proj(x, nw, wp, l, tm=512, rider=None):
    t = x.shape[0]

    def body(*refs):
        (x_ref, nw_ref, w_ref, h_ref, p_ref), ride = _split_refs(refs, 3, 2, 0, rider)
        _ride_begin(rider, ride, (t // tm,))
        h = _rms(x_ref[...], nw_ref[l:l + 1, :])[0].astype(BF16)
        h_ref[...] = h
        p_ref[...] = jnp.dot(h, w_ref[...], preferred_element_type=F32)
        _ride_end(rider, ride, (t // tm,))

    specs, operands = _host(
        rider, [_row(tm, D), _resident((DEPTH, D)), _resident((D, P))], [_row(tm, D), _row(tm, P)],
        [jax.ShapeDtypeStruct((t, D), BF16), jax.ShapeDtypeStruct((t, P), F32)], [], [x, nw, wp])
    return pl.pallas_call(body, name="in_proj", grid=(t // tm,), compiler_params=_cparams("arbitrary"),
                          **specs)(*operands)


def _gu_spec(l):
    return pl.BlockSpec((N_DEV, None, GU_SHARD, D), lambda *_: (0, l, 0, 0), pipeline_mode=pl.Buffered(1))


def _out_mlp(x, ys, w_out, nw2, w_gu_t, w_down, l, tm=256):
    t = x.shape[0]

    def body(x_ref, ya, yb, yc, yd, wo_ref, nw_ref, wgu_ref, wd_ref, mix_ref, x1_ref, h2_ref, gu_ref, act_ref, x2_ref):
        mix = jnp.concatenate([ya[...], yb[...], yc[...], yd[...]], axis=1)
        mix_ref[...] = mix
        x1 = x_ref[...] + jnp.dot(mix, wo_ref[...], preferred_element_type=F32)
        x1_ref[...] = x1
        h2 = _rms(x1, nw_ref[l:l + 1, :])[0].astype(BF16)
        h2_ref[...] = h2
        gu = _dg(h2, wgu_ref[...].reshape(2 * FF, D), 1, 1)
        gu_ref[...] = gu.astype(BF16)
        gate, up = gu[:, :FF], gu[:, FF:]
        act = (gate * _sigmoid(gate) * up).astype(BF16)
        act_ref[...] = act
        x2_ref[...] = x1 + jnp.dot(act, wd_ref[...], preferred_element_type=F32)

    return pl.pallas_call(
        body, name="out_mlp", grid=(t // tm,),
        in_specs=[_row(tm, D), _row(tm, G), _row(tm, G), _row(tm, G), _row(tm, G), _layer((D, D), l),
                  _resident((DEPTH, D)), _gu_spec(l), _layer((FF, D), l)],
        out_specs=[_row(tm, D), _row(tm, D), _row(tm, D), _row(tm, 2 * FF), _row(tm, FF), _row(tm, D)],
        out_shape=[jax.ShapeDtypeStruct((t, D), BF16), jax.ShapeDtypeStruct((t, D), F32),
                   jax.ShapeDtypeStruct((t, D), BF16), jax.ShapeDtypeStruct((t, 2 * FF), BF16),
                   jax.ShapeDtypeStruct((t, FF), BF16), jax.ShapeDtypeStruct((t, D), F32)],
        compiler_params=_cparams("parallel"))(x, *ys, w_out, nw2, w_gu_t, w_down)


def _loss_head(x, nw, target, tm=512):
    t = x.shape[0]

    def body(x_ref, nw_ref, tg_ref, loss_ref, dnw_ref, dx_ref):
        @pl.when(pl.program_id(0) == 0)
        def _():
            loss_ref[...] = jnp.zeros_like(loss_ref)
            dnw_ref[...] = jnp.zeros_like(dnw_ref)
        nw_v = nw_ref[...]
        y, xh, r = _rms(x_ref[...], nw_v)
        err = y - tg_ref[...]
        loss_ref[...] += 0.5 * jnp.sum(err * err) * (1.0 / D)
        dx, dnw = _rms_bwd(err * (1.0 / D), xh, r, nw_v)
        dx_ref[...] = dx
        dnw_ref[...] += dnw

    return pl.pallas_call(
        body, name="loss_head", grid=(t // tm,),
        in_specs=[_row(tm, D), _resident((1, D)), _row(tm, D)],
        out_specs=[_acc((8, LANES)), _acc((1, D)), _row(tm, D)],
        out_shape=[jax.ShapeDtypeStruct((8, LANES), F32), jax.ShapeDtypeStruct((1, D), F32),
                   jax.ShapeDtypeStruct((t, D), F32)],
        compiler_params=_cparams("arbitrary"))(x, nw, target)


def _mlp_out_bwd(dx2, x1, gu, nw2, w_down, w_gu, w_out, l, tm=256, rider=None):
    t = dx2.shape[0]

    def body(*refs):
        (dx2_ref, x1_ref, gu_ref, nw_ref, wd_ref, wgu_ref, wo_ref, dgu_ref, dx1_ref, dmix_ref, dnw_ref), ride = \
            _split_refs(refs, 7, 4, 0, rider)
        _ride_begin(rider, ride, (t // tm,))

        @pl.when(pl.program_id(0) == 0)
        def _():
            dnw_ref[...] = jnp.zeros_like(dnw_ref)
        dx2 = dx2_ref[...]
        dact = _dg(dx2.astype(BF16), wd_ref[...], 1, 1)
        gu = gu_ref[...].astype(F32)
        gate, up = gu[:, :FF], gu[:, FF:]
        s = _sigmoid(gate)
        dgu = jnp.concatenate([dact * up * (s * (1.0 + gate * (1.0 - s))), dact * (gate * s)], axis=1).astype(BF16)
        dgu_ref[...] = dgu
        dh2 = jnp.dot(dgu, wgu_ref[...].reshape(2 * FF, D), preferred_element_type=F32)
        nw_v = nw_ref[l:l + 1, :]
        _, xh, r = _rms(x1_ref[...], nw_v)
        dxn, dnw = _rms_bwd(dh2, xh, r, nw_v)
        dx1 = dx2 + dxn
        dx1_ref[...] = dx1
        dnw_ref[...] += dnw
        dmix_ref[...] = _dg(dx1.astype(BF16), wo_ref[...], 1, 1)
        _ride_end(rider, ride, (t // tm,))

    specs, operands = _host(
        rider, [_row(tm, D), _row(tm, D), _row(tm, 2 * FF), _resident((DEPTH, D)), _layer((FF, D), l), _gu_spec(l),
                _layer((D, D), l)],
        [_row(tm, 2 * FF), _row(tm, D), _row(tm, D), _acc((1, D))],
        [jax.ShapeDtypeStruct((t, 2 * FF), BF16), jax.ShapeDtypeStruct((t, D), F32),
         jax.ShapeDtypeStruct((t, D), F32), jax.ShapeDtypeStruct((1, D), F32)],
        [], [dx2, x1, gu, nw2, w_down, w_gu, w_out])
    return pl.pallas_call(body, name="mlp_out_bwd", grid=(t // tm,), compiler_params=_cparams("arbitrary"),
                          **specs)(*operands)


_DP_WIDTHS = (768, 768, 512, 768, 256, 256, 128, 128)


def _in_proj_bwd(dps, x, dx1, nw, wp, l, tm=512):
    t = x.shape[0]

    def body(*refs):
        dp_refs, (x_ref, dx1_ref, nw_ref, w_ref, dp_ref, dx_ref, dnw_ref) = refs[:8], refs[8:]

        @pl.when(pl.program_id(0) == 0)
        def _():
            dnw_ref[...] = jnp.zeros_like(dnw_ref)
        dp = jnp.concatenate([r[...] for r in dp_refs], axis=1)
        dp_ref[...] = dp
        dh = _dg(dp, w_ref[...], 1, 1)
        nw_v = nw_ref[l:l + 1, :]
        _, xh, r = _rms(x_ref[...], nw_v)
        dxn, dnw = _rms_bwd(dh, xh, r, nw_v)
        dx_ref[...] = dx1_ref[...] + dxn
        dnw_ref[...] += dnw

    return pl.pallas_call(
        body, name="in_proj_bwd", grid=(t // tm,),
        in_specs=[_row(tm, w) for w in _DP_WIDTHS] + [_row(tm, D), _row(tm, D), _resident((DEPTH, D)), _resident((D, P))],
        out_specs=[_row(tm, P), _row(tm, D), _acc((1, D))],
        out_shape=[jax.ShapeDtypeStruct((t, P), BF16), jax.ShapeDtypeStruct((t, D), F32),
                   jax.ShapeDtypeStruct((1, D), F32)],
        compiler_params=_cparams("arbitrary"))(*dps, x, dx1, nw, wp)


def _accumulate(acc, o_ref, t_axis, term):
    @pl.when(pl.program_id(t_axis) == 0)
    def _():
        acc[...] = jnp.zeros_like(acc)
    acc[...] += term

    @pl.when(pl.program_id(t_axis) == pl.num_programs(t_axis) - 1)
    def _():
        o_ref[...] = acc[...].astype(o_ref.dtype)


WGRAD_TOKENS = 2048


WGRAD_ROWS = 2 * GU_SHARD


def _wgrad_rows(a, b, name, tt=WGRAD_TOKENS, rows=WGRAD_ROWS, out_dtype=BF16, rider=None):
    t, m = a.shape
    tt = min(tt, t)
    grid = (m // rows, t // tt)

    def body(*refs):
        (a_ref, b_ref, o_ref, acc), ride = _split_refs(refs, 2, 1, 1, rider)
        _ride_begin(rider, ride, grid)
        _accumulate(acc, o_ref, 1, _dg(a_ref[...], b_ref[...].astype(BF16), 0, 0))
        _ride_end(rider, ride, grid)

    specs, operands = _host(
        rider, [pl.BlockSpec((tt, rows), lambda j, i: (i, j)), pl.BlockSpec((tt, D), lambda j, i: (i, 0))],
        [pl.BlockSpec((rows, D), lambda j, i: (j, 0))], [jax.ShapeDtypeStruct((m, D), out_dtype)],
        [pltpu.VMEM((rows, D), F32)], [a, b])
    out = pl.pallas_call(body, name=name, grid=grid, compiler_params=_cparams("arbitrary", "arbitrary"), **specs)(*operands)
    return out[0] if rider is None else out


def _relayout_w_in(g, tr=256):
    def body(w_ref, o_ref):
        full = jnp.concatenate([w_ref[d] for d in range(N_DEV)], axis=1)
        parts = []
        for a, b, z in _PAD_SEGS:
            parts.append(full[:, a:b])
            if z:
                parts.append(jnp.zeros((tr, z), full.dtype))
        o_ref[...] = jnp.concatenate(parts, axis=1)

    return pl.pallas_call(
        body, name="relayout_w_in", grid=(D // tr,),
        in_specs=[pl.BlockSpec((N_DEV, tr, IN_SHARD), lambda i: (0, i, 0))], out_specs=_row(tr, P),
        out_shape=jax.ShapeDtypeStruct((D, P), g.dtype), compiler_params=_cparams("parallel"))(g)


IN_ROWS = 432


def _scatter_w_in_grad(gp_t, tc=256):
    def body(g_ref, o_ref):
        g = g_ref[...]
        pieces, off = [], 0
        for a, b, z in _PAD_SEGS:
            pieces.append((a, g[off:off + (b - a)]))
            off += (b - a) + z
        spill = -(-(IN_SHARD * (N_DEV - 1) + IN_ROWS - IN_COLS) // 8) * 8
        nat = jnp.concatenate([piece for _, piece in sorted(pieces, key=lambda ap: ap[0])]
                              + [jnp.zeros((spill, tc), F32)], axis=0)
        for d in range(N_DEV):
            o_ref[d] = nat[IN_SHARD * d:IN_SHARD * d + IN_ROWS].astype(BF16)

    return pl.pallas_call(
        body, name="scatter_w_in_grad", grid=(D // tc,),
        in_specs=[pl.BlockSpec((P, tc), lambda i: (0, i))],
        out_specs=pl.BlockSpec((N_DEV, IN_ROWS, tc), lambda i: (0, 0, i)),
        out_shape=jax.ShapeDtypeStruct((N_DEV, IN_ROWS, D), BF16),
        compiler_params=_cparams("parallel"))(gp_t)


_A_BLK = 3


def _sgu_specs(l):
    return [_resident((DEPTH, G)), _resident((DEPTH, G)), _layer((NH, SGU_C, SGU_C), l), _layer((NH, SGU_C), l)]


def _sgu_fwd(p, ln_w, ln_b, ws, bs, l, tm=256):
    t = p.shape[0]

    def body(uv_ref, lw_ref, lb_ref, ws_ref, bs_ref, y_ref):
        ws_v = [ws_ref[h] for h in range(NH)]
        for c in range(tm // SGU_C):
            rows = pl.ds(c * SGU_C, SGU_C)
            y_ref[rows, :] = _sgu_chunk(uv_ref[rows, :], lw_ref[l:l + 1, :], lb_ref[l:l + 1, :], ws_v,
                                        bs_ref[...]).astype(BF16)

    return pl.pallas_call(
        body, name="sgu_fwd", grid=(t // tm,),
        in_specs=[_row(tm, 2 * G, _A_BLK)] + _sgu_specs(l), out_specs=_row(tm, G),
        out_shape=jax.ShapeDtypeStruct((t, G), BF16),
        compiler_params=_cparams("parallel"))(p, ln_w, ln_b, ws, bs)


def _sgu_bwd(p, dmix, ln_w, ln_b, ws, bs, l, tm=256):
    t = p.shape[0]

    def body(uv_ref, dy_ref, lw_ref, lb_ref, ws_ref, bs_ref, duv_ref, dlw_ref, dlb_ref, dws_ref, dbs_ref):
        @pl.when(pl.program_id(0) == 0)
        def _():
            for r in (dlw_ref, dlb_ref, dws_ref, dbs_ref):
                r[...] = jnp.zeros_like(r)
        ws_v = [ws_ref[h] for h in range(NH)]
        for c in range(tm // SGU_C):
            rows = pl.ds(c * SGU_C, SGU_C)
            _, vjp = jax.vjp(_sgu_chunk, uv_ref[rows, :], lw_ref[l:l + 1, :], lb_ref[l:l + 1, :], ws_v, bs_ref[...])
            duv, dlw, dlb, dws, dbs = vjp(dy_ref[rows, :])
            duv_ref[rows, :] = duv.astype(BF16)
            dlw_ref[...] += dlw
            dlb_ref[...] += dlb
            dbs_ref[...] += dbs
            for h in range(NH):
                dws_ref[h] += dws[h]

    return pl.pallas_call(
        body, name="sgu_bwd", grid=(t // tm,),
        in_specs=[_row(tm, 2 * G, _A_BLK), _row(tm, G, 0)] + _sgu_specs(l),
        out_specs=[_row(tm, 2 * G), _acc((1, G)), _acc((1, G)), _acc((NH, SGU_C, SGU_C)), _acc((NH, SGU_C))],
        out_shape=[jax.ShapeDtypeStruct((t, 2 * G), BF16), jax.ShapeDtypeStruct((1, G), F32),
                   jax.ShapeDtypeStruct((1, G), F32), jax.ShapeDtypeStruct((NH, SGU_C, SGU_C), F32),
                   jax.ShapeDtypeStruct((NH, SGU_C), F32)],
        compiler_params=_cparams("arbitrary"))(p, dmix, ln_w, ln_b, ws, bs)


HALO = 8


def _prev_halo(tm, width, col):
    return pl.BlockSpec((HALO, width), lambda i: (jnp.maximum(i * (tm // HALO) - 1, 0), col))


def _next_halo(tm, width, col, t):
    return pl.BlockSpec((HALO, width), lambda i: (jnp.minimum((i + 1) * (tm // HALO), t // HALO - 1), col))


def _with_prev(halo_ref, x_ref):
    halo = jnp.where(pl.program_id(0) == 0, 0.0, halo_ref[...])
    return jnp.concatenate([halo, x_ref[...]], axis=0)


def _with_next(x_ref, halo_ref):
    halo = jnp.where(pl.program_id(0) == pl.num_programs(0) - 1, 0.0, halo_ref[...])
    return jnp.concatenate([x_ref[...], halo], axis=0)


def _delay(ext, j):
    return ext if j == 0 else pltpu.roll(ext, j, axis=0)


def _advance(ext, j):
    return ext if j == 0 else pltpu.roll(ext, ext.shape[0] - j, axis=0)


def _causal_conv(ext, w, tm):
    kw = w.shape[0]
    out = None
    for k in range(kw):
        term = _delay(ext, kw - 1 - k)[HALO:HALO + tm] * w[k:k + 1, :]
        out = term if out is None else out + term
    return out


def _causal_conv_t(ext, w, tm):
    kw = w.shape[0]
    out = None
    for k in range(kw):
        term = _advance(ext, kw - 1 - k)[0:tm] * w[k:k + 1, :]
        out = term if out is None else out + term
    return out


def _conv_wgrad(ext, dy, kw, tm):
    return jnp.concatenate(
        [jnp.sum(_delay(ext, kw - 1 - k)[HALO:HALO + tm] * dy, axis=0, keepdims=True) for k in range(kw)], axis=0)


_B_GB, _B_GC, _B_H = 8, 9, 10
_C_QKV, _D_QKV = 0, 1
_C_Z, _D_Z = 11, 12
_C_AB, _D_GLR = 26, 27


def _sconv_fwd(p, w, l, tm=512):
    t = p.shape[0]

    def body(gb_ref, gc_ref, h_ref, gc_halo, h_halo, w_ref, y_ref):
        s_ext = _with_prev(gc_halo, gc_ref) * _with_prev(h_halo, h_ref)
        y_ref[...] = (gb_ref[...] * _causal_conv(s_ext, w_ref[...], tm)).astype(BF16)

    return pl.pallas_call(
        body, name="sconv_fwd", grid=(t // tm,),
        in_specs=[_row(tm, G, _B_GB), _row(tm, G, _B_GC), _row(tm, G, _B_H), _prev_halo(tm, G, _B_GC),
                  _prev_halo(tm, G, _B_H), _layer((3, G), l)],
        out_specs=_row(tm, G), out_shape=jax.ShapeDtypeStruct((t, G), BF16),
        compiler_params=_cparams("parallel"))(p, p, p, p, p, w)


def _sconv_bwd(p, dmix, w, l, tm=512):
    t = p.shape[0]

    def body(gb_ref, gc_ref, h_ref, gc_halo, h_halo, gb_next, dy_ref, dy_next, w_ref, dp_ref, dw_ref):
        @pl.when(pl.program_id(0) == 0)
        def _():
            dw_ref[...] = jnp.zeros_like(dw_ref)
        w_v = w_ref[...]
        s_ext = _with_prev(gc_halo, gc_ref) * _with_prev(h_halo, h_ref)
        dout = dy_ref[...]
        d_gb = dout * _causal_conv(s_ext, w_v, tm)
        dconv_ext = _with_next(dy_ref, dy_next) * _with_next(gb_ref, gb_next)
        ds = _causal_conv_t(dconv_ext, w_v, tm)
        dw_ref[...] += _conv_wgrad(s_ext, dconv_ext[0:tm], 3, tm)
        dp_ref[...] = jnp.concatenate([d_gb, ds * h_ref[...], ds * gc_ref[...]], axis=1).astype(BF16)

    return pl.pallas_call(
        body, name="sconv_bwd", grid=(t // tm,),
        in_specs=[_row(tm, G, _B_GB), _row(tm, G, _B_GC), _row(tm, G, _B_H), _prev_halo(tm, G, _B_GC),
                  _prev_halo(tm, G, _B_H), _next_halo(tm, G, _B_GB, t), _row(tm, G, 1), _next_halo(tm, G, 1, t),
                  _layer((3, G), l)],
        out_specs=[_row(tm, 3 * G), _acc((3, G))],
        out_shape=[jax.ShapeDtypeStruct((t, 3 * G), BF16), jax.ShapeDtypeStruct((3, G), F32)],
        compiler_params=_cparams("arbitrary"))(p, p, p, p, p, p, dmix, dmix, w)


def _qkv_conv_fwd(p, w, l, tm=512):
    t = p.shape[0]

    def body(x_ref, x_halo, w_ref, y_ref):
        c = _causal_conv(_with_prev(x_halo, x_ref), w_ref[...], tm)
        y_ref[...] = c * _sigmoid(c)

    return pl.pallas_call(
        body, name="qkv_conv_fwd", grid=(t // tm,),
        in_specs=[_row(tm, 3 * G, _C_QKV), _prev_halo(tm, 3 * G, _C_QKV), _layer((4, 3 * G), l)],
        out_specs=_row(tm, 3 * G), out_shape=jax.ShapeDtypeStruct((t, 3 * G), F32),
        compiler_params=_cparams("parallel"))(p, p, w)


def _qkv_conv_bwd(p, dy, w, l, tm=512):
    t = p.shape[0]

    def body(x_ref, x_halo, x_next, dy_ref, dy_next, w_ref, dx_ref, dw_ref):
        @pl.when(pl.program_id(0) == 0)
        def _():
            dw_ref[...] = jnp.zeros_like(dw_ref)
        w_v = w_ref[...]
        x_all = jnp.concatenate([_with_prev(x_halo, x_ref), jnp.where(
            pl.program_id(0) == pl.num_programs(0) - 1, 0.0, x_next[...])], axis=0)
        c = _causal_conv(x_all, w_v, tm + HALO)
        s = _sigmoid(c)
        dc_ext = _with_next(dy_ref, dy_next) * (s * (1.0 + c * (1.0 - s)))
        dx_ref[...] = _causal_conv_t(dc_ext, w_v, tm).astype(BF16)
        dw_ref[...] += _conv_wgrad(x_all[0:HALO + tm], dc_ext[0:tm], 4, tm)

    return pl.pallas_call(
        body, name="qkv_conv_bwd", grid=(t // tm,),
        in_specs=[_row(tm, 3 * G, _C_QKV), _prev_halo(tm, 3 * G, _C_QKV), _next_halo(tm, 3 * G, _C_QKV, t),
                  _row(tm, 3 * G), _next_halo(tm, 3 * G, 0, t), _layer((4, 3 * G), l)],
        out_specs=[_row(tm, 3 * G), _acc((4, 3 * G))],
        out_shape=[jax.ShapeDtypeStruct((t, 3 * G), BF16), jax.ShapeDtypeStruct((4, 3 * G), F32)],
        compiler_params=_cparams("arbitrary"))(p, p, p, dy, dy, w)


_STATE = pl.BlockSpec((1, NH, HD, HD), lambda i: (i, 0, 0, 0))


def _dn_specs():
    return [_resident((DEPTH, NH)), _resident((DEPTH, NH)), _resident((DEPTH, HD))]


def _dn_fwd(qkv, p, params, l, cps=4, rider=None):
    t = qkv.shape[0]
    tm = DN_C * cps
    nb = cps * NH

    def body(*refs):
        (qkv_ref, z_ref, ab_ref, alog_ref, dt_ref, nw_ref, y_ref, st_ref, inv_ref, state), ride = _split_refs(
            refs, 6, 3, 1, rider)
        _ride_begin(rider, ride, (t // tm,))

        @pl.when(pl.program_id(0) == 0)
        def _():
            state[...] = jnp.zeros_like(state)
        st_ref[0] = state[...]
        y, new, inv = _dn_tile(qkv_ref[:, 0:G], qkv_ref[:, G:2 * G], qkv_ref[:, 2 * G:3 * G], ab_ref[...], z_ref[...],
                               state[...], None, alog_ref[l:l + 1, :], dt_ref[l:l + 1, :], nw_ref[l:l + 1, :])
        y_ref[...] = y.astype(BF16)
        inv_ref[...] = inv
        state[...] = new
        _ride_end(rider, ride, (t // tm,))

    specs, operands = _host(
        rider, [_row(tm, 3 * G), _row(tm, G, _C_Z), _row(tm, LANES, _C_AB)] + _dn_specs(),
        [_row(tm, G), _STATE, pl.BlockSpec((nb, DN_C, DN_C), lambda i: (i, 0, 0))],
        [jax.ShapeDtypeStruct((t, G), BF16), jax.ShapeDtypeStruct((t // tm, NH, HD, HD), F32),
         jax.ShapeDtypeStruct((t // DN_C * NH, DN_C, DN_C), F32)],
        [pltpu.VMEM((NH, HD, HD), F32)], [qkv, p, p, *params])
    return pl.pallas_call(body, name="dn_fwd", grid=(t // tm,), compiler_params=_cparams("arbitrary"), **specs)(*operands)


def _dn_bwd(qkv, p, states, inv, dmix, params, l, cps=4, rider=None):
    t = qkv.shape[0]
    tm = DN_C * cps
    n = t // tm
    nb = cps * NH

    def body(*refs):
        (qkv_ref, z_ref, ab_ref, st_ref, inv_ref, dy_ref, alog_ref, dt_ref, nw_ref,
         dqkv_ref, dz_ref, dab_ref, dalog_ref, ddt_ref, dnw_ref, dstate), ride = _split_refs(refs, 9, 6, 1, rider)
        _ride_begin(rider, ride, (n,))
        dprm_refs = (dalog_ref, ddt_ref, dnw_ref)

        @pl.when(pl.program_id(0) == 0)
        def _():
            dstate[...] = jnp.zeros_like(dstate)
            for r in dprm_refs:
                r[...] = jnp.zeros_like(r)
        inv_v = inv_ref[...]
        tile = lambda q, k, v, ab, z, st, alog, dt, nw: _dn_tile(q, k, v, ab, z, st, inv_v, alog, dt, nw)[:2]
        _, vjp = jax.vjp(tile, qkv_ref[:, 0:G], qkv_ref[:, G:2 * G], qkv_ref[:, 2 * G:3 * G], ab_ref[...], z_ref[...],
                         st_ref[0], alog_ref[l:l + 1, :], dt_ref[l:l + 1, :], nw_ref[l:l + 1, :])
        dq, dk, dv, dab, dz, dst, *dprm = vjp((dy_ref[...], dstate[...]))
        dqkv_ref[...] = jnp.concatenate([dq, dk, dv], axis=1)
        dz_ref[...] = dz.astype(BF16)
        dab_ref[...] = dab.astype(BF16)
        dstate[...] = dst
        for r, g in zip(dprm_refs, dprm):
            r[...] += g
        _ride_end(rider, ride, (n,))

    rev = lambda w, blk: pl.BlockSpec((tm, w), lambda i: (n - 1 - i, blk))
    specs, operands = _host(
        rider, [rev(3 * G, 0), rev(G, _C_Z), rev(LANES, _C_AB),
                pl.BlockSpec((1, NH, HD, HD), lambda i: (n - 1 - i, 0, 0, 0)),
                pl.BlockSpec((nb, DN_C, DN_C), lambda i: (n - 1 - i, 0, 0)), rev(G, 2)] + _dn_specs(),
        [rev(3 * G, 0), rev(G, 0), rev(LANES, 0), _acc((1, NH)), _acc((1, NH)), _acc((1, HD))],
        [jax.ShapeDtypeStruct((t, 3 * G), F32), jax.ShapeDtypeStruct((t, G), BF16),
         jax.ShapeDtypeStruct((t, LANES), BF16), jax.ShapeDtypeStruct((1, NH), F32),
         jax.ShapeDtypeStruct((1, NH), F32), jax.ShapeDtypeStruct((1, HD), F32)],
        [pltpu.VMEM((NH, HD, HD), F32)], [qkv, p, p, states, inv, dmix, *params])
    return pl.pallas_call(body, name="dn_bwd", grid=(n,), compiler_params=_cparams("arbitrary"), **specs)(*operands)


def _gla_specs(l):
    return [_layer((16, G), l), _resident((DEPTH, G)), _resident((DEPTH, HD))]


def _gla_fwd(p, params, l, cps=4, rider=None):
    t = p.shape[0]
    tm = GLA_C * cps

    def body(*refs):
        (qkv_ref, z_ref, glr_ref, w2_ref, gb_ref, nw_ref, y_ref, st_ref, state), ride = _split_refs(refs, 6, 2, 1, rider)
        _ride_begin(rider, ride, (t // tm,))

        @pl.when(pl.program_id(0) == 0)
        def _():
            state[...] = jnp.zeros_like(state)
        st_ref[0] = state[...]
        y, new = _gla_tile(qkv_ref[:, 0:G], qkv_ref[:, G:2 * G], qkv_ref[:, 2 * G:3 * G], glr_ref[...], z_ref[...],
                           state[...], w2_ref[...], gb_ref[l:l + 1, :], nw_ref[l:l + 1, :])
        y_ref[...] = y.astype(BF16)
        state[...] = new
        _ride_end(rider, ride, (t // tm,))

    specs, operands = _host(
        rider, [_row(tm, 3 * G, _D_QKV), _row(tm, G, _D_Z), _row(tm, LANES, _D_GLR)] + _gla_specs(l),
        [_row(tm, G), _STATE],
        [jax.ShapeDtypeStruct((t, G), BF16), jax.ShapeDtypeStruct((t // tm, NH, HD, HD), F32)],
        [pltpu.VMEM((NH, HD, HD), F32)], [p, p, p, *params])
    return pl.pallas_call(body, name="gla_fwd", grid=(t // tm,), compiler_params=_cparams("arbitrary"), **specs)(*operands)


def _gla_bwd(p, states, dmix, params, l, cps=4):
    t = p.shape[0]
    tm = GLA_C * cps
    n = t // tm

    def body(qkv_ref, z_ref, glr_ref, st_ref, dy_ref, w2_ref, gb_ref, nw_ref,
             dqkv_ref, dz_ref, dglr_ref, dw2_ref, dgb_ref, dnw_ref, dstate):
        dprm_refs = (dw2_ref, dgb_ref, dnw_ref)

        @pl.when(pl.program_id(0) == 0)
        def _():
            dstate[...] = jnp.zeros_like(dstate)
            for r in dprm_refs:
                r[...] = jnp.zeros_like(r)
        _, vjp = jax.vjp(_gla_tile, qkv_ref[:, 0:G], qkv_ref[:, G:2 * G], qkv_ref[:, 2 * G:3 * G], glr_ref[...],
                         z_ref[...], st_ref[0], w2_ref[...], gb_ref[l:l + 1, :], nw_ref[l:l + 1, :])
        dq, dk, dv, dglr, dz, dst, *dprm = vjp((dy_ref[...], dstate[...]))
        dqkv_ref[...] = jnp.concatenate([dq, dk, dv], axis=1).astype(BF16)
        dz_ref[...] = dz.astype(BF16)
        dglr_ref[...] = dglr.astype(BF16)
        dstate[...] = dst
        for r, g in zip(dprm_refs, dprm):
            r[...] += g

    rev = lambda w, blk: pl.BlockSpec((tm, w), lambda i: (n - 1 - i, blk))
    return pl.pallas_call(
        body, name="gla_bwd", grid=(n,),
        in_specs=[rev(3 * G, _D_QKV), rev(G, _D_Z), rev(LANES, _D_GLR),
                  pl.BlockSpec((1, NH, HD, HD), lambda i: (n - 1 - i, 0, 0, 0)), rev(G, 3)] + _gla_specs(l),
        out_specs=[rev(3 * G, 0), rev(G, 0), rev(LANES, 0), _acc((16, G)), _acc((1, G)), _acc((1, HD))],
        out_shape=[jax.ShapeDtypeStruct((t, 3 * G), BF16), jax.ShapeDtypeStruct((t, G), BF16),
                   jax.ShapeDtypeStruct((t, LANES), BF16), jax.ShapeDtypeStruct((16, G), F32),
                   jax.ShapeDtypeStruct((1, G), F32), jax.ShapeDtypeStruct((1, HD), F32)],
        scratch_shapes=[pltpu.VMEM((NH, HD, HD), F32)],
        compiler_params=_cparams("arbitrary"))(p, p, p, states, dmix, *params)


def _adamw_math(w, g, m, v):
    m = ADAM_B1 * m + (1.0 - ADAM_B1) * g
    v = ADAM_B2 * v + (1.0 - ADAM_B2) * (g * g)
    m_hat = m / (1.0 - ADAM_B1 ** ADAM_STEP)
    v_hat = v / (1.0 - ADAM_B2 ** ADAM_STEP)
    return -ADAM_LR * (m_hat / (jnp.sqrt(v_hat) + ADAM_EPS) + ADAM_WD * w), m, v


def _adamw_large(parts, w, m, v, name, tr):
    _, r, c = w.shape

    def body(p0_ref, p1_ref, w_ref, m_ref, v_ref, g_ref, d_ref, nm_ref, nv_ref):
        def total(p_ref):
            g = p_ref[0].astype(F32)
            for k in range(1, N_DEV):
                g = g + p_ref[k].astype(F32)
            return g

        g = jnp.where(pl.program_id(0) == 0, total(p0_ref), total(p1_ref))
        g_ref[...] = g
        d_ref[...], nm_ref[...], nv_ref[...] = _adamw_math(w_ref[...], g, m_ref[...], v_ref[...])

    blk = pl.BlockSpec((None, tr, c), lambda l, i: (l, i, 0))
    part = pl.BlockSpec((N_DEV, tr, c), lambda l, i: (0, i, 0))
    return pl.pallas_call(
        body, name=name, grid=(DEPTH, r // tr), in_specs=[part, part, blk, blk, blk],
        out_specs=[blk] * 4, out_shape=[jax.ShapeDtypeStruct(w.shape, F32)] * 4,
        compiler_params=_cparams("parallel", "parallel"))(*parts, w, m, v)


def _adamw_w_in(parts, w, m, v, tc=256):
    def body(p0_ref, p1_ref, w_ref, m_ref, v_ref, g_ref, d_ref, nm_ref, nv_ref):
        for l, p_ref in enumerate((p0_ref, p1_ref)):
            g = p_ref[0, 0:IN_SHARD, :].astype(F32)
            for k in range(1, N_DEV):
                g = g + p_ref[k, 0:IN_SHARD, :].astype(F32)
            g_ref[:, l, :] = g
            d_ref[:, l, :], nm_ref[:, l, :], nv_ref[:, l, :] = _adamw_math(w_ref[:, l, :], g, m_ref[:, l, :], v_ref[:, l, :])

    blk = pl.BlockSpec((IN_SHARD, DEPTH, tc), lambda i: (0, 0, i))
    part = pl.BlockSpec((N_DEV, IN_ROWS, tc), lambda i: (0, 0, i))
    return pl.pallas_call(
        body, name="adamw_w_in", grid=(D // tc,), in_specs=[part, part, blk, blk, blk], out_specs=[blk] * 4,
        out_shape=[jax.ShapeDtypeStruct(w.shape, F32)] * 4, compiler_params=_cparams("parallel"))(*parts, w, m, v)


_SLAB_AT = {
    "sgu_w_spatial": (0, 0, SGU_C, LANES),
    "sgu_b_spatial": (128, 0, NH, SGU_C),
    "norm1_w": (132, 0, 1, D),
    "norm2_w": (133, 0, 1, D),
    "sgu_ln_w": (134, 0, 1, G),
    "sgu_ln_b": (134, 256, 1, G),
    "gla_gate_bias": (134, 512, 1, G),
    "dn_norm_w": (134, 768, 1, HD),
    "gla_norm_w": (134, 896, 1, HD),
    "dn_a_log": (135, 0, 1, NH),
    "dn_dt_bias": (135, 128, 1, NH),
    "gla_w_gate2": (136, 0, 16, G),
    "dn_conv_w": (136, 256, 4, 3 * G),
    "sc_conv_w": (140, 256, 3, G),
}
_LAYER_ROWS = 152
_FINAL_ROW = DEPTH * _LAYER_ROWS
_SLAB_ROWS, _SLAB_COLS = _FINAL_ROW + 8, 1024
_SLAB_ORDER = tuple(_SLAB_AT)
_SHARDED_SMALL = ("sc_conv_w", "dn_conv_w", "gla_w_gate2")
_REPLICATED = tuple(k for k in _SLAB_ORDER if k not in _SHARDED_SMALL)


def _region(name, l, h=0):
    r0, c0, nr, nc = _SLAB_AT[name]
    return slice(_LAYER_ROWS * l + r0, _LAYER_ROWS * l + r0 + nr), slice(c0 + nc * h, c0 + nc * (h + 1))


def _pack_small(layer_grads, d_final):
    def body(*refs):
        slab = refs[-1]
        slab[...] = jnp.zeros_like(slab)
        it = iter(refs[:-1])
        for l in range(DEPTH):
            for name in _SLAB_ORDER:
                ref = next(it)
                if name == "sgu_w_spatial":
                    for h in range(NH):
                        slab[_region(name, l, h)] = ref[h]
                else:
                    slab[_region(name, l)] = ref[...]
        slab[_FINAL_ROW:_FINAL_ROW + 1, :] = next(it)[...]

    flat = [layer_grads[l][name] for l in range(DEPTH) for name in _SLAB_ORDER] + [d_final]
    return pl.pallas_call(body, name="pack_small_grads", out_shape=jax.ShapeDtypeStruct((_SLAB_ROWS, _SLAB_COLS), F32),
                          compiler_params=_cparams())(*flat)


def _adamw_small(parts, w, m, v):
    names = _REPLICATED + ("final_norm_w",)
    n_in = len(names)

    def body(*refs):
        def total(rows, cols):
            g = refs[0][0, rows, cols]
            for k in range(1, N_DEV):
                g = g + refs[0][k, rows, cols]
            return g

        w_refs = dict(zip(names, refs[1:1 + n_in]))
        m_refs = dict(zip(names, refs[1 + n_in:1 + 2 * n_in]))
        v_refs = dict(zip(names, refs[1 + 2 * n_in:1 + 3 * n_in]))
        outs = refs[1 + 3 * n_in:]
        out_refs = [dict(zip(names, outs[j * n_in:(j + 1) * n_in])) for j in range(4)]
        full_refs = dict(zip(_SHARDED_SMALL, outs[4 * n_in:]))

        def update(name, idx, g):
            res = (g,) + _adamw_math(w_refs[name][idx], g, m_refs[name][idx], v_refs[name][idx])
            for o, val in zip(out_refs, res):
                o[name][idx] = val

        for l in range(DEPTH):
            for name in _REPLICATED:
                if name == "sgu_w_spatial":
                    for h in range(NH):
                        update(name, (l, h), total(*_region(name, l, h)))
                elif name == "sgu_b_spatial":
                    update(name, (l,), total(*_region(name, l)))
                else:
                    update(name, (slice(l, l + 1), slice(None)), total(*_region(name, l)))
            for name in _SHARDED_SMALL:
                full_refs[name][l] = total(*_region(name, l))
        update("final_norm_w", (slice(None), slice(None)), total(slice(_FINAL_ROW, _FINAL_ROW + 1), slice(None)))

    shapes = [jax.ShapeDtypeStruct(w[k].shape, F32) for k in names]
    full_shapes = [jax.ShapeDtypeStruct((DEPTH,) + _SLAB_AT[k][2:], F32) for k in _SHARDED_SMALL]
    outs = pl.pallas_call(body, name="adamw_small", out_shape=shapes * 4 + full_shapes, compiler_params=_cparams())(
        parts, *[w[k] for k in names], *[m[k] for k in names], *[v[k] for k in names])
    result = [dict(zip(names, outs[j * n_in:(j + 1) * n_in])) for j in range(4)]
    return result, dict(zip(_SHARDED_SMALL, outs[4 * n_in:]))


def _adamw_shards(g, w, m, v):
    names = _SHARDED_SMALL
    n = len(names)

    def body(*refs):
        for j in range(n):
            g_v = refs[j][...]
            res = _adamw_math(refs[n + j][...], g_v, refs[2 * n + j][...], refs[3 * n + j][...])
            for o, val in zip(refs[4 * n + j::n], res):
                o[...] = val

    shapes = [jax.ShapeDtypeStruct(w[k].shape, F32) for k in names]
    outs = pl.pallas_call(body, name="adamw_small_shards", out_shape=shapes * 3, compiler_params=_cparams())(
        *[g[k] for k in names], *[w[k] for k in names], *[m[k] for k in names], *[v[k] for k in names])
    return [dict(zip(names, outs[j * n:(j + 1) * n])) for j in range(3)]


_ANY = pl.BlockSpec(memory_space=pl.ANY)


def _place():
    return lax.axis_index("x"), lax.axis_index("y"), lax.axis_index("c")


def _sems(n):
    return [pltpu.SemaphoreType.DMA((n, 7)), pltpu.SemaphoreType.DMA((n, 7)), pltpu.SemaphoreType.DMA((n,))]


class _Gather:
    def __init__(self, blocks, second, forward_at=1.0):
        self.inputs, self.second, self.forward_at = list(blocks), list(second), forward_at
        self.out_shape = [jax.ShapeDtypeStruct((a.shape[0], N_DEV) + a.shape[1:] if s else (N_DEV,) + a.shape, a.dtype)
                          for a, s in zip(blocks, second)]
        self.scratch = _sems(len(blocks))

    def _copies(self, refs):
        x_refs, out_refs, (send_sems, recv_sems, local_sems) = refs
        n = len(x_refs)
        x, y, c = _place()
        me, sibling = (x, y, c), (x, y, 1 - c)
        chips = [(1 - x, y), (x, 1 - y), (1 - x, 1 - y)]

        def slot(j, px, py, pc):
            idx = 4 * px + 2 * py + pc
            return out_refs[j].at[:, idx] if self.second[j] else out_refs[j].at[idx]

        def copies(k, block, to, own=False):
            return [pltpu.make_async_remote_copy(
                src_ref=x_refs[j] if own else slot(j, *block), dst_ref=slot(j, *block),
                send_sem=send_sems.at[j, k], recv_sem=recv_sems.at[j, k], device_id=to,
                device_id_type=pl.DeviceIdType.MESH) for j in range(n)]

        mine = [pltpu.make_async_copy(x_refs[j], slot(j, *me), local_sems.at[j]) for j in range(n)]
        first = copies(0, me, sibling, own=True)
        for j, chip in enumerate(chips):
            first += copies(1 + j, me, (*chip, c), own=True)
        landed = [copies(1 + j, (*chip, c), me) for j, chip in enumerate(chips)]
        onward = [copies(4 + j, (*chip, c), sibling) for j, chip in enumerate(chips)]
        from_sibling = copies(0, sibling, me)
        for j, chip in enumerate(chips):
            from_sibling += copies(4 + j, (*chip, 1 - c), me)
        return mine, first, landed, onward, from_sibling

    def start(self, refs):
        mine, first, _, _, _ = self._copies(refs)
        for cp in mine + first:
            cp.start()

    def forward(self, refs):
        _, _, landed, onward, _ = self._copies(refs)
        for arrived, passed in zip(landed, onward):
            for cp in arrived:
                cp.wait_recv()
            for cp in passed:
                cp.start()

    def finish(self, refs):
        mine, first, _, onward, from_sibling = self._copies(refs)
        for cp in from_sibling:
            cp.wait_recv()
        for cp in first + [cp for passed in onward for cp in passed]:
            cp.wait_send()
        for cp in mine:
            cp.wait()


class _Exchange:
    forward_at = 1.0

    def __init__(self, sends):
        self.inputs = list(sends)
        self.out_shape = [jax.ShapeDtypeStruct(a.shape, a.dtype) for a in sends]
        self.scratch = _sems(len(sends))

    def _copies(self, refs):
        x_refs, out_refs, (send_sems, recv_sems, local_sems) = refs
        n = len(x_refs)
        x, y, c = _place()
        me = 4 * x + 2 * y + c
        sent, landing = [], []
        for k in range(1, N_DEV):
            px, py, pc = x ^ ((k >> 2) & 1), y ^ ((k >> 1) & 1), c ^ (k & 1)
            them = 4 * px + 2 * py + pc
            for j in range(n):
                for here, there, into in ((them, me, sent), (me, them, landing)):
                    into.append(pltpu.make_async_remote_copy(
                        src_ref=x_refs[j].at[here], dst_ref=out_refs[j].at[there], send_sem=send_sems.at[j, k - 1],
                        recv_sem=recv_sems.at[j, k - 1], device_id=(px, py, pc), device_id_type=pl.DeviceIdType.MESH))
        mine = [pltpu.make_async_copy(x_refs[j].at[me], out_refs[j].at[me], local_sems.at[j]) for j in range(n)]
        return mine, sent, landing

    def start(self, refs):
        mine, sent, _ = self._copies(refs)
        for cp in mine + sent:
            cp.start()

    def forward(self, refs):
        pass

    def finish(self, refs):
        mine, sent, landing = self._copies(refs)
        for cp in landing:
            cp.wait_recv()
        for cp in sent:
            cp.wait_send()
        for cp in mine:
            cp.wait()


def _run(rider, name):
    n_in, n_out = len(rider.inputs), len(rider.out_shape)

    def body(*refs):
        parts = (refs[:n_in], refs[n_in:n_in + n_out], refs[n_in + n_out:])
        rider.start(parts)
        rider.forward(parts)
        rider.finish(parts)

    return pl.pallas_call(body, name=name, out_shape=rider.out_shape, in_specs=[_ANY] * n_in, out_specs=[_ANY] * n_out,
                          scratch_shapes=rider.scratch)(*rider.inputs)


def _split_refs(refs, n_in, n_out, n_scratch, rider):
    r_in = len(rider.inputs) if rider else 0
    r_out = len(rider.out_shape) if rider else 0
    a = n_in + r_in
    b = a + n_out + r_out
    own = refs[:n_in] + refs[a:a + n_out] + refs[b:b + n_scratch]
    return own, (refs[n_in:a], refs[a + n_out:b], refs[b + n_scratch:])


def _grid_step(grid):
    step, stride = 0, 1
    for axis in reversed(range(len(grid))):
        step = step + pl.program_id(axis) * stride
        stride *= grid[axis]
    return step


def _ride_begin(rider, ride_refs, grid):
    if rider is not None:
        pl.when(_grid_step(grid) == 0)(lambda: rider.start(ride_refs))


def _ride_end(rider, ride_refs, grid):
    if rider is not None:
        steps = 1
        for n in grid:
            steps *= n
        step = _grid_step(grid)
        pl.when(step == min(steps - 1, int(steps * rider.forward_at)))(lambda: rider.forward(ride_refs))
        pl.when(step == steps - 1)(lambda: rider.finish(ride_refs))


def _host(rider, in_specs, out_specs, out_shape, scratch, operands):
    if rider is None:
        return dict(in_specs=in_specs, out_specs=out_specs, out_shape=out_shape, scratch_shapes=scratch), operands
    return dict(in_specs=in_specs + [_ANY] * len(rider.inputs), out_specs=out_specs + [_ANY] * len(rider.out_shape),
                out_shape=out_shape + rider.out_shape, scratch_shapes=scratch + rider.scratch), operands + rider.inputs


_LATE = ("w_gate_up", "w_out", "w_down")


def _local_grads(x, target, w, late=None, exchange=False):
    w = dict(w)
    w_in = list(w["w_in"])
    dn_params = (w["dn_a_log"], w["dn_dt_bias"], w["dn_norm_w"])
    gla_params = (w["gla_w_gate2"], w["gla_gate_bias"], w["gla_norm_w"])
    sgu_params = (w["sgu_ln_w"], w["sgu_ln_b"], w["sgu_w_spatial"], w["sgu_b_spatial"])
    saved = []
    for l in range(DEPTH):
        riding = l == 0 and late is not None
        h, p, *got = _in_proj(x, w["norm1_w"], w_in[l], l,
                              rider=_Gather([late["w_down"]], [True], forward_at=0.85) if riding else None)
        if riding:
            w["w_down"] = got[0].reshape(DEPTH, FF, D)
        ya = _sgu_fwd(p, *sgu_params, l)
        yb = _sconv_fwd(p, w["sc_conv_w"], l)
        qkv_c = _qkv_conv_fwd(p, w["dn_conv_w"], l)
        yc, st_c, inv_c, *got = _dn_fwd(
            qkv_c, p, dn_params, l,
            rider=_Gather([late["w_gate_up"], late["w_out"]], [False, True], forward_at=0.85) if riding else None)
        if riding:
            w.update(w_gate_up=got[0], w_out=got[1].reshape(DEPTH, D, D))
        yd, st_d, *got = _gla_fwd(p, gla_params, l,
                                  rider=_Gather([late["w_in"]], [False], forward_at=0.7) if riding else None)
        if riding:
            w_in[1] = _relayout_w_in(got[0])
        mix, x1, h2, gu, act, x2 = _out_mlp(x, (ya, yb, yc, yd), w["w_out"], w["norm2_w"], w["w_gate_up"], w["w_down"], l)
        saved.append(dict(x=x, h=h, p=p, qkv_c=qkv_c, st_c=st_c, inv_c=inv_c, st_d=st_d, mix=mix, x1=x1, h2=h2, gu=gu,
                          act=act))
        x = x2
    loss, d_final, dx = _loss_head(x, w["final_norm_w"], target)
    large = {k: [None] * DEPTH for k in ("w_in", "w_out", "w_gate_up", "w_down")}
    small = [None] * DEPTH
    for l in reversed(range(DEPTH)):
        s = saved[l]
        p = s["p"]
        g = {}
        riding = exchange and l == 0

        def exchanging(pairs):
            return _Exchange([large[k][j] for k, j in pairs]) if riding else None

        def arrive(pairs, arrived):
            for (k, j), a in zip(pairs, arrived):
                large[k][j] = a

        pairs = (("w_gate_up", 1), ("w_down", 1))
        dgu, dx1, dmix, g["norm2_w"], *arrived = _mlp_out_bwd(
            dx, s["x1"], s["gu"], w["norm2_w"], w["w_down"], w["w_gate_up"], w["w_out"], l, rider=exchanging(pairs))
        arrive(pairs, arrived)
        pairs = (("w_in", 1), ("w_out", 1))
        d_gu, *arrived = _wgrad_rows(dgu, s["h2"], "wgrad_gate_up", rider=exchanging(pairs)) if riding else (
            _wgrad_rows(dgu, s["h2"], "wgrad_gate_up"),)
        arrive(pairs, arrived)
        large["w_gate_up"][l] = d_gu.reshape(N_DEV, GU_SHARD, D)
        large["w_down"][l] = _wgrad_rows(s["act"], dx, "wgrad_down").reshape(N_DEV, FF // N_DEV, D)
        large["w_out"][l] = _wgrad_rows(s["mix"], dx1, "wgrad_out", rows=D).reshape(N_DEV, D // N_DEV, D)
        d_a, g["sgu_ln_w"], g["sgu_ln_b"], g["sgu_w_spatial"], g["sgu_b_spatial"] = _sgu_bwd(p, dmix, *sgu_params, l)
        d_b, g["sc_conv_w"] = _sconv_bwd(p, dmix, w["sc_conv_w"], l)
        pairs = tuple((k, 0) for k in _LATE)
        dqkv_c, dz_c, dab, g["dn_a_log"], g["dn_dt_bias"], g["dn_norm_w"], *arrived = _dn_bwd(
            s["qkv_c"], p, s["st_c"], s["inv_c"], dmix, dn_params, l, rider=exchanging(pairs))
        arrive(pairs, arrived)
        d_c, g["dn_conv_w"] = _qkv_conv_bwd(p, dqkv_c, w["dn_conv_w"], l)
        d_d, dz_d, dglr, g["gla_w_gate2"], g["gla_gate_bias"], g["gla_norm_w"] = _gla_bwd(p, s["st_d"], dmix, gla_params, l)
        dp, dx, g["norm1_w"] = _in_proj_bwd((d_c, d_d, d_a, d_b, dz_c, dz_d, dab, dglr), s["x"], dx1, w["norm1_w"],
                                            w_in[l], l)
        small[l] = g
        w_in_rows = dict(tt=WGRAD_TOKENS // 2, rows=P // 2, out_dtype=F32)
        if riding:
            d_w_in, small = _wgrad_rows(dp, s["h"], "wgrad_in", **w_in_rows,
                                        rider=_Gather([_pack_small(small, d_final)], [False], forward_at=0.75))
        else:
            d_w_in = _wgrad_rows(dp, s["h"], "wgrad_in", **w_in_rows)
        large["w_in"][l] = _scatter_w_in_grad(d_w_in)
    return loss[0, 0], dx, large, small, d_final


_ORDER = ("norm1_w", "w_in", "sgu_ln_w", "sgu_ln_b", "sgu_w_spatial", "sgu_b_spatial", "sc_conv_w", "dn_conv_w",
          "dn_a_log", "dn_dt_bias", "dn_norm_w", "gla_w_gate2", "gla_gate_bias", "gla_norm_w", "w_out", "norm2_w",
          "w_gate_up", "w_down", "final_norm_w")
_LARGE = ("w_in", "w_gate_up", "w_out", "w_down")
_LARGE_TILE = {"w_gate_up": 352, "w_out": 128, "w_down": 352}


def _gather_cols(g):
    return jnp.transpose(g, (1, 2, 0, 3)).reshape(g.shape[1], g.shape[2], N_DEV * g.shape[3])


def kernel(x, norm1_w, w_in, sgu_ln_w, sgu_ln_b, sgu_w_spatial, sgu_b_spatial, sc_conv_w, dn_conv_w, dn_a_log, dn_dt_bias, dn_norm_w, gla_w_gate2, gla_gate_bias, gla_norm_w, w_out, norm2_w, w_gate_up, w_down, final_norm_w, loss_target, m_norm1_w, m_w_in, m_sgu_ln_w, m_sgu_ln_b, m_sgu_w_spatial, m_sgu_b_spatial, m_sc_conv_w, m_dn_conv_w, m_dn_a_log, m_dn_dt_bias, m_dn_norm_w, m_gla_w_gate2, m_gla_gate_bias, m_gla_norm_w, m_w_out, m_norm2_w, m_w_gate_up, m_w_down, m_final_norm_w, v_norm1_w, v_w_in, v_sgu_ln_w, v_sgu_ln_b, v_sgu_w_spatial, v_sgu_b_spatial, v_sc_conv_w, v_dn_conv_w, v_dn_a_log, v_dn_dt_bias, v_dn_norm_w, v_gla_w_gate2, v_gla_gate_bias, v_gla_norm_w, v_w_out, v_norm2_w, v_w_gate_up, v_w_down, v_final_norm_w):
    args = dict(locals())
    wts = {k: args[k] for k in _ORDER}
    mom = {k: args["m_" + k] for k in _ORDER}
    var = {k: args["v_" + k] for k in _ORDER}
    for d in (wts, mom, var):
        d["final_norm_w"] = d["final_norm_w"][None]
    me = 4 * lax.axis_index("x") + 2 * lax.axis_index("y") + lax.axis_index("c")
    for d in (wts, mom, var):
        d["w_gate_up"] = jnp.swapaxes(d["w_gate_up"], 1, 2)

    late = {k: wts[k].astype(BF16) for k in _LATE}
    w_in = wts["w_in"].astype(BF16)
    late["w_in"] = w_in[1]
    got = _run(_Gather([w_in[0]] + [wts[k] for k in _SHARDED_SMALL], [False] * 4), "gather_w_in")
    full = dict(wts)
    full["w_in"] = [_relayout_w_in(got[0]), None]
    for k, g in zip(_SHARDED_SMALL, got[1:]):
        full[k] = _gather_cols(g)

    loss, dx, large, small, d_final = _local_grads(x[0], loss_target[0], full, late=late, exchange=True)
    loss = lax.psum(loss, ("x", "y", "c"))

    large["w_in"][0], = _run(_Exchange([large["w_in"][0]]), "exchange_grads")
    result = {}
    for k in ("w_gate_up", "w_out", "w_down"):
        outs = _adamw_large(large[k], wts[k], mom[k], var[k], "adamw_" + k, _LARGE_TILE[k])
        for kind, a in zip(("grad", "delta", "new_m", "new_v"), outs):
            result[kind, k] = jnp.swapaxes(a, 1, 2) if k == "w_gate_up" else a
    outs = _adamw_w_in(large["w_in"], *[jnp.transpose(d["w_in"], (2, 0, 1)) for d in (wts, mom, var)])
    for kind, a in zip(("grad", "delta", "new_m", "new_v"), outs):
        result[kind, "w_in"] = jnp.transpose(a, (1, 2, 0))

    slabs = small
    rep = _REPLICATED + ("final_norm_w",)
    outs, summed = _adamw_small(slabs, {k: wts[k] for k in rep}, {k: mom[k] for k in rep}, {k: var[k] for k in rep})
    for kind, d in zip(("grad", "delta", "new_m", "new_v"), outs):
        for k, a in d.items():
            result[kind, k] = a[0] if k == "final_norm_w" else a
    own = {k: lax.dynamic_slice_in_dim(summed[k], me * wts[k].shape[-1], wts[k].shape[-1], axis=2) for k in _SHARDED_SMALL}
    outs = _adamw_shards(own, {k: wts[k] for k in _SHARDED_SMALL}, {k: mom[k] for k in _SHARDED_SMALL},
                         {k: var[k] for k in _SHARDED_SMALL})
    for kind, d in zip(("grad", "delta", "new_m", "new_v"), [own] + outs):
        for k, a in d.items():
            result[kind, k] = a

    return (loss, dx[None], *[result[kind, k] for kind in ("grad", "delta", "new_m", "new_v") for k in _ORDER])
```

```python
import functools

import jax
import jax.numpy as jnp
from jax import lax
from jax.experimental import pallas as pl
from jax.experimental.pallas import tpu as pltpu

F32, BF16 = jnp.float32, jnp.bfloat16
DEPTH = 2
D = 1024
G = 256
NH, HD = 4, 64
FF = 2816
IN_COLS = 3352
P = 3584
EPS = 1e-6
SGU_C, DN_C, GLA_C = 128, 64, 64
N_DEV = 8
IN_SHARD = IN_COLS // N_DEV
GU_SHARD = 2 * FF // N_DEV
LANES = 128
VMEM_LIMIT = 56 * 2 ** 20

_PAD_SEGS = ((1280, 2048, 0),
             (2312, 3080, 0),
             (0, 512, 0),
             (512, 1280, 0),
             (2056, 2312, 0),
             (3096, 3352, 0),
             (2048, 2056, 120),
             (3080, 3096, 112))

ADAM_LR, ADAM_B1, ADAM_B2, ADAM_EPS, ADAM_WD, ADAM_STEP = 0.001, 0.9, 0.999, 1e-08, 0.01, 10


def _cparams(*sem):
    return pltpu.CompilerParams(dimension_semantics=sem, vmem_limit_bytes=VMEM_LIMIT)


def _resident(shape):
    return pl.BlockSpec(shape, lambda *_: (0,) * len(shape), pipeline_mode=pl.Buffered(1))


def _layer(shape, l):
    return pl.BlockSpec((None,) + tuple(shape), lambda *_: (l,) + (0,) * len(shape), pipeline_mode=pl.Buffered(1))


def _acc(shape):
    return pl.BlockSpec(shape, lambda *_: (0,) * len(shape))


def _dg(a, b, ca, cb):
    lead = a.ndim - 2
    batch = ((0,), (0,)) if lead else ((), ())
    return lax.dot_general(a, b, (((ca + lead,), (cb + lead,)), batch), preferred_element_type=F32)


def _split(t):
    hi = t.astype(BF16)
    return hi, (t - hi.astype(F32)).astype(BF16)


def _dg3(a, b, ca, cb):
    (ah, al), (bh, bl) = a, b
    return _dg(ah, bh, ca, cb) + (_dg(ah, bl, ca, cb) + _dg(al, bh, ca, cb))


def _make_dot(accurate):
    def raw(a, b, form):
        ca = 0 if form == "tn" else 1
        cb = 1 if form == "nt" else 0
        if accurate:
            return _dg3(_split(a), _split(b), ca, cb)
        return _dg(a.astype(BF16), b.astype(BF16), ca, cb)

    @functools.partial(jax.custom_vjp, nondiff_argnums=(2,))
    def dot(a, b, form):
        return raw(a, b, form)

    def fwd(a, b, form):
        return raw(a, b, form), (a, b)

    def bwd(form, res, g):
        a, b = res
        if form == "nn":
            return raw(g, b, "nt"), raw(a, g, "tn")
        if form == "nt":
            return raw(g, b, "nn"), raw(g, a, "tn")
        return raw(b, g, "nt"), raw(a, g, "nn")

    dot.defvjp(fwd, bwd)
    return dot


_bdot = _make_dot(False)
_hdot = _make_dot(True)


def _inv_unit_lower(low):
    n = low.shape[-1]
    eye = (lax.broadcasted_iota(jnp.int32, (n, n), 0) == lax.broadcasted_iota(jnp.int32, (n, n), 1)).astype(F32)
    p = -low
    inv = eye + p
    ps = _split(p)
    k = 1
    while 2 * k < n:
        ps = _split(_dg3(ps, ps, 1, 0))
        inv = inv + _dg3(_split(inv), ps, 1, 0)
        k *= 2
    return inv


@jax.custom_vjp
def _unit_lower_solve(low, rhs, inv):
    return _dg3(_split(inv), _split(rhs), 1, 0)


def _uls_fwd(low, rhs, inv):
    sol = _dg3(_split(inv), _split(rhs), 1, 0)
    return sol, (inv, sol)


def _uls_bwd(res, g):
    inv, sol = res
    d_rhs = _dg3(_split(inv), _split(g), 0, 0)
    return -_dg3(_split(d_rhs), _split(sol), 1, 1), d_rhs, jnp.zeros_like(inv)


_unit_lower_solve.defvjp(_uls_fwd, _uls_bwd)


def _sigmoid(x):
    return 0.5 * jnp.tanh(0.5 * x) + 0.5


def _softplus(x):
    return jnp.maximum(x, 0.0) + jnp.log(1.0 + jnp.exp(-jnp.maximum(x, -x)))


def _gelu(x):
    return 0.5 * x * (1.0 + jnp.tanh(0.7978845608028654 * (x + 0.044715 * (x * x * x))))


def _tri(n):
    ri = lax.broadcasted_iota(jnp.int32, (n, n), 0)
    ci = lax.broadcasted_iota(jnp.int32, (n, n), 1)
    return ri, ci


def _stack(ts):
    return jnp.concatenate([t[None] for t in ts], axis=0)


@jax.custom_vjp
def _head_sums(x):
    ri, ci = _tri(x.shape[-1])
    shift = HD.bit_length() - 1
    ones = (jnp.right_shift(ri, shift) == jnp.right_shift(ci, shift)).astype(BF16)
    hi, lo = _split(x)
    return _dg(hi, ones, 1, 0) + _dg(lo, ones, 1, 0)


_head_sums.defvjp(lambda x: (_head_sums(x), None), lambda _, g: (_head_sums(g),))


def _chunk_mask_dot(x, c, running, transpose):
    ri, ci = _tri(x.shape[0])
    shift = c.bit_length() - 1
    mask = jnp.right_shift(ri, shift) == jnp.right_shift(ci, shift)
    if running:
        mask = mask & (ri >= ci)
    hi, lo = _split(x)
    m = mask.astype(BF16)
    ca = 0 if transpose else 1
    return _dg(m, hi, ca, 0) + _dg(m, lo, ca, 0)


@functools.partial(jax.custom_vjp, nondiff_argnums=(1, 2))
def _chunk_sums(x, c, running):
    return _chunk_mask_dot(x, c, running, False)


_chunk_sums.defvjp(lambda x, c, running: (_chunk_mask_dot(x, c, running, False), None),
                   lambda c, running, _, g: (_chunk_mask_dot(g, c, running, True),))


def _spread_onto(x, first, transpose):
    ri = lax.broadcasted_iota(jnp.int32, (LANES, G), 0)
    ci = lax.broadcasted_iota(jnp.int32, (LANES, G), 1)
    sel = (ri == first + jnp.right_shift(ci, HD.bit_length() - 1)).astype(BF16)
    hi, lo = _split(x)
    cb = 1 if transpose else 0
    return _dg(hi, sel, 1, cb) + _dg(lo, sel, 1, cb)


@functools.partial(jax.custom_vjp, nondiff_argnums=(1,))
def _spread(x, first):
    return _spread_onto(x, first, False)


_spread.defvjp(lambda x, first: (_spread_onto(x, first, False), None),
               lambda first, _, g: (_spread_onto(g, first, True),))


def _head_columns_dot(x, transpose):
    sel = (lax.broadcasted_iota(jnp.int32, (NH, G), 0)
           == jnp.right_shift(lax.broadcasted_iota(jnp.int32, (NH, G), 1), HD.bit_length() - 1)).astype(BF16)
    hi, lo = _split(x)
    if transpose:
        return _dg(sel, hi, 1, 1) + _dg(sel, lo, 1, 1)
    return _dg(hi, sel, 0, 0) + _dg(lo, sel, 0, 0)


@jax.custom_vjp
def _head_columns(x):
    return _head_columns_dot(x, False)


_head_columns.defvjp(lambda x: (_head_columns_dot(x, False), None), lambda _, g: (_head_columns_dot(g, True),))


def _sgu_chunk(uv, ln_w, ln_b, ws, bs):
    u = _gelu(uv[:, :G])
    v = _gelu(uv[:, G:])
    mu = jnp.mean(v, axis=-1, keepdims=True)
    vc = v - mu
    var = jnp.mean(vc * vc, axis=-1, keepdims=True)
    vn = vc * lax.rsqrt(var + EPS) * ln_w + ln_b
    ri, ci = _tri(SGU_C)
    mixed = [_bdot(jnp.where(ri >= ci, ws[h], 0.0), vn[:, HD * h:HD * (h + 1)], "nn") for h in range(NH)]
    return u * (jnp.concatenate(mixed, axis=1) + _head_columns(bs))


def _dn_tile(q, k, v, ab, z, state, inv, a_log, dt_bias, nw):
    c = DN_C
    tm = q.shape[0]
    cps = tm // c
    ri, ci = _tri(tm)
    shift = c.bit_length() - 1
    same = jnp.right_shift(ri, shift) == jnp.right_shift(ci, shift)
    g4 = -jnp.exp(a_log) * _softplus(ab[:, 0:NH] + dt_bias)
    gates = jnp.concatenate([g4, _sigmoid(ab[:, NH:2 * NH]), jnp.zeros((tm, LANES - 2 * NH), F32)], axis=1)
    g, beta = _spread(gates, 0), _spread(gates, NH)
    gc = _chunk_sums(g, c, True)
    gl = _chunk_sums(g, c, False)
    gr4 = _hdot(g4, (same & (ri <= ci)).astype(F32), "tn")
    pairs = [(slice(c * j, c * (j + 1)), h) for j in range(cps) for h in range(NH)]
    heads = lambda t: _stack([t[rows, HD * h:HD * (h + 1)] for rows, h in pairs])
    qn = q * lax.rsqrt(_head_sums(q * q) + EPS) * (HD ** -0.5)
    kn = k * lax.rsqrt(_head_sums(k * k) + EPS)
    eg = jnp.exp(gc)
    kb = kn * beta
    rhs = jnp.concatenate([heads(v * beta), heads(kb * eg)], axis=2)
    qg, kd = heads(qn * eg), heads(kn * jnp.exp(gl - gc))
    cd = _stack([jnp.exp(gl[c * j:c * j + 1, HD * h:HD * (h + 1)]) for j in range(cps) for h in range(NH)])
    qn, kn, kb, gc = heads(qn), heads(kn), heads(kb), heads(gc)
    gr = _stack([gr4[h:h + 1, rows] for rows, h in pairs])
    r2, c2 = _tri(c)
    decay = jnp.exp(jnp.where(r2 >= c2, gc - gr, -jnp.inf))
    low = jnp.where(r2 > c2, _bdot(kb, kn, "nt") * decay, 0.0)
    if inv is None:
        inv = _inv_unit_lower(low)
    sol = _unit_lower_solve(low, rhs, inv)
    u, w = sol[:, :, :HD], sol[:, :, HD:]
    attn = _bdot(qn, kn, "nt") * decay
    ys = []
    for j in range(cps):
        b = slice(NH * j, NH * (j + 1))
        v_new = u[b] - _bdot(w[b], state, "nn")
        o = _bdot(qg[b], state, "nn") + _bdot(attn[b], v_new, "nn")
        state = state * cd[b] + _bdot(kd[b], v_new, "tn")
        on = o * lax.rsqrt(jnp.mean(o * o, axis=-1, keepdims=True) + EPS) * nw
        ys.append(jnp.concatenate([on[h] for h in range(NH)], axis=1))
    return jnp.concatenate(ys, axis=0) * (z * _sigmoid(z)), state, inv


def _gla_tile(q, k, v, glr, z, state_t, w2, gate_bias, nw):
    c = GLA_C
    tm = q.shape[0]
    cps = tm // c
    w2_rows = jnp.concatenate([w2, jnp.zeros((LANES - w2.shape[0], G), F32)], axis=0)
    pre = _bdot(glr, w2_rows, "nn") + gate_bias
    log_a = (jnp.minimum(pre, 0.0) - jnp.log(1.0 + jnp.exp(-jnp.maximum(pre, -pre)))) * (1.0 / 16.0)
    gcum = _chunk_sums(log_a, c, True)
    row = lax.broadcasted_iota(jnp.int32, (c, G), 0)
    qs = q * (HD ** -0.5)
    qa, ka, qg, kl, dec = [], [], [], [], []
    for j in range(cps):
        rows = slice(c * j, c * (j + 1))
        gj = gcum[rows]
        g_mid = jnp.sum(jnp.where(row == c // 2, gj, 0.0), axis=0, keepdims=True)
        g_last = jnp.sum(log_a[rows], axis=0, keepdims=True)
        qa.append(qs[rows] * jnp.exp(gj - g_mid))
        ka.append(k[rows] * jnp.exp(g_mid - gj))
        qg.append(qs[rows] * jnp.exp(gj))
        kl.append(k[rows] * jnp.exp(g_last - gj))
        dec.append(jnp.exp(g_last))
    heads = lambda ts: _stack([t[:, HD * h:HD * (h + 1)] for t in ts for h in range(NH)])
    qa, ka, qg, kl, dec = heads(qa), heads(ka), heads(qg), heads(kl), heads(dec)
    vh = heads([v[c * j:c * (j + 1)] for j in range(cps)])
    r2, c2 = _tri(c)
    o_intra = _bdot(jnp.where(r2 >= c2, _bdot(qa, ka, "nt"), 0.0), vh, "nn")
    ys = []
    for j in range(cps):
        b = slice(NH * j, NH * (j + 1))
        o = o_intra[b] + _bdot(qg[b], state_t, "nt")
        state_t = state_t * dec[b] + _bdot(vh[b], kl[b], "tn")
        on = o * lax.rsqrt(jnp.mean(o * o, axis=-1, keepdims=True) + EPS) * nw
        ys.append(jnp.concatenate([on[h] for h in range(NH)], axis=1))
    return jnp.concatenate(ys, axis=0) * (z * _sigmoid(z)), state_t


def _rms(x, nw):
    r = lax.rsqrt(jnp.mean(x * x, axis=-1, keepdims=True) + EPS)
    xh = x * r
    return xh * nw, xh, r


def _rms_bwd(g, xh, r, nw):
    gw = g * nw
    return r * (gw - xh * jnp.mean(gw * xh, axis=-1, keepdims=True)), jnp.sum(g * xh, axis=0, keepdims=True)


def _row(tm, w, blk=0):
    return pl.BlockSpec((tm, w), lambda i: (i, blk))


def _in_proj(x, nw, wp, l, tm=512, rider=None):
    t = x.shape[0]

    def body(*refs):
        (x_ref, nw_ref, w_ref, h_ref, p_ref), ride = _split_refs(refs, 3, 2, 0, rider)
        _ride_begin(rider, ride, (t // tm,))
        h = _rms(x_ref[...], nw_ref[l:l + 1, :])[0].astype(BF16)
        h_ref[...] = h
        p_ref[...] = jnp.dot(h, w_ref[...], preferred_element_type=F32)
        _ride_end(rider, ride, (t // tm,))

    specs, operands = _host(
        rider, [_row(tm, D), _resident((DEPTH, D)), _resident((D, P))], [_row(tm, D), _row(tm, P)],
        [jax.ShapeDtypeStruct((t, D), BF16), jax.ShapeDtypeStruct((t, P), F32)], [], [x, nw, wp])
    return pl.pallas_call(body, name="in_proj", grid=(t // tm,), compiler_params=_cparams("arbitrary"),
                          **specs)(*operands)


def _gu_spec(l):
    return pl.BlockSpec((N_DEV, None, GU_SHARD, D), lambda *_: (0, l, 0, 0), pipeline_mode=pl.Buffered(1))


def _out_mlp(x, ys, w_out, nw2, w_gu_t, w_down, l, tm=256):
    t = x.shape[0]

    def body(x_ref, ya, yb, yc, yd, wo_ref, nw_ref, wgu_ref, wd_ref, mix_ref, x1_ref, h2_ref, gu_ref, act_ref, x2_ref):
        mix = jnp.concatenate([ya[...], yb[...], yc[...], yd[...]], axis=1)
        mix_ref[...] = mix
        x1 = x_ref[...] + jnp.dot(mix, wo_ref[...], preferred_element_type=F32)
        x1_ref[...] = x1
        h2 = _rms(x1, nw_ref[l:l + 1, :])[0].astype(BF16)
        h2_ref[...] = h2
        gu = _dg(h2, wgu_ref[...].reshape(2 * FF, D), 1, 1)
        gu_ref[...] = gu.astype(BF16)
        gate, up = gu[:, :FF], gu[:, FF:]
        act = (gate * _sigmoid(gate) * up).astype(BF16)
        act_ref[...] = act
        x2_ref[...] = x1 + jnp.dot(act, wd_ref[...], preferred_element_type=F32)

    return pl.pallas_call(
        body, name="out_mlp", grid=(t // tm,),
        in_specs=[_row(tm, D), _row(tm, G), _row(tm, G), _row(tm, G), _row(tm, G), _layer((D, D), l),
                  _resident((DEPTH, D)), _gu_spec(l), _layer((FF, D), l)],
        out_specs=[_row(tm, D), _row(tm, D), _row(tm, D), _row(tm, 2 * FF), _row(tm, FF), _row(tm, D)],
        out_shape=[jax.ShapeDtypeStruct((t, D), BF16), jax.ShapeDtypeStruct((t, D), F32),
                   jax.ShapeDtypeStruct((t, D), BF16), jax.ShapeDtypeStruct((t, 2 * FF), BF16),
                   jax.ShapeDtypeStruct((t, FF), BF16), jax.ShapeDtypeStruct((t, D), F32)],
        compiler_params=_cparams("parallel"))(x, *ys, w_out, nw2, w_gu_t, w_down)


def _loss_head(x, nw, target, tm=512):
    t = x.shape[0]

    def body(x_ref, nw_ref, tg_ref, loss_ref, dnw_ref, dx_ref):
        @pl.when(pl.program_id(0) == 0)
        def _():
            loss_ref[...] = jnp.zeros_like(loss_ref)
            dnw_ref[...] = jnp.zeros_like(dnw_ref)
        nw_v = nw_ref[...]
        y, xh, r = _rms(x_ref[...], nw_v)
        err = y - tg_ref[...]
        loss_ref[...] += 0.5 * jnp.sum(err * err) * (1.0 / D)
        dx, dnw = _rms_bwd(err * (1.0 / D), xh, r, nw_v)
        dx_ref[...] = dx
        dnw_ref[...] += dnw

    return pl.pallas_call(
        body, name="loss_head", grid=(t // tm,),
        in_specs=[_row(tm, D), _resident((1, D)), _row(tm, D)],
        out_specs=[_acc((8, LANES)), _acc((1, D)), _row(tm, D)],
        out_shape=[jax.ShapeDtypeStruct((8, LANES), F32), jax.ShapeDtypeStruct((1, D), F32),
                   jax.ShapeDtypeStruct((t, D), F32)],
        compiler_params=_cparams("arbitrary"))(x, nw, target)


def _mlp_out_bwd(dx2, x1, gu, nw2, w_down, w_gu, w_out, l, tm=256, rider=None):
    t = dx2.shape[0]

    def body(*refs):
        (dx2_ref, x1_ref, gu_ref, nw_ref, wd_ref, wgu_ref, wo_ref, dgu_ref, dx1_ref, dmix_ref, dnw_ref), ride = \
            _split_refs(refs, 7, 4, 0, rider)
        _ride_begin(rider, ride, (t // tm,))

        @pl.when(pl.program_id(0) == 0)
        def _():
            dnw_ref[...] = jnp.zeros_like(dnw_ref)
        dx2 = dx2_ref[...]
        dact = _dg(dx2.astype(BF16), wd_ref[...], 1, 1)
        gu = gu_ref[...].astype(F32)
        gate, up = gu[:, :FF], gu[:, FF:]
        s = _sigmoid(gate)
        dgu = jnp.concatenate([dact * up * (s * (1.0 + gate * (1.0 - s))), dact * (gate * s)], axis=1).astype(BF16)
        dgu_ref[...] = dgu
        dh2 = jnp.dot(dgu, wgu_ref[...].reshape(2 * FF, D), preferred_element_type=F32)
        nw_v = nw_ref[l:l + 1, :]
        _, xh, r = _rms(x1_ref[...], nw_v)
        dxn, dnw = _rms_bwd(dh2, xh, r, nw_v)
        dx1 = dx2 + dxn
        dx1_ref[...] = dx1
        dnw_ref[...] += dnw
        dmix_ref[...] = _dg(dx1.astype(BF16), wo_ref[...], 1, 1)
        _ride_end(rider, ride, (t // tm,))

    specs, operands = _host(
        rider, [_row(tm, D), _row(tm, D), _row(tm, 2 * FF), _resident((DEPTH, D)), _layer((FF, D), l), _gu_spec(l),
                _layer((D, D), l)],
        [_row(tm, 2 * FF), _row(tm, D), _row(tm, D), _acc((1, D))],
        [jax.ShapeDtypeStruct((t, 2 * FF), BF16), jax.ShapeDtypeStruct((t, D), F32),
         jax.ShapeDtypeStruct((t, D), F32), jax.ShapeDtypeStruct((1, D), F32)],
        [], [dx2, x1, gu, nw2, w_down, w_gu, w_out])
    return pl.pallas_call(body, name="mlp_out_bwd", grid=(t // tm,), compiler_params=_cparams("arbitrary"),
                          **specs)(*operands)


_DP_WIDTHS = (768, 768, 512, 768, 256, 256, 128, 128)


def _in_proj_bwd(dps, x, dx1, nw, wp, l, tm=512):
    t = x.shape[0]

    def body(*refs):
        dp_refs, (x_ref, dx1_ref, nw_ref, w_ref, dp_ref, dx_ref, dnw_ref) = refs[:8], refs[8:]

        @pl.when(pl.program_id(0) == 0)
        def _():
            dnw_ref[...] = jnp.zeros_like(dnw_ref)
        dp = jnp.concatenate([r[...] for r in dp_refs], axis=1)
        dp_ref[...] = dp
        dh = _dg(dp, w_ref[...], 1, 1)
        nw_v = nw_ref[l:l + 1, :]
        _, xh, r = _rms(x_ref[...], nw_v)
        dxn, dnw = _rms_bwd(dh, xh, r, nw_v)
        dx_ref[...] = dx1_ref[...] + dxn
        dnw_ref[...] += dnw

    return pl.pallas_call(
        body, name="in_proj_bwd", grid=(t // tm,),
        in_specs=[_row(tm, w) for w in _DP_WIDTHS] + [_row(tm, D), _row(tm, D), _resident((DEPTH, D)), _resident((D, P))],
        out_specs=[_row(tm, P), _row(tm, D), _acc((1, D))],
        out_shape=[jax.ShapeDtypeStruct((t, P), BF16), jax.ShapeDtypeStruct((t, D), F32),
                   jax.ShapeDtypeStruct((1, D), F32)],
        compiler_params=_cparams("arbitrary"))(*dps, x, dx1, nw, wp)


def _accumulate(acc, o_ref, t_axis, term):
    @pl.when(pl.program_id(t_axis) == 0)
    def _():
        acc[...] = jnp.zeros_like(acc)
    acc[...] += term

    @pl.when(pl.program_id(t_axis) == pl.num_programs(t_axis) - 1)
    def _():
        o_ref[...] = acc[...].astype(o_ref.dtype)


WGRAD_TOKENS = 2048


WGRAD_ROWS = 2 * GU_SHARD


def _wgrad_rows(a, b, name, tt=WGRAD_TOKENS, rows=WGRAD_ROWS, out_dtype=BF16, rider=None):
    t, m = a.shape
    tt = min(tt, t)
    grid = (m // rows, t // tt)

    def body(*refs):
        (a_ref, b_ref, o_ref, acc), ride = _split_refs(refs, 2, 1, 1, rider)
        _ride_begin(rider, ride, grid)
        _accumulate(acc, o_ref, 1, _dg(a_ref[...], b_ref[...].astype(BF16), 0, 0))
        _ride_end(rider, ride, grid)

    specs, operands = _host(
        rider, [pl.BlockSpec((tt, rows), lambda j, i: (i, j)), pl.BlockSpec((tt, D), lambda j, i: (i, 0))],
        [pl.BlockSpec((rows, D), lambda j, i: (j, 0))], [jax.ShapeDtypeStruct((m, D), out_dtype)],
        [pltpu.VMEM((rows, D), F32)], [a, b])
    out = pl.pallas_call(body, name=name, grid=grid, compiler_params=_cparams("arbitrary", "arbitrary"), **specs)(*operands)
    return out[0] if rider is None else out


def _relayout_w_in(g, tr=256):
    def body(w_ref, o_ref):
        full = jnp.concatenate([w_ref[d] for d in range(N_DEV)], axis=1)
        parts = []
        for a, b, z in _PAD_SEGS:
            parts.append(full[:, a:b])
            if z:
                parts.append(jnp.zeros((tr, z), full.dtype))
        o_ref[...] = jnp.concatenate(parts, axis=1)

    return pl.pallas_call(
        body, name="relayout_w_in", grid=(D // tr,),
        in_specs=[pl.BlockSpec((N_DEV, tr, IN_SHARD), lambda i: (0, i, 0))], out_specs=_row(tr, P),
        out_shape=jax.ShapeDtypeStruct((D, P), g.dtype), compiler_params=_cparams("parallel"))(g)


IN_ROWS = 432


def _scatter_w_in_grad(gp_t, tc=256):
    def body(g_ref, o_ref):
        g = g_ref[...]
        pieces, off = [], 0
        for a, b, z in _PAD_SEGS:
            pieces.append((a, g[off:off + (b - a)]))
            off += (b - a) + z
        spill = -(-(IN_SHARD * (N_DEV - 1) + IN_ROWS - IN_COLS) // 8) * 8
        nat = jnp.concatenate([piece for _, piece in sorted(pieces, key=lambda ap: ap[0])]
                              + [jnp.zeros((spill, tc), F32)], axis=0)
        for d in range(N_DEV):
            o_ref[d] = nat[IN_SHARD * d:IN_SHARD * d + IN_ROWS].astype(BF16)

    return pl.pallas_call(
        body, name="scatter_w_in_grad", grid=(D // tc,),
        in_specs=[pl.BlockSpec((P, tc), lambda i: (0, i))],
        out_specs=pl.BlockSpec((N_DEV, IN_ROWS, tc), lambda i: (0, 0, i)),
        out_shape=jax.ShapeDtypeStruct((N_DEV, IN_ROWS, D), BF16),
        compiler_params=_cparams("parallel"))(gp_t)


_A_BLK = 3


def _sgu_specs(l):
    return [_resident((DEPTH, G)), _resident((DEPTH, G)), _layer((NH, SGU_C, SGU_C), l), _layer((NH, SGU_C), l)]


def _sgu_fwd(p, ln_w, ln_b, ws, bs, l, tm=256):
    t = p.shape[0]

    def body(uv_ref, lw_ref, lb_ref, ws_ref, bs_ref, y_ref):
        ws_v = [ws_ref[h] for h in range(NH)]
        for c in range(tm // SGU_C):
            rows = pl.ds(c * SGU_C, SGU_C)
            y_ref[rows, :] = _sgu_chunk(uv_ref[rows, :], lw_ref[l:l + 1, :], lb_ref[l:l + 1, :], ws_v,
                                        bs_ref[...]).astype(BF16)

    return pl.pallas_call(
        body, name="sgu_fwd", grid=(t // tm,),
        in_specs=[_row(tm, 2 * G, _A_BLK)] + _sgu_specs(l), out_specs=_row(tm, G),
        out_shape=jax.ShapeDtypeStruct((t, G), BF16),
        compiler_params=_cparams("parallel"))(p, ln_w, ln_b, ws, bs)


def _sgu_bwd(p, dmix, ln_w, ln_b, ws, bs, l, tm=256):
    t = p.shape[0]

    def body(uv_ref, dy_ref, lw_ref, lb_ref, ws_ref, bs_ref, duv_ref, dlw_ref, dlb_ref, dws_ref, dbs_ref):
        @pl.when(pl.program_id(0) == 0)
        def _():
            for r in (dlw_ref, dlb_ref, dws_ref, dbs_ref):
                r[...] = jnp.zeros_like(r)
        ws_v = [ws_ref[h] for h in range(NH)]
        for c in range(tm // SGU_C):
            rows = pl.ds(c * SGU_C, SGU_C)
            _, vjp = jax.vjp(_sgu_chunk, uv_ref[rows, :], lw_ref[l:l + 1, :], lb_ref[l:l + 1, :], ws_v, bs_ref[...])
            duv, dlw, dlb, dws, dbs = vjp(dy_ref[rows, :])
            duv_ref[rows, :] = duv.astype(BF16)
            dlw_ref[...] += dlw
            dlb_ref[...] += dlb
            dbs_ref[...] += dbs
            for h in range(NH):
                dws_ref[h] += dws[h]

    return pl.pallas_call(
        body, name="sgu_bwd", grid=(t // tm,),
        in_specs=[_row(tm, 2 * G, _A_BLK), _row(tm, G, 0)] + _sgu_specs(l),
        out_specs=[_row(tm, 2 * G), _acc((1, G)), _acc((1, G)), _acc((NH, SGU_C, SGU_C)), _acc((NH, SGU_C))],
        out_shape=[jax.ShapeDtypeStruct((t, 2 * G), BF16), jax.ShapeDtypeStruct((1, G), F32),
                   jax.ShapeDtypeStruct((1, G), F32), jax.ShapeDtypeStruct((NH, SGU_C, SGU_C), F32),
                   jax.ShapeDtypeStruct((NH, SGU_C), F32)],
        compiler_params=_cparams("arbitrary"))(p, dmix, ln_w, ln_b, ws, bs)


HALO = 8


def _prev_halo(tm, width, col):
    return pl.BlockSpec((HALO, width), lambda i: (jnp.maximum(i * (tm // HALO) - 1, 0), col))


def _next_halo(tm, width, col, t):
    return pl.BlockSpec((HALO, width), lambda i: (jnp.minimum((i + 1) * (tm // HALO), t // HALO - 1), col))


def _with_prev(halo_ref, x_ref):
    halo = jnp.where(pl.program_id(0) == 0, 0.0, halo_ref[...])
    return jnp.concatenate([halo, x_ref[...]], axis=0)


def _with_next(x_ref, halo_ref):
    halo = jnp.where(pl.program_id(0) == pl.num_programs(0) - 1, 0.0, halo_ref[...])
    return jnp.concatenate([x_ref[...], halo], axis=0)


def _delay(ext, j):
    return ext if j == 0 else pltpu.roll(ext, j, axis=0)


def _advance(ext, j):
    return ext if j == 0 else pltpu.roll(ext, ext.shape[0] - j, axis=0)


def _causal_conv(ext, w, tm):
    kw = w.shape[0]
    out = None
    for k in range(kw):
        term = _delay(ext, kw - 1 - k)[HALO:HALO + tm] * w[k:k + 1, :]
        out = term if out is None else out + term
    return out


def _causal_conv_t(ext, w, tm):
    kw = w.shape[0]
    out = None
    for k in range(kw):
        term = _advance(ext, kw - 1 - k)[0:tm] * w[k:k + 1, :]
        out = term if out is None else out + term
    return out


def _conv_wgrad(ext, dy, kw, tm):
    return jnp.concatenate(
        [jnp.sum(_delay(ext, kw - 1 - k)[HALO:HALO + tm] * dy, axis=0, keepdims=True) for k in range(kw)], axis=0)


_B_GB, _B_GC, _B_H = 8, 9, 10
_C_QKV, _D_QKV = 0, 1
_C_Z, _D_Z = 11, 12
_C_AB, _D_GLR = 26, 27


def _sconv_fwd(p, w, l, tm=512):
    t = p.shape[0]

    def body(gb_ref, gc_ref, h_ref, gc_halo, h_halo, w_ref, y_ref):
        s_ext = _with_prev(gc_halo, gc_ref) * _with_prev(h_halo, h_ref)
        y_ref[...] = (gb_ref[...] * _causal_conv(s_ext, w_ref[...], tm)).astype(BF16)

    return pl.pallas_call(
        body, name="sconv_fwd", grid=(t // tm,),
        in_specs=[_row(tm, G, _B_GB), _row(tm, G, _B_GC), _row(tm, G, _B_H), _prev_halo(tm, G, _B_GC),
                  _prev_halo(tm, G, _B_H), _layer((3, G), l)],
        out_specs=_row(tm, G), out_shape=jax.ShapeDtypeStruct((t, G), BF16),
        compiler_params=_cparams("parallel"))(p, p, p, p, p, w)


def _sconv_bwd(p, dmix, w, l, tm=512):
    t = p.shape[0]

    def body(gb_ref, gc_ref, h_ref, gc_halo, h_halo, gb_next, dy_ref, dy_next, w_ref, dp_ref, dw_ref):
        @pl.when(pl.program_id(0) == 0)
        def _():
            dw_ref[...] = jnp.zeros_like(dw_ref)
        w_v = w_ref[...]
        s_ext = _with_prev(gc_halo, gc_ref) * _with_prev(h_halo, h_ref)
        dout = dy_ref[...]
        d_gb = dout * _causal_conv(s_ext, w_v, tm)
        dconv_ext = _with_next(dy_ref, dy_next) * _with_next(gb_ref, gb_next)
        ds = _causal_conv_t(dconv_ext, w_v, tm)
        dw_ref[...] += _conv_wgrad(s_ext, dconv_ext[0:tm], 3, tm)
        dp_ref[...] = jnp.concatenate([d_gb, ds * h_ref[...], ds * gc_ref[...]], axis=1).astype(BF16)

    return pl.pallas_call(
        body, name="sconv_bwd", grid=(t // tm,),
        in_specs=[_row(tm, G, _B_GB), _row(tm, G, _B_GC), _row(tm, G, _B_H), _prev_halo(tm, G, _B_GC),
                  _prev_halo(tm, G, _B_H), _next_halo(tm, G, _B_GB, t), _row(tm, G, 1), _next_halo(tm, G, 1, t),
                  _layer((3, G), l)],
        out_specs=[_row(tm, 3 * G), _acc((3, G))],
        out_shape=[jax.ShapeDtypeStruct((t, 3 * G), BF16), jax.ShapeDtypeStruct((3, G), F32)],
        compiler_params=_cparams("arbitrary"))(p, p, p, p, p, p, dmix, dmix, w)


def _qkv_conv_fwd(p, w, l, tm=512):
    t = p.shape[0]

    def body(x_ref, x_halo, w_ref, y_ref):
        c = _causal_conv(_with_prev(x_halo, x_ref), w_ref[...], tm)
        y_ref[...] = c * _sigmoid(c)

    return pl.pallas_call(
        body, name="qkv_conv_fwd", grid=(t // tm,),
        in_specs=[_row(tm, 3 * G, _C_QKV), _prev_halo(tm, 3 * G, _C_QKV), _layer((4, 3 * G), l)],
        out_specs=_row(tm, 3 * G), out_shape=jax.ShapeDtypeStruct((t, 3 * G), F32),
        compiler_params=_cparams("parallel"))(p, p, w)


def _qkv_conv_bwd(p, dy, w, l, tm=512):
    t = p.shape[0]

    def body(x_ref, x_halo, x_next, dy_ref, dy_next, w_ref, dx_ref, dw_ref):
        @pl.when(pl.program_id(0) == 0)
        def _():
            dw_ref[...] = jnp.zeros_like(dw_ref)
        w_v = w_ref[...]
        x_all = jnp.concatenate([_with_prev(x_halo, x_ref), jnp.where(
            pl.program_id(0) == pl.num_programs(0) - 1, 0.0, x_next[...])], axis=0)
        c = _causal_conv(x_all, w_v, tm + HALO)
        s = _sigmoid(c)
        dc_ext = _with_next(dy_ref, dy_next) * (s * (1.0 + c * (1.0 - s)))
        dx_ref[...] = _causal_conv_t(dc_ext, w_v, tm).astype(BF16)
        dw_ref[...] += _conv_wgrad(x_all[0:HALO + tm], dc_ext[0:tm], 4, tm)

    return pl.pallas_call(
        body, name="qkv_conv_bwd", grid=(t // tm,),
        in_specs=[_row(tm, 3 * G, _C_QKV), _prev_halo(tm, 3 * G, _C_QKV), _next_halo(tm, 3 * G, _C_QKV, t),
                  _row(tm, 3 * G), _next_halo(tm, 3 * G, 0, t), _layer((4, 3 * G), l)],
        out_specs=[_row(tm, 3 * G), _acc((4, 3 * G))],
        out_shape=[jax.ShapeDtypeStruct((t, 3 * G), BF16), jax.ShapeDtypeStruct((4, 3 * G), F32)],
        compiler_params=_cparams("arbitrary"))(p, p, p, dy, dy, w)


_STATE = pl.BlockSpec((1, NH, HD, HD), lambda i: (i, 0, 0, 0))


def _dn_specs():
    return [_resident((DEPTH, NH)), _resident((DEPTH, NH)), _resident((DEPTH, HD))]


def _dn_fwd(qkv, p, params, l, cps=8, rider=None):
    t = qkv.shape[0]
    tm = DN_C * cps
    nb = cps * NH

    def body(*refs):
        (qkv_ref, z_ref, ab_ref, alog_ref, dt_ref, nw_ref, y_ref, st_ref, inv_ref, state), ride = _split_refs(
            refs, 6, 3, 1, rider)
        _ride_begin(rider, ride, (t // tm,))

        @pl.when(pl.program_id(0) == 0)
        def _():
            state[...] = jnp.zeros_like(state)
        st_ref[0] = state[...]
        y, new, inv = _dn_tile(qkv_ref[:, 0:G], qkv_ref[:, G:2 * G], qkv_ref[:, 2 * G:3 * G], ab_ref[...], z_ref[...],
                               state[...], None, alog_ref[l:l + 1, :], dt_ref[l:l + 1, :], nw_ref[l:l + 1, :])
        y_ref[...] = y.astype(BF16)
        inv_ref[...] = inv
        state[...] = new
        _ride_end(rider, ride, (t // tm,))

    specs, operands = _host(
        rider, [_row(tm, 3 * G), _row(tm, G, _C_Z), _row(tm, LANES, _C_AB)] + _dn_specs(),
        [_row(tm, G), _STATE, pl.BlockSpec((nb, DN_C, DN_C), lambda i: (i, 0, 0))],
        [jax.ShapeDtypeStruct((t, G), BF16), jax.ShapeDtypeStruct((t // tm, NH, HD, HD), F32),
         jax.ShapeDtypeStruct((t // DN_C * NH, DN_C, DN_C), F32)],
        [pltpu.VMEM((NH, HD, HD), F32)], [qkv, p, p, *params])
    return pl.pallas_call(body, name="dn_fwd", grid=(t // tm,), compiler_params=_cparams("arbitrary"), **specs)(*operands)


def _dn_bwd(qkv, p, states, inv, dmix, params, l, cps=8, rider=None):
    t = qkv.shape[0]
    tm = DN_C * cps
    n = t // tm
    nb = cps * NH

    def body(*refs):
        (qkv_ref, z_ref, ab_ref, st_ref, inv_ref, dy_ref, alog_ref, dt_ref, nw_ref,
         dqkv_ref, dz_ref, dab_ref, dalog_ref, ddt_ref, dnw_ref, dstate), ride = _split_refs(refs, 9, 6, 1, rider)
        _ride_begin(rider, ride, (n,))
        dprm_refs = (dalog_ref, ddt_ref, dnw_ref)

        @pl.when(pl.program_id(0) == 0)
        def _():
            dstate[...] = jnp.zeros_like(dstate)
            for r in dprm_refs:
                r[...] = jnp.zeros_like(r)
        inv_v = inv_ref[...]
        tile = lambda q, k, v, ab, z, st, alog, dt, nw: _dn_tile(q, k, v, ab, z, st, inv_v, alog, dt, nw)[:2]
        _, vjp = jax.vjp(tile, qkv_ref[:, 0:G], qkv_ref[:, G:2 * G], qkv_ref[:, 2 * G:3 * G], ab_ref[...], z_ref[...],
                         st_ref[0], alog_ref[l:l + 1, :], dt_ref[l:l + 1, :], nw_ref[l:l + 1, :])
        dq, dk, dv, dab, dz, dst, *dprm = vjp((dy_ref[...], dstate[...]))
        dqkv_ref[...] = jnp.concatenate([dq, dk, dv], axis=1)
        dz_ref[...] = dz.astype(BF16)
        dab_ref[...] = dab.astype(BF16)
        dstate[...] = dst
        for r, g in zip(dprm_refs, dprm):
            r[...] += g
        _ride_end(rider, ride, (n,))

    rev = lambda w, blk: pl.BlockSpec((tm, w), lambda i: (n - 1 - i, blk))
    specs, operands = _host(
        rider, [rev(3 * G, 0), rev(G, _C_Z), rev(LANES, _C_AB),
                pl.BlockSpec((1, NH, HD, HD), lambda i: (n - 1 - i, 0, 0, 0)),
                pl.BlockSpec((nb, DN_C, DN_C), lambda i: (n - 1 - i, 0, 0)), rev(G, 2)] + _dn_specs(),
        [rev(3 * G, 0), rev(G, 0), rev(LANES, 0), _acc((1, NH)), _acc((1, NH)), _acc((1, HD))],
        [jax.ShapeDtypeStruct((t, 3 * G), F32), jax.ShapeDtypeStruct((t, G), BF16),
         jax.ShapeDtypeStruct((t, LANES), BF16), jax.ShapeDtypeStruct((1, NH), F32),
         jax.ShapeDtypeStruct((1, NH), F32), jax.ShapeDtypeStruct((1, HD), F32)],
        [pltpu.VMEM((NH, HD, HD), F32)], [qkv, p, p, states, inv, dmix, *params])
    return pl.pallas_call(body, name="dn_bwd", grid=(n,), compiler_params=_cparams("arbitrary"), **specs)(*operands)


def _gla_specs(l):
    return [_layer((16, G), l), _resident((DEPTH, G)), _resident((DEPTH, HD))]


def _gla_fwd(p, params, l, cps=8, rider=None):
    t = p.shape[0]
    tm = GLA_C * cps

    def body(*refs):
        (qkv_ref, z_ref, glr_ref, w2_ref, gb_ref, nw_ref, y_ref, st_ref, state), ride = _split_refs(refs, 6, 2, 1, rider)
        _ride_begin(rider, ride, (t // tm,))

        @pl.when(pl.program_id(0) == 0)
        def _():
            state[...] = jnp.zeros_like(state)
        st_ref[0] = state[...]
        y, new = _gla_tile(qkv_ref[:, 0:G], qkv_ref[:, G:2 * G], qkv_ref[:, 2 * G:3 * G], glr_ref[...], z_ref[...],
                           state[...], w2_ref[...], gb_ref[l:l + 1, :], nw_ref[l:l + 1, :])
        y_ref[...] = y.astype(BF16)
        state[...] = new
        _ride_end(rider, ride, (t // tm,))

    specs, operands = _host(
        rider, [_row(tm, 3 * G, _D_QKV), _row(tm, G, _D_Z), _row(tm, LANES, _D_GLR)] + _gla_specs(l),
        [_row(tm, G), _STATE],
        [jax.ShapeDtypeStruct((t, G), BF16), jax.ShapeDtypeStruct((t // tm, NH, HD, HD), F32)],
        [pltpu.VMEM((NH, HD, HD), F32)], [p, p, p, *params])
    return pl.pallas_call(body, name="gla_fwd", grid=(t // tm,), compiler_params=_cparams("arbitrary"), **specs)(*operands)


def _gla_bwd(p, states, dmix, params, l, cps=8):
    t = p.shape[0]
    tm = GLA_C * cps
    n = t // tm

    def body(qkv_ref, z_ref, glr_ref, st_ref, dy_ref, w2_ref, gb_ref, nw_ref,
             dqkv_ref, dz_ref, dglr_ref, dw2_ref, dgb_ref, dnw_ref, dstate):
        dprm_refs = (dw2_ref, dgb_ref, dnw_ref)

        @pl.when(pl.program_id(0) == 0)
        def _():
            dstate[...] = jnp.zeros_like(dstate)
            for r in dprm_refs:
                r[...] = jnp.zeros_like(r)
        _, vjp = jax.vjp(_gla_tile, qkv_ref[:, 0:G], qkv_ref[:, G:2 * G], qkv_ref[:, 2 * G:3 * G], glr_ref[...],
                         z_ref[...], st_ref[0], w2_ref[...], gb_ref[l:l + 1, :], nw_ref[l:l + 1, :])
        dq, dk, dv, dglr, dz, dst, *dprm = vjp((dy_ref[...], dstate[...]))
        dqkv_ref[...] = jnp.concatenate([dq, dk, dv], axis=1).astype(BF16)
        dz_ref[...] = dz.astype(BF16)
        dglr_ref[...] = dglr.astype(BF16)
        dstate[...] = dst
        for r, g in zip(dprm_refs, dprm):
            r[...] += g

    rev = lambda w, blk: pl.BlockSpec((tm, w), lambda i: (n - 1 - i, blk))
    return pl.pallas_call(
        body, name="gla_bwd", grid=(n,),
        in_specs=[rev(3 * G, _D_QKV), rev(G, _D_Z), rev(LANES, _D_GLR),
                  pl.BlockSpec((1, NH, HD, HD), lambda i: (n - 1 - i, 0, 0, 0)), rev(G, 3)] + _gla_specs(l),
        out_specs=[rev(3 * G, 0), rev(G, 0), rev(LANES, 0), _acc((16, G)), _acc((1, G)), _acc((1, HD))],
        out_shape=[jax.ShapeDtypeStruct((t, 3 * G), BF16), jax.ShapeDtypeStruct((t, G), BF16),
                   jax.ShapeDtypeStruct((t, LANES), BF16), jax.ShapeDtypeStruct((16, G), F32),
                   jax.ShapeDtypeStruct((1, G), F32), jax.ShapeDtypeStruct((1, HD), F32)],
        scratch_shapes=[pltpu.VMEM((NH, HD, HD), F32)],
        compiler_params=_cparams("arbitrary"))(p, p, p, states, dmix, *params)


def _adamw_math(w, g, m, v):
    m = ADAM_B1 * m + (1.0 - ADAM_B1) * g
    v = ADAM_B2 * v + (1.0 - ADAM_B2) * (g * g)
    m_hat = m / (1.0 - ADAM_B1 ** ADAM_STEP)
    v_hat = v / (1.0 - ADAM_B2 ** ADAM_STEP)
    return -ADAM_LR * (m_hat / (jnp.sqrt(v_hat) + ADAM_EPS) + ADAM_WD * w), m, v


def _adamw_large(parts, w, m, v, name, tr):
    _, r, c = w.shape

    def body(p0_ref, p1_ref, w_ref, m_ref, v_ref, g_ref, d_ref, nm_ref, nv_ref):
        def total(p_ref):
            g = p_ref[0].astype(F32)
            for k in range(1, N_DEV):
                g = g + p_ref[k].astype(F32)
            return g

        g = jnp.where(pl.program_id(0) == 0, total(p0_ref), total(p1_ref))
        g_ref[...] = g
        d_ref[...], nm_ref[...], nv_ref[...] = _adamw_math(w_ref[...], g, m_ref[...], v_ref[...])

    blk = pl.BlockSpec((None, tr, c), lambda l, i: (l, i, 0))
    part = pl.BlockSpec((N_DEV, tr, c), lambda l, i: (0, i, 0))
    return pl.pallas_call(
        body, name=name, grid=(DEPTH, r // tr), in_specs=[part, part, blk, blk, blk],
        out_specs=[blk] * 4, out_shape=[jax.ShapeDtypeStruct(w.shape, F32)] * 4,
        compiler_params=_cparams("parallel", "parallel"))(*parts, w, m, v)


def _adamw_w_in(parts, w, m, v, tc=256):
    def body(p0_ref, p1_ref, w_ref, m_ref, v_ref, g_ref, d_ref, nm_ref, nv_ref):
        for l, p_ref in enumerate((p0_ref, p1_ref)):
            g = p_ref[0, 0:IN_SHARD, :].astype(F32)
            for k in range(1, N_DEV):
                g = g + p_ref[k, 0:IN_SHARD, :].astype(F32)
            g_ref[:, l, :] = g
            d_ref[:, l, :], nm_ref[:, l, :], nv_ref[:, l, :] = _adamw_math(w_ref[:, l, :], g, m_ref[:, l, :], v_ref[:, l, :])

    blk = pl.BlockSpec((IN_SHARD, DEPTH, tc), lambda i: (0, 0, i))
    part = pl.BlockSpec((N_DEV, IN_ROWS, tc), lambda i: (0, 0, i))
    return pl.pallas_call(
        body, name="adamw_w_in", grid=(D // tc,), in_specs=[part, part, blk, blk, blk], out_specs=[blk] * 4,
        out_shape=[jax.ShapeDtypeStruct(w.shape, F32)] * 4, compiler_params=_cparams("parallel"))(*parts, w, m, v)


_SLAB_AT = {
    "sgu_w_spatial": (0, 0, SGU_C, LANES),
    "sgu_b_spatial": (128, 0, NH, SGU_C),
    "norm1_w": (132, 0, 1, D),
    "norm2_w": (133, 0, 1, D),
    "sgu_ln_w": (134, 0, 1, G),
    "sgu_ln_b": (134, 256, 1, G),
    "gla_gate_bias": (134, 512, 1, G),
    "dn_norm_w": (134, 768, 1, HD),
    "gla_norm_w": (134, 896, 1, HD),
    "dn_a_log": (135, 0, 1, NH),
    "dn_dt_bias": (135, 128, 1, NH),
    "gla_w_gate2": (136, 0, 16, G),
    "dn_conv_w": (136, 256, 4, 3 * G),
    "sc_conv_w": (140, 256, 3, G),
}
_LAYER_ROWS = 152
_FINAL_ROW = DEPTH * _LAYER_ROWS
_SLAB_ROWS, _SLAB_COLS = _FINAL_ROW + 8, 1024
_SLAB_ORDER = tuple(_SLAB_AT)
_SHARDED_SMALL = ("sc_conv_w", "dn_conv_w", "gla_w_gate2")
_REPLICATED = tuple(k for k in _SLAB_ORDER if k not in _SHARDED_SMALL)


def _region(name, l, h=0):
    r0, c0, nr, nc = _SLAB_AT[name]
    return slice(_LAYER_ROWS * l + r0, _LAYER_ROWS * l + r0 + nr), slice(c0 + nc * h, c0 + nc * (h + 1))


def _pack_small(layer_grads, d_final):
    def body(*refs):
        slab = refs[-1]
        slab[...] = jnp.zeros_like(slab)
        it = iter(refs[:-1])
        for l in range(DEPTH):
            for name in _SLAB_ORDER:
                ref = next(it)
                if name == "sgu_w_spatial":
                    for h in range(NH):
                        slab[_region(name, l, h)] = ref[h]
                else:
                    slab[_region(name, l)] = ref[...]
        slab[_FINAL_ROW:_FINAL_ROW + 1, :] = next(it)[...]

    flat = [layer_grads[l][name] for l in range(DEPTH) for name in _SLAB_ORDER] + [d_final]
    return pl.pallas_call(body, name="pack_small_grads", out_shape=jax.ShapeDtypeStruct((_SLAB_ROWS, _SLAB_COLS), F32),
                          compiler_params=_cparams())(*flat)


def _adamw_small(parts, w, m, v):
    names = _REPLICATED + ("final_norm_w",)
    n_in = len(names)

    def body(*refs):
        def total(rows, cols):
            g = refs[0][0, rows, cols]
            for k in range(1, N_DEV):
                g = g + refs[0][k, rows, cols]
            return g

        w_refs = dict(zip(names, refs[1:1 + n_in]))
        m_refs = dict(zip(names, refs[1 + n_in:1 + 2 * n_in]))
        v_refs = dict(zip(names, refs[1 + 2 * n_in:1 + 3 * n_in]))
        outs = refs[1 + 3 * n_in:]
        out_refs = [dict(zip(names, outs[j * n_in:(j + 1) * n_in])) for j in range(4)]
        full_refs = dict(zip(_SHARDED_SMALL, outs[4 * n_in:]))

        def update(name, idx, g):
            res = (g,) + _adamw_math(w_refs[name][idx], g, m_refs[name][idx], v_refs[name][idx])
            for o, val in zip(out_refs, res):
                o[name][idx] = val

        for l in range(DEPTH):
            for name in _REPLICATED:
                if name == "sgu_w_spatial":
                    for h in range(NH):
                        update(name, (l, h), total(*_region(name, l, h)))
                elif name == "sgu_b_spatial":
                    update(name, (l,), total(*_region(name, l)))
                else:
                    update(name, (slice(l, l + 1), slice(None)), total(*_region(name, l)))
            for name in _SHARDED_SMALL:
                full_refs[name][l] = total(*_region(name, l))
        update("final_norm_w", (slice(None), slice(None)), total(slice(_FINAL_ROW, _FINAL_ROW + 1), slice(None)))

    shapes = [jax.ShapeDtypeStruct(w[k].shape, F32) for k in names]
    full_shapes = [jax.ShapeDtypeStruct((DEPTH,) + _SLAB_AT[k][2:], F32) for k in _SHARDED_SMALL]
    outs = pl.pallas_call(body, name="adamw_small", out_shape=shapes * 4 + full_shapes, compiler_params=_cparams())(
        parts, *[w[k] for k in names], *[m[k] for k in names], *[v[k] for k in names])
    result = [dict(zip(names, outs[j * n_in:(j + 1) * n_in])) for j in range(4)]
    return result, dict(zip(_SHARDED_SMALL, outs[4 * n_in:]))


def _adamw_shards(g, w, m, v):
    names = _SHARDED_SMALL
    n = len(names)

    def body(*refs):
        for j in range(n):
            g_v = refs[j][...]
            res = _adamw_math(refs[n + j][...], g_v, refs[2 * n + j][...], refs[3 * n + j][...])
            for o, val in zip(refs[4 * n + j::n], res):
                o[...] = val

    shapes = [jax.ShapeDtypeStruct(w[k].shape, F32) for k in names]
    outs = pl.pallas_call(body, name="adamw_small_shards", out_shape=shapes * 3, compiler_params=_cparams())(
        *[g[k] for k in names], *[w[k] for k in names], *[m[k] for k in names], *[v[k] for k in names])
    return [dict(zip(names, outs[j * n:(j + 1) * n])) for j in range(3)]


_ANY = pl.BlockSpec(memory_space=pl.ANY)


def _place():
    return lax.axis_index("x"), lax.axis_index("y"), lax.axis_index("c")


def _sems(n):
    return [pltpu.SemaphoreType.DMA((n, 7)), pltpu.SemaphoreType.DMA((n, 7)), pltpu.SemaphoreType.DMA((n,))]


class _Gather:
    def __init__(self, blocks, second, forward_at=1.0):
        self.inputs, self.second, self.forward_at = list(blocks), list(second), forward_at
        self.out_shape = [jax.ShapeDtypeStruct((a.shape[0], N_DEV) + a.shape[1:] if s else (N_DEV,) + a.shape, a.dtype)
                          for a, s in zip(blocks, second)]
        self.scratch = _sems(len(blocks))

    def _copies(self, refs):
        x_refs, out_refs, (send_sems, recv_sems, local_sems) = refs
        n = len(x_refs)
        x, y, c = _place()
        me, sibling = (x, y, c), (x, y, 1 - c)
        chips = [(1 - x, y), (x, 1 - y), (1 - x, 1 - y)]

        def slot(j, px, py, pc):
            idx = 4 * px + 2 * py + pc
            return out_refs[j].at[:, idx] if self.second[j] else out_refs[j].at[idx]

        def copies(k, block, to, own=False):
            return [pltpu.make_async_remote_copy(
                src_ref=x_refs[j] if own else slot(j, *block), dst_ref=slot(j, *block),
                send_sem=send_sems.at[j, k], recv_sem=recv_sems.at[j, k], device_id=to,
                device_id_type=pl.DeviceIdType.MESH) for j in range(n)]

        mine = [pltpu.make_async_copy(x_refs[j], slot(j, *me), local_sems.at[j]) for j in range(n)]
        first = copies(0, me, sibling, own=True)
        for j, chip in enumerate(chips):
            first += copies(1 + j, me, (*chip, c), own=True)
        landed = [copies(1 + j, (*chip, c), me) for j, chip in enumerate(chips)]
        onward = [copies(4 + j, (*chip, c), sibling) for j, chip in enumerate(chips)]
        from_sibling = copies(0, sibling, me)
        for j, chip in enumerate(chips):
            from_sibling += copies(4 + j, (*chip, 1 - c), me)
        return mine, first, landed, onward, from_sibling

    def start(self, refs):
        mine, first, _, _, _ = self._copies(refs)
        for cp in mine + first:
            cp.start()

    def forward(self, refs):
        _, _, landed, onward, _ = self._copies(refs)
        for arrived, passed in zip(landed, onward):
            for cp in arrived:
                cp.wait_recv()
            for cp in passed:
                cp.start()

    def finish(self, refs):
        mine, first, _, onward, from_sibling = self._copies(refs)
        for cp in from_sibling:
            cp.wait_recv()
        for cp in first + [cp for passed in onward for cp in passed]:
            cp.wait_send()
        for cp in mine:
            cp.wait()


class _Exchange:
    forward_at = 1.0

    def __init__(self, sends):
        self.inputs = list(sends)
        self.out_shape = [jax.ShapeDtypeStruct(a.shape, a.dtype) for a in sends]
        self.scratch = _sems(len(sends))

    def _copies(self, refs):
        x_refs, out_refs, (send_sems, recv_sems, local_sems) = refs
        n = len(x_refs)
        x, y, c = _place()
        me = 4 * x + 2 * y + c
        sent, landing = [], []
        for k in range(1, N_DEV):
            px, py, pc = x ^ ((k >> 2) & 1), y ^ ((k >> 1) & 1), c ^ (k & 1)
            them = 4 * px + 2 * py + pc
            for j in range(n):
                for here, there, into in ((them, me, sent), (me, them, landing)):
                    into.append(pltpu.make_async_remote_copy(
                        src_ref=x_refs[j].at[here], dst_ref=out_refs[j].at[there], send_sem=send_sems.at[j, k - 1],
                        recv_sem=recv_sems.at[j, k - 1], device_id=(px, py, pc), device_id_type=pl.DeviceIdType.MESH))
        mine = [pltpu.make_async_copy(x_refs[j].at[me], out_refs[j].at[me], local_sems.at[j]) for j in range(n)]
        return mine, sent, landing

    def start(self, refs):
        mine, sent, _ = self._copies(refs)
        for cp in mine + sent:
            cp.start()

    def forward(self, refs):
        pass

    def finish(self, refs):
        mine, sent, landing = self._copies(refs)
        for cp in landing:
            cp.wait_recv()
        for cp in sent:
            cp.wait_send()
        for cp in mine:
            cp.wait()


def _run(rider, name):
    n_in, n_out = len(rider.inputs), len(rider.out_shape)

    def body(*refs):
        parts = (refs[:n_in], refs[n_in:n_in + n_out], refs[n_in + n_out:])
        rider.start(parts)
        rider.forward(parts)
        rider.finish(parts)

    return pl.pallas_call(body, name=name, out_shape=rider.out_shape, in_specs=[_ANY] * n_in, out_specs=[_ANY] * n_out,
                          scratch_shapes=rider.scratch)(*rider.inputs)


def _split_refs(refs, n_in, n_out, n_scratch, rider):
    r_in = len(rider.inputs) if rider else 0
    r_out = len(rider.out_shape) if rider else 0
    a = n_in + r_in
    b = a + n_out + r_out
    own = refs[:n_in] + refs[a:a + n_out] + refs[b:b + n_scratch]
    return own, (refs[n_in:a], refs[a + n_out:b], refs[b + n_scratch:])


def _grid_step(grid):
    step, stride = 0, 1
    for axis in reversed(range(len(grid))):
        step = step + pl.program_id(axis) * stride
        stride *= grid[axis]
    return step


def _ride_begin(rider, ride_refs, grid):
    if rider is not None:
        pl.when(_grid_step(grid) == 0)(lambda: rider.start(ride_refs))


def _ride_end(rider, ride_refs, grid):
    if rider is not None:
        steps = 1
        for n in grid:
            steps *= n
        step = _grid_step(grid)
        pl.when(step == min(steps - 1, int(steps * rider.forward_at)))(lambda: rider.forward(ride_refs))
        pl.when(step == steps - 1)(lambda: rider.finish(ride_refs))


def _host(rider, in_specs, out_specs, out_shape, scratch, operands):
    if rider is None:
        return dict(in_specs=in_specs, out_specs=out_specs, out_shape=out_shape, scratch_shapes=scratch), operands
    return dict(in_specs=in_specs + [_ANY] * len(rider.inputs), out_specs=out_specs + [_ANY] * len(rider.out_shape),
                out_shape=out_shape + rider.out_shape, scratch_shapes=scratch + rider.scratch), operands + rider.inputs


_LATE = ("w_gate_up", "w_out", "w_down")


def _local_grads(x, target, w, late=None, exchange=False):
    w = dict(w)
    w_in = list(w["w_in"])
    dn_params = (w["dn_a_log"], w["dn_dt_bias"], w["dn_norm_w"])
    gla_params = (w["gla_w_gate2"], w["gla_gate_bias"], w["gla_norm_w"])
    sgu_params = (w["sgu_ln_w"], w["sgu_ln_b"], w["sgu_w_spatial"], w["sgu_b_spatial"])
    saved = []
    for l in range(DEPTH):
        riding = l == 0 and late is not None
        h, p, *got = _in_proj(x, w["norm1_w"], w_in[l], l,
                              rider=_Gather([late["w_down"]], [True], forward_at=0.85) if riding else None)
        if riding:
            w["w_down"] = got[0].reshape(DEPTH, FF, D)
        ya = _sgu_fwd(p, *sgu_params, l)
        yb = _sconv_fwd(p, w["sc_conv_w"], l)
        qkv_c = _qkv_conv_fwd(p, w["dn_conv_w"], l)
        yc, st_c, inv_c, *got = _dn_fwd(
            qkv_c, p, dn_params, l,
            rider=_Gather([late["w_gate_up"], late["w_out"]], [False, True], forward_at=0.85) if riding else None)
        if riding:
            w.update(w_gate_up=got[0], w_out=got[1].reshape(DEPTH, D, D))
        yd, st_d, *got = _gla_fwd(p, gla_params, l,
                                  rider=_Gather([late["w_in"]], [False], forward_at=0.7) if riding else None)
        if riding:
            w_in[1] = _relayout_w_in(got[0])
        mix, x1, h2, gu, act, x2 = _out_mlp(x, (ya, yb, yc, yd), w["w_out"], w["norm2_w"], w["w_gate_up"], w["w_down"], l)
        saved.append(dict(x=x, h=h, p=p, qkv_c=qkv_c, st_c=st_c, inv_c=inv_c, st_d=st_d, mix=mix, x1=x1, h2=h2, gu=gu,
                          act=act))
        x = x2
    loss, d_final, dx = _loss_head(x, w["final_norm_w"], target)
    large = {k: [None] * DEPTH for k in ("w_in", "w_out", "w_gate_up", "w_down")}
    small = [None] * DEPTH
    for l in reversed(range(DEPTH)):
        s = saved[l]
        p = s["p"]
        g = {}
        riding = exchange and l == 0

        def exchanging(pairs):
            return _Exchange([large[k][j] for k, j in pairs]) if riding else None

        def arrive(pairs, arrived):
            for (k, j), a in zip(pairs, arrived):
                large[k][j] = a

        pairs = (("w_gate_up", 1), ("w_down", 1))
        dgu, dx1, dmix, g["norm2_w"], *arrived = _mlp_out_bwd(
            dx, s["x1"], s["gu"], w["norm2_w"], w["w_down"], w["w_gate_up"], w["w_out"], l, rider=exchanging(pairs))
        arrive(pairs, arrived)
        pairs = (("w_in", 1), ("w_out", 1))
        d_gu, *arrived = _wgrad_rows(dgu, s["h2"], "wgrad_gate_up", rider=exchanging(pairs)) if riding else (
            _wgrad_rows(dgu, s["h2"], "wgrad_gate_up"),)
        arrive(pairs, arrived)
        large["w_gate_up"][l] = d_gu.reshape(N_DEV, GU_SHARD, D)
        large["w_down"][l] = _wgrad_rows(s["act"], dx, "wgrad_down").reshape(N_DEV, FF // N_DEV, D)
        large["w_out"][l] = _wgrad_rows(s["mix"], dx1, "wgrad_out", rows=D).reshape(N_DEV, D // N_DEV, D)
        d_a, g["sgu_ln_w"], g["sgu_ln_b"], g["sgu_w_spatial"], g["sgu_b_spatial"] = _sgu_bwd(p, dmix, *sgu_params, l)
        d_b, g["sc_conv_w"] = _sconv_bwd(p, dmix, w["sc_conv_w"], l)
        pairs = tuple((k, 0) for k in _LATE)
        dqkv_c, dz_c, dab, g["dn_a_log"], g["dn_dt_bias"], g["dn_norm_w"], *arrived = _dn_bwd(
            s["qkv_c"], p, s["st_c"], s["inv_c"], dmix, dn_params, l, rider=exchanging(pairs))
        arrive(pairs, arrived)
        d_c, g["dn_conv_w"] = _qkv_conv_bwd(p, dqkv_c, w["dn_conv_w"], l)
        d_d, dz_d, dglr, g["gla_w_gate2"], g["gla_gate_bias"], g["gla_norm_w"] = _gla_bwd(p, s["st_d"], dmix, gla_params, l)
        dp, dx, g["norm1_w"] = _in_proj_bwd((d_c, d_d, d_a, d_b, dz_c, dz_d, dab, dglr), s["x"], dx1, w["norm1_w"],
                                            w_in[l], l)
        small[l] = g
        w_in_rows = dict(tt=WGRAD_TOKENS // 2, rows=P // 2, out_dtype=F32)
        if riding:
            d_w_in, small = _wgrad_rows(dp, s["h"], "wgrad_in", **w_in_rows,
                                        rider=_Gather([_pack_small(small, d_final)], [False], forward_at=0.75))
        else:
            d_w_in = _wgrad_rows(dp, s["h"], "wgrad_in", **w_in_rows)
        large["w_in"][l] = _scatter_w_in_grad(d_w_in)
    return loss[0, 0], dx, large, small, d_final


_ORDER = ("norm1_w", "w_in", "sgu_ln_w", "sgu_ln_b", "sgu_w_spatial", "sgu_b_spatial", "sc_conv_w", "dn_conv_w",
          "dn_a_log", "dn_dt_bias", "dn_norm_w", "gla_w_gate2", "gla_gate_bias", "gla_norm_w", "w_out", "norm2_w",
          "w_gate_up", "w_down", "final_norm_w")
_LARGE = ("w_in", "w_gate_up", "w_out", "w_down")
_LARGE_TILE = {"w_gate_up": 352, "w_out": 128, "w_down": 352}


def _gather_cols(g):
    return jnp.transpose(g, (1, 2, 0, 3)).reshape(g.shape[1], g.shape[2], N_DEV * g.shape[3])


def kernel(x, norm1_w, w_in, sgu_ln_w, sgu_ln_b, sgu_w_spatial, sgu_b_spatial, sc_conv_w, dn_conv_w, dn_a_log, dn_dt_bias, dn_norm_w, gla_w_gate2, gla_gate_bias, gla_norm_w, w_out, norm2_w, w_gate_up, w_down, final_norm_w, loss_target, m_norm1_w, m_w_in, m_sgu_ln_w, m_sgu_ln_b, m_sgu_w_spatial, m_sgu_b_spatial, m_sc_conv_w, m_dn_conv_w, m_dn_a_log, m_dn_dt_bias, m_dn_norm_w, m_gla_w_gate2, m_gla_gate_bias, m_gla_norm_w, m_w_out, m_norm2_w, m_w_gate_up, m_w_down, m_final_norm_w, v_norm1_w, v_w_in, v_sgu_ln_w, v_sgu_ln_b, v_sgu_w_spatial, v_sgu_b_spatial, v_sc_conv_w, v_dn_conv_w, v_dn_a_log, v_dn_dt_bias, v_dn_norm_w, v_gla_w_gate2, v_gla_gate_bias, v_gla_norm_w, v_w_out, v_norm2_w, v_w_gate_up, v_w_down, v_final_norm_w):
    args = dict(locals())
    wts = {k: args[k] for k in _ORDER}
    mom = {k: args["m_" + k] for k in _ORDER}
    var = {k: args["v_" + k] for k in _ORDER}
    for d in (wts, mom, var):
        d["final_norm_w"] = d["final_norm_w"][None]
    me = 4 * lax.axis_index("x") + 2 * lax.axis_index("y") + lax.axis_index("c")
    for d in (wts, mom, var):
        d["w_gate_up"] = jnp.swapaxes(d["w_gate_up"], 1, 2)

    late = {k: wts[k].astype(BF16) for k in _LATE}
    w_in = wts["w_in"].astype(BF16)
    late["w_in"] = w_in[1]
    got = _run(_Gather([w_in[0]] + [wts[k] for k in _SHARDED_SMALL], [False] * 4), "gather_w_in")
    full = dict(wts)
    full["w_in"] = [_relayout_w_in(got[0]), None]
    for k, g in zip(_SHARDED_SMALL, got[1:]):
        full[k] = _gather_cols(g)

    loss, dx, large, small, d_final = _local_grads(x[0], loss_target[0], full, late=late, exchange=True)
    loss = lax.psum(loss, ("x", "y", "c"))

    large["w_in"][0], = _run(_Exchange([large["w_in"][0]]), "exchange_grads")
    result = {}
    for k in ("w_gate_up", "w_out", "w_down"):
        outs = _adamw_large(large[k], wts[k], mom[k], var[k], "adamw_" + k, _LARGE_TILE[k])
        for kind, a in zip(("grad", "delta", "new_m", "new_v"), outs):
            result[kind, k] = jnp.swapaxes(a, 1, 2) if k == "w_gate_up" else a
    outs = _adamw_w_in(large["w_in"], *[jnp.transpose(d["w_in"], (2, 0, 1)) for d in (wts, mom, var)])
    for kind, a in zip(("grad", "delta", "new_m", "new_v"), outs):
        result[kind, "w_in"] = jnp.transpose(a, (1, 2, 0))

    slabs = small
    rep = _REPLICATED + ("final_norm_w",)
    outs, summed = _adamw_small(slabs, {k: wts[k] for k in rep}, {k: mom[k] for k in rep}, {k: var[k] for k in rep})
    for kind, d in zip(("grad", "delta", "new_m", "new_v"), outs):
        for k, a in d.items():
            result[kind, k] = a[0] if k == "final_norm_w" else a
    own = {k: lax.dynamic_slice_in_dim(summed[k], me * wts[k].shape[-1], wts[k].shape[-1], axis=2) for k in _SHARDED_SMALL}
    outs = _adamw_shards(own, {k: wts[k] for k in _SHARDED_SMALL}, {k: mom[k] for k in _SHARDED_SMALL},
                         {k: var[k] for k in _SHARDED_SMALL})
    for kind, d in zip(("grad", "delta", "new_m", "new_v"), [own] + outs):
        for k, a in d.items():
            result[kind, k] = a

    return (loss, dx[None], *[result[kind, k] for kind in ("grad", "delta", "new_m", "new_v") for k in _ORDER])
```

```python
import functools

import jax
import jax.numpy as jnp
from jax import lax
from jax.experimental import pallas as pl
from jax.experimental.pallas import tpu as pltpu

F32, BF16 = jnp.float32, jnp.bfloat16
DEPTH = 2
D = 1024
G = 256
NH, HD = 4, 64
FF = 2816
IN_COLS = 3352
P = 3584
EPS = 1e-6
SGU_C, DN_C, GLA_C = 128, 64, 64
N_DEV = 8
IN_SHARD = IN_COLS // N_DEV
GU_SHARD = 2 * FF // N_DEV
LANES = 128
VMEM_LIMIT = 56 * 2 ** 20

_PAD_SEGS = ((1280, 2048, 0),
             (2312, 3080, 0),
             (0, 512, 0),
             (512, 1280, 0),
             (2056, 2312, 0),
             (3096, 3352, 0),
             (2048, 2056, 120),
             (3080, 3096, 112))

ADAM_LR, ADAM_B1, ADAM_B2, ADAM_EPS, ADAM_WD, ADAM_STEP = 0.001, 0.9, 0.999, 1e-08, 0.01, 10


def _cparams(*sem):
    return pltpu.CompilerParams(dimension_semantics=sem, vmem_limit_bytes=VMEM_LIMIT)


def _resident(shape):
    return pl.BlockSpec(shape, lambda *_: (0,) * len(shape), pipeline_mode=pl.Buffered(1))


def _layer(shape, l):
    return pl.BlockSpec((None,) + tuple(shape), lambda *_: (l,) + (0,) * len(shape), pipeline_mode=pl.Buffered(1))


def _acc(shape):
    return pl.BlockSpec(shape, lambda *_: (0,) * len(shape))


def _dg(a, b, ca, cb):
    lead = a.ndim - 2
    batch = ((0,), (0,)) if lead else ((), ())
    return lax.dot_general(a, b, (((ca + lead,), (cb + lead,)), batch), preferred_element_type=F32)


def _split(t):
    hi = t.astype(BF16)
    return hi, (t - hi.astype(F32)).astype(BF16)


def _dg3(a, b, ca, cb):
    (ah, al), (bh, bl) = a, b
    return _dg(ah, bh, ca, cb) + (_dg(ah, bl, ca, cb) + _dg(al, bh, ca, cb))


def _make_dot(accurate):
    def raw(a, b, form):
        ca = 0 if form == "tn" else 1
        cb = 1 if form == "nt" else 0
        if accurate:
            return _dg3(_split(a), _split(b), ca, cb)
        return _dg(a.astype(BF16), b.astype(BF16), ca, cb)

    @functools.partial(jax.custom_vjp, nondiff_argnums=(2,))
    def dot(a, b, form):
        return raw(a, b, form)

    def fwd(a, b, form):
        return raw(a, b, form), (a, b)

    def bwd(form, res, g):
        a, b = res
        if form == "nn":
            return raw(g, b, "nt"), raw(a, g, "tn")
        if form == "nt":
            return raw(g, b, "nn"), raw(g, a, "tn")
        return raw(b, g, "nt"), raw(a, g, "nn")

    dot.defvjp(fwd, bwd)
    return dot


_bdot = _make_dot(False)
_hdot = _make_dot(True)


def _inv_unit_lower(low):
    n = low.shape[-1]
    eye = (lax.broadcasted_iota(jnp.int32, (n, n), 0) == lax.broadcasted_iota(jnp.int32, (n, n), 1)).astype(F32)
    p = -low
    inv = eye + p
    ps = _split(p)
    k = 1
    while 2 * k < n:
        ps = _split(_dg3(ps, ps, 1, 0))
        inv = inv + _dg3(_split(inv), ps, 1, 0)
        k *= 2
    return inv


@jax.custom_vjp
def _unit_lower_solve(low, rhs, inv):
    return _dg3(_split(inv), _split(rhs), 1, 0)


def _uls_fwd(low, rhs, inv):
    sol = _dg3(_split(inv), _split(rhs), 1, 0)
    return sol, (inv, sol)


def _uls_bwd(res, g):
    inv, sol = res
    d_rhs = _dg3(_split(inv), _split(g), 0, 0)
    return -_dg3(_split(d_rhs), _split(sol), 1, 1), d_rhs, jnp.zeros_like(inv)


_unit_lower_solve.defvjp(_uls_fwd, _uls_bwd)


def _sigmoid(x):
    return 0.5 * jnp.tanh(0.5 * x) + 0.5


def _softplus(x):
    return jnp.maximum(x, 0.0) + jnp.log(1.0 + jnp.exp(-jnp.maximum(x, -x)))


def _gelu(x):
    return 0.5 * x * (1.0 + jnp.tanh(0.7978845608028654 * (x + 0.044715 * (x * x * x))))


def _tri(n):
    ri = lax.broadcasted_iota(jnp.int32, (n, n), 0)
    ci = lax.broadcasted_iota(jnp.int32, (n, n), 1)
    return ri, ci


def _stack(ts):
    return jnp.concatenate([t[None] for t in ts], axis=0)


@jax.custom_vjp
def _head_sums(x):
    ri, ci = _tri(x.shape[-1])
    shift = HD.bit_length() - 1
    ones = (jnp.right_shift(ri, shift) == jnp.right_shift(ci, shift)).astype(BF16)
    hi, lo = _split(x)
    return _dg(hi, ones, 1, 0) + _dg(lo, ones, 1, 0)


_head_sums.defvjp(lambda x: (_head_sums(x), None), lambda _, g: (_head_sums(g),))


def _chunk_mask_dot(x, c, running, transpose):
    ri, ci = _tri(x.shape[0])
    shift = c.bit_length() - 1
    mask = jnp.right_shift(ri, shift) == jnp.right_shift(ci, shift)
    if running:
        mask = mask & (ri >= ci)
    hi, lo = _split(x)
    m = mask.astype(BF16)
    ca = 0 if transpose else 1
    return _dg(m, hi, ca, 0) + _dg(m, lo, ca, 0)


@functools.partial(jax.custom_vjp, nondiff_argnums=(1, 2))
def _chunk_sums(x, c, running):
    return _chunk_mask_dot(x, c, running, False)


_chunk_sums.defvjp(lambda x, c, running: (_chunk_mask_dot(x, c, running, False), None),
                   lambda c, running, _, g: (_chunk_mask_dot(g, c, running, True),))


def _spread_onto(x, first, transpose):
    ri = lax.broadcasted_iota(jnp.int32, (LANES, G), 0)
    ci = lax.broadcasted_iota(jnp.int32, (LANES, G), 1)
    sel = (ri == first + jnp.right_shift(ci, HD.bit_length() - 1)).astype(BF16)
    hi, lo = _split(x)
    cb = 1 if transpose else 0
    return _dg(hi, sel, 1, cb) + _dg(lo, sel, 1, cb)


@functools.partial(jax.custom_vjp, nondiff_argnums=(1,))
def _spread(x, first):
    return _spread_onto(x, first, False)


_spread.defvjp(lambda x, first: (_spread_onto(x, first, False), None),
               lambda first, _, g: (_spread_onto(g, first, True),))


def _head_columns_dot(x, transpose):
    sel = (lax.broadcasted_iota(jnp.int32, (NH, G), 0)
           == jnp.right_shift(lax.broadcasted_iota(jnp.int32, (NH, G), 1), HD.bit_length() - 1)).astype(BF16)
    hi, lo = _split(x)
    if transpose:
        return _dg(sel, hi, 1, 1) + _dg(sel, lo, 1, 1)
    return _dg(hi, sel, 0, 0) + _dg(lo, sel, 0, 0)


@jax.custom_vjp
def _head_columns(x):
    return _head_columns_dot(x, False)


_head_columns.defvjp(lambda x: (_head_columns_dot(x, False), None), lambda _, g: (_head_columns_dot(g, True),))


def _sgu_chunk(uv, ln_w, ln_b, ws, bs):
    u = _gelu(uv[:, :G])
    v = _gelu(uv[:, G:])
    mu = jnp.mean(v, axis=-1, keepdims=True)
    vc = v - mu
    var = jnp.mean(vc * vc, axis=-1, keepdims=True)
    vn = vc * lax.rsqrt(var + EPS) * ln_w + ln_b
    ri, ci = _tri(SGU_C)
    mixed = [_bdot(jnp.where(ri >= ci, ws[h], 0.0), vn[:, HD * h:HD * (h + 1)], "nn") for h in range(NH)]
    return u * (jnp.concatenate(mixed, axis=1) + _head_columns(bs))


def _dn_tile(q, k, v, ab, z, state, inv, a_log, dt_bias, nw):
    c = DN_C
    tm = q.shape[0]
    cps = tm // c
    ri, ci = _tri(tm)
    shift = c.bit_length() - 1
    same = jnp.right_shift(ri, shift) == jnp.right_shift(ci, shift)
    g4 = -jnp.exp(a_log) * _softplus(ab[:, 0:NH] + dt_bias)
    gates = jnp.concatenate([g4, _sigmoid(ab[:, NH:2 * NH]), jnp.zeros((tm, LANES - 2 * NH), F32)], axis=1)
    g, beta = _spread(gates, 0), _spread(gates, NH)
    gc = _chunk_sums(g, c, True)
    gl = _chunk_sums(g, c, False)
    gr4 = _hdot(g4, (same & (ri <= ci)).astype(F32), "tn")
    pairs = [(slice(c * j, c * (j + 1)), h) for j in range(cps) for h in range(NH)]
    heads = lambda t: _stack([t[rows, HD * h:HD * (h + 1)] for rows, h in pairs])
    qn = q * lax.rsqrt(_head_sums(q * q) + EPS) * (HD ** -0.5)
    kn = k * lax.rsqrt(_head_sums(k * k) + EPS)
    eg = jnp.exp(gc)
    kb = kn * beta
    rhs = jnp.concatenate([heads(v * beta), heads(kb * eg)], axis=2)
    qg, kd = heads(qn * eg), heads(kn * jnp.exp(gl - gc))
    cd = _stack([jnp.exp(gl[c * j:c * j + 1, HD * h:HD * (h + 1)]) for j in range(cps) for h in range(NH)])
    qn, kn, kb, gc = heads(qn), heads(kn), heads(kb), heads(gc)
    gr = _stack([gr4[h:h + 1, rows] for rows, h in pairs])
    r2, c2 = _tri(c)
    decay = jnp.exp(jnp.where(r2 >= c2, gc - gr, -jnp.inf))
    low = jnp.where(r2 > c2, _bdot(kb, kn, "nt") * decay, 0.0)
    if inv is None:
        inv = _inv_unit_lower(low)
    sol = _unit_lower_solve(low, rhs, inv)
    u, w = sol[:, :, :HD], sol[:, :, HD:]
    attn = _bdot(qn, kn, "nt") * decay
    ys = []
    for j in range(cps):
        b = slice(NH * j, NH * (j + 1))
        v_new = u[b] - _bdot(w[b], state, "nn")
        o = _bdot(qg[b], state, "nn") + _bdot(attn[b], v_new, "nn")
        state = state * cd[b] + _bdot(kd[b], v_new, "tn")
        on = o * lax.rsqrt(jnp.mean(o * o, axis=-1, keepdims=True) + EPS) * nw
        ys.append(jnp.concatenate([on[h] for h in range(NH)], axis=1))
    return jnp.concatenate(ys, axis=0) * (z * _sigmoid(z)), state, inv


def _gla_tile(q, k, v, glr, z, state_t, w2, gate_bias, nw):
    c = GLA_C
    tm = q.shape[0]
    cps = tm // c
    w2_rows = jnp.concatenate([w2, jnp.zeros((LANES - w2.shape[0], G), F32)], axis=0)
    pre = _bdot(glr, w2_rows, "nn") + gate_bias
    log_a = (jnp.minimum(pre, 0.0) - jnp.log(1.0 + jnp.exp(-jnp.maximum(pre, -pre)))) * (1.0 / 16.0)
    gcum = _chunk_sums(log_a, c, True)
    row = lax.broadcasted_iota(jnp.int32, (c, G), 0)
    qs = q * (HD ** -0.5)
    qa, ka, qg, kl, dec = [], [], [], [], []
    for j in range(cps):
        rows = slice(c * j, c * (j + 1))
        gj = gcum[rows]
        g_mid = jnp.sum(jnp.where(row == c // 2, gj, 0.0), axis=0, keepdims=True)
        g_last = jnp.sum(log_a[rows], axis=0, keepdims=True)
        qa.append(qs[rows] * jnp.exp(gj - g_mid))
        ka.append(k[rows] * jnp.exp(g_mid - gj))
        qg.append(qs[rows] * jnp.exp(gj))
        kl.append(k[rows] * jnp.exp(g_last - gj))
        dec.append(jnp.exp(g_last))
    heads = lambda ts: _stack([t[:, HD * h:HD * (h + 1)] for t in ts for h in range(NH)])
    qa, ka, qg, kl, dec = heads(qa), heads(ka), heads(qg), heads(kl), heads(dec)
    vh = heads([v[c * j:c * (j + 1)] for j in range(cps)])
    r2, c2 = _tri(c)
    o_intra = _bdot(jnp.where(r2 >= c2, _bdot(qa, ka, "nt"), 0.0), vh, "nn")
    ys = []
    for j in range(cps):
        b = slice(NH * j, NH * (j + 1))
        o = o_intra[b] + _bdot(qg[b], state_t, "nt")
        state_t = state_t * dec[b] + _bdot(vh[b], kl[b], "tn")
        on = o * lax.rsqrt(jnp.mean(o * o, axis=-1, keepdims=True) + EPS) * nw
        ys.append(jnp.concatenate([on[h] for h in range(NH)], axis=1))
    return jnp.concatenate(ys, axis=0) * (z * _sigmoid(z)), state_t


def _rms(x, nw):
    r = lax.rsqrt(jnp.mean(x * x, axis=-1, keepdims=True) + EPS)
    xh = x * r
    return xh * nw, xh, r


def _rms_bwd(g, xh, r, nw):
    gw = g * nw
    return r * (gw - xh * jnp.mean(gw * xh, axis=-1, keepdims=True)), jnp.sum(g * xh, axis=0, keepdims=True)


def _row(tm, w, blk=0):
    return pl.BlockSpec((tm, w), lambda i: (i, blk))


def _in_proj(x, nw, wp, l, tm=512, rider=None):
    t = x.shape[0]

    def body(*refs):
        (x_ref, nw_ref, w_ref, h_ref, p_ref), ride = _split_refs(refs, 3, 2, 0, rider)
        _ride_begin(rider, ride, (t // tm,))
        h = _rms(x_ref[...], nw_ref[l:l + 1, :])[0].astype(BF16)
        h_ref[...] = h
        p_ref[...] = jnp.dot(h, w_ref[...], preferred_element_type=F32)
        _ride_end(rider, ride, (t // tm,))

    specs, operands = _host(
        rider, [_row(tm, D), _resident((DEPTH, D)), _resident((D, P))], [_row(tm, D), _row(tm, P)],
        [jax.ShapeDtypeStruct((t, D), BF16), jax.ShapeDtypeStruct((t, P), F32)], [], [x, nw, wp])
    return pl.pallas_call(body, name="in_proj", grid=(t // tm,), compiler_params=_cparams("arbitrary"),
                          **specs)(*operands)


def _gu_spec(l):
    return pl.BlockSpec((N_DEV, None, GU_SHARD, D), lambda *_: (0, l, 0, 0), pipeline_mode=pl.Buffered(1))


def _out_mlp(x, ys, w_out, nw2, w_gu_t, w_down, l, tm=256):
    t = x.shape[0]

    def body(x_ref, ya, yb, yc, yd, wo_ref, nw_ref, wgu_ref, wd_ref, mix_ref, x1_ref, h2_ref, gu_ref, act_ref, x2_ref):
        mix = jnp.concatenate([ya[...], yb[...], yc[...], yd[...]], axis=1)
        mix_ref[...] = mix
        x1 = x_ref[...] + jnp.dot(mix, wo_ref[...], preferred_element_type=F32)
        x1_ref[...] = x1
        h2 = _rms(x1, nw_ref[l:l + 1, :])[0].astype(BF16)
        h2_ref[...] = h2
        gu = _dg(h2, wgu_ref[...].reshape(2 * FF, D), 1, 1)
        gu_ref[...] = gu.astype(BF16)
        gate, up = gu[:, :FF], gu[:, FF:]
        act = (gate * _sigmoid(gate) * up).astype(BF16)
        act_ref[...] = act
        x2_ref[...] = x1 + jnp.dot(act, wd_ref[...], preferred_element_type=F32)

    return pl.pallas_call(
        body, name="out_mlp", grid=(t // tm,),
        in_specs=[_row(tm, D), _row(tm, G), _row(tm, G), _row(tm, G), _row(tm, G), _layer((D, D), l),
                  _resident((DEPTH, D)), _gu_spec(l), _layer((FF, D), l)],
        out_specs=[_row(tm, D), _row(tm, D), _row(tm, D), _row(tm, 2 * FF), _row(tm, FF), _row(tm, D)],
        out_shape=[jax.ShapeDtypeStruct((t, D), BF16), jax.ShapeDtypeStruct((t, D), F32),
                   jax.ShapeDtypeStruct((t, D), BF16), jax.ShapeDtypeStruct((t, 2 * FF), BF16),
                   jax.ShapeDtypeStruct((t, FF), BF16), jax.ShapeDtypeStruct((t, D), F32)],
        compiler_params=_cparams("parallel"))(x, *ys, w_out, nw2, w_gu_t, w_down)


def _loss_head(x, nw, target, tm=1024):
    t = x.shape[0]
    tm = min(tm, t)

    def body(x_ref, nw_ref, tg_ref, loss_ref, dnw_ref, dx_ref):
        @pl.when(pl.program_id(0) == 0)
        def _():
            loss_ref[...] = jnp.zeros_like(loss_ref)
            dnw_ref[...] = jnp.zeros_like(dnw_ref)
        nw_v = nw_ref[...]
        y, xh, r = _rms(x_ref[...], nw_v)
        err = y - tg_ref[...]
        loss_ref[...] += 0.5 * jnp.sum(err * err) * (1.0 / D)
        dx, dnw = _rms_bwd(err * (1.0 / D), xh, r, nw_v)
        dx_ref[...] = dx
        dnw_ref[...] += dnw

    return pl.pallas_call(
        body, name="loss_head", grid=(t // tm,),
        in_specs=[_row(tm, D), _resident((1, D)), _row(tm, D)],
        out_specs=[_acc((8, LANES)), _acc((1, D)), _row(tm, D)],
        out_shape=[jax.ShapeDtypeStruct((8, LANES), F32), jax.ShapeDtypeStruct((1, D), F32),
                   jax.ShapeDtypeStruct((t, D), F32)],
        compiler_params=_cparams("arbitrary"))(x, nw, target)


def _mlp_out_bwd(dx2, x1, gu, nw2, w_down, w_gu, w_out, l, tm=256, rider=None):
    t = dx2.shape[0]

    def body(*refs):
        (dx2_ref, x1_ref, gu_ref, nw_ref, wd_ref, wgu_ref, wo_ref, dgu_ref, dx1_ref, dmix_ref, dnw_ref), ride = \
            _split_refs(refs, 7, 4, 0, rider)
        _ride_begin(rider, ride, (t // tm,))

        @pl.when(pl.program_id(0) == 0)
        def _():
            dnw_ref[...] = jnp.zeros_like(dnw_ref)
        dx2 = dx2_ref[...]
        dact = _dg(dx2.astype(BF16), wd_ref[...], 1, 1)
        gu = gu_ref[...].astype(F32)
        gate, up = gu[:, :FF], gu[:, FF:]
        s = _sigmoid(gate)
        dgu = jnp.concatenate([dact * up * (s * (1.0 + gate * (1.0 - s))), dact * (gate * s)], axis=1).astype(BF16)
        dgu_ref[...] = dgu
        dh2 = jnp.dot(dgu, wgu_ref[...].reshape(2 * FF, D), preferred_element_type=F32)
        nw_v = nw_ref[l:l + 1, :]
        _, xh, r = _rms(x1_ref[...], nw_v)
        dxn, dnw = _rms_bwd(dh2, xh, r, nw_v)
        dx1 = dx2 + dxn
        dx1_ref[...] = dx1
        dnw_ref[...] += dnw
        dmix_ref[...] = _dg(dx1.astype(BF16), wo_ref[...], 1, 1)
        _ride_end(rider, ride, (t // tm,))

    specs, operands = _host(
        rider, [_row(tm, D), _row(tm, D), _row(tm, 2 * FF), _resident((DEPTH, D)), _layer((FF, D), l), _gu_spec(l),
                _layer((D, D), l)],
        [_row(tm, 2 * FF), _row(tm, D), _row(tm, D), _acc((1, D))],
        [jax.ShapeDtypeStruct((t, 2 * FF), BF16), jax.ShapeDtypeStruct((t, D), F32),
         jax.ShapeDtypeStruct((t, D), F32), jax.ShapeDtypeStruct((1, D), F32)],
        [], [dx2, x1, gu, nw2, w_down, w_gu, w_out])
    return pl.pallas_call(body, name="mlp_out_bwd", grid=(t // tm,), compiler_params=_cparams("arbitrary"),
                          **specs)(*operands)


_DP_WIDTHS = (768, 768, 512, 768, 256, 256, 128, 128)


def _in_proj_bwd(dps, x, dx1, nw, wp, l, tm=512):
    t = x.shape[0]

    def body(*refs):
        dp_refs, (x_ref, dx1_ref, nw_ref, w_ref, dp_ref, dx_ref, dnw_ref) = refs[:8], refs[8:]

        @pl.when(pl.program_id(0) == 0)
        def _():
            dnw_ref[...] = jnp.zeros_like(dnw_ref)
        dp = jnp.concatenate([r[...] for r in dp_refs], axis=1)
        dp_ref[...] = dp
        dh = _dg(dp, w_ref[...], 1, 1)
        nw_v = nw_ref[l:l + 1, :]
        _, xh, r = _rms(x_ref[...], nw_v)
        dxn, dnw = _rms_bwd(dh, xh, r, nw_v)
        dx_ref[...] = dx1_ref[...] + dxn
        dnw_ref[...] += dnw

    return pl.pallas_call(
        body, name="in_proj_bwd", grid=(t // tm,),
        in_specs=[_row(tm, w) for w in _DP_WIDTHS] + [_row(tm, D), _row(tm, D), _resident((DEPTH, D)), _resident((D, P))],
        out_specs=[_row(tm, P), _row(tm, D), _acc((1, D))],
        out_shape=[jax.ShapeDtypeStruct((t, P), BF16), jax.ShapeDtypeStruct((t, D), F32),
                   jax.ShapeDtypeStruct((1, D), F32)],
        compiler_params=_cparams("arbitrary"))(*dps, x, dx1, nw, wp)


def _accumulate(acc, o_ref, t_axis, term):
    @pl.when(pl.program_id(t_axis) == 0)
    def _():
        acc[...] = jnp.zeros_like(acc)
    acc[...] += term

    @pl.when(pl.program_id(t_axis) == pl.num_programs(t_axis) - 1)
    def _():
        o_ref[...] = acc[...].astype(o_ref.dtype)


WGRAD_TOKENS = 2048


WGRAD_ROWS = 2 * GU_SHARD


def _wgrad_rows(a, b, name, tt=WGRAD_TOKENS, rows=WGRAD_ROWS, out_dtype=BF16, rider=None):
    t, m = a.shape
    tt = min(tt, t)
    grid = (m // rows, t // tt)

    def body(*refs):
        (a_ref, b_ref, o_ref, acc), ride = _split_refs(refs, 2, 1, 1, rider)
        _ride_begin(rider, ride, grid)
        _accumulate(acc, o_ref, 1, _dg(a_ref[...], b_ref[...].astype(BF16), 0, 0))
        _ride_end(rider, ride, grid)

    specs, operands = _host(
        rider, [pl.BlockSpec((tt, rows), lambda j, i: (i, j)), pl.BlockSpec((tt, D), lambda j, i: (i, 0))],
        [pl.BlockSpec((rows, D), lambda j, i: (j, 0))], [jax.ShapeDtypeStruct((m, D), out_dtype)],
        [pltpu.VMEM((rows, D), F32)], [a, b])
    out = pl.pallas_call(body, name=name, grid=grid, compiler_params=_cparams("arbitrary", "arbitrary"), **specs)(*operands)
    return out[0] if rider is None else out


def _relayout_w_in(g, tr=256):
    def body(w_ref, o_ref):
        full = jnp.concatenate([w_ref[d] for d in range(N_DEV)], axis=1)
        parts = []
        for a, b, z in _PAD_SEGS:
            parts.append(full[:, a:b])
            if z:
                parts.append(jnp.zeros((tr, z), full.dtype))
        o_ref[...] = jnp.concatenate(parts, axis=1)

    return pl.pallas_call(
        body, name="relayout_w_in", grid=(D // tr,),
        in_specs=[pl.BlockSpec((N_DEV, tr, IN_SHARD), lambda i: (0, i, 0))], out_specs=_row(tr, P),
        out_shape=jax.ShapeDtypeStruct((D, P), g.dtype), compiler_params=_cparams("parallel"))(g)


IN_ROWS = 432


def _scatter_w_in_grad(gp_t, tc=256):
    def body(g_ref, o_ref):
        g = g_ref[...]
        pieces, off = [], 0
        for a, b, z in _PAD_SEGS:
            pieces.append((a, g[off:off + (b - a)]))
            off += (b - a) + z
        spill = -(-(IN_SHARD * (N_DEV - 1) + IN_ROWS - IN_COLS) // 8) * 8
        nat = jnp.concatenate([piece for _, piece in sorted(pieces, key=lambda ap: ap[0])]
                              + [jnp.zeros((spill, tc), F32)], axis=0)
        for d in range(N_DEV):
            o_ref[d] = nat[IN_SHARD * d:IN_SHARD * d + IN_ROWS].astype(BF16)

    return pl.pallas_call(
        body, name="scatter_w_in_grad", grid=(D // tc,),
        in_specs=[pl.BlockSpec((P, tc), lambda i: (0, i))],
        out_specs=pl.BlockSpec((N_DEV, IN_ROWS, tc), lambda i: (0, 0, i)),
        out_shape=jax.ShapeDtypeStruct((N_DEV, IN_ROWS, D), BF16),
        compiler_params=_cparams("parallel"))(gp_t)


_A_BLK = 3


def _sgu_specs(l):
    return [_resident((DEPTH, G)), _resident((DEPTH, G)), _layer((NH, SGU_C, SGU_C), l), _layer((NH, SGU_C), l)]


def _sgu_fwd(p, ln_w, ln_b, ws, bs, l, tm=512):
    t = p.shape[0]

    def body(uv_ref, lw_ref, lb_ref, ws_ref, bs_ref, y_ref):
        ws_v = [ws_ref[h] for h in range(NH)]
        for c in range(tm // SGU_C):
            rows = pl.ds(c * SGU_C, SGU_C)
            y_ref[rows, :] = _sgu_chunk(uv_ref[rows, :], lw_ref[l:l + 1, :], lb_ref[l:l + 1, :], ws_v,
                                        bs_ref[...]).astype(BF16)

    return pl.pallas_call(
        body, name="sgu_fwd", grid=(t // tm,),
        in_specs=[_row(tm, 2 * G, _A_BLK)] + _sgu_specs(l), out_specs=_row(tm, G),
        out_shape=jax.ShapeDtypeStruct((t, G), BF16),
        compiler_params=_cparams("parallel"))(p, ln_w, ln_b, ws, bs)


def _sgu_bwd(p, dmix, ln_w, ln_b, ws, bs, l, tm=512):
    t = p.shape[0]

    def body(uv_ref, dy_ref, lw_ref, lb_ref, ws_ref, bs_ref, duv_ref, dlw_ref, dlb_ref, dws_ref, dbs_ref):
        @pl.when(pl.program_id(0) == 0)
        def _():
            for r in (dlw_ref, dlb_ref, dws_ref, dbs_ref):
                r[...] = jnp.zeros_like(r)
        ws_v = [ws_ref[h] for h in range(NH)]
        for c in range(tm // SGU_C):
            rows = pl.ds(c * SGU_C, SGU_C)
            _, vjp = jax.vjp(_sgu_chunk, uv_ref[rows, :], lw_ref[l:l + 1, :], lb_ref[l:l + 1, :], ws_v, bs_ref[...])
            duv, dlw, dlb, dws, dbs = vjp(dy_ref[rows, :])
            duv_ref[rows, :] = duv.astype(BF16)
            dlw_ref[...] += dlw
            dlb_ref[...] += dlb
            dbs_ref[...] += dbs
            for h in range(NH):
                dws_ref[h] += dws[h]

    return pl.pallas_call(
        body, name="sgu_bwd", grid=(t // tm,),
        in_specs=[_row(tm, 2 * G, _A_BLK), _row(tm, G, 0)] + _sgu_specs(l),
        out_specs=[_row(tm, 2 * G), _acc((1, G)), _acc((1, G)), _acc((NH, SGU_C, SGU_C)), _acc((NH, SGU_C))],
        out_shape=[jax.ShapeDtypeStruct((t, 2 * G), BF16), jax.ShapeDtypeStruct((1, G), F32),
                   jax.ShapeDtypeStruct((1, G), F32), jax.ShapeDtypeStruct((NH, SGU_C, SGU_C), F32),
                   jax.ShapeDtypeStruct((NH, SGU_C), F32)],
        compiler_params=_cparams("arbitrary"))(p, dmix, ln_w, ln_b, ws, bs)


HALO = 8


def _prev_halo(tm, width, col):
    return pl.BlockSpec((HALO, width), lambda i: (jnp.maximum(i * (tm // HALO) - 1, 0), col))


def _next_halo(tm, width, col, t):
    return pl.BlockSpec((HALO, width), lambda i: (jnp.minimum((i + 1) * (tm // HALO), t // HALO - 1), col))


def _with_prev(halo_ref, x_ref):
    halo = jnp.where(pl.program_id(0) == 0, 0.0, halo_ref[...])
    return jnp.concatenate([halo, x_ref[...]], axis=0)


def _with_next(x_ref, halo_ref):
    halo = jnp.where(pl.program_id(0) == pl.num_programs(0) - 1, 0.0, halo_ref[...])
    return jnp.concatenate([x_ref[...], halo], axis=0)


def _delay(ext, j):
    return ext if j == 0 else pltpu.roll(ext, j, axis=0)


def _advance(ext, j):
    return ext if j == 0 else pltpu.roll(ext, ext.shape[0] - j, axis=0)


def _causal_conv(ext, w, tm):
    kw = w.shape[0]
    out = None
    for k in range(kw):
        term = _delay(ext, kw - 1 - k)[HALO:HALO + tm] * w[k:k + 1, :]
        out = term if out is None else out + term
    return out


def _causal_conv_t(ext, w, tm):
    kw = w.shape[0]
    out = None
    for k in range(kw):
        term = _advance(ext, kw - 1 - k)[0:tm] * w[k:k + 1, :]
        out = term if out is None else out + term
    return out


def _conv_wgrad(ext, dy, kw, tm):
    return jnp.concatenate(
        [jnp.sum(_delay(ext, kw - 1 - k)[HALO:HALO + tm] * dy, axis=0, keepdims=True) for k in range(kw)], axis=0)


_B_GB, _B_GC, _B_H = 8, 9, 10
_C_QKV, _D_QKV = 0, 1
_C_Z, _D_Z = 11, 12
_C_AB, _D_GLR = 26, 27


def _sconv_fwd(p, w, l, tm=1024):
    t = p.shape[0]
    tm = min(tm, t)

    def body(gb_ref, gc_ref, h_ref, gc_halo, h_halo, w_ref, y_ref):
        s_ext = _with_prev(gc_halo, gc_ref) * _with_prev(h_halo, h_ref)
        y_ref[...] = (gb_ref[...] * _causal_conv(s_ext, w_ref[...], tm)).astype(BF16)

    return pl.pallas_call(
        body, name="sconv_fwd", grid=(t // tm,),
        in_specs=[_row(tm, G, _B_GB), _row(tm, G, _B_GC), _row(tm, G, _B_H), _prev_halo(tm, G, _B_GC),
                  _prev_halo(tm, G, _B_H), _layer((3, G), l)],
        out_specs=_row(tm, G), out_shape=jax.ShapeDtypeStruct((t, G), BF16),
        compiler_params=_cparams("parallel"))(p, p, p, p, p, w)


def _sconv_bwd(p, dmix, w, l, tm=1024):
    t = p.shape[0]
    tm = min(tm, t)

    def body(gb_ref, gc_ref, h_ref, gc_halo, h_halo, gb_next, dy_ref, dy_next, w_ref, dp_ref, dw_ref):
        @pl.when(pl.program_id(0) == 0)
        def _():
            dw_ref[...] = jnp.zeros_like(dw_ref)
        w_v = w_ref[...]
        s_ext = _with_prev(gc_halo, gc_ref) * _with_prev(h_halo, h_ref)
        dout = dy_ref[...]
        d_gb = dout * _causal_conv(s_ext, w_v, tm)
        dconv_ext = _with_next(dy_ref, dy_next) * _with_next(gb_ref, gb_next)
        ds = _causal_conv_t(dconv_ext, w_v, tm)
        dw_ref[...] += _conv_wgrad(s_ext, dconv_ext[0:tm], 3, tm)
        dp_ref[...] = jnp.concatenate([d_gb, ds * h_ref[...], ds * gc_ref[...]], axis=1).astype(BF16)

    return pl.pallas_call(
        body, name="sconv_bwd", grid=(t // tm,),
        in_specs=[_row(tm, G, _B_GB), _row(tm, G, _B_GC), _row(tm, G, _B_H), _prev_halo(tm, G, _B_GC),
                  _prev_halo(tm, G, _B_H), _next_halo(tm, G, _B_GB, t), _row(tm, G, 1), _next_halo(tm, G, 1, t),
                  _layer((3, G), l)],
        out_specs=[_row(tm, 3 * G), _acc((3, G))],
        out_shape=[jax.ShapeDtypeStruct((t, 3 * G), BF16), jax.ShapeDtypeStruct((3, G), F32)],
        compiler_params=_cparams("arbitrary"))(p, p, p, p, p, p, dmix, dmix, w)


def _qkv_conv_fwd(p, w, l, tm=1024):
    t = p.shape[0]
    tm = min(tm, t)

    def body(x_ref, x_halo, w_ref, y_ref):
        c = _causal_conv(_with_prev(x_halo, x_ref), w_ref[...], tm)
        y_ref[...] = c * _sigmoid(c)

    return pl.pallas_call(
        body, name="qkv_conv_fwd", grid=(t // tm,),
        in_specs=[_row(tm, 3 * G, _C_QKV), _prev_halo(tm, 3 * G, _C_QKV), _layer((4, 3 * G), l)],
        out_specs=_row(tm, 3 * G), out_shape=jax.ShapeDtypeStruct((t, 3 * G), F32),
        compiler_params=_cparams("parallel"))(p, p, w)


def _qkv_conv_bwd(p, dy, w, l, tm=1024):
    t = p.shape[0]
    tm = min(tm, t)

    def body(x_ref, x_halo, x_next, dy_ref, dy_next, w_ref, dx_ref, dw_ref):
        @pl.when(pl.program_id(0) == 0)
        def _():
            dw_ref[...] = jnp.zeros_like(dw_ref)
        w_v = w_ref[...]
        x_all = jnp.concatenate([_with_prev(x_halo, x_ref), jnp.where(
            pl.program_id(0) == pl.num_programs(0) - 1, 0.0, x_next[...])], axis=0)
        c = _causal_conv(x_all, w_v, tm + HALO)
        s = _sigmoid(c)
        dc_ext = _with_next(dy_ref, dy_next) * (s * (1.0 + c * (1.0 - s)))
        dx_ref[...] = _causal_conv_t(dc_ext, w_v, tm).astype(BF16)
        dw_ref[...] += _conv_wgrad(x_all[0:HALO + tm], dc_ext[0:tm], 4, tm)

    return pl.pallas_call(
        body, name="qkv_conv_bwd", grid=(t // tm,),
        in_specs=[_row(tm, 3 * G, _C_QKV), _prev_halo(tm, 3 * G, _C_QKV), _next_halo(tm, 3 * G, _C_QKV, t),
                  _row(tm, 3 * G), _next_halo(tm, 3 * G, 0, t), _layer((4, 3 * G), l)],
        out_specs=[_row(tm, 3 * G), _acc((4, 3 * G))],
        out_shape=[jax.ShapeDtypeStruct((t, 3 * G), BF16), jax.ShapeDtypeStruct((4, 3 * G), F32)],
        compiler_params=_cparams("arbitrary"))(p, p, p, dy, dy, w)


_STATE = pl.BlockSpec((1, NH, HD, HD), lambda i: (i, 0, 0, 0))


def _dn_specs():
    return [_resident((DEPTH, NH)), _resident((DEPTH, NH)), _resident((DEPTH, HD))]


def _dn_fwd(qkv, p, params, l, cps=8, rider=None):
    t = qkv.shape[0]
    tm = DN_C * cps
    nb = cps * NH

    def body(*refs):
        (qkv_ref, z_ref, ab_ref, alog_ref, dt_ref, nw_ref, y_ref, st_ref, inv_ref, state), ride = _split_refs(
            refs, 6, 3, 1, rider)
        _ride_begin(rider, ride, (t // tm,))

        @pl.when(pl.program_id(0) == 0)
        def _():
            state[...] = jnp.zeros_like(state)
        st_ref[0] = state[...]
        y, new, inv = _dn_tile(qkv_ref[:, 0:G], qkv_ref[:, G:2 * G], qkv_ref[:, 2 * G:3 * G], ab_ref[...], z_ref[...],
                               state[...], None, alog_ref[l:l + 1, :], dt_ref[l:l + 1, :], nw_ref[l:l + 1, :])
        y_ref[...] = y.astype(BF16)
        inv_ref[...] = inv
        state[...] = new
        _ride_end(rider, ride, (t // tm,))

    specs, operands = _host(
        rider, [_row(tm, 3 * G), _row(tm, G, _C_Z), _row(tm, LANES, _C_AB)] + _dn_specs(),
        [_row(tm, G), _STATE, pl.BlockSpec((nb, DN_C, DN_C), lambda i: (i, 0, 0))],
        [jax.ShapeDtypeStruct((t, G), BF16), jax.ShapeDtypeStruct((t // tm, NH, HD, HD), F32),
         jax.ShapeDtypeStruct((t // DN_C * NH, DN_C, DN_C), F32)],
        [pltpu.VMEM((NH, HD, HD), F32)], [qkv, p, p, *params])
    return pl.pallas_call(body, name="dn_fwd", grid=(t // tm,), compiler_params=_cparams("arbitrary"), **specs)(*operands)


def _dn_bwd(qkv, p, states, inv, dmix, params, l, cps=8, rider=None):
    t = qkv.shape[0]
    tm = DN_C * cps
    n = t // tm
    nb = cps * NH

    def body(*refs):
        (qkv_ref, z_ref, ab_ref, st_ref, inv_ref, dy_ref, alog_ref, dt_ref, nw_ref,
         dqkv_ref, dz_ref, dab_ref, dalog_ref, ddt_ref, dnw_ref, dstate), ride = _split_refs(refs, 9, 6, 1, rider)
        _ride_begin(rider, ride, (n,))
        dprm_refs = (dalog_ref, ddt_ref, dnw_ref)

        @pl.when(pl.program_id(0) == 0)
        def _():
            dstate[...] = jnp.zeros_like(dstate)
            for r in dprm_refs:
                r[...] = jnp.zeros_like(r)
        inv_v = inv_ref[...]
        tile = lambda q, k, v, ab, z, st, alog, dt, nw: _dn_tile(q, k, v, ab, z, st, inv_v, alog, dt, nw)[:2]
        _, vjp = jax.vjp(tile, qkv_ref[:, 0:G], qkv_ref[:, G:2 * G], qkv_ref[:, 2 * G:3 * G], ab_ref[...], z_ref[...],
                         st_ref[0], alog_ref[l:l + 1, :], dt_ref[l:l + 1, :], nw_ref[l:l + 1, :])
        dq, dk, dv, dab, dz, dst, *dprm = vjp((dy_ref[...], dstate[...]))
        dqkv_ref[...] = jnp.concatenate([dq, dk, dv], axis=1)
        dz_ref[...] = dz.astype(BF16)
        dab_ref[...] = dab.astype(BF16)
        dstate[...] = dst
        for r, g in zip(dprm_refs, dprm):
            r[...] += g
        _ride_end(rider, ride, (n,))

    rev = lambda w, blk: pl.BlockSpec((tm, w), lambda i: (n - 1 - i, blk))
    specs, operands = _host(
        rider, [rev(3 * G, 0), rev(G, _C_Z), rev(LANES, _C_AB),
                pl.BlockSpec((1, NH, HD, HD), lambda i: (n - 1 - i, 0, 0, 0)),
                pl.BlockSpec((nb, DN_C, DN_C), lambda i: (n - 1 - i, 0, 0)), rev(G, 2)] + _dn_specs(),
        [rev(3 * G, 0), rev(G, 0), rev(LANES, 0), _acc((1, NH)), _acc((1, NH)), _acc((1, HD))],
        [jax.ShapeDtypeStruct((t, 3 * G), F32), jax.ShapeDtypeStruct((t, G), BF16),
         jax.ShapeDtypeStruct((t, LANES), BF16), jax.ShapeDtypeStruct((1, NH), F32),
         jax.ShapeDtypeStruct((1, NH), F32), jax.ShapeDtypeStruct((1, HD), F32)],
        [pltpu.VMEM((NH, HD, HD), F32)], [qkv, p, p, states, inv, dmix, *params])
    return pl.pallas_call(body, name="dn_bwd", grid=(n,), compiler_params=_cparams("arbitrary"), **specs)(*operands)


def _gla_specs(l):
    return [_layer((16, G), l), _resident((DEPTH, G)), _resident((DEPTH, HD))]


def _gla_fwd(p, params, l, cps=8, rider=None):
    t = p.shape[0]
    tm = GLA_C * cps

    def body(*refs):
        (qkv_ref, z_ref, glr_ref, w2_ref, gb_ref, nw_ref, y_ref, st_ref, state), ride = _split_refs(refs, 6, 2, 1, rider)
        _ride_begin(rider, ride, (t // tm,))

        @pl.when(pl.program_id(0) == 0)
        def _():
            state[...] = jnp.zeros_like(state)
        st_ref[0] = state[...]
        y, new = _gla_tile(qkv_ref[:, 0:G], qkv_ref[:, G:2 * G], qkv_ref[:, 2 * G:3 * G], glr_ref[...], z_ref[...],
                           state[...], w2_ref[...], gb_ref[l:l + 1, :], nw_ref[l:l + 1, :])
        y_ref[...] = y.astype(BF16)
        state[...] = new
        _ride_end(rider, ride, (t // tm,))

    specs, operands = _host(
        rider, [_row(tm, 3 * G, _D_QKV), _row(tm, G, _D_Z), _row(tm, LANES, _D_GLR)] + _gla_specs(l),
        [_row(tm, G), _STATE],
        [jax.ShapeDtypeStruct((t, G), BF16), jax.ShapeDtypeStruct((t // tm, NH, HD, HD), F32)],
        [pltpu.VMEM((NH, HD, HD), F32)], [p, p, p, *params])
    return pl.pallas_call(body, name="gla_fwd", grid=(t // tm,), compiler_params=_cparams("arbitrary"), **specs)(*operands)


def _gla_bwd(p, states, dmix, params, l, cps=8):
    t = p.shape[0]
    tm = GLA_C * cps
    n = t // tm

    def body(qkv_ref, z_ref, glr_ref, st_ref, dy_ref, w2_ref, gb_ref, nw_ref,
             dqkv_ref, dz_ref, dglr_ref, dw2_ref, dgb_ref, dnw_ref, dstate):
        dprm_refs = (dw2_ref, dgb_ref, dnw_ref)

        @pl.when(pl.program_id(0) == 0)
        def _():
            dstate[...] = jnp.zeros_like(dstate)
            for r in dprm_refs:
                r[...] = jnp.zeros_like(r)
        _, vjp = jax.vjp(_gla_tile, qkv_ref[:, 0:G], qkv_ref[:, G:2 * G], qkv_ref[:, 2 * G:3 * G], glr_ref[...],
                         z_ref[...], st_ref[0], w2_ref[...], gb_ref[l:l + 1, :], nw_ref[l:l + 1, :])
        dq, dk, dv, dglr, dz, dst, *dprm = vjp((dy_ref[...], dstate[...]))
        dqkv_ref[...] = jnp.concatenate([dq, dk, dv], axis=1).astype(BF16)
        dz_ref[...] = dz.astype(BF16)
        dglr_ref[...] = dglr.astype(BF16)
        dstate[...] = dst
        for r, g in zip(dprm_refs, dprm):
            r[...] += g

    rev = lambda w, blk: pl.BlockSpec((tm, w), lambda i: (n - 1 - i, blk))
    return pl.pallas_call(
        body, name="gla_bwd", grid=(n,),
        in_specs=[rev(3 * G, _D_QKV), rev(G, _D_Z), rev(LANES, _D_GLR),
                  pl.BlockSpec((1, NH, HD, HD), lambda i: (n - 1 - i, 0, 0, 0)), rev(G, 3)] + _gla_specs(l),
        out_specs=[rev(3 * G, 0), rev(G, 0), rev(LANES, 0), _acc((16, G)), _acc((1, G)), _acc((1, HD))],
        out_shape=[jax.ShapeDtypeStruct((t, 3 * G), BF16), jax.ShapeDtypeStruct((t, G), BF16),
                   jax.ShapeDtypeStruct((t, LANES), BF16), jax.ShapeDtypeStruct((16, G), F32),
                   jax.ShapeDtypeStruct((1, G), F32), jax.ShapeDtypeStruct((1, HD), F32)],
        scratch_shapes=[pltpu.VMEM((NH, HD, HD), F32)],
        compiler_params=_cparams("arbitrary"))(p, p, p, states, dmix, *params)


def _adamw_math(w, g, m, v):
    m = ADAM_B1 * m + (1.0 - ADAM_B1) * g
    v = ADAM_B2 * v + (1.0 - ADAM_B2) * (g * g)
    m_hat = m / (1.0 - ADAM_B1 ** ADAM_STEP)
    v_hat = v / (1.0 - ADAM_B2 ** ADAM_STEP)
    return -ADAM_LR * (m_hat / (jnp.sqrt(v_hat) + ADAM_EPS) + ADAM_WD * w), m, v


def _adamw_large(parts, w, m, v, name, tr):
    _, r, c = w.shape

    def body(p0_ref, p1_ref, w_ref, m_ref, v_ref, g_ref, d_ref, nm_ref, nv_ref):
        def total(p_ref):
            g = p_ref[0].astype(F32)
            for k in range(1, N_DEV):
                g = g + p_ref[k].astype(F32)
            return g

        g = jnp.where(pl.program_id(0) == 0, total(p0_ref), total(p1_ref))
        g_ref[...] = g
        d_ref[...], nm_ref[...], nv_ref[...] = _adamw_math(w_ref[...], g, m_ref[...], v_ref[...])

    blk = pl.BlockSpec((None, tr, c), lambda l, i: (l, i, 0))
    part = pl.BlockSpec((N_DEV, tr, c), lambda l, i: (0, i, 0))
    return pl.pallas_call(
        body, name=name, grid=(DEPTH, r // tr), in_specs=[part, part, blk, blk, blk],
        out_specs=[blk] * 4, out_shape=[jax.ShapeDtypeStruct(w.shape, F32)] * 4,
        compiler_params=_cparams("parallel", "parallel"))(*parts, w, m, v)


def _adamw_w_in(parts, w, m, v, tc=256):
    def body(p0_ref, p1_ref, w_ref, m_ref, v_ref, g_ref, d_ref, nm_ref, nv_ref):
        for l, p_ref in enumerate((p0_ref, p1_ref)):
            g = p_ref[0, 0:IN_SHARD, :].astype(F32)
            for k in range(1, N_DEV):
                g = g + p_ref[k, 0:IN_SHARD, :].astype(F32)
            g_ref[:, l, :] = g
            d_ref[:, l, :], nm_ref[:, l, :], nv_ref[:, l, :] = _adamw_math(w_ref[:, l, :], g, m_ref[:, l, :], v_ref[:, l, :])

    blk = pl.BlockSpec((IN_SHARD, DEPTH, tc), lambda i: (0, 0, i))
    part = pl.BlockSpec((N_DEV, IN_ROWS, tc), lambda i: (0, 0, i))
    return pl.pallas_call(
        body, name="adamw_w_in", grid=(D // tc,), in_specs=[part, part, blk, blk, blk], out_specs=[blk] * 4,
        out_shape=[jax.ShapeDtypeStruct(w.shape, F32)] * 4, compiler_params=_cparams("parallel"))(*parts, w, m, v)


_SLAB_AT = {
    "sgu_w_spatial": (0, 0, SGU_C, LANES),
    "sgu_b_spatial": (128, 0, NH, SGU_C),
    "norm1_w": (132, 0, 1, D),
    "norm2_w": (133, 0, 1, D),
    "sgu_ln_w": (134, 0, 1, G),
    "sgu_ln_b": (134, 256, 1, G),
    "gla_gate_bias": (134, 512, 1, G),
    "dn_norm_w": (134, 768, 1, HD),
    "gla_norm_w": (134, 896, 1, HD),
    "dn_a_log": (135, 0, 1, NH),
    "dn_dt_bias": (135, 128, 1, NH),
    "gla_w_gate2": (136, 0, 16, G),
    "dn_conv_w": (136, 256, 4, 3 * G),
    "sc_conv_w": (140, 256, 3, G),
}
_LAYER_ROWS = 152
_FINAL_ROW = DEPTH * _LAYER_ROWS
_SLAB_ROWS, _SLAB_COLS = _FINAL_ROW + 8, 1024
_SLAB_ORDER = tuple(_SLAB_AT)
_SHARDED_SMALL = ("sc_conv_w", "dn_conv_w", "gla_w_gate2")
_REPLICATED = tuple(k for k in _SLAB_ORDER if k not in _SHARDED_SMALL)


def _region(name, l, h=0):
    r0, c0, nr, nc = _SLAB_AT[name]
    return slice(_LAYER_ROWS * l + r0, _LAYER_ROWS * l + r0 + nr), slice(c0 + nc * h, c0 + nc * (h + 1))


def _pack_small(layer_grads, d_final):
    def body(*refs):
        slab = refs[-1]
        slab[...] = jnp.zeros_like(slab)
        it = iter(refs[:-1])
        for l in range(DEPTH):
            for name in _SLAB_ORDER:
                ref = next(it)
                if name == "sgu_w_spatial":
                    for h in range(NH):
                        slab[_region(name, l, h)] = ref[h]
                else:
                    slab[_region(name, l)] = ref[...]
        slab[_FINAL_ROW:_FINAL_ROW + 1, :] = next(it)[...]

    flat = [layer_grads[l][name] for l in range(DEPTH) for name in _SLAB_ORDER] + [d_final]
    return pl.pallas_call(body, name="pack_small_grads", out_shape=jax.ShapeDtypeStruct((_SLAB_ROWS, _SLAB_COLS), F32),
                          compiler_params=_cparams())(*flat)


def _adamw_small(parts, w, m, v):
    names = _REPLICATED + ("final_norm_w",)
    n_in = len(names)

    def body(*refs):
        def total(rows, cols):
            g = refs[0][0, rows, cols]
            for k in range(1, N_DEV):
                g = g + refs[0][k, rows, cols]
            return g

        w_refs = dict(zip(names, refs[1:1 + n_in]))
        m_refs = dict(zip(names, refs[1 + n_in:1 + 2 * n_in]))
        v_refs = dict(zip(names, refs[1 + 2 * n_in:1 + 3 * n_in]))
        outs = refs[1 + 3 * n_in:]
        out_refs = [dict(zip(names, outs[j * n_in:(j + 1) * n_in])) for j in range(4)]
        full_refs = dict(zip(_SHARDED_SMALL, outs[4 * n_in:]))

        def update(name, idx, g):
            res = (g,) + _adamw_math(w_refs[name][idx], g, m_refs[name][idx], v_refs[name][idx])
            for o, val in zip(out_refs, res):
                o[name][idx] = val

        for l in range(DEPTH):
            for name in _REPLICATED:
                if name == "sgu_w_spatial":
                    for h in range(NH):
                        update(name, (l, h), total(*_region(name, l, h)))
                elif name == "sgu_b_spatial":
                    update(name, (l,), total(*_region(name, l)))
                else:
                    update(name, (slice(l, l + 1), slice(None)), total(*_region(name, l)))
            for name in _SHARDED_SMALL:
                full_refs[name][l] = total(*_region(name, l))
        update("final_norm_w", (slice(None), slice(None)), total(slice(_FINAL_ROW, _FINAL_ROW + 1), slice(None)))

    shapes = [jax.ShapeDtypeStruct(w[k].shape, F32) for k in names]
    full_shapes = [jax.ShapeDtypeStruct((DEPTH,) + _SLAB_AT[k][2:], F32) for k in _SHARDED_SMALL]
    outs = pl.pallas_call(body, name="adamw_small", out_shape=shapes * 4 + full_shapes, compiler_params=_cparams())(
        parts, *[w[k] for k in names], *[m[k] for k in names], *[v[k] for k in names])
    result = [dict(zip(names, outs[j * n_in:(j + 1) * n_in])) for j in range(4)]
    return result, dict(zip(_SHARDED_SMALL, outs[4 * n_in:]))


def _adamw_shards(g, w, m, v):
    names = _SHARDED_SMALL
    n = len(names)

    def body(*refs):
        for j in range(n):
            g_v = refs[j][...]
            res = _adamw_math(refs[n + j][...], g_v, refs[2 * n + j][...], refs[3 * n + j][...])
            for o, val in zip(refs[4 * n + j::n], res):
                o[...] = val

    shapes = [jax.ShapeDtypeStruct(w[k].shape, F32) for k in names]
    outs = pl.pallas_call(body, name="adamw_small_shards", out_shape=shapes * 3, compiler_params=_cparams())(
        *[g[k] for k in names], *[w[k] for k in names], *[m[k] for k in names], *[v[k] for k in names])
    return [dict(zip(names, outs[j * n:(j + 1) * n])) for j in range(3)]


_ANY = pl.BlockSpec(memory_space=pl.ANY)


def _place():
    return lax.axis_index("x"), lax.axis_index("y"), lax.axis_index("c")


def _sems(n):
    return [pltpu.SemaphoreType.DMA((n, 7)), pltpu.SemaphoreType.DMA((n, 7)), pltpu.SemaphoreType.DMA((n,))]


class _Gather:
    def __init__(self, blocks, second, forward_at=1.0):
        self.inputs, self.second, self.forward_at = list(blocks), list(second), forward_at
        self.out_shape = [jax.ShapeDtypeStruct((a.shape[0], N_DEV) + a.shape[1:] if s else (N_DEV,) + a.shape, a.dtype)
                          for a, s in zip(blocks, second)]
        self.scratch = _sems(len(blocks))

    def _copies(self, refs):
        x_refs, out_refs, (send_sems, recv_sems, local_sems) = refs
        n = len(x_refs)
        x, y, c = _place()
        me, sibling = (x, y, c), (x, y, 1 - c)
        chips = [(1 - x, y), (x, 1 - y), (1 - x, 1 - y)]

        def slot(j, px, py, pc):
            idx = 4 * px + 2 * py + pc
            return out_refs[j].at[:, idx] if self.second[j] else out_refs[j].at[idx]

        def copies(k, block, to, own=False):
            return [pltpu.make_async_remote_copy(
                src_ref=x_refs[j] if own else slot(j, *block), dst_ref=slot(j, *block),
                send_sem=send_sems.at[j, k], recv_sem=recv_sems.at[j, k], device_id=to,
                device_id_type=pl.DeviceIdType.MESH) for j in range(n)]

        mine = [pltpu.make_async_copy(x_refs[j], slot(j, *me), local_sems.at[j]) for j in range(n)]
        first = copies(0, me, sibling, own=True)
        for j, chip in enumerate(chips):
            first += copies(1 + j, me, (*chip, c), own=True)
        landed = [copies(1 + j, (*chip, c), me) for j, chip in enumerate(chips)]
        onward = [copies(4 + j, (*chip, c), sibling) for j, chip in enumerate(chips)]
        from_sibling = copies(0, sibling, me)
        for j, chip in enumerate(chips):
            from_sibling += copies(4 + j, (*chip, 1 - c), me)
        return mine, first, landed, onward, from_sibling

    def start(self, refs):
        mine, first, _, _, _ = self._copies(refs)
        for cp in mine + first:
            cp.start()

    def forward(self, refs):
        _, _, landed, onward, _ = self._copies(refs)
        for arrived, passed in zip(landed, onward):
            for cp in arrived:
                cp.wait_recv()
            for cp in passed:
                cp.start()

    def finish(self, refs):
        mine, first, _, onward, from_sibling = self._copies(refs)
        for cp in from_sibling:
            cp.wait_recv()
        for cp in first + [cp for passed in onward for cp in passed]:
            cp.wait_send()
        for cp in mine:
            cp.wait()


class _Exchange:
    forward_at = 1.0

    def __init__(self, sends):
        self.inputs = list(sends)
        self.out_shape = [jax.ShapeDtypeStruct(a.shape, a.dtype) for a in sends]
        self.scratch = _sems(len(sends))

    def _copies(self, refs):
        x_refs, out_refs, (send_sems, recv_sems, local_sems) = refs
        n = len(x_refs)
        x, y, c = _place()
        me = 4 * x + 2 * y + c
        sent, landing = [], []
        for k in range(1, N_DEV):
            px, py, pc = x ^ ((k >> 2) & 1), y ^ ((k >> 1) & 1), c ^ (k & 1)
            them = 4 * px + 2 * py + pc
            for j in range(n):
                for here, there, into in ((them, me, sent), (me, them, landing)):
                    into.append(pltpu.make_async_remote_copy(
                        src_ref=x_refs[j].at[here], dst_ref=out_refs[j].at[there], send_sem=send_sems.at[j, k - 1],
                        recv_sem=recv_sems.at[j, k - 1], device_id=(px, py, pc), device_id_type=pl.DeviceIdType.MESH))
        mine = [pltpu.make_async_copy(x_refs[j].at[me], out_refs[j].at[me], local_sems.at[j]) for j in range(n)]
        return mine, sent, landing

    def start(self, refs):
        mine, sent, _ = self._copies(refs)
        for cp in mine + sent:
            cp.start()

    def forward(self, refs):
        pass

    def finish(self, refs):
        mine, sent, landing = self._copies(refs)
        for cp in landing:
            cp.wait_recv()
        for cp in sent:
            cp.wait_send()
        for cp in mine:
            cp.wait()


def _run(rider, name):
    n_in, n_out = len(rider.inputs), len(rider.out_shape)

    def body(*refs):
        parts = (refs[:n_in], refs[n_in:n_in + n_out], refs[n_in + n_out:])
        rider.start(parts)
        rider.forward(parts)
        rider.finish(parts)

    return pl.pallas_call(body, name=name, out_shape=rider.out_shape, in_specs=[_ANY] * n_in, out_specs=[_ANY] * n_out,
                          scratch_shapes=rider.scratch)(*rider.inputs)


def _split_refs(refs, n_in, n_out, n_scratch, rider):
    r_in = len(rider.inputs) if rider else 0
    r_out = len(rider.out_shape) if rider else 0
    a = n_in + r_in
    b = a + n_out + r_out
    own = refs[:n_in] + refs[a:a + n_out] + refs[b:b + n_scratch]
    return own, (refs[n_in:a], refs[a + n_out:b], refs[b + n_scratch:])


def _grid_step(grid):
    step, stride = 0, 1
    for axis in reversed(range(len(grid))):
        step = step + pl.program_id(axis) * stride
        stride *= grid[axis]
    return step


def _ride_begin(rider, ride_refs, grid):
    if rider is not None:
        pl.when(_grid_step(grid) == 0)(lambda: rider.start(ride_refs))


def _ride_end(rider, ride_refs, grid):
    if rider is not None:
        steps = 1
        for n in grid:
            steps *= n
        step = _grid_step(grid)
        pl.when(step == min(steps - 1, int(steps * rider.forward_at)))(lambda: rider.forward(ride_refs))
        pl.when(step == steps - 1)(lambda: rider.finish(ride_refs))


def _host(rider, in_specs, out_specs, out_shape, scratch, operands):
    if rider is None:
        return dict(in_specs=in_specs, out_specs=out_specs, out_shape=out_shape, scratch_shapes=scratch), operands
    return dict(in_specs=in_specs + [_ANY] * len(rider.inputs), out_specs=out_specs + [_ANY] * len(rider.out_shape),
                out_shape=out_shape + rider.out_shape, scratch_shapes=scratch + rider.scratch), operands + rider.inputs


_LATE = ("w_gate_up", "w_out", "w_down")


def _local_grads(x, target, w, late=None, exchange=False):
    w = dict(w)
    w_in = list(w["w_in"])
    dn_params = (w["dn_a_log"], w["dn_dt_bias"], w["dn_norm_w"])
    gla_params = (w["gla_w_gate2"], w["gla_gate_bias"], w["gla_norm_w"])
    sgu_params = (w["sgu_ln_w"], w["sgu_ln_b"], w["sgu_w_spatial"], w["sgu_b_spatial"])
    saved = []
    for l in range(DEPTH):
        riding = l == 0 and late is not None
        h, p, *got = _in_proj(x, w["norm1_w"], w_in[l], l,
                              rider=_Gather([late["w_down"]], [True], forward_at=0.85) if riding else None)
        if riding:
            w["w_down"] = got[0].reshape(DEPTH, FF, D)
        ya = _sgu_fwd(p, *sgu_params, l)
        yb = _sconv_fwd(p, w["sc_conv_w"], l)
        qkv_c = _qkv_conv_fwd(p, w["dn_conv_w"], l)
        yc, st_c, inv_c, *got = _dn_fwd(
            qkv_c, p, dn_params, l,
            rider=_Gather([late["w_gate_up"], late["w_out"]], [False, True], forward_at=0.85) if riding else None)
        if riding:
            w.update(w_gate_up=got[0], w_out=got[1].reshape(DEPTH, D, D))
        yd, st_d, *got = _gla_fwd(p, gla_params, l,
                                  rider=_Gather([late["w_in"]], [False], forward_at=0.7) if riding else None)
        if riding:
            w_in[1] = _relayout_w_in(got[0])
        mix, x1, h2, gu, act, x2 = _out_mlp(x, (ya, yb, yc, yd), w["w_out"], w["norm2_w"], w["w_gate_up"], w["w_down"], l)
        saved.append(dict(x=x, h=h, p=p, qkv_c=qkv_c, st_c=st_c, inv_c=inv_c, st_d=st_d, mix=mix, x1=x1, h2=h2, gu=gu,
                          act=act))
        x = x2
    loss, d_final, dx = _loss_head(x, w["final_norm_w"], target)
    large = {k: [None] * DEPTH for k in ("w_in", "w_out", "w_gate_up", "w_down")}
    small = [None] * DEPTH
    for l in reversed(range(DEPTH)):
        s = saved[l]
        p = s["p"]
        g = {}
        riding = exchange and l == 0

        def exchanging(pairs):
            return _Exchange([large[k][j] for k, j in pairs]) if riding else None

        def arrive(pairs, arrived):
            for (k, j), a in zip(pairs, arrived):
                large[k][j] = a

        pairs = (("w_gate_up", 1), ("w_down", 1))
        dgu, dx1, dmix, g["norm2_w"], *arrived = _mlp_out_bwd(
            dx, s["x1"], s["gu"], w["norm2_w"], w["w_down"], w["w_gate_up"], w["w_out"], l, rider=exchanging(pairs))
        arrive(pairs, arrived)
        pairs = (("w_in", 1), ("w_out", 1))
        d_gu, *arrived = _wgrad_rows(dgu, s["h2"], "wgrad_gate_up", rider=exchanging(pairs)) if riding else (
            _wgrad_rows(dgu, s["h2"], "wgrad_gate_up"),)
        arrive(pairs, arrived)
        large["w_gate_up"][l] = d_gu.reshape(N_DEV, GU_SHARD, D)
        large["w_down"][l] = _wgrad_rows(s["act"], dx, "wgrad_down").reshape(N_DEV, FF // N_DEV, D)
        large["w_out"][l] = _wgrad_rows(s["mix"], dx1, "wgrad_out", rows=D).reshape(N_DEV, D // N_DEV, D)
        d_a, g["sgu_ln_w"], g["sgu_ln_b"], g["sgu_w_spatial"], g["sgu_b_spatial"] = _sgu_bwd(p, dmix, *sgu_params, l)
        d_b, g["sc_conv_w"] = _sconv_bwd(p, dmix, w["sc_conv_w"], l)
        pairs = tuple((k, 0) for k in _LATE)
        dqkv_c, dz_c, dab, g["dn_a_log"], g["dn_dt_bias"], g["dn_norm_w"], *arrived = _dn_bwd(
            s["qkv_c"], p, s["st_c"], s["inv_c"], dmix, dn_params, l, rider=exchanging(pairs))
        arrive(pairs, arrived)
        d_c, g["dn_conv_w"] = _qkv_conv_bwd(p, dqkv_c, w["dn_conv_w"], l)
        d_d, dz_d, dglr, g["gla_w_gate2"], g["gla_gate_bias"], g["gla_norm_w"] = _gla_bwd(p, s["st_d"], dmix, gla_params, l)
        dp, dx, g["norm1_w"] = _in_proj_bwd((d_c, d_d, d_a, d_b, dz_c, dz_d, dab, dglr), s["x"], dx1, w["norm1_w"],
                                            w_in[l], l)
        small[l] = g
        w_in_rows = dict(tt=WGRAD_TOKENS // 2, rows=P // 2, out_dtype=F32)
        if riding:
            d_w_in, small = _wgrad_rows(dp, s["h"], "wgrad_in", **w_in_rows,
                                        rider=_Gather([_pack_small(small, d_final)], [False], forward_at=0.75))
        else:
            d_w_in = _wgrad_rows(dp, s["h"], "wgrad_in", **w_in_rows)
        large["w_in"][l] = _scatter_w_in_grad(d_w_in)
    return loss[0, 0], dx, large, small, d_final


_ORDER = ("norm1_w", "w_in", "sgu_ln_w", "sgu_ln_b", "sgu_w_spatial", "sgu_b_spatial", "sc_conv_w", "dn_conv_w",
          "dn_a_log", "dn_dt_bias", "dn_norm_w", "gla_w_gate2", "gla_gate_bias", "gla_norm_w", "w_out", "norm2_w",
          "w_gate_up", "w_down", "final_norm_w")
_LARGE = ("w_in", "w_gate_up", "w_out", "w_down")
_LARGE_TILE = {"w_gate_up": 352, "w_out": 128, "w_down": 352}


def _gather_cols(g):
    return jnp.transpose(g, (1, 2, 0, 3)).reshape(g.shape[1], g.shape[2], N_DEV * g.shape[3])


def kernel(x, norm1_w, w_in, sgu_ln_w, sgu_ln_b, sgu_w_spatial, sgu_b_spatial, sc_conv_w, dn_conv_w, dn_a_log, dn_dt_bias, dn_norm_w, gla_w_gate2, gla_gate_bias, gla_norm_w, w_out, norm2_w, w_gate_up, w_down, final_norm_w, loss_target, m_norm1_w, m_w_in, m_sgu_ln_w, m_sgu_ln_b, m_sgu_w_spatial, m_sgu_b_spatial, m_sc_conv_w, m_dn_conv_w, m_dn_a_log, m_dn_dt_bias, m_dn_norm_w, m_gla_w_gate2, m_gla_gate_bias, m_gla_norm_w, m_w_out, m_norm2_w, m_w_gate_up, m_w_down, m_final_norm_w, v_norm1_w, v_w_in, v_sgu_ln_w, v_sgu_ln_b, v_sgu_w_spatial, v_sgu_b_spatial, v_sc_conv_w, v_dn_conv_w, v_dn_a_log, v_dn_dt_bias, v_dn_norm_w, v_gla_w_gate2, v_gla_gate_bias, v_gla_norm_w, v_w_out, v_norm2_w, v_w_gate_up, v_w_down, v_final_norm_w):
    args = dict(locals())
    wts = {k: args[k] for k in _ORDER}
    mom = {k: args["m_" + k] for k in _ORDER}
    var = {k: args["v_" + k] for k in _ORDER}
    for d in (wts, mom, var):
        d["final_norm_w"] = d["final_norm_w"][None]
    me = 4 * lax.axis_index("x") + 2 * lax.axis_index("y") + lax.axis_index("c")
    for d in (wts, mom, var):
        d["w_gate_up"] = jnp.swapaxes(d["w_gate_up"], 1, 2)

    late = {k: wts[k].astype(BF16) for k in _LATE}
    w_in = wts["w_in"].astype(BF16)
    late["w_in"] = w_in[1]
    got = _run(_Gather([w_in[0]] + [wts[k] for k in _SHARDED_SMALL], [False] * 4), "gather_w_in")
    full = dict(wts)
    full["w_in"] = [_relayout_w_in(got[0]), None]
    for k, g in zip(_SHARDED_SMALL, got[1:]):
        full[k] = _gather_cols(g)

    loss, dx, large, small, d_final = _local_grads(x[0], loss_target[0], full, late=late, exchange=True)
    loss = lax.psum(loss, ("x", "y", "c"))

    large["w_in"][0], = _run(_Exchange([large["w_in"][0]]), "exchange_grads")
    result = {}
    for k in ("w_gate_up", "w_out", "w_down"):
        outs = _adamw_large(large[k], wts[k], mom[k], var[k], "adamw_" + k, _LARGE_TILE[k])
        for kind, a in zip(("grad", "delta", "new_m", "new_v"), outs):
            result[kind, k] = jnp.swapaxes(a, 1, 2) if k == "w_gate_up" else a
    outs = _adamw_w_in(large["w_in"], *[jnp.transpose(d["w_in"], (2, 0, 1)) for d in (wts, mom, var)])
    for kind, a in zip(("grad", "delta", "new_m", "new_v"), outs):
        result[kind, "w_in"] = jnp.transpose(a, (1, 2, 0))

    slabs = small
    rep = _REPLICATED + ("final_norm_w",)
    outs, summed = _adamw_small(slabs, {k: wts[k] for k in rep}, {k: mom[k] for k in rep}, {k: var[k] for k in rep})
    for kind, d in zip(("grad", "delta", "new_m", "new_v"), outs):
        for k, a in d.items():
            result[kind, k] = a[0] if k == "final_norm_w" else a
    own = {k: lax.dynamic_slice_in_dim(summed[k], me * wts[k].shape[-1], wts[k].shape[-1], axis=2) for k in _SHARDED_SMALL}
    outs = _adamw_shards(own, {k: wts[k] for k in _SHARDED_SMALL}, {k: mom[k] for k in _SHARDED_SMALL},
                         {k: var[k] for k in _SHARDED_SMALL})
    for kind, d in zip(("grad", "delta", "new_m", "new_v"), [own] + outs):
        for k, a in d.items():
            result[kind, k] = a

    return (loss, dx[None], *[result[kind, k] for kind in ("grad", "delta", "new_m", "new_v") for k in _ORDER])
```

```python
import functools

import jax
import jax.numpy as jnp
from jax import lax
from jax.experimental import pallas as pl
from jax.experimental.pallas import tpu as pltpu

F32, BF16 = jnp.float32, jnp.bfloat16
DEPTH = 2
D = 1024
G = 256
NH, HD = 4, 64
FF = 2816
IN_COLS = 3352
P = 3584
EPS = 1e-6
SGU_C, DN_C, GLA_C = 128, 64, 64
N_DEV = 8
IN_SHARD = IN_COLS // N_DEV
GU_SHARD = 2 * FF // N_DEV
LANES = 128
VMEM_LIMIT = 56 * 2 ** 20

_PAD_SEGS = ((1280, 2048, 0),
             (2312, 3080, 0),
             (0, 512, 0),
             (512, 1280, 0),
             (2056, 2312, 0),
             (3096, 3352, 0),
             (2048, 2056, 120),
             (3080, 3096, 112))

ADAM_LR, ADAM_B1, ADAM_B2, ADAM_EPS, ADAM_WD, ADAM_STEP = 0.001, 0.9, 0.999, 1e-08, 0.01, 10


def _cparams(*sem):
    return pltpu.CompilerParams(dimension_semantics=sem, vmem_limit_bytes=VMEM_LIMIT)


def _resident(shape):
    return pl.BlockSpec(shape, lambda *_: (0,) * len(shape), pipeline_mode=pl.Buffered(1))


def _layer(shape, l):
    return pl.BlockSpec((None,) + tuple(shape), lambda *_: (l,) + (0,) * len(shape), pipeline_mode=pl.Buffered(1))


def _acc(shape):
    return pl.BlockSpec(shape, lambda *_: (0,) * len(shape))


def _dg(a, b, ca, cb):
    lead = a.ndim - 2
    batch = ((0,), (0,)) if lead else ((), ())
    return lax.dot_general(a, b, (((ca + lead,), (cb + lead,)), batch), preferred_element_type=F32)


def _split(t):
    hi = t.astype(BF16)
    return hi, (t - hi.astype(F32)).astype(BF16)


def _dg3(a, b, ca, cb):
    (ah, al), (bh, bl) = a, b
    return _dg(ah, bh, ca, cb) + (_dg(ah, bl, ca, cb) + _dg(al, bh, ca, cb))


def _make_dot(accurate):
    def raw(a, b, form):
        ca = 0 if form == "tn" else 1
        cb = 1 if form == "nt" else 0
        if accurate:
            return _dg3(_split(a), _split(b), ca, cb)
        return _dg(a.astype(BF16), b.astype(BF16), ca, cb)

    @functools.partial(jax.custom_vjp, nondiff_argnums=(2,))
    def dot(a, b, form):
        return raw(a, b, form)

    def fwd(a, b, form):
        return raw(a, b, form), (a, b)

    def bwd(form, res, g):
        a, b = res
        if form == "nn":
            return raw(g, b, "nt"), raw(a, g, "tn")
        if form == "nt":
            return raw(g, b, "nn"), raw(g, a, "tn")
        return raw(b, g, "nt"), raw(a, g, "nn")

    dot.defvjp(fwd, bwd)
    return dot


_bdot = _make_dot(False)
_hdot = _make_dot(True)


def _inv_unit_lower(low):
    n = low.shape[-1]
    eye = (lax.broadcasted_iota(jnp.int32, (n, n), 0) == lax.broadcasted_iota(jnp.int32, (n, n), 1)).astype(F32)
    p = -low
    inv = eye + p
    ps = _split(p)
    k = 1
    while 2 * k < n:
        ps = _split(_dg3(ps, ps, 1, 0))
        inv = inv + _dg3(_split(inv), ps, 1, 0)
        k *= 2
    return inv


@jax.custom_vjp
def _unit_lower_solve(low, rhs, inv):
    return _dg3(_split(inv), _split(rhs), 1, 0)


def _uls_fwd(low, rhs, inv):
    sol = _dg3(_split(inv), _split(rhs), 1, 0)
    return sol, (inv, sol)


def _uls_bwd(res, g):
    inv, sol = res
    d_rhs = _dg3(_split(inv), _split(g), 0, 0)
    return -_dg3(_split(d_rhs), _split(sol), 1, 1), d_rhs, jnp.zeros_like(inv)


_unit_lower_solve.defvjp(_uls_fwd, _uls_bwd)


def _sigmoid(x):
    return 0.5 * jnp.tanh(0.5 * x) + 0.5


def _softplus(x):
    return jnp.maximum(x, 0.0) + jnp.log(1.0 + jnp.exp(-jnp.maximum(x, -x)))


def _gelu(x):
    return 0.5 * x * (1.0 + jnp.tanh(0.7978845608028654 * (x + 0.044715 * (x * x * x))))


def _tri(n):
    ri = lax.broadcasted_iota(jnp.int32, (n, n), 0)
    ci = lax.broadcasted_iota(jnp.int32, (n, n), 1)
    return ri, ci


def _stack(ts):
    return jnp.concatenate([t[None] for t in ts], axis=0)


@jax.custom_vjp
def _head_sums(x):
    ri, ci = _tri(x.shape[-1])
    shift = HD.bit_length() - 1
    ones = (jnp.right_shift(ri, shift) == jnp.right_shift(ci, shift)).astype(BF16)
    hi, lo = _split(x)
    return _dg(hi, ones, 1, 0) + _dg(lo, ones, 1, 0)


_head_sums.defvjp(lambda x: (_head_sums(x), None), lambda _, g: (_head_sums(g),))


def _chunk_mask_dot(x, c, running, transpose):
    ri, ci = _tri(x.shape[0])
    shift = c.bit_length() - 1
    mask = jnp.right_shift(ri, shift) == jnp.right_shift(ci, shift)
    if running:
        mask = mask & (ri >= ci)
    hi, lo = _split(x)
    m = mask.astype(BF16)
    ca = 0 if transpose else 1
    return _dg(m, hi, ca, 0) + _dg(m, lo, ca, 0)


@functools.partial(jax.custom_vjp, nondiff_argnums=(1, 2))
def _chunk_sums(x, c, running):
    return _chunk_mask_dot(x, c, running, False)


_chunk_sums.defvjp(lambda x, c, running: (_chunk_mask_dot(x, c, running, False), None),
                   lambda c, running, _, g: (_chunk_mask_dot(g, c, running, True),))


def _spread_onto(x, first, transpose):
    ri = lax.broadcasted_iota(jnp.int32, (LANES, G), 0)
    ci = lax.broadcasted_iota(jnp.int32, (LANES, G), 1)
    sel = (ri == first + jnp.right_shift(ci, HD.bit_length() - 1)).astype(BF16)
    hi, lo = _split(x)
    cb = 1 if transpose else 0
    return _dg(hi, sel, 1, cb) + _dg(lo, sel, 1, cb)


@functools.partial(jax.custom_vjp, nondiff_argnums=(1,))
def _spread(x, first):
    return _spread_onto(x, first, False)


_spread.defvjp(lambda x, first: (_spread_onto(x, first, False), None),
               lambda first, _, g: (_spread_onto(g, first, True),))


def _head_columns_dot(x, transpose):
    sel = (lax.broadcasted_iota(jnp.int32, (NH, G), 0)
           == jnp.right_shift(lax.broadcasted_iota(jnp.int32, (NH, G), 1), HD.bit_length() - 1)).astype(BF16)
    hi, lo = _split(x)
    if transpose:
        return _dg(sel, hi, 1, 1) + _dg(sel, lo, 1, 1)
    return _dg(hi, sel, 0, 0) + _dg(lo, sel, 0, 0)


@jax.custom_vjp
def _head_columns(x):
    return _head_columns_dot(x, False)


_head_columns.defvjp(lambda x: (_head_columns_dot(x, False), None), lambda _, g: (_head_columns_dot(g, True),))


def _sgu_chunk(uv, ln_w, ln_b, ws, bs):
    u = _gelu(uv[:, :G])
    v = _gelu(uv[:, G:])
    mu = jnp.mean(v, axis=-1, keepdims=True)
    vc = v - mu
    var = jnp.mean(vc * vc, axis=-1, keepdims=True)
    vn = vc * lax.rsqrt(var + EPS) * ln_w + ln_b
    ri, ci = _tri(SGU_C)
    mixed = [_bdot(jnp.where(ri >= ci, ws[h], 0.0), vn[:, HD * h:HD * (h + 1)], "nn") for h in range(NH)]
    return u * (jnp.concatenate(mixed, axis=1) + _head_columns(bs))


def _dn_tile(q, k, v, ab, z, state, inv, a_log, dt_bias, nw):
    c = DN_C
    tm = q.shape[0]
    cps = tm // c
    ri, ci = _tri(tm)
    shift = c.bit_length() - 1
    same = jnp.right_shift(ri, shift) == jnp.right_shift(ci, shift)
    g4 = -jnp.exp(a_log) * _softplus(ab[:, 0:NH] + dt_bias)
    gates = jnp.concatenate([g4, _sigmoid(ab[:, NH:2 * NH]), jnp.zeros((tm, LANES - 2 * NH), F32)], axis=1)
    g, beta = _spread(gates, 0), _spread(gates, NH)
    gc = _chunk_sums(g, c, True)
    gl = _chunk_sums(g, c, False)
    gr4 = _hdot(g4, (same & (ri <= ci)).astype(F32), "tn")
    pairs = [(slice(c * j, c * (j + 1)), h) for j in range(cps) for h in range(NH)]
    heads = lambda t: _stack([t[rows, HD * h:HD * (h + 1)] for rows, h in pairs])
    qn = q * lax.rsqrt(_head_sums(q * q) + EPS) * (HD ** -0.5)
    kn = k * lax.rsqrt(_head_sums(k * k) + EPS)
    eg = jnp.exp(gc)
    kb = kn * beta
    rhs = jnp.concatenate([heads(v * beta), heads(kb * eg)], axis=2)
    qg, kd = heads(qn * eg), heads(kn * jnp.exp(gl - gc))
    cd = _stack([jnp.exp(gl[c * j:c * j + 1, HD * h:HD * (h + 1)]) for j in range(cps) for h in range(NH)])
    qn, kn, kb, gc = heads(qn), heads(kn), heads(kb), heads(gc)
    gr = _stack([gr4[h:h + 1, rows] for rows, h in pairs])
    r2, c2 = _tri(c)
    decay = jnp.exp(jnp.where(r2 >= c2, gc - gr, -jnp.inf))
    low = jnp.where(r2 > c2, _bdot(kb, kn, "nt") * decay, 0.0)
    if inv is None:
        inv = _inv_unit_lower(low)
    sol = _unit_lower_solve(low, rhs, inv)
    u, w = sol[:, :, :HD], sol[:, :, HD:]
    attn = _bdot(qn, kn, "nt") * decay
    ys = []
    for j in range(cps):
        b = slice(NH * j, NH * (j + 1))
        v_new = u[b] - _bdot(w[b], state, "nn")
        o = _bdot(qg[b], state, "nn") + _bdot(attn[b], v_new, "nn")
        state = state * cd[b] + _bdot(kd[b], v_new, "tn")
        on = o * lax.rsqrt(jnp.mean(o * o, axis=-1, keepdims=True) + EPS) * nw
        ys.append(jnp.concatenate([on[h] for h in range(NH)], axis=1))
    return jnp.concatenate(ys, axis=0) * (z * _sigmoid(z)), state, inv


def _gla_tile(q, k, v, glr, z, state_t, w2, gate_bias, nw):
    c = GLA_C
    tm = q.shape[0]
    cps = tm // c
    w2_rows = jnp.concatenate([w2, jnp.zeros((LANES - w2.shape[0], G), F32)], axis=0)
    pre = _bdot(glr, w2_rows, "nn") + gate_bias
    log_a = (jnp.minimum(pre, 0.0) - jnp.log(1.0 + jnp.exp(-jnp.maximum(pre, -pre)))) * (1.0 / 16.0)
    gcum = _chunk_sums(log_a, c, True)
    row = lax.broadcasted_iota(jnp.int32, (c, G), 0)
    qs = q * (HD ** -0.5)
    qa, ka, qg, kl, dec = [], [], [], [], []
    for j in range(cps):
        rows = slice(c * j, c * (j + 1))
        gj = gcum[rows]
        g_mid = jnp.sum(jnp.where(row == c // 2, gj, 0.0), axis=0, keepdims=True)
        g_last = jnp.sum(log_a[rows], axis=0, keepdims=True)
        qa.append(qs[rows] * jnp.exp(gj - g_mid))
        ka.append(k[rows] * jnp.exp(g_mid - gj))
        qg.append(qs[rows] * jnp.exp(gj))
        kl.append(k[rows] * jnp.exp(g_last - gj))
        dec.append(jnp.exp(g_last))
    heads = lambda ts: _stack([t[:, HD * h:HD * (h + 1)] for t in ts for h in range(NH)])
    qa, ka, qg, kl, dec = heads(qa), heads(ka), heads(qg), heads(kl), heads(dec)
    vh = heads([v[c * j:c * (j + 1)] for j in range(cps)])
    r2, c2 = _tri(c)
    o_intra = _bdot(jnp.where(r2 >= c2, _bdot(qa, ka, "nt"), 0.0), vh, "nn")
    ys = []
    for j in range(cps):
        b = slice(NH * j, NH * (j + 1))
        o = o_intra[b] + _bdot(qg[b], state_t, "nt")
        state_t = state_t * dec[b] + _bdot(vh[b], kl[b], "tn")
        on = o * lax.rsqrt(jnp.mean(o * o, axis=-1, keepdims=True) + EPS) * nw
        ys.append(jnp.concatenate([on[h] for h in range(NH)], axis=1))
    return jnp.concatenate(ys, axis=0) * (z * _sigmoid(z)), state_t


def _rms(x, nw):
    r = lax.rsqrt(jnp.mean(x * x, axis=-1, keepdims=True) + EPS)
    xh = x * r
    return xh * nw, xh, r


def _rms_bwd(g, xh, r, nw):
    gw = g * nw
    return r * (gw - xh * jnp.mean(gw * xh, axis=-1, keepdims=True)), jnp.sum(g * xh, axis=0, keepdims=True)


def _row(tm, w, blk=0):
    return pl.BlockSpec((tm, w), lambda i: (i, blk))


def _in_proj(x, nw, wp, l, tm=512, rider=None):
    t = x.shape[0]

    def body(*refs):
        (x_ref, nw_ref, w_ref, h_ref, p_ref), ride = _split_refs(refs, 3, 2, 0, rider)
        _ride_begin(rider, ride, (t // tm,))
        h = _rms(x_ref[...], nw_ref[l:l + 1, :])[0].astype(BF16)
        h_ref[...] = h
        p_ref[...] = jnp.dot(h, w_ref[...], preferred_element_type=F32)
        _ride_end(rider, ride, (t // tm,))

    specs, operands = _host(
        rider, [_row(tm, D), _resident((DEPTH, D)), _resident((D, P))], [_row(tm, D), _row(tm, P)],
        [jax.ShapeDtypeStruct((t, D), BF16), jax.ShapeDtypeStruct((t, P), F32)], [], [x, nw, wp])
    return pl.pallas_call(body, name="in_proj", grid=(t // tm,), compiler_params=_cparams("arbitrary"),
                          **specs)(*operands)


def _gu_spec(l):
    return pl.BlockSpec((N_DEV, None, GU_SHARD, D), lambda *_: (0, l, 0, 0), pipeline_mode=pl.Buffered(1))


def _out_mlp(x, ys, w_out, nw2, w_gu_t, w_down, l, tm=256):
    t = x.shape[0]

    def body(x_ref, ya, yb, yc, yd, wo_ref, nw_ref, wgu_ref, wd_ref, mix_ref, x1_ref, h2_ref, gu_ref, act_ref, x2_ref):
        mix = jnp.concatenate([ya[...], yb[...], yc[...], yd[...]], axis=1)
        mix_ref[...] = mix
        x1 = x_ref[...] + jnp.dot(mix, wo_ref[...], preferred_element_type=F32)
        x1_ref[...] = x1
        h2 = _rms(x1, nw_ref[l:l + 1, :])[0].astype(BF16)
        h2_ref[...] = h2
        gu = _dg(h2, wgu_ref[...].reshape(2 * FF, D), 1, 1)
        gu_ref[...] = gu.astype(BF16)
        gate, up = gu[:, :FF], gu[:, FF:]
        act = (gate * _sigmoid(gate) * up).astype(BF16)
        act_ref[...] = act
        x2_ref[...] = x1 + jnp.dot(act, wd_ref[...], preferred_element_type=F32)

    return pl.pallas_call(
        body, name="out_mlp", grid=(t // tm,),
        in_specs=[_row(tm, D), _row(tm, G), _row(tm, G), _row(tm, G), _row(tm, G), _layer((D, D), l),
                  _resident((DEPTH, D)), _gu_spec(l), _layer((FF, D), l)],
        out_specs=[_row(tm, D), _row(tm, D), _row(tm, D), _row(tm, 2 * FF), _row(tm, FF), _row(tm, D)],
        out_shape=[jax.ShapeDtypeStruct((t, D), BF16), jax.ShapeDtypeStruct((t, D), F32),
                   jax.ShapeDtypeStruct((t, D), BF16), jax.ShapeDtypeStruct((t, 2 * FF), BF16),
                   jax.ShapeDtypeStruct((t, FF), BF16), jax.ShapeDtypeStruct((t, D), F32)],
        compiler_params=_cparams("parallel"))(x, *ys, w_out, nw2, w_gu_t, w_down)


def _loss_head(x, nw, target, tm=1024):
    t = x.shape[0]
    tm = min(tm, t)

    def body(x_ref, nw_ref, tg_ref, loss_ref, dnw_ref, dx_ref):
        @pl.when(pl.program_id(0) == 0)
        def _():
            loss_ref[...] = jnp.zeros_like(loss_ref)
            dnw_ref[...] = jnp.zeros_like(dnw_ref)
        nw_v = nw_ref[...]
        y, xh, r = _rms(x_ref[...], nw_v)
        err = y - tg_ref[...]
        loss_ref[...] += 0.5 * jnp.sum(err * err) * (1.0 / D)
        dx, dnw = _rms_bwd(err * (1.0 / D), xh, r, nw_v)
        dx_ref[...] = dx
        dnw_ref[...] += dnw

    return pl.pallas_call(
        body, name="loss_head", grid=(t // tm,),
        in_specs=[_row(tm, D), _resident((1, D)), _row(tm, D)],
        out_specs=[_acc((8, LANES)), _acc((1, D)), _row(tm, D)],
        out_shape=[jax.ShapeDtypeStruct((8, LANES), F32), jax.ShapeDtypeStruct((1, D), F32),
                   jax.ShapeDtypeStruct((t, D), F32)],
        compiler_params=_cparams("arbitrary"))(x, nw, target)


def _mlp_out_bwd(dx2, x1, gu, nw2, w_down, w_gu, w_out, l, tm=256, rider=None):
    t = dx2.shape[0]

    def body(*refs):
        (dx2_ref, x1_ref, gu_ref, nw_ref, wd_ref, wgu_ref, wo_ref, dgu_ref, dx1_ref, dmix_ref, dnw_ref), ride = \
            _split_refs(refs, 7, 4, 0, rider)
        _ride_begin(rider, ride, (t // tm,))

        @pl.when(pl.program_id(0) == 0)
        def _():
            dnw_ref[...] = jnp.zeros_like(dnw_ref)
        dx2 = dx2_ref[...]
        dact = _dg(dx2.astype(BF16), wd_ref[...], 1, 1)
        gu = gu_ref[...].astype(F32)
        gate, up = gu[:, :FF], gu[:, FF:]
        s = _sigmoid(gate)
        dgu = jnp.concatenate([dact * up * (s * (1.0 + gate * (1.0 - s))), dact * (gate * s)], axis=1).astype(BF16)
        dgu_ref[...] = dgu
        dh2 = jnp.dot(dgu, wgu_ref[...].reshape(2 * FF, D), preferred_element_type=F32)
        nw_v = nw_ref[l:l + 1, :]
        _, xh, r = _rms(x1_ref[...], nw_v)
        dxn, dnw = _rms_bwd(dh2, xh, r, nw_v)
        dx1 = dx2 + dxn
        dx1_ref[...] = dx1
        dnw_ref[...] += dnw
        dmix_ref[...] = _dg(dx1.astype(BF16), wo_ref[...], 1, 1)
        _ride_end(rider, ride, (t // tm,))

    specs, operands = _host(
        rider, [_row(tm, D), _row(tm, D), _row(tm, 2 * FF), _resident((DEPTH, D)), _layer((FF, D), l), _gu_spec(l),
                _layer((D, D), l)],
        [_row(tm, 2 * FF), _row(tm, D), _row(tm, D), _acc((1, D))],
        [jax.ShapeDtypeStruct((t, 2 * FF), BF16), jax.ShapeDtypeStruct((t, D), F32),
         jax.ShapeDtypeStruct((t, D), F32), jax.ShapeDtypeStruct((1, D), F32)],
        [], [dx2, x1, gu, nw2, w_down, w_gu, w_out])
    return pl.pallas_call(body, name="mlp_out_bwd", grid=(t // tm,), compiler_params=_cparams("arbitrary"),
                          **specs)(*operands)


_DP_WIDTHS = (768, 768, 512, 768, 256, 256, 128, 128)


def _in_proj_bwd(dps, x, dx1, nw, wp, l, tm=512):
    t = x.shape[0]

    def body(*refs):
        dp_refs, (x_ref, dx1_ref, nw_ref, w_ref, dp_ref, dx_ref, dnw_ref) = refs[:8], refs[8:]

        @pl.when(pl.program_id(0) == 0)
        def _():
            dnw_ref[...] = jnp.zeros_like(dnw_ref)
        dp = jnp.concatenate([r[...] for r in dp_refs], axis=1)
        dp_ref[...] = dp
        dh = _dg(dp, w_ref[...], 1, 1)
        nw_v = nw_ref[l:l + 1, :]
        _, xh, r = _rms(x_ref[...], nw_v)
        dxn, dnw = _rms_bwd(dh, xh, r, nw_v)
        dx_ref[...] = dx1_ref[...] + dxn
        dnw_ref[...] += dnw

    return pl.pallas_call(
        body, name="in_proj_bwd", grid=(t // tm,),
        in_specs=[_row(tm, w) for w in _DP_WIDTHS] + [_row(tm, D), _row(tm, D), _resident((DEPTH, D)), _resident((D, P))],
        out_specs=[_row(tm, P), _row(tm, D), _acc((1, D))],
        out_shape=[jax.ShapeDtypeStruct((t, P), BF16), jax.ShapeDtypeStruct((t, D), F32),
                   jax.ShapeDtypeStruct((1, D), F32)],
        compiler_params=_cparams("arbitrary"))(*dps, x, dx1, nw, wp)


def _accumulate(acc, o_ref, t_axis, term):
    @pl.when(pl.program_id(t_axis) == 0)
    def _():
        acc[...] = jnp.zeros_like(acc)
    acc[...] += term

    @pl.when(pl.program_id(t_axis) == pl.num_programs(t_axis) - 1)
    def _():
        o_ref[...] = acc[...].astype(o_ref.dtype)


WGRAD_TOKENS = 2048


WGRAD_ROWS = 2 * GU_SHARD


def _wgrad_rows(a, b, name, tt=WGRAD_TOKENS, rows=WGRAD_ROWS, out_dtype=BF16, rider=None):
    t, m = a.shape
    tt = min(tt, t)
    grid = (m // rows, t // tt)

    def body(*refs):
        (a_ref, b_ref, o_ref, acc), ride = _split_refs(refs, 2, 1, 1, rider)
        _ride_begin(rider, ride, grid)
        _accumulate(acc, o_ref, 1, _dg(a_ref[...], b_ref[...].astype(BF16), 0, 0))
        _ride_end(rider, ride, grid)

    specs, operands = _host(
        rider, [pl.BlockSpec((tt, rows), lambda j, i: (i, j)), pl.BlockSpec((tt, D), lambda j, i: (i, 0))],
        [pl.BlockSpec((rows, D), lambda j, i: (j, 0))], [jax.ShapeDtypeStruct((m, D), out_dtype)],
        [pltpu.VMEM((rows, D), F32)], [a, b])
    out = pl.pallas_call(body, name=name, grid=grid, compiler_params=_cparams("arbitrary", "arbitrary"), **specs)(*operands)
    return out[0] if rider is None else out


def _relayout_w_in(g, tr=512):
    def body(w_ref, o_ref):
        full = jnp.concatenate([w_ref[d] for d in range(N_DEV)], axis=1)
        parts = []
        for a, b, z in _PAD_SEGS:
            parts.append(full[:, a:b])
            if z:
                parts.append(jnp.zeros((tr, z), full.dtype))
        o_ref[...] = jnp.concatenate(parts, axis=1)

    return pl.pallas_call(
        body, name="relayout_w_in", grid=(D // tr,),
        in_specs=[pl.BlockSpec((N_DEV, tr, IN_SHARD), lambda i: (0, i, 0))], out_specs=_row(tr, P),
        out_shape=jax.ShapeDtypeStruct((D, P), g.dtype), compiler_params=_cparams("parallel"))(g)


IN_ROWS = 432


def _scatter_w_in_grad(gp_t, tc=512):
    def body(g_ref, o_ref):
        g = g_ref[...]
        pieces, off = [], 0
        for a, b, z in _PAD_SEGS:
            pieces.append((a, g[off:off + (b - a)]))
            off += (b - a) + z
        spill = -(-(IN_SHARD * (N_DEV - 1) + IN_ROWS - IN_COLS) // 8) * 8
        nat = jnp.concatenate([piece for _, piece in sorted(pieces, key=lambda ap: ap[0])]
                              + [jnp.zeros((spill, tc), F32)], axis=0)
        for d in range(N_DEV):
            o_ref[d] = nat[IN_SHARD * d:IN_SHARD * d + IN_ROWS].astype(BF16)

    return pl.pallas_call(
        body, name="scatter_w_in_grad", grid=(D // tc,),
        in_specs=[pl.BlockSpec((P, tc), lambda i: (0, i))],
        out_specs=pl.BlockSpec((N_DEV, IN_ROWS, tc), lambda i: (0, 0, i)),
        out_shape=jax.ShapeDtypeStruct((N_DEV, IN_ROWS, D), BF16),
        compiler_params=_cparams("parallel"))(gp_t)


_A_BLK = 3


def _sgu_specs(l):
    return [_resident((DEPTH, G)), _resident((DEPTH, G)), _layer((NH, SGU_C, SGU_C), l), _layer((NH, SGU_C), l)]


def _sgu_fwd(p, ln_w, ln_b, ws, bs, l, tm=1024):
    t = p.shape[0]
    tm = min(tm, t)

    def body(uv_ref, lw_ref, lb_ref, ws_ref, bs_ref, y_ref):
        ws_v = [ws_ref[h] for h in range(NH)]
        for c in range(tm // SGU_C):
            rows = pl.ds(c * SGU_C, SGU_C)
            y_ref[rows, :] = _sgu_chunk(uv_ref[rows, :], lw_ref[l:l + 1, :], lb_ref[l:l + 1, :], ws_v,
                                        bs_ref[...]).astype(BF16)

    return pl.pallas_call(
        body, name="sgu_fwd", grid=(t // tm,),
        in_specs=[_row(tm, 2 * G, _A_BLK)] + _sgu_specs(l), out_specs=_row(tm, G),
        out_shape=jax.ShapeDtypeStruct((t, G), BF16),
        compiler_params=_cparams("parallel"))(p, ln_w, ln_b, ws, bs)


def _sgu_bwd(p, dmix, ln_w, ln_b, ws, bs, l, tm=1024):
    t = p.shape[0]
    tm = min(tm, t)

    def body(uv_ref, dy_ref, lw_ref, lb_ref, ws_ref, bs_ref, duv_ref, dlw_ref, dlb_ref, dws_ref, dbs_ref):
        @pl.when(pl.program_id(0) == 0)
        def _():
            for r in (dlw_ref, dlb_ref, dws_ref, dbs_ref):
                r[...] = jnp.zeros_like(r)
        ws_v = [ws_ref[h] for h in range(NH)]
        for c in range(tm // SGU_C):
            rows = pl.ds(c * SGU_C, SGU_C)
            _, vjp = jax.vjp(_sgu_chunk, uv_ref[rows, :], lw_ref[l:l + 1, :], lb_ref[l:l + 1, :], ws_v, bs_ref[...])
            duv, dlw, dlb, dws, dbs = vjp(dy_ref[rows, :])
            duv_ref[rows, :] = duv.astype(BF16)
            dlw_ref[...] += dlw
            dlb_ref[...] += dlb
            dbs_ref[...] += dbs
            for h in range(NH):
                dws_ref[h] += dws[h]

    return pl.pallas_call(
        body, name="sgu_bwd", grid=(t // tm,),
        in_specs=[_row(tm, 2 * G, _A_BLK), _row(tm, G, 0)] + _sgu_specs(l),
        out_specs=[_row(tm, 2 * G), _acc((1, G)), _acc((1, G)), _acc((NH, SGU_C, SGU_C)), _acc((NH, SGU_C))],
        out_shape=[jax.ShapeDtypeStruct((t, 2 * G), BF16), jax.ShapeDtypeStruct((1, G), F32),
                   jax.ShapeDtypeStruct((1, G), F32), jax.ShapeDtypeStruct((NH, SGU_C, SGU_C), F32),
                   jax.ShapeDtypeStruct((NH, SGU_C), F32)],
        compiler_params=_cparams("arbitrary"))(p, dmix, ln_w, ln_b, ws, bs)


HALO = 8


def _prev_halo(tm, width, col):
    return pl.BlockSpec((HALO, width), lambda i: (jnp.maximum(i * (tm // HALO) - 1, 0), col))


def _next_halo(tm, width, col, t):
    return pl.BlockSpec((HALO, width), lambda i: (jnp.minimum((i + 1) * (tm // HALO), t // HALO - 1), col))


def _with_prev(halo_ref, x_ref):
    halo = jnp.where(pl.program_id(0) == 0, 0.0, halo_ref[...])
    return jnp.concatenate([halo, x_ref[...]], axis=0)


def _with_next(x_ref, halo_ref):
    halo = jnp.where(pl.program_id(0) == pl.num_programs(0) - 1, 0.0, halo_ref[...])
    return jnp.concatenate([x_ref[...], halo], axis=0)


def _delay(ext, j):
    return ext if j == 0 else pltpu.roll(ext, j, axis=0)


def _advance(ext, j):
    return ext if j == 0 else pltpu.roll(ext, ext.shape[0] - j, axis=0)


def _causal_conv(ext, w, tm):
    kw = w.shape[0]
    out = None
    for k in range(kw):
        term = _delay(ext, kw - 1 - k)[HALO:HALO + tm] * w[k:k + 1, :]
        out = term if out is None else out + term
    return out


def _causal_conv_t(ext, w, tm):
    kw = w.shape[0]
    out = None
    for k in range(kw):
        term = _advance(ext, kw - 1 - k)[0:tm] * w[k:k + 1, :]
        out = term if out is None else out + term
    return out


def _conv_wgrad(ext, dy, kw, tm):
    return jnp.concatenate(
        [jnp.sum(_delay(ext, kw - 1 - k)[HALO:HALO + tm] * dy, axis=0, keepdims=True) for k in range(kw)], axis=0)


_B_GB, _B_GC, _B_H = 8, 9, 10
_C_QKV, _D_QKV = 0, 1
_C_Z, _D_Z = 11, 12
_C_AB, _D_GLR = 26, 27


def _sconv_fwd(p, w, l, tm=2048):
    t = p.shape[0]
    tm = min(tm, t)

    def body(gb_ref, gc_ref, h_ref, gc_halo, h_halo, w_ref, y_ref):
        s_ext = _with_prev(gc_halo, gc_ref) * _with_prev(h_halo, h_ref)
        y_ref[...] = (gb_ref[...] * _causal_conv(s_ext, w_ref[...], tm)).astype(BF16)

    return pl.pallas_call(
        body, name="sconv_fwd", grid=(t // tm,),
        in_specs=[_row(tm, G, _B_GB), _row(tm, G, _B_GC), _row(tm, G, _B_H), _prev_halo(tm, G, _B_GC),
                  _prev_halo(tm, G, _B_H), _layer((3, G), l)],
        out_specs=_row(tm, G), out_shape=jax.ShapeDtypeStruct((t, G), BF16),
        compiler_params=_cparams("parallel"))(p, p, p, p, p, w)


def _sconv_bwd(p, dmix, w, l, tm=2048):
    t = p.shape[0]
    tm = min(tm, t)

    def body(gb_ref, gc_ref, h_ref, gc_halo, h_halo, gb_next, dy_ref, dy_next, w_ref, dp_ref, dw_ref):
        @pl.when(pl.program_id(0) == 0)
        def _():
            dw_ref[...] = jnp.zeros_like(dw_ref)
        w_v = w_ref[...]
        s_ext = _with_prev(gc_halo, gc_ref) * _with_prev(h_halo, h_ref)
        dout = dy_ref[...]
        d_gb = dout * _causal_conv(s_ext, w_v, tm)
        dconv_ext = _with_next(dy_ref, dy_next) * _with_next(gb_ref, gb_next)
        ds = _causal_conv_t(dconv_ext, w_v, tm)
        dw_ref[...] += _conv_wgrad(s_ext, dconv_ext[0:tm], 3, tm)
        dp_ref[...] = jnp.concatenate([d_gb, ds * h_ref[...], ds * gc_ref[...]], axis=1).astype(BF16)

    return pl.pallas_call(
        body, name="sconv_bwd", grid=(t // tm,),
        in_specs=[_row(tm, G, _B_GB), _row(tm, G, _B_GC), _row(tm, G, _B_H), _prev_halo(tm, G, _B_GC),
                  _prev_halo(tm, G, _B_H), _next_halo(tm, G, _B_GB, t), _row(tm, G, 1), _next_halo(tm, G, 1, t),
                  _layer((3, G), l)],
        out_specs=[_row(tm, 3 * G), _acc((3, G))],
        out_shape=[jax.ShapeDtypeStruct((t, 3 * G), BF16), jax.ShapeDtypeStruct((3, G), F32)],
        compiler_params=_cparams("arbitrary"))(p, p, p, p, p, p, dmix, dmix, w)


def _qkv_conv_fwd(p, w, l, tm=1024):
    t = p.shape[0]
    tm = min(tm, t)

    def body(x_ref, x_halo, w_ref, y_ref):
        c = _causal_conv(_with_prev(x_halo, x_ref), w_ref[...], tm)
        y_ref[...] = c * _sigmoid(c)

    return pl.pallas_call(
        body, name="qkv_conv_fwd", grid=(t // tm,),
        in_specs=[_row(tm, 3 * G, _C_QKV), _prev_halo(tm, 3 * G, _C_QKV), _layer((4, 3 * G), l)],
        out_specs=_row(tm, 3 * G), out_shape=jax.ShapeDtypeStruct((t, 3 * G), F32),
        compiler_params=_cparams("parallel"))(p, p, w)


def _qkv_conv_bwd(p, dy, w, l, tm=1024):
    t = p.shape[0]
    tm = min(tm, t)

    def body(x_ref, x_halo, x_next, dy_ref, dy_next, w_ref, dx_ref, dw_ref):
        @pl.when(pl.program_id(0) == 0)
        def _():
            dw_ref[...] = jnp.zeros_like(dw_ref)
        w_v = w_ref[...]
        x_all = jnp.concatenate([_with_prev(x_halo, x_ref), jnp.where(
            pl.program_id(0) == pl.num_programs(0) - 1, 0.0, x_next[...])], axis=0)
        c = _causal_conv(x_all, w_v, tm + HALO)
        s = _sigmoid(c)
        dc_ext = _with_next(dy_ref, dy_next) * (s * (1.0 + c * (1.0 - s)))
        dx_ref[...] = _causal_conv_t(dc_ext, w_v, tm).astype(BF16)
        dw_ref[...] += _conv_wgrad(x_all[0:HALO + tm], dc_ext[0:tm], 4, tm)

    return pl.pallas_call(
        body, name="qkv_conv_bwd", grid=(t // tm,),
        in_specs=[_row(tm, 3 * G, _C_QKV), _prev_halo(tm, 3 * G, _C_QKV), _next_halo(tm, 3 * G, _C_QKV, t),
                  _row(tm, 3 * G), _next_halo(tm, 3 * G, 0, t), _layer((4, 3 * G), l)],
        out_specs=[_row(tm, 3 * G), _acc((4, 3 * G))],
        out_shape=[jax.ShapeDtypeStruct((t, 3 * G), BF16), jax.ShapeDtypeStruct((4, 3 * G), F32)],
        compiler_params=_cparams("arbitrary"))(p, p, p, dy, dy, w)


_STATE = pl.BlockSpec((1, NH, HD, HD), lambda i: (i, 0, 0, 0))


def _dn_specs():
    return [_resident((DEPTH, NH)), _resident((DEPTH, NH)), _resident((DEPTH, HD))]


def _dn_fwd(qkv, p, params, l, cps=8, rider=None):
    t = qkv.shape[0]
    tm = DN_C * cps
    nb = cps * NH

    def body(*refs):
        (qkv_ref, z_ref, ab_ref, alog_ref, dt_ref, nw_ref, y_ref, st_ref, inv_ref, state), ride = _split_refs(
            refs, 6, 3, 1, rider)
        _ride_begin(rider, ride, (t // tm,))

        @pl.when(pl.program_id(0) == 0)
        def _():
            state[...] = jnp.zeros_like(state)
        st_ref[0] = state[...]
        y, new, inv = _dn_tile(qkv_ref[:, 0:G], qkv_ref[:, G:2 * G], qkv_ref[:, 2 * G:3 * G], ab_ref[...], z_ref[...],
                               state[...], None, alog_ref[l:l + 1, :], dt_ref[l:l + 1, :], nw_ref[l:l + 1, :])
        y_ref[...] = y.astype(BF16)
        inv_ref[...] = inv
        state[...] = new
        _ride_end(rider, ride, (t // tm,))

    specs, operands = _host(
        rider, [_row(tm, 3 * G), _row(tm, G, _C_Z), _row(tm, LANES, _C_AB)] + _dn_specs(),
        [_row(tm, G), _STATE, pl.BlockSpec((nb, DN_C, DN_C), lambda i: (i, 0, 0))],
        [jax.ShapeDtypeStruct((t, G), BF16), jax.ShapeDtypeStruct((t // tm, NH, HD, HD), F32),
         jax.ShapeDtypeStruct((t // DN_C * NH, DN_C, DN_C), F32)],
        [pltpu.VMEM((NH, HD, HD), F32)], [qkv, p, p, *params])
    return pl.pallas_call(body, name="dn_fwd", grid=(t // tm,), compiler_params=_cparams("arbitrary"), **specs)(*operands)


def _dn_bwd(qkv, p, states, inv, dmix, params, l, cps=8, rider=None):
    t = qkv.shape[0]
    tm = DN_C * cps
    n = t // tm
    nb = cps * NH

    def body(*refs):
        (qkv_ref, z_ref, ab_ref, st_ref, inv_ref, dy_ref, alog_ref, dt_ref, nw_ref,
         dqkv_ref, dz_ref, dab_ref, dalog_ref, ddt_ref, dnw_ref, dstate), ride = _split_refs(refs, 9, 6, 1, rider)
        _ride_begin(rider, ride, (n,))
        dprm_refs = (dalog_ref, ddt_ref, dnw_ref)

        @pl.when(pl.program_id(0) == 0)
        def _():
            dstate[...] = jnp.zeros_like(dstate)
            for r in dprm_refs:
                r[...] = jnp.zeros_like(r)
        inv_v = inv_ref[...]
        tile = lambda q, k, v, ab, z, st, alog, dt, nw: _dn_tile(q, k, v, ab, z, st, inv_v, alog, dt, nw)[:2]
        _, vjp = jax.vjp(tile, qkv_ref[:, 0:G], qkv_ref[:, G:2 * G], qkv_ref[:, 2 * G:3 * G], ab_ref[...], z_ref[...],
                         st_ref[0], alog_ref[l:l + 1, :], dt_ref[l:l + 1, :], nw_ref[l:l + 1, :])
        dq, dk, dv, dab, dz, dst, *dprm = vjp((dy_ref[...], dstate[...]))
        dqkv_ref[...] = jnp.concatenate([dq, dk, dv], axis=1)
        dz_ref[...] = dz.astype(BF16)
        dab_ref[...] = dab.astype(BF16)
        dstate[...] = dst
        for r, g in zip(dprm_refs, dprm):
            r[...] += g
        _ride_end(rider, ride, (n,))

    rev = lambda w, blk: pl.BlockSpec((tm, w), lambda i: (n - 1 - i, blk))
    specs, operands = _host(
        rider, [rev(3 * G, 0), rev(G, _C_Z), rev(LANES, _C_AB),
                pl.BlockSpec((1, NH, HD, HD), lambda i: (n - 1 - i, 0, 0, 0)),
                pl.BlockSpec((nb, DN_C, DN_C), lambda i: (n - 1 - i, 0, 0)), rev(G, 2)] + _dn_specs(),
        [rev(3 * G, 0), rev(G, 0), rev(LANES, 0), _acc((1, NH)), _acc((1, NH)), _acc((1, HD))],
        [jax.ShapeDtypeStruct((t, 3 * G), F32), jax.ShapeDtypeStruct((t, G), BF16),
         jax.ShapeDtypeStruct((t, LANES), BF16), jax.ShapeDtypeStruct((1, NH), F32),
         jax.ShapeDtypeStruct((1, NH), F32), jax.ShapeDtypeStruct((1, HD), F32)],
        [pltpu.VMEM((NH, HD, HD), F32)], [qkv, p, p, states, inv, dmix, *params])
    return pl.pallas_call(body, name="dn_bwd", grid=(n,), compiler_params=_cparams("arbitrary"), **specs)(*operands)


def _gla_specs(l):
    return [_layer((16, G), l), _resident((DEPTH, G)), _resident((DEPTH, HD))]


def _gla_fwd(p, params, l, cps=8, rider=None):
    t = p.shape[0]
    tm = GLA_C * cps

    def body(*refs):
        (qkv_ref, z_ref, glr_ref, w2_ref, gb_ref, nw_ref, y_ref, st_ref, state), ride = _split_refs(refs, 6, 2, 1, rider)
        _ride_begin(rider, ride, (t // tm,))

        @pl.when(pl.program_id(0) == 0)
        def _():
            state[...] = jnp.zeros_like(state)
        st_ref[0] = state[...]
        y, new = _gla_tile(qkv_ref[:, 0:G], qkv_ref[:, G:2 * G], qkv_ref[:, 2 * G:3 * G], glr_ref[...], z_ref[...],
                           state[...], w2_ref[...], gb_ref[l:l + 1, :], nw_ref[l:l + 1, :])
        y_ref[...] = y.astype(BF16)
        state[...] = new
        _ride_end(rider, ride, (t // tm,))

    specs, operands = _host(
        rider, [_row(tm, 3 * G, _D_QKV), _row(tm, G, _D_Z), _row(tm, LANES, _D_GLR)] + _gla_specs(l),
        [_row(tm, G), _STATE],
        [jax.ShapeDtypeStruct((t, G), BF16), jax.ShapeDtypeStruct((t // tm, NH, HD, HD), F32)],
        [pltpu.VMEM((NH, HD, HD), F32)], [p, p, p, *params])
    return pl.pallas_call(body, name="gla_fwd", grid=(t // tm,), compiler_params=_cparams("arbitrary"), **specs)(*operands)


def _gla_bwd(p, states, dmix, params, l, cps=8):
    t = p.shape[0]
    tm = GLA_C * cps
    n = t // tm

    def body(qkv_ref, z_ref, glr_ref, st_ref, dy_ref, w2_ref, gb_ref, nw_ref,
             dqkv_ref, dz_ref, dglr_ref, dw2_ref, dgb_ref, dnw_ref, dstate):
        dprm_refs = (dw2_ref, dgb_ref, dnw_ref)

        @pl.when(pl.program_id(0) == 0)
        def _():
            dstate[...] = jnp.zeros_like(dstate)
            for r in dprm_refs:
                r[...] = jnp.zeros_like(r)
        _, vjp = jax.vjp(_gla_tile, qkv_ref[:, 0:G], qkv_ref[:, G:2 * G], qkv_ref[:, 2 * G:3 * G], glr_ref[...],
                         z_ref[...], st_ref[0], w2_ref[...], gb_ref[l:l + 1, :], nw_ref[l:l + 1, :])
        dq, dk, dv, dglr, dz, dst, *dprm = vjp((dy_ref[...], dstate[...]))
        dqkv_ref[...] = jnp.concatenate([dq, dk, dv], axis=1).astype(BF16)
        dz_ref[...] = dz.astype(BF16)
        dglr_ref[...] = dglr.astype(BF16)
        dstate[...] = dst
        for r, g in zip(dprm_refs, dprm):
            r[...] += g

    rev = lambda w, blk: pl.BlockSpec((tm, w), lambda i: (n - 1 - i, blk))
    return pl.pallas_call(
        body, name="gla_bwd", grid=(n,),
        in_specs=[rev(3 * G, _D_QKV), rev(G, _D_Z), rev(LANES, _D_GLR),
                  pl.BlockSpec((1, NH, HD, HD), lambda i: (n - 1 - i, 0, 0, 0)), rev(G, 3)] + _gla_specs(l),
        out_specs=[rev(3 * G, 0), rev(G, 0), rev(LANES, 0), _acc((16, G)), _acc((1, G)), _acc((1, HD))],
        out_shape=[jax.ShapeDtypeStruct((t, 3 * G), BF16), jax.ShapeDtypeStruct((t, G), BF16),
                   jax.ShapeDtypeStruct((t, LANES), BF16), jax.ShapeDtypeStruct((16, G), F32),
                   jax.ShapeDtypeStruct((1, G), F32), jax.ShapeDtypeStruct((1, HD), F32)],
        scratch_shapes=[pltpu.VMEM((NH, HD, HD), F32)],
        compiler_params=_cparams("arbitrary"))(p, p, p, states, dmix, *params)


def _adamw_math(w, g, m, v):
    m = ADAM_B1 * m + (1.0 - ADAM_B1) * g
    v = ADAM_B2 * v + (1.0 - ADAM_B2) * (g * g)
    m_hat = m / (1.0 - ADAM_B1 ** ADAM_STEP)
    v_hat = v / (1.0 - ADAM_B2 ** ADAM_STEP)
    return -ADAM_LR * (m_hat / (jnp.sqrt(v_hat) + ADAM_EPS) + ADAM_WD * w), m, v


def _adamw_large(parts, w, m, v, name, tr):
    _, r, c = w.shape

    def body(p0_ref, p1_ref, w_ref, m_ref, v_ref, g_ref, d_ref, nm_ref, nv_ref):
        def total(p_ref):
            g = p_ref[0].astype(F32)
            for k in range(1, N_DEV):
                g = g + p_ref[k].astype(F32)
            return g

        g = jnp.where(pl.program_id(0) == 0, total(p0_ref), total(p1_ref))
        g_ref[...] = g
        d_ref[...], nm_ref[...], nv_ref[...] = _adamw_math(w_ref[...], g, m_ref[...], v_ref[...])

    blk = pl.BlockSpec((None, tr, c), lambda l, i: (l, i, 0))
    part = pl.BlockSpec((N_DEV, tr, c), lambda l, i: (0, i, 0))
    return pl.pallas_call(
        body, name=name, grid=(DEPTH, r // tr), in_specs=[part, part, blk, blk, blk],
        out_specs=[blk] * 4, out_shape=[jax.ShapeDtypeStruct(w.shape, F32)] * 4,
        compiler_params=_cparams("parallel", "parallel"))(*parts, w, m, v)


def _adamw_w_in(parts, w, m, v, tc=512):
    def body(p0_ref, p1_ref, w_ref, m_ref, v_ref, g_ref, d_ref, nm_ref, nv_ref):
        for l, p_ref in enumerate((p0_ref, p1_ref)):
            g = p_ref[0, 0:IN_SHARD, :].astype(F32)
            for k in range(1, N_DEV):
                g = g + p_ref[k, 0:IN_SHARD, :].astype(F32)
            g_ref[:, l, :] = g
            d_ref[:, l, :], nm_ref[:, l, :], nv_ref[:, l, :] = _adamw_math(w_ref[:, l, :], g, m_ref[:, l, :], v_ref[:, l, :])

    blk = pl.BlockSpec((IN_SHARD, DEPTH, tc), lambda i: (0, 0, i))
    part = pl.BlockSpec((N_DEV, IN_ROWS, tc), lambda i: (0, 0, i))
    return pl.pallas_call(
        body, name="adamw_w_in", grid=(D // tc,), in_specs=[part, part, blk, blk, blk], out_specs=[blk] * 4,
        out_shape=[jax.ShapeDtypeStruct(w.shape, F32)] * 4, compiler_params=_cparams("parallel"))(*parts, w, m, v)


_SLAB_AT = {
    "sgu_w_spatial": (0, 0, SGU_C, LANES),
    "sgu_b_spatial": (128, 0, NH, SGU_C),
    "norm1_w": (132, 0, 1, D),
    "norm2_w": (133, 0, 1, D),
    "sgu_ln_w": (134, 0, 1, G),
    "sgu_ln_b": (134, 256, 1, G),
    "gla_gate_bias": (134, 512, 1, G),
    "dn_norm_w": (134, 768, 1, HD),
    "gla_norm_w": (134, 896, 1, HD),
    "dn_a_log": (135, 0, 1, NH),
    "dn_dt_bias": (135, 128, 1, NH),
    "gla_w_gate2": (136, 0, 16, G),
    "dn_conv_w": (136, 256, 4, 3 * G),
    "sc_conv_w": (140, 256, 3, G),
}
_LAYER_ROWS = 152
_FINAL_ROW = DEPTH * _LAYER_ROWS
_SLAB_ROWS, _SLAB_COLS = _FINAL_ROW + 8, 1024
_SLAB_ORDER = tuple(_SLAB_AT)
_SHARDED_SMALL = ("sc_conv_w", "dn_conv_w", "gla_w_gate2")
_REPLICATED = tuple(k for k in _SLAB_ORDER if k not in _SHARDED_SMALL)


def _region(name, l, h=0):
    r0, c0, nr, nc = _SLAB_AT[name]
    return slice(_LAYER_ROWS * l + r0, _LAYER_ROWS * l + r0 + nr), slice(c0 + nc * h, c0 + nc * (h + 1))


def _pack_small(layer_grads, d_final):
    def body(*refs):
        slab = refs[-1]
        slab[...] = jnp.zeros_like(slab)
        it = iter(refs[:-1])
        for l in range(DEPTH):
            for name in _SLAB_ORDER:
                ref = next(it)
                if name == "sgu_w_spatial":
                    for h in range(NH):
                        slab[_region(name, l, h)] = ref[h]
                else:
                    slab[_region(name, l)] = ref[...]
        slab[_FINAL_ROW:_FINAL_ROW + 1, :] = next(it)[...]

    flat = [layer_grads[l][name] for l in range(DEPTH) for name in _SLAB_ORDER] + [d_final]
    return pl.pallas_call(body, name="pack_small_grads", out_shape=jax.ShapeDtypeStruct((_SLAB_ROWS, _SLAB_COLS), F32),
                          compiler_params=_cparams())(*flat)


def _adamw_small(parts, w, m, v):
    names = _REPLICATED + ("final_norm_w",)
    n_in = len(names)

    def body(*refs):
        def total(rows, cols):
            g = refs[0][0, rows, cols]
            for k in range(1, N_DEV):
                g = g + refs[0][k, rows, cols]
            return g

        w_refs = dict(zip(names, refs[1:1 + n_in]))
        m_refs = dict(zip(names, refs[1 + n_in:1 + 2 * n_in]))
        v_refs = dict(zip(names, refs[1 + 2 * n_in:1 + 3 * n_in]))
        outs = refs[1 + 3 * n_in:]
        out_refs = [dict(zip(names, outs[j * n_in:(j + 1) * n_in])) for j in range(4)]
        full_refs = dict(zip(_SHARDED_SMALL, outs[4 * n_in:]))

        def update(name, idx, g):
            res = (g,) + _adamw_math(w_refs[name][idx], g, m_refs[name][idx], v_refs[name][idx])
            for o, val in zip(out_refs, res):
                o[name][idx] = val

        for l in range(DEPTH):
            for name in _REPLICATED:
                if name == "sgu_w_spatial":
                    for h in range(NH):
                        update(name, (l, h), total(*_region(name, l, h)))
                elif name == "sgu_b_spatial":
                    update(name, (l,), total(*_region(name, l)))
                else:
                    update(name, (slice(l, l + 1), slice(None)), total(*_region(name, l)))
            for name in _SHARDED_SMALL:
                full_refs[name][l] = total(*_region(name, l))
        update("final_norm_w", (slice(None), slice(None)), total(slice(_FINAL_ROW, _FINAL_ROW + 1), slice(None)))

    shapes = [jax.ShapeDtypeStruct(w[k].shape, F32) for k in names]
    full_shapes = [jax.ShapeDtypeStruct((DEPTH,) + _SLAB_AT[k][2:], F32) for k in _SHARDED_SMALL]
    outs = pl.pallas_call(body, name="adamw_small", out_shape=shapes * 4 + full_shapes, compiler_params=_cparams())(
        parts, *[w[k] for k in names], *[m[k] for k in names], *[v[k] for k in names])
    result = [dict(zip(names, outs[j * n_in:(j + 1) * n_in])) for j in range(4)]
    return result, dict(zip(_SHARDED_SMALL, outs[4 * n_in:]))


def _adamw_shards(g, w, m, v):
    names = _SHARDED_SMALL
    n = len(names)

    def body(*refs):
        for j in range(n):
            g_v = refs[j][...]
            res = _adamw_math(refs[n + j][...], g_v, refs[2 * n + j][...], refs[3 * n + j][...])
            for o, val in zip(refs[4 * n + j::n], res):
                o[...] = val

    shapes = [jax.ShapeDtypeStruct(w[k].shape, F32) for k in names]
    outs = pl.pallas_call(body, name="adamw_small_shards", out_shape=shapes * 3, compiler_params=_cparams())(
        *[g[k] for k in names], *[w[k] for k in names], *[m[k] for k in names], *[v[k] for k in names])
    return [dict(zip(names, outs[j * n:(j + 1) * n])) for j in range(3)]


_ANY = pl.BlockSpec(memory_space=pl.ANY)


def _place():
    return lax.axis_index("x"), lax.axis_index("y"), lax.axis_index("c")


def _sems(n):
    return [pltpu.SemaphoreType.DMA((n, 7)), pltpu.SemaphoreType.DMA((n, 7)), pltpu.SemaphoreType.DMA((n,))]


class _Gather:
    def __init__(self, blocks, second, forward_at=1.0):
        self.inputs, self.second, self.forward_at = list(blocks), list(second), forward_at
        self.out_shape = [jax.ShapeDtypeStruct((a.shape[0], N_DEV) + a.shape[1:] if s else (N_DEV,) + a.shape, a.dtype)
                          for a, s in zip(blocks, second)]
        self.scratch = _sems(len(blocks))

    def _copies(self, refs):
        x_refs, out_refs, (send_sems, recv_sems, local_sems) = refs
        n = len(x_refs)
        x, y, c = _place()
        me, sibling = (x, y, c), (x, y, 1 - c)
        chips = [(1 - x, y), (x, 1 - y), (1 - x, 1 - y)]

        def slot(j, px, py, pc):
            idx = 4 * px + 2 * py + pc
            return out_refs[j].at[:, idx] if self.second[j] else out_refs[j].at[idx]

        def copies(k, block, to, own=False):
            return [pltpu.make_async_remote_copy(
                src_ref=x_refs[j] if own else slot(j, *block), dst_ref=slot(j, *block),
                send_sem=send_sems.at[j, k], recv_sem=recv_sems.at[j, k], device_id=to,
                device_id_type=pl.DeviceIdType.MESH) for j in range(n)]

        mine = [pltpu.make_async_copy(x_refs[j], slot(j, *me), local_sems.at[j]) for j in range(n)]
        first = copies(0, me, sibling, own=True)
        for j, chip in enumerate(chips):
            first += copies(1 + j, me, (*chip, c), own=True)
        landed = [copies(1 + j, (*chip, c), me) for j, chip in enumerate(chips)]
        onward = [copies(4 + j, (*chip, c), sibling) for j, chip in enumerate(chips)]
        from_sibling = copies(0, sibling, me)
        for j, chip in enumerate(chips):
            from_sibling += copies(4 + j, (*chip, 1 - c), me)
        return mine, first, landed, onward, from_sibling

    def start(self, refs):
        mine, first, _, _, _ = self._copies(refs)
        for cp in mine + first:
            cp.start()

    def forward(self, refs):
        _, _, landed, onward, _ = self._copies(refs)
        for arrived, passed in zip(landed, onward):
            for cp in arrived:
                cp.wait_recv()
            for cp in passed:
                cp.start()

    def finish(self, refs):
        mine, first, _, onward, from_sibling = self._copies(refs)
        for cp in from_sibling:
            cp.wait_recv()
        for cp in first + [cp for passed in onward for cp in passed]:
            cp.wait_send()
        for cp in mine:
            cp.wait()


class _Exchange:
    forward_at = 1.0

    def __init__(self, sends):
        self.inputs = list(sends)
        self.out_shape = [jax.ShapeDtypeStruct(a.shape, a.dtype) for a in sends]
        self.scratch = _sems(len(sends))

    def _copies(self, refs):
        x_refs, out_refs, (send_sems, recv_sems, local_sems) = refs
        n = len(x_refs)
        x, y, c = _place()
        me = 4 * x + 2 * y + c
        sent, landing = [], []
        for k in range(1, N_DEV):
            px, py, pc = x ^ ((k >> 2) & 1), y ^ ((k >> 1) & 1), c ^ (k & 1)
            them = 4 * px + 2 * py + pc
            for j in range(n):
                for here, there, into in ((them, me, sent), (me, them, landing)):
                    into.append(pltpu.make_async_remote_copy(
                        src_ref=x_refs[j].at[here], dst_ref=out_refs[j].at[there], send_sem=send_sems.at[j, k - 1],
                        recv_sem=recv_sems.at[j, k - 1], device_id=(px, py, pc), device_id_type=pl.DeviceIdType.MESH))
        mine = [pltpu.make_async_copy(x_refs[j].at[me], out_refs[j].at[me], local_sems.at[j]) for j in range(n)]
        return mine, sent, landing

    def start(self, refs):
        mine, sent, _ = self._copies(refs)
        for cp in mine + sent:
            cp.start()

    def forward(self, refs):
        pass

    def finish(self, refs):
        mine, sent, landing = self._copies(refs)
        for cp in landing:
            cp.wait_recv()
        for cp in sent:
            cp.wait_send()
        for cp in mine:
            cp.wait()


def _run(rider, name):
    n_in, n_out = len(rider.inputs), len(rider.out_shape)

    def body(*refs):
        parts = (refs[:n_in], refs[n_in:n_in + n_out], refs[n_in + n_out:])
        rider.start(parts)
        rider.forward(parts)
        rider.finish(parts)

    return pl.pallas_call(body, name=name, out_shape=rider.out_shape, in_specs=[_ANY] * n_in, out_specs=[_ANY] * n_out,
                          scratch_shapes=rider.scratch)(*rider.inputs)


def _split_refs(refs, n_in, n_out, n_scratch, rider):
    r_in = len(rider.inputs) if rider else 0
    r_out = len(rider.out_shape) if rider else 0
    a = n_in + r_in
    b = a + n_out + r_out
    own = refs[:n_in] + refs[a:a + n_out] + refs[b:b + n_scratch]
    return own, (refs[n_in:a], refs[a + n_out:b], refs[b + n_scratch:])


def _grid_step(grid):
    step, stride = 0, 1
    for axis in reversed(range(len(grid))):
        step = step + pl.program_id(axis) * stride
        stride *= grid[axis]
    return step


def _ride_begin(rider, ride_refs, grid):
    if rider is not None:
        pl.when(_grid_step(grid) == 0)(lambda: rider.start(ride_refs))


def _ride_end(rider, ride_refs, grid):
    if rider is not None:
        steps = 1
        for n in grid:
            steps *= n
        step = _grid_step(grid)
        pl.when(step == min(steps - 1, int(steps * rider.forward_at)))(lambda: rider.forward(ride_refs))
        pl.when(step == steps - 1)(lambda: rider.finish(ride_refs))


def _host(rider, in_specs, out_specs, out_shape, scratch, operands):
    if rider is None:
        return dict(in_specs=in_specs, out_specs=out_specs, out_shape=out_shape, scratch_shapes=scratch), operands
    return dict(in_specs=in_specs + [_ANY] * len(rider.inputs), out_specs=out_specs + [_ANY] * len(rider.out_shape),
                out_shape=out_shape + rider.out_shape, scratch_shapes=scratch + rider.scratch), operands + rider.inputs


_LATE = ("w_gate_up", "w_out", "w_down")


def _local_grads(x, target, w, late=None, exchange=False):
    w = dict(w)
    w_in = list(w["w_in"])
    dn_params = (w["dn_a_log"], w["dn_dt_bias"], w["dn_norm_w"])
    gla_params = (w["gla_w_gate2"], w["gla_gate_bias"], w["gla_norm_w"])
    sgu_params = (w["sgu_ln_w"], w["sgu_ln_b"], w["sgu_w_spatial"], w["sgu_b_spatial"])
    saved = []
    for l in range(DEPTH):
        riding = l == 0 and late is not None
        h, p, *got = _in_proj(x, w["norm1_w"], w_in[l], l,
                              rider=_Gather([late["w_down"]], [True], forward_at=0.85) if riding else None)
        if riding:
            w["w_down"] = got[0].reshape(DEPTH, FF, D)
        ya = _sgu_fwd(p, *sgu_params, l)
        yb = _sconv_fwd(p, w["sc_conv_w"], l)
        qkv_c = _qkv_conv_fwd(p, w["dn_conv_w"], l)
        yc, st_c, inv_c, *got = _dn_fwd(
            qkv_c, p, dn_params, l,
            rider=_Gather([late["w_gate_up"], late["w_out"]], [False, True], forward_at=0.85) if riding else None)
        if riding:
            w.update(w_gate_up=got[0], w_out=got[1].reshape(DEPTH, D, D))
        yd, st_d, *got = _gla_fwd(p, gla_params, l,
                                  rider=_Gather([late["w_in"]], [False], forward_at=0.7) if riding else None)
        if riding:
            w_in[1] = _relayout_w_in(got[0])
        mix, x1, h2, gu, act, x2 = _out_mlp(x, (ya, yb, yc, yd), w["w_out"], w["norm2_w"], w["w_gate_up"], w["w_down"], l)
        saved.append(dict(x=x, h=h, p=p, qkv_c=qkv_c, st_c=st_c, inv_c=inv_c, st_d=st_d, mix=mix, x1=x1, h2=h2, gu=gu,
                          act=act))
        x = x2
    loss, d_final, dx = _loss_head(x, w["final_norm_w"], target)
    large = {k: [None] * DEPTH for k in ("w_in", "w_out", "w_gate_up", "w_down")}
    small = [None] * DEPTH
    for l in reversed(range(DEPTH)):
        s = saved[l]
        p = s["p"]
        g = {}
        riding = exchange and l == 0

        def exchanging(pairs):
            return _Exchange([large[k][j] for k, j in pairs]) if riding else None

        def arrive(pairs, arrived):
            for (k, j), a in zip(pairs, arrived):
                large[k][j] = a

        pairs = (("w_gate_up", 1), ("w_down", 1))
        dgu, dx1, dmix, g["norm2_w"], *arrived = _mlp_out_bwd(
            dx, s["x1"], s["gu"], w["norm2_w"], w["w_down"], w["w_gate_up"], w["w_out"], l, rider=exchanging(pairs))
        arrive(pairs, arrived)
        pairs = (("w_in", 1), ("w_out", 1))
        d_gu, *arrived = _wgrad_rows(dgu, s["h2"], "wgrad_gate_up", rider=exchanging(pairs)) if riding else (
            _wgrad_rows(dgu, s["h2"], "wgrad_gate_up"),)
        arrive(pairs, arrived)
        large["w_gate_up"][l] = d_gu.reshape(N_DEV, GU_SHARD, D)
        large["w_down"][l] = _wgrad_rows(s["act"], dx, "wgrad_down").reshape(N_DEV, FF // N_DEV, D)
        large["w_out"][l] = _wgrad_rows(s["mix"], dx1, "wgrad_out", rows=D).reshape(N_DEV, D // N_DEV, D)
        d_a, g["sgu_ln_w"], g["sgu_ln_b"], g["sgu_w_spatial"], g["sgu_b_spatial"] = _sgu_bwd(p, dmix, *sgu_params, l)
        d_b, g["sc_conv_w"] = _sconv_bwd(p, dmix, w["sc_conv_w"], l)
        pairs = tuple((k, 0) for k in _LATE)
        dqkv_c, dz_c, dab, g["dn_a_log"], g["dn_dt_bias"], g["dn_norm_w"], *arrived = _dn_bwd(
            s["qkv_c"], p, s["st_c"], s["inv_c"], dmix, dn_params, l, rider=exchanging(pairs))
        arrive(pairs, arrived)
        d_c, g["dn_conv_w"] = _qkv_conv_bwd(p, dqkv_c, w["dn_conv_w"], l)
        d_d, dz_d, dglr, g["gla_w_gate2"], g["gla_gate_bias"], g["gla_norm_w"] = _gla_bwd(p, s["st_d"], dmix, gla_params, l)
        dp, dx, g["norm1_w"] = _in_proj_bwd((d_c, d_d, d_a, d_b, dz_c, dz_d, dab, dglr), s["x"], dx1, w["norm1_w"],
                                            w_in[l], l)
        small[l] = g
        w_in_rows = dict(rows=P // 2, out_dtype=F32)
        if riding:
            d_w_in, small = _wgrad_rows(dp, s["h"], "wgrad_in", **w_in_rows,
                                        rider=_Gather([_pack_small(small, d_final)], [False], forward_at=0.75))
        else:
            d_w_in = _wgrad_rows(dp, s["h"], "wgrad_in", **w_in_rows)
        large["w_in"][l] = _scatter_w_in_grad(d_w_in)
    return loss[0, 0], dx, large, small, d_final


_ORDER = ("norm1_w", "w_in", "sgu_ln_w", "sgu_ln_b", "sgu_w_spatial", "sgu_b_spatial", "sc_conv_w", "dn_conv_w",
          "dn_a_log", "dn_dt_bias", "dn_norm_w", "gla_w_gate2", "gla_gate_bias", "gla_norm_w", "w_out", "norm2_w",
          "w_gate_up", "w_down", "final_norm_w")
_LARGE = ("w_in", "w_gate_up", "w_out", "w_down")
_LARGE_TILE = {"w_gate_up": 352, "w_out": 128, "w_down": 352}


def _gather_cols(g):
    return jnp.transpose(g, (1, 2, 0, 3)).reshape(g.shape[1], g.shape[2], N_DEV * g.shape[3])


def kernel(x, norm1_w, w_in, sgu_ln_w, sgu_ln_b, sgu_w_spatial, sgu_b_spatial, sc_conv_w, dn_conv_w, dn_a_log, dn_dt_bias, dn_norm_w, gla_w_gate2, gla_gate_bias, gla_norm_w, w_out, norm2_w, w_gate_up, w_down, final_norm_w, loss_target, m_norm1_w, m_w_in, m_sgu_ln_w, m_sgu_ln_b, m_sgu_w_spatial, m_sgu_b_spatial, m_sc_conv_w, m_dn_conv_w, m_dn_a_log, m_dn_dt_bias, m_dn_norm_w, m_gla_w_gate2, m_gla_gate_bias, m_gla_norm_w, m_w_out, m_norm2_w, m_w_gate_up, m_w_down, m_final_norm_w, v_norm1_w, v_w_in, v_sgu_ln_w, v_sgu_ln_b, v_sgu_w_spatial, v_sgu_b_spatial, v_sc_conv_w, v_dn_conv_w, v_dn_a_log, v_dn_dt_bias, v_dn_norm_w, v_gla_w_gate2, v_gla_gate_bias, v_gla_norm_w, v_w_out, v_norm2_w, v_w_gate_up, v_w_down, v_final_norm_w):
    args = dict(locals())
    wts = {k: args[k] for k in _ORDER}
    mom = {k: args["m_" + k] for k in _ORDER}
    var = {k: args["v_" + k] for k in _ORDER}
    for d in (wts, mom, var):
        d["final_norm_w"] = d["final_norm_w"][None]
    me = 4 * lax.axis_index("x") + 2 * lax.axis_index("y") + lax.axis_index("c")
    for d in (wts, mom, var):
        d["w_gate_up"] = jnp.swapaxes(d["w_gate_up"], 1, 2)

    late = {k: wts[k].astype(BF16) for k in _LATE}
    w_in = wts["w_in"].astype(BF16)
    late["w_in"] = w_in[1]
    got = _run(_Gather([w_in[0]] + [wts[k] for k in _SHARDED_SMALL], [False] * 4), "gather_w_in")
    full = dict(wts)
    full["w_in"] = [_relayout_w_in(got[0]), None]
    for k, g in zip(_SHARDED_SMALL, got[1:]):
        full[k] = _gather_cols(g)

    loss, dx, large, small, d_final = _local_grads(x[0], loss_target[0], full, late=late, exchange=True)
    loss = lax.psum(loss, ("x", "y", "c"))

    large["w_in"][0], = _run(_Exchange([large["w_in"][0]]), "exchange_grads")
    result = {}
    for k in ("w_gate_up", "w_out", "w_down"):
        outs = _adamw_large(large[k], wts[k], mom[k], var[k], "adamw_" + k, _LARGE_TILE[k])
        for kind, a in zip(("grad", "delta", "new_m", "new_v"), outs):
            result[kind, k] = jnp.swapaxes(a, 1, 2) if k == "w_gate_up" else a
    outs = _adamw_w_in(large["w_in"], *[jnp.transpose(d["w_in"], (2, 0, 1)) for d in (wts, mom, var)])
    for kind, a in zip(("grad", "delta", "new_m", "new_v"), outs):
        result[kind, "w_in"] = jnp.transpose(a, (1, 2, 0))

    slabs = small
    rep = _REPLICATED + ("final_norm_w",)
    outs, summed = _adamw_small(slabs, {k: wts[k] for k in rep}, {k: mom[k] for k in rep}, {k: var[k] for k in rep})
    for kind, d in zip(("grad", "delta", "new_m", "new_v"), outs):
        for k, a in d.items():
            result[kind, k] = a[0] if k == "final_norm_w" else a
    own = {k: lax.dynamic_slice_in_dim(summed[k], me * wts[k].shape[-1], wts[k].shape[-1], axis=2) for k in _SHARDED_SMALL}
    outs = _adamw_shards(own, {k: wts[k] for k in _SHARDED_SMALL}, {k: mom[k] for k in _SHARDED_SMALL},
                         {k: var[k] for k in _SHARDED_SMALL})
    for kind, d in zip(("grad", "delta", "new_m", "new_v"), [own] + outs):
        for k, a in d.items():
            result[kind, k] = a

    return (loss, dx[None], *[result[kind, k] for kind in ("grad", "delta", "new_m", "new_v") for k in _ORDER])
```

```python
import functools

import jax
import jax.numpy as jnp
from jax import lax
from jax.experimental import pallas as pl
from jax.experimental.pallas import tpu as pltpu

F32, BF16 = jnp.float32, jnp.bfloat16
DEPTH = 2
D = 1024
G = 256
NH, HD = 4, 64
FF = 2816
IN_COLS = 3352
P = 3584
EPS = 1e-6
SGU_C, DN_C, GLA_C = 128, 64, 64
N_DEV = 8
IN_SHARD = IN_COLS // N_DEV
GU_SHARD = 2 * FF // N_DEV
LANES = 128
VMEM_LIMIT = 56 * 2 ** 20

_PAD_SEGS = ((1280, 2048, 0),
             (2312, 3080, 0),
             (0, 512, 0),
             (512, 1280, 0),
             (2056, 2312, 0),
             (3096, 3352, 0),
             (2048, 2056, 120),
             (3080, 3096, 112))

ADAM_LR, ADAM_B1, ADAM_B2, ADAM_EPS, ADAM_WD, ADAM_STEP = 0.001, 0.9, 0.999, 1e-08, 0.01, 10


def _cparams(*sem):
    return pltpu.CompilerParams(dimension_semantics=sem, vmem_limit_bytes=VMEM_LIMIT)


def _resident(shape):
    return pl.BlockSpec(shape, lambda *_: (0,) * len(shape), pipeline_mode=pl.Buffered(1))


def _layer(shape, l):
    return pl.BlockSpec((None,) + tuple(shape), lambda *_: (l,) + (0,) * len(shape), pipeline_mode=pl.Buffered(1))


def _acc(shape):
    return pl.BlockSpec(shape, lambda *_: (0,) * len(shape))


def _dg(a, b, ca, cb):
    lead = a.ndim - 2
    batch = ((0,), (0,)) if lead else ((), ())
    return lax.dot_general(a, b, (((ca + lead,), (cb + lead,)), batch), preferred_element_type=F32)


def _split(t):
    hi = t.astype(BF16)
    return hi, (t - hi.astype(F32)).astype(BF16)


def _dg3(a, b, ca, cb):
    (ah, al), (bh, bl) = a, b
    return _dg(ah, bh, ca, cb) + (_dg(ah, bl, ca, cb) + _dg(al, bh, ca, cb))


def _make_dot(accurate):
    def raw(a, b, form):
        ca = 0 if form == "tn" else 1
        cb = 1 if form == "nt" else 0
        if accurate:
            return _dg3(_split(a), _split(b), ca, cb)
        return _dg(a.astype(BF16), b.astype(BF16), ca, cb)

    @functools.partial(jax.custom_vjp, nondiff_argnums=(2,))
    def dot(a, b, form):
        return raw(a, b, form)

    def fwd(a, b, form):
        return raw(a, b, form), (a, b)

    def bwd(form, res, g):
        a, b = res
        if form == "nn":
            return raw(g, b, "nt"), raw(a, g, "tn")
        if form == "nt":
            return raw(g, b, "nn"), raw(g, a, "tn")
        return raw(b, g, "nt"), raw(a, g, "nn")

    dot.defvjp(fwd, bwd)
    return dot


_bdot = _make_dot(False)
_hdot = _make_dot(True)


def _inv_unit_lower(low):
    n = low.shape[-1]
    eye = (lax.broadcasted_iota(jnp.int32, (n, n), 0) == lax.broadcasted_iota(jnp.int32, (n, n), 1)).astype(F32)
    p = -low
    inv = eye + p
    ps = _split(p)
    k = 1
    while 2 * k < n:
        ps = _split(_dg3(ps, ps, 1, 0))
        inv = inv + _dg3(_split(inv), ps, 1, 0)
        k *= 2
    return inv


@jax.custom_vjp
def _unit_lower_solve(low, rhs, inv):
    return _dg3(_split(inv), _split(rhs), 1, 0)


def _uls_fwd(low, rhs, inv):
    sol = _dg3(_split(inv), _split(rhs), 1, 0)
    return sol, (inv, sol)


def _uls_bwd(res, g):
    inv, sol = res
    d_rhs = _dg3(_split(inv), _split(g), 0, 0)
    return -_dg3(_split(d_rhs), _split(sol), 1, 1), d_rhs, jnp.zeros_like(inv)


_unit_lower_solve.defvjp(_uls_fwd, _uls_bwd)


def _sigmoid(x):
    return 0.5 * jnp.tanh(0.5 * x) + 0.5


def _softplus(x):
    return jnp.maximum(x, 0.0) + jnp.log(1.0 + jnp.exp(-jnp.maximum(x, -x)))


def _gelu(x):
    return 0.5 * x * (1.0 + jnp.tanh(0.7978845608028654 * (x + 0.044715 * (x * x * x))))


def _tri(n):
    ri = lax.broadcasted_iota(jnp.int32, (n, n), 0)
    ci = lax.broadcasted_iota(jnp.int32, (n, n), 1)
    return ri, ci


def _stack(ts):
    return jnp.concatenate([t[None] for t in ts], axis=0)


@jax.custom_vjp
def _head_sums(x):
    ri, ci = _tri(x.shape[-1])
    shift = HD.bit_length() - 1
    ones = (jnp.right_shift(ri, shift) == jnp.right_shift(ci, shift)).astype(BF16)
    hi, lo = _split(x)
    return _dg(hi, ones, 1, 0) + _dg(lo, ones, 1, 0)


_head_sums.defvjp(lambda x: (_head_sums(x), None), lambda _, g: (_head_sums(g),))


def _chunk_mask_dot(x, c, running, transpose):
    ri, ci = _tri(x.shape[0])
    shift = c.bit_length() - 1
    mask = jnp.right_shift(ri, shift) == jnp.right_shift(ci, shift)
    if running:
        mask = mask & (ri >= ci)
    hi, lo = _split(x)
    m = mask.astype(BF16)
    ca = 0 if transpose else 1
    return _dg(m, hi, ca, 0) + _dg(m, lo, ca, 0)


@functools.partial(jax.custom_vjp, nondiff_argnums=(1, 2))
def _chunk_sums(x, c, running):
    return _chunk_mask_dot(x, c, running, False)


_chunk_sums.defvjp(lambda x, c, running: (_chunk_mask_dot(x, c, running, False), None),
                   lambda c, running, _, g: (_chunk_mask_dot(g, c, running, True),))


def _spread_onto(x, first, transpose):
    ri = lax.broadcasted_iota(jnp.int32, (LANES, G), 0)
    ci = lax.broadcasted_iota(jnp.int32, (LANES, G), 1)
    sel = (ri == first + jnp.right_shift(ci, HD.bit_length() - 1)).astype(BF16)
    hi, lo = _split(x)
    cb = 1 if transpose else 0
    return _dg(hi, sel, 1, cb) + _dg(lo, sel, 1, cb)


@functools.partial(jax.custom_vjp, nondiff_argnums=(1,))
def _spread(x, first):
    return _spread_onto(x, first, False)


_spread.defvjp(lambda x, first: (_spread_onto(x, first, False), None),
               lambda first, _, g: (_spread_onto(g, first, True),))


def _head_columns_dot(x, transpose):
    sel = (lax.broadcasted_iota(jnp.int32, (NH, G), 0)
           == jnp.right_shift(lax.broadcasted_iota(jnp.int32, (NH, G), 1), HD.bit_length() - 1)).astype(BF16)
    hi, lo = _split(x)
    if transpose:
        return _dg(sel, hi, 1, 1) + _dg(sel, lo, 1, 1)
    return _dg(hi, sel, 0, 0) + _dg(lo, sel, 0, 0)


@jax.custom_vjp
def _head_columns(x):
    return _head_columns_dot(x, False)


_head_columns.defvjp(lambda x: (_head_columns_dot(x, False), None), lambda _, g: (_head_columns_dot(g, True),))


def _sgu_chunk(uv, ln_w, ln_b, ws, bs):
    u = _gelu(uv[:, :G])
    v = _gelu(uv[:, G:])
    mu = jnp.mean(v, axis=-1, keepdims=True)
    vc = v - mu
    var = jnp.mean(vc * vc, axis=-1, keepdims=True)
    vn = vc * lax.rsqrt(var + EPS) * ln_w + ln_b
    ri, ci = _tri(SGU_C)
    mixed = [_bdot(jnp.where(ri >= ci, ws[h], 0.0), vn[:, HD * h:HD * (h + 1)], "nn") for h in range(NH)]
    return u * (jnp.concatenate(mixed, axis=1) + _head_columns(bs))


def _dn_tile(q, k, v, ab, z, state, inv, a_log, dt_bias, nw):
    c = DN_C
    tm = q.shape[0]
    cps = tm // c
    ri, ci = _tri(tm)
    shift = c.bit_length() - 1
    same = jnp.right_shift(ri, shift) == jnp.right_shift(ci, shift)
    g4 = -jnp.exp(a_log) * _softplus(ab[:, 0:NH] + dt_bias)
    gates = jnp.concatenate([g4, _sigmoid(ab[:, NH:2 * NH]), jnp.zeros((tm, LANES - 2 * NH), F32)], axis=1)
    g, beta = _spread(gates, 0), _spread(gates, NH)
    gc = _chunk_sums(g, c, True)
    gl = _chunk_sums(g, c, False)
    gr4 = _hdot(g4, (same & (ri <= ci)).astype(F32), "tn")
    pairs = [(slice(c * j, c * (j + 1)), h) for j in range(cps) for h in range(NH)]
    heads = lambda t: _stack([t[rows, HD * h:HD * (h + 1)] for rows, h in pairs])
    qn = q * lax.rsqrt(_head_sums(q * q) + EPS) * (HD ** -0.5)
    kn = k * lax.rsqrt(_head_sums(k * k) + EPS)
    eg = jnp.exp(gc)
    kb = kn * beta
    rhs = jnp.concatenate([heads(v * beta), heads(kb * eg)], axis=2)
    qg, kd = heads(qn * eg), heads(kn * jnp.exp(gl - gc))
    cd = _stack([jnp.exp(gl[c * j:c * j + 1, HD * h:HD * (h + 1)]) for j in range(cps) for h in range(NH)])
    qn, kn, kb, gc = heads(qn), heads(kn), heads(kb), heads(gc)
    gr = _stack([gr4[h:h + 1, rows] for rows, h in pairs])
    r2, c2 = _tri(c)
    decay = jnp.exp(jnp.where(r2 >= c2, gc - gr, -jnp.inf))
    low = jnp.where(r2 > c2, _bdot(kb, kn, "nt") * decay, 0.0)
    if inv is None:
        inv = _inv_unit_lower(low)
    sol = _unit_lower_solve(low, rhs, inv)
    u, w = sol[:, :, :HD], sol[:, :, HD:]
    attn = _bdot(qn, kn, "nt") * decay
    ys = []
    for j in range(cps):
        b = slice(NH * j, NH * (j + 1))
        v_new = u[b] - _bdot(w[b], state, "nn")
        o = _bdot(qg[b], state, "nn") + _bdot(attn[b], v_new, "nn")
        state = state * cd[b] + _bdot(kd[b], v_new, "tn")
        on = o * lax.rsqrt(jnp.mean(o * o, axis=-1, keepdims=True) + EPS) * nw
        ys.append(jnp.concatenate([on[h] for h in range(NH)], axis=1))
    return jnp.concatenate(ys, axis=0) * (z * _sigmoid(z)), state, inv


def _gla_tile(q, k, v, glr, z, state_t, w2, gate_bias, nw):
    c = GLA_C
    tm = q.shape[0]
    cps = tm // c
    w2_rows = jnp.concatenate([w2, jnp.zeros((LANES - w2.shape[0], G), F32)], axis=0)
    pre = _bdot(glr, w2_rows, "nn") + gate_bias
    log_a = (jnp.minimum(pre, 0.0) - jnp.log(1.0 + jnp.exp(-jnp.maximum(pre, -pre)))) * (1.0 / 16.0)
    gcum = _chunk_sums(log_a, c, True)
    row = lax.broadcasted_iota(jnp.int32, (c, G), 0)
    qs = q * (HD ** -0.5)
    qa, ka, qg, kl, dec = [], [], [], [], []
    for j in range(cps):
        rows = slice(c * j, c * (j + 1))
        gj = gcum[rows]
        g_mid = jnp.sum(jnp.where(row == c // 2, gj, 0.0), axis=0, keepdims=True)
        g_last = jnp.sum(log_a[rows], axis=0, keepdims=True)
        qa.append(qs[rows] * jnp.exp(gj - g_mid))
        ka.append(k[rows] * jnp.exp(g_mid - gj))
        qg.append(qs[rows] * jnp.exp(gj))
        kl.append(k[rows] * jnp.exp(g_last - gj))
        dec.append(jnp.exp(g_last))
    heads = lambda ts: _stack([t[:, HD * h:HD * (h + 1)] for t in ts for h in range(NH)])
    qa, ka, qg, kl, dec = heads(qa), heads(ka), heads(qg), heads(kl), heads(dec)
    vh = heads([v[c * j:c * (j + 1)] for j in range(cps)])
    r2, c2 = _tri(c)
    o_intra = _bdot(jnp.where(r2 >= c2, _bdot(qa, ka, "nt"), 0.0), vh, "nn")
    ys = []
    for j in range(cps):
        b = slice(NH * j, NH * (j + 1))
        o = o_intra[b] + _bdot(qg[b], state_t, "nt")
        state_t = state_t * dec[b] + _bdot(vh[b], kl[b], "tn")
        on = o * lax.rsqrt(jnp.mean(o * o, axis=-1, keepdims=True) + EPS) * nw
        ys.append(jnp.concatenate([on[h] for h in range(NH)], axis=1))
    return jnp.concatenate(ys, axis=0) * (z * _sigmoid(z)), state_t


def _rms(x, nw):
    r = lax.rsqrt(jnp.mean(x * x, axis=-1, keepdims=True) + EPS)
    xh = x * r
    return xh * nw, xh, r


def _rms_bwd(g, xh, r, nw):
    gw = g * nw
    return r * (gw - xh * jnp.mean(gw * xh, axis=-1, keepdims=True)), jnp.sum(g * xh, axis=0, keepdims=True)


def _row(tm, w, blk=0):
    return pl.BlockSpec((tm, w), lambda i: (i, blk))


def _in_proj(x, nw, wp, l, tm=1024, rider=None):
    t = x.shape[0]
    tm = min(tm, t)

    def body(*refs):
        (x_ref, nw_ref, w_ref, h_ref, p_ref), ride = _split_refs(refs, 3, 2, 0, rider)
        _ride_begin(rider, ride, (t // tm,))
        h = _rms(x_ref[...], nw_ref[l:l + 1, :])[0].astype(BF16)
        h_ref[...] = h
        p_ref[...] = jnp.dot(h, w_ref[...], preferred_element_type=F32)
        _ride_end(rider, ride, (t // tm,))

    specs, operands = _host(
        rider, [_row(tm, D), _resident((DEPTH, D)), _resident((D, P))], [_row(tm, D), _row(tm, P)],
        [jax.ShapeDtypeStruct((t, D), BF16), jax.ShapeDtypeStruct((t, P), F32)], [], [x, nw, wp])
    return pl.pallas_call(body, name="in_proj", grid=(t // tm,), compiler_params=_cparams("arbitrary"),
                          **specs)(*operands)


def _gu_spec(l):
    return pl.BlockSpec((N_DEV, None, GU_SHARD, D), lambda *_: (0, l, 0, 0), pipeline_mode=pl.Buffered(1))


def _out_mlp(x, ys, w_out, nw2, w_gu_t, w_down, l, tm=256):
    t = x.shape[0]

    def body(x_ref, ya, yb, yc, yd, wo_ref, nw_ref, wgu_ref, wd_ref, mix_ref, x1_ref, h2_ref, gu_ref, act_ref, x2_ref):
        mix = jnp.concatenate([ya[...], yb[...], yc[...], yd[...]], axis=1)
        mix_ref[...] = mix
        x1 = x_ref[...] + jnp.dot(mix, wo_ref[...], preferred_element_type=F32)
        x1_ref[...] = x1
        h2 = _rms(x1, nw_ref[l:l + 1, :])[0].astype(BF16)
        h2_ref[...] = h2
        gu = _dg(h2, wgu_ref[...].reshape(2 * FF, D), 1, 1)
        gu_ref[...] = gu.astype(BF16)
        gate, up = gu[:, :FF], gu[:, FF:]
        act = (gate * _sigmoid(gate) * up).astype(BF16)
        act_ref[...] = act
        x2_ref[...] = x1 + jnp.dot(act, wd_ref[...], preferred_element_type=F32)

    return pl.pallas_call(
        body, name="out_mlp", grid=(t // tm,),
        in_specs=[_row(tm, D), _row(tm, G), _row(tm, G), _row(tm, G), _row(tm, G), _layer((D, D), l),
                  _resident((DEPTH, D)), _gu_spec(l), _layer((FF, D), l)],
        out_specs=[_row(tm, D), _row(tm, D), _row(tm, D), _row(tm, 2 * FF), _row(tm, FF), _row(tm, D)],
        out_shape=[jax.ShapeDtypeStruct((t, D), BF16), jax.ShapeDtypeStruct((t, D), F32),
                   jax.ShapeDtypeStruct((t, D), BF16), jax.ShapeDtypeStruct((t, 2 * FF), BF16),
                   jax.ShapeDtypeStruct((t, FF), BF16), jax.ShapeDtypeStruct((t, D), F32)],
        compiler_params=_cparams("parallel"))(x, *ys, w_out, nw2, w_gu_t, w_down)


def _loss_head(x, nw, target, tm=1024):
    t = x.shape[0]
    tm = min(tm, t)

    def body(x_ref, nw_ref, tg_ref, loss_ref, dnw_ref, dx_ref):
        @pl.when(pl.program_id(0) == 0)
        def _():
            loss_ref[...] = jnp.zeros_like(loss_ref)
            dnw_ref[...] = jnp.zeros_like(dnw_ref)
        nw_v = nw_ref[...]
        y, xh, r = _rms(x_ref[...], nw_v)
        err = y - tg_ref[...]
        loss_ref[...] += 0.5 * jnp.sum(err * err) * (1.0 / D)
        dx, dnw = _rms_bwd(err * (1.0 / D), xh, r, nw_v)
        dx_ref[...] = dx
        dnw_ref[...] += dnw

    return pl.pallas_call(
        body, name="loss_head", grid=(t // tm,),
        in_specs=[_row(tm, D), _resident((1, D)), _row(tm, D)],
        out_specs=[_acc((8, LANES)), _acc((1, D)), _row(tm, D)],
        out_shape=[jax.ShapeDtypeStruct((8, LANES), F32), jax.ShapeDtypeStruct((1, D), F32),
                   jax.ShapeDtypeStruct((t, D), F32)],
        compiler_params=_cparams("arbitrary"))(x, nw, target)


def _mlp_out_bwd(dx2, x1, gu, nw2, w_down, w_gu, w_out, l, tm=256, rider=None):
    t = dx2.shape[0]

    def body(*refs):
        (dx2_ref, x1_ref, gu_ref, nw_ref, wd_ref, wgu_ref, wo_ref, dgu_ref, dx1_ref, dmix_ref, dnw_ref), ride = \
            _split_refs(refs, 7, 4, 0, rider)
        _ride_begin(rider, ride, (t // tm,))

        @pl.when(pl.program_id(0) == 0)
        def _():
            dnw_ref[...] = jnp.zeros_like(dnw_ref)
        dx2 = dx2_ref[...]
        dact = _dg(dx2.astype(BF16), wd_ref[...], 1, 1)
        gu = gu_ref[...].astype(F32)
        gate, up = gu[:, :FF], gu[:, FF:]
        s = _sigmoid(gate)
        dgu = jnp.concatenate([dact * up * (s * (1.0 + gate * (1.0 - s))), dact * (gate * s)], axis=1).astype(BF16)
        dgu_ref[...] = dgu
        dh2 = jnp.dot(dgu, wgu_ref[...].reshape(2 * FF, D), preferred_element_type=F32)
        nw_v = nw_ref[l:l + 1, :]
        _, xh, r = _rms(x1_ref[...], nw_v)
        dxn, dnw = _rms_bwd(dh2, xh, r, nw_v)
        dx1 = dx2 + dxn
        dx1_ref[...] = dx1
        dnw_ref[...] += dnw
        dmix_ref[...] = _dg(dx1.astype(BF16), wo_ref[...], 1, 1)
        _ride_end(rider, ride, (t // tm,))

    specs, operands = _host(
        rider, [_row(tm, D), _row(tm, D), _row(tm, 2 * FF), _resident((DEPTH, D)), _layer((FF, D), l), _gu_spec(l),
                _layer((D, D), l)],
        [_row(tm, 2 * FF), _row(tm, D), _row(tm, D), _acc((1, D))],
        [jax.ShapeDtypeStruct((t, 2 * FF), BF16), jax.ShapeDtypeStruct((t, D), F32),
         jax.ShapeDtypeStruct((t, D), F32), jax.ShapeDtypeStruct((1, D), F32)],
        [], [dx2, x1, gu, nw2, w_down, w_gu, w_out])
    return pl.pallas_call(body, name="mlp_out_bwd", grid=(t // tm,), compiler_params=_cparams("arbitrary"),
                          **specs)(*operands)


_DP_WIDTHS = (768, 768, 512, 768, 256, 256, 128, 128)


def _in_proj_bwd(dps, x, dx1, nw, wp, l, tm=512):
    t = x.shape[0]

    def body(*refs):
        dp_refs, (x_ref, dx1_ref, nw_ref, w_ref, dp_ref, dx_ref, dnw_ref) = refs[:8], refs[8:]

        @pl.when(pl.program_id(0) == 0)
        def _():
            dnw_ref[...] = jnp.zeros_like(dnw_ref)
        dp = jnp.concatenate([r[...] for r in dp_refs], axis=1)
        dp_ref[...] = dp
        dh = _dg(dp, w_ref[...], 1, 1)
        nw_v = nw_ref[l:l + 1, :]
        _, xh, r = _rms(x_ref[...], nw_v)
        dxn, dnw = _rms_bwd(dh, xh, r, nw_v)
        dx_ref[...] = dx1_ref[...] + dxn
        dnw_ref[...] += dnw

    return pl.pallas_call(
        body, name="in_proj_bwd", grid=(t // tm,),
        in_specs=[_row(tm, w) for w in _DP_WIDTHS] + [_row(tm, D), _row(tm, D), _resident((DEPTH, D)), _resident((D, P))],
        out_specs=[_row(tm, P), _row(tm, D), _acc((1, D))],
        out_shape=[jax.ShapeDtypeStruct((t, P), BF16), jax.ShapeDtypeStruct((t, D), F32),
                   jax.ShapeDtypeStruct((1, D), F32)],
        compiler_params=_cparams("arbitrary"))(*dps, x, dx1, nw, wp)


def _accumulate(acc, o_ref, t_axis, term):
    @pl.when(pl.program_id(t_axis) == 0)
    def _():
        acc[...] = jnp.zeros_like(acc)
    acc[...] += term

    @pl.when(pl.program_id(t_axis) == pl.num_programs(t_axis) - 1)
    def _():
        o_ref[...] = acc[...].astype(o_ref.dtype)


WGRAD_TOKENS = 2048


WGRAD_ROWS = 2 * GU_SHARD


def _wgrad_rows(a, b, name, tt=WGRAD_TOKENS, rows=WGRAD_ROWS, out_dtype=BF16, rider=None):
    t, m = a.shape
    tt = min(tt, t)
    grid = (m // rows, t // tt)

    def body(*refs):
        (a_ref, b_ref, o_ref, acc), ride = _split_refs(refs, 2, 1, 1, rider)
        _ride_begin(rider, ride, grid)
        _accumulate(acc, o_ref, 1, _dg(a_ref[...], b_ref[...].astype(BF16), 0, 0))
        _ride_end(rider, ride, grid)

    specs, operands = _host(
        rider, [pl.BlockSpec((tt, rows), lambda j, i: (i, j)), pl.BlockSpec((tt, D), lambda j, i: (i, 0))],
        [pl.BlockSpec((rows, D), lambda j, i: (j, 0))], [jax.ShapeDtypeStruct((m, D), out_dtype)],
        [pltpu.VMEM((rows, D), F32)], [a, b])
    out = pl.pallas_call(body, name=name, grid=grid, compiler_params=_cparams("arbitrary", "arbitrary"), **specs)(*operands)
    return out[0] if rider is None else out


def _relayout_w_in(g, tr=512):
    def body(w_ref, o_ref):
        full = jnp.concatenate([w_ref[d] for d in range(N_DEV)], axis=1)
        parts = []
        for a, b, z in _PAD_SEGS:
            parts.append(full[:, a:b])
            if z:
                parts.append(jnp.zeros((tr, z), full.dtype))
        o_ref[...] = jnp.concatenate(parts, axis=1)

    return pl.pallas_call(
        body, name="relayout_w_in", grid=(D // tr,),
        in_specs=[pl.BlockSpec((N_DEV, tr, IN_SHARD), lambda i: (0, i, 0))], out_specs=_row(tr, P),
        out_shape=jax.ShapeDtypeStruct((D, P), g.dtype), compiler_params=_cparams("parallel"))(g)


IN_ROWS = 432


def _scatter_w_in_grad(gp_t, tc=512):
    def body(g_ref, o_ref):
        g = g_ref[...]
        pieces, off = [], 0
        for a, b, z in _PAD_SEGS:
            pieces.append((a, g[off:off + (b - a)]))
            off += (b - a) + z
        spill = -(-(IN_SHARD * (N_DEV - 1) + IN_ROWS - IN_COLS) // 8) * 8
        nat = jnp.concatenate([piece for _, piece in sorted(pieces, key=lambda ap: ap[0])]
                              + [jnp.zeros((spill, tc), F32)], axis=0)
        for d in range(N_DEV):
            o_ref[d] = nat[IN_SHARD * d:IN_SHARD * d + IN_ROWS].astype(BF16)

    return pl.pallas_call(
        body, name="scatter_w_in_grad", grid=(D // tc,),
        in_specs=[pl.BlockSpec((P, tc), lambda i: (0, i))],
        out_specs=pl.BlockSpec((N_DEV, IN_ROWS, tc), lambda i: (0, 0, i)),
        out_shape=jax.ShapeDtypeStruct((N_DEV, IN_ROWS, D), BF16),
        compiler_params=_cparams("parallel"))(gp_t)


_A_BLK = 3


def _sgu_specs(l):
    return [_resident((DEPTH, G)), _resident((DEPTH, G)), _layer((NH, SGU_C, SGU_C), l), _layer((NH, SGU_C), l)]


def _sgu_fwd(p, ln_w, ln_b, ws, bs, l, tm=1024):
    t = p.shape[0]
    tm = min(tm, t)

    def body(uv_ref, lw_ref, lb_ref, ws_ref, bs_ref, y_ref):
        ws_v = [ws_ref[h] for h in range(NH)]
        for c in range(tm // SGU_C):
            rows = pl.ds(c * SGU_C, SGU_C)
            y_ref[rows, :] = _sgu_chunk(uv_ref[rows, :], lw_ref[l:l + 1, :], lb_ref[l:l + 1, :], ws_v,
                                        bs_ref[...]).astype(BF16)

    return pl.pallas_call(
        body, name="sgu_fwd", grid=(t // tm,),
        in_specs=[_row(tm, 2 * G, _A_BLK)] + _sgu_specs(l), out_specs=_row(tm, G),
        out_shape=jax.ShapeDtypeStruct((t, G), BF16),
        compiler_params=_cparams("parallel"))(p, ln_w, ln_b, ws, bs)


def _sgu_bwd(p, dmix, ln_w, ln_b, ws, bs, l, tm=1024):
    t = p.shape[0]
    tm = min(tm, t)

    def body(uv_ref, dy_ref, lw_ref, lb_ref, ws_ref, bs_ref, duv_ref, dlw_ref, dlb_ref, dws_ref, dbs_ref):
        @pl.when(pl.program_id(0) == 0)
        def _():
            for r in (dlw_ref, dlb_ref, dws_ref, dbs_ref):
                r[...] = jnp.zeros_like(r)
        ws_v = [ws_ref[h] for h in range(NH)]
        for c in range(tm // SGU_C):
            rows = pl.ds(c * SGU_C, SGU_C)
            _, vjp = jax.vjp(_sgu_chunk, uv_ref[rows, :], lw_ref[l:l + 1, :], lb_ref[l:l + 1, :], ws_v, bs_ref[...])
            duv, dlw, dlb, dws, dbs = vjp(dy_ref[rows, :])
            duv_ref[rows, :] = duv.astype(BF16)
            dlw_ref[...] += dlw
            dlb_ref[...] += dlb
            dbs_ref[...] += dbs
            for h in range(NH):
                dws_ref[h] += dws[h]

    return pl.pallas_call(
        body, name="sgu_bwd", grid=(t // tm,),
        in_specs=[_row(tm, 2 * G, _A_BLK), _row(tm, G, 0)] + _sgu_specs(l),
        out_specs=[_row(tm, 2 * G), _acc((1, G)), _acc((1, G)), _acc((NH, SGU_C, SGU_C)), _acc((NH, SGU_C))],
        out_shape=[jax.ShapeDtypeStruct((t, 2 * G), BF16), jax.ShapeDtypeStruct((1, G), F32),
                   jax.ShapeDtypeStruct((1, G), F32), jax.ShapeDtypeStruct((NH, SGU_C, SGU_C), F32),
                   jax.ShapeDtypeStruct((NH, SGU_C), F32)],
        compiler_params=_cparams("arbitrary"))(p, dmix, ln_w, ln_b, ws, bs)


HALO = 8


def _prev_halo(tm, width, col):
    return pl.BlockSpec((HALO, width), lambda i: (jnp.maximum(i * (tm // HALO) - 1, 0), col))


def _next_halo(tm, width, col, t):
    return pl.BlockSpec((HALO, width), lambda i: (jnp.minimum((i + 1) * (tm // HALO), t // HALO - 1), col))


def _with_prev(halo_ref, x_ref):
    halo = jnp.where(pl.program_id(0) == 0, 0.0, halo_ref[...])
    return jnp.concatenate([halo, x_ref[...]], axis=0)


def _with_next(x_ref, halo_ref):
    halo = jnp.where(pl.program_id(0) == pl.num_programs(0) - 1, 0.0, halo_ref[...])
    return jnp.concatenate([x_ref[...], halo], axis=0)


def _delay(ext, j):
    return ext if j == 0 else pltpu.roll(ext, j, axis=0)


def _advance(ext, j):
    return ext if j == 0 else pltpu.roll(ext, ext.shape[0] - j, axis=0)


def _causal_conv(ext, w, tm):
    kw = w.shape[0]
    out = None
    for k in range(kw):
        term = _delay(ext, kw - 1 - k)[HALO:HALO + tm] * w[k:k + 1, :]
        out = term if out is None else out + term
    return out


def _causal_conv_t(ext, w, tm):
    kw = w.shape[0]
    out = None
    for k in range(kw):
        term = _advance(ext, kw - 1 - k)[0:tm] * w[k:k + 1, :]
        out = term if out is None else out + term
    return out


def _conv_wgrad(ext, dy, kw, tm):
    return jnp.concatenate(
        [jnp.sum(_delay(ext, kw - 1 - k)[HALO:HALO + tm] * dy, axis=0, keepdims=True) for k in range(kw)], axis=0)


_B_GB, _B_GC, _B_H = 8, 9, 10
_C_QKV, _D_QKV = 0, 1
_C_Z, _D_Z = 11, 12
_C_AB, _D_GLR = 26, 27


def _sconv_fwd(p, w, l, tm=2048):
    t = p.shape[0]
    tm = min(tm, t)

    def body(gb_ref, gc_ref, h_ref, gc_halo, h_halo, w_ref, y_ref):
        s_ext = _with_prev(gc_halo, gc_ref) * _with_prev(h_halo, h_ref)
        y_ref[...] = (gb_ref[...] * _causal_conv(s_ext, w_ref[...], tm)).astype(BF16)

    return pl.pallas_call(
        body, name="sconv_fwd", grid=(t // tm,),
        in_specs=[_row(tm, G, _B_GB), _row(tm, G, _B_GC), _row(tm, G, _B_H), _prev_halo(tm, G, _B_GC),
                  _prev_halo(tm, G, _B_H), _layer((3, G), l)],
        out_specs=_row(tm, G), out_shape=jax.ShapeDtypeStruct((t, G), BF16),
        compiler_params=_cparams("parallel"))(p, p, p, p, p, w)


def _sconv_bwd(p, dmix, w, l, tm=2048):
    t = p.shape[0]
    tm = min(tm, t)

    def body(gb_ref, gc_ref, h_ref, gc_halo, h_halo, gb_next, dy_ref, dy_next, w_ref, dp_ref, dw_ref):
        @pl.when(pl.program_id(0) == 0)
        def _():
            dw_ref[...] = jnp.zeros_like(dw_ref)
        w_v = w_ref[...]
        s_ext = _with_prev(gc_halo, gc_ref) * _with_prev(h_halo, h_ref)
        dout = dy_ref[...]
        d_gb = dout * _causal_conv(s_ext, w_v, tm)
        dconv_ext = _with_next(dy_ref, dy_next) * _with_next(gb_ref, gb_next)
        ds = _causal_conv_t(dconv_ext, w_v, tm)
        dw_ref[...] += _conv_wgrad(s_ext, dconv_ext[0:tm], 3, tm)
        dp_ref[...] = jnp.concatenate([d_gb, ds * h_ref[...], ds * gc_ref[...]], axis=1).astype(BF16)

    return pl.pallas_call(
        body, name="sconv_bwd", grid=(t // tm,),
        in_specs=[_row(tm, G, _B_GB), _row(tm, G, _B_GC), _row(tm, G, _B_H), _prev_halo(tm, G, _B_GC),
                  _prev_halo(tm, G, _B_H), _next_halo(tm, G, _B_GB, t), _row(tm, G, 1), _next_halo(tm, G, 1, t),
                  _layer((3, G), l)],
        out_specs=[_row(tm, 3 * G), _acc((3, G))],
        out_shape=[jax.ShapeDtypeStruct((t, 3 * G), BF16), jax.ShapeDtypeStruct((3, G), F32)],
        compiler_params=_cparams("arbitrary"))(p, p, p, p, p, p, dmix, dmix, w)


def _qkv_conv_fwd(p, w, l, tm=2048):
    t = p.shape[0]
    tm = min(tm, t)

    def body(x_ref, x_halo, w_ref, y_ref):
        c = _causal_conv(_with_prev(x_halo, x_ref), w_ref[...], tm)
        y_ref[...] = c * _sigmoid(c)

    return pl.pallas_call(
        body, name="qkv_conv_fwd", grid=(t // tm,),
        in_specs=[_row(tm, 3 * G, _C_QKV), _prev_halo(tm, 3 * G, _C_QKV), _layer((4, 3 * G), l)],
        out_specs=_row(tm, 3 * G), out_shape=jax.ShapeDtypeStruct((t, 3 * G), F32),
        compiler_params=_cparams("parallel"))(p, p, w)


def _qkv_conv_bwd(p, dy, w, l, tm=1024):
    t = p.shape[0]
    tm = min(tm, t)

    def body(x_ref, x_halo, x_next, dy_ref, dy_next, w_ref, dx_ref, dw_ref):
        @pl.when(pl.program_id(0) == 0)
        def _():
            dw_ref[...] = jnp.zeros_like(dw_ref)
        w_v = w_ref[...]
        x_all = jnp.concatenate([_with_prev(x_halo, x_ref), jnp.where(
            pl.program_id(0) == pl.num_programs(0) - 1, 0.0, x_next[...])], axis=0)
        c = _causal_conv(x_all, w_v, tm + HALO)
        s = _sigmoid(c)
        dc_ext = _with_next(dy_ref, dy_next) * (s * (1.0 + c * (1.0 - s)))
        dx_ref[...] = _causal_conv_t(dc_ext, w_v, tm).astype(BF16)
        dw_ref[...] += _conv_wgrad(x_all[0:HALO + tm], dc_ext[0:tm], 4, tm)

    return pl.pallas_call(
        body, name="qkv_conv_bwd", grid=(t // tm,),
        in_specs=[_row(tm, 3 * G, _C_QKV), _prev_halo(tm, 3 * G, _C_QKV), _next_halo(tm, 3 * G, _C_QKV, t),
                  _row(tm, 3 * G), _next_halo(tm, 3 * G, 0, t), _layer((4, 3 * G), l)],
        out_specs=[_row(tm, 3 * G), _acc((4, 3 * G))],
        out_shape=[jax.ShapeDtypeStruct((t, 3 * G), BF16), jax.ShapeDtypeStruct((4, 3 * G), F32)],
        compiler_params=_cparams("arbitrary"))(p, p, p, dy, dy, w)


_STATE = pl.BlockSpec((1, NH, HD, HD), lambda i: (i, 0, 0, 0))


def _dn_specs():
    return [_resident((DEPTH, NH)), _resident((DEPTH, NH)), _resident((DEPTH, HD))]


def _dn_fwd(qkv, p, params, l, cps=8, rider=None):
    t = qkv.shape[0]
    tm = DN_C * cps
    nb = cps * NH

    def body(*refs):
        (qkv_ref, z_ref, ab_ref, alog_ref, dt_ref, nw_ref, y_ref, st_ref, inv_ref, state), ride = _split_refs(
            refs, 6, 3, 1, rider)
        _ride_begin(rider, ride, (t // tm,))

        @pl.when(pl.program_id(0) == 0)
        def _():
            state[...] = jnp.zeros_like(state)
        st_ref[0] = state[...]
        y, new, inv = _dn_tile(qkv_ref[:, 0:G], qkv_ref[:, G:2 * G], qkv_ref[:, 2 * G:3 * G], ab_ref[...], z_ref[...],
                               state[...], None, alog_ref[l:l + 1, :], dt_ref[l:l + 1, :], nw_ref[l:l + 1, :])
        y_ref[...] = y.astype(BF16)
        inv_ref[...] = inv
        state[...] = new
        _ride_end(rider, ride, (t // tm,))

    specs, operands = _host(
        rider, [_row(tm, 3 * G), _row(tm, G, _C_Z), _row(tm, LANES, _C_AB)] + _dn_specs(),
        [_row(tm, G), _STATE, pl.BlockSpec((nb, DN_C, DN_C), lambda i: (i, 0, 0))],
        [jax.ShapeDtypeStruct((t, G), BF16), jax.ShapeDtypeStruct((t // tm, NH, HD, HD), F32),
         jax.ShapeDtypeStruct((t // DN_C * NH, DN_C, DN_C), F32)],
        [pltpu.VMEM((NH, HD, HD), F32)], [qkv, p, p, *params])
    return pl.pallas_call(body, name="dn_fwd", grid=(t // tm,), compiler_params=_cparams("arbitrary"), **specs)(*operands)


def _dn_bwd(qkv, p, states, inv, dmix, params, l, cps=8, rider=None):
    t = qkv.shape[0]
    tm = DN_C * cps
    n = t // tm
    nb = cps * NH

    def body(*refs):
        (qkv_ref, z_ref, ab_ref, st_ref, inv_ref, dy_ref, alog_ref, dt_ref, nw_ref,
         dqkv_ref, dz_ref, dab_ref, dalog_ref, ddt_ref, dnw_ref, dstate), ride = _split_refs(refs, 9, 6, 1, rider)
        _ride_begin(rider, ride, (n,))
        dprm_refs = (dalog_ref, ddt_ref, dnw_ref)

        @pl.when(pl.program_id(0) == 0)
        def _():
            dstate[...] = jnp.zeros_like(dstate)
            for r in dprm_refs:
                r[...] = jnp.zeros_like(r)
        inv_v = inv_ref[...]
        tile = lambda q, k, v, ab, z, st, alog, dt, nw: _dn_tile(q, k, v, ab, z, st, inv_v, alog, dt, nw)[:2]
        _, vjp = jax.vjp(tile, qkv_ref[:, 0:G], qkv_ref[:, G:2 * G], qkv_ref[:, 2 * G:3 * G], ab_ref[...], z_ref[...],
                         st_ref[0], alog_ref[l:l + 1, :], dt_ref[l:l + 1, :], nw_ref[l:l + 1, :])
        dq, dk, dv, dab, dz, dst, *dprm = vjp((dy_ref[...], dstate[...]))
        dqkv_ref[...] = jnp.concatenate([dq, dk, dv], axis=1)
        dz_ref[...] = dz.astype(BF16)
        dab_ref[...] = dab.astype(BF16)
        dstate[...] = dst
        for r, g in zip(dprm_refs, dprm):
            r[...] += g
        _ride_end(rider, ride, (n,))

    rev = lambda w, blk: pl.BlockSpec((tm, w), lambda i: (n - 1 - i, blk))
    specs, operands = _host(
        rider, [rev(3 * G, 0), rev(G, _C_Z), rev(LANES, _C_AB),
                pl.BlockSpec((1, NH, HD, HD), lambda i: (n - 1 - i, 0, 0, 0)),
                pl.BlockSpec((nb, DN_C, DN_C), lambda i: (n - 1 - i, 0, 0)), rev(G, 2)] + _dn_specs(),
        [rev(3 * G, 0), rev(G, 0), rev(LANES, 0), _acc((1, NH)), _acc((1, NH)), _acc((1, HD))],
        [jax.ShapeDtypeStruct((t, 3 * G), F32), jax.ShapeDtypeStruct((t, G), BF16),
         jax.ShapeDtypeStruct((t, LANES), BF16), jax.ShapeDtypeStruct((1, NH), F32),
         jax.ShapeDtypeStruct((1, NH), F32), jax.ShapeDtypeStruct((1, HD), F32)],
        [pltpu.VMEM((NH, HD, HD), F32)], [qkv, p, p, states, inv, dmix, *params])
    return pl.pallas_call(body, name="dn_bwd", grid=(n,), compiler_params=_cparams("arbitrary"), **specs)(*operands)


def _gla_specs(l):
    return [_layer((16, G), l), _resident((DEPTH, G)), _resident((DEPTH, HD))]


def _gla_fwd(p, params, l, cps=8, rider=None):
    t = p.shape[0]
    tm = GLA_C * cps

    def body(*refs):
        (qkv_ref, z_ref, glr_ref, w2_ref, gb_ref, nw_ref, y_ref, st_ref, state), ride = _split_refs(refs, 6, 2, 1, rider)
        _ride_begin(rider, ride, (t // tm,))

        @pl.when(pl.program_id(0) == 0)
        def _():
            state[...] = jnp.zeros_like(state)
        st_ref[0] = state[...]
        y, new = _gla_tile(qkv_ref[:, 0:G], qkv_ref[:, G:2 * G], qkv_ref[:, 2 * G:3 * G], glr_ref[...], z_ref[...],
                           state[...], w2_ref[...], gb_ref[l:l + 1, :], nw_ref[l:l + 1, :])
        y_ref[...] = y.astype(BF16)
        state[...] = new
        _ride_end(rider, ride, (t // tm,))

    specs, operands = _host(
        rider, [_row(tm, 3 * G, _D_QKV), _row(tm, G, _D_Z), _row(tm, LANES, _D_GLR)] + _gla_specs(l),
        [_row(tm, G), _STATE],
        [jax.ShapeDtypeStruct((t, G), BF16), jax.ShapeDtypeStruct((t // tm, NH, HD, HD), F32)],
        [pltpu.VMEM((NH, HD, HD), F32)], [p, p, p, *params])
    return pl.pallas_call(body, name="gla_fwd", grid=(t // tm,), compiler_params=_cparams("arbitrary"), **specs)(*operands)


def _gla_bwd(p, states, dmix, params, l, cps=8):
    t = p.shape[0]
    tm = GLA_C * cps
    n = t // tm

    def body(qkv_ref, z_ref, glr_ref, st_ref, dy_ref, w2_ref, gb_ref, nw_ref,
             dqkv_ref, dz_ref, dglr_ref, dw2_ref, dgb_ref, dnw_ref, dstate):
        dprm_refs = (dw2_ref, dgb_ref, dnw_ref)

        @pl.when(pl.program_id(0) == 0)
        def _():
            dstate[...] = jnp.zeros_like(dstate)
            for r in dprm_refs:
                r[...] = jnp.zeros_like(r)
        _, vjp = jax.vjp(_gla_tile, qkv_ref[:, 0:G], qkv_ref[:, G:2 * G], qkv_ref[:, 2 * G:3 * G], glr_ref[...],
                         z_ref[...], st_ref[0], w2_ref[...], gb_ref[l:l + 1, :], nw_ref[l:l + 1, :])
        dq, dk, dv, dglr, dz, dst, *dprm = vjp((dy_ref[...], dstate[...]))
        dqkv_ref[...] = jnp.concatenate([dq, dk, dv], axis=1).astype(BF16)
        dz_ref[...] = dz.astype(BF16)
        dglr_ref[...] = dglr.astype(BF16)
        dstate[...] = dst
        for r, g in zip(dprm_refs, dprm):
            r[...] += g

    rev = lambda w, blk: pl.BlockSpec((tm, w), lambda i: (n - 1 - i, blk))
    return pl.pallas_call(
        body, name="gla_bwd", grid=(n,),
        in_specs=[rev(3 * G, _D_QKV), rev(G, _D_Z), rev(LANES, _D_GLR),
                  pl.BlockSpec((1, NH, HD, HD), lambda i: (n - 1 - i, 0, 0, 0)), rev(G, 3)] + _gla_specs(l),
        out_specs=[rev(3 * G, 0), rev(G, 0), rev(LANES, 0), _acc((16, G)), _acc((1, G)), _acc((1, HD))],
        out_shape=[jax.ShapeDtypeStruct((t, 3 * G), BF16), jax.ShapeDtypeStruct((t, G), BF16),
                   jax.ShapeDtypeStruct((t, LANES), BF16), jax.ShapeDtypeStruct((16, G), F32),
                   jax.ShapeDtypeStruct((1, G), F32), jax.ShapeDtypeStruct((1, HD), F32)],
        scratch_shapes=[pltpu.VMEM((NH, HD, HD), F32)],
        compiler_params=_cparams("arbitrary"))(p, p, p, states, dmix, *params)


def _adamw_math(w, g, m, v):
    m = ADAM_B1 * m + (1.0 - ADAM_B1) * g
    v = ADAM_B2 * v + (1.0 - ADAM_B2) * (g * g)
    m_hat = m / (1.0 - ADAM_B1 ** ADAM_STEP)
    v_hat = v / (1.0 - ADAM_B2 ** ADAM_STEP)
    return -ADAM_LR * (m_hat / (jnp.sqrt(v_hat) + ADAM_EPS) + ADAM_WD * w), m, v


def _adamw_large(parts, w, m, v, name, tr):
    _, r, c = w.shape

    def body(p0_ref, p1_ref, w_ref, m_ref, v_ref, g_ref, d_ref, nm_ref, nv_ref):
        def total(p_ref):
            g = p_ref[0].astype(F32)
            for k in range(1, N_DEV):
                g = g + p_ref[k].astype(F32)
            return g

        g = jnp.where(pl.program_id(0) == 0, total(p0_ref), total(p1_ref))
        g_ref[...] = g
        d_ref[...], nm_ref[...], nv_ref[...] = _adamw_math(w_ref[...], g, m_ref[...], v_ref[...])

    blk = pl.BlockSpec((None, tr, c), lambda l, i: (l, i, 0))
    part = pl.BlockSpec((N_DEV, tr, c), lambda l, i: (0, i, 0))
    return pl.pallas_call(
        body, name=name, grid=(DEPTH, r // tr), in_specs=[part, part, blk, blk, blk],
        out_specs=[blk] * 4, out_shape=[jax.ShapeDtypeStruct(w.shape, F32)] * 4,
        compiler_params=_cparams("parallel", "parallel"))(*parts, w, m, v)


def _adamw_w_in(parts, w, m, v, tc=512):
    def body(p0_ref, p1_ref, w_ref, m_ref, v_ref, g_ref, d_ref, nm_ref, nv_ref):
        for l, p_ref in enumerate((p0_ref, p1_ref)):
            g = p_ref[0, 0:IN_SHARD, :].astype(F32)
            for k in range(1, N_DEV):
                g = g + p_ref[k, 0:IN_SHARD, :].astype(F32)
            g_ref[:, l, :] = g
            d_ref[:, l, :], nm_ref[:, l, :], nv_ref[:, l, :] = _adamw_math(w_ref[:, l, :], g, m_ref[:, l, :], v_ref[:, l, :])

    blk = pl.BlockSpec((IN_SHARD, DEPTH, tc), lambda i: (0, 0, i))
    part = pl.BlockSpec((N_DEV, IN_ROWS, tc), lambda i: (0, 0, i))
    return pl.pallas_call(
        body, name="adamw_w_in", grid=(D // tc,), in_specs=[part, part, blk, blk, blk], out_specs=[blk] * 4,
        out_shape=[jax.ShapeDtypeStruct(w.shape, F32)] * 4, compiler_params=_cparams("parallel"))(*parts, w, m, v)


_SLAB_AT = {
    "sgu_w_spatial": (0, 0, SGU_C, LANES),
    "sgu_b_spatial": (128, 0, NH, SGU_C),
    "norm1_w": (132, 0, 1, D),
    "norm2_w": (133, 0, 1, D),
    "sgu_ln_w": (134, 0, 1, G),
    "sgu_ln_b": (134, 256, 1, G),
    "gla_gate_bias": (134, 512, 1, G),
    "dn_norm_w": (134, 768, 1, HD),
    "gla_norm_w": (134, 896, 1, HD),
    "dn_a_log": (135, 0, 1, NH),
    "dn_dt_bias": (135, 128, 1, NH),
    "gla_w_gate2": (136, 0, 16, G),
    "dn_conv_w": (136, 256, 4, 3 * G),
    "sc_conv_w": (140, 256, 3, G),
}
_LAYER_ROWS = 152
_FINAL_ROW = DEPTH * _LAYER_ROWS
_SLAB_ROWS, _SLAB_COLS = _FINAL_ROW + 8, 1024
_SLAB_ORDER = tuple(_SLAB_AT)
_SHARDED_SMALL = ("sc_conv_w", "dn_conv_w", "gla_w_gate2")
_REPLICATED = tuple(k for k in _SLAB_ORDER if k not in _SHARDED_SMALL)


def _region(name, l, h=0):
    r0, c0, nr, nc = _SLAB_AT[name]
    return slice(_LAYER_ROWS * l + r0, _LAYER_ROWS * l + r0 + nr), slice(c0 + nc * h, c0 + nc * (h + 1))


def _pack_small(layer_grads, d_final):
    def body(*refs):
        slab = refs[-1]
        slab[...] = jnp.zeros_like(slab)
        it = iter(refs[:-1])
        for l in range(DEPTH):
            for name in _SLAB_ORDER:
                ref = next(it)
                if name == "sgu_w_spatial":
                    for h in range(NH):
                        slab[_region(name, l, h)] = ref[h]
                else:
                    slab[_region(name, l)] = ref[...]
        slab[_FINAL_ROW:_FINAL_ROW + 1, :] = next(it)[...]

    flat = [layer_grads[l][name] for l in range(DEPTH) for name in _SLAB_ORDER] + [d_final]
    return pl.pallas_call(body, name="pack_small_grads", out_shape=jax.ShapeDtypeStruct((_SLAB_ROWS, _SLAB_COLS), F32),
                          compiler_params=_cparams())(*flat)


def _adamw_small(parts, w, m, v):
    names = _REPLICATED + ("final_norm_w",)
    n_in = len(names)

    def body(*refs):
        def total(rows, cols):
            g = refs[0][0, rows, cols]
            for k in range(1, N_DEV):
                g = g + refs[0][k, rows, cols]
            return g

        w_refs = dict(zip(names, refs[1:1 + n_in]))
        m_refs = dict(zip(names, refs[1 + n_in:1 + 2 * n_in]))
        v_refs = dict(zip(names, refs[1 + 2 * n_in:1 + 3 * n_in]))
        outs = refs[1 + 3 * n_in:]
        out_refs = [dict(zip(names, outs[j * n_in:(j + 1) * n_in])) for j in range(4)]
        full_refs = dict(zip(_SHARDED_SMALL, outs[4 * n_in:]))

        def update(name, idx, g):
            res = (g,) + _adamw_math(w_refs[name][idx], g, m_refs[name][idx], v_refs[name][idx])
            for o, val in zip(out_refs, res):
                o[name][idx] = val

        for l in range(DEPTH):
            for name in _REPLICATED:
                if name == "sgu_w_spatial":
                    for h in range(NH):
                        update(name, (l, h), total(*_region(name, l, h)))
                elif name == "sgu_b_spatial":
                    update(name, (l,), total(*_region(name, l)))
                else:
                    update(name, (slice(l, l + 1), slice(None)), total(*_region(name, l)))
            for name in _SHARDED_SMALL:
                full_refs[name][l] = total(*_region(name, l))
        update("final_norm_w", (slice(None), slice(None)), total(slice(_FINAL_ROW, _FINAL_ROW + 1), slice(None)))

    shapes = [jax.ShapeDtypeStruct(w[k].shape, F32) for k in names]
    full_shapes = [jax.ShapeDtypeStruct((DEPTH,) + _SLAB_AT[k][2:], F32) for k in _SHARDED_SMALL]
    outs = pl.pallas_call(body, name="adamw_small", out_shape=shapes * 4 + full_shapes, compiler_params=_cparams())(
        parts, *[w[k] for k in names], *[m[k] for k in names], *[v[k] for k in names])
    result = [dict(zip(names, outs[j * n_in:(j + 1) * n_in])) for j in range(4)]
    return result, dict(zip(_SHARDED_SMALL, outs[4 * n_in:]))


def _adamw_shards(g, w, m, v):
    names = _SHARDED_SMALL
    n = len(names)

    def body(*refs):
        for j in range(n):
            g_v = refs[j][...]
            res = _adamw_math(refs[n + j][...], g_v, refs[2 * n + j][...], refs[3 * n + j][...])
            for o, val in zip(refs[4 * n + j::n], res):
                o[...] = val

    shapes = [jax.ShapeDtypeStruct(w[k].shape, F32) for k in names]
    outs = pl.pallas_call(body, name="adamw_small_shards", out_shape=shapes * 3, compiler_params=_cparams())(
        *[g[k] for k in names], *[w[k] for k in names], *[m[k] for k in names], *[v[k] for k in names])
    return [dict(zip(names, outs[j * n:(j + 1) * n])) for j in range(3)]


_ANY = pl.BlockSpec(memory_space=pl.ANY)


def _place():
    return lax.axis_index("x"), lax.axis_index("y"), lax.axis_index("c")


def _sems(n):
    return [pltpu.SemaphoreType.DMA((n, 7)), pltpu.SemaphoreType.DMA((n, 7)), pltpu.SemaphoreType.DMA((n,))]


class _Gather:
    def __init__(self, blocks, second, forward_at=1.0):
        self.inputs, self.second, self.forward_at = list(blocks), list(second), forward_at
        self.out_shape = [jax.ShapeDtypeStruct((a.shape[0], N_DEV) + a.shape[1:] if s else (N_DEV,) + a.shape, a.dtype)
                          for a, s in zip(blocks, second)]
        self.scratch = _sems(len(blocks))

    def _copies(self, refs):
        x_refs, out_refs, (send_sems, recv_sems, local_sems) = refs
        n = len(x_refs)
        x, y, c = _place()
        me, sibling = (x, y, c), (x, y, 1 - c)
        chips = [(1 - x, y), (x, 1 - y), (1 - x, 1 - y)]

        def slot(j, px, py, pc):
            idx = 4 * px + 2 * py + pc
            return out_refs[j].at[:, idx] if self.second[j] else out_refs[j].at[idx]

        def copies(k, block, to, own=False):
            return [pltpu.make_async_remote_copy(
                src_ref=x_refs[j] if own else slot(j, *block), dst_ref=slot(j, *block),
                send_sem=send_sems.at[j, k], recv_sem=recv_sems.at[j, k], device_id=to,
                device_id_type=pl.DeviceIdType.MESH) for j in range(n)]

        mine = [pltpu.make_async_copy(x_refs[j], slot(j, *me), local_sems.at[j]) for j in range(n)]
        first = copies(0, me, sibling, own=True)
        for j, chip in enumerate(chips):
            first += copies(1 + j, me, (*chip, c), own=True)
        landed = [copies(1 + j, (*chip, c), me) for j, chip in enumerate(chips)]
        onward = [copies(4 + j, (*chip, c), sibling) for j, chip in enumerate(chips)]
        from_sibling = copies(0, sibling, me)
        for j, chip in enumerate(chips):
            from_sibling += copies(4 + j, (*chip, 1 - c), me)
        return mine, first, landed, onward, from_sibling

    def start(self, refs):
        mine, first, _, _, _ = self._copies(refs)
        for cp in mine + first:
            cp.start()

    def forward(self, refs):
        _, _, landed, onward, _ = self._copies(refs)
        for arrived, passed in zip(landed, onward):
            for cp in arrived:
                cp.wait_recv()
            for cp in passed:
                cp.start()

    def finish(self, refs):
        mine, first, _, onward, from_sibling = self._copies(refs)
        for cp in from_sibling:
            cp.wait_recv()
        for cp in first + [cp for passed in onward for cp in passed]:
            cp.wait_send()
        for cp in mine:
            cp.wait()


class _Exchange:
    forward_at = 1.0

    def __init__(self, sends):
        self.inputs = list(sends)
        self.out_shape = [jax.ShapeDtypeStruct(a.shape, a.dtype) for a in sends]
        self.scratch = _sems(len(sends))

    def _copies(self, refs):
        x_refs, out_refs, (send_sems, recv_sems, local_sems) = refs
        n = len(x_refs)
        x, y, c = _place()
        me = 4 * x + 2 * y + c
        sent, landing = [], []
        for k in range(1, N_DEV):
            px, py, pc = x ^ ((k >> 2) & 1), y ^ ((k >> 1) & 1), c ^ (k & 1)
            them = 4 * px + 2 * py + pc
            for j in range(n):
                for here, there, into in ((them, me, sent), (me, them, landing)):
                    into.append(pltpu.make_async_remote_copy(
                        src_ref=x_refs[j].at[here], dst_ref=out_refs[j].at[there], send_sem=send_sems.at[j, k - 1],
                        recv_sem=recv_sems.at[j, k - 1], device_id=(px, py, pc), device_id_type=pl.DeviceIdType.MESH))
        mine = [pltpu.make_async_copy(x_refs[j].at[me], out_refs[j].at[me], local_sems.at[j]) for j in range(n)]
        return mine, sent, landing

    def start(self, refs):
        mine, sent, _ = self._copies(refs)
        for cp in mine + sent:
            cp.start()

    def forward(self, refs):
        pass

    def finish(self, refs):
        mine, sent, landing = self._copies(refs)
        for cp in landing:
            cp.wait_recv()
        for cp in sent:
            cp.wait_send()
        for cp in mine:
            cp.wait()


def _run(rider, name):
    n_in, n_out = len(rider.inputs), len(rider.out_shape)

    def body(*refs):
        parts = (refs[:n_in], refs[n_in:n_in + n_out], refs[n_in + n_out:])
        rider.start(parts)
        rider.forward(parts)
        rider.finish(parts)

    return pl.pallas_call(body, name=name, out_shape=rider.out_shape, in_specs=[_ANY] * n_in, out_specs=[_ANY] * n_out,
                          scratch_shapes=rider.scratch)(*rider.inputs)


def _split_refs(refs, n_in, n_out, n_scratch, rider):
    r_in = len(rider.inputs) if rider else 0
    r_out = len(rider.out_shape) if rider else 0
    a = n_in + r_in
    b = a + n_out + r_out
    own = refs[:n_in] + refs[a:a + n_out] + refs[b:b + n_scratch]
    return own, (refs[n_in:a], refs[a + n_out:b], refs[b + n_scratch:])


def _grid_step(grid):
    step, stride = 0, 1
    for axis in reversed(range(len(grid))):
        step = step + pl.program_id(axis) * stride
        stride *= grid[axis]
    return step


def _ride_begin(rider, ride_refs, grid):
    if rider is not None:
        pl.when(_grid_step(grid) == 0)(lambda: rider.start(ride_refs))


def _ride_end(rider, ride_refs, grid):
    if rider is not None:
        steps = 1
        for n in grid:
            steps *= n
        step = _grid_step(grid)
        pl.when(step == min(steps - 1, int(steps * rider.forward_at)))(lambda: rider.forward(ride_refs))
        pl.when(step == steps - 1)(lambda: rider.finish(ride_refs))


def _host(rider, in_specs, out_specs, out_shape, scratch, operands):
    if rider is None:
        return dict(in_specs=in_specs, out_specs=out_specs, out_shape=out_shape, scratch_shapes=scratch), operands
    return dict(in_specs=in_specs + [_ANY] * len(rider.inputs), out_specs=out_specs + [_ANY] * len(rider.out_shape),
                out_shape=out_shape + rider.out_shape, scratch_shapes=scratch + rider.scratch), operands + rider.inputs


_LATE = ("w_gate_up", "w_out", "w_down")


def _local_grads(x, target, w, late=None, exchange=False):
    w = dict(w)
    w_in = list(w["w_in"])
    dn_params = (w["dn_a_log"], w["dn_dt_bias"], w["dn_norm_w"])
    gla_params = (w["gla_w_gate2"], w["gla_gate_bias"], w["gla_norm_w"])
    sgu_params = (w["sgu_ln_w"], w["sgu_ln_b"], w["sgu_w_spatial"], w["sgu_b_spatial"])
    saved = []
    for l in range(DEPTH):
        riding = l == 0 and late is not None
        h, p, *got = _in_proj(x, w["norm1_w"], w_in[l], l,
                              rider=_Gather([late["w_down"]], [True], forward_at=0.85) if riding else None)
        if riding:
            w["w_down"] = got[0].reshape(DEPTH, FF, D)
        ya = _sgu_fwd(p, *sgu_params, l)
        yb = _sconv_fwd(p, w["sc_conv_w"], l)
        qkv_c = _qkv_conv_fwd(p, w["dn_conv_w"], l)
        yc, st_c, inv_c, *got = _dn_fwd(
            qkv_c, p, dn_params, l,
            rider=_Gather([late["w_gate_up"], late["w_out"]], [False, True], forward_at=0.85) if riding else None)
        if riding:
            w.update(w_gate_up=got[0], w_out=got[1].reshape(DEPTH, D, D))
        yd, st_d, *got = _gla_fwd(p, gla_params, l,
                                  rider=_Gather([late["w_in"]], [False], forward_at=0.7) if riding else None)
        if riding:
            w_in[1] = _relayout_w_in(got[0])
        mix, x1, h2, gu, act, x2 = _out_mlp(x, (ya, yb, yc, yd), w["w_out"], w["norm2_w"], w["w_gate_up"], w["w_down"], l)
        saved.append(dict(x=x, h=h, p=p, qkv_c=qkv_c, st_c=st_c, inv_c=inv_c, st_d=st_d, mix=mix, x1=x1, h2=h2, gu=gu,
                          act=act))
        x = x2
    loss, d_final, dx = _loss_head(x, w["final_norm_w"], target)
    large = {k: [None] * DEPTH for k in ("w_in", "w_out", "w_gate_up", "w_down")}
    small = [None] * DEPTH
    for l in reversed(range(DEPTH)):
        s = saved[l]
        p = s["p"]
        g = {}
        riding = exchange and l == 0

        def exchanging(pairs):
            return _Exchange([large[k][j] for k, j in pairs]) if riding else None

        def arrive(pairs, arrived):
            for (k, j), a in zip(pairs, arrived):
                large[k][j] = a

        pairs = (("w_gate_up", 1), ("w_down", 1))
        dgu, dx1, dmix, g["norm2_w"], *arrived = _mlp_out_bwd(
            dx, s["x1"], s["gu"], w["norm2_w"], w["w_down"], w["w_gate_up"], w["w_out"], l, rider=exchanging(pairs))
        arrive(pairs, arrived)
        pairs = (("w_in", 1), ("w_out", 1))
        d_gu, *arrived = _wgrad_rows(dgu, s["h2"], "wgrad_gate_up", rider=exchanging(pairs)) if riding else (
            _wgrad_rows(dgu, s["h2"], "wgrad_gate_up"),)
        arrive(pairs, arrived)
        large["w_gate_up"][l] = d_gu.reshape(N_DEV, GU_SHARD, D)
        large["w_down"][l] = _wgrad_rows(s["act"], dx, "wgrad_down").reshape(N_DEV, FF // N_DEV, D)
        large["w_out"][l] = _wgrad_rows(s["mix"], dx1, "wgrad_out", rows=D).reshape(N_DEV, D // N_DEV, D)
        d_a, g["sgu_ln_w"], g["sgu_ln_b"], g["sgu_w_spatial"], g["sgu_b_spatial"] = _sgu_bwd(p, dmix, *sgu_params, l)
        d_b, g["sc_conv_w"] = _sconv_bwd(p, dmix, w["sc_conv_w"], l)
        pairs = tuple((k, 0) for k in _LATE)
        dqkv_c, dz_c, dab, g["dn_a_log"], g["dn_dt_bias"], g["dn_norm_w"], *arrived = _dn_bwd(
            s["qkv_c"], p, s["st_c"], s["inv_c"], dmix, dn_params, l, rider=exchanging(pairs))
        arrive(pairs, arrived)
        d_c, g["dn_conv_w"] = _qkv_conv_bwd(p, dqkv_c, w["dn_conv_w"], l)
        d_d, dz_d, dglr, g["gla_w_gate2"], g["gla_gate_bias"], g["gla_norm_w"] = _gla_bwd(p, s["st_d"], dmix, gla_params, l)
        dp, dx, g["norm1_w"] = _in_proj_bwd((d_c, d_d, d_a, d_b, dz_c, dz_d, dab, dglr), s["x"], dx1, w["norm1_w"],
                                            w_in[l], l)
        small[l] = g
        w_in_rows = dict(rows=P // 2, out_dtype=F32)
        if riding:
            d_w_in, small = _wgrad_rows(dp, s["h"], "wgrad_in", **w_in_rows,
                                        rider=_Gather([_pack_small(small, d_final)], [False], forward_at=0.75))
        else:
            d_w_in = _wgrad_rows(dp, s["h"], "wgrad_in", **w_in_rows)
        large["w_in"][l] = _scatter_w_in_grad(d_w_in)
    return loss[0, 0], dx, large, small, d_final


_ORDER = ("norm1_w", "w_in", "sgu_ln_w", "sgu_ln_b", "sgu_w_spatial", "sgu_b_spatial", "sc_conv_w", "dn_conv_w",
          "dn_a_log", "dn_dt_bias", "dn_norm_w", "gla_w_gate2", "gla_gate_bias", "gla_norm_w", "w_out", "norm2_w",
          "w_gate_up", "w_down", "final_norm_w")
_LARGE = ("w_in", "w_gate_up", "w_out", "w_down")
_LARGE_TILE = {"w_gate_up": 352, "w_out": 128, "w_down": 352}


def _gather_cols(g):
    return jnp.transpose(g, (1, 2, 0, 3)).reshape(g.shape[1], g.shape[2], N_DEV * g.shape[3])


def kernel(x, norm1_w, w_in, sgu_ln_w, sgu_ln_b, sgu_w_spatial, sgu_b_spatial, sc_conv_w, dn_conv_w, dn_a_log, dn_dt_bias, dn_norm_w, gla_w_gate2, gla_gate_bias, gla_norm_w, w_out, norm2_w, w_gate_up, w_down, final_norm_w, loss_target, m_norm1_w, m_w_in, m_sgu_ln_w, m_sgu_ln_b, m_sgu_w_spatial, m_sgu_b_spatial, m_sc_conv_w, m_dn_conv_w, m_dn_a_log, m_dn_dt_bias, m_dn_norm_w, m_gla_w_gate2, m_gla_gate_bias, m_gla_norm_w, m_w_out, m_norm2_w, m_w_gate_up, m_w_down, m_final_norm_w, v_norm1_w, v_w_in, v_sgu_ln_w, v_sgu_ln_b, v_sgu_w_spatial, v_sgu_b_spatial, v_sc_conv_w, v_dn_conv_w, v_dn_a_log, v_dn_dt_bias, v_dn_norm_w, v_gla_w_gate2, v_gla_gate_bias, v_gla_norm_w, v_w_out, v_norm2_w, v_w_gate_up, v_w_down, v_final_norm_w):
    args = dict(locals())
    wts = {k: args[k] for k in _ORDER}
    mom = {k: args["m_" + k] for k in _ORDER}
    var = {k: args["v_" + k] for k in _ORDER}
    for d in (wts, mom, var):
        d["final_norm_w"] = d["final_norm_w"][None]
    me = 4 * lax.axis_index("x") + 2 * lax.axis_index("y") + lax.axis_index("c")
    for d in (wts, mom, var):
        d["w_gate_up"] = jnp.swapaxes(d["w_gate_up"], 1, 2)

    late = {k: wts[k].astype(BF16) for k in _LATE}
    w_in = wts["w_in"].astype(BF16)
    late["w_in"] = w_in[1]
    got = _run(_Gather([w_in[0]] + [wts[k] for k in _SHARDED_SMALL], [False] * 4), "gather_w_in")
    full = dict(wts)
    full["w_in"] = [_relayout_w_in(got[0]), None]
    for k, g in zip(_SHARDED_SMALL, got[1:]):
        full[k] = _gather_cols(g)

    loss, dx, large, small, d_final = _local_grads(x[0], loss_target[0], full, late=late, exchange=True)
    loss = lax.psum(loss, ("x", "y", "c"))

    large["w_in"][0], = _run(_Exchange([large["w_in"][0]]), "exchange_grads")
    result = {}
    for k in ("w_gate_up", "w_out", "w_down"):
        outs = _adamw_large(large[k], wts[k], mom[k], var[k], "adamw_" + k, _LARGE_TILE[k])
        for kind, a in zip(("grad", "delta", "new_m", "new_v"), outs):
            result[kind, k] = jnp.swapaxes(a, 1, 2) if k == "w_gate_up" else a
    outs = _adamw_w_in(large["w_in"], *[jnp.transpose(d["w_in"], (2, 0, 1)) for d in (wts, mom, var)])
    for kind, a in zip(("grad", "delta", "new_m", "new_v"), outs):
        result[kind, "w_in"] = jnp.transpose(a, (1, 2, 0))

    slabs = small
    rep = _REPLICATED + ("final_norm_w",)
    outs, summed = _adamw_small(slabs, {k: wts[k] for k in rep}, {k: mom[k] for k in rep}, {k: var[k] for k in rep})
    for kind, d in zip(("grad", "delta", "new_m", "new_v"), outs):
        for k, a in d.items():
            result[kind, k] = a[0] if k == "final_norm_w" else a
    own = {k: lax.dynamic_slice_in_dim(summed[k], me * wts[k].shape[-1], wts[k].shape[-1], axis=2) for k in _SHARDED_SMALL}
    outs = _adamw_shards(own, {k: wts[k] for k in _SHARDED_SMALL}, {k: mom[k] for k in _SHARDED_SMALL},
                         {k: var[k] for k in _SHARDED_SMALL})
    for kind, d in zip(("grad", "delta", "new_m", "new_v"), [own] + outs):
        for k, a in d.items():
            result[kind, k] = a

    return (loss, dx[None], *[result[kind, k] for kind in ("grad", "delta", "new_m", "new_v") for k in _ORDER])
```

```python
import functools

import jax
import jax.numpy as jnp
from jax import lax
from jax.experimental import pallas as pl
from jax.experimental.pallas import tpu as pltpu

F32, BF16 = jnp.float32, jnp.bfloat16
DEPTH = 2
D = 1024
G = 256
NH, HD = 4, 64
FF = 2816
IN_COLS = 3352
P = 3584
EPS = 1e-6
SGU_C, DN_C, GLA_C = 128, 64, 64
N_DEV = 8
IN_SHARD = IN_COLS // N_DEV
GU_SHARD = 2 * FF // N_DEV
LANES = 128
VMEM_LIMIT = 56 * 2 ** 20

_PAD_SEGS = ((1280, 2048, 0),
             (2312, 3080, 0),
             (0, 512, 0),
             (512, 1280, 0),
             (2056, 2312, 0),
             (3096, 3352, 0),
             (2048, 2056, 120),
             (3080, 3096, 112))

ADAM_LR, ADAM_B1, ADAM_B2, ADAM_EPS, ADAM_WD, ADAM_STEP = 0.001, 0.9, 0.999, 1e-08, 0.01, 10


def _cparams(*sem):
    return pltpu.CompilerParams(dimension_semantics=sem, vmem_limit_bytes=VMEM_LIMIT)


def _resident(shape):
    return pl.BlockSpec(shape, lambda *_: (0,) * len(shape), pipeline_mode=pl.Buffered(1))


def _layer(shape, l):
    return pl.BlockSpec((None,) + tuple(shape), lambda *_: (l,) + (0,) * len(shape), pipeline_mode=pl.Buffered(1))


def _acc(shape):
    return pl.BlockSpec(shape, lambda *_: (0,) * len(shape))


def _dg(a, b, ca, cb):
    lead = a.ndim - 2
    batch = ((0,), (0,)) if lead else ((), ())
    return lax.dot_general(a, b, (((ca + lead,), (cb + lead,)), batch), preferred_element_type=F32)


def _split(t):
    hi = t.astype(BF16)
    return hi, (t - hi.astype(F32)).astype(BF16)


def _dg3(a, b, ca, cb):
    (ah, al), (bh, bl) = a, b
    return _dg(ah, bh, ca, cb) + (_dg(ah, bl, ca, cb) + _dg(al, bh, ca, cb))


def _make_dot(accurate):
    def raw(a, b, form):
        ca = 0 if form == "tn" else 1
        cb = 1 if form == "nt" else 0
        if accurate:
            return _dg3(_split(a), _split(b), ca, cb)
        return _dg(a.astype(BF16), b.astype(BF16), ca, cb)

    @functools.partial(jax.custom_vjp, nondiff_argnums=(2,))
    def dot(a, b, form):
        return raw(a, b, form)

    def fwd(a, b, form):
        return raw(a, b, form), (a, b)

    def bwd(form, res, g):
        a, b = res
        if form == "nn":
            return raw(g, b, "nt"), raw(a, g, "tn")
        if form == "nt":
            return raw(g, b, "nn"), raw(g, a, "tn")
        return raw(b, g, "nt"), raw(a, g, "nn")

    dot.defvjp(fwd, bwd)
    return dot


_bdot = _make_dot(False)
_hdot = _make_dot(True)


def _inv_unit_lower(low):
    n = low.shape[-1]
    eye = (lax.broadcasted_iota(jnp.int32, (n, n), 0) == lax.broadcasted_iota(jnp.int32, (n, n), 1)).astype(F32)
    p = -low
    inv = eye + p
    ps = _split(p)
    k = 1
    while 2 * k < n:
        ps = _split(_dg3(ps, ps, 1, 0))
        inv = inv + _dg3(_split(inv), ps, 1, 0)
        k *= 2
    return inv


@jax.custom_vjp
def _unit_lower_solve(low, rhs, inv):
    return _dg3(_split(inv), _split(rhs), 1, 0)


def _uls_fwd(low, rhs, inv):
    sol = _dg3(_split(inv), _split(rhs), 1, 0)
    return sol, (inv, sol)


def _uls_bwd(res, g):
    inv, sol = res
    d_rhs = _dg3(_split(inv), _split(g), 0, 0)
    return -_dg3(_split(d_rhs), _split(sol), 1, 1), d_rhs, jnp.zeros_like(inv)


_unit_lower_solve.defvjp(_uls_fwd, _uls_bwd)


def _sigmoid(x):
    return 0.5 * jnp.tanh(0.5 * x) + 0.5


def _softplus(x):
    return jnp.maximum(x, 0.0) + jnp.log(1.0 + jnp.exp(-jnp.maximum(x, -x)))


def _gelu(x):
    return 0.5 * x * (1.0 + jnp.tanh(0.7978845608028654 * (x + 0.044715 * (x * x * x))))


def _tri(n):
    ri = lax.broadcasted_iota(jnp.int32, (n, n), 0)
    ci = lax.broadcasted_iota(jnp.int32, (n, n), 1)
    return ri, ci


def _stack(ts):
    return jnp.concatenate([t[None] for t in ts], axis=0)


@jax.custom_vjp
def _head_sums(x):
    ri, ci = _tri(x.shape[-1])
    shift = HD.bit_length() - 1
    ones = (jnp.right_shift(ri, shift) == jnp.right_shift(ci, shift)).astype(BF16)
    hi, lo = _split(x)
    return _dg(hi, ones, 1, 0) + _dg(lo, ones, 1, 0)


_head_sums.defvjp(lambda x: (_head_sums(x), None), lambda _, g: (_head_sums(g),))


def _chunk_mask_dot(x, c, running, transpose):
    ri, ci = _tri(x.shape[0])
    shift = c.bit_length() - 1
    mask = jnp.right_shift(ri, shift) == jnp.right_shift(ci, shift)
    if running:
        mask = mask & (ri >= ci)
    hi, lo = _split(x)
    m = mask.astype(BF16)
    ca = 0 if transpose else 1
    return _dg(m, hi, ca, 0) + _dg(m, lo, ca, 0)


@functools.partial(jax.custom_vjp, nondiff_argnums=(1, 2))
def _chunk_sums(x, c, running):
    return _chunk_mask_dot(x, c, running, False)


_chunk_sums.defvjp(lambda x, c, running: (_chunk_mask_dot(x, c, running, False), None),
                   lambda c, running, _, g: (_chunk_mask_dot(g, c, running, True),))


def _spread_onto(x, first, transpose):
    ri = lax.broadcasted_iota(jnp.int32, (LANES, G), 0)
    ci = lax.broadcasted_iota(jnp.int32, (LANES, G), 1)
    sel = (ri == first + jnp.right_shift(ci, HD.bit_length() - 1)).astype(BF16)
    hi, lo = _split(x)
    cb = 1 if transpose else 0
    return _dg(hi, sel, 1, cb) + _dg(lo, sel, 1, cb)


@functools.partial(jax.custom_vjp, nondiff_argnums=(1,))
def _spread(x, first):
    return _spread_onto(x, first, False)


_spread.defvjp(lambda x, first: (_spread_onto(x, first, False), None),
               lambda first, _, g: (_spread_onto(g, first, True),))


def _head_columns_dot(x, transpose):
    sel = (lax.broadcasted_iota(jnp.int32, (NH, G), 0)
           == jnp.right_shift(lax.broadcasted_iota(jnp.int32, (NH, G), 1), HD.bit_length() - 1)).astype(BF16)
    hi, lo = _split(x)
    if transpose:
        return _dg(sel, hi, 1, 1) + _dg(sel, lo, 1, 1)
    return _dg(hi, sel, 0, 0) + _dg(lo, sel, 0, 0)


@jax.custom_vjp
def _head_columns(x):
    return _head_columns_dot(x, False)


_head_columns.defvjp(lambda x: (_head_columns_dot(x, False), None), lambda _, g: (_head_columns_dot(g, True),))


def _sgu_chunk(uv, ln_w, ln_b, ws, bs):
    u = _gelu(uv[:, :G])
    v = _gelu(uv[:, G:])
    mu = jnp.mean(v, axis=-1, keepdims=True)
    vc = v - mu
    var = jnp.mean(vc * vc, axis=-1, keepdims=True)
    vn = vc * lax.rsqrt(var + EPS) * ln_w + ln_b
    ri, ci = _tri(SGU_C)
    mixed = [_bdot(jnp.where(ri >= ci, ws[h], 0.0), vn[:, HD * h:HD * (h + 1)], "nn") for h in range(NH)]
    return u * (jnp.concatenate(mixed, axis=1) + _head_columns(bs))


def _dn_tile(q, k, v, ab, z, state, inv, a_log, dt_bias, nw):
    c = DN_C
    tm = q.shape[0]
    cps = tm // c
    ri, ci = _tri(tm)
    shift = c.bit_length() - 1
    same = jnp.right_shift(ri, shift) == jnp.right_shift(ci, shift)
    g4 = -jnp.exp(a_log) * _softplus(ab[:, 0:NH] + dt_bias)
    gates = jnp.concatenate([g4, _sigmoid(ab[:, NH:2 * NH]), jnp.zeros((tm, LANES - 2 * NH), F32)], axis=1)
    g, beta = _spread(gates, 0), _spread(gates, NH)
    gc = _chunk_sums(g, c, True)
    gl = _chunk_sums(g, c, False)
    gr4 = _hdot(g4, (same & (ri <= ci)).astype(F32), "tn")
    pairs = [(slice(c * j, c * (j + 1)), h) for j in range(cps) for h in range(NH)]
    heads = lambda t: _stack([t[rows, HD * h:HD * (h + 1)] for rows, h in pairs])
    qn = q * lax.rsqrt(_head_sums(q * q) + EPS) * (HD ** -0.5)
    kn = k * lax.rsqrt(_head_sums(k * k) + EPS)
    eg = jnp.exp(gc)
    kb = kn * beta
    rhs = jnp.concatenate([heads(v * beta), heads(kb * eg)], axis=2)
    qg, kd = heads(qn * eg), heads(kn * jnp.exp(gl - gc))
    cd = _stack([jnp.exp(gl[c * j:c * j + 1, HD * h:HD * (h + 1)]) for j in range(cps) for h in range(NH)])
    qn, kn, kb, gc = heads(qn), heads(kn), heads(kb), heads(gc)
    gr = _stack([gr4[h:h + 1, rows] for rows, h in pairs])
    r2, c2 = _tri(c)
    decay = jnp.exp(jnp.where(r2 >= c2, gc - gr, -jnp.inf))
    low = jnp.where(r2 > c2, _bdot(kb, kn, "nt") * decay, 0.0)
    if inv is None:
        inv = _inv_unit_lower(low)
    sol = _unit_lower_solve(low, rhs, inv)
    u, w = sol[:, :, :HD], sol[:, :, HD:]
    attn = _bdot(qn, kn, "nt") * decay
    ys = []
    for j in range(cps):
        b = slice(NH * j, NH * (j + 1))
        v_new = u[b] - _bdot(w[b], state, "nn")
        o = _bdot(qg[b], state, "nn") + _bdot(attn[b], v_new, "nn")
        state = state * cd[b] + _bdot(kd[b], v_new, "tn")
        on = o * lax.rsqrt(jnp.mean(o * o, axis=-1, keepdims=True) + EPS) * nw
        ys.append(jnp.concatenate([on[h] for h in range(NH)], axis=1))
    return jnp.concatenate(ys, axis=0) * (z * _sigmoid(z)), state, inv


def _gla_tile(q, k, v, glr, z, state_t, w2, gate_bias, nw):
    c = GLA_C
    tm = q.shape[0]
    cps = tm // c
    w2_rows = jnp.concatenate([w2, jnp.zeros((LANES - w2.shape[0], G), F32)], axis=0)
    pre = _bdot(glr, w2_rows, "nn") + gate_bias
    log_a = (jnp.minimum(pre, 0.0) - jnp.log(1.0 + jnp.exp(-jnp.maximum(pre, -pre)))) * (1.0 / 16.0)
    gcum = _chunk_sums(log_a, c, True)
    row = lax.broadcasted_iota(jnp.int32, (c, G), 0)
    qs = q * (HD ** -0.5)
    qa, ka, qg, kl, dec = [], [], [], [], []
    for j in range(cps):
        rows = slice(c * j, c * (j + 1))
        gj = gcum[rows]
        g_mid = jnp.sum(jnp.where(row == c // 2, gj, 0.0), axis=0, keepdims=True)
        g_last = jnp.sum(log_a[rows], axis=0, keepdims=True)
        qa.append(qs[rows] * jnp.exp(gj - g_mid))
        ka.append(k[rows] * jnp.exp(g_mid - gj))
        qg.append(qs[rows] * jnp.exp(gj))
        kl.append(k[rows] * jnp.exp(g_last - gj))
        dec.append(jnp.exp(g_last))
    heads = lambda ts: _stack([t[:, HD * h:HD * (h + 1)] for t in ts for h in range(NH)])
    qa, ka, qg, kl, dec = heads(qa), heads(ka), heads(qg), heads(kl), heads(dec)
    vh = heads([v[c * j:c * (j + 1)] for j in range(cps)])
    r2, c2 = _tri(c)
    o_intra = _bdot(jnp.where(r2 >= c2, _bdot(qa, ka, "nt"), 0.0), vh, "nn")
    ys = []
    for j in range(cps):
        b = slice(NH * j, NH * (j + 1))
        o = o_intra[b] + _bdot(qg[b], state_t, "nt")
        state_t = state_t * dec[b] + _bdot(vh[b], kl[b], "tn")
        on = o * lax.rsqrt(jnp.mean(o * o, axis=-1, keepdims=True) + EPS) * nw
        ys.append(jnp.concatenate([on[h] for h in range(NH)], axis=1))
    return jnp.concatenate(ys, axis=0) * (z * _sigmoid(z)), state_t


def _rms(x, nw):
    r = lax.rsqrt(jnp.mean(x * x, axis=-1, keepdims=True) + EPS)
    xh = x * r
    return xh * nw, xh, r


def _rms_bwd(g, xh, r, nw):
    gw = g * nw
    return r * (gw - xh * jnp.mean(gw * xh, axis=-1, keepdims=True)), jnp.sum(g * xh, axis=0, keepdims=True)


def _row(tm, w, blk=0):
    return pl.BlockSpec((tm, w), lambda i: (i, blk))


def _in_proj(x, nw, wp, l, tm=1024, rider=None):
    t = x.shape[0]
    tm = min(tm, t)

    def body(*refs):
        (x_ref, nw_ref, w_ref, h_ref, p_ref), ride = _split_refs(refs, 3, 2, 0, rider)
        _ride_begin(rider, ride, (t // tm,))
        h = _rms(x_ref[...], nw_ref[l:l + 1, :])[0].astype(BF16)
        h_ref[...] = h
        p_ref[...] = jnp.dot(h, w_ref[...], preferred_element_type=F32)
        _ride_end(rider, ride, (t // tm,))

    specs, operands = _host(
        rider, [_row(tm, D), _resident((DEPTH, D)), _resident((D, P))], [_row(tm, D), _row(tm, P)],
        [jax.ShapeDtypeStruct((t, D), BF16), jax.ShapeDtypeStruct((t, P), F32)], [], [x, nw, wp])
    return pl.pallas_call(body, name="in_proj", grid=(t // tm,), compiler_params=_cparams("arbitrary"),
                          **specs)(*operands)


def _gu_spec(l):
    return pl.BlockSpec((N_DEV, None, GU_SHARD, D), lambda *_: (0, l, 0, 0), pipeline_mode=pl.Buffered(1))


def _out_mlp(x, ys, w_out, nw2, w_gu_t, w_down, l, tm=256, head=None):
    t = x.shape[0]

    def body(*refs):
        x_ref, ya, yb, yc, yd, wo_ref, nw_ref, wgu_ref, wd_ref = refs[:9]
        if head is None:
            mix_ref, x1_ref, h2_ref, gu_ref, act_ref, x2_ref = refs[9:]
        else:
            fnw_ref, tg_ref, mix_ref, x1_ref, h2_ref, gu_ref, act_ref, dx_ref, loss_ref, dfnw_ref = refs[9:]
        mix = jnp.concatenate([ya[...], yb[...], yc[...], yd[...]], axis=1)
        mix_ref[...] = mix
        x1 = x_ref[...] + jnp.dot(mix, wo_ref[...], preferred_element_type=F32)
        x1_ref[...] = x1
        h2 = _rms(x1, nw_ref[l:l + 1, :])[0].astype(BF16)
        h2_ref[...] = h2
        gu = _dg(h2, wgu_ref[...].reshape(2 * FF, D), 1, 1)
        gu_ref[...] = gu.astype(BF16)
        gate, up = gu[:, :FF], gu[:, FF:]
        act = (gate * _sigmoid(gate) * up).astype(BF16)
        act_ref[...] = act
        x2 = x1 + jnp.dot(act, wd_ref[...], preferred_element_type=F32)
        if head is None:
            x2_ref[...] = x2
            return

        @pl.when(pl.program_id(0) == 0)
        def _():
            loss_ref[...] = jnp.zeros_like(loss_ref)
            dfnw_ref[...] = jnp.zeros_like(dfnw_ref)
        fnw = fnw_ref[...]
        y, xh, r = _rms(x2, fnw)
        err = y - tg_ref[...]
        loss_ref[...] += 0.5 * jnp.sum(err * err) * (1.0 / D)
        dx_ref[...], dfnw = _rms_bwd(err * (1.0 / D), xh, r, fnw)
        dfnw_ref[...] += dfnw

    in_specs = [_row(tm, D), _row(tm, G), _row(tm, G), _row(tm, G), _row(tm, G), _layer((D, D), l),
                _resident((DEPTH, D)), _gu_spec(l), _layer((FF, D), l)]
    out_specs = [_row(tm, D), _row(tm, D), _row(tm, D), _row(tm, 2 * FF), _row(tm, FF), _row(tm, D)]
    out_shape = [jax.ShapeDtypeStruct((t, D), BF16), jax.ShapeDtypeStruct((t, D), F32),
                 jax.ShapeDtypeStruct((t, D), BF16), jax.ShapeDtypeStruct((t, 2 * FF), BF16),
                 jax.ShapeDtypeStruct((t, FF), BF16), jax.ShapeDtypeStruct((t, D), F32)]
    operands = [x, *ys, w_out, nw2, w_gu_t, w_down]
    if head is not None:
        in_specs += [_resident((1, D)), _row(tm, D)]
        out_specs += [_acc((8, LANES)), _acc((1, D))]
        out_shape += [jax.ShapeDtypeStruct((8, LANES), F32), jax.ShapeDtypeStruct((1, D), F32)]
        operands += list(head)
    return pl.pallas_call(body, name="out_mlp", grid=(t // tm,), in_specs=in_specs, out_specs=out_specs,
                          out_shape=out_shape, compiler_params=_cparams("arbitrary"))(*operands)


def _mlp_out_bwd(dx2, x1, gu, nw2, w_down, w_gu, w_out, l, tm=256, rider=None):
    t = dx2.shape[0]

    def body(*refs):
        (dx2_ref, x1_ref, gu_ref, nw_ref, wd_ref, wgu_ref, wo_ref, dgu_ref, dx1_ref, dmix_ref, dnw_ref), ride = \
            _split_refs(refs, 7, 4, 0, rider)
        _ride_begin(rider, ride, (t // tm,))

        @pl.when(pl.program_id(0) == 0)
        def _():
            dnw_ref[...] = jnp.zeros_like(dnw_ref)
        dx2 = dx2_ref[...]
        dact = _dg(dx2.astype(BF16), wd_ref[...], 1, 1)
        gu = gu_ref[...].astype(F32)
        gate, up = gu[:, :FF], gu[:, FF:]
        s = _sigmoid(gate)
        dgu = jnp.concatenate([dact * up * (s * (1.0 + gate * (1.0 - s))), dact * (gate * s)], axis=1).astype(BF16)
        dgu_ref[...] = dgu
        dh2 = jnp.dot(dgu, wgu_ref[...].reshape(2 * FF, D), preferred_element_type=F32)
        nw_v = nw_ref[l:l + 1, :]
        _, xh, r = _rms(x1_ref[...], nw_v)
        dxn, dnw = _rms_bwd(dh2, xh, r, nw_v)
        dx1 = dx2 + dxn
        dx1_ref[...] = dx1
        dnw_ref[...] += dnw
        dmix_ref[...] = _dg(dx1.astype(BF16), wo_ref[...], 1, 1)
        _ride_end(rider, ride, (t // tm,))

    specs, operands = _host(
        rider, [_row(tm, D), _row(tm, D), _row(tm, 2 * FF), _resident((DEPTH, D)), _layer((FF, D), l), _gu_spec(l),
                _layer((D, D), l)],
        [_row(tm, 2 * FF), _row(tm, D), _row(tm, D), _acc((1, D))],
        [jax.ShapeDtypeStruct((t, 2 * FF), BF16), jax.ShapeDtypeStruct((t, D), F32),
         jax.ShapeDtypeStruct((t, D), F32), jax.ShapeDtypeStruct((1, D), F32)],
        [], [dx2, x1, gu, nw2, w_down, w_gu, w_out])
    return pl.pallas_call(body, name="mlp_out_bwd", grid=(t // tm,), compiler_params=_cparams("arbitrary"),
                          **specs)(*operands)


_DP_WIDTHS = (768, 768, 512, 768, 256, 256, 128, 128)


def _in_proj_bwd(dps, x, dx1, nw, wp, l, tm=512):
    t = x.shape[0]

    def body(*refs):
        dp_refs, (x_ref, dx1_ref, nw_ref, w_ref, dp_ref, dx_ref, dnw_ref) = refs[:8], refs[8:]

        @pl.when(pl.program_id(0) == 0)
        def _():
            dnw_ref[...] = jnp.zeros_like(dnw_ref)
        dp = jnp.concatenate([r[...] for r in dp_refs], axis=1)
        dp_ref[...] = dp
        dh = _dg(dp, w_ref[...], 1, 1)
        nw_v = nw_ref[l:l + 1, :]
        _, xh, r = _rms(x_ref[...], nw_v)
        dxn, dnw = _rms_bwd(dh, xh, r, nw_v)
        dx_ref[...] = dx1_ref[...] + dxn
        dnw_ref[...] += dnw

    return pl.pallas_call(
        body, name="in_proj_bwd", grid=(t // tm,),
        in_specs=[_row(tm, w) for w in _DP_WIDTHS] + [_row(tm, D), _row(tm, D), _resident((DEPTH, D)), _resident((D, P))],
        out_specs=[_row(tm, P), _row(tm, D), _acc((1, D))],
        out_shape=[jax.ShapeDtypeStruct((t, P), BF16), jax.ShapeDtypeStruct((t, D), F32),
                   jax.ShapeDtypeStruct((1, D), F32)],
        compiler_params=_cparams("arbitrary"))(*dps, x, dx1, nw, wp)


def _accumulate(acc, o_ref, t_axis, term):
    @pl.when(pl.program_id(t_axis) == 0)
    def _():
        acc[...] = jnp.zeros_like(acc)
    acc[...] += term

    @pl.when(pl.program_id(t_axis) == pl.num_programs(t_axis) - 1)
    def _():
        o_ref[...] = acc[...].astype(o_ref.dtype)


WGRAD_TOKENS = 2048


WGRAD_ROWS = 2 * GU_SHARD


def _wgrad_rows(a, b, name, tt=WGRAD_TOKENS, rows=WGRAD_ROWS, out_dtype=BF16, rider=None):
    t, m = a.shape
    tt = min(tt, t)
    grid = (m // rows, t // tt)

    def body(*refs):
        (a_ref, b_ref, o_ref, acc), ride = _split_refs(refs, 2, 1, 1, rider)
        _ride_begin(rider, ride, grid)
        _accumulate(acc, o_ref, 1, _dg(a_ref[...], b_ref[...].astype(BF16), 0, 0))
        _ride_end(rider, ride, grid)

    specs, operands = _host(
        rider, [pl.BlockSpec((tt, rows), lambda j, i: (i, j)), pl.BlockSpec((tt, D), lambda j, i: (i, 0))],
        [pl.BlockSpec((rows, D), lambda j, i: (j, 0))], [jax.ShapeDtypeStruct((m, D), out_dtype)],
        [pltpu.VMEM((rows, D), F32)], [a, b])
    out = pl.pallas_call(body, name=name, grid=grid, compiler_params=_cparams("arbitrary", "arbitrary"), **specs)(*operands)
    return out[0] if rider is None else out


def _relayout_w_in(g, tr=512):
    def body(w_ref, o_ref):
        full = jnp.concatenate([w_ref[d] for d in range(N_DEV)], axis=1)
        parts = []
        for a, b, z in _PAD_SEGS:
            parts.append(full[:, a:b])
            if z:
                parts.append(jnp.zeros((tr, z), full.dtype))
        o_ref[...] = jnp.concatenate(parts, axis=1)

    return pl.pallas_call(
        body, name="relayout_w_in", grid=(D // tr,),
        in_specs=[pl.BlockSpec((N_DEV, tr, IN_SHARD), lambda i: (0, i, 0))], out_specs=_row(tr, P),
        out_shape=jax.ShapeDtypeStruct((D, P), g.dtype), compiler_params=_cparams("parallel"))(g)


IN_ROWS = 432


def _scatter_w_in_grad(gp_t, tc=512):
    def body(g_ref, o_ref):
        g = g_ref[...]
        pieces, off = [], 0
        for a, b, z in _PAD_SEGS:
            pieces.append((a, g[off:off + (b - a)]))
            off += (b - a) + z
        spill = -(-(IN_SHARD * (N_DEV - 1) + IN_ROWS - IN_COLS) // 8) * 8
        nat = jnp.concatenate([piece for _, piece in sorted(pieces, key=lambda ap: ap[0])]
                              + [jnp.zeros((spill, tc), F32)], axis=0)
        for d in range(N_DEV):
            o_ref[d] = nat[IN_SHARD * d:IN_SHARD * d + IN_ROWS].astype(BF16)

    return pl.pallas_call(
        body, name="scatter_w_in_grad", grid=(D // tc,),
        in_specs=[pl.BlockSpec((P, tc), lambda i: (0, i))],
        out_specs=pl.BlockSpec((N_DEV, IN_ROWS, tc), lambda i: (0, 0, i)),
        out_shape=jax.ShapeDtypeStruct((N_DEV, IN_ROWS, D), BF16),
        compiler_params=_cparams("parallel"))(gp_t)


_A_BLK = 3


def _sgu_specs(l):
    return [_resident((DEPTH, G)), _resident((DEPTH, G)), _layer((NH, SGU_C, SGU_C), l), _layer((NH, SGU_C), l)]


def _sgu_fwd(p, ln_w, ln_b, ws, bs, l, tm=1024):
    t = p.shape[0]
    tm = min(tm, t)

    def body(uv_ref, lw_ref, lb_ref, ws_ref, bs_ref, y_ref):
        ws_v = [ws_ref[h] for h in range(NH)]
        for c in range(tm // SGU_C):
            rows = pl.ds(c * SGU_C, SGU_C)
            y_ref[rows, :] = _sgu_chunk(uv_ref[rows, :], lw_ref[l:l + 1, :], lb_ref[l:l + 1, :], ws_v,
                                        bs_ref[...]).astype(BF16)

    return pl.pallas_call(
        body, name="sgu_fwd", grid=(t // tm,),
        in_specs=[_row(tm, 2 * G, _A_BLK)] + _sgu_specs(l), out_specs=_row(tm, G),
        out_shape=jax.ShapeDtypeStruct((t, G), BF16),
        compiler_params=_cparams("parallel"))(p, ln_w, ln_b, ws, bs)


def _sgu_bwd(p, dmix, ln_w, ln_b, ws, bs, l, tm=1024):
    t = p.shape[0]
    tm = min(tm, t)

    def body(uv_ref, dy_ref, lw_ref, lb_ref, ws_ref, bs_ref, duv_ref, dlw_ref, dlb_ref, dws_ref, dbs_ref):
        @pl.when(pl.program_id(0) == 0)
        def _():
            for r in (dlw_ref, dlb_ref, dws_ref, dbs_ref):
                r[...] = jnp.zeros_like(r)
        ws_v = [ws_ref[h] for h in range(NH)]
        for c in range(tm // SGU_C):
            rows = pl.ds(c * SGU_C, SGU_C)
            _, vjp = jax.vjp(_sgu_chunk, uv_ref[rows, :], lw_ref[l:l + 1, :], lb_ref[l:l + 1, :], ws_v, bs_ref[...])
            duv, dlw, dlb, dws, dbs = vjp(dy_ref[rows, :])
            duv_ref[rows, :] = duv.astype(BF16)
            dlw_ref[...] += dlw
            dlb_ref[...] += dlb
            dbs_ref[...] += dbs
            for h in range(NH):
                dws_ref[h] += dws[h]

    return pl.pallas_call(
        body, name="sgu_bwd", grid=(t // tm,),
        in_specs=[_row(tm, 2 * G, _A_BLK), _row(tm, G, 0)] + _sgu_specs(l),
        out_specs=[_row(tm, 2 * G), _acc((1, G)), _acc((1, G)), _acc((NH, SGU_C, SGU_C)), _acc((NH, SGU_C))],
        out_shape=[jax.ShapeDtypeStruct((t, 2 * G), BF16), jax.ShapeDtypeStruct((1, G), F32),
                   jax.ShapeDtypeStruct((1, G), F32), jax.ShapeDtypeStruct((NH, SGU_C, SGU_C), F32),
                   jax.ShapeDtypeStruct((NH, SGU_C), F32)],
        compiler_params=_cparams("arbitrary"))(p, dmix, ln_w, ln_b, ws, bs)


HALO = 8


def _prev_halo(tm, width, col):
    return pl.BlockSpec((HALO, width), lambda i: (jnp.maximum(i * (tm // HALO) - 1, 0), col))


def _next_halo(tm, width, col, t):
    return pl.BlockSpec((HALO, width), lambda i: (jnp.minimum((i + 1) * (tm // HALO), t // HALO - 1), col))


def _with_prev(halo_ref, x_ref):
    halo = jnp.where(pl.program_id(0) == 0, 0.0, halo_ref[...])
    return jnp.concatenate([halo, x_ref[...]], axis=0)


def _with_next(x_ref, halo_ref):
    halo = jnp.where(pl.program_id(0) == pl.num_programs(0) - 1, 0.0, halo_ref[...])
    return jnp.concatenate([x_ref[...], halo], axis=0)


def _delay(ext, j):
    return ext if j == 0 else pltpu.roll(ext, j, axis=0)


def _advance(ext, j):
    return ext if j == 0 else pltpu.roll(ext, ext.shape[0] - j, axis=0)


def _causal_conv(ext, w, tm):
    kw = w.shape[0]
    out = None
    for k in range(kw):
        term = _delay(ext, kw - 1 - k)[HALO:HALO + tm] * w[k:k + 1, :]
        out = term if out is None else out + term
    return out


def _causal_conv_t(ext, w, tm):
    kw = w.shape[0]
    out = None
    for k in range(kw):
        term = _advance(ext, kw - 1 - k)[0:tm] * w[k:k + 1, :]
        out = term if out is None else out + term
    return out


def _conv_wgrad(ext, dy, kw, tm):
    return jnp.concatenate(
        [jnp.sum(_delay(ext, kw - 1 - k)[HALO:HALO + tm] * dy, axis=0, keepdims=True) for k in range(kw)], axis=0)


_B_GB, _B_GC, _B_H = 8, 9, 10
_C_QKV, _D_QKV = 0, 1
_C_Z, _D_Z = 11, 12
_C_AB, _D_GLR = 26, 27


def _sconv_fwd(p, w, l, tm=2048):
    t = p.shape[0]
    tm = min(tm, t)

    def body(gb_ref, gc_ref, h_ref, gc_halo, h_halo, w_ref, y_ref):
        s_ext = _with_prev(gc_halo, gc_ref) * _with_prev(h_halo, h_ref)
        y_ref[...] = (gb_ref[...] * _causal_conv(s_ext, w_ref[...], tm)).astype(BF16)

    return pl.pallas_call(
        body, name="sconv_fwd", grid=(t // tm,),
        in_specs=[_row(tm, G, _B_GB), _row(tm, G, _B_GC), _row(tm, G, _B_H), _prev_halo(tm, G, _B_GC),
                  _prev_halo(tm, G, _B_H), _layer((3, G), l)],
        out_specs=_row(tm, G), out_shape=jax.ShapeDtypeStruct((t, G), BF16),
        compiler_params=_cparams("parallel"))(p, p, p, p, p, w)


def _sconv_bwd(p, dmix, w, l, tm=2048):
    t = p.shape[0]
    tm = min(tm, t)

    def body(gb_ref, gc_ref, h_ref, gc_halo, h_halo, gb_next, dy_ref, dy_next, w_ref, dp_ref, dw_ref):
        @pl.when(pl.program_id(0) == 0)
        def _():
            dw_ref[...] = jnp.zeros_like(dw_ref)
        w_v = w_ref[...]
        s_ext = _with_prev(gc_halo, gc_ref) * _with_prev(h_halo, h_ref)
        dout = dy_ref[...]
        d_gb = dout * _causal_conv(s_ext, w_v, tm)
        dconv_ext = _with_next(dy_ref, dy_next) * _with_next(gb_ref, gb_next)
        ds = _causal_conv_t(dconv_ext, w_v, tm)
        dw_ref[...] += _conv_wgrad(s_ext, dconv_ext[0:tm], 3, tm)
        dp_ref[...] = jnp.concatenate([d_gb, ds * h_ref[...], ds * gc_ref[...]], axis=1).astype(BF16)

    return pl.pallas_call(
        body, name="sconv_bwd", grid=(t // tm,),
        in_specs=[_row(tm, G, _B_GB), _row(tm, G, _B_GC), _row(tm, G, _B_H), _prev_halo(tm, G, _B_GC),
                  _prev_halo(tm, G, _B_H), _next_halo(tm, G, _B_GB, t), _row(tm, G, 1), _next_halo(tm, G, 1, t),
                  _layer((3, G), l)],
        out_specs=[_row(tm, 3 * G), _acc((3, G))],
        out_shape=[jax.ShapeDtypeStruct((t, 3 * G), BF16), jax.ShapeDtypeStruct((3, G), F32)],
        compiler_params=_cparams("arbitrary"))(p, p, p, p, p, p, dmix, dmix, w)


def _qkv_conv_fwd(p, w, l, tm=2048):
    t = p.shape[0]
    tm = min(tm, t)

    def body(x_ref, x_halo, w_ref, y_ref):
        c = _causal_conv(_with_prev(x_halo, x_ref), w_ref[...], tm)
        y_ref[...] = c * _sigmoid(c)

    return pl.pallas_call(
        body, name="qkv_conv_fwd", grid=(t // tm,),
        in_specs=[_row(tm, 3 * G, _C_QKV), _prev_halo(tm, 3 * G, _C_QKV), _layer((4, 3 * G), l)],
        out_specs=_row(tm, 3 * G), out_shape=jax.ShapeDtypeStruct((t, 3 * G), F32),
        compiler_params=_cparams("parallel"))(p, p, w)


def _qkv_conv_bwd(p, dy, w, l, tm=1024):
    t = p.shape[0]
    tm = min(tm, t)

    def body(x_ref, x_halo, x_next, dy_ref, dy_next, w_ref, dx_ref, dw_ref):
        @pl.when(pl.program_id(0) == 0)
        def _():
            dw_ref[...] = jnp.zeros_like(dw_ref)
        w_v = w_ref[...]
        x_all = jnp.concatenate([_with_prev(x_halo, x_ref), jnp.where(
            pl.program_id(0) == pl.num_programs(0) - 1, 0.0, x_next[...])], axis=0)
        c = _causal_conv(x_all, w_v, tm + HALO)
        s = _sigmoid(c)
        dc_ext = _with_next(dy_ref, dy_next) * (s * (1.0 + c * (1.0 - s)))
        dx_ref[...] = _causal_conv_t(dc_ext, w_v, tm).astype(BF16)
        dw_ref[...] += _conv_wgrad(x_all[0:HALO + tm], dc_ext[0:tm], 4, tm)

    return pl.pallas_call(
        body, name="qkv_conv_bwd", grid=(t // tm,),
        in_specs=[_row(tm, 3 * G, _C_QKV), _prev_halo(tm, 3 * G, _C_QKV), _next_halo(tm, 3 * G, _C_QKV, t),
                  _row(tm, 3 * G), _next_halo(tm, 3 * G, 0, t), _layer((4, 3 * G), l)],
        out_specs=[_row(tm, 3 * G), _acc((4, 3 * G))],
        out_shape=[jax.ShapeDtypeStruct((t, 3 * G), BF16), jax.ShapeDtypeStruct((4, 3 * G), F32)],
        compiler_params=_cparams("arbitrary"))(p, p, p, dy, dy, w)


_STATE = pl.BlockSpec((1, NH, HD, HD), lambda i: (i, 0, 0, 0))


def _dn_specs():
    return [_resident((DEPTH, NH)), _resident((DEPTH, NH)), _resident((DEPTH, HD))]


def _dn_fwd(qkv, p, params, l, cps=8, rider=None):
    t = qkv.shape[0]
    tm = DN_C * cps
    nb = cps * NH

    def body(*refs):
        (qkv_ref, z_ref, ab_ref, alog_ref, dt_ref, nw_ref, y_ref, st_ref, inv_ref, state), ride = _split_refs(
            refs, 6, 3, 1, rider)
        _ride_begin(rider, ride, (t // tm,))

        @pl.when(pl.program_id(0) == 0)
        def _():
            state[...] = jnp.zeros_like(state)
        st_ref[0] = state[...]
        y, new, inv = _dn_tile(qkv_ref[:, 0:G], qkv_ref[:, G:2 * G], qkv_ref[:, 2 * G:3 * G], ab_ref[...], z_ref[...],
                               state[...], None, alog_ref[l:l + 1, :], dt_ref[l:l + 1, :], nw_ref[l:l + 1, :])
        y_ref[...] = y.astype(BF16)
        inv_ref[...] = inv
        state[...] = new
        _ride_end(rider, ride, (t // tm,))

    specs, operands = _host(
        rider, [_row(tm, 3 * G), _row(tm, G, _C_Z), _row(tm, LANES, _C_AB)] + _dn_specs(),
        [_row(tm, G), _STATE, pl.BlockSpec((nb, DN_C, DN_C), lambda i: (i, 0, 0))],
        [jax.ShapeDtypeStruct((t, G), BF16), jax.ShapeDtypeStruct((t // tm, NH, HD, HD), F32),
         jax.ShapeDtypeStruct((t // DN_C * NH, DN_C, DN_C), F32)],
        [pltpu.VMEM((NH, HD, HD), F32)], [qkv, p, p, *params])
    return pl.pallas_call(body, name="dn_fwd", grid=(t // tm,), compiler_params=_cparams("arbitrary"), **specs)(*operands)


def _dn_bwd(qkv, p, states, inv, dmix, params, l, cps=8, rider=None):
    t = qkv.shape[0]
    tm = DN_C * cps
    n = t // tm
    nb = cps * NH

    def body(*refs):
        (qkv_ref, z_ref, ab_ref, st_ref, inv_ref, dy_ref, alog_ref, dt_ref, nw_ref,
         dqkv_ref, dz_ref, dab_ref, dalog_ref, ddt_ref, dnw_ref, dstate), ride = _split_refs(refs, 9, 6, 1, rider)
        _ride_begin(rider, ride, (n,))
        dprm_refs = (dalog_ref, ddt_ref, dnw_ref)

        @pl.when(pl.program_id(0) == 0)
        def _():
            dstate[...] = jnp.zeros_like(dstate)
            for r in dprm_refs:
                r[...] = jnp.zeros_like(r)
        inv_v = inv_ref[...]
        tile = lambda q, k, v, ab, z, st, alog, dt, nw: _dn_tile(q, k, v, ab, z, st, inv_v, alog, dt, nw)[:2]
        _, vjp = jax.vjp(tile, qkv_ref[:, 0:G], qkv_ref[:, G:2 * G], qkv_ref[:, 2 * G:3 * G], ab_ref[...], z_ref[...],
                         st_ref[0], alog_ref[l:l + 1, :], dt_ref[l:l + 1, :], nw_ref[l:l + 1, :])
        dq, dk, dv, dab, dz, dst, *dprm = vjp((dy_ref[...], dstate[...]))
        dqkv_ref[...] = jnp.concatenate([dq, dk, dv], axis=1)
        dz_ref[...] = dz.astype(BF16)
        dab_ref[...] = dab.astype(BF16)
        dstate[...] = dst
        for r, g in zip(dprm_refs, dprm):
            r[...] += g
        _ride_end(rider, ride, (n,))

    rev = lambda w, blk: pl.BlockSpec((tm, w), lambda i: (n - 1 - i, blk))
    specs, operands = _host(
        rider, [rev(3 * G, 0), rev(G, _C_Z), rev(LANES, _C_AB),
                pl.BlockSpec((1, NH, HD, HD), lambda i: (n - 1 - i, 0, 0, 0)),
                pl.BlockSpec((nb, DN_C, DN_C), lambda i: (n - 1 - i, 0, 0)), rev(G, 2)] + _dn_specs(),
        [rev(3 * G, 0), rev(G, 0), rev(LANES, 0), _acc((1, NH)), _acc((1, NH)), _acc((1, HD))],
        [jax.ShapeDtypeStruct((t, 3 * G), F32), jax.ShapeDtypeStruct((t, G), BF16),
         jax.ShapeDtypeStruct((t, LANES), BF16), jax.ShapeDtypeStruct((1, NH), F32),
         jax.ShapeDtypeStruct((1, NH), F32), jax.ShapeDtypeStruct((1, HD), F32)],
        [pltpu.VMEM((NH, HD, HD), F32)], [qkv, p, p, states, inv, dmix, *params])
    return pl.pallas_call(body, name="dn_bwd", grid=(n,), compiler_params=_cparams("arbitrary"), **specs)(*operands)


def _gla_specs(l):
    return [_layer((16, G), l), _resident((DEPTH, G)), _resident((DEPTH, HD))]


def _gla_fwd(p, params, l, cps=8, rider=None):
    t = p.shape[0]
    tm = GLA_C * cps

    def body(*refs):
        (qkv_ref, z_ref, glr_ref, w2_ref, gb_ref, nw_ref, y_ref, st_ref, state), ride = _split_refs(refs, 6, 2, 1, rider)
        _ride_begin(rider, ride, (t // tm,))

        @pl.when(pl.program_id(0) == 0)
        def _():
            state[...] = jnp.zeros_like(state)
        st_ref[0] = state[...]
        y, new = _gla_tile(qkv_ref[:, 0:G], qkv_ref[:, G:2 * G], qkv_ref[:, 2 * G:3 * G], glr_ref[...], z_ref[...],
                           state[...], w2_ref[...], gb_ref[l:l + 1, :], nw_ref[l:l + 1, :])
        y_ref[...] = y.astype(BF16)
        state[...] = new
        _ride_end(rider, ride, (t // tm,))

    specs, operands = _host(
        rider, [_row(tm, 3 * G, _D_QKV), _row(tm, G, _D_Z), _row(tm, LANES, _D_GLR)] + _gla_specs(l),
        [_row(tm, G), _STATE],
        [jax.ShapeDtypeStruct((t, G), BF16), jax.ShapeDtypeStruct((t // tm, NH, HD, HD), F32)],
        [pltpu.VMEM((NH, HD, HD), F32)], [p, p, p, *params])
    return pl.pallas_call(body, name="gla_fwd", grid=(t // tm,), compiler_params=_cparams("arbitrary"), **specs)(*operands)


def _gla_bwd(p, states, dmix, params, l, cps=8):
    t = p.shape[0]
    tm = GLA_C * cps
    n = t // tm

    def body(qkv_ref, z_ref, glr_ref, st_ref, dy_ref, w2_ref, gb_ref, nw_ref,
             dqkv_ref, dz_ref, dglr_ref, dw2_ref, dgb_ref, dnw_ref, dstate):
        dprm_refs = (dw2_ref, dgb_ref, dnw_ref)

        @pl.when(pl.program_id(0) == 0)
        def _():
            dstate[...] = jnp.zeros_like(dstate)
            for r in dprm_refs:
                r[...] = jnp.zeros_like(r)
        _, vjp = jax.vjp(_gla_tile, qkv_ref[:, 0:G], qkv_ref[:, G:2 * G], qkv_ref[:, 2 * G:3 * G], glr_ref[...],
                         z_ref[...], st_ref[0], w2_ref[...], gb_ref[l:l + 1, :], nw_ref[l:l + 1, :])
        dq, dk, dv, dglr, dz, dst, *dprm = vjp((dy_ref[...], dstate[...]))
        dqkv_ref[...] = jnp.concatenate([dq, dk, dv], axis=1).astype(BF16)
        dz_ref[...] = dz.astype(BF16)
        dglr_ref[...] = dglr.astype(BF16)
        dstate[...] = dst
        for r, g in zip(dprm_refs, dprm):
            r[...] += g

    rev = lambda w, blk: pl.BlockSpec((tm, w), lambda i: (n - 1 - i, blk))
    return pl.pallas_call(
        body, name="gla_bwd", grid=(n,),
        in_specs=[rev(3 * G, _D_QKV), rev(G, _D_Z), rev(LANES, _D_GLR),
                  pl.BlockSpec((1, NH, HD, HD), lambda i: (n - 1 - i, 0, 0, 0)), rev(G, 3)] + _gla_specs(l),
        out_specs=[rev(3 * G, 0), rev(G, 0), rev(LANES, 0), _acc((16, G)), _acc((1, G)), _acc((1, HD))],
        out_shape=[jax.ShapeDtypeStruct((t, 3 * G), BF16), jax.ShapeDtypeStruct((t, G), BF16),
                   jax.ShapeDtypeStruct((t, LANES), BF16), jax.ShapeDtypeStruct((16, G), F32),
                   jax.ShapeDtypeStruct((1, G), F32), jax.ShapeDtypeStruct((1, HD), F32)],
        scratch_shapes=[pltpu.VMEM((NH, HD, HD), F32)],
        compiler_params=_cparams("arbitrary"))(p, p, p, states, dmix, *params)


def _adamw_math(w, g, m, v):
    m = ADAM_B1 * m + (1.0 - ADAM_B1) * g
    v = ADAM_B2 * v + (1.0 - ADAM_B2) * (g * g)
    m_hat = m / (1.0 - ADAM_B1 ** ADAM_STEP)
    v_hat = v / (1.0 - ADAM_B2 ** ADAM_STEP)
    return -ADAM_LR * (m_hat / (jnp.sqrt(v_hat) + ADAM_EPS) + ADAM_WD * w), m, v


def _adamw_large(parts, w, m, v, name, tr):
    _, r, c = w.shape

    def body(p0_ref, p1_ref, w_ref, m_ref, v_ref, g_ref, d_ref, nm_ref, nv_ref):
        def total(p_ref):
            g = p_ref[0].astype(F32)
            for k in range(1, N_DEV):
                g = g + p_ref[k].astype(F32)
            return g

        g = jnp.where(pl.program_id(0) == 0, total(p0_ref), total(p1_ref))
        g_ref[...] = g
        d_ref[...], nm_ref[...], nv_ref[...] = _adamw_math(w_ref[...], g, m_ref[...], v_ref[...])

    blk = pl.BlockSpec((None, tr, c), lambda l, i: (l, i, 0))
    part = pl.BlockSpec((N_DEV, tr, c), lambda l, i: (0, i, 0))
    return pl.pallas_call(
        body, name=name, grid=(DEPTH, r // tr), in_specs=[part, part, blk, blk, blk],
        out_specs=[blk] * 4, out_shape=[jax.ShapeDtypeStruct(w.shape, F32)] * 4,
        compiler_params=_cparams("parallel", "parallel"))(*parts, w, m, v)


def _adamw_w_in(parts, w, m, v, tc=512):
    def body(p0_ref, p1_ref, w_ref, m_ref, v_ref, g_ref, d_ref, nm_ref, nv_ref):
        for l, p_ref in enumerate((p0_ref, p1_ref)):
            g = p_ref[0, 0:IN_SHARD, :].astype(F32)
            for k in range(1, N_DEV):
                g = g + p_ref[k, 0:IN_SHARD, :].astype(F32)
            g_ref[:, l, :] = g
            d_ref[:, l, :], nm_ref[:, l, :], nv_ref[:, l, :] = _adamw_math(w_ref[:, l, :], g, m_ref[:, l, :], v_ref[:, l, :])

    blk = pl.BlockSpec((IN_SHARD, DEPTH, tc), lambda i: (0, 0, i))
    part = pl.BlockSpec((N_DEV, IN_ROWS, tc), lambda i: (0, 0, i))
    return pl.pallas_call(
        body, name="adamw_w_in", grid=(D // tc,), in_specs=[part, part, blk, blk, blk], out_specs=[blk] * 4,
        out_shape=[jax.ShapeDtypeStruct(w.shape, F32)] * 4, compiler_params=_cparams("parallel"))(*parts, w, m, v)


_SLAB_AT = {
    "sgu_w_spatial": (0, 0, SGU_C, LANES),
    "sgu_b_spatial": (128, 0, NH, SGU_C),
    "norm1_w": (132, 0, 1, D),
    "norm2_w": (133, 0, 1, D),
    "sgu_ln_w": (134, 0, 1, G),
    "sgu_ln_b": (134, 256, 1, G),
    "gla_gate_bias": (134, 512, 1, G),
    "dn_norm_w": (134, 768, 1, HD),
    "gla_norm_w": (134, 896, 1, HD),
    "dn_a_log": (135, 0, 1, NH),
    "dn_dt_bias": (135, 128, 1, NH),
    "gla_w_gate2": (136, 0, 16, G),
    "dn_conv_w": (136, 256, 4, 3 * G),
    "sc_conv_w": (140, 256, 3, G),
}
_LAYER_ROWS = 152
_FINAL_ROW = DEPTH * _LAYER_ROWS
_SLAB_ROWS, _SLAB_COLS = _FINAL_ROW + 8, 1024
_SLAB_ORDER = tuple(_SLAB_AT)
_SHARDED_SMALL = ("sc_conv_w", "dn_conv_w", "gla_w_gate2")
_REPLICATED = tuple(k for k in _SLAB_ORDER if k not in _SHARDED_SMALL)


def _region(name, l, h=0):
    r0, c0, nr, nc = _SLAB_AT[name]
    return slice(_LAYER_ROWS * l + r0, _LAYER_ROWS * l + r0 + nr), slice(c0 + nc * h, c0 + nc * (h + 1))


def _pack_small(layer_grads, d_final):
    def body(*refs):
        slab = refs[-1]
        slab[...] = jnp.zeros_like(slab)
        it = iter(refs[:-1])
        for l in range(DEPTH):
            for name in _SLAB_ORDER:
                ref = next(it)
                if name == "sgu_w_spatial":
                    for h in range(NH):
                        slab[_region(name, l, h)] = ref[h]
                else:
                    slab[_region(name, l)] = ref[...]
        slab[_FINAL_ROW:_FINAL_ROW + 1, :] = next(it)[...]

    flat = [layer_grads[l][name] for l in range(DEPTH) for name in _SLAB_ORDER] + [d_final]
    return pl.pallas_call(body, name="pack_small_grads", out_shape=jax.ShapeDtypeStruct((_SLAB_ROWS, _SLAB_COLS), F32),
                          compiler_params=_cparams())(*flat)


def _adamw_small(parts, w, m, v):
    names = _REPLICATED + ("final_norm_w",)
    n_in = len(names)

    def body(*refs):
        def total(rows, cols):
            g = refs[0][0, rows, cols]
            for k in range(1, N_DEV):
                g = g + refs[0][k, rows, cols]
            return g

        w_refs = dict(zip(names, refs[1:1 + n_in]))
        m_refs = dict(zip(names, refs[1 + n_in:1 + 2 * n_in]))
        v_refs = dict(zip(names, refs[1 + 2 * n_in:1 + 3 * n_in]))
        outs = refs[1 + 3 * n_in:]
        out_refs = [dict(zip(names, outs[j * n_in:(j + 1) * n_in])) for j in range(4)]
        full_refs = dict(zip(_SHARDED_SMALL, outs[4 * n_in:]))

        def update(name, idx, g):
            res = (g,) + _adamw_math(w_refs[name][idx], g, m_refs[name][idx], v_refs[name][idx])
            for o, val in zip(out_refs, res):
                o[name][idx] = val

        for l in range(DEPTH):
            for name in _REPLICATED:
                if name == "sgu_w_spatial":
                    for h in range(NH):
                        update(name, (l, h), total(*_region(name, l, h)))
                elif name == "sgu_b_spatial":
                    update(name, (l,), total(*_region(name, l)))
                else:
                    update(name, (slice(l, l + 1), slice(None)), total(*_region(name, l)))
            for name in _SHARDED_SMALL:
                full_refs[name][l] = total(*_region(name, l))
        update("final_norm_w", (slice(None), slice(None)), total(slice(_FINAL_ROW, _FINAL_ROW + 1), slice(None)))

    shapes = [jax.ShapeDtypeStruct(w[k].shape, F32) for k in names]
    full_shapes = [jax.ShapeDtypeStruct((DEPTH,) + _SLAB_AT[k][2:], F32) for k in _SHARDED_SMALL]
    outs = pl.pallas_call(body, name="adamw_small", out_shape=shapes * 4 + full_shapes, compiler_params=_cparams())(
        parts, *[w[k] for k in names], *[m[k] for k in names], *[v[k] for k in names])
    result = [dict(zip(names, outs[j * n_in:(j + 1) * n_in])) for j in range(4)]
    return result, dict(zip(_SHARDED_SMALL, outs[4 * n_in:]))


def _adamw_shards(g, w, m, v):
    names = _SHARDED_SMALL
    n = len(names)

    def body(*refs):
        for j in range(n):
            g_v = refs[j][...]
            res = _adamw_math(refs[n + j][...], g_v, refs[2 * n + j][...], refs[3 * n + j][...])
            for o, val in zip(refs[4 * n + j::n], res):
                o[...] = val

    shapes = [jax.ShapeDtypeStruct(w[k].shape, F32) for k in names]
    outs = pl.pallas_call(body, name="adamw_small_shards", out_shape=shapes * 3, compiler_params=_cparams())(
        *[g[k] for k in names], *[w[k] for k in names], *[m[k] for k in names], *[v[k] for k in names])
    return [dict(zip(names, outs[j * n:(j + 1) * n])) for j in range(3)]


_ANY = pl.BlockSpec(memory_space=pl.ANY)


def _place():
    return lax.axis_index("x"), lax.axis_index("y"), lax.axis_index("c")


def _sems(n):
    return [pltpu.SemaphoreType.DMA((n, 7)), pltpu.SemaphoreType.DMA((n, 7)), pltpu.SemaphoreType.DMA((n,))]


class _Gather:
    def __init__(self, blocks, second, forward_at=1.0):
        self.inputs, self.second, self.forward_at = list(blocks), list(second), forward_at
        self.out_shape = [jax.ShapeDtypeStruct((a.shape[0], N_DEV) + a.shape[1:] if s else (N_DEV,) + a.shape, a.dtype)
                          for a, s in zip(blocks, second)]
        self.scratch = _sems(len(blocks))

    def _copies(self, refs):
        x_refs, out_refs, (send_sems, recv_sems, local_sems) = refs
        n = len(x_refs)
        x, y, c = _place()
        me, sibling = (x, y, c), (x, y, 1 - c)
        chips = [(1 - x, y), (x, 1 - y), (1 - x, 1 - y)]

        def slot(j, px, py, pc):
            idx = 4 * px + 2 * py + pc
            return out_refs[j].at[:, idx] if self.second[j] else out_refs[j].at[idx]

        def copies(k, block, to, own=False):
            return [pltpu.make_async_remote_copy(
                src_ref=x_refs[j] if own else slot(j, *block), dst_ref=slot(j, *block),
                send_sem=send_sems.at[j, k], recv_sem=recv_sems.at[j, k], device_id=to,
                device_id_type=pl.DeviceIdType.MESH) for j in range(n)]

        mine = [pltpu.make_async_copy(x_refs[j], slot(j, *me), local_sems.at[j]) for j in range(n)]
        first = copies(0, me, sibling, own=True)
        for j, chip in enumerate(chips):
            first += copies(1 + j, me, (*chip, c), own=True)
        landed = [copies(1 + j, (*chip, c), me) for j, chip in enumerate(chips)]
        onward = [copies(4 + j, (*chip, c), sibling) for j, chip in enumerate(chips)]
        from_sibling = copies(0, sibling, me)
        for j, chip in enumerate(chips):
            from_sibling += copies(4 + j, (*chip, 1 - c), me)
        return mine, first, landed, onward, from_sibling

    def start(self, refs):
        mine, first, _, _, _ = self._copies(refs)
        for cp in mine + first:
            cp.start()

    def forward(self, refs):
        _, _, landed, onward, _ = self._copies(refs)
        for arrived, passed in zip(landed, onward):
            for cp in arrived:
                cp.wait_recv()
            for cp in passed:
                cp.start()

    def finish(self, refs):
        mine, first, _, onward, from_sibling = self._copies(refs)
        for cp in from_sibling:
            cp.wait_recv()
        for cp in first + [cp for passed in onward for cp in passed]:
            cp.wait_send()
        for cp in mine:
            cp.wait()


class _Exchange:
    forward_at = 1.0

    def __init__(self, sends):
        self.inputs = list(sends)
        self.out_shape = [jax.ShapeDtypeStruct(a.shape, a.dtype) for a in sends]
        self.scratch = _sems(len(sends))

    def _copies(self, refs):
        x_refs, out_refs, (send_sems, recv_sems, local_sems) = refs
        n = len(x_refs)
        x, y, c = _place()
        me = 4 * x + 2 * y + c
        sent, landing = [], []
        for k in range(1, N_DEV):
            px, py, pc = x ^ ((k >> 2) & 1), y ^ ((k >> 1) & 1), c ^ (k & 1)
            them = 4 * px + 2 * py + pc
            for j in range(n):
                for here, there, into in ((them, me, sent), (me, them, landing)):
                    into.append(pltpu.make_async_remote_copy(
                        src_ref=x_refs[j].at[here], dst_ref=out_refs[j].at[there], send_sem=send_sems.at[j, k - 1],
                        recv_sem=recv_sems.at[j, k - 1], device_id=(px, py, pc), device_id_type=pl.DeviceIdType.MESH))
        mine = [pltpu.make_async_copy(x_refs[j].at[me], out_refs[j].at[me], local_sems.at[j]) for j in range(n)]
        return mine, sent, landing

    def start(self, refs):
        mine, sent, _ = self._copies(refs)
        for cp in mine + sent:
            cp.start()

    def forward(self, refs):
        pass

    def finish(self, refs):
        mine, sent, landing = self._copies(refs)
        for cp in landing:
            cp.wait_recv()
        for cp in sent:
            cp.wait_send()
        for cp in mine:
            cp.wait()


def _run(rider, name):
    n_in, n_out = len(rider.inputs), len(rider.out_shape)

    def body(*refs):
        parts = (refs[:n_in], refs[n_in:n_in + n_out], refs[n_in + n_out:])
        rider.start(parts)
        rider.forward(parts)
        rider.finish(parts)

    return pl.pallas_call(body, name=name, out_shape=rider.out_shape, in_specs=[_ANY] * n_in, out_specs=[_ANY] * n_out,
                          scratch_shapes=rider.scratch)(*rider.inputs)


def _split_refs(refs, n_in, n_out, n_scratch, rider):
    r_in = len(rider.inputs) if rider else 0
    r_out = len(rider.out_shape) if rider else 0
    a = n_in + r_in
    b = a + n_out + r_out
    own = refs[:n_in] + refs[a:a + n_out] + refs[b:b + n_scratch]
    return own, (refs[n_in:a], refs[a + n_out:b], refs[b + n_scratch:])


def _grid_step(grid):
    step, stride = 0, 1
    for axis in reversed(range(len(grid))):
        step = step + pl.program_id(axis) * stride
        stride *= grid[axis]
    return step


def _ride_begin(rider, ride_refs, grid):
    if rider is not None:
        pl.when(_grid_step(grid) == 0)(lambda: rider.start(ride_refs))


def _ride_end(rider, ride_refs, grid):
    if rider is not None:
        steps = 1
        for n in grid:
            steps *= n
        step = _grid_step(grid)
        pl.when(step == min(steps - 1, int(steps * rider.forward_at)))(lambda: rider.forward(ride_refs))
        pl.when(step == steps - 1)(lambda: rider.finish(ride_refs))


def _host(rider, in_specs, out_specs, out_shape, scratch, operands):
    if rider is None:
        return dict(in_specs=in_specs, out_specs=out_specs, out_shape=out_shape, scratch_shapes=scratch), operands
    return dict(in_specs=in_specs + [_ANY] * len(rider.inputs), out_specs=out_specs + [_ANY] * len(rider.out_shape),
                out_shape=out_shape + rider.out_shape, scratch_shapes=scratch + rider.scratch), operands + rider.inputs


_LATE = ("w_gate_up", "w_out", "w_down")


def _local_grads(x, target, w, late=None, exchange=False):
    w = dict(w)
    w_in = list(w["w_in"])
    dn_params = (w["dn_a_log"], w["dn_dt_bias"], w["dn_norm_w"])
    gla_params = (w["gla_w_gate2"], w["gla_gate_bias"], w["gla_norm_w"])
    sgu_params = (w["sgu_ln_w"], w["sgu_ln_b"], w["sgu_w_spatial"], w["sgu_b_spatial"])
    saved = []
    for l in range(DEPTH):
        riding = l == 0 and late is not None
        h, p, *got = _in_proj(x, w["norm1_w"], w_in[l], l,
                              rider=_Gather([late["w_down"]], [True], forward_at=0.85) if riding else None)
        if riding:
            w["w_down"] = got[0].reshape(DEPTH, FF, D)
        ya = _sgu_fwd(p, *sgu_params, l)
        yb = _sconv_fwd(p, w["sc_conv_w"], l)
        qkv_c = _qkv_conv_fwd(p, w["dn_conv_w"], l)
        yc, st_c, inv_c, *got = _dn_fwd(
            qkv_c, p, dn_params, l,
            rider=_Gather([late["w_gate_up"], late["w_out"]], [False, True], forward_at=0.85) if riding else None)
        if riding:
            w.update(w_gate_up=got[0], w_out=got[1].reshape(DEPTH, D, D))
        yd, st_d, *got = _gla_fwd(p, gla_params, l,
                                  rider=_Gather([late["w_in"]], [False], forward_at=0.7) if riding else None)
        if riding:
            w_in[1] = _relayout_w_in(got[0])
        head = (w["final_norm_w"], target) if l == DEPTH - 1 else None
        mix, x1, h2, gu, act, x2, *tail = _out_mlp(x, (ya, yb, yc, yd), w["w_out"], w["norm2_w"], w["w_gate_up"],
                                                   w["w_down"], l, head=head)
        saved.append(dict(x=x, h=h, p=p, qkv_c=qkv_c, st_c=st_c, inv_c=inv_c, st_d=st_d, mix=mix, x1=x1, h2=h2, gu=gu,
                          act=act))
        x = x2
    dx, (loss, d_final) = x, tail
    large = {k: [None] * DEPTH for k in ("w_in", "w_out", "w_gate_up", "w_down")}
    small = [None] * DEPTH
    for l in reversed(range(DEPTH)):
        s = saved[l]
        p = s["p"]
        g = {}
        riding = exchange and l == 0

        def exchanging(pairs):
            return _Exchange([large[k][j] for k, j in pairs]) if riding else None

        def arrive(pairs, arrived):
            for (k, j), a in zip(pairs, arrived):
                large[k][j] = a

        pairs = (("w_gate_up", 1), ("w_down", 1))
        dgu, dx1, dmix, g["norm2_w"], *arrived = _mlp_out_bwd(
            dx, s["x1"], s["gu"], w["norm2_w"], w["w_down"], w["w_gate_up"], w["w_out"], l, rider=exchanging(pairs))
        arrive(pairs, arrived)
        pairs = (("w_in", 1), ("w_out", 1))
        d_gu, *arrived = _wgrad_rows(dgu, s["h2"], "wgrad_gate_up", rider=exchanging(pairs)) if riding else (
            _wgrad_rows(dgu, s["h2"], "wgrad_gate_up"),)
        arrive(pairs, arrived)
        large["w_gate_up"][l] = d_gu.reshape(N_DEV, GU_SHARD, D)
        large["w_down"][l] = _wgrad_rows(s["act"], dx, "wgrad_down").reshape(N_DEV, FF // N_DEV, D)
        large["w_out"][l] = _wgrad_rows(s["mix"], dx1, "wgrad_out", rows=D).reshape(N_DEV, D // N_DEV, D)
        d_a, g["sgu_ln_w"], g["sgu_ln_b"], g["sgu_w_spatial"], g["sgu_b_spatial"] = _sgu_bwd(p, dmix, *sgu_params, l)
        d_b, g["sc_conv_w"] = _sconv_bwd(p, dmix, w["sc_conv_w"], l)
        pairs = tuple((k, 0) for k in _LATE)
        dqkv_c, dz_c, dab, g["dn_a_log"], g["dn_dt_bias"], g["dn_norm_w"], *arrived = _dn_bwd(
            s["qkv_c"], p, s["st_c"], s["inv_c"], dmix, dn_params, l, rider=exchanging(pairs))
        arrive(pairs, arrived)
        d_c, g["dn_conv_w"] = _qkv_conv_bwd(p, dqkv_c, w["dn_conv_w"], l)
        d_d, dz_d, dglr, g["gla_w_gate2"], g["gla_gate_bias"], g["gla_norm_w"] = _gla_bwd(p, s["st_d"], dmix, gla_params, l)
        dp, dx, g["norm1_w"] = _in_proj_bwd((d_c, d_d, d_a, d_b, dz_c, dz_d, dab, dglr), s["x"], dx1, w["norm1_w"],
                                            w_in[l], l)
        small[l] = g
        w_in_rows = dict(rows=P // 2, out_dtype=F32)
        if riding:
            d_w_in, small = _wgrad_rows(dp, s["h"], "wgrad_in", **w_in_rows,
                                        rider=_Gather([_pack_small(small, d_final)], [False], forward_at=0.75))
        else:
            d_w_in = _wgrad_rows(dp, s["h"], "wgrad_in", **w_in_rows)
        large["w_in"][l] = _scatter_w_in_grad(d_w_in)
    return loss[0, 0], dx, large, small, d_final


_ORDER = ("norm1_w", "w_in", "sgu_ln_w", "sgu_ln_b", "sgu_w_spatial", "sgu_b_spatial", "sc_conv_w", "dn_conv_w",
          "dn_a_log", "dn_dt_bias", "dn_norm_w", "gla_w_gate2", "gla_gate_bias", "gla_norm_w", "w_out", "norm2_w",
          "w_gate_up", "w_down", "final_norm_w")
_LARGE = ("w_in", "w_gate_up", "w_out", "w_down")
_LARGE_TILE = {"w_gate_up": 352, "w_out": 128, "w_down": 352}


def _gather_cols(g):
    return jnp.transpose(g, (1, 2, 0, 3)).reshape(g.shape[1], g.shape[2], N_DEV * g.shape[3])


def kernel(x, norm1_w, w_in, sgu_ln_w, sgu_ln_b, sgu_w_spatial, sgu_b_spatial, sc_conv_w, dn_conv_w, dn_a_log, dn_dt_bias, dn_norm_w, gla_w_gate2, gla_gate_bias, gla_norm_w, w_out, norm2_w, w_gate_up, w_down, final_norm_w, loss_target, m_norm1_w, m_w_in, m_sgu_ln_w, m_sgu_ln_b, m_sgu_w_spatial, m_sgu_b_spatial, m_sc_conv_w, m_dn_conv_w, m_dn_a_log, m_dn_dt_bias, m_dn_norm_w, m_gla_w_gate2, m_gla_gate_bias, m_gla_norm_w, m_w_out, m_norm2_w, m_w_gate_up, m_w_down, m_final_norm_w, v_norm1_w, v_w_in, v_sgu_ln_w, v_sgu_ln_b, v_sgu_w_spatial, v_sgu_b_spatial, v_sc_conv_w, v_dn_conv_w, v_dn_a_log, v_dn_dt_bias, v_dn_norm_w, v_gla_w_gate2, v_gla_gate_bias, v_gla_norm_w, v_w_out, v_norm2_w, v_w_gate_up, v_w_down, v_final_norm_w):
    args = dict(locals())
    wts = {k: args[k] for k in _ORDER}
    mom = {k: args["m_" + k] for k in _ORDER}
    var = {k: args["v_" + k] for k in _ORDER}
    for d in (wts, mom, var):
        d["final_norm_w"] = d["final_norm_w"][None]
    me = 4 * lax.axis_index("x") + 2 * lax.axis_index("y") + lax.axis_index("c")
    for d in (wts, mom, var):
        d["w_gate_up"] = jnp.swapaxes(d["w_gate_up"], 1, 2)

    late = {k: wts[k].astype(BF16) for k in _LATE}
    w_in = wts["w_in"].astype(BF16)
    late["w_in"] = w_in[1]
    got = _run(_Gather([w_in[0]] + [wts[k] for k in _SHARDED_SMALL], [False] * 4), "gather_w_in")
    full = dict(wts)
    full["w_in"] = [_relayout_w_in(got[0]), None]
    for k, g in zip(_SHARDED_SMALL, got[1:]):
        full[k] = _gather_cols(g)

    loss, dx, large, small, d_final = _local_grads(x[0], loss_target[0], full, late=late, exchange=True)
    loss = lax.psum(loss, ("x", "y", "c"))

    large["w_in"][0], = _run(_Exchange([large["w_in"][0]]), "exchange_grads")
    result = {}
    for k in ("w_gate_up", "w_out", "w_down"):
        outs = _adamw_large(large[k], wts[k], mom[k], var[k], "adamw_" + k, _LARGE_TILE[k])
        for kind, a in zip(("grad", "delta", "new_m", "new_v"), outs):
            result[kind, k] = jnp.swapaxes(a, 1, 2) if k == "w_gate_up" else a
    outs = _adamw_w_in(large["w_in"], *[jnp.transpose(d["w_in"], (2, 0, 1)) for d in (wts, mom, var)])
    for kind, a in zip(("grad", "delta", "new_m", "new_v"), outs):
        result[kind, "w_in"] = jnp.transpose(a, (1, 2, 0))

    slabs = small
    rep = _REPLICATED + ("final_norm_w",)
    outs, summed = _adamw_small(slabs, {k: wts[k] for k in rep}, {k: mom[k] for k in rep}, {k: var[k] for k in rep})
    for kind, d in zip(("grad", "delta", "new_m", "new_v"), outs):
        for k, a in d.items():
            result[kind, k] = a[0] if k == "final_norm_w" else a
    own = {k: lax.dynamic_slice_in_dim(summed[k], me * wts[k].shape[-1], wts[k].shape[-1], axis=2) for k in _SHARDED_SMALL}
    outs = _adamw_shards(own, {k: wts[k] for k in _SHARDED_SMALL}, {k: mom[k] for k in _SHARDED_SMALL},
                         {k: var[k] for k in _SHARDED_SMALL})
    for kind, d in zip(("grad", "delta", "new_m", "new_v"), [own] + outs):
        for k, a in d.items():
            result[kind, k] = a

    return (loss, dx[None], *[result[kind, k] for kind in ("grad", "delta", "new_m", "new_v") for k in _ORDER])
```

```python
import functools

import jax
import jax.numpy as jnp
from jax import lax
from jax.experimental import pallas as pl
from jax.experimental.pallas import tpu as pltpu

F32, BF16 = jnp.float32, jnp.bfloat16
DEPTH = 2
D = 1024
G = 256
NH, HD = 4, 64
FF = 2816
IN_COLS = 3352
P = 3584
EPS = 1e-6
SGU_C, DN_C, GLA_C = 128, 64, 64
N_DEV = 8
IN_SHARD = IN_COLS // N_DEV
GU_SHARD = 2 * FF // N_DEV
LANES = 128
VMEM_LIMIT = 56 * 2 ** 20

_PAD_SEGS = ((1280, 2048, 0),
             (2312, 3080, 0),
             (0, 512, 0),
             (512, 1280, 0),
             (2056, 2312, 0),
             (3096, 3352, 0),
             (2048, 2056, 120),
             (3080, 3096, 112))

ADAM_LR, ADAM_B1, ADAM_B2, ADAM_EPS, ADAM_WD, ADAM_STEP = 0.001, 0.9, 0.999, 1e-08, 0.01, 10


def _cparams(*sem):
    return pltpu.CompilerParams(dimension_semantics=sem, vmem_limit_bytes=VMEM_LIMIT)


def _resident(shape):
    return pl.BlockSpec(shape, lambda *_: (0,) * len(shape), pipeline_mode=pl.Buffered(1))


def _layer(shape, l):
    return pl.BlockSpec((None,) + tuple(shape), lambda *_: (l,) + (0,) * len(shape), pipeline_mode=pl.Buffered(1))


def _acc(shape):
    return pl.BlockSpec(shape, lambda *_: (0,) * len(shape))


def _dg(a, b, ca, cb):
    lead = a.ndim - 2
    batch = ((0,), (0,)) if lead else ((), ())
    return lax.dot_general(a, b, (((ca + lead,), (cb + lead,)), batch), preferred_element_type=F32)


def _split(t):
    hi = t.astype(BF16)
    return hi, (t - hi.astype(F32)).astype(BF16)


def _dg3(a, b, ca, cb):
    (ah, al), (bh, bl) = a, b
    return _dg(ah, bh, ca, cb) + (_dg(ah, bl, ca, cb) + _dg(al, bh, ca, cb))


def _make_dot(accurate):
    def raw(a, b, form):
        ca = 0 if form == "tn" else 1
        cb = 1 if form == "nt" else 0
        if accurate:
            return _dg3(_split(a), _split(b), ca, cb)
        return _dg(a.astype(BF16), b.astype(BF16), ca, cb)

    @functools.partial(jax.custom_vjp, nondiff_argnums=(2,))
    def dot(a, b, form):
        return raw(a, b, form)

    def fwd(a, b, form):
        return raw(a, b, form), (a, b)

    def bwd(form, res, g):
        a, b = res
        if form == "nn":
            return raw(g, b, "nt"), raw(a, g, "tn")
        if form == "nt":
            return raw(g, b, "nn"), raw(g, a, "tn")
        return raw(b, g, "nt"), raw(a, g, "nn")

    dot.defvjp(fwd, bwd)
    return dot


_bdot = _make_dot(False)
_hdot = _make_dot(True)


def _inv_unit_lower(low):
    n = low.shape[-1]
    eye = (lax.broadcasted_iota(jnp.int32, (n, n), 0) == lax.broadcasted_iota(jnp.int32, (n, n), 1)).astype(F32)
    p = -low
    inv = eye + p
    ps = _split(p)
    k = 1
    while 2 * k < n:
        ps = _split(_dg3(ps, ps, 1, 0))
        inv = inv + _dg3(_split(inv), ps, 1, 0)
        k *= 2
    return inv


@jax.custom_vjp
def _unit_lower_solve(low, rhs, inv):
    return _dg3(_split(inv), _split(rhs), 1, 0)


def _uls_fwd(low, rhs, inv):
    sol = _dg3(_split(inv), _split(rhs), 1, 0)
    return sol, (inv, sol)


def _uls_bwd(res, g):
    inv, sol = res
    d_rhs = _dg3(_split(inv), _split(g), 0, 0)
    return -_dg3(_split(d_rhs), _split(sol), 1, 1), d_rhs, jnp.zeros_like(inv)


_unit_lower_solve.defvjp(_uls_fwd, _uls_bwd)


def _sigmoid(x):
    return 0.5 * jnp.tanh(0.5 * x) + 0.5


def _softplus(x):
    return jnp.maximum(x, 0.0) + jnp.log(1.0 + jnp.exp(-jnp.maximum(x, -x)))


def _gelu(x):
    return 0.5 * x * (1.0 + jnp.tanh(0.7978845608028654 * (x + 0.044715 * (x * x * x))))


def _tri(n):
    ri = lax.broadcasted_iota(jnp.int32, (n, n), 0)
    ci = lax.broadcasted_iota(jnp.int32, (n, n), 1)
    return ri, ci


def _stack(ts):
    return jnp.concatenate([t[None] for t in ts], axis=0)


@jax.custom_vjp
def _head_sums(x):
    ri, ci = _tri(x.shape[-1])
    shift = HD.bit_length() - 1
    ones = (jnp.right_shift(ri, shift) == jnp.right_shift(ci, shift)).astype(BF16)
    hi, lo = _split(x)
    return _dg(hi, ones, 1, 0) + _dg(lo, ones, 1, 0)


_head_sums.defvjp(lambda x: (_head_sums(x), None), lambda _, g: (_head_sums(g),))


def _chunk_mask_dot(x, c, running, transpose):
    ri, ci = _tri(x.shape[0])
    shift = c.bit_length() - 1
    mask = jnp.right_shift(ri, shift) == jnp.right_shift(ci, shift)
    if running:
        mask = mask & (ri >= ci)
    hi, lo = _split(x)
    m = mask.astype(BF16)
    ca = 0 if transpose else 1
    return _dg(m, hi, ca, 0) + _dg(m, lo, ca, 0)


@functools.partial(jax.custom_vjp, nondiff_argnums=(1, 2))
def _chunk_sums(x, c, running):
    return _chunk_mask_dot(x, c, running, False)


_chunk_sums.defvjp(lambda x, c, running: (_chunk_mask_dot(x, c, running, False), None),
                   lambda c, running, _, g: (_chunk_mask_dot(g, c, running, True),))


def _spread_onto(x, first, transpose):
    ri = lax.broadcasted_iota(jnp.int32, (LANES, G), 0)
    ci = lax.broadcasted_iota(jnp.int32, (LANES, G), 1)
    sel = (ri == first + jnp.right_shift(ci, HD.bit_length() - 1)).astype(BF16)
    hi, lo = _split(x)
    cb = 1 if transpose else 0
    return _dg(hi, sel, 1, cb) + _dg(lo, sel, 1, cb)


@functools.partial(jax.custom_vjp, nondiff_argnums=(1,))
def _spread(x, first):
    return _spread_onto(x, first, False)


_spread.defvjp(lambda x, first: (_spread_onto(x, first, False), None),
               lambda first, _, g: (_spread_onto(g, first, True),))


def _head_columns_dot(x, transpose):
    sel = (lax.broadcasted_iota(jnp.int32, (NH, G), 0)
           == jnp.right_shift(lax.broadcasted_iota(jnp.int32, (NH, G), 1), HD.bit_length() - 1)).astype(BF16)
    hi, lo = _split(x)
    if transpose:
        return _dg(sel, hi, 1, 1) + _dg(sel, lo, 1, 1)
    return _dg(hi, sel, 0, 0) + _dg(lo, sel, 0, 0)


@jax.custom_vjp
def _head_columns(x):
    return _head_columns_dot(x, False)


_head_columns.defvjp(lambda x: (_head_columns_dot(x, False), None), lambda _, g: (_head_columns_dot(g, True),))


def _sgu_chunk(uv, ln_w, ln_b, ws, bs):
    u = _gelu(uv[:, :G])
    v = _gelu(uv[:, G:])
    mu = jnp.mean(v, axis=-1, keepdims=True)
    vc = v - mu
    var = jnp.mean(vc * vc, axis=-1, keepdims=True)
    vn = vc * lax.rsqrt(var + EPS) * ln_w + ln_b
    ri, ci = _tri(SGU_C)
    mixed = [_bdot(jnp.where(ri >= ci, ws[h], 0.0), vn[:, HD * h:HD * (h + 1)], "nn") for h in range(NH)]
    return u * (jnp.concatenate(mixed, axis=1) + _head_columns(bs))


def _dn_tile(q, k, v, ab, z, state, inv, a_log, dt_bias, nw):
    c = DN_C
    tm = q.shape[0]
    cps = tm // c
    ri, ci = _tri(tm)
    shift = c.bit_length() - 1
    same = jnp.right_shift(ri, shift) == jnp.right_shift(ci, shift)
    g4 = -jnp.exp(a_log) * _softplus(ab[:, 0:NH] + dt_bias)
    gates = jnp.concatenate([g4, _sigmoid(ab[:, NH:2 * NH]), jnp.zeros((tm, LANES - 2 * NH), F32)], axis=1)
    g, beta = _spread(gates, 0), _spread(gates, NH)
    gc = _chunk_sums(g, c, True)
    gl = _chunk_sums(g, c, False)
    gr4 = _hdot(g4, (same & (ri <= ci)).astype(F32), "tn")
    pairs = [(slice(c * j, c * (j + 1)), h) for j in range(cps) for h in range(NH)]
    heads = lambda t: _stack([t[rows, HD * h:HD * (h + 1)] for rows, h in pairs])
    qn = q * lax.rsqrt(_head_sums(q * q) + EPS) * (HD ** -0.5)
    kn = k * lax.rsqrt(_head_sums(k * k) + EPS)
    eg = jnp.exp(gc)
    kb = kn * beta
    rhs = jnp.concatenate([heads(v * beta), heads(kb * eg)], axis=2)
    qg, kd = heads(qn * eg), heads(kn * jnp.exp(gl - gc))
    cd = _stack([jnp.exp(gl[c * j:c * j + 1, HD * h:HD * (h + 1)]) for j in range(cps) for h in range(NH)])
    qn, kn, kb, gc = heads(qn), heads(kn), heads(kb), heads(gc)
    gr = _stack([gr4[h:h + 1, rows] for rows, h in pairs])
    r2, c2 = _tri(c)
    decay = jnp.exp(jnp.where(r2 >= c2, gc - gr, -jnp.inf))
    low = jnp.where(r2 > c2, _bdot(kb, kn, "nt") * decay, 0.0)
    if inv is None:
        inv = _inv_unit_lower(low)
    sol = _unit_lower_solve(low, rhs, inv)
    u, w = sol[:, :, :HD], sol[:, :, HD:]
    attn = _bdot(qn, kn, "nt") * decay
    ys = []
    for j in range(cps):
        b = slice(NH * j, NH * (j + 1))
        v_new = u[b] - _bdot(w[b], state, "nn")
        o = _bdot(qg[b], state, "nn") + _bdot(attn[b], v_new, "nn")
        state = state * cd[b] + _bdot(kd[b], v_new, "tn")
        on = o * lax.rsqrt(jnp.mean(o * o, axis=-1, keepdims=True) + EPS) * nw
        ys.append(jnp.concatenate([on[h] for h in range(NH)], axis=1))
    return jnp.concatenate(ys, axis=0) * (z * _sigmoid(z)), state, inv


def _gla_tile(q, k, v, glr, z, state_t, w2, gate_bias, nw):
    c = GLA_C
    tm = q.shape[0]
    cps = tm // c
    w2_rows = jnp.concatenate([w2, jnp.zeros((LANES - w2.shape[0], G), F32)], axis=0)
    pre = _bdot(glr, w2_rows, "nn") + gate_bias
    log_a = (jnp.minimum(pre, 0.0) - jnp.log(1.0 + jnp.exp(-jnp.maximum(pre, -pre)))) * (1.0 / 16.0)
    gcum = _chunk_sums(log_a, c, True)
    row = lax.broadcasted_iota(jnp.int32, (c, G), 0)
    qs = q * (HD ** -0.5)
    qa, ka, qg, kl, dec = [], [], [], [], []
    for j in range(cps):
        rows = slice(c * j, c * (j + 1))
        gj = gcum[rows]
        g_mid = jnp.sum(jnp.where(row == c // 2, gj, 0.0), axis=0, keepdims=True)
        g_last = jnp.sum(log_a[rows], axis=0, keepdims=True)
        qa.append(qs[rows] * jnp.exp(gj - g_mid))
        ka.append(k[rows] * jnp.exp(g_mid - gj))
        qg.append(qs[rows] * jnp.exp(gj))
        kl.append(k[rows] * jnp.exp(g_last - gj))
        dec.append(jnp.exp(g_last))
    heads = lambda ts: _stack([t[:, HD * h:HD * (h + 1)] for t in ts for h in range(NH)])
    qa, ka, qg, kl, dec = heads(qa), heads(ka), heads(qg), heads(kl), heads(dec)
    vh = heads([v[c * j:c * (j + 1)] for j in range(cps)])
    r2, c2 = _tri(c)
    o_intra = _bdot(jnp.where(r2 >= c2, _bdot(qa, ka, "nt"), 0.0), vh, "nn")
    ys = []
    for j in range(cps):
        b = slice(NH * j, NH * (j + 1))
        o = o_intra[b] + _bdot(qg[b], state_t, "nt")
        state_t = state_t * dec[b] + _bdot(vh[b], kl[b], "tn")
        on = o * lax.rsqrt(jnp.mean(o * o, axis=-1, keepdims=True) + EPS) * nw
        ys.append(jnp.concatenate([on[h] for h in range(NH)], axis=1))
    return jnp.concatenate(ys, axis=0) * (z * _sigmoid(z)), state_t


def _rms(x, nw):
    r = lax.rsqrt(jnp.mean(x * x, axis=-1, keepdims=True) + EPS)
    xh = x * r
    return xh * nw, xh, r


def _rms_bwd(g, xh, r, nw):
    gw = g * nw
    return r * (gw - xh * jnp.mean(gw * xh, axis=-1, keepdims=True)), jnp.sum(g * xh, axis=0, keepdims=True)


def _row(tm, w, blk=0):
    return pl.BlockSpec((tm, w), lambda i: (i, blk))


def _in_proj(x, nw, wp, l, tm=1024, rider=None):
    t = x.shape[0]
    tm = min(tm, t)

    def body(*refs):
        (x_ref, nw_ref, w_ref, h_ref, p_ref), ride = _split_refs(refs, 3, 2, 0, rider)
        _ride_begin(rider, ride, (t // tm,))
        h = _rms(x_ref[...], nw_ref[l:l + 1, :])[0].astype(BF16)
        h_ref[...] = h
        p_ref[...] = jnp.dot(h, w_ref[...], preferred_element_type=F32)
        _ride_end(rider, ride, (t // tm,))

    specs, operands = _host(
        rider, [_row(tm, D), _resident((DEPTH, D)), _resident((D, P))], [_row(tm, D), _row(tm, P)],
        [jax.ShapeDtypeStruct((t, D), BF16), jax.ShapeDtypeStruct((t, P), F32)], [], [x, nw, wp])
    return pl.pallas_call(body, name="in_proj", grid=(t // tm,), compiler_params=_cparams("arbitrary"),
                          **specs)(*operands)


def _gu_spec(l):
    return pl.BlockSpec((N_DEV, None, GU_SHARD, D), lambda *_: (0, l, 0, 0), pipeline_mode=pl.Buffered(1))


def _out_mlp(x, ys, w_out, nw2, w_gu_t, w_down, l, tm=256, head=None):
    t = x.shape[0]

    def body(*refs):
        x_ref, ya, yb, yc, yd, wo_ref, nw_ref, wgu_ref, wd_ref = refs[:9]
        if head is None:
            mix_ref, x1_ref, h2_ref, gu_ref, act_ref, x2_ref = refs[9:]
        else:
            fnw_ref, tg_ref, mix_ref, x1_ref, h2_ref, gu_ref, act_ref, dx_ref, loss_ref, dfnw_ref = refs[9:]
        mix = jnp.concatenate([ya[...], yb[...], yc[...], yd[...]], axis=1)
        mix_ref[...] = mix
        x1 = x_ref[...] + jnp.dot(mix, wo_ref[...], preferred_element_type=F32)
        x1_ref[...] = x1
        h2 = _rms(x1, nw_ref[l:l + 1, :])[0].astype(BF16)
        h2_ref[...] = h2
        gu = _dg(h2, wgu_ref[...].reshape(2 * FF, D), 1, 1)
        gu_ref[...] = gu.astype(BF16)
        gate, up = gu[:, :FF], gu[:, FF:]
        act = (gate * _sigmoid(gate) * up).astype(BF16)
        act_ref[...] = act
        x2 = x1 + jnp.dot(act, wd_ref[...], preferred_element_type=F32)
        if head is None:
            x2_ref[...] = x2
            return

        @pl.when(pl.program_id(0) == 0)
        def _():
            loss_ref[...] = jnp.zeros_like(loss_ref)
            dfnw_ref[...] = jnp.zeros_like(dfnw_ref)
        fnw = fnw_ref[...]
        y, xh, r = _rms(x2, fnw)
        err = y - tg_ref[...]
        loss_ref[...] += 0.5 * jnp.sum(err * err) * (1.0 / D)
        dx_ref[...], dfnw = _rms_bwd(err * (1.0 / D), xh, r, fnw)
        dfnw_ref[...] += dfnw

    in_specs = [_row(tm, D), _row(tm, G), _row(tm, G), _row(tm, G), _row(tm, G), _layer((D, D), l),
                _resident((DEPTH, D)), _gu_spec(l), _layer((FF, D), l)]
    out_specs = [_row(tm, D), _row(tm, D), _row(tm, D), _row(tm, 2 * FF), _row(tm, FF), _row(tm, D)]
    out_shape = [jax.ShapeDtypeStruct((t, D), BF16), jax.ShapeDtypeStruct((t, D), F32),
                 jax.ShapeDtypeStruct((t, D), BF16), jax.ShapeDtypeStruct((t, 2 * FF), BF16),
                 jax.ShapeDtypeStruct((t, FF), BF16), jax.ShapeDtypeStruct((t, D), F32)]
    operands = [x, *ys, w_out, nw2, w_gu_t, w_down]
    if head is not None:
        in_specs += [_resident((1, D)), _row(tm, D)]
        out_specs += [_acc((8, LANES)), _acc((1, D))]
        out_shape += [jax.ShapeDtypeStruct((8, LANES), F32), jax.ShapeDtypeStruct((1, D), F32)]
        operands += list(head)
    return pl.pallas_call(body, name="out_mlp", grid=(t // tm,), in_specs=in_specs, out_specs=out_specs,
                          out_shape=out_shape, compiler_params=_cparams("arbitrary"))(*operands)


def _mlp_out_bwd(dx2, x1, gu, nw2, w_down, w_gu, w_out, l, tm=256, rider=None):
    t = dx2.shape[0]

    def body(*refs):
        (dx2_ref, x1_ref, gu_ref, nw_ref, wd_ref, wgu_ref, wo_ref, dgu_ref, dx1_ref, dmix_ref, dnw_ref), ride = \
            _split_refs(refs, 7, 4, 0, rider)
        _ride_begin(rider, ride, (t // tm,))

        @pl.when(pl.program_id(0) == 0)
        def _():
            dnw_ref[...] = jnp.zeros_like(dnw_ref)
        dx2 = dx2_ref[...]
        dact = _dg(dx2.astype(BF16), wd_ref[...], 1, 1)
        gu = gu_ref[...].astype(F32)
        gate, up = gu[:, :FF], gu[:, FF:]
        s = _sigmoid(gate)
        dgu = jnp.concatenate([dact * up * (s * (1.0 + gate * (1.0 - s))), dact * (gate * s)], axis=1).astype(BF16)
        dgu_ref[...] = dgu
        dh2 = jnp.dot(dgu, wgu_ref[...].reshape(2 * FF, D), preferred_element_type=F32)
        nw_v = nw_ref[l:l + 1, :]
        _, xh, r = _rms(x1_ref[...], nw_v)
        dxn, dnw = _rms_bwd(dh2, xh, r, nw_v)
        dx1 = dx2 + dxn
        dx1_ref[...] = dx1
        dnw_ref[...] += dnw
        dmix_ref[...] = _dg(dx1.astype(BF16), wo_ref[...], 1, 1)
        _ride_end(rider, ride, (t // tm,))

    specs, operands = _host(
        rider, [_row(tm, D), _row(tm, D), _row(tm, 2 * FF), _resident((DEPTH, D)), _layer((FF, D), l), _gu_spec(l),
                _layer((D, D), l)],
        [_row(tm, 2 * FF), _row(tm, D), _row(tm, D), _acc((1, D))],
        [jax.ShapeDtypeStruct((t, 2 * FF), BF16), jax.ShapeDtypeStruct((t, D), F32),
         jax.ShapeDtypeStruct((t, D), F32), jax.ShapeDtypeStruct((1, D), F32)],
        [], [dx2, x1, gu, nw2, w_down, w_gu, w_out])
    return pl.pallas_call(body, name="mlp_out_bwd", grid=(t // tm,), compiler_params=_cparams("arbitrary"),
                          **specs)(*operands)


_DP_WIDTHS = (768, 768, 512, 768, 256, 256, 128, 128)


def _in_proj_bwd(dps, x, dx1, nw, wp, l, tm=512):
    t = x.shape[0]

    def body(*refs):
        dp_refs, (x_ref, dx1_ref, nw_ref, w_ref, dp_ref, dx_ref, dnw_ref) = refs[:8], refs[8:]

        @pl.when(pl.program_id(0) == 0)
        def _():
            dnw_ref[...] = jnp.zeros_like(dnw_ref)
        dp = jnp.concatenate([r[...] for r in dp_refs], axis=1)
        dp_ref[...] = dp
        dh = _dg(dp, w_ref[...], 1, 1)
        nw_v = nw_ref[l:l + 1, :]
        _, xh, r = _rms(x_ref[...], nw_v)
        dxn, dnw = _rms_bwd(dh, xh, r, nw_v)
        dx_ref[...] = dx1_ref[...] + dxn
        dnw_ref[...] += dnw

    return pl.pallas_call(
        body, name="in_proj_bwd", grid=(t // tm,),
        in_specs=[_row(tm, w) for w in _DP_WIDTHS] + [_row(tm, D), _row(tm, D), _resident((DEPTH, D)), _resident((D, P))],
        out_specs=[_row(tm, P), _row(tm, D), _acc((1, D))],
        out_shape=[jax.ShapeDtypeStruct((t, P), BF16), jax.ShapeDtypeStruct((t, D), F32),
                   jax.ShapeDtypeStruct((1, D), F32)],
        compiler_params=_cparams("arbitrary"))(*dps, x, dx1, nw, wp)


def _accumulate(acc, o_ref, t_axis, term):
    @pl.when(pl.program_id(t_axis) == 0)
    def _():
        acc[...] = jnp.zeros_like(acc)
    acc[...] += term

    @pl.when(pl.program_id(t_axis) == pl.num_programs(t_axis) - 1)
    def _():
        o_ref[...] = acc[...].astype(o_ref.dtype)


WGRAD_TOKENS = 2048


WGRAD_ROWS = 2 * GU_SHARD


def _wgrad_rows(a, b, name, tt=WGRAD_TOKENS, rows=WGRAD_ROWS, out_dtype=BF16, rider=None):
    t, m = a.shape
    tt = min(tt, t)
    grid = (m // rows, t // tt)

    def body(*refs):
        (a_ref, b_ref, o_ref, acc), ride = _split_refs(refs, 2, 1, 1, rider)
        _ride_begin(rider, ride, grid)
        _accumulate(acc, o_ref, 1, _dg(a_ref[...], b_ref[...].astype(BF16), 0, 0))
        _ride_end(rider, ride, grid)

    specs, operands = _host(
        rider, [pl.BlockSpec((tt, rows), lambda j, i: (i, j)), pl.BlockSpec((tt, D), lambda j, i: (i, 0))],
        [pl.BlockSpec((rows, D), lambda j, i: (j, 0))], [jax.ShapeDtypeStruct((m, D), out_dtype)],
        [pltpu.VMEM((rows, D), F32)], [a, b])
    out = pl.pallas_call(body, name=name, grid=grid, compiler_params=_cparams("arbitrary", "arbitrary"), **specs)(*operands)
    return out[0] if rider is None else out


def _relayout_w_in(g, tr=512):
    def body(w_ref, o_ref):
        full = jnp.concatenate([w_ref[d] for d in range(N_DEV)], axis=1)
        parts = []
        for a, b, z in _PAD_SEGS:
            parts.append(full[:, a:b])
            if z:
                parts.append(jnp.zeros((tr, z), full.dtype))
        o_ref[...] = jnp.concatenate(parts, axis=1)

    return pl.pallas_call(
        body, name="relayout_w_in", grid=(D // tr,),
        in_specs=[pl.BlockSpec((N_DEV, tr, IN_SHARD), lambda i: (0, i, 0))], out_specs=_row(tr, P),
        out_shape=jax.ShapeDtypeStruct((D, P), g.dtype), compiler_params=_cparams("parallel"))(g)


IN_ROWS = 432


def _scatter_w_in_grad(gp_t, tc=512):
    def body(g_ref, o_ref):
        g = g_ref[...]
        pieces, off = [], 0
        for a, b, z in _PAD_SEGS:
            pieces.append((a, g[off:off + (b - a)]))
            off += (b - a) + z
        spill = -(-(IN_SHARD * (N_DEV - 1) + IN_ROWS - IN_COLS) // 8) * 8
        nat = jnp.concatenate([piece for _, piece in sorted(pieces, key=lambda ap: ap[0])]
                              + [jnp.zeros((spill, tc), F32)], axis=0)
        for d in range(N_DEV):
            o_ref[d] = nat[IN_SHARD * d:IN_SHARD * d + IN_ROWS].astype(BF16)

    return pl.pallas_call(
        body, name="scatter_w_in_grad", grid=(D // tc,),
        in_specs=[pl.BlockSpec((P, tc), lambda i: (0, i))],
        out_specs=pl.BlockSpec((N_DEV, IN_ROWS, tc), lambda i: (0, 0, i)),
        out_shape=jax.ShapeDtypeStruct((N_DEV, IN_ROWS, D), BF16),
        compiler_params=_cparams("parallel"))(gp_t)


_A_BLK = 3


def _sgu_specs(l):
    return [_resident((DEPTH, G)), _resident((DEPTH, G)), _layer((NH, SGU_C, SGU_C), l), _layer((NH, SGU_C), l)]


def _sgu_fwd(p, ln_w, ln_b, ws, bs, l, tm=1024):
    t = p.shape[0]
    tm = min(tm, t)

    def body(uv_ref, lw_ref, lb_ref, ws_ref, bs_ref, y_ref):
        ws_v = [ws_ref[h] for h in range(NH)]
        for c in range(tm // SGU_C):
            rows = pl.ds(c * SGU_C, SGU_C)
            y_ref[rows, :] = _sgu_chunk(uv_ref[rows, :], lw_ref[l:l + 1, :], lb_ref[l:l + 1, :], ws_v,
                                        bs_ref[...]).astype(BF16)

    return pl.pallas_call(
        body, name="sgu_fwd", grid=(t // tm,),
        in_specs=[_row(tm, 2 * G, _A_BLK)] + _sgu_specs(l), out_specs=_row(tm, G),
        out_shape=jax.ShapeDtypeStruct((t, G), BF16),
        compiler_params=_cparams("parallel"))(p, ln_w, ln_b, ws, bs)


def _sgu_bwd(p, dmix, ln_w, ln_b, ws, bs, l, tm=1024):
    t = p.shape[0]
    tm = min(tm, t)

    def body(uv_ref, dy_ref, lw_ref, lb_ref, ws_ref, bs_ref, duv_ref, dlw_ref, dlb_ref, dws_ref, dbs_ref):
        @pl.when(pl.program_id(0) == 0)
        def _():
            for r in (dlw_ref, dlb_ref, dws_ref, dbs_ref):
                r[...] = jnp.zeros_like(r)
        ws_v = [ws_ref[h] for h in range(NH)]
        for c in range(tm // SGU_C):
            rows = pl.ds(c * SGU_C, SGU_C)
            _, vjp = jax.vjp(_sgu_chunk, uv_ref[rows, :], lw_ref[l:l + 1, :], lb_ref[l:l + 1, :], ws_v, bs_ref[...])
            duv, dlw, dlb, dws, dbs = vjp(dy_ref[rows, :])
            duv_ref[rows, :] = duv.astype(BF16)
            dlw_ref[...] += dlw
            dlb_ref[...] += dlb
            dbs_ref[...] += dbs
            for h in range(NH):
                dws_ref[h] += dws[h]

    return pl.pallas_call(
        body, name="sgu_bwd", grid=(t // tm,),
        in_specs=[_row(tm, 2 * G, _A_BLK), _row(tm, G, 0)] + _sgu_specs(l),
        out_specs=[_row(tm, 2 * G), _acc((1, G)), _acc((1, G)), _acc((NH, SGU_C, SGU_C)), _acc((NH, SGU_C))],
        out_shape=[jax.ShapeDtypeStruct((t, 2 * G), BF16), jax.ShapeDtypeStruct((1, G), F32),
                   jax.ShapeDtypeStruct((1, G), F32), jax.ShapeDtypeStruct((NH, SGU_C, SGU_C), F32),
                   jax.ShapeDtypeStruct((NH, SGU_C), F32)],
        compiler_params=_cparams("arbitrary"))(p, dmix, ln_w, ln_b, ws, bs)


HALO = 8


def _prev_halo(tm, width, col):
    return pl.BlockSpec((HALO, width), lambda i: (jnp.maximum(i * (tm // HALO) - 1, 0), col))


def _next_halo(tm, width, col, t):
    return pl.BlockSpec((HALO, width), lambda i: (jnp.minimum((i + 1) * (tm // HALO), t // HALO - 1), col))


def _with_prev(halo_ref, x_ref):
    halo = jnp.where(pl.program_id(0) == 0, 0.0, halo_ref[...])
    return jnp.concatenate([halo, x_ref[...]], axis=0)


def _with_next(x_ref, halo_ref):
    halo = jnp.where(pl.program_id(0) == pl.num_programs(0) - 1, 0.0, halo_ref[...])
    return jnp.concatenate([x_ref[...], halo], axis=0)


def _delay(ext, j):
    return ext if j == 0 else pltpu.roll(ext, j, axis=0)


def _advance(ext, j):
    return ext if j == 0 else pltpu.roll(ext, ext.shape[0] - j, axis=0)


def _causal_conv(ext, w, tm):
    kw = w.shape[0]
    out = None
    for k in range(kw):
        term = _delay(ext, kw - 1 - k)[HALO:HALO + tm] * w[k:k + 1, :]
        out = term if out is None else out + term
    return out


def _causal_conv_t(ext, w, tm):
    kw = w.shape[0]
    out = None
    for k in range(kw):
        term = _advance(ext, kw - 1 - k)[0:tm] * w[k:k + 1, :]
        out = term if out is None else out + term
    return out


def _conv_wgrad(ext, dy, kw, tm):
    return jnp.concatenate(
        [jnp.sum(_delay(ext, kw - 1 - k)[HALO:HALO + tm] * dy, axis=0, keepdims=True) for k in range(kw)], axis=0)


_B_GB, _B_GC, _B_H = 8, 9, 10
_C_QKV, _D_QKV = 0, 1
_C_Z, _D_Z = 11, 12
_C_AB, _D_GLR = 26, 27


def _sconv_fwd(p, w, l, tm=2048):
    t = p.shape[0]
    tm = min(tm, t)

    def body(gb_ref, gc_ref, h_ref, gc_halo, h_halo, w_ref, y_ref):
        s_ext = _with_prev(gc_halo, gc_ref) * _with_prev(h_halo, h_ref)
        y_ref[...] = (gb_ref[...] * _causal_conv(s_ext, w_ref[...], tm)).astype(BF16)

    return pl.pallas_call(
        body, name="sconv_fwd", grid=(t // tm,),
        in_specs=[_row(tm, G, _B_GB), _row(tm, G, _B_GC), _row(tm, G, _B_H), _prev_halo(tm, G, _B_GC),
                  _prev_halo(tm, G, _B_H), _layer((3, G), l)],
        out_specs=_row(tm, G), out_shape=jax.ShapeDtypeStruct((t, G), BF16),
        compiler_params=_cparams("parallel"))(p, p, p, p, p, w)


def _sconv_bwd(p, dmix, w, l, tm=2048):
    t = p.shape[0]
    tm = min(tm, t)

    def body(gb_ref, gc_ref, h_ref, gc_halo, h_halo, gb_next, dy_ref, dy_next, w_ref, dp_ref, dw_ref):
        @pl.when(pl.program_id(0) == 0)
        def _():
            dw_ref[...] = jnp.zeros_like(dw_ref)
        w_v = w_ref[...]
        s_ext = _with_prev(gc_halo, gc_ref) * _with_prev(h_halo, h_ref)
        dout = dy_ref[...]
        d_gb = dout * _causal_conv(s_ext, w_v, tm)
        dconv_ext = _with_next(dy_ref, dy_next) * _with_next(gb_ref, gb_next)
        ds = _causal_conv_t(dconv_ext, w_v, tm)
        dw_ref[...] += _conv_wgrad(s_ext, dconv_ext[0:tm], 3, tm)
        dp_ref[...] = jnp.concatenate([d_gb, ds * h_ref[...], ds * gc_ref[...]], axis=1).astype(BF16)

    return pl.pallas_call(
        body, name="sconv_bwd", grid=(t // tm,),
        in_specs=[_row(tm, G, _B_GB), _row(tm, G, _B_GC), _row(tm, G, _B_H), _prev_halo(tm, G, _B_GC),
                  _prev_halo(tm, G, _B_H), _next_halo(tm, G, _B_GB, t), _row(tm, G, 1), _next_halo(tm, G, 1, t),
                  _layer((3, G), l)],
        out_specs=[_row(tm, 3 * G), _acc((3, G))],
        out_shape=[jax.ShapeDtypeStruct((t, 3 * G), BF16), jax.ShapeDtypeStruct((3, G), F32)],
        compiler_params=_cparams("arbitrary"))(p, p, p, p, p, p, dmix, dmix, w)


def _qkv_conv_fwd(p, w, l, tm=2048):
    t = p.shape[0]
    tm = min(tm, t)

    def body(x_ref, x_halo, w_ref, y_ref):
        c = _causal_conv(_with_prev(x_halo, x_ref), w_ref[...], tm)
        y_ref[...] = c * _sigmoid(c)

    return pl.pallas_call(
        body, name="qkv_conv_fwd", grid=(t // tm,),
        in_specs=[_row(tm, 3 * G, _C_QKV), _prev_halo(tm, 3 * G, _C_QKV), _layer((4, 3 * G), l)],
        out_specs=_row(tm, 3 * G), out_shape=jax.ShapeDtypeStruct((t, 3 * G), F32),
        compiler_params=_cparams("parallel"))(p, p, w)


def _qkv_conv_bwd(p, dy, w, l, tm=1024):
    t = p.shape[0]
    tm = min(tm, t)

    def body(x_ref, x_halo, x_next, dy_ref, dy_next, w_ref, dx_ref, dw_ref):
        @pl.when(pl.program_id(0) == 0)
        def _():
            dw_ref[...] = jnp.zeros_like(dw_ref)
        w_v = w_ref[...]
        x_all = jnp.concatenate([_with_prev(x_halo, x_ref), jnp.where(
            pl.program_id(0) == pl.num_programs(0) - 1, 0.0, x_next[...])], axis=0)
        c = _causal_conv(x_all, w_v, tm + HALO)
        s = _sigmoid(c)
        dc_ext = _with_next(dy_ref, dy_next) * (s * (1.0 + c * (1.0 - s)))
        dx_ref[...] = _causal_conv_t(dc_ext, w_v, tm).astype(BF16)
        dw_ref[...] += _conv_wgrad(x_all[0:HALO + tm], dc_ext[0:tm], 4, tm)

    return pl.pallas_call(
        body, name="qkv_conv_bwd", grid=(t // tm,),
        in_specs=[_row(tm, 3 * G, _C_QKV), _prev_halo(tm, 3 * G, _C_QKV), _next_halo(tm, 3 * G, _C_QKV, t),
                  _row(tm, 3 * G), _next_halo(tm, 3 * G, 0, t), _layer((4, 3 * G), l)],
        out_specs=[_row(tm, 3 * G), _acc((4, 3 * G))],
        out_shape=[jax.ShapeDtypeStruct((t, 3 * G), BF16), jax.ShapeDtypeStruct((4, 3 * G), F32)],
        compiler_params=_cparams("arbitrary"))(p, p, p, dy, dy, w)


_STATE = pl.BlockSpec((1, NH, HD, HD), lambda i: (i, 0, 0, 0))


def _dn_specs():
    return [_resident((DEPTH, NH)), _resident((DEPTH, NH)), _resident((DEPTH, HD))]


def _dn_fwd(qkv, p, params, l, cps=8, rider=None):
    t = qkv.shape[0]
    tm = DN_C * cps
    nb = cps * NH

    def body(*refs):
        (qkv_ref, z_ref, ab_ref, alog_ref, dt_ref, nw_ref, y_ref, st_ref, inv_ref, state), ride = _split_refs(
            refs, 6, 3, 1, rider)
        _ride_begin(rider, ride, (t // tm,))

        @pl.when(pl.program_id(0) == 0)
        def _():
            state[...] = jnp.zeros_like(state)
        st_ref[0] = state[...]
        y, new, inv = _dn_tile(qkv_ref[:, 0:G], qkv_ref[:, G:2 * G], qkv_ref[:, 2 * G:3 * G], ab_ref[...], z_ref[...],
                               state[...], None, alog_ref[l:l + 1, :], dt_ref[l:l + 1, :], nw_ref[l:l + 1, :])
        y_ref[...] = y.astype(BF16)
        inv_ref[...] = inv
        state[...] = new
        _ride_end(rider, ride, (t // tm,))

    specs, operands = _host(
        rider, [_row(tm, 3 * G), _row(tm, G, _C_Z), _row(tm, LANES, _C_AB)] + _dn_specs(),
        [_row(tm, G), _STATE, pl.BlockSpec((nb, DN_C, DN_C), lambda i: (i, 0, 0))],
        [jax.ShapeDtypeStruct((t, G), BF16), jax.ShapeDtypeStruct((t // tm, NH, HD, HD), F32),
         jax.ShapeDtypeStruct((t // DN_C * NH, DN_C, DN_C), F32)],
        [pltpu.VMEM((NH, HD, HD), F32)], [qkv, p, p, *params])
    return pl.pallas_call(body, name="dn_fwd", grid=(t // tm,), compiler_params=_cparams("arbitrary"), **specs)(*operands)


def _dn_bwd(qkv, p, states, inv, dmix, params, l, cps=8, rider=None):
    t = qkv.shape[0]
    tm = DN_C * cps
    n = t // tm
    nb = cps * NH

    def body(*refs):
        (qkv_ref, z_ref, ab_ref, st_ref, inv_ref, dy_ref, alog_ref, dt_ref, nw_ref,
         dqkv_ref, dz_ref, dab_ref, dalog_ref, ddt_ref, dnw_ref, dstate), ride = _split_refs(refs, 9, 6, 1, rider)
        _ride_begin(rider, ride, (n,))
        dprm_refs = (dalog_ref, ddt_ref, dnw_ref)

        @pl.when(pl.program_id(0) == 0)
        def _():
            dstate[...] = jnp.zeros_like(dstate)
            for r in dprm_refs:
                r[...] = jnp.zeros_like(r)
        inv_v = inv_ref[...]
        tile = lambda q, k, v, ab, z, st, alog, dt, nw: _dn_tile(q, k, v, ab, z, st, inv_v, alog, dt, nw)[:2]
        _, vjp = jax.vjp(tile, qkv_ref[:, 0:G], qkv_ref[:, G:2 * G], qkv_ref[:, 2 * G:3 * G], ab_ref[...], z_ref[...],
                         st_ref[0], alog_ref[l:l + 1, :], dt_ref[l:l + 1, :], nw_ref[l:l + 1, :])
        dq, dk, dv, dab, dz, dst, *dprm = vjp((dy_ref[...], dstate[...]))
        dqkv_ref[...] = jnp.concatenate([dq, dk, dv], axis=1)
        dz_ref[...] = dz.astype(BF16)
        dab_ref[...] = dab.astype(BF16)
        dstate[...] = dst
        for r, g in zip(dprm_refs, dprm):
            r[...] += g
        _ride_end(rider, ride, (n,))

    rev = lambda w, blk: pl.BlockSpec((tm, w), lambda i: (n - 1 - i, blk))
    specs, operands = _host(
        rider, [rev(3 * G, 0), rev(G, _C_Z), rev(LANES, _C_AB),
                pl.BlockSpec((1, NH, HD, HD), lambda i: (n - 1 - i, 0, 0, 0)),
                pl.BlockSpec((nb, DN_C, DN_C), lambda i: (n - 1 - i, 0, 0)), rev(G, 2)] + _dn_specs(),
        [rev(3 * G, 0), rev(G, 0), rev(LANES, 0), _acc((1, NH)), _acc((1, NH)), _acc((1, HD))],
        [jax.ShapeDtypeStruct((t, 3 * G), F32), jax.ShapeDtypeStruct((t, G), BF16),
         jax.ShapeDtypeStruct((t, LANES), BF16), jax.ShapeDtypeStruct((1, NH), F32),
         jax.ShapeDtypeStruct((1, NH), F32), jax.ShapeDtypeStruct((1, HD), F32)],
        [pltpu.VMEM((NH, HD, HD), F32)], [qkv, p, p, states, inv, dmix, *params])
    return pl.pallas_call(body, name="dn_bwd", grid=(n,), compiler_params=_cparams("arbitrary"), **specs)(*operands)


def _gla_specs(l):
    return [_layer((16, G), l), _resident((DEPTH, G)), _resident((DEPTH, HD))]


def _gla_fwd(p, params, l, cps=8, rider=None):
    t = p.shape[0]
    tm = GLA_C * cps

    def body(*refs):
        (qkv_ref, z_ref, glr_ref, w2_ref, gb_ref, nw_ref, y_ref, st_ref, state), ride = _split_refs(refs, 6, 2, 1, rider)
        _ride_begin(rider, ride, (t // tm,))

        @pl.when(pl.program_id(0) == 0)
        def _():
            state[...] = jnp.zeros_like(state)
        st_ref[0] = state[...]
        y, new = _gla_tile(qkv_ref[:, 0:G], qkv_ref[:, G:2 * G], qkv_ref[:, 2 * G:3 * G], glr_ref[...], z_ref[...],
                           state[...], w2_ref[...], gb_ref[l:l + 1, :], nw_ref[l:l + 1, :])
        y_ref[...] = y.astype(BF16)
        state[...] = new
        _ride_end(rider, ride, (t // tm,))

    specs, operands = _host(
        rider, [_row(tm, 3 * G, _D_QKV), _row(tm, G, _D_Z), _row(tm, LANES, _D_GLR)] + _gla_specs(l),
        [_row(tm, G), _STATE],
        [jax.ShapeDtypeStruct((t, G), BF16), jax.ShapeDtypeStruct((t // tm, NH, HD, HD), F32)],
        [pltpu.VMEM((NH, HD, HD), F32)], [p, p, p, *params])
    return pl.pallas_call(body, name="gla_fwd", grid=(t // tm,), compiler_params=_cparams("arbitrary"), **specs)(*operands)


def _gla_bwd(p, states, dmix, params, l, cps=8):
    t = p.shape[0]
    tm = GLA_C * cps
    n = t // tm

    def body(qkv_ref, z_ref, glr_ref, st_ref, dy_ref, w2_ref, gb_ref, nw_ref,
             dqkv_ref, dz_ref, dglr_ref, dw2_ref, dgb_ref, dnw_ref, dstate):
        dprm_refs = (dw2_ref, dgb_ref, dnw_ref)

        @pl.when(pl.program_id(0) == 0)
        def _():
            dstate[...] = jnp.zeros_like(dstate)
            for r in dprm_refs:
                r[...] = jnp.zeros_like(r)
        _, vjp = jax.vjp(_gla_tile, qkv_ref[:, 0:G], qkv_ref[:, G:2 * G], qkv_ref[:, 2 * G:3 * G], glr_ref[...],
                         z_ref[...], st_ref[0], w2_ref[...], gb_ref[l:l + 1, :], nw_ref[l:l + 1, :])
        dq, dk, dv, dglr, dz, dst, *dprm = vjp((dy_ref[...], dstate[...]))
        dqkv_ref[...] = jnp.concatenate([dq, dk, dv], axis=1).astype(BF16)
        dz_ref[...] = dz.astype(BF16)
        dglr_ref[...] = dglr.astype(BF16)
        dstate[...] = dst
        for r, g in zip(dprm_refs, dprm):
            r[...] += g

    rev = lambda w, blk: pl.BlockSpec((tm, w), lambda i: (n - 1 - i, blk))
    return pl.pallas_call(
        body, name="gla_bwd", grid=(n,),
        in_specs=[rev(3 * G, _D_QKV), rev(G, _D_Z), rev(LANES, _D_GLR),
                  pl.BlockSpec((1, NH, HD, HD), lambda i: (n - 1 - i, 0, 0, 0)), rev(G, 3)] + _gla_specs(l),
        out_specs=[rev(3 * G, 0), rev(G, 0), rev(LANES, 0), _acc((16, G)), _acc((1, G)), _acc((1, HD))],
        out_shape=[jax.ShapeDtypeStruct((t, 3 * G), BF16), jax.ShapeDtypeStruct((t, G), BF16),
                   jax.ShapeDtypeStruct((t, LANES), BF16), jax.ShapeDtypeStruct((16, G), F32),
                   jax.ShapeDtypeStruct((1, G), F32), jax.ShapeDtypeStruct((1, HD), F32)],
        scratch_shapes=[pltpu.VMEM((NH, HD, HD), F32)],
        compiler_params=_cparams("arbitrary"))(p, p, p, states, dmix, *params)


def _adamw_math(w, g, m, v):
    m = ADAM_B1 * m + (1.0 - ADAM_B1) * g
    v = ADAM_B2 * v + (1.0 - ADAM_B2) * (g * g)
    m_hat = m / (1.0 - ADAM_B1 ** ADAM_STEP)
    v_hat = v / (1.0 - ADAM_B2 ** ADAM_STEP)
    return -ADAM_LR * (m_hat / (jnp.sqrt(v_hat) + ADAM_EPS) + ADAM_WD * w), m, v


def _adamw_large(parts, w, m, v, name, tr):
    _, r, c = w.shape

    def body(p0_ref, p1_ref, w_ref, m_ref, v_ref, g_ref, d_ref, nm_ref, nv_ref):
        def total(p_ref):
            g = p_ref[0].astype(F32)
            for k in range(1, N_DEV):
                g = g + p_ref[k].astype(F32)
            return g

        g = jnp.where(pl.program_id(0) == 0, total(p0_ref), total(p1_ref))
        g_ref[...] = g
        d_ref[...], nm_ref[...], nv_ref[...] = _adamw_math(w_ref[...], g, m_ref[...], v_ref[...])

    blk = pl.BlockSpec((None, tr, c), lambda l, i: (l, i, 0))
    part = pl.BlockSpec((N_DEV, tr, c), lambda l, i: (0, i, 0))
    return pl.pallas_call(
        body, name=name, grid=(DEPTH, r // tr), in_specs=[part, part, blk, blk, blk],
        out_specs=[blk] * 4, out_shape=[jax.ShapeDtypeStruct(w.shape, F32)] * 4,
        compiler_params=_cparams("parallel", "parallel"))(*parts, w, m, v)


def _adamw_w_in(parts, w, m, v, tc=512):
    def body(p0_ref, p1_ref, w_ref, m_ref, v_ref, g_ref, d_ref, nm_ref, nv_ref):
        for l, p_ref in enumerate((p0_ref, p1_ref)):
            g = p_ref[0, 0:IN_SHARD, :].astype(F32)
            for k in range(1, N_DEV):
                g = g + p_ref[k, 0:IN_SHARD, :].astype(F32)
            g_ref[:, l, :] = g
            d_ref[:, l, :], nm_ref[:, l, :], nv_ref[:, l, :] = _adamw_math(w_ref[:, l, :], g, m_ref[:, l, :], v_ref[:, l, :])

    blk = pl.BlockSpec((IN_SHARD, DEPTH, tc), lambda i: (0, 0, i))
    part = pl.BlockSpec((N_DEV, IN_ROWS, tc), lambda i: (0, 0, i))
    return pl.pallas_call(
        body, name="adamw_w_in", grid=(D // tc,), in_specs=[part, part, blk, blk, blk], out_specs=[blk] * 4,
        out_shape=[jax.ShapeDtypeStruct(w.shape, F32)] * 4, compiler_params=_cparams("parallel"))(*parts, w, m, v)


_SLAB_AT = {
    "sgu_w_spatial": (0, 0, SGU_C, LANES),
    "sgu_b_spatial": (128, 0, NH, SGU_C),
    "norm1_w": (132, 0, 1, D),
    "norm2_w": (133, 0, 1, D),
    "sgu_ln_w": (134, 0, 1, G),
    "sgu_ln_b": (134, 256, 1, G),
    "gla_gate_bias": (134, 512, 1, G),
    "dn_norm_w": (134, 768, 1, HD),
    "gla_norm_w": (134, 896, 1, HD),
    "dn_a_log": (135, 0, 1, NH),
    "dn_dt_bias": (135, 128, 1, NH),
    "gla_w_gate2": (136, 0, 16, G),
    "dn_conv_w": (136, 256, 4, 3 * G),
    "sc_conv_w": (140, 256, 3, G),
}
_LAYER_ROWS = 152
_FINAL_ROW = DEPTH * _LAYER_ROWS
_SLAB_ROWS, _SLAB_COLS = _FINAL_ROW + 8, 1024
_SLAB_ORDER = tuple(_SLAB_AT)
_SHARDED_SMALL = ("sc_conv_w", "dn_conv_w", "gla_w_gate2")
_REPLICATED = tuple(k for k in _SLAB_ORDER if k not in _SHARDED_SMALL)


def _region(name, l, h=0):
    r0, c0, nr, nc = _SLAB_AT[name]
    return slice(_LAYER_ROWS * l + r0, _LAYER_ROWS * l + r0 + nr), slice(c0 + nc * h, c0 + nc * (h + 1))


def _pack_small(layer_grads, d_final):
    def body(*refs):
        slab = refs[-1]
        slab[...] = jnp.zeros_like(slab)
        it = iter(refs[:-1])
        for l in range(DEPTH):
            for name in _SLAB_ORDER:
                ref = next(it)
                if name == "sgu_w_spatial":
                    for h in range(NH):
                        slab[_region(name, l, h)] = ref[h]
                else:
                    slab[_region(name, l)] = ref[...]
        slab[_FINAL_ROW:_FINAL_ROW + 1, :] = next(it)[...]

    flat = [layer_grads[l][name] for l in range(DEPTH) for name in _SLAB_ORDER] + [d_final]
    return pl.pallas_call(body, name="pack_small_grads", out_shape=jax.ShapeDtypeStruct((_SLAB_ROWS, _SLAB_COLS), F32),
                          compiler_params=_cparams())(*flat)


def _adamw_small(parts, w, m, v):
    names = _REPLICATED + ("final_norm_w",)
    n_in = len(names)

    def body(*refs):
        def total(rows, cols):
            g = refs[0][0, rows, cols]
            for k in range(1, N_DEV):
                g = g + refs[0][k, rows, cols]
            return g

        w_refs = dict(zip(names, refs[1:1 + n_in]))
        m_refs = dict(zip(names, refs[1 + n_in:1 + 2 * n_in]))
        v_refs = dict(zip(names, refs[1 + 2 * n_in:1 + 3 * n_in]))
        outs = refs[1 + 3 * n_in:]
        out_refs = [dict(zip(names, outs[j * n_in:(j + 1) * n_in])) for j in range(4)]
        full_refs = dict(zip(_SHARDED_SMALL, outs[4 * n_in:]))

        def update(name, idx, g):
            res = (g,) + _adamw_math(w_refs[name][idx], g, m_refs[name][idx], v_refs[name][idx])
            for o, val in zip(out_refs, res):
                o[name][idx] = val

        for l in range(DEPTH):
            for name in _REPLICATED:
                if name == "sgu_w_spatial":
                    for h in range(NH):
                        update(name, (l, h), total(*_region(name, l, h)))
                elif name == "sgu_b_spatial":
                    update(name, (l,), total(*_region(name, l)))
                else:
                    update(name, (slice(l, l + 1), slice(None)), total(*_region(name, l)))
            for name in _SHARDED_SMALL:
                full_refs[name][l] = total(*_region(name, l))
        update("final_norm_w", (slice(None), slice(None)), total(slice(_FINAL_ROW, _FINAL_ROW + 1), slice(None)))

    shapes = [jax.ShapeDtypeStruct(w[k].shape, F32) for k in names]
    full_shapes = [jax.ShapeDtypeStruct((DEPTH,) + _SLAB_AT[k][2:], F32) for k in _SHARDED_SMALL]
    outs = pl.pallas_call(body, name="adamw_small", out_shape=shapes * 4 + full_shapes, compiler_params=_cparams())(
        parts, *[w[k] for k in names], *[m[k] for k in names], *[v[k] for k in names])
    result = [dict(zip(names, outs[j * n_in:(j + 1) * n_in])) for j in range(4)]
    return result, dict(zip(_SHARDED_SMALL, outs[4 * n_in:]))


def _adamw_shards(g, w, m, v):
    names = _SHARDED_SMALL
    n = len(names)

    def body(*refs):
        for j in range(n):
            g_v = refs[j][...]
            res = _adamw_math(refs[n + j][...], g_v, refs[2 * n + j][...], refs[3 * n + j][...])
            for o, val in zip(refs[4 * n + j::n], res):
                o[...] = val

    shapes = [jax.ShapeDtypeStruct(w[k].shape, F32) for k in names]
    outs = pl.pallas_call(body, name="adamw_small_shards", out_shape=shapes * 3, compiler_params=_cparams())(
        *[g[k] for k in names], *[w[k] for k in names], *[m[k] for k in names], *[v[k] for k in names])
    return [dict(zip(names, outs[j * n:(j + 1) * n])) for j in range(3)]


_ANY = pl.BlockSpec(memory_space=pl.ANY)


def _place():
    return lax.axis_index("x"), lax.axis_index("y"), lax.axis_index("c")


def _sems(n):
    return [pltpu.SemaphoreType.DMA((n, 7)), pltpu.SemaphoreType.DMA((n, 7)), pltpu.SemaphoreType.DMA((n,))]


class _Gather:
    def __init__(self, blocks, second, forward_at=1.0):
        self.inputs, self.second, self.forward_at = list(blocks), list(second), forward_at
        self.out_shape = [jax.ShapeDtypeStruct((a.shape[0], N_DEV) + a.shape[1:] if s else (N_DEV,) + a.shape, a.dtype)
                          for a, s in zip(blocks, second)]
        self.scratch = _sems(len(blocks))

    def _copies(self, refs):
        x_refs, out_refs, (send_sems, recv_sems, local_sems) = refs
        n = len(x_refs)
        x, y, c = _place()
        me, sibling = (x, y, c), (x, y, 1 - c)
        chips = [(1 - x, y), (x, 1 - y), (1 - x, 1 - y)]

        def slot(j, px, py, pc):
            idx = 4 * px + 2 * py + pc
            return out_refs[j].at[:, idx] if self.second[j] else out_refs[j].at[idx]

        def copies(k, block, to, own=False):
            return [pltpu.make_async_remote_copy(
                src_ref=x_refs[j] if own else slot(j, *block), dst_ref=slot(j, *block),
                send_sem=send_sems.at[j, k], recv_sem=recv_sems.at[j, k], device_id=to,
                device_id_type=pl.DeviceIdType.MESH) for j in range(n)]

        mine = [pltpu.make_async_copy(x_refs[j], slot(j, *me), local_sems.at[j]) for j in range(n)]
        first = copies(0, me, sibling, own=True)
        for j, chip in enumerate(chips):
            first += copies(1 + j, me, (*chip, c), own=True)
        landed = [copies(1 + j, (*chip, c), me) for j, chip in enumerate(chips)]
        onward = [copies(4 + j, (*chip, c), sibling) for j, chip in enumerate(chips)]
        from_sibling = copies(0, sibling, me)
        for j, chip in enumerate(chips):
            from_sibling += copies(4 + j, (*chip, 1 - c), me)
        return mine, first, landed, onward, from_sibling

    def start(self, refs):
        mine, first, _, _, _ = self._copies(refs)
        for cp in mine + first:
            cp.start()

    def forward(self, refs):
        _, _, landed, onward, _ = self._copies(refs)
        for arrived, passed in zip(landed, onward):
            for cp in arrived:
                cp.wait_recv()
            for cp in passed:
                cp.start()

    def finish(self, refs):
        mine, first, _, onward, from_sibling = self._copies(refs)
        for cp in from_sibling:
            cp.wait_recv()
        for cp in first + [cp for passed in onward for cp in passed]:
            cp.wait_send()
        for cp in mine:
            cp.wait()


class _Exchange:
    forward_at = 1.0

    def __init__(self, sends):
        self.inputs = list(sends)
        self.out_shape = [jax.ShapeDtypeStruct(a.shape, a.dtype) for a in sends]
        self.scratch = _sems(len(sends))

    def _copies(self, refs):
        x_refs, out_refs, (send_sems, recv_sems, local_sems) = refs
        n = len(x_refs)
        x, y, c = _place()
        me = 4 * x + 2 * y + c
        sent, landing = [], []
        for k in range(1, N_DEV):
            px, py, pc = x ^ ((k >> 2) & 1), y ^ ((k >> 1) & 1), c ^ (k & 1)
            them = 4 * px + 2 * py + pc
            for j in range(n):
                for here, there, into in ((them, me, sent), (me, them, landing)):
                    into.append(pltpu.make_async_remote_copy(
                        src_ref=x_refs[j].at[here], dst_ref=out_refs[j].at[there], send_sem=send_sems.at[j, k - 1],
                        recv_sem=recv_sems.at[j, k - 1], device_id=(px, py, pc), device_id_type=pl.DeviceIdType.MESH))
        mine = [pltpu.make_async_copy(x_refs[j].at[me], out_refs[j].at[me], local_sems.at[j]) for j in range(n)]
        return mine, sent, landing

    def start(self, refs):
        mine, sent, _ = self._copies(refs)
        for cp in mine + sent:
            cp.start()

    def forward(self, refs):
        pass

    def finish(self, refs):
        mine, sent, landing = self._copies(refs)
        for cp in landing:
            cp.wait_recv()
        for cp in sent:
            cp.wait_send()
        for cp in mine:
            cp.wait()


def _run(rider, name):
    n_in, n_out = len(rider.inputs), len(rider.out_shape)

    def body(*refs):
        parts = (refs[:n_in], refs[n_in:n_in + n_out], refs[n_in + n_out:])
        rider.start(parts)
        rider.forward(parts)
        rider.finish(parts)

    return pl.pallas_call(body, name=name, out_shape=rider.out_shape, in_specs=[_ANY] * n_in, out_specs=[_ANY] * n_out,
                          scratch_shapes=rider.scratch)(*rider.inputs)


def _split_refs(refs, n_in, n_out, n_scratch, rider):
    r_in = len(rider.inputs) if rider else 0
    r_out = len(rider.out_shape) if rider else 0
    a = n_in + r_in
    b = a + n_out + r_out
    own = refs[:n_in] + refs[a:a + n_out] + refs[b:b + n_scratch]
    return own, (refs[n_in:a], refs[a + n_out:b], refs[b + n_scratch:])


def _grid_step(grid):
    step, stride = 0, 1
    for axis in reversed(range(len(grid))):
        step = step + pl.program_id(axis) * stride
        stride *= grid[axis]
    return step


def _ride_begin(rider, ride_refs, grid):
    if rider is not None:
        pl.when(_grid_step(grid) == 0)(lambda: rider.start(ride_refs))


def _ride_end(rider, ride_refs, grid):
    if rider is not None:
        steps = 1
        for n in grid:
            steps *= n
        step = _grid_step(grid)
        pl.when(step == min(steps - 1, int(steps * rider.forward_at)))(lambda: rider.forward(ride_refs))
        pl.when(step == steps - 1)(lambda: rider.finish(ride_refs))


def _host(rider, in_specs, out_specs, out_shape, scratch, operands):
    if rider is None:
        return dict(in_specs=in_specs, out_specs=out_specs, out_shape=out_shape, scratch_shapes=scratch), operands
    return dict(in_specs=in_specs + [_ANY] * len(rider.inputs), out_specs=out_specs + [_ANY] * len(rider.out_shape),
                out_shape=out_shape + rider.out_shape, scratch_shapes=scratch + rider.scratch), operands + rider.inputs


_LATE = ("w_gate_up", "w_out", "w_down")


def _local_grads(x, target, w, late=None, exchange=False):
    w = dict(w)
    w_in = list(w["w_in"])
    dn_params = (w["dn_a_log"], w["dn_dt_bias"], w["dn_norm_w"])
    gla_params = (w["gla_w_gate2"], w["gla_gate_bias"], w["gla_norm_w"])
    sgu_params = (w["sgu_ln_w"], w["sgu_ln_b"], w["sgu_w_spatial"], w["sgu_b_spatial"])
    saved = []
    for l in range(DEPTH):
        riding = l == 0 and late is not None
        h, p, *got = _in_proj(x, w["norm1_w"], w_in[l], l,
                              rider=_Gather([late["w_down"]], [True], forward_at=0.9) if riding else None)
        if riding:
            w["w_down"] = got[0].reshape(DEPTH, FF, D)
        ya = _sgu_fwd(p, *sgu_params, l)
        yb = _sconv_fwd(p, w["sc_conv_w"], l)
        qkv_c = _qkv_conv_fwd(p, w["dn_conv_w"], l)
        yc, st_c, inv_c, *got = _dn_fwd(
            qkv_c, p, dn_params, l,
            rider=_Gather([late["w_gate_up"], late["w_out"]], [False, True], forward_at=0.9) if riding else None)
        if riding:
            w.update(w_gate_up=got[0], w_out=got[1].reshape(DEPTH, D, D))
        yd, st_d, *got = _gla_fwd(p, gla_params, l,
                                  rider=_Gather([late["w_in"]], [False], forward_at=0.85) if riding else None)
        if riding:
            w_in[1] = _relayout_w_in(got[0])
        head = (w["final_norm_w"], target) if l == DEPTH - 1 else None
        mix, x1, h2, gu, act, x2, *tail = _out_mlp(x, (ya, yb, yc, yd), w["w_out"], w["norm2_w"], w["w_gate_up"],
                                                   w["w_down"], l, head=head)
        saved.append(dict(x=x, h=h, p=p, qkv_c=qkv_c, st_c=st_c, inv_c=inv_c, st_d=st_d, mix=mix, x1=x1, h2=h2, gu=gu,
                          act=act))
        x = x2
    dx, (loss, d_final) = x, tail
    large = {k: [None] * DEPTH for k in ("w_in", "w_out", "w_gate_up", "w_down")}
    small = [None] * DEPTH
    for l in reversed(range(DEPTH)):
        s = saved[l]
        p = s["p"]
        g = {}
        riding = exchange and l == 0

        def exchanging(pairs):
            return _Exchange([large[k][j] for k, j in pairs]) if riding else None

        def arrive(pairs, arrived):
            for (k, j), a in zip(pairs, arrived):
                large[k][j] = a

        pairs = (("w_gate_up", 1), ("w_down", 1))
        dgu, dx1, dmix, g["norm2_w"], *arrived = _mlp_out_bwd(
            dx, s["x1"], s["gu"], w["norm2_w"], w["w_down"], w["w_gate_up"], w["w_out"], l, rider=exchanging(pairs))
        arrive(pairs, arrived)
        pairs = (("w_in", 1), ("w_out", 1))
        d_gu, *arrived = _wgrad_rows(dgu, s["h2"], "wgrad_gate_up", rider=exchanging(pairs)) if riding else (
            _wgrad_rows(dgu, s["h2"], "wgrad_gate_up"),)
        arrive(pairs, arrived)
        large["w_gate_up"][l] = d_gu.reshape(N_DEV, GU_SHARD, D)
        large["w_down"][l] = _wgrad_rows(s["act"], dx, "wgrad_down").reshape(N_DEV, FF // N_DEV, D)
        large["w_out"][l] = _wgrad_rows(s["mix"], dx1, "wgrad_out", rows=D).reshape(N_DEV, D // N_DEV, D)
        d_a, g["sgu_ln_w"], g["sgu_ln_b"], g["sgu_w_spatial"], g["sgu_b_spatial"] = _sgu_bwd(p, dmix, *sgu_params, l)
        d_b, g["sc_conv_w"] = _sconv_bwd(p, dmix, w["sc_conv_w"], l)
        pairs = tuple((k, 0) for k in _LATE)
        dqkv_c, dz_c, dab, g["dn_a_log"], g["dn_dt_bias"], g["dn_norm_w"], *arrived = _dn_bwd(
            s["qkv_c"], p, s["st_c"], s["inv_c"], dmix, dn_params, l, rider=exchanging(pairs))
        arrive(pairs, arrived)
        d_c, g["dn_conv_w"] = _qkv_conv_bwd(p, dqkv_c, w["dn_conv_w"], l)
        d_d, dz_d, dglr, g["gla_w_gate2"], g["gla_gate_bias"], g["gla_norm_w"] = _gla_bwd(p, s["st_d"], dmix, gla_params, l)
        dp, dx, g["norm1_w"] = _in_proj_bwd((d_c, d_d, d_a, d_b, dz_c, dz_d, dab, dglr), s["x"], dx1, w["norm1_w"],
                                            w_in[l], l)
        small[l] = g
        w_in_rows = dict(rows=P // 2, out_dtype=F32)
        if riding:
            d_w_in, small = _wgrad_rows(dp, s["h"], "wgrad_in", **w_in_rows,
                                        rider=_Gather([_pack_small(small, d_final)], [False], forward_at=0.75))
        else:
            d_w_in = _wgrad_rows(dp, s["h"], "wgrad_in", **w_in_rows)
        large["w_in"][l] = _scatter_w_in_grad(d_w_in)
    return loss[0, 0], dx, large, small, d_final


_ORDER = ("norm1_w", "w_in", "sgu_ln_w", "sgu_ln_b", "sgu_w_spatial", "sgu_b_spatial", "sc_conv_w", "dn_conv_w",
          "dn_a_log", "dn_dt_bias", "dn_norm_w", "gla_w_gate2", "gla_gate_bias", "gla_norm_w", "w_out", "norm2_w",
          "w_gate_up", "w_down", "final_norm_w")
_LARGE = ("w_in", "w_gate_up", "w_out", "w_down")
_LARGE_TILE = {"w_gate_up": 352, "w_out": 128, "w_down": 352}


def _gather_cols(g):
    return jnp.transpose(g, (1, 2, 0, 3)).reshape(g.shape[1], g.shape[2], N_DEV * g.shape[3])


def kernel(x, norm1_w, w_in, sgu_ln_w, sgu_ln_b, sgu_w_spatial, sgu_b_spatial, sc_conv_w, dn_conv_w, dn_a_log, dn_dt_bias, dn_norm_w, gla_w_gate2, gla_gate_bias, gla_norm_w, w_out, norm2_w, w_gate_up, w_down, final_norm_w, loss_target, m_norm1_w, m_w_in, m_sgu_ln_w, m_sgu_ln_b, m_sgu_w_spatial, m_sgu_b_spatial, m_sc_conv_w, m_dn_conv_w, m_dn_a_log, m_dn_dt_bias, m_dn_norm_w, m_gla_w_gate2, m_gla_gate_bias, m_gla_norm_w, m_w_out, m_norm2_w, m_w_gate_up, m_w_down, m_final_norm_w, v_norm1_w, v_w_in, v_sgu_ln_w, v_sgu_ln_b, v_sgu_w_spatial, v_sgu_b_spatial, v_sc_conv_w, v_dn_conv_w, v_dn_a_log, v_dn_dt_bias, v_dn_norm_w, v_gla_w_gate2, v_gla_gate_bias, v_gla_norm_w, v_w_out, v_norm2_w, v_w_gate_up, v_w_down, v_final_norm_w):
    args = dict(locals())
    wts = {k: args[k] for k in _ORDER}
    mom = {k: args["m_" + k] for k in _ORDER}
    var = {k: args["v_" + k] for k in _ORDER}
    for d in (wts, mom, var):
        d["final_norm_w"] = d["final_norm_w"][None]
    me = 4 * lax.axis_index("x") + 2 * lax.axis_index("y") + lax.axis_index("c")
    for d in (wts, mom, var):
        d["w_gate_up"] = jnp.swapaxes(d["w_gate_up"], 1, 2)

    late = {k: wts[k].astype(BF16) for k in _LATE}
    w_in = wts["w_in"].astype(BF16)
    late["w_in"] = w_in[1]
    got = _run(_Gather([w_in[0]] + [wts[k] for k in _SHARDED_SMALL], [False] * 4), "gather_w_in")
    full = dict(wts)
    full["w_in"] = [_relayout_w_in(got[0]), None]
    for k, g in zip(_SHARDED_SMALL, got[1:]):
        full[k] = _gather_cols(g)

    loss, dx, large, small, d_final = _local_grads(x[0], loss_target[0], full, late=late, exchange=True)
    loss = lax.psum(loss, ("x", "y", "c"))

    large["w_in"][0], = _run(_Exchange([large["w_in"][0]]), "exchange_grads")
    result = {}
    for k in ("w_gate_up", "w_out", "w_down"):
        outs = _adamw_large(large[k], wts[k], mom[k], var[k], "adamw_" + k, _LARGE_TILE[k])
        for kind, a in zip(("grad", "delta", "new_m", "new_v"), outs):
            result[kind, k] = jnp.swapaxes(a, 1, 2) if k == "w_gate_up" else a
    outs = _adamw_w_in(large["w_in"], *[jnp.transpose(d["w_in"], (2, 0, 1)) for d in (wts, mom, var)])
    for kind, a in zip(("grad", "delta", "new_m", "new_v"), outs):
        result[kind, "w_in"] = jnp.transpose(a, (1, 2, 0))

    slabs = small
    rep = _REPLICATED + ("final_norm_w",)
    outs, summed = _adamw_small(slabs, {k: wts[k] for k in rep}, {k: mom[k] for k in rep}, {k: var[k] for k in rep})
    for kind, d in zip(("grad", "delta", "new_m", "new_v"), outs):
        for k, a in d.items():
            result[kind, k] = a[0] if k == "final_norm_w" else a
    own = {k: lax.dynamic_slice_in_dim(summed[k], me * wts[k].shape[-1], wts[k].shape[-1], axis=2) for k in _SHARDED_SMALL}
    outs = _adamw_shards(own, {k: wts[k] for k in _SHARDED_SMALL}, {k: mom[k] for k in _SHARDED_SMALL},
                         {k: var[k] for k in _SHARDED_SMALL})
    for kind, d in zip(("grad", "delta", "new_m", "new_v"), [own] + outs):
        for k, a in d.items():
            result[kind, k] = a

    return (loss, dx[None], *[result[kind, k] for kind in ("grad", "delta", "new_m", "new_v") for k in _ORDER])
```

```python
import functools

import jax
import jax.numpy as jnp
from jax import lax
from jax.experimental import pallas as pl
from jax.experimental.pallas import tpu as pltpu

F32, BF16 = jnp.float32, jnp.bfloat16
DEPTH = 2
D = 1024
G = 256
NH, HD = 4, 64
FF = 2816
IN_COLS = 3352
P = 3584
EPS = 1e-6
SGU_C, DN_C, GLA_C = 128, 64, 64
N_DEV = 8
IN_SHARD = IN_COLS // N_DEV
GU_SHARD = 2 * FF // N_DEV
LANES = 128
VMEM_LIMIT = 56 * 2 ** 20

_PAD_SEGS = ((1280, 2048, 0),
             (2312, 3080, 0),
             (0, 512, 0),
             (512, 1280, 0),
             (2056, 2312, 0),
             (3096, 3352, 0),
             (2048, 2056, 120),
             (3080, 3096, 112))

ADAM_LR, ADAM_B1, ADAM_B2, ADAM_EPS, ADAM_WD, ADAM_STEP = 0.001, 0.9, 0.999, 1e-08, 0.01, 10


def _cparams(*sem):
    return pltpu.CompilerParams(dimension_semantics=sem, vmem_limit_bytes=VMEM_LIMIT)


def _resident(shape):
    return pl.BlockSpec(shape, lambda *_: (0,) * len(shape), pipeline_mode=pl.Buffered(1))


def _layer(shape, l):
    return pl.BlockSpec((None,) + tuple(shape), lambda *_: (l,) + (0,) * len(shape), pipeline_mode=pl.Buffered(1))


def _acc(shape):
    return pl.BlockSpec(shape, lambda *_: (0,) * len(shape))


def _dg(a, b, ca, cb):
    lead = a.ndim - 2
    batch = ((0,), (0,)) if lead else ((), ())
    return lax.dot_general(a, b, (((ca + lead,), (cb + lead,)), batch), preferred_element_type=F32)


def _split(t):
    hi = t.astype(BF16)
    return hi, (t - hi.astype(F32)).astype(BF16)


def _dg3(a, b, ca, cb):
    (ah, al), (bh, bl) = a, b
    return _dg(ah, bh, ca, cb) + (_dg(ah, bl, ca, cb) + _dg(al, bh, ca, cb))


def _make_dot(accurate):
    def raw(a, b, form):
        ca = 0 if form == "tn" else 1
        cb = 1 if form == "nt" else 0
        if accurate:
            return _dg3(_split(a), _split(b), ca, cb)
        return _dg(a.astype(BF16), b.astype(BF16), ca, cb)

    @functools.partial(jax.custom_vjp, nondiff_argnums=(2,))
    def dot(a, b, form):
        return raw(a, b, form)

    def fwd(a, b, form):
        return raw(a, b, form), (a, b)

    def bwd(form, res, g):
        a, b = res
        if form == "nn":
            return raw(g, b, "nt"), raw(a, g, "tn")
        if form == "nt":
            return raw(g, b, "nn"), raw(g, a, "tn")
        return raw(b, g, "nt"), raw(a, g, "nn")

    dot.defvjp(fwd, bwd)
    return dot


_bdot = _make_dot(False)
_hdot = _make_dot(True)


def _inv_unit_lower(low):
    n = low.shape[-1]
    eye = (lax.broadcasted_iota(jnp.int32, (n, n), 0) == lax.broadcasted_iota(jnp.int32, (n, n), 1)).astype(F32)
    p = -low
    inv = eye + p
    ps = _split(p)
    k = 1
    while 2 * k < n:
        ps = _split(_dg3(ps, ps, 1, 0))
        inv = inv + _dg3(_split(inv), ps, 1, 0)
        k *= 2
    return inv


@jax.custom_vjp
def _unit_lower_solve(low, rhs, inv):
    return _dg3(_split(inv), _split(rhs), 1, 0)


def _uls_fwd(low, rhs, inv):
    sol = _dg3(_split(inv), _split(rhs), 1, 0)
    return sol, (inv, sol)


def _uls_bwd(res, g):
    inv, sol = res
    d_rhs = _dg3(_split(inv), _split(g), 0, 0)
    return -_dg3(_split(d_rhs), _split(sol), 1, 1), d_rhs, jnp.zeros_like(inv)


_unit_lower_solve.defvjp(_uls_fwd, _uls_bwd)


def _sigmoid(x):
    return 0.5 * jnp.tanh(0.5 * x) + 0.5


def _softplus(x):
    return jnp.maximum(x, 0.0) + jnp.log(1.0 + jnp.exp(-jnp.maximum(x, -x)))


def _gelu(x):
    return 0.5 * x * (1.0 + jnp.tanh(0.7978845608028654 * (x + 0.044715 * (x * x * x))))


def _tri(n):
    ri = lax.broadcasted_iota(jnp.int32, (n, n), 0)
    ci = lax.broadcasted_iota(jnp.int32, (n, n), 1)
    return ri, ci


def _stack(ts):
    return jnp.concatenate([t[None] for t in ts], axis=0)


@jax.custom_vjp
def _head_sums(x):
    ri, ci = _tri(x.shape[-1])
    shift = HD.bit_length() - 1
    ones = (jnp.right_shift(ri, shift) == jnp.right_shift(ci, shift)).astype(BF16)
    hi, lo = _split(x)
    return _dg(hi, ones, 1, 0) + _dg(lo, ones, 1, 0)


_head_sums.defvjp(lambda x: (_head_sums(x), None), lambda _, g: (_head_sums(g),))


def _chunk_mask_dot(x, c, running, transpose):
    ri, ci = _tri(x.shape[0])
    shift = c.bit_length() - 1
    mask = jnp.right_shift(ri, shift) == jnp.right_shift(ci, shift)
    if running:
        mask = mask & (ri >= ci)
    hi, lo = _split(x)
    m = mask.astype(BF16)
    ca = 0 if transpose else 1
    return _dg(m, hi, ca, 0) + _dg(m, lo, ca, 0)


@functools.partial(jax.custom_vjp, nondiff_argnums=(1, 2))
def _chunk_sums(x, c, running):
    return _chunk_mask_dot(x, c, running, False)


_chunk_sums.defvjp(lambda x, c, running: (_chunk_mask_dot(x, c, running, False), None),
                   lambda c, running, _, g: (_chunk_mask_dot(g, c, running, True),))


def _spread_onto(x, first, transpose):
    ri = lax.broadcasted_iota(jnp.int32, (LANES, G), 0)
    ci = lax.broadcasted_iota(jnp.int32, (LANES, G), 1)
    sel = (ri == first + jnp.right_shift(ci, HD.bit_length() - 1)).astype(BF16)
    hi, lo = _split(x)
    cb = 1 if transpose else 0
    return _dg(hi, sel, 1, cb) + _dg(lo, sel, 1, cb)


@functools.partial(jax.custom_vjp, nondiff_argnums=(1,))
def _spread(x, first):
    return _spread_onto(x, first, False)


_spread.defvjp(lambda x, first: (_spread_onto(x, first, False), None),
               lambda first, _, g: (_spread_onto(g, first, True),))


def _head_columns_dot(x, transpose):
    sel = (lax.broadcasted_iota(jnp.int32, (NH, G), 0)
           == jnp.right_shift(lax.broadcasted_iota(jnp.int32, (NH, G), 1), HD.bit_length() - 1)).astype(BF16)
    hi, lo = _split(x)
    if transpose:
        return _dg(sel, hi, 1, 1) + _dg(sel, lo, 1, 1)
    return _dg(hi, sel, 0, 0) + _dg(lo, sel, 0, 0)


@jax.custom_vjp
def _head_columns(x):
    return _head_columns_dot(x, False)


_head_columns.defvjp(lambda x: (_head_columns_dot(x, False), None), lambda _, g: (_head_columns_dot(g, True),))


def _sgu_chunk(uv, ln_w, ln_b, ws, bs):
    u = _gelu(uv[:, :G])
    v = _gelu(uv[:, G:])
    mu = jnp.mean(v, axis=-1, keepdims=True)
    vc = v - mu
    var = jnp.mean(vc * vc, axis=-1, keepdims=True)
    vn = vc * lax.rsqrt(var + EPS) * ln_w + ln_b
    ri, ci = _tri(SGU_C)
    mixed = [_bdot(jnp.where(ri >= ci, ws[h], 0.0), vn[:, HD * h:HD * (h + 1)], "nn") for h in range(NH)]
    return u * (jnp.concatenate(mixed, axis=1) + _head_columns(bs))


def _dn_tile(q, k, v, ab, z, state, inv, a_log, dt_bias, nw):
    c = DN_C
    tm = q.shape[0]
    cps = tm // c
    ri, ci = _tri(tm)
    shift = c.bit_length() - 1
    same = jnp.right_shift(ri, shift) == jnp.right_shift(ci, shift)
    g4 = -jnp.exp(a_log) * _softplus(ab[:, 0:NH] + dt_bias)
    gates = jnp.concatenate([g4, _sigmoid(ab[:, NH:2 * NH]), jnp.zeros((tm, LANES - 2 * NH), F32)], axis=1)
    g, beta = _spread(gates, 0), _spread(gates, NH)
    gc = _chunk_sums(g, c, True)
    gl = _chunk_sums(g, c, False)
    gr4 = _hdot(g4, (same & (ri <= ci)).astype(F32), "tn")
    pairs = [(slice(c * j, c * (j + 1)), h) for j in range(cps) for h in range(NH)]
    heads = lambda t: _stack([t[rows, HD * h:HD * (h + 1)] for rows, h in pairs])
    qn = q * lax.rsqrt(_head_sums(q * q) + EPS) * (HD ** -0.5)
    kn = k * lax.rsqrt(_head_sums(k * k) + EPS)
    eg = jnp.exp(gc)
    kb = kn * beta
    rhs = jnp.concatenate([heads(v * beta), heads(kb * eg)], axis=2)
    qg, kd = heads(qn * eg), heads(kn * jnp.exp(gl - gc))
    cd = _stack([jnp.exp(gl[c * j:c * j + 1, HD * h:HD * (h + 1)]) for j in range(cps) for h in range(NH)])
    qn, kn, kb, gc = heads(qn), heads(kn), heads(kb), heads(gc)
    gr = _stack([gr4[h:h + 1, rows] for rows, h in pairs])
    r2, c2 = _tri(c)
    decay = jnp.exp(jnp.where(r2 >= c2, gc - gr, -jnp.inf))
    low = jnp.where(r2 > c2, _bdot(kb, kn, "nt") * decay, 0.0)
    if inv is None:
        inv = _inv_unit_lower(low)
    sol = _unit_lower_solve(low, rhs, inv)
    u, w = sol[:, :, :HD], sol[:, :, HD:]
    attn = _bdot(qn, kn, "nt") * decay
    ys = []
    for j in range(cps):
        b = slice(NH * j, NH * (j + 1))
        v_new = u[b] - _bdot(w[b], state, "nn")
        o = _bdot(qg[b], state, "nn") + _bdot(attn[b], v_new, "nn")
        state = state * cd[b] + _bdot(kd[b], v_new, "tn")
        on = o * lax.rsqrt(jnp.mean(o * o, axis=-1, keepdims=True) + EPS) * nw
        ys.append(jnp.concatenate([on[h] for h in range(NH)], axis=1))
    return jnp.concatenate(ys, axis=0) * (z * _sigmoid(z)), state, inv


def _gla_tile(q, k, v, glr, z, state_t, w2, gate_bias, nw):
    c = GLA_C
    tm = q.shape[0]
    cps = tm // c
    w2_rows = jnp.concatenate([w2, jnp.zeros((LANES - w2.shape[0], G), F32)], axis=0)
    pre = _bdot(glr, w2_rows, "nn") + gate_bias
    log_a = (jnp.minimum(pre, 0.0) - jnp.log(1.0 + jnp.exp(-jnp.maximum(pre, -pre)))) * (1.0 / 16.0)
    gcum = _chunk_sums(log_a, c, True)
    row = lax.broadcasted_iota(jnp.int32, (c, G), 0)
    qs = q * (HD ** -0.5)
    qa, ka, qg, kl, dec = [], [], [], [], []
    for j in range(cps):
        rows = slice(c * j, c * (j + 1))
        gj = gcum[rows]
        g_mid = jnp.sum(jnp.where(row == c // 2, gj, 0.0), axis=0, keepdims=True)
        g_last = jnp.sum(log_a[rows], axis=0, keepdims=True)
        qa.append(qs[rows] * jnp.exp(gj - g_mid))
        ka.append(k[rows] * jnp.exp(g_mid - gj))
        qg.append(qs[rows] * jnp.exp(gj))
        kl.append(k[rows] * jnp.exp(g_last - gj))
        dec.append(jnp.exp(g_last))
    heads = lambda ts: _stack([t[:, HD * h:HD * (h + 1)] for t in ts for h in range(NH)])
    qa, ka, qg, kl, dec = heads(qa), heads(ka), heads(qg), heads(kl), heads(dec)
    vh = heads([v[c * j:c * (j + 1)] for j in range(cps)])
    r2, c2 = _tri(c)
    o_intra = _bdot(jnp.where(r2 >= c2, _bdot(qa, ka, "nt"), 0.0), vh, "nn")
    ys = []
    for j in range(cps):
        b = slice(NH * j, NH * (j + 1))
        o = o_intra[b] + _bdot(qg[b], state_t, "nt")
        state_t = state_t * dec[b] + _bdot(vh[b], kl[b], "tn")
        on = o * lax.rsqrt(jnp.mean(o * o, axis=-1, keepdims=True) + EPS) * nw
        ys.append(jnp.concatenate([on[h] for h in range(NH)], axis=1))
    return jnp.concatenate(ys, axis=0) * (z * _sigmoid(z)), state_t


def _rms(x, nw):
    r = lax.rsqrt(jnp.mean(x * x, axis=-1, keepdims=True) + EPS)
    xh = x * r
    return xh * nw, xh, r


def _rms_bwd(g, xh, r, nw):
    gw = g * nw
    return r * (gw - xh * jnp.mean(gw * xh, axis=-1, keepdims=True)), jnp.sum(g * xh, axis=0, keepdims=True)


def _row(tm, w, blk=0):
    return pl.BlockSpec((tm, w), lambda i: (i, blk))


def _in_proj(x, nw, wp, l, tm=1024, rider=None):
    t = x.shape[0]
    tm = min(tm, t)

    def body(*refs):
        (x_ref, nw_ref, w_ref, h_ref, p_ref), ride = _split_refs(refs, 3, 2, 0, rider)
        _ride_begin(rider, ride, (t // tm,))
        h = _rms(x_ref[...], nw_ref[l:l + 1, :])[0].astype(BF16)
        h_ref[...] = h
        p_ref[...] = jnp.dot(h, w_ref[...], preferred_element_type=F32)
        _ride_end(rider, ride, (t // tm,))

    specs, operands = _host(
        rider, [_row(tm, D), _resident((DEPTH, D)), _resident((D, P))], [_row(tm, D), _row(tm, P)],
        [jax.ShapeDtypeStruct((t, D), BF16), jax.ShapeDtypeStruct((t, P), F32)], [], [x, nw, wp])
    return pl.pallas_call(body, name="in_proj", grid=(t // tm,), compiler_params=_cparams("arbitrary"),
                          **specs)(*operands)


def _gu_spec(l):
    return pl.BlockSpec((N_DEV, None, GU_SHARD, D), lambda *_: (0, l, 0, 0), pipeline_mode=pl.Buffered(1))


def _out_mlp(x, ys, w_out, nw2, w_gu_t, w_down, l, tm=256, head=None):
    t = x.shape[0]

    def body(*refs):
        x_ref, ya, yb, yc, yd, wo_ref, nw_ref, wgu_ref, wd_ref = refs[:9]
        if head is None:
            mix_ref, x1_ref, h2_ref, gu_ref, act_ref, x2_ref = refs[9:]
        else:
            fnw_ref, tg_ref, mix_ref, x1_ref, h2_ref, gu_ref, act_ref, dx_ref, loss_ref, dfnw_ref = refs[9:]
        mix = jnp.concatenate([ya[...], yb[...], yc[...], yd[...]], axis=1)
        mix_ref[...] = mix
        x1 = x_ref[...] + jnp.dot(mix, wo_ref[...], preferred_element_type=F32)
        x1_ref[...] = x1
        h2 = _rms(x1, nw_ref[l:l + 1, :])[0].astype(BF16)
        h2_ref[...] = h2
        gu = _dg(h2, wgu_ref[...].reshape(2 * FF, D), 1, 1)
        gu_ref[...] = gu.astype(BF16)
        gate, up = gu[:, :FF], gu[:, FF:]
        act = (gate * _sigmoid(gate) * up).astype(BF16)
        act_ref[...] = act
        x2 = x1 + jnp.dot(act, wd_ref[...], preferred_element_type=F32)
        if head is None:
            x2_ref[...] = x2
            return

        @pl.when(pl.program_id(0) == 0)
        def _():
            loss_ref[...] = jnp.zeros_like(loss_ref)
            dfnw_ref[...] = jnp.zeros_like(dfnw_ref)
        fnw = fnw_ref[...]
        y, xh, r = _rms(x2, fnw)
        err = y - tg_ref[...]
        loss_ref[...] += 0.5 * jnp.sum(err * err) * (1.0 / D)
        dx_ref[...], dfnw = _rms_bwd(err * (1.0 / D), xh, r, fnw)
        dfnw_ref[...] += dfnw

    in_specs = [_row(tm, D), _row(tm, G), _row(tm, G), _row(tm, G), _row(tm, G), _layer((D, D), l),
                _resident((DEPTH, D)), _gu_spec(l), _layer((FF, D), l)]
    out_specs = [_row(tm, D), _row(tm, D), _row(tm, D), _row(tm, 2 * FF), _row(tm, FF), _row(tm, D)]
    out_shape = [jax.ShapeDtypeStruct((t, D), BF16), jax.ShapeDtypeStruct((t, D), F32),
                 jax.ShapeDtypeStruct((t, D), BF16), jax.ShapeDtypeStruct((t, 2 * FF), BF16),
                 jax.ShapeDtypeStruct((t, FF), BF16), jax.ShapeDtypeStruct((t, D), F32)]
    operands = [x, *ys, w_out, nw2, w_gu_t, w_down]
    if head is not None:
        in_specs += [_resident((1, D)), _row(tm, D)]
        out_specs += [_acc((8, LANES)), _acc((1, D))]
        out_shape += [jax.ShapeDtypeStruct((8, LANES), F32), jax.ShapeDtypeStruct((1, D), F32)]
        operands += list(head)
    return pl.pallas_call(body, name="out_mlp", grid=(t // tm,), in_specs=in_specs, out_specs=out_specs,
                          out_shape=out_shape, compiler_params=_cparams("arbitrary"))(*operands)


def _mlp_out_bwd(dx2, x1, gu, nw2, w_down, w_gu, w_out, l, tm=256, rider=None):
    t = dx2.shape[0]

    def body(*refs):
        (dx2_ref, x1_ref, gu_ref, nw_ref, wd_ref, wgu_ref, wo_ref, dgu_ref, dx1_ref, dmix_ref, dnw_ref), ride = \
            _split_refs(refs, 7, 4, 0, rider)
        _ride_begin(rider, ride, (t // tm,))

        @pl.when(pl.program_id(0) == 0)
        def _():
            dnw_ref[...] = jnp.zeros_like(dnw_ref)
        dx2 = dx2_ref[...]
        dact = _dg(dx2.astype(BF16), wd_ref[...], 1, 1)
        gu = gu_ref[...].astype(F32)
        gate, up = gu[:, :FF], gu[:, FF:]
        s = _sigmoid(gate)
        dgu = jnp.concatenate([dact * up * (s * (1.0 + gate * (1.0 - s))), dact * (gate * s)], axis=1).astype(BF16)
        dgu_ref[...] = dgu
        dh2 = jnp.dot(dgu, wgu_ref[...].reshape(2 * FF, D), preferred_element_type=F32)
        nw_v = nw_ref[l:l + 1, :]
        _, xh, r = _rms(x1_ref[...], nw_v)
        dxn, dnw = _rms_bwd(dh2, xh, r, nw_v)
        dx1 = dx2 + dxn
        dx1_ref[...] = dx1
        dnw_ref[...] += dnw
        dmix_ref[...] = _dg(dx1.astype(BF16), wo_ref[...], 1, 1)
        _ride_end(rider, ride, (t // tm,))

    specs, operands = _host(
        rider, [_row(tm, D), _row(tm, D), _row(tm, 2 * FF), _resident((DEPTH, D)), _layer((FF, D), l), _gu_spec(l),
                _layer((D, D), l)],
        [_row(tm, 2 * FF), _row(tm, D), _row(tm, D), _acc((1, D))],
        [jax.ShapeDtypeStruct((t, 2 * FF), BF16), jax.ShapeDtypeStruct((t, D), F32),
         jax.ShapeDtypeStruct((t, D), F32), jax.ShapeDtypeStruct((1, D), F32)],
        [], [dx2, x1, gu, nw2, w_down, w_gu, w_out])
    return pl.pallas_call(body, name="mlp_out_bwd", grid=(t // tm,), compiler_params=_cparams("arbitrary"),
                          **specs)(*operands)


_DP_WIDTHS = (768, 768, 512, 768, 256, 256, 128, 128)


def _in_proj_bwd(dps, x, dx1, nw, wp, l, tm=512):
    t = x.shape[0]

    def body(*refs):
        dp_refs, (x_ref, dx1_ref, nw_ref, w_ref, dp_ref, dx_ref, dnw_ref) = refs[:8], refs[8:]

        @pl.when(pl.program_id(0) == 0)
        def _():
            dnw_ref[...] = jnp.zeros_like(dnw_ref)
        dp = jnp.concatenate([r[...] for r in dp_refs], axis=1)
        dp_ref[...] = dp
        dh = _dg(dp, w_ref[...], 1, 1)
        nw_v = nw_ref[l:l + 1, :]
        _, xh, r = _rms(x_ref[...], nw_v)
        dxn, dnw = _rms_bwd(dh, xh, r, nw_v)
        dx_ref[...] = dx1_ref[...] + dxn
        dnw_ref[...] += dnw

    return pl.pallas_call(
        body, name="in_proj_bwd", grid=(t // tm,),
        in_specs=[_row(tm, w) for w in _DP_WIDTHS] + [_row(tm, D), _row(tm, D), _resident((DEPTH, D)), _resident((D, P))],
        out_specs=[_row(tm, P), _row(tm, D), _acc((1, D))],
        out_shape=[jax.ShapeDtypeStruct((t, P), BF16), jax.ShapeDtypeStruct((t, D), F32),
                   jax.ShapeDtypeStruct((1, D), F32)],
        compiler_params=_cparams("arbitrary"))(*dps, x, dx1, nw, wp)


def _accumulate(acc, o_ref, t_axis, term):
    @pl.when(pl.program_id(t_axis) == 0)
    def _():
        acc[...] = jnp.zeros_like(acc)
    acc[...] += term

    @pl.when(pl.program_id(t_axis) == pl.num_programs(t_axis) - 1)
    def _():
        o_ref[...] = acc[...].astype(o_ref.dtype)


WGRAD_TOKENS = 2048


WGRAD_ROWS = 2 * GU_SHARD


def _wgrad_rows(a, b, name, tt=WGRAD_TOKENS, rows=WGRAD_ROWS, out_dtype=BF16, rider=None, b_cols=(0, D)):
    t, m = a.shape
    tt = min(tt, t)
    grid = (m // rows, t // tt)
    b_blk, width = b_cols

    def body(*refs):
        (a_ref, b_ref, o_ref, acc), ride = _split_refs(refs, 2, 1, 1, rider)
        _ride_begin(rider, ride, grid)
        _accumulate(acc, o_ref, 1, _dg(a_ref[...], b_ref[...].astype(BF16), 0, 0))
        _ride_end(rider, ride, grid)

    specs, operands = _host(
        rider, [pl.BlockSpec((tt, rows), lambda j, i: (i, j)), pl.BlockSpec((tt, width), lambda j, i: (i, b_blk))],
        [pl.BlockSpec((rows, width), lambda j, i: (j, 0))], [jax.ShapeDtypeStruct((m, width), out_dtype)],
        [pltpu.VMEM((rows, width), F32)], [a, b])
    out = pl.pallas_call(body, name=name, grid=grid, compiler_params=_cparams("arbitrary", "arbitrary"), **specs)(*operands)
    return out[0] if rider is None else out


def _relayout_w_in(g, tr=512):
    def body(w_ref, o_ref):
        full = jnp.concatenate([w_ref[d] for d in range(N_DEV)], axis=1)
        parts = []
        for a, b, z in _PAD_SEGS:
            parts.append(full[:, a:b])
            if z:
                parts.append(jnp.zeros((tr, z), full.dtype))
        o_ref[...] = jnp.concatenate(parts, axis=1)

    return pl.pallas_call(
        body, name="relayout_w_in", grid=(D // tr,),
        in_specs=[pl.BlockSpec((N_DEV, tr, IN_SHARD), lambda i: (0, i, 0))], out_specs=_row(tr, P),
        out_shape=jax.ShapeDtypeStruct((D, P), g.dtype), compiler_params=_cparams("parallel"))(g)


IN_ROWS = 432


def _scatter_w_in_grad(gp_t, tc=512):
    def body(g_ref, o_ref):
        g = g_ref[...]
        pieces, off = [], 0
        for a, b, z in _PAD_SEGS:
            pieces.append((a, g[off:off + (b - a)]))
            off += (b - a) + z
        spill = -(-(IN_SHARD * (N_DEV - 1) + IN_ROWS - IN_COLS) // 8) * 8
        nat = jnp.concatenate([piece for _, piece in sorted(pieces, key=lambda ap: ap[0])]
                              + [jnp.zeros((spill, tc), F32)], axis=0)
        for d in range(N_DEV):
            o_ref[d] = nat[IN_SHARD * d:IN_SHARD * d + IN_ROWS].astype(BF16)

    return pl.pallas_call(
        body, name="scatter_w_in_grad", grid=(gp_t.shape[1] // tc,),
        in_specs=[pl.BlockSpec((P, tc), lambda i: (0, i))],
        out_specs=pl.BlockSpec((N_DEV, IN_ROWS, tc), lambda i: (0, 0, i)),
        out_shape=jax.ShapeDtypeStruct((N_DEV, IN_ROWS, gp_t.shape[1]), BF16),
        compiler_params=_cparams("parallel"))(gp_t)


_A_BLK = 3


def _sgu_specs(l):
    return [_resident((DEPTH, G)), _resident((DEPTH, G)), _layer((NH, SGU_C, SGU_C), l), _layer((NH, SGU_C), l)]


def _sgu_fwd(p, ln_w, ln_b, ws, bs, l, tm=1024):
    t = p.shape[0]
    tm = min(tm, t)

    def body(uv_ref, lw_ref, lb_ref, ws_ref, bs_ref, y_ref):
        ws_v = [ws_ref[h] for h in range(NH)]
        for c in range(tm // SGU_C):
            rows = pl.ds(c * SGU_C, SGU_C)
            y_ref[rows, :] = _sgu_chunk(uv_ref[rows, :], lw_ref[l:l + 1, :], lb_ref[l:l + 1, :], ws_v,
                                        bs_ref[...]).astype(BF16)

    return pl.pallas_call(
        body, name="sgu_fwd", grid=(t // tm,),
        in_specs=[_row(tm, 2 * G, _A_BLK)] + _sgu_specs(l), out_specs=_row(tm, G),
        out_shape=jax.ShapeDtypeStruct((t, G), BF16),
        compiler_params=_cparams("parallel"))(p, ln_w, ln_b, ws, bs)


def _sgu_bwd(p, dmix, ln_w, ln_b, ws, bs, l, tm=1024):
    t = p.shape[0]
    tm = min(tm, t)

    def body(uv_ref, dy_ref, lw_ref, lb_ref, ws_ref, bs_ref, duv_ref, dlw_ref, dlb_ref, dws_ref, dbs_ref):
        @pl.when(pl.program_id(0) == 0)
        def _():
            for r in (dlw_ref, dlb_ref, dws_ref, dbs_ref):
                r[...] = jnp.zeros_like(r)
        ws_v = [ws_ref[h] for h in range(NH)]
        for c in range(tm // SGU_C):
            rows = pl.ds(c * SGU_C, SGU_C)
            _, vjp = jax.vjp(_sgu_chunk, uv_ref[rows, :], lw_ref[l:l + 1, :], lb_ref[l:l + 1, :], ws_v, bs_ref[...])
            duv, dlw, dlb, dws, dbs = vjp(dy_ref[rows, :])
            duv_ref[rows, :] = duv.astype(BF16)
            dlw_ref[...] += dlw
            dlb_ref[...] += dlb
            dbs_ref[...] += dbs
            for h in range(NH):
                dws_ref[h] += dws[h]

    return pl.pallas_call(
        body, name="sgu_bwd", grid=(t // tm,),
        in_specs=[_row(tm, 2 * G, _A_BLK), _row(tm, G, 0)] + _sgu_specs(l),
        out_specs=[_row(tm, 2 * G), _acc((1, G)), _acc((1, G)), _acc((NH, SGU_C, SGU_C)), _acc((NH, SGU_C))],
        out_shape=[jax.ShapeDtypeStruct((t, 2 * G), BF16), jax.ShapeDtypeStruct((1, G), F32),
                   jax.ShapeDtypeStruct((1, G), F32), jax.ShapeDtypeStruct((NH, SGU_C, SGU_C), F32),
                   jax.ShapeDtypeStruct((NH, SGU_C), F32)],
        compiler_params=_cparams("arbitrary"))(p, dmix, ln_w, ln_b, ws, bs)


HALO = 8


def _prev_halo(tm, width, col):
    return pl.BlockSpec((HALO, width), lambda i: (jnp.maximum(i * (tm // HALO) - 1, 0), col))


def _next_halo(tm, width, col, t):
    return pl.BlockSpec((HALO, width), lambda i: (jnp.minimum((i + 1) * (tm // HALO), t // HALO - 1), col))


def _with_prev(halo_ref, x_ref):
    halo = jnp.where(pl.program_id(0) == 0, 0.0, halo_ref[...])
    return jnp.concatenate([halo, x_ref[...]], axis=0)


def _with_next(x_ref, halo_ref):
    halo = jnp.where(pl.program_id(0) == pl.num_programs(0) - 1, 0.0, halo_ref[...])
    return jnp.concatenate([x_ref[...], halo], axis=0)


def _delay(ext, j):
    return ext if j == 0 else pltpu.roll(ext, j, axis=0)


def _advance(ext, j):
    return ext if j == 0 else pltpu.roll(ext, ext.shape[0] - j, axis=0)


def _causal_conv(ext, w, tm):
    kw = w.shape[0]
    out = None
    for k in range(kw):
        term = _delay(ext, kw - 1 - k)[HALO:HALO + tm] * w[k:k + 1, :]
        out = term if out is None else out + term
    return out


def _causal_conv_t(ext, w, tm):
    kw = w.shape[0]
    out = None
    for k in range(kw):
        term = _advance(ext, kw - 1 - k)[0:tm] * w[k:k + 1, :]
        out = term if out is None else out + term
    return out


def _conv_wgrad(ext, dy, kw, tm):
    return jnp.concatenate(
        [jnp.sum(_delay(ext, kw - 1 - k)[HALO:HALO + tm] * dy, axis=0, keepdims=True) for k in range(kw)], axis=0)


_B_GB, _B_GC, _B_H = 8, 9, 10
_C_QKV, _D_QKV = 0, 1
_C_Z, _D_Z = 11, 12
_C_AB, _D_GLR = 26, 27


def _sconv_fwd(p, w, l, tm=2048):
    t = p.shape[0]
    tm = min(tm, t)

    def body(gb_ref, gc_ref, h_ref, gc_halo, h_halo, w_ref, y_ref):
        s_ext = _with_prev(gc_halo, gc_ref) * _with_prev(h_halo, h_ref)
        y_ref[...] = (gb_ref[...] * _causal_conv(s_ext, w_ref[...], tm)).astype(BF16)

    return pl.pallas_call(
        body, name="sconv_fwd", grid=(t // tm,),
        in_specs=[_row(tm, G, _B_GB), _row(tm, G, _B_GC), _row(tm, G, _B_H), _prev_halo(tm, G, _B_GC),
                  _prev_halo(tm, G, _B_H), _layer((3, G), l)],
        out_specs=_row(tm, G), out_shape=jax.ShapeDtypeStruct((t, G), BF16),
        compiler_params=_cparams("parallel"))(p, p, p, p, p, w)


def _sconv_bwd(p, dmix, w, l, tm=2048):
    t = p.shape[0]
    tm = min(tm, t)

    def body(gb_ref, gc_ref, h_ref, gc_halo, h_halo, gb_next, dy_ref, dy_next, w_ref, dp_ref, dw_ref):
        @pl.when(pl.program_id(0) == 0)
        def _():
            dw_ref[...] = jnp.zeros_like(dw_ref)
        w_v = w_ref[...]
        s_ext = _with_prev(gc_halo, gc_ref) * _with_prev(h_halo, h_ref)
        dout = dy_ref[...]
        d_gb = dout * _causal_conv(s_ext, w_v, tm)
        dconv_ext = _with_next(dy_ref, dy_next) * _with_next(gb_ref, gb_next)
        ds = _causal_conv_t(dconv_ext, w_v, tm)
        dw_ref[...] += _conv_wgrad(s_ext, dconv_ext[0:tm], 3, tm)
        dp_ref[...] = jnp.concatenate([d_gb, ds * h_ref[...], ds * gc_ref[...]], axis=1).astype(BF16)

    return pl.pallas_call(
        body, name="sconv_bwd", grid=(t // tm,),
        in_specs=[_row(tm, G, _B_GB), _row(tm, G, _B_GC), _row(tm, G, _B_H), _prev_halo(tm, G, _B_GC),
                  _prev_halo(tm, G, _B_H), _next_halo(tm, G, _B_GB, t), _row(tm, G, 1), _next_halo(tm, G, 1, t),
                  _layer((3, G), l)],
        out_specs=[_row(tm, 3 * G), _acc((3, G))],
        out_shape=[jax.ShapeDtypeStruct((t, 3 * G), BF16), jax.ShapeDtypeStruct((3, G), F32)],
        compiler_params=_cparams("arbitrary"))(p, p, p, p, p, p, dmix, dmix, w)


def _qkv_conv_fwd(p, w, l, tm=2048):
    t = p.shape[0]
    tm = min(tm, t)

    def body(x_ref, x_halo, w_ref, y_ref):
        c = _causal_conv(_with_prev(x_halo, x_ref), w_ref[...], tm)
        y_ref[...] = c * _sigmoid(c)

    return pl.pallas_call(
        body, name="qkv_conv_fwd", grid=(t // tm,),
        in_specs=[_row(tm, 3 * G, _C_QKV), _prev_halo(tm, 3 * G, _C_QKV), _layer((4, 3 * G), l)],
        out_specs=_row(tm, 3 * G), out_shape=jax.ShapeDtypeStruct((t, 3 * G), F32),
        compiler_params=_cparams("parallel"))(p, p, w)


def _qkv_conv_bwd(p, dy, w, l, tm=1024):
    t = p.shape[0]
    tm = min(tm, t)

    def body(x_ref, x_halo, x_next, dy_ref, dy_next, w_ref, dx_ref, dw_ref):
        @pl.when(pl.program_id(0) == 0)
        def _():
            dw_ref[...] = jnp.zeros_like(dw_ref)
        w_v = w_ref[...]
        x_all = jnp.concatenate([_with_prev(x_halo, x_ref), jnp.where(
            pl.program_id(0) == pl.num_programs(0) - 1, 0.0, x_next[...])], axis=0)
        c = _causal_conv(x_all, w_v, tm + HALO)
        s = _sigmoid(c)
        dc_ext = _with_next(dy_ref, dy_next) * (s * (1.0 + c * (1.0 - s)))
        dx_ref[...] = _causal_conv_t(dc_ext, w_v, tm).astype(BF16)
        dw_ref[...] += _conv_wgrad(x_all[0:HALO + tm], dc_ext[0:tm], 4, tm)

    return pl.pallas_call(
        body, name="qkv_conv_bwd", grid=(t // tm,),
        in_specs=[_row(tm, 3 * G, _C_QKV), _prev_halo(tm, 3 * G, _C_QKV), _next_halo(tm, 3 * G, _C_QKV, t),
                  _row(tm, 3 * G), _next_halo(tm, 3 * G, 0, t), _layer((4, 3 * G), l)],
        out_specs=[_row(tm, 3 * G), _acc((4, 3 * G))],
        out_shape=[jax.ShapeDtypeStruct((t, 3 * G), BF16), jax.ShapeDtypeStruct((4, 3 * G), F32)],
        compiler_params=_cparams("arbitrary"))(p, p, p, dy, dy, w)


_STATE = pl.BlockSpec((1, NH, HD, HD), lambda i: (i, 0, 0, 0))


def _dn_specs():
    return [_resident((DEPTH, NH)), _resident((DEPTH, NH)), _resident((DEPTH, HD))]


def _dn_fwd(qkv, p, params, l, cps=8, rider=None):
    t = qkv.shape[0]
    tm = DN_C * cps
    nb = cps * NH

    def body(*refs):
        (qkv_ref, z_ref, ab_ref, alog_ref, dt_ref, nw_ref, y_ref, st_ref, inv_ref, state), ride = _split_refs(
            refs, 6, 3, 1, rider)
        _ride_begin(rider, ride, (t // tm,))

        @pl.when(pl.program_id(0) == 0)
        def _():
            state[...] = jnp.zeros_like(state)
        st_ref[0] = state[...]
        y, new, inv = _dn_tile(qkv_ref[:, 0:G], qkv_ref[:, G:2 * G], qkv_ref[:, 2 * G:3 * G], ab_ref[...], z_ref[...],
                               state[...], None, alog_ref[l:l + 1, :], dt_ref[l:l + 1, :], nw_ref[l:l + 1, :])
        y_ref[...] = y.astype(BF16)
        inv_ref[...] = inv
        state[...] = new
        _ride_end(rider, ride, (t // tm,))

    specs, operands = _host(
        rider, [_row(tm, 3 * G), _row(tm, G, _C_Z), _row(tm, LANES, _C_AB)] + _dn_specs(),
        [_row(tm, G), _STATE, pl.BlockSpec((nb, DN_C, DN_C), lambda i: (i, 0, 0))],
        [jax.ShapeDtypeStruct((t, G), BF16), jax.ShapeDtypeStruct((t // tm, NH, HD, HD), F32),
         jax.ShapeDtypeStruct((t // DN_C * NH, DN_C, DN_C), F32)],
        [pltpu.VMEM((NH, HD, HD), F32)], [qkv, p, p, *params])
    return pl.pallas_call(body, name="dn_fwd", grid=(t // tm,), compiler_params=_cparams("arbitrary"), **specs)(*operands)


def _dn_bwd(qkv, p, states, inv, dmix, params, l, cps=8, rider=None):
    t = qkv.shape[0]
    tm = DN_C * cps
    n = t // tm
    nb = cps * NH

    def body(*refs):
        (qkv_ref, z_ref, ab_ref, st_ref, inv_ref, dy_ref, alog_ref, dt_ref, nw_ref,
         dqkv_ref, dz_ref, dab_ref, dalog_ref, ddt_ref, dnw_ref, dstate), ride = _split_refs(refs, 9, 6, 1, rider)
        _ride_begin(rider, ride, (n,))
        dprm_refs = (dalog_ref, ddt_ref, dnw_ref)

        @pl.when(pl.program_id(0) == 0)
        def _():
            dstate[...] = jnp.zeros_like(dstate)
            for r in dprm_refs:
                r[...] = jnp.zeros_like(r)
        inv_v = inv_ref[...]
        tile = lambda q, k, v, ab, z, st, alog, dt, nw: _dn_tile(q, k, v, ab, z, st, inv_v, alog, dt, nw)[:2]
        _, vjp = jax.vjp(tile, qkv_ref[:, 0:G], qkv_ref[:, G:2 * G], qkv_ref[:, 2 * G:3 * G], ab_ref[...], z_ref[...],
                         st_ref[0], alog_ref[l:l + 1, :], dt_ref[l:l + 1, :], nw_ref[l:l + 1, :])
        dq, dk, dv, dab, dz, dst, *dprm = vjp((dy_ref[...], dstate[...]))
        dqkv_ref[...] = jnp.concatenate([dq, dk, dv], axis=1)
        dz_ref[...] = dz.astype(BF16)
        dab_ref[...] = dab.astype(BF16)
        dstate[...] = dst
        for r, g in zip(dprm_refs, dprm):
            r[...] += g
        _ride_end(rider, ride, (n,))

    rev = lambda w, blk: pl.BlockSpec((tm, w), lambda i: (n - 1 - i, blk))
    specs, operands = _host(
        rider, [rev(3 * G, 0), rev(G, _C_Z), rev(LANES, _C_AB),
                pl.BlockSpec((1, NH, HD, HD), lambda i: (n - 1 - i, 0, 0, 0)),
                pl.BlockSpec((nb, DN_C, DN_C), lambda i: (n - 1 - i, 0, 0)), rev(G, 2)] + _dn_specs(),
        [rev(3 * G, 0), rev(G, 0), rev(LANES, 0), _acc((1, NH)), _acc((1, NH)), _acc((1, HD))],
        [jax.ShapeDtypeStruct((t, 3 * G), F32), jax.ShapeDtypeStruct((t, G), BF16),
         jax.ShapeDtypeStruct((t, LANES), BF16), jax.ShapeDtypeStruct((1, NH), F32),
         jax.ShapeDtypeStruct((1, NH), F32), jax.ShapeDtypeStruct((1, HD), F32)],
        [pltpu.VMEM((NH, HD, HD), F32)], [qkv, p, p, states, inv, dmix, *params])
    return pl.pallas_call(body, name="dn_bwd", grid=(n,), compiler_params=_cparams("arbitrary"), **specs)(*operands)


def _gla_specs(l):
    return [_layer((16, G), l), _resident((DEPTH, G)), _resident((DEPTH, HD))]


def _gla_fwd(p, params, l, cps=8, rider=None):
    t = p.shape[0]
    tm = GLA_C * cps

    def body(*refs):
        (qkv_ref, z_ref, glr_ref, w2_ref, gb_ref, nw_ref, y_ref, st_ref, state), ride = _split_refs(refs, 6, 2, 1, rider)
        _ride_begin(rider, ride, (t // tm,))

        @pl.when(pl.program_id(0) == 0)
        def _():
            state[...] = jnp.zeros_like(state)
        st_ref[0] = state[...]
        y, new = _gla_tile(qkv_ref[:, 0:G], qkv_ref[:, G:2 * G], qkv_ref[:, 2 * G:3 * G], glr_ref[...], z_ref[...],
                           state[...], w2_ref[...], gb_ref[l:l + 1, :], nw_ref[l:l + 1, :])
        y_ref[...] = y.astype(BF16)
        state[...] = new
        _ride_end(rider, ride, (t // tm,))

    specs, operands = _host(
        rider, [_row(tm, 3 * G, _D_QKV), _row(tm, G, _D_Z), _row(tm, LANES, _D_GLR)] + _gla_specs(l),
        [_row(tm, G), _STATE],
        [jax.ShapeDtypeStruct((t, G), BF16), jax.ShapeDtypeStruct((t // tm, NH, HD, HD), F32)],
        [pltpu.VMEM((NH, HD, HD), F32)], [p, p, p, *params])
    return pl.pallas_call(body, name="gla_fwd", grid=(t // tm,), compiler_params=_cparams("arbitrary"), **specs)(*operands)


def _gla_bwd(p, states, dmix, params, l, cps=8):
    t = p.shape[0]
    tm = GLA_C * cps
    n = t // tm

    def body(qkv_ref, z_ref, glr_ref, st_ref, dy_ref, w2_ref, gb_ref, nw_ref,
             dqkv_ref, dz_ref, dglr_ref, dw2_ref, dgb_ref, dnw_ref, dstate):
        dprm_refs = (dw2_ref, dgb_ref, dnw_ref)

        @pl.when(pl.program_id(0) == 0)
        def _():
            dstate[...] = jnp.zeros_like(dstate)
            for r in dprm_refs:
                r[...] = jnp.zeros_like(r)
        _, vjp = jax.vjp(_gla_tile, qkv_ref[:, 0:G], qkv_ref[:, G:2 * G], qkv_ref[:, 2 * G:3 * G], glr_ref[...],
                         z_ref[...], st_ref[0], w2_ref[...], gb_ref[l:l + 1, :], nw_ref[l:l + 1, :])
        dq, dk, dv, dglr, dz, dst, *dprm = vjp((dy_ref[...], dstate[...]))
        dqkv_ref[...] = jnp.concatenate([dq, dk, dv], axis=1).astype(BF16)
        dz_ref[...] = dz.astype(BF16)
        dglr_ref[...] = dglr.astype(BF16)
        dstate[...] = dst
        for r, g in zip(dprm_refs, dprm):
            r[...] += g

    rev = lambda w, blk: pl.BlockSpec((tm, w), lambda i: (n - 1 - i, blk))
    return pl.pallas_call(
        body, name="gla_bwd", grid=(n,),
        in_specs=[rev(3 * G, _D_QKV), rev(G, _D_Z), rev(LANES, _D_GLR),
                  pl.BlockSpec((1, NH, HD, HD), lambda i: (n - 1 - i, 0, 0, 0)), rev(G, 3)] + _gla_specs(l),
        out_specs=[rev(3 * G, 0), rev(G, 0), rev(LANES, 0), _acc((16, G)), _acc((1, G)), _acc((1, HD))],
        out_shape=[jax.ShapeDtypeStruct((t, 3 * G), BF16), jax.ShapeDtypeStruct((t, G), BF16),
                   jax.ShapeDtypeStruct((t, LANES), BF16), jax.ShapeDtypeStruct((16, G), F32),
                   jax.ShapeDtypeStruct((1, G), F32), jax.ShapeDtypeStruct((1, HD), F32)],
        scratch_shapes=[pltpu.VMEM((NH, HD, HD), F32)],
        compiler_params=_cparams("arbitrary"))(p, p, p, states, dmix, *params)


def _adamw_math(w, g, m, v):
    m = ADAM_B1 * m + (1.0 - ADAM_B1) * g
    v = ADAM_B2 * v + (1.0 - ADAM_B2) * (g * g)
    m_hat = m / (1.0 - ADAM_B1 ** ADAM_STEP)
    v_hat = v / (1.0 - ADAM_B2 ** ADAM_STEP)
    return -ADAM_LR * (m_hat / (jnp.sqrt(v_hat) + ADAM_EPS) + ADAM_WD * w), m, v


def _adamw_large(parts, w, m, v, name, tr):
    _, r, c = w.shape

    def body(p0_ref, p1_ref, w_ref, m_ref, v_ref, g_ref, d_ref, nm_ref, nv_ref):
        def total(p_ref):
            g = p_ref[0].astype(F32)
            for k in range(1, N_DEV):
                g = g + p_ref[k].astype(F32)
            return g

        g = jnp.where(pl.program_id(0) == 0, total(p0_ref), total(p1_ref))
        g_ref[...] = g
        d_ref[...], nm_ref[...], nv_ref[...] = _adamw_math(w_ref[...], g, m_ref[...], v_ref[...])

    blk = pl.BlockSpec((None, tr, c), lambda l, i: (l, i, 0))
    part = pl.BlockSpec((N_DEV, tr, c), lambda l, i: (0, i, 0))
    return pl.pallas_call(
        body, name=name, grid=(DEPTH, r // tr), in_specs=[part, part, blk, blk, blk],
        out_specs=[blk] * 4, out_shape=[jax.ShapeDtypeStruct(w.shape, F32)] * 4,
        compiler_params=_cparams("parallel", "parallel"))(*parts, w, m, v)


def _adamw_w_in(parts, w, m, v, tc=512):
    def body(p0_ref, p1_ref, w_ref, m_ref, v_ref, g_ref, d_ref, nm_ref, nv_ref):
        for l, p_ref in enumerate((p0_ref, p1_ref)):
            g = p_ref[0, 0:IN_SHARD, :].astype(F32)
            for k in range(1, N_DEV):
                g = g + p_ref[k, 0:IN_SHARD, :].astype(F32)
            g_ref[:, l, :] = g
            d_ref[:, l, :], nm_ref[:, l, :], nv_ref[:, l, :] = _adamw_math(w_ref[:, l, :], g, m_ref[:, l, :], v_ref[:, l, :])

    blk = pl.BlockSpec((IN_SHARD, DEPTH, tc), lambda i: (0, 0, i))
    part = pl.BlockSpec((N_DEV, IN_ROWS, tc), lambda i: (0, 0, i))
    return pl.pallas_call(
        body, name="adamw_w_in", grid=(D // tc,), in_specs=[part, part, blk, blk, blk], out_specs=[blk] * 4,
        out_shape=[jax.ShapeDtypeStruct(w.shape, F32)] * 4, compiler_params=_cparams("parallel"))(*parts, w, m, v)


_SLAB_AT = {
    "sgu_w_spatial": (0, 0, SGU_C, LANES),
    "sgu_b_spatial": (128, 0, NH, SGU_C),
    "norm1_w": (132, 0, 1, D),
    "norm2_w": (133, 0, 1, D),
    "sgu_ln_w": (134, 0, 1, G),
    "sgu_ln_b": (134, 256, 1, G),
    "gla_gate_bias": (134, 512, 1, G),
    "dn_norm_w": (134, 768, 1, HD),
    "gla_norm_w": (134, 896, 1, HD),
    "dn_a_log": (135, 0, 1, NH),
    "dn_dt_bias": (135, 128, 1, NH),
    "gla_w_gate2": (136, 0, 16, G),
    "dn_conv_w": (136, 256, 4, 3 * G),
    "sc_conv_w": (140, 256, 3, G),
}
_LAYER_ROWS = 152
_FINAL_ROW = DEPTH * _LAYER_ROWS
_SLAB_ROWS, _SLAB_COLS = _FINAL_ROW + 8, 1024
_SLAB_ORDER = tuple(_SLAB_AT)
_SHARDED_SMALL = ("sc_conv_w", "dn_conv_w", "gla_w_gate2")
_REPLICATED = tuple(k for k in _SLAB_ORDER if k not in _SHARDED_SMALL)


def _region(name, l, h=0):
    r0, c0, nr, nc = _SLAB_AT[name]
    return slice(_LAYER_ROWS * l + r0, _LAYER_ROWS * l + r0 + nr), slice(c0 + nc * h, c0 + nc * (h + 1))


def _pack_small(layer_grads, d_final):
    def body(*refs):
        slab = refs[-1]
        slab[...] = jnp.zeros_like(slab)
        it = iter(refs[:-1])
        for l in range(DEPTH):
            for name in _SLAB_ORDER:
                ref = next(it)
                if name == "sgu_w_spatial":
                    for h in range(NH):
                        slab[_region(name, l, h)] = ref[h]
                else:
                    slab[_region(name, l)] = ref[...]
        slab[_FINAL_ROW:_FINAL_ROW + 1, :] = next(it)[...]

    flat = [layer_grads[l][name] for l in range(DEPTH) for name in _SLAB_ORDER] + [d_final]
    return pl.pallas_call(body, name="pack_small_grads", out_shape=jax.ShapeDtypeStruct((_SLAB_ROWS, _SLAB_COLS), F32),
                          compiler_params=_cparams())(*flat)


def _adamw_small(parts, w, m, v):
    names = _REPLICATED + ("final_norm_w",)
    n_in = len(names)

    def body(*refs):
        def total(rows, cols):
            g = refs[0][0, rows, cols]
            for k in range(1, N_DEV):
                g = g + refs[0][k, rows, cols]
            return g

        w_refs = dict(zip(names, refs[1:1 + n_in]))
        m_refs = dict(zip(names, refs[1 + n_in:1 + 2 * n_in]))
        v_refs = dict(zip(names, refs[1 + 2 * n_in:1 + 3 * n_in]))
        outs = refs[1 + 3 * n_in:]
        out_refs = [dict(zip(names, outs[j * n_in:(j + 1) * n_in])) for j in range(4)]
        full_refs = dict(zip(_SHARDED_SMALL, outs[4 * n_in:]))

        def update(name, idx, g):
            res = (g,) + _adamw_math(w_refs[name][idx], g, m_refs[name][idx], v_refs[name][idx])
            for o, val in zip(out_refs, res):
                o[name][idx] = val

        for l in range(DEPTH):
            for name in _REPLICATED:
                if name == "sgu_w_spatial":
                    for h in range(NH):
                        update(name, (l, h), total(*_region(name, l, h)))
                elif name == "sgu_b_spatial":
                    update(name, (l,), total(*_region(name, l)))
                else:
                    update(name, (slice(l, l + 1), slice(None)), total(*_region(name, l)))
            for name in _SHARDED_SMALL:
                full_refs[name][l] = total(*_region(name, l))
        update("final_norm_w", (slice(None), slice(None)), total(slice(_FINAL_ROW, _FINAL_ROW + 1), slice(None)))

    shapes = [jax.ShapeDtypeStruct(w[k].shape, F32) for k in names]
    full_shapes = [jax.ShapeDtypeStruct((DEPTH,) + _SLAB_AT[k][2:], F32) for k in _SHARDED_SMALL]
    outs = pl.pallas_call(body, name="adamw_small", out_shape=shapes * 4 + full_shapes, compiler_params=_cparams())(
        parts, *[w[k] for k in names], *[m[k] for k in names], *[v[k] for k in names])
    result = [dict(zip(names, outs[j * n_in:(j + 1) * n_in])) for j in range(4)]
    return result, dict(zip(_SHARDED_SMALL, outs[4 * n_in:]))


def _adamw_shards(g, w, m, v):
    names = _SHARDED_SMALL
    n = len(names)

    def body(*refs):
        for j in range(n):
            g_v = refs[j][...]
            res = _adamw_math(refs[n + j][...], g_v, refs[2 * n + j][...], refs[3 * n + j][...])
            for o, val in zip(refs[4 * n + j::n], res):
                o[...] = val

    shapes = [jax.ShapeDtypeStruct(w[k].shape, F32) for k in names]
    outs = pl.pallas_call(body, name="adamw_small_shards", out_shape=shapes * 3, compiler_params=_cparams())(
        *[g[k] for k in names], *[w[k] for k in names], *[m[k] for k in names], *[v[k] for k in names])
    return [dict(zip(names, outs[j * n:(j + 1) * n])) for j in range(3)]


_ANY = pl.BlockSpec(memory_space=pl.ANY)


def _place():
    return lax.axis_index("x"), lax.axis_index("y"), lax.axis_index("c")


def _sems(n):
    return [pltpu.SemaphoreType.DMA((n, 7)), pltpu.SemaphoreType.DMA((n, 7)), pltpu.SemaphoreType.DMA((n,))]


class _Gather:
    def __init__(self, blocks, second, forward_at=1.0):
        self.inputs, self.second, self.forward_at = list(blocks), list(second), forward_at
        self.out_shape = [jax.ShapeDtypeStruct((a.shape[0], N_DEV) + a.shape[1:] if s else (N_DEV,) + a.shape, a.dtype)
                          for a, s in zip(blocks, second)]
        self.scratch = _sems(len(blocks))

    def _copies(self, refs):
        x_refs, out_refs, (send_sems, recv_sems, local_sems) = refs
        n = len(x_refs)
        x, y, c = _place()
        me, sibling = (x, y, c), (x, y, 1 - c)
        chips = [(1 - x, y), (x, 1 - y), (1 - x, 1 - y)]

        def slot(j, px, py, pc):
            idx = 4 * px + 2 * py + pc
            return out_refs[j].at[:, idx] if self.second[j] else out_refs[j].at[idx]

        def copies(k, block, to, own=False):
            return [pltpu.make_async_remote_copy(
                src_ref=x_refs[j] if own else slot(j, *block), dst_ref=slot(j, *block),
                send_sem=send_sems.at[j, k], recv_sem=recv_sems.at[j, k], device_id=to,
                device_id_type=pl.DeviceIdType.MESH) for j in range(n)]

        mine = [pltpu.make_async_copy(x_refs[j], slot(j, *me), local_sems.at[j]) for j in range(n)]
        first = copies(0, me, sibling, own=True)
        for j, chip in enumerate(chips):
            first += copies(1 + j, me, (*chip, c), own=True)
        landed = [copies(1 + j, (*chip, c), me) for j, chip in enumerate(chips)]
        onward = [copies(4 + j, (*chip, c), sibling) for j, chip in enumerate(chips)]
        from_sibling = copies(0, sibling, me)
        for j, chip in enumerate(chips):
            from_sibling += copies(4 + j, (*chip, 1 - c), me)
        return mine, first, landed, onward, from_sibling

    def start(self, refs):
        mine, first, _, _, _ = self._copies(refs)
        for cp in mine + first:
            cp.start()

    def forward(self, refs):
        _, _, landed, onward, _ = self._copies(refs)
        for arrived, passed in zip(landed, onward):
            for cp in arrived:
                cp.wait_recv()
            for cp in passed:
                cp.start()

    def finish(self, refs):
        mine, first, _, onward, from_sibling = self._copies(refs)
        for cp in from_sibling:
            cp.wait_recv()
        for cp in first + [cp for passed in onward for cp in passed]:
            cp.wait_send()
        for cp in mine:
            cp.wait()


class _Exchange:
    forward_at = 1.0

    def __init__(self, sends):
        self.inputs = list(sends)
        self.out_shape = [jax.ShapeDtypeStruct(a.shape, a.dtype) for a in sends]
        self.scratch = _sems(len(sends))

    def _copies(self, refs):
        x_refs, out_refs, (send_sems, recv_sems, local_sems) = refs
        n = len(x_refs)
        x, y, c = _place()
        me = 4 * x + 2 * y + c
        sent, landing = [], []
        for k in range(1, N_DEV):
            px, py, pc = x ^ ((k >> 2) & 1), y ^ ((k >> 1) & 1), c ^ (k & 1)
            them = 4 * px + 2 * py + pc
            for j in range(n):
                for here, there, into in ((them, me, sent), (me, them, landing)):
                    into.append(pltpu.make_async_remote_copy(
                        src_ref=x_refs[j].at[here], dst_ref=out_refs[j].at[there], send_sem=send_sems.at[j, k - 1],
                        recv_sem=recv_sems.at[j, k - 1], device_id=(px, py, pc), device_id_type=pl.DeviceIdType.MESH))
        mine = [pltpu.make_async_copy(x_refs[j].at[me], out_refs[j].at[me], local_sems.at[j]) for j in range(n)]
        return mine, sent, landing

    def start(self, refs):
        mine, sent, _ = self._copies(refs)
        for cp in mine + sent:
            cp.start()

    def forward(self, refs):
        pass

    def finish(self, refs):
        mine, sent, landing = self._copies(refs)
        for cp in landing:
            cp.wait_recv()
        for cp in sent:
            cp.wait_send()
        for cp in mine:
            cp.wait()


def _run(rider, name):
    n_in, n_out = len(rider.inputs), len(rider.out_shape)

    def body(*refs):
        parts = (refs[:n_in], refs[n_in:n_in + n_out], refs[n_in + n_out:])
        rider.start(parts)
        rider.forward(parts)
        rider.finish(parts)

    return pl.pallas_call(body, name=name, out_shape=rider.out_shape, in_specs=[_ANY] * n_in, out_specs=[_ANY] * n_out,
                          scratch_shapes=rider.scratch)(*rider.inputs)


def _split_refs(refs, n_in, n_out, n_scratch, rider):
    r_in = len(rider.inputs) if rider else 0
    r_out = len(rider.out_shape) if rider else 0
    a = n_in + r_in
    b = a + n_out + r_out
    own = refs[:n_in] + refs[a:a + n_out] + refs[b:b + n_scratch]
    return own, (refs[n_in:a], refs[a + n_out:b], refs[b + n_scratch:])


def _grid_step(grid):
    step, stride = 0, 1
    for axis in reversed(range(len(grid))):
        step = step + pl.program_id(axis) * stride
        stride *= grid[axis]
    return step


def _ride_begin(rider, ride_refs, grid):
    if rider is not None:
        pl.when(_grid_step(grid) == 0)(lambda: rider.start(ride_refs))


def _ride_end(rider, ride_refs, grid):
    if rider is not None:
        steps = 1
        for n in grid:
            steps *= n
        step = _grid_step(grid)
        pl.when(step == min(steps - 1, int(steps * rider.forward_at)))(lambda: rider.forward(ride_refs))
        pl.when(step == steps - 1)(lambda: rider.finish(ride_refs))


def _host(rider, in_specs, out_specs, out_shape, scratch, operands):
    if rider is None:
        return dict(in_specs=in_specs, out_specs=out_specs, out_shape=out_shape, scratch_shapes=scratch), operands
    return dict(in_specs=in_specs + [_ANY] * len(rider.inputs), out_specs=out_specs + [_ANY] * len(rider.out_shape),
                out_shape=out_shape + rider.out_shape, scratch_shapes=scratch + rider.scratch), operands + rider.inputs


_LATE = ("w_gate_up", "w_out", "w_down")


def _local_grads(x, target, w, late=None, exchange=False):
    w = dict(w)
    w_in = list(w["w_in"])
    dn_params = (w["dn_a_log"], w["dn_dt_bias"], w["dn_norm_w"])
    gla_params = (w["gla_w_gate2"], w["gla_gate_bias"], w["gla_norm_w"])
    sgu_params = (w["sgu_ln_w"], w["sgu_ln_b"], w["sgu_w_spatial"], w["sgu_b_spatial"])
    saved = []
    for l in range(DEPTH):
        riding = l == 0 and late is not None
        h, p, *got = _in_proj(x, w["norm1_w"], w_in[l], l,
                              rider=_Gather([late["w_down"]], [True], forward_at=0.9) if riding else None)
        if riding:
            w["w_down"] = got[0].reshape(DEPTH, FF, D)
        ya = _sgu_fwd(p, *sgu_params, l)
        yb = _sconv_fwd(p, w["sc_conv_w"], l)
        qkv_c = _qkv_conv_fwd(p, w["dn_conv_w"], l)
        yc, st_c, inv_c, *got = _dn_fwd(
            qkv_c, p, dn_params, l,
            rider=_Gather([late["w_gate_up"], late["w_out"]], [False, True], forward_at=0.9) if riding else None)
        if riding:
            w.update(w_gate_up=got[0], w_out=got[1].reshape(DEPTH, D, D))
        yd, st_d, *got = _gla_fwd(p, gla_params, l,
                                  rider=_Gather([late["w_in"]], [False], forward_at=0.85) if riding else None)
        if riding:
            w_in[1] = _relayout_w_in(got[0])
        head = (w["final_norm_w"], target) if l == DEPTH - 1 else None
        mix, x1, h2, gu, act, x2, *tail = _out_mlp(x, (ya, yb, yc, yd), w["w_out"], w["norm2_w"], w["w_gate_up"],
                                                   w["w_down"], l, head=head)
        saved.append(dict(x=x, h=h, p=p, qkv_c=qkv_c, st_c=st_c, inv_c=inv_c, st_d=st_d, mix=mix, x1=x1, h2=h2, gu=gu,
                          act=act))
        x = x2
    dx, (loss, d_final) = x, tail
    large = {k: [None] * DEPTH for k in ("w_in", "w_out", "w_gate_up", "w_down")}
    small = [None] * DEPTH
    for l in reversed(range(DEPTH)):
        s = saved[l]
        p = s["p"]
        g = {}
        riding = exchange and l == 0

        def exchanging(pairs):
            return _Exchange([large[k][j] for k, j in pairs]) if riding else None

        def arrive(pairs, arrived):
            for (k, j), a in zip(pairs, arrived):
                large[k][j] = a

        pairs = (("w_gate_up", 1), ("w_down", 1))
        dgu, dx1, dmix, g["norm2_w"], *arrived = _mlp_out_bwd(
            dx, s["x1"], s["gu"], w["norm2_w"], w["w_down"], w["w_gate_up"], w["w_out"], l, rider=exchanging(pairs))
        arrive(pairs, arrived)
        pairs = (("w_in", 1), ("w_out", 1))
        d_gu, *arrived = _wgrad_rows(dgu, s["h2"], "wgrad_gate_up", rider=exchanging(pairs)) if riding else (
            _wgrad_rows(dgu, s["h2"], "wgrad_gate_up"),)
        arrive(pairs, arrived)
        large["w_gate_up"][l] = d_gu.reshape(N_DEV, GU_SHARD, D)
        large["w_down"][l] = _wgrad_rows(s["act"], dx, "wgrad_down").reshape(N_DEV, FF // N_DEV, D)
        large["w_out"][l] = _wgrad_rows(s["mix"], dx1, "wgrad_out", rows=D).reshape(N_DEV, D // N_DEV, D)
        d_a, g["sgu_ln_w"], g["sgu_ln_b"], g["sgu_w_spatial"], g["sgu_b_spatial"] = _sgu_bwd(p, dmix, *sgu_params, l)
        d_b, g["sc_conv_w"] = _sconv_bwd(p, dmix, w["sc_conv_w"], l)
        pairs = tuple((k, 0) for k in _LATE)
        dqkv_c, dz_c, dab, g["dn_a_log"], g["dn_dt_bias"], g["dn_norm_w"], *arrived = _dn_bwd(
            s["qkv_c"], p, s["st_c"], s["inv_c"], dmix, dn_params, l, rider=exchanging(pairs))
        arrive(pairs, arrived)
        d_c, g["dn_conv_w"] = _qkv_conv_bwd(p, dqkv_c, w["dn_conv_w"], l)
        d_d, dz_d, dglr, g["gla_w_gate2"], g["gla_gate_bias"], g["gla_norm_w"] = _gla_bwd(p, s["st_d"], dmix, gla_params, l)
        dp, dx, g["norm1_w"] = _in_proj_bwd((d_c, d_d, d_a, d_b, dz_c, dz_d, dab, dglr), s["x"], dx1, w["norm1_w"],
                                            w_in[l], l)
        small[l] = g
        w_in_rows = dict(rows=P // 2, out_dtype=F32)
        if riding:
            half = D // 2
            first, small = _wgrad_rows(dp, s["h"], "wgrad_in_a", **w_in_rows, b_cols=(0, half),
                                       rider=_Gather([_pack_small(small, d_final)], [False], forward_at=0.75))
            first = _scatter_w_in_grad(first)
            second, first = _wgrad_rows(dp, s["h"], "wgrad_in_b", **w_in_rows, b_cols=(1, half), rider=_Exchange([first]))
            large["w_in"][l] = (first, _scatter_w_in_grad(second))
        else:
            large["w_in"][l] = _scatter_w_in_grad(_wgrad_rows(dp, s["h"], "wgrad_in", **w_in_rows))
    return loss[0, 0], dx, large, small, d_final


_ORDER = ("norm1_w", "w_in", "sgu_ln_w", "sgu_ln_b", "sgu_w_spatial", "sgu_b_spatial", "sc_conv_w", "dn_conv_w",
          "dn_a_log", "dn_dt_bias", "dn_norm_w", "gla_w_gate2", "gla_gate_bias", "gla_norm_w", "w_out", "norm2_w",
          "w_gate_up", "w_down", "final_norm_w")
_LARGE = ("w_in", "w_gate_up", "w_out", "w_down")
_LARGE_TILE = {"w_gate_up": 352, "w_out": 128, "w_down": 352}


def _gather_cols(g):
    return jnp.transpose(g, (1, 2, 0, 3)).reshape(g.shape[1], g.shape[2], N_DEV * g.shape[3])


def kernel(x, norm1_w, w_in, sgu_ln_w, sgu_ln_b, sgu_w_spatial, sgu_b_spatial, sc_conv_w, dn_conv_w, dn_a_log, dn_dt_bias, dn_norm_w, gla_w_gate2, gla_gate_bias, gla_norm_w, w_out, norm2_w, w_gate_up, w_down, final_norm_w, loss_target, m_norm1_w, m_w_in, m_sgu_ln_w, m_sgu_ln_b, m_sgu_w_spatial, m_sgu_b_spatial, m_sc_conv_w, m_dn_conv_w, m_dn_a_log, m_dn_dt_bias, m_dn_norm_w, m_gla_w_gate2, m_gla_gate_bias, m_gla_norm_w, m_w_out, m_norm2_w, m_w_gate_up, m_w_down, m_final_norm_w, v_norm1_w, v_w_in, v_sgu_ln_w, v_sgu_ln_b, v_sgu_w_spatial, v_sgu_b_spatial, v_sc_conv_w, v_dn_conv_w, v_dn_a_log, v_dn_dt_bias, v_dn_norm_w, v_gla_w_gate2, v_gla_gate_bias, v_gla_norm_w, v_w_out, v_norm2_w, v_w_gate_up, v_w_down, v_final_norm_w):
    args = dict(locals())
    wts = {k: args[k] for k in _ORDER}
    mom = {k: args["m_" + k] for k in _ORDER}
    var = {k: args["v_" + k] for k in _ORDER}
    for d in (wts, mom, var):
        d["final_norm_w"] = d["final_norm_w"][None]
    me = 4 * lax.axis_index("x") + 2 * lax.axis_index("y") + lax.axis_index("c")
    for d in (wts, mom, var):
        d["w_gate_up"] = jnp.swapaxes(d["w_gate_up"], 1, 2)

    late = {k: wts[k].astype(BF16) for k in _LATE}
    w_in = wts["w_in"].astype(BF16)
    late["w_in"] = w_in[1]
    got = _run(_Gather([w_in[0]] + [wts[k] for k in _SHARDED_SMALL], [False] * 4), "gather_w_in")
    full = dict(wts)
    full["w_in"] = [_relayout_w_in(got[0]), None]
    for k, g in zip(_SHARDED_SMALL, got[1:]):
        full[k] = _gather_cols(g)

    loss, dx, large, small, d_final = _local_grads(x[0], loss_target[0], full, late=late, exchange=True)
    loss = lax.psum(loss, ("x", "y", "c"))

    first, second = large["w_in"][0]
    second, = _run(_Exchange([second]), "exchange_grads")
    large["w_in"][0] = jnp.concatenate([first, second], axis=2)
    result = {}
    for k in ("w_gate_up", "w_out", "w_down"):
        outs = _adamw_large(large[k], wts[k], mom[k], var[k], "adamw_" + k, _LARGE_TILE[k])
        for kind, a in zip(("grad", "delta", "new_m", "new_v"), outs):
            result[kind, k] = jnp.swapaxes(a, 1, 2) if k == "w_gate_up" else a
    outs = _adamw_w_in(large["w_in"], *[jnp.transpose(d["w_in"], (2, 0, 1)) for d in (wts, mom, var)])
    for kind, a in zip(("grad", "delta", "new_m", "new_v"), outs):
        result[kind, "w_in"] = jnp.transpose(a, (1, 2, 0))

    slabs = small
    rep = _REPLICATED + ("final_norm_w",)
    outs, summed = _adamw_small(slabs, {k: wts[k] for k in rep}, {k: mom[k] for k in rep}, {k: var[k] for k in rep})
    for kind, d in zip(("grad", "delta", "new_m", "new_v"), outs):
        for k, a in d.items():
            result[kind, k] = a[0] if k == "final_norm_w" else a
    own = {k: lax.dynamic_slice_in_dim(summed[k], me * wts[k].shape[-1], wts[k].shape[-1], axis=2) for k in _SHARDED_SMALL}
    outs = _adamw_shards(own, {k: wts[k] for k in _SHARDED_SMALL}, {k: mom[k] for k in _SHARDED_SMALL},
                         {k: var[k] for k in _SHARDED_SMALL})
    for kind, d in zip(("grad", "delta", "new_m", "new_v"), [own] + outs):
        for k, a in d.items():
            result[kind, k] = a

    return (loss, dx[None], *[result[kind, k] for kind in ("grad", "delta", "new_m", "new_v") for k in _ORDER])
```

```python
import functools

import jax
import jax.numpy as jnp
from jax import lax
from jax.experimental import pallas as pl
from jax.experimental.pallas import tpu as pltpu

F32, BF16 = jnp.float32, jnp.bfloat16
DEPTH = 2
D = 1024
G = 256
NH, HD = 4, 64
FF = 2816
IN_COLS = 3352
P = 3584
EPS = 1e-6
SGU_C, DN_C, GLA_C = 128, 64, 64
N_DEV = 8
IN_SHARD = IN_COLS // N_DEV
GU_SHARD = 2 * FF // N_DEV
LANES = 128
VMEM_LIMIT = 56 * 2 ** 20

_PAD_SEGS = ((1280, 2048, 0),
             (2312, 3080, 0),
             (0, 512, 0),
             (512, 1280, 0),
             (2056, 2312, 0),
             (3096, 3352, 0),
             (2048, 2056, 120),
             (3080, 3096, 112))

ADAM_LR, ADAM_B1, ADAM_B2, ADAM_EPS, ADAM_WD, ADAM_STEP = 0.001, 0.9, 0.999, 1e-08, 0.01, 10


def _cparams(*sem):
    return pltpu.CompilerParams(dimension_semantics=sem, vmem_limit_bytes=VMEM_LIMIT)


def _resident(shape):
    return pl.BlockSpec(shape, lambda *_: (0,) * len(shape), pipeline_mode=pl.Buffered(1))


def _layer(shape, l):
    return pl.BlockSpec((None,) + tuple(shape), lambda *_: (l,) + (0,) * len(shape), pipeline_mode=pl.Buffered(1))


def _acc(shape):
    return pl.BlockSpec(shape, lambda *_: (0,) * len(shape))


def _dg(a, b, ca, cb):
    lead = a.ndim - 2
    batch = ((0,), (0,)) if lead else ((), ())
    return lax.dot_general(a, b, (((ca + lead,), (cb + lead,)), batch), preferred_element_type=F32)


def _split(t):
    hi = t.astype(BF16)
    return hi, (t - hi.astype(F32)).astype(BF16)


def _dg3(a, b, ca, cb):
    (ah, al), (bh, bl) = a, b
    return _dg(ah, bh, ca, cb) + (_dg(ah, bl, ca, cb) + _dg(al, bh, ca, cb))


def _make_dot(accurate):
    def raw(a, b, form):
        ca = 0 if form == "tn" else 1
        cb = 1 if form == "nt" else 0
        if accurate:
            return _dg3(_split(a), _split(b), ca, cb)
        return _dg(a.astype(BF16), b.astype(BF16), ca, cb)

    @functools.partial(jax.custom_vjp, nondiff_argnums=(2,))
    def dot(a, b, form):
        return raw(a, b, form)

    def fwd(a, b, form):
        return raw(a, b, form), (a, b)

    def bwd(form, res, g):
        a, b = res
        if form == "nn":
            return raw(g, b, "nt"), raw(a, g, "tn")
        if form == "nt":
            return raw(g, b, "nn"), raw(g, a, "tn")
        return raw(b, g, "nt"), raw(a, g, "nn")

    dot.defvjp(fwd, bwd)
    return dot


_bdot = _make_dot(False)
_hdot = _make_dot(True)


def _inv_unit_lower(low):
    n = low.shape[-1]
    eye = (lax.broadcasted_iota(jnp.int32, (n, n), 0) == lax.broadcasted_iota(jnp.int32, (n, n), 1)).astype(F32)
    p = -low
    inv = eye + p
    ps = _split(p)
    k = 1
    while 2 * k < n:
        ps = _split(_dg3(ps, ps, 1, 0))
        inv = inv + _dg3(_split(inv), ps, 1, 0)
        k *= 2
    return inv


@jax.custom_vjp
def _unit_lower_solve(low, rhs, inv):
    return _dg3(_split(inv), _split(rhs), 1, 0)


def _uls_fwd(low, rhs, inv):
    sol = _dg3(_split(inv), _split(rhs), 1, 0)
    return sol, (inv, sol)


def _uls_bwd(res, g):
    inv, sol = res
    d_rhs = _dg3(_split(inv), _split(g), 0, 0)
    return -_dg3(_split(d_rhs), _split(sol), 1, 1), d_rhs, jnp.zeros_like(inv)


_unit_lower_solve.defvjp(_uls_fwd, _uls_bwd)


def _sigmoid(x):
    return 0.5 * jnp.tanh(0.5 * x) + 0.5


def _softplus(x):
    return jnp.maximum(x, 0.0) + jnp.log(1.0 + jnp.exp(-jnp.maximum(x, -x)))


def _gelu(x):
    return 0.5 * x * (1.0 + jnp.tanh(0.7978845608028654 * (x + 0.044715 * (x * x * x))))


def _tri(n):
    ri = lax.broadcasted_iota(jnp.int32, (n, n), 0)
    ci = lax.broadcasted_iota(jnp.int32, (n, n), 1)
    return ri, ci


def _stack(ts):
    return jnp.concatenate([t[None] for t in ts], axis=0)


@jax.custom_vjp
def _head_sums(x):
    ri, ci = _tri(x.shape[-1])
    shift = HD.bit_length() - 1
    ones = (jnp.right_shift(ri, shift) == jnp.right_shift(ci, shift)).astype(BF16)
    hi, lo = _split(x)
    return _dg(hi, ones, 1, 0) + _dg(lo, ones, 1, 0)


_head_sums.defvjp(lambda x: (_head_sums(x), None), lambda _, g: (_head_sums(g),))


def _chunk_mask_dot(x, c, running, transpose):
    ri, ci = _tri(x.shape[0])
    shift = c.bit_length() - 1
    mask = jnp.right_shift(ri, shift) == jnp.right_shift(ci, shift)
    if running:
        mask = mask & (ri >= ci)
    hi, lo = _split(x)
    m = mask.astype(BF16)
    ca = 0 if transpose else 1
    return _dg(m, hi, ca, 0) + _dg(m, lo, ca, 0)


@functools.partial(jax.custom_vjp, nondiff_argnums=(1, 2))
def _chunk_sums(x, c, running):
    return _chunk_mask_dot(x, c, running, False)


_chunk_sums.defvjp(lambda x, c, running: (_chunk_mask_dot(x, c, running, False), None),
                   lambda c, running, _, g: (_chunk_mask_dot(g, c, running, True),))


def _spread_onto(x, first, transpose):
    ri = lax.broadcasted_iota(jnp.int32, (LANES, G), 0)
    ci = lax.broadcasted_iota(jnp.int32, (LANES, G), 1)
    sel = (ri == first + jnp.right_shift(ci, HD.bit_length() - 1)).astype(BF16)
    hi, lo = _split(x)
    cb = 1 if transpose else 0
    return _dg(hi, sel, 1, cb) + _dg(lo, sel, 1, cb)


@functools.partial(jax.custom_vjp, nondiff_argnums=(1,))
def _spread(x, first):
    return _spread_onto(x, first, False)


_spread.defvjp(lambda x, first: (_spread_onto(x, first, False), None),
               lambda first, _, g: (_spread_onto(g, first, True),))


def _head_columns_dot(x, transpose):
    sel = (lax.broadcasted_iota(jnp.int32, (NH, G), 0)
           == jnp.right_shift(lax.broadcasted_iota(jnp.int32, (NH, G), 1), HD.bit_length() - 1)).astype(BF16)
    hi, lo = _split(x)
    if transpose:
        return _dg(sel, hi, 1, 1) + _dg(sel, lo, 1, 1)
    return _dg(hi, sel, 0, 0) + _dg(lo, sel, 0, 0)


@jax.custom_vjp
def _head_columns(x):
    return _head_columns_dot(x, False)


_head_columns.defvjp(lambda x: (_head_columns_dot(x, False), None), lambda _, g: (_head_columns_dot(g, True),))


def _sgu_chunk(uv, ln_w, ln_b, ws, bs):
    u = _gelu(uv[:, :G])
    v = _gelu(uv[:, G:])
    mu = jnp.mean(v, axis=-1, keepdims=True)
    vc = v - mu
    var = jnp.mean(vc * vc, axis=-1, keepdims=True)
    vn = vc * lax.rsqrt(var + EPS) * ln_w + ln_b
    ri, ci = _tri(SGU_C)
    mixed = [_bdot(jnp.where(ri >= ci, ws[h], 0.0), vn[:, HD * h:HD * (h + 1)], "nn") for h in range(NH)]
    return u * (jnp.concatenate(mixed, axis=1) + _head_columns(bs))


def _dn_tile(q, k, v, ab, z, state, inv, a_log, dt_bias, nw):
    c = DN_C
    tm = q.shape[0]
    cps = tm // c
    ri, ci = _tri(tm)
    shift = c.bit_length() - 1
    same = jnp.right_shift(ri, shift) == jnp.right_shift(ci, shift)
    g4 = -jnp.exp(a_log) * _softplus(ab[:, 0:NH] + dt_bias)
    gates = jnp.concatenate([g4, _sigmoid(ab[:, NH:2 * NH]), jnp.zeros((tm, LANES - 2 * NH), F32)], axis=1)
    g, beta = _spread(gates, 0), _spread(gates, NH)
    gc = _chunk_sums(g, c, True)
    gl = _chunk_sums(g, c, False)
    gr4 = _hdot(g4, (same & (ri <= ci)).astype(F32), "tn")
    pairs = [(slice(c * j, c * (j + 1)), h) for j in range(cps) for h in range(NH)]
    heads = lambda t: _stack([t[rows, HD * h:HD * (h + 1)] for rows, h in pairs])
    qn = q * lax.rsqrt(_head_sums(q * q) + EPS) * (HD ** -0.5)
    kn = k * lax.rsqrt(_head_sums(k * k) + EPS)
    eg = jnp.exp(gc)
    kb = kn * beta
    rhs = jnp.concatenate([heads(v * beta), heads(kb * eg)], axis=2)
    qg, kd = heads(qn * eg), heads(kn * jnp.exp(gl - gc))
    cd = _stack([jnp.exp(gl[c * j:c * j + 1, HD * h:HD * (h + 1)]) for j in range(cps) for h in range(NH)])
    qn, kn, kb, gc = heads(qn), heads(kn), heads(kb), heads(gc)
    gr = _stack([gr4[h:h + 1, rows] for rows, h in pairs])
    r2, c2 = _tri(c)
    decay = jnp.exp(jnp.where(r2 >= c2, gc - gr, -jnp.inf))
    low = jnp.where(r2 > c2, _bdot(kb, kn, "nt") * decay, 0.0)
    if inv is None:
        inv = _inv_unit_lower(low)
    sol = _unit_lower_solve(low, rhs, inv)
    u, w = sol[:, :, :HD], sol[:, :, HD:]
    attn = _bdot(qn, kn, "nt") * decay
    ys = []
    for j in range(cps):
        b = slice(NH * j, NH * (j + 1))
        v_new = u[b] - _bdot(w[b], state, "nn")
        o = _bdot(qg[b], state, "nn") + _bdot(attn[b], v_new, "nn")
        state = state * cd[b] + _bdot(kd[b], v_new, "tn")
        on = o * lax.rsqrt(jnp.mean(o * o, axis=-1, keepdims=True) + EPS) * nw
        ys.append(jnp.concatenate([on[h] for h in range(NH)], axis=1))
    return jnp.concatenate(ys, axis=0) * (z * _sigmoid(z)), state, inv


def _gla_tile(q, k, v, glr, z, state_t, w2, gate_bias, nw):
    c = GLA_C
    tm = q.shape[0]
    cps = tm // c
    w2_rows = jnp.concatenate([w2, jnp.zeros((LANES - w2.shape[0], G), F32)], axis=0)
    pre = _bdot(glr, w2_rows, "nn") + gate_bias
    log_a = (jnp.minimum(pre, 0.0) - jnp.log(1.0 + jnp.exp(-jnp.maximum(pre, -pre)))) * (1.0 / 16.0)
    gcum = _chunk_sums(log_a, c, True)
    row = lax.broadcasted_iota(jnp.int32, (c, G), 0)
    qs = q * (HD ** -0.5)
    qa, ka, qg, kl, dec = [], [], [], [], []
    for j in range(cps):
        rows = slice(c * j, c * (j + 1))
        gj = gcum[rows]
        g_mid = jnp.sum(jnp.where(row == c // 2, gj, 0.0), axis=0, keepdims=True)
        g_last = jnp.sum(log_a[rows], axis=0, keepdims=True)
        qa.append(qs[rows] * jnp.exp(gj - g_mid))
        ka.append(k[rows] * jnp.exp(g_mid - gj))
        qg.append(qs[rows] * jnp.exp(gj))
        kl.append(k[rows] * jnp.exp(g_last - gj))
        dec.append(jnp.exp(g_last))
    heads = lambda ts: _stack([t[:, HD * h:HD * (h + 1)] for t in ts for h in range(NH)])
    qa, ka, qg, kl, dec = heads(qa), heads(ka), heads(qg), heads(kl), heads(dec)
    vh = heads([v[c * j:c * (j + 1)] for j in range(cps)])
    r2, c2 = _tri(c)
    o_intra = _bdot(jnp.where(r2 >= c2, _bdot(qa, ka, "nt"), 0.0), vh, "nn")
    ys = []
    for j in range(cps):
        b = slice(NH * j, NH * (j + 1))
        o = o_intra[b] + _bdot(qg[b], state_t, "nt")
        state_t = state_t * dec[b] + _bdot(vh[b], kl[b], "tn")
        on = o * lax.rsqrt(jnp.mean(o * o, axis=-1, keepdims=True) + EPS) * nw
        ys.append(jnp.concatenate([on[h] for h in range(NH)], axis=1))
    return jnp.concatenate(ys, axis=0) * (z * _sigmoid(z)), state_t


def _rms(x, nw):
    r = lax.rsqrt(jnp.mean(x * x, axis=-1, keepdims=True) + EPS)
    xh = x * r
    return xh * nw, xh, r


def _rms_bwd(g, xh, r, nw):
    gw = g * nw
    return r * (gw - xh * jnp.mean(gw * xh, axis=-1, keepdims=True)), jnp.sum(g * xh, axis=0, keepdims=True)


def _row(tm, w, blk=0):
    return pl.BlockSpec((tm, w), lambda i: (i, blk))


def _in_proj(x, nw, wp, l, tm=1024, rider=None):
    t = x.shape[0]
    tm = min(tm, t)

    def body(*refs):
        (x_ref, nw_ref, w_ref, h_ref, p_ref), ride = _split_refs(refs, 3, 2, 0, rider)
        _ride_begin(rider, ride, (t // tm,))
        h = _rms(x_ref[...], nw_ref[l:l + 1, :])[0].astype(BF16)
        h_ref[...] = h
        p_ref[...] = jnp.dot(h, w_ref[...], preferred_element_type=F32)
        _ride_end(rider, ride, (t // tm,))

    specs, operands = _host(
        rider, [_row(tm, D), _resident((DEPTH, D)), _resident((D, P))], [_row(tm, D), _row(tm, P)],
        [jax.ShapeDtypeStruct((t, D), BF16), jax.ShapeDtypeStruct((t, P), F32)], [], [x, nw, wp])
    return pl.pallas_call(body, name="in_proj", grid=(t // tm,), compiler_params=_cparams("arbitrary"),
                          **specs)(*operands)


def _gu_spec(l):
    return pl.BlockSpec((N_DEV, None, GU_SHARD, D), lambda *_: (0, l, 0, 0), pipeline_mode=pl.Buffered(1))


def _out_mlp(x, ys, w_out, nw2, w_gu_t, w_down, l, tm=256, head=None):
    t = x.shape[0]

    def body(*refs):
        x_ref, ya, yb, yc, yd, wo_ref, nw_ref, wgu_ref, wd_ref = refs[:9]
        if head is None:
            mix_ref, x1_ref, h2_ref, gu_ref, act_ref, x2_ref = refs[9:]
        else:
            fnw_ref, tg_ref, mix_ref, x1_ref, h2_ref, gu_ref, act_ref, dx_ref, loss_ref, dfnw_ref = refs[9:]
        mix = jnp.concatenate([ya[...], yb[...], yc[...], yd[...]], axis=1)
        mix_ref[...] = mix
        x1 = x_ref[...] + jnp.dot(mix, wo_ref[...], preferred_element_type=F32)
        x1_ref[...] = x1
        h2 = _rms(x1, nw_ref[l:l + 1, :])[0].astype(BF16)
        h2_ref[...] = h2
        gu = _dg(h2, wgu_ref[...].reshape(2 * FF, D), 1, 1)
        gu_ref[...] = gu.astype(BF16)
        gate, up = gu[:, :FF], gu[:, FF:]
        act = (gate * _sigmoid(gate) * up).astype(BF16)
        act_ref[...] = act
        x2 = x1 + jnp.dot(act, wd_ref[...], preferred_element_type=F32)
        if head is None:
            x2_ref[...] = x2
            return

        @pl.when(pl.program_id(0) == 0)
        def _():
            loss_ref[...] = jnp.zeros_like(loss_ref)
            dfnw_ref[...] = jnp.zeros_like(dfnw_ref)
        fnw = fnw_ref[...]
        y, xh, r = _rms(x2, fnw)
        err = y - tg_ref[...]
        loss_ref[...] += 0.5 * jnp.sum(err * err) * (1.0 / D)
        dx_ref[...], dfnw = _rms_bwd(err * (1.0 / D), xh, r, fnw)
        dfnw_ref[...] += dfnw

    in_specs = [_row(tm, D), _row(tm, G), _row(tm, G), _row(tm, G), _row(tm, G), _layer((D, D), l),
                _resident((DEPTH, D)), _gu_spec(l), _layer((FF, D), l)]
    out_specs = [_row(tm, D), _row(tm, D), _row(tm, D), _row(tm, 2 * FF), _row(tm, FF), _row(tm, D)]
    out_shape = [jax.ShapeDtypeStruct((t, D), BF16), jax.ShapeDtypeStruct((t, D), F32),
                 jax.ShapeDtypeStruct((t, D), BF16), jax.ShapeDtypeStruct((t, 2 * FF), BF16),
                 jax.ShapeDtypeStruct((t, FF), BF16), jax.ShapeDtypeStruct((t, D), F32)]
    operands = [x, *ys, w_out, nw2, w_gu_t, w_down]
    if head is not None:
        in_specs += [_resident((1, D)), _row(tm, D)]
        out_specs += [_acc((8, LANES)), _acc((1, D))]
        out_shape += [jax.ShapeDtypeStruct((8, LANES), F32), jax.ShapeDtypeStruct((1, D), F32)]
        operands += list(head)
    return pl.pallas_call(body, name="out_mlp", grid=(t // tm,), in_specs=in_specs, out_specs=out_specs,
                          out_shape=out_shape, compiler_params=_cparams("arbitrary"))(*operands)


def _mlp_out_bwd(dx2, x1, gu, nw2, w_down, w_gu, w_out, l, tm=256, rider=None):
    t = dx2.shape[0]

    def body(*refs):
        (dx2_ref, x1_ref, gu_ref, nw_ref, wd_ref, wgu_ref, wo_ref, dgu_ref, dx1_ref, dmix_ref, dnw_ref), ride = \
            _split_refs(refs, 7, 4, 0, rider)
        _ride_begin(rider, ride, (t // tm,))

        @pl.when(pl.program_id(0) == 0)
        def _():
            dnw_ref[...] = jnp.zeros_like(dnw_ref)
        dx2 = dx2_ref[...]
        dact = _dg(dx2.astype(BF16), wd_ref[...], 1, 1)
        gu = gu_ref[...].astype(F32)
        gate, up = gu[:, :FF], gu[:, FF:]
        s = _sigmoid(gate)
        dgu = jnp.concatenate([dact * up * (s * (1.0 + gate * (1.0 - s))), dact * (gate * s)], axis=1).astype(BF16)
        dgu_ref[...] = dgu
        dh2 = jnp.dot(dgu, wgu_ref[...].reshape(2 * FF, D), preferred_element_type=F32)
        nw_v = nw_ref[l:l + 1, :]
        _, xh, r = _rms(x1_ref[...], nw_v)
        dxn, dnw = _rms_bwd(dh2, xh, r, nw_v)
        dx1 = dx2 + dxn
        dx1_ref[...] = dx1
        dnw_ref[...] += dnw
        dmix_ref[...] = _dg(dx1.astype(BF16), wo_ref[...], 1, 1)
        _ride_end(rider, ride, (t // tm,))

    specs, operands = _host(
        rider, [_row(tm, D), _row(tm, D), _row(tm, 2 * FF), _resident((DEPTH, D)), _layer((FF, D), l), _gu_spec(l),
                _layer((D, D), l)],
        [_row(tm, 2 * FF), _row(tm, D), _row(tm, D), _acc((1, D))],
        [jax.ShapeDtypeStruct((t, 2 * FF), BF16), jax.ShapeDtypeStruct((t, D), F32),
         jax.ShapeDtypeStruct((t, D), F32), jax.ShapeDtypeStruct((1, D), F32)],
        [], [dx2, x1, gu, nw2, w_down, w_gu, w_out])
    return pl.pallas_call(body, name="mlp_out_bwd", grid=(t // tm,), compiler_params=_cparams("arbitrary"),
                          **specs)(*operands)


_DP_WIDTHS = (768, 768, 512, 768, 256, 256, 128, 128)


def _in_proj_bwd(dps, x, dx1, nw, wp, l, tm=512):
    t = x.shape[0]

    def body(*refs):
        dp_refs, (x_ref, dx1_ref, nw_ref, w_ref, dp_ref, dx_ref, dnw_ref) = refs[:8], refs[8:]

        @pl.when(pl.program_id(0) == 0)
        def _():
            dnw_ref[...] = jnp.zeros_like(dnw_ref)
        dp = jnp.concatenate([r[...] for r in dp_refs], axis=1)
        dp_ref[...] = dp
        dh = _dg(dp, w_ref[...], 1, 1)
        nw_v = nw_ref[l:l + 1, :]
        _, xh, r = _rms(x_ref[...], nw_v)
        dxn, dnw = _rms_bwd(dh, xh, r, nw_v)
        dx_ref[...] = dx1_ref[...] + dxn
        dnw_ref[...] += dnw

    return pl.pallas_call(
        body, name="in_proj_bwd", grid=(t // tm,),
        in_specs=[_row(tm, w) for w in _DP_WIDTHS] + [_row(tm, D), _row(tm, D), _resident((DEPTH, D)), _resident((D, P))],
        out_specs=[_row(tm, P), _row(tm, D), _acc((1, D))],
        out_shape=[jax.ShapeDtypeStruct((t, P), BF16), jax.ShapeDtypeStruct((t, D), F32),
                   jax.ShapeDtypeStruct((1, D), F32)],
        compiler_params=_cparams("arbitrary"))(*dps, x, dx1, nw, wp)


def _accumulate(acc, o_ref, t_axis, term):
    @pl.when(pl.program_id(t_axis) == 0)
    def _():
        acc[...] = jnp.zeros_like(acc)
    acc[...] += term

    @pl.when(pl.program_id(t_axis) == pl.num_programs(t_axis) - 1)
    def _():
        o_ref[...] = acc[...].astype(o_ref.dtype)


WGRAD_TOKENS = 2048


WGRAD_ROWS = 2 * GU_SHARD


def _wgrad_rows(a, b, name, tt=WGRAD_TOKENS, rows=WGRAD_ROWS, out_dtype=BF16, rider=None, b_cols=(0, D)):
    t, m = a.shape
    tt = min(tt, t)
    grid = (m // rows, t // tt)
    b_blk, width = b_cols

    def body(*refs):
        (a_ref, b_ref, o_ref, acc), ride = _split_refs(refs, 2, 1, 1, rider)
        _ride_begin(rider, ride, grid)
        _accumulate(acc, o_ref, 1, _dg(a_ref[...], b_ref[...].astype(BF16), 0, 0))
        _ride_end(rider, ride, grid)

    specs, operands = _host(
        rider, [pl.BlockSpec((tt, rows), lambda j, i: (i, j)), pl.BlockSpec((tt, width), lambda j, i: (i, b_blk))],
        [pl.BlockSpec((rows, width), lambda j, i: (j, 0))], [jax.ShapeDtypeStruct((m, width), out_dtype)],
        [pltpu.VMEM((rows, width), F32)], [a, b])
    out = pl.pallas_call(body, name=name, grid=grid, compiler_params=_cparams("arbitrary", "arbitrary"), **specs)(*operands)
    return out[0] if rider is None else out


def _relayout_w_in(g, tr=512):
    def body(w_ref, o_ref):
        full = jnp.concatenate([w_ref[d] for d in range(N_DEV)], axis=1)
        parts = []
        for a, b, z in _PAD_SEGS:
            parts.append(full[:, a:b])
            if z:
                parts.append(jnp.zeros((tr, z), full.dtype))
        o_ref[...] = jnp.concatenate(parts, axis=1)

    return pl.pallas_call(
        body, name="relayout_w_in", grid=(D // tr,),
        in_specs=[pl.BlockSpec((N_DEV, tr, IN_SHARD), lambda i: (0, i, 0))], out_specs=_row(tr, P),
        out_shape=jax.ShapeDtypeStruct((D, P), g.dtype), compiler_params=_cparams("parallel"))(g)


IN_ROWS = 432


def _scatter_w_in_grad(gp_t, tc=512):
    def body(g_ref, o_ref):
        g = g_ref[...]
        pieces, off = [], 0
        for a, b, z in _PAD_SEGS:
            pieces.append((a, g[off:off + (b - a)]))
            off += (b - a) + z
        spill = -(-(IN_SHARD * (N_DEV - 1) + IN_ROWS - IN_COLS) // 8) * 8
        nat = jnp.concatenate([piece for _, piece in sorted(pieces, key=lambda ap: ap[0])]
                              + [jnp.zeros((spill, tc), F32)], axis=0)
        for d in range(N_DEV):
            o_ref[d] = nat[IN_SHARD * d:IN_SHARD * d + IN_ROWS].astype(BF16)

    return pl.pallas_call(
        body, name="scatter_w_in_grad", grid=(gp_t.shape[1] // tc,),
        in_specs=[pl.BlockSpec((P, tc), lambda i: (0, i))],
        out_specs=pl.BlockSpec((N_DEV, IN_ROWS, tc), lambda i: (0, 0, i)),
        out_shape=jax.ShapeDtypeStruct((N_DEV, IN_ROWS, gp_t.shape[1]), BF16),
        compiler_params=_cparams("parallel"))(gp_t)


_A_BLK = 3


def _sgu_specs(l):
    return [_resident((DEPTH, G)), _resident((DEPTH, G)), _layer((NH, SGU_C, SGU_C), l), _layer((NH, SGU_C), l)]


def _sgu_fwd(p, ln_w, ln_b, ws, bs, l, tm=1024):
    t = p.shape[0]
    tm = min(tm, t)

    def body(uv_ref, lw_ref, lb_ref, ws_ref, bs_ref, y_ref):
        ws_v = [ws_ref[h] for h in range(NH)]
        for c in range(tm // SGU_C):
            rows = pl.ds(c * SGU_C, SGU_C)
            y_ref[rows, :] = _sgu_chunk(uv_ref[rows, :], lw_ref[l:l + 1, :], lb_ref[l:l + 1, :], ws_v,
                                        bs_ref[...]).astype(BF16)

    return pl.pallas_call(
        body, name="sgu_fwd", grid=(t // tm,),
        in_specs=[_row(tm, 2 * G, _A_BLK)] + _sgu_specs(l), out_specs=_row(tm, G),
        out_shape=jax.ShapeDtypeStruct((t, G), BF16),
        compiler_params=_cparams("parallel"))(p, ln_w, ln_b, ws, bs)


def _sgu_bwd(p, dmix, ln_w, ln_b, ws, bs, l, tm=1024):
    t = p.shape[0]
    tm = min(tm, t)

    def body(uv_ref, dy_ref, lw_ref, lb_ref, ws_ref, bs_ref, duv_ref, dlw_ref, dlb_ref, dws_ref, dbs_ref):
        @pl.when(pl.program_id(0) == 0)
        def _():
            for r in (dlw_ref, dlb_ref, dws_ref, dbs_ref):
                r[...] = jnp.zeros_like(r)
        ws_v = [ws_ref[h] for h in range(NH)]
        for c in range(tm // SGU_C):
            rows = pl.ds(c * SGU_C, SGU_C)
            _, vjp = jax.vjp(_sgu_chunk, uv_ref[rows, :], lw_ref[l:l + 1, :], lb_ref[l:l + 1, :], ws_v, bs_ref[...])
            duv, dlw, dlb, dws, dbs = vjp(dy_ref[rows, :])
            duv_ref[rows, :] = duv.astype(BF16)
            dlw_ref[...] += dlw
            dlb_ref[...] += dlb
            dbs_ref[...] += dbs
            for h in range(NH):
                dws_ref[h] += dws[h]

    return pl.pallas_call(
        body, name="sgu_bwd", grid=(t // tm,),
        in_specs=[_row(tm, 2 * G, _A_BLK), _row(tm, G, 0)] + _sgu_specs(l),
        out_specs=[_row(tm, 2 * G), _acc((1, G)), _acc((1, G)), _acc((NH, SGU_C, SGU_C)), _acc((NH, SGU_C))],
        out_shape=[jax.ShapeDtypeStruct((t, 2 * G), BF16), jax.ShapeDtypeStruct((1, G), F32),
                   jax.ShapeDtypeStruct((1, G), F32), jax.ShapeDtypeStruct((NH, SGU_C, SGU_C), F32),
                   jax.ShapeDtypeStruct((NH, SGU_C), F32)],
        compiler_params=_cparams("arbitrary"))(p, dmix, ln_w, ln_b, ws, bs)


HALO = 8


def _prev_halo(tm, width, col):
    return pl.BlockSpec((HALO, width), lambda i: (jnp.maximum(i * (tm // HALO) - 1, 0), col))


def _next_halo(tm, width, col, t):
    return pl.BlockSpec((HALO, width), lambda i: (jnp.minimum((i + 1) * (tm // HALO), t // HALO - 1), col))


def _with_prev(halo_ref, x_ref):
    halo = jnp.where(pl.program_id(0) == 0, 0.0, halo_ref[...])
    return jnp.concatenate([halo, x_ref[...]], axis=0)


def _with_next(x_ref, halo_ref):
    halo = jnp.where(pl.program_id(0) == pl.num_programs(0) - 1, 0.0, halo_ref[...])
    return jnp.concatenate([x_ref[...], halo], axis=0)


def _delay(ext, j):
    return ext if j == 0 else pltpu.roll(ext, j, axis=0)


def _advance(ext, j):
    return ext if j == 0 else pltpu.roll(ext, ext.shape[0] - j, axis=0)


def _causal_conv(ext, w, tm):
    kw = w.shape[0]
    out = None
    for k in range(kw):
        term = _delay(ext, kw - 1 - k)[HALO:HALO + tm] * w[k:k + 1, :]
        out = term if out is None else out + term
    return out


def _causal_conv_t(ext, w, tm):
    kw = w.shape[0]
    out = None
    for k in range(kw):
        term = _advance(ext, kw - 1 - k)[0:tm] * w[k:k + 1, :]
        out = term if out is None else out + term
    return out


def _conv_wgrad(ext, dy, kw, tm):
    return jnp.concatenate(
        [jnp.sum(_delay(ext, kw - 1 - k)[HALO:HALO + tm] * dy, axis=0, keepdims=True) for k in range(kw)], axis=0)


_B_GB, _B_GC, _B_H = 8, 9, 10
_C_QKV, _D_QKV = 0, 1
_C_Z, _D_Z = 11, 12
_C_AB, _D_GLR = 26, 27


def _sconv_fwd(p, w, l, tm=2048):
    t = p.shape[0]
    tm = min(tm, t)

    def body(gb_ref, gc_ref, h_ref, gc_halo, h_halo, w_ref, y_ref):
        s_ext = _with_prev(gc_halo, gc_ref) * _with_prev(h_halo, h_ref)
        y_ref[...] = (gb_ref[...] * _causal_conv(s_ext, w_ref[...], tm)).astype(BF16)

    return pl.pallas_call(
        body, name="sconv_fwd", grid=(t // tm,),
        in_specs=[_row(tm, G, _B_GB), _row(tm, G, _B_GC), _row(tm, G, _B_H), _prev_halo(tm, G, _B_GC),
                  _prev_halo(tm, G, _B_H), _layer((3, G), l)],
        out_specs=_row(tm, G), out_shape=jax.ShapeDtypeStruct((t, G), BF16),
        compiler_params=_cparams("parallel"))(p, p, p, p, p, w)


def _sconv_bwd(p, dmix, w, l, tm=2048):
    t = p.shape[0]
    tm = min(tm, t)

    def body(gb_ref, gc_ref, h_ref, gc_halo, h_halo, gb_next, dy_ref, dy_next, w_ref, dp_ref, dw_ref):
        @pl.when(pl.program_id(0) == 0)
        def _():
            dw_ref[...] = jnp.zeros_like(dw_ref)
        w_v = w_ref[...]
        s_ext = _with_prev(gc_halo, gc_ref) * _with_prev(h_halo, h_ref)
        dout = dy_ref[...]
        d_gb = dout * _causal_conv(s_ext, w_v, tm)
        dconv_ext = _with_next(dy_ref, dy_next) * _with_next(gb_ref, gb_next)
        ds = _causal_conv_t(dconv_ext, w_v, tm)
        dw_ref[...] += _conv_wgrad(s_ext, dconv_ext[0:tm], 3, tm)
        dp_ref[...] = jnp.concatenate([d_gb, ds * h_ref[...], ds * gc_ref[...]], axis=1).astype(BF16)

    return pl.pallas_call(
        body, name="sconv_bwd", grid=(t // tm,),
        in_specs=[_row(tm, G, _B_GB), _row(tm, G, _B_GC), _row(tm, G, _B_H), _prev_halo(tm, G, _B_GC),
                  _prev_halo(tm, G, _B_H), _next_halo(tm, G, _B_GB, t), _row(tm, G, 1), _next_halo(tm, G, 1, t),
                  _layer((3, G), l)],
        out_specs=[_row(tm, 3 * G), _acc((3, G))],
        out_shape=[jax.ShapeDtypeStruct((t, 3 * G), BF16), jax.ShapeDtypeStruct((3, G), F32)],
        compiler_params=_cparams("arbitrary"))(p, p, p, p, p, p, dmix, dmix, w)


def _qkv_conv_fwd(p, w, l, tm=2048):
    t = p.shape[0]
    tm = min(tm, t)

    def body(x_ref, x_halo, w_ref, y_ref):
        c = _causal_conv(_with_prev(x_halo, x_ref), w_ref[...], tm)
        y_ref[...] = c * _sigmoid(c)

    return pl.pallas_call(
        body, name="qkv_conv_fwd", grid=(t // tm,),
        in_specs=[_row(tm, 3 * G, _C_QKV), _prev_halo(tm, 3 * G, _C_QKV), _layer((4, 3 * G), l)],
        out_specs=_row(tm, 3 * G), out_shape=jax.ShapeDtypeStruct((t, 3 * G), F32),
        compiler_params=_cparams("parallel"))(p, p, w)


def _qkv_conv_bwd(p, dy, w, l, tm=1024):
    t = p.shape[0]
    tm = min(tm, t)

    def body(x_ref, x_halo, x_next, dy_ref, dy_next, w_ref, dx_ref, dw_ref):
        @pl.when(pl.program_id(0) == 0)
        def _():
            dw_ref[...] = jnp.zeros_like(dw_ref)
        w_v = w_ref[...]
        x_all = jnp.concatenate([_with_prev(x_halo, x_ref), jnp.where(
            pl.program_id(0) == pl.num_programs(0) - 1, 0.0, x_next[...])], axis=0)
        c = _causal_conv(x_all, w_v, tm + HALO)
        s = _sigmoid(c)
        dc_ext = _with_next(dy_ref, dy_next) * (s * (1.0 + c * (1.0 - s)))
        dx_ref[...] = _causal_conv_t(dc_ext, w_v, tm).astype(BF16)
        dw_ref[...] += _conv_wgrad(x_all[0:HALO + tm], dc_ext[0:tm], 4, tm)

    return pl.pallas_call(
        body, name="qkv_conv_bwd", grid=(t // tm,),
        in_specs=[_row(tm, 3 * G, _C_QKV), _prev_halo(tm, 3 * G, _C_QKV), _next_halo(tm, 3 * G, _C_QKV, t),
                  _row(tm, 3 * G), _next_halo(tm, 3 * G, 0, t), _layer((4, 3 * G), l)],
        out_specs=[_row(tm, 3 * G), _acc((4, 3 * G))],
        out_shape=[jax.ShapeDtypeStruct((t, 3 * G), BF16), jax.ShapeDtypeStruct((4, 3 * G), F32)],
        compiler_params=_cparams("arbitrary"))(p, p, p, dy, dy, w)


_STATE = pl.BlockSpec((1, NH, HD, HD), lambda i: (i, 0, 0, 0))


def _dn_specs():
    return [_resident((DEPTH, NH)), _resident((DEPTH, NH)), _resident((DEPTH, HD))]


def _dn_fwd(qkv, p, params, l, cps=8, rider=None):
    t = qkv.shape[0]
    tm = DN_C * cps
    nb = cps * NH

    def body(*refs):
        (qkv_ref, z_ref, ab_ref, alog_ref, dt_ref, nw_ref, y_ref, st_ref, inv_ref, state), ride = _split_refs(
            refs, 6, 3, 1, rider)
        _ride_begin(rider, ride, (t // tm,))

        @pl.when(pl.program_id(0) == 0)
        def _():
            state[...] = jnp.zeros_like(state)
        st_ref[0] = state[...]
        y, new, inv = _dn_tile(qkv_ref[:, 0:G], qkv_ref[:, G:2 * G], qkv_ref[:, 2 * G:3 * G], ab_ref[...], z_ref[...],
                               state[...], None, alog_ref[l:l + 1, :], dt_ref[l:l + 1, :], nw_ref[l:l + 1, :])
        y_ref[...] = y.astype(BF16)
        inv_ref[...] = inv
        state[...] = new
        _ride_end(rider, ride, (t // tm,))

    specs, operands = _host(
        rider, [_row(tm, 3 * G), _row(tm, G, _C_Z), _row(tm, LANES, _C_AB)] + _dn_specs(),
        [_row(tm, G), _STATE, pl.BlockSpec((nb, DN_C, DN_C), lambda i: (i, 0, 0))],
        [jax.ShapeDtypeStruct((t, G), BF16), jax.ShapeDtypeStruct((t // tm, NH, HD, HD), F32),
         jax.ShapeDtypeStruct((t // DN_C * NH, DN_C, DN_C), F32)],
        [pltpu.VMEM((NH, HD, HD), F32)], [qkv, p, p, *params])
    return pl.pallas_call(body, name="dn_fwd", grid=(t // tm,), compiler_params=_cparams("arbitrary"), **specs)(*operands)


def _dn_bwd(qkv, p, states, inv, dmix, params, l, cps=8, rider=None):
    t = qkv.shape[0]
    tm = DN_C * cps
    n = t // tm
    nb = cps * NH

    def body(*refs):
        (qkv_ref, z_ref, ab_ref, st_ref, inv_ref, dy_ref, alog_ref, dt_ref, nw_ref,
         dqkv_ref, dz_ref, dab_ref, dalog_ref, ddt_ref, dnw_ref, dstate), ride = _split_refs(refs, 9, 6, 1, rider)
        _ride_begin(rider, ride, (n,))
        dprm_refs = (dalog_ref, ddt_ref, dnw_ref)

        @pl.when(pl.program_id(0) == 0)
        def _():
            dstate[...] = jnp.zeros_like(dstate)
            for r in dprm_refs:
                r[...] = jnp.zeros_like(r)
        inv_v = inv_ref[...]
        tile = lambda q, k, v, ab, z, st, alog, dt, nw: _dn_tile(q, k, v, ab, z, st, inv_v, alog, dt, nw)[:2]
        _, vjp = jax.vjp(tile, qkv_ref[:, 0:G], qkv_ref[:, G:2 * G], qkv_ref[:, 2 * G:3 * G], ab_ref[...], z_ref[...],
                         st_ref[0], alog_ref[l:l + 1, :], dt_ref[l:l + 1, :], nw_ref[l:l + 1, :])
        dq, dk, dv, dab, dz, dst, *dprm = vjp((dy_ref[...], dstate[...]))
        dqkv_ref[...] = jnp.concatenate([dq, dk, dv], axis=1)
        dz_ref[...] = dz.astype(BF16)
        dab_ref[...] = dab.astype(BF16)
        dstate[...] = dst
        for r, g in zip(dprm_refs, dprm):
            r[...] += g
        _ride_end(rider, ride, (n,))

    rev = lambda w, blk: pl.BlockSpec((tm, w), lambda i: (n - 1 - i, blk))
    specs, operands = _host(
        rider, [rev(3 * G, 0), rev(G, _C_Z), rev(LANES, _C_AB),
                pl.BlockSpec((1, NH, HD, HD), lambda i: (n - 1 - i, 0, 0, 0)),
                pl.BlockSpec((nb, DN_C, DN_C), lambda i: (n - 1 - i, 0, 0)), rev(G, 2)] + _dn_specs(),
        [rev(3 * G, 0), rev(G, 0), rev(LANES, 0), _acc((1, NH)), _acc((1, NH)), _acc((1, HD))],
        [jax.ShapeDtypeStruct((t, 3 * G), F32), jax.ShapeDtypeStruct((t, G), BF16),
         jax.ShapeDtypeStruct((t, LANES), BF16), jax.ShapeDtypeStruct((1, NH), F32),
         jax.ShapeDtypeStruct((1, NH), F32), jax.ShapeDtypeStruct((1, HD), F32)],
        [pltpu.VMEM((NH, HD, HD), F32)], [qkv, p, p, states, inv, dmix, *params])
    return pl.pallas_call(body, name="dn_bwd", grid=(n,), compiler_params=_cparams("arbitrary"), **specs)(*operands)


def _gla_specs(l):
    return [_layer((16, G), l), _resident((DEPTH, G)), _resident((DEPTH, HD))]


def _gla_fwd(p, params, l, cps=8, rider=None):
    t = p.shape[0]
    tm = GLA_C * cps

    def body(*refs):
        (qkv_ref, z_ref, glr_ref, w2_ref, gb_ref, nw_ref, y_ref, st_ref, state), ride = _split_refs(refs, 6, 2, 1, rider)
        _ride_begin(rider, ride, (t // tm,))

        @pl.when(pl.program_id(0) == 0)
        def _():
            state[...] = jnp.zeros_like(state)
        st_ref[0] = state[...]
        y, new = _gla_tile(qkv_ref[:, 0:G], qkv_ref[:, G:2 * G], qkv_ref[:, 2 * G:3 * G], glr_ref[...], z_ref[...],
                           state[...], w2_ref[...], gb_ref[l:l + 1, :], nw_ref[l:l + 1, :])
        y_ref[...] = y.astype(BF16)
        state[...] = new
        _ride_end(rider, ride, (t // tm,))

    specs, operands = _host(
        rider, [_row(tm, 3 * G, _D_QKV), _row(tm, G, _D_Z), _row(tm, LANES, _D_GLR)] + _gla_specs(l),
        [_row(tm, G), _STATE],
        [jax.ShapeDtypeStruct((t, G), BF16), jax.ShapeDtypeStruct((t // tm, NH, HD, HD), F32)],
        [pltpu.VMEM((NH, HD, HD), F32)], [p, p, p, *params])
    return pl.pallas_call(body, name="gla_fwd", grid=(t // tm,), compiler_params=_cparams("arbitrary"), **specs)(*operands)


def _gla_bwd(p, states, dmix, params, l, cps=8):
    t = p.shape[0]
    tm = GLA_C * cps
    n = t // tm

    def body(qkv_ref, z_ref, glr_ref, st_ref, dy_ref, w2_ref, gb_ref, nw_ref,
             dqkv_ref, dz_ref, dglr_ref, dw2_ref, dgb_ref, dnw_ref, dstate):
        dprm_refs = (dw2_ref, dgb_ref, dnw_ref)

        @pl.when(pl.program_id(0) == 0)
        def _():
            dstate[...] = jnp.zeros_like(dstate)
            for r in dprm_refs:
                r[...] = jnp.zeros_like(r)
        _, vjp = jax.vjp(_gla_tile, qkv_ref[:, 0:G], qkv_ref[:, G:2 * G], qkv_ref[:, 2 * G:3 * G], glr_ref[...],
                         z_ref[...], st_ref[0], w2_ref[...], gb_ref[l:l + 1, :], nw_ref[l:l + 1, :])
        dq, dk, dv, dglr, dz, dst, *dprm = vjp((dy_ref[...], dstate[...]))
        dqkv_ref[...] = jnp.concatenate([dq, dk, dv], axis=1).astype(BF16)
        dz_ref[...] = dz.astype(BF16)
        dglr_ref[...] = dglr.astype(BF16)
        dstate[...] = dst
        for r, g in zip(dprm_refs, dprm):
            r[...] += g

    rev = lambda w, blk: pl.BlockSpec((tm, w), lambda i: (n - 1 - i, blk))
    return pl.pallas_call(
        body, name="gla_bwd", grid=(n,),
        in_specs=[rev(3 * G, _D_QKV), rev(G, _D_Z), rev(LANES, _D_GLR),
                  pl.BlockSpec((1, NH, HD, HD), lambda i: (n - 1 - i, 0, 0, 0)), rev(G, 3)] + _gla_specs(l),
        out_specs=[rev(3 * G, 0), rev(G, 0), rev(LANES, 0), _acc((16, G)), _acc((1, G)), _acc((1, HD))],
        out_shape=[jax.ShapeDtypeStruct((t, 3 * G), BF16), jax.ShapeDtypeStruct((t, G), BF16),
                   jax.ShapeDtypeStruct((t, LANES), BF16), jax.ShapeDtypeStruct((16, G), F32),
                   jax.ShapeDtypeStruct((1, G), F32), jax.ShapeDtypeStruct((1, HD), F32)],
        scratch_shapes=[pltpu.VMEM((NH, HD, HD), F32)],
        compiler_params=_cparams("arbitrary"))(p, p, p, states, dmix, *params)


def _adamw_math(w, g, m, v):
    m = ADAM_B1 * m + (1.0 - ADAM_B1) * g
    v = ADAM_B2 * v + (1.0 - ADAM_B2) * (g * g)
    m_hat = m / (1.0 - ADAM_B1 ** ADAM_STEP)
    v_hat = v / (1.0 - ADAM_B2 ** ADAM_STEP)
    return -ADAM_LR * (m_hat / (jnp.sqrt(v_hat) + ADAM_EPS) + ADAM_WD * w), m, v


def _adamw_large(parts, w, m, v, name, tr):
    _, r, c = w.shape

    def body(p0_ref, p1_ref, w_ref, m_ref, v_ref, g_ref, d_ref, nm_ref, nv_ref):
        def total(p_ref):
            g = p_ref[0].astype(F32)
            for k in range(1, N_DEV):
                g = g + p_ref[k].astype(F32)
            return g

        g = jnp.where(pl.program_id(0) == 0, total(p0_ref), total(p1_ref))
        g_ref[...] = g
        d_ref[...], nm_ref[...], nv_ref[...] = _adamw_math(w_ref[...], g, m_ref[...], v_ref[...])

    blk = pl.BlockSpec((None, tr, c), lambda l, i: (l, i, 0))
    part = pl.BlockSpec((N_DEV, tr, c), lambda l, i: (0, i, 0))
    return pl.pallas_call(
        body, name=name, grid=(DEPTH, r // tr), in_specs=[part, part, blk, blk, blk],
        out_specs=[blk] * 4, out_shape=[jax.ShapeDtypeStruct(w.shape, F32)] * 4,
        compiler_params=_cparams("parallel", "parallel"))(*parts, w, m, v)


def _adamw_w_in(parts, w, m, v, tc=512):
    def body(p0_ref, p1_ref, w_ref, m_ref, v_ref, g_ref, d_ref, nm_ref, nv_ref):
        for l, p_ref in enumerate((p0_ref, p1_ref)):
            g = p_ref[0, 0:IN_SHARD, :].astype(F32)
            for k in range(1, N_DEV):
                g = g + p_ref[k, 0:IN_SHARD, :].astype(F32)
            g_ref[:, l, :] = g
            d_ref[:, l, :], nm_ref[:, l, :], nv_ref[:, l, :] = _adamw_math(w_ref[:, l, :], g, m_ref[:, l, :], v_ref[:, l, :])

    blk = pl.BlockSpec((IN_SHARD, DEPTH, tc), lambda i: (0, 0, i))
    part = pl.BlockSpec((N_DEV, IN_ROWS, tc), lambda i: (0, 0, i))
    return pl.pallas_call(
        body, name="adamw_w_in", grid=(D // tc,), in_specs=[part, part, blk, blk, blk], out_specs=[blk] * 4,
        out_shape=[jax.ShapeDtypeStruct(w.shape, F32)] * 4, compiler_params=_cparams("parallel"))(*parts, w, m, v)


_SLAB_AT = {
    "sgu_w_spatial": (0, 0, SGU_C, LANES),
    "sgu_b_spatial": (0, 0, NH, SGU_C),
    "norm1_w": (4, 0, 1, D),
    "norm2_w": (5, 0, 1, D),
    "sgu_ln_w": (6, 0, 1, G),
    "sgu_ln_b": (6, 256, 1, G),
    "gla_gate_bias": (6, 512, 1, G),
    "dn_norm_w": (6, 768, 1, HD),
    "gla_norm_w": (6, 896, 1, HD),
    "dn_a_log": (7, 0, 1, NH),
    "dn_dt_bias": (7, 128, 1, NH),
    "gla_w_gate2": (8, 0, 16, G),
    "dn_conv_w": (8, 256, 4, 3 * G),
    "sc_conv_w": (12, 256, 3, G),
}
_LAYER_ROWS = 24
_FINAL_ROW = SGU_C + DEPTH * _LAYER_ROWS
_SLAB_ROWS, _SLAB_COLS = _FINAL_ROW + 8, 1024
_SLAB_ORDER = tuple(_SLAB_AT)
_SHARDED_SMALL = ("sc_conv_w", "dn_conv_w", "gla_w_gate2")
_REPLICATED = tuple(k for k in _SLAB_ORDER if k not in _SHARDED_SMALL)


def _region(name, l, h=0):
    r0, c0, nr, nc = _SLAB_AT[name]
    if name == "sgu_w_spatial":
        lane = nc * (NH * l + h)
        return slice(0, nr), slice(lane, lane + nc)
    row = SGU_C + _LAYER_ROWS * l + r0
    return slice(row, row + nr), slice(c0, c0 + nc)


def _pack_small(layer_grads, d_final):
    def body(*refs):
        slab = refs[-1]
        slab[...] = jnp.zeros_like(slab)
        it = iter(refs[:-1])
        for l in range(DEPTH):
            for name in _SLAB_ORDER:
                ref = next(it)
                if name == "sgu_w_spatial":
                    for h in range(NH):
                        slab[_region(name, l, h)] = ref[h]
                else:
                    slab[_region(name, l)] = ref[...]
        slab[_FINAL_ROW:_FINAL_ROW + 1, :] = next(it)[...]

    flat = [layer_grads[l][name] for l in range(DEPTH) for name in _SLAB_ORDER] + [d_final]
    return pl.pallas_call(body, name="pack_small_grads", out_shape=jax.ShapeDtypeStruct((_SLAB_ROWS, _SLAB_COLS), F32),
                          compiler_params=_cparams())(*flat)


def _adamw_small(parts, w, m, v):
    names = _REPLICATED + ("final_norm_w",)
    n_in = len(names)

    def body(*refs):
        def total(rows, cols):
            g = refs[0][0, rows, cols]
            for k in range(1, N_DEV):
                g = g + refs[0][k, rows, cols]
            return g

        w_refs = dict(zip(names, refs[1:1 + n_in]))
        m_refs = dict(zip(names, refs[1 + n_in:1 + 2 * n_in]))
        v_refs = dict(zip(names, refs[1 + 2 * n_in:1 + 3 * n_in]))
        outs = refs[1 + 3 * n_in:]
        out_refs = [dict(zip(names, outs[j * n_in:(j + 1) * n_in])) for j in range(4)]
        full_refs = dict(zip(_SHARDED_SMALL, outs[4 * n_in:]))

        def update(name, idx, g):
            res = (g,) + _adamw_math(w_refs[name][idx], g, m_refs[name][idx], v_refs[name][idx])
            for o, val in zip(out_refs, res):
                o[name][idx] = val

        for l in range(DEPTH):
            for name in _REPLICATED:
                if name == "sgu_w_spatial":
                    for h in range(NH):
                        update(name, (l, h), total(*_region(name, l, h)))
                elif name == "sgu_b_spatial":
                    update(name, (l,), total(*_region(name, l)))
                else:
                    update(name, (slice(l, l + 1), slice(None)), total(*_region(name, l)))
            for name in _SHARDED_SMALL:
                full_refs[name][l] = total(*_region(name, l))
        update("final_norm_w", (slice(None), slice(None)), total(slice(_FINAL_ROW, _FINAL_ROW + 1), slice(None)))

    shapes = [jax.ShapeDtypeStruct(w[k].shape, F32) for k in names]
    full_shapes = [jax.ShapeDtypeStruct((DEPTH,) + _SLAB_AT[k][2:], F32) for k in _SHARDED_SMALL]
    outs = pl.pallas_call(body, name="adamw_small", out_shape=shapes * 4 + full_shapes, compiler_params=_cparams())(
        parts, *[w[k] for k in names], *[m[k] for k in names], *[v[k] for k in names])
    result = [dict(zip(names, outs[j * n_in:(j + 1) * n_in])) for j in range(4)]
    return result, dict(zip(_SHARDED_SMALL, outs[4 * n_in:]))


def _adamw_shards(g, w, m, v):
    names = _SHARDED_SMALL
    n = len(names)

    def body(*refs):
        for j in range(n):
            g_v = refs[j][...]
            res = _adamw_math(refs[n + j][...], g_v, refs[2 * n + j][...], refs[3 * n + j][...])
            for o, val in zip(refs[4 * n + j::n], res):
                o[...] = val

    shapes = [jax.ShapeDtypeStruct(w[k].shape, F32) for k in names]
    outs = pl.pallas_call(body, name="adamw_small_shards", out_shape=shapes * 3, compiler_params=_cparams())(
        *[g[k] for k in names], *[w[k] for k in names], *[m[k] for k in names], *[v[k] for k in names])
    return [dict(zip(names, outs[j * n:(j + 1) * n])) for j in range(3)]


_ANY = pl.BlockSpec(memory_space=pl.ANY)


def _place():
    return lax.axis_index("x"), lax.axis_index("y"), lax.axis_index("c")


def _sems(n):
    return [pltpu.SemaphoreType.DMA((n, 7)), pltpu.SemaphoreType.DMA((n, 7)), pltpu.SemaphoreType.DMA((n,))]


class _Gather:
    def __init__(self, blocks, second, forward_at=1.0):
        self.inputs, self.second, self.forward_at = list(blocks), list(second), forward_at
        self.out_shape = [jax.ShapeDtypeStruct((a.shape[0], N_DEV) + a.shape[1:] if s else (N_DEV,) + a.shape, a.dtype)
                          for a, s in zip(blocks, second)]
        self.scratch = _sems(len(blocks))

    def _copies(self, refs):
        x_refs, out_refs, (send_sems, recv_sems, local_sems) = refs
        n = len(x_refs)
        x, y, c = _place()
        me, sibling = (x, y, c), (x, y, 1 - c)
        chips = [(1 - x, y), (x, 1 - y), (1 - x, 1 - y)]

        def slot(j, px, py, pc):
            idx = 4 * px + 2 * py + pc
            return out_refs[j].at[:, idx] if self.second[j] else out_refs[j].at[idx]

        def copies(k, block, to, own=False):
            return [pltpu.make_async_remote_copy(
                src_ref=x_refs[j] if own else slot(j, *block), dst_ref=slot(j, *block),
                send_sem=send_sems.at[j, k], recv_sem=recv_sems.at[j, k], device_id=to,
                device_id_type=pl.DeviceIdType.MESH) for j in range(n)]

        mine = [pltpu.make_async_copy(x_refs[j], slot(j, *me), local_sems.at[j]) for j in range(n)]
        first = copies(0, me, sibling, own=True)
        for j, chip in enumerate(chips):
            first += copies(1 + j, me, (*chip, c), own=True)
        landed = [copies(1 + j, (*chip, c), me) for j, chip in enumerate(chips)]
        onward = [copies(4 + j, (*chip, c), sibling) for j, chip in enumerate(chips)]
        from_sibling = copies(0, sibling, me)
        for j, chip in enumerate(chips):
            from_sibling += copies(4 + j, (*chip, 1 - c), me)
        return mine, first, landed, onward, from_sibling

    def start(self, refs):
        mine, first, _, _, _ = self._copies(refs)
        for cp in mine + first:
            cp.start()

    def forward(self, refs):
        _, _, landed, onward, _ = self._copies(refs)
        for arrived, passed in zip(landed, onward):
            for cp in arrived:
                cp.wait_recv()
            for cp in passed:
                cp.start()

    def finish(self, refs):
        mine, first, _, onward, from_sibling = self._copies(refs)
        for cp in from_sibling:
            cp.wait_recv()
        for cp in first + [cp for passed in onward for cp in passed]:
            cp.wait_send()
        for cp in mine:
            cp.wait()


class _Exchange:
    forward_at = 1.0

    def __init__(self, sends):
        self.inputs = list(sends)
        self.out_shape = [jax.ShapeDtypeStruct(a.shape, a.dtype) for a in sends]
        self.scratch = _sems(len(sends))

    def _copies(self, refs):
        x_refs, out_refs, (send_sems, recv_sems, local_sems) = refs
        n = len(x_refs)
        x, y, c = _place()
        me = 4 * x + 2 * y + c
        sent, landing = [], []
        for k in range(1, N_DEV):
            px, py, pc = x ^ ((k >> 2) & 1), y ^ ((k >> 1) & 1), c ^ (k & 1)
            them = 4 * px + 2 * py + pc
            for j in range(n):
                for here, there, into in ((them, me, sent), (me, them, landing)):
                    into.append(pltpu.make_async_remote_copy(
                        src_ref=x_refs[j].at[here], dst_ref=out_refs[j].at[there], send_sem=send_sems.at[j, k - 1],
                        recv_sem=recv_sems.at[j, k - 1], device_id=(px, py, pc), device_id_type=pl.DeviceIdType.MESH))
        mine = [pltpu.make_async_copy(x_refs[j].at[me], out_refs[j].at[me], local_sems.at[j]) for j in range(n)]
        return mine, sent, landing

    def start(self, refs):
        mine, sent, _ = self._copies(refs)
        for cp in mine + sent:
            cp.start()

    def forward(self, refs):
        pass

    def finish(self, refs):
        mine, sent, landing = self._copies(refs)
        for cp in landing:
            cp.wait_recv()
        for cp in sent:
            cp.wait_send()
        for cp in mine:
            cp.wait()


def _run(rider, name):
    n_in, n_out = len(rider.inputs), len(rider.out_shape)

    def body(*refs):
        parts = (refs[:n_in], refs[n_in:n_in + n_out], refs[n_in + n_out:])
        rider.start(parts)
        rider.forward(parts)
        rider.finish(parts)

    return pl.pallas_call(body, name=name, out_shape=rider.out_shape, in_specs=[_ANY] * n_in, out_specs=[_ANY] * n_out,
                          scratch_shapes=rider.scratch)(*rider.inputs)


def _split_refs(refs, n_in, n_out, n_scratch, rider):
    r_in = len(rider.inputs) if rider else 0
    r_out = len(rider.out_shape) if rider else 0
    a = n_in + r_in
    b = a + n_out + r_out
    own = refs[:n_in] + refs[a:a + n_out] + refs[b:b + n_scratch]
    return own, (refs[n_in:a], refs[a + n_out:b], refs[b + n_scratch:])


def _grid_step(grid):
    step, stride = 0, 1
    for axis in reversed(range(len(grid))):
        step = step + pl.program_id(axis) * stride
        stride *= grid[axis]
    return step


def _ride_begin(rider, ride_refs, grid):
    if rider is not None:
        pl.when(_grid_step(grid) == 0)(lambda: rider.start(ride_refs))


def _ride_end(rider, ride_refs, grid):
    if rider is not None:
        steps = 1
        for n in grid:
            steps *= n
        step = _grid_step(grid)
        pl.when(step == min(steps - 1, int(steps * rider.forward_at)))(lambda: rider.forward(ride_refs))
        pl.when(step == steps - 1)(lambda: rider.finish(ride_refs))


def _host(rider, in_specs, out_specs, out_shape, scratch, operands):
    if rider is None:
        return dict(in_specs=in_specs, out_specs=out_specs, out_shape=out_shape, scratch_shapes=scratch), operands
    return dict(in_specs=in_specs + [_ANY] * len(rider.inputs), out_specs=out_specs + [_ANY] * len(rider.out_shape),
                out_shape=out_shape + rider.out_shape, scratch_shapes=scratch + rider.scratch), operands + rider.inputs


_LATE = ("w_gate_up", "w_out", "w_down")


def _local_grads(x, target, w, late=None, exchange=False):
    w = dict(w)
    w_in = list(w["w_in"])
    dn_params = (w["dn_a_log"], w["dn_dt_bias"], w["dn_norm_w"])
    gla_params = (w["gla_w_gate2"], w["gla_gate_bias"], w["gla_norm_w"])
    sgu_params = (w["sgu_ln_w"], w["sgu_ln_b"], w["sgu_w_spatial"], w["sgu_b_spatial"])
    saved = []
    for l in range(DEPTH):
        riding = l == 0 and late is not None
        h, p, *got = _in_proj(x, w["norm1_w"], w_in[l], l,
                              rider=_Gather([late["w_down"]], [True], forward_at=0.9) if riding else None)
        if riding:
            w["w_down"] = got[0].reshape(DEPTH, FF, D)
        ya = _sgu_fwd(p, *sgu_params, l)
        yb = _sconv_fwd(p, w["sc_conv_w"], l)
        qkv_c = _qkv_conv_fwd(p, w["dn_conv_w"], l)
        yc, st_c, inv_c, *got = _dn_fwd(
            qkv_c, p, dn_params, l,
            rider=_Gather([late["w_gate_up"], late["w_out"]], [False, True], forward_at=0.9) if riding else None)
        if riding:
            w.update(w_gate_up=got[0], w_out=got[1].reshape(DEPTH, D, D))
        yd, st_d, *got = _gla_fwd(p, gla_params, l,
                                  rider=_Gather([late["w_in"]], [False], forward_at=0.85) if riding else None)
        if riding:
            w_in[1] = _relayout_w_in(got[0])
        head = (w["final_norm_w"], target) if l == DEPTH - 1 else None
        mix, x1, h2, gu, act, x2, *tail = _out_mlp(x, (ya, yb, yc, yd), w["w_out"], w["norm2_w"], w["w_gate_up"],
                                                   w["w_down"], l, head=head)
        saved.append(dict(x=x, h=h, p=p, qkv_c=qkv_c, st_c=st_c, inv_c=inv_c, st_d=st_d, mix=mix, x1=x1, h2=h2, gu=gu,
                          act=act))
        x = x2
    dx, (loss, d_final) = x, tail
    large = {k: [None] * DEPTH for k in ("w_in", "w_out", "w_gate_up", "w_down")}
    small = [None] * DEPTH
    for l in reversed(range(DEPTH)):
        s = saved[l]
        p = s["p"]
        g = {}
        riding = exchange and l == 0

        def exchanging(pairs):
            return _Exchange([large[k][j] for k, j in pairs]) if riding else None

        def arrive(pairs, arrived):
            for (k, j), a in zip(pairs, arrived):
                large[k][j] = a

        pairs = (("w_gate_up", 1), ("w_down", 1))
        dgu, dx1, dmix, g["norm2_w"], *arrived = _mlp_out_bwd(
            dx, s["x1"], s["gu"], w["norm2_w"], w["w_down"], w["w_gate_up"], w["w_out"], l, rider=exchanging(pairs))
        arrive(pairs, arrived)
        pairs = (("w_in", 1), ("w_out", 1))
        d_gu, *arrived = _wgrad_rows(dgu, s["h2"], "wgrad_gate_up", rider=exchanging(pairs)) if riding else (
            _wgrad_rows(dgu, s["h2"], "wgrad_gate_up"),)
        arrive(pairs, arrived)
        large["w_gate_up"][l] = d_gu.reshape(N_DEV, GU_SHARD, D)
        large["w_down"][l] = _wgrad_rows(s["act"], dx, "wgrad_down").reshape(N_DEV, FF // N_DEV, D)
        large["w_out"][l] = _wgrad_rows(s["mix"], dx1, "wgrad_out", rows=D).reshape(N_DEV, D // N_DEV, D)
        d_a, g["sgu_ln_w"], g["sgu_ln_b"], g["sgu_w_spatial"], g["sgu_b_spatial"] = _sgu_bwd(p, dmix, *sgu_params, l)
        d_b, g["sc_conv_w"] = _sconv_bwd(p, dmix, w["sc_conv_w"], l)
        pairs = tuple((k, 0) for k in _LATE)
        dqkv_c, dz_c, dab, g["dn_a_log"], g["dn_dt_bias"], g["dn_norm_w"], *arrived = _dn_bwd(
            s["qkv_c"], p, s["st_c"], s["inv_c"], dmix, dn_params, l, rider=exchanging(pairs))
        arrive(pairs, arrived)
        d_c, g["dn_conv_w"] = _qkv_conv_bwd(p, dqkv_c, w["dn_conv_w"], l)
        d_d, dz_d, dglr, g["gla_w_gate2"], g["gla_gate_bias"], g["gla_norm_w"] = _gla_bwd(p, s["st_d"], dmix, gla_params, l)
        dp, dx, g["norm1_w"] = _in_proj_bwd((d_c, d_d, d_a, d_b, dz_c, dz_d, dab, dglr), s["x"], dx1, w["norm1_w"],
                                            w_in[l], l)
        small[l] = g
        w_in_rows = dict(rows=P // 2, out_dtype=F32)
        if riding:
            half = D // 2
            first, small = _wgrad_rows(dp, s["h"], "wgrad_in_a", **w_in_rows, b_cols=(0, half),
                                       rider=_Gather([_pack_small(small, d_final)], [False], forward_at=0.75))
            first = _scatter_w_in_grad(first)
            second, first = _wgrad_rows(dp, s["h"], "wgrad_in_b", **w_in_rows, b_cols=(1, half), rider=_Exchange([first]))
            large["w_in"][l] = (first, _scatter_w_in_grad(second))
        else:
            large["w_in"][l] = _scatter_w_in_grad(_wgrad_rows(dp, s["h"], "wgrad_in", **w_in_rows))
    return loss[0, 0], dx, large, small, d_final


_ORDER = ("norm1_w", "w_in", "sgu_ln_w", "sgu_ln_b", "sgu_w_spatial", "sgu_b_spatial", "sc_conv_w", "dn_conv_w",
          "dn_a_log", "dn_dt_bias", "dn_norm_w", "gla_w_gate2", "gla_gate_bias", "gla_norm_w", "w_out", "norm2_w",
          "w_gate_up", "w_down", "final_norm_w")
_LARGE = ("w_in", "w_gate_up", "w_out", "w_down")
_LARGE_TILE = {"w_gate_up": 352, "w_out": 128, "w_down": 352}


def _gather_cols(g):
    return jnp.transpose(g, (1, 2, 0, 3)).reshape(g.shape[1], g.shape[2], N_DEV * g.shape[3])


def kernel(x, norm1_w, w_in, sgu_ln_w, sgu_ln_b, sgu_w_spatial, sgu_b_spatial, sc_conv_w, dn_conv_w, dn_a_log, dn_dt_bias, dn_norm_w, gla_w_gate2, gla_gate_bias, gla_norm_w, w_out, norm2_w, w_gate_up, w_down, final_norm_w, loss_target, m_norm1_w, m_w_in, m_sgu_ln_w, m_sgu_ln_b, m_sgu_w_spatial, m_sgu_b_spatial, m_sc_conv_w, m_dn_conv_w, m_dn_a_log, m_dn_dt_bias, m_dn_norm_w, m_gla_w_gate2, m_gla_gate_bias, m_gla_norm_w, m_w_out, m_norm2_w, m_w_gate_up, m_w_down, m_final_norm_w, v_norm1_w, v_w_in, v_sgu_ln_w, v_sgu_ln_b, v_sgu_w_spatial, v_sgu_b_spatial, v_sc_conv_w, v_dn_conv_w, v_dn_a_log, v_dn_dt_bias, v_dn_norm_w, v_gla_w_gate2, v_gla_gate_bias, v_gla_norm_w, v_w_out, v_norm2_w, v_w_gate_up, v_w_down, v_final_norm_w):
    args = dict(locals())
    wts = {k: args[k] for k in _ORDER}
    mom = {k: args["m_" + k] for k in _ORDER}
    var = {k: args["v_" + k] for k in _ORDER}
    for d in (wts, mom, var):
        d["final_norm_w"] = d["final_norm_w"][None]
    me = 4 * lax.axis_index("x") + 2 * lax.axis_index("y") + lax.axis_index("c")
    for d in (wts, mom, var):
        d["w_gate_up"] = jnp.swapaxes(d["w_gate_up"], 1, 2)

    late = {k: wts[k].astype(BF16) for k in _LATE}
    w_in = wts["w_in"].astype(BF16)
    late["w_in"] = w_in[1]
    got = _run(_Gather([w_in[0]] + [wts[k] for k in _SHARDED_SMALL], [False] * 4), "gather_w_in")
    full = dict(wts)
    full["w_in"] = [_relayout_w_in(got[0]), None]
    for k, g in zip(_SHARDED_SMALL, got[1:]):
        full[k] = _gather_cols(g)

    loss, dx, large, small, d_final = _local_grads(x[0], loss_target[0], full, late=late, exchange=True)
    loss = lax.psum(loss, ("x", "y", "c"))

    first, second = large["w_in"][0]
    second, = _run(_Exchange([second]), "exchange_grads")
    large["w_in"][0] = jnp.concatenate([first, second], axis=2)
    result = {}
    for k in ("w_gate_up", "w_out", "w_down"):
        outs = _adamw_large(large[k], wts[k], mom[k], var[k], "adamw_" + k, _LARGE_TILE[k])
        for kind, a in zip(("grad", "delta", "new_m", "new_v"), outs):
            result[kind, k] = jnp.swapaxes(a, 1, 2) if k == "w_gate_up" else a
    outs = _adamw_w_in(large["w_in"], *[jnp.transpose(d["w_in"], (2, 0, 1)) for d in (wts, mom, var)])
    for kind, a in zip(("grad", "delta", "new_m", "new_v"), outs):
        result[kind, "w_in"] = jnp.transpose(a, (1, 2, 0))

    slabs = small
    rep = _REPLICATED + ("final_norm_w",)
    outs, summed = _adamw_small(slabs, {k: wts[k] for k in rep}, {k: mom[k] for k in rep}, {k: var[k] for k in rep})
    for kind, d in zip(("grad", "delta", "new_m", "new_v"), outs):
        for k, a in d.items():
            result[kind, k] = a[0] if k == "final_norm_w" else a
    own = {k: lax.dynamic_slice_in_dim(summed[k], me * wts[k].shape[-1], wts[k].shape[-1], axis=2) for k in _SHARDED_SMALL}
    outs = _adamw_shards(own, {k: wts[k] for k in _SHARDED_SMALL}, {k: mom[k] for k in _SHARDED_SMALL},
                         {k: var[k] for k in _SHARDED_SMALL})
    for kind, d in zip(("grad", "delta", "new_m", "new_v"), [own] + outs):
        for k, a in d.items():
            result[kind, k] = a

    return (loss, dx[None], *[result[kind, k] for kind in ("grad", "delta", "new_m", "new_v") for k in _ORDER])
```
